```python
import math
import jax, jax.numpy as jnp
from jax import lax
import numpy as np

D_MODEL = 1024
BATCH = 8
SEQ = 4096
DEPTH = 1

HEAD_DIM = 64
POOL_WIDTH = D_MODEL // 4
POOL_WINDOWS = (2, 4, 8, 16)
POOL_GROUPS = len(POOL_WINDOWS)
POOL_GROUP_DIM = POOL_WIDTH // POOL_GROUPS
ATTN_WIDTH = D_MODEL - POOL_WIDTH
ATTN_HEADS = ATTN_WIDTH // HEAD_DIM
DILATION_CFG = ((128, 1), (512, 4), (2048, 16))
N_DIL = len(DILATION_CFG)
HEADS_PER_DIL = ATTN_HEADS // N_DIL
ATTN_OUT_WIDTH = HEADS_PER_DIL * HEAD_DIM
IN_PROJ_WIDTH = POOL_WIDTH + 3 * ATTN_WIDTH
OUT_PROJ_WIDTH = POOL_WIDTH + ATTN_OUT_WIDTH
ROT_DIM = HEAD_DIM // 4
ROPE_THETA = 500000.0
D_FF = 11 * D_MODEL // 4
CONV_WIDTH = 3
BLOCK = 128
NORM_EPS = 1e-6
N_MOD = 6

kernel_name = "hybrid_pool_dilated_attn_convffn_block"


def rms_norm(x, g):
    x32 = x.astype(jnp.float32)
    y = x32 * lax.rsqrt(jnp.mean(x32 * x32, axis=-1, keepdims=True) + NORM_EPS)
    return (y * g.astype(jnp.float32)).astype(x.dtype)


def partial_rope(t, cos, sin):
    half = ROT_DIM // 2
    t1 = t[..., :half]
    t2 = t[..., half:ROT_DIM]
    return jnp.concatenate(
        [t1 * cos - t2 * sin, t2 * cos + t1 * sin, t[..., ROT_DIM:]], axis=-1)


def causal_multiscale_pool(u, w_pool, b_pool, pool_scale):
    B, S, _ = u.shape
    u32 = u.astype(jnp.float32).reshape(B, S, POOL_GROUPS, POOL_GROUP_DIM)
    cs = jnp.cumsum(u32, axis=1)
    t = jnp.arange(S, dtype=jnp.float32)
    means = []
    for gi, w in enumerate(POOL_WINDOWS):
        csg = cs[:, :, gi]
        lagged = jnp.pad(csg[:, :S - w], ((0, 0), (w, 0), (0, 0)))
        count = jnp.minimum(t + 1.0, float(w))
        means.append((csg - lagged) / count[None, :, None])
    mixed = jnp.stack(means, axis=2) - u32
    y = jnp.einsum('bsgc,gcd->bsgd', mixed, w_pool.astype(jnp.float32)) + b_pool.astype(jnp.float32)
    return (y.reshape(B, S, POOL_WIDTH) * pool_scale.astype(jnp.float32)).astype(u.dtype)


def dilated_window_attention(q, k, v, window, dilation):
    B, S, H, hd = q.shape
    span = window // dilation
    assert span <= BLOCK
    chunk = dilation * BLOCK
    s_pad = -(-S // chunk) * chunk
    padw = ((0, 0), (0, s_pad - S), (0, 0), (0, 0))
    q, k, v = (jnp.pad(a.astype(jnp.float32), padw) for a in (q, k, v))
    nb = s_pad // chunk
    qb, kb, vb = (a.reshape(B, nb, BLOCK, dilation, H, hd) for a in (q, k, v))

    def with_prev(a):
        prev = jnp.pad(a[:, :-1], ((0, 0), (1, 0), (0, 0), (0, 0), (0, 0), (0, 0)))
        return jnp.concatenate([prev, a], axis=2)

    kk, vv = with_prev(kb), with_prev(vb)
    scores = jnp.einsum('bnirhd,bnjrhd->bnrhij', qb, kk)
    i = jnp.arange(BLOCK)[:, None]
    j = jnp.arange(2 * BLOCK)[None, :]
    n = jnp.arange(nb)[:, None, None]
    dist = BLOCK + i - j
    valid = (dist >= 0) & (dist <= span) & (n * BLOCK + j - BLOCK >= 0)
    scores = jnp.where(valid[None, :, None, None], scores, -jnp.inf)
    m = jnp.max(scores, axis=-1, keepdims=True)
    p = jnp.exp(scores - m)
    den = jnp.sum(p, axis=-1)
    o = jnp.einsum('bnrhij,bnjrhd->bnirhd', p, vv)
    den_t = jnp.moveaxis(den, -1, 2)
    lse_t = jnp.moveaxis(m[..., 0] + jnp.log(den), -1, 2)
    o = (o / den_t[..., None]).reshape(B, s_pad, H, hd)[:, :S]
    lse = lse_t.reshape(B, s_pad, H)[:, :S]
    return o, lse


def mixing_sublayer(h, cos, sin, w_in, w_pool, b_pool, pool_scale, w_out):
    B, S, _ = h.shape
    proj = h @ w_in
    u_pool = proj[..., :POOL_WIDTH]
    qkv = proj[..., POOL_WIDTH:].reshape(B, S, 3, ATTN_HEADS, HEAD_DIM)
    q = partial_rope(qkv[:, :, 0], cos, sin) * (HEAD_DIM ** -0.5)
    k = partial_rope(qkv[:, :, 1], cos, sin)
    v = qkv[:, :, 2]
    outs, lses = [], []
    for g, (window, dilation) in enumerate(DILATION_CFG):
        sl = slice(g * HEADS_PER_DIL, (g + 1) * HEADS_PER_DIL)
        o, l = dilated_window_attention(q[:, :, sl], k[:, :, sl], v[:, :, sl], window, dilation)
        outs.append(o)
        lses.append(l)
    alpha = jax.nn.softmax(jnp.stack(lses, axis=0), axis=0)
    attn = jnp.sum(alpha[..., None] * jnp.stack(outs, axis=0), axis=0)
    attn = attn.reshape(B, S, ATTN_OUT_WIDTH).astype(h.dtype)
    pool = causal_multiscale_pool(u_pool, w_pool, b_pool, pool_scale)
    return jnp.concatenate([pool, attn], axis=-1) @ w_out


def conv_ffn(h, w_up, conv_w, conv_b, w_down):
    S = h.shape[1]
    up = h @ w_up
    gate, val = up[..., :D_FF], up[..., D_FF:]
    gp = jnp.pad(gate, ((0, 0), (CONV_WIDTH - 1, 0), (0, 0)))
    gate = gp[:, 0:S] * conv_w[0] + gp[:, 1:S + 1] * conv_w[1] + gp[:, 2:S + 2] * conv_w[2] + conv_b
    return (jax.nn.gelu(gate, approximate=True) * val) @ w_down


def _fwd_setup_inputs(seed: int = 0) -> dict:
    key = jax.random.key(seed)
    ks = jax.random.split(key, 20)
    f32 = jnp.float32
    nrm = lambda k, shape, s: jax.random.normal(k, shape, f32) * s
    L = DEPTH
    return {
        "x": nrm(ks[0], (BATCH, SEQ, D_MODEL), 1.0),
        "c": nrm(ks[1], (BATCH, D_MODEL), 1.0),
        "positions": jnp.broadcast_to(jnp.arange(SEQ, dtype=jnp.int32), (BATCH, SEQ)),
        "w_ada": nrm(ks[2], (L, D_MODEL, N_MOD * D_MODEL), D_MODEL ** -0.5),
        "b_ada": nrm(ks[3], (L, N_MOD * D_MODEL), 0.02),
        "g_pre_mix": 1.0 + nrm(ks[4], (L, D_MODEL), 0.05),
        "g_post_mix": 1.0 + nrm(ks[5], (L, D_MODEL), 0.05),
        "g_pre_ffn": 1.0 + nrm(ks[6], (L, D_MODEL), 0.05),
        "g_post_ffn": 1.0 + nrm(ks[7], (L, D_MODEL), 0.05),
        "w_in": nrm(ks[8], (L, D_MODEL, IN_PROJ_WIDTH), D_MODEL ** -0.5),
        "w_pool": nrm(ks[9], (L, POOL_GROUPS, POOL_GROUP_DIM, POOL_GROUP_DIM), POOL_GROUP_DIM ** -0.5),
        "b_pool": nrm(ks[10], (L, POOL_GROUPS, POOL_GROUP_DIM), 0.02),
        "pool_scale": 1.0 + nrm(ks[11], (L, POOL_WIDTH), 0.05),
        "w_out": nrm(ks[12], (L, OUT_PROJ_WIDTH, D_MODEL), OUT_PROJ_WIDTH ** -0.5),
        "w_up": nrm(ks[13], (L, D_MODEL, 2 * D_FF), D_MODEL ** -0.5),
        "conv_w": nrm(ks[14], (L, CONV_WIDTH, D_FF), CONV_WIDTH ** -0.5),
        "conv_b": nrm(ks[15], (L, D_FF), 0.02),
        "w_down": nrm(ks[16], (L, D_FF, D_MODEL), D_FF ** -0.5),
    }


def _fwd_reference(x, c, positions, w_ada, b_ada, g_pre_mix, g_post_mix, g_pre_ffn, g_post_ffn,
              w_in, w_pool, b_pool, pool_scale, w_out, w_up, conv_w, conv_b, w_down):
    inv_freq = ROPE_THETA ** (-jnp.arange(0, ROT_DIM, 2, dtype=jnp.float32) / ROT_DIM)
    ang = positions.astype(jnp.float32)[..., None] * inv_freq
    cos = jnp.cos(ang)[:, :, None, :].astype(x.dtype)
    sin = jnp.sin(ang)[:, :, None, :].astype(x.dtype)
    c_act = jax.nn.silu(c)
    for l in range(DEPTH):
        mod = c_act @ w_ada[l] + b_ada[l]
        sh_m, sc_m, gt_m, sh_f, sc_f, gt_f = (t[:, None, :] for t in jnp.split(mod, N_MOD, axis=-1))
        h = rms_norm(x, g_pre_mix[l]) * (1.0 + sc_m) + sh_m
        y = mixing_sublayer(h, cos, sin, w_in[l], w_pool[l], b_pool[l], pool_scale[l], w_out[l])
        x = x + gt_m * rms_norm(y, g_post_mix[l])
        h = rms_norm(x, g_pre_ffn[l]) * (1.0 + sc_f) + sh_f
        y = conv_ffn(h, w_up[l], conv_w[l], conv_b[l], w_down[l])
        x = x + gt_f * rms_norm(y, g_post_ffn[l])
    return x


import jax as _jax
import jax.numpy as _jnp

TWIN_FORMAT = 'train_step'
FWD_PARAMS = ['x', 'c', 'positions', 'w_ada', 'b_ada', 'g_pre_mix', 'g_post_mix', 'g_pre_ffn', 'g_post_ffn', 'w_in', 'w_pool', 'b_pool', 'pool_scale', 'w_out', 'w_up', 'conv_w', 'conv_b', 'w_down']
TWIN_WEIGHTS = ['w_ada', 'b_ada', 'g_pre_mix', 'g_post_mix', 'g_pre_ffn', 'g_post_ffn', 'w_in', 'w_pool', 'b_pool', 'pool_scale', 'w_out', 'w_up', 'conv_w', 'conv_b', 'w_down']
TWIN_DIFF_INPUT = 'x'
TWIN_INPUTS = ['x', 'c', 'positions', 'w_ada', 'b_ada', 'g_pre_mix', 'g_post_mix', 'g_pre_ffn', 'g_post_ffn', 'w_in', 'w_pool', 'b_pool', 'pool_scale', 'w_out', 'w_up', 'conv_w', 'conv_b', 'w_down', 'loss_target', 'm_w_ada', 'm_b_ada', 'm_g_pre_mix', 'm_g_post_mix', 'm_g_pre_ffn', 'm_g_post_ffn', 'm_w_in', 'm_w_pool', 'm_b_pool', 'm_pool_scale', 'm_w_out', 'm_w_up', 'm_conv_w', 'm_conv_b', 'm_w_down', 'v_w_ada', 'v_b_ada', 'v_g_pre_mix', 'v_g_post_mix', 'v_g_pre_ffn', 'v_g_post_ffn', 'v_w_in', 'v_w_pool', 'v_b_pool', 'v_pool_scale', 'v_w_out', 'v_w_up', 'v_conv_w', 'v_conv_b', 'v_w_down']
TWIN_OUTPUTS = ['loss', 'grad_x', 'grad_w_ada', 'grad_b_ada', 'grad_g_pre_mix', 'grad_g_post_mix', 'grad_g_pre_ffn', 'grad_g_post_ffn', 'grad_w_in', 'grad_w_pool', 'grad_b_pool', 'grad_pool_scale', 'grad_w_out', 'grad_w_up', 'grad_conv_w', 'grad_conv_b', 'grad_w_down', 'delta_w_ada', 'delta_b_ada', 'delta_g_pre_mix', 'delta_g_post_mix', 'delta_g_pre_ffn', 'delta_g_post_ffn', 'delta_w_in', 'delta_w_pool', 'delta_b_pool', 'delta_pool_scale', 'delta_w_out', 'delta_w_up', 'delta_conv_w', 'delta_conv_b', 'delta_w_down', 'new_m_w_ada', 'new_m_b_ada', 'new_m_g_pre_mix', 'new_m_g_post_mix', 'new_m_g_pre_ffn', 'new_m_g_post_ffn', 'new_m_w_in', 'new_m_w_pool', 'new_m_b_pool', 'new_m_pool_scale', 'new_m_w_out', 'new_m_w_up', 'new_m_conv_w', 'new_m_conv_b', 'new_m_w_down', 'new_v_w_ada', 'new_v_b_ada', 'new_v_g_pre_mix', 'new_v_g_post_mix', 'new_v_g_pre_ffn', 'new_v_g_post_ffn', 'new_v_w_in', 'new_v_w_pool', 'new_v_b_pool', 'new_v_pool_scale', 'new_v_w_out', 'new_v_w_up', 'new_v_conv_w', 'new_v_conv_b', 'new_v_w_down']
TWIN_LEAF_KINDS = {'loss': 'loss', 'grad_x': 'grad_x', 'grad_w_ada': 'grad_w', 'grad_b_ada': 'grad_w', 'grad_g_pre_mix': 'grad_w', 'grad_g_post_mix': 'grad_w', 'grad_g_pre_ffn': 'grad_w', 'grad_g_post_ffn': 'grad_w', 'grad_w_in': 'grad_w', 'grad_w_pool': 'grad_w', 'grad_b_pool': 'grad_w', 'grad_pool_scale': 'grad_w', 'grad_w_out': 'grad_w', 'grad_w_up': 'grad_w', 'grad_conv_w': 'grad_w', 'grad_conv_b': 'grad_w', 'grad_w_down': 'grad_w', 'delta_w_ada': 'delta_w', 'delta_b_ada': 'delta_w', 'delta_g_pre_mix': 'delta_w', 'delta_g_post_mix': 'delta_w', 'delta_g_pre_ffn': 'delta_w', 'delta_g_post_ffn': 'delta_w', 'delta_w_in': 'delta_w', 'delta_w_pool': 'delta_w', 'delta_b_pool': 'delta_w', 'delta_pool_scale': 'delta_w', 'delta_w_out': 'delta_w', 'delta_w_up': 'delta_w', 'delta_conv_w': 'delta_w', 'delta_conv_b': 'delta_w', 'delta_w_down': 'delta_w', 'new_m_w_ada': 'new_m', 'new_m_b_ada': 'new_m', 'new_m_g_pre_mix': 'new_m', 'new_m_g_post_mix': 'new_m', 'new_m_g_pre_ffn': 'new_m', 'new_m_g_post_ffn': 'new_m', 'new_m_w_in': 'new_m', 'new_m_w_pool': 'new_m', 'new_m_b_pool': 'new_m', 'new_m_pool_scale': 'new_m', 'new_m_w_out': 'new_m', 'new_m_w_up': 'new_m', 'new_m_conv_w': 'new_m', 'new_m_conv_b': 'new_m', 'new_m_w_down': 'new_m', 'new_v_w_ada': 'new_v', 'new_v_b_ada': 'new_v', 'new_v_g_pre_mix': 'new_v', 'new_v_g_post_mix': 'new_v', 'new_v_g_pre_ffn': 'new_v', 'new_v_g_post_ffn': 'new_v', 'new_v_w_in': 'new_v', 'new_v_w_pool': 'new_v', 'new_v_b_pool': 'new_v', 'new_v_pool_scale': 'new_v', 'new_v_w_out': 'new_v', 'new_v_w_up': 'new_v', 'new_v_conv_w': 'new_v', 'new_v_conv_b': 'new_v', 'new_v_w_down': 'new_v'}


def _forward(args):
    return _fwd_reference(*[args[k] for k in FWD_PARAMS])


def _output_shape():
    out = _jax.eval_shape(lambda: _forward(_fwd_setup_inputs(0)))
    return out.shape, out.dtype

N_MICROBATCH = 1
ADAM_LR = 0.001
ADAM_B1 = 0.9
ADAM_B2 = 0.999
ADAM_EPS = 1e-08
ADAM_WD = 0.01
ADAM_STEP = 10
PER_EXAMPLE_BATCH_AXIS = {'x': 0, 'c': 0, 'positions': 0, 'loss_target': 0}
SHARED_INPUTS = []
_WEIGHT_DTYPES = {'w_ada': _jnp.float32, 'b_ada': _jnp.float32, 'g_pre_mix': _jnp.float32, 'g_post_mix': _jnp.float32, 'g_pre_ffn': _jnp.float32, 'g_post_ffn': _jnp.float32, 'w_in': _jnp.float32, 'w_pool': _jnp.float32, 'b_pool': _jnp.float32, 'pool_scale': _jnp.float32, 'w_out': _jnp.float32, 'w_up': _jnp.float32, 'conv_w': _jnp.float32, 'conv_b': _jnp.float32, 'w_down': _jnp.float32}
MOMENT_SCALE = {'w_ada': 3.648519e+00, 'b_ada': 6.821166e+00, 'g_pre_mix': 4.125173e-01, 'g_post_mix': 1.548224e+01, 'g_pre_ffn': 3.989306e-01, 'g_post_ffn': 1.511742e+01, 'w_in': 6.192129e-01, 'w_pool': 1.007210e+00, 'b_pool': 5.552886e+00, 'pool_scale': 1.751414e+00, 'w_out': 1.208777e+00, 'w_up': 4.261542e-01, 'conv_w': 4.447154e-01, 'conv_b': 5.217497e-01, 'w_down': 8.913790e-01}


def _to_microbatches(a, axis):
    t = _jnp.moveaxis(a, axis, 0)
    t = t.reshape((N_MICROBATCH, t.shape[0] // N_MICROBATCH) + t.shape[1:])
    return _jnp.moveaxis(t, 1, axis + 1)


def setup_inputs(seed: int = 0) -> dict:
    inp = _fwd_setup_inputs(seed)
    key = _jax.random.fold_in(_jax.random.key(seed), 7919)
    shape, _ = _output_shape()
    out = dict(inp)
    out["loss_target"] = _jax.random.normal(_jax.random.fold_in(key, 0), shape, _jnp.float32)
    for i, name in enumerate(TWIN_WEIGHTS):
        w = inp[name].astype(_jnp.float32)
        if MOMENT_SCALE is None:
            s = _jnp.sqrt(_jnp.mean(_jnp.square(w)) + 1e-30)
        else:
            s = MOMENT_SCALE[name]
        km, kv = _jax.random.split(_jax.random.fold_in(key, i + 1))
        out[name] = w
        out["m_" + name] = s * _jax.random.normal(km, w.shape, _jnp.float32)
        out["v_" + name] = (s * s) * _jax.random.uniform(kv, w.shape, _jnp.float32, 0.5, 1.5)
    if N_MICROBATCH > 1:
        for name, axis in PER_EXAMPLE_BATCH_AXIS.items():
            out[name] = _to_microbatches(out[name], axis)
    return {'x': out['x'], 'c': out['c'], 'positions': out['positions'], 'w_ada': out['w_ada'], 'b_ada': out['b_ada'], 'g_pre_mix': out['g_pre_mix'], 'g_post_mix': out['g_post_mix'], 'g_pre_ffn': out['g_pre_ffn'], 'g_post_ffn': out['g_post_ffn'], 'w_in': out['w_in'], 'w_pool': out['w_pool'], 'b_pool': out['b_pool'], 'pool_scale': out['pool_scale'], 'w_out': out['w_out'], 'w_up': out['w_up'], 'conv_w': out['conv_w'], 'conv_b': out['conv_b'], 'w_down': out['w_down'], 'loss_target': out['loss_target'], 'm_w_ada': out['m_w_ada'], 'm_b_ada': out['m_b_ada'], 'm_g_pre_mix': out['m_g_pre_mix'], 'm_g_post_mix': out['m_g_post_mix'], 'm_g_pre_ffn': out['m_g_pre_ffn'], 'm_g_post_ffn': out['m_g_post_ffn'], 'm_w_in': out['m_w_in'], 'm_w_pool': out['m_w_pool'], 'm_b_pool': out['m_b_pool'], 'm_pool_scale': out['m_pool_scale'], 'm_w_out': out['m_w_out'], 'm_w_up': out['m_w_up'], 'm_conv_w': out['m_conv_w'], 'm_conv_b': out['m_conv_b'], 'm_w_down': out['m_w_down'], 'v_w_ada': out['v_w_ada'], 'v_b_ada': out['v_b_ada'], 'v_g_pre_mix': out['v_g_pre_mix'], 'v_g_post_mix': out['v_g_post_mix'], 'v_g_pre_ffn': out['v_g_pre_ffn'], 'v_g_post_ffn': out['v_g_post_ffn'], 'v_w_in': out['v_w_in'], 'v_w_pool': out['v_w_pool'], 'v_b_pool': out['v_b_pool'], 'v_pool_scale': out['v_pool_scale'], 'v_w_out': out['v_w_out'], 'v_w_up': out['v_w_up'], 'v_conv_w': out['v_conv_w'], 'v_conv_b': out['v_conv_b'], 'v_w_down': out['v_w_down']}


def _loss(weights, diff, rest, loss_target):
    with _jax.named_scope("forward"):
        args = {**rest, TWIN_DIFF_INPUT: diff, **{k: w.astype(_WEIGHT_DTYPES[k]) for k, w in weights.items()}}
        y = _forward(args)
    with _jax.named_scope("loss_head"):
        err = _jnp.square(y.astype(_jnp.float32) - loss_target)
        return 0.5 * _jnp.sum(_jnp.mean(err, axis=-1)) if err.ndim else 0.5 * err


def _adamw(w, g, m, v):
    m = ADAM_B1 * m + (1.0 - ADAM_B1) * g
    v = ADAM_B2 * v + (1.0 - ADAM_B2) * _jnp.square(g)
    m_hat = m / (1.0 - ADAM_B1 ** ADAM_STEP)
    v_hat = v / (1.0 - ADAM_B2 ** ADAM_STEP)
    delta = -ADAM_LR * (m_hat / (_jnp.sqrt(v_hat) + ADAM_EPS) + ADAM_WD * w)
    return delta, m, v


def reference(x, c, positions, w_ada, b_ada, g_pre_mix, g_post_mix, g_pre_ffn, g_post_ffn, w_in, w_pool, b_pool, pool_scale, w_out, w_up, conv_w, conv_b, w_down, loss_target, m_w_ada, m_b_ada, m_g_pre_mix, m_g_post_mix, m_g_pre_ffn, m_g_post_ffn, m_w_in, m_w_pool, m_b_pool, m_pool_scale, m_w_out, m_w_up, m_conv_w, m_conv_b, m_w_down, v_w_ada, v_b_ada, v_g_pre_mix, v_g_post_mix, v_g_pre_ffn, v_g_post_ffn, v_w_in, v_w_pool, v_b_pool, v_pool_scale, v_w_out, v_w_up, v_conv_w, v_conv_b, v_w_down):
    given = dict(x=x, c=c, positions=positions, w_ada=w_ada, b_ada=b_ada, g_pre_mix=g_pre_mix, g_post_mix=g_post_mix, g_pre_ffn=g_pre_ffn, g_post_ffn=g_post_ffn, w_in=w_in, w_pool=w_pool, b_pool=b_pool, pool_scale=pool_scale, w_out=w_out, w_up=w_up, conv_w=conv_w, conv_b=conv_b, w_down=w_down, loss_target=loss_target, m_w_ada=m_w_ada, m_b_ada=m_b_ada, m_g_pre_mix=m_g_pre_mix, m_g_post_mix=m_g_post_mix, m_g_pre_ffn=m_g_pre_ffn, m_g_post_ffn=m_g_post_ffn, m_w_in=m_w_in, m_w_pool=m_w_pool, m_b_pool=m_b_pool, m_pool_scale=m_pool_scale, m_w_out=m_w_out, m_w_up=m_w_up, m_conv_w=m_conv_w, m_conv_b=m_conv_b, m_w_down=m_w_down, v_w_ada=v_w_ada, v_b_ada=v_b_ada, v_g_pre_mix=v_g_pre_mix, v_g_post_mix=v_g_post_mix, v_g_pre_ffn=v_g_pre_ffn, v_g_post_ffn=v_g_post_ffn, v_w_in=v_w_in, v_w_pool=v_w_pool, v_b_pool=v_b_pool, v_pool_scale=v_pool_scale, v_w_out=v_w_out, v_w_up=v_w_up, v_conv_w=v_conv_w, v_conv_b=v_conv_b, v_w_down=v_w_down)
    weights = {n: given[n] for n in TWIN_WEIGHTS}
    shared = {n: given[n] for n in SHARED_INPUTS}
    per_example = {n: given[n] for n in ['x', 'c', 'positions']}
    grad_fn = _jax.value_and_grad(_loss, argnums=(0, 1))

    def one_microbatch(ex, loss_target):
        ex = dict(ex)
        diff = ex.pop(TWIN_DIFF_INPUT)
        return grad_fn(weights, diff, {**shared, **ex}, loss_target)

    if N_MICROBATCH == 1:
        loss, (grad_w, grad_x) = one_microbatch(per_example, given["loss_target"])
    else:
        def body(carry, xs):
            loss_sum, grad_sum = carry
            l_k, (gw_k, gx_k) = one_microbatch(xs[0], xs[1])
            with _jax.named_scope("update"):
                return (loss_sum + l_k, _jax.tree.map(_jnp.add, grad_sum, gw_k)), gx_k

        init = (_jnp.zeros((), _jnp.float32), _jax.tree.map(_jnp.zeros_like, weights))
        (loss, grad_w), grad_x = _jax.lax.scan(body, init, (per_example, given["loss_target"]))
    with _jax.named_scope("update"):
        delta_w, new_m, new_v = {}, {}, {}
        for n in TWIN_WEIGHTS:
            delta_w[n], new_m[n], new_v[n] = _adamw(weights[n], grad_w[n], given["m_" + n], given["v_" + n])
    return (loss, grad_x, *[grad_w[n] for n in TWIN_WEIGHTS], *[delta_w[n] for n in TWIN_WEIGHTS],
            *[new_m[n] for n in TWIN_WEIGHTS], *[new_v[n] for n in TWIN_WEIGHTS])
```

```python
import functools
import math

import jax
import jax.numpy as jnp
from jax import lax
from jax.experimental import pallas as pl
from jax.experimental.pallas import tpu as pltpu

F32 = jnp.float32
BF16 = jnp.bfloat16
MESH = pl.DeviceIdType.MESH

N_DEV = 8
HEAD_DIM = 64
ROT_HALF = 8
ROPE_THETA = 500000.0
POOL_WINDOWS = (2, 4, 8, 16)
DILATIONS = (1, 4, 16)
BLOCK = 128
NORM_EPS = 1e-6
HALO = 16
MASKED = -1e30

ADAM_LR = 0.001
ADAM_B1 = 0.9
ADAM_B2 = 0.999
ADAM_EPS = 1e-08
ADAM_WD = 0.01
ADAM_STEP = 10

V7X_VMEM_LIMIT = 56 * 1024 * 1024
LANES = 128

NT = (((1,), (1,)), ((), ()))
NN = (((1,), (0,)), ((), ()))
TN = (((0,), (0,)), ((), ()))


def _dot(a, b, dims):
    return lax.dot_general(a, b, dims, preferred_element_type=F32)


def _params(sem=None, vmem=V7X_VMEM_LIMIT):
    if sem is None:
        return pltpu.CompilerParams(vmem_limit_bytes=vmem)
    return pltpu.CompilerParams(dimension_semantics=sem, vmem_limit_bytes=vmem)


def _rstd(v):
    return lax.rsqrt(jnp.mean(v * v, axis=-1, keepdims=True) + NORM_EPS)


def _norm_bwd(dn, n, rstd):
    return rstd * (dn - n * jnp.mean(dn * n, axis=-1, keepdims=True))


def _rope_fwd(p, rope_ref):
    return p * rope_ref[0] + pltpu.roll(p, LANES - ROT_HALF, 1) * rope_ref[1] + pltpu.roll(p, ROT_HALF, 1) * rope_ref[2]


def _rope_bwd(dp, rope_ref):
    return dp * rope_ref[0] + pltpu.roll(dp * rope_ref[1], ROT_HALF, 1) + pltpu.roll(dp * rope_ref[2], LANES - ROT_HALF, 1)


def _gelu_parts(v):
    k = math.sqrt(2.0 / math.pi)
    t = jnp.tanh(k * (v + 0.044715 * v * v * v))
    g = 0.5 * v * (1.0 + t)
    dg = 0.5 * (1.0 + t) + 0.5 * v * (1.0 - t * t) * k * (1.0 + 3.0 * 0.044715 * v * v)
    return g, dg


def _halo_before(i, tile):
    return jnp.maximum(i * (tile // HALO) - 1, 0)


def _premix_inproj(x, sh, sc, g, w_in_t, rope, tm):
    s_len, d = x.shape
    n_proj = w_in_t.shape[0]
    n_slab = (n_proj - 256) // LANES

    def body(x_ref, sh_ref, sc_ref, g_ref, w_ref, rope_ref, h_ref, up_ref, qkv_ref):
        xv = x_ref[...]
        h = (xv * _rstd(xv) * g_ref[...]) * (1.0 + sc_ref[...]) + sh_ref[...]
        hb = h.astype(BF16)
        h_ref[...] = hb
        up_ref[...] = _dot(hb, w_ref[0:256, :], NT)
        for pair in range(n_slab // 2):
            p = _dot(hb, w_ref[256 + 256 * pair:512 + 256 * pair, :], NT)
            for half in range(2):
                ph = p[:, half * LANES:(half + 1) * LANES]
                if pair < 6:
                    ph = _rope_fwd(ph, rope_ref)
                if pair < 3:
                    ph = ph * (HEAD_DIM ** -0.5)
                qkv_ref[2 * pair + half] = ph

    vec = pl.BlockSpec((1, d), lambda i: (0, 0))
    return pl.pallas_call(
        body, name="premix_inproj", grid=(s_len // tm,),
        in_specs=[pl.BlockSpec((tm, d), lambda i: (i, 0)), vec, vec, vec,
                  pl.BlockSpec((n_proj, d), lambda i: (0, 0)),
                  pl.BlockSpec((3, tm, LANES), lambda i: (0, i, 0))],
        out_specs=[pl.BlockSpec((tm, d), lambda i: (i, 0)),
                   pl.BlockSpec((tm, 256), lambda i: (i, 0)),
                   pl.BlockSpec((n_slab, tm, LANES), lambda i: (0, i, 0))],
        out_shape=[jax.ShapeDtypeStruct((s_len, d), BF16),
                   jax.ShapeDtypeStruct((s_len, 256), F32),
                   jax.ShapeDtypeStruct((n_slab, s_len, LANES), F32)],
        compiler_params=_params(("arbitrary",)),
    )(x, sh, sc, g, w_in_t, rope)


def _block_rows(n, r, dil):
    start = n * (BLOCK * dil) + r
    if dil == 1:
        return pl.ds(pl.multiple_of(start, BLOCK), BLOCK)
    return pl.ds(start, BLOCK, stride=dil)


def _band_mask(n):
    ri = lax.broadcasted_iota(jnp.int32, (BLOCK, 2 * BLOCK), 0)
    cj = lax.broadcasted_iota(jnp.int32, (BLOCK, 2 * BLOCK), 1)
    cur = (cj >= BLOCK) & (cj - BLOCK <= ri)
    prev = (cj < BLOCK) & (cj >= ri) & (n > 0)
    return cur | prev


def _attn_fwd(qkv, group, dil):
    s_len = qkv.shape[1]
    nb = s_len // (BLOCK * dil)

    def body(q_ref, k_ref, v_ref, o_ref, lse_ref):
        lane = lax.broadcasted_iota(jnp.int32, (BLOCK, LANES), 1)
        first = lane < HEAD_DIM

        def block(t, carry):
            r, n = t // nb, t % nb
            cur = _block_rows(n, r, dil)
            prev = _block_rows(jnp.maximum(n - 1, 0), r, dil)
            q = q_ref[0, cur, :]
            kcat = jnp.concatenate([k_ref[0, prev, :], k_ref[0, cur, :]], axis=0).astype(BF16)
            vcat = jnp.concatenate([v_ref[0, prev, :], v_ref[0, cur, :]], axis=0).astype(BF16)
            valid = _band_mask(n)
            outs, lses = [], []
            for keep in (first, ~first):
                s = _dot(jnp.where(keep, q, 0.0).astype(BF16), kcat, NT)
                s = jnp.where(valid, s, MASKED)
                m = jnp.max(s, axis=-1, keepdims=True)
                p = jnp.exp(s - m)
                den = jnp.sum(p, axis=-1, keepdims=True)
                outs.append(_dot(p.astype(BF16), vcat, NN) / den)
                lses.append(m + jnp.log(den))
            o_ref[0, cur, :] = jnp.where(first, outs[0], outs[1])
            lse_ref[0, cur, :] = jnp.where(first, lses[0], lses[1])
            return carry

        lax.fori_loop(0, nb * dil, block, 0)

    def slab(base):
        return pl.BlockSpec((1, s_len, LANES), lambda s: (base + 2 * group + s, 0, 0))

    out = pl.BlockSpec((1, s_len, LANES), lambda s: (s, 0, 0))
    shape = jax.ShapeDtypeStruct((2, s_len, LANES), F32)
    return pl.pallas_call(
        body, name=f"attn_fwd_d{dil}", grid=(2,),
        in_specs=[slab(0), slab(6), slab(12)], out_specs=[out, out], out_shape=[shape, shape],
        compiler_params=_params(("arbitrary",)),
    )(qkv, qkv, qkv)


def _pool_mixed(u, halo, i, tm):
    ue = jnp.concatenate([halo, u], axis=0)
    s2 = ue + pltpu.roll(ue, 1, 0)
    s4 = s2 + pltpu.roll(s2, 2, 0)
    s8 = s4 + pltpu.roll(s4, 4, 0)
    s16 = s8 + pltpu.roll(s8, 8, 0)
    grp = lax.broadcasted_iota(jnp.int32, (tm, 256), 1) // HEAD_DIM
    pick = lambda a, b, c, e: jnp.where(grp == 0, a, jnp.where(grp == 1, b, jnp.where(grp == 2, c, e)))
    win_sum = pick(s2[HALO:], s4[HALO:], s8[HALO:], s16[HALO:])
    pos = (i * tm + lax.broadcasted_iota(jnp.int32, (tm, 256), 0)).astype(F32)
    count = jnp.minimum(pos + 1.0, pick(*[float(w) for w in POOL_WINDOWS]))
    return win_sum / count - u, count


def _mix_out(x, u_pool, o_g, lse_g, w_blk, b_pool, pool_scale, w_out_t, gt_m, g_post_mix, g_pre_ffn, sc_f, sh_f, tm):
    s_len, d = x.shape

    def body(x_ref, u_ref, uh_ref, o0, o1, o2, l0, l1, l2, wb_ref, bp_ref, ps_ref, wo_ref,
             gt_ref, g1_ref, g2_ref, sc_ref, sh_ref,
             x1_ref, y1_ref, h2_ref, cat_ref, attn_ref, lall_ref):
        i = pl.program_id(0)
        u = u_ref[...]
        halo = uh_ref[...] * (i > 0).astype(F32)
        mixed, _ = _pool_mixed(u, halo, i, tm)
        y = _dot(mixed.astype(BF16), wb_ref[...], NN) + bp_ref[...]
        pool = y * ps_ref[...]
        attn = []
        for s in range(2):
            la, lb, lc = l0[s], l1[s], l2[s]
            mx = jnp.maximum(jnp.maximum(la, lb), lc)
            ea, eb, ec = jnp.exp(la - mx), jnp.exp(lb - mx), jnp.exp(lc - mx)
            den = ea + eb + ec
            lall_ref[s] = mx + jnp.log(den)
            attn.append((ea / den) * o0[s] + (eb / den) * o1[s] + (ec / den) * o2[s])
        attn = jnp.concatenate(attn, axis=1)
        attn_ref[...] = attn
        cat = jnp.concatenate([pool, attn], axis=1).astype(BF16)
        cat_ref[...] = cat
        y1 = _dot(cat, wo_ref[...], NT)
        y1_ref[...] = y1
        x1 = x_ref[...] + gt_ref[...] * (y1 * _rstd(y1) * g1_ref[...])
        x1_ref[...] = x1
        h2 = (x1 * _rstd(x1) * g2_ref[...]) * (1.0 + sc_ref[...]) + sh_ref[...]
        h2_ref[...] = h2.astype(BF16)

    tile = lambda w: pl.BlockSpec((tm, w), lambda i: (i, 0))
    slab = pl.BlockSpec((2, tm, LANES), lambda i: (0, i, 0))
    const = lambda a: pl.BlockSpec(a.shape, lambda i: (0,) * a.ndim)
    return pl.pallas_call(
        body, name="mix_out", grid=(s_len // tm,),
        in_specs=[tile(d), tile(256), pl.BlockSpec((HALO, 256), lambda i: (_halo_before(i, tm), 0)),
                  slab, slab, slab, slab, slab, slab,
                  const(w_blk), const(b_pool), const(pool_scale), const(w_out_t),
                  const(gt_m), const(g_post_mix), const(g_pre_ffn), const(sc_f), const(sh_f)],
        out_specs=[tile(d), tile(d), tile(d), tile(512), tile(256), slab],
        out_shape=[jax.ShapeDtypeStruct((s_len, d), F32), jax.ShapeDtypeStruct((s_len, d), F32),
                   jax.ShapeDtypeStruct((s_len, d), BF16), jax.ShapeDtypeStruct((s_len, 512), BF16),
                   jax.ShapeDtypeStruct((s_len, 256), F32), jax.ShapeDtypeStruct((2, s_len, LANES), F32)],
        compiler_params=_params(("arbitrary",)),
    )(x, u_pool, u_pool, *o_g, *lse_g, w_blk, b_pool, pool_scale, w_out_t, gt_m, g_post_mix, g_pre_ffn, sc_f, sh_f)


def _conv_gate(gate_ext, cw_ref, cb_ref):
    gc = gate_ext * cw_ref[2:3, :] + pltpu.roll(gate_ext, 1, 0) * cw_ref[1:2, :] + pltpu.roll(gate_ext, 2, 0) * cw_ref[0:1, :]
    return gc[HALO:] + cb_ref[...]


def _ffn_fwd_loss(h2, x1, target, w_up_t, w_down, conv_w, conv_b, gt_f, g_post_ffn, tm, tf):
    s_len, d = x1.shape
    d_ff = w_down.shape[0]
    n_f = d_ff // tf

    def body(h_ref, hh_ref, x1_ref, tgt_ref, wg_ref, wv_ref, wd_ref, cw_ref, cb_ref, gt_ref, g_ref,
             gate_ref, val_ref, dy2_ref, dout_ref, sums_ref, loss_ref, acc_ref):
        i, j = pl.program_id(0), pl.program_id(1)

        @pl.when((i == 0) & (j == 0))
        def _():
            sums_ref[...] = jnp.zeros_like(sums_ref)
            loss_ref[...] = jnp.zeros_like(loss_ref)

        h_ext = jnp.concatenate([hh_ref[...], h_ref[...]], axis=0)
        gate_ext = _dot(h_ext, wg_ref[...], NT)
        row = lax.broadcasted_iota(jnp.int32, gate_ext.shape, 0)
        gate_ext = jnp.where((row < HALO) & (i == 0), 0.0, gate_ext)
        val = _dot(h_ref[...], wv_ref[...], NT)
        act, _ = _gelu_parts(_conv_gate(gate_ext, cw_ref, cb_ref))
        gate_ref[...] = gate_ext[HALO:].astype(BF16)
        val_ref[...] = val.astype(BF16)
        part = _dot((act * val).astype(BF16), wd_ref[...], NN)

        @pl.when(j == 0)
        def _():
            acc_ref[...] = part

        @pl.when(j > 0)
        def _():
            acc_ref[...] += part

        @pl.when(j == n_f - 1)
        def _():
            y2 = acc_ref[...]
            rstd = _rstd(y2)
            n = y2 * rstd
            rn = n * g_ref[...]
            err = x1_ref[...] + gt_ref[...] * rn - tgt_ref[...]
            loss_ref[...] += 0.5 * jnp.sum(jnp.mean(err * err, axis=-1, keepdims=True), axis=0, keepdims=True)
            dout = err * (1.0 / d)
            dout_ref[...] = dout
            drn = dout * gt_ref[...]
            sums_ref[0:1, :] += jnp.sum(dout * rn, axis=0, keepdims=True)
            sums_ref[1:2, :] += jnp.sum(drn * n, axis=0, keepdims=True)
            dy2_ref[...] = _norm_bwd(drn * g_ref[...], n, rstd).astype(BF16)

    tok = lambda w: pl.BlockSpec((tm, w), lambda i, j: (i, 0))
    tokf = pl.BlockSpec((tm, tf), lambda i, j: (i, j))
    vec = pl.BlockSpec((1, d), lambda i, j: (0, 0))
    return pl.pallas_call(
        body, name="ffn_fwd_loss", grid=(s_len // tm, n_f),
        in_specs=[tok(d), pl.BlockSpec((HALO, d), lambda i, j: (_halo_before(i, tm), 0)), tok(d), tok(d),
                  pl.BlockSpec((tf, d), lambda i, j: (j, 0)), pl.BlockSpec((tf, d), lambda i, j: (j + n_f, 0)),
                  pl.BlockSpec((tf, d), lambda i, j: (j, 0)),
                  pl.BlockSpec((3, tf), lambda i, j: (0, j)), pl.BlockSpec((1, tf), lambda i, j: (0, j)), vec, vec],
        out_specs=[tokf, tokf, tok(d), tok(d), pl.BlockSpec((8, d), lambda i, j: (0, 0)),
                   pl.BlockSpec((8, LANES), lambda i, j: (0, 0))],
        out_shape=[jax.ShapeDtypeStruct((s_len, d_ff), BF16), jax.ShapeDtypeStruct((s_len, d_ff), BF16),
                   jax.ShapeDtypeStruct((s_len, d), BF16), jax.ShapeDtypeStruct((s_len, d), F32),
                   jax.ShapeDtypeStruct((8, d), F32), jax.ShapeDtypeStruct((8, LANES), F32)],
        scratch_shapes=[pltpu.VMEM((tm, d), F32)],
        compiler_params=_params(("arbitrary", "arbitrary")),
    )(h2, h2, x1, target, w_up_t, w_up_t, w_down, conv_w, conv_b, gt_f, g_post_ffn)


def _ffn_bwd_act(dy2, gate, val, w_down, conv_w, conv_b, tm, tf):
    s_len, d = dy2.shape
    d_ff = w_down.shape[0]
    n_t = s_len // tm

    def body(dy_ref, g_ref, gh_ref, v_ref, wd_ref, cw_ref, cb_ref,
             dgc_ref, dval_ref, dwd_ref, dcw_ref, dcb_ref, acc_ref):
        i = pl.program_id(1)
        gate_ext = jnp.concatenate([gh_ref[...], g_ref[...]], axis=0).astype(F32)
        row = lax.broadcasted_iota(jnp.int32, gate_ext.shape, 0)
        gate_ext = jnp.where((row < HALO) & (i == 0), 0.0, gate_ext)
        act, dact = _gelu_parts(_conv_gate(gate_ext, cw_ref, cb_ref))
        val = v_ref[...].astype(F32)
        da = _dot(dy_ref[...], wd_ref[...], NT)
        dgc = da * val * dact
        dgc_ref[...] = dgc.astype(BF16)
        dval_ref[...] = (da * act).astype(BF16)
        dwd = _dot((act * val).astype(BF16), dy_ref[...], TN)
        taps = jnp.concatenate(
            [jnp.sum(dgc * pltpu.roll(gate_ext, 2 - k, 0)[HALO:], axis=0, keepdims=True) if k < 2
             else jnp.sum(dgc * gate_ext[HALO:], axis=0, keepdims=True) for k in range(3)], axis=0)
        bias = jnp.sum(dgc, axis=0, keepdims=True)

        @pl.when(i == 0)
        def _():
            acc_ref[...] = dwd
            dcw_ref[...] = taps
            dcb_ref[...] = bias

        @pl.when(i > 0)
        def _():
            acc_ref[...] += dwd
            dcw_ref[...] += taps
            dcb_ref[...] += bias

        @pl.when(i == n_t - 1)
        def _():
            dwd_ref[...] = acc_ref[...].astype(BF16)

    tokf = pl.BlockSpec((tm, tf), lambda j, i: (i, j))
    return pl.pallas_call(
        body, name="ffn_bwd_act", grid=(d_ff // tf, n_t),
        in_specs=[pl.BlockSpec((tm, d), lambda j, i: (i, 0)), tokf,
                  pl.BlockSpec((HALO, tf), lambda j, i: (_halo_before(i, tm), j)), tokf,
                  pl.BlockSpec((tf, d), lambda j, i: (j, 0)),
                  pl.BlockSpec((3, tf), lambda j, i: (0, j)), pl.BlockSpec((1, tf), lambda j, i: (0, j))],
        out_specs=[tokf, tokf, pl.BlockSpec((tf, d), lambda j, i: (j, 0)),
                   pl.BlockSpec((3, tf), lambda j, i: (0, j)), pl.BlockSpec((1, tf), lambda j, i: (0, j))],
        out_shape=[jax.ShapeDtypeStruct((s_len, d_ff), BF16), jax.ShapeDtypeStruct((s_len, d_ff), BF16),
                   jax.ShapeDtypeStruct((d_ff, d), BF16), jax.ShapeDtypeStruct((3, d_ff), F32),
                   jax.ShapeDtypeStruct((1, d_ff), F32)],
        scratch_shapes=[pltpu.VMEM((tf, d), F32)],
        compiler_params=_params(("arbitrary", "arbitrary")),
    )(dy2, gate, gate, val, w_down, conv_w, conv_b)


def _ffn_bwd_up(dgc, dval, w_up_t, conv_w, tm):
    s_len, d_ff = dgc.shape
    d = w_up_t.shape[1]
    n_t = s_len // tm

    def body(dg_ref, dgn_ref, dv_ref, cw_ref, w_ref, dup_ref, dh_ref):
        i = pl.program_id(0)
        nxt = dgn_ref[...].astype(F32) * (i < n_t - 1).astype(F32)
        ext = jnp.concatenate([dg_ref[...].astype(F32), nxt], axis=0)
        rows = tm + HALO
        dgate = (ext * cw_ref[2:3, :] + pltpu.roll(ext, rows - 1, 0) * cw_ref[1:2, :]
                 + pltpu.roll(ext, rows - 2, 0) * cw_ref[0:1, :])[:tm]
        dup = jnp.concatenate([dgate.astype(BF16), dv_ref[...]], axis=1)
        dup_ref[...] = dup
        dh_ref[...] = _dot(dup, w_ref[...], NN)

    tokf = pl.BlockSpec((tm, d_ff), lambda i: (i, 0))
    return pl.pallas_call(
        body, name="ffn_bwd_up", grid=(n_t,),
        in_specs=[tokf, pl.BlockSpec((HALO, d_ff), lambda i: (jnp.minimum((i + 1) * (tm // HALO), s_len // HALO - 1), 0)),
                  tokf, pl.BlockSpec((3, d_ff), lambda i: (0, 0)), pl.BlockSpec((2 * d_ff, d), lambda i: (0, 0))],
        out_specs=[pl.BlockSpec((tm, 2 * d_ff), lambda i: (i, 0)), pl.BlockSpec((tm, d), lambda i: (i, 0))],
        out_shape=[jax.ShapeDtypeStruct((s_len, 2 * d_ff), BF16), jax.ShapeDtypeStruct((s_len, d), F32)],
        compiler_params=_params(("arbitrary",)),
    )(dgc, dgc, dval, conv_w, w_up_t)


def _mix_bwd(dh2, dout, x1, y1, cat, attn, w_out_t, sc_f, g_pre_ffn, gt_m, g_post_mix, tm):
    s_len, d = x1.shape
    n_t = s_len // tm

    def body(dh_ref, do_ref, x1_ref, y1_ref, cat_ref, at_ref, wo_ref, sc_ref, g2_ref, gt_ref, g1_ref,
             dx1_ref, dpool_ref, dattn_ref, delta_ref, dwo_ref, sums_ref, acc_ref):
        i = pl.program_id(0)
        dh = dh_ref[...]
        x1 = x1_ref[...]
        r2 = _rstd(x1)
        n2 = x1 * r2
        ng = n2 * g2_ref[...]
        dng = dh * (1.0 + sc_ref[...])
        dx1 = do_ref[...] + _norm_bwd(dng * g2_ref[...], n2, r2)
        dx1_ref[...] = dx1
        y1 = y1_ref[...]
        r1 = _rstd(y1)
        n1 = y1 * r1
        drn = dx1 * gt_ref[...]
        dy1 = _norm_bwd(drn * g1_ref[...], n1, r1).astype(BF16)
        dcat = _dot(dy1, wo_ref[...], NN)
        dpool_ref[...] = dcat[:, 0:256]
        lane = lax.broadcasted_iota(jnp.int32, (tm, LANES), 1)
        first = lane < HEAD_DIM
        for s in range(2):
            da = dcat[:, 256 + s * LANES:256 + (s + 1) * LANES]
            dattn_ref[s] = da
            prod = da * at_ref[:, s * LANES:(s + 1) * LANES]
            tot = jnp.sum(prod, axis=-1, keepdims=True)
            lo = jnp.sum(jnp.where(first, prod, 0.0), axis=-1, keepdims=True)
            delta_ref[s] = jnp.where(first, lo, tot - lo)
        dwo = _dot(dy1, cat_ref[...], TN)
        sums = jnp.concatenate(
            [jnp.sum(dh, axis=0, keepdims=True), jnp.sum(dh * ng, axis=0, keepdims=True),
             jnp.sum(dng * n2, axis=0, keepdims=True), jnp.sum(dx1 * (n1 * g1_ref[...]), axis=0, keepdims=True),
             jnp.sum(drn * n1, axis=0, keepdims=True), jnp.zeros((3, d), F32)], axis=0)

        @pl.when(i == 0)
        def _():
            acc_ref[...] = dwo
            sums_ref[...] = sums

        @pl.when(i > 0)
        def _():
            acc_ref[...] += dwo
            sums_ref[...] += sums

        @pl.when(i == n_t - 1)
        def _():
            dwo_ref[...] = acc_ref[...].astype(BF16)

    tile = lambda w: pl.BlockSpec((tm, w), lambda i: (i, 0))
    slab = pl.BlockSpec((2, tm, LANES), lambda i: (0, i, 0))
    vec = pl.BlockSpec((1, d), lambda i: (0, 0))
    return pl.pallas_call(
        body, name="mix_bwd", grid=(n_t,),
        in_specs=[tile(d), tile(d), tile(d), tile(d), tile(512), tile(256),
                  pl.BlockSpec((d, 512), lambda i: (0, 0)), vec, vec, vec, vec],
        out_specs=[tile(d), tile(256), slab, slab, pl.BlockSpec((d, 512), lambda i: (0, 0)),
                   pl.BlockSpec((8, d), lambda i: (0, 0))],
        out_shape=[jax.ShapeDtypeStruct((s_len, d), F32), jax.ShapeDtypeStruct((s_len, 256), F32),
                   jax.ShapeDtypeStruct((2, s_len, LANES), F32), jax.ShapeDtypeStruct((2, s_len, LANES), F32),
                   jax.ShapeDtypeStruct((d, 512), BF16), jax.ShapeDtypeStruct((8, d), F32)],
        scratch_shapes=[pltpu.VMEM((d, 512), F32)],
        compiler_params=_params(("arbitrary",)),
    )(dh2, dout, x1, y1, cat, attn, w_out_t, sc_f, g_pre_ffn, gt_m, g_post_mix)


def _pool_bwd(dpool, u_pool, w_blk, b_pool, pool_scale, tm):
    s_len = dpool.shape[0]
    n_t = s_len // tm

    def body(dp_ref, dpn_ref, u_ref, uh_ref, wb_ref, bp_ref, ps_ref, du_ref, dwb_ref, sums_ref):
        i = pl.program_id(0)
        u = u_ref[...]
        mixed, _ = _pool_mixed(u, uh_ref[...] * (i > 0).astype(F32), i, tm)
        mixed_b = mixed.astype(BF16)
        y = _dot(mixed_b, wb_ref[...], NN) + bp_ref[...]
        dp = dp_ref[...]
        dy = dp * ps_ref[...]
        dwb = _dot(mixed_b, dy.astype(BF16), TN)
        sums = jnp.concatenate([jnp.sum(dy, axis=0, keepdims=True), jnp.sum(dp * y, axis=0, keepdims=True),
                                jnp.zeros((6, 256), F32)], axis=0)
        dp_ext = jnp.concatenate([dp, dpn_ref[...] * (i < n_t - 1).astype(F32)], axis=0)
        dmix = _dot((dp_ext * ps_ref[...]).astype(BF16), wb_ref[...], NT)
        rows = tm + HALO
        grp = lax.broadcasted_iota(jnp.int32, (rows, 256), 1) // HEAD_DIM
        pick = lambda a, b, c, e: jnp.where(grp == 0, a, jnp.where(grp == 1, b, jnp.where(grp == 2, c, e)))
        pos = (i * tm + lax.broadcasted_iota(jnp.int32, (rows, 256), 0)).astype(F32)
        z = dmix / jnp.minimum(pos + 1.0, pick(*[float(w) for w in POOL_WINDOWS]))
        f2 = z + pltpu.roll(z, rows - 1, 0)
        f4 = f2 + pltpu.roll(f2, rows - 2, 0)
        f8 = f4 + pltpu.roll(f4, rows - 4, 0)
        f16 = f8 + pltpu.roll(f8, rows - 8, 0)
        du_ref[...] = (pick(f2, f4, f8, f16) - dmix)[:tm]

        @pl.when(i == 0)
        def _():
            dwb_ref[...] = dwb
            sums_ref[...] = sums

        @pl.when(i > 0)
        def _():
            dwb_ref[...] += dwb
            sums_ref[...] += sums

    tile = pl.BlockSpec((tm, 256), lambda i: (i, 0))
    const = lambda a: pl.BlockSpec(a.shape, lambda i: (0,) * a.ndim)
    return pl.pallas_call(
        body, name="pool_bwd", grid=(n_t,),
        in_specs=[tile, pl.BlockSpec((HALO, 256), lambda i: (jnp.minimum((i + 1) * (tm // HALO), s_len // HALO - 1), 0)),
                  tile, pl.BlockSpec((HALO, 256), lambda i: (_halo_before(i, tm), 0)),
                  const(w_blk), const(b_pool), const(pool_scale)],
        out_specs=[tile, pl.BlockSpec((256, 256), lambda i: (0, 0)), pl.BlockSpec((8, 256), lambda i: (0, 0))],
        out_shape=[jax.ShapeDtypeStruct((s_len, 256), F32), jax.ShapeDtypeStruct((256, 256), F32),
                   jax.ShapeDtypeStruct((8, 256), F32)],
        compiler_params=_params(("arbitrary",)),
    )(dpool, dpool, u_pool, u_pool, w_blk, b_pool, pool_scale)


def _attn_bwd(qkv, dattn, lse_all, delta, group, dil):
    s_len = qkv.shape[1]
    nb = s_len // (BLOCK * dil)

    def body(q_ref, k_ref, v_ref, do_ref, l_ref, dl_ref, dq_ref, dk_ref, dv_ref):
        lane = lax.broadcasted_iota(jnp.int32, (BLOCK, LANES), 1)
        first = lane < HEAD_DIM
        dk_ref[...] = jnp.zeros_like(dk_ref)
        dv_ref[...] = jnp.zeros_like(dv_ref)

        def block(t, carry):
            r, n = t // nb, t % nb
            cur = _block_rows(n, r, dil)
            prev = _block_rows(jnp.maximum(n - 1, 0), r, dil)
            q = q_ref[0, cur, :]
            do = do_ref[0, cur, :]
            lse = l_ref[0, cur, :]
            dlt = dl_ref[0, cur, :]
            kcat = jnp.concatenate([k_ref[0, prev, :], k_ref[0, cur, :]], axis=0).astype(BF16)
            vcat = jnp.concatenate([v_ref[0, prev, :], v_ref[0, cur, :]], axis=0).astype(BF16)
            valid = _band_mask(n)
            dq, dkc, dvc = [], None, None
            for h, keep in enumerate((first, ~first)):
                col = slice(h * HEAD_DIM, h * HEAD_DIM + 1)
                qh = jnp.where(keep, q, 0.0).astype(BF16)
                doh = jnp.where(keep, do, 0.0).astype(BF16)
                s = _dot(qh, kcat, NT)
                p = jnp.where(valid, jnp.exp(s - lse[:, col]), 0.0)
                dp = _dot(doh, vcat, NT)
                ds = (p * (dp - dlt[:, col])).astype(BF16)
                dq.append(_dot(ds, kcat, NN))
                dk_h = _dot(ds, qh, TN)
                dv_h = _dot(p.astype(BF16), doh, TN)
                dkc = dk_h if dkc is None else dkc + dk_h
                dvc = dv_h if dvc is None else dvc + dv_h
            dq_ref[0, cur, :] = jnp.where(first, dq[0], dq[1])
            dk_ref[0, prev, :] += dkc[:BLOCK]
            dv_ref[0, prev, :] += dvc[:BLOCK]
            dk_ref[0, cur, :] += dkc[BLOCK:]
            dv_ref[0, cur, :] += dvc[BLOCK:]
            return carry

        lax.fori_loop(0, nb * dil, block, 0)

    def slab(base):
        return pl.BlockSpec((1, s_len, LANES), lambda s: (base + 2 * group + s, 0, 0))

    one = pl.BlockSpec((1, s_len, LANES), lambda s: (s, 0, 0))
    shape = jax.ShapeDtypeStruct((2, s_len, LANES), F32)
    return pl.pallas_call(
        body, name=f"attn_bwd_d{dil}", grid=(2,),
        in_specs=[slab(0), slab(6), slab(12), one, one, one],
        out_specs=[one, one, one], out_shape=[shape, shape, shape],
        compiler_params=_params(("arbitrary",)),
    )(qkv, qkv, qkv, dattn, lse_all, delta)


def _inproj_bwd(du, dqkv, rope, w_in_t, x, dx1, sc_m, g_pre_mix, tm):
    s_len, d = x.shape
    n_proj = w_in_t.shape[0]
    n_t = s_len // tm

    def body(du_ref, *refs):
        dref = refs[:9]
        rope_ref, w_ref, x_ref, dx1_ref, sc_ref, g_ref, dproj_ref, dx_ref, sums_ref = refs[9:]
        i = pl.program_id(0)
        cols = [du_ref[...].astype(BF16)]
        for kind in range(3):
            for grp in range(3):
                for s in range(2):
                    piece = dref[3 * grp + kind][s]
                    if kind < 2:
                        piece = _rope_bwd(piece, rope_ref)
                    if kind == 0:
                        piece = piece * (HEAD_DIM ** -0.5)
                    cols.append(piece.astype(BF16))
        dproj = jnp.concatenate(cols, axis=1)
        dproj_ref[...] = dproj
        dh = _dot(dproj, w_ref[...], NN)
        xv = x_ref[...]
        r = _rstd(xv)
        n = xv * r
        dng = dh * (1.0 + sc_ref[...])
        dx_ref[...] = dx1_ref[...] + _norm_bwd(dng * g_ref[...], n, r)
        sums = jnp.concatenate([jnp.sum(dh, axis=0, keepdims=True), jnp.sum(dh * (n * g_ref[...]), axis=0, keepdims=True),
                                jnp.sum(dng * n, axis=0, keepdims=True), jnp.zeros((5, d), F32)], axis=0)

        @pl.when(i == 0)
        def _():
            sums_ref[...] = sums

        @pl.when(i > 0)
        def _():
            sums_ref[...] += sums

    tile = lambda w: pl.BlockSpec((tm, w), lambda i: (i, 0))
    slab = pl.BlockSpec((2, tm, LANES), lambda i: (0, i, 0))
    vec = pl.BlockSpec((1, d), lambda i: (0, 0))
    return pl.pallas_call(
        body, name="inproj_bwd", grid=(n_t,),
        in_specs=[tile(256)] + [slab] * 9 + [pl.BlockSpec((3, tm, LANES), lambda i: (0, i, 0)),
                                             pl.BlockSpec((n_proj, d), lambda i: (0, 0)), tile(d), tile(d), vec, vec],
        out_specs=[tile(n_proj), tile(d), pl.BlockSpec((8, d), lambda i: (0, 0))],
        out_shape=[jax.ShapeDtypeStruct((s_len, n_proj), BF16), jax.ShapeDtypeStruct((s_len, d), F32),
                   jax.ShapeDtypeStruct((8, d), F32)],
        compiler_params=_params(("arbitrary",)),
    )(du, *dqkv, rope, w_in_t, x, dx1, sc_m, g_pre_mix)


def _wgrad(a, b, name, tk, tmm):
    s_len, m = a.shape
    n = b.shape[1]
    n_k = s_len // tk

    def body(a_ref, b_ref, o_ref, acc_ref):
        k = pl.program_id(1)
        part = _dot(a_ref[...], b_ref[...], TN)

        @pl.when(k == 0)
        def _():
            acc_ref[...] = part

        @pl.when(k > 0)
        def _():
            acc_ref[...] += part

        @pl.when(k == n_k - 1)
        def _():
            o_ref[...] = acc_ref[...].astype(BF16)

    return pl.pallas_call(
        body, name=name, grid=(m // tmm, n_k),
        in_specs=[pl.BlockSpec((tk, tmm), lambda j, k: (k, j)), pl.BlockSpec((tk, n), lambda j, k: (k, 0))],
        out_specs=pl.BlockSpec((tmm, n), lambda j, k: (j, 0)),
        out_shape=jax.ShapeDtypeStruct((m, n), BF16),
        scratch_shapes=[pltpu.VMEM((tmm, n), F32)],
        compiler_params=_params(("arbitrary", "arbitrary")),
    )(a, b)


def _place():
    return lax.axis_index("x"), lax.axis_index("y"), lax.axis_index("c")


def _peer(k):
    x, y, c = _place()
    bx, by, bc = (k >> 2) & 1, (k >> 1) & 1, k & 1
    return (x ^ bx if bx else x, y ^ by if by else y, c ^ bc if bc else c)


def _index(pos):
    return 4 * pos[0] + 2 * pos[1] + pos[2]


def _ada_exchange(c_rows, w_ada, b_ada_cols):
    d = c_rows.shape[1]
    ncol = w_ada.shape[1]

    def body(c_ref, w_ref, b_ref, call_ref, mod_ref, stage_ref, send_sems, recv_sems):
        me = _index(_place())
        call_ref[me] = c_ref[...]

        def gather(k):
            return pltpu.make_async_remote_copy(
                src_ref=c_ref, dst_ref=call_ref.at[me], send_sem=send_sems.at[0, k - 1], recv_sem=recv_sems.at[0, k - 1],
                device_id=_peer(k), device_id_type=MESH)

        for k in range(1, N_DEV):
            gather(k).start()
        for k in range(1, N_DEV):
            gather(k).wait_recv()
        cv = jnp.concatenate([call_ref[b, 0:1, :] for b in range(N_DEV)], axis=0)
        act = cv * jax.nn.sigmoid(cv)
        mod = lax.dot_general(act, w_ref[...], NN, preferred_element_type=F32,
                              precision=lax.Precision.HIGHEST) + b_ref[...]
        for b in range(N_DEV):
            stage_ref[b] = jnp.broadcast_to(mod[b:b + 1, :], (8, ncol))
        mod_ref[me] = stage_ref[me]

        def scatter(k):
            return pltpu.make_async_remote_copy(
                src_ref=stage_ref.at[_index(_peer(k))], dst_ref=mod_ref.at[me],
                send_sem=send_sems.at[1, k - 1], recv_sem=recv_sems.at[1, k - 1],
                device_id=_peer(k), device_id_type=MESH)

        for k in range(1, N_DEV):
            scatter(k).start()
        for k in range(1, N_DEV):
            scatter(k).wait_recv()
        for k in range(1, N_DEV):
            gather(k).wait_send()
            scatter(k).wait_send()

    vmem = pl.BlockSpec(memory_space=pltpu.VMEM)
    return pl.pallas_call(
        body, name="ada_exchange",
        in_specs=[vmem, vmem, vmem], out_specs=[vmem, vmem],
        out_shape=[jax.ShapeDtypeStruct((N_DEV, 8, d), F32), jax.ShapeDtypeStruct((N_DEV, 8, ncol), F32)],
        scratch_shapes=[pltpu.VMEM((N_DEV, 8, ncol), F32), pltpu.SemaphoreType.DMA((2, N_DEV - 1)),
                        pltpu.SemaphoreType.DMA((2, N_DEV - 1))],
        compiler_params=_params(),
    )(c_rows, w_ada, b_ada_cols)


def _gather_weights(shards):
    n_w = len(shards)

    def body(*refs):
        srcs, outs = refs[:n_w], refs[n_w:2 * n_w]
        send_sems, recv_sems, local_sems = refs[2 * n_w:]
        x, y, c = _place()
        me, sibling = (x, y, c), (x, y, 1 - c)
        chips = [(1 - x, y), (x, 1 - y), (1 - x, 1 - y)]

        def rows(w, pos):
            r = shards[w].shape[0]
            return outs[w].at[pl.ds(pl.multiple_of(_index(pos) * r, 16), r), :]

        def copy(k, w, block, to, own=False):
            return pltpu.make_async_remote_copy(
                src_ref=srcs[w] if own else rows(w, block), dst_ref=rows(w, block),
                send_sem=send_sems.at[k, w], recv_sem=recv_sems.at[k, w], device_id=to, device_id_type=MESH)

        mine = [pltpu.make_async_copy(srcs[w], rows(w, me), local_sems.at[w]) for w in range(n_w)]
        for cp in mine:
            cp.start()
        first = [copy(0, w, me, sibling, own=True) for w in range(n_w)]
        first += [copy(1 + j, w, me, (*chip, c), own=True) for j, chip in enumerate(chips) for w in range(n_w)]
        for cp in first:
            cp.start()
        passed = []
        for j, chip in enumerate(chips):
            for w in range(n_w):
                copy(1 + j, w, (*chip, c), me).wait_recv()
                fwd = copy(4 + j, w, (*chip, c), sibling)
                fwd.start()
                passed.append(fwd)
        for w in range(n_w):
            copy(0, w, sibling, me).wait_recv()
        for j, chip in enumerate(chips):
            for w in range(n_w):
                copy(4 + j, w, (*chip, 1 - c), me).wait_recv()
        for cp in first + passed:
            cp.wait_send()
        for cp in mine:
            cp.wait()

    hbm = pl.BlockSpec(memory_space=pltpu.HBM)
    return pl.pallas_call(
        body, name="gather_weights",
        in_specs=[hbm] * n_w, out_specs=[hbm] * n_w,
        out_shape=[jax.ShapeDtypeStruct((N_DEV * s.shape[0], s.shape[1]), s.dtype) for s in shards],
        scratch_shapes=[pltpu.SemaphoreType.DMA((N_DEV - 1, n_w)), pltpu.SemaphoreType.DMA((N_DEV - 1, n_w)),
                        pltpu.SemaphoreType.DMA((n_w,))],
        compiler_params=_params(),
    )(*shards)


def _scatter_grads(grads):
    n_w = len(grads)

    def body(*refs):
        srcs, outs = refs[:n_w], refs[n_w:2 * n_w]
        send_sems, recv_sems, local_sems = refs[2 * n_w:]
        me = _index(_place())

        def slab(w, dev):
            r = grads[w].shape[0] // N_DEV
            return srcs[w].at[pl.ds(pl.multiple_of(dev * r, 16), r), :]

        def copy(k, w):
            return pltpu.make_async_remote_copy(
                src_ref=slab(w, _index(_peer(k))), dst_ref=outs[w].at[me],
                send_sem=send_sems.at[k - 1, w], recv_sem=recv_sems.at[k - 1, w],
                device_id=_peer(k), device_id_type=MESH)

        mine = [pltpu.make_async_copy(slab(w, me), outs[w].at[me], local_sems.at[w]) for w in range(n_w)]
        for cp in mine:
            cp.start()
        sends = [copy(k, w) for k in range(1, N_DEV) for w in range(n_w)]
        for cp in sends:
            cp.start()
        for cp in sends:
            cp.wait_recv()
        for cp in sends:
            cp.wait_send()
        for cp in mine:
            cp.wait()

    hbm = pl.BlockSpec(memory_space=pltpu.HBM)
    return pl.pallas_call(
        body, name="scatter_grads",
        in_specs=[hbm] * n_w, out_specs=[hbm] * n_w,
        out_shape=[jax.ShapeDtypeStruct((N_DEV, g.shape[0] // N_DEV, g.shape[1]), g.dtype) for g in grads],
        scratch_shapes=[pltpu.SemaphoreType.DMA((N_DEV - 1, n_w)), pltpu.SemaphoreType.DMA((N_DEV - 1, n_w)),
                        pltpu.SemaphoreType.DMA((n_w,))],
        compiler_params=_params(),
    )(*grads)


def _allreduce_small(packed, name):
    rows = packed.shape[0]

    def body(p_ref, all_ref, tot_ref, send_sems, recv_sems):
        me = _index(_place())
        all_ref[me] = p_ref[...]

        def copy(k):
            return pltpu.make_async_remote_copy(
                src_ref=p_ref, dst_ref=all_ref.at[me], send_sem=send_sems.at[k - 1], recv_sem=recv_sems.at[k - 1],
                device_id=_peer(k), device_id_type=MESH)

        for k in range(1, N_DEV):
            copy(k).start()
        for k in range(1, N_DEV):
            copy(k).wait_recv()
        tot = all_ref[0]
        for dev in range(1, N_DEV):
            tot = tot + all_ref[dev]
        tot_ref[...] = tot
        for k in range(1, N_DEV):
            copy(k).wait_send()

    vmem = pl.BlockSpec(memory_space=pltpu.VMEM)
    return pl.pallas_call(
        body, name=name, in_specs=[vmem], out_specs=[vmem, vmem],
        out_shape=[jax.ShapeDtypeStruct((N_DEV, rows, LANES), F32), jax.ShapeDtypeStruct((rows, LANES), F32)],
        scratch_shapes=[pltpu.SemaphoreType.DMA((N_DEV - 1,)), pltpu.SemaphoreType.DMA((N_DEV - 1,))],
        compiler_params=_params(),
    )(packed)


def _sum_slabs(parts, name, tr):
    _, rows, cols = parts.shape

    def body(p_ref, o_ref):
        tot = p_ref[0].astype(F32)
        for dev in range(1, N_DEV):
            tot = tot + p_ref[dev].astype(F32)
        o_ref[...] = tot

    return pl.pallas_call(
        body, name=name, grid=(rows // tr,),
        in_specs=[pl.BlockSpec((N_DEV, tr, cols), lambda i: (0, i, 0))],
        out_specs=pl.BlockSpec((tr, cols), lambda i: (i, 0)),
        out_shape=jax.ShapeDtypeStruct((rows, cols), F32),
        compiler_params=_params(("arbitrary",)),
    )(parts)


def _adam_math(w, g, m, v):
    m = ADAM_B1 * m + (1.0 - ADAM_B1) * g
    v = ADAM_B2 * v + (1.0 - ADAM_B2) * (g * g)
    m_hat = m / (1.0 - ADAM_B1 ** ADAM_STEP)
    v_hat = v / (1.0 - ADAM_B2 ** ADAM_STEP)
    delta = -ADAM_LR * (m_hat / (jnp.sqrt(v_hat) + ADAM_EPS) + ADAM_WD * w)
    return delta, m, v


def _adam(w, g, m, v, name, tr):
    rows, cols = w.shape

    def body(w_ref, g_ref, m_ref, v_ref, d_ref, nm_ref, nv_ref):
        d_ref[...], nm_ref[...], nv_ref[...] = _adam_math(w_ref[...], g_ref[...], m_ref[...], v_ref[...])

    spec = pl.BlockSpec((tr, cols), lambda i: (i, 0))
    shape = jax.ShapeDtypeStruct((rows, cols), F32)
    return pl.pallas_call(
        body, name=name, grid=(rows // tr,), in_specs=[spec] * 4, out_specs=[spec] * 3,
        out_shape=[shape] * 3, compiler_params=_params(("arbitrary",)),
    )(w, g, m, v)


def _ada_grad_adam(c_all, dmod_cols, w, m, v, tr):
    rows, cols = w.shape

    def body(c_ref, dm_ref, w_ref, m_ref, v_ref, g_ref, d_ref, nm_ref, nv_ref):
        cv = c_ref[...]
        act = cv * jax.nn.sigmoid(cv)
        g = lax.dot_general(act, dm_ref[...], TN, preferred_element_type=F32, precision=lax.Precision.HIGHEST)
        g_ref[...] = g
        d_ref[...], nm_ref[...], nv_ref[...] = _adam_math(w_ref[...], g, m_ref[...], v_ref[...])

    spec = pl.BlockSpec((tr, cols), lambda i: (i, 0))
    shape = jax.ShapeDtypeStruct((rows, cols), F32)
    return pl.pallas_call(
        body, name="ada_grad_adam", grid=(rows // tr,),
        in_specs=[pl.BlockSpec((N_DEV, tr), lambda i: (0, i)), pl.BlockSpec((N_DEV, cols), lambda i: (0, 0)), spec, spec, spec],
        out_specs=[spec] * 4, out_shape=[shape] * 4, compiler_params=_params(("arbitrary",)),
    )(c_all, dmod_cols, w, m, v)


def _rope_tables(positions):
    s_len = positions.shape[0]
    inv_freq = ROPE_THETA ** (-jnp.arange(0, 2 * ROT_HALF, 2, dtype=F32) / (2 * ROT_HALF))
    ang = positions.astype(F32)[:, None] * inv_freq
    cos, sin = jnp.cos(ang), jnp.sin(ang)
    rest = HEAD_DIM - 2 * ROT_HALF
    zero = lambda n: jnp.zeros((s_len, n), F32)
    head = jnp.stack([jnp.concatenate([cos, cos, jnp.ones((s_len, rest), F32)], axis=1),
                      jnp.concatenate([-sin, zero(HEAD_DIM - ROT_HALF)], axis=1),
                      jnp.concatenate([zero(ROT_HALF), sin, zero(rest)], axis=1)])
    return jnp.tile(head, (1, 1, LANES // HEAD_DIM))


def _pad_rows(a, rows):
    return jnp.pad(a, ((0, rows - a.shape[0]), (0, 0)))


def _as_rows(a, rows):
    flat = a.reshape(-1)
    return jnp.pad(flat, (0, rows * LANES - flat.shape[0])).reshape(rows, LANES)


def _sequence_step(xs, target, rope, mods, gains, w_in_t, w_out_t, w_up_t, w_down_f, w_blk_b, b_pool_r, pool_scale_r,
                   conv_w_all, conv_b):
    sh_m, sc_m, gt_m, sh_f, sc_f, gt_f = mods
    g_pre_mix, g_post_mix, g_pre_ffn, g_post_ffn = gains
    h1, u_pool, qkv = _premix_inproj(xs, sh_m, sc_m, g_pre_mix, w_in_t, rope, tm=512)
    o_g, lse_g = [], []
    for gi, dil in enumerate(DILATIONS):
        o, lse = _attn_fwd(qkv, gi, dil)
        o_g.append(o)
        lse_g.append(lse)
    x1, y1, h2, cat, attn, lse_all = _mix_out(xs, u_pool, o_g, lse_g, w_blk_b, b_pool_r, pool_scale_r, w_out_t,
                                              gt_m, g_post_mix, g_pre_ffn, sc_f, sh_f, tm=256)
    gate, val, dy2, dout, sums_ffn, loss_loc = _ffn_fwd_loss(h2, x1, target, w_up_t, w_down_f, conv_w_all, conv_b,
                                                              gt_f, g_post_ffn, tm=512, tf=256)

    dgc, dval, dw_down, dconv_w, dconv_b = _ffn_bwd_act(dy2, gate, val, w_down_f, conv_w_all, conv_b, tm=512, tf=256)
    dup, dh2 = _ffn_bwd_up(dgc, dval, w_up_t, conv_w_all, tm=256)
    dw_up_t = _wgrad(dup, h2, "wgrad_up", tk=512, tmm=512)
    dx1, dpool, dattn, delta, dw_out_t, sums_mix = _mix_bwd(dh2, dout, x1, y1, cat, attn, w_out_t, sc_f, g_pre_ffn,
                                                           gt_m, g_post_mix, tm=256)
    du, dw_blk, sums_pool = _pool_bwd(dpool, u_pool, w_blk_b, b_pool_r, pool_scale_r, tm=512)
    dqkv = []
    for gi, dil in enumerate(DILATIONS):
        dqkv += list(_attn_bwd(qkv, dattn, lse_all, delta, gi, dil))
    dproj, grad_x, sums_in = _inproj_bwd(du, dqkv, rope, w_in_t, xs, dx1, sc_m, g_pre_mix, tm=256)
    dw_in_t = _wgrad(dproj, h1, "wgrad_in", tk=512, tmm=640)
    return (loss_loc, grad_x, dw_in_t, dw_out_t, dw_up_t, dw_down, dw_blk, dconv_w, dconv_b,
            sums_in, sums_mix, sums_ffn, sums_pool)


def kernel(x, c, positions, w_ada, b_ada, g_pre_mix, g_post_mix, g_pre_ffn, g_post_ffn, w_in, w_pool, b_pool, pool_scale, w_out, w_up, conv_w, conv_b, w_down, loss_target, m_w_ada, m_b_ada, m_g_pre_mix, m_g_post_mix, m_g_pre_ffn, m_g_post_ffn, m_w_in, m_w_pool, m_b_pool, m_pool_scale, m_w_out, m_w_up, m_conv_w, m_conv_b, m_w_down, v_w_ada, v_b_ada, v_g_pre_mix, v_g_post_mix, v_g_pre_ffn, v_g_post_ffn, v_w_in, v_w_pool, v_b_pool, v_pool_scale, v_w_out, v_w_up, v_conv_w, v_conv_b, v_w_down):
    s_len, d = x.shape[1], x.shape[2]
    d_ff = w_down.shape[1] * N_DEV
    me = _index(_place())
    xs, target = x[0], loss_target[0]

    ncol = w_ada.shape[2]
    b_cols = lax.dynamic_slice(b_ada, (0, me * ncol), (1, ncol))
    c_all, mod = _ada_exchange(jnp.broadcast_to(c, (8, d)), w_ada[0], b_cols)
    c_all = c_all[:, 0, :]
    sh_m, sc_m, gt_m, sh_f, sc_f, gt_f = [mod[:, 0, :].reshape(1, -1)[:, k * d:(k + 1) * d] for k in range(6)]

    w_in_t, w_out_t, w_up_t, w_down_f = _gather_weights(
        [w_in[0].T.astype(BF16), w_out[0].T.astype(BF16), w_up[0].T.astype(BF16), w_down[0].astype(BF16)])

    rope = _rope_tables(positions[0])
    w_blk = jnp.zeros((256, 256), F32)
    for gi in range(4):
        w_blk = lax.dynamic_update_slice(w_blk, w_pool[0, gi], (gi * HEAD_DIM, gi * HEAD_DIM))
    w_blk_b = w_blk.astype(BF16)
    b_pool_r, pool_scale_r = b_pool.reshape(1, 256), pool_scale.reshape(1, 256)

    cw_rows = d_ff // LANES
    cw_pack = jnp.concatenate([_pad_rows(lax.dynamic_update_slice(jnp.zeros((1, d_ff), F32), conv_w[0, k:k + 1], (0, me * (d_ff // N_DEV))).reshape(cw_rows, LANES), 24) for k in range(3)], axis=0)
    _, cw_tot = _allreduce_small(cw_pack, "gather_conv_w")
    conv_w_all = jnp.concatenate([cw_tot[24 * k:24 * k + cw_rows].reshape(1, d_ff) for k in range(3)], axis=0)

    (loss_loc, grad_x, dw_in_t, dw_out_t, dw_up_t, dw_down, dw_blk, dconv_w, dconv_b,
     sums_in, sums_mix, sums_ffn, sums_pool) = _sequence_step(
        xs, target, rope, (sh_m, sc_m, gt_m, sh_f, sc_f, gt_f), (g_pre_mix, g_post_mix, g_pre_ffn, g_post_ffn),
        w_in_t, w_out_t, w_up_t, w_down_f, w_blk_b, b_pool_r, pool_scale_r, conv_w_all, conv_b)

    parts = _scatter_grads([dw_in_t, dw_out_t, dw_up_t, dw_down])
    g_w_in = _sum_slabs(parts[0], "sum_w_in", 64).T
    g_w_out = _sum_slabs(parts[1], "sum_w_out", 128).T
    g_w_up = _sum_slabs(parts[2], "sum_w_up", 64).T
    g_w_down = _sum_slabs(parts[3], "sum_w_down", 32)

    dmod = jnp.concatenate([sums_in[0:1], sums_in[1:2], sums_mix[3:4], sums_mix[0:1], sums_mix[1:2], sums_ffn[0:1]], axis=1)
    dw_pool = jnp.stack([dw_blk[gi * HEAD_DIM:(gi + 1) * HEAD_DIM, gi * HEAD_DIM:(gi + 1) * HEAD_DIM] for gi in range(4)])
    pieces = [(dmod, 48), (sums_in[2:3], 8), (sums_mix[4:5], 8), (sums_mix[2:3], 8), (sums_ffn[1:2], 8),
              (dw_pool, 128), (sums_pool[0:1], 8), (sums_pool[1:2], 8), (dconv_b, 24),
              (dconv_w[0:1], 24), (dconv_w[1:2], 24), (dconv_w[2:3], 24)]
    packed = jnp.concatenate([_as_rows(a, r) for a, r in pieces], axis=0)
    gathered, total = _allreduce_small(packed, "allreduce_small")
    n_rep = 248
    rep_w = [b_ada, g_pre_mix, g_post_mix, g_pre_ffn, g_post_ffn, w_pool, b_pool, pool_scale, conv_b]
    rep_m = [m_b_ada, m_g_pre_mix, m_g_post_mix, m_g_pre_ffn, m_g_post_ffn, m_w_pool, m_b_pool, m_pool_scale, m_conv_b]
    rep_v = [v_b_ada, v_g_pre_mix, v_g_post_mix, v_g_pre_ffn, v_g_post_ffn, v_w_pool, v_b_pool, v_pool_scale, v_conv_b]
    rep_rows = [r for _, r in pieces[:9]]
    pack_rep = lambda arrs: jnp.concatenate([_as_rows(a, r) for a, r in zip(arrs, rep_rows)], axis=0)
    rep_g = total[:n_rep]
    rep_d, rep_nm, rep_nv = _adam(pack_rep(rep_w), rep_g, pack_rep(rep_m), pack_rep(rep_v), "adam_small", n_rep)

    def unpack(p):
        out, row = [], 0
        for a, r in zip(rep_w, rep_rows):
            out.append(p[row:row + r].reshape(-1)[:a.size].reshape(a.shape))
            row += r
        return out

    g_rep, d_rep, nm_rep, nv_rep = unpack(rep_g), unpack(rep_d), unpack(rep_nm), unpack(rep_nv)

    fcol = d_ff // N_DEV
    g_cw_full = jnp.concatenate([total[n_rep + 24 * k:n_rep + 24 * k + cw_rows].reshape(1, d_ff) for k in range(3)], axis=0)
    g_cw = lax.dynamic_slice(g_cw_full, (0, me * fcol), (3, fcol))
    d_cw, nm_cw, nv_cw = _adam(conv_w[0], g_cw, m_conv_w[0], v_conv_w[0], "adam_conv_w", 3)

    dmod_all = gathered[:, :48].reshape(N_DEV, 6 * d)
    dmod_cols = lax.dynamic_slice(dmod_all, (0, me * ncol), (N_DEV, ncol))
    g_ada, d_ada, nm_ada, nv_ada = _ada_grad_adam(c_all, dmod_cols, w_ada[0], m_w_ada[0], v_w_ada[0], 256)

    big = {}
    for nm, w, g, m, v, tr in (("w_in", w_in, g_w_in, m_w_in, v_w_in, 256), ("w_out", w_out, g_w_out, m_w_out, v_w_out, 256),
                               ("w_up", w_up, g_w_up, m_w_up, v_w_up, 256), ("w_down", w_down, g_w_down, m_w_down, v_w_down, 88)):
        big[nm] = (g,) + tuple(_adam(w[0], g, m[0], v[0], "adam_" + nm, tr))

    loss = lax.psum(loss_loc[0, 0], ("x", "y", "c"))

    def group(k):
        rep = (g_rep, d_rep, nm_rep, nv_rep)[k]
        ada = (g_ada, d_ada, nm_ada, nv_ada)[k][None]
        cw = (g_cw, d_cw, nm_cw, nv_cw)[k][None]
        return [ada, rep[0], rep[1], rep[2], rep[3], rep[4], big["w_in"][k][None], rep[5], rep[6], rep[7],
                big["w_out"][k][None], big["w_up"][k][None], cw, rep[8], big["w_down"][k][None]]

    return (loss, grad_x[None], *group(0), *group(1), *group(2), *group(3))
```

```python
import functools
import math

import jax
import jax.numpy as jnp
from jax import lax
from jax.experimental import pallas as pl
from jax.experimental.pallas import tpu as pltpu

F32 = jnp.float32
BF16 = jnp.bfloat16
MESH = pl.DeviceIdType.MESH

N_DEV = 8
HEAD_DIM = 64
ROT_HALF = 8
ROPE_THETA = 500000.0
POOL_WINDOWS = (2, 4, 8, 16)
DILATIONS = (1, 4, 16)
BLOCK = 128
NORM_EPS = 1e-6
HALO = 16
MASKED = -1e30

ADAM_LR = 0.001
ADAM_B1 = 0.9
ADAM_B2 = 0.999
ADAM_EPS = 1e-08
ADAM_WD = 0.01
ADAM_STEP = 10

V7X_VMEM_LIMIT = 56 * 1024 * 1024
LANES = 128

NT = (((1,), (1,)), ((), ()))
NN = (((1,), (0,)), ((), ()))
TN = (((0,), (0,)), ((), ()))


def _dot(a, b, dims):
    return lax.dot_general(a, b, dims, preferred_element_type=F32)


def _params(sem=None, vmem=V7X_VMEM_LIMIT):
    if sem is None:
        return pltpu.CompilerParams(vmem_limit_bytes=vmem)
    return pltpu.CompilerParams(dimension_semantics=sem, vmem_limit_bytes=vmem)


def _rstd(v):
    return lax.rsqrt(jnp.mean(v * v, axis=-1, keepdims=True) + NORM_EPS)


def _norm_bwd(dn, n, rstd):
    return rstd * (dn - n * jnp.mean(dn * n, axis=-1, keepdims=True))


def _rope_fwd(p, rope_ref):
    return p * rope_ref[0] + pltpu.roll(p, LANES - ROT_HALF, 1) * rope_ref[1] + pltpu.roll(p, ROT_HALF, 1) * rope_ref[2]


def _rope_bwd(dp, rope_ref):
    return dp * rope_ref[0] + pltpu.roll(dp * rope_ref[1], ROT_HALF, 1) + pltpu.roll(dp * rope_ref[2], LANES - ROT_HALF, 1)


def _gelu_parts(v):
    k = math.sqrt(2.0 / math.pi)
    t = jnp.tanh(k * (v + 0.044715 * v * v * v))
    g = 0.5 * v * (1.0 + t)
    dg = 0.5 * (1.0 + t) + 0.5 * v * (1.0 - t * t) * k * (1.0 + 3.0 * 0.044715 * v * v)
    return g, dg


def _halo_before(i, tile):
    return jnp.maximum(i * (tile // HALO) - 1, 0)


def _premix_inproj(x, sh, sc, g, w_in_t, rope, tm):
    s_len, d = x.shape
    n_proj = w_in_t.shape[0]
    n_slab = (n_proj - 256) // LANES

    def body(x_ref, sh_ref, sc_ref, g_ref, w_ref, rope_ref, h_ref, up_ref, qkv_ref):
        xv = x_ref[...]
        h = (xv * _rstd(xv) * g_ref[...]) * (1.0 + sc_ref[...]) + sh_ref[...]
        hb = h.astype(BF16)
        h_ref[...] = hb
        up_ref[...] = _dot(hb, w_ref[0:256, :], NT)
        for pair in range(n_slab // 2):
            p = _dot(hb, w_ref[256 + 256 * pair:512 + 256 * pair, :], NT)
            for half in range(2):
                ph = p[:, half * LANES:(half + 1) * LANES]
                if pair < 6:
                    ph = _rope_fwd(ph, rope_ref)
                if pair < 3:
                    ph = ph * (HEAD_DIM ** -0.5)
                qkv_ref[2 * pair + half] = ph

    vec = pl.BlockSpec((1, d), lambda i: (0, 0))
    return pl.pallas_call(
        body, name="premix_inproj", grid=(s_len // tm,),
        in_specs=[pl.BlockSpec((tm, d), lambda i: (i, 0)), vec, vec, vec,
                  pl.BlockSpec((n_proj, d), lambda i: (0, 0)),
                  pl.BlockSpec((3, tm, LANES), lambda i: (0, i, 0))],
        out_specs=[pl.BlockSpec((tm, d), lambda i: (i, 0)),
                   pl.BlockSpec((tm, 256), lambda i: (i, 0)),
                   pl.BlockSpec((n_slab, tm, LANES), lambda i: (0, i, 0))],
        out_shape=[jax.ShapeDtypeStruct((s_len, d), BF16),
                   jax.ShapeDtypeStruct((s_len, 256), F32),
                   jax.ShapeDtypeStruct((n_slab, s_len, LANES), F32)],
        compiler_params=_params(("arbitrary",)),
    )(x, sh, sc, g, w_in_t, rope)


def _block_rows(n, r, dil):
    start = n * (BLOCK * dil) + r
    if dil == 1:
        return pl.ds(pl.multiple_of(start, BLOCK), BLOCK)
    return pl.ds(start, BLOCK, stride=dil)


def _band_mask(n):
    ri = lax.broadcasted_iota(jnp.int32, (BLOCK, 2 * BLOCK), 0)
    cj = lax.broadcasted_iota(jnp.int32, (BLOCK, 2 * BLOCK), 1)
    cur = (cj >= BLOCK) & (cj - BLOCK <= ri)
    prev = (cj < BLOCK) & (cj >= ri) & (n > 0)
    return cur | prev


def _attn_fwd(qkv, group, dil):
    s_len = qkv.shape[1]
    nb = s_len // (BLOCK * dil)

    def body(q_ref, k_ref, v_ref, o_ref, lse_ref):
        lane = lax.broadcasted_iota(jnp.int32, (BLOCK, LANES), 1)
        first = lane < HEAD_DIM

        def block(t, carry):
            r, n = t // nb, t % nb
            cur = _block_rows(n, r, dil)
            prev = _block_rows(jnp.maximum(n - 1, 0), r, dil)
            q = q_ref[0, cur, :]
            kcat = jnp.concatenate([k_ref[0, prev, :], k_ref[0, cur, :]], axis=0).astype(BF16)
            vcat = jnp.concatenate([v_ref[0, prev, :], v_ref[0, cur, :]], axis=0).astype(BF16)
            valid = _band_mask(n)
            outs, lses = [], []
            for keep in (first, ~first):
                s = _dot(jnp.where(keep, q, 0.0).astype(BF16), kcat, NT)
                s = jnp.where(valid, s, MASKED)
                m = jnp.max(s, axis=-1, keepdims=True)
                p = jnp.exp(s - m)
                den = jnp.sum(p, axis=-1, keepdims=True)
                outs.append(_dot(p.astype(BF16), vcat, NN) / den)
                lses.append(m + jnp.log(den))
            o_ref[0, cur, :] = jnp.where(first, outs[0], outs[1])
            lse_ref[0, cur, :] = jnp.where(first, lses[0], lses[1])
            return carry

        lax.fori_loop(0, nb * dil, block, 0)

    def slab(base):
        return pl.BlockSpec((1, s_len, LANES), lambda s: (base + 2 * group + s, 0, 0))

    out = pl.BlockSpec((1, s_len, LANES), lambda s: (s, 0, 0))
    shape = jax.ShapeDtypeStruct((2, s_len, LANES), F32)
    return pl.pallas_call(
        body, name=f"attn_fwd_d{dil}", grid=(2,),
        in_specs=[slab(0), slab(6), slab(12)], out_specs=[out, out], out_shape=[shape, shape],
        compiler_params=_params(("arbitrary",)),
    )(qkv, qkv, qkv)


def _pool_mixed(u, halo, i, tm):
    ue = jnp.concatenate([halo, u], axis=0)
    s2 = ue + pltpu.roll(ue, 1, 0)
    s4 = s2 + pltpu.roll(s2, 2, 0)
    s8 = s4 + pltpu.roll(s4, 4, 0)
    s16 = s8 + pltpu.roll(s8, 8, 0)
    grp = lax.broadcasted_iota(jnp.int32, (tm, 256), 1) // HEAD_DIM
    pick = lambda a, b, c, e: jnp.where(grp == 0, a, jnp.where(grp == 1, b, jnp.where(grp == 2, c, e)))
    win_sum = pick(s2[HALO:], s4[HALO:], s8[HALO:], s16[HALO:])
    pos = (i * tm + lax.broadcasted_iota(jnp.int32, (tm, 256), 0)).astype(F32)
    count = jnp.minimum(pos + 1.0, pick(*[float(w) for w in POOL_WINDOWS]))
    return win_sum / count - u, count


def _mix_out(x, u_pool, o_g, lse_g, w_blk, b_pool, pool_scale, w_out_t, gt_m, g_post_mix, g_pre_ffn, sc_f, sh_f, tm):
    s_len, d = x.shape

    def body(x_ref, u_ref, uh_ref, o0, o1, o2, l0, l1, l2, wb_ref, bp_ref, ps_ref, wo_ref,
             gt_ref, g1_ref, g2_ref, sc_ref, sh_ref,
             x1_ref, y1_ref, h2_ref, cat_ref, attn_ref, lall_ref):
        i = pl.program_id(0)
        u = u_ref[...]
        halo = uh_ref[...] * (i > 0).astype(F32)
        mixed, _ = _pool_mixed(u, halo, i, tm)
        y = _dot(mixed.astype(BF16), wb_ref[...], NN) + bp_ref[...]
        pool = y * ps_ref[...]
        attn = []
        for s in range(2):
            la, lb, lc = l0[s], l1[s], l2[s]
            mx = jnp.maximum(jnp.maximum(la, lb), lc)
            ea, eb, ec = jnp.exp(la - mx), jnp.exp(lb - mx), jnp.exp(lc - mx)
            den = ea + eb + ec
            lall_ref[s] = mx + jnp.log(den)
            attn.append((ea / den) * o0[s] + (eb / den) * o1[s] + (ec / den) * o2[s])
        attn = jnp.concatenate(attn, axis=1)
        attn_ref[...] = attn
        cat = jnp.concatenate([pool, attn], axis=1).astype(BF16)
        cat_ref[...] = cat
        y1 = _dot(cat, wo_ref[...], NT)
        y1_ref[...] = y1
        x1 = x_ref[...] + gt_ref[...] * (y1 * _rstd(y1) * g1_ref[...])
        x1_ref[...] = x1
        h2 = (x1 * _rstd(x1) * g2_ref[...]) * (1.0 + sc_ref[...]) + sh_ref[...]
        h2_ref[...] = h2.astype(BF16)

    tile = lambda w: pl.BlockSpec((tm, w), lambda i: (i, 0))
    slab = pl.BlockSpec((2, tm, LANES), lambda i: (0, i, 0))
    const = lambda a: pl.BlockSpec(a.shape, lambda i: (0,) * a.ndim)
    return pl.pallas_call(
        body, name="mix_out", grid=(s_len // tm,),
        in_specs=[tile(d), tile(256), pl.BlockSpec((HALO, 256), lambda i: (_halo_before(i, tm), 0)),
                  slab, slab, slab, slab, slab, slab,
                  const(w_blk), const(b_pool), const(pool_scale), const(w_out_t),
                  const(gt_m), const(g_post_mix), const(g_pre_ffn), const(sc_f), const(sh_f)],
        out_specs=[tile(d), tile(d), tile(d), tile(512), tile(256), slab],
        out_shape=[jax.ShapeDtypeStruct((s_len, d), F32), jax.ShapeDtypeStruct((s_len, d), F32),
                   jax.ShapeDtypeStruct((s_len, d), BF16), jax.ShapeDtypeStruct((s_len, 512), BF16),
                   jax.ShapeDtypeStruct((s_len, 256), F32), jax.ShapeDtypeStruct((2, s_len, LANES), F32)],
        compiler_params=_params(("arbitrary",)),
    )(x, u_pool, u_pool, *o_g, *lse_g, w_blk, b_pool, pool_scale, w_out_t, gt_m, g_post_mix, g_pre_ffn, sc_f, sh_f)


def _conv_gate(gate_ext, cw_ref, cb_ref):
    gc = gate_ext * cw_ref[2:3, :] + pltpu.roll(gate_ext, 1, 0) * cw_ref[1:2, :] + pltpu.roll(gate_ext, 2, 0) * cw_ref[0:1, :]
    return gc[HALO:] + cb_ref[...]


def _ffn_fwd_loss(h2, x1, target, w_up_t, w_down, conv_w, conv_b, gt_f, g_post_ffn, tm, tf):
    s_len, d = x1.shape
    d_ff = w_down.shape[0]
    n_f = d_ff // tf

    def body(h_ref, hh_ref, x1_ref, tgt_ref, wg_ref, wv_ref, wd_ref, cw_ref, cb_ref, gt_ref, g_ref,
             gate_ref, val_ref, dy2_ref, dout_ref, sums_ref, loss_ref, acc_ref):
        i, j = pl.program_id(0), pl.program_id(1)

        @pl.when((i == 0) & (j == 0))
        def _():
            sums_ref[...] = jnp.zeros_like(sums_ref)
            loss_ref[...] = jnp.zeros_like(loss_ref)

        h_ext = jnp.concatenate([hh_ref[...], h_ref[...]], axis=0)
        gate_ext = _dot(h_ext, wg_ref[...], NT)
        row = lax.broadcasted_iota(jnp.int32, gate_ext.shape, 0)
        gate_ext = jnp.where((row < HALO) & (i == 0), 0.0, gate_ext)
        val = _dot(h_ref[...], wv_ref[...], NT)
        act, _ = _gelu_parts(_conv_gate(gate_ext, cw_ref, cb_ref))
        gate_ref[...] = gate_ext[HALO:].astype(BF16)
        val_ref[...] = val.astype(BF16)
        part = _dot((act * val).astype(BF16), wd_ref[...], NN)

        @pl.when(j == 0)
        def _():
            acc_ref[...] = part

        @pl.when(j > 0)
        def _():
            acc_ref[...] += part

        @pl.when(j == n_f - 1)
        def _():
            y2 = acc_ref[...]
            rstd = _rstd(y2)
            n = y2 * rstd
            rn = n * g_ref[...]
            err = x1_ref[...] + gt_ref[...] * rn - tgt_ref[...]
            loss_ref[...] += 0.5 * jnp.sum(jnp.mean(err * err, axis=-1, keepdims=True), axis=0, keepdims=True)
            dout = err * (1.0 / d)
            dout_ref[...] = dout
            drn = dout * gt_ref[...]
            sums_ref[0:1, :] += jnp.sum(dout * rn, axis=0, keepdims=True)
            sums_ref[1:2, :] += jnp.sum(drn * n, axis=0, keepdims=True)
            dy2_ref[...] = _norm_bwd(drn * g_ref[...], n, rstd).astype(BF16)

    tok = lambda w: pl.BlockSpec((tm, w), lambda i, j: (i, 0))
    tokf = pl.BlockSpec((tm, tf), lambda i, j: (i, j))
    vec = pl.BlockSpec((1, d), lambda i, j: (0, 0))
    return pl.pallas_call(
        body, name="ffn_fwd_loss", grid=(s_len // tm, n_f),
        in_specs=[tok(d), pl.BlockSpec((HALO, d), lambda i, j: (_halo_before(i, tm), 0)), tok(d), tok(d),
                  pl.BlockSpec((tf, d), lambda i, j: (j, 0)), pl.BlockSpec((tf, d), lambda i, j: (j + n_f, 0)),
                  pl.BlockSpec((tf, d), lambda i, j: (j, 0)),
                  pl.BlockSpec((3, tf), lambda i, j: (0, j)), pl.BlockSpec((1, tf), lambda i, j: (0, j)), vec, vec],
        out_specs=[tokf, tokf, tok(d), tok(d), pl.BlockSpec((8, d), lambda i, j: (0, 0)),
                   pl.BlockSpec((8, LANES), lambda i, j: (0, 0))],
        out_shape=[jax.ShapeDtypeStruct((s_len, d_ff), BF16), jax.ShapeDtypeStruct((s_len, d_ff), BF16),
                   jax.ShapeDtypeStruct((s_len, d), BF16), jax.ShapeDtypeStruct((s_len, d), F32),
                   jax.ShapeDtypeStruct((8, d), F32), jax.ShapeDtypeStruct((8, LANES), F32)],
        scratch_shapes=[pltpu.VMEM((tm, d), F32)],
        compiler_params=_params(("arbitrary", "arbitrary")),
    )(h2, h2, x1, target, w_up_t, w_up_t, w_down, conv_w, conv_b, gt_f, g_post_ffn)


def _ffn_bwd_act(dy2, gate, val, w_down, conv_w, conv_b, tm, tf):
    s_len, d = dy2.shape
    d_ff = w_down.shape[0]
    n_t = s_len // tm

    def body(dy_ref, g_ref, gh_ref, v_ref, wd_ref, cw_ref, cb_ref,
             dgc_ref, dval_ref, dwd_ref, dcw_ref, dcb_ref, acc_ref):
        i = pl.program_id(1)
        gate_ext = jnp.concatenate([gh_ref[...], g_ref[...]], axis=0).astype(F32)
        row = lax.broadcasted_iota(jnp.int32, gate_ext.shape, 0)
        gate_ext = jnp.where((row < HALO) & (i == 0), 0.0, gate_ext)
        act, dact = _gelu_parts(_conv_gate(gate_ext, cw_ref, cb_ref))
        val = v_ref[...].astype(F32)
        da = _dot(dy_ref[...], wd_ref[...], NT)
        dgc = da * val * dact
        dgc_ref[...] = dgc.astype(BF16)
        dval_ref[...] = (da * act).astype(BF16)
        dwd = _dot((act * val).astype(BF16), dy_ref[...], TN)
        taps = jnp.concatenate(
            [jnp.sum(dgc * pltpu.roll(gate_ext, 2 - k, 0)[HALO:], axis=0, keepdims=True) if k < 2
             else jnp.sum(dgc * gate_ext[HALO:], axis=0, keepdims=True) for k in range(3)], axis=0)
        bias = jnp.sum(dgc, axis=0, keepdims=True)

        @pl.when(i == 0)
        def _():
            acc_ref[...] = dwd
            dcw_ref[...] = taps
            dcb_ref[...] = bias

        @pl.when(i > 0)
        def _():
            acc_ref[...] += dwd
            dcw_ref[...] += taps
            dcb_ref[...] += bias

        @pl.when(i == n_t - 1)
        def _():
            dwd_ref[...] = acc_ref[...].astype(BF16)

    tokf = pl.BlockSpec((tm, tf), lambda j, i: (i, j))
    return pl.pallas_call(
        body, name="ffn_bwd_act", grid=(d_ff // tf, n_t),
        in_specs=[pl.BlockSpec((tm, d), lambda j, i: (i, 0)), tokf,
                  pl.BlockSpec((HALO, tf), lambda j, i: (_halo_before(i, tm), j)), tokf,
                  pl.BlockSpec((tf, d), lambda j, i: (j, 0)),
                  pl.BlockSpec((3, tf), lambda j, i: (0, j)), pl.BlockSpec((1, tf), lambda j, i: (0, j))],
        out_specs=[tokf, tokf, pl.BlockSpec((tf, d), lambda j, i: (j, 0)),
                   pl.BlockSpec((3, tf), lambda j, i: (0, j)), pl.BlockSpec((1, tf), lambda j, i: (0, j))],
        out_shape=[jax.ShapeDtypeStruct((s_len, d_ff), BF16), jax.ShapeDtypeStruct((s_len, d_ff), BF16),
                   jax.ShapeDtypeStruct((d_ff, d), BF16), jax.ShapeDtypeStruct((3, d_ff), F32),
                   jax.ShapeDtypeStruct((1, d_ff), F32)],
        scratch_shapes=[pltpu.VMEM((tf, d), F32)],
        compiler_params=_params(("arbitrary", "arbitrary")),
    )(dy2, gate, gate, val, w_down, conv_w, conv_b)


def _ffn_bwd_up(dgc, dval, w_up_t, conv_w, tm):
    s_len, d_ff = dgc.shape
    d = w_up_t.shape[1]
    n_t = s_len // tm

    def body(dg_ref, dgn_ref, dv_ref, cw_ref, w_ref, dup_ref, dh_ref):
        i = pl.program_id(0)
        nxt = dgn_ref[...].astype(F32) * (i < n_t - 1).astype(F32)
        ext = jnp.concatenate([dg_ref[...].astype(F32), nxt], axis=0)
        rows = tm + HALO
        dgate = (ext * cw_ref[2:3, :] + pltpu.roll(ext, rows - 1, 0) * cw_ref[1:2, :]
                 + pltpu.roll(ext, rows - 2, 0) * cw_ref[0:1, :])[:tm]
        dup = jnp.concatenate([dgate.astype(BF16), dv_ref[...]], axis=1)
        dup_ref[...] = dup
        dh_ref[...] = _dot(dup, w_ref[...], NN)

    tokf = pl.BlockSpec((tm, d_ff), lambda i: (i, 0))
    return pl.pallas_call(
        body, name="ffn_bwd_up", grid=(n_t,),
        in_specs=[tokf, pl.BlockSpec((HALO, d_ff), lambda i: (jnp.minimum((i + 1) * (tm // HALO), s_len // HALO - 1), 0)),
                  tokf, pl.BlockSpec((3, d_ff), lambda i: (0, 0)), pl.BlockSpec((2 * d_ff, d), lambda i: (0, 0))],
        out_specs=[pl.BlockSpec((tm, 2 * d_ff), lambda i: (i, 0)), pl.BlockSpec((tm, d), lambda i: (i, 0))],
        out_shape=[jax.ShapeDtypeStruct((s_len, 2 * d_ff), BF16), jax.ShapeDtypeStruct((s_len, d), F32)],
        compiler_params=_params(("arbitrary",)),
    )(dgc, dgc, dval, conv_w, w_up_t)


def _mix_bwd(dh2, dout, x1, y1, cat, attn, w_out_t, sc_f, g_pre_ffn, gt_m, g_post_mix, tm):
    s_len, d = x1.shape
    n_t = s_len // tm

    def body(dh_ref, do_ref, x1_ref, y1_ref, cat_ref, at_ref, wo_ref, sc_ref, g2_ref, gt_ref, g1_ref,
             dx1_ref, dpool_ref, dattn_ref, delta_ref, dwo_ref, sums_ref, acc_ref):
        i = pl.program_id(0)
        dh = dh_ref[...]
        x1 = x1_ref[...]
        r2 = _rstd(x1)
        n2 = x1 * r2
        ng = n2 * g2_ref[...]
        dng = dh * (1.0 + sc_ref[...])
        dx1 = do_ref[...] + _norm_bwd(dng * g2_ref[...], n2, r2)
        dx1_ref[...] = dx1
        y1 = y1_ref[...]
        r1 = _rstd(y1)
        n1 = y1 * r1
        drn = dx1 * gt_ref[...]
        dy1 = _norm_bwd(drn * g1_ref[...], n1, r1).astype(BF16)
        dcat = _dot(dy1, wo_ref[...], NN)
        dpool_ref[...] = dcat[:, 0:256]
        lane = lax.broadcasted_iota(jnp.int32, (tm, LANES), 1)
        first = lane < HEAD_DIM
        for s in range(2):
            da = dcat[:, 256 + s * LANES:256 + (s + 1) * LANES]
            dattn_ref[s] = da
            prod = da * at_ref[:, s * LANES:(s + 1) * LANES]
            tot = jnp.sum(prod, axis=-1, keepdims=True)
            lo = jnp.sum(jnp.where(first, prod, 0.0), axis=-1, keepdims=True)
            delta_ref[s] = jnp.where(first, lo, tot - lo)
        dwo = _dot(dy1, cat_ref[...], TN)
        sums = jnp.concatenate(
            [jnp.sum(dh, axis=0, keepdims=True), jnp.sum(dh * ng, axis=0, keepdims=True),
             jnp.sum(dng * n2, axis=0, keepdims=True), jnp.sum(dx1 * (n1 * g1_ref[...]), axis=0, keepdims=True),
             jnp.sum(drn * n1, axis=0, keepdims=True), jnp.zeros((3, d), F32)], axis=0)

        @pl.when(i == 0)
        def _():
            acc_ref[...] = dwo
            sums_ref[...] = sums

        @pl.when(i > 0)
        def _():
            acc_ref[...] += dwo
            sums_ref[...] += sums

        @pl.when(i == n_t - 1)
        def _():
            dwo_ref[...] = acc_ref[...].astype(BF16)

    tile = lambda w: pl.BlockSpec((tm, w), lambda i: (i, 0))
    slab = pl.BlockSpec((2, tm, LANES), lambda i: (0, i, 0))
    vec = pl.BlockSpec((1, d), lambda i: (0, 0))
    return pl.pallas_call(
        body, name="mix_bwd", grid=(n_t,),
        in_specs=[tile(d), tile(d), tile(d), tile(d), tile(512), tile(256),
                  pl.BlockSpec((d, 512), lambda i: (0, 0)), vec, vec, vec, vec],
        out_specs=[tile(d), tile(256), slab, slab, pl.BlockSpec((d, 512), lambda i: (0, 0)),
                   pl.BlockSpec((8, d), lambda i: (0, 0))],
        out_shape=[jax.ShapeDtypeStruct((s_len, d), F32), jax.ShapeDtypeStruct((s_len, 256), F32),
                   jax.ShapeDtypeStruct((2, s_len, LANES), F32), jax.ShapeDtypeStruct((2, s_len, LANES), F32),
                   jax.ShapeDtypeStruct((d, 512), BF16), jax.ShapeDtypeStruct((8, d), F32)],
        scratch_shapes=[pltpu.VMEM((d, 512), F32)],
        compiler_params=_params(("arbitrary",)),
    )(dh2, dout, x1, y1, cat, attn, w_out_t, sc_f, g_pre_ffn, gt_m, g_post_mix)


def _pool_bwd(dpool, u_pool, w_blk, b_pool, pool_scale, tm):
    s_len = dpool.shape[0]
    n_t = s_len // tm

    def body(dp_ref, dpn_ref, u_ref, uh_ref, wb_ref, bp_ref, ps_ref, du_ref, dwb_ref, sums_ref):
        i = pl.program_id(0)
        u = u_ref[...]
        mixed, _ = _pool_mixed(u, uh_ref[...] * (i > 0).astype(F32), i, tm)
        mixed_b = mixed.astype(BF16)
        y = _dot(mixed_b, wb_ref[...], NN) + bp_ref[...]
        dp = dp_ref[...]
        dy = dp * ps_ref[...]
        dwb = _dot(mixed_b, dy.astype(BF16), TN)
        sums = jnp.concatenate([jnp.sum(dy, axis=0, keepdims=True), jnp.sum(dp * y, axis=0, keepdims=True),
                                jnp.zeros((6, 256), F32)], axis=0)
        dp_ext = jnp.concatenate([dp, dpn_ref[...] * (i < n_t - 1).astype(F32)], axis=0)
        dmix = _dot((dp_ext * ps_ref[...]).astype(BF16), wb_ref[...], NT)
        rows = tm + HALO
        grp = lax.broadcasted_iota(jnp.int32, (rows, 256), 1) // HEAD_DIM
        pick = lambda a, b, c, e: jnp.where(grp == 0, a, jnp.where(grp == 1, b, jnp.where(grp == 2, c, e)))
        pos = (i * tm + lax.broadcasted_iota(jnp.int32, (rows, 256), 0)).astype(F32)
        z = dmix / jnp.minimum(pos + 1.0, pick(*[float(w) for w in POOL_WINDOWS]))
        f2 = z + pltpu.roll(z, rows - 1, 0)
        f4 = f2 + pltpu.roll(f2, rows - 2, 0)
        f8 = f4 + pltpu.roll(f4, rows - 4, 0)
        f16 = f8 + pltpu.roll(f8, rows - 8, 0)
        du_ref[...] = (pick(f2, f4, f8, f16) - dmix)[:tm]

        @pl.when(i == 0)
        def _():
            dwb_ref[...] = dwb
            sums_ref[...] = sums

        @pl.when(i > 0)
        def _():
            dwb_ref[...] += dwb
            sums_ref[...] += sums

    tile = pl.BlockSpec((tm, 256), lambda i: (i, 0))
    const = lambda a: pl.BlockSpec(a.shape, lambda i: (0,) * a.ndim)
    return pl.pallas_call(
        body, name="pool_bwd", grid=(n_t,),
        in_specs=[tile, pl.BlockSpec((HALO, 256), lambda i: (jnp.minimum((i + 1) * (tm // HALO), s_len // HALO - 1), 0)),
                  tile, pl.BlockSpec((HALO, 256), lambda i: (_halo_before(i, tm), 0)),
                  const(w_blk), const(b_pool), const(pool_scale)],
        out_specs=[tile, pl.BlockSpec((256, 256), lambda i: (0, 0)), pl.BlockSpec((8, 256), lambda i: (0, 0))],
        out_shape=[jax.ShapeDtypeStruct((s_len, 256), F32), jax.ShapeDtypeStruct((256, 256), F32),
                   jax.ShapeDtypeStruct((8, 256), F32)],
        compiler_params=_params(("arbitrary",)),
    )(dpool, dpool, u_pool, u_pool, w_blk, b_pool, pool_scale)


def _attn_bwd(qkv, dattn, lse_all, delta, group, dil):
    s_len = qkv.shape[1]
    nb = s_len // (BLOCK * dil)

    def body(q_ref, k_ref, v_ref, do_ref, l_ref, dl_ref, dq_ref, dk_ref, dv_ref):
        lane = lax.broadcasted_iota(jnp.int32, (BLOCK, LANES), 1)
        first = lane < HEAD_DIM
        dk_ref[...] = jnp.zeros_like(dk_ref)
        dv_ref[...] = jnp.zeros_like(dv_ref)

        def block(t, carry):
            r, n = t // nb, t % nb
            cur = _block_rows(n, r, dil)
            prev = _block_rows(jnp.maximum(n - 1, 0), r, dil)
            q = q_ref[0, cur, :]
            do = do_ref[0, cur, :]
            lse = l_ref[0, cur, :]
            dlt = dl_ref[0, cur, :]
            kcat = jnp.concatenate([k_ref[0, prev, :], k_ref[0, cur, :]], axis=0).astype(BF16)
            vcat = jnp.concatenate([v_ref[0, prev, :], v_ref[0, cur, :]], axis=0).astype(BF16)
            valid = _band_mask(n)
            dq, dkc, dvc = [], None, None
            for h, keep in enumerate((first, ~first)):
                col = slice(h * HEAD_DIM, h * HEAD_DIM + 1)
                qh = jnp.where(keep, q, 0.0).astype(BF16)
                doh = jnp.where(keep, do, 0.0).astype(BF16)
                s = _dot(qh, kcat, NT)
                p = jnp.where(valid, jnp.exp(s - lse[:, col]), 0.0)
                dp = _dot(doh, vcat, NT)
                ds = (p * (dp - dlt[:, col])).astype(BF16)
                dq.append(_dot(ds, kcat, NN))
                dk_h = _dot(ds, qh, TN)
                dv_h = _dot(p.astype(BF16), doh, TN)
                dkc = dk_h if dkc is None else dkc + dk_h
                dvc = dv_h if dvc is None else dvc + dv_h
            dq_ref[0, cur, :] = jnp.where(first, dq[0], dq[1])
            dk_ref[0, prev, :] += dkc[:BLOCK]
            dv_ref[0, prev, :] += dvc[:BLOCK]
            dk_ref[0, cur, :] += dkc[BLOCK:]
            dv_ref[0, cur, :] += dvc[BLOCK:]
            return carry

        lax.fori_loop(0, nb * dil, block, 0)

    def slab(base):
        return pl.BlockSpec((1, s_len, LANES), lambda s: (base + 2 * group + s, 0, 0))

    one = pl.BlockSpec((1, s_len, LANES), lambda s: (s, 0, 0))
    shape = jax.ShapeDtypeStruct((2, s_len, LANES), F32)
    return pl.pallas_call(
        body, name=f"attn_bwd_d{dil}", grid=(2,),
        in_specs=[slab(0), slab(6), slab(12), one, one, one],
        out_specs=[one, one, one], out_shape=[shape, shape, shape],
        compiler_params=_params(("arbitrary",)),
    )(qkv, qkv, qkv, dattn, lse_all, delta)


def _inproj_bwd(du, dqkv, rope, w_in_t, x, dx1, sc_m, g_pre_mix, tm):
    s_len, d = x.shape
    n_proj = w_in_t.shape[0]
    n_t = s_len // tm

    def body(du_ref, *refs):
        dref = refs[:9]
        rope_ref, w_ref, x_ref, dx1_ref, sc_ref, g_ref, dproj_ref, dx_ref, sums_ref = refs[9:]
        i = pl.program_id(0)
        cols = [du_ref[...].astype(BF16)]
        for kind in range(3):
            for grp in range(3):
                for s in range(2):
                    piece = dref[3 * grp + kind][s]
                    if kind < 2:
                        piece = _rope_bwd(piece, rope_ref)
                    if kind == 0:
                        piece = piece * (HEAD_DIM ** -0.5)
                    cols.append(piece.astype(BF16))
        dproj = jnp.concatenate(cols, axis=1)
        dproj_ref[...] = dproj
        dh = _dot(dproj, w_ref[...], NN)
        xv = x_ref[...]
        r = _rstd(xv)
        n = xv * r
        dng = dh * (1.0 + sc_ref[...])
        dx_ref[...] = dx1_ref[...] + _norm_bwd(dng * g_ref[...], n, r)
        sums = jnp.concatenate([jnp.sum(dh, axis=0, keepdims=True), jnp.sum(dh * (n * g_ref[...]), axis=0, keepdims=True),
                                jnp.sum(dng * n, axis=0, keepdims=True), jnp.zeros((5, d), F32)], axis=0)

        @pl.when(i == 0)
        def _():
            sums_ref[...] = sums

        @pl.when(i > 0)
        def _():
            sums_ref[...] += sums

    tile = lambda w: pl.BlockSpec((tm, w), lambda i: (i, 0))
    slab = pl.BlockSpec((2, tm, LANES), lambda i: (0, i, 0))
    vec = pl.BlockSpec((1, d), lambda i: (0, 0))
    return pl.pallas_call(
        body, name="inproj_bwd", grid=(n_t,),
        in_specs=[tile(256)] + [slab] * 9 + [pl.BlockSpec((3, tm, LANES), lambda i: (0, i, 0)),
                                             pl.BlockSpec((n_proj, d), lambda i: (0, 0)), tile(d), tile(d), vec, vec],
        out_specs=[tile(n_proj), tile(d), pl.BlockSpec((8, d), lambda i: (0, 0))],
        out_shape=[jax.ShapeDtypeStruct((s_len, n_proj), BF16), jax.ShapeDtypeStruct((s_len, d), F32),
                   jax.ShapeDtypeStruct((8, d), F32)],
        compiler_params=_params(("arbitrary",)),
    )(du, *dqkv, rope, w_in_t, x, dx1, sc_m, g_pre_mix)


def _wgrad(a, b, name, tk, tmm):
    s_len, m = a.shape
    n = b.shape[1]
    n_k = s_len // tk

    def body(a_ref, b_ref, o_ref, acc_ref):
        k = pl.program_id(1)
        part = _dot(a_ref[...], b_ref[...], TN)

        @pl.when(k == 0)
        def _():
            acc_ref[...] = part

        @pl.when(k > 0)
        def _():
            acc_ref[...] += part

        @pl.when(k == n_k - 1)
        def _():
            o_ref[...] = acc_ref[...].astype(BF16)

    return pl.pallas_call(
        body, name=name, grid=(m // tmm, n_k),
        in_specs=[pl.BlockSpec((tk, tmm), lambda j, k: (k, j)), pl.BlockSpec((tk, n), lambda j, k: (k, 0))],
        out_specs=pl.BlockSpec((tmm, n), lambda j, k: (j, 0)),
        out_shape=jax.ShapeDtypeStruct((m, n), BF16),
        scratch_shapes=[pltpu.VMEM((tmm, n), F32)],
        compiler_params=_params(("arbitrary", "arbitrary")),
    )(a, b)


def _place():
    return lax.axis_index("x"), lax.axis_index("y"), lax.axis_index("c")


def _peer(k):
    x, y, c = _place()
    bx, by, bc = (k >> 2) & 1, (k >> 1) & 1, k & 1
    return (x ^ bx if bx else x, y ^ by if by else y, c ^ bc if bc else c)


def _index(pos):
    return 4 * pos[0] + 2 * pos[1] + pos[2]


def _ada_exchange(c_rows, w_ada, b_ada_cols):
    d = c_rows.shape[1]
    ncol = w_ada.shape[1]

    def body(c_ref, w_ref, b_ref, call_ref, mod_ref, stage_ref, send_sems, recv_sems):
        me = _index(_place())
        call_ref[me] = c_ref[...]

        def gather(k):
            return pltpu.make_async_remote_copy(
                src_ref=c_ref, dst_ref=call_ref.at[me], send_sem=send_sems.at[0, k - 1], recv_sem=recv_sems.at[0, k - 1],
                device_id=_peer(k), device_id_type=MESH)

        for k in range(1, N_DEV):
            gather(k).start()
        for k in range(1, N_DEV):
            gather(k).wait_recv()
        cv = jnp.concatenate([call_ref[b, 0:1, :] for b in range(N_DEV)], axis=0)
        act = cv * jax.nn.sigmoid(cv)
        mod = lax.dot_general(act, w_ref[...], NN, preferred_element_type=F32,
                              precision=lax.Precision.HIGHEST) + b_ref[...]
        for b in range(N_DEV):
            stage_ref[b] = jnp.broadcast_to(mod[b:b + 1, :], (8, ncol))
        mod_ref[me] = stage_ref[me]

        def scatter(k):
            return pltpu.make_async_remote_copy(
                src_ref=stage_ref.at[_index(_peer(k))], dst_ref=mod_ref.at[me],
                send_sem=send_sems.at[1, k - 1], recv_sem=recv_sems.at[1, k - 1],
                device_id=_peer(k), device_id_type=MESH)

        for k in range(1, N_DEV):
            scatter(k).start()
        for k in range(1, N_DEV):
            scatter(k).wait_recv()
        for k in range(1, N_DEV):
            gather(k).wait_send()
            scatter(k).wait_send()

    vmem = pl.BlockSpec(memory_space=pltpu.VMEM)
    return pl.pallas_call(
        body, name="ada_exchange",
        in_specs=[vmem, vmem, vmem], out_specs=[vmem, vmem],
        out_shape=[jax.ShapeDtypeStruct((N_DEV, 8, d), F32), jax.ShapeDtypeStruct((N_DEV, 8, ncol), F32)],
        scratch_shapes=[pltpu.VMEM((N_DEV, 8, ncol), F32), pltpu.SemaphoreType.DMA((2, N_DEV - 1)),
                        pltpu.SemaphoreType.DMA((2, N_DEV - 1))],
        compiler_params=_params(),
    )(c_rows, w_ada, b_ada_cols)


def _gather_weights(shards):
    n_w = len(shards)

    def body(*refs):
        srcs, outs = refs[:n_w], refs[n_w:2 * n_w]
        send_sems, recv_sems, local_sems = refs[2 * n_w:]
        x, y, c = _place()
        me, sibling = (x, y, c), (x, y, 1 - c)
        chips = [(1 - x, y), (x, 1 - y), (1 - x, 1 - y)]

        def rows(w, pos):
            r = shards[w].shape[0]
            return outs[w].at[pl.ds(pl.multiple_of(_index(pos) * r, 16), r), :]

        def copy(k, w, block, to, own=False):
            return pltpu.make_async_remote_copy(
                src_ref=srcs[w] if own else rows(w, block), dst_ref=rows(w, block),
                send_sem=send_sems.at[k, w], recv_sem=recv_sems.at[k, w], device_id=to, device_id_type=MESH)

        mine = [pltpu.make_async_copy(srcs[w], rows(w, me), local_sems.at[w]) for w in range(n_w)]
        for cp in mine:
            cp.start()
        first = [copy(0, w, me, sibling, own=True) for w in range(n_w)]
        first += [copy(1 + j, w, me, (*chip, c), own=True) for j, chip in enumerate(chips) for w in range(n_w)]
        for cp in first:
            cp.start()
        passed = []
        for j, chip in enumerate(chips):
            for w in range(n_w):
                copy(1 + j, w, (*chip, c), me).wait_recv()
                fwd = copy(4 + j, w, (*chip, c), sibling)
                fwd.start()
                passed.append(fwd)
        for w in range(n_w):
            copy(0, w, sibling, me).wait_recv()
        for j, chip in enumerate(chips):
            for w in range(n_w):
                copy(4 + j, w, (*chip, 1 - c), me).wait_recv()
        for cp in first + passed:
            cp.wait_send()
        for cp in mine:
            cp.wait()

    hbm = pl.BlockSpec(memory_space=pltpu.HBM)
    return pl.pallas_call(
        body, name="gather_weights",
        in_specs=[hbm] * n_w, out_specs=[hbm] * n_w,
        out_shape=[jax.ShapeDtypeStruct((N_DEV * s.shape[0], s.shape[1]), s.dtype) for s in shards],
        scratch_shapes=[pltpu.SemaphoreType.DMA((N_DEV - 1, n_w)), pltpu.SemaphoreType.DMA((N_DEV - 1, n_w)),
                        pltpu.SemaphoreType.DMA((n_w,))],
        compiler_params=_params(),
    )(*shards)


def _scatter_grads(grads):
    n_w = len(grads)

    def body(*refs):
        srcs, outs = refs[:n_w], refs[n_w:2 * n_w]
        send_sems, recv_sems, local_sems = refs[2 * n_w:]
        me = _index(_place())

        def slab(w, dev):
            r = grads[w].shape[0] // N_DEV
            return srcs[w].at[pl.ds(pl.multiple_of(dev * r, 16), r), :]

        def copy(k, w):
            return pltpu.make_async_remote_copy(
                src_ref=slab(w, _index(_peer(k))), dst_ref=outs[w].at[me],
                send_sem=send_sems.at[k - 1, w], recv_sem=recv_sems.at[k - 1, w],
                device_id=_peer(k), device_id_type=MESH)

        mine = [pltpu.make_async_copy(slab(w, me), outs[w].at[me], local_sems.at[w]) for w in range(n_w)]
        for cp in mine:
            cp.start()
        sends = [copy(k, w) for k in range(1, N_DEV) for w in range(n_w)]
        for cp in sends:
            cp.start()
        for cp in sends:
            cp.wait_recv()
        for cp in sends:
            cp.wait_send()
        for cp in mine:
            cp.wait()

    hbm = pl.BlockSpec(memory_space=pltpu.HBM)
    return pl.pallas_call(
        body, name="scatter_grads",
        in_specs=[hbm] * n_w, out_specs=[hbm] * n_w,
        out_shape=[jax.ShapeDtypeStruct((N_DEV, g.shape[0] // N_DEV, g.shape[1]), g.dtype) for g in grads],
        scratch_shapes=[pltpu.SemaphoreType.DMA((N_DEV - 1, n_w)), pltpu.SemaphoreType.DMA((N_DEV - 1, n_w)),
                        pltpu.SemaphoreType.DMA((n_w,))],
        compiler_params=_params(),
    )(*grads)


def _allreduce_small(packed, name):
    rows = packed.shape[0]

    def body(p_ref, all_ref, tot_ref, send_sems, recv_sems):
        me = _index(_place())
        all_ref[me] = p_ref[...]

        def copy(k):
            return pltpu.make_async_remote_copy(
                src_ref=p_ref, dst_ref=all_ref.at[me], send_sem=send_sems.at[k - 1], recv_sem=recv_sems.at[k - 1],
                device_id=_peer(k), device_id_type=MESH)

        for k in range(1, N_DEV):
            copy(k).start()
        for k in range(1, N_DEV):
            copy(k).wait_recv()
        tot = all_ref[0]
        for dev in range(1, N_DEV):
            tot = tot + all_ref[dev]
        tot_ref[...] = tot
        for k in range(1, N_DEV):
            copy(k).wait_send()

    vmem = pl.BlockSpec(memory_space=pltpu.VMEM)
    return pl.pallas_call(
        body, name=name, in_specs=[vmem], out_specs=[vmem, vmem],
        out_shape=[jax.ShapeDtypeStruct((N_DEV, rows, LANES), F32), jax.ShapeDtypeStruct((rows, LANES), F32)],
        scratch_shapes=[pltpu.SemaphoreType.DMA((N_DEV - 1,)), pltpu.SemaphoreType.DMA((N_DEV - 1,))],
        compiler_params=_params(),
    )(packed)


def _sum_slabs(parts, name, tr):
    _, rows, cols = parts.shape

    def body(p_ref, o_ref):
        tot = p_ref[0].astype(F32)
        for dev in range(1, N_DEV):
            tot = tot + p_ref[dev].astype(F32)
        o_ref[...] = tot

    return pl.pallas_call(
        body, name=name, grid=(rows // tr,),
        in_specs=[pl.BlockSpec((N_DEV, tr, cols), lambda i: (0, i, 0))],
        out_specs=pl.BlockSpec((tr, cols), lambda i: (i, 0)),
        out_shape=jax.ShapeDtypeStruct((rows, cols), F32),
        compiler_params=_params(("arbitrary",)),
    )(parts)


def _adam_math(w, g, m, v):
    m = ADAM_B1 * m + (1.0 - ADAM_B1) * g
    v = ADAM_B2 * v + (1.0 - ADAM_B2) * (g * g)
    m_hat = m / (1.0 - ADAM_B1 ** ADAM_STEP)
    v_hat = v / (1.0 - ADAM_B2 ** ADAM_STEP)
    delta = -ADAM_LR * (m_hat / (jnp.sqrt(v_hat) + ADAM_EPS) + ADAM_WD * w)
    return delta, m, v


def _adam(w, g, m, v, name, tr):
    rows, cols = w.shape

    def body(w_ref, g_ref, m_ref, v_ref, d_ref, nm_ref, nv_ref):
        d_ref[...], nm_ref[...], nv_ref[...] = _adam_math(w_ref[...], g_ref[...], m_ref[...], v_ref[...])

    spec = pl.BlockSpec((tr, cols), lambda i: (i, 0))
    shape = jax.ShapeDtypeStruct((rows, cols), F32)
    return pl.pallas_call(
        body, name=name, grid=(rows // tr,), in_specs=[spec] * 4, out_specs=[spec] * 3,
        out_shape=[shape] * 3, compiler_params=_params(("arbitrary",)),
    )(w, g, m, v)


def _ada_grad_adam(c_all, dmod_cols, w, m, v, tr):
    rows, cols = w.shape

    def body(c_ref, dm_ref, w_ref, m_ref, v_ref, g_ref, d_ref, nm_ref, nv_ref):
        cv = c_ref[...]
        act = cv * jax.nn.sigmoid(cv)
        g = lax.dot_general(act, dm_ref[...], TN, preferred_element_type=F32, precision=lax.Precision.HIGHEST)
        g_ref[...] = g
        d_ref[...], nm_ref[...], nv_ref[...] = _adam_math(w_ref[...], g, m_ref[...], v_ref[...])

    spec = pl.BlockSpec((tr, cols), lambda i: (i, 0))
    shape = jax.ShapeDtypeStruct((rows, cols), F32)
    return pl.pallas_call(
        body, name="ada_grad_adam", grid=(rows // tr,),
        in_specs=[pl.BlockSpec((N_DEV, tr), lambda i: (0, i)), pl.BlockSpec((N_DEV, cols), lambda i: (0, 0)), spec, spec, spec],
        out_specs=[spec] * 4, out_shape=[shape] * 4, compiler_params=_params(("arbitrary",)),
    )(c_all, dmod_cols, w, m, v)


def _rope_tables(positions):
    s_len = positions.shape[0]
    inv_freq = ROPE_THETA ** (-jnp.arange(0, 2 * ROT_HALF, 2, dtype=F32) / (2 * ROT_HALF))
    ang = positions.astype(F32)[:, None] * inv_freq
    cos, sin = jnp.cos(ang), jnp.sin(ang)
    rest = HEAD_DIM - 2 * ROT_HALF
    zero = lambda n: jnp.zeros((s_len, n), F32)
    head = jnp.stack([jnp.concatenate([cos, cos, jnp.ones((s_len, rest), F32)], axis=1),
                      jnp.concatenate([-sin, zero(HEAD_DIM - ROT_HALF)], axis=1),
                      jnp.concatenate([zero(ROT_HALF), sin, zero(rest)], axis=1)])
    return jnp.tile(head, (1, 1, LANES // HEAD_DIM))


def _pad_rows(a, rows):
    return jnp.pad(a, ((0, rows - a.shape[0]), (0, 0)))


def _as_rows(a, rows):
    flat = a.reshape(-1)
    return jnp.pad(flat, (0, rows * LANES - flat.shape[0])).reshape(rows, LANES)


def _sequence_step(xs, target, rope, mods, gains, w_in_t, w_out_t, w_up_t, w_down_f, w_blk_b, b_pool_r, pool_scale_r,
                   conv_w_all, conv_b):
    sh_m, sc_m, gt_m, sh_f, sc_f, gt_f = mods
    g_pre_mix, g_post_mix, g_pre_ffn, g_post_ffn = gains
    h1, u_pool, qkv = _premix_inproj(xs, sh_m, sc_m, g_pre_mix, w_in_t, rope, tm=512)
    o_g, lse_g = [], []
    for gi, dil in enumerate(DILATIONS):
        o, lse = _attn_fwd(qkv, gi, dil)
        o_g.append(o)
        lse_g.append(lse)
    x1, y1, h2, cat, attn, lse_all = _mix_out(xs, u_pool, o_g, lse_g, w_blk_b, b_pool_r, pool_scale_r, w_out_t,
                                              gt_m, g_post_mix, g_pre_ffn, sc_f, sh_f, tm=256)
    gate, val, dy2, dout, sums_ffn, loss_loc = _ffn_fwd_loss(h2, x1, target, w_up_t, w_down_f, conv_w_all, conv_b,
                                                              gt_f, g_post_ffn, tm=256, tf=1408)

    dgc, dval, dw_down, dconv_w, dconv_b = _ffn_bwd_act(dy2, gate, val, w_down_f, conv_w_all, conv_b, tm=1024, tf=256)
    dup, dh2 = _ffn_bwd_up(dgc, dval, w_up_t, conv_w_all, tm=256)
    dw_up_t = _wgrad(dup, h2, "wgrad_up", tk=1024, tmm=1408)
    dx1, dpool, dattn, delta, dw_out_t, sums_mix = _mix_bwd(dh2, dout, x1, y1, cat, attn, w_out_t, sc_f, g_pre_ffn,
                                                           gt_m, g_post_mix, tm=256)
    du, dw_blk, sums_pool = _pool_bwd(dpool, u_pool, w_blk_b, b_pool_r, pool_scale_r, tm=512)
    dqkv = []
    for gi, dil in enumerate(DILATIONS):
        dqkv += list(_attn_bwd(qkv, dattn, lse_all, delta, gi, dil))
    dproj, grad_x, sums_in = _inproj_bwd(du, dqkv, rope, w_in_t, xs, dx1, sc_m, g_pre_mix, tm=256)
    dw_in_t = _wgrad(dproj, h1, "wgrad_in", tk=1024, tmm=1280)
    return (loss_loc, grad_x, dw_in_t, dw_out_t, dw_up_t, dw_down, dw_blk, dconv_w, dconv_b,
            sums_in, sums_mix, sums_ffn, sums_pool)


def kernel(x, c, positions, w_ada, b_ada, g_pre_mix, g_post_mix, g_pre_ffn, g_post_ffn, w_in, w_pool, b_pool, pool_scale, w_out, w_up, conv_w, conv_b, w_down, loss_target, m_w_ada, m_b_ada, m_g_pre_mix, m_g_post_mix, m_g_pre_ffn, m_g_post_ffn, m_w_in, m_w_pool, m_b_pool, m_pool_scale, m_w_out, m_w_up, m_conv_w, m_conv_b, m_w_down, v_w_ada, v_b_ada, v_g_pre_mix, v_g_post_mix, v_g_pre_ffn, v_g_post_ffn, v_w_in, v_w_pool, v_b_pool, v_pool_scale, v_w_out, v_w_up, v_conv_w, v_conv_b, v_w_down):
    s_len, d = x.shape[1], x.shape[2]
    d_ff = w_down.shape[1] * N_DEV
    me = _index(_place())
    xs, target = x[0], loss_target[0]

    ncol = w_ada.shape[2]
    b_cols = lax.dynamic_slice(b_ada, (0, me * ncol), (1, ncol))
    c_all, mod = _ada_exchange(jnp.broadcast_to(c, (8, d)), w_ada[0], b_cols)
    c_all = c_all[:, 0, :]
    sh_m, sc_m, gt_m, sh_f, sc_f, gt_f = [mod[:, 0, :].reshape(1, -1)[:, k * d:(k + 1) * d] for k in range(6)]

    w_in_t, w_out_t, w_up_t, w_down_f = _gather_weights(
        [w_in[0].T.astype(BF16), w_out[0].T.astype(BF16), w_up[0].T.astype(BF16), w_down[0].astype(BF16)])

    rope = _rope_tables(positions[0])
    w_blk = jnp.zeros((256, 256), F32)
    for gi in range(4):
        w_blk = lax.dynamic_update_slice(w_blk, w_pool[0, gi], (gi * HEAD_DIM, gi * HEAD_DIM))
    w_blk_b = w_blk.astype(BF16)
    b_pool_r, pool_scale_r = b_pool.reshape(1, 256), pool_scale.reshape(1, 256)

    cw_rows = d_ff // LANES
    cw_pack = jnp.concatenate([_pad_rows(lax.dynamic_update_slice(jnp.zeros((1, d_ff), F32), conv_w[0, k:k + 1], (0, me * (d_ff // N_DEV))).reshape(cw_rows, LANES), 24) for k in range(3)], axis=0)
    _, cw_tot = _allreduce_small(cw_pack, "gather_conv_w")
    conv_w_all = jnp.concatenate([cw_tot[24 * k:24 * k + cw_rows].reshape(1, d_ff) for k in range(3)], axis=0)

    (loss_loc, grad_x, dw_in_t, dw_out_t, dw_up_t, dw_down, dw_blk, dconv_w, dconv_b,
     sums_in, sums_mix, sums_ffn, sums_pool) = _sequence_step(
        xs, target, rope, (sh_m, sc_m, gt_m, sh_f, sc_f, gt_f), (g_pre_mix, g_post_mix, g_pre_ffn, g_post_ffn),
        w_in_t, w_out_t, w_up_t, w_down_f, w_blk_b, b_pool_r, pool_scale_r, conv_w_all, conv_b)

    parts = _scatter_grads([dw_in_t, dw_out_t, dw_up_t, dw_down])
    g_w_in = _sum_slabs(parts[0], "sum_w_in", 64).T
    g_w_out = _sum_slabs(parts[1], "sum_w_out", 128).T
    g_w_up = _sum_slabs(parts[2], "sum_w_up", 64).T
    g_w_down = _sum_slabs(parts[3], "sum_w_down", 32)

    dmod = jnp.concatenate([sums_in[0:1], sums_in[1:2], sums_mix[3:4], sums_mix[0:1], sums_mix[1:2], sums_ffn[0:1]], axis=1)
    dw_pool = jnp.stack([dw_blk[gi * HEAD_DIM:(gi + 1) * HEAD_DIM, gi * HEAD_DIM:(gi + 1) * HEAD_DIM] for gi in range(4)])
    pieces = [(dmod, 48), (sums_in[2:3], 8), (sums_mix[4:5], 8), (sums_mix[2:3], 8), (sums_ffn[1:2], 8),
              (dw_pool, 128), (sums_pool[0:1], 8), (sums_pool[1:2], 8), (dconv_b, 24),
              (dconv_w[0:1], 24), (dconv_w[1:2], 24), (dconv_w[2:3], 24)]
    packed = jnp.concatenate([_as_rows(a, r) for a, r in pieces], axis=0)
    gathered, total = _allreduce_small(packed, "allreduce_small")
    n_rep = 248
    rep_w = [b_ada, g_pre_mix, g_post_mix, g_pre_ffn, g_post_ffn, w_pool, b_pool, pool_scale, conv_b]
    rep_m = [m_b_ada, m_g_pre_mix, m_g_post_mix, m_g_pre_ffn, m_g_post_ffn, m_w_pool, m_b_pool, m_pool_scale, m_conv_b]
    rep_v = [v_b_ada, v_g_pre_mix, v_g_post_mix, v_g_pre_ffn, v_g_post_ffn, v_w_pool, v_b_pool, v_pool_scale, v_conv_b]
    rep_rows = [r for _, r in pieces[:9]]
    pack_rep = lambda arrs: jnp.concatenate([_as_rows(a, r) for a, r in zip(arrs, rep_rows)], axis=0)
    rep_g = total[:n_rep]
    rep_d, rep_nm, rep_nv = _adam(pack_rep(rep_w), rep_g, pack_rep(rep_m), pack_rep(rep_v), "adam_small", n_rep)

    def unpack(p):
        out, row = [], 0
        for a, r in zip(rep_w, rep_rows):
            out.append(p[row:row + r].reshape(-1)[:a.size].reshape(a.shape))
            row += r
        return out

    g_rep, d_rep, nm_rep, nv_rep = unpack(rep_g), unpack(rep_d), unpack(rep_nm), unpack(rep_nv)

    fcol = d_ff // N_DEV
    g_cw_full = jnp.concatenate([total[n_rep + 24 * k:n_rep + 24 * k + cw_rows].reshape(1, d_ff) for k in range(3)], axis=0)
    g_cw = lax.dynamic_slice(g_cw_full, (0, me * fcol), (3, fcol))
    d_cw, nm_cw, nv_cw = _adam(conv_w[0], g_cw, m_conv_w[0], v_conv_w[0], "adam_conv_w", 3)

    dmod_all = gathered[:, :48].reshape(N_DEV, 6 * d)
    dmod_cols = lax.dynamic_slice(dmod_all, (0, me * ncol), (N_DEV, ncol))
    g_ada, d_ada, nm_ada, nv_ada = _ada_grad_adam(c_all, dmod_cols, w_ada[0], m_w_ada[0], v_w_ada[0], 256)

    big = {}
    for nm, w, g, m, v, tr in (("w_in", w_in, g_w_in, m_w_in, v_w_in, 256), ("w_out", w_out, g_w_out, m_w_out, v_w_out, 256),
                               ("w_up", w_up, g_w_up, m_w_up, v_w_up, 256), ("w_down", w_down, g_w_down, m_w_down, v_w_down, 88)):
        big[nm] = (g,) + tuple(_adam(w[0], g, m[0], v[0], "adam_" + nm, tr))

    loss = lax.psum(loss_loc[0, 0], ("x", "y", "c"))

    def group(k):
        rep = (g_rep, d_rep, nm_rep, nv_rep)[k]
        ada = (g_ada, d_ada, nm_ada, nv_ada)[k][None]
        cw = (g_cw, d_cw, nm_cw, nv_cw)[k][None]
        return [ada, rep[0], rep[1], rep[2], rep[3], rep[4], big["w_in"][k][None], rep[5], rep[6], rep[7],
                big["w_out"][k][None], big["w_up"][k][None], cw, rep[8], big["w_down"][k][None]]

    return (loss, grad_x[None], *group(0), *group(1), *group(2), *group(3))
```

```python
import functools
import math

import jax
import jax.numpy as jnp
from jax import lax
from jax.experimental import pallas as pl
from jax.experimental.pallas import tpu as pltpu

F32 = jnp.float32
BF16 = jnp.bfloat16
MESH = pl.DeviceIdType.MESH

N_DEV = 8
HEAD_DIM = 64
ROT_HALF = 8
ROPE_THETA = 500000.0
POOL_WINDOWS = (2, 4, 8, 16)
DILATIONS = (1, 4, 16)
BLOCK = 128
NORM_EPS = 1e-6
HALO = 16
MASKED = -1e30

ADAM_LR = 0.001
ADAM_B1 = 0.9
ADAM_B2 = 0.999
ADAM_EPS = 1e-08
ADAM_WD = 0.01
ADAM_STEP = 10

V7X_VMEM_LIMIT = 56 * 1024 * 1024
LANES = 128

NT = (((1,), (1,)), ((), ()))
NN = (((1,), (0,)), ((), ()))
TN = (((0,), (0,)), ((), ()))


def _dot(a, b, dims):
    return lax.dot_general(a, b, dims, preferred_element_type=F32)


def _params(sem=None, vmem=V7X_VMEM_LIMIT):
    if sem is None:
        return pltpu.CompilerParams(vmem_limit_bytes=vmem)
    return pltpu.CompilerParams(dimension_semantics=sem, vmem_limit_bytes=vmem)


def _rstd(v):
    return lax.rsqrt(jnp.mean(v * v, axis=-1, keepdims=True) + NORM_EPS)


def _norm_bwd(dn, n, rstd):
    return rstd * (dn - n * jnp.mean(dn * n, axis=-1, keepdims=True))


def _rope_fwd(p, rope_ref):
    return p * rope_ref[0] + pltpu.roll(p, LANES - ROT_HALF, 1) * rope_ref[1] + pltpu.roll(p, ROT_HALF, 1) * rope_ref[2]


def _rope_bwd(dp, rope_ref):
    return dp * rope_ref[0] + pltpu.roll(dp * rope_ref[1], ROT_HALF, 1) + pltpu.roll(dp * rope_ref[2], LANES - ROT_HALF, 1)


def _gelu_parts(v):
    k = math.sqrt(2.0 / math.pi)
    t = jnp.tanh(k * (v + 0.044715 * v * v * v))
    g = 0.5 * v * (1.0 + t)
    dg = 0.5 * (1.0 + t) + 0.5 * v * (1.0 - t * t) * k * (1.0 + 3.0 * 0.044715 * v * v)
    return g, dg


def _halo_before(i, tile):
    return jnp.maximum(i * (tile // HALO) - 1, 0)


def _premix_inproj(x, sh, sc, g, w_in_t, rope, tm):
    s_len, d = x.shape
    n_proj = w_in_t.shape[0]
    n_slab = (n_proj - 256) // LANES

    def body(x_ref, sh_ref, sc_ref, g_ref, w_ref, rope_ref, h_ref, up_ref, qkv_ref):
        xv = x_ref[...]
        h = (xv * _rstd(xv) * g_ref[...]) * (1.0 + sc_ref[...]) + sh_ref[...]
        hb = h.astype(BF16)
        h_ref[...] = hb
        up_ref[...] = _dot(hb, w_ref[0:256, :], NT)
        for pair in range(n_slab // 2):
            p = _dot(hb, w_ref[256 + 256 * pair:512 + 256 * pair, :], NT)
            for half in range(2):
                ph = p[:, half * LANES:(half + 1) * LANES]
                if pair < 6:
                    ph = _rope_fwd(ph, rope_ref)
                if pair < 3:
                    ph = ph * (HEAD_DIM ** -0.5)
                qkv_ref[2 * pair + half] = ph

    vec = pl.BlockSpec((1, d), lambda i: (0, 0))
    return pl.pallas_call(
        body, name="premix_inproj", grid=(s_len // tm,),
        in_specs=[pl.BlockSpec((tm, d), lambda i: (i, 0)), vec, vec, vec,
                  pl.BlockSpec((n_proj, d), lambda i: (0, 0)),
                  pl.BlockSpec((3, tm, LANES), lambda i: (0, i, 0))],
        out_specs=[pl.BlockSpec((tm, d), lambda i: (i, 0)),
                   pl.BlockSpec((tm, 256), lambda i: (i, 0)),
                   pl.BlockSpec((n_slab, tm, LANES), lambda i: (0, i, 0))],
        out_shape=[jax.ShapeDtypeStruct((s_len, d), BF16),
                   jax.ShapeDtypeStruct((s_len, 256), F32),
                   jax.ShapeDtypeStruct((n_slab, s_len, LANES), F32)],
        compiler_params=_params(("arbitrary",)),
    )(x, sh, sc, g, w_in_t, rope)


def _block_rows(n, r, dil):
    start = n * (BLOCK * dil) + r
    if dil == 1:
        return pl.ds(pl.multiple_of(start, BLOCK), BLOCK)
    return pl.ds(start, BLOCK, stride=dil)


def _band_mask(n):
    ri = lax.broadcasted_iota(jnp.int32, (BLOCK, 2 * BLOCK), 0)
    cj = lax.broadcasted_iota(jnp.int32, (BLOCK, 2 * BLOCK), 1)
    cur = (cj >= BLOCK) & (cj - BLOCK <= ri)
    prev = (cj < BLOCK) & (cj >= ri) & (n > 0)
    return cur | prev


def _attn_fwd(qkv, group, dil):
    s_len = qkv.shape[1]
    nb = s_len // (BLOCK * dil)

    def body(q_ref, k_ref, v_ref, o_ref, lse_ref):
        lane = lax.broadcasted_iota(jnp.int32, (BLOCK, LANES), 1)
        first = lane < HEAD_DIM

        def block(t, carry):
            r, n = t // nb, t % nb
            cur = _block_rows(n, r, dil)
            prev = _block_rows(jnp.maximum(n - 1, 0), r, dil)
            q = q_ref[0, cur, :]
            kcat = jnp.concatenate([k_ref[0, prev, :], k_ref[0, cur, :]], axis=0).astype(BF16)
            vcat = jnp.concatenate([v_ref[0, prev, :], v_ref[0, cur, :]], axis=0).astype(BF16)
            valid = _band_mask(n)
            outs, lses = [], []
            for keep in (first, ~first):
                s = _dot(jnp.where(keep, q, 0.0).astype(BF16), kcat, NT)
                s = jnp.where(valid, s, MASKED)
                m = jnp.max(s, axis=-1, keepdims=True)
                p = jnp.exp(s - m)
                den = jnp.sum(p, axis=-1, keepdims=True)
                outs.append(_dot(p.astype(BF16), vcat, NN) / den)
                lses.append(m + jnp.log(den))
            o_ref[0, cur, :] = jnp.where(first, outs[0], outs[1])
            lse_ref[0, cur, :] = jnp.where(first, lses[0], lses[1])
            return carry

        lax.fori_loop(0, nb * dil, block, 0)

    def slab(base):
        return pl.BlockSpec((1, s_len, LANES), lambda s: (base + 2 * group + s, 0, 0))

    out = pl.BlockSpec((1, s_len, LANES), lambda s: (s, 0, 0))
    shape = jax.ShapeDtypeStruct((2, s_len, LANES), F32)
    return pl.pallas_call(
        body, name=f"attn_fwd_d{dil}", grid=(2,),
        in_specs=[slab(0), slab(6), slab(12)], out_specs=[out, out], out_shape=[shape, shape],
        compiler_params=_params(("arbitrary",)),
    )(qkv, qkv, qkv)


def _pool_mixed(u, halo, i, tm):
    ue = jnp.concatenate([halo, u], axis=0)
    s2 = ue + pltpu.roll(ue, 1, 0)
    s4 = s2 + pltpu.roll(s2, 2, 0)
    s8 = s4 + pltpu.roll(s4, 4, 0)
    s16 = s8 + pltpu.roll(s8, 8, 0)
    grp = lax.broadcasted_iota(jnp.int32, (tm, 256), 1) // HEAD_DIM
    pick = lambda a, b, c, e: jnp.where(grp == 0, a, jnp.where(grp == 1, b, jnp.where(grp == 2, c, e)))
    win_sum = pick(s2[HALO:], s4[HALO:], s8[HALO:], s16[HALO:])
    pos = (i * tm + lax.broadcasted_iota(jnp.int32, (tm, 256), 0)).astype(F32)
    count = jnp.minimum(pos + 1.0, pick(*[float(w) for w in POOL_WINDOWS]))
    return win_sum / count - u, count


def _mix_out(x, u_pool, o_g, lse_g, w_blk, b_pool, pool_scale, w_out_t, gt_m, g_post_mix, g_pre_ffn, sc_f, sh_f, tm):
    s_len, d = x.shape

    def body(x_ref, u_ref, uh_ref, o0, o1, o2, l0, l1, l2, wb_ref, bp_ref, ps_ref, wo_ref,
             gt_ref, g1_ref, g2_ref, sc_ref, sh_ref,
             x1_ref, y1_ref, h2_ref, cat_ref, attn_ref, lall_ref):
        i = pl.program_id(0)
        u = u_ref[...]
        halo = uh_ref[...] * (i > 0).astype(F32)
        mixed, _ = _pool_mixed(u, halo, i, tm)
        y = _dot(mixed.astype(BF16), wb_ref[...], NN) + bp_ref[...]
        pool = y * ps_ref[...]
        attn = []
        for s in range(2):
            la, lb, lc = l0[s], l1[s], l2[s]
            mx = jnp.maximum(jnp.maximum(la, lb), lc)
            ea, eb, ec = jnp.exp(la - mx), jnp.exp(lb - mx), jnp.exp(lc - mx)
            den = ea + eb + ec
            lall_ref[s] = mx + jnp.log(den)
            attn.append((ea / den) * o0[s] + (eb / den) * o1[s] + (ec / den) * o2[s])
        attn = jnp.concatenate(attn, axis=1)
        attn_ref[...] = attn
        cat = jnp.concatenate([pool, attn], axis=1).astype(BF16)
        cat_ref[...] = cat
        y1 = _dot(cat, wo_ref[...], NT)
        y1_ref[...] = y1
        x1 = x_ref[...] + gt_ref[...] * (y1 * _rstd(y1) * g1_ref[...])
        x1_ref[...] = x1
        h2 = (x1 * _rstd(x1) * g2_ref[...]) * (1.0 + sc_ref[...]) + sh_ref[...]
        h2_ref[...] = h2.astype(BF16)

    tile = lambda w: pl.BlockSpec((tm, w), lambda i: (i, 0))
    slab = pl.BlockSpec((2, tm, LANES), lambda i: (0, i, 0))
    const = lambda a: pl.BlockSpec(a.shape, lambda i: (0,) * a.ndim)
    return pl.pallas_call(
        body, name="mix_out", grid=(s_len // tm,),
        in_specs=[tile(d), tile(256), pl.BlockSpec((HALO, 256), lambda i: (_halo_before(i, tm), 0)),
                  slab, slab, slab, slab, slab, slab,
                  const(w_blk), const(b_pool), const(pool_scale), const(w_out_t),
                  const(gt_m), const(g_post_mix), const(g_pre_ffn), const(sc_f), const(sh_f)],
        out_specs=[tile(d), tile(d), tile(d), tile(512), tile(256), slab],
        out_shape=[jax.ShapeDtypeStruct((s_len, d), F32), jax.ShapeDtypeStruct((s_len, d), F32),
                   jax.ShapeDtypeStruct((s_len, d), BF16), jax.ShapeDtypeStruct((s_len, 512), BF16),
                   jax.ShapeDtypeStruct((s_len, 256), F32), jax.ShapeDtypeStruct((2, s_len, LANES), F32)],
        compiler_params=_params(("arbitrary",)),
    )(x, u_pool, u_pool, *o_g, *lse_g, w_blk, b_pool, pool_scale, w_out_t, gt_m, g_post_mix, g_pre_ffn, sc_f, sh_f)


def _conv_gate(gate_ext, cw_ref, cb_ref):
    gc = gate_ext * cw_ref[2:3, :] + pltpu.roll(gate_ext, 1, 0) * cw_ref[1:2, :] + pltpu.roll(gate_ext, 2, 0) * cw_ref[0:1, :]
    return gc[HALO:] + cb_ref[...]


def _ffn_fwd_loss(h2, x1, target, w_up_t, w_down, conv_w, conv_b, gt_f, g_post_ffn, tm, tf):
    s_len, d = x1.shape
    d_ff = w_down.shape[0]
    n_f = d_ff // tf

    def body(h_ref, hh_ref, x1_ref, tgt_ref, wg_ref, wv_ref, wd_ref, cw_ref, cb_ref, gt_ref, g_ref,
             gate_ref, val_ref, dy2_ref, dout_ref, sums_ref, loss_ref, acc_ref):
        i, j = pl.program_id(0), pl.program_id(1)

        @pl.when((i == 0) & (j == 0))
        def _():
            sums_ref[...] = jnp.zeros_like(sums_ref)
            loss_ref[...] = jnp.zeros_like(loss_ref)

        h_ext = jnp.concatenate([hh_ref[...], h_ref[...]], axis=0)
        gate_ext = _dot(h_ext, wg_ref[...], NT)
        row = lax.broadcasted_iota(jnp.int32, gate_ext.shape, 0)
        gate_ext = jnp.where((row < HALO) & (i == 0), 0.0, gate_ext)
        val = _dot(h_ref[...], wv_ref[...], NT)
        act, _ = _gelu_parts(_conv_gate(gate_ext, cw_ref, cb_ref))
        gate_ref[...] = gate_ext[HALO:].astype(BF16)
        val_ref[...] = val.astype(BF16)
        part = _dot((act * val).astype(BF16), wd_ref[...], NN)

        @pl.when(j == 0)
        def _():
            acc_ref[...] = part

        @pl.when(j > 0)
        def _():
            acc_ref[...] += part

        @pl.when(j == n_f - 1)
        def _():
            y2 = acc_ref[...]
            rstd = _rstd(y2)
            n = y2 * rstd
            rn = n * g_ref[...]
            err = x1_ref[...] + gt_ref[...] * rn - tgt_ref[...]
            loss_ref[...] += 0.5 * jnp.sum(jnp.mean(err * err, axis=-1, keepdims=True), axis=0, keepdims=True)
            dout = err * (1.0 / d)
            dout_ref[...] = dout
            drn = dout * gt_ref[...]
            sums_ref[0:1, :] += jnp.sum(dout * rn, axis=0, keepdims=True)
            sums_ref[1:2, :] += jnp.sum(drn * n, axis=0, keepdims=True)
            dy2_ref[...] = _norm_bwd(drn * g_ref[...], n, rstd).astype(BF16)

    tok = lambda w: pl.BlockSpec((tm, w), lambda i, j: (i, 0))
    tokf = pl.BlockSpec((tm, tf), lambda i, j: (i, j))
    vec = pl.BlockSpec((1, d), lambda i, j: (0, 0))
    return pl.pallas_call(
        body, name="ffn_fwd_loss", grid=(s_len // tm, n_f),
        in_specs=[tok(d), pl.BlockSpec((HALO, d), lambda i, j: (_halo_before(i, tm), 0)), tok(d), tok(d),
                  pl.BlockSpec((tf, d), lambda i, j: (j, 0)), pl.BlockSpec((tf, d), lambda i, j: (j + n_f, 0)),
                  pl.BlockSpec((tf, d), lambda i, j: (j, 0)),
                  pl.BlockSpec((3, tf), lambda i, j: (0, j)), pl.BlockSpec((1, tf), lambda i, j: (0, j)), vec, vec],
        out_specs=[tokf, tokf, tok(d), tok(d), pl.BlockSpec((8, d), lambda i, j: (0, 0)),
                   pl.BlockSpec((8, LANES), lambda i, j: (0, 0))],
        out_shape=[jax.ShapeDtypeStruct((s_len, d_ff), BF16), jax.ShapeDtypeStruct((s_len, d_ff), BF16),
                   jax.ShapeDtypeStruct((s_len, d), BF16), jax.ShapeDtypeStruct((s_len, d), F32),
                   jax.ShapeDtypeStruct((8, d), F32), jax.ShapeDtypeStruct((8, LANES), F32)],
        scratch_shapes=[pltpu.VMEM((tm, d), F32)],
        compiler_params=_params(("arbitrary", "arbitrary")),
    )(h2, h2, x1, target, w_up_t, w_up_t, w_down, conv_w, conv_b, gt_f, g_post_ffn)


def _ffn_bwd_act(dy2, gate, val, w_down, conv_w, conv_b, tm, tf):
    s_len, d = dy2.shape
    d_ff = w_down.shape[0]
    n_t = s_len // tm

    def body(dy_ref, g_ref, gh_ref, v_ref, wd_ref, cw_ref, cb_ref,
             dgc_ref, dval_ref, dwd_ref, dcw_ref, dcb_ref, acc_ref):
        i = pl.program_id(1)
        gate_ext = jnp.concatenate([gh_ref[...], g_ref[...]], axis=0).astype(F32)
        row = lax.broadcasted_iota(jnp.int32, gate_ext.shape, 0)
        gate_ext = jnp.where((row < HALO) & (i == 0), 0.0, gate_ext)
        act, dact = _gelu_parts(_conv_gate(gate_ext, cw_ref, cb_ref))
        val = v_ref[...].astype(F32)
        da = _dot(dy_ref[...], wd_ref[...], NT)
        dgc = da * val * dact
        dgc_ref[...] = dgc.astype(BF16)
        dval_ref[...] = (da * act).astype(BF16)
        dwd = _dot((act * val).astype(BF16), dy_ref[...], TN)
        taps = jnp.concatenate(
            [jnp.sum(dgc * pltpu.roll(gate_ext, 2 - k, 0)[HALO:], axis=0, keepdims=True) if k < 2
             else jnp.sum(dgc * gate_ext[HALO:], axis=0, keepdims=True) for k in range(3)], axis=0)
        bias = jnp.sum(dgc, axis=0, keepdims=True)

        @pl.when(i == 0)
        def _():
            acc_ref[...] = dwd
            dcw_ref[...] = taps
            dcb_ref[...] = bias

        @pl.when(i > 0)
        def _():
            acc_ref[...] += dwd
            dcw_ref[...] += taps
            dcb_ref[...] += bias

        @pl.when(i == n_t - 1)
        def _():
            dwd_ref[...] = acc_ref[...].astype(BF16)

    tokf = pl.BlockSpec((tm, tf), lambda j, i: (i, j))
    return pl.pallas_call(
        body, name="ffn_bwd_act", grid=(d_ff // tf, n_t),
        in_specs=[pl.BlockSpec((tm, d), lambda j, i: (i, 0)), tokf,
                  pl.BlockSpec((HALO, tf), lambda j, i: (_halo_before(i, tm), j)), tokf,
                  pl.BlockSpec((tf, d), lambda j, i: (j, 0)),
                  pl.BlockSpec((3, tf), lambda j, i: (0, j)), pl.BlockSpec((1, tf), lambda j, i: (0, j))],
        out_specs=[tokf, tokf, pl.BlockSpec((tf, d), lambda j, i: (j, 0)),
                   pl.BlockSpec((3, tf), lambda j, i: (0, j)), pl.BlockSpec((1, tf), lambda j, i: (0, j))],
        out_shape=[jax.ShapeDtypeStruct((s_len, d_ff), BF16), jax.ShapeDtypeStruct((s_len, d_ff), BF16),
                   jax.ShapeDtypeStruct((d_ff, d), BF16), jax.ShapeDtypeStruct((3, d_ff), F32),
                   jax.ShapeDtypeStruct((1, d_ff), F32)],
        scratch_shapes=[pltpu.VMEM((tf, d), F32)],
        compiler_params=_params(("arbitrary", "arbitrary")),
    )(dy2, gate, gate, val, w_down, conv_w, conv_b)


def _ffn_bwd_up(dgc, dval, w_up_t, conv_w, tm):
    s_len, d_ff = dgc.shape
    d = w_up_t.shape[1]
    n_t = s_len // tm

    def body(dg_ref, dgn_ref, dv_ref, cw_ref, w_ref, dup_ref, dh_ref):
        i = pl.program_id(0)
        nxt = dgn_ref[...].astype(F32) * (i < n_t - 1).astype(F32)
        ext = jnp.concatenate([dg_ref[...].astype(F32), nxt], axis=0)
        rows = tm + HALO
        dgate = (ext * cw_ref[2:3, :] + pltpu.roll(ext, rows - 1, 0) * cw_ref[1:2, :]
                 + pltpu.roll(ext, rows - 2, 0) * cw_ref[0:1, :])[:tm]
        dup = jnp.concatenate([dgate.astype(BF16), dv_ref[...]], axis=1)
        dup_ref[...] = dup
        dh_ref[...] = _dot(dup, w_ref[...], NN)

    tokf = pl.BlockSpec((tm, d_ff), lambda i: (i, 0))
    return pl.pallas_call(
        body, name="ffn_bwd_up", grid=(n_t,),
        in_specs=[tokf, pl.BlockSpec((HALO, d_ff), lambda i: (jnp.minimum((i + 1) * (tm // HALO), s_len // HALO - 1), 0)),
                  tokf, pl.BlockSpec((3, d_ff), lambda i: (0, 0)), pl.BlockSpec((2 * d_ff, d), lambda i: (0, 0))],
        out_specs=[pl.BlockSpec((tm, 2 * d_ff), lambda i: (i, 0)), pl.BlockSpec((tm, d), lambda i: (i, 0))],
        out_shape=[jax.ShapeDtypeStruct((s_len, 2 * d_ff), BF16), jax.ShapeDtypeStruct((s_len, d), F32)],
        compiler_params=_params(("arbitrary",)),
    )(dgc, dgc, dval, conv_w, w_up_t)


def _mix_bwd(dh2, dout, x1, y1, cat, attn, w_out_t, sc_f, g_pre_ffn, gt_m, g_post_mix, tm):
    s_len, d = x1.shape
    n_t = s_len // tm

    def body(dh_ref, do_ref, x1_ref, y1_ref, cat_ref, at_ref, wo_ref, sc_ref, g2_ref, gt_ref, g1_ref,
             dx1_ref, dpool_ref, dattn_ref, delta_ref, dwo_ref, sums_ref, acc_ref):
        i = pl.program_id(0)
        dh = dh_ref[...]
        x1 = x1_ref[...]
        r2 = _rstd(x1)
        n2 = x1 * r2
        ng = n2 * g2_ref[...]
        dng = dh * (1.0 + sc_ref[...])
        dx1 = do_ref[...] + _norm_bwd(dng * g2_ref[...], n2, r2)
        dx1_ref[...] = dx1
        y1 = y1_ref[...]
        r1 = _rstd(y1)
        n1 = y1 * r1
        drn = dx1 * gt_ref[...]
        dy1 = _norm_bwd(drn * g1_ref[...], n1, r1).astype(BF16)
        dcat = _dot(dy1, wo_ref[...], NN)
        dpool_ref[...] = dcat[:, 0:256]
        lane = lax.broadcasted_iota(jnp.int32, (tm, LANES), 1)
        first = lane < HEAD_DIM
        for s in range(2):
            da = dcat[:, 256 + s * LANES:256 + (s + 1) * LANES]
            dattn_ref[s] = da
            prod = da * at_ref[:, s * LANES:(s + 1) * LANES]
            tot = jnp.sum(prod, axis=-1, keepdims=True)
            lo = jnp.sum(jnp.where(first, prod, 0.0), axis=-1, keepdims=True)
            delta_ref[s] = jnp.where(first, lo, tot - lo)
        dwo = _dot(dy1, cat_ref[...], TN)
        sums = jnp.concatenate(
            [jnp.sum(dh, axis=0, keepdims=True), jnp.sum(dh * ng, axis=0, keepdims=True),
             jnp.sum(dng * n2, axis=0, keepdims=True), jnp.sum(dx1 * (n1 * g1_ref[...]), axis=0, keepdims=True),
             jnp.sum(drn * n1, axis=0, keepdims=True), jnp.zeros((3, d), F32)], axis=0)

        @pl.when(i == 0)
        def _():
            acc_ref[...] = dwo
            sums_ref[...] = sums

        @pl.when(i > 0)
        def _():
            acc_ref[...] += dwo
            sums_ref[...] += sums

        @pl.when(i == n_t - 1)
        def _():
            dwo_ref[...] = acc_ref[...].astype(BF16)

    tile = lambda w: pl.BlockSpec((tm, w), lambda i: (i, 0))
    slab = pl.BlockSpec((2, tm, LANES), lambda i: (0, i, 0))
    vec = pl.BlockSpec((1, d), lambda i: (0, 0))
    return pl.pallas_call(
        body, name="mix_bwd", grid=(n_t,),
        in_specs=[tile(d), tile(d), tile(d), tile(d), tile(512), tile(256),
                  pl.BlockSpec((d, 512), lambda i: (0, 0)), vec, vec, vec, vec],
        out_specs=[tile(d), tile(256), slab, slab, pl.BlockSpec((d, 512), lambda i: (0, 0)),
                   pl.BlockSpec((8, d), lambda i: (0, 0))],
        out_shape=[jax.ShapeDtypeStruct((s_len, d), F32), jax.ShapeDtypeStruct((s_len, 256), F32),
                   jax.ShapeDtypeStruct((2, s_len, LANES), F32), jax.ShapeDtypeStruct((2, s_len, LANES), F32),
                   jax.ShapeDtypeStruct((d, 512), BF16), jax.ShapeDtypeStruct((8, d), F32)],
        scratch_shapes=[pltpu.VMEM((d, 512), F32)],
        compiler_params=_params(("arbitrary",)),
    )(dh2, dout, x1, y1, cat, attn, w_out_t, sc_f, g_pre_ffn, gt_m, g_post_mix)


def _pool_bwd(dpool, u_pool, w_blk, b_pool, pool_scale, tm):
    s_len = dpool.shape[0]
    n_t = s_len // tm

    def body(dp_ref, dpn_ref, u_ref, uh_ref, wb_ref, bp_ref, ps_ref, du_ref, dwb_ref, sums_ref):
        i = pl.program_id(0)
        u = u_ref[...]
        mixed, _ = _pool_mixed(u, uh_ref[...] * (i > 0).astype(F32), i, tm)
        mixed_b = mixed.astype(BF16)
        y = _dot(mixed_b, wb_ref[...], NN) + bp_ref[...]
        dp = dp_ref[...]
        dy = dp * ps_ref[...]
        dwb = _dot(mixed_b, dy.astype(BF16), TN)
        sums = jnp.concatenate([jnp.sum(dy, axis=0, keepdims=True), jnp.sum(dp * y, axis=0, keepdims=True),
                                jnp.zeros((6, 256), F32)], axis=0)
        dp_ext = jnp.concatenate([dp, dpn_ref[...] * (i < n_t - 1).astype(F32)], axis=0)
        dmix = _dot((dp_ext * ps_ref[...]).astype(BF16), wb_ref[...], NT)
        rows = tm + HALO
        grp = lax.broadcasted_iota(jnp.int32, (rows, 256), 1) // HEAD_DIM
        pick = lambda a, b, c, e: jnp.where(grp == 0, a, jnp.where(grp == 1, b, jnp.where(grp == 2, c, e)))
        pos = (i * tm + lax.broadcasted_iota(jnp.int32, (rows, 256), 0)).astype(F32)
        z = dmix / jnp.minimum(pos + 1.0, pick(*[float(w) for w in POOL_WINDOWS]))
        f2 = z + pltpu.roll(z, rows - 1, 0)
        f4 = f2 + pltpu.roll(f2, rows - 2, 0)
        f8 = f4 + pltpu.roll(f4, rows - 4, 0)
        f16 = f8 + pltpu.roll(f8, rows - 8, 0)
        du_ref[...] = (pick(f2, f4, f8, f16) - dmix)[:tm]

        @pl.when(i == 0)
        def _():
            dwb_ref[...] = dwb
            sums_ref[...] = sums

        @pl.when(i > 0)
        def _():
            dwb_ref[...] += dwb
            sums_ref[...] += sums

    tile = pl.BlockSpec((tm, 256), lambda i: (i, 0))
    const = lambda a: pl.BlockSpec(a.shape, lambda i: (0,) * a.ndim)
    return pl.pallas_call(
        body, name="pool_bwd", grid=(n_t,),
        in_specs=[tile, pl.BlockSpec((HALO, 256), lambda i: (jnp.minimum((i + 1) * (tm // HALO), s_len // HALO - 1), 0)),
                  tile, pl.BlockSpec((HALO, 256), lambda i: (_halo_before(i, tm), 0)),
                  const(w_blk), const(b_pool), const(pool_scale)],
        out_specs=[tile, pl.BlockSpec((256, 256), lambda i: (0, 0)), pl.BlockSpec((8, 256), lambda i: (0, 0))],
        out_shape=[jax.ShapeDtypeStruct((s_len, 256), F32), jax.ShapeDtypeStruct((256, 256), F32),
                   jax.ShapeDtypeStruct((8, 256), F32)],
        compiler_params=_params(("arbitrary",)),
    )(dpool, dpool, u_pool, u_pool, w_blk, b_pool, pool_scale)


def _attn_bwd(qkv, dattn, lse_all, delta, group, dil):
    s_len = qkv.shape[1]
    nb = s_len // (BLOCK * dil)

    def body(q_ref, k_ref, v_ref, do_ref, l_ref, dl_ref, dq_ref, dk_ref, dv_ref):
        lane = lax.broadcasted_iota(jnp.int32, (BLOCK, LANES), 1)
        first = lane < HEAD_DIM
        dk_ref[...] = jnp.zeros_like(dk_ref)
        dv_ref[...] = jnp.zeros_like(dv_ref)

        def block(t, carry):
            r, n = t // nb, t % nb
            cur = _block_rows(n, r, dil)
            prev = _block_rows(jnp.maximum(n - 1, 0), r, dil)
            q = q_ref[0, cur, :]
            do = do_ref[0, cur, :]
            lse = l_ref[0, cur, :]
            dlt = dl_ref[0, cur, :]
            kcat = jnp.concatenate([k_ref[0, prev, :], k_ref[0, cur, :]], axis=0).astype(BF16)
            vcat = jnp.concatenate([v_ref[0, prev, :], v_ref[0, cur, :]], axis=0).astype(BF16)
            valid = _band_mask(n)
            dq, dkc, dvc = [], None, None
            for h, keep in enumerate((first, ~first)):
                col = slice(h * HEAD_DIM, h * HEAD_DIM + 1)
                qh = jnp.where(keep, q, 0.0).astype(BF16)
                doh = jnp.where(keep, do, 0.0).astype(BF16)
                s = _dot(qh, kcat, NT)
                p = jnp.where(valid, jnp.exp(s - lse[:, col]), 0.0)
                dp = _dot(doh, vcat, NT)
                ds = (p * (dp - dlt[:, col])).astype(BF16)
                dq.append(_dot(ds, kcat, NN))
                dk_h = _dot(ds, qh, TN)
                dv_h = _dot(p.astype(BF16), doh, TN)
                dkc = dk_h if dkc is None else dkc + dk_h
                dvc = dv_h if dvc is None else dvc + dv_h
            dq_ref[0, cur, :] = jnp.where(first, dq[0], dq[1])
            dk_ref[0, prev, :] += dkc[:BLOCK]
            dv_ref[0, prev, :] += dvc[:BLOCK]
            dk_ref[0, cur, :] += dkc[BLOCK:]
            dv_ref[0, cur, :] += dvc[BLOCK:]
            return carry

        lax.fori_loop(0, nb * dil, block, 0)

    def slab(base):
        return pl.BlockSpec((1, s_len, LANES), lambda s: (base + 2 * group + s, 0, 0))

    one = pl.BlockSpec((1, s_len, LANES), lambda s: (s, 0, 0))
    shape = jax.ShapeDtypeStruct((2, s_len, LANES), F32)
    return pl.pallas_call(
        body, name=f"attn_bwd_d{dil}", grid=(2,),
        in_specs=[slab(0), slab(6), slab(12), one, one, one],
        out_specs=[one, one, one], out_shape=[shape, shape, shape],
        compiler_params=_params(("arbitrary",)),
    )(qkv, qkv, qkv, dattn, lse_all, delta)


def _inproj_bwd(du, dqkv, rope, w_in_t, x, dx1, sc_m, g_pre_mix, tm):
    s_len, d = x.shape
    n_proj = w_in_t.shape[0]
    n_t = s_len // tm

    def body(du_ref, *refs):
        dref = refs[:9]
        rope_ref, w_ref, x_ref, dx1_ref, sc_ref, g_ref, dproj_ref, dx_ref, sums_ref = refs[9:]
        i = pl.program_id(0)
        cols = [du_ref[...].astype(BF16)]
        for kind in range(3):
            for grp in range(3):
                for s in range(2):
                    piece = dref[3 * grp + kind][s]
                    if kind < 2:
                        piece = _rope_bwd(piece, rope_ref)
                    if kind == 0:
                        piece = piece * (HEAD_DIM ** -0.5)
                    cols.append(piece.astype(BF16))
        dproj = jnp.concatenate(cols, axis=1)
        dproj_ref[...] = dproj
        dh = _dot(dproj, w_ref[...], NN)
        xv = x_ref[...]
        r = _rstd(xv)
        n = xv * r
        dng = dh * (1.0 + sc_ref[...])
        dx_ref[...] = dx1_ref[...] + _norm_bwd(dng * g_ref[...], n, r)
        sums = jnp.concatenate([jnp.sum(dh, axis=0, keepdims=True), jnp.sum(dh * (n * g_ref[...]), axis=0, keepdims=True),
                                jnp.sum(dng * n, axis=0, keepdims=True), jnp.zeros((5, d), F32)], axis=0)

        @pl.when(i == 0)
        def _():
            sums_ref[...] = sums

        @pl.when(i > 0)
        def _():
            sums_ref[...] += sums

    tile = lambda w: pl.BlockSpec((tm, w), lambda i: (i, 0))
    slab = pl.BlockSpec((2, tm, LANES), lambda i: (0, i, 0))
    vec = pl.BlockSpec((1, d), lambda i: (0, 0))
    return pl.pallas_call(
        body, name="inproj_bwd", grid=(n_t,),
        in_specs=[tile(256)] + [slab] * 9 + [pl.BlockSpec((3, tm, LANES), lambda i: (0, i, 0)),
                                             pl.BlockSpec((n_proj, d), lambda i: (0, 0)), tile(d), tile(d), vec, vec],
        out_specs=[tile(n_proj), tile(d), pl.BlockSpec((8, d), lambda i: (0, 0))],
        out_shape=[jax.ShapeDtypeStruct((s_len, n_proj), BF16), jax.ShapeDtypeStruct((s_len, d), F32),
                   jax.ShapeDtypeStruct((8, d), F32)],
        compiler_params=_params(("arbitrary",)),
    )(du, *dqkv, rope, w_in_t, x, dx1, sc_m, g_pre_mix)


def _wgrad(a, b, name, tk, tmm):
    s_len, m = a.shape
    n = b.shape[1]
    n_k = s_len // tk

    def body(a_ref, b_ref, o_ref, acc_ref):
        k = pl.program_id(1)
        part = _dot(a_ref[...], b_ref[...], TN)

        @pl.when(k == 0)
        def _():
            acc_ref[...] = part

        @pl.when(k > 0)
        def _():
            acc_ref[...] += part

        @pl.when(k == n_k - 1)
        def _():
            o_ref[...] = acc_ref[...].astype(BF16)

    return pl.pallas_call(
        body, name=name, grid=(m // tmm, n_k),
        in_specs=[pl.BlockSpec((tk, tmm), lambda j, k: (k, j)), pl.BlockSpec((tk, n), lambda j, k: (k, 0))],
        out_specs=pl.BlockSpec((tmm, n), lambda j, k: (j, 0)),
        out_shape=jax.ShapeDtypeStruct((m, n), BF16),
        scratch_shapes=[pltpu.VMEM((tmm, n), F32)],
        compiler_params=_params(("arbitrary", "arbitrary")),
    )(a, b)


def _place():
    return lax.axis_index("x"), lax.axis_index("y"), lax.axis_index("c")


def _peer(k):
    x, y, c = _place()
    bx, by, bc = (k >> 2) & 1, (k >> 1) & 1, k & 1
    return (x ^ bx if bx else x, y ^ by if by else y, c ^ bc if bc else c)


def _index(pos):
    return 4 * pos[0] + 2 * pos[1] + pos[2]


def _ada_exchange(c_rows, w_ada, b_ada_cols):
    d = c_rows.shape[1]
    ncol = w_ada.shape[1]

    def body(c_ref, w_ref, b_ref, call_ref, mod_ref, stage_ref, send_sems, recv_sems):
        me = _index(_place())
        call_ref[me] = c_ref[...]

        def gather(k):
            return pltpu.make_async_remote_copy(
                src_ref=c_ref, dst_ref=call_ref.at[me], send_sem=send_sems.at[0, k - 1], recv_sem=recv_sems.at[0, k - 1],
                device_id=_peer(k), device_id_type=MESH)

        for k in range(1, N_DEV):
            gather(k).start()
        for k in range(1, N_DEV):
            gather(k).wait_recv()
        cv = jnp.concatenate([call_ref[b, 0:1, :] for b in range(N_DEV)], axis=0)
        act = cv * jax.nn.sigmoid(cv)
        mod = lax.dot_general(act, w_ref[...], NN, preferred_element_type=F32,
                              precision=lax.Precision.HIGHEST) + b_ref[...]
        for b in range(N_DEV):
            stage_ref[b] = jnp.broadcast_to(mod[b:b + 1, :], (8, ncol))
        mod_ref[me] = stage_ref[me]

        def scatter(k):
            return pltpu.make_async_remote_copy(
                src_ref=stage_ref.at[_index(_peer(k))], dst_ref=mod_ref.at[me],
                send_sem=send_sems.at[1, k - 1], recv_sem=recv_sems.at[1, k - 1],
                device_id=_peer(k), device_id_type=MESH)

        for k in range(1, N_DEV):
            scatter(k).start()
        for k in range(1, N_DEV):
            scatter(k).wait_recv()
        for k in range(1, N_DEV):
            gather(k).wait_send()
            scatter(k).wait_send()

    vmem = pl.BlockSpec(memory_space=pltpu.VMEM)
    return pl.pallas_call(
        body, name="ada_exchange",
        in_specs=[vmem, vmem, vmem], out_specs=[vmem, vmem],
        out_shape=[jax.ShapeDtypeStruct((N_DEV, 8, d), F32), jax.ShapeDtypeStruct((N_DEV, 8, ncol), F32)],
        scratch_shapes=[pltpu.VMEM((N_DEV, 8, ncol), F32), pltpu.SemaphoreType.DMA((2, N_DEV - 1)),
                        pltpu.SemaphoreType.DMA((2, N_DEV - 1))],
        compiler_params=_params(),
    )(c_rows, w_ada, b_ada_cols)


def _gather_weights(shards):
    n_w = len(shards)

    def body(*refs):
        srcs, outs = refs[:n_w], refs[n_w:2 * n_w]
        send_sems, recv_sems, local_sems = refs[2 * n_w:]
        x, y, c = _place()
        me, sibling = (x, y, c), (x, y, 1 - c)
        chips = [(1 - x, y), (x, 1 - y), (1 - x, 1 - y)]

        def rows(w, pos):
            r = shards[w].shape[0]
            return outs[w].at[pl.ds(pl.multiple_of(_index(pos) * r, 16), r), :]

        def copy(k, w, block, to, own=False):
            return pltpu.make_async_remote_copy(
                src_ref=srcs[w] if own else rows(w, block), dst_ref=rows(w, block),
                send_sem=send_sems.at[k, w], recv_sem=recv_sems.at[k, w], device_id=to, device_id_type=MESH)

        mine = [pltpu.make_async_copy(srcs[w], rows(w, me), local_sems.at[w]) for w in range(n_w)]
        for cp in mine:
            cp.start()
        first = [copy(0, w, me, sibling, own=True) for w in range(n_w)]
        first += [copy(1 + j, w, me, (*chip, c), own=True) for j, chip in enumerate(chips) for w in range(n_w)]
        for cp in first:
            cp.start()
        passed = []
        for j, chip in enumerate(chips):
            for w in range(n_w):
                copy(1 + j, w, (*chip, c), me).wait_recv()
                fwd = copy(4 + j, w, (*chip, c), sibling)
                fwd.start()
                passed.append(fwd)
        for w in range(n_w):
            copy(0, w, sibling, me).wait_recv()
        for j, chip in enumerate(chips):
            for w in range(n_w):
                copy(4 + j, w, (*chip, 1 - c), me).wait_recv()
        for cp in first + passed:
            cp.wait_send()
        for cp in mine:
            cp.wait()

    hbm = pl.BlockSpec(memory_space=pltpu.HBM)
    return pl.pallas_call(
        body, name="gather_weights",
        in_specs=[hbm] * n_w, out_specs=[hbm] * n_w,
        out_shape=[jax.ShapeDtypeStruct((N_DEV * s.shape[0], s.shape[1]), s.dtype) for s in shards],
        scratch_shapes=[pltpu.SemaphoreType.DMA((N_DEV - 1, n_w)), pltpu.SemaphoreType.DMA((N_DEV - 1, n_w)),
                        pltpu.SemaphoreType.DMA((n_w,))],
        compiler_params=_params(),
    )(*shards)


def _scatter_grads(grads):
    n_w = len(grads)

    def body(*refs):
        srcs, outs = refs[:n_w], refs[n_w:2 * n_w]
        send_sems, recv_sems, local_sems = refs[2 * n_w:]
        me = _index(_place())

        def slab(w, dev):
            r = grads[w].shape[0] // N_DEV
            return srcs[w].at[pl.ds(pl.multiple_of(dev * r, 16), r), :]

        def copy(k, w):
            return pltpu.make_async_remote_copy(
                src_ref=slab(w, _index(_peer(k))), dst_ref=outs[w].at[me],
                send_sem=send_sems.at[k - 1, w], recv_sem=recv_sems.at[k - 1, w],
                device_id=_peer(k), device_id_type=MESH)

        mine = [pltpu.make_async_copy(slab(w, me), outs[w].at[me], local_sems.at[w]) for w in range(n_w)]
        for cp in mine:
            cp.start()
        sends = [copy(k, w) for k in range(1, N_DEV) for w in range(n_w)]
        for cp in sends:
            cp.start()
        for cp in sends:
            cp.wait_recv()
        for cp in sends:
            cp.wait_send()
        for cp in mine:
            cp.wait()

    hbm = pl.BlockSpec(memory_space=pltpu.HBM)
    return pl.pallas_call(
        body, name="scatter_grads",
        in_specs=[hbm] * n_w, out_specs=[hbm] * n_w,
        out_shape=[jax.ShapeDtypeStruct((N_DEV, g.shape[0] // N_DEV, g.shape[1]), g.dtype) for g in grads],
        scratch_shapes=[pltpu.SemaphoreType.DMA((N_DEV - 1, n_w)), pltpu.SemaphoreType.DMA((N_DEV - 1, n_w)),
                        pltpu.SemaphoreType.DMA((n_w,))],
        compiler_params=_params(),
    )(*grads)


def _peer_copies(mode, srcs, lands, send_sems, recv_sems):
    me = _index(_place())
    copies = []
    for k in range(1, N_DEV):
        peer = _peer(k)
        for w, (src, land) in enumerate(zip(srcs, lands)):
            if mode == "gather":
                r = src.shape[0]
                dst = land.at[pl.ds(pl.multiple_of(me * r, 16), r), :]
            else:
                r = src.shape[0] // N_DEV
                src = src.at[pl.ds(pl.multiple_of(_index(peer) * r, 16), r), :]
                dst = land.at[me]
            copies.append(pltpu.make_async_remote_copy(
                src_ref=src, dst_ref=dst, send_sem=send_sems.at[(k - 1) * len(srcs) + w],
                recv_sem=recv_sems.at[(k - 1) * len(srcs) + w],
                device_id=peer, device_id_type=MESH))
    return copies


def _exchange_start(mode, srcs, lands, name):
    n = len(srcs)

    def body(*refs):
        for cp in _peer_copies(mode, refs[:n], refs[n:2 * n], refs[2 * n], refs[2 * n + 1]):
            cp.start()
        refs[-1][...] = jnp.zeros_like(refs[-1])

    hbm, sem = pl.BlockSpec(memory_space=pltpu.HBM), pl.BlockSpec(memory_space=pltpu.SEMAPHORE)
    arrays = list(srcs) + list(lands)
    out = pl.pallas_call(
        body, name=name,
        out_shape=(pltpu.SemaphoreType.DMA(((N_DEV - 1) * n,)), pltpu.SemaphoreType.DMA(((N_DEV - 1) * n,)),
                   *[pltpu.HBM(a.shape, a.dtype) for a in arrays], jax.ShapeDtypeStruct((8, LANES), F32)),
        in_specs=[hbm] * (2 * n), out_specs=(sem, sem, *[hbm] * (2 * n), pl.BlockSpec(memory_space=pltpu.VMEM)),
        input_output_aliases={i: 2 + i for i in range(2 * n)},
        compiler_params=pltpu.CompilerParams(has_side_effects=pltpu.SideEffectType.DATAFLOW_SIDE_EFFECTING),
    )(*[pltpu.with_memory_space_constraint(a, pltpu.HBM) for a in arrays])
    return out[0], out[1], out[2:2 + n], out[2 + n:2 + 2 * n], out[-1]


def _exchange_wait(mode, send_sems, recv_sems, srcs, lands, after, name):
    n = len(srcs)

    def body(*refs):
        copies = _peer_copies(mode, refs[:n], refs[n:2 * n], refs[2 * n], refs[2 * n + 1])
        for cp in copies:
            cp.wait_send()
        for cp in copies:
            cp.wait_recv()

    hbm, sem = pl.BlockSpec(memory_space=pltpu.HBM), pl.BlockSpec(memory_space=pltpu.SEMAPHORE)
    arrays = list(srcs) + list(lands)
    out = pl.pallas_call(
        body, name=name, out_shape=tuple(pltpu.HBM(a.shape, a.dtype) for a in arrays),
        in_specs=[hbm] * (2 * n) + [sem, sem, pl.BlockSpec(memory_space=pl.ANY)], out_specs=tuple([hbm] * (2 * n)),
        input_output_aliases={i: i for i in range(2 * n)},
        compiler_params=pltpu.CompilerParams(has_side_effects=pltpu.SideEffectType.DATAFLOW_SIDE_EFFECTING),
    )(*arrays, send_sems, recv_sems, after)
    return out[n:]


def _allreduce_small(packed, name):
    rows = packed.shape[0]

    def body(p_ref, all_ref, tot_ref, send_sems, recv_sems):
        me = _index(_place())
        all_ref[me] = p_ref[...]

        def copy(k):
            return pltpu.make_async_remote_copy(
                src_ref=p_ref, dst_ref=all_ref.at[me], send_sem=send_sems.at[k - 1], recv_sem=recv_sems.at[k - 1],
                device_id=_peer(k), device_id_type=MESH)

        for k in range(1, N_DEV):
            copy(k).start()
        for k in range(1, N_DEV):
            copy(k).wait_recv()
        tot = all_ref[0]
        for dev in range(1, N_DEV):
            tot = tot + all_ref[dev]
        tot_ref[...] = tot
        for k in range(1, N_DEV):
            copy(k).wait_send()

    vmem = pl.BlockSpec(memory_space=pltpu.VMEM)
    return pl.pallas_call(
        body, name=name, in_specs=[vmem], out_specs=[vmem, vmem],
        out_shape=[jax.ShapeDtypeStruct((N_DEV, rows, LANES), F32), jax.ShapeDtypeStruct((rows, LANES), F32)],
        scratch_shapes=[pltpu.SemaphoreType.DMA((N_DEV - 1,)), pltpu.SemaphoreType.DMA((N_DEV - 1,))],
        compiler_params=_params(),
    )(packed)


def _sum_slabs(parts, name, tr):
    _, rows, cols = parts.shape

    def body(p_ref, o_ref):
        tot = p_ref[0].astype(F32)
        for dev in range(1, N_DEV):
            tot = tot + p_ref[dev].astype(F32)
        o_ref[...] = tot

    return pl.pallas_call(
        body, name=name, grid=(rows // tr,),
        in_specs=[pl.BlockSpec((N_DEV, tr, cols), lambda i: (0, i, 0))],
        out_specs=pl.BlockSpec((tr, cols), lambda i: (i, 0)),
        out_shape=jax.ShapeDtypeStruct((rows, cols), F32),
        compiler_params=_params(("arbitrary",)),
    )(parts)


def _adam_math(w, g, m, v):
    m = ADAM_B1 * m + (1.0 - ADAM_B1) * g
    v = ADAM_B2 * v + (1.0 - ADAM_B2) * (g * g)
    m_hat = m / (1.0 - ADAM_B1 ** ADAM_STEP)
    v_hat = v / (1.0 - ADAM_B2 ** ADAM_STEP)
    delta = -ADAM_LR * (m_hat / (jnp.sqrt(v_hat) + ADAM_EPS) + ADAM_WD * w)
    return delta, m, v


def _adam(w, g, m, v, name, tr):
    rows, cols = w.shape

    def body(w_ref, g_ref, m_ref, v_ref, d_ref, nm_ref, nv_ref):
        d_ref[...], nm_ref[...], nv_ref[...] = _adam_math(w_ref[...], g_ref[...], m_ref[...], v_ref[...])

    spec = pl.BlockSpec((tr, cols), lambda i: (i, 0))
    shape = jax.ShapeDtypeStruct((rows, cols), F32)
    return pl.pallas_call(
        body, name=name, grid=(rows // tr,), in_specs=[spec] * 4, out_specs=[spec] * 3,
        out_shape=[shape] * 3, compiler_params=_params(("arbitrary",)),
    )(w, g, m, v)


def _ada_grad_adam(c_all, dmod_cols, w, m, v, tr):
    rows, cols = w.shape

    def body(c_ref, dm_ref, w_ref, m_ref, v_ref, g_ref, d_ref, nm_ref, nv_ref):
        cv = c_ref[...]
        act = cv * jax.nn.sigmoid(cv)
        g = lax.dot_general(act, dm_ref[...], TN, preferred_element_type=F32, precision=lax.Precision.HIGHEST)
        g_ref[...] = g
        d_ref[...], nm_ref[...], nv_ref[...] = _adam_math(w_ref[...], g, m_ref[...], v_ref[...])

    spec = pl.BlockSpec((tr, cols), lambda i: (i, 0))
    shape = jax.ShapeDtypeStruct((rows, cols), F32)
    return pl.pallas_call(
        body, name="ada_grad_adam", grid=(rows // tr,),
        in_specs=[pl.BlockSpec((N_DEV, tr), lambda i: (0, i)), pl.BlockSpec((N_DEV, cols), lambda i: (0, 0)), spec, spec, spec],
        out_specs=[spec] * 4, out_shape=[shape] * 4, compiler_params=_params(("arbitrary",)),
    )(c_all, dmod_cols, w, m, v)


def _rope_tables(positions):
    s_len = positions.shape[0]
    inv_freq = ROPE_THETA ** (-jnp.arange(0, 2 * ROT_HALF, 2, dtype=F32) / (2 * ROT_HALF))
    ang = positions.astype(F32)[:, None] * inv_freq
    cos, sin = jnp.cos(ang), jnp.sin(ang)
    rest = HEAD_DIM - 2 * ROT_HALF
    zero = lambda n: jnp.zeros((s_len, n), F32)
    head = jnp.stack([jnp.concatenate([cos, cos, jnp.ones((s_len, rest), F32)], axis=1),
                      jnp.concatenate([-sin, zero(HEAD_DIM - ROT_HALF)], axis=1),
                      jnp.concatenate([zero(ROT_HALF), sin, zero(rest)], axis=1)])
    return jnp.tile(head, (1, 1, LANES // HEAD_DIM))


def _pad_rows(a, rows):
    return jnp.pad(a, ((0, rows - a.shape[0]), (0, 0)))


def _as_rows(a, rows):
    flat = a.reshape(-1)
    return jnp.pad(flat, (0, rows * LANES - flat.shape[0])).reshape(rows, LANES)


def _sequence_step(xs, target, rope, mods, gains, w_in_t, w_out_t, fetch_ffn, send_ffn_grads, w_blk_b, b_pool_r,
                   pool_scale_r, conv_w_all, conv_b):
    sh_m, sc_m, gt_m, sh_f, sc_f, gt_f = mods
    g_pre_mix, g_post_mix, g_pre_ffn, g_post_ffn = gains
    h1, u_pool, qkv = _premix_inproj(xs, sh_m, sc_m, g_pre_mix, w_in_t, rope, tm=512)
    o_g, lse_g = [], []
    for gi, dil in enumerate(DILATIONS):
        o, lse = _attn_fwd(qkv, gi, dil)
        o_g.append(o)
        lse_g.append(lse)
    x1, y1, h2, cat, attn, lse_all = _mix_out(xs, u_pool, o_g, lse_g, w_blk_b, b_pool_r, pool_scale_r, w_out_t,
                                              gt_m, g_post_mix, g_pre_ffn, sc_f, sh_f, tm=256)
    w_up_t, w_down_f = fetch_ffn(x1)
    gate, val, dy2, dout, sums_ffn, loss_loc = _ffn_fwd_loss(h2, x1, target, w_up_t, w_down_f, conv_w_all, conv_b,
                                                              gt_f, g_post_ffn, tm=256, tf=1408)

    dgc, dval, dw_down, dconv_w, dconv_b = _ffn_bwd_act(dy2, gate, val, w_down_f, conv_w_all, conv_b, tm=1024, tf=256)
    dup, dh2 = _ffn_bwd_up(dgc, dval, w_up_t, conv_w_all, tm=256)
    dw_up_t = _wgrad(dup, h2, "wgrad_up", tk=1024, tmm=1408)
    token = send_ffn_grads(dw_up_t, dw_down)
    if token is not None:
        sc_f = sc_f + token[0:1, 0:1]
    dx1, dpool, dattn, delta, dw_out_t, sums_mix = _mix_bwd(dh2, dout, x1, y1, cat, attn, w_out_t, sc_f, g_pre_ffn,
                                                           gt_m, g_post_mix, tm=256)
    du, dw_blk, sums_pool = _pool_bwd(dpool, u_pool, w_blk_b, b_pool_r, pool_scale_r, tm=512)
    dqkv = []
    for gi, dil in enumerate(DILATIONS):
        dqkv += list(_attn_bwd(qkv, dattn, lse_all, delta, gi, dil))
    dproj, grad_x, sums_in = _inproj_bwd(du, dqkv, rope, w_in_t, xs, dx1, sc_m, g_pre_mix, tm=256)
    dw_in_t = _wgrad(dproj, h1, "wgrad_in", tk=1024, tmm=1280)
    return (loss_loc, grad_x, dw_in_t, dw_out_t, dw_up_t, dw_down, dw_blk, dconv_w, dconv_b,
            sums_in, sums_mix, sums_ffn, sums_pool)


def kernel(x, c, positions, w_ada, b_ada, g_pre_mix, g_post_mix, g_pre_ffn, g_post_ffn, w_in, w_pool, b_pool, pool_scale, w_out, w_up, conv_w, conv_b, w_down, loss_target, m_w_ada, m_b_ada, m_g_pre_mix, m_g_post_mix, m_g_pre_ffn, m_g_post_ffn, m_w_in, m_w_pool, m_b_pool, m_pool_scale, m_w_out, m_w_up, m_conv_w, m_conv_b, m_w_down, v_w_ada, v_b_ada, v_g_pre_mix, v_g_post_mix, v_g_pre_ffn, v_g_post_ffn, v_w_in, v_w_pool, v_b_pool, v_pool_scale, v_w_out, v_w_up, v_conv_w, v_conv_b, v_w_down):
    s_len, d = x.shape[1], x.shape[2]
    d_ff = w_down.shape[1] * N_DEV
    me = _index(_place())
    xs, target = x[0], loss_target[0]

    ncol = w_ada.shape[2]
    b_cols = lax.dynamic_slice(b_ada, (0, me * ncol), (1, ncol))
    c_all, mod = _ada_exchange(jnp.broadcast_to(c, (8, d)), w_ada[0], b_cols)
    c_all = c_all[:, 0, :]
    sh_m, sc_m, gt_m, sh_f, sc_f, gt_f = [mod[:, 0, :].reshape(1, -1)[:, k * d:(k + 1) * d] for k in range(6)]

    w_in_t, w_out_t = _gather_weights([w_in[0].T.astype(BF16), w_out[0].T.astype(BF16)])

    rope = _rope_tables(positions[0])
    w_blk = jnp.zeros((256, 256), F32)
    for gi in range(4):
        w_blk = lax.dynamic_update_slice(w_blk, w_pool[0, gi], (gi * HEAD_DIM, gi * HEAD_DIM))
    w_blk_b = w_blk.astype(BF16)
    b_pool_r, pool_scale_r = b_pool.reshape(1, 256), pool_scale.reshape(1, 256)

    cw_rows = d_ff // LANES
    cw_pack = jnp.concatenate([_pad_rows(lax.dynamic_update_slice(jnp.zeros((1, d_ff), F32), conv_w[0, k:k + 1], (0, me * (d_ff // N_DEV))).reshape(cw_rows, LANES), 24) for k in range(3)], axis=0)
    _, cw_tot = _allreduce_small(cw_pack, "gather_conv_w")
    conv_w_all = jnp.concatenate([cw_tot[24 * k:24 * k + cw_rows].reshape(1, d_ff) for k in range(3)], axis=0)

    up_sh, down_sh = w_up[0].T.astype(BF16), w_down[0].astype(BF16)
    w_in_t, conv_w_all, up_sh, down_sh = lax.optimization_barrier((w_in_t, conv_w_all, up_sh, down_sh))
    lands = [lax.dynamic_update_slice(lax.empty((N_DEV * s.shape[0], s.shape[1]), BF16), s, (me * s.shape[0], 0))
             for s in (up_sh, down_sh)]
    w_send, w_recv, w_src, w_land, w_token = _exchange_start("gather", [up_sh, down_sh], lands, "ffn_weights_start")

    def fetch_ffn(after):
        return _exchange_wait("gather", w_send, w_recv, w_src, w_land, after, "ffn_weights_wait")

    flight = []

    def send_ffn_grads(dw_up_t, dw_down):
        lands = []
        for g in (dw_up_t, dw_down):
            r = g.shape[0] // N_DEV
            own = lax.dynamic_slice(g, (me * r, 0), (r, g.shape[1]))
            lands.append(lax.dynamic_update_slice(lax.empty((N_DEV, r, g.shape[1]), BF16), own[None], (me, 0, 0)))
        flight.extend(_exchange_start("scatter", [dw_up_t, dw_down], lands, "ffn_grads_start"))
        return flight[4]

    (loss_loc, grad_x, dw_in_t, dw_out_t, _, _, dw_blk, dconv_w, dconv_b,
     sums_in, sums_mix, sums_ffn, sums_pool) = _sequence_step(
        xs, target, rope, (sh_m + w_token[0:1, 0:1], sc_m, gt_m, sh_f, sc_f, gt_f),
        (g_pre_mix, g_post_mix, g_pre_ffn, g_post_ffn),
        w_in_t, w_out_t, fetch_ffn, send_ffn_grads, w_blk_b, b_pool_r, pool_scale_r, conv_w_all, conv_b)

    parts_ffn = _exchange_wait("scatter", *flight[:4], dw_in_t, "ffn_grads_wait")
    dw_in_t, dw_out_t, *parts_ffn = lax.optimization_barrier((dw_in_t, dw_out_t, *parts_ffn))
    parts_mix = _scatter_grads([dw_in_t, dw_out_t])
    g_w_in = _sum_slabs(parts_mix[0], "sum_w_in", 64).T
    g_w_out = _sum_slabs(parts_mix[1], "sum_w_out", 128).T
    g_w_up = _sum_slabs(parts_ffn[0], "sum_w_up", 64).T
    g_w_down = _sum_slabs(parts_ffn[1], "sum_w_down", 32)

    dmod = jnp.concatenate([sums_in[0:1], sums_in[1:2], sums_mix[3:4], sums_mix[0:1], sums_mix[1:2], sums_ffn[0:1]], axis=1)
    dw_pool = jnp.stack([dw_blk[gi * HEAD_DIM:(gi + 1) * HEAD_DIM, gi * HEAD_DIM:(gi + 1) * HEAD_DIM] for gi in range(4)])
    pieces = [(dmod, 48), (sums_in[2:3], 8), (sums_mix[4:5], 8), (sums_mix[2:3], 8), (sums_ffn[1:2], 8),
              (dw_pool, 128), (sums_pool[0:1], 8), (sums_pool[1:2], 8), (dconv_b, 24),
              (dconv_w[0:1], 24), (dconv_w[1:2], 24), (dconv_w[2:3], 24)]
    packed = jnp.concatenate([_as_rows(a, r) for a, r in pieces], axis=0)
    gathered, total = _allreduce_small(packed, "allreduce_small")
    n_rep = 248
    rep_w = [b_ada, g_pre_mix, g_post_mix, g_pre_ffn, g_post_ffn, w_pool, b_pool, pool_scale, conv_b]
    rep_m = [m_b_ada, m_g_pre_mix, m_g_post_mix, m_g_pre_ffn, m_g_post_ffn, m_w_pool, m_b_pool, m_pool_scale, m_conv_b]
    rep_v = [v_b_ada, v_g_pre_mix, v_g_post_mix, v_g_pre_ffn, v_g_post_ffn, v_w_pool, v_b_pool, v_pool_scale, v_conv_b]
    rep_rows = [r for _, r in pieces[:9]]
    pack_rep = lambda arrs: jnp.concatenate([_as_rows(a, r) for a, r in zip(arrs, rep_rows)], axis=0)
    rep_g = total[:n_rep]
    rep_d, rep_nm, rep_nv = _adam(pack_rep(rep_w), rep_g, pack_rep(rep_m), pack_rep(rep_v), "adam_small", n_rep)

    def unpack(p):
        out, row = [], 0
        for a, r in zip(rep_w, rep_rows):
            out.append(p[row:row + r].reshape(-1)[:a.size].reshape(a.shape))
            row += r
        return out

    g_rep, d_rep, nm_rep, nv_rep = unpack(rep_g), unpack(rep_d), unpack(rep_nm), unpack(rep_nv)

    fcol = d_ff // N_DEV
    g_cw_full = jnp.concatenate([total[n_rep + 24 * k:n_rep + 24 * k + cw_rows].reshape(1, d_ff) for k in range(3)], axis=0)
    g_cw = lax.dynamic_slice(g_cw_full, (0, me * fcol), (3, fcol))
    d_cw, nm_cw, nv_cw = _adam(conv_w[0], g_cw, m_conv_w[0], v_conv_w[0], "adam_conv_w", 3)

    dmod_all = gathered[:, :48].reshape(N_DEV, 6 * d)
    dmod_cols = lax.dynamic_slice(dmod_all, (0, me * ncol), (N_DEV, ncol))
    g_ada, d_ada, nm_ada, nv_ada = _ada_grad_adam(c_all, dmod_cols, w_ada[0], m_w_ada[0], v_w_ada[0], 256)

    big = {}
    for nm, w, g, m, v, tr in (("w_in", w_in, g_w_in, m_w_in, v_w_in, 256), ("w_out", w_out, g_w_out, m_w_out, v_w_out, 256),
                               ("w_up", w_up, g_w_up, m_w_up, v_w_up, 256), ("w_down", w_down, g_w_down, m_w_down, v_w_down, 88)):
        big[nm] = (g,) + tuple(_adam(w[0], g, m[0], v[0], "adam_" + nm, tr))

    loss = lax.psum(loss_loc[0, 0], ("x", "y", "c"))

    def group(k):
        rep = (g_rep, d_rep, nm_rep, nv_rep)[k]
        ada = (g_ada, d_ada, nm_ada, nv_ada)[k][None]
        cw = (g_cw, d_cw, nm_cw, nv_cw)[k][None]
        return [ada, rep[0], rep[1], rep[2], rep[3], rep[4], big["w_in"][k][None], rep[5], rep[6], rep[7],
                big["w_out"][k][None], big["w_up"][k][None], cw, rep[8], big["w_down"][k][None]]

    return (loss, grad_x[None], *group(0), *group(1), *group(2), *group(3))
```

```python
import functools
import math

import jax
import jax.numpy as jnp
from jax import lax
from jax.experimental import pallas as pl
from jax.experimental.pallas import tpu as pltpu

F32 = jnp.float32
BF16 = jnp.bfloat16
MESH = pl.DeviceIdType.MESH

N_DEV = 8
HEAD_DIM = 64
ROT_HALF = 8
ROPE_THETA = 500000.0
POOL_WINDOWS = (2, 4, 8, 16)
DILATIONS = (1, 4, 16)
BLOCK = 128
NORM_EPS = 1e-6
HALO = 16
MASKED = -1e30
ATTN_FWD_UNROLL = 4
ATTN_BWD_UNROLL = 2

ADAM_LR = 0.001
ADAM_B1 = 0.9
ADAM_B2 = 0.999
ADAM_EPS = 1e-08
ADAM_WD = 0.01
ADAM_STEP = 10

V7X_VMEM_LIMIT = 56 * 1024 * 1024
LANES = 128

NT = (((1,), (1,)), ((), ()))
NN = (((1,), (0,)), ((), ()))
TN = (((0,), (0,)), ((), ()))


def _dot(a, b, dims):
    return lax.dot_general(a, b, dims, preferred_element_type=F32)


def _params(sem=None, vmem=V7X_VMEM_LIMIT):
    if sem is None:
        return pltpu.CompilerParams(vmem_limit_bytes=vmem)
    return pltpu.CompilerParams(dimension_semantics=sem, vmem_limit_bytes=vmem)


def _rstd(v):
    return lax.rsqrt(jnp.mean(v * v, axis=-1, keepdims=True) + NORM_EPS)


def _norm_bwd(dn, n, rstd):
    return rstd * (dn - n * jnp.mean(dn * n, axis=-1, keepdims=True))


def _rope_fwd(p, rope_ref):
    return p * rope_ref[0] + pltpu.roll(p, LANES - ROT_HALF, 1) * rope_ref[1] + pltpu.roll(p, ROT_HALF, 1) * rope_ref[2]


def _rope_bwd(dp, rope_ref):
    return dp * rope_ref[0] + pltpu.roll(dp * rope_ref[1], ROT_HALF, 1) + pltpu.roll(dp * rope_ref[2], LANES - ROT_HALF, 1)


def _gelu_parts(v):
    k = math.sqrt(2.0 / math.pi)
    t = jnp.tanh(k * (v + 0.044715 * v * v * v))
    g = 0.5 * v * (1.0 + t)
    dg = 0.5 * (1.0 + t) + 0.5 * v * (1.0 - t * t) * k * (1.0 + 3.0 * 0.044715 * v * v)
    return g, dg


def _halo_before(i, tile):
    return jnp.maximum(i * (tile // HALO) - 1, 0)


def _premix_inproj(x, sh, sc, g, w_in_t, rope, tm):
    s_len, d = x.shape
    n_proj = w_in_t.shape[0]
    n_slab = (n_proj - 256) // LANES

    def body(x_ref, sh_ref, sc_ref, g_ref, w_ref, rope_ref, h_ref, up_ref, qkv_ref):
        xv = x_ref[...]
        h = (xv * _rstd(xv) * g_ref[...]) * (1.0 + sc_ref[...]) + sh_ref[...]
        hb = h.astype(BF16)
        h_ref[...] = hb
        up_ref[...] = _dot(hb, w_ref[0:256, :], NT)
        for pair in range(n_slab // 2):
            p = _dot(hb, w_ref[256 + 256 * pair:512 + 256 * pair, :], NT)
            for half in range(2):
                ph = p[:, half * LANES:(half + 1) * LANES]
                if pair < 6:
                    ph = _rope_fwd(ph, rope_ref)
                if pair < 3:
                    ph = ph * (HEAD_DIM ** -0.5)
                qkv_ref[2 * pair + half] = ph

    vec = pl.BlockSpec((1, d), lambda i: (0, 0))
    return pl.pallas_call(
        body, name="premix_inproj", grid=(s_len // tm,),
        in_specs=[pl.BlockSpec((tm, d), lambda i: (i, 0)), vec, vec, vec,
                  pl.BlockSpec((n_proj, d), lambda i: (0, 0)),
                  pl.BlockSpec((3, tm, LANES), lambda i: (0, i, 0))],
        out_specs=[pl.BlockSpec((tm, d), lambda i: (i, 0)),
                   pl.BlockSpec((tm, 256), lambda i: (i, 0)),
                   pl.BlockSpec((n_slab, tm, LANES), lambda i: (0, i, 0))],
        out_shape=[jax.ShapeDtypeStruct((s_len, d), BF16),
                   jax.ShapeDtypeStruct((s_len, 256), F32),
                   jax.ShapeDtypeStruct((n_slab, s_len, LANES), F32)],
        compiler_params=_params(("arbitrary",)),
    )(x, sh, sc, g, w_in_t, rope)


def _block_rows(n, r, dil):
    start = n * (BLOCK * dil) + r
    if dil == 1:
        return pl.ds(pl.multiple_of(start, BLOCK), BLOCK)
    return pl.ds(start, BLOCK, stride=dil)


def _band_mask(n):
    ri = lax.broadcasted_iota(jnp.int32, (BLOCK, 2 * BLOCK), 0)
    cj = lax.broadcasted_iota(jnp.int32, (BLOCK, 2 * BLOCK), 1)
    cur = (cj >= BLOCK) & (cj - BLOCK <= ri)
    prev = (cj < BLOCK) & (cj >= ri) & (n > 0)
    return cur | prev


def _attn_fwd(qkv, group, dil):
    s_len = qkv.shape[1]
    nb = s_len // (BLOCK * dil)

    def body(q_ref, k_ref, v_ref, o_ref, lse_ref):
        lane = lax.broadcasted_iota(jnp.int32, (BLOCK, LANES), 1)
        first = lane < HEAD_DIM

        def block(t, carry):
            r, n = t // nb, t % nb
            cur = _block_rows(n, r, dil)
            prev = _block_rows(jnp.maximum(n - 1, 0), r, dil)
            q = q_ref[0, cur, :]
            kcat = jnp.concatenate([k_ref[0, prev, :], k_ref[0, cur, :]], axis=0).astype(BF16)
            vcat = jnp.concatenate([v_ref[0, prev, :], v_ref[0, cur, :]], axis=0).astype(BF16)
            valid = _band_mask(n)
            q2 = jnp.concatenate([jnp.where(first, q, 0.0), jnp.where(first, 0.0, q)], axis=0).astype(BF16)
            s = jnp.where(jnp.concatenate([valid, valid], axis=0), _dot(q2, kcat, NT), MASKED)
            m = jnp.max(s, axis=-1, keepdims=True)
            p = jnp.exp(s - m)
            den = jnp.sum(p, axis=-1, keepdims=True)
            o2 = _dot(p.astype(BF16), vcat, NN) / den
            lse2 = m + jnp.log(den)
            o_ref[0, cur, :] = jnp.where(first, o2[:BLOCK], o2[BLOCK:])
            lse_ref[0, cur, :] = jnp.where(first, lse2[:BLOCK], lse2[BLOCK:])
            return carry

        lax.fori_loop(0, nb * dil, block, 0, unroll=ATTN_FWD_UNROLL)

    def slab(base):
        return pl.BlockSpec((1, s_len, LANES), lambda s: (base + 2 * group + s, 0, 0))

    out = pl.BlockSpec((1, s_len, LANES), lambda s: (s, 0, 0))
    shape = jax.ShapeDtypeStruct((2, s_len, LANES), F32)
    return pl.pallas_call(
        body, name=f"attn_fwd_d{dil}", grid=(2,),
        in_specs=[slab(0), slab(6), slab(12)], out_specs=[out, out], out_shape=[shape, shape],
        compiler_params=_params(("arbitrary",)),
    )(qkv, qkv, qkv)


def _pool_mixed(u, halo, i, tm):
    ue = jnp.concatenate([halo, u], axis=0)
    s2 = ue + pltpu.roll(ue, 1, 0)
    s4 = s2 + pltpu.roll(s2, 2, 0)
    s8 = s4 + pltpu.roll(s4, 4, 0)
    s16 = s8 + pltpu.roll(s8, 8, 0)
    grp = lax.broadcasted_iota(jnp.int32, (tm, 256), 1) // HEAD_DIM
    pick = lambda a, b, c, e: jnp.where(grp == 0, a, jnp.where(grp == 1, b, jnp.where(grp == 2, c, e)))
    win_sum = pick(s2[HALO:], s4[HALO:], s8[HALO:], s16[HALO:])
    pos = (i * tm + lax.broadcasted_iota(jnp.int32, (tm, 256), 0)).astype(F32)
    count = jnp.minimum(pos + 1.0, pick(*[float(w) for w in POOL_WINDOWS]))
    return win_sum / count - u, count


def _mix_out(x, u_pool, o_g, lse_g, w_blk, b_pool, pool_scale, w_out_t, gt_m, g_post_mix, g_pre_ffn, sc_f, sh_f, tm):
    s_len, d = x.shape

    def body(x_ref, u_ref, uh_ref, o0, o1, o2, l0, l1, l2, wb_ref, bp_ref, ps_ref, wo_ref,
             gt_ref, g1_ref, g2_ref, sc_ref, sh_ref,
             x1_ref, y1_ref, h2_ref, cat_ref, attn_ref, lall_ref):
        i = pl.program_id(0)
        u = u_ref[...]
        halo = uh_ref[...] * (i > 0).astype(F32)
        mixed, _ = _pool_mixed(u, halo, i, tm)
        y = _dot(mixed.astype(BF16), wb_ref[...], NN) + bp_ref[...]
        pool = y * ps_ref[...]
        attn = []
        for s in range(2):
            la, lb, lc = l0[s], l1[s], l2[s]
            mx = jnp.maximum(jnp.maximum(la, lb), lc)
            ea, eb, ec = jnp.exp(la - mx), jnp.exp(lb - mx), jnp.exp(lc - mx)
            den = ea + eb + ec
            lall_ref[s] = mx + jnp.log(den)
            attn.append((ea / den) * o0[s] + (eb / den) * o1[s] + (ec / den) * o2[s])
        attn = jnp.concatenate(attn, axis=1)
        attn_ref[...] = attn
        cat = jnp.concatenate([pool, attn], axis=1).astype(BF16)
        cat_ref[...] = cat
        y1 = _dot(cat, wo_ref[...], NT)
        y1_ref[...] = y1
        x1 = x_ref[...] + gt_ref[...] * (y1 * _rstd(y1) * g1_ref[...])
        x1_ref[...] = x1
        h2 = (x1 * _rstd(x1) * g2_ref[...]) * (1.0 + sc_ref[...]) + sh_ref[...]
        h2_ref[...] = h2.astype(BF16)

    tile = lambda w: pl.BlockSpec((tm, w), lambda i: (i, 0))
    slab = pl.BlockSpec((2, tm, LANES), lambda i: (0, i, 0))
    const = lambda a: pl.BlockSpec(a.shape, lambda i: (0,) * a.ndim)
    return pl.pallas_call(
        body, name="mix_out", grid=(s_len // tm,),
        in_specs=[tile(d), tile(256), pl.BlockSpec((HALO, 256), lambda i: (_halo_before(i, tm), 0)),
                  slab, slab, slab, slab, slab, slab,
                  const(w_blk), const(b_pool), const(pool_scale), const(w_out_t),
                  const(gt_m), const(g_post_mix), const(g_pre_ffn), const(sc_f), const(sh_f)],
        out_specs=[tile(d), tile(d), tile(d), tile(512), tile(256), slab],
        out_shape=[jax.ShapeDtypeStruct((s_len, d), F32), jax.ShapeDtypeStruct((s_len, d), F32),
                   jax.ShapeDtypeStruct((s_len, d), BF16), jax.ShapeDtypeStruct((s_len, 512), BF16),
                   jax.ShapeDtypeStruct((s_len, 256), F32), jax.ShapeDtypeStruct((2, s_len, LANES), F32)],
        compiler_params=_params(("arbitrary",)),
    )(x, u_pool, u_pool, *o_g, *lse_g, w_blk, b_pool, pool_scale, w_out_t, gt_m, g_post_mix, g_pre_ffn, sc_f, sh_f)


def _conv_gate(gate_ext, cw_ref, cb_ref):
    gc = gate_ext * cw_ref[2:3, :] + pltpu.roll(gate_ext, 1, 0) * cw_ref[1:2, :] + pltpu.roll(gate_ext, 2, 0) * cw_ref[0:1, :]
    return gc[HALO:] + cb_ref[...]


def _ffn_fwd_loss(h2, x1, target, w_up_t, w_down, conv_w, conv_b, gt_f, g_post_ffn, tm, tf):
    s_len, d = x1.shape
    d_ff = w_down.shape[0]
    n_f = d_ff // tf

    def body(h_ref, hh_ref, x1_ref, tgt_ref, wg_ref, wv_ref, wd_ref, cw_ref, cb_ref, gt_ref, g_ref,
             gate_ref, val_ref, dy2_ref, dout_ref, sums_ref, loss_ref, acc_ref):
        i, j = pl.program_id(0), pl.program_id(1)

        @pl.when((i == 0) & (j == 0))
        def _():
            sums_ref[...] = jnp.zeros_like(sums_ref)
            loss_ref[...] = jnp.zeros_like(loss_ref)

        h_ext = jnp.concatenate([hh_ref[...], h_ref[...]], axis=0)
        gate_ext = _dot(h_ext, wg_ref[...], NT)
        row = lax.broadcasted_iota(jnp.int32, gate_ext.shape, 0)
        gate_ext = jnp.where((row < HALO) & (i == 0), 0.0, gate_ext)
        val = _dot(h_ref[...], wv_ref[...], NT)
        act, _ = _gelu_parts(_conv_gate(gate_ext, cw_ref, cb_ref))
        gate_ref[...] = gate_ext[HALO:].astype(BF16)
        val_ref[...] = val.astype(BF16)
        part = _dot((act * val).astype(BF16), wd_ref[...], NN)

        @pl.when(j == 0)
        def _():
            acc_ref[...] = part

        @pl.when(j > 0)
        def _():
            acc_ref[...] += part

        @pl.when(j == n_f - 1)
        def _():
            y2 = acc_ref[...]
            rstd = _rstd(y2)
            n = y2 * rstd
            rn = n * g_ref[...]
            err = x1_ref[...] + gt_ref[...] * rn - tgt_ref[...]
            loss_ref[...] += 0.5 * jnp.sum(jnp.mean(err * err, axis=-1, keepdims=True), axis=0, keepdims=True)
            dout = err * (1.0 / d)
            dout_ref[...] = dout
            drn = dout * gt_ref[...]
            sums_ref[0:1, :] += jnp.sum(dout * rn, axis=0, keepdims=True)
            sums_ref[1:2, :] += jnp.sum(drn * n, axis=0, keepdims=True)
            dy2_ref[...] = _norm_bwd(drn * g_ref[...], n, rstd).astype(BF16)

    tok = lambda w: pl.BlockSpec((tm, w), lambda i, j: (i, 0))
    tokf = pl.BlockSpec((tm, tf), lambda i, j: (i, j))
    vec = pl.BlockSpec((1, d), lambda i, j: (0, 0))
    return pl.pallas_call(
        body, name="ffn_fwd_loss", grid=(s_len // tm, n_f),
        in_specs=[tok(d), pl.BlockSpec((HALO, d), lambda i, j: (_halo_before(i, tm), 0)), tok(d), tok(d),
                  pl.BlockSpec((tf, d), lambda i, j: (j, 0)), pl.BlockSpec((tf, d), lambda i, j: (j + n_f, 0)),
                  pl.BlockSpec((tf, d), lambda i, j: (j, 0)),
                  pl.BlockSpec((3, tf), lambda i, j: (0, j)), pl.BlockSpec((1, tf), lambda i, j: (0, j)), vec, vec],
        out_specs=[tokf, tokf, tok(d), tok(d), pl.BlockSpec((8, d), lambda i, j: (0, 0)),
                   pl.BlockSpec((8, LANES), lambda i, j: (0, 0))],
        out_shape=[jax.ShapeDtypeStruct((s_len, d_ff), BF16), jax.ShapeDtypeStruct((s_len, d_ff), BF16),
                   jax.ShapeDtypeStruct((s_len, d), BF16), jax.ShapeDtypeStruct((s_len, d), F32),
                   jax.ShapeDtypeStruct((8, d), F32), jax.ShapeDtypeStruct((8, LANES), F32)],
        scratch_shapes=[pltpu.VMEM((tm, d), F32)],
        compiler_params=_params(("arbitrary", "arbitrary")),
    )(h2, h2, x1, target, w_up_t, w_up_t, w_down, conv_w, conv_b, gt_f, g_post_ffn)


def _ffn_bwd_act(dy2, gate, val, w_down, conv_w, conv_b, tm, tf):
    s_len, d = dy2.shape
    d_ff = w_down.shape[0]
    n_t = s_len // tm

    def body(dy_ref, g_ref, gh_ref, v_ref, wd_ref, cw_ref, cb_ref,
             dgc_ref, dval_ref, dwd_ref, dcw_ref, dcb_ref, acc_ref):
        i = pl.program_id(1)
        gate_ext = jnp.concatenate([gh_ref[...], g_ref[...]], axis=0).astype(F32)
        row = lax.broadcasted_iota(jnp.int32, gate_ext.shape, 0)
        gate_ext = jnp.where((row < HALO) & (i == 0), 0.0, gate_ext)
        act, dact = _gelu_parts(_conv_gate(gate_ext, cw_ref, cb_ref))
        val = v_ref[...].astype(F32)
        da = _dot(dy_ref[...], wd_ref[...], NT)
        dgc = da * val * dact
        dgc_ref[...] = dgc.astype(BF16)
        dval_ref[...] = (da * act).astype(BF16)
        dwd = _dot((act * val).astype(BF16), dy_ref[...], TN)
        taps = jnp.concatenate(
            [jnp.sum(dgc * pltpu.roll(gate_ext, 2 - k, 0)[HALO:], axis=0, keepdims=True) if k < 2
             else jnp.sum(dgc * gate_ext[HALO:], axis=0, keepdims=True) for k in range(3)], axis=0)
        bias = jnp.sum(dgc, axis=0, keepdims=True)

        @pl.when(i == 0)
        def _():
            acc_ref[...] = dwd
            dcw_ref[...] = taps
            dcb_ref[...] = bias

        @pl.when(i > 0)
        def _():
            acc_ref[...] += dwd
            dcw_ref[...] += taps
            dcb_ref[...] += bias

        @pl.when(i == n_t - 1)
        def _():
            dwd_ref[...] = acc_ref[...].astype(BF16)

    tokf = pl.BlockSpec((tm, tf), lambda j, i: (i, j))
    return pl.pallas_call(
        body, name="ffn_bwd_act", grid=(d_ff // tf, n_t),
        in_specs=[pl.BlockSpec((tm, d), lambda j, i: (i, 0)), tokf,
                  pl.BlockSpec((HALO, tf), lambda j, i: (_halo_before(i, tm), j)), tokf,
                  pl.BlockSpec((tf, d), lambda j, i: (j, 0)),
                  pl.BlockSpec((3, tf), lambda j, i: (0, j)), pl.BlockSpec((1, tf), lambda j, i: (0, j))],
        out_specs=[tokf, tokf, pl.BlockSpec((tf, d), lambda j, i: (j, 0)),
                   pl.BlockSpec((3, tf), lambda j, i: (0, j)), pl.BlockSpec((1, tf), lambda j, i: (0, j))],
        out_shape=[jax.ShapeDtypeStruct((s_len, d_ff), BF16), jax.ShapeDtypeStruct((s_len, d_ff), BF16),
                   jax.ShapeDtypeStruct((d_ff, d), BF16), jax.ShapeDtypeStruct((3, d_ff), F32),
                   jax.ShapeDtypeStruct((1, d_ff), F32)],
        scratch_shapes=[pltpu.VMEM((tf, d), F32)],
        compiler_params=_params(("arbitrary", "arbitrary")),
    )(dy2, gate, gate, val, w_down, conv_w, conv_b)


def _ffn_bwd_up(dgc, dval, w_up_t, conv_w, tm):
    s_len, d_ff = dgc.shape
    d = w_up_t.shape[1]
    n_t = s_len // tm

    def body(dg_ref, dgn_ref, dv_ref, cw_ref, w_ref, dup_ref, dh_ref):
        i = pl.program_id(0)
        nxt = dgn_ref[...].astype(F32) * (i < n_t - 1).astype(F32)
        ext = jnp.concatenate([dg_ref[...].astype(F32), nxt], axis=0)
        rows = tm + HALO
        dgate = (ext * cw_ref[2:3, :] + pltpu.roll(ext, rows - 1, 0) * cw_ref[1:2, :]
                 + pltpu.roll(ext, rows - 2, 0) * cw_ref[0:1, :])[:tm]
        dup = jnp.concatenate([dgate.astype(BF16), dv_ref[...]], axis=1)
        dup_ref[...] = dup
        dh_ref[...] = _dot(dup, w_ref[...], NN)

    tokf = pl.BlockSpec((tm, d_ff), lambda i: (i, 0))
    return pl.pallas_call(
        body, name="ffn_bwd_up", grid=(n_t,),
        in_specs=[tokf, pl.BlockSpec((HALO, d_ff), lambda i: (jnp.minimum((i + 1) * (tm // HALO), s_len // HALO - 1), 0)),
                  tokf, pl.BlockSpec((3, d_ff), lambda i: (0, 0)), pl.BlockSpec((2 * d_ff, d), lambda i: (0, 0))],
        out_specs=[pl.BlockSpec((tm, 2 * d_ff), lambda i: (i, 0)), pl.BlockSpec((tm, d), lambda i: (i, 0))],
        out_shape=[jax.ShapeDtypeStruct((s_len, 2 * d_ff), BF16), jax.ShapeDtypeStruct((s_len, d), F32)],
        compiler_params=_params(("arbitrary",)),
    )(dgc, dgc, dval, conv_w, w_up_t)


def _mix_bwd(dh2, dout, x1, y1, cat, attn, w_out_t, sc_f, g_pre_ffn, gt_m, g_post_mix, tm):
    s_len, d = x1.shape
    n_t = s_len // tm

    def body(dh_ref, do_ref, x1_ref, y1_ref, cat_ref, at_ref, wo_ref, sc_ref, g2_ref, gt_ref, g1_ref,
             dx1_ref, dpool_ref, dattn_ref, delta_ref, dwo_ref, sums_ref, acc_ref):
        i = pl.program_id(0)
        dh = dh_ref[...]
        x1 = x1_ref[...]
        r2 = _rstd(x1)
        n2 = x1 * r2
        ng = n2 * g2_ref[...]
        dng = dh * (1.0 + sc_ref[...])
        dx1 = do_ref[...] + _norm_bwd(dng * g2_ref[...], n2, r2)
        dx1_ref[...] = dx1
        y1 = y1_ref[...]
        r1 = _rstd(y1)
        n1 = y1 * r1
        drn = dx1 * gt_ref[...]
        dy1 = _norm_bwd(drn * g1_ref[...], n1, r1).astype(BF16)
        dcat = _dot(dy1, wo_ref[...], NN)
        dpool_ref[...] = dcat[:, 0:256]
        lane = lax.broadcasted_iota(jnp.int32, (tm, LANES), 1)
        first = lane < HEAD_DIM
        for s in range(2):
            da = dcat[:, 256 + s * LANES:256 + (s + 1) * LANES]
            dattn_ref[s] = da
            prod = da * at_ref[:, s * LANES:(s + 1) * LANES]
            tot = jnp.sum(prod, axis=-1, keepdims=True)
            lo = jnp.sum(jnp.where(first, prod, 0.0), axis=-1, keepdims=True)
            delta_ref[s] = jnp.where(first, lo, tot - lo)
        dwo = _dot(dy1, cat_ref[...], TN)
        sums = jnp.concatenate(
            [jnp.sum(dh, axis=0, keepdims=True), jnp.sum(dh * ng, axis=0, keepdims=True),
             jnp.sum(dng * n2, axis=0, keepdims=True), jnp.sum(dx1 * (n1 * g1_ref[...]), axis=0, keepdims=True),
             jnp.sum(drn * n1, axis=0, keepdims=True), jnp.zeros((3, d), F32)], axis=0)

        @pl.when(i == 0)
        def _():
            acc_ref[...] = dwo
            sums_ref[...] = sums

        @pl.when(i > 0)
        def _():
            acc_ref[...] += dwo
            sums_ref[...] += sums

        @pl.when(i == n_t - 1)
        def _():
            dwo_ref[...] = acc_ref[...].astype(BF16)

    tile = lambda w: pl.BlockSpec((tm, w), lambda i: (i, 0))
    slab = pl.BlockSpec((2, tm, LANES), lambda i: (0, i, 0))
    vec = pl.BlockSpec((1, d), lambda i: (0, 0))
    return pl.pallas_call(
        body, name="mix_bwd", grid=(n_t,),
        in_specs=[tile(d), tile(d), tile(d), tile(d), tile(512), tile(256),
                  pl.BlockSpec((d, 512), lambda i: (0, 0)), vec, vec, vec, vec],
        out_specs=[tile(d), tile(256), slab, slab, pl.BlockSpec((d, 512), lambda i: (0, 0)),
                   pl.BlockSpec((8, d), lambda i: (0, 0))],
        out_shape=[jax.ShapeDtypeStruct((s_len, d), F32), jax.ShapeDtypeStruct((s_len, 256), F32),
                   jax.ShapeDtypeStruct((2, s_len, LANES), F32), jax.ShapeDtypeStruct((2, s_len, LANES), F32),
                   jax.ShapeDtypeStruct((d, 512), BF16), jax.ShapeDtypeStruct((8, d), F32)],
        scratch_shapes=[pltpu.VMEM((d, 512), F32)],
        compiler_params=_params(("arbitrary",)),
    )(dh2, dout, x1, y1, cat, attn, w_out_t, sc_f, g_pre_ffn, gt_m, g_post_mix)


def _pool_bwd(dpool, u_pool, w_blk, b_pool, pool_scale, tm):
    s_len = dpool.shape[0]
    n_t = s_len // tm

    def body(dp_ref, dpn_ref, u_ref, uh_ref, wb_ref, bp_ref, ps_ref, du_ref, dwb_ref, sums_ref):
        i = pl.program_id(0)
        u = u_ref[...]
        mixed, _ = _pool_mixed(u, uh_ref[...] * (i > 0).astype(F32), i, tm)
        mixed_b = mixed.astype(BF16)
        y = _dot(mixed_b, wb_ref[...], NN) + bp_ref[...]
        dp = dp_ref[...]
        dy = dp * ps_ref[...]
        dwb = _dot(mixed_b, dy.astype(BF16), TN)
        sums = jnp.concatenate([jnp.sum(dy, axis=0, keepdims=True), jnp.sum(dp * y, axis=0, keepdims=True),
                                jnp.zeros((6, 256), F32)], axis=0)
        dp_ext = jnp.concatenate([dp, dpn_ref[...] * (i < n_t - 1).astype(F32)], axis=0)
        dmix = _dot((dp_ext * ps_ref[...]).astype(BF16), wb_ref[...], NT)
        rows = tm + HALO
        grp = lax.broadcasted_iota(jnp.int32, (rows, 256), 1) // HEAD_DIM
        pick = lambda a, b, c, e: jnp.where(grp == 0, a, jnp.where(grp == 1, b, jnp.where(grp == 2, c, e)))
        pos = (i * tm + lax.broadcasted_iota(jnp.int32, (rows, 256), 0)).astype(F32)
        z = dmix / jnp.minimum(pos + 1.0, pick(*[float(w) for w in POOL_WINDOWS]))
        f2 = z + pltpu.roll(z, rows - 1, 0)
        f4 = f2 + pltpu.roll(f2, rows - 2, 0)
        f8 = f4 + pltpu.roll(f4, rows - 4, 0)
        f16 = f8 + pltpu.roll(f8, rows - 8, 0)
        du_ref[...] = (pick(f2, f4, f8, f16) - dmix)[:tm]

        @pl.when(i == 0)
        def _():
            dwb_ref[...] = dwb
            sums_ref[...] = sums

        @pl.when(i > 0)
        def _():
            dwb_ref[...] += dwb
            sums_ref[...] += sums

    tile = pl.BlockSpec((tm, 256), lambda i: (i, 0))
    const = lambda a: pl.BlockSpec(a.shape, lambda i: (0,) * a.ndim)
    return pl.pallas_call(
        body, name="pool_bwd", grid=(n_t,),
        in_specs=[tile, pl.BlockSpec((HALO, 256), lambda i: (jnp.minimum((i + 1) * (tm // HALO), s_len // HALO - 1), 0)),
                  tile, pl.BlockSpec((HALO, 256), lambda i: (_halo_before(i, tm), 0)),
                  const(w_blk), const(b_pool), const(pool_scale)],
        out_specs=[tile, pl.BlockSpec((256, 256), lambda i: (0, 0)), pl.BlockSpec((8, 256), lambda i: (0, 0))],
        out_shape=[jax.ShapeDtypeStruct((s_len, 256), F32), jax.ShapeDtypeStruct((256, 256), F32),
                   jax.ShapeDtypeStruct((8, 256), F32)],
        compiler_params=_params(("arbitrary",)),
    )(dpool, dpool, u_pool, u_pool, w_blk, b_pool, pool_scale)


def _attn_bwd(qkv, dattn, lse_all, delta, group, dil):
    s_len = qkv.shape[1]
    nb = s_len // (BLOCK * dil)

    def body(q_ref, k_ref, v_ref, do_ref, l_ref, dl_ref, dq_ref, dk_ref, dv_ref):
        lane = lax.broadcasted_iota(jnp.int32, (BLOCK, LANES), 1)
        first = lane < HEAD_DIM

        def block(t, carry):
            dk_part, dv_part = carry
            r, n = t // nb, t % nb
            cur = _block_rows(n, r, dil)
            prev = _block_rows(jnp.maximum(n - 1, 0), r, dil)
            q = q_ref[0, cur, :]
            do = do_ref[0, cur, :]
            lse = l_ref[0, cur, :]
            dlt = dl_ref[0, cur, :]
            kcat = jnp.concatenate([k_ref[0, prev, :], k_ref[0, cur, :]], axis=0).astype(BF16)
            vcat = jnp.concatenate([v_ref[0, prev, :], v_ref[0, cur, :]], axis=0).astype(BF16)
            valid = _band_mask(n)
            stack = lambda a: jnp.concatenate([jnp.where(first, a, 0.0), jnp.where(first, 0.0, a)], axis=0)
            rows2 = lambda a: jnp.concatenate([a[:, 0:1], a[:, HEAD_DIM:HEAD_DIM + 1]], axis=0)
            q2, do2 = stack(q).astype(BF16), stack(do).astype(BF16)
            valid2 = jnp.concatenate([valid, valid], axis=0)
            p = jnp.where(valid2, jnp.exp(_dot(q2, kcat, NT) - rows2(lse)), 0.0)
            ds = (p * (_dot(do2, vcat, NT) - rows2(dlt))).astype(BF16)
            dq2 = _dot(ds, kcat, NN)
            dq_ref[0, cur, :] = jnp.where(first, dq2[:BLOCK], dq2[BLOCK:])
            dkc = _dot(ds, q2, TN)
            dvc = _dot(p.astype(BF16), do2, TN)
            dk_ref[0, prev, :] = dk_part + dkc[:BLOCK]
            dv_ref[0, prev, :] = dv_part + dvc[:BLOCK]
            dk_ref[0, cur, :] = dkc[BLOCK:]
            dv_ref[0, cur, :] = dvc[BLOCK:]
            return dkc[BLOCK:], dvc[BLOCK:]

        def blocks(tt, carry):
            for u in range(ATTN_BWD_UNROLL):
                carry = block(tt * ATTN_BWD_UNROLL + u, carry)
            return carry

        zero = jnp.zeros((BLOCK, LANES), F32)
        lax.fori_loop(0, nb * dil // ATTN_BWD_UNROLL, blocks, (zero, zero))

    def slab(base):
        return pl.BlockSpec((1, s_len, LANES), lambda s: (base + 2 * group + s, 0, 0))

    one = pl.BlockSpec((1, s_len, LANES), lambda s: (s, 0, 0))
    shape = jax.ShapeDtypeStruct((2, s_len, LANES), F32)
    return pl.pallas_call(
        body, name=f"attn_bwd_d{dil}", grid=(2,),
        in_specs=[slab(0), slab(6), slab(12), one, one, one],
        out_specs=[one, one, one], out_shape=[shape, shape, shape],
        compiler_params=_params(("arbitrary",)),
    )(qkv, qkv, qkv, dattn, lse_all, delta)


def _inproj_bwd(du, dqkv, rope, w_in_t, x, dx1, sc_m, g_pre_mix, tm):
    s_len, d = x.shape
    n_proj = w_in_t.shape[0]
    n_t = s_len // tm

    def body(du_ref, *refs):
        dref = refs[:9]
        rope_ref, w_ref, x_ref, dx1_ref, sc_ref, g_ref, dproj_ref, dx_ref, sums_ref = refs[9:]
        i = pl.program_id(0)
        cols = [du_ref[...].astype(BF16)]
        for kind in range(3):
            for grp in range(3):
                for s in range(2):
                    piece = dref[3 * grp + kind][s]
                    if kind < 2:
                        piece = _rope_bwd(piece, rope_ref)
                    if kind == 0:
                        piece = piece * (HEAD_DIM ** -0.5)
                    cols.append(piece.astype(BF16))
        dproj = jnp.concatenate(cols, axis=1)
        dproj_ref[...] = dproj
        dh = _dot(dproj, w_ref[...], NN)
        xv = x_ref[...]
        r = _rstd(xv)
        n = xv * r
        dng = dh * (1.0 + sc_ref[...])
        dx_ref[...] = dx1_ref[...] + _norm_bwd(dng * g_ref[...], n, r)
        sums = jnp.concatenate([jnp.sum(dh, axis=0, keepdims=True), jnp.sum(dh * (n * g_ref[...]), axis=0, keepdims=True),
                                jnp.sum(dng * n, axis=0, keepdims=True), jnp.zeros((5, d), F32)], axis=0)

        @pl.when(i == 0)
        def _():
            sums_ref[...] = sums

        @pl.when(i > 0)
        def _():
            sums_ref[...] += sums

    tile = lambda w: pl.BlockSpec((tm, w), lambda i: (i, 0))
    slab = pl.BlockSpec((2, tm, LANES), lambda i: (0, i, 0))
    vec = pl.BlockSpec((1, d), lambda i: (0, 0))
    return pl.pallas_call(
        body, name="inproj_bwd", grid=(n_t,),
        in_specs=[tile(256)] + [slab] * 9 + [pl.BlockSpec((3, tm, LANES), lambda i: (0, i, 0)),
                                             pl.BlockSpec((n_proj, d), lambda i: (0, 0)), tile(d), tile(d), vec, vec],
        out_specs=[tile(n_proj), tile(d), pl.BlockSpec((8, d), lambda i: (0, 0))],
        out_shape=[jax.ShapeDtypeStruct((s_len, n_proj), BF16), jax.ShapeDtypeStruct((s_len, d), F32),
                   jax.ShapeDtypeStruct((8, d), F32)],
        compiler_params=_params(("arbitrary",)),
    )(du, *dqkv, rope, w_in_t, x, dx1, sc_m, g_pre_mix)


def _wgrad(a, b, name, tk, tmm):
    s_len, m = a.shape
    n = b.shape[1]
    n_k = s_len // tk

    def body(a_ref, b_ref, o_ref, acc_ref):
        k = pl.program_id(1)
        part = _dot(a_ref[...], b_ref[...], TN)

        @pl.when(k == 0)
        def _():
            acc_ref[...] = part

        @pl.when(k > 0)
        def _():
            acc_ref[...] += part

        @pl.when(k == n_k - 1)
        def _():
            o_ref[...] = acc_ref[...].astype(BF16)

    return pl.pallas_call(
        body, name=name, grid=(m // tmm, n_k),
        in_specs=[pl.BlockSpec((tk, tmm), lambda j, k: (k, j)), pl.BlockSpec((tk, n), lambda j, k: (k, 0))],
        out_specs=pl.BlockSpec((tmm, n), lambda j, k: (j, 0)),
        out_shape=jax.ShapeDtypeStruct((m, n), BF16),
        scratch_shapes=[pltpu.VMEM((tmm, n), F32)],
        compiler_params=_params(("arbitrary", "arbitrary")),
    )(a, b)


def _place():
    return lax.axis_index("x"), lax.axis_index("y"), lax.axis_index("c")


def _peer(k):
    x, y, c = _place()
    bx, by, bc = (k >> 2) & 1, (k >> 1) & 1, k & 1
    return (x ^ bx if bx else x, y ^ by if by else y, c ^ bc if bc else c)


def _index(pos):
    return 4 * pos[0] + 2 * pos[1] + pos[2]


def _ada_exchange(c_rows, w_ada, b_ada_cols, taps):
    d = c_rows.shape[1]
    ncol = w_ada.shape[1]

    def body(c_ref, w_ref, b_ref, t_ref, call_ref, mod_ref, tall_ref, stage_ref, send_sems, recv_sems):
        me = _index(_place())
        call_ref[me] = c_ref[...]
        tall_ref[me] = t_ref[...]

        def gather(k):
            return pltpu.make_async_remote_copy(
                src_ref=c_ref, dst_ref=call_ref.at[me], send_sem=send_sems.at[0, k - 1], recv_sem=recv_sems.at[0, k - 1],
                device_id=_peer(k), device_id_type=MESH)

        def gather_taps(k):
            return pltpu.make_async_remote_copy(
                src_ref=t_ref, dst_ref=tall_ref.at[me], send_sem=send_sems.at[2, k - 1], recv_sem=recv_sems.at[2, k - 1],
                device_id=_peer(k), device_id_type=MESH)

        for k in range(1, N_DEV):
            gather(k).start()
        for k in range(1, N_DEV):
            gather_taps(k).start()
        for k in range(1, N_DEV):
            gather(k).wait_recv()
        cv = jnp.concatenate([call_ref[b, 0:1, :] for b in range(N_DEV)], axis=0)
        act = cv * jax.nn.sigmoid(cv)
        mod = lax.dot_general(act, w_ref[...], NN, preferred_element_type=F32,
                              precision=lax.Precision.HIGHEST) + b_ref[...]
        for b in range(N_DEV):
            stage_ref[b] = jnp.broadcast_to(mod[b:b + 1, :], (8, ncol))
        mod_ref[me] = stage_ref[me]

        def scatter(k):
            return pltpu.make_async_remote_copy(
                src_ref=stage_ref.at[_index(_peer(k))], dst_ref=mod_ref.at[me],
                send_sem=send_sems.at[1, k - 1], recv_sem=recv_sems.at[1, k - 1],
                device_id=_peer(k), device_id_type=MESH)

        for k in range(1, N_DEV):
            scatter(k).start()
        for k in range(1, N_DEV):
            scatter(k).wait_recv()
        for k in range(1, N_DEV):
            gather_taps(k).wait_recv()
        for k in range(1, N_DEV):
            gather(k).wait_send()
            scatter(k).wait_send()
            gather_taps(k).wait_send()

    vmem = pl.BlockSpec(memory_space=pltpu.VMEM)
    return pl.pallas_call(
        body, name="ada_exchange",
        in_specs=[vmem] * 4, out_specs=[vmem] * 3,
        out_shape=[jax.ShapeDtypeStruct((N_DEV, 8, d), F32), jax.ShapeDtypeStruct((N_DEV, 8, ncol), F32),
                   jax.ShapeDtypeStruct((N_DEV,) + taps.shape, F32)],
        scratch_shapes=[pltpu.VMEM((N_DEV, 8, ncol), F32), pltpu.SemaphoreType.DMA((3, N_DEV - 1)),
                        pltpu.SemaphoreType.DMA((3, N_DEV - 1))],
        compiler_params=_params(),
    )(c_rows, w_ada, b_ada_cols, taps)


def _gather_weights(shards):
    n_w = len(shards)

    def body(*refs):
        srcs, outs = refs[:n_w], refs[n_w:2 * n_w]
        send_sems, recv_sems, local_sems = refs[2 * n_w:]
        x, y, c = _place()
        me, sibling = (x, y, c), (x, y, 1 - c)
        chips = [(1 - x, y), (x, 1 - y), (1 - x, 1 - y)]

        def rows(w, pos):
            r = shards[w].shape[0]
            return outs[w].at[pl.ds(pl.multiple_of(_index(pos) * r, 16), r), :]

        def copy(k, w, block, to, own=False):
            return pltpu.make_async_remote_copy(
                src_ref=srcs[w] if own else rows(w, block), dst_ref=rows(w, block),
                send_sem=send_sems.at[k, w], recv_sem=recv_sems.at[k, w], device_id=to, device_id_type=MESH)

        mine = [pltpu.make_async_copy(srcs[w], rows(w, me), local_sems.at[w]) for w in range(n_w)]
        for cp in mine:
            cp.start()
        first = [copy(0, w, me, sibling, own=True) for w in range(n_w)]
        first += [copy(1 + j, w, me, (*chip, c), own=True) for j, chip in enumerate(chips) for w in range(n_w)]
        for cp in first:
            cp.start()
        passed = []
        for j, chip in enumerate(chips):
            for w in range(n_w):
                copy(1 + j, w, (*chip, c), me).wait_recv()
                fwd = copy(4 + j, w, (*chip, c), sibling)
                fwd.start()
                passed.append(fwd)
        for w in range(n_w):
            copy(0, w, sibling, me).wait_recv()
        for j, chip in enumerate(chips):
            for w in range(n_w):
                copy(4 + j, w, (*chip, 1 - c), me).wait_recv()
        for cp in first + passed:
            cp.wait_send()
        for cp in mine:
            cp.wait()

    hbm = pl.BlockSpec(memory_space=pltpu.HBM)
    return pl.pallas_call(
        body, name="gather_weights",
        in_specs=[hbm] * n_w, out_specs=[hbm] * n_w,
        out_shape=[jax.ShapeDtypeStruct((N_DEV * s.shape[0], s.shape[1]), s.dtype) for s in shards],
        scratch_shapes=[pltpu.SemaphoreType.DMA((N_DEV - 1, n_w)), pltpu.SemaphoreType.DMA((N_DEV - 1, n_w)),
                        pltpu.SemaphoreType.DMA((n_w,))],
        compiler_params=_params(),
    )(*shards)


def _scatter_grads(grads):
    n_w = len(grads)

    def body(*refs):
        srcs, outs = refs[:n_w], refs[n_w:2 * n_w]
        send_sems, recv_sems, local_sems = refs[2 * n_w:]
        me = _index(_place())

        def slab(w, dev):
            r = grads[w].shape[0] // N_DEV
            return srcs[w].at[pl.ds(pl.multiple_of(dev * r, 16), r), :]

        def copy(k, w):
            return pltpu.make_async_remote_copy(
                src_ref=slab(w, _index(_peer(k))), dst_ref=outs[w].at[me],
                send_sem=send_sems.at[k - 1, w], recv_sem=recv_sems.at[k - 1, w],
                device_id=_peer(k), device_id_type=MESH)

        mine = [pltpu.make_async_copy(slab(w, me), outs[w].at[me], local_sems.at[w]) for w in range(n_w)]
        for cp in mine:
            cp.start()
        sends = [copy(k, w) for k in range(1, N_DEV) for w in range(n_w)]
        for cp in sends:
            cp.start()
        for cp in sends:
            cp.wait_recv()
        for cp in sends:
            cp.wait_send()
        for cp in mine:
            cp.wait()

    hbm = pl.BlockSpec(memory_space=pltpu.HBM)
    return pl.pallas_call(
        body, name="scatter_grads",
        in_specs=[hbm] * n_w, out_specs=[hbm] * n_w,
        out_shape=[jax.ShapeDtypeStruct((N_DEV, g.shape[0] // N_DEV, g.shape[1]), g.dtype) for g in grads],
        scratch_shapes=[pltpu.SemaphoreType.DMA((N_DEV - 1, n_w)), pltpu.SemaphoreType.DMA((N_DEV - 1, n_w)),
                        pltpu.SemaphoreType.DMA((n_w,))],
        compiler_params=_params(),
    )(*grads)


def _peer_copies(mode, srcs, lands, send_sems, recv_sems):
    me = _index(_place())
    copies = []
    for k in range(1, N_DEV):
        peer = _peer(k)
        for w, (src, land) in enumerate(zip(srcs, lands)):
            if mode == "gather":
                r = src.shape[0]
                dst = land.at[pl.ds(pl.multiple_of(me * r, 16), r), :]
            else:
                r = src.shape[0] // N_DEV
                src = src.at[pl.ds(pl.multiple_of(_index(peer) * r, 16), r), :]
                dst = land.at[me]
            copies.append(pltpu.make_async_remote_copy(
                src_ref=src, dst_ref=dst, send_sem=send_sems.at[(k - 1) * len(srcs) + w],
                recv_sem=recv_sems.at[(k - 1) * len(srcs) + w],
                device_id=peer, device_id_type=MESH))
    return copies


def _exchange_start(mode, srcs, lands, name):
    n = len(srcs)

    def body(*refs):
        for cp in _peer_copies(mode, refs[:n], refs[n:2 * n], refs[2 * n], refs[2 * n + 1]):
            cp.start()
        refs[-1][...] = jnp.zeros_like(refs[-1])

    hbm, sem = pl.BlockSpec(memory_space=pltpu.HBM), pl.BlockSpec(memory_space=pltpu.SEMAPHORE)
    arrays = list(srcs) + list(lands)
    out = pl.pallas_call(
        body, name=name,
        out_shape=(pltpu.SemaphoreType.DMA(((N_DEV - 1) * n,)), pltpu.SemaphoreType.DMA(((N_DEV - 1) * n,)),
                   *[pltpu.HBM(a.shape, a.dtype) for a in arrays], jax.ShapeDtypeStruct((8, LANES), F32)),
        in_specs=[hbm] * (2 * n), out_specs=(sem, sem, *[hbm] * (2 * n), pl.BlockSpec(memory_space=pltpu.VMEM)),
        input_output_aliases={i: 2 + i for i in range(2 * n)},
        compiler_params=pltpu.CompilerParams(has_side_effects=pltpu.SideEffectType.DATAFLOW_SIDE_EFFECTING),
    )(*[pltpu.with_memory_space_constraint(a, pltpu.HBM) for a in arrays])
    return out[0], out[1], out[2:2 + n], out[2 + n:2 + 2 * n], out[-1]


def _exchange_wait(mode, send_sems, recv_sems, srcs, lands, after, name):
    n = len(srcs)

    def body(*refs):
        copies = _peer_copies(mode, refs[:n], refs[n:2 * n], refs[2 * n], refs[2 * n + 1])
        for cp in copies:
            cp.wait_send()
        for cp in copies:
            cp.wait_recv()

    hbm, sem = pl.BlockSpec(memory_space=pltpu.HBM), pl.BlockSpec(memory_space=pltpu.SEMAPHORE)
    arrays = list(srcs) + list(lands)
    out = pl.pallas_call(
        body, name=name, out_shape=tuple(pltpu.HBM(a.shape, a.dtype) for a in arrays),
        in_specs=[hbm] * (2 * n) + [sem, sem, pl.BlockSpec(memory_space=pl.ANY)], out_specs=tuple([hbm] * (2 * n)),
        input_output_aliases={i: i for i in range(2 * n)},
        compiler_params=pltpu.CompilerParams(has_side_effects=pltpu.SideEffectType.DATAFLOW_SIDE_EFFECTING),
    )(*arrays, send_sems, recv_sems, after)
    return out[n:]


def _allreduce_small(packed, name):
    rows = packed.shape[0]

    def body(p_ref, all_ref, tot_ref, send_sems, recv_sems):
        me = _index(_place())
        all_ref[me] = p_ref[...]

        def copy(k):
            return pltpu.make_async_remote_copy(
                src_ref=p_ref, dst_ref=all_ref.at[me], send_sem=send_sems.at[k - 1], recv_sem=recv_sems.at[k - 1],
                device_id=_peer(k), device_id_type=MESH)

        for k in range(1, N_DEV):
            copy(k).start()
        for k in range(1, N_DEV):
            copy(k).wait_recv()
        tot = all_ref[0]
        for dev in range(1, N_DEV):
            tot = tot + all_ref[dev]
        tot_ref[...] = tot
        for k in range(1, N_DEV):
            copy(k).wait_send()

    vmem = pl.BlockSpec(memory_space=pltpu.VMEM)
    return pl.pallas_call(
        body, name=name, in_specs=[vmem], out_specs=[vmem, vmem],
        out_shape=[jax.ShapeDtypeStruct((N_DEV, rows, LANES), F32), jax.ShapeDtypeStruct((rows, LANES), F32)],
        scratch_shapes=[pltpu.SemaphoreType.DMA((N_DEV - 1,)), pltpu.SemaphoreType.DMA((N_DEV - 1,))],
        compiler_params=_params(),
    )(packed)


def _sum_slabs(parts, name, tr):
    _, rows, cols = parts.shape

    def body(p_ref, o_ref):
        tot = p_ref[0].astype(F32)
        for dev in range(1, N_DEV):
            tot = tot + p_ref[dev].astype(F32)
        o_ref[...] = tot

    return pl.pallas_call(
        body, name=name, grid=(rows // tr,),
        in_specs=[pl.BlockSpec((N_DEV, tr, cols), lambda i: (0, i, 0))],
        out_specs=pl.BlockSpec((tr, cols), lambda i: (i, 0)),
        out_shape=jax.ShapeDtypeStruct((rows, cols), F32),
        compiler_params=_params(("arbitrary",)),
    )(parts)


def _adam_math(w, g, m, v):
    m = ADAM_B1 * m + (1.0 - ADAM_B1) * g
    v = ADAM_B2 * v + (1.0 - ADAM_B2) * (g * g)
    m_hat = m / (1.0 - ADAM_B1 ** ADAM_STEP)
    v_hat = v / (1.0 - ADAM_B2 ** ADAM_STEP)
    delta = -ADAM_LR * (m_hat / (jnp.sqrt(v_hat) + ADAM_EPS) + ADAM_WD * w)
    return delta, m, v


def _adam(w, g, m, v, name, tr):
    rows, cols = w.shape

    def body(w_ref, g_ref, m_ref, v_ref, d_ref, nm_ref, nv_ref):
        d_ref[...], nm_ref[...], nv_ref[...] = _adam_math(w_ref[...], g_ref[...], m_ref[...], v_ref[...])

    spec = pl.BlockSpec((tr, cols), lambda i: (i, 0))
    shape = jax.ShapeDtypeStruct((rows, cols), F32)
    return pl.pallas_call(
        body, name=name, grid=(rows // tr,), in_specs=[spec] * 4, out_specs=[spec] * 3,
        out_shape=[shape] * 3, compiler_params=_params(("arbitrary",)),
    )(w, g, m, v)


def _ada_grad_adam(c_all, dmod_cols, w, m, v, tr):
    rows, cols = w.shape

    def body(c_ref, dm_ref, w_ref, m_ref, v_ref, g_ref, d_ref, nm_ref, nv_ref):
        cv = c_ref[...]
        act = cv * jax.nn.sigmoid(cv)
        g = lax.dot_general(act, dm_ref[...], TN, preferred_element_type=F32, precision=lax.Precision.HIGHEST)
        g_ref[...] = g
        d_ref[...], nm_ref[...], nv_ref[...] = _adam_math(w_ref[...], g, m_ref[...], v_ref[...])

    spec = pl.BlockSpec((tr, cols), lambda i: (i, 0))
    shape = jax.ShapeDtypeStruct((rows, cols), F32)
    return pl.pallas_call(
        body, name="ada_grad_adam", grid=(rows // tr,),
        in_specs=[pl.BlockSpec((N_DEV, tr), lambda i: (0, i)), pl.BlockSpec((N_DEV, cols), lambda i: (0, 0)), spec, spec, spec],
        out_specs=[spec] * 4, out_shape=[shape] * 4, compiler_params=_params(("arbitrary",)),
    )(c_all, dmod_cols, w, m, v)


def _rope_tables(positions):
    s_len = positions.shape[0]
    inv_freq = ROPE_THETA ** (-jnp.arange(0, 2 * ROT_HALF, 2, dtype=F32) / (2 * ROT_HALF))
    ang = positions.astype(F32)[:, None] * inv_freq
    cos, sin = jnp.cos(ang), jnp.sin(ang)
    rest = HEAD_DIM - 2 * ROT_HALF
    zero = lambda n: jnp.zeros((s_len, n), F32)
    head = jnp.stack([jnp.concatenate([cos, cos, jnp.ones((s_len, rest), F32)], axis=1),
                      jnp.concatenate([-sin, zero(HEAD_DIM - ROT_HALF)], axis=1),
                      jnp.concatenate([zero(ROT_HALF), sin, zero(rest)], axis=1)])
    return jnp.tile(head, (1, 1, LANES // HEAD_DIM))


def _pad_rows(a, rows):
    return jnp.pad(a, ((0, rows - a.shape[0]), (0, 0)))


def _as_rows(a, rows):
    flat = a.reshape(-1)
    return jnp.pad(flat, (0, rows * LANES - flat.shape[0])).reshape(rows, LANES)


def _sequence_step(xs, target, rope, mods, gains, w_in_t, w_out_t, fetch_ffn, send_ffn_grads, w_blk_b, b_pool_r,
                   pool_scale_r, conv_w_all, conv_b):
    sh_m, sc_m, gt_m, sh_f, sc_f, gt_f = mods
    g_pre_mix, g_post_mix, g_pre_ffn, g_post_ffn = gains
    h1, u_pool, qkv = _premix_inproj(xs, sh_m, sc_m, g_pre_mix, w_in_t, rope, tm=512)
    o_g, lse_g = [], []
    for gi, dil in enumerate(DILATIONS):
        o, lse = _attn_fwd(qkv, gi, dil)
        o_g.append(o)
        lse_g.append(lse)
    x1, y1, h2, cat, attn, lse_all = _mix_out(xs, u_pool, o_g, lse_g, w_blk_b, b_pool_r, pool_scale_r, w_out_t,
                                              gt_m, g_post_mix, g_pre_ffn, sc_f, sh_f, tm=256)
    w_up_t, w_down_f = fetch_ffn(x1)
    gate, val, dy2, dout, sums_ffn, loss_loc = _ffn_fwd_loss(h2, x1, target, w_up_t, w_down_f, conv_w_all, conv_b,
                                                              gt_f, g_post_ffn, tm=256, tf=1408)

    dgc, dval, dw_down, dconv_w, dconv_b = _ffn_bwd_act(dy2, gate, val, w_down_f, conv_w_all, conv_b, tm=1024, tf=256)
    dup, dh2 = _ffn_bwd_up(dgc, dval, w_up_t, conv_w_all, tm=256)
    dw_up_t = _wgrad(dup, h2, "wgrad_up", tk=1024, tmm=1408)
    token = send_ffn_grads(dw_up_t, dw_down)
    if token is not None:
        sc_f = sc_f + token[0:1, 0:1]
    dx1, dpool, dattn, delta, dw_out_t, sums_mix = _mix_bwd(dh2, dout, x1, y1, cat, attn, w_out_t, sc_f, g_pre_ffn,
                                                           gt_m, g_post_mix, tm=256)
    du, dw_blk, sums_pool = _pool_bwd(dpool, u_pool, w_blk_b, b_pool_r, pool_scale_r, tm=512)
    dqkv = []
    for gi, dil in enumerate(DILATIONS):
        dqkv += list(_attn_bwd(qkv, dattn, lse_all, delta, gi, dil))
    dproj, grad_x, sums_in = _inproj_bwd(du, dqkv, rope, w_in_t, xs, dx1, sc_m, g_pre_mix, tm=256)
    dw_in_t = _wgrad(dproj, h1, "wgrad_in", tk=1024, tmm=1280)
    return (loss_loc, grad_x, dw_in_t, dw_out_t, dw_up_t, dw_down, dw_blk, dconv_w, dconv_b,
            sums_in, sums_mix, sums_ffn, sums_pool)


def kernel(x, c, positions, w_ada, b_ada, g_pre_mix, g_post_mix, g_pre_ffn, g_post_ffn, w_in, w_pool, b_pool, pool_scale, w_out, w_up, conv_w, conv_b, w_down, loss_target, m_w_ada, m_b_ada, m_g_pre_mix, m_g_post_mix, m_g_pre_ffn, m_g_post_ffn, m_w_in, m_w_pool, m_b_pool, m_pool_scale, m_w_out, m_w_up, m_conv_w, m_conv_b, m_w_down, v_w_ada, v_b_ada, v_g_pre_mix, v_g_post_mix, v_g_pre_ffn, v_g_post_ffn, v_w_in, v_w_pool, v_b_pool, v_pool_scale, v_w_out, v_w_up, v_conv_w, v_conv_b, v_w_down):
    s_len, d = x.shape[1], x.shape[2]
    d_ff = w_down.shape[1] * N_DEV
    me = _index(_place())
    xs, target = x[0], loss_target[0]

    ncol = w_ada.shape[2]
    b_cols = lax.dynamic_slice(b_ada, (0, me * ncol), (1, ncol))
    c_all, mod, taps_all = _ada_exchange(jnp.broadcast_to(c, (8, d)), w_ada[0], b_cols, _pad_rows(conv_w[0], 8))
    c_all = c_all[:, 0, :]
    conv_w_all = jnp.transpose(taps_all[:, :3, :], (1, 0, 2)).reshape(3, d_ff)
    sh_m, sc_m, gt_m, sh_f, sc_f, gt_f = [mod[:, 0, :].reshape(1, -1)[:, k * d:(k + 1) * d] for k in range(6)]

    w_in_t, w_out_t = _gather_weights([w_in[0].T.astype(BF16), w_out[0].T.astype(BF16)])

    rope = _rope_tables(positions[0])
    w_blk = jnp.zeros((256, 256), F32)
    for gi in range(4):
        w_blk = lax.dynamic_update_slice(w_blk, w_pool[0, gi], (gi * HEAD_DIM, gi * HEAD_DIM))
    w_blk_b = w_blk.astype(BF16)
    b_pool_r, pool_scale_r = b_pool.reshape(1, 256), pool_scale.reshape(1, 256)

    cw_rows = d_ff // LANES

    up_sh, down_sh = w_up[0].T.astype(BF16), w_down[0].astype(BF16)
    w_in_t, conv_w_all, up_sh, down_sh = lax.optimization_barrier((w_in_t, conv_w_all, up_sh, down_sh))
    lands = [lax.dynamic_update_slice(lax.empty((N_DEV * s.shape[0], s.shape[1]), BF16), s, (me * s.shape[0], 0))
             for s in (up_sh, down_sh)]
    w_send, w_recv, w_src, w_land, w_token = _exchange_start("gather", [up_sh, down_sh], lands, "ffn_weights_start")

    def fetch_ffn(after):
        return _exchange_wait("gather", w_send, w_recv, w_src, w_land, after, "ffn_weights_wait")

    flight = []

    def send_ffn_grads(dw_up_t, dw_down):
        lands = []
        for g in (dw_up_t, dw_down):
            r = g.shape[0] // N_DEV
            own = lax.dynamic_slice(g, (me * r, 0), (r, g.shape[1]))
            lands.append(lax.dynamic_update_slice(lax.empty((N_DEV, r, g.shape[1]), BF16), own[None], (me, 0, 0)))
        flight.extend(_exchange_start("scatter", [dw_up_t, dw_down], lands, "ffn_grads_start"))
        return flight[4]

    (loss_loc, grad_x, dw_in_t, dw_out_t, _, _, dw_blk, dconv_w, dconv_b,
     sums_in, sums_mix, sums_ffn, sums_pool) = _sequence_step(
        xs, target, rope, (sh_m + w_token[0:1, 0:1], sc_m, gt_m, sh_f, sc_f, gt_f),
        (g_pre_mix, g_post_mix, g_pre_ffn, g_post_ffn),
        w_in_t, w_out_t, fetch_ffn, send_ffn_grads, w_blk_b, b_pool_r, pool_scale_r, conv_w_all, conv_b)

    parts_ffn = _exchange_wait("scatter", *flight[:4], dw_in_t, "ffn_grads_wait")
    dw_in_t, dw_out_t, *parts_ffn = lax.optimization_barrier((dw_in_t, dw_out_t, *parts_ffn))
    parts_mix = _scatter_grads([dw_in_t, dw_out_t])
    g_w_in = _sum_slabs(parts_mix[0], "sum_w_in", 64).T
    g_w_out = _sum_slabs(parts_mix[1], "sum_w_out", 128).T
    g_w_up = _sum_slabs(parts_ffn[0], "sum_w_up", 64).T
    g_w_down = _sum_slabs(parts_ffn[1], "sum_w_down", 32)

    dmod = jnp.concatenate([sums_in[0:1], sums_in[1:2], sums_mix[3:4], sums_mix[0:1], sums_mix[1:2], sums_ffn[0:1]], axis=1)
    dw_pool = jnp.stack([dw_blk[gi * HEAD_DIM:(gi + 1) * HEAD_DIM, gi * HEAD_DIM:(gi + 1) * HEAD_DIM] for gi in range(4)])
    pieces = [(dmod, 48), (sums_in[2:3], 8), (sums_mix[4:5], 8), (sums_mix[2:3], 8), (sums_ffn[1:2], 8),
              (dw_pool, 128), (sums_pool[0:1], 8), (sums_pool[1:2], 8), (dconv_b, 24),
              (dconv_w[0:1], 24), (dconv_w[1:2], 24), (dconv_w[2:3], 24), (loss_loc[0:1], 8)]
    packed = jnp.concatenate([_as_rows(a, r) for a, r in pieces], axis=0)
    gathered, total = _allreduce_small(packed, "allreduce_small")
    n_rep = 248
    rep_w = [b_ada, g_pre_mix, g_post_mix, g_pre_ffn, g_post_ffn, w_pool, b_pool, pool_scale, conv_b]
    rep_m = [m_b_ada, m_g_pre_mix, m_g_post_mix, m_g_pre_ffn, m_g_post_ffn, m_w_pool, m_b_pool, m_pool_scale, m_conv_b]
    rep_v = [v_b_ada, v_g_pre_mix, v_g_post_mix, v_g_pre_ffn, v_g_post_ffn, v_w_pool, v_b_pool, v_pool_scale, v_conv_b]
    rep_rows = [r for _, r in pieces[:9]]
    pack_rep = lambda arrs: jnp.concatenate([_as_rows(a, r) for a, r in zip(arrs, rep_rows)], axis=0)
    rep_g = total[:n_rep]
    rep_d, rep_nm, rep_nv = _adam(pack_rep(rep_w), rep_g, pack_rep(rep_m), pack_rep(rep_v), "adam_small", n_rep)

    def unpack(p):
        out, row = [], 0
        for a, r in zip(rep_w, rep_rows):
            out.append(p[row:row + r].reshape(-1)[:a.size].reshape(a.shape))
            row += r
        return out

    g_rep, d_rep, nm_rep, nv_rep = unpack(rep_g), unpack(rep_d), unpack(rep_nm), unpack(rep_nv)

    fcol = d_ff // N_DEV
    g_cw_full = jnp.concatenate([total[n_rep + 24 * k:n_rep + 24 * k + cw_rows].reshape(1, d_ff) for k in range(3)], axis=0)
    g_cw = lax.dynamic_slice(g_cw_full, (0, me * fcol), (3, fcol))
    d_cw, nm_cw, nv_cw = _adam(conv_w[0], g_cw, m_conv_w[0], v_conv_w[0], "adam_conv_w", 3)

    dmod_all = gathered[:, :48].reshape(N_DEV, 6 * d)
    dmod_cols = lax.dynamic_slice(dmod_all, (0, me * ncol), (N_DEV, ncol))
    g_ada, d_ada, nm_ada, nv_ada = _ada_grad_adam(c_all, dmod_cols, w_ada[0], m_w_ada[0], v_w_ada[0], 256)

    big = {}
    for nm, w, g, m, v, tr in (("w_in", w_in, g_w_in, m_w_in, v_w_in, 256), ("w_out", w_out, g_w_out, m_w_out, v_w_out, 256),
                               ("w_up", w_up, g_w_up, m_w_up, v_w_up, 256), ("w_down", w_down, g_w_down, m_w_down, v_w_down, 88)):
        big[nm] = (g,) + tuple(_adam(w[0], g, m[0], v[0], "adam_" + nm, tr))

    loss = total[n_rep + 72, 0]

    def group(k):
        rep = (g_rep, d_rep, nm_rep, nv_rep)[k]
        ada = (g_ada, d_ada, nm_ada, nv_ada)[k][None]
        cw = (g_cw, d_cw, nm_cw, nv_cw)[k][None]
        return [ada, rep[0], rep[1], rep[2], rep[3], rep[4], big["w_in"][k][None], rep[5], rep[6], rep[7],
                big["w_out"][k][None], big["w_up"][k][None], cw, rep[8], big["w_down"][k][None]]

    return (loss, grad_x[None], *group(0), *group(1), *group(2), *group(3))
```

```python
import functools
import math

import jax
import jax.numpy as jnp
from jax import lax
from jax.experimental import pallas as pl
from jax.experimental.pallas import tpu as pltpu

F32 = jnp.float32
BF16 = jnp.bfloat16
MESH = pl.DeviceIdType.MESH

N_DEV = 8
HEAD_DIM = 64
ROT_HALF = 8
ROPE_THETA = 500000.0
POOL_WINDOWS = (2, 4, 8, 16)
DILATIONS = (1, 4, 16)
BLOCK = 128
NORM_EPS = 1e-6
HALO = 16
MASKED = -1e30
ATTN_FWD_UNROLL = 4
ATTN_BWD_UNROLL = 2

ADAM_LR = 0.001
ADAM_B1 = 0.9
ADAM_B2 = 0.999
ADAM_EPS = 1e-08
ADAM_WD = 0.01
ADAM_STEP = 10

V7X_VMEM_LIMIT = 56 * 1024 * 1024
LANES = 128

NT = (((1,), (1,)), ((), ()))
NN = (((1,), (0,)), ((), ()))
TN = (((0,), (0,)), ((), ()))


def _dot(a, b, dims):
    return lax.dot_general(a, b, dims, preferred_element_type=F32)


def _params(sem=None, vmem=V7X_VMEM_LIMIT):
    if sem is None:
        return pltpu.CompilerParams(vmem_limit_bytes=vmem)
    return pltpu.CompilerParams(dimension_semantics=sem, vmem_limit_bytes=vmem)


def _rstd(v):
    return lax.rsqrt(jnp.mean(v * v, axis=-1, keepdims=True) + NORM_EPS)


def _norm_bwd(dn, n, rstd):
    return rstd * (dn - n * jnp.mean(dn * n, axis=-1, keepdims=True))


def _rope_fwd(p, rope_ref):
    return p * rope_ref[0] + pltpu.roll(p, LANES - ROT_HALF, 1) * rope_ref[1] + pltpu.roll(p, ROT_HALF, 1) * rope_ref[2]


def _rope_bwd(dp, rope_ref):
    return dp * rope_ref[0] + pltpu.roll(dp * rope_ref[1], ROT_HALF, 1) + pltpu.roll(dp * rope_ref[2], LANES - ROT_HALF, 1)


def _gelu_parts(v):
    k = math.sqrt(2.0 / math.pi)
    t = jnp.tanh(k * (v + 0.044715 * v * v * v))
    g = 0.5 * v * (1.0 + t)
    dg = 0.5 * (1.0 + t) + 0.5 * v * (1.0 - t * t) * k * (1.0 + 3.0 * 0.044715 * v * v)
    return g, dg


def _halo_before(i, tile):
    return jnp.maximum(i * (tile // HALO) - 1, 0)


def _premix_inproj(x, sh, sc, g, w_in_t, rope, tm):
    s_len, d = x.shape
    n_proj = w_in_t.shape[0]
    n_slab = (n_proj - 256) // LANES

    def body(x_ref, sh_ref, sc_ref, g_ref, w_ref, rope_ref, h_ref, up_ref, qkv_ref):
        xv = x_ref[...]
        h = (xv * _rstd(xv) * g_ref[...]) * (1.0 + sc_ref[...]) + sh_ref[...]
        hb = h.astype(BF16)
        h_ref[...] = hb
        up_ref[...] = _dot(hb, w_ref[0:256, :], NT)
        for pair in range(n_slab // 2):
            p = _dot(hb, w_ref[256 + 256 * pair:512 + 256 * pair, :], NT)
            for half in range(2):
                ph = p[:, half * LANES:(half + 1) * LANES]
                if pair < 6:
                    ph = _rope_fwd(ph, rope_ref)
                if pair < 3:
                    ph = ph * (HEAD_DIM ** -0.5)
                qkv_ref[2 * pair + half] = ph

    vec = pl.BlockSpec((1, d), lambda i: (0, 0))
    return pl.pallas_call(
        body, name="premix_inproj", grid=(s_len // tm,),
        in_specs=[pl.BlockSpec((tm, d), lambda i: (i, 0)), vec, vec, vec,
                  pl.BlockSpec((n_proj, d), lambda i: (0, 0)),
                  pl.BlockSpec((3, tm, LANES), lambda i: (0, i, 0))],
        out_specs=[pl.BlockSpec((tm, d), lambda i: (i, 0)),
                   pl.BlockSpec((tm, 256), lambda i: (i, 0)),
                   pl.BlockSpec((n_slab, tm, LANES), lambda i: (0, i, 0))],
        out_shape=[jax.ShapeDtypeStruct((s_len, d), BF16),
                   jax.ShapeDtypeStruct((s_len, 256), F32),
                   jax.ShapeDtypeStruct((n_slab, s_len, LANES), F32)],
        compiler_params=_params(("arbitrary",)),
    )(x, sh, sc, g, w_in_t, rope)


def _block_rows(n, r, dil):
    start = n * (BLOCK * dil) + r
    if dil == 1:
        return pl.ds(pl.multiple_of(start, BLOCK), BLOCK)
    return pl.ds(start, BLOCK, stride=dil)


def _band_mask(n):
    ri = lax.broadcasted_iota(jnp.int32, (BLOCK, 2 * BLOCK), 0)
    cj = lax.broadcasted_iota(jnp.int32, (BLOCK, 2 * BLOCK), 1)
    cur = (cj >= BLOCK) & (cj - BLOCK <= ri)
    prev = (cj < BLOCK) & (cj >= ri) & (n > 0)
    return cur | prev


def _attn_fwd(qkv, group, dil):
    s_len = qkv.shape[1]
    nb = s_len // (BLOCK * dil)

    def body(q_ref, k_ref, v_ref, o_ref, lse_ref):
        lane = lax.broadcasted_iota(jnp.int32, (BLOCK, LANES), 1)
        first = lane < HEAD_DIM

        def block(t, carry):
            r, n = t // nb, t % nb
            cur = _block_rows(n, r, dil)
            prev = _block_rows(jnp.maximum(n - 1, 0), r, dil)
            q = q_ref[0, cur, :]
            kcat = jnp.concatenate([k_ref[0, prev, :], k_ref[0, cur, :]], axis=0).astype(BF16)
            vcat = jnp.concatenate([v_ref[0, prev, :], v_ref[0, cur, :]], axis=0).astype(BF16)
            valid = _band_mask(n)
            q2 = jnp.concatenate([jnp.where(first, q, 0.0), jnp.where(first, 0.0, q)], axis=0).astype(BF16)
            s = jnp.where(jnp.concatenate([valid, valid], axis=0), _dot(q2, kcat, NT), MASKED)
            m = jnp.max(s, axis=-1, keepdims=True)
            p = jnp.exp(s - m)
            den = jnp.sum(p, axis=-1, keepdims=True)
            o2 = _dot(p.astype(BF16), vcat, NN) / den
            lse2 = m + jnp.log(den)
            o_ref[0, cur, :] = jnp.where(first, o2[:BLOCK], o2[BLOCK:])
            lse_ref[0, cur, :] = jnp.where(first, lse2[:BLOCK], lse2[BLOCK:])
            return carry

        lax.fori_loop(0, nb * dil, block, 0, unroll=ATTN_FWD_UNROLL)

    def slab(base):
        return pl.BlockSpec((1, s_len, LANES), lambda s: (base + 2 * group + s, 0, 0))

    out = pl.BlockSpec((1, s_len, LANES), lambda s: (s, 0, 0))
    shape = jax.ShapeDtypeStruct((2, s_len, LANES), F32)
    return pl.pallas_call(
        body, name=f"attn_fwd_d{dil}", grid=(2,),
        in_specs=[slab(0), slab(6), slab(12)], out_specs=[out, out], out_shape=[shape, shape],
        compiler_params=_params(("arbitrary",)),
    )(qkv, qkv, qkv)


def _pool_mixed(u, halo, i, tm):
    ue = jnp.concatenate([halo, u], axis=0)
    s2 = ue + pltpu.roll(ue, 1, 0)
    s4 = s2 + pltpu.roll(s2, 2, 0)
    s8 = s4 + pltpu.roll(s4, 4, 0)
    s16 = s8 + pltpu.roll(s8, 8, 0)
    grp = lax.broadcasted_iota(jnp.int32, (tm, 256), 1) // HEAD_DIM
    pick = lambda a, b, c, e: jnp.where(grp == 0, a, jnp.where(grp == 1, b, jnp.where(grp == 2, c, e)))
    win_sum = pick(s2[HALO:], s4[HALO:], s8[HALO:], s16[HALO:])
    pos = (i * tm + lax.broadcasted_iota(jnp.int32, (tm, 256), 0)).astype(F32)
    count = jnp.minimum(pos + 1.0, pick(*[float(w) for w in POOL_WINDOWS]))
    return win_sum / count - u, count


def _mix_out(x, u_pool, o_g, lse_g, w_blk, b_pool, pool_scale, w_out_t, gt_m, g_post_mix, g_pre_ffn, sc_f, sh_f, tm):
    s_len, d = x.shape

    def body(x_ref, u_ref, uh_ref, o0, o1, o2, l0, l1, l2, wb_ref, bp_ref, ps_ref, wo_ref,
             gt_ref, g1_ref, g2_ref, sc_ref, sh_ref,
             x1_ref, y1_ref, h2_ref, cat_ref, attn_ref, lall_ref):
        i = pl.program_id(0)
        u = u_ref[...]
        halo = uh_ref[...] * (i > 0).astype(F32)
        mixed, _ = _pool_mixed(u, halo, i, tm)
        y = _dot(mixed.astype(BF16), wb_ref[...], NN) + bp_ref[...]
        pool = y * ps_ref[...]
        attn = []
        for s in range(2):
            la, lb, lc = l0[s], l1[s], l2[s]
            mx = jnp.maximum(jnp.maximum(la, lb), lc)
            ea, eb, ec = jnp.exp(la - mx), jnp.exp(lb - mx), jnp.exp(lc - mx)
            den = ea + eb + ec
            lall_ref[s] = mx + jnp.log(den)
            attn.append((ea / den) * o0[s] + (eb / den) * o1[s] + (ec / den) * o2[s])
        attn = jnp.concatenate(attn, axis=1)
        attn_ref[...] = attn
        cat = jnp.concatenate([pool, attn], axis=1).astype(BF16)
        cat_ref[...] = cat
        y1 = _dot(cat, wo_ref[...], NT)
        y1_ref[...] = y1
        x1 = x_ref[...] + gt_ref[...] * (y1 * _rstd(y1) * g1_ref[...])
        x1_ref[...] = x1
        h2 = (x1 * _rstd(x1) * g2_ref[...]) * (1.0 + sc_ref[...]) + sh_ref[...]
        h2_ref[...] = h2.astype(BF16)

    tile = lambda w: pl.BlockSpec((tm, w), lambda i: (i, 0))
    slab = pl.BlockSpec((2, tm, LANES), lambda i: (0, i, 0))
    const = lambda a: pl.BlockSpec(a.shape, lambda i: (0,) * a.ndim)
    return pl.pallas_call(
        body, name="mix_out", grid=(s_len // tm,),
        in_specs=[tile(d), tile(256), pl.BlockSpec((HALO, 256), lambda i: (_halo_before(i, tm), 0)),
                  slab, slab, slab, slab, slab, slab,
                  const(w_blk), const(b_pool), const(pool_scale), const(w_out_t),
                  const(gt_m), const(g_post_mix), const(g_pre_ffn), const(sc_f), const(sh_f)],
        out_specs=[tile(d), tile(d), tile(d), tile(512), tile(256), slab],
        out_shape=[jax.ShapeDtypeStruct((s_len, d), F32), jax.ShapeDtypeStruct((s_len, d), F32),
                   jax.ShapeDtypeStruct((s_len, d), BF16), jax.ShapeDtypeStruct((s_len, 512), BF16),
                   jax.ShapeDtypeStruct((s_len, 256), F32), jax.ShapeDtypeStruct((2, s_len, LANES), F32)],
        compiler_params=_params(("arbitrary",)),
    )(x, u_pool, u_pool, *o_g, *lse_g, w_blk, b_pool, pool_scale, w_out_t, gt_m, g_post_mix, g_pre_ffn, sc_f, sh_f)


def _conv_gate(gate_ext, cw_ref, cb_ref):
    gc = gate_ext * cw_ref[2:3, :] + pltpu.roll(gate_ext, 1, 0) * cw_ref[1:2, :] + pltpu.roll(gate_ext, 2, 0) * cw_ref[0:1, :]
    return gc[HALO:] + cb_ref[...]


def _ffn_fwd_loss(h2, x1, target, w_up_t, w_down, conv_w, conv_b, gt_f, g_post_ffn, tm, tf):
    s_len, d = x1.shape
    d_ff = w_down.shape[0]
    n_f = d_ff // tf

    def body(h_ref, hh_ref, x1_ref, tgt_ref, wg_ref, wv_ref, wd_ref, cw_ref, cb_ref, gt_ref, g_ref,
             gate_ref, val_ref, dy2_ref, dout_ref, sums_ref, loss_ref, acc_ref):
        i, j = pl.program_id(0), pl.program_id(1)

        @pl.when((i == 0) & (j == 0))
        def _():
            sums_ref[...] = jnp.zeros_like(sums_ref)
            loss_ref[...] = jnp.zeros_like(loss_ref)

        h_ext = jnp.concatenate([hh_ref[...], h_ref[...]], axis=0)
        gate_ext = _dot(h_ext, wg_ref[...], NT)
        row = lax.broadcasted_iota(jnp.int32, gate_ext.shape, 0)
        gate_ext = jnp.where((row < HALO) & (i == 0), 0.0, gate_ext)
        val = _dot(h_ref[...], wv_ref[...], NT)
        act, _ = _gelu_parts(_conv_gate(gate_ext, cw_ref, cb_ref))
        gate_ref[...] = gate_ext[HALO:].astype(BF16)
        val_ref[...] = val.astype(BF16)
        part = _dot((act * val).astype(BF16), wd_ref[...], NN)

        @pl.when(j == 0)
        def _():
            acc_ref[...] = part

        @pl.when(j > 0)
        def _():
            acc_ref[...] += part

        @pl.when(j == n_f - 1)
        def _():
            y2 = acc_ref[...]
            rstd = _rstd(y2)
            n = y2 * rstd
            rn = n * g_ref[...]
            err = x1_ref[...] + gt_ref[...] * rn - tgt_ref[...]
            loss_ref[...] += 0.5 * jnp.sum(jnp.mean(err * err, axis=-1, keepdims=True), axis=0, keepdims=True)
            dout = err * (1.0 / d)
            dout_ref[...] = dout
            drn = dout * gt_ref[...]
            sums_ref[0:1, :] += jnp.sum(dout * rn, axis=0, keepdims=True)
            sums_ref[1:2, :] += jnp.sum(drn * n, axis=0, keepdims=True)
            dy2_ref[...] = _norm_bwd(drn * g_ref[...], n, rstd).astype(BF16)

    tok = lambda w: pl.BlockSpec((tm, w), lambda i, j: (i, 0))
    tokf = pl.BlockSpec((tm, tf), lambda i, j: (i, j))
    vec = pl.BlockSpec((1, d), lambda i, j: (0, 0))
    return pl.pallas_call(
        body, name="ffn_fwd_loss", grid=(s_len // tm, n_f),
        in_specs=[tok(d), pl.BlockSpec((HALO, d), lambda i, j: (_halo_before(i, tm), 0)), tok(d), tok(d),
                  pl.BlockSpec((tf, d), lambda i, j: (j, 0)), pl.BlockSpec((tf, d), lambda i, j: (j + n_f, 0)),
                  pl.BlockSpec((tf, d), lambda i, j: (j, 0)),
                  pl.BlockSpec((3, tf), lambda i, j: (0, j)), pl.BlockSpec((1, tf), lambda i, j: (0, j)), vec, vec],
        out_specs=[tokf, tokf, tok(d), tok(d), pl.BlockSpec((8, d), lambda i, j: (0, 0)),
                   pl.BlockSpec((8, LANES), lambda i, j: (0, 0))],
        out_shape=[jax.ShapeDtypeStruct((s_len, d_ff), BF16), jax.ShapeDtypeStruct((s_len, d_ff), BF16),
                   jax.ShapeDtypeStruct((s_len, d), BF16), jax.ShapeDtypeStruct((s_len, d), F32),
                   jax.ShapeDtypeStruct((8, d), F32), jax.ShapeDtypeStruct((8, LANES), F32)],
        scratch_shapes=[pltpu.VMEM((tm, d), F32)],
        compiler_params=_params(("arbitrary", "arbitrary")),
    )(h2, h2, x1, target, w_up_t, w_up_t, w_down, conv_w, conv_b, gt_f, g_post_ffn)


def _ffn_bwd_act(dy2, gate, val, w_down, conv_w, conv_b, tm, tf):
    s_len, d = dy2.shape
    d_ff = w_down.shape[0]
    n_t = s_len // tm

    def body(dy_ref, g_ref, gh_ref, v_ref, wd_ref, cw_ref, cb_ref,
             dgc_ref, dval_ref, dwd_ref, dcw_ref, dcb_ref, acc_ref):
        i = pl.program_id(1)
        gate_ext = jnp.concatenate([gh_ref[...], g_ref[...]], axis=0).astype(F32)
        row = lax.broadcasted_iota(jnp.int32, gate_ext.shape, 0)
        gate_ext = jnp.where((row < HALO) & (i == 0), 0.0, gate_ext)
        act, dact = _gelu_parts(_conv_gate(gate_ext, cw_ref, cb_ref))
        val = v_ref[...].astype(F32)
        da = _dot(dy_ref[...], wd_ref[...], NT)
        dgc = da * val * dact
        dgc_ref[...] = dgc.astype(BF16)
        dval_ref[...] = (da * act).astype(BF16)
        dwd = _dot((act * val).astype(BF16), dy_ref[...], TN)
        taps = jnp.concatenate(
            [jnp.sum(dgc * pltpu.roll(gate_ext, 2 - k, 0)[HALO:], axis=0, keepdims=True) if k < 2
             else jnp.sum(dgc * gate_ext[HALO:], axis=0, keepdims=True) for k in range(3)], axis=0)
        bias = jnp.sum(dgc, axis=0, keepdims=True)

        @pl.when(i == 0)
        def _():
            acc_ref[...] = dwd
            dcw_ref[...] = taps
            dcb_ref[...] = bias

        @pl.when(i > 0)
        def _():
            acc_ref[...] += dwd
            dcw_ref[...] += taps
            dcb_ref[...] += bias

        @pl.when(i == n_t - 1)
        def _():
            dwd_ref[...] = acc_ref[...].astype(BF16)

    tokf = pl.BlockSpec((tm, tf), lambda j, i: (i, j))
    return pl.pallas_call(
        body, name="ffn_bwd_act", grid=(d_ff // tf, n_t),
        in_specs=[pl.BlockSpec((tm, d), lambda j, i: (i, 0)), tokf,
                  pl.BlockSpec((HALO, tf), lambda j, i: (_halo_before(i, tm), j)), tokf,
                  pl.BlockSpec((tf, d), lambda j, i: (j, 0)),
                  pl.BlockSpec((3, tf), lambda j, i: (0, j)), pl.BlockSpec((1, tf), lambda j, i: (0, j))],
        out_specs=[tokf, tokf, pl.BlockSpec((tf, d), lambda j, i: (j, 0)),
                   pl.BlockSpec((3, tf), lambda j, i: (0, j)), pl.BlockSpec((1, tf), lambda j, i: (0, j))],
        out_shape=[jax.ShapeDtypeStruct((s_len, d_ff), BF16), jax.ShapeDtypeStruct((s_len, d_ff), BF16),
                   jax.ShapeDtypeStruct((d_ff, d), BF16), jax.ShapeDtypeStruct((3, d_ff), F32),
                   jax.ShapeDtypeStruct((1, d_ff), F32)],
        scratch_shapes=[pltpu.VMEM((tf, d), F32)],
        compiler_params=_params(("arbitrary", "arbitrary")),
    )(dy2, gate, gate, val, w_down, conv_w, conv_b)


def _ffn_bwd_up(dgc, dval, w_up_t, conv_w, tm):
    s_len, d_ff = dgc.shape
    d = w_up_t.shape[1]
    n_t = s_len // tm

    def body(dg_ref, dgn_ref, dv_ref, cw_ref, w_ref, dup_ref, dh_ref):
        i = pl.program_id(0)
        nxt = dgn_ref[...].astype(F32) * (i < n_t - 1).astype(F32)
        ext = jnp.concatenate([dg_ref[...].astype(F32), nxt], axis=0)
        rows = tm + HALO
        dgate = (ext * cw_ref[2:3, :] + pltpu.roll(ext, rows - 1, 0) * cw_ref[1:2, :]
                 + pltpu.roll(ext, rows - 2, 0) * cw_ref[0:1, :])[:tm]
        dup = jnp.concatenate([dgate.astype(BF16), dv_ref[...]], axis=1)
        dup_ref[...] = dup
        dh_ref[...] = _dot(dup, w_ref[...], NN)

    tokf = pl.BlockSpec((tm, d_ff), lambda i: (i, 0))
    return pl.pallas_call(
        body, name="ffn_bwd_up", grid=(n_t,),
        in_specs=[tokf, pl.BlockSpec((HALO, d_ff), lambda i: (jnp.minimum((i + 1) * (tm // HALO), s_len // HALO - 1), 0)),
                  tokf, pl.BlockSpec((3, d_ff), lambda i: (0, 0)), pl.BlockSpec((2 * d_ff, d), lambda i: (0, 0))],
        out_specs=[pl.BlockSpec((tm, 2 * d_ff), lambda i: (i, 0)), pl.BlockSpec((tm, d), lambda i: (i, 0))],
        out_shape=[jax.ShapeDtypeStruct((s_len, 2 * d_ff), BF16), jax.ShapeDtypeStruct((s_len, d), F32)],
        compiler_params=_params(("arbitrary",)),
    )(dgc, dgc, dval, conv_w, w_up_t)


def _mix_bwd(dh2, dout, x1, y1, cat, attn, w_out_t, sc_f, g_pre_ffn, gt_m, g_post_mix, tm):
    s_len, d = x1.shape
    n_t = s_len // tm

    def body(dh_ref, do_ref, x1_ref, y1_ref, cat_ref, at_ref, wo_ref, sc_ref, g2_ref, gt_ref, g1_ref,
             dx1_ref, dpool_ref, dattn_ref, delta_ref, dwo_ref, sums_ref, acc_ref):
        i = pl.program_id(0)
        dh = dh_ref[...]
        x1 = x1_ref[...]
        r2 = _rstd(x1)
        n2 = x1 * r2
        ng = n2 * g2_ref[...]
        dng = dh * (1.0 + sc_ref[...])
        dx1 = do_ref[...] + _norm_bwd(dng * g2_ref[...], n2, r2)
        dx1_ref[...] = dx1
        y1 = y1_ref[...]
        r1 = _rstd(y1)
        n1 = y1 * r1
        drn = dx1 * gt_ref[...]
        dy1 = _norm_bwd(drn * g1_ref[...], n1, r1).astype(BF16)
        dcat = _dot(dy1, wo_ref[...], NN)
        dpool_ref[...] = dcat[:, 0:256]
        lane = lax.broadcasted_iota(jnp.int32, (tm, LANES), 1)
        first = lane < HEAD_DIM
        for s in range(2):
            da = dcat[:, 256 + s * LANES:256 + (s + 1) * LANES]
            dattn_ref[s] = da
            prod = da * at_ref[:, s * LANES:(s + 1) * LANES]
            tot = jnp.sum(prod, axis=-1, keepdims=True)
            lo = jnp.sum(jnp.where(first, prod, 0.0), axis=-1, keepdims=True)
            delta_ref[s] = jnp.where(first, lo, tot - lo)
        dwo = _dot(dy1, cat_ref[...], TN)
        sums = jnp.concatenate(
            [jnp.sum(dh, axis=0, keepdims=True), jnp.sum(dh * ng, axis=0, keepdims=True),
             jnp.sum(dng * n2, axis=0, keepdims=True), jnp.sum(dx1 * (n1 * g1_ref[...]), axis=0, keepdims=True),
             jnp.sum(drn * n1, axis=0, keepdims=True), jnp.zeros((3, d), F32)], axis=0)

        @pl.when(i == 0)
        def _():
            acc_ref[...] = dwo
            sums_ref[...] = sums

        @pl.when(i > 0)
        def _():
            acc_ref[...] += dwo
            sums_ref[...] += sums

        @pl.when(i == n_t - 1)
        def _():
            dwo_ref[...] = acc_ref[...].astype(BF16)

    tile = lambda w: pl.BlockSpec((tm, w), lambda i: (i, 0))
    slab = pl.BlockSpec((2, tm, LANES), lambda i: (0, i, 0))
    vec = pl.BlockSpec((1, d), lambda i: (0, 0))
    return pl.pallas_call(
        body, name="mix_bwd", grid=(n_t,),
        in_specs=[tile(d), tile(d), tile(d), tile(d), tile(512), tile(256),
                  pl.BlockSpec((d, 512), lambda i: (0, 0)), vec, vec, vec, vec],
        out_specs=[tile(d), tile(256), slab, slab, pl.BlockSpec((d, 512), lambda i: (0, 0)),
                   pl.BlockSpec((8, d), lambda i: (0, 0))],
        out_shape=[jax.ShapeDtypeStruct((s_len, d), F32), jax.ShapeDtypeStruct((s_len, 256), F32),
                   jax.ShapeDtypeStruct((2, s_len, LANES), F32), jax.ShapeDtypeStruct((2, s_len, LANES), F32),
                   jax.ShapeDtypeStruct((d, 512), BF16), jax.ShapeDtypeStruct((8, d), F32)],
        scratch_shapes=[pltpu.VMEM((d, 512), F32)],
        compiler_params=_params(("arbitrary",)),
    )(dh2, dout, x1, y1, cat, attn, w_out_t, sc_f, g_pre_ffn, gt_m, g_post_mix)


def _pool_bwd(dpool, u_pool, w_blk, b_pool, pool_scale, tm):
    s_len = dpool.shape[0]
    n_t = s_len // tm

    def body(dp_ref, dpn_ref, u_ref, uh_ref, wb_ref, bp_ref, ps_ref, du_ref, dwb_ref, sums_ref):
        i = pl.program_id(0)
        u = u_ref[...]
        mixed, _ = _pool_mixed(u, uh_ref[...] * (i > 0).astype(F32), i, tm)
        mixed_b = mixed.astype(BF16)
        y = _dot(mixed_b, wb_ref[...], NN) + bp_ref[...]
        dp = dp_ref[...]
        dy = dp * ps_ref[...]
        dwb = _dot(mixed_b, dy.astype(BF16), TN)
        sums = jnp.concatenate([jnp.sum(dy, axis=0, keepdims=True), jnp.sum(dp * y, axis=0, keepdims=True),
                                jnp.zeros((6, 256), F32)], axis=0)
        dp_ext = jnp.concatenate([dp, dpn_ref[...] * (i < n_t - 1).astype(F32)], axis=0)
        dmix = _dot((dp_ext * ps_ref[...]).astype(BF16), wb_ref[...], NT)
        rows = tm + HALO
        grp = lax.broadcasted_iota(jnp.int32, (rows, 256), 1) // HEAD_DIM
        pick = lambda a, b, c, e: jnp.where(grp == 0, a, jnp.where(grp == 1, b, jnp.where(grp == 2, c, e)))
        pos = (i * tm + lax.broadcasted_iota(jnp.int32, (rows, 256), 0)).astype(F32)
        z = dmix / jnp.minimum(pos + 1.0, pick(*[float(w) for w in POOL_WINDOWS]))
        f2 = z + pltpu.roll(z, rows - 1, 0)
        f4 = f2 + pltpu.roll(f2, rows - 2, 0)
        f8 = f4 + pltpu.roll(f4, rows - 4, 0)
        f16 = f8 + pltpu.roll(f8, rows - 8, 0)
        du_ref[...] = (pick(f2, f4, f8, f16) - dmix)[:tm]

        @pl.when(i == 0)
        def _():
            dwb_ref[...] = dwb
            sums_ref[...] = sums

        @pl.when(i > 0)
        def _():
            dwb_ref[...] += dwb
            sums_ref[...] += sums

    tile = pl.BlockSpec((tm, 256), lambda i: (i, 0))
    const = lambda a: pl.BlockSpec(a.shape, lambda i: (0,) * a.ndim)
    return pl.pallas_call(
        body, name="pool_bwd", grid=(n_t,),
        in_specs=[tile, pl.BlockSpec((HALO, 256), lambda i: (jnp.minimum((i + 1) * (tm // HALO), s_len // HALO - 1), 0)),
                  tile, pl.BlockSpec((HALO, 256), lambda i: (_halo_before(i, tm), 0)),
                  const(w_blk), const(b_pool), const(pool_scale)],
        out_specs=[tile, pl.BlockSpec((256, 256), lambda i: (0, 0)), pl.BlockSpec((8, 256), lambda i: (0, 0))],
        out_shape=[jax.ShapeDtypeStruct((s_len, 256), F32), jax.ShapeDtypeStruct((256, 256), F32),
                   jax.ShapeDtypeStruct((8, 256), F32)],
        compiler_params=_params(("arbitrary",)),
    )(dpool, dpool, u_pool, u_pool, w_blk, b_pool, pool_scale)


def _attn_bwd(qkv, dattn, lse_all, delta, group, dil):
    s_len = qkv.shape[1]
    nb = s_len // (BLOCK * dil)

    def body(q_ref, k_ref, v_ref, do_ref, l_ref, dl_ref, dq_ref, dk_ref, dv_ref):
        lane = lax.broadcasted_iota(jnp.int32, (BLOCK, LANES), 1)
        first = lane < HEAD_DIM

        def block(t, carry):
            dk_part, dv_part = carry
            r, n = t // nb, t % nb
            cur = _block_rows(n, r, dil)
            prev = _block_rows(jnp.maximum(n - 1, 0), r, dil)
            q = q_ref[0, cur, :]
            do = do_ref[0, cur, :]
            lse = l_ref[0, cur, :]
            dlt = dl_ref[0, cur, :]
            kcat = jnp.concatenate([k_ref[0, prev, :], k_ref[0, cur, :]], axis=0).astype(BF16)
            vcat = jnp.concatenate([v_ref[0, prev, :], v_ref[0, cur, :]], axis=0).astype(BF16)
            valid = _band_mask(n)
            stack = lambda a: jnp.concatenate([jnp.where(first, a, 0.0), jnp.where(first, 0.0, a)], axis=0)
            rows2 = lambda a: jnp.concatenate([a[:, 0:1], a[:, HEAD_DIM:HEAD_DIM + 1]], axis=0)
            q2, do2 = stack(q).astype(BF16), stack(do).astype(BF16)
            valid2 = jnp.concatenate([valid, valid], axis=0)
            p = jnp.where(valid2, jnp.exp(_dot(q2, kcat, NT) - rows2(lse)), 0.0)
            ds = (p * (_dot(do2, vcat, NT) - rows2(dlt))).astype(BF16)
            dq2 = _dot(ds, kcat, NN)
            dq_ref[0, cur, :] = jnp.where(first, dq2[:BLOCK], dq2[BLOCK:])
            dkc = _dot(ds, q2, TN)
            dvc = _dot(p.astype(BF16), do2, TN)
            dk_ref[0, prev, :] = dk_part + dkc[:BLOCK]
            dv_ref[0, prev, :] = dv_part + dvc[:BLOCK]
            dk_ref[0, cur, :] = dkc[BLOCK:]
            dv_ref[0, cur, :] = dvc[BLOCK:]
            return dkc[BLOCK:], dvc[BLOCK:]

        def blocks(tt, carry):
            for u in range(ATTN_BWD_UNROLL):
                carry = block(tt * ATTN_BWD_UNROLL + u, carry)
            return carry

        zero = jnp.zeros((BLOCK, LANES), F32)
        lax.fori_loop(0, nb * dil // ATTN_BWD_UNROLL, blocks, (zero, zero))

    def slab(base):
        return pl.BlockSpec((1, s_len, LANES), lambda s: (base + 2 * group + s, 0, 0))

    one = pl.BlockSpec((1, s_len, LANES), lambda s: (s, 0, 0))
    shape = jax.ShapeDtypeStruct((2, s_len, LANES), F32)
    return pl.pallas_call(
        body, name=f"attn_bwd_d{dil}", grid=(2,),
        in_specs=[slab(0), slab(6), slab(12), one, one, one],
        out_specs=[one, one, one], out_shape=[shape, shape, shape],
        compiler_params=_params(("arbitrary",)),
    )(qkv, qkv, qkv, dattn, lse_all, delta)


def _inproj_bwd(du, dqkv, rope, w_in_t, x, dx1, sc_m, g_pre_mix, tm):
    s_len, d = x.shape
    n_proj = w_in_t.shape[0]
    n_t = s_len // tm

    def body(du_ref, *refs):
        dref = refs[:9]
        rope_ref, w_ref, x_ref, dx1_ref, sc_ref, g_ref, dproj_ref, dx_ref, sums_ref = refs[9:]
        i = pl.program_id(0)
        cols = [du_ref[...].astype(BF16)]
        for kind in range(3):
            for grp in range(3):
                for s in range(2):
                    piece = dref[3 * grp + kind][s]
                    if kind < 2:
                        piece = _rope_bwd(piece, rope_ref)
                    if kind == 0:
                        piece = piece * (HEAD_DIM ** -0.5)
                    cols.append(piece.astype(BF16))
        dproj = jnp.concatenate(cols, axis=1)
        dproj_ref[...] = dproj
        dh = _dot(dproj, w_ref[...], NN)
        xv = x_ref[...]
        r = _rstd(xv)
        n = xv * r
        dng = dh * (1.0 + sc_ref[...])
        dx_ref[...] = dx1_ref[...] + _norm_bwd(dng * g_ref[...], n, r)
        sums = jnp.concatenate([jnp.sum(dh, axis=0, keepdims=True), jnp.sum(dh * (n * g_ref[...]), axis=0, keepdims=True),
                                jnp.sum(dng * n, axis=0, keepdims=True), jnp.zeros((5, d), F32)], axis=0)

        @pl.when(i == 0)
        def _():
            sums_ref[...] = sums

        @pl.when(i > 0)
        def _():
            sums_ref[...] += sums

    tile = lambda w: pl.BlockSpec((tm, w), lambda i: (i, 0))
    slab = pl.BlockSpec((2, tm, LANES), lambda i: (0, i, 0))
    vec = pl.BlockSpec((1, d), lambda i: (0, 0))
    return pl.pallas_call(
        body, name="inproj_bwd", grid=(n_t,),
        in_specs=[tile(256)] + [slab] * 9 + [pl.BlockSpec((3, tm, LANES), lambda i: (0, i, 0)),
                                             pl.BlockSpec((n_proj, d), lambda i: (0, 0)), tile(d), tile(d), vec, vec],
        out_specs=[tile(n_proj), tile(d), pl.BlockSpec((8, d), lambda i: (0, 0))],
        out_shape=[jax.ShapeDtypeStruct((s_len, n_proj), BF16), jax.ShapeDtypeStruct((s_len, d), F32),
                   jax.ShapeDtypeStruct((8, d), F32)],
        compiler_params=_params(("arbitrary",)),
    )(du, *dqkv, rope, w_in_t, x, dx1, sc_m, g_pre_mix)


def _wgrad(a, b, name, tk, tmm):
    s_len, m = a.shape
    n = b.shape[1]
    n_k = s_len // tk

    def body(a_ref, b_ref, o_ref, acc_ref):
        k = pl.program_id(1)
        part = _dot(a_ref[...], b_ref[...], TN)

        @pl.when(k == 0)
        def _():
            acc_ref[...] = part

        @pl.when(k > 0)
        def _():
            acc_ref[...] += part

        @pl.when(k == n_k - 1)
        def _():
            o_ref[...] = acc_ref[...].astype(BF16)

    return pl.pallas_call(
        body, name=name, grid=(m // tmm, n_k),
        in_specs=[pl.BlockSpec((tk, tmm), lambda j, k: (k, j)), pl.BlockSpec((tk, n), lambda j, k: (k, 0))],
        out_specs=pl.BlockSpec((tmm, n), lambda j, k: (j, 0)),
        out_shape=jax.ShapeDtypeStruct((m, n), BF16),
        scratch_shapes=[pltpu.VMEM((tmm, n), F32)],
        compiler_params=_params(("arbitrary", "arbitrary")),
    )(a, b)


def _place():
    return lax.axis_index("x"), lax.axis_index("y"), lax.axis_index("c")


def _peer(k):
    x, y, c = _place()
    bx, by, bc = (k >> 2) & 1, (k >> 1) & 1, k & 1
    return (x ^ bx if bx else x, y ^ by if by else y, c ^ bc if bc else c)


def _index(pos):
    return 4 * pos[0] + 2 * pos[1] + pos[2]


def _ada_exchange(c_rows, w_ada, b_ada_cols, taps):
    d = c_rows.shape[1]
    ncol = w_ada.shape[1]

    def body(c_ref, w_ref, b_ref, t_ref, call_ref, mod_ref, tall_ref, stage_ref, send_sems, recv_sems):
        me = _index(_place())
        call_ref[me] = c_ref[...]
        tall_ref[me] = t_ref[...]

        def gather(k):
            return pltpu.make_async_remote_copy(
                src_ref=c_ref, dst_ref=call_ref.at[me], send_sem=send_sems.at[0, k - 1], recv_sem=recv_sems.at[0, k - 1],
                device_id=_peer(k), device_id_type=MESH)

        def gather_taps(k):
            return pltpu.make_async_remote_copy(
                src_ref=t_ref, dst_ref=tall_ref.at[me], send_sem=send_sems.at[2, k - 1], recv_sem=recv_sems.at[2, k - 1],
                device_id=_peer(k), device_id_type=MESH)

        for k in range(1, N_DEV):
            gather(k).start()
        for k in range(1, N_DEV):
            gather_taps(k).start()
        for k in range(1, N_DEV):
            gather(k).wait_recv()
        cv = jnp.concatenate([call_ref[b, 0:1, :] for b in range(N_DEV)], axis=0)
        act = cv * jax.nn.sigmoid(cv)
        mod = lax.dot_general(act, w_ref[...], NN, preferred_element_type=F32,
                              precision=lax.Precision.HIGHEST) + b_ref[...]
        for b in range(N_DEV):
            stage_ref[b] = jnp.broadcast_to(mod[b:b + 1, :], (8, ncol))
        mod_ref[me] = stage_ref[me]

        def scatter(k):
            return pltpu.make_async_remote_copy(
                src_ref=stage_ref.at[_index(_peer(k))], dst_ref=mod_ref.at[me],
                send_sem=send_sems.at[1, k - 1], recv_sem=recv_sems.at[1, k - 1],
                device_id=_peer(k), device_id_type=MESH)

        for k in range(1, N_DEV):
            scatter(k).start()
        for k in range(1, N_DEV):
            scatter(k).wait_recv()
        for k in range(1, N_DEV):
            gather_taps(k).wait_recv()
        for k in range(1, N_DEV):
            gather(k).wait_send()
            scatter(k).wait_send()
            gather_taps(k).wait_send()

    vmem = pl.BlockSpec(memory_space=pltpu.VMEM)
    return pl.pallas_call(
        body, name="ada_exchange",
        in_specs=[vmem] * 4, out_specs=[vmem] * 3,
        out_shape=[jax.ShapeDtypeStruct((N_DEV, 8, d), F32), jax.ShapeDtypeStruct((N_DEV, 8, ncol), F32),
                   jax.ShapeDtypeStruct((N_DEV,) + taps.shape, F32)],
        scratch_shapes=[pltpu.VMEM((N_DEV, 8, ncol), F32), pltpu.SemaphoreType.DMA((3, N_DEV - 1)),
                        pltpu.SemaphoreType.DMA((3, N_DEV - 1))],
        compiler_params=_params(),
    )(c_rows, w_ada, b_ada_cols, taps)


def _gather_weights(shards):
    n_w = len(shards)

    def body(*refs):
        srcs, outs = refs[:n_w], refs[n_w:2 * n_w]
        send_sems, recv_sems, local_sems = refs[2 * n_w:]
        x, y, c = _place()
        me, sibling = (x, y, c), (x, y, 1 - c)
        chips = [(1 - x, y), (x, 1 - y), (1 - x, 1 - y)]

        def rows(w, pos):
            r = shards[w].shape[0]
            return outs[w].at[pl.ds(pl.multiple_of(_index(pos) * r, 16), r), :]

        def copy(k, w, block, to, own=False):
            return pltpu.make_async_remote_copy(
                src_ref=srcs[w] if own else rows(w, block), dst_ref=rows(w, block),
                send_sem=send_sems.at[k, w], recv_sem=recv_sems.at[k, w], device_id=to, device_id_type=MESH)

        mine = [pltpu.make_async_copy(srcs[w], rows(w, me), local_sems.at[w]) for w in range(n_w)]
        for cp in mine:
            cp.start()
        first = [copy(0, w, me, sibling, own=True) for w in range(n_w)]
        first += [copy(1 + j, w, me, (*chip, c), own=True) for j, chip in enumerate(chips) for w in range(n_w)]
        for cp in first:
            cp.start()
        passed = []
        for j, chip in enumerate(chips):
            for w in range(n_w):
                copy(1 + j, w, (*chip, c), me).wait_recv()
                fwd = copy(4 + j, w, (*chip, c), sibling)
                fwd.start()
                passed.append(fwd)
        for w in range(n_w):
            copy(0, w, sibling, me).wait_recv()
        for j, chip in enumerate(chips):
            for w in range(n_w):
                copy(4 + j, w, (*chip, 1 - c), me).wait_recv()
        for cp in first + passed:
            cp.wait_send()
        for cp in mine:
            cp.wait()

    hbm = pl.BlockSpec(memory_space=pltpu.HBM)
    return pl.pallas_call(
        body, name="gather_weights",
        in_specs=[hbm] * n_w, out_specs=[hbm] * n_w,
        out_shape=[jax.ShapeDtypeStruct((N_DEV * s.shape[0], s.shape[1]), s.dtype) for s in shards],
        scratch_shapes=[pltpu.SemaphoreType.DMA((N_DEV - 1, n_w)), pltpu.SemaphoreType.DMA((N_DEV - 1, n_w)),
                        pltpu.SemaphoreType.DMA((n_w,))],
        compiler_params=_params(),
    )(*shards)


def _scatter_grads(grads):
    n_w = len(grads)

    def body(*refs):
        srcs, outs = refs[:n_w], refs[n_w:2 * n_w]
        send_sems, recv_sems, local_sems = refs[2 * n_w:]
        me = _index(_place())

        def slab(w, dev):
            r = grads[w].shape[0] // N_DEV
            return srcs[w].at[pl.ds(pl.multiple_of(dev * r, 16), r), :]

        def copy(k, w):
            return pltpu.make_async_remote_copy(
                src_ref=slab(w, _index(_peer(k))), dst_ref=outs[w].at[me],
                send_sem=send_sems.at[k - 1, w], recv_sem=recv_sems.at[k - 1, w],
                device_id=_peer(k), device_id_type=MESH)

        mine = [pltpu.make_async_copy(slab(w, me), outs[w].at[me], local_sems.at[w]) for w in range(n_w)]
        for cp in mine:
            cp.start()
        sends = [copy(k, w) for k in range(1, N_DEV) for w in range(n_w)]
        for cp in sends:
            cp.start()
        for cp in sends:
            cp.wait_recv()
        for cp in sends:
            cp.wait_send()
        for cp in mine:
            cp.wait()

    hbm = pl.BlockSpec(memory_space=pltpu.HBM)
    return pl.pallas_call(
        body, name="scatter_grads",
        in_specs=[hbm] * n_w, out_specs=[hbm] * n_w,
        out_shape=[jax.ShapeDtypeStruct((N_DEV, g.shape[0] // N_DEV, g.shape[1]), g.dtype) for g in grads],
        scratch_shapes=[pltpu.SemaphoreType.DMA((N_DEV - 1, n_w)), pltpu.SemaphoreType.DMA((N_DEV - 1, n_w)),
                        pltpu.SemaphoreType.DMA((n_w,))],
        compiler_params=_params(),
    )(*grads)


def _peer_copies(mode, srcs, lands, send_sems, recv_sems):
    me = _index(_place())
    copies = []
    for k in range(1, N_DEV):
        peer = _peer(k)
        for w, (src, land) in enumerate(zip(srcs, lands)):
            if mode == "gather":
                r = src.shape[0]
                dst = land.at[pl.ds(pl.multiple_of(me * r, 16), r), :]
            else:
                r = src.shape[0] // N_DEV
                src = src.at[pl.ds(pl.multiple_of(_index(peer) * r, 16), r), :]
                dst = land.at[me]
            copies.append(pltpu.make_async_remote_copy(
                src_ref=src, dst_ref=dst, send_sem=send_sems.at[(k - 1) * len(srcs) + w],
                recv_sem=recv_sems.at[(k - 1) * len(srcs) + w],
                device_id=peer, device_id_type=MESH))
    return copies


def _exchange_start(mode, srcs, lands, name):
    n = len(srcs)

    def body(*refs):
        for cp in _peer_copies(mode, refs[:n], refs[n:2 * n], refs[2 * n], refs[2 * n + 1]):
            cp.start()
        me = _index(_place())
        local = []
        for w, (src, land) in enumerate(zip(refs[:n], refs[n:2 * n])):
            if mode == "gather":
                r = src.shape[0]
                local.append(pltpu.make_async_copy(src, land.at[pl.ds(pl.multiple_of(me * r, 16), r), :], refs[-1].at[w]))
            else:
                r = src.shape[0] // N_DEV
                local.append(pltpu.make_async_copy(src.at[pl.ds(pl.multiple_of(me * r, 16), r), :], land.at[me], refs[-1].at[w]))
        for cp in local:
            cp.start()
        for cp in local:
            cp.wait()
        refs[-2][...] = jnp.zeros_like(refs[-2])

    hbm, sem = pl.BlockSpec(memory_space=pltpu.HBM), pl.BlockSpec(memory_space=pltpu.SEMAPHORE)
    arrays = list(srcs) + list(lands)
    out = pl.pallas_call(
        body, name=name, scratch_shapes=[pltpu.SemaphoreType.DMA((n,))],
        out_shape=(pltpu.SemaphoreType.DMA(((N_DEV - 1) * n,)), pltpu.SemaphoreType.DMA(((N_DEV - 1) * n,)),
                   *[pltpu.HBM(a.shape, a.dtype) for a in arrays], jax.ShapeDtypeStruct((8, LANES), F32)),
        in_specs=[hbm] * (2 * n), out_specs=(sem, sem, *[hbm] * (2 * n), pl.BlockSpec(memory_space=pltpu.VMEM)),
        input_output_aliases={i: 2 + i for i in range(2 * n)},
        compiler_params=pltpu.CompilerParams(has_side_effects=pltpu.SideEffectType.DATAFLOW_SIDE_EFFECTING),
    )(*[pltpu.with_memory_space_constraint(a, pltpu.HBM) for a in arrays])
    return out[0], out[1], out[2:2 + n], out[2 + n:2 + 2 * n], out[-1]


def _exchange_wait(mode, send_sems, recv_sems, srcs, lands, after, name):
    n = len(srcs)

    def body(*refs):
        copies = _peer_copies(mode, refs[:n], refs[n:2 * n], refs[2 * n], refs[2 * n + 1])
        for cp in copies:
            cp.wait_send()
        for cp in copies:
            cp.wait_recv()

    hbm, sem = pl.BlockSpec(memory_space=pltpu.HBM), pl.BlockSpec(memory_space=pltpu.SEMAPHORE)
    arrays = list(srcs) + list(lands)
    out = pl.pallas_call(
        body, name=name, out_shape=tuple(pltpu.HBM(a.shape, a.dtype) for a in arrays),
        in_specs=[hbm] * (2 * n) + [sem, sem, pl.BlockSpec(memory_space=pl.ANY)], out_specs=tuple([hbm] * (2 * n)),
        input_output_aliases={i: i for i in range(2 * n)},
        compiler_params=pltpu.CompilerParams(has_side_effects=pltpu.SideEffectType.DATAFLOW_SIDE_EFFECTING),
    )(*arrays, send_sems, recv_sems, after)
    return out[n:]


def _allreduce_small(packed, name):
    rows = packed.shape[0]

    def body(p_ref, all_ref, tot_ref, send_sems, recv_sems):
        me = _index(_place())
        all_ref[me] = p_ref[...]

        def copy(k):
            return pltpu.make_async_remote_copy(
                src_ref=p_ref, dst_ref=all_ref.at[me], send_sem=send_sems.at[k - 1], recv_sem=recv_sems.at[k - 1],
                device_id=_peer(k), device_id_type=MESH)

        for k in range(1, N_DEV):
            copy(k).start()
        for k in range(1, N_DEV):
            copy(k).wait_recv()
        tot = all_ref[0]
        for dev in range(1, N_DEV):
            tot = tot + all_ref[dev]
        tot_ref[...] = tot
        for k in range(1, N_DEV):
            copy(k).wait_send()

    vmem = pl.BlockSpec(memory_space=pltpu.VMEM)
    return pl.pallas_call(
        body, name=name, in_specs=[vmem], out_specs=[vmem, vmem],
        out_shape=[jax.ShapeDtypeStruct((N_DEV, rows, LANES), F32), jax.ShapeDtypeStruct((rows, LANES), F32)],
        scratch_shapes=[pltpu.SemaphoreType.DMA((N_DEV - 1,)), pltpu.SemaphoreType.DMA((N_DEV - 1,))],
        compiler_params=_params(),
    )(packed)


def _sum_slabs(parts, name, tr):
    _, rows, cols = parts.shape

    def body(p_ref, o_ref):
        tot = p_ref[0].astype(F32)
        for dev in range(1, N_DEV):
            tot = tot + p_ref[dev].astype(F32)
        o_ref[...] = tot

    return pl.pallas_call(
        body, name=name, grid=(rows // tr,),
        in_specs=[pl.BlockSpec((N_DEV, tr, cols), lambda i: (0, i, 0))],
        out_specs=pl.BlockSpec((tr, cols), lambda i: (i, 0)),
        out_shape=jax.ShapeDtypeStruct((rows, cols), F32),
        compiler_params=_params(("arbitrary",)),
    )(parts)


def _adam_math(w, g, m, v):
    m = ADAM_B1 * m + (1.0 - ADAM_B1) * g
    v = ADAM_B2 * v + (1.0 - ADAM_B2) * (g * g)
    m_hat = m / (1.0 - ADAM_B1 ** ADAM_STEP)
    v_hat = v / (1.0 - ADAM_B2 ** ADAM_STEP)
    delta = -ADAM_LR * (m_hat / (jnp.sqrt(v_hat) + ADAM_EPS) + ADAM_WD * w)
    return delta, m, v


def _adam(w, g, m, v, name, tr):
    rows, cols = w.shape

    def body(w_ref, g_ref, m_ref, v_ref, d_ref, nm_ref, nv_ref):
        d_ref[...], nm_ref[...], nv_ref[...] = _adam_math(w_ref[...], g_ref[...], m_ref[...], v_ref[...])

    spec = pl.BlockSpec((tr, cols), lambda i: (i, 0))
    shape = jax.ShapeDtypeStruct((rows, cols), F32)
    return pl.pallas_call(
        body, name=name, grid=(rows // tr,), in_specs=[spec] * 4, out_specs=[spec] * 3,
        out_shape=[shape] * 3, compiler_params=_params(("arbitrary",)),
    )(w, g, m, v)


def _sum_adam(parts, w, m, v, name, tr):
    _, rows, cols = parts.shape

    def body(p_ref, w_ref, m_ref, v_ref, g_ref, d_ref, nm_ref, nv_ref):
        g = p_ref[0].astype(F32)
        for dev in range(1, N_DEV):
            g = g + p_ref[dev].astype(F32)
        g_ref[...] = g
        d_ref[...], nm_ref[...], nv_ref[...] = _adam_math(w_ref[...], g, m_ref[...], v_ref[...])

    spec = pl.BlockSpec((tr, cols), lambda i: (i, 0))
    shape = jax.ShapeDtypeStruct((rows, cols), F32)
    return pl.pallas_call(
        body, name=name, grid=(rows // tr,),
        in_specs=[pl.BlockSpec((N_DEV, tr, cols), lambda i: (0, i, 0)), spec, spec, spec],
        out_specs=[spec] * 4, out_shape=[shape] * 4, compiler_params=_params(("arbitrary",)),
    )(parts, w, m, v)


def _ada_grad_adam(c_all, dmod_cols, w, m, v, tr):
    rows, cols = w.shape

    def body(c_ref, dm_ref, w_ref, m_ref, v_ref, g_ref, d_ref, nm_ref, nv_ref):
        cv = c_ref[...]
        act = cv * jax.nn.sigmoid(cv)
        g = lax.dot_general(act, dm_ref[...], TN, preferred_element_type=F32, precision=lax.Precision.HIGHEST)
        g_ref[...] = g
        d_ref[...], nm_ref[...], nv_ref[...] = _adam_math(w_ref[...], g, m_ref[...], v_ref[...])

    spec = pl.BlockSpec((tr, cols), lambda i: (i, 0))
    shape = jax.ShapeDtypeStruct((rows, cols), F32)
    return pl.pallas_call(
        body, name="ada_grad_adam", grid=(rows // tr,),
        in_specs=[pl.BlockSpec((N_DEV, tr), lambda i: (0, i)), pl.BlockSpec((N_DEV, cols), lambda i: (0, 0)), spec, spec, spec],
        out_specs=[spec] * 4, out_shape=[shape] * 4, compiler_params=_params(("arbitrary",)),
    )(c_all, dmod_cols, w, m, v)


def _rope_tables(positions):
    s_len = positions.shape[0]
    inv_freq = ROPE_THETA ** (-jnp.arange(0, 2 * ROT_HALF, 2, dtype=F32) / (2 * ROT_HALF))
    ang = positions.astype(F32)[:, None] * inv_freq
    cos, sin = jnp.cos(ang), jnp.sin(ang)
    rest = HEAD_DIM - 2 * ROT_HALF
    zero = lambda n: jnp.zeros((s_len, n), F32)
    head = jnp.stack([jnp.concatenate([cos, cos, jnp.ones((s_len, rest), F32)], axis=1),
                      jnp.concatenate([-sin, zero(HEAD_DIM - ROT_HALF)], axis=1),
                      jnp.concatenate([zero(ROT_HALF), sin, zero(rest)], axis=1)])
    return jnp.tile(head, (1, 1, LANES // HEAD_DIM))


def _pad_rows(a, rows):
    return jnp.pad(a, ((0, rows - a.shape[0]), (0, 0)))


def _as_rows(a, rows):
    flat = a.reshape(-1)
    return jnp.pad(flat, (0, rows * LANES - flat.shape[0])).reshape(rows, LANES)


def _sequence_step(xs, target, rope, mods, gains, w_in_t, w_out_t, fetch_ffn, send_ffn_grads, w_blk_b, b_pool_r,
                   pool_scale_r, conv_w_all, conv_b):
    sh_m, sc_m, gt_m, sh_f, sc_f, gt_f = mods
    g_pre_mix, g_post_mix, g_pre_ffn, g_post_ffn = gains
    h1, u_pool, qkv = _premix_inproj(xs, sh_m, sc_m, g_pre_mix, w_in_t, rope, tm=512)
    o_g, lse_g = [], []
    for gi, dil in enumerate(DILATIONS):
        o, lse = _attn_fwd(qkv, gi, dil)
        o_g.append(o)
        lse_g.append(lse)
    x1, y1, h2, cat, attn, lse_all = _mix_out(xs, u_pool, o_g, lse_g, w_blk_b, b_pool_r, pool_scale_r, w_out_t,
                                              gt_m, g_post_mix, g_pre_ffn, sc_f, sh_f, tm=256)
    w_up_t, w_down_f = fetch_ffn(x1)
    gate, val, dy2, dout, sums_ffn, loss_loc = _ffn_fwd_loss(h2, x1, target, w_up_t, w_down_f, conv_w_all, conv_b,
                                                              gt_f, g_post_ffn, tm=256, tf=1408)

    dgc, dval, dw_down, dconv_w, dconv_b = _ffn_bwd_act(dy2, gate, val, w_down_f, conv_w_all, conv_b, tm=1024, tf=256)
    dup, dh2 = _ffn_bwd_up(dgc, dval, w_up_t, conv_w_all, tm=256)
    dw_up_t = _wgrad(dup, h2, "wgrad_up", tk=1024, tmm=1408)
    token = send_ffn_grads(dw_up_t, dw_down)
    if token is not None:
        sc_f = sc_f + token[0:1, 0:1]
    dx1, dpool, dattn, delta, dw_out_t, sums_mix = _mix_bwd(dh2, dout, x1, y1, cat, attn, w_out_t, sc_f, g_pre_ffn,
                                                           gt_m, g_post_mix, tm=256)
    du, dw_blk, sums_pool = _pool_bwd(dpool, u_pool, w_blk_b, b_pool_r, pool_scale_r, tm=512)
    dqkv = []
    for gi, dil in enumerate(DILATIONS):
        dqkv += list(_attn_bwd(qkv, dattn, lse_all, delta, gi, dil))
    dproj, grad_x, sums_in = _inproj_bwd(du, dqkv, rope, w_in_t, xs, dx1, sc_m, g_pre_mix, tm=256)
    dw_in_t = _wgrad(dproj, h1, "wgrad_in", tk=1024, tmm=1280)
    return (loss_loc, grad_x, dw_in_t, dw_out_t, dw_up_t, dw_down, dw_blk, dconv_w, dconv_b,
            sums_in, sums_mix, sums_ffn, sums_pool)


def kernel(x, c, positions, w_ada, b_ada, g_pre_mix, g_post_mix, g_pre_ffn, g_post_ffn, w_in, w_pool, b_pool, pool_scale, w_out, w_up, conv_w, conv_b, w_down, loss_target, m_w_ada, m_b_ada, m_g_pre_mix, m_g_post_mix, m_g_pre_ffn, m_g_post_ffn, m_w_in, m_w_pool, m_b_pool, m_pool_scale, m_w_out, m_w_up, m_conv_w, m_conv_b, m_w_down, v_w_ada, v_b_ada, v_g_pre_mix, v_g_post_mix, v_g_pre_ffn, v_g_post_ffn, v_w_in, v_w_pool, v_b_pool, v_pool_scale, v_w_out, v_w_up, v_conv_w, v_conv_b, v_w_down):
    s_len, d = x.shape[1], x.shape[2]
    d_ff = w_down.shape[1] * N_DEV
    me = _index(_place())
    xs, target = x[0], loss_target[0]

    ncol = w_ada.shape[2]
    b_cols = lax.dynamic_slice(b_ada, (0, me * ncol), (1, ncol))
    c_all, mod, taps_all = _ada_exchange(jnp.broadcast_to(c, (8, d)), w_ada[0], b_cols, _pad_rows(conv_w[0], 8))
    c_all = c_all[:, 0, :]
    conv_w_all = jnp.transpose(taps_all[:, :3, :], (1, 0, 2)).reshape(3, d_ff)
    sh_m, sc_m, gt_m, sh_f, sc_f, gt_f = [mod[:, 0, :].reshape(1, -1)[:, k * d:(k + 1) * d] for k in range(6)]

    w_in_t, w_out_t = _gather_weights([w_in[0].T.astype(BF16), w_out[0].T.astype(BF16)])

    rope = _rope_tables(positions[0])
    w_blk = jnp.zeros((256, 256), F32)
    for gi in range(4):
        w_blk = lax.dynamic_update_slice(w_blk, w_pool[0, gi], (gi * HEAD_DIM, gi * HEAD_DIM))
    w_blk_b = w_blk.astype(BF16)
    b_pool_r, pool_scale_r = b_pool.reshape(1, 256), pool_scale.reshape(1, 256)

    cw_rows = d_ff // LANES

    up_sh, down_sh = w_up[0].T.astype(BF16), w_down[0].astype(BF16)
    w_in_t, conv_w_all, up_sh, down_sh = lax.optimization_barrier((w_in_t, conv_w_all, up_sh, down_sh))
    lands = [lax.empty((N_DEV * s.shape[0], s.shape[1]), BF16) for s in (up_sh, down_sh)]
    w_send, w_recv, w_src, w_land, w_token = _exchange_start("gather", [up_sh, down_sh], lands, "ffn_weights_start")

    def fetch_ffn(after):
        return _exchange_wait("gather", w_send, w_recv, w_src, w_land, after, "ffn_weights_wait")

    flight = []

    def send_ffn_grads(dw_up_t, dw_down):
        lands = [lax.empty((N_DEV, g.shape[0] // N_DEV, g.shape[1]), BF16) for g in (dw_up_t, dw_down)]
        flight.extend(_exchange_start("scatter", [dw_up_t, dw_down], lands, "ffn_grads_start"))
        return flight[4]

    (loss_loc, grad_x, dw_in_t, dw_out_t, _, _, dw_blk, dconv_w, dconv_b,
     sums_in, sums_mix, sums_ffn, sums_pool) = _sequence_step(
        xs, target, rope, (sh_m + w_token[0:1, 0:1], sc_m, gt_m, sh_f, sc_f, gt_f),
        (g_pre_mix, g_post_mix, g_pre_ffn, g_post_ffn),
        w_in_t, w_out_t, fetch_ffn, send_ffn_grads, w_blk_b, b_pool_r, pool_scale_r, conv_w_all, conv_b)

    parts_ffn = _exchange_wait("scatter", *flight[:4], dw_in_t, "ffn_grads_wait")
    dw_in_t, dw_out_t, *parts_ffn = lax.optimization_barrier((dw_in_t, dw_out_t, *parts_ffn))
    parts_mix = _scatter_grads([dw_in_t, dw_out_t])
    big = {
        "w_in": [a.T for a in _sum_adam(parts_mix[0], w_in[0].T, m_w_in[0].T, v_w_in[0].T, "adam_w_in", 64)],
        "w_out": [a.T for a in _sum_adam(parts_mix[1], w_out[0].T, m_w_out[0].T, v_w_out[0].T, "adam_w_out", 128)],
        "w_up": [a.T for a in _sum_adam(parts_ffn[0], w_up[0].T, m_w_up[0].T, v_w_up[0].T, "adam_w_up", 64)],
        "w_down": _sum_adam(parts_ffn[1], w_down[0], m_w_down[0], v_w_down[0], "adam_w_down", 32),
    }

    dmod = jnp.concatenate([sums_in[0:1], sums_in[1:2], sums_mix[3:4], sums_mix[0:1], sums_mix[1:2], sums_ffn[0:1]], axis=1)
    dw_pool = jnp.stack([dw_blk[gi * HEAD_DIM:(gi + 1) * HEAD_DIM, gi * HEAD_DIM:(gi + 1) * HEAD_DIM] for gi in range(4)])
    pieces = [(dmod, 48), (sums_in[2:3], 8), (sums_mix[4:5], 8), (sums_mix[2:3], 8), (sums_ffn[1:2], 8),
              (dw_pool, 128), (sums_pool[0:1], 8), (sums_pool[1:2], 8), (dconv_b, 24),
              (dconv_w[0:1], 24), (dconv_w[1:2], 24), (dconv_w[2:3], 24), (loss_loc[0:1], 8)]
    packed = jnp.concatenate([_as_rows(a, r) for a, r in pieces], axis=0)
    gathered, total = _allreduce_small(packed, "allreduce_small")
    n_rep = 248
    rep_w = [b_ada, g_pre_mix, g_post_mix, g_pre_ffn, g_post_ffn, w_pool, b_pool, pool_scale, conv_b]
    rep_m = [m_b_ada, m_g_pre_mix, m_g_post_mix, m_g_pre_ffn, m_g_post_ffn, m_w_pool, m_b_pool, m_pool_scale, m_conv_b]
    rep_v = [v_b_ada, v_g_pre_mix, v_g_post_mix, v_g_pre_ffn, v_g_post_ffn, v_w_pool, v_b_pool, v_pool_scale, v_conv_b]
    rep_rows = [r for _, r in pieces[:9]]
    pack_rep = lambda arrs: jnp.concatenate([_as_rows(a, r) for a, r in zip(arrs, rep_rows)], axis=0)
    rep_g = total[:n_rep]
    rep_d, rep_nm, rep_nv = _adam(pack_rep(rep_w), rep_g, pack_rep(rep_m), pack_rep(rep_v), "adam_small", n_rep)

    def unpack(p):
        out, row = [], 0
        for a, r in zip(rep_w, rep_rows):
            out.append(p[row:row + r].reshape(-1)[:a.size].reshape(a.shape))
            row += r
        return out

    g_rep, d_rep, nm_rep, nv_rep = unpack(rep_g), unpack(rep_d), unpack(rep_nm), unpack(rep_nv)

    fcol = d_ff // N_DEV
    g_cw_full = jnp.concatenate([total[n_rep + 24 * k:n_rep + 24 * k + cw_rows].reshape(1, d_ff) for k in range(3)], axis=0)
    g_cw = lax.dynamic_slice(g_cw_full, (0, me * fcol), (3, fcol))
    d_cw, nm_cw, nv_cw = _adam(conv_w[0], g_cw, m_conv_w[0], v_conv_w[0], "adam_conv_w", 3)

    dmod_all = gathered[:, :48].reshape(N_DEV, 6 * d)
    dmod_cols = lax.dynamic_slice(dmod_all, (0, me * ncol), (N_DEV, ncol))
    g_ada, d_ada, nm_ada, nv_ada = _ada_grad_adam(c_all, dmod_cols, w_ada[0], m_w_ada[0], v_w_ada[0], 256)

    loss = total[n_rep + 72, 0]

    def group(k):
        rep = (g_rep, d_rep, nm_rep, nv_rep)[k]
        ada = (g_ada, d_ada, nm_ada, nv_ada)[k][None]
        cw = (g_cw, d_cw, nm_cw, nv_cw)[k][None]
        return [ada, rep[0], rep[1], rep[2], rep[3], rep[4], big["w_in"][k][None], rep[5], rep[6], rep[7],
                big["w_out"][k][None], big["w_up"][k][None], cw, rep[8], big["w_down"][k][None]]

    return (loss, grad_x[None], *group(0), *group(1), *group(2), *group(3))
```

```python
import functools
import math

import jax
import jax.numpy as jnp
from jax import lax
from jax.experimental import pallas as pl
from jax.experimental.pallas import tpu as pltpu

F32 = jnp.float32
BF16 = jnp.bfloat16
MESH = pl.DeviceIdType.MESH

N_DEV = 8
HEAD_DIM = 64
ROT_HALF = 8
ROPE_THETA = 500000.0
POOL_WINDOWS = (2, 4, 8, 16)
DILATIONS = (1, 4, 16)
BLOCK = 128
NORM_EPS = 1e-6
HALO = 16
MASKED = -1e30
ATTN_FWD_UNROLL = 4
ATTN_BWD_UNROLL = 2

ADAM_LR = 0.001
ADAM_B1 = 0.9
ADAM_B2 = 0.999
ADAM_EPS = 1e-08
ADAM_WD = 0.01
ADAM_STEP = 10

V7X_VMEM_LIMIT = 56 * 1024 * 1024
LANES = 128

NT = (((1,), (1,)), ((), ()))
NN = (((1,), (0,)), ((), ()))
TN = (((0,), (0,)), ((), ()))


def _dot(a, b, dims):
    return lax.dot_general(a, b, dims, preferred_element_type=F32)


def _params(sem=None, vmem=V7X_VMEM_LIMIT):
    if sem is None:
        return pltpu.CompilerParams(vmem_limit_bytes=vmem)
    return pltpu.CompilerParams(dimension_semantics=sem, vmem_limit_bytes=vmem)


def _rstd(v):
    return lax.rsqrt(jnp.mean(v * v, axis=-1, keepdims=True) + NORM_EPS)


def _norm_bwd(dn, n, rstd):
    return rstd * (dn - n * jnp.mean(dn * n, axis=-1, keepdims=True))


def _rope_fwd(p, rope_ref):
    return p * rope_ref[0] + pltpu.roll(p, LANES - ROT_HALF, 1) * rope_ref[1] + pltpu.roll(p, ROT_HALF, 1) * rope_ref[2]


def _rope_bwd(dp, rope_ref):
    return dp * rope_ref[0] + pltpu.roll(dp * rope_ref[1], ROT_HALF, 1) + pltpu.roll(dp * rope_ref[2], LANES - ROT_HALF, 1)


def _gelu_parts(v):
    k = math.sqrt(2.0 / math.pi)
    t = jnp.tanh(k * (v + 0.044715 * v * v * v))
    g = 0.5 * v * (1.0 + t)
    dg = 0.5 * (1.0 + t) + 0.5 * v * (1.0 - t * t) * k * (1.0 + 3.0 * 0.044715 * v * v)
    return g, dg


def _halo_before(i, tile):
    return jnp.maximum(i * (tile // HALO) - 1, 0)


def _premix_inproj(x, sh, sc, g, w_in_t, rope, tm):
    s_len, d = x.shape
    n_proj = w_in_t.shape[0]
    n_slab = (n_proj - 256) // LANES

    def body(x_ref, sh_ref, sc_ref, g_ref, w_ref, rope_ref, h_ref, up_ref, qkv_ref):
        xv = x_ref[...]
        h = (xv * _rstd(xv) * g_ref[...]) * (1.0 + sc_ref[...]) + sh_ref[...]
        hb = h.astype(BF16)
        h_ref[...] = hb
        up_ref[...] = _dot(hb, w_ref[0:256, :], NT)
        for pair in range(n_slab // 2):
            p = _dot(hb, w_ref[256 + 256 * pair:512 + 256 * pair, :], NT)
            for half in range(2):
                ph = p[:, half * LANES:(half + 1) * LANES]
                if pair < 6:
                    ph = _rope_fwd(ph, rope_ref)
                if pair < 3:
                    ph = ph * (HEAD_DIM ** -0.5)
                qkv_ref[2 * pair + half] = ph

    vec = pl.BlockSpec((1, d), lambda i: (0, 0))
    return pl.pallas_call(
        body, name="premix_inproj", grid=(s_len // tm,),
        in_specs=[pl.BlockSpec((tm, d), lambda i: (i, 0)), vec, vec, vec,
                  pl.BlockSpec((n_proj, d), lambda i: (0, 0)),
                  pl.BlockSpec((3, tm, LANES), lambda i: (0, i, 0))],
        out_specs=[pl.BlockSpec((tm, d), lambda i: (i, 0)),
                   pl.BlockSpec((tm, 256), lambda i: (i, 0)),
                   pl.BlockSpec((n_slab, tm, LANES), lambda i: (0, i, 0))],
        out_shape=[jax.ShapeDtypeStruct((s_len, d), BF16),
                   jax.ShapeDtypeStruct((s_len, 256), F32),
                   jax.ShapeDtypeStruct((n_slab, s_len, LANES), F32)],
        compiler_params=_params(("arbitrary",)),
    )(x, sh, sc, g, w_in_t, rope)


def _block_rows(n, r, dil):
    start = n * (BLOCK * dil) + r
    if dil == 1:
        return pl.ds(pl.multiple_of(start, BLOCK), BLOCK)
    return pl.ds(start, BLOCK, stride=dil)


def _band_mask(n):
    ri = lax.broadcasted_iota(jnp.int32, (BLOCK, 2 * BLOCK), 0)
    cj = lax.broadcasted_iota(jnp.int32, (BLOCK, 2 * BLOCK), 1)
    cur = (cj >= BLOCK) & (cj - BLOCK <= ri)
    prev = (cj < BLOCK) & (cj >= ri) & (n > 0)
    return cur | prev


def _attn_fwd(qkv, group, dil):
    s_len = qkv.shape[1]
    nb = s_len // (BLOCK * dil)

    def body(q_ref, k_ref, v_ref, o_ref, lse_ref):
        lane = lax.broadcasted_iota(jnp.int32, (BLOCK, LANES), 1)
        first = lane < HEAD_DIM

        def block(t, carry):
            r, n = t // nb, t % nb
            cur = _block_rows(n, r, dil)
            prev = _block_rows(jnp.maximum(n - 1, 0), r, dil)
            q = q_ref[0, cur, :]
            kcat = jnp.concatenate([k_ref[0, prev, :], k_ref[0, cur, :]], axis=0).astype(BF16)
            vcat = jnp.concatenate([v_ref[0, prev, :], v_ref[0, cur, :]], axis=0).astype(BF16)
            valid = _band_mask(n)
            q2 = jnp.concatenate([jnp.where(first, q, 0.0), jnp.where(first, 0.0, q)], axis=0).astype(BF16)
            s = jnp.where(jnp.concatenate([valid, valid], axis=0), _dot(q2, kcat, NT), MASKED)
            m = jnp.max(s, axis=-1, keepdims=True)
            p = jnp.exp(s - m)
            den = jnp.sum(p, axis=-1, keepdims=True)
            o2 = _dot(p.astype(BF16), vcat, NN) / den
            lse2 = m + jnp.log(den)
            o_ref[0, cur, :] = jnp.where(first, o2[:BLOCK], o2[BLOCK:])
            lse_ref[0, cur, :] = jnp.where(first, lse2[:BLOCK], lse2[BLOCK:])
            return carry

        lax.fori_loop(0, nb * dil, block, 0, unroll=ATTN_FWD_UNROLL)

    def slab(base):
        return pl.BlockSpec((1, s_len, LANES), lambda s: (base + 2 * group + s, 0, 0))

    out = pl.BlockSpec((1, s_len, LANES), lambda s: (s, 0, 0))
    shape = jax.ShapeDtypeStruct((2, s_len, LANES), F32)
    return pl.pallas_call(
        body, name=f"attn_fwd_d{dil}", grid=(2,),
        in_specs=[slab(0), slab(6), slab(12)], out_specs=[out, out], out_shape=[shape, shape],
        compiler_params=_params(("arbitrary",)),
    )(qkv, qkv, qkv)


def _pool_mixed(u, halo, i, tm):
    ue = jnp.concatenate([halo, u], axis=0)
    s2 = ue + pltpu.roll(ue, 1, 0)
    s4 = s2 + pltpu.roll(s2, 2, 0)
    s8 = s4 + pltpu.roll(s4, 4, 0)
    s16 = s8 + pltpu.roll(s8, 8, 0)
    grp = lax.broadcasted_iota(jnp.int32, (tm, 256), 1) // HEAD_DIM
    pick = lambda a, b, c, e: jnp.where(grp == 0, a, jnp.where(grp == 1, b, jnp.where(grp == 2, c, e)))
    win_sum = pick(s2[HALO:], s4[HALO:], s8[HALO:], s16[HALO:])
    pos = (i * tm + lax.broadcasted_iota(jnp.int32, (tm, 256), 0)).astype(F32)
    count = jnp.minimum(pos + 1.0, pick(*[float(w) for w in POOL_WINDOWS]))
    return win_sum / count - u, count


def _mix_out(x, u_pool, o_g, lse_g, w_blk, b_pool, pool_scale, w_out_t, gt_m, g_post_mix, g_pre_ffn, sc_f, sh_f, tm):
    s_len, d = x.shape

    def body(x_ref, u_ref, uh_ref, o0, o1, o2, l0, l1, l2, wb_ref, bp_ref, ps_ref, wo_ref,
             gt_ref, g1_ref, g2_ref, sc_ref, sh_ref,
             x1_ref, y1_ref, h2_ref, cat_ref, attn_ref, lall_ref):
        i = pl.program_id(0)
        u = u_ref[...]
        halo = uh_ref[...] * (i > 0).astype(F32)
        mixed, _ = _pool_mixed(u, halo, i, tm)
        y = _dot(mixed.astype(BF16), wb_ref[...], NN) + bp_ref[...]
        pool = y * ps_ref[...]
        attn = []
        for s in range(2):
            la, lb, lc = l0[s], l1[s], l2[s]
            mx = jnp.maximum(jnp.maximum(la, lb), lc)
            ea, eb, ec = jnp.exp(la - mx), jnp.exp(lb - mx), jnp.exp(lc - mx)
            den = ea + eb + ec
            lall_ref[s] = mx + jnp.log(den)
            attn.append((ea / den) * o0[s] + (eb / den) * o1[s] + (ec / den) * o2[s])
        attn = jnp.concatenate(attn, axis=1)
        attn_ref[...] = attn
        cat = jnp.concatenate([pool, attn], axis=1).astype(BF16)
        cat_ref[...] = cat
        y1 = _dot(cat, wo_ref[...], NT)
        y1_ref[...] = y1
        x1 = x_ref[...] + gt_ref[...] * (y1 * _rstd(y1) * g1_ref[...])
        x1_ref[...] = x1
        h2 = (x1 * _rstd(x1) * g2_ref[...]) * (1.0 + sc_ref[...]) + sh_ref[...]
        h2_ref[...] = h2.astype(BF16)

    tile = lambda w: pl.BlockSpec((tm, w), lambda i: (i, 0))
    slab = pl.BlockSpec((2, tm, LANES), lambda i: (0, i, 0))
    const = lambda a: pl.BlockSpec(a.shape, lambda i: (0,) * a.ndim)
    return pl.pallas_call(
        body, name="mix_out", grid=(s_len // tm,),
        in_specs=[tile(d), tile(256), pl.BlockSpec((HALO, 256), lambda i: (_halo_before(i, tm), 0)),
                  slab, slab, slab, slab, slab, slab,
                  const(w_blk), const(b_pool), const(pool_scale), const(w_out_t),
                  const(gt_m), const(g_post_mix), const(g_pre_ffn), const(sc_f), const(sh_f)],
        out_specs=[tile(d), tile(d), tile(d), tile(512), tile(256), slab],
        out_shape=[jax.ShapeDtypeStruct((s_len, d), F32), jax.ShapeDtypeStruct((s_len, d), F32),
                   jax.ShapeDtypeStruct((s_len, d), BF16), jax.ShapeDtypeStruct((s_len, 512), BF16),
                   jax.ShapeDtypeStruct((s_len, 256), F32), jax.ShapeDtypeStruct((2, s_len, LANES), F32)],
        compiler_params=_params(("arbitrary",)),
    )(x, u_pool, u_pool, *o_g, *lse_g, w_blk, b_pool, pool_scale, w_out_t, gt_m, g_post_mix, g_pre_ffn, sc_f, sh_f)


def _conv_gate(gate_ext, cw, cb):
    gc = gate_ext * cw[2:3, :] + pltpu.roll(gate_ext, 1, 0) * cw[1:2, :] + pltpu.roll(gate_ext, 2, 0) * cw[0:1, :]
    return gc[HALO:] + cb


def _ffn_fwd_loss(h2, x1, target, w_up_t, w_down, conv_w, conv_b, gt_f, g_post_ffn, tm, tf, ck):
    s_len, d = x1.shape
    d_ff = w_down.shape[0]
    n_f = d_ff // tf

    def body(h_ref, hh_ref, x1_ref, tgt_ref, wg_ref, wv_ref, wd_ref, cw_ref, cb_ref, gt_ref, g_ref,
             gate_ref, val_ref, dy2_ref, dout_ref, sums_ref, loss_ref, acc_ref):
        i, j = pl.program_id(0), pl.program_id(1)

        @pl.when((i == 0) & (j == 0))
        def _():
            sums_ref[...] = jnp.zeros_like(sums_ref)
            loss_ref[...] = jnp.zeros_like(loss_ref)

        h = h_ref[...]
        h_ext = jnp.concatenate([hh_ref[...], h], axis=0)
        row = lax.broadcasted_iota(jnp.int32, (tm + HALO, ck), 0)
        no_halo = (row < HALO) & (i == 0)
        part = None
        for c in range(tf // ck):
            cs = slice(c * ck, (c + 1) * ck)
            gate_ext = jnp.where(no_halo, 0.0, _dot(h_ext, wg_ref[cs, :], NT))
            val = _dot(h, wv_ref[cs, :], NT)
            act, _ = _gelu_parts(_conv_gate(gate_ext, cw_ref[:, cs], cb_ref[:, cs]))
            gate_ref[:, cs] = gate_ext[HALO:].astype(BF16)
            val_ref[:, cs] = val.astype(BF16)
            p = _dot((act * val).astype(BF16), wd_ref[cs, :], NN)
            part = p if part is None else part + p

        @pl.when(j == 0)
        def _():
            acc_ref[...] = part

        @pl.when(j > 0)
        def _():
            acc_ref[...] += part

        @pl.when(j == n_f - 1)
        def _():
            y2 = acc_ref[...]
            rstd = _rstd(y2)
            n = y2 * rstd
            rn = n * g_ref[...]
            err = x1_ref[...] + gt_ref[...] * rn - tgt_ref[...]
            loss_ref[...] += 0.5 * jnp.sum(jnp.mean(err * err, axis=-1, keepdims=True), axis=0, keepdims=True)
            dout = err * (1.0 / d)
            dout_ref[...] = dout
            drn = dout * gt_ref[...]
            sums_ref[0:1, :] += jnp.sum(dout * rn, axis=0, keepdims=True)
            sums_ref[1:2, :] += jnp.sum(drn * n, axis=0, keepdims=True)
            dy2_ref[...] = _norm_bwd(drn * g_ref[...], n, rstd).astype(BF16)

    tok = lambda w: pl.BlockSpec((tm, w), lambda i, j: (i, 0))
    tokf = pl.BlockSpec((tm, tf), lambda i, j: (i, j))
    vec = pl.BlockSpec((1, d), lambda i, j: (0, 0))
    once = {"pipeline_mode": pl.Buffered(1)} if n_f == 1 else {}
    return pl.pallas_call(
        body, name="ffn_fwd_loss", grid=(s_len // tm, n_f),
        in_specs=[tok(d), pl.BlockSpec((HALO, d), lambda i, j: (_halo_before(i, tm), 0)), tok(d), tok(d),
                  pl.BlockSpec((tf, d), lambda i, j: (j, 0), **once),
                  pl.BlockSpec((tf, d), lambda i, j: (j + n_f, 0), **once),
                  pl.BlockSpec((tf, d), lambda i, j: (j, 0), **once),
                  pl.BlockSpec((3, tf), lambda i, j: (0, j)), pl.BlockSpec((1, tf), lambda i, j: (0, j)), vec, vec],
        out_specs=[tokf, tokf, tok(d), tok(d), pl.BlockSpec((8, d), lambda i, j: (0, 0)),
                   pl.BlockSpec((8, LANES), lambda i, j: (0, 0))],
        out_shape=[jax.ShapeDtypeStruct((s_len, d_ff), BF16), jax.ShapeDtypeStruct((s_len, d_ff), BF16),
                   jax.ShapeDtypeStruct((s_len, d), BF16), jax.ShapeDtypeStruct((s_len, d), F32),
                   jax.ShapeDtypeStruct((8, d), F32), jax.ShapeDtypeStruct((8, LANES), F32)],
        scratch_shapes=[pltpu.VMEM((tm, d), F32)],
        compiler_params=_params(("arbitrary", "arbitrary")),
    )(h2, h2, x1, target, w_up_t, w_up_t, w_down, conv_w, conv_b, gt_f, g_post_ffn)


def _ffn_bwd_act(dy2, gate, val, w_down, conv_w, conv_b, tm, tf, ck):
    s_len, d = dy2.shape
    d_ff = w_down.shape[0]
    n_t = s_len // tm

    def body(dy_ref, g_ref, gh_ref, v_ref, wd_ref, cw_ref, cb_ref,
             dgc_ref, dval_ref, dwd_ref, dcw_ref, dcb_ref, acc_ref):
        i = pl.program_id(1)
        cw, cb, wd = cw_ref[...], cb_ref[...], wd_ref[...]
        row = lax.broadcasted_iota(jnp.int32, (ck + HALO, tf), 0)
        dwd = taps = bias = None
        for c in range(tm // ck):
            rs = slice(c * ck, (c + 1) * ck)
            halo = gh_ref[...] if c == 0 else g_ref[c * ck - HALO:c * ck, :]
            gate_ext = jnp.concatenate([halo, g_ref[rs, :]], axis=0).astype(F32)
            if c == 0:
                gate_ext = jnp.where((row < HALO) & (i == 0), 0.0, gate_ext)
            act, dact = _gelu_parts(_conv_gate(gate_ext, cw, cb))
            v = v_ref[rs, :].astype(F32)
            dy = dy_ref[rs, :]
            da = _dot(dy, wd, NT)
            dgc = da * v * dact
            dgc_ref[rs, :] = dgc.astype(BF16)
            dval_ref[rs, :] = (da * act).astype(BF16)
            dwd_c = _dot((act * v).astype(BF16), dy, TN)
            taps_c = jnp.concatenate(
                [jnp.sum(dgc * pltpu.roll(gate_ext, 2 - k, 0)[HALO:], axis=0, keepdims=True) if k < 2
                 else jnp.sum(dgc * gate_ext[HALO:], axis=0, keepdims=True) for k in range(3)], axis=0)
            bias_c = jnp.sum(dgc, axis=0, keepdims=True)
            dwd = dwd_c if c == 0 else dwd + dwd_c
            taps = taps_c if c == 0 else taps + taps_c
            bias = bias_c if c == 0 else bias + bias_c

        @pl.when(i == 0)
        def _():
            acc_ref[...] = dwd
            dcw_ref[...] = taps
            dcb_ref[...] = bias

        @pl.when(i > 0)
        def _():
            acc_ref[...] += dwd
            dcw_ref[...] += taps
            dcb_ref[...] += bias

        @pl.when(i == n_t - 1)
        def _():
            dwd_ref[...] = acc_ref[...].astype(BF16)

    tokf = pl.BlockSpec((tm, tf), lambda j, i: (i, j))
    return pl.pallas_call(
        body, name="ffn_bwd_act", grid=(d_ff // tf, n_t),
        in_specs=[pl.BlockSpec((tm, d), lambda j, i: (i, 0)), tokf,
                  pl.BlockSpec((HALO, tf), lambda j, i: (_halo_before(i, tm), j)), tokf,
                  pl.BlockSpec((tf, d), lambda j, i: (j, 0)),
                  pl.BlockSpec((3, tf), lambda j, i: (0, j)), pl.BlockSpec((1, tf), lambda j, i: (0, j))],
        out_specs=[tokf, tokf, pl.BlockSpec((tf, d), lambda j, i: (j, 0)),
                   pl.BlockSpec((3, tf), lambda j, i: (0, j)), pl.BlockSpec((1, tf), lambda j, i: (0, j))],
        out_shape=[jax.ShapeDtypeStruct((s_len, d_ff), BF16), jax.ShapeDtypeStruct((s_len, d_ff), BF16),
                   jax.ShapeDtypeStruct((d_ff, d), BF16), jax.ShapeDtypeStruct((3, d_ff), F32),
                   jax.ShapeDtypeStruct((1, d_ff), F32)],
        scratch_shapes=[pltpu.VMEM((tf, d), F32)],
        compiler_params=_params(("arbitrary", "arbitrary")),
    )(dy2, gate, gate, val, w_down, conv_w, conv_b)


def _ffn_bwd_up(dgc, dval, w_up_t, conv_w, tm):
    s_len, d_ff = dgc.shape
    d = w_up_t.shape[1]
    n_t = s_len // tm

    def body(dg_ref, dgn_ref, dv_ref, cw_ref, w_ref, dup_ref, dh_ref):
        i = pl.program_id(0)
        nxt = dgn_ref[...].astype(F32) * (i < n_t - 1).astype(F32)
        ext = jnp.concatenate([dg_ref[...].astype(F32), nxt], axis=0)
        rows = tm + HALO
        dgate = (ext * cw_ref[2:3, :] + pltpu.roll(ext, rows - 1, 0) * cw_ref[1:2, :]
                 + pltpu.roll(ext, rows - 2, 0) * cw_ref[0:1, :])[:tm]
        dup = jnp.concatenate([dgate.astype(BF16), dv_ref[...]], axis=1)
        dup_ref[...] = dup
        dh_ref[...] = _dot(dup, w_ref[...], NN)

    tokf = pl.BlockSpec((tm, d_ff), lambda i: (i, 0))
    return pl.pallas_call(
        body, name="ffn_bwd_up", grid=(n_t,),
        in_specs=[tokf, pl.BlockSpec((HALO, d_ff), lambda i: (jnp.minimum((i + 1) * (tm // HALO), s_len // HALO - 1), 0)),
                  tokf, pl.BlockSpec((3, d_ff), lambda i: (0, 0)), pl.BlockSpec((2 * d_ff, d), lambda i: (0, 0))],
        out_specs=[pl.BlockSpec((tm, 2 * d_ff), lambda i: (i, 0)), pl.BlockSpec((tm, d), lambda i: (i, 0))],
        out_shape=[jax.ShapeDtypeStruct((s_len, 2 * d_ff), BF16), jax.ShapeDtypeStruct((s_len, d), F32)],
        compiler_params=_params(("arbitrary",)),
    )(dgc, dgc, dval, conv_w, w_up_t)


def _mix_bwd(dh2, dout, x1, y1, cat, attn, w_out_t, sc_f, g_pre_ffn, gt_m, g_post_mix, tm):
    s_len, d = x1.shape
    n_t = s_len // tm

    def body(dh_ref, do_ref, x1_ref, y1_ref, cat_ref, at_ref, wo_ref, sc_ref, g2_ref, gt_ref, g1_ref,
             dx1_ref, dpool_ref, dattn_ref, delta_ref, dwo_ref, sums_ref, acc_ref):
        i = pl.program_id(0)
        dh = dh_ref[...]
        x1 = x1_ref[...]
        r2 = _rstd(x1)
        n2 = x1 * r2
        ng = n2 * g2_ref[...]
        dng = dh * (1.0 + sc_ref[...])
        dx1 = do_ref[...] + _norm_bwd(dng * g2_ref[...], n2, r2)
        dx1_ref[...] = dx1
        y1 = y1_ref[...]
        r1 = _rstd(y1)
        n1 = y1 * r1
        drn = dx1 * gt_ref[...]
        dy1 = _norm_bwd(drn * g1_ref[...], n1, r1).astype(BF16)
        dcat = _dot(dy1, wo_ref[...], NN)
        dpool_ref[...] = dcat[:, 0:256]
        lane = lax.broadcasted_iota(jnp.int32, (tm, LANES), 1)
        first = lane < HEAD_DIM
        for s in range(2):
            da = dcat[:, 256 + s * LANES:256 + (s + 1) * LANES]
            dattn_ref[s] = da
            prod = da * at_ref[:, s * LANES:(s + 1) * LANES]
            tot = jnp.sum(prod, axis=-1, keepdims=True)
            lo = jnp.sum(jnp.where(first, prod, 0.0), axis=-1, keepdims=True)
            delta_ref[s] = jnp.where(first, lo, tot - lo)
        dwo = _dot(dy1, cat_ref[...], TN)
        sums = jnp.concatenate(
            [jnp.sum(dh, axis=0, keepdims=True), jnp.sum(dh * ng, axis=0, keepdims=True),
             jnp.sum(dng * n2, axis=0, keepdims=True), jnp.sum(dx1 * (n1 * g1_ref[...]), axis=0, keepdims=True),
             jnp.sum(drn * n1, axis=0, keepdims=True), jnp.zeros((3, d), F32)], axis=0)

        @pl.when(i == 0)
        def _():
            acc_ref[...] = dwo
            sums_ref[...] = sums

        @pl.when(i > 0)
        def _():
            acc_ref[...] += dwo
            sums_ref[...] += sums

        @pl.when(i == n_t - 1)
        def _():
            dwo_ref[...] = acc_ref[...].astype(BF16)

    tile = lambda w: pl.BlockSpec((tm, w), lambda i: (i, 0))
    slab = pl.BlockSpec((2, tm, LANES), lambda i: (0, i, 0))
    vec = pl.BlockSpec((1, d), lambda i: (0, 0))
    return pl.pallas_call(
        body, name="mix_bwd", grid=(n_t,),
        in_specs=[tile(d), tile(d), tile(d), tile(d), tile(512), tile(256),
                  pl.BlockSpec((d, 512), lambda i: (0, 0)), vec, vec, vec, vec],
        out_specs=[tile(d), tile(256), slab, slab, pl.BlockSpec((d, 512), lambda i: (0, 0)),
                   pl.BlockSpec((8, d), lambda i: (0, 0))],
        out_shape=[jax.ShapeDtypeStruct((s_len, d), F32), jax.ShapeDtypeStruct((s_len, 256), F32),
                   jax.ShapeDtypeStruct((2, s_len, LANES), F32), jax.ShapeDtypeStruct((2, s_len, LANES), F32),
                   jax.ShapeDtypeStruct((d, 512), BF16), jax.ShapeDtypeStruct((8, d), F32)],
        scratch_shapes=[pltpu.VMEM((d, 512), F32)],
        compiler_params=_params(("arbitrary",)),
    )(dh2, dout, x1, y1, cat, attn, w_out_t, sc_f, g_pre_ffn, gt_m, g_post_mix)


def _pool_bwd(dpool, u_pool, w_blk, b_pool, pool_scale, tm):
    s_len = dpool.shape[0]
    n_t = s_len // tm

    def body(dp_ref, dpn_ref, u_ref, uh_ref, wb_ref, bp_ref, ps_ref, du_ref, dwb_ref, sums_ref):
        i = pl.program_id(0)
        u = u_ref[...]
        mixed, _ = _pool_mixed(u, uh_ref[...] * (i > 0).astype(F32), i, tm)
        mixed_b = mixed.astype(BF16)
        y = _dot(mixed_b, wb_ref[...], NN) + bp_ref[...]
        dp = dp_ref[...]
        dy = dp * ps_ref[...]
        dwb = _dot(mixed_b, dy.astype(BF16), TN)
        sums = jnp.concatenate([jnp.sum(dy, axis=0, keepdims=True), jnp.sum(dp * y, axis=0, keepdims=True),
                                jnp.zeros((6, 256), F32)], axis=0)
        dp_ext = jnp.concatenate([dp, dpn_ref[...] * (i < n_t - 1).astype(F32)], axis=0)
        dmix = _dot((dp_ext * ps_ref[...]).astype(BF16), wb_ref[...], NT)
        rows = tm + HALO
        grp = lax.broadcasted_iota(jnp.int32, (rows, 256), 1) // HEAD_DIM
        pick = lambda a, b, c, e: jnp.where(grp == 0, a, jnp.where(grp == 1, b, jnp.where(grp == 2, c, e)))
        pos = (i * tm + lax.broadcasted_iota(jnp.int32, (rows, 256), 0)).astype(F32)
        z = dmix / jnp.minimum(pos + 1.0, pick(*[float(w) for w in POOL_WINDOWS]))
        f2 = z + pltpu.roll(z, rows - 1, 0)
        f4 = f2 + pltpu.roll(f2, rows - 2, 0)
        f8 = f4 + pltpu.roll(f4, rows - 4, 0)
        f16 = f8 + pltpu.roll(f8, rows - 8, 0)
        du_ref[...] = (pick(f2, f4, f8, f16) - dmix)[:tm]

        @pl.when(i == 0)
        def _():
            dwb_ref[...] = dwb
            sums_ref[...] = sums

        @pl.when(i > 0)
        def _():
            dwb_ref[...] += dwb
            sums_ref[...] += sums

    tile = pl.BlockSpec((tm, 256), lambda i: (i, 0))
    const = lambda a: pl.BlockSpec(a.shape, lambda i: (0,) * a.ndim)
    return pl.pallas_call(
        body, name="pool_bwd", grid=(n_t,),
        in_specs=[tile, pl.BlockSpec((HALO, 256), lambda i: (jnp.minimum((i + 1) * (tm // HALO), s_len // HALO - 1), 0)),
                  tile, pl.BlockSpec((HALO, 256), lambda i: (_halo_before(i, tm), 0)),
                  const(w_blk), const(b_pool), const(pool_scale)],
        out_specs=[tile, pl.BlockSpec((256, 256), lambda i: (0, 0)), pl.BlockSpec((8, 256), lambda i: (0, 0))],
        out_shape=[jax.ShapeDtypeStruct((s_len, 256), F32), jax.ShapeDtypeStruct((256, 256), F32),
                   jax.ShapeDtypeStruct((8, 256), F32)],
        compiler_params=_params(("arbitrary",)),
    )(dpool, dpool, u_pool, u_pool, w_blk, b_pool, pool_scale)


def _attn_bwd(qkv, dattn, lse_all, delta, group, dil):
    s_len = qkv.shape[1]
    nb = s_len // (BLOCK * dil)

    def body(q_ref, k_ref, v_ref, do_ref, l_ref, dl_ref, dq_ref, dk_ref, dv_ref):
        lane = lax.broadcasted_iota(jnp.int32, (BLOCK, LANES), 1)
        first = lane < HEAD_DIM

        def block(t, carry):
            dk_part, dv_part = carry
            r, n = t // nb, t % nb
            cur = _block_rows(n, r, dil)
            prev = _block_rows(jnp.maximum(n - 1, 0), r, dil)
            q = q_ref[0, cur, :]
            do = do_ref[0, cur, :]
            lse = l_ref[0, cur, :]
            dlt = dl_ref[0, cur, :]
            kcat = jnp.concatenate([k_ref[0, prev, :], k_ref[0, cur, :]], axis=0).astype(BF16)
            vcat = jnp.concatenate([v_ref[0, prev, :], v_ref[0, cur, :]], axis=0).astype(BF16)
            valid = _band_mask(n)
            stack = lambda a: jnp.concatenate([jnp.where(first, a, 0.0), jnp.where(first, 0.0, a)], axis=0)
            rows2 = lambda a: jnp.concatenate([a[:, 0:1], a[:, HEAD_DIM:HEAD_DIM + 1]], axis=0)
            q2, do2 = stack(q).astype(BF16), stack(do).astype(BF16)
            valid2 = jnp.concatenate([valid, valid], axis=0)
            p = jnp.where(valid2, jnp.exp(_dot(q2, kcat, NT) - rows2(lse)), 0.0)
            ds = (p * (_dot(do2, vcat, NT) - rows2(dlt))).astype(BF16)
            dq2 = _dot(ds, kcat, NN)
            dq_ref[0, cur, :] = jnp.where(first, dq2[:BLOCK], dq2[BLOCK:])
            dkc = _dot(ds, q2, TN)
            dvc = _dot(p.astype(BF16), do2, TN)
            dk_ref[0, prev, :] = dk_part + dkc[:BLOCK]
            dv_ref[0, prev, :] = dv_part + dvc[:BLOCK]
            dk_ref[0, cur, :] = dkc[BLOCK:]
            dv_ref[0, cur, :] = dvc[BLOCK:]
            return dkc[BLOCK:], dvc[BLOCK:]

        def blocks(tt, carry):
            for u in range(ATTN_BWD_UNROLL):
                carry = block(tt * ATTN_BWD_UNROLL + u, carry)
            return carry

        zero = jnp.zeros((BLOCK, LANES), F32)
        lax.fori_loop(0, nb * dil // ATTN_BWD_UNROLL, blocks, (zero, zero))

    def slab(base):
        return pl.BlockSpec((1, s_len, LANES), lambda s: (base + 2 * group + s, 0, 0))

    one = pl.BlockSpec((1, s_len, LANES), lambda s: (s, 0, 0))
    shape = jax.ShapeDtypeStruct((2, s_len, LANES), F32)
    return pl.pallas_call(
        body, name=f"attn_bwd_d{dil}", grid=(2,),
        in_specs=[slab(0), slab(6), slab(12), one, one, one],
        out_specs=[one, one, one], out_shape=[shape, shape, shape],
        compiler_params=_params(("arbitrary",)),
    )(qkv, qkv, qkv, dattn, lse_all, delta)


def _inproj_bwd(du, dqkv, rope, w_in_t, x, dx1, sc_m, g_pre_mix, tm):
    s_len, d = x.shape
    n_proj = w_in_t.shape[0]
    n_t = s_len // tm

    def body(du_ref, *refs):
        dref = refs[:9]
        rope_ref, w_ref, x_ref, dx1_ref, sc_ref, g_ref, dproj_ref, dx_ref, sums_ref = refs[9:]
        i = pl.program_id(0)
        cols = [du_ref[...].astype(BF16)]
        for kind in range(3):
            for grp in range(3):
                for s in range(2):
                    piece = dref[3 * grp + kind][s]
                    if kind < 2:
                        piece = _rope_bwd(piece, rope_ref)
                    if kind == 0:
                        piece = piece * (HEAD_DIM ** -0.5)
                    cols.append(piece.astype(BF16))
        dproj = jnp.concatenate(cols, axis=1)
        dproj_ref[...] = dproj
        dh = _dot(dproj, w_ref[...], NN)
        xv = x_ref[...]
        r = _rstd(xv)
        n = xv * r
        dng = dh * (1.0 + sc_ref[...])
        dx_ref[...] = dx1_ref[...] + _norm_bwd(dng * g_ref[...], n, r)
        sums = jnp.concatenate([jnp.sum(dh, axis=0, keepdims=True), jnp.sum(dh * (n * g_ref[...]), axis=0, keepdims=True),
                                jnp.sum(dng * n, axis=0, keepdims=True), jnp.zeros((5, d), F32)], axis=0)

        @pl.when(i == 0)
        def _():
            sums_ref[...] = sums

        @pl.when(i > 0)
        def _():
            sums_ref[...] += sums

    tile = lambda w: pl.BlockSpec((tm, w), lambda i: (i, 0))
    slab = pl.BlockSpec((2, tm, LANES), lambda i: (0, i, 0))
    vec = pl.BlockSpec((1, d), lambda i: (0, 0))
    return pl.pallas_call(
        body, name="inproj_bwd", grid=(n_t,),
        in_specs=[tile(256)] + [slab] * 9 + [pl.BlockSpec((3, tm, LANES), lambda i: (0, i, 0)),
                                             pl.BlockSpec((n_proj, d), lambda i: (0, 0)), tile(d), tile(d), vec, vec],
        out_specs=[tile(n_proj), tile(d), pl.BlockSpec((8, d), lambda i: (0, 0))],
        out_shape=[jax.ShapeDtypeStruct((s_len, n_proj), BF16), jax.ShapeDtypeStruct((s_len, d), F32),
                   jax.ShapeDtypeStruct((8, d), F32)],
        compiler_params=_params(("arbitrary",)),
    )(du, *dqkv, rope, w_in_t, x, dx1, sc_m, g_pre_mix)


def _wgrad(a, b, name, tk, tmm):
    s_len, m = a.shape
    n = b.shape[1]
    n_k = s_len // tk

    def body(a_ref, b_ref, o_ref, acc_ref):
        k = pl.program_id(1)
        part = _dot(a_ref[...], b_ref[...], TN)

        @pl.when(k == 0)
        def _():
            acc_ref[...] = part

        @pl.when(k > 0)
        def _():
            acc_ref[...] += part

        @pl.when(k == n_k - 1)
        def _():
            o_ref[...] = acc_ref[...].astype(BF16)

    return pl.pallas_call(
        body, name=name, grid=(m // tmm, n_k),
        in_specs=[pl.BlockSpec((tk, tmm), lambda j, k: (k, j)), pl.BlockSpec((tk, n), lambda j, k: (k, 0))],
        out_specs=pl.BlockSpec((tmm, n), lambda j, k: (j, 0)),
        out_shape=jax.ShapeDtypeStruct((m, n), BF16),
        scratch_shapes=[pltpu.VMEM((tmm, n), F32)],
        compiler_params=_params(("arbitrary", "arbitrary")),
    )(a, b)


def _place():
    return lax.axis_index("x"), lax.axis_index("y"), lax.axis_index("c")


def _peer(k):
    x, y, c = _place()
    bx, by, bc = (k >> 2) & 1, (k >> 1) & 1, k & 1
    return (x ^ bx if bx else x, y ^ by if by else y, c ^ bc if bc else c)


def _index(pos):
    return 4 * pos[0] + 2 * pos[1] + pos[2]


def _ada_exchange(c_rows, w_ada, b_ada_cols, taps):
    d = c_rows.shape[1]
    ncol = w_ada.shape[1]

    def body(c_ref, w_ref, b_ref, t_ref, call_ref, mod_ref, tall_ref, stage_ref, send_sems, recv_sems):
        me = _index(_place())
        call_ref[me] = c_ref[...]
        tall_ref[me] = t_ref[...]

        def gather(k):
            return pltpu.make_async_remote_copy(
                src_ref=c_ref, dst_ref=call_ref.at[me], send_sem=send_sems.at[0, k - 1], recv_sem=recv_sems.at[0, k - 1],
                device_id=_peer(k), device_id_type=MESH)

        def gather_taps(k):
            return pltpu.make_async_remote_copy(
                src_ref=t_ref, dst_ref=tall_ref.at[me], send_sem=send_sems.at[2, k - 1], recv_sem=recv_sems.at[2, k - 1],
                device_id=_peer(k), device_id_type=MESH)

        for k in range(1, N_DEV):
            gather(k).start()
        for k in range(1, N_DEV):
            gather_taps(k).start()
        for k in range(1, N_DEV):
            gather(k).wait_recv()
        cv = jnp.concatenate([call_ref[b, 0:1, :] for b in range(N_DEV)], axis=0)
        act = cv * jax.nn.sigmoid(cv)
        mod = lax.dot_general(act, w_ref[...], NN, preferred_element_type=F32,
                              precision=lax.Precision.HIGHEST) + b_ref[...]
        for b in range(N_DEV):
            stage_ref[b] = jnp.broadcast_to(mod[b:b + 1, :], (8, ncol))
        mod_ref[me] = stage_ref[me]

        def scatter(k):
            return pltpu.make_async_remote_copy(
                src_ref=stage_ref.at[_index(_peer(k))], dst_ref=mod_ref.at[me],
                send_sem=send_sems.at[1, k - 1], recv_sem=recv_sems.at[1, k - 1],
                device_id=_peer(k), device_id_type=MESH)

        for k in range(1, N_DEV):
            scatter(k).start()
        for k in range(1, N_DEV):
            scatter(k).wait_recv()
        for k in range(1, N_DEV):
            gather_taps(k).wait_recv()
        for k in range(1, N_DEV):
            gather(k).wait_send()
            scatter(k).wait_send()
            gather_taps(k).wait_send()

    vmem = pl.BlockSpec(memory_space=pltpu.VMEM)
    return pl.pallas_call(
        body, name="ada_exchange",
        in_specs=[vmem] * 4, out_specs=[vmem] * 3,
        out_shape=[jax.ShapeDtypeStruct((N_DEV, 8, d), F32), jax.ShapeDtypeStruct((N_DEV, 8, ncol), F32),
                   jax.ShapeDtypeStruct((N_DEV,) + taps.shape, F32)],
        scratch_shapes=[pltpu.VMEM((N_DEV, 8, ncol), F32), pltpu.SemaphoreType.DMA((3, N_DEV - 1)),
                        pltpu.SemaphoreType.DMA((3, N_DEV - 1))],
        compiler_params=_params(),
    )(c_rows, w_ada, b_ada_cols, taps)


def _gather_weights(shards):
    n_w = len(shards)

    def body(*refs):
        srcs, outs = refs[:n_w], refs[n_w:2 * n_w]
        send_sems, recv_sems, local_sems = refs[2 * n_w:]
        x, y, c = _place()
        me, sibling = (x, y, c), (x, y, 1 - c)
        chips = [(1 - x, y), (x, 1 - y), (1 - x, 1 - y)]

        def rows(w, pos):
            r = shards[w].shape[0]
            return outs[w].at[pl.ds(pl.multiple_of(_index(pos) * r, 16), r), :]

        def copy(k, w, block, to, own=False):
            return pltpu.make_async_remote_copy(
                src_ref=srcs[w] if own else rows(w, block), dst_ref=rows(w, block),
                send_sem=send_sems.at[k, w], recv_sem=recv_sems.at[k, w], device_id=to, device_id_type=MESH)

        mine = [pltpu.make_async_copy(srcs[w], rows(w, me), local_sems.at[w]) for w in range(n_w)]
        for cp in mine:
            cp.start()
        first = [copy(0, w, me, sibling, own=True) for w in range(n_w)]
        first += [copy(1 + j, w, me, (*chip, c), own=True) for j, chip in enumerate(chips) for w in range(n_w)]
        for cp in first:
            cp.start()
        passed = []
        for j, chip in enumerate(chips):
            for w in range(n_w):
                copy(1 + j, w, (*chip, c), me).wait_recv()
                fwd = copy(4 + j, w, (*chip, c), sibling)
                fwd.start()
                passed.append(fwd)
        for w in range(n_w):
            copy(0, w, sibling, me).wait_recv()
        for j, chip in enumerate(chips):
            for w in range(n_w):
                copy(4 + j, w, (*chip, 1 - c), me).wait_recv()
        for cp in first + passed:
            cp.wait_send()
        for cp in mine:
            cp.wait()

    hbm = pl.BlockSpec(memory_space=pltpu.HBM)
    return pl.pallas_call(
        body, name="gather_weights",
        in_specs=[hbm] * n_w, out_specs=[hbm] * n_w,
        out_shape=[jax.ShapeDtypeStruct((N_DEV * s.shape[0], s.shape[1]), s.dtype) for s in shards],
        scratch_shapes=[pltpu.SemaphoreType.DMA((N_DEV - 1, n_w)), pltpu.SemaphoreType.DMA((N_DEV - 1, n_w)),
                        pltpu.SemaphoreType.DMA((n_w,))],
        compiler_params=_params(),
    )(*shards)


def _scatter_grads(grads):
    n_w = len(grads)

    def body(*refs):
        srcs, outs = refs[:n_w], refs[n_w:2 * n_w]
        send_sems, recv_sems, local_sems = refs[2 * n_w:]
        me = _index(_place())

        def slab(w, dev):
            r = grads[w].shape[0] // N_DEV
            return srcs[w].at[pl.ds(pl.multiple_of(dev * r, 16), r), :]

        def copy(k, w):
            return pltpu.make_async_remote_copy(
                src_ref=slab(w, _index(_peer(k))), dst_ref=outs[w].at[me],
                send_sem=send_sems.at[k - 1, w], recv_sem=recv_sems.at[k - 1, w],
                device_id=_peer(k), device_id_type=MESH)

        mine = [pltpu.make_async_copy(slab(w, me), outs[w].at[me], local_sems.at[w]) for w in range(n_w)]
        for cp in mine:
            cp.start()
        sends = [copy(k, w) for k in range(1, N_DEV) for w in range(n_w)]
        for cp in sends:
            cp.start()
        for cp in sends:
            cp.wait_recv()
        for cp in sends:
            cp.wait_send()
        for cp in mine:
            cp.wait()

    hbm = pl.BlockSpec(memory_space=pltpu.HBM)
    return pl.pallas_call(
        body, name="scatter_grads",
        in_specs=[hbm] * n_w, out_specs=[hbm] * n_w,
        out_shape=[jax.ShapeDtypeStruct((N_DEV, g.shape[0] // N_DEV, g.shape[1]), g.dtype) for g in grads],
        scratch_shapes=[pltpu.SemaphoreType.DMA((N_DEV - 1, n_w)), pltpu.SemaphoreType.DMA((N_DEV - 1, n_w)),
                        pltpu.SemaphoreType.DMA((n_w,))],
        compiler_params=_params(),
    )(*grads)


def _peer_copies(mode, srcs, lands, send_sems, recv_sems):
    me = _index(_place())
    copies = []
    for k in range(1, N_DEV):
        peer = _peer(k)
        for w, (src, land) in enumerate(zip(srcs, lands)):
            if mode == "gather":
                r = src.shape[0]
                dst = land.at[pl.ds(pl.multiple_of(me * r, 16), r), :]
            else:
                r = src.shape[0] // N_DEV
                src = src.at[pl.ds(pl.multiple_of(_index(peer) * r, 16), r), :]
                dst = land.at[me]
            copies.append(pltpu.make_async_remote_copy(
                src_ref=src, dst_ref=dst, send_sem=send_sems.at[(k - 1) * len(srcs) + w],
                recv_sem=recv_sems.at[(k - 1) * len(srcs) + w],
                device_id=peer, device_id_type=MESH))
    return copies


def _exchange_start(mode, srcs, lands, name):
    n = len(srcs)

    def body(*refs):
        me = _index(_place())
        local = []
        for w, (src, land) in enumerate(zip(refs[:n], refs[n:2 * n])):
            if mode == "gather":
                r = src.shape[0]
                local.append(pltpu.make_async_copy(src, land.at[pl.ds(pl.multiple_of(me * r, 16), r), :], refs[-1].at[w]))
            else:
                r = src.shape[0] // N_DEV
                local.append(pltpu.make_async_copy(src.at[pl.ds(pl.multiple_of(me * r, 16), r), :], land.at[me], refs[-1].at[w]))
        for cp in local:
            cp.start()
        for cp in local:
            cp.wait()
        for cp in _peer_copies(mode, refs[:n], refs[n:2 * n], refs[2 * n], refs[2 * n + 1]):
            cp.start()
        refs[-2][...] = jnp.zeros_like(refs[-2])

    hbm, sem = pl.BlockSpec(memory_space=pltpu.HBM), pl.BlockSpec(memory_space=pltpu.SEMAPHORE)
    arrays = list(srcs) + list(lands)
    out = pl.pallas_call(
        body, name=name, scratch_shapes=[pltpu.SemaphoreType.DMA((n,))],
        out_shape=(pltpu.SemaphoreType.DMA(((N_DEV - 1) * n,)), pltpu.SemaphoreType.DMA(((N_DEV - 1) * n,)),
                   *[pltpu.HBM(a.shape, a.dtype) for a in arrays], jax.ShapeDtypeStruct((8, LANES), F32)),
        in_specs=[hbm] * (2 * n), out_specs=(sem, sem, *[hbm] * (2 * n), pl.BlockSpec(memory_space=pltpu.VMEM)),
        input_output_aliases={i: 2 + i for i in range(2 * n)},
        compiler_params=pltpu.CompilerParams(has_side_effects=pltpu.SideEffectType.DATAFLOW_SIDE_EFFECTING),
    )(*[pltpu.with_memory_space_constraint(a, pltpu.HBM) for a in arrays])
    return out[0], out[1], out[2:2 + n], out[2 + n:2 + 2 * n], out[-1]


def _exchange_wait(mode, send_sems, recv_sems, srcs, lands, after, name):
    n = len(srcs)

    def body(*refs):
        copies = _peer_copies(mode, refs[:n], refs[n:2 * n], refs[2 * n], refs[2 * n + 1])
        for cp in copies:
            cp.wait_send()
        for cp in copies:
            cp.wait_recv()

    hbm, sem = pl.BlockSpec(memory_space=pltpu.HBM), pl.BlockSpec(memory_space=pltpu.SEMAPHORE)
    arrays = list(srcs) + list(lands)
    out = pl.pallas_call(
        body, name=name, out_shape=tuple(pltpu.HBM(a.shape, a.dtype) for a in arrays),
        in_specs=[hbm] * (2 * n) + [sem, sem, pl.BlockSpec(memory_space=pl.ANY)], out_specs=tuple([hbm] * (2 * n)),
        input_output_aliases={i: i for i in range(2 * n)},
        compiler_params=pltpu.CompilerParams(has_side_effects=pltpu.SideEffectType.DATAFLOW_SIDE_EFFECTING),
    )(*arrays, send_sems, recv_sems, after)
    return out[n:]


def _allreduce_small(packed, name):
    rows = packed.shape[0]

    def body(p_ref, all_ref, tot_ref, send_sems, recv_sems):
        me = _index(_place())
        all_ref[me] = p_ref[...]

        def copy(k):
            return pltpu.make_async_remote_copy(
                src_ref=p_ref, dst_ref=all_ref.at[me], send_sem=send_sems.at[k - 1], recv_sem=recv_sems.at[k - 1],
                device_id=_peer(k), device_id_type=MESH)

        for k in range(1, N_DEV):
            copy(k).start()
        for k in range(1, N_DEV):
            copy(k).wait_recv()
        tot = all_ref[0]
        for dev in range(1, N_DEV):
            tot = tot + all_ref[dev]
        tot_ref[...] = tot
        for k in range(1, N_DEV):
            copy(k).wait_send()

    vmem = pl.BlockSpec(memory_space=pltpu.VMEM)
    return pl.pallas_call(
        body, name=name, in_specs=[vmem], out_specs=[vmem, vmem],
        out_shape=[jax.ShapeDtypeStruct((N_DEV, rows, LANES), F32), jax.ShapeDtypeStruct((rows, LANES), F32)],
        scratch_shapes=[pltpu.SemaphoreType.DMA((N_DEV - 1,)), pltpu.SemaphoreType.DMA((N_DEV - 1,))],
        compiler_params=_params(),
    )(packed)


def _adam_math(w, g, m, v):
    m = ADAM_B1 * m + (1.0 - ADAM_B1) * g
    v = ADAM_B2 * v + (1.0 - ADAM_B2) * (g * g)
    m_hat = m / (1.0 - ADAM_B1 ** ADAM_STEP)
    v_hat = v / (1.0 - ADAM_B2 ** ADAM_STEP)
    delta = -ADAM_LR * (m_hat / (jnp.sqrt(v_hat) + ADAM_EPS) + ADAM_WD * w)
    return delta, m, v


def _adam(w, g, m, v, name, tr):
    rows, cols = w.shape

    def body(w_ref, g_ref, m_ref, v_ref, d_ref, nm_ref, nv_ref):
        d_ref[...], nm_ref[...], nv_ref[...] = _adam_math(w_ref[...], g_ref[...], m_ref[...], v_ref[...])

    spec = pl.BlockSpec((tr, cols), lambda i: (i, 0))
    shape = jax.ShapeDtypeStruct((rows, cols), F32)
    return pl.pallas_call(
        body, name=name, grid=(rows // tr,), in_specs=[spec] * 4, out_specs=[spec] * 3,
        out_shape=[shape] * 3, compiler_params=_params(("arbitrary",)),
    )(w, g, m, v)


def _sum_adam(parts, w, m, v, name, tr):
    _, rows, cols = parts.shape

    def body(p_ref, w_ref, m_ref, v_ref, g_ref, d_ref, nm_ref, nv_ref):
        g = p_ref[0].astype(F32)
        for dev in range(1, N_DEV):
            g = g + p_ref[dev].astype(F32)
        g_ref[...] = g
        d_ref[...], nm_ref[...], nv_ref[...] = _adam_math(w_ref[...], g, m_ref[...], v_ref[...])

    spec = pl.BlockSpec((tr, cols), lambda i: (i, 0))
    shape = jax.ShapeDtypeStruct((rows, cols), F32)
    return pl.pallas_call(
        body, name=name, grid=(rows // tr,),
        in_specs=[pl.BlockSpec((N_DEV, tr, cols), lambda i: (0, i, 0)), spec, spec, spec],
        out_specs=[spec] * 4, out_shape=[shape] * 4, compiler_params=_params(("arbitrary",)),
    )(parts, w, m, v)


def _ada_grad_adam(c_all, dmod_cols, w, m, v, tr):
    rows, cols = w.shape

    def body(c_ref, dm_ref, w_ref, m_ref, v_ref, g_ref, d_ref, nm_ref, nv_ref):
        cv = c_ref[...]
        act = cv * jax.nn.sigmoid(cv)
        g = lax.dot_general(act, dm_ref[...], TN, preferred_element_type=F32, precision=lax.Precision.HIGHEST)
        g_ref[...] = g
        d_ref[...], nm_ref[...], nv_ref[...] = _adam_math(w_ref[...], g, m_ref[...], v_ref[...])

    spec = pl.BlockSpec((tr, cols), lambda i: (i, 0))
    shape = jax.ShapeDtypeStruct((rows, cols), F32)
    return pl.pallas_call(
        body, name="ada_grad_adam", grid=(rows // tr,),
        in_specs=[pl.BlockSpec((N_DEV, tr), lambda i: (0, i)), pl.BlockSpec((N_DEV, cols), lambda i: (0, 0)), spec, spec, spec],
        out_specs=[spec] * 4, out_shape=[shape] * 4, compiler_params=_params(("arbitrary",)),
    )(c_all, dmod_cols, w, m, v)


def _rope_tables(positions):
    s_len = positions.shape[0]
    inv_freq = ROPE_THETA ** (-jnp.arange(0, 2 * ROT_HALF, 2, dtype=F32) / (2 * ROT_HALF))
    ang = positions.astype(F32)[:, None] * inv_freq
    cos, sin = jnp.cos(ang), jnp.sin(ang)
    rest = HEAD_DIM - 2 * ROT_HALF
    zero = lambda n: jnp.zeros((s_len, n), F32)
    head = jnp.stack([jnp.concatenate([cos, cos, jnp.ones((s_len, rest), F32)], axis=1),
                      jnp.concatenate([-sin, zero(HEAD_DIM - ROT_HALF)], axis=1),
                      jnp.concatenate([zero(ROT_HALF), sin, zero(rest)], axis=1)])
    return jnp.tile(head, (1, 1, LANES // HEAD_DIM))


def _pad_rows(a, rows):
    return jnp.pad(a, ((0, rows - a.shape[0]), (0, 0)))


def _as_rows(a, rows):
    flat = a.reshape(-1)
    return jnp.pad(flat, (0, rows * LANES - flat.shape[0])).reshape(rows, LANES)


def _sequence_step(xs, target, rope, mods, gains, w_in_t, w_out_t, fetch_ffn, send_ffn_grads, w_blk_b, b_pool_r,
                   pool_scale_r, conv_w_all, conv_b):
    sh_m, sc_m, gt_m, sh_f, sc_f, gt_f = mods
    g_pre_mix, g_post_mix, g_pre_ffn, g_post_ffn = gains
    h1, u_pool, qkv = _premix_inproj(xs, sh_m, sc_m, g_pre_mix, w_in_t, rope, tm=512)
    o_g, lse_g = [], []
    for gi, dil in enumerate(DILATIONS):
        o, lse = _attn_fwd(qkv, gi, dil)
        o_g.append(o)
        lse_g.append(lse)
    x1, y1, h2, cat, attn, lse_all = _mix_out(xs, u_pool, o_g, lse_g, w_blk_b, b_pool_r, pool_scale_r, w_out_t,
                                              gt_m, g_post_mix, g_pre_ffn, sc_f, sh_f, tm=256)
    w_up_t, w_down_f = fetch_ffn(x1)
    gate, val, dy2, dout, sums_ffn, loss_loc = _ffn_fwd_loss(h2, x1, target, w_up_t, w_down_f, conv_w_all, conv_b,
                                                              gt_f, g_post_ffn, tm=256, tf=2816, ck=256)

    dgc, dval, dw_down, dconv_w, dconv_b = _ffn_bwd_act(dy2, gate, val, w_down_f, conv_w_all, conv_b, tm=1024, tf=256, ck=256)
    dup, dh2 = _ffn_bwd_up(dgc, dval, w_up_t, conv_w_all, tm=256)
    dw_up_t = _wgrad(dup, h2, "wgrad_up", tk=1024, tmm=1408)
    token = send_ffn_grads(dw_up_t, dw_down)
    if token is not None:
        sc_f = sc_f + token[0:1, 0:1]
    dx1, dpool, dattn, delta, dw_out_t, sums_mix = _mix_bwd(dh2, dout, x1, y1, cat, attn, w_out_t, sc_f, g_pre_ffn,
                                                           gt_m, g_post_mix, tm=256)
    du, dw_blk, sums_pool = _pool_bwd(dpool, u_pool, w_blk_b, b_pool_r, pool_scale_r, tm=512)
    dqkv = []
    for gi, dil in enumerate(DILATIONS):
        dqkv += list(_attn_bwd(qkv, dattn, lse_all, delta, gi, dil))
    dproj, grad_x, sums_in = _inproj_bwd(du, dqkv, rope, w_in_t, xs, dx1, sc_m, g_pre_mix, tm=256)
    dw_in_t = _wgrad(dproj, h1, "wgrad_in", tk=1024, tmm=1280)
    return (loss_loc, grad_x, dw_in_t, dw_out_t, dw_up_t, dw_down, dw_blk, dconv_w, dconv_b,
            sums_in, sums_mix, sums_ffn, sums_pool)


def kernel(x, c, positions, w_ada, b_ada, g_pre_mix, g_post_mix, g_pre_ffn, g_post_ffn, w_in, w_pool, b_pool, pool_scale, w_out, w_up, conv_w, conv_b, w_down, loss_target, m_w_ada, m_b_ada, m_g_pre_mix, m_g_post_mix, m_g_pre_ffn, m_g_post_ffn, m_w_in, m_w_pool, m_b_pool, m_pool_scale, m_w_out, m_w_up, m_conv_w, m_conv_b, m_w_down, v_w_ada, v_b_ada, v_g_pre_mix, v_g_post_mix, v_g_pre_ffn, v_g_post_ffn, v_w_in, v_w_pool, v_b_pool, v_pool_scale, v_w_out, v_w_up, v_conv_w, v_conv_b, v_w_down):
    s_len, d = x.shape[1], x.shape[2]
    d_ff = w_down.shape[1] * N_DEV
    me = _index(_place())
    xs, target = x[0], loss_target[0]

    ncol = w_ada.shape[2]
    b_cols = lax.dynamic_slice(b_ada, (0, me * ncol), (1, ncol))
    c_all, mod, taps_all = _ada_exchange(jnp.broadcast_to(c, (8, d)), w_ada[0], b_cols, _pad_rows(conv_w[0], 8))
    c_all = c_all[:, 0, :]
    conv_w_all = jnp.transpose(taps_all[:, :3, :], (1, 0, 2)).reshape(3, d_ff)
    sh_m, sc_m, gt_m, sh_f, sc_f, gt_f = [mod[:, 0, :].reshape(1, -1)[:, k * d:(k + 1) * d] for k in range(6)]

    w_in_t, w_out_t = _gather_weights([w_in[0].T.astype(BF16), w_out[0].T.astype(BF16)])

    rope = _rope_tables(positions[0])
    w_blk = jnp.zeros((256, 256), F32)
    for gi in range(4):
        w_blk = lax.dynamic_update_slice(w_blk, w_pool[0, gi], (gi * HEAD_DIM, gi * HEAD_DIM))
    w_blk_b = w_blk.astype(BF16)
    b_pool_r, pool_scale_r = b_pool.reshape(1, 256), pool_scale.reshape(1, 256)

    cw_rows = d_ff // LANES

    up_sh, down_sh = w_up[0].T.astype(BF16), w_down[0].astype(BF16)
    w_in_t, conv_w_all, up_sh, down_sh = lax.optimization_barrier((w_in_t, conv_w_all, up_sh, down_sh))
    lands = [lax.empty((N_DEV * s.shape[0], s.shape[1]), BF16) for s in (up_sh, down_sh)]
    w_send, w_recv, w_src, w_land, w_token = _exchange_start("gather", [up_sh, down_sh], lands, "ffn_weights_start")

    def fetch_ffn(after):
        return _exchange_wait("gather", w_send, w_recv, w_src, w_land, after, "ffn_weights_wait")

    flight = []

    def send_ffn_grads(dw_up_t, dw_down):
        lands = [lax.empty((N_DEV, g.shape[0] // N_DEV, g.shape[1]), BF16) for g in (dw_up_t, dw_down)]
        flight.extend(_exchange_start("scatter", [dw_up_t, dw_down], lands, "ffn_grads_start"))
        return flight[4]

    (loss_loc, grad_x, dw_in_t, dw_out_t, _, _, dw_blk, dconv_w, dconv_b,
     sums_in, sums_mix, sums_ffn, sums_pool) = _sequence_step(
        xs, target, rope, (sh_m + w_token[0:1, 0:1], sc_m, gt_m, sh_f, sc_f, gt_f),
        (g_pre_mix, g_post_mix, g_pre_ffn, g_post_ffn),
        w_in_t, w_out_t, fetch_ffn, send_ffn_grads, w_blk_b, b_pool_r, pool_scale_r, conv_w_all, conv_b)

    parts_ffn = _exchange_wait("scatter", *flight[:4], dw_in_t, "ffn_grads_wait")
    dw_in_t, dw_out_t, *parts_ffn = lax.optimization_barrier((dw_in_t, dw_out_t, *parts_ffn))
    parts_mix = _scatter_grads([dw_in_t, dw_out_t])
    big = {
        "w_in": [a.T for a in _sum_adam(parts_mix[0], w_in[0].T, m_w_in[0].T, v_w_in[0].T, "adam_w_in", 64)],
        "w_out": [a.T for a in _sum_adam(parts_mix[1], w_out[0].T, m_w_out[0].T, v_w_out[0].T, "adam_w_out", 128)],
        "w_up": [a.T for a in _sum_adam(parts_ffn[0], w_up[0].T, m_w_up[0].T, v_w_up[0].T, "adam_w_up", 64)],
        "w_down": _sum_adam(parts_ffn[1], w_down[0], m_w_down[0], v_w_down[0], "adam_w_down", 32),
    }

    dmod = jnp.concatenate([sums_in[0:1], sums_in[1:2], sums_mix[3:4], sums_mix[0:1], sums_mix[1:2], sums_ffn[0:1]], axis=1)
    dw_pool = jnp.stack([dw_blk[gi * HEAD_DIM:(gi + 1) * HEAD_DIM, gi * HEAD_DIM:(gi + 1) * HEAD_DIM] for gi in range(4)])
    pieces = [(dmod, 48), (sums_in[2:3], 8), (sums_mix[4:5], 8), (sums_mix[2:3], 8), (sums_ffn[1:2], 8),
              (dw_pool, 128), (sums_pool[0:1], 8), (sums_pool[1:2], 8), (dconv_b, 24),
              (dconv_w[0:1], 24), (dconv_w[1:2], 24), (dconv_w[2:3], 24), (loss_loc[0:1], 8)]
    packed = jnp.concatenate([_as_rows(a, r) for a, r in pieces], axis=0)
    gathered, total = _allreduce_small(packed, "allreduce_small")
    n_rep = 248
    rep_w = [b_ada, g_pre_mix, g_post_mix, g_pre_ffn, g_post_ffn, w_pool, b_pool, pool_scale, conv_b]
    rep_m = [m_b_ada, m_g_pre_mix, m_g_post_mix, m_g_pre_ffn, m_g_post_ffn, m_w_pool, m_b_pool, m_pool_scale, m_conv_b]
    rep_v = [v_b_ada, v_g_pre_mix, v_g_post_mix, v_g_pre_ffn, v_g_post_ffn, v_w_pool, v_b_pool, v_pool_scale, v_conv_b]
    rep_rows = [r for _, r in pieces[:9]]
    pack_rep = lambda arrs: jnp.concatenate([_as_rows(a, r) for a, r in zip(arrs, rep_rows)], axis=0)
    rep_g = total[:n_rep]
    rep_d, rep_nm, rep_nv = _adam(pack_rep(rep_w), rep_g, pack_rep(rep_m), pack_rep(rep_v), "adam_small", n_rep)

    def unpack(p):
        out, row = [], 0
        for a, r in zip(rep_w, rep_rows):
            out.append(p[row:row + r].reshape(-1)[:a.size].reshape(a.shape))
            row += r
        return out

    g_rep, d_rep, nm_rep, nv_rep = unpack(rep_g), unpack(rep_d), unpack(rep_nm), unpack(rep_nv)

    fcol = d_ff // N_DEV
    g_cw_full = jnp.concatenate([total[n_rep + 24 * k:n_rep + 24 * k + cw_rows].reshape(1, d_ff) for k in range(3)], axis=0)
    g_cw = lax.dynamic_slice(g_cw_full, (0, me * fcol), (3, fcol))
    d_cw, nm_cw, nv_cw = _adam(conv_w[0], g_cw, m_conv_w[0], v_conv_w[0], "adam_conv_w", 3)

    dmod_all = gathered[:, :48].reshape(N_DEV, 6 * d)
    dmod_cols = lax.dynamic_slice(dmod_all, (0, me * ncol), (N_DEV, ncol))
    g_ada, d_ada, nm_ada, nv_ada = _ada_grad_adam(c_all, dmod_cols, w_ada[0], m_w_ada[0], v_w_ada[0], 256)

    loss = total[n_rep + 72, 0]

    def group(k):
        rep = (g_rep, d_rep, nm_rep, nv_rep)[k]
        ada = (g_ada, d_ada, nm_ada, nv_ada)[k][None]
        cw = (g_cw, d_cw, nm_cw, nv_cw)[k][None]
        return [ada, rep[0], rep[1], rep[2], rep[3], rep[4], big["w_in"][k][None], rep[5], rep[6], rep[7],
                big["w_out"][k][None], big["w_up"][k][None], cw, rep[8], big["w_down"][k][None]]

    return (loss, grad_x[None], *group(0), *group(1), *group(2), *group(3))
```

```python
import functools
import math

import jax
import jax.numpy as jnp
from jax import lax
from jax.experimental import pallas as pl
from jax.experimental.pallas import tpu as pltpu

F32 = jnp.float32
BF16 = jnp.bfloat16
MESH = pl.DeviceIdType.MESH

N_DEV = 8
HEAD_DIM = 64
ROT_HALF = 8
ROPE_THETA = 500000.0
POOL_WINDOWS = (2, 4, 8, 16)
DILATIONS = (1, 4, 16)
BLOCK = 128
NORM_EPS = 1e-6
HALO = 16
MASKED = -1e30
ATTN_FWD_UNROLL = 4
ATTN_BWD_UNROLL = 2

ADAM_LR = 0.001
ADAM_B1 = 0.9
ADAM_B2 = 0.999
ADAM_EPS = 1e-08
ADAM_WD = 0.01
ADAM_STEP = 10

V7X_VMEM_LIMIT = 56 * 1024 * 1024
LANES = 128

NT = (((1,), (1,)), ((), ()))
NN = (((1,), (0,)), ((), ()))
TN = (((0,), (0,)), ((), ()))


def _dot(a, b, dims):
    return lax.dot_general(a, b, dims, preferred_element_type=F32)


def _params(sem=None, vmem=V7X_VMEM_LIMIT):
    if sem is None:
        return pltpu.CompilerParams(vmem_limit_bytes=vmem)
    return pltpu.CompilerParams(dimension_semantics=sem, vmem_limit_bytes=vmem)


def _rstd(v):
    return lax.rsqrt(jnp.mean(v * v, axis=-1, keepdims=True) + NORM_EPS)


def _norm_bwd(dn, n, rstd):
    return rstd * (dn - n * jnp.mean(dn * n, axis=-1, keepdims=True))


def _rope_fwd(p, rope_ref):
    return p * rope_ref[0] + pltpu.roll(p, LANES - ROT_HALF, 1) * rope_ref[1] + pltpu.roll(p, ROT_HALF, 1) * rope_ref[2]


def _rope_bwd(dp, rope_ref):
    return dp * rope_ref[0] + pltpu.roll(dp * rope_ref[1], ROT_HALF, 1) + pltpu.roll(dp * rope_ref[2], LANES - ROT_HALF, 1)


def _gelu_parts(v):
    k = math.sqrt(2.0 / math.pi)
    t = jnp.tanh(k * (v + 0.044715 * v * v * v))
    g = 0.5 * v * (1.0 + t)
    dg = 0.5 * (1.0 + t) + 0.5 * v * (1.0 - t * t) * k * (1.0 + 3.0 * 0.044715 * v * v)
    return g, dg


def _halo_before(i, tile):
    return jnp.maximum(i * (tile // HALO) - 1, 0)


def _premix_inproj(x, sh, sc, g, w_in_t, rope, tm):
    s_len, d = x.shape
    n_proj = w_in_t.shape[0]
    n_slab = (n_proj - 256) // LANES

    def body(x_ref, sh_ref, sc_ref, g_ref, w_ref, rope_ref, h_ref, up_ref, qkv_ref):
        xv = x_ref[...]
        h = (xv * _rstd(xv) * g_ref[...]) * (1.0 + sc_ref[...]) + sh_ref[...]
        hb = h.astype(BF16)
        h_ref[...] = hb
        up_ref[...] = _dot(hb, w_ref[0:256, :], NT)
        for pair in range(n_slab // 2):
            p = _dot(hb, w_ref[256 + 256 * pair:512 + 256 * pair, :], NT)
            for half in range(2):
                ph = p[:, half * LANES:(half + 1) * LANES]
                if pair < 6:
                    ph = _rope_fwd(ph, rope_ref)
                if pair < 3:
                    ph = ph * (HEAD_DIM ** -0.5)
                qkv_ref[2 * pair + half] = ph

    vec = pl.BlockSpec((1, d), lambda i: (0, 0))
    return pl.pallas_call(
        body, name="premix_inproj", grid=(s_len // tm,),
        in_specs=[pl.BlockSpec((tm, d), lambda i: (i, 0)), vec, vec, vec,
                  pl.BlockSpec((n_proj, d), lambda i: (0, 0)),
                  pl.BlockSpec((3, tm, LANES), lambda i: (0, i, 0))],
        out_specs=[pl.BlockSpec((tm, d), lambda i: (i, 0)),
                   pl.BlockSpec((tm, 256), lambda i: (i, 0)),
                   pl.BlockSpec((n_slab, tm, LANES), lambda i: (0, i, 0))],
        out_shape=[jax.ShapeDtypeStruct((s_len, d), BF16),
                   jax.ShapeDtypeStruct((s_len, 256), F32),
                   jax.ShapeDtypeStruct((n_slab, s_len, LANES), F32)],
        compiler_params=_params(("arbitrary",)),
    )(x, sh, sc, g, w_in_t, rope)


def _block_rows(n, r, dil):
    start = n * (BLOCK * dil) + r
    if dil == 1:
        return pl.ds(pl.multiple_of(start, BLOCK), BLOCK)
    return pl.ds(start, BLOCK, stride=dil)


def _band_mask(n):
    ri = lax.broadcasted_iota(jnp.int32, (BLOCK, 2 * BLOCK), 0)
    cj = lax.broadcasted_iota(jnp.int32, (BLOCK, 2 * BLOCK), 1)
    cur = (cj >= BLOCK) & (cj - BLOCK <= ri)
    prev = (cj < BLOCK) & (cj >= ri) & (n > 0)
    return cur | prev


def _attn_fwd(qkv, group, dil):
    s_len = qkv.shape[1]
    nb = s_len // (BLOCK * dil)

    def body(q_ref, k_ref, v_ref, o_ref, lse_ref):
        lane = lax.broadcasted_iota(jnp.int32, (BLOCK, LANES), 1)
        first = lane < HEAD_DIM

        def block(t, carry):
            r, n = t // nb, t % nb
            cur = _block_rows(n, r, dil)
            prev = _block_rows(jnp.maximum(n - 1, 0), r, dil)
            q = q_ref[0, cur, :]
            kcat = jnp.concatenate([k_ref[0, prev, :], k_ref[0, cur, :]], axis=0).astype(BF16)
            vcat = jnp.concatenate([v_ref[0, prev, :], v_ref[0, cur, :]], axis=0).astype(BF16)
            valid = _band_mask(n)
            q2 = jnp.concatenate([jnp.where(first, q, 0.0), jnp.where(first, 0.0, q)], axis=0).astype(BF16)
            s = jnp.where(jnp.concatenate([valid, valid], axis=0), _dot(q2, kcat, NT), MASKED)
            m = jnp.max(s, axis=-1, keepdims=True)
            p = jnp.exp(s - m)
            den = jnp.sum(p, axis=-1, keepdims=True)
            o2 = _dot(p.astype(BF16), vcat, NN) / den
            lse2 = m + jnp.log(den)
            o_ref[0, cur, :] = jnp.where(first, o2[:BLOCK], o2[BLOCK:])
            lse_ref[0, cur, :] = jnp.where(first, lse2[:BLOCK], lse2[BLOCK:])
            return carry

        lax.fori_loop(0, nb * dil, block, 0, unroll=ATTN_FWD_UNROLL)

    def slab(base):
        return pl.BlockSpec((1, s_len, LANES), lambda s: (base + 2 * group + s, 0, 0))

    out = pl.BlockSpec((1, s_len, LANES), lambda s: (s, 0, 0))
    shape = jax.ShapeDtypeStruct((2, s_len, LANES), F32)
    return pl.pallas_call(
        body, name=f"attn_fwd_d{dil}", grid=(2,),
        in_specs=[slab(0), slab(6), slab(12)], out_specs=[out, out], out_shape=[shape, shape],
        compiler_params=_params(("arbitrary",)),
    )(qkv, qkv, qkv)


def _pool_mixed(u, halo, i, tm):
    ue = jnp.concatenate([halo, u], axis=0)
    s2 = ue + pltpu.roll(ue, 1, 0)
    s4 = s2 + pltpu.roll(s2, 2, 0)
    s8 = s4 + pltpu.roll(s4, 4, 0)
    s16 = s8 + pltpu.roll(s8, 8, 0)
    grp = lax.broadcasted_iota(jnp.int32, (tm, 256), 1) // HEAD_DIM
    pick = lambda a, b, c, e: jnp.where(grp == 0, a, jnp.where(grp == 1, b, jnp.where(grp == 2, c, e)))
    win_sum = pick(s2[HALO:], s4[HALO:], s8[HALO:], s16[HALO:])
    pos = (i * tm + lax.broadcasted_iota(jnp.int32, (tm, 256), 0)).astype(F32)
    count = jnp.minimum(pos + 1.0, pick(*[float(w) for w in POOL_WINDOWS]))
    return win_sum / count - u, count


def _mix_out(x, u_pool, o_g, lse_g, w_blk, b_pool, pool_scale, w_out_t, gt_m, g_post_mix, g_pre_ffn, sc_f, sh_f, tm):
    s_len, d = x.shape

    def body(x_ref, u_ref, uh_ref, o0, o1, o2, l0, l1, l2, wb_ref, bp_ref, ps_ref, wo_ref,
             gt_ref, g1_ref, g2_ref, sc_ref, sh_ref,
             x1_ref, y1_ref, h2_ref, cat_ref, attn_ref, lall_ref):
        i = pl.program_id(0)
        u = u_ref[...]
        halo = uh_ref[...] * (i > 0).astype(F32)
        mixed, _ = _pool_mixed(u, halo, i, tm)
        y = _dot(mixed.astype(BF16), wb_ref[...], NN) + bp_ref[...]
        pool = y * ps_ref[...]
        attn = []
        for s in range(2):
            la, lb, lc = l0[s], l1[s], l2[s]
            mx = jnp.maximum(jnp.maximum(la, lb), lc)
            ea, eb, ec = jnp.exp(la - mx), jnp.exp(lb - mx), jnp.exp(lc - mx)
            den = ea + eb + ec
            lall_ref[s] = mx + jnp.log(den)
            attn.append((ea / den) * o0[s] + (eb / den) * o1[s] + (ec / den) * o2[s])
        attn = jnp.concatenate(attn, axis=1)
        attn_ref[...] = attn
        cat = jnp.concatenate([pool, attn], axis=1).astype(BF16)
        cat_ref[...] = cat
        y1 = _dot(cat, wo_ref[...], NT)
        y1_ref[...] = y1
        x1 = x_ref[...] + gt_ref[...] * (y1 * _rstd(y1) * g1_ref[...])
        x1_ref[...] = x1
        h2 = (x1 * _rstd(x1) * g2_ref[...]) * (1.0 + sc_ref[...]) + sh_ref[...]
        h2_ref[...] = h2.astype(BF16)

    tile = lambda w: pl.BlockSpec((tm, w), lambda i: (i, 0))
    slab = pl.BlockSpec((2, tm, LANES), lambda i: (0, i, 0))
    const = lambda a: pl.BlockSpec(a.shape, lambda i: (0,) * a.ndim)
    return pl.pallas_call(
        body, name="mix_out", grid=(s_len // tm,),
        in_specs=[tile(d), tile(256), pl.BlockSpec((HALO, 256), lambda i: (_halo_before(i, tm), 0)),
                  slab, slab, slab, slab, slab, slab,
                  const(w_blk), const(b_pool), const(pool_scale), const(w_out_t),
                  const(gt_m), const(g_post_mix), const(g_pre_ffn), const(sc_f), const(sh_f)],
        out_specs=[tile(d), tile(d), tile(d), tile(512), tile(256), slab],
        out_shape=[jax.ShapeDtypeStruct((s_len, d), F32), jax.ShapeDtypeStruct((s_len, d), F32),
                   jax.ShapeDtypeStruct((s_len, d), BF16), jax.ShapeDtypeStruct((s_len, 512), BF16),
                   jax.ShapeDtypeStruct((s_len, 256), F32), jax.ShapeDtypeStruct((2, s_len, LANES), F32)],
        compiler_params=_params(("arbitrary",)),
    )(x, u_pool, u_pool, *o_g, *lse_g, w_blk, b_pool, pool_scale, w_out_t, gt_m, g_post_mix, g_pre_ffn, sc_f, sh_f)


def _conv_gate(gate_ext, cw, cb):
    gc = gate_ext * cw[2:3, :] + pltpu.roll(gate_ext, 1, 0) * cw[1:2, :] + pltpu.roll(gate_ext, 2, 0) * cw[0:1, :]
    return gc[HALO:] + cb


def _ffn_fwd_loss(h2, x1, target, w_up_t, w_down, conv_w, conv_b, gt_f, g_post_ffn, tm, tf, ck):
    s_len, d = x1.shape
    d_ff = w_down.shape[0]
    n_f = d_ff // tf

    def body(h_ref, hh_ref, x1_ref, tgt_ref, wg_ref, wv_ref, wd_ref, cw_ref, cb_ref, gt_ref, g_ref,
             gate_ref, val_ref, dy2_ref, dout_ref, sums_ref, loss_ref, acc_ref):
        i, j = pl.program_id(0), pl.program_id(1)

        @pl.when((i == 0) & (j == 0))
        def _():
            sums_ref[...] = jnp.zeros_like(sums_ref)
            loss_ref[...] = jnp.zeros_like(loss_ref)

        h = h_ref[...]
        h_ext = jnp.concatenate([hh_ref[...], h], axis=0)
        row = lax.broadcasted_iota(jnp.int32, (tm + HALO, ck), 0)
        no_halo = (row < HALO) & (i == 0)
        part = None
        for c in range(tf // ck):
            cs = slice(c * ck, (c + 1) * ck)
            gate_ext = jnp.where(no_halo, 0.0, _dot(h_ext, wg_ref[cs, :], NT))
            val = _dot(h, wv_ref[cs, :], NT)
            act, _ = _gelu_parts(_conv_gate(gate_ext, cw_ref[:, cs], cb_ref[:, cs]))
            gate_ref[:, cs] = gate_ext[HALO:].astype(BF16)
            val_ref[:, cs] = val.astype(BF16)
            p = _dot((act * val).astype(BF16), wd_ref[cs, :], NN)
            part = p if part is None else part + p

        @pl.when(j == 0)
        def _():
            acc_ref[...] = part

        @pl.when(j > 0)
        def _():
            acc_ref[...] += part

        @pl.when(j == n_f - 1)
        def _():
            y2 = acc_ref[...]
            rstd = _rstd(y2)
            n = y2 * rstd
            rn = n * g_ref[...]
            err = x1_ref[...] + gt_ref[...] * rn - tgt_ref[...]
            loss_ref[...] += 0.5 * jnp.sum(jnp.mean(err * err, axis=-1, keepdims=True), axis=0, keepdims=True)
            dout = err * (1.0 / d)
            dout_ref[...] = dout
            drn = dout * gt_ref[...]
            sums_ref[0:1, :] += jnp.sum(dout * rn, axis=0, keepdims=True)
            sums_ref[1:2, :] += jnp.sum(drn * n, axis=0, keepdims=True)
            dy2_ref[...] = _norm_bwd(drn * g_ref[...], n, rstd).astype(BF16)

    tok = lambda w: pl.BlockSpec((tm, w), lambda i, j: (i, 0))
    tokf = pl.BlockSpec((tm, tf), lambda i, j: (i, j))
    vec = pl.BlockSpec((1, d), lambda i, j: (0, 0))
    once = {"pipeline_mode": pl.Buffered(1)} if n_f == 1 else {}
    return pl.pallas_call(
        body, name="ffn_fwd_loss", grid=(s_len // tm, n_f),
        in_specs=[tok(d), pl.BlockSpec((HALO, d), lambda i, j: (_halo_before(i, tm), 0)), tok(d), tok(d),
                  pl.BlockSpec((tf, d), lambda i, j: (j, 0), **once),
                  pl.BlockSpec((tf, d), lambda i, j: (j + n_f, 0), **once),
                  pl.BlockSpec((tf, d), lambda i, j: (j, 0), **once),
                  pl.BlockSpec((3, tf), lambda i, j: (0, j)), pl.BlockSpec((1, tf), lambda i, j: (0, j)), vec, vec],
        out_specs=[tokf, tokf, tok(d), tok(d), pl.BlockSpec((8, d), lambda i, j: (0, 0)),
                   pl.BlockSpec((8, LANES), lambda i, j: (0, 0))],
        out_shape=[jax.ShapeDtypeStruct((s_len, d_ff), BF16), jax.ShapeDtypeStruct((s_len, d_ff), BF16),
                   jax.ShapeDtypeStruct((s_len, d), BF16), jax.ShapeDtypeStruct((s_len, d), F32),
                   jax.ShapeDtypeStruct((8, d), F32), jax.ShapeDtypeStruct((8, LANES), F32)],
        scratch_shapes=[pltpu.VMEM((tm, d), F32)],
        compiler_params=_params(("arbitrary", "arbitrary")),
    )(h2, h2, x1, target, w_up_t, w_up_t, w_down, conv_w, conv_b, gt_f, g_post_ffn)


def _ffn_bwd_act(dy2, gate, val, w_down, conv_w, conv_b, tm, tf, ck):
    s_len, d = dy2.shape
    d_ff = w_down.shape[0]
    n_t = s_len // tm

    def body(dy_ref, g_ref, gh_ref, v_ref, wd_ref, cw_ref, cb_ref,
             dgc_ref, dval_ref, dwd_ref, dcw_ref, dcb_ref, acc_ref):
        i = pl.program_id(1)
        cw, cb, wd = cw_ref[...], cb_ref[...], wd_ref[...]
        row = lax.broadcasted_iota(jnp.int32, (ck + HALO, tf), 0)
        dwd = taps = bias = None
        for c in range(tm // ck):
            rs = slice(c * ck, (c + 1) * ck)
            halo = gh_ref[...] if c == 0 else g_ref[c * ck - HALO:c * ck, :]
            gate_ext = jnp.concatenate([halo, g_ref[rs, :]], axis=0).astype(F32)
            if c == 0:
                gate_ext = jnp.where((row < HALO) & (i == 0), 0.0, gate_ext)
            act, dact = _gelu_parts(_conv_gate(gate_ext, cw, cb))
            v = v_ref[rs, :].astype(F32)
            dy = dy_ref[rs, :]
            da = _dot(dy, wd, NT)
            dgc = da * v * dact
            dgc_ref[rs, :] = dgc.astype(BF16)
            dval_ref[rs, :] = (da * act).astype(BF16)
            dwd_c = _dot((act * v).astype(BF16), dy, TN)
            taps_c = jnp.concatenate(
                [jnp.sum(dgc * pltpu.roll(gate_ext, 2 - k, 0)[HALO:], axis=0, keepdims=True) if k < 2
                 else jnp.sum(dgc * gate_ext[HALO:], axis=0, keepdims=True) for k in range(3)], axis=0)
            bias_c = jnp.sum(dgc, axis=0, keepdims=True)
            dwd = dwd_c if c == 0 else dwd + dwd_c
            taps = taps_c if c == 0 else taps + taps_c
            bias = bias_c if c == 0 else bias + bias_c

        @pl.when(i == 0)
        def _():
            acc_ref[...] = dwd
            dcw_ref[...] = taps
            dcb_ref[...] = bias

        @pl.when(i > 0)
        def _():
            acc_ref[...] += dwd
            dcw_ref[...] += taps
            dcb_ref[...] += bias

        @pl.when(i == n_t - 1)
        def _():
            dwd_ref[...] = acc_ref[...].astype(BF16)

    tokf = pl.BlockSpec((tm, tf), lambda j, i: (i, j))
    return pl.pallas_call(
        body, name="ffn_bwd_act", grid=(d_ff // tf, n_t),
        in_specs=[pl.BlockSpec((tm, d), lambda j, i: (i, 0)), tokf,
                  pl.BlockSpec((HALO, tf), lambda j, i: (_halo_before(i, tm), j)), tokf,
                  pl.BlockSpec((tf, d), lambda j, i: (j, 0)),
                  pl.BlockSpec((3, tf), lambda j, i: (0, j)), pl.BlockSpec((1, tf), lambda j, i: (0, j))],
        out_specs=[tokf, tokf, pl.BlockSpec((tf, d), lambda j, i: (j, 0)),
                   pl.BlockSpec((3, tf), lambda j, i: (0, j)), pl.BlockSpec((1, tf), lambda j, i: (0, j))],
        out_shape=[jax.ShapeDtypeStruct((s_len, d_ff), BF16), jax.ShapeDtypeStruct((s_len, d_ff), BF16),
                   jax.ShapeDtypeStruct((d_ff, d), BF16), jax.ShapeDtypeStruct((3, d_ff), F32),
                   jax.ShapeDtypeStruct((1, d_ff), F32)],
        scratch_shapes=[pltpu.VMEM((tf, d), F32)],
        compiler_params=_params(("arbitrary", "arbitrary")),
    )(dy2, gate, gate, val, w_down, conv_w, conv_b)


def _ffn_bwd_up(dgc, dval, w_up_t, conv_w, tm):
    s_len, d_ff = dgc.shape
    d = w_up_t.shape[1]
    n_t = s_len // tm

    def body(dg_ref, dgn_ref, dv_ref, cw_ref, w_ref, dup_ref, dh_ref):
        i = pl.program_id(0)
        nxt = dgn_ref[...].astype(F32) * (i < n_t - 1).astype(F32)
        ext = jnp.concatenate([dg_ref[...].astype(F32), nxt], axis=0)
        rows = tm + HALO
        dgate = (ext * cw_ref[2:3, :] + pltpu.roll(ext, rows - 1, 0) * cw_ref[1:2, :]
                 + pltpu.roll(ext, rows - 2, 0) * cw_ref[0:1, :])[:tm]
        dup = jnp.concatenate([dgate.astype(BF16), dv_ref[...]], axis=1)
        dup_ref[...] = dup
        dh_ref[...] = _dot(dup, w_ref[...], NN)

    tokf = pl.BlockSpec((tm, d_ff), lambda i: (i, 0))
    return pl.pallas_call(
        body, name="ffn_bwd_up", grid=(n_t,),
        in_specs=[tokf, pl.BlockSpec((HALO, d_ff), lambda i: (jnp.minimum((i + 1) * (tm // HALO), s_len // HALO - 1), 0)),
                  tokf, pl.BlockSpec((3, d_ff), lambda i: (0, 0)), pl.BlockSpec((2 * d_ff, d), lambda i: (0, 0))],
        out_specs=[pl.BlockSpec((tm, 2 * d_ff), lambda i: (i, 0)), pl.BlockSpec((tm, d), lambda i: (i, 0))],
        out_shape=[jax.ShapeDtypeStruct((s_len, 2 * d_ff), BF16), jax.ShapeDtypeStruct((s_len, d), F32)],
        compiler_params=_params(("arbitrary",)),
    )(dgc, dgc, dval, conv_w, w_up_t)


def _mix_bwd(dh2, dout, x1, y1, cat, attn, w_out_t, sc_f, g_pre_ffn, gt_m, g_post_mix, tm):
    s_len, d = x1.shape
    n_t = s_len // tm

    def body(dh_ref, do_ref, x1_ref, y1_ref, cat_ref, at_ref, wo_ref, sc_ref, g2_ref, gt_ref, g1_ref,
             dx1_ref, dpool_ref, dattn_ref, delta_ref, dwo_ref, sums_ref, acc_ref):
        i = pl.program_id(0)
        dh = dh_ref[...]
        x1 = x1_ref[...]
        r2 = _rstd(x1)
        n2 = x1 * r2
        ng = n2 * g2_ref[...]
        dng = dh * (1.0 + sc_ref[...])
        dx1 = do_ref[...] + _norm_bwd(dng * g2_ref[...], n2, r2)
        dx1_ref[...] = dx1
        y1 = y1_ref[...]
        r1 = _rstd(y1)
        n1 = y1 * r1
        drn = dx1 * gt_ref[...]
        dy1 = _norm_bwd(drn * g1_ref[...], n1, r1).astype(BF16)
        dcat = _dot(dy1, wo_ref[...], NN)
        dpool_ref[...] = dcat[:, 0:256]
        lane = lax.broadcasted_iota(jnp.int32, (tm, LANES), 1)
        first = lane < HEAD_DIM
        for s in range(2):
            da = dcat[:, 256 + s * LANES:256 + (s + 1) * LANES]
            dattn_ref[s] = da
            prod = da * at_ref[:, s * LANES:(s + 1) * LANES]
            tot = jnp.sum(prod, axis=-1, keepdims=True)
            lo = jnp.sum(jnp.where(first, prod, 0.0), axis=-1, keepdims=True)
            delta_ref[s] = jnp.where(first, lo, tot - lo)
        dwo = _dot(dy1, cat_ref[...], TN)
        sums = jnp.concatenate(
            [jnp.sum(dh, axis=0, keepdims=True), jnp.sum(dh * ng, axis=0, keepdims=True),
             jnp.sum(dng * n2, axis=0, keepdims=True), jnp.sum(dx1 * (n1 * g1_ref[...]), axis=0, keepdims=True),
             jnp.sum(drn * n1, axis=0, keepdims=True), jnp.zeros((3, d), F32)], axis=0)

        @pl.when(i == 0)
        def _():
            acc_ref[...] = dwo
            sums_ref[...] = sums

        @pl.when(i > 0)
        def _():
            acc_ref[...] += dwo
            sums_ref[...] += sums

        @pl.when(i == n_t - 1)
        def _():
            dwo_ref[...] = acc_ref[...].astype(BF16)

    tile = lambda w: pl.BlockSpec((tm, w), lambda i: (i, 0))
    slab = pl.BlockSpec((2, tm, LANES), lambda i: (0, i, 0))
    vec = pl.BlockSpec((1, d), lambda i: (0, 0))
    return pl.pallas_call(
        body, name="mix_bwd", grid=(n_t,),
        in_specs=[tile(d), tile(d), tile(d), tile(d), tile(512), tile(256),
                  pl.BlockSpec((d, 512), lambda i: (0, 0)), vec, vec, vec, vec],
        out_specs=[tile(d), tile(256), slab, slab, pl.BlockSpec((d, 512), lambda i: (0, 0)),
                   pl.BlockSpec((8, d), lambda i: (0, 0))],
        out_shape=[jax.ShapeDtypeStruct((s_len, d), F32), jax.ShapeDtypeStruct((s_len, 256), F32),
                   jax.ShapeDtypeStruct((2, s_len, LANES), F32), jax.ShapeDtypeStruct((2, s_len, LANES), F32),
                   jax.ShapeDtypeStruct((d, 512), BF16), jax.ShapeDtypeStruct((8, d), F32)],
        scratch_shapes=[pltpu.VMEM((d, 512), F32)],
        compiler_params=_params(("arbitrary",)),
    )(dh2, dout, x1, y1, cat, attn, w_out_t, sc_f, g_pre_ffn, gt_m, g_post_mix)


def _pool_bwd(dpool, u_pool, w_blk, b_pool, pool_scale, tm):
    s_len = dpool.shape[0]
    n_t = s_len // tm

    def body(dp_ref, dpn_ref, u_ref, uh_ref, wb_ref, bp_ref, ps_ref, du_ref, dwb_ref, sums_ref):
        i = pl.program_id(0)
        u = u_ref[...]
        mixed, _ = _pool_mixed(u, uh_ref[...] * (i > 0).astype(F32), i, tm)
        mixed_b = mixed.astype(BF16)
        y = _dot(mixed_b, wb_ref[...], NN) + bp_ref[...]
        dp = dp_ref[...]
        dy = dp * ps_ref[...]
        dwb = _dot(mixed_b, dy.astype(BF16), TN)
        sums = jnp.concatenate([jnp.sum(dy, axis=0, keepdims=True), jnp.sum(dp * y, axis=0, keepdims=True),
                                jnp.zeros((6, 256), F32)], axis=0)
        dp_ext = jnp.concatenate([dp, dpn_ref[...] * (i < n_t - 1).astype(F32)], axis=0)
        dmix = _dot((dp_ext * ps_ref[...]).astype(BF16), wb_ref[...], NT)
        rows = tm + HALO
        grp = lax.broadcasted_iota(jnp.int32, (rows, 256), 1) // HEAD_DIM
        pick = lambda a, b, c, e: jnp.where(grp == 0, a, jnp.where(grp == 1, b, jnp.where(grp == 2, c, e)))
        pos = (i * tm + lax.broadcasted_iota(jnp.int32, (rows, 256), 0)).astype(F32)
        z = dmix / jnp.minimum(pos + 1.0, pick(*[float(w) for w in POOL_WINDOWS]))
        f2 = z + pltpu.roll(z, rows - 1, 0)
        f4 = f2 + pltpu.roll(f2, rows - 2, 0)
        f8 = f4 + pltpu.roll(f4, rows - 4, 0)
        f16 = f8 + pltpu.roll(f8, rows - 8, 0)
        du_ref[...] = (pick(f2, f4, f8, f16) - dmix)[:tm]

        @pl.when(i == 0)
        def _():
            dwb_ref[...] = dwb
            sums_ref[...] = sums

        @pl.when(i > 0)
        def _():
            dwb_ref[...] += dwb
            sums_ref[...] += sums

    tile = pl.BlockSpec((tm, 256), lambda i: (i, 0))
    const = lambda a: pl.BlockSpec(a.shape, lambda i: (0,) * a.ndim)
    return pl.pallas_call(
        body, name="pool_bwd", grid=(n_t,),
        in_specs=[tile, pl.BlockSpec((HALO, 256), lambda i: (jnp.minimum((i + 1) * (tm // HALO), s_len // HALO - 1), 0)),
                  tile, pl.BlockSpec((HALO, 256), lambda i: (_halo_before(i, tm), 0)),
                  const(w_blk), const(b_pool), const(pool_scale)],
        out_specs=[tile, pl.BlockSpec((256, 256), lambda i: (0, 0)), pl.BlockSpec((8, 256), lambda i: (0, 0))],
        out_shape=[jax.ShapeDtypeStruct((s_len, 256), F32), jax.ShapeDtypeStruct((256, 256), F32),
                   jax.ShapeDtypeStruct((8, 256), F32)],
        compiler_params=_params(("arbitrary",)),
    )(dpool, dpool, u_pool, u_pool, w_blk, b_pool, pool_scale)


def _attn_bwd(qkv, dattn, lse_all, delta, group, dil):
    s_len = qkv.shape[1]
    nb = s_len // (BLOCK * dil)

    def body(q_ref, k_ref, v_ref, do_ref, l_ref, dl_ref, dq_ref, dk_ref, dv_ref):
        lane = lax.broadcasted_iota(jnp.int32, (BLOCK, LANES), 1)
        first = lane < HEAD_DIM

        def block(t, carry):
            dk_part, dv_part = carry
            r, n = t // nb, t % nb
            cur = _block_rows(n, r, dil)
            prev = _block_rows(jnp.maximum(n - 1, 0), r, dil)
            q = q_ref[0, cur, :]
            do = do_ref[0, cur, :]
            lse = l_ref[0, cur, :]
            dlt = dl_ref[0, cur, :]
            kcat = jnp.concatenate([k_ref[0, prev, :], k_ref[0, cur, :]], axis=0).astype(BF16)
            vcat = jnp.concatenate([v_ref[0, prev, :], v_ref[0, cur, :]], axis=0).astype(BF16)
            valid = _band_mask(n)
            stack = lambda a: jnp.concatenate([jnp.where(first, a, 0.0), jnp.where(first, 0.0, a)], axis=0)
            rows2 = lambda a: jnp.concatenate([a[:, 0:1], a[:, HEAD_DIM:HEAD_DIM + 1]], axis=0)
            q2, do2 = stack(q).astype(BF16), stack(do).astype(BF16)
            valid2 = jnp.concatenate([valid, valid], axis=0)
            p = jnp.where(valid2, jnp.exp(_dot(q2, kcat, NT) - rows2(lse)), 0.0)
            ds = (p * (_dot(do2, vcat, NT) - rows2(dlt))).astype(BF16)
            dq2 = _dot(ds, kcat, NN)
            dq_ref[0, cur, :] = jnp.where(first, dq2[:BLOCK], dq2[BLOCK:])
            dkc = _dot(ds, q2, TN)
            dvc = _dot(p.astype(BF16), do2, TN)
            dk_ref[0, prev, :] = dk_part + dkc[:BLOCK]
            dv_ref[0, prev, :] = dv_part + dvc[:BLOCK]
            dk_ref[0, cur, :] = dkc[BLOCK:]
            dv_ref[0, cur, :] = dvc[BLOCK:]
            return dkc[BLOCK:], dvc[BLOCK:]

        def blocks(tt, carry):
            for u in range(ATTN_BWD_UNROLL):
                carry = block(tt * ATTN_BWD_UNROLL + u, carry)
            return carry

        zero = jnp.zeros((BLOCK, LANES), F32)
        lax.fori_loop(0, nb * dil // ATTN_BWD_UNROLL, blocks, (zero, zero))

    def slab(base):
        return pl.BlockSpec((1, s_len, LANES), lambda s: (base + 2 * group + s, 0, 0))

    one = pl.BlockSpec((1, s_len, LANES), lambda s: (s, 0, 0))
    shape = jax.ShapeDtypeStruct((2, s_len, LANES), F32)
    return pl.pallas_call(
        body, name=f"attn_bwd_d{dil}", grid=(2,),
        in_specs=[slab(0), slab(6), slab(12), one, one, one],
        out_specs=[one, one, one], out_shape=[shape, shape, shape],
        compiler_params=_params(("arbitrary",)),
    )(qkv, qkv, qkv, dattn, lse_all, delta)


def _inproj_bwd(du, dqkv, rope, w_in_t, x, dx1, sc_m, g_pre_mix, tm):
    s_len, d = x.shape
    n_proj = w_in_t.shape[0]
    n_t = s_len // tm

    def body(du_ref, *refs):
        dref = refs[:9]
        rope_ref, w_ref, x_ref, dx1_ref, sc_ref, g_ref, dproj_ref, dx_ref, sums_ref = refs[9:]
        i = pl.program_id(0)
        cols = [du_ref[...].astype(BF16)]
        for kind in range(3):
            for grp in range(3):
                for s in range(2):
                    piece = dref[3 * grp + kind][s]
                    if kind < 2:
                        piece = _rope_bwd(piece, rope_ref)
                    if kind == 0:
                        piece = piece * (HEAD_DIM ** -0.5)
                    cols.append(piece.astype(BF16))
        dproj = jnp.concatenate(cols, axis=1)
        dproj_ref[...] = dproj
        dh = _dot(dproj, w_ref[...], NN)
        xv = x_ref[...]
        r = _rstd(xv)
        n = xv * r
        dng = dh * (1.0 + sc_ref[...])
        dx_ref[...] = dx1_ref[...] + _norm_bwd(dng * g_ref[...], n, r)
        sums = jnp.concatenate([jnp.sum(dh, axis=0, keepdims=True), jnp.sum(dh * (n * g_ref[...]), axis=0, keepdims=True),
                                jnp.sum(dng * n, axis=0, keepdims=True), jnp.zeros((5, d), F32)], axis=0)

        @pl.when(i == 0)
        def _():
            sums_ref[...] = sums

        @pl.when(i > 0)
        def _():
            sums_ref[...] += sums

    tile = lambda w: pl.BlockSpec((tm, w), lambda i: (i, 0))
    slab = pl.BlockSpec((2, tm, LANES), lambda i: (0, i, 0))
    vec = pl.BlockSpec((1, d), lambda i: (0, 0))
    return pl.pallas_call(
        body, name="inproj_bwd", grid=(n_t,),
        in_specs=[tile(256)] + [slab] * 9 + [pl.BlockSpec((3, tm, LANES), lambda i: (0, i, 0)),
                                             pl.BlockSpec((n_proj, d), lambda i: (0, 0)), tile(d), tile(d), vec, vec],
        out_specs=[tile(n_proj), tile(d), pl.BlockSpec((8, d), lambda i: (0, 0))],
        out_shape=[jax.ShapeDtypeStruct((s_len, n_proj), BF16), jax.ShapeDtypeStruct((s_len, d), F32),
                   jax.ShapeDtypeStruct((8, d), F32)],
        compiler_params=_params(("arbitrary",)),
    )(du, *dqkv, rope, w_in_t, x, dx1, sc_m, g_pre_mix)


def _wgrad(a, b, name, tk, tmm):
    s_len, m = a.shape
    n = b.shape[1]
    n_k = s_len // tk

    def body(a_ref, b_ref, o_ref, acc_ref):
        k = pl.program_id(1)
        part = _dot(a_ref[...], b_ref[...], TN)

        @pl.when(k == 0)
        def _():
            acc_ref[...] = part

        @pl.when(k > 0)
        def _():
            acc_ref[...] += part

        @pl.when(k == n_k - 1)
        def _():
            o_ref[...] = acc_ref[...].astype(BF16)

    return pl.pallas_call(
        body, name=name, grid=(m // tmm, n_k),
        in_specs=[pl.BlockSpec((tk, tmm), lambda j, k: (k, j)), pl.BlockSpec((tk, n), lambda j, k: (k, 0))],
        out_specs=pl.BlockSpec((tmm, n), lambda j, k: (j, 0)),
        out_shape=jax.ShapeDtypeStruct((m, n), BF16),
        scratch_shapes=[pltpu.VMEM((tmm, n), F32)],
        compiler_params=_params(("arbitrary", "arbitrary")),
    )(a, b)


def _place():
    return lax.axis_index("x"), lax.axis_index("y"), lax.axis_index("c")


def _peer(k):
    x, y, c = _place()
    bx, by, bc = (k >> 2) & 1, (k >> 1) & 1, k & 1
    return (x ^ bx if bx else x, y ^ by if by else y, c ^ bc if bc else c)


def _index(pos):
    return 4 * pos[0] + 2 * pos[1] + pos[2]


def _ada_exchange(c_rows, w_ada, b_ada_cols, taps):
    d = c_rows.shape[1]
    ncol = w_ada.shape[1]

    def body(c_ref, w_ref, b_ref, t_ref, call_ref, mod_ref, tall_ref, stage_ref, send_sems, recv_sems):
        me = _index(_place())
        call_ref[me] = c_ref[...]
        tall_ref[me] = t_ref[...]

        def gather(k):
            return pltpu.make_async_remote_copy(
                src_ref=c_ref, dst_ref=call_ref.at[me], send_sem=send_sems.at[0, k - 1], recv_sem=recv_sems.at[0, k - 1],
                device_id=_peer(k), device_id_type=MESH)

        def gather_taps(k):
            return pltpu.make_async_remote_copy(
                src_ref=t_ref, dst_ref=tall_ref.at[me], send_sem=send_sems.at[2, k - 1], recv_sem=recv_sems.at[2, k - 1],
                device_id=_peer(k), device_id_type=MESH)

        for k in range(1, N_DEV):
            gather(k).start()
        for k in range(1, N_DEV):
            gather_taps(k).start()
        for k in range(1, N_DEV):
            gather(k).wait_recv()
        cv = jnp.concatenate([call_ref[b, 0:1, :] for b in range(N_DEV)], axis=0)
        act = cv * jax.nn.sigmoid(cv)
        mod = lax.dot_general(act, w_ref[...], NN, preferred_element_type=F32,
                              precision=lax.Precision.HIGHEST) + b_ref[...]
        for b in range(N_DEV):
            stage_ref[b] = jnp.broadcast_to(mod[b:b + 1, :], (8, ncol))
        mod_ref[me] = stage_ref[me]

        def scatter(k):
            return pltpu.make_async_remote_copy(
                src_ref=stage_ref.at[_index(_peer(k))], dst_ref=mod_ref.at[me],
                send_sem=send_sems.at[1, k - 1], recv_sem=recv_sems.at[1, k - 1],
                device_id=_peer(k), device_id_type=MESH)

        for k in range(1, N_DEV):
            scatter(k).start()
        for k in range(1, N_DEV):
            scatter(k).wait_recv()
        for k in range(1, N_DEV):
            gather_taps(k).wait_recv()
        for k in range(1, N_DEV):
            gather(k).wait_send()
            scatter(k).wait_send()
            gather_taps(k).wait_send()

    vmem = pl.BlockSpec(memory_space=pltpu.VMEM)
    return pl.pallas_call(
        body, name="ada_exchange",
        in_specs=[vmem] * 4, out_specs=[vmem] * 3,
        out_shape=[jax.ShapeDtypeStruct((N_DEV, 8, d), F32), jax.ShapeDtypeStruct((N_DEV, 8, ncol), F32),
                   jax.ShapeDtypeStruct((N_DEV,) + taps.shape, F32)],
        scratch_shapes=[pltpu.VMEM((N_DEV, 8, ncol), F32), pltpu.SemaphoreType.DMA((3, N_DEV - 1)),
                        pltpu.SemaphoreType.DMA((3, N_DEV - 1))],
        compiler_params=_params(),
    )(c_rows, w_ada, b_ada_cols, taps)


def _gather_weights(shards):
    n_w = len(shards)

    def body(*refs):
        srcs, outs = refs[:n_w], refs[n_w:2 * n_w]
        send_sems, recv_sems, local_sems = refs[2 * n_w:]
        x, y, c = _place()
        me, sibling = (x, y, c), (x, y, 1 - c)
        chips = [(1 - x, y), (x, 1 - y), (1 - x, 1 - y)]

        def rows(w, pos):
            r = shards[w].shape[0]
            return outs[w].at[pl.ds(pl.multiple_of(_index(pos) * r, 16), r), :]

        def copy(k, w, block, to, own=False):
            return pltpu.make_async_remote_copy(
                src_ref=srcs[w] if own else rows(w, block), dst_ref=rows(w, block),
                send_sem=send_sems.at[k, w], recv_sem=recv_sems.at[k, w], device_id=to, device_id_type=MESH)

        mine = [pltpu.make_async_copy(srcs[w], rows(w, me), local_sems.at[w]) for w in range(n_w)]
        for cp in mine:
            cp.start()
        first = [copy(0, w, me, sibling, own=True) for w in range(n_w)]
        first += [copy(1 + j, w, me, (*chip, c), own=True) for j, chip in enumerate(chips) for w in range(n_w)]
        for cp in first:
            cp.start()
        passed = []
        for j, chip in enumerate(chips):
            for w in range(n_w):
                copy(1 + j, w, (*chip, c), me).wait_recv()
                fwd = copy(4 + j, w, (*chip, c), sibling)
                fwd.start()
                passed.append(fwd)
        for w in range(n_w):
            copy(0, w, sibling, me).wait_recv()
        for j, chip in enumerate(chips):
            for w in range(n_w):
                copy(4 + j, w, (*chip, 1 - c), me).wait_recv()
        for cp in first + passed:
            cp.wait_send()
        for cp in mine:
            cp.wait()

    hbm = pl.BlockSpec(memory_space=pltpu.HBM)
    return pl.pallas_call(
        body, name="gather_weights",
        in_specs=[hbm] * n_w, out_specs=[hbm] * n_w,
        out_shape=[jax.ShapeDtypeStruct((N_DEV * s.shape[0], s.shape[1]), s.dtype) for s in shards],
        scratch_shapes=[pltpu.SemaphoreType.DMA((N_DEV - 1, n_w)), pltpu.SemaphoreType.DMA((N_DEV - 1, n_w)),
                        pltpu.SemaphoreType.DMA((n_w,))],
        compiler_params=_params(),
    )(*shards)


def _scatter_grads(grads):
    n_w = len(grads)

    def body(*refs):
        srcs, outs = refs[:n_w], refs[n_w:2 * n_w]
        send_sems, recv_sems, local_sems = refs[2 * n_w:]
        me = _index(_place())

        def slab(w, dev):
            r = grads[w].shape[0] // N_DEV
            return srcs[w].at[pl.ds(pl.multiple_of(dev * r, 16), r), :]

        def copy(k, w):
            return pltpu.make_async_remote_copy(
                src_ref=slab(w, _index(_peer(k))), dst_ref=outs[w].at[me],
                send_sem=send_sems.at[k - 1, w], recv_sem=recv_sems.at[k - 1, w],
                device_id=_peer(k), device_id_type=MESH)

        mine = [pltpu.make_async_copy(slab(w, me), outs[w].at[me], local_sems.at[w]) for w in range(n_w)]
        for cp in mine:
            cp.start()
        sends = [copy(k, w) for k in range(1, N_DEV) for w in range(n_w)]
        for cp in sends:
            cp.start()
        for cp in sends:
            cp.wait_recv()
        for cp in sends:
            cp.wait_send()
        for cp in mine:
            cp.wait()

    hbm = pl.BlockSpec(memory_space=pltpu.HBM)
    return pl.pallas_call(
        body, name="scatter_grads",
        in_specs=[hbm] * n_w, out_specs=[hbm] * n_w,
        out_shape=[jax.ShapeDtypeStruct((N_DEV, g.shape[0] // N_DEV, g.shape[1]), g.dtype) for g in grads],
        scratch_shapes=[pltpu.SemaphoreType.DMA((N_DEV - 1, n_w)), pltpu.SemaphoreType.DMA((N_DEV - 1, n_w)),
                        pltpu.SemaphoreType.DMA((n_w,))],
        compiler_params=_params(),
    )(*grads)


def _peer_copies(mode, srcs, lands, send_sems, recv_sems):
    me = _index(_place())
    copies = []
    for k in range(1, N_DEV):
        peer = _peer(k)
        for w, (src, land) in enumerate(zip(srcs, lands)):
            if mode == "gather":
                r = src.shape[0]
                dst = land.at[pl.ds(pl.multiple_of(me * r, 16), r), :]
            else:
                r = src.shape[0] // N_DEV
                src = src.at[pl.ds(pl.multiple_of(_index(peer) * r, 16), r), :]
                dst = land.at[me]
            copies.append(pltpu.make_async_remote_copy(
                src_ref=src, dst_ref=dst, send_sem=send_sems.at[(k - 1) * len(srcs) + w],
                recv_sem=recv_sems.at[(k - 1) * len(srcs) + w],
                device_id=peer, device_id_type=MESH))
    return copies


def _exchange_start(mode, srcs, lands, name):
    n = len(srcs)

    def body(*refs):
        for cp in _peer_copies(mode, refs[:n], refs[n:2 * n], refs[2 * n], refs[2 * n + 1]):
            cp.start()
        refs[-1][...] = jnp.zeros_like(refs[-1])

    hbm, sem = pl.BlockSpec(memory_space=pltpu.HBM), pl.BlockSpec(memory_space=pltpu.SEMAPHORE)
    arrays = list(srcs) + list(lands)
    out = pl.pallas_call(
        body, name=name,
        out_shape=(pltpu.SemaphoreType.DMA(((N_DEV - 1) * n,)), pltpu.SemaphoreType.DMA(((N_DEV - 1) * n,)),
                   *[pltpu.HBM(a.shape, a.dtype) for a in arrays], jax.ShapeDtypeStruct((8, LANES), F32)),
        in_specs=[hbm] * (2 * n), out_specs=(sem, sem, *[hbm] * (2 * n), pl.BlockSpec(memory_space=pltpu.VMEM)),
        input_output_aliases={i: 2 + i for i in range(2 * n)},
        compiler_params=pltpu.CompilerParams(has_side_effects=pltpu.SideEffectType.DATAFLOW_SIDE_EFFECTING),
    )(*[pltpu.with_memory_space_constraint(a, pltpu.HBM) for a in arrays])
    return out[0], out[1], out[2:2 + n], out[2 + n:2 + 2 * n], out[-1]


def _exchange_wait(mode, send_sems, recv_sems, srcs, lands, after, name):
    n = len(srcs)

    def body(*refs):
        me = _index(_place())
        local = []
        for w, (src, land) in enumerate(zip(refs[:n], refs[n:2 * n])):
            if mode == "gather":
                r = src.shape[0]
                local.append(pltpu.make_async_copy(src, land.at[pl.ds(pl.multiple_of(me * r, 16), r), :], refs[-1].at[w]))
            else:
                r = src.shape[0] // N_DEV
                local.append(pltpu.make_async_copy(src.at[pl.ds(pl.multiple_of(me * r, 16), r), :], land.at[me], refs[-1].at[w]))
        for cp in local:
            cp.start()
        copies = _peer_copies(mode, refs[:n], refs[n:2 * n], refs[2 * n], refs[2 * n + 1])
        for cp in copies:
            cp.wait_send()
        for cp in copies:
            cp.wait_recv()
        for cp in local:
            cp.wait()

    hbm, sem = pl.BlockSpec(memory_space=pltpu.HBM), pl.BlockSpec(memory_space=pltpu.SEMAPHORE)
    arrays = list(srcs) + list(lands)
    out = pl.pallas_call(
        body, name=name, out_shape=tuple(pltpu.HBM(a.shape, a.dtype) for a in arrays),
        scratch_shapes=[pltpu.SemaphoreType.DMA((n,))],
        in_specs=[hbm] * (2 * n) + [sem, sem, pl.BlockSpec(memory_space=pl.ANY)], out_specs=tuple([hbm] * (2 * n)),
        input_output_aliases={i: i for i in range(2 * n)},
        compiler_params=pltpu.CompilerParams(has_side_effects=pltpu.SideEffectType.DATAFLOW_SIDE_EFFECTING),
    )(*arrays, send_sems, recv_sems, after)
    return out[n:]


def _allreduce_small(packed, name):
    rows = packed.shape[0]

    def body(p_ref, all_ref, tot_ref, send_sems, recv_sems):
        me = _index(_place())
        all_ref[me] = p_ref[...]

        def copy(k):
            return pltpu.make_async_remote_copy(
                src_ref=p_ref, dst_ref=all_ref.at[me], send_sem=send_sems.at[k - 1], recv_sem=recv_sems.at[k - 1],
                device_id=_peer(k), device_id_type=MESH)

        for k in range(1, N_DEV):
            copy(k).start()
        for k in range(1, N_DEV):
            copy(k).wait_recv()
        tot = all_ref[0]
        for dev in range(1, N_DEV):
            tot = tot + all_ref[dev]
        tot_ref[...] = tot
        for k in range(1, N_DEV):
            copy(k).wait_send()

    vmem = pl.BlockSpec(memory_space=pltpu.VMEM)
    return pl.pallas_call(
        body, name=name, in_specs=[vmem], out_specs=[vmem, vmem],
        out_shape=[jax.ShapeDtypeStruct((N_DEV, rows, LANES), F32), jax.ShapeDtypeStruct((rows, LANES), F32)],
        scratch_shapes=[pltpu.SemaphoreType.DMA((N_DEV - 1,)), pltpu.SemaphoreType.DMA((N_DEV - 1,))],
        compiler_params=_params(),
    )(packed)


def _adam_math(w, g, m, v):
    m = ADAM_B1 * m + (1.0 - ADAM_B1) * g
    v = ADAM_B2 * v + (1.0 - ADAM_B2) * (g * g)
    m_hat = m / (1.0 - ADAM_B1 ** ADAM_STEP)
    v_hat = v / (1.0 - ADAM_B2 ** ADAM_STEP)
    delta = -ADAM_LR * (m_hat / (jnp.sqrt(v_hat) + ADAM_EPS) + ADAM_WD * w)
    return delta, m, v


def _adam(w, g, m, v, name, tr):
    rows, cols = w.shape

    def body(w_ref, g_ref, m_ref, v_ref, d_ref, nm_ref, nv_ref):
        d_ref[...], nm_ref[...], nv_ref[...] = _adam_math(w_ref[...], g_ref[...], m_ref[...], v_ref[...])

    spec = pl.BlockSpec((tr, cols), lambda i: (i, 0))
    shape = jax.ShapeDtypeStruct((rows, cols), F32)
    return pl.pallas_call(
        body, name=name, grid=(rows // tr,), in_specs=[spec] * 4, out_specs=[spec] * 3,
        out_shape=[shape] * 3, compiler_params=_params(("arbitrary",)),
    )(w, g, m, v)


def _sum_adam(parts, w, m, v, name, tr):
    _, rows, cols = parts.shape

    def body(p_ref, w_ref, m_ref, v_ref, g_ref, d_ref, nm_ref, nv_ref):
        g = p_ref[0].astype(F32)
        for dev in range(1, N_DEV):
            g = g + p_ref[dev].astype(F32)
        g_ref[...] = g
        d_ref[...], nm_ref[...], nv_ref[...] = _adam_math(w_ref[...], g, m_ref[...], v_ref[...])

    spec = pl.BlockSpec((tr, cols), lambda i: (i, 0))
    shape = jax.ShapeDtypeStruct((rows, cols), F32)
    return pl.pallas_call(
        body, name=name, grid=(rows // tr,),
        in_specs=[pl.BlockSpec((N_DEV, tr, cols), lambda i: (0, i, 0)), spec, spec, spec],
        out_specs=[spec] * 4, out_shape=[shape] * 4, compiler_params=_params(("arbitrary",)),
    )(parts, w, m, v)


def _ada_grad_adam(c_all, dmod_cols, w, m, v, tr):
    rows, cols = w.shape

    def body(c_ref, dm_ref, w_ref, m_ref, v_ref, g_ref, d_ref, nm_ref, nv_ref):
        cv = c_ref[...]
        act = cv * jax.nn.sigmoid(cv)
        g = lax.dot_general(act, dm_ref[...], TN, preferred_element_type=F32, precision=lax.Precision.HIGHEST)
        g_ref[...] = g
        d_ref[...], nm_ref[...], nv_ref[...] = _adam_math(w_ref[...], g, m_ref[...], v_ref[...])

    spec = pl.BlockSpec((tr, cols), lambda i: (i, 0))
    shape = jax.ShapeDtypeStruct((rows, cols), F32)
    return pl.pallas_call(
        body, name="ada_grad_adam", grid=(rows // tr,),
        in_specs=[pl.BlockSpec((N_DEV, tr), lambda i: (0, i)), pl.BlockSpec((N_DEV, cols), lambda i: (0, 0)), spec, spec, spec],
        out_specs=[spec] * 4, out_shape=[shape] * 4, compiler_params=_params(("arbitrary",)),
    )(c_all, dmod_cols, w, m, v)


def _rope_tables(positions):
    s_len = positions.shape[0]
    inv_freq = ROPE_THETA ** (-jnp.arange(0, 2 * ROT_HALF, 2, dtype=F32) / (2 * ROT_HALF))
    ang = positions.astype(F32)[:, None] * inv_freq
    cos, sin = jnp.cos(ang), jnp.sin(ang)
    rest = HEAD_DIM - 2 * ROT_HALF
    zero = lambda n: jnp.zeros((s_len, n), F32)
    head = jnp.stack([jnp.concatenate([cos, cos, jnp.ones((s_len, rest), F32)], axis=1),
                      jnp.concatenate([-sin, zero(HEAD_DIM - ROT_HALF)], axis=1),
                      jnp.concatenate([zero(ROT_HALF), sin, zero(rest)], axis=1)])
    return jnp.tile(head, (1, 1, LANES // HEAD_DIM))


def _pad_rows(a, rows):
    return jnp.pad(a, ((0, rows - a.shape[0]), (0, 0)))


def _as_rows(a, rows):
    flat = a.reshape(-1)
    return jnp.pad(flat, (0, rows * LANES - flat.shape[0])).reshape(rows, LANES)


def _sequence_step(xs, target, rope, mods, gains, w_in_t, w_out_t, fetch_ffn, send_ffn_grads, w_blk_b, b_pool_r,
                   pool_scale_r, conv_w_all, conv_b):
    sh_m, sc_m, gt_m, sh_f, sc_f, gt_f = mods
    g_pre_mix, g_post_mix, g_pre_ffn, g_post_ffn = gains
    h1, u_pool, qkv = _premix_inproj(xs, sh_m, sc_m, g_pre_mix, w_in_t, rope, tm=512)
    o_g, lse_g = [], []
    for gi, dil in enumerate(DILATIONS):
        o, lse = _attn_fwd(qkv, gi, dil)
        o_g.append(o)
        lse_g.append(lse)
    x1, y1, h2, cat, attn, lse_all = _mix_out(xs, u_pool, o_g, lse_g, w_blk_b, b_pool_r, pool_scale_r, w_out_t,
                                              gt_m, g_post_mix, g_pre_ffn, sc_f, sh_f, tm=256)
    w_up_t, w_down_f = fetch_ffn(x1)
    gate, val, dy2, dout, sums_ffn, loss_loc = _ffn_fwd_loss(h2, x1, target, w_up_t, w_down_f, conv_w_all, conv_b,
                                                              gt_f, g_post_ffn, tm=256, tf=2816, ck=256)

    dgc, dval, dw_down, dconv_w, dconv_b = _ffn_bwd_act(dy2, gate, val, w_down_f, conv_w_all, conv_b, tm=1024, tf=256, ck=1024)
    dup, dh2 = _ffn_bwd_up(dgc, dval, w_up_t, conv_w_all, tm=256)
    dw_up_t = _wgrad(dup, h2, "wgrad_up", tk=1024, tmm=1408)
    token = send_ffn_grads(dw_up_t, dw_down)
    if token is not None:
        sc_f = sc_f + token[0:1, 0:1]
    dx1, dpool, dattn, delta, dw_out_t, sums_mix = _mix_bwd(dh2, dout, x1, y1, cat, attn, w_out_t, sc_f, g_pre_ffn,
                                                           gt_m, g_post_mix, tm=256)
    du, dw_blk, sums_pool = _pool_bwd(dpool, u_pool, w_blk_b, b_pool_r, pool_scale_r, tm=512)
    dqkv = []
    for gi, dil in enumerate(DILATIONS):
        dqkv += list(_attn_bwd(qkv, dattn, lse_all, delta, gi, dil))
    dproj, grad_x, sums_in = _inproj_bwd(du, dqkv, rope, w_in_t, xs, dx1, sc_m, g_pre_mix, tm=256)
    dw_in_t = _wgrad(dproj, h1, "wgrad_in", tk=1024, tmm=1280)
    return (loss_loc, grad_x, dw_in_t, dw_out_t, dw_up_t, dw_down, dw_blk, dconv_w, dconv_b,
            sums_in, sums_mix, sums_ffn, sums_pool)


def kernel(x, c, positions, w_ada, b_ada, g_pre_mix, g_post_mix, g_pre_ffn, g_post_ffn, w_in, w_pool, b_pool, pool_scale, w_out, w_up, conv_w, conv_b, w_down, loss_target, m_w_ada, m_b_ada, m_g_pre_mix, m_g_post_mix, m_g_pre_ffn, m_g_post_ffn, m_w_in, m_w_pool, m_b_pool, m_pool_scale, m_w_out, m_w_up, m_conv_w, m_conv_b, m_w_down, v_w_ada, v_b_ada, v_g_pre_mix, v_g_post_mix, v_g_pre_ffn, v_g_post_ffn, v_w_in, v_w_pool, v_b_pool, v_pool_scale, v_w_out, v_w_up, v_conv_w, v_conv_b, v_w_down):
    s_len, d = x.shape[1], x.shape[2]
    d_ff = w_down.shape[1] * N_DEV
    me = _index(_place())
    xs, target = x[0], loss_target[0]

    ncol = w_ada.shape[2]
    b_cols = lax.dynamic_slice(b_ada, (0, me * ncol), (1, ncol))
    c_all, mod, taps_all = _ada_exchange(jnp.broadcast_to(c, (8, d)), w_ada[0], b_cols, _pad_rows(conv_w[0], 8))
    c_all = c_all[:, 0, :]
    conv_w_all = jnp.transpose(taps_all[:, :3, :], (1, 0, 2)).reshape(3, d_ff)
    sh_m, sc_m, gt_m, sh_f, sc_f, gt_f = [mod[:, 0, :].reshape(1, -1)[:, k * d:(k + 1) * d] for k in range(6)]

    w_in_t, w_out_t = _gather_weights([w_in[0].T.astype(BF16), w_out[0].T.astype(BF16)])

    rope = _rope_tables(positions[0])
    w_blk = jnp.zeros((256, 256), F32)
    for gi in range(4):
        w_blk = lax.dynamic_update_slice(w_blk, w_pool[0, gi], (gi * HEAD_DIM, gi * HEAD_DIM))
    w_blk_b = w_blk.astype(BF16)
    b_pool_r, pool_scale_r = b_pool.reshape(1, 256), pool_scale.reshape(1, 256)

    cw_rows = d_ff // LANES

    up_sh, down_sh = w_up[0].T.astype(BF16), w_down[0].astype(BF16)
    w_in_t, conv_w_all, up_sh, down_sh = lax.optimization_barrier((w_in_t, conv_w_all, up_sh, down_sh))
    lands = [lax.empty((N_DEV * s.shape[0], s.shape[1]), BF16) for s in (up_sh, down_sh)]
    w_send, w_recv, w_src, w_land, w_token = _exchange_start("gather", [up_sh, down_sh], lands, "ffn_weights_start")

    def fetch_ffn(after):
        return _exchange_wait("gather", w_send, w_recv, w_src, w_land, after, "ffn_weights_wait")

    flight = []

    def send_ffn_grads(dw_up_t, dw_down):
        lands = [lax.empty((N_DEV, g.shape[0] // N_DEV, g.shape[1]), BF16) for g in (dw_up_t, dw_down)]
        flight.extend(_exchange_start("scatter", [dw_up_t, dw_down], lands, "ffn_grads_start"))
        return flight[4]

    (loss_loc, grad_x, dw_in_t, dw_out_t, _, _, dw_blk, dconv_w, dconv_b,
     sums_in, sums_mix, sums_ffn, sums_pool) = _sequence_step(
        xs, target, rope, (sh_m + w_token[0:1, 0:1], sc_m, gt_m, sh_f, sc_f, gt_f),
        (g_pre_mix, g_post_mix, g_pre_ffn, g_post_ffn),
        w_in_t, w_out_t, fetch_ffn, send_ffn_grads, w_blk_b, b_pool_r, pool_scale_r, conv_w_all, conv_b)

    parts_ffn = _exchange_wait("scatter", *flight[:4], dw_in_t, "ffn_grads_wait")
    dw_in_t, dw_out_t, *parts_ffn = lax.optimization_barrier((dw_in_t, dw_out_t, *parts_ffn))
    parts_mix = _scatter_grads([dw_in_t, dw_out_t])
    big = {
        "w_in": [a.T for a in _sum_adam(parts_mix[0], w_in[0].T, m_w_in[0].T, v_w_in[0].T, "adam_w_in", 64)],
        "w_out": [a.T for a in _sum_adam(parts_mix[1], w_out[0].T, m_w_out[0].T, v_w_out[0].T, "adam_w_out", 128)],
        "w_up": [a.T for a in _sum_adam(parts_ffn[0], w_up[0].T, m_w_up[0].T, v_w_up[0].T, "adam_w_up", 64)],
        "w_down": _sum_adam(parts_ffn[1], w_down[0], m_w_down[0], v_w_down[0], "adam_w_down", 32),
    }

    dmod = jnp.concatenate([sums_in[0:1], sums_in[1:2], sums_mix[3:4], sums_mix[0:1], sums_mix[1:2], sums_ffn[0:1]], axis=1)
    dw_pool = jnp.stack([dw_blk[gi * HEAD_DIM:(gi + 1) * HEAD_DIM, gi * HEAD_DIM:(gi + 1) * HEAD_DIM] for gi in range(4)])
    pieces = [(dmod, 48), (sums_in[2:3], 8), (sums_mix[4:5], 8), (sums_mix[2:3], 8), (sums_ffn[1:2], 8),
              (dw_pool, 128), (sums_pool[0:1], 8), (sums_pool[1:2], 8), (dconv_b, 24),
              (dconv_w[0:1], 24), (dconv_w[1:2], 24), (dconv_w[2:3], 24), (loss_loc[0:1], 8)]
    packed = jnp.concatenate([_as_rows(a, r) for a, r in pieces], axis=0)
    gathered, total = _allreduce_small(packed, "allreduce_small")
    n_rep = 248
    rep_w = [b_ada, g_pre_mix, g_post_mix, g_pre_ffn, g_post_ffn, w_pool, b_pool, pool_scale, conv_b]
    rep_m = [m_b_ada, m_g_pre_mix, m_g_post_mix, m_g_pre_ffn, m_g_post_ffn, m_w_pool, m_b_pool, m_pool_scale, m_conv_b]
    rep_v = [v_b_ada, v_g_pre_mix, v_g_post_mix, v_g_pre_ffn, v_g_post_ffn, v_w_pool, v_b_pool, v_pool_scale, v_conv_b]
    rep_rows = [r for _, r in pieces[:9]]
    pack_rep = lambda arrs: jnp.concatenate([_as_rows(a, r) for a, r in zip(arrs, rep_rows)], axis=0)
    rep_g = total[:n_rep]
    rep_d, rep_nm, rep_nv = _adam(pack_rep(rep_w), rep_g, pack_rep(rep_m), pack_rep(rep_v), "adam_small", n_rep)

    def unpack(p):
        out, row = [], 0
        for a, r in zip(rep_w, rep_rows):
            out.append(p[row:row + r].reshape(-1)[:a.size].reshape(a.shape))
            row += r
        return out

    g_rep, d_rep, nm_rep, nv_rep = unpack(rep_g), unpack(rep_d), unpack(rep_nm), unpack(rep_nv)

    fcol = d_ff // N_DEV
    g_cw_full = jnp.concatenate([total[n_rep + 24 * k:n_rep + 24 * k + cw_rows].reshape(1, d_ff) for k in range(3)], axis=0)
    g_cw = lax.dynamic_slice(g_cw_full, (0, me * fcol), (3, fcol))
    d_cw, nm_cw, nv_cw = _adam(conv_w[0], g_cw, m_conv_w[0], v_conv_w[0], "adam_conv_w", 3)

    dmod_all = gathered[:, :48].reshape(N_DEV, 6 * d)
    dmod_cols = lax.dynamic_slice(dmod_all, (0, me * ncol), (N_DEV, ncol))
    g_ada, d_ada, nm_ada, nv_ada = _ada_grad_adam(c_all, dmod_cols, w_ada[0], m_w_ada[0], v_w_ada[0], 256)

    loss = total[n_rep + 72, 0]

    def group(k):
        rep = (g_rep, d_rep, nm_rep, nv_rep)[k]
        ada = (g_ada, d_ada, nm_ada, nv_ada)[k][None]
        cw = (g_cw, d_cw, nm_cw, nv_cw)[k][None]
        return [ada, rep[0], rep[1], rep[2], rep[3], rep[4], big["w_in"][k][None], rep[5], rep[6], rep[7],
                big["w_out"][k][None], big["w_up"][k][None], cw, rep[8], big["w_down"][k][None]]

    return (loss, grad_x[None], *group(0), *group(1), *group(2), *group(3))
```

```python
import functools
import math

import jax
import jax.numpy as jnp
from jax import lax
from jax.experimental import pallas as pl
from jax.experimental.pallas import tpu as pltpu

F32 = jnp.float32
BF16 = jnp.bfloat16
MESH = pl.DeviceIdType.MESH

N_DEV = 8
HEAD_DIM = 64
ROT_HALF = 8
ROPE_THETA = 500000.0
POOL_WINDOWS = (2, 4, 8, 16)
DILATIONS = (1, 4, 16)
BLOCK = 128
NORM_EPS = 1e-6
HALO = 16
MASKED = -1e30
ATTN_FWD_UNROLL = 4
ATTN_BWD_UNROLL = 2

ADAM_LR = 0.001
ADAM_B1 = 0.9
ADAM_B2 = 0.999
ADAM_EPS = 1e-08
ADAM_WD = 0.01
ADAM_STEP = 10

V7X_VMEM_LIMIT = 56 * 1024 * 1024
LANES = 128

NT = (((1,), (1,)), ((), ()))
NN = (((1,), (0,)), ((), ()))
TN = (((0,), (0,)), ((), ()))


def _dot(a, b, dims):
    return lax.dot_general(a, b, dims, preferred_element_type=F32)


def _params(sem=None, vmem=V7X_VMEM_LIMIT):
    if sem is None:
        return pltpu.CompilerParams(vmem_limit_bytes=vmem)
    return pltpu.CompilerParams(dimension_semantics=sem, vmem_limit_bytes=vmem)


def _rstd(v):
    return lax.rsqrt(jnp.mean(v * v, axis=-1, keepdims=True) + NORM_EPS)


def _norm_bwd(dn, n, rstd):
    return rstd * (dn - n * jnp.mean(dn * n, axis=-1, keepdims=True))


def _rope_fwd(p, rope_ref):
    return p * rope_ref[0] + pltpu.roll(p, LANES - ROT_HALF, 1) * rope_ref[1] + pltpu.roll(p, ROT_HALF, 1) * rope_ref[2]


def _rope_bwd(dp, rope_ref):
    return dp * rope_ref[0] + pltpu.roll(dp * rope_ref[1], ROT_HALF, 1) + pltpu.roll(dp * rope_ref[2], LANES - ROT_HALF, 1)


def _gelu_parts(v):
    k = math.sqrt(2.0 / math.pi)
    t = jnp.tanh(k * (v + 0.044715 * v * v * v))
    g = 0.5 * v * (1.0 + t)
    dg = 0.5 * (1.0 + t) + 0.5 * v * (1.0 - t * t) * k * (1.0 + 3.0 * 0.044715 * v * v)
    return g, dg


def _halo_before(i, tile):
    return jnp.maximum(i * (tile // HALO) - 1, 0)


def _premix_inproj(x, sh, sc, g, w_in_t, rope, tm):
    s_len, d = x.shape
    n_proj = w_in_t.shape[0]
    n_slab = (n_proj - 256) // LANES

    def body(x_ref, sh_ref, sc_ref, g_ref, w_ref, rope_ref, h_ref, up_ref, qkv_ref):
        xv = x_ref[...]
        h = (xv * _rstd(xv) * g_ref[...]) * (1.0 + sc_ref[...]) + sh_ref[...]
        hb = h.astype(BF16)
        h_ref[...] = hb
        up_ref[...] = _dot(hb, w_ref[0:256, :], NT)
        for pair in range(n_slab // 2):
            p = _dot(hb, w_ref[256 + 256 * pair:512 + 256 * pair, :], NT)
            for half in range(2):
                ph = p[:, half * LANES:(half + 1) * LANES]
                if pair < 6:
                    ph = _rope_fwd(ph, rope_ref)
                if pair < 3:
                    ph = ph * (HEAD_DIM ** -0.5)
                qkv_ref[2 * pair + half] = ph

    vec = pl.BlockSpec((1, d), lambda i: (0, 0))
    return pl.pallas_call(
        body, name="premix_inproj", grid=(s_len // tm,),
        in_specs=[pl.BlockSpec((tm, d), lambda i: (i, 0)), vec, vec, vec,
                  pl.BlockSpec((n_proj, d), lambda i: (0, 0)),
                  pl.BlockSpec((3, tm, LANES), lambda i: (0, i, 0))],
        out_specs=[pl.BlockSpec((tm, d), lambda i: (i, 0)),
                   pl.BlockSpec((tm, 256), lambda i: (i, 0)),
                   pl.BlockSpec((n_slab, tm, LANES), lambda i: (0, i, 0))],
        out_shape=[jax.ShapeDtypeStruct((s_len, d), BF16),
                   jax.ShapeDtypeStruct((s_len, 256), F32),
                   jax.ShapeDtypeStruct((n_slab, s_len, LANES), F32)],
        compiler_params=_params(("arbitrary",)),
    )(x, sh, sc, g, w_in_t, rope)


def _block_rows(n, r, dil):
    start = n * (BLOCK * dil) + r
    if dil == 1:
        return pl.ds(pl.multiple_of(start, BLOCK), BLOCK)
    return pl.ds(start, BLOCK, stride=dil)


def _band_mask(n):
    ri = lax.broadcasted_iota(jnp.int32, (BLOCK, 2 * BLOCK), 0)
    cj = lax.broadcasted_iota(jnp.int32, (BLOCK, 2 * BLOCK), 1)
    cur = (cj >= BLOCK) & (cj - BLOCK <= ri)
    prev = (cj < BLOCK) & (cj >= ri) & (n > 0)
    return cur | prev


def _attn_fwd(qkv, group, dil):
    s_len = qkv.shape[1]
    nb = s_len // (BLOCK * dil)

    def body(q_ref, k_ref, v_ref, o_ref, lse_ref):
        lane = lax.broadcasted_iota(jnp.int32, (BLOCK, LANES), 1)
        first = lane < HEAD_DIM

        def block(t, carry):
            r, n = t // nb, t % nb
            cur = _block_rows(n, r, dil)
            prev = _block_rows(jnp.maximum(n - 1, 0), r, dil)
            q = q_ref[0, cur, :]
            kcat = jnp.concatenate([k_ref[0, prev, :], k_ref[0, cur, :]], axis=0).astype(BF16)
            vcat = jnp.concatenate([v_ref[0, prev, :], v_ref[0, cur, :]], axis=0).astype(BF16)
            valid = _band_mask(n)
            q2 = jnp.concatenate([jnp.where(first, q, 0.0), jnp.where(first, 0.0, q)], axis=0).astype(BF16)
            s = jnp.where(jnp.concatenate([valid, valid], axis=0), _dot(q2, kcat, NT), MASKED)
            m = jnp.max(s, axis=-1, keepdims=True)
            p = jnp.exp(s - m)
            den = jnp.sum(p, axis=-1, keepdims=True)
            o2 = _dot(p.astype(BF16), vcat, NN) / den
            lse2 = m + jnp.log(den)
            o_ref[0, cur, :] = jnp.where(first, o2[:BLOCK], o2[BLOCK:])
            lse_ref[0, cur, :] = jnp.where(first, lse2[:BLOCK], lse2[BLOCK:])
            return carry

        lax.fori_loop(0, nb * dil, block, 0, unroll=ATTN_FWD_UNROLL)

    def slab(base):
        return pl.BlockSpec((1, s_len, LANES), lambda s: (base + 2 * group + s, 0, 0))

    out = pl.BlockSpec((1, s_len, LANES), lambda s: (s, 0, 0))
    shape = jax.ShapeDtypeStruct((2, s_len, LANES), F32)
    return pl.pallas_call(
        body, name=f"attn_fwd_d{dil}", grid=(2,),
        in_specs=[slab(0), slab(6), slab(12)], out_specs=[out, out], out_shape=[shape, shape],
        compiler_params=_params(("arbitrary",)),
    )(qkv, qkv, qkv)


def _pool_mixed(u, halo, i, tm):
    ue = jnp.concatenate([halo, u], axis=0)
    s2 = ue + pltpu.roll(ue, 1, 0)
    s4 = s2 + pltpu.roll(s2, 2, 0)
    s8 = s4 + pltpu.roll(s4, 4, 0)
    s16 = s8 + pltpu.roll(s8, 8, 0)
    grp = lax.broadcasted_iota(jnp.int32, (tm, 256), 1) // HEAD_DIM
    pick = lambda a, b, c, e: jnp.where(grp == 0, a, jnp.where(grp == 1, b, jnp.where(grp == 2, c, e)))
    win_sum = pick(s2[HALO:], s4[HALO:], s8[HALO:], s16[HALO:])
    pos = (i * tm + lax.broadcasted_iota(jnp.int32, (tm, 256), 0)).astype(F32)
    count = jnp.minimum(pos + 1.0, pick(*[float(w) for w in POOL_WINDOWS]))
    return win_sum / count - u, count


def _mix_out(x, u_pool, o_g, lse_g, w_blk, b_pool, pool_scale, w_out_t, gt_m, g_post_mix, g_pre_ffn, sc_f, sh_f, tm):
    s_len, d = x.shape

    def body(x_ref, u_ref, uh_ref, o0, o1, o2, l0, l1, l2, wb_ref, bp_ref, ps_ref, wo_ref,
             gt_ref, g1_ref, g2_ref, sc_ref, sh_ref,
             x1_ref, y1_ref, h2_ref, cat_ref, attn_ref, lall_ref):
        i = pl.program_id(0)
        u = u_ref[...]
        halo = uh_ref[...] * (i > 0).astype(F32)
        mixed, _ = _pool_mixed(u, halo, i, tm)
        y = _dot(mixed.astype(BF16), wb_ref[...], NN) + bp_ref[...]
        pool = y * ps_ref[...]
        attn = []
        for s in range(2):
            la, lb, lc = l0[s], l1[s], l2[s]
            mx = jnp.maximum(jnp.maximum(la, lb), lc)
            ea, eb, ec = jnp.exp(la - mx), jnp.exp(lb - mx), jnp.exp(lc - mx)
            den = ea + eb + ec
            lall_ref[s] = mx + jnp.log(den)
            attn.append((ea / den) * o0[s] + (eb / den) * o1[s] + (ec / den) * o2[s])
        attn = jnp.concatenate(attn, axis=1)
        attn_ref[...] = attn
        cat = jnp.concatenate([pool, attn], axis=1).astype(BF16)
        cat_ref[...] = cat
        y1 = _dot(cat, wo_ref[...], NT)
        y1_ref[...] = y1
        x1 = x_ref[...] + gt_ref[...] * (y1 * _rstd(y1) * g1_ref[...])
        x1_ref[...] = x1
        h2 = (x1 * _rstd(x1) * g2_ref[...]) * (1.0 + sc_ref[...]) + sh_ref[...]
        h2_ref[...] = h2.astype(BF16)

    tile = lambda w: pl.BlockSpec((tm, w), lambda i: (i, 0))
    slab = pl.BlockSpec((2, tm, LANES), lambda i: (0, i, 0))
    const = lambda a: pl.BlockSpec(a.shape, lambda i: (0,) * a.ndim)
    return pl.pallas_call(
        body, name="mix_out", grid=(s_len // tm,),
        in_specs=[tile(d), tile(256), pl.BlockSpec((HALO, 256), lambda i: (_halo_before(i, tm), 0)),
                  slab, slab, slab, slab, slab, slab,
                  const(w_blk), const(b_pool), const(pool_scale), const(w_out_t),
                  const(gt_m), const(g_post_mix), const(g_pre_ffn), const(sc_f), const(sh_f)],
        out_specs=[tile(d), tile(d), tile(d), tile(512), tile(256), slab],
        out_shape=[jax.ShapeDtypeStruct((s_len, d), F32), jax.ShapeDtypeStruct((s_len, d), F32),
                   jax.ShapeDtypeStruct((s_len, d), BF16), jax.ShapeDtypeStruct((s_len, 512), BF16),
                   jax.ShapeDtypeStruct((s_len, 256), F32), jax.ShapeDtypeStruct((2, s_len, LANES), F32)],
        compiler_params=_params(("arbitrary",)),
    )(x, u_pool, u_pool, *o_g, *lse_g, w_blk, b_pool, pool_scale, w_out_t, gt_m, g_post_mix, g_pre_ffn, sc_f, sh_f)


def _conv_gate(gate_ext, cw, cb):
    gc = gate_ext * cw[2:3, :] + pltpu.roll(gate_ext, 1, 0) * cw[1:2, :] + pltpu.roll(gate_ext, 2, 0) * cw[0:1, :]
    return gc[HALO:] + cb


def _ffn_fwd_loss(h2, x1, target, w_up_t, w_down, conv_w, conv_b, gt_f, g_post_ffn, tm, tf, ck):
    s_len, d = x1.shape
    d_ff = w_down.shape[0]
    n_f = d_ff // tf

    def body(h_ref, hh_ref, x1_ref, tgt_ref, wg_ref, wv_ref, wd_ref, cw_ref, cb_ref, gt_ref, g_ref,
             gate_ref, val_ref, dy2_ref, dout_ref, sums_ref, loss_ref, acc_ref):
        i, j = pl.program_id(0), pl.program_id(1)

        @pl.when((i == 0) & (j == 0))
        def _():
            sums_ref[...] = jnp.zeros_like(sums_ref)
            loss_ref[...] = jnp.zeros_like(loss_ref)

        h = h_ref[...]
        h_ext = jnp.concatenate([hh_ref[...], h], axis=0)
        row = lax.broadcasted_iota(jnp.int32, (tm + HALO, ck), 0)
        no_halo = (row < HALO) & (i == 0)
        part = None
        for c in range(tf // ck):
            cs = slice(c * ck, (c + 1) * ck)
            gate_ext = jnp.where(no_halo, 0.0, _dot(h_ext, wg_ref[cs, :], NT))
            val = _dot(h, wv_ref[cs, :], NT)
            act, _ = _gelu_parts(_conv_gate(gate_ext, cw_ref[:, cs], cb_ref[:, cs]))
            gate_ref[:, cs] = gate_ext[HALO:].astype(BF16)
            val_ref[:, cs] = val.astype(BF16)
            p = _dot((act * val).astype(BF16), wd_ref[cs, :], NN)
            part = p if part is None else part + p

        @pl.when(j == 0)
        def _():
            acc_ref[...] = part

        @pl.when(j > 0)
        def _():
            acc_ref[...] += part

        @pl.when(j == n_f - 1)
        def _():
            y2 = acc_ref[...]
            rstd = _rstd(y2)
            n = y2 * rstd
            rn = n * g_ref[...]
            err = x1_ref[...] + gt_ref[...] * rn - tgt_ref[...]
            loss_ref[...] += 0.5 * jnp.sum(jnp.mean(err * err, axis=-1, keepdims=True), axis=0, keepdims=True)
            dout = err * (1.0 / d)
            dout_ref[...] = dout
            drn = dout * gt_ref[...]
            sums_ref[0:1, :] += jnp.sum(dout * rn, axis=0, keepdims=True)
            sums_ref[1:2, :] += jnp.sum(drn * n, axis=0, keepdims=True)
            dy2_ref[...] = _norm_bwd(drn * g_ref[...], n, rstd).astype(BF16)

    tok = lambda w: pl.BlockSpec((tm, w), lambda i, j: (i, 0))
    tokf = pl.BlockSpec((tm, tf), lambda i, j: (i, j))
    vec = pl.BlockSpec((1, d), lambda i, j: (0, 0))
    once = {"pipeline_mode": pl.Buffered(1)} if n_f == 1 else {}
    return pl.pallas_call(
        body, name="ffn_fwd_loss", grid=(s_len // tm, n_f),
        in_specs=[tok(d), pl.BlockSpec((HALO, d), lambda i, j: (_halo_before(i, tm), 0)), tok(d), tok(d),
                  pl.BlockSpec((tf, d), lambda i, j: (j, 0), **once),
                  pl.BlockSpec((tf, d), lambda i, j: (j + n_f, 0), **once),
                  pl.BlockSpec((tf, d), lambda i, j: (j, 0), **once),
                  pl.BlockSpec((3, tf), lambda i, j: (0, j)), pl.BlockSpec((1, tf), lambda i, j: (0, j)), vec, vec],
        out_specs=[tokf, tokf, tok(d), tok(d), pl.BlockSpec((8, d), lambda i, j: (0, 0)),
                   pl.BlockSpec((8, LANES), lambda i, j: (0, 0))],
        out_shape=[jax.ShapeDtypeStruct((s_len, d_ff), BF16), jax.ShapeDtypeStruct((s_len, d_ff), BF16),
                   jax.ShapeDtypeStruct((s_len, d), BF16), jax.ShapeDtypeStruct((s_len, d), F32),
                   jax.ShapeDtypeStruct((8, d), F32), jax.ShapeDtypeStruct((8, LANES), F32)],
        scratch_shapes=[pltpu.VMEM((tm, d), F32)],
        compiler_params=_params(("arbitrary", "arbitrary")),
    )(h2, h2, x1, target, w_up_t, w_up_t, w_down, conv_w, conv_b, gt_f, g_post_ffn)


def _ffn_bwd_act(dy2, gate, val, w_down, conv_w, conv_b, tm, tf, ck):
    s_len, d = dy2.shape
    d_ff = w_down.shape[0]
    n_t = s_len // tm

    def body(dy_ref, g_ref, gh_ref, v_ref, wd_ref, cw_ref, cb_ref,
             dgc_ref, dval_ref, dwd_ref, dcw_ref, dcb_ref, acc_ref):
        i = pl.program_id(1)
        cw, cb, wd = cw_ref[...], cb_ref[...], wd_ref[...]
        row = lax.broadcasted_iota(jnp.int32, (ck + HALO, tf), 0)
        dwd = taps = bias = None
        for c in range(tm // ck):
            rs = slice(c * ck, (c + 1) * ck)
            halo = gh_ref[...] if c == 0 else g_ref[c * ck - HALO:c * ck, :]
            gate_ext = jnp.concatenate([halo, g_ref[rs, :]], axis=0).astype(F32)
            if c == 0:
                gate_ext = jnp.where((row < HALO) & (i == 0), 0.0, gate_ext)
            act, dact = _gelu_parts(_conv_gate(gate_ext, cw, cb))
            v = v_ref[rs, :].astype(F32)
            dy = dy_ref[rs, :]
            da = _dot(dy, wd, NT)
            dgc = da * v * dact
            dgc_ref[rs, :] = dgc.astype(BF16)
            dval_ref[rs, :] = (da * act).astype(BF16)
            dwd_c = _dot((act * v).astype(BF16), dy, TN)
            taps_c = jnp.concatenate(
                [jnp.sum(dgc * pltpu.roll(gate_ext, 2 - k, 0)[HALO:], axis=0, keepdims=True) if k < 2
                 else jnp.sum(dgc * gate_ext[HALO:], axis=0, keepdims=True) for k in range(3)], axis=0)
            bias_c = jnp.sum(dgc, axis=0, keepdims=True)
            dwd = dwd_c if c == 0 else dwd + dwd_c
            taps = taps_c if c == 0 else taps + taps_c
            bias = bias_c if c == 0 else bias + bias_c

        @pl.when(i == 0)
        def _():
            acc_ref[...] = dwd
            dcw_ref[...] = taps
            dcb_ref[...] = bias

        @pl.when(i > 0)
        def _():
            acc_ref[...] += dwd
            dcw_ref[...] += taps
            dcb_ref[...] += bias

        @pl.when(i == n_t - 1)
        def _():
            dwd_ref[...] = acc_ref[...].astype(BF16)

    tokf = pl.BlockSpec((tm, tf), lambda j, i: (i, j))
    return pl.pallas_call(
        body, name="ffn_bwd_act", grid=(d_ff // tf, n_t),
        in_specs=[pl.BlockSpec((tm, d), lambda j, i: (i, 0)), tokf,
                  pl.BlockSpec((HALO, tf), lambda j, i: (_halo_before(i, tm), j)), tokf,
                  pl.BlockSpec((tf, d), lambda j, i: (j, 0)),
                  pl.BlockSpec((3, tf), lambda j, i: (0, j)), pl.BlockSpec((1, tf), lambda j, i: (0, j))],
        out_specs=[tokf, tokf, pl.BlockSpec((tf, d), lambda j, i: (j, 0)),
                   pl.BlockSpec((3, tf), lambda j, i: (0, j)), pl.BlockSpec((1, tf), lambda j, i: (0, j))],
        out_shape=[jax.ShapeDtypeStruct((s_len, d_ff), BF16), jax.ShapeDtypeStruct((s_len, d_ff), BF16),
                   jax.ShapeDtypeStruct((d_ff, d), BF16), jax.ShapeDtypeStruct((3, d_ff), F32),
                   jax.ShapeDtypeStruct((1, d_ff), F32)],
        scratch_shapes=[pltpu.VMEM((tf, d), F32)],
        compiler_params=_params(("arbitrary", "arbitrary")),
    )(dy2, gate, gate, val, w_down, conv_w, conv_b)


def _ffn_bwd_up(dgc, dval, w_up_t, conv_w, tm):
    s_len, d_ff = dgc.shape
    d = w_up_t.shape[1]
    n_t = s_len // tm

    def body(dg_ref, dgn_ref, dv_ref, cw_ref, w_ref, dup_ref, dh_ref):
        i = pl.program_id(0)
        nxt = dgn_ref[...].astype(F32) * (i < n_t - 1).astype(F32)
        ext = jnp.concatenate([dg_ref[...].astype(F32), nxt], axis=0)
        rows = tm + HALO
        dgate = (ext * cw_ref[2:3, :] + pltpu.roll(ext, rows - 1, 0) * cw_ref[1:2, :]
                 + pltpu.roll(ext, rows - 2, 0) * cw_ref[0:1, :])[:tm]
        dup = jnp.concatenate([dgate.astype(BF16), dv_ref[...]], axis=1)
        dup_ref[...] = dup
        dh_ref[...] = _dot(dup, w_ref[...], NN)

    tokf = pl.BlockSpec((tm, d_ff), lambda i: (i, 0))
    return pl.pallas_call(
        body, name="ffn_bwd_up", grid=(n_t,),
        in_specs=[tokf, pl.BlockSpec((HALO, d_ff), lambda i: (jnp.minimum((i + 1) * (tm // HALO), s_len // HALO - 1), 0)),
                  tokf, pl.BlockSpec((3, d_ff), lambda i: (0, 0)), pl.BlockSpec((2 * d_ff, d), lambda i: (0, 0))],
        out_specs=[pl.BlockSpec((tm, 2 * d_ff), lambda i: (i, 0)), pl.BlockSpec((tm, d), lambda i: (i, 0))],
        out_shape=[jax.ShapeDtypeStruct((s_len, 2 * d_ff), BF16), jax.ShapeDtypeStruct((s_len, d), F32)],
        compiler_params=_params(("arbitrary",)),
    )(dgc, dgc, dval, conv_w, w_up_t)


def _mix_bwd(dh2, dout, x1, y1, cat, attn, w_out_t, sc_f, g_pre_ffn, gt_m, g_post_mix, tm):
    s_len, d = x1.shape
    n_t = s_len // tm

    def body(dh_ref, do_ref, x1_ref, y1_ref, cat_ref, at_ref, wo_ref, sc_ref, g2_ref, gt_ref, g1_ref,
             dx1_ref, dpool_ref, dattn_ref, delta_ref, dwo_ref, sums_ref, acc_ref):
        i = pl.program_id(0)
        dh = dh_ref[...]
        x1 = x1_ref[...]
        r2 = _rstd(x1)
        n2 = x1 * r2
        ng = n2 * g2_ref[...]
        dng = dh * (1.0 + sc_ref[...])
        dx1 = do_ref[...] + _norm_bwd(dng * g2_ref[...], n2, r2)
        dx1_ref[...] = dx1
        y1 = y1_ref[...]
        r1 = _rstd(y1)
        n1 = y1 * r1
        drn = dx1 * gt_ref[...]
        dy1 = _norm_bwd(drn * g1_ref[...], n1, r1).astype(BF16)
        dcat = _dot(dy1, wo_ref[...], NN)
        dpool_ref[...] = dcat[:, 0:256]
        lane = lax.broadcasted_iota(jnp.int32, (tm, LANES), 1)
        first = lane < HEAD_DIM
        for s in range(2):
            da = dcat[:, 256 + s * LANES:256 + (s + 1) * LANES]
            dattn_ref[s] = da
            prod = da * at_ref[:, s * LANES:(s + 1) * LANES]
            tot = jnp.sum(prod, axis=-1, keepdims=True)
            lo = jnp.sum(jnp.where(first, prod, 0.0), axis=-1, keepdims=True)
            delta_ref[s] = jnp.where(first, lo, tot - lo)
        dwo = _dot(dy1, cat_ref[...], TN)
        sums = jnp.concatenate(
            [jnp.sum(dh, axis=0, keepdims=True), jnp.sum(dh * ng, axis=0, keepdims=True),
             jnp.sum(dng * n2, axis=0, keepdims=True), jnp.sum(dx1 * (n1 * g1_ref[...]), axis=0, keepdims=True),
             jnp.sum(drn * n1, axis=0, keepdims=True), jnp.zeros((3, d), F32)], axis=0)

        @pl.when(i == 0)
        def _():
            acc_ref[...] = dwo
            sums_ref[...] = sums

        @pl.when(i > 0)
        def _():
            acc_ref[...] += dwo
            sums_ref[...] += sums

        @pl.when(i == n_t - 1)
        def _():
            dwo_ref[...] = acc_ref[...].astype(BF16)

    tile = lambda w: pl.BlockSpec((tm, w), lambda i: (i, 0))
    slab = pl.BlockSpec((2, tm, LANES), lambda i: (0, i, 0))
    vec = pl.BlockSpec((1, d), lambda i: (0, 0))
    return pl.pallas_call(
        body, name="mix_bwd", grid=(n_t,),
        in_specs=[tile(d), tile(d), tile(d), tile(d), tile(512), tile(256),
                  pl.BlockSpec((d, 512), lambda i: (0, 0)), vec, vec, vec, vec],
        out_specs=[tile(d), tile(256), slab, slab, pl.BlockSpec((d, 512), lambda i: (0, 0)),
                   pl.BlockSpec((8, d), lambda i: (0, 0))],
        out_shape=[jax.ShapeDtypeStruct((s_len, d), F32), jax.ShapeDtypeStruct((s_len, 256), F32),
                   jax.ShapeDtypeStruct((2, s_len, LANES), F32), jax.ShapeDtypeStruct((2, s_len, LANES), F32),
                   jax.ShapeDtypeStruct((d, 512), BF16), jax.ShapeDtypeStruct((8, d), F32)],
        scratch_shapes=[pltpu.VMEM((d, 512), F32)],
        compiler_params=_params(("arbitrary",)),
    )(dh2, dout, x1, y1, cat, attn, w_out_t, sc_f, g_pre_ffn, gt_m, g_post_mix)


def _pool_bwd(dpool, u_pool, w_blk, b_pool, pool_scale, tm):
    s_len = dpool.shape[0]
    n_t = s_len // tm

    def body(dp_ref, dpn_ref, u_ref, uh_ref, wb_ref, bp_ref, ps_ref, du_ref, dwb_ref, sums_ref):
        i = pl.program_id(0)
        u = u_ref[...]
        mixed, _ = _pool_mixed(u, uh_ref[...] * (i > 0).astype(F32), i, tm)
        mixed_b = mixed.astype(BF16)
        y = _dot(mixed_b, wb_ref[...], NN) + bp_ref[...]
        dp = dp_ref[...]
        dy = dp * ps_ref[...]
        dwb = _dot(mixed_b, dy.astype(BF16), TN)
        sums = jnp.concatenate([jnp.sum(dy, axis=0, keepdims=True), jnp.sum(dp * y, axis=0, keepdims=True),
                                jnp.zeros((6, 256), F32)], axis=0)
        dp_ext = jnp.concatenate([dp, dpn_ref[...] * (i < n_t - 1).astype(F32)], axis=0)
        dmix = _dot((dp_ext * ps_ref[...]).astype(BF16), wb_ref[...], NT)
        rows = tm + HALO
        grp = lax.broadcasted_iota(jnp.int32, (rows, 256), 1) // HEAD_DIM
        pick = lambda a, b, c, e: jnp.where(grp == 0, a, jnp.where(grp == 1, b, jnp.where(grp == 2, c, e)))
        pos = (i * tm + lax.broadcasted_iota(jnp.int32, (rows, 256), 0)).astype(F32)
        z = dmix / jnp.minimum(pos + 1.0, pick(*[float(w) for w in POOL_WINDOWS]))
        f2 = z + pltpu.roll(z, rows - 1, 0)
        f4 = f2 + pltpu.roll(f2, rows - 2, 0)
        f8 = f4 + pltpu.roll(f4, rows - 4, 0)
        f16 = f8 + pltpu.roll(f8, rows - 8, 0)
        du_ref[...] = (pick(f2, f4, f8, f16) - dmix)[:tm]

        @pl.when(i == 0)
        def _():
            dwb_ref[...] = dwb
            sums_ref[...] = sums

        @pl.when(i > 0)
        def _():
            dwb_ref[...] += dwb
            sums_ref[...] += sums

    tile = pl.BlockSpec((tm, 256), lambda i: (i, 0))
    const = lambda a: pl.BlockSpec(a.shape, lambda i: (0,) * a.ndim)
    return pl.pallas_call(
        body, name="pool_bwd", grid=(n_t,),
        in_specs=[tile, pl.BlockSpec((HALO, 256), lambda i: (jnp.minimum((i + 1) * (tm // HALO), s_len // HALO - 1), 0)),
                  tile, pl.BlockSpec((HALO, 256), lambda i: (_halo_before(i, tm), 0)),
                  const(w_blk), const(b_pool), const(pool_scale)],
        out_specs=[tile, pl.BlockSpec((256, 256), lambda i: (0, 0)), pl.BlockSpec((8, 256), lambda i: (0, 0))],
        out_shape=[jax.ShapeDtypeStruct((s_len, 256), F32), jax.ShapeDtypeStruct((256, 256), F32),
                   jax.ShapeDtypeStruct((8, 256), F32)],
        compiler_params=_params(("arbitrary",)),
    )(dpool, dpool, u_pool, u_pool, w_blk, b_pool, pool_scale)


def _attn_bwd(qkv, dattn, lse_all, delta, group, dil):
    s_len = qkv.shape[1]
    nb = s_len // (BLOCK * dil)

    def body(q_ref, k_ref, v_ref, do_ref, l_ref, dl_ref, dq_ref, dk_ref, dv_ref):
        lane = lax.broadcasted_iota(jnp.int32, (BLOCK, LANES), 1)
        first = lane < HEAD_DIM

        def block(t, carry):
            dk_part, dv_part = carry
            r, n = t // nb, t % nb
            cur = _block_rows(n, r, dil)
            prev = _block_rows(jnp.maximum(n - 1, 0), r, dil)
            q = q_ref[0, cur, :]
            do = do_ref[0, cur, :]
            lse = l_ref[0, cur, :]
            dlt = dl_ref[0, cur, :]
            kcat = jnp.concatenate([k_ref[0, prev, :], k_ref[0, cur, :]], axis=0).astype(BF16)
            vcat = jnp.concatenate([v_ref[0, prev, :], v_ref[0, cur, :]], axis=0).astype(BF16)
            valid = _band_mask(n)
            stack = lambda a: jnp.concatenate([jnp.where(first, a, 0.0), jnp.where(first, 0.0, a)], axis=0)
            rows2 = lambda a: jnp.concatenate([a[:, 0:1], a[:, HEAD_DIM:HEAD_DIM + 1]], axis=0)
            q2, do2 = stack(q).astype(BF16), stack(do).astype(BF16)
            valid2 = jnp.concatenate([valid, valid], axis=0)
            p = jnp.where(valid2, jnp.exp(_dot(q2, kcat, NT) - rows2(lse)), 0.0)
            ds = (p * (_dot(do2, vcat, NT) - rows2(dlt))).astype(BF16)
            dq2 = _dot(ds, kcat, NN)
            dq_ref[0, cur, :] = jnp.where(first, dq2[:BLOCK], dq2[BLOCK:])
            dkc = _dot(ds, q2, TN)
            dvc = _dot(p.astype(BF16), do2, TN)
            dk_ref[0, prev, :] = dk_part + dkc[:BLOCK]
            dv_ref[0, prev, :] = dv_part + dvc[:BLOCK]
            dk_ref[0, cur, :] = dkc[BLOCK:]
            dv_ref[0, cur, :] = dvc[BLOCK:]
            return dkc[BLOCK:], dvc[BLOCK:]

        def blocks(tt, carry):
            for u in range(ATTN_BWD_UNROLL):
                carry = block(tt * ATTN_BWD_UNROLL + u, carry)
            return carry

        zero = jnp.zeros((BLOCK, LANES), F32)
        lax.fori_loop(0, nb * dil // ATTN_BWD_UNROLL, blocks, (zero, zero))

    def slab(base):
        return pl.BlockSpec((1, s_len, LANES), lambda s: (base + 2 * group + s, 0, 0))

    one = pl.BlockSpec((1, s_len, LANES), lambda s: (s, 0, 0))
    shape = jax.ShapeDtypeStruct((2, s_len, LANES), F32)
    return pl.pallas_call(
        body, name=f"attn_bwd_d{dil}", grid=(2,),
        in_specs=[slab(0), slab(6), slab(12), one, one, one],
        out_specs=[one, one, one], out_shape=[shape, shape, shape],
        compiler_params=_params(("arbitrary",)),
    )(qkv, qkv, qkv, dattn, lse_all, delta)


def _inproj_bwd(du, dqkv, rope, w_in_t, x, dx1, sc_m, g_pre_mix, tm):
    s_len, d = x.shape
    n_proj = w_in_t.shape[0]
    n_t = s_len // tm

    def body(du_ref, *refs):
        dref = refs[:9]
        rope_ref, w_ref, x_ref, dx1_ref, sc_ref, g_ref, dproj_ref, dx_ref, sums_ref = refs[9:]
        i = pl.program_id(0)
        cols = [du_ref[...].astype(BF16)]
        for kind in range(3):
            for grp in range(3):
                for s in range(2):
                    piece = dref[3 * grp + kind][s]
                    if kind < 2:
                        piece = _rope_bwd(piece, rope_ref)
                    if kind == 0:
                        piece = piece * (HEAD_DIM ** -0.5)
                    cols.append(piece.astype(BF16))
        dproj = jnp.concatenate(cols, axis=1)
        dproj_ref[...] = dproj
        dh = _dot(dproj, w_ref[...], NN)
        xv = x_ref[...]
        r = _rstd(xv)
        n = xv * r
        dng = dh * (1.0 + sc_ref[...])
        dx_ref[...] = dx1_ref[...] + _norm_bwd(dng * g_ref[...], n, r)
        sums = jnp.concatenate([jnp.sum(dh, axis=0, keepdims=True), jnp.sum(dh * (n * g_ref[...]), axis=0, keepdims=True),
                                jnp.sum(dng * n, axis=0, keepdims=True), jnp.zeros((5, d), F32)], axis=0)

        @pl.when(i == 0)
        def _():
            sums_ref[...] = sums

        @pl.when(i > 0)
        def _():
            sums_ref[...] += sums

    tile = lambda w: pl.BlockSpec((tm, w), lambda i: (i, 0))
    slab = pl.BlockSpec((2, tm, LANES), lambda i: (0, i, 0))
    vec = pl.BlockSpec((1, d), lambda i: (0, 0))
    return pl.pallas_call(
        body, name="inproj_bwd", grid=(n_t,),
        in_specs=[tile(256)] + [slab] * 9 + [pl.BlockSpec((3, tm, LANES), lambda i: (0, i, 0)),
                                             pl.BlockSpec((n_proj, d), lambda i: (0, 0)), tile(d), tile(d), vec, vec],
        out_specs=[tile(n_proj), tile(d), pl.BlockSpec((8, d), lambda i: (0, 0))],
        out_shape=[jax.ShapeDtypeStruct((s_len, n_proj), BF16), jax.ShapeDtypeStruct((s_len, d), F32),
                   jax.ShapeDtypeStruct((8, d), F32)],
        compiler_params=_params(("arbitrary",)),
    )(du, *dqkv, rope, w_in_t, x, dx1, sc_m, g_pre_mix)


def _wgrad(a, b, name, tk, tmm):
    s_len, m = a.shape
    n = b.shape[1]
    n_k = s_len // tk

    def body(a_ref, b_ref, o_ref, acc_ref):
        k = pl.program_id(1)
        part = _dot(a_ref[...], b_ref[...], TN)

        @pl.when(k == 0)
        def _():
            acc_ref[...] = part

        @pl.when(k > 0)
        def _():
            acc_ref[...] += part

        @pl.when(k == n_k - 1)
        def _():
            o_ref[...] = acc_ref[...].astype(BF16)

    return pl.pallas_call(
        body, name=name, grid=(m // tmm, n_k),
        in_specs=[pl.BlockSpec((tk, tmm), lambda j, k: (k, j)), pl.BlockSpec((tk, n), lambda j, k: (k, 0))],
        out_specs=pl.BlockSpec((tmm, n), lambda j, k: (j, 0)),
        out_shape=jax.ShapeDtypeStruct((m, n), BF16),
        scratch_shapes=[pltpu.VMEM((tmm, n), F32)],
        compiler_params=_params(("arbitrary", "arbitrary")),
    )(a, b)


def _place():
    return lax.axis_index("x"), lax.axis_index("y"), lax.axis_index("c")


def _peer(k):
    x, y, c = _place()
    bx, by, bc = (k >> 2) & 1, (k >> 1) & 1, k & 1
    return (x ^ bx if bx else x, y ^ by if by else y, c ^ bc if bc else c)


def _index(pos):
    return 4 * pos[0] + 2 * pos[1] + pos[2]


def _ada_exchange(c_rows, w_ada, b_ada_cols, taps):
    d = c_rows.shape[1]
    ncol = w_ada.shape[1]

    def body(c_ref, w_ref, b_ref, t_ref, call_ref, mod_ref, tall_ref, stage_ref, send_sems, recv_sems):
        me = _index(_place())
        call_ref[me] = c_ref[...]
        tall_ref[me] = t_ref[...]

        def gather(k):
            return pltpu.make_async_remote_copy(
                src_ref=c_ref, dst_ref=call_ref.at[me], send_sem=send_sems.at[0, k - 1], recv_sem=recv_sems.at[0, k - 1],
                device_id=_peer(k), device_id_type=MESH)

        def gather_taps(k):
            return pltpu.make_async_remote_copy(
                src_ref=t_ref, dst_ref=tall_ref.at[me], send_sem=send_sems.at[2, k - 1], recv_sem=recv_sems.at[2, k - 1],
                device_id=_peer(k), device_id_type=MESH)

        for k in range(1, N_DEV):
            gather(k).start()
        for k in range(1, N_DEV):
            gather_taps(k).start()
        for k in range(1, N_DEV):
            gather(k).wait_recv()
        cv = jnp.concatenate([call_ref[b, 0:1, :] for b in range(N_DEV)], axis=0)
        act = cv * jax.nn.sigmoid(cv)
        mod = lax.dot_general(act, w_ref[...], NN, preferred_element_type=F32,
                              precision=lax.Precision.HIGHEST) + b_ref[...]
        for b in range(N_DEV):
            stage_ref[b] = jnp.broadcast_to(mod[b:b + 1, :], (8, ncol))
        mod_ref[me] = stage_ref[me]

        def scatter(k):
            return pltpu.make_async_remote_copy(
                src_ref=stage_ref.at[_index(_peer(k))], dst_ref=mod_ref.at[me],
                send_sem=send_sems.at[1, k - 1], recv_sem=recv_sems.at[1, k - 1],
                device_id=_peer(k), device_id_type=MESH)

        for k in range(1, N_DEV):
            scatter(k).start()
        for k in range(1, N_DEV):
            scatter(k).wait_recv()
        for k in range(1, N_DEV):
            gather_taps(k).wait_recv()
        for k in range(1, N_DEV):
            gather(k).wait_send()
            scatter(k).wait_send()
            gather_taps(k).wait_send()

    vmem = pl.BlockSpec(memory_space=pltpu.VMEM)
    return pl.pallas_call(
        body, name="ada_exchange",
        in_specs=[vmem] * 4, out_specs=[vmem] * 3,
        out_shape=[jax.ShapeDtypeStruct((N_DEV, 8, d), F32), jax.ShapeDtypeStruct((N_DEV, 8, ncol), F32),
                   jax.ShapeDtypeStruct((N_DEV,) + taps.shape, F32)],
        scratch_shapes=[pltpu.VMEM((N_DEV, 8, ncol), F32), pltpu.SemaphoreType.DMA((3, N_DEV - 1)),
                        pltpu.SemaphoreType.DMA((3, N_DEV - 1))],
        compiler_params=_params(),
    )(c_rows, w_ada, b_ada_cols, taps)


def _gather_weights(shards):
    n_w = len(shards)

    def body(*refs):
        srcs, outs = refs[:n_w], refs[n_w:2 * n_w]
        send_sems, recv_sems, local_sems = refs[2 * n_w:]
        x, y, c = _place()
        me, sibling = (x, y, c), (x, y, 1 - c)
        chips = [(1 - x, y), (x, 1 - y), (1 - x, 1 - y)]

        def rows(w, pos):
            r = shards[w].shape[0]
            return outs[w].at[pl.ds(pl.multiple_of(_index(pos) * r, 16), r), :]

        def copy(k, w, block, to, own=False):
            return pltpu.make_async_remote_copy(
                src_ref=srcs[w] if own else rows(w, block), dst_ref=rows(w, block),
                send_sem=send_sems.at[k, w], recv_sem=recv_sems.at[k, w], device_id=to, device_id_type=MESH)

        mine = [pltpu.make_async_copy(srcs[w], rows(w, me), local_sems.at[w]) for w in range(n_w)]
        for cp in mine:
            cp.start()
        first = [copy(0, w, me, sibling, own=True) for w in range(n_w)]
        first += [copy(1 + j, w, me, (*chip, c), own=True) for j, chip in enumerate(chips) for w in range(n_w)]
        for cp in first:
            cp.start()
        passed = []
        for j, chip in enumerate(chips):
            for w in range(n_w):
                copy(1 + j, w, (*chip, c), me).wait_recv()
                fwd = copy(4 + j, w, (*chip, c), sibling)
                fwd.start()
                passed.append(fwd)
        for w in range(n_w):
            copy(0, w, sibling, me).wait_recv()
        for j, chip in enumerate(chips):
            for w in range(n_w):
                copy(4 + j, w, (*chip, 1 - c), me).wait_recv()
        for cp in first + passed:
            cp.wait_send()
        for cp in mine:
            cp.wait()

    hbm = pl.BlockSpec(memory_space=pltpu.HBM)
    return pl.pallas_call(
        body, name="gather_weights",
        in_specs=[hbm] * n_w, out_specs=[hbm] * n_w,
        out_shape=[jax.ShapeDtypeStruct((N_DEV * s.shape[0], s.shape[1]), s.dtype) for s in shards],
        scratch_shapes=[pltpu.SemaphoreType.DMA((N_DEV - 1, n_w)), pltpu.SemaphoreType.DMA((N_DEV - 1, n_w)),
                        pltpu.SemaphoreType.DMA((n_w,))],
        compiler_params=_params(),
    )(*shards)


def _scatter_grads(grads):
    n_w = len(grads)

    def body(*refs):
        srcs, outs = refs[:n_w], refs[n_w:2 * n_w]
        send_sems, recv_sems, local_sems = refs[2 * n_w:]
        me = _index(_place())

        def slab(w, dev):
            r = grads[w].shape[0] // N_DEV
            return srcs[w].at[pl.ds(pl.multiple_of(dev * r, 16), r), :]

        def copy(k, w):
            return pltpu.make_async_remote_copy(
                src_ref=slab(w, _index(_peer(k))), dst_ref=outs[w].at[me],
                send_sem=send_sems.at[k - 1, w], recv_sem=recv_sems.at[k - 1, w],
                device_id=_peer(k), device_id_type=MESH)

        mine = [pltpu.make_async_copy(slab(w, me), outs[w].at[me], local_sems.at[w]) for w in range(n_w)]
        for cp in mine:
            cp.start()
        sends = [copy(k, w) for k in range(1, N_DEV) for w in range(n_w)]
        for cp in sends:
            cp.start()
        for cp in sends:
            cp.wait_recv()
        for cp in sends:
            cp.wait_send()
        for cp in mine:
            cp.wait()

    hbm = pl.BlockSpec(memory_space=pltpu.HBM)
    return pl.pallas_call(
        body, name="scatter_grads",
        in_specs=[hbm] * n_w, out_specs=[hbm] * n_w,
        out_shape=[jax.ShapeDtypeStruct((N_DEV, g.shape[0] // N_DEV, g.shape[1]), g.dtype) for g in grads],
        scratch_shapes=[pltpu.SemaphoreType.DMA((N_DEV - 1, n_w)), pltpu.SemaphoreType.DMA((N_DEV - 1, n_w)),
                        pltpu.SemaphoreType.DMA((n_w,))],
        compiler_params=_params(),
    )(*grads)


def _peer_copies(mode, srcs, lands, send_sems, recv_sems):
    me = _index(_place())
    copies = []
    for k in range(1, N_DEV):
        peer = _peer(k)
        for w, (src, land) in enumerate(zip(srcs, lands)):
            if mode == "gather":
                r = src.shape[0]
                dst = land.at[pl.ds(pl.multiple_of(me * r, 16), r), :]
            else:
                r = src.shape[0] // N_DEV
                src = src.at[pl.ds(pl.multiple_of(_index(peer) * r, 16), r), :]
                dst = land.at[me]
            copies.append(pltpu.make_async_remote_copy(
                src_ref=src, dst_ref=dst, send_sem=send_sems.at[(k - 1) * len(srcs) + w],
                recv_sem=recv_sems.at[(k - 1) * len(srcs) + w],
                device_id=peer, device_id_type=MESH))
    return copies


def _exchange_start(mode, srcs, lands, name):
    n = len(srcs)

    def body(*refs):
        for cp in _peer_copies(mode, refs[:n], refs[n:2 * n], refs[2 * n], refs[2 * n + 1]):
            cp.start()
        refs[-1][...] = jnp.zeros_like(refs[-1])

    hbm, sem = pl.BlockSpec(memory_space=pltpu.HBM), pl.BlockSpec(memory_space=pltpu.SEMAPHORE)
    arrays = list(srcs) + list(lands)
    out = pl.pallas_call(
        body, name=name,
        out_shape=(pltpu.SemaphoreType.DMA(((N_DEV - 1) * n,)), pltpu.SemaphoreType.DMA(((N_DEV - 1) * n,)),
                   *[pltpu.HBM(a.shape, a.dtype) for a in arrays], jax.ShapeDtypeStruct((8, LANES), F32)),
        in_specs=[hbm] * (2 * n), out_specs=(sem, sem, *[hbm] * (2 * n), pl.BlockSpec(memory_space=pltpu.VMEM)),
        input_output_aliases={i: 2 + i for i in range(2 * n)},
        compiler_params=pltpu.CompilerParams(has_side_effects=pltpu.SideEffectType.DATAFLOW_SIDE_EFFECTING),
    )(*[pltpu.with_memory_space_constraint(a, pltpu.HBM) for a in arrays])
    return out[0], out[1], out[2:2 + n], out[2 + n:2 + 2 * n], out[-1]


def _exchange_wait(mode, send_sems, recv_sems, srcs, lands, after, name):
    n = len(srcs)

    def body(*refs):
        copies = _peer_copies(mode, refs[:n], refs[n:2 * n], refs[2 * n], refs[2 * n + 1])
        for cp in copies:
            cp.wait_send()
        for cp in copies:
            cp.wait_recv()

    hbm, sem = pl.BlockSpec(memory_space=pltpu.HBM), pl.BlockSpec(memory_space=pltpu.SEMAPHORE)
    arrays = list(srcs) + list(lands)
    out = pl.pallas_call(
        body, name=name, out_shape=tuple(pltpu.HBM(a.shape, a.dtype) for a in arrays),
        in_specs=[hbm] * (2 * n) + [sem, sem, pl.BlockSpec(memory_space=pl.ANY)], out_specs=tuple([hbm] * (2 * n)),
        input_output_aliases={i: i for i in range(2 * n)},
        compiler_params=pltpu.CompilerParams(has_side_effects=pltpu.SideEffectType.DATAFLOW_SIDE_EFFECTING),
    )(*arrays, send_sems, recv_sems, after)
    return out[n:]


def _allreduce_small(packed, name):
    rows = packed.shape[0]

    def body(p_ref, all_ref, tot_ref, send_sems, recv_sems):
        me = _index(_place())
        all_ref[me] = p_ref[...]

        def copy(k):
            return pltpu.make_async_remote_copy(
                src_ref=p_ref, dst_ref=all_ref.at[me], send_sem=send_sems.at[k - 1], recv_sem=recv_sems.at[k - 1],
                device_id=_peer(k), device_id_type=MESH)

        for k in range(1, N_DEV):
            copy(k).start()
        for k in range(1, N_DEV):
            copy(k).wait_recv()
        tot = all_ref[0]
        for dev in range(1, N_DEV):
            tot = tot + all_ref[dev]
        tot_ref[...] = tot
        for k in range(1, N_DEV):
            copy(k).wait_send()

    vmem = pl.BlockSpec(memory_space=pltpu.VMEM)
    return pl.pallas_call(
        body, name=name, in_specs=[vmem], out_specs=[vmem, vmem],
        out_shape=[jax.ShapeDtypeStruct((N_DEV, rows, LANES), F32), jax.ShapeDtypeStruct((rows, LANES), F32)],
        scratch_shapes=[pltpu.SemaphoreType.DMA((N_DEV - 1,)), pltpu.SemaphoreType.DMA((N_DEV - 1,))],
        compiler_params=_params(),
    )(packed)


def _adam_math(w, g, m, v):
    m = ADAM_B1 * m + (1.0 - ADAM_B1) * g
    v = ADAM_B2 * v + (1.0 - ADAM_B2) * (g * g)
    m_hat = m / (1.0 - ADAM_B1 ** ADAM_STEP)
    v_hat = v / (1.0 - ADAM_B2 ** ADAM_STEP)
    delta = -ADAM_LR * (m_hat / (jnp.sqrt(v_hat) + ADAM_EPS) + ADAM_WD * w)
    return delta, m, v


def _adam(w, g, m, v, name, tr):
    rows, cols = w.shape

    def body(w_ref, g_ref, m_ref, v_ref, d_ref, nm_ref, nv_ref):
        d_ref[...], nm_ref[...], nv_ref[...] = _adam_math(w_ref[...], g_ref[...], m_ref[...], v_ref[...])

    spec = pl.BlockSpec((tr, cols), lambda i: (i, 0))
    shape = jax.ShapeDtypeStruct((rows, cols), F32)
    return pl.pallas_call(
        body, name=name, grid=(rows // tr,), in_specs=[spec] * 4, out_specs=[spec] * 3,
        out_shape=[shape] * 3, compiler_params=_params(("arbitrary",)),
    )(w, g, m, v)


def _sum_adam(parts, w, m, v, name, tr):
    _, rows, cols = parts.shape

    def body(p_ref, w_ref, m_ref, v_ref, g_ref, d_ref, nm_ref, nv_ref):
        g = p_ref[0].astype(F32)
        for dev in range(1, N_DEV):
            g = g + p_ref[dev].astype(F32)
        g_ref[...] = g
        d_ref[...], nm_ref[...], nv_ref[...] = _adam_math(w_ref[...], g, m_ref[...], v_ref[...])

    spec = pl.BlockSpec((tr, cols), lambda i: (i, 0))
    shape = jax.ShapeDtypeStruct((rows, cols), F32)
    return pl.pallas_call(
        body, name=name, grid=(rows // tr,),
        in_specs=[pl.BlockSpec((N_DEV, tr, cols), lambda i: (0, i, 0)), spec, spec, spec],
        out_specs=[spec] * 4, out_shape=[shape] * 4, compiler_params=_params(("arbitrary",)),
    )(parts, w, m, v)


def _ada_grad_adam(c_all, dmod_cols, w, m, v, tr):
    rows, cols = w.shape

    def body(c_ref, dm_ref, w_ref, m_ref, v_ref, g_ref, d_ref, nm_ref, nv_ref):
        cv = c_ref[...]
        act = cv * jax.nn.sigmoid(cv)
        g = lax.dot_general(act, dm_ref[...], TN, preferred_element_type=F32, precision=lax.Precision.HIGHEST)
        g_ref[...] = g
        d_ref[...], nm_ref[...], nv_ref[...] = _adam_math(w_ref[...], g, m_ref[...], v_ref[...])

    spec = pl.BlockSpec((tr, cols), lambda i: (i, 0))
    shape = jax.ShapeDtypeStruct((rows, cols), F32)
    return pl.pallas_call(
        body, name="ada_grad_adam", grid=(rows // tr,),
        in_specs=[pl.BlockSpec((N_DEV, tr), lambda i: (0, i)), pl.BlockSpec((N_DEV, cols), lambda i: (0, 0)), spec, spec, spec],
        out_specs=[spec] * 4, out_shape=[shape] * 4, compiler_params=_params(("arbitrary",)),
    )(c_all, dmod_cols, w, m, v)


def _rope_tables(positions):
    s_len = positions.shape[0]
    inv_freq = ROPE_THETA ** (-jnp.arange(0, 2 * ROT_HALF, 2, dtype=F32) / (2 * ROT_HALF))
    ang = positions.astype(F32)[:, None] * inv_freq
    cos, sin = jnp.cos(ang), jnp.sin(ang)
    rest = HEAD_DIM - 2 * ROT_HALF
    zero = lambda n: jnp.zeros((s_len, n), F32)
    head = jnp.stack([jnp.concatenate([cos, cos, jnp.ones((s_len, rest), F32)], axis=1),
                      jnp.concatenate([-sin, zero(HEAD_DIM - ROT_HALF)], axis=1),
                      jnp.concatenate([zero(ROT_HALF), sin, zero(rest)], axis=1)])
    return jnp.tile(head, (1, 1, LANES // HEAD_DIM))


def _pad_rows(a, rows):
    return jnp.pad(a, ((0, rows - a.shape[0]), (0, 0)))


def _as_rows(a, rows):
    flat = a.reshape(-1)
    return jnp.pad(flat, (0, rows * LANES - flat.shape[0])).reshape(rows, LANES)


def _sequence_step(xs, target, rope, mods, gains, w_in_t, w_out_t, fetch_ffn, send_ffn_grads, w_blk_b, b_pool_r,
                   pool_scale_r, conv_w_all, conv_b):
    sh_m, sc_m, gt_m, sh_f, sc_f, gt_f = mods
    g_pre_mix, g_post_mix, g_pre_ffn, g_post_ffn = gains
    h1, u_pool, qkv = _premix_inproj(xs, sh_m, sc_m, g_pre_mix, w_in_t, rope, tm=512)
    o_g, lse_g = [], []
    for gi, dil in enumerate(DILATIONS):
        o, lse = _attn_fwd(qkv, gi, dil)
        o_g.append(o)
        lse_g.append(lse)
    x1, y1, h2, cat, attn, lse_all = _mix_out(xs, u_pool, o_g, lse_g, w_blk_b, b_pool_r, pool_scale_r, w_out_t,
                                              gt_m, g_post_mix, g_pre_ffn, sc_f, sh_f, tm=256)
    w_up_t, w_down_f = fetch_ffn(x1)
    gate, val, dy2, dout, sums_ffn, loss_loc = _ffn_fwd_loss(h2, x1, target, w_up_t, w_down_f, conv_w_all, conv_b,
                                                              gt_f, g_post_ffn, tm=256, tf=2816, ck=256)

    dgc, dval, dw_down, dconv_w, dconv_b = _ffn_bwd_act(dy2, gate, val, w_down_f, conv_w_all, conv_b, tm=1024, tf=256, ck=1024)
    dup, dh2 = _ffn_bwd_up(dgc, dval, w_up_t, conv_w_all, tm=256)
    dw_up_t = _wgrad(dup, h2, "wgrad_up", tk=1024, tmm=1408)
    token = send_ffn_grads(dw_up_t, dw_down)
    if token is not None:
        sc_f = sc_f + token[0:1, 0:1]
    dx1, dpool, dattn, delta, dw_out_t, sums_mix = _mix_bwd(dh2, dout, x1, y1, cat, attn, w_out_t, sc_f, g_pre_ffn,
                                                           gt_m, g_post_mix, tm=256)
    du, dw_blk, sums_pool = _pool_bwd(dpool, u_pool, w_blk_b, b_pool_r, pool_scale_r, tm=512)
    dqkv = []
    for gi, dil in enumerate(DILATIONS):
        dqkv += list(_attn_bwd(qkv, dattn, lse_all, delta, gi, dil))
    dproj, grad_x, sums_in = _inproj_bwd(du, dqkv, rope, w_in_t, xs, dx1, sc_m, g_pre_mix, tm=256)
    dw_in_t = _wgrad(dproj, h1, "wgrad_in", tk=1024, tmm=1280)
    return (loss_loc, grad_x, dw_in_t, dw_out_t, dw_up_t, dw_down, dw_blk, dconv_w, dconv_b,
            sums_in, sums_mix, sums_ffn, sums_pool)


def kernel(x, c, positions, w_ada, b_ada, g_pre_mix, g_post_mix, g_pre_ffn, g_post_ffn, w_in, w_pool, b_pool, pool_scale, w_out, w_up, conv_w, conv_b, w_down, loss_target, m_w_ada, m_b_ada, m_g_pre_mix, m_g_post_mix, m_g_pre_ffn, m_g_post_ffn, m_w_in, m_w_pool, m_b_pool, m_pool_scale, m_w_out, m_w_up, m_conv_w, m_conv_b, m_w_down, v_w_ada, v_b_ada, v_g_pre_mix, v_g_post_mix, v_g_pre_ffn, v_g_post_ffn, v_w_in, v_w_pool, v_b_pool, v_pool_scale, v_w_out, v_w_up, v_conv_w, v_conv_b, v_w_down):
    s_len, d = x.shape[1], x.shape[2]
    d_ff = w_down.shape[1] * N_DEV
    me = _index(_place())
    xs, target = x[0], loss_target[0]

    ncol = w_ada.shape[2]
    b_cols = lax.dynamic_slice(b_ada, (0, me * ncol), (1, ncol))
    c_all, mod, taps_all = _ada_exchange(jnp.broadcast_to(c, (8, d)), w_ada[0], b_cols, _pad_rows(conv_w[0], 8))
    c_all = c_all[:, 0, :]
    conv_w_all = jnp.transpose(taps_all[:, :3, :], (1, 0, 2)).reshape(3, d_ff)
    sh_m, sc_m, gt_m, sh_f, sc_f, gt_f = [mod[:, 0, :].reshape(1, -1)[:, k * d:(k + 1) * d] for k in range(6)]

    w_in_t, w_out_t = _gather_weights([w_in[0].T.astype(BF16), w_out[0].T.astype(BF16)])

    rope = _rope_tables(positions[0])
    w_blk = jnp.zeros((256, 256), F32)
    for gi in range(4):
        w_blk = lax.dynamic_update_slice(w_blk, w_pool[0, gi], (gi * HEAD_DIM, gi * HEAD_DIM))
    w_blk_b = w_blk.astype(BF16)
    b_pool_r, pool_scale_r = b_pool.reshape(1, 256), pool_scale.reshape(1, 256)

    cw_rows = d_ff // LANES

    up_sh, down_sh = w_up[0].T.astype(BF16), w_down[0].astype(BF16)
    w_in_t, conv_w_all, up_sh, down_sh = lax.optimization_barrier((w_in_t, conv_w_all, up_sh, down_sh))
    lands = [lax.dynamic_update_slice(lax.empty((N_DEV * s.shape[0], s.shape[1]), BF16), s, (me * s.shape[0], 0))
             for s in (up_sh, down_sh)]
    w_send, w_recv, w_src, w_land, w_token = _exchange_start("gather", [up_sh, down_sh], lands, "ffn_weights_start")

    def fetch_ffn(after):
        return _exchange_wait("gather", w_send, w_recv, w_src, w_land, after, "ffn_weights_wait")

    flight = []

    def send_ffn_grads(dw_up_t, dw_down):
        lands = []
        for g in (dw_up_t, dw_down):
            r = g.shape[0] // N_DEV
            own = lax.dynamic_slice(g, (me * r, 0), (r, g.shape[1]))
            lands.append(lax.dynamic_update_slice(lax.empty((N_DEV, r, g.shape[1]), BF16), own[None], (me, 0, 0)))
        flight.extend(_exchange_start("scatter", [dw_up_t, dw_down], lands, "ffn_grads_start"))
        return flight[4]

    (loss_loc, grad_x, dw_in_t, dw_out_t, _, _, dw_blk, dconv_w, dconv_b,
     sums_in, sums_mix, sums_ffn, sums_pool) = _sequence_step(
        xs, target, rope, (sh_m + w_token[0:1, 0:1], sc_m, gt_m, sh_f, sc_f, gt_f),
        (g_pre_mix, g_post_mix, g_pre_ffn, g_post_ffn),
        w_in_t, w_out_t, fetch_ffn, send_ffn_grads, w_blk_b, b_pool_r, pool_scale_r, conv_w_all, conv_b)

    parts_ffn = _exchange_wait("scatter", *flight[:4], dw_in_t, "ffn_grads_wait")
    dw_in_t, dw_out_t, *parts_ffn = lax.optimization_barrier((dw_in_t, dw_out_t, *parts_ffn))
    parts_mix = _scatter_grads([dw_in_t, dw_out_t])
    big = {
        "w_in": [a.T for a in _sum_adam(parts_mix[0], w_in[0].T, m_w_in[0].T, v_w_in[0].T, "adam_w_in", 64)],
        "w_out": [a.T for a in _sum_adam(parts_mix[1], w_out[0].T, m_w_out[0].T, v_w_out[0].T, "adam_w_out", 128)],
        "w_up": [a.T for a in _sum_adam(parts_ffn[0], w_up[0].T, m_w_up[0].T, v_w_up[0].T, "adam_w_up", 64)],
        "w_down": _sum_adam(parts_ffn[1], w_down[0], m_w_down[0], v_w_down[0], "adam_w_down", 32),
    }

    dmod = jnp.concatenate([sums_in[0:1], sums_in[1:2], sums_mix[3:4], sums_mix[0:1], sums_mix[1:2], sums_ffn[0:1]], axis=1)
    dw_pool = jnp.stack([dw_blk[gi * HEAD_DIM:(gi + 1) * HEAD_DIM, gi * HEAD_DIM:(gi + 1) * HEAD_DIM] for gi in range(4)])
    pieces = [(dmod, 48), (sums_in[2:3], 8), (sums_mix[4:5], 8), (sums_mix[2:3], 8), (sums_ffn[1:2], 8),
              (dw_pool, 128), (sums_pool[0:1], 8), (sums_pool[1:2], 8), (dconv_b, 24),
              (dconv_w[0:1], 24), (dconv_w[1:2], 24), (dconv_w[2:3], 24), (loss_loc[0:1], 8)]
    packed = jnp.concatenate([_as_rows(a, r) for a, r in pieces], axis=0)
    gathered, total = _allreduce_small(packed, "allreduce_small")
    n_rep = 248
    rep_w = [b_ada, g_pre_mix, g_post_mix, g_pre_ffn, g_post_ffn, w_pool, b_pool, pool_scale, conv_b]
    rep_m = [m_b_ada, m_g_pre_mix, m_g_post_mix, m_g_pre_ffn, m_g_post_ffn, m_w_pool, m_b_pool, m_pool_scale, m_conv_b]
    rep_v = [v_b_ada, v_g_pre_mix, v_g_post_mix, v_g_pre_ffn, v_g_post_ffn, v_w_pool, v_b_pool, v_pool_scale, v_conv_b]
    rep_rows = [r for _, r in pieces[:9]]
    pack_rep = lambda arrs: jnp.concatenate([_as_rows(a, r) for a, r in zip(arrs, rep_rows)], axis=0)
    rep_g = total[:n_rep]
    rep_d, rep_nm, rep_nv = _adam(pack_rep(rep_w), rep_g, pack_rep(rep_m), pack_rep(rep_v), "adam_small", n_rep)

    def unpack(p):
        out, row = [], 0
        for a, r in zip(rep_w, rep_rows):
            out.append(p[row:row + r].reshape(-1)[:a.size].reshape(a.shape))
            row += r
        return out

    g_rep, d_rep, nm_rep, nv_rep = unpack(rep_g), unpack(rep_d), unpack(rep_nm), unpack(rep_nv)

    fcol = d_ff // N_DEV
    g_cw_full = jnp.concatenate([total[n_rep + 24 * k:n_rep + 24 * k + cw_rows].reshape(1, d_ff) for k in range(3)], axis=0)
    g_cw = lax.dynamic_slice(g_cw_full, (0, me * fcol), (3, fcol))
    d_cw, nm_cw, nv_cw = _adam(conv_w[0], g_cw, m_conv_w[0], v_conv_w[0], "adam_conv_w", 3)

    dmod_all = gathered[:, :48].reshape(N_DEV, 6 * d)
    dmod_cols = lax.dynamic_slice(dmod_all, (0, me * ncol), (N_DEV, ncol))
    g_ada, d_ada, nm_ada, nv_ada = _ada_grad_adam(c_all, dmod_cols, w_ada[0], m_w_ada[0], v_w_ada[0], 256)

    loss = total[n_rep + 72, 0]

    def group(k):
        rep = (g_rep, d_rep, nm_rep, nv_rep)[k]
        ada = (g_ada, d_ada, nm_ada, nv_ada)[k][None]
        cw = (g_cw, d_cw, nm_cw, nv_cw)[k][None]
        return [ada, rep[0], rep[1], rep[2], rep[3], rep[4], big["w_in"][k][None], rep[5], rep[6], rep[7],
                big["w_out"][k][None], big["w_up"][k][None], cw, rep[8], big["w_down"][k][None]]

    return (loss, grad_x[None], *group(0), *group(1), *group(2), *group(3))
```

```python
import functools
import math

import jax
import jax.numpy as jnp
from jax import lax
from jax.experimental import pallas as pl
from jax.experimental.pallas import tpu as pltpu

F32 = jnp.float32
BF16 = jnp.bfloat16
MESH = pl.DeviceIdType.MESH

N_DEV = 8
HEAD_DIM = 64
ROT_HALF = 8
ROPE_THETA = 500000.0
POOL_WINDOWS = (2, 4, 8, 16)
DILATIONS = (1, 4, 16)
BLOCK = 128
NORM_EPS = 1e-6
HALO = 16
MASKED = -1e30
ATTN_FWD_UNROLL = 4
ATTN_BWD_UNROLL = 2

ADAM_LR = 0.001
ADAM_B1 = 0.9
ADAM_B2 = 0.999
ADAM_EPS = 1e-08
ADAM_WD = 0.01
ADAM_STEP = 10

V7X_VMEM_LIMIT = 56 * 1024 * 1024
LANES = 128

NT = (((1,), (1,)), ((), ()))
NN = (((1,), (0,)), ((), ()))
TN = (((0,), (0,)), ((), ()))


def _dot(a, b, dims):
    return lax.dot_general(a, b, dims, preferred_element_type=F32)


def _params(sem=None, vmem=V7X_VMEM_LIMIT):
    if sem is None:
        return pltpu.CompilerParams(vmem_limit_bytes=vmem)
    return pltpu.CompilerParams(dimension_semantics=sem, vmem_limit_bytes=vmem)


def _rstd(v):
    return lax.rsqrt(jnp.mean(v * v, axis=-1, keepdims=True) + NORM_EPS)


def _norm_bwd(dn, n, rstd):
    return rstd * (dn - n * jnp.mean(dn * n, axis=-1, keepdims=True))


def _rope_fwd(p, rope_ref):
    return p * rope_ref[0] + pltpu.roll(p, LANES - ROT_HALF, 1) * rope_ref[1] + pltpu.roll(p, ROT_HALF, 1) * rope_ref[2]


def _rope_bwd(dp, rope_ref):
    return dp * rope_ref[0] + pltpu.roll(dp * rope_ref[1], ROT_HALF, 1) + pltpu.roll(dp * rope_ref[2], LANES - ROT_HALF, 1)


def _gelu_parts(v):
    k = math.sqrt(2.0 / math.pi)
    t = jnp.tanh(k * (v + 0.044715 * v * v * v))
    g = 0.5 * v * (1.0 + t)
    dg = 0.5 * (1.0 + t) + 0.5 * v * (1.0 - t * t) * k * (1.0 + 3.0 * 0.044715 * v * v)
    return g, dg


def _halo_before(i, tile):
    return jnp.maximum(i * (tile // HALO) - 1, 0)


def _premix_inproj(x, sh, sc, g, w_in_t, rope, tm):
    s_len, d = x.shape
    n_proj = w_in_t.shape[0]
    n_slab = (n_proj - 256) // LANES

    def body(x_ref, sh_ref, sc_ref, g_ref, w_ref, rope_ref, h_ref, up_ref, qkv_ref):
        xv = x_ref[...]
        h = (xv * _rstd(xv) * g_ref[...]) * (1.0 + sc_ref[...]) + sh_ref[...]
        hb = h.astype(BF16)
        h_ref[...] = hb
        up_ref[...] = _dot(hb, w_ref[0:256, :], NT)
        for pair in range(n_slab // 2):
            p = _dot(hb, w_ref[256 + 256 * pair:512 + 256 * pair, :], NT)
            for half in range(2):
                ph = p[:, half * LANES:(half + 1) * LANES]
                if pair < 6:
                    ph = _rope_fwd(ph, rope_ref)
                if pair < 3:
                    ph = ph * (HEAD_DIM ** -0.5)
                qkv_ref[2 * pair + half] = ph

    vec = pl.BlockSpec((1, d), lambda i: (0, 0))
    return pl.pallas_call(
        body, name="premix_inproj", grid=(s_len // tm,),
        in_specs=[pl.BlockSpec((tm, d), lambda i: (i, 0)), vec, vec, vec,
                  pl.BlockSpec((n_proj, d), lambda i: (0, 0)),
                  pl.BlockSpec((3, tm, LANES), lambda i: (0, i, 0))],
        out_specs=[pl.BlockSpec((tm, d), lambda i: (i, 0)),
                   pl.BlockSpec((tm, 256), lambda i: (i, 0)),
                   pl.BlockSpec((n_slab, tm, LANES), lambda i: (0, i, 0))],
        out_shape=[jax.ShapeDtypeStruct((s_len, d), BF16),
                   jax.ShapeDtypeStruct((s_len, 256), F32),
                   jax.ShapeDtypeStruct((n_slab, s_len, LANES), F32)],
        compiler_params=_params(("arbitrary",)),
    )(x, sh, sc, g, w_in_t, rope)


def _block_rows(n, r, dil):
    start = n * (BLOCK * dil) + r
    if dil == 1:
        return pl.ds(pl.multiple_of(start, BLOCK), BLOCK)
    return pl.ds(start, BLOCK, stride=dil)


def _band_mask(n):
    ri = lax.broadcasted_iota(jnp.int32, (BLOCK, 2 * BLOCK), 0)
    cj = lax.broadcasted_iota(jnp.int32, (BLOCK, 2 * BLOCK), 1)
    cur = (cj >= BLOCK) & (cj - BLOCK <= ri)
    prev = (cj < BLOCK) & (cj >= ri) & (n > 0)
    return cur | prev


def _attn_fwd(qkv, group, dil):
    s_len = qkv.shape[1]
    nb = s_len // (BLOCK * dil)

    def body(q_ref, k_ref, v_ref, o_ref, lse_ref):
        lane = lax.broadcasted_iota(jnp.int32, (BLOCK, LANES), 1)
        first = lane < HEAD_DIM

        def block(t, carry):
            r, n = t // nb, t % nb
            cur = _block_rows(n, r, dil)
            prev = _block_rows(jnp.maximum(n - 1, 0), r, dil)
            q = q_ref[0, cur, :]
            kcat = jnp.concatenate([k_ref[0, prev, :], k_ref[0, cur, :]], axis=0).astype(BF16)
            vcat = jnp.concatenate([v_ref[0, prev, :], v_ref[0, cur, :]], axis=0).astype(BF16)
            valid = _band_mask(n)
            q2 = jnp.concatenate([jnp.where(first, q, 0.0), jnp.where(first, 0.0, q)], axis=0).astype(BF16)
            s = jnp.where(jnp.concatenate([valid, valid], axis=0), _dot(q2, kcat, NT), MASKED)
            m = jnp.max(s, axis=-1, keepdims=True)
            p = jnp.exp(s - m)
            den = jnp.sum(p, axis=-1, keepdims=True)
            o2 = _dot(p.astype(BF16), vcat, NN) / den
            lse2 = m + jnp.log(den)
            o_ref[0, cur, :] = jnp.where(first, o2[:BLOCK], o2[BLOCK:])
            lse_ref[0, cur, :] = jnp.where(first, lse2[:BLOCK], lse2[BLOCK:])
            return carry

        lax.fori_loop(0, nb * dil, block, 0, unroll=ATTN_FWD_UNROLL)

    def slab(base):
        return pl.BlockSpec((1, s_len, LANES), lambda s: (base + 2 * group + s, 0, 0))

    out = pl.BlockSpec((1, s_len, LANES), lambda s: (s, 0, 0))
    shape = jax.ShapeDtypeStruct((2, s_len, LANES), F32)
    return pl.pallas_call(
        body, name=f"attn_fwd_d{dil}", grid=(2,),
        in_specs=[slab(0), slab(6), slab(12)], out_specs=[out, out], out_shape=[shape, shape],
        compiler_params=_params(("arbitrary",)),
    )(qkv, qkv, qkv)


def _pool_mixed(u, halo, i, tm):
    ue = jnp.concatenate([halo, u], axis=0)
    s2 = ue + pltpu.roll(ue, 1, 0)
    s4 = s2 + pltpu.roll(s2, 2, 0)
    s8 = s4 + pltpu.roll(s4, 4, 0)
    s16 = s8 + pltpu.roll(s8, 8, 0)
    grp = lax.broadcasted_iota(jnp.int32, (tm, 256), 1) // HEAD_DIM
    pick = lambda a, b, c, e: jnp.where(grp == 0, a, jnp.where(grp == 1, b, jnp.where(grp == 2, c, e)))
    win_sum = pick(s2[HALO:], s4[HALO:], s8[HALO:], s16[HALO:])
    pos = (i * tm + lax.broadcasted_iota(jnp.int32, (tm, 256), 0)).astype(F32)
    count = jnp.minimum(pos + 1.0, pick(*[float(w) for w in POOL_WINDOWS]))
    return win_sum / count - u, count


def _mix_out(x, u_pool, o_g, lse_g, w_blk, b_pool, pool_scale, w_out_t, gt_m, g_post_mix, g_pre_ffn, sc_f, sh_f, tm):
    s_len, d = x.shape

    def body(x_ref, u_ref, uh_ref, o0, o1, o2, l0, l1, l2, wb_ref, bp_ref, ps_ref, wo_ref,
             gt_ref, g1_ref, g2_ref, sc_ref, sh_ref,
             x1_ref, y1_ref, h2_ref, cat_ref, attn_ref, lall_ref):
        i = pl.program_id(0)
        u = u_ref[...]
        halo = uh_ref[...] * (i > 0).astype(F32)
        mixed, _ = _pool_mixed(u, halo, i, tm)
        y = _dot(mixed.astype(BF16), wb_ref[...], NN) + bp_ref[...]
        pool = y * ps_ref[...]
        attn = []
        for s in range(2):
            la, lb, lc = l0[s], l1[s], l2[s]
            mx = jnp.maximum(jnp.maximum(la, lb), lc)
            ea, eb, ec = jnp.exp(la - mx), jnp.exp(lb - mx), jnp.exp(lc - mx)
            den = ea + eb + ec
            lall_ref[s] = mx + jnp.log(den)
            attn.append((ea / den) * o0[s] + (eb / den) * o1[s] + (ec / den) * o2[s])
        attn = jnp.concatenate(attn, axis=1)
        attn_ref[...] = attn
        cat = jnp.concatenate([pool, attn], axis=1).astype(BF16)
        cat_ref[...] = cat
        y1 = _dot(cat, wo_ref[...], NT)
        y1_ref[...] = y1
        x1 = x_ref[...] + gt_ref[...] * (y1 * _rstd(y1) * g1_ref[...])
        x1_ref[...] = x1
        h2 = (x1 * _rstd(x1) * g2_ref[...]) * (1.0 + sc_ref[...]) + sh_ref[...]
        h2_ref[...] = h2.astype(BF16)

    tile = lambda w: pl.BlockSpec((tm, w), lambda i: (i, 0))
    slab = pl.BlockSpec((2, tm, LANES), lambda i: (0, i, 0))
    const = lambda a: pl.BlockSpec(a.shape, lambda i: (0,) * a.ndim)
    return pl.pallas_call(
        body, name="mix_out", grid=(s_len // tm,),
        in_specs=[tile(d), tile(256), pl.BlockSpec((HALO, 256), lambda i: (_halo_before(i, tm), 0)),
                  slab, slab, slab, slab, slab, slab,
                  const(w_blk), const(b_pool), const(pool_scale), const(w_out_t),
                  const(gt_m), const(g_post_mix), const(g_pre_ffn), const(sc_f), const(sh_f)],
        out_specs=[tile(d), tile(d), tile(d), tile(512), tile(256), slab],
        out_shape=[jax.ShapeDtypeStruct((s_len, d), F32), jax.ShapeDtypeStruct((s_len, d), F32),
                   jax.ShapeDtypeStruct((s_len, d), BF16), jax.ShapeDtypeStruct((s_len, 512), BF16),
                   jax.ShapeDtypeStruct((s_len, 256), F32), jax.ShapeDtypeStruct((2, s_len, LANES), F32)],
        compiler_params=_params(("arbitrary",)),
    )(x, u_pool, u_pool, *o_g, *lse_g, w_blk, b_pool, pool_scale, w_out_t, gt_m, g_post_mix, g_pre_ffn, sc_f, sh_f)


def _conv_gate(gate_ext, cw, cb):
    gc = gate_ext * cw[2:3, :] + pltpu.roll(gate_ext, 1, 0) * cw[1:2, :] + pltpu.roll(gate_ext, 2, 0) * cw[0:1, :]
    return gc[HALO:] + cb


def _ffn_fwd_loss(h2, x1, target, w_up_t, w_down, conv_w, conv_b, gt_f, g_post_ffn, tm, tf, ck):
    s_len, d = x1.shape
    d_ff = w_down.shape[0]
    n_f = d_ff // tf

    def body(h_ref, hh_ref, x1_ref, tgt_ref, wg_ref, wv_ref, wd_ref, cw_ref, cb_ref, gt_ref, g_ref,
             gate_ref, val_ref, dy2_ref, dout_ref, sums_ref, loss_ref, acc_ref):
        i, j = pl.program_id(0), pl.program_id(1)

        @pl.when((i == 0) & (j == 0))
        def _():
            sums_ref[...] = jnp.zeros_like(sums_ref)
            loss_ref[...] = jnp.zeros_like(loss_ref)

        h = h_ref[...]
        h_ext = jnp.concatenate([hh_ref[...], h], axis=0)
        row = lax.broadcasted_iota(jnp.int32, (tm + HALO, ck), 0)
        no_halo = (row < HALO) & (i == 0)
        part = None
        for c in range(tf // ck):
            cs = slice(c * ck, (c + 1) * ck)
            gate_ext = jnp.where(no_halo, 0.0, _dot(h_ext, wg_ref[cs, :], NT))
            val = _dot(h, wv_ref[cs, :], NT)
            act, _ = _gelu_parts(_conv_gate(gate_ext, cw_ref[:, cs], cb_ref[:, cs]))
            gate_ref[:, cs] = gate_ext[HALO:].astype(BF16)
            val_ref[:, cs] = val.astype(BF16)
            p = _dot((act * val).astype(BF16), wd_ref[cs, :], NN)
            part = p if part is None else part + p

        @pl.when(j == 0)
        def _():
            acc_ref[...] = part

        @pl.when(j > 0)
        def _():
            acc_ref[...] += part

        @pl.when(j == n_f - 1)
        def _():
            y2 = acc_ref[...]
            rstd = _rstd(y2)
            n = y2 * rstd
            rn = n * g_ref[...]
            err = x1_ref[...] + gt_ref[...] * rn - tgt_ref[...]
            loss_ref[...] += 0.5 * jnp.sum(jnp.mean(err * err, axis=-1, keepdims=True), axis=0, keepdims=True)
            dout = err * (1.0 / d)
            dout_ref[...] = dout
            drn = dout * gt_ref[...]
            sums_ref[0:1, :] += jnp.sum(dout * rn, axis=0, keepdims=True)
            sums_ref[1:2, :] += jnp.sum(drn * n, axis=0, keepdims=True)
            dy2_ref[...] = _norm_bwd(drn * g_ref[...], n, rstd).astype(BF16)

    tok = lambda w: pl.BlockSpec((tm, w), lambda i, j: (i, 0))
    tokf = pl.BlockSpec((tm, tf), lambda i, j: (i, j))
    vec = pl.BlockSpec((1, d), lambda i, j: (0, 0))
    once = {"pipeline_mode": pl.Buffered(1)} if n_f == 1 else {}
    return pl.pallas_call(
        body, name="ffn_fwd_loss", grid=(s_len // tm, n_f),
        in_specs=[tok(d), pl.BlockSpec((HALO, d), lambda i, j: (_halo_before(i, tm), 0)), tok(d), tok(d),
                  pl.BlockSpec((tf, d), lambda i, j: (j, 0), **once),
                  pl.BlockSpec((tf, d), lambda i, j: (j + n_f, 0), **once),
                  pl.BlockSpec((tf, d), lambda i, j: (j, 0), **once),
                  pl.BlockSpec((3, tf), lambda i, j: (0, j)), pl.BlockSpec((1, tf), lambda i, j: (0, j)), vec, vec],
        out_specs=[tokf, tokf, tok(d), tok(d), pl.BlockSpec((8, d), lambda i, j: (0, 0)),
                   pl.BlockSpec((8, LANES), lambda i, j: (0, 0))],
        out_shape=[jax.ShapeDtypeStruct((s_len, d_ff), BF16), jax.ShapeDtypeStruct((s_len, d_ff), BF16),
                   jax.ShapeDtypeStruct((s_len, d), BF16), jax.ShapeDtypeStruct((s_len, d), F32),
                   jax.ShapeDtypeStruct((8, d), F32), jax.ShapeDtypeStruct((8, LANES), F32)],
        scratch_shapes=[pltpu.VMEM((tm, d), F32)],
        compiler_params=_params(("arbitrary", "arbitrary")),
    )(h2, h2, x1, target, w_up_t, w_up_t, w_down, conv_w, conv_b, gt_f, g_post_ffn)


def _ffn_bwd_act(dy2, gate, val, w_down, conv_w, conv_b, tm, tf):
    s_len, d = dy2.shape
    d_ff = w_down.shape[0]
    n_t = s_len // tm

    def body(dy_ref, g_ref, gh_ref, v_ref, wd_ref, cw_ref, cb_ref, dgc_ref, dval_ref, dwd_ref, dconv_ref, acc_ref):
        i = pl.program_id(1)
        gate_ext = jnp.concatenate([gh_ref[...], g_ref[...]], axis=0).astype(F32)
        row = lax.broadcasted_iota(jnp.int32, gate_ext.shape, 0)
        gate_ext = jnp.where((row < HALO) & (i == 0), 0.0, gate_ext)
        act, dact = _gelu_parts(_conv_gate(gate_ext, cw_ref[...], cb_ref[...]))
        v = v_ref[...].astype(F32)
        da = _dot(dy_ref[...], wd_ref[...], NT)
        dgc = da * v * dact
        dgc_ref[...] = dgc.astype(BF16)
        dval_ref[...] = (da * act).astype(BF16)
        dwd = _dot((act * v).astype(BF16), dy_ref[...], TN)
        rows = [jnp.sum(dgc * pltpu.roll(gate_ext, 2 - k, 0)[HALO:], axis=0, keepdims=True) for k in range(2)]
        rows += [jnp.sum(dgc * gate_ext[HALO:], axis=0, keepdims=True), jnp.sum(dgc, axis=0, keepdims=True),
                 jnp.zeros((4, tf), F32)]
        dconv = jnp.concatenate(rows, axis=0)

        @pl.when(i == 0)
        def _():
            acc_ref[...] = dwd
            dconv_ref[...] = dconv

        @pl.when(i > 0)
        def _():
            acc_ref[...] += dwd
            dconv_ref[...] += dconv

        @pl.when(i == n_t - 1)
        def _():
            dwd_ref[...] = acc_ref[...].astype(BF16)

    tokf = pl.BlockSpec((tm, tf), lambda j, i: (i, j))
    return pl.pallas_call(
        body, name="ffn_bwd_act", grid=(d_ff // tf, n_t),
        in_specs=[pl.BlockSpec((tm, d), lambda j, i: (i, 0)), tokf,
                  pl.BlockSpec((HALO, tf), lambda j, i: (_halo_before(i, tm), j)), tokf,
                  pl.BlockSpec((tf, d), lambda j, i: (j, 0)),
                  pl.BlockSpec((3, tf), lambda j, i: (0, j)), pl.BlockSpec((1, tf), lambda j, i: (0, j))],
        out_specs=[tokf, tokf, pl.BlockSpec((tf, d), lambda j, i: (j, 0)), pl.BlockSpec((8, tf), lambda j, i: (0, j))],
        out_shape=[jax.ShapeDtypeStruct((s_len, d_ff), BF16), jax.ShapeDtypeStruct((s_len, d_ff), BF16),
                   jax.ShapeDtypeStruct((d_ff, d), BF16), jax.ShapeDtypeStruct((8, d_ff), F32)],
        scratch_shapes=[pltpu.VMEM((tf, d), F32)],
        compiler_params=_params(("arbitrary", "arbitrary")),
    )(dy2, gate, gate, val, w_down, conv_w, conv_b)


def _ffn_bwd_up(dgc, dval, w_up_t, conv_w, tm):
    s_len, d_ff = dgc.shape
    d = w_up_t.shape[1]
    n_t = s_len // tm

    def body(dg_ref, dgn_ref, dv_ref, cw_ref, w_ref, dup_ref, dh_ref):
        i = pl.program_id(0)
        nxt = dgn_ref[...].astype(F32) * (i < n_t - 1).astype(F32)
        ext = jnp.concatenate([dg_ref[...].astype(F32), nxt], axis=0)
        rows = tm + HALO
        dgate = (ext * cw_ref[2:3, :] + pltpu.roll(ext, rows - 1, 0) * cw_ref[1:2, :]
                 + pltpu.roll(ext, rows - 2, 0) * cw_ref[0:1, :])[:tm]
        dup = jnp.concatenate([dgate.astype(BF16), dv_ref[...]], axis=1)
        dup_ref[...] = dup
        dh_ref[...] = _dot(dup, w_ref[...], NN)

    tokf = pl.BlockSpec((tm, d_ff), lambda i: (i, 0))
    return pl.pallas_call(
        body, name="ffn_bwd_up", grid=(n_t,),
        in_specs=[tokf, pl.BlockSpec((HALO, d_ff), lambda i: (jnp.minimum((i + 1) * (tm // HALO), s_len // HALO - 1), 0)),
                  tokf, pl.BlockSpec((3, d_ff), lambda i: (0, 0)), pl.BlockSpec((2 * d_ff, d), lambda i: (0, 0))],
        out_specs=[pl.BlockSpec((tm, 2 * d_ff), lambda i: (i, 0)), pl.BlockSpec((tm, d), lambda i: (i, 0))],
        out_shape=[jax.ShapeDtypeStruct((s_len, 2 * d_ff), BF16), jax.ShapeDtypeStruct((s_len, d), F32)],
        compiler_params=_params(("arbitrary",)),
    )(dgc, dgc, dval, conv_w, w_up_t)


def _mix_bwd(dh2, dout, x1, y1, cat, attn, w_out_t, sc_f, g_pre_ffn, gt_m, g_post_mix, tm):
    s_len, d = x1.shape
    n_t = s_len // tm

    def body(dh_ref, do_ref, x1_ref, y1_ref, cat_ref, at_ref, wo_ref, sc_ref, g2_ref, gt_ref, g1_ref,
             dx1_ref, dpool_ref, dattn_ref, delta_ref, dwo_ref, sums_ref, acc_ref):
        i = pl.program_id(0)
        dh = dh_ref[...]
        x1 = x1_ref[...]
        r2 = _rstd(x1)
        n2 = x1 * r2
        ng = n2 * g2_ref[...]
        dng = dh * (1.0 + sc_ref[...])
        dx1 = do_ref[...] + _norm_bwd(dng * g2_ref[...], n2, r2)
        dx1_ref[...] = dx1
        y1 = y1_ref[...]
        r1 = _rstd(y1)
        n1 = y1 * r1
        drn = dx1 * gt_ref[...]
        dy1 = _norm_bwd(drn * g1_ref[...], n1, r1).astype(BF16)
        dcat = _dot(dy1, wo_ref[...], NN)
        dpool_ref[...] = dcat[:, 0:256]
        lane = lax.broadcasted_iota(jnp.int32, (tm, LANES), 1)
        first = lane < HEAD_DIM
        for s in range(2):
            da = dcat[:, 256 + s * LANES:256 + (s + 1) * LANES]
            dattn_ref[s] = da
            prod = da * at_ref[:, s * LANES:(s + 1) * LANES]
            tot = jnp.sum(prod, axis=-1, keepdims=True)
            lo = jnp.sum(jnp.where(first, prod, 0.0), axis=-1, keepdims=True)
            delta_ref[s] = jnp.where(first, lo, tot - lo)
        dwo = _dot(dy1, cat_ref[...], TN)
        sums = jnp.concatenate(
            [jnp.sum(dh, axis=0, keepdims=True), jnp.sum(dh * ng, axis=0, keepdims=True),
             jnp.sum(dng * n2, axis=0, keepdims=True), jnp.sum(dx1 * (n1 * g1_ref[...]), axis=0, keepdims=True),
             jnp.sum(drn * n1, axis=0, keepdims=True), jnp.zeros((3, d), F32)], axis=0)

        @pl.when(i == 0)
        def _():
            acc_ref[...] = dwo
            sums_ref[...] = sums

        @pl.when(i > 0)
        def _():
            acc_ref[...] += dwo
            sums_ref[...] += sums

        @pl.when(i == n_t - 1)
        def _():
            dwo_ref[...] = acc_ref[...].astype(BF16)

    tile = lambda w: pl.BlockSpec((tm, w), lambda i: (i, 0))
    slab = pl.BlockSpec((2, tm, LANES), lambda i: (0, i, 0))
    vec = pl.BlockSpec((1, d), lambda i: (0, 0))
    return pl.pallas_call(
        body, name="mix_bwd", grid=(n_t,),
        in_specs=[tile(d), tile(d), tile(d), tile(d), tile(512), tile(256),
                  pl.BlockSpec((d, 512), lambda i: (0, 0)), vec, vec, vec, vec],
        out_specs=[tile(d), tile(256), slab, slab, pl.BlockSpec((d, 512), lambda i: (0, 0)),
                   pl.BlockSpec((8, d), lambda i: (0, 0))],
        out_shape=[jax.ShapeDtypeStruct((s_len, d), F32), jax.ShapeDtypeStruct((s_len, 256), F32),
                   jax.ShapeDtypeStruct((2, s_len, LANES), F32), jax.ShapeDtypeStruct((2, s_len, LANES), F32),
                   jax.ShapeDtypeStruct((d, 512), BF16), jax.ShapeDtypeStruct((8, d), F32)],
        scratch_shapes=[pltpu.VMEM((d, 512), F32)],
        compiler_params=_params(("arbitrary",)),
    )(dh2, dout, x1, y1, cat, attn, w_out_t, sc_f, g_pre_ffn, gt_m, g_post_mix)


def _pool_bwd(dpool, u_pool, w_blk, b_pool, pool_scale, tm):
    s_len = dpool.shape[0]
    n_t = s_len // tm

    def body(dp_ref, dpn_ref, u_ref, uh_ref, wb_ref, bp_ref, ps_ref, du_ref, dwb_ref, sums_ref):
        i = pl.program_id(0)
        u = u_ref[...]
        mixed, _ = _pool_mixed(u, uh_ref[...] * (i > 0).astype(F32), i, tm)
        mixed_b = mixed.astype(BF16)
        y = _dot(mixed_b, wb_ref[...], NN) + bp_ref[...]
        dp = dp_ref[...]
        dy = dp * ps_ref[...]
        dwb = _dot(mixed_b, dy.astype(BF16), TN)
        sums = jnp.concatenate([jnp.sum(dy, axis=0, keepdims=True), jnp.sum(dp * y, axis=0, keepdims=True),
                                jnp.zeros((6, 256), F32)], axis=0)
        dp_ext = jnp.concatenate([dp, dpn_ref[...] * (i < n_t - 1).astype(F32)], axis=0)
        dmix = _dot((dp_ext * ps_ref[...]).astype(BF16), wb_ref[...], NT)
        rows = tm + HALO
        grp = lax.broadcasted_iota(jnp.int32, (rows, 256), 1) // HEAD_DIM
        pick = lambda a, b, c, e: jnp.where(grp == 0, a, jnp.where(grp == 1, b, jnp.where(grp == 2, c, e)))
        pos = (i * tm + lax.broadcasted_iota(jnp.int32, (rows, 256), 0)).astype(F32)
        z = dmix / jnp.minimum(pos + 1.0, pick(*[float(w) for w in POOL_WINDOWS]))
        f2 = z + pltpu.roll(z, rows - 1, 0)
        f4 = f2 + pltpu.roll(f2, rows - 2, 0)
        f8 = f4 + pltpu.roll(f4, rows - 4, 0)
        f16 = f8 + pltpu.roll(f8, rows - 8, 0)
        du_ref[...] = (pick(f2, f4, f8, f16) - dmix)[:tm]

        @pl.when(i == 0)
        def _():
            dwb_ref[...] = dwb
            sums_ref[...] = sums

        @pl.when(i > 0)
        def _():
            dwb_ref[...] += dwb
            sums_ref[...] += sums

    tile = pl.BlockSpec((tm, 256), lambda i: (i, 0))
    const = lambda a: pl.BlockSpec(a.shape, lambda i: (0,) * a.ndim)
    return pl.pallas_call(
        body, name="pool_bwd", grid=(n_t,),
        in_specs=[tile, pl.BlockSpec((HALO, 256), lambda i: (jnp.minimum((i + 1) * (tm // HALO), s_len // HALO - 1), 0)),
                  tile, pl.BlockSpec((HALO, 256), lambda i: (_halo_before(i, tm), 0)),
                  const(w_blk), const(b_pool), const(pool_scale)],
        out_specs=[tile, pl.BlockSpec((256, 256), lambda i: (0, 0)), pl.BlockSpec((8, 256), lambda i: (0, 0))],
        out_shape=[jax.ShapeDtypeStruct((s_len, 256), F32), jax.ShapeDtypeStruct((256, 256), F32),
                   jax.ShapeDtypeStruct((8, 256), F32)],
        compiler_params=_params(("arbitrary",)),
    )(dpool, dpool, u_pool, u_pool, w_blk, b_pool, pool_scale)


def _attn_bwd(qkv, dattn, lse_all, delta, group, dil):
    s_len = qkv.shape[1]
    nb = s_len // (BLOCK * dil)

    def body(q_ref, k_ref, v_ref, do_ref, l_ref, dl_ref, dq_ref, dk_ref, dv_ref):
        lane = lax.broadcasted_iota(jnp.int32, (BLOCK, LANES), 1)
        first = lane < HEAD_DIM

        def block(t, carry):
            dk_part, dv_part = carry
            r, n = t // nb, t % nb
            cur = _block_rows(n, r, dil)
            prev = _block_rows(jnp.maximum(n - 1, 0), r, dil)
            q = q_ref[0, cur, :]
            do = do_ref[0, cur, :]
            lse = l_ref[0, cur, :]
            dlt = dl_ref[0, cur, :]
            kcat = jnp.concatenate([k_ref[0, prev, :], k_ref[0, cur, :]], axis=0).astype(BF16)
            vcat = jnp.concatenate([v_ref[0, prev, :], v_ref[0, cur, :]], axis=0).astype(BF16)
            valid = _band_mask(n)
            stack = lambda a: jnp.concatenate([jnp.where(first, a, 0.0), jnp.where(first, 0.0, a)], axis=0)
            rows2 = lambda a: jnp.concatenate([a[:, 0:1], a[:, HEAD_DIM:HEAD_DIM + 1]], axis=0)
            q2, do2 = stack(q).astype(BF16), stack(do).astype(BF16)
            valid2 = jnp.concatenate([valid, valid], axis=0)
            p = jnp.where(valid2, jnp.exp(_dot(q2, kcat, NT) - rows2(lse)), 0.0)
            ds = (p * (_dot(do2, vcat, NT) - rows2(dlt))).astype(BF16)
            dq2 = _dot(ds, kcat, NN)
            dq_ref[0, cur, :] = jnp.where(first, dq2[:BLOCK], dq2[BLOCK:])
            dkc = _dot(ds, q2, TN)
            dvc = _dot(p.astype(BF16), do2, TN)
            dk_ref[0, prev, :] = dk_part + dkc[:BLOCK]
            dv_ref[0, prev, :] = dv_part + dvc[:BLOCK]
            dk_ref[0, cur, :] = dkc[BLOCK:]
            dv_ref[0, cur, :] = dvc[BLOCK:]
            return dkc[BLOCK:], dvc[BLOCK:]

        def blocks(tt, carry):
            for u in range(ATTN_BWD_UNROLL):
                carry = block(tt * ATTN_BWD_UNROLL + u, carry)
            return carry

        zero = jnp.zeros((BLOCK, LANES), F32)
        lax.fori_loop(0, nb * dil // ATTN_BWD_UNROLL, blocks, (zero, zero))

    def slab(base):
        return pl.BlockSpec((1, s_len, LANES), lambda s: (base + 2 * group + s, 0, 0))

    one = pl.BlockSpec((1, s_len, LANES), lambda s: (s, 0, 0))
    shape = jax.ShapeDtypeStruct((2, s_len, LANES), F32)
    return pl.pallas_call(
        body, name=f"attn_bwd_d{dil}", grid=(2,),
        in_specs=[slab(0), slab(6), slab(12), one, one, one],
        out_specs=[one, one, one], out_shape=[shape, shape, shape],
        compiler_params=_params(("arbitrary",)),
    )(qkv, qkv, qkv, dattn, lse_all, delta)


def _inproj_bwd(du, dqkv, rope, w_in_t, x, dx1, sc_m, g_pre_mix, tm):
    s_len, d = x.shape
    n_proj = w_in_t.shape[0]
    n_t = s_len // tm

    def body(du_ref, *refs):
        dref = refs[:9]
        rope_ref, w_ref, x_ref, dx1_ref, sc_ref, g_ref, dproj_ref, dx_ref, sums_ref = refs[9:]
        i = pl.program_id(0)
        cols = [du_ref[...].astype(BF16)]
        for kind in range(3):
            for grp in range(3):
                for s in range(2):
                    piece = dref[3 * grp + kind][s]
                    if kind < 2:
                        piece = _rope_bwd(piece, rope_ref)
                    if kind == 0:
                        piece = piece * (HEAD_DIM ** -0.5)
                    cols.append(piece.astype(BF16))
        dproj = jnp.concatenate(cols, axis=1)
        dproj_ref[...] = dproj
        dh = _dot(dproj, w_ref[...], NN)
        xv = x_ref[...]
        r = _rstd(xv)
        n = xv * r
        dng = dh * (1.0 + sc_ref[...])
        dx_ref[...] = dx1_ref[...] + _norm_bwd(dng * g_ref[...], n, r)
        sums = jnp.concatenate([jnp.sum(dh, axis=0, keepdims=True), jnp.sum(dh * (n * g_ref[...]), axis=0, keepdims=True),
                                jnp.sum(dng * n, axis=0, keepdims=True), jnp.zeros((5, d), F32)], axis=0)

        @pl.when(i == 0)
        def _():
            sums_ref[...] = sums

        @pl.when(i > 0)
        def _():
            sums_ref[...] += sums

    tile = lambda w: pl.BlockSpec((tm, w), lambda i: (i, 0))
    slab = pl.BlockSpec((2, tm, LANES), lambda i: (0, i, 0))
    vec = pl.BlockSpec((1, d), lambda i: (0, 0))
    return pl.pallas_call(
        body, name="inproj_bwd", grid=(n_t,),
        in_specs=[tile(256)] + [slab] * 9 + [pl.BlockSpec((3, tm, LANES), lambda i: (0, i, 0)),
                                             pl.BlockSpec((n_proj, d), lambda i: (0, 0)), tile(d), tile(d), vec, vec],
        out_specs=[tile(n_proj), tile(d), pl.BlockSpec((8, d), lambda i: (0, 0))],
        out_shape=[jax.ShapeDtypeStruct((s_len, n_proj), BF16), jax.ShapeDtypeStruct((s_len, d), F32),
                   jax.ShapeDtypeStruct((8, d), F32)],
        compiler_params=_params(("arbitrary",)),
    )(du, *dqkv, rope, w_in_t, x, dx1, sc_m, g_pre_mix)


def _wgrad(a, b, name, tk, tmm):
    s_len, m = a.shape
    n = b.shape[1]
    n_k = s_len // tk

    def body(a_ref, b_ref, o_ref, acc_ref):
        k = pl.program_id(1)
        part = _dot(a_ref[...], b_ref[...], TN)

        @pl.when(k == 0)
        def _():
            acc_ref[...] = part

        @pl.when(k > 0)
        def _():
            acc_ref[...] += part

        @pl.when(k == n_k - 1)
        def _():
            o_ref[...] = acc_ref[...].astype(BF16)

    return pl.pallas_call(
        body, name=name, grid=(m // tmm, n_k),
        in_specs=[pl.BlockSpec((tk, tmm), lambda j, k: (k, j)), pl.BlockSpec((tk, n), lambda j, k: (k, 0))],
        out_specs=pl.BlockSpec((tmm, n), lambda j, k: (j, 0)),
        out_shape=jax.ShapeDtypeStruct((m, n), BF16),
        scratch_shapes=[pltpu.VMEM((tmm, n), F32)],
        compiler_params=_params(("arbitrary", "arbitrary")),
    )(a, b)


def _place():
    return lax.axis_index("x"), lax.axis_index("y"), lax.axis_index("c")


def _peer(k):
    x, y, c = _place()
    bx, by, bc = (k >> 2) & 1, (k >> 1) & 1, k & 1
    return (x ^ bx if bx else x, y ^ by if by else y, c ^ bc if bc else c)


def _index(pos):
    return 4 * pos[0] + 2 * pos[1] + pos[2]


def _ada_exchange(c_rows, w_ada, b_ada_cols, taps):
    d = c_rows.shape[1]
    ncol = w_ada.shape[1]

    def body(c_ref, w_ref, b_ref, t_ref, call_ref, mod_ref, tall_ref, stage_ref, send_sems, recv_sems):
        me = _index(_place())
        call_ref[me] = c_ref[...]
        tall_ref[me] = t_ref[...]

        def gather(k):
            return pltpu.make_async_remote_copy(
                src_ref=c_ref, dst_ref=call_ref.at[me], send_sem=send_sems.at[0, k - 1], recv_sem=recv_sems.at[0, k - 1],
                device_id=_peer(k), device_id_type=MESH)

        def gather_taps(k):
            return pltpu.make_async_remote_copy(
                src_ref=t_ref, dst_ref=tall_ref.at[me], send_sem=send_sems.at[2, k - 1], recv_sem=recv_sems.at[2, k - 1],
                device_id=_peer(k), device_id_type=MESH)

        for k in range(1, N_DEV):
            gather(k).start()
        for k in range(1, N_DEV):
            gather_taps(k).start()
        for k in range(1, N_DEV):
            gather(k).wait_recv()
        cv = jnp.concatenate([call_ref[b, 0:1, :] for b in range(N_DEV)], axis=0)
        act = cv * jax.nn.sigmoid(cv)
        mod = lax.dot_general(act, w_ref[...], NN, preferred_element_type=F32,
                              precision=lax.Precision.HIGHEST) + b_ref[...]
        for b in range(N_DEV):
            stage_ref[b] = jnp.broadcast_to(mod[b:b + 1, :], (8, ncol))
        mod_ref[me] = stage_ref[me]

        def scatter(k):
            return pltpu.make_async_remote_copy(
                src_ref=stage_ref.at[_index(_peer(k))], dst_ref=mod_ref.at[me],
                send_sem=send_sems.at[1, k - 1], recv_sem=recv_sems.at[1, k - 1],
                device_id=_peer(k), device_id_type=MESH)

        for k in range(1, N_DEV):
            scatter(k).start()
        for k in range(1, N_DEV):
            scatter(k).wait_recv()
        for k in range(1, N_DEV):
            gather_taps(k).wait_recv()
        for k in range(1, N_DEV):
            gather(k).wait_send()
            scatter(k).wait_send()
            gather_taps(k).wait_send()

    vmem = pl.BlockSpec(memory_space=pltpu.VMEM)
    return pl.pallas_call(
        body, name="ada_exchange",
        in_specs=[vmem] * 4, out_specs=[vmem] * 3,
        out_shape=[jax.ShapeDtypeStruct((N_DEV, 8, d), F32), jax.ShapeDtypeStruct((N_DEV, 8, ncol), F32),
                   jax.ShapeDtypeStruct((N_DEV,) + taps.shape, F32)],
        scratch_shapes=[pltpu.VMEM((N_DEV, 8, ncol), F32), pltpu.SemaphoreType.DMA((3, N_DEV - 1)),
                        pltpu.SemaphoreType.DMA((3, N_DEV - 1))],
        compiler_params=_params(),
    )(c_rows, w_ada, b_ada_cols, taps)


def _gather_weights(shards):
    n_w = len(shards)

    def body(*refs):
        srcs, outs = refs[:n_w], refs[n_w:2 * n_w]
        send_sems, recv_sems, local_sems = refs[2 * n_w:]
        x, y, c = _place()
        me, sibling = (x, y, c), (x, y, 1 - c)
        chips = [(1 - x, y), (x, 1 - y), (1 - x, 1 - y)]

        def rows(w, pos):
            r = shards[w].shape[0]
            return outs[w].at[pl.ds(pl.multiple_of(_index(pos) * r, 16), r), :]

        def copy(k, w, block, to, own=False):
            return pltpu.make_async_remote_copy(
                src_ref=srcs[w] if own else rows(w, block), dst_ref=rows(w, block),
                send_sem=send_sems.at[k, w], recv_sem=recv_sems.at[k, w], device_id=to, device_id_type=MESH)

        mine = [pltpu.make_async_copy(srcs[w], rows(w, me), local_sems.at[w]) for w in range(n_w)]
        for cp in mine:
            cp.start()
        first = [copy(0, w, me, sibling, own=True) for w in range(n_w)]
        first += [copy(1 + j, w, me, (*chip, c), own=True) for j, chip in enumerate(chips) for w in range(n_w)]
        for cp in first:
            cp.start()
        passed = []
        for j, chip in enumerate(chips):
            for w in range(n_w):
                copy(1 + j, w, (*chip, c), me).wait_recv()
                fwd = copy(4 + j, w, (*chip, c), sibling)
                fwd.start()
                passed.append(fwd)
        for w in range(n_w):
            copy(0, w, sibling, me).wait_recv()
        for j, chip in enumerate(chips):
            for w in range(n_w):
                copy(4 + j, w, (*chip, 1 - c), me).wait_recv()
        for cp in first + passed:
            cp.wait_send()
        for cp in mine:
            cp.wait()

    hbm = pl.BlockSpec(memory_space=pltpu.HBM)
    return pl.pallas_call(
        body, name="gather_weights",
        in_specs=[hbm] * n_w, out_specs=[hbm] * n_w,
        out_shape=[jax.ShapeDtypeStruct((N_DEV * s.shape[0], s.shape[1]), s.dtype) for s in shards],
        scratch_shapes=[pltpu.SemaphoreType.DMA((N_DEV - 1, n_w)), pltpu.SemaphoreType.DMA((N_DEV - 1, n_w)),
                        pltpu.SemaphoreType.DMA((n_w,))],
        compiler_params=_params(),
    )(*shards)


def _scatter_grads(grads):
    n_w = len(grads)

    def body(*refs):
        srcs, outs = refs[:n_w], refs[n_w:2 * n_w]
        send_sems, recv_sems, local_sems = refs[2 * n_w:]
        me = _index(_place())

        def slab(w, dev):
            r = grads[w].shape[0] // N_DEV
            return srcs[w].at[pl.ds(pl.multiple_of(dev * r, 16), r), :]

        def copy(k, w):
            return pltpu.make_async_remote_copy(
                src_ref=slab(w, _index(_peer(k))), dst_ref=outs[w].at[me],
                send_sem=send_sems.at[k - 1, w], recv_sem=recv_sems.at[k - 1, w],
                device_id=_peer(k), device_id_type=MESH)

        mine = [pltpu.make_async_copy(slab(w, me), outs[w].at[me], local_sems.at[w]) for w in range(n_w)]
        for cp in mine:
            cp.start()
        sends = [copy(k, w) for k in range(1, N_DEV) for w in range(n_w)]
        for cp in sends:
            cp.start()
        for cp in sends:
            cp.wait_recv()
        for cp in sends:
            cp.wait_send()
        for cp in mine:
            cp.wait()

    hbm = pl.BlockSpec(memory_space=pltpu.HBM)
    return pl.pallas_call(
        body, name="scatter_grads",
        in_specs=[hbm] * n_w, out_specs=[hbm] * n_w,
        out_shape=[jax.ShapeDtypeStruct((N_DEV, g.shape[0] // N_DEV, g.shape[1]), g.dtype) for g in grads],
        scratch_shapes=[pltpu.SemaphoreType.DMA((N_DEV - 1, n_w)), pltpu.SemaphoreType.DMA((N_DEV - 1, n_w)),
                        pltpu.SemaphoreType.DMA((n_w,))],
        compiler_params=_params(),
    )(*grads)


def _peer_copies(mode, srcs, lands, send_sems, recv_sems):
    me = _index(_place())
    copies = []
    for k in range(1, N_DEV):
        peer = _peer(k)
        for w, (src, land) in enumerate(zip(srcs, lands)):
            if mode == "gather":
                r = src.shape[0]
                dst = land.at[pl.ds(pl.multiple_of(me * r, 16), r), :]
            else:
                r = src.shape[0] // N_DEV
                src = src.at[pl.ds(pl.multiple_of(_index(peer) * r, 16), r), :]
                dst = land.at[me]
            copies.append(pltpu.make_async_remote_copy(
                src_ref=src, dst_ref=dst, send_sem=send_sems.at[(k - 1) * len(srcs) + w],
                recv_sem=recv_sems.at[(k - 1) * len(srcs) + w],
                device_id=peer, device_id_type=MESH))
    return copies


def _exchange_start(mode, srcs, lands, name):
    n = len(srcs)

    def body(*refs):
        for cp in _peer_copies(mode, refs[:n], refs[n:2 * n], refs[2 * n], refs[2 * n + 1]):
            cp.start()
        refs[-1][...] = jnp.zeros_like(refs[-1])

    hbm, sem = pl.BlockSpec(memory_space=pltpu.HBM), pl.BlockSpec(memory_space=pltpu.SEMAPHORE)
    arrays = list(srcs) + list(lands)
    out = pl.pallas_call(
        body, name=name,
        out_shape=(pltpu.SemaphoreType.DMA(((N_DEV - 1) * n,)), pltpu.SemaphoreType.DMA(((N_DEV - 1) * n,)),
                   *[pltpu.HBM(a.shape, a.dtype) for a in arrays], jax.ShapeDtypeStruct((8, LANES), F32)),
        in_specs=[hbm] * (2 * n), out_specs=(sem, sem, *[hbm] * (2 * n), pl.BlockSpec(memory_space=pltpu.VMEM)),
        input_output_aliases={i: 2 + i for i in range(2 * n)},
        compiler_params=pltpu.CompilerParams(has_side_effects=pltpu.SideEffectType.DATAFLOW_SIDE_EFFECTING),
    )(*[pltpu.with_memory_space_constraint(a, pltpu.HBM) for a in arrays])
    return out[0], out[1], out[2:2 + n], out[2 + n:2 + 2 * n], out[-1]


def _exchange_wait(mode, send_sems, recv_sems, srcs, lands, after, name):
    n = len(srcs)

    def body(*refs):
        copies = _peer_copies(mode, refs[:n], refs[n:2 * n], refs[2 * n], refs[2 * n + 1])
        for cp in copies:
            cp.wait_send()
        for cp in copies:
            cp.wait_recv()

    hbm, sem = pl.BlockSpec(memory_space=pltpu.HBM), pl.BlockSpec(memory_space=pltpu.SEMAPHORE)
    arrays = list(srcs) + list(lands)
    out = pl.pallas_call(
        body, name=name, out_shape=tuple(pltpu.HBM(a.shape, a.dtype) for a in arrays),
        in_specs=[hbm] * (2 * n) + [sem, sem, pl.BlockSpec(memory_space=pl.ANY)], out_specs=tuple([hbm] * (2 * n)),
        input_output_aliases={i: i for i in range(2 * n)},
        compiler_params=pltpu.CompilerParams(has_side_effects=pltpu.SideEffectType.DATAFLOW_SIDE_EFFECTING),
    )(*arrays, send_sems, recv_sems, after)
    return out[n:]


SMALL_WEIGHTS = ("b_ada", "g_pre_mix", "g_post_mix", "g_pre_ffn", "g_post_ffn", "w_pool", "b_pool", "pool_scale", "conv_b")


def _small_reduce_adam(sums_in, sums_mix, sums_ffn, sums_pool, dw_blk, dconv, loss_loc, weights, moms, vels):
    locals_ = [sums_in, sums_mix, sums_ffn, sums_pool, dw_blk, dconv, loss_loc]
    n_l, n_w = len(locals_), len(weights)
    d = sums_in.shape[1]

    def body(*refs):
        loc = refs[:n_l]
        w_refs, m_refs, v_refs = (refs[n_l + k * n_w:n_l + (k + 1) * n_w] for k in range(3))
        outs = refs[n_l + 3 * n_w:n_l + 7 * n_w]
        dmod_ref, dconv_ref, loss_ref = refs[n_l + 7 * n_w:n_l + 7 * n_w + 3]
        gathered = refs[n_l + 7 * n_w + 3:n_l + 7 * n_w + 3 + n_l]
        send_sems, recv_sems = refs[-2:]
        me = _index(_place())

        def copy(a, k):
            return pltpu.make_async_remote_copy(
                src_ref=loc[a], dst_ref=gathered[a].at[me], send_sem=send_sems.at[a, k - 1],
                recv_sem=recv_sems.at[a, k - 1], device_id=_peer(k), device_id_type=MESH)

        copies = [copy(a, k) for k in range(1, N_DEV) for a in range(n_l)]
        for cp in copies:
            cp.start()
        for a in range(n_l):
            gathered[a][me] = loc[a][...]
        for cp in copies:
            cp.wait_recv()

        def total(a):
            tot = gathered[a][0]
            for dev in range(1, N_DEV):
                tot = tot + gathered[a][dev]
            return tot

        t_in, t_mix, t_ffn, t_pool, t_blk, t_conv, t_loss = (total(a) for a in range(n_l))
        dconv_ref[...] = t_conv
        loss_ref[...] = t_loss
        mod_rows = ((0, 0), (0, 1), (1, 3), (1, 0), (1, 1), (2, 0))
        for dev in range(N_DEV):
            for k, (a, r) in enumerate(mod_rows):
                dmod_ref[dev:dev + 1, k * d:(k + 1) * d] = gathered[a][dev, r:r + 1, :]

        def update(idx, g, at=()):
            sel = lambda ref: ref.at[at] if at else ref
            delta, nm, nv = _adam_math(sel(w_refs[idx])[...], g, sel(m_refs[idx])[...], sel(v_refs[idx])[...])
            for k, val in enumerate((g, delta, nm, nv)):
                sel(outs[4 * idx + k])[...] = val

        tots = (t_in, t_mix, t_ffn)
        update(0, jnp.concatenate([tots[a][r:r + 1] for a, r in mod_rows], axis=1))
        update(1, t_in[2:3])
        update(2, t_mix[4:5])
        update(3, t_mix[2:3])
        update(4, t_ffn[1:2])
        for gi in range(len(POOL_WINDOWS)):
            lo = gi * HEAD_DIM
            update(5, t_blk[lo:lo + HEAD_DIM, lo:lo + HEAD_DIM], at=(0, gi))
        update(6, jnp.concatenate([t_pool[0:1, gi * HEAD_DIM:(gi + 1) * HEAD_DIM] for gi in range(len(POOL_WINDOWS))], axis=0),
               at=(0,))
        update(7, t_pool[1:2])
        update(8, t_conv[3:4])
        for cp in copies:
            cp.wait_send()

    vmem = pl.BlockSpec(memory_space=pltpu.VMEM)
    shape = lambda a: jax.ShapeDtypeStruct(a.shape, F32)
    out = pl.pallas_call(
        body, name="small_reduce_adam",
        in_specs=[vmem] * (n_l + 3 * n_w), out_specs=[vmem] * (4 * n_w + 3),
        out_shape=[shape(w) for w in weights for _ in range(4)]
        + [jax.ShapeDtypeStruct((N_DEV, 6 * d), F32), shape(dconv), shape(loss_loc)],
        scratch_shapes=[pltpu.VMEM((N_DEV,) + a.shape, F32) for a in locals_]
        + [pltpu.SemaphoreType.DMA((n_l, N_DEV - 1)), pltpu.SemaphoreType.DMA((n_l, N_DEV - 1))],
        compiler_params=_params(),
    )(*locals_, *weights, *moms, *vels)
    return out[:4 * n_w], out[4 * n_w], out[4 * n_w + 1], out[4 * n_w + 2]


def _adam_math(w, g, m, v):
    m = ADAM_B1 * m + (1.0 - ADAM_B1) * g
    v = ADAM_B2 * v + (1.0 - ADAM_B2) * (g * g)
    m_hat = m / (1.0 - ADAM_B1 ** ADAM_STEP)
    v_hat = v / (1.0 - ADAM_B2 ** ADAM_STEP)
    delta = -ADAM_LR * (m_hat / (jnp.sqrt(v_hat) + ADAM_EPS) + ADAM_WD * w)
    return delta, m, v


def _adam(w, g, m, v, name, tr):
    rows, cols = w.shape

    def body(w_ref, g_ref, m_ref, v_ref, d_ref, nm_ref, nv_ref):
        d_ref[...], nm_ref[...], nv_ref[...] = _adam_math(w_ref[...], g_ref[...], m_ref[...], v_ref[...])

    spec = pl.BlockSpec((tr, cols), lambda i: (i, 0))
    shape = jax.ShapeDtypeStruct((rows, cols), F32)
    return pl.pallas_call(
        body, name=name, grid=(rows // tr,), in_specs=[spec] * 4, out_specs=[spec] * 3,
        out_shape=[shape] * 3, compiler_params=_params(("arbitrary",)),
    )(w, g, m, v)


def _sum_adam(parts, w, m, v, name, tr):
    _, rows, cols = parts.shape

    def body(p_ref, w_ref, m_ref, v_ref, g_ref, d_ref, nm_ref, nv_ref):
        g = p_ref[0].astype(F32)
        for dev in range(1, N_DEV):
            g = g + p_ref[dev].astype(F32)
        g_ref[...] = g
        d_ref[...], nm_ref[...], nv_ref[...] = _adam_math(w_ref[...], g, m_ref[...], v_ref[...])

    spec = pl.BlockSpec((tr, cols), lambda i: (i, 0))
    shape = jax.ShapeDtypeStruct((rows, cols), F32)
    return pl.pallas_call(
        body, name=name, grid=(rows // tr,),
        in_specs=[pl.BlockSpec((N_DEV, tr, cols), lambda i: (0, i, 0)), spec, spec, spec],
        out_specs=[spec] * 4, out_shape=[shape] * 4, compiler_params=_params(("arbitrary",)),
    )(parts, w, m, v)


def _ada_grad_adam(c_all, dmod_cols, w, m, v, tr):
    rows, cols = w.shape

    def body(c_ref, dm_ref, w_ref, m_ref, v_ref, g_ref, d_ref, nm_ref, nv_ref):
        cv = c_ref[...]
        act = cv * jax.nn.sigmoid(cv)
        g = lax.dot_general(act, dm_ref[...], TN, preferred_element_type=F32, precision=lax.Precision.HIGHEST)
        g_ref[...] = g
        d_ref[...], nm_ref[...], nv_ref[...] = _adam_math(w_ref[...], g, m_ref[...], v_ref[...])

    spec = pl.BlockSpec((tr, cols), lambda i: (i, 0))
    shape = jax.ShapeDtypeStruct((rows, cols), F32)
    return pl.pallas_call(
        body, name="ada_grad_adam", grid=(rows // tr,),
        in_specs=[pl.BlockSpec((N_DEV, tr), lambda i: (0, i)), pl.BlockSpec((N_DEV, cols), lambda i: (0, 0)), spec, spec, spec],
        out_specs=[spec] * 4, out_shape=[shape] * 4, compiler_params=_params(("arbitrary",)),
    )(c_all, dmod_cols, w, m, v)


def _rope_tables(positions):
    s_len = positions.shape[0]
    inv_freq = ROPE_THETA ** (-jnp.arange(0, 2 * ROT_HALF, 2, dtype=F32) / (2 * ROT_HALF))
    ang = positions.astype(F32)[:, None] * inv_freq
    cos, sin = jnp.cos(ang), jnp.sin(ang)
    rest = HEAD_DIM - 2 * ROT_HALF
    zero = lambda n: jnp.zeros((s_len, n), F32)
    head = jnp.stack([jnp.concatenate([cos, cos, jnp.ones((s_len, rest), F32)], axis=1),
                      jnp.concatenate([-sin, zero(HEAD_DIM - ROT_HALF)], axis=1),
                      jnp.concatenate([zero(ROT_HALF), sin, zero(rest)], axis=1)])
    return jnp.tile(head, (1, 1, LANES // HEAD_DIM))


def _pad_rows(a, rows):
    return jnp.pad(a, ((0, rows - a.shape[0]), (0, 0)))


def _as_rows(a, rows):
    flat = a.reshape(-1)
    return jnp.pad(flat, (0, rows * LANES - flat.shape[0])).reshape(rows, LANES)


def _sequence_step(xs, target, rope, mods, gains, w_in_t, w_out_t, fetch_ffn, send_ffn_grads, w_blk_b, b_pool_r,
                   pool_scale_r, conv_w_all, conv_b):
    sh_m, sc_m, gt_m, sh_f, sc_f, gt_f = mods
    g_pre_mix, g_post_mix, g_pre_ffn, g_post_ffn = gains
    h1, u_pool, qkv = _premix_inproj(xs, sh_m, sc_m, g_pre_mix, w_in_t, rope, tm=512)
    o_g, lse_g = [], []
    for gi, dil in enumerate(DILATIONS):
        o, lse = _attn_fwd(qkv, gi, dil)
        o_g.append(o)
        lse_g.append(lse)
    x1, y1, h2, cat, attn, lse_all = _mix_out(xs, u_pool, o_g, lse_g, w_blk_b, b_pool_r, pool_scale_r, w_out_t,
                                              gt_m, g_post_mix, g_pre_ffn, sc_f, sh_f, tm=256)
    w_up_t, w_down_f = fetch_ffn(x1)
    gate, val, dy2, dout, sums_ffn, loss_loc = _ffn_fwd_loss(h2, x1, target, w_up_t, w_down_f, conv_w_all, conv_b,
                                                              gt_f, g_post_ffn, tm=256, tf=2816, ck=256)

    dgc, dval, dw_down, dconv = _ffn_bwd_act(dy2, gate, val, w_down_f, conv_w_all, conv_b, tm=1024, tf=256)
    dup, dh2 = _ffn_bwd_up(dgc, dval, w_up_t, conv_w_all, tm=256)
    dw_up_t = _wgrad(dup, h2, "wgrad_up", tk=1024, tmm=1408)
    token = send_ffn_grads(dw_up_t, dw_down)
    if token is not None:
        sc_f = sc_f + token[0:1, 0:1]
    dx1, dpool, dattn, delta, dw_out_t, sums_mix = _mix_bwd(dh2, dout, x1, y1, cat, attn, w_out_t, sc_f, g_pre_ffn,
                                                           gt_m, g_post_mix, tm=256)
    du, dw_blk, sums_pool = _pool_bwd(dpool, u_pool, w_blk_b, b_pool_r, pool_scale_r, tm=512)
    dqkv = []
    for gi, dil in enumerate(DILATIONS):
        dqkv += list(_attn_bwd(qkv, dattn, lse_all, delta, gi, dil))
    dproj, grad_x, sums_in = _inproj_bwd(du, dqkv, rope, w_in_t, xs, dx1, sc_m, g_pre_mix, tm=256)
    dw_in_t = _wgrad(dproj, h1, "wgrad_in", tk=1024, tmm=1280)
    return (loss_loc, grad_x, dw_in_t, dw_out_t, dw_up_t, dw_down, dw_blk, dconv,
            sums_in, sums_mix, sums_ffn, sums_pool)


def kernel(x, c, positions, w_ada, b_ada, g_pre_mix, g_post_mix, g_pre_ffn, g_post_ffn, w_in, w_pool, b_pool, pool_scale, w_out, w_up, conv_w, conv_b, w_down, loss_target, m_w_ada, m_b_ada, m_g_pre_mix, m_g_post_mix, m_g_pre_ffn, m_g_post_ffn, m_w_in, m_w_pool, m_b_pool, m_pool_scale, m_w_out, m_w_up, m_conv_w, m_conv_b, m_w_down, v_w_ada, v_b_ada, v_g_pre_mix, v_g_post_mix, v_g_pre_ffn, v_g_post_ffn, v_w_in, v_w_pool, v_b_pool, v_pool_scale, v_w_out, v_w_up, v_conv_w, v_conv_b, v_w_down):
    s_len, d = x.shape[1], x.shape[2]
    d_ff = w_down.shape[1] * N_DEV
    me = _index(_place())
    xs, target = x[0], loss_target[0]

    ncol = w_ada.shape[2]
    b_cols = lax.dynamic_slice(b_ada, (0, me * ncol), (1, ncol))
    c_all, mod, taps_all = _ada_exchange(jnp.broadcast_to(c, (8, d)), w_ada[0], b_cols, _pad_rows(conv_w[0], 8))
    c_all = c_all[:, 0, :]
    conv_w_all = jnp.transpose(taps_all[:, :3, :], (1, 0, 2)).reshape(3, d_ff)
    sh_m, sc_m, gt_m, sh_f, sc_f, gt_f = [mod[:, 0, :].reshape(1, -1)[:, k * d:(k + 1) * d] for k in range(6)]

    w_in_t, w_out_t = _gather_weights([w_in[0].T.astype(BF16), w_out[0].T.astype(BF16)])

    rope = _rope_tables(positions[0])
    w_blk = jnp.zeros((256, 256), F32)
    for gi in range(4):
        w_blk = lax.dynamic_update_slice(w_blk, w_pool[0, gi], (gi * HEAD_DIM, gi * HEAD_DIM))
    w_blk_b = w_blk.astype(BF16)
    b_pool_r, pool_scale_r = b_pool.reshape(1, 256), pool_scale.reshape(1, 256)

    up_sh, down_sh = w_up[0].T.astype(BF16), w_down[0].astype(BF16)
    w_in_t, conv_w_all, up_sh, down_sh = lax.optimization_barrier((w_in_t, conv_w_all, up_sh, down_sh))
    lands = [lax.dynamic_update_slice(lax.empty((N_DEV * s.shape[0], s.shape[1]), BF16), s, (me * s.shape[0], 0))
             for s in (up_sh, down_sh)]
    w_send, w_recv, w_src, w_land, w_token = _exchange_start("gather", [up_sh, down_sh], lands, "ffn_weights_start")

    def fetch_ffn(after):
        return _exchange_wait("gather", w_send, w_recv, w_src, w_land, after, "ffn_weights_wait")

    flight = []

    def send_ffn_grads(dw_up_t, dw_down):
        lands = []
        for g in (dw_up_t, dw_down):
            r = g.shape[0] // N_DEV
            own = lax.dynamic_slice(g, (me * r, 0), (r, g.shape[1]))
            lands.append(lax.dynamic_update_slice(lax.empty((N_DEV, r, g.shape[1]), BF16), own[None], (me, 0, 0)))
        flight.extend(_exchange_start("scatter", [dw_up_t, dw_down], lands, "ffn_grads_start"))
        return flight[4]

    (loss_loc, grad_x, dw_in_t, dw_out_t, _, _, dw_blk, dconv,
     sums_in, sums_mix, sums_ffn, sums_pool) = _sequence_step(
        xs, target, rope, (sh_m + w_token[0:1, 0:1], sc_m, gt_m, sh_f, sc_f, gt_f),
        (g_pre_mix, g_post_mix, g_pre_ffn, g_post_ffn),
        w_in_t, w_out_t, fetch_ffn, send_ffn_grads, w_blk_b, b_pool_r, pool_scale_r, conv_w_all, conv_b)

    parts_ffn = _exchange_wait("scatter", *flight[:4], dw_in_t, "ffn_grads_wait")
    dw_in_t, dw_out_t, *parts_ffn = lax.optimization_barrier((dw_in_t, dw_out_t, *parts_ffn))
    parts_mix = _scatter_grads([dw_in_t, dw_out_t])
    big = {
        "w_in": [a.T for a in _sum_adam(parts_mix[0], w_in[0].T, m_w_in[0].T, v_w_in[0].T, "adam_w_in", 64)],
        "w_out": [a.T for a in _sum_adam(parts_mix[1], w_out[0].T, m_w_out[0].T, v_w_out[0].T, "adam_w_out", 128)],
        "w_up": [a.T for a in _sum_adam(parts_ffn[0], w_up[0].T, m_w_up[0].T, v_w_up[0].T, "adam_w_up", 64)],
        "w_down": _sum_adam(parts_ffn[1], w_down[0], m_w_down[0], v_w_down[0], "adam_w_down", 32),
    }

    rep_w = [b_ada, g_pre_mix, g_post_mix, g_pre_ffn, g_post_ffn, w_pool, b_pool, pool_scale, conv_b]
    rep_m = [m_b_ada, m_g_pre_mix, m_g_post_mix, m_g_pre_ffn, m_g_post_ffn, m_w_pool, m_b_pool, m_pool_scale, m_conv_b]
    rep_v = [v_b_ada, v_g_pre_mix, v_g_post_mix, v_g_pre_ffn, v_g_post_ffn, v_w_pool, v_b_pool, v_pool_scale, v_conv_b]
    rep_out, dmod_all, dconv_tot, loss_tot = _small_reduce_adam(
        sums_in, sums_mix, sums_ffn, sums_pool, dw_blk, dconv, loss_loc, rep_w, rep_m, rep_v)
    g_rep, d_rep, nm_rep, nv_rep = (rep_out[k::4] for k in range(4))

    fcol = d_ff // N_DEV
    g_cw = lax.dynamic_slice(dconv_tot, (0, me * fcol), (3, fcol))
    d_cw, nm_cw, nv_cw = _adam(conv_w[0], g_cw, m_conv_w[0], v_conv_w[0], "adam_conv_w", 3)

    dmod_cols = lax.dynamic_slice(dmod_all, (0, me * ncol), (N_DEV, ncol))
    g_ada, d_ada, nm_ada, nv_ada = _ada_grad_adam(c_all, dmod_cols, w_ada[0], m_w_ada[0], v_w_ada[0], 256)

    loss = loss_tot[0, 0]

    def group(k):
        rep = (g_rep, d_rep, nm_rep, nv_rep)[k]
        ada = (g_ada, d_ada, nm_ada, nv_ada)[k][None]
        cw = (g_cw, d_cw, nm_cw, nv_cw)[k][None]
        return [ada, rep[0], rep[1], rep[2], rep[3], rep[4], big["w_in"][k][None], rep[5], rep[6], rep[7],
                big["w_out"][k][None], big["w_up"][k][None], cw, rep[8], big["w_down"][k][None]]

    return (loss, grad_x[None], *group(0), *group(1), *group(2), *group(3))
```

```python
import functools
import math

import jax
import jax.numpy as jnp
from jax import lax
from jax.experimental import pallas as pl
from jax.experimental.pallas import tpu as pltpu

F32 = jnp.float32
BF16 = jnp.bfloat16
MESH = pl.DeviceIdType.MESH

N_DEV = 8
HEAD_DIM = 64
ROT_HALF = 8
ROPE_THETA = 500000.0
POOL_WINDOWS = (2, 4, 8, 16)
DILATIONS = (1, 4, 16)
BLOCK = 128
NORM_EPS = 1e-6
HALO = 16
MASKED = -1e30
ATTN_FWD_UNROLL = 4
ATTN_BWD_UNROLL = 2

ADAM_LR = 0.001
ADAM_B1 = 0.9
ADAM_B2 = 0.999
ADAM_EPS = 1e-08
ADAM_WD = 0.01
ADAM_STEP = 10

V7X_VMEM_LIMIT = 56 * 1024 * 1024
LANES = 128

NT = (((1,), (1,)), ((), ()))
NN = (((1,), (0,)), ((), ()))
TN = (((0,), (0,)), ((), ()))


def _dot(a, b, dims):
    return lax.dot_general(a, b, dims, preferred_element_type=F32)


def _params(sem=None, vmem=V7X_VMEM_LIMIT):
    if sem is None:
        return pltpu.CompilerParams(vmem_limit_bytes=vmem)
    return pltpu.CompilerParams(dimension_semantics=sem, vmem_limit_bytes=vmem)


def _rstd(v):
    return lax.rsqrt(jnp.mean(v * v, axis=-1, keepdims=True) + NORM_EPS)


def _norm_bwd(dn, n, rstd):
    return rstd * (dn - n * jnp.mean(dn * n, axis=-1, keepdims=True))


def _rope_fwd(p, rope_ref):
    return p * rope_ref[0] + pltpu.roll(p, LANES - ROT_HALF, 1) * rope_ref[1] + pltpu.roll(p, ROT_HALF, 1) * rope_ref[2]


def _rope_bwd(dp, rope_ref):
    return dp * rope_ref[0] + pltpu.roll(dp * rope_ref[1], ROT_HALF, 1) + pltpu.roll(dp * rope_ref[2], LANES - ROT_HALF, 1)


def _gelu_parts(v):
    k = math.sqrt(2.0 / math.pi)
    t = jnp.tanh(k * (v + 0.044715 * v * v * v))
    g = 0.5 * v * (1.0 + t)
    dg = 0.5 * (1.0 + t) + 0.5 * v * (1.0 - t * t) * k * (1.0 + 3.0 * 0.044715 * v * v)
    return g, dg


def _halo_before(i, tile):
    return jnp.maximum(i * (tile // HALO) - 1, 0)


def _premix_inproj(x, sh, sc, g, w_in_t, rope, tm):
    s_len, d = x.shape
    n_proj = w_in_t.shape[0]
    n_slab = (n_proj - 256) // LANES

    def body(x_ref, sh_ref, sc_ref, g_ref, w_ref, rope_ref, h_ref, up_ref, qkv_ref):
        xv = x_ref[...]
        h = (xv * _rstd(xv) * g_ref[...]) * (1.0 + sc_ref[...]) + sh_ref[...]
        hb = h.astype(BF16)
        h_ref[...] = hb
        up_ref[...] = _dot(hb, w_ref[0:256, :], NT)
        for pair in range(n_slab // 2):
            p = _dot(hb, w_ref[256 + 256 * pair:512 + 256 * pair, :], NT)
            for half in range(2):
                ph = p[:, half * LANES:(half + 1) * LANES]
                if pair < 6:
                    ph = _rope_fwd(ph, rope_ref)
                if pair < 3:
                    ph = ph * (HEAD_DIM ** -0.5)
                qkv_ref[2 * pair + half] = ph

    vec = pl.BlockSpec((1, d), lambda i: (0, 0))
    return pl.pallas_call(
        body, name="premix_inproj", grid=(s_len // tm,),
        in_specs=[pl.BlockSpec((tm, d), lambda i: (i, 0)), vec, vec, vec,
                  pl.BlockSpec((n_proj, d), lambda i: (0, 0)),
                  pl.BlockSpec((3, tm, LANES), lambda i: (0, i, 0))],
        out_specs=[pl.BlockSpec((tm, d), lambda i: (i, 0)),
                   pl.BlockSpec((tm, 256), lambda i: (i, 0)),
                   pl.BlockSpec((n_slab, tm, LANES), lambda i: (0, i, 0))],
        out_shape=[jax.ShapeDtypeStruct((s_len, d), BF16),
                   jax.ShapeDtypeStruct((s_len, 256), F32),
                   jax.ShapeDtypeStruct((n_slab, s_len, LANES), F32)],
        compiler_params=_params(("arbitrary",)),
    )(x, sh, sc, g, w_in_t, rope)


def _block_rows(n, r, dil):
    start = n * (BLOCK * dil) + r
    if dil == 1:
        return pl.ds(pl.multiple_of(start, BLOCK), BLOCK)
    return pl.ds(start, BLOCK, stride=dil)


def _band_mask(n):
    ri = lax.broadcasted_iota(jnp.int32, (BLOCK, 2 * BLOCK), 0)
    cj = lax.broadcasted_iota(jnp.int32, (BLOCK, 2 * BLOCK), 1)
    cur = (cj >= BLOCK) & (cj - BLOCK <= ri)
    prev = (cj < BLOCK) & (cj >= ri) & (n > 0)
    return cur | prev


def _attn_fwd(qkv, group, dil):
    s_len = qkv.shape[1]
    nb = s_len // (BLOCK * dil)

    def body(q_ref, k_ref, v_ref, o_ref, lse_ref):
        lane = lax.broadcasted_iota(jnp.int32, (BLOCK, LANES), 1)
        first = lane < HEAD_DIM

        def block(t, carry):
            r, n = t // nb, t % nb
            cur = _block_rows(n, r, dil)
            prev = _block_rows(jnp.maximum(n - 1, 0), r, dil)
            q = q_ref[0, cur, :]
            kcat = jnp.concatenate([k_ref[0, prev, :], k_ref[0, cur, :]], axis=0).astype(BF16)
            vcat = jnp.concatenate([v_ref[0, prev, :], v_ref[0, cur, :]], axis=0).astype(BF16)
            valid = _band_mask(n)
            q2 = jnp.concatenate([jnp.where(first, q, 0.0), jnp.where(first, 0.0, q)], axis=0).astype(BF16)
            s = jnp.where(jnp.concatenate([valid, valid], axis=0), _dot(q2, kcat, NT), MASKED)
            m = jnp.max(s, axis=-1, keepdims=True)
            p = jnp.exp(s - m)
            den = jnp.sum(p, axis=-1, keepdims=True)
            o2 = _dot(p.astype(BF16), vcat, NN) / den
            lse2 = m + jnp.log(den)
            o_ref[0, cur, :] = jnp.where(first, o2[:BLOCK], o2[BLOCK:])
            lse_ref[0, cur, :] = jnp.where(first, lse2[:BLOCK], lse2[BLOCK:])
            return carry

        lax.fori_loop(0, nb * dil, block, 0, unroll=ATTN_FWD_UNROLL)

    def slab(base):
        return pl.BlockSpec((1, s_len, LANES), lambda s: (base + 2 * group + s, 0, 0))

    out = pl.BlockSpec((1, s_len, LANES), lambda s: (s, 0, 0))
    shape = jax.ShapeDtypeStruct((2, s_len, LANES), F32)
    return pl.pallas_call(
        body, name=f"attn_fwd_d{dil}", grid=(2,),
        in_specs=[slab(0), slab(6), slab(12)], out_specs=[out, out], out_shape=[shape, shape],
        compiler_params=_params(("arbitrary",)),
    )(qkv, qkv, qkv)


def _pool_mixed(u, halo, i, tm):
    ue = jnp.concatenate([halo, u], axis=0)
    s2 = ue + pltpu.roll(ue, 1, 0)
    s4 = s2 + pltpu.roll(s2, 2, 0)
    s8 = s4 + pltpu.roll(s4, 4, 0)
    s16 = s8 + pltpu.roll(s8, 8, 0)
    grp = lax.broadcasted_iota(jnp.int32, (tm, 256), 1) // HEAD_DIM
    pick = lambda a, b, c, e: jnp.where(grp == 0, a, jnp.where(grp == 1, b, jnp.where(grp == 2, c, e)))
    win_sum = pick(s2[HALO:], s4[HALO:], s8[HALO:], s16[HALO:])
    pos = (i * tm + lax.broadcasted_iota(jnp.int32, (tm, 256), 0)).astype(F32)
    count = jnp.minimum(pos + 1.0, pick(*[float(w) for w in POOL_WINDOWS]))
    return win_sum / count - u, count


def _mix_out(x, u_pool, o_g, lse_g, w_blk, b_pool, pool_scale, w_out_t, gt_m, g_post_mix, g_pre_ffn, sc_f, sh_f, tm):
    s_len, d = x.shape

    def body(x_ref, u_ref, uh_ref, o0, o1, o2, l0, l1, l2, wb_ref, bp_ref, ps_ref, wo_ref,
             gt_ref, g1_ref, g2_ref, sc_ref, sh_ref,
             x1_ref, y1_ref, h2_ref, cat_ref, attn_ref, lall_ref):
        i = pl.program_id(0)
        u = u_ref[...]
        halo = uh_ref[...] * (i > 0).astype(F32)
        mixed, _ = _pool_mixed(u, halo, i, tm)
        y = _dot(mixed.astype(BF16), wb_ref[...], NN) + bp_ref[...]
        pool = y * ps_ref[...]
        attn = []
        for s in range(2):
            la, lb, lc = l0[s], l1[s], l2[s]
            mx = jnp.maximum(jnp.maximum(la, lb), lc)
            ea, eb, ec = jnp.exp(la - mx), jnp.exp(lb - mx), jnp.exp(lc - mx)
            den = ea + eb + ec
            lall_ref[s] = mx + jnp.log(den)
            attn.append((ea / den) * o0[s] + (eb / den) * o1[s] + (ec / den) * o2[s])
        attn = jnp.concatenate(attn, axis=1)
        attn_ref[...] = attn
        cat = jnp.concatenate([pool, attn], axis=1).astype(BF16)
        cat_ref[...] = cat
        y1 = _dot(cat, wo_ref[...], NT)
        y1_ref[...] = y1
        x1 = x_ref[...] + gt_ref[...] * (y1 * _rstd(y1) * g1_ref[...])
        x1_ref[...] = x1
        h2 = (x1 * _rstd(x1) * g2_ref[...]) * (1.0 + sc_ref[...]) + sh_ref[...]
        h2_ref[...] = h2.astype(BF16)

    tile = lambda w: pl.BlockSpec((tm, w), lambda i: (i, 0))
    slab = pl.BlockSpec((2, tm, LANES), lambda i: (0, i, 0))
    const = lambda a: pl.BlockSpec(a.shape, lambda i: (0,) * a.ndim)
    return pl.pallas_call(
        body, name="mix_out", grid=(s_len // tm,),
        in_specs=[tile(d), tile(256), pl.BlockSpec((HALO, 256), lambda i: (_halo_before(i, tm), 0)),
                  slab, slab, slab, slab, slab, slab,
                  const(w_blk), const(b_pool), const(pool_scale), const(w_out_t),
                  const(gt_m), const(g_post_mix), const(g_pre_ffn), const(sc_f), const(sh_f)],
        out_specs=[tile(d), tile(d), tile(d), tile(512), tile(256), slab],
        out_shape=[jax.ShapeDtypeStruct((s_len, d), F32), jax.ShapeDtypeStruct((s_len, d), F32),
                   jax.ShapeDtypeStruct((s_len, d), BF16), jax.ShapeDtypeStruct((s_len, 512), BF16),
                   jax.ShapeDtypeStruct((s_len, 256), F32), jax.ShapeDtypeStruct((2, s_len, LANES), F32)],
        compiler_params=_params(("arbitrary",)),
    )(x, u_pool, u_pool, *o_g, *lse_g, w_blk, b_pool, pool_scale, w_out_t, gt_m, g_post_mix, g_pre_ffn, sc_f, sh_f)


def _conv_gate(gate_ext, cw, cb):
    gc = gate_ext * cw[2:3, :] + pltpu.roll(gate_ext, 1, 0) * cw[1:2, :] + pltpu.roll(gate_ext, 2, 0) * cw[0:1, :]
    return gc[HALO:] + cb


def _ffn_fwd_loss(h2, x1, target, w_up_t, w_down, conv_w, conv_b, gt_f, g_post_ffn, tm, tf, ck):
    s_len, d = x1.shape
    d_ff = w_down.shape[0]
    n_f = d_ff // tf

    def body(h_ref, hh_ref, x1_ref, tgt_ref, wg_ref, wv_ref, wd_ref, cw_ref, cb_ref, gt_ref, g_ref,
             gate_ref, val_ref, dy2_ref, dout_ref, sums_ref, loss_ref, acc_ref):
        i, j = pl.program_id(0), pl.program_id(1)

        @pl.when((i == 0) & (j == 0))
        def _():
            sums_ref[...] = jnp.zeros_like(sums_ref)
            loss_ref[...] = jnp.zeros_like(loss_ref)

        h = h_ref[...]
        h_ext = jnp.concatenate([hh_ref[...], h], axis=0)
        row = lax.broadcasted_iota(jnp.int32, (tm + HALO, ck), 0)
        no_halo = (row < HALO) & (i == 0)
        part = None
        for c in range(tf // ck):
            cs = slice(c * ck, (c + 1) * ck)
            gate_ext = jnp.where(no_halo, 0.0, _dot(h_ext, wg_ref[cs, :], NT))
            val = _dot(h, wv_ref[cs, :], NT)
            act, _ = _gelu_parts(_conv_gate(gate_ext, cw_ref[:, cs], cb_ref[:, cs]))
            gate_ref[:, cs] = gate_ext[HALO:].astype(BF16)
            val_ref[:, cs] = val.astype(BF16)
            p = _dot((act * val).astype(BF16), wd_ref[cs, :], NN)
            part = p if part is None else part + p

        @pl.when(j == 0)
        def _():
            acc_ref[...] = part

        @pl.when(j > 0)
        def _():
            acc_ref[...] += part

        @pl.when(j == n_f - 1)
        def _():
            y2 = acc_ref[...]
            rstd = _rstd(y2)
            n = y2 * rstd
            rn = n * g_ref[...]
            err = x1_ref[...] + gt_ref[...] * rn - tgt_ref[...]
            loss_ref[...] += 0.5 * jnp.sum(jnp.mean(err * err, axis=-1, keepdims=True), axis=0, keepdims=True)
            dout = err * (1.0 / d)
            dout_ref[...] = dout
            drn = dout * gt_ref[...]
            sums_ref[0:1, :] += jnp.sum(dout * rn, axis=0, keepdims=True)
            sums_ref[1:2, :] += jnp.sum(drn * n, axis=0, keepdims=True)
            dy2_ref[...] = _norm_bwd(drn * g_ref[...], n, rstd).astype(BF16)

    tok = lambda w: pl.BlockSpec((tm, w), lambda i, j: (i, 0))
    tokf = pl.BlockSpec((tm, tf), lambda i, j: (i, j))
    vec = pl.BlockSpec((1, d), lambda i, j: (0, 0))
    once = {"pipeline_mode": pl.Buffered(1)} if n_f == 1 else {}
    return pl.pallas_call(
        body, name="ffn_fwd_loss", grid=(s_len // tm, n_f),
        in_specs=[tok(d), pl.BlockSpec((HALO, d), lambda i, j: (_halo_before(i, tm), 0)), tok(d), tok(d),
                  pl.BlockSpec((tf, d), lambda i, j: (j, 0), **once),
                  pl.BlockSpec((tf, d), lambda i, j: (j + n_f, 0), **once),
                  pl.BlockSpec((tf, d), lambda i, j: (j, 0), **once),
                  pl.BlockSpec((3, tf), lambda i, j: (0, j)), pl.BlockSpec((1, tf), lambda i, j: (0, j)), vec, vec],
        out_specs=[tokf, tokf, tok(d), tok(d), pl.BlockSpec((8, d), lambda i, j: (0, 0)),
                   pl.BlockSpec((8, LANES), lambda i, j: (0, 0))],
        out_shape=[jax.ShapeDtypeStruct((s_len, d_ff), BF16), jax.ShapeDtypeStruct((s_len, d_ff), BF16),
                   jax.ShapeDtypeStruct((s_len, d), BF16), jax.ShapeDtypeStruct((s_len, d), F32),
                   jax.ShapeDtypeStruct((8, d), F32), jax.ShapeDtypeStruct((8, LANES), F32)],
        scratch_shapes=[pltpu.VMEM((tm, d), F32)],
        compiler_params=_params(("arbitrary", "arbitrary")),
    )(h2, h2, x1, target, w_up_t, w_up_t, w_down, conv_w, conv_b, gt_f, g_post_ffn)


def _ffn_bwd_act(dy2, gate, val, w_down, conv_w, conv_b, tm, tf):
    s_len, d = dy2.shape
    d_ff = w_down.shape[0]
    n_t = s_len // tm

    def body(dy_ref, g_ref, gh_ref, v_ref, wd_ref, cw_ref, cb_ref, dgc_ref, dval_ref, dwd_ref, dconv_ref, acc_ref):
        i = pl.program_id(1)
        gate_ext = jnp.concatenate([gh_ref[...], g_ref[...]], axis=0).astype(F32)
        row = lax.broadcasted_iota(jnp.int32, gate_ext.shape, 0)
        gate_ext = jnp.where((row < HALO) & (i == 0), 0.0, gate_ext)
        act, dact = _gelu_parts(_conv_gate(gate_ext, cw_ref[...], cb_ref[...]))
        v = v_ref[...].astype(F32)
        da = _dot(dy_ref[...], wd_ref[...], NT)
        dgc = da * v * dact
        dgc_ref[...] = dgc.astype(BF16)
        dval_ref[...] = (da * act).astype(BF16)
        dwd = _dot((act * v).astype(BF16), dy_ref[...], TN)
        rows = [jnp.sum(dgc * pltpu.roll(gate_ext, 2 - k, 0)[HALO:], axis=0, keepdims=True) for k in range(2)]
        rows += [jnp.sum(dgc * gate_ext[HALO:], axis=0, keepdims=True), jnp.sum(dgc, axis=0, keepdims=True),
                 jnp.zeros((4, tf), F32)]
        dconv = jnp.concatenate(rows, axis=0)

        @pl.when(i == 0)
        def _():
            acc_ref[...] = dwd
            dconv_ref[...] = dconv

        @pl.when(i > 0)
        def _():
            acc_ref[...] += dwd
            dconv_ref[...] += dconv

        @pl.when(i == n_t - 1)
        def _():
            dwd_ref[...] = acc_ref[...].astype(BF16)

    tokf = pl.BlockSpec((tm, tf), lambda j, i: (i, j))
    return pl.pallas_call(
        body, name="ffn_bwd_act", grid=(d_ff // tf, n_t),
        in_specs=[pl.BlockSpec((tm, d), lambda j, i: (i, 0)), tokf,
                  pl.BlockSpec((HALO, tf), lambda j, i: (_halo_before(i, tm), j)), tokf,
                  pl.BlockSpec((tf, d), lambda j, i: (j, 0)),
                  pl.BlockSpec((3, tf), lambda j, i: (0, j)), pl.BlockSpec((1, tf), lambda j, i: (0, j))],
        out_specs=[tokf, tokf, pl.BlockSpec((tf, d), lambda j, i: (j, 0)), pl.BlockSpec((8, tf), lambda j, i: (0, j))],
        out_shape=[jax.ShapeDtypeStruct((s_len, d_ff), BF16), jax.ShapeDtypeStruct((s_len, d_ff), BF16),
                   jax.ShapeDtypeStruct((d_ff, d), BF16), jax.ShapeDtypeStruct((8, d_ff), F32)],
        scratch_shapes=[pltpu.VMEM((tf, d), F32)],
        compiler_params=_params(("arbitrary", "arbitrary")),
    )(dy2, gate, gate, val, w_down, conv_w, conv_b)


def _ffn_bwd_up(dgc, dval, w_up_t, conv_w, tm):
    s_len, d_ff = dgc.shape
    d = w_up_t.shape[1]
    n_t = s_len // tm

    def body(dg_ref, dgn_ref, dv_ref, cw_ref, w_ref, dup_ref, dh_ref):
        i = pl.program_id(0)
        nxt = dgn_ref[...].astype(F32) * (i < n_t - 1).astype(F32)
        ext = jnp.concatenate([dg_ref[...].astype(F32), nxt], axis=0)
        rows = tm + HALO
        dgate = (ext * cw_ref[2:3, :] + pltpu.roll(ext, rows - 1, 0) * cw_ref[1:2, :]
                 + pltpu.roll(ext, rows - 2, 0) * cw_ref[0:1, :])[:tm]
        dup = jnp.concatenate([dgate.astype(BF16), dv_ref[...]], axis=1)
        dup_ref[...] = dup
        dh_ref[...] = _dot(dup, w_ref[...], NN)

    tokf = pl.BlockSpec((tm, d_ff), lambda i: (i, 0))
    return pl.pallas_call(
        body, name="ffn_bwd_up", grid=(n_t,),
        in_specs=[tokf, pl.BlockSpec((HALO, d_ff), lambda i: (jnp.minimum((i + 1) * (tm // HALO), s_len // HALO - 1), 0)),
                  tokf, pl.BlockSpec((3, d_ff), lambda i: (0, 0)), pl.BlockSpec((2 * d_ff, d), lambda i: (0, 0))],
        out_specs=[pl.BlockSpec((tm, 2 * d_ff), lambda i: (i, 0)), pl.BlockSpec((tm, d), lambda i: (i, 0))],
        out_shape=[jax.ShapeDtypeStruct((s_len, 2 * d_ff), BF16), jax.ShapeDtypeStruct((s_len, d), F32)],
        compiler_params=_params(("arbitrary",)),
    )(dgc, dgc, dval, conv_w, w_up_t)


def _mix_bwd(dh2, dout, x1, y1, cat, attn, w_out_t, sc_f, g_pre_ffn, gt_m, g_post_mix, tm):
    s_len, d = x1.shape
    n_t = s_len // tm

    def body(dh_ref, do_ref, x1_ref, y1_ref, cat_ref, at_ref, wo_ref, sc_ref, g2_ref, gt_ref, g1_ref,
             dx1_ref, dpool_ref, dattn_ref, delta_ref, dwo_ref, sums_ref, acc_ref):
        i = pl.program_id(0)
        dh = dh_ref[...]
        x1 = x1_ref[...]
        r2 = _rstd(x1)
        n2 = x1 * r2
        ng = n2 * g2_ref[...]
        dng = dh * (1.0 + sc_ref[...])
        dx1 = do_ref[...] + _norm_bwd(dng * g2_ref[...], n2, r2)
        dx1_ref[...] = dx1
        y1 = y1_ref[...]
        r1 = _rstd(y1)
        n1 = y1 * r1
        drn = dx1 * gt_ref[...]
        dy1 = _norm_bwd(drn * g1_ref[...], n1, r1).astype(BF16)
        dcat = _dot(dy1, wo_ref[...], NN)
        dpool_ref[...] = dcat[:, 0:256]
        lane = lax.broadcasted_iota(jnp.int32, (tm, LANES), 1)
        first = lane < HEAD_DIM
        for s in range(2):
            da = dcat[:, 256 + s * LANES:256 + (s + 1) * LANES]
            dattn_ref[s] = da
            prod = da * at_ref[:, s * LANES:(s + 1) * LANES]
            tot = jnp.sum(prod, axis=-1, keepdims=True)
            lo = jnp.sum(jnp.where(first, prod, 0.0), axis=-1, keepdims=True)
            delta_ref[s] = jnp.where(first, lo, tot - lo)
        dwo = _dot(dy1, cat_ref[...], TN)
        sums = jnp.concatenate(
            [jnp.sum(dh, axis=0, keepdims=True), jnp.sum(dh * ng, axis=0, keepdims=True),
             jnp.sum(dng * n2, axis=0, keepdims=True), jnp.sum(dx1 * (n1 * g1_ref[...]), axis=0, keepdims=True),
             jnp.sum(drn * n1, axis=0, keepdims=True), jnp.zeros((3, d), F32)], axis=0)

        @pl.when(i == 0)
        def _():
            acc_ref[...] = dwo
            sums_ref[...] = sums

        @pl.when(i > 0)
        def _():
            acc_ref[...] += dwo
            sums_ref[...] += sums

        @pl.when(i == n_t - 1)
        def _():
            dwo_ref[...] = acc_ref[...].astype(BF16)

    tile = lambda w: pl.BlockSpec((tm, w), lambda i: (i, 0))
    slab = pl.BlockSpec((2, tm, LANES), lambda i: (0, i, 0))
    vec = pl.BlockSpec((1, d), lambda i: (0, 0))
    return pl.pallas_call(
        body, name="mix_bwd", grid=(n_t,),
        in_specs=[tile(d), tile(d), tile(d), tile(d), tile(512), tile(256),
                  pl.BlockSpec((d, 512), lambda i: (0, 0)), vec, vec, vec, vec],
        out_specs=[tile(d), tile(256), slab, slab, pl.BlockSpec((d, 512), lambda i: (0, 0)),
                   pl.BlockSpec((8, d), lambda i: (0, 0))],
        out_shape=[jax.ShapeDtypeStruct((s_len, d), F32), jax.ShapeDtypeStruct((s_len, 256), F32),
                   jax.ShapeDtypeStruct((2, s_len, LANES), F32), jax.ShapeDtypeStruct((2, s_len, LANES), F32),
                   jax.ShapeDtypeStruct((d, 512), BF16), jax.ShapeDtypeStruct((8, d), F32)],
        scratch_shapes=[pltpu.VMEM((d, 512), F32)],
        compiler_params=_params(("arbitrary",)),
    )(dh2, dout, x1, y1, cat, attn, w_out_t, sc_f, g_pre_ffn, gt_m, g_post_mix)


def _pool_bwd(dpool, u_pool, w_blk, b_pool, pool_scale, tm):
    s_len = dpool.shape[0]
    n_t = s_len // tm

    def body(dp_ref, dpn_ref, u_ref, uh_ref, wb_ref, bp_ref, ps_ref, du_ref, dwb_ref, sums_ref):
        i = pl.program_id(0)
        u = u_ref[...]
        mixed, _ = _pool_mixed(u, uh_ref[...] * (i > 0).astype(F32), i, tm)
        mixed_b = mixed.astype(BF16)
        y = _dot(mixed_b, wb_ref[...], NN) + bp_ref[...]
        dp = dp_ref[...]
        dy = dp * ps_ref[...]
        dwb = _dot(mixed_b, dy.astype(BF16), TN)
        sums = jnp.concatenate([jnp.sum(dy, axis=0, keepdims=True), jnp.sum(dp * y, axis=0, keepdims=True),
                                jnp.zeros((6, 256), F32)], axis=0)
        dp_ext = jnp.concatenate([dp, dpn_ref[...] * (i < n_t - 1).astype(F32)], axis=0)
        dmix = _dot((dp_ext * ps_ref[...]).astype(BF16), wb_ref[...], NT)
        rows = tm + HALO
        grp = lax.broadcasted_iota(jnp.int32, (rows, 256), 1) // HEAD_DIM
        pick = lambda a, b, c, e: jnp.where(grp == 0, a, jnp.where(grp == 1, b, jnp.where(grp == 2, c, e)))
        pos = (i * tm + lax.broadcasted_iota(jnp.int32, (rows, 256), 0)).astype(F32)
        z = dmix / jnp.minimum(pos + 1.0, pick(*[float(w) for w in POOL_WINDOWS]))
        f2 = z + pltpu.roll(z, rows - 1, 0)
        f4 = f2 + pltpu.roll(f2, rows - 2, 0)
        f8 = f4 + pltpu.roll(f4, rows - 4, 0)
        f16 = f8 + pltpu.roll(f8, rows - 8, 0)
        du_ref[...] = (pick(f2, f4, f8, f16) - dmix)[:tm]

        @pl.when(i == 0)
        def _():
            dwb_ref[...] = dwb
            sums_ref[...] = sums

        @pl.when(i > 0)
        def _():
            dwb_ref[...] += dwb
            sums_ref[...] += sums

    tile = pl.BlockSpec((tm, 256), lambda i: (i, 0))
    const = lambda a: pl.BlockSpec(a.shape, lambda i: (0,) * a.ndim)
    return pl.pallas_call(
        body, name="pool_bwd", grid=(n_t,),
        in_specs=[tile, pl.BlockSpec((HALO, 256), lambda i: (jnp.minimum((i + 1) * (tm // HALO), s_len // HALO - 1), 0)),
                  tile, pl.BlockSpec((HALO, 256), lambda i: (_halo_before(i, tm), 0)),
                  const(w_blk), const(b_pool), const(pool_scale)],
        out_specs=[tile, pl.BlockSpec((256, 256), lambda i: (0, 0)), pl.BlockSpec((8, 256), lambda i: (0, 0))],
        out_shape=[jax.ShapeDtypeStruct((s_len, 256), F32), jax.ShapeDtypeStruct((256, 256), F32),
                   jax.ShapeDtypeStruct((8, 256), F32)],
        compiler_params=_params(("arbitrary",)),
    )(dpool, dpool, u_pool, u_pool, w_blk, b_pool, pool_scale)


def _attn_bwd(qkv, dattn, lse_all, delta, group, dil):
    s_len = qkv.shape[1]
    nb = s_len // (BLOCK * dil)

    def body(q_ref, k_ref, v_ref, do_ref, l_ref, dl_ref, dq_ref, dk_ref, dv_ref):
        lane = lax.broadcasted_iota(jnp.int32, (BLOCK, LANES), 1)
        first = lane < HEAD_DIM

        def block(t, carry):
            dk_part, dv_part = carry
            r, n = t // nb, t % nb
            cur = _block_rows(n, r, dil)
            prev = _block_rows(jnp.maximum(n - 1, 0), r, dil)
            q = q_ref[0, cur, :]
            do = do_ref[0, cur, :]
            lse = l_ref[0, cur, :]
            dlt = dl_ref[0, cur, :]
            kcat = jnp.concatenate([k_ref[0, prev, :], k_ref[0, cur, :]], axis=0).astype(BF16)
            vcat = jnp.concatenate([v_ref[0, prev, :], v_ref[0, cur, :]], axis=0).astype(BF16)
            valid = _band_mask(n)
            stack = lambda a: jnp.concatenate([jnp.where(first, a, 0.0), jnp.where(first, 0.0, a)], axis=0)
            rows2 = lambda a: jnp.concatenate([a[:, 0:1], a[:, HEAD_DIM:HEAD_DIM + 1]], axis=0)
            q2, do2 = stack(q).astype(BF16), stack(do).astype(BF16)
            valid2 = jnp.concatenate([valid, valid], axis=0)
            p = jnp.where(valid2, jnp.exp(_dot(q2, kcat, NT) - rows2(lse)), 0.0)
            ds = (p * (_dot(do2, vcat, NT) - rows2(dlt))).astype(BF16)
            dq2 = _dot(ds, kcat, NN)
            dq_ref[0, cur, :] = jnp.where(first, dq2[:BLOCK], dq2[BLOCK:])
            dkc = _dot(ds, q2, TN)
            dvc = _dot(p.astype(BF16), do2, TN)
            dk_ref[0, prev, :] = dk_part + dkc[:BLOCK]
            dv_ref[0, prev, :] = dv_part + dvc[:BLOCK]
            dk_ref[0, cur, :] = dkc[BLOCK:]
            dv_ref[0, cur, :] = dvc[BLOCK:]
            return dkc[BLOCK:], dvc[BLOCK:]

        def blocks(tt, carry):
            for u in range(ATTN_BWD_UNROLL):
                carry = block(tt * ATTN_BWD_UNROLL + u, carry)
            return carry

        zero = jnp.zeros((BLOCK, LANES), F32)
        lax.fori_loop(0, nb * dil // ATTN_BWD_UNROLL, blocks, (zero, zero))

    def slab(base):
        return pl.BlockSpec((1, s_len, LANES), lambda s: (base + 2 * group + s, 0, 0))

    one = pl.BlockSpec((1, s_len, LANES), lambda s: (s, 0, 0))
    shape = jax.ShapeDtypeStruct((2, s_len, LANES), F32)
    return pl.pallas_call(
        body, name=f"attn_bwd_d{dil}", grid=(2,),
        in_specs=[slab(0), slab(6), slab(12), one, one, one],
        out_specs=[one, one, one], out_shape=[shape, shape, shape],
        compiler_params=_params(("arbitrary",)),
    )(qkv, qkv, qkv, dattn, lse_all, delta)


def _inproj_bwd(du, dqkv, rope, w_in_t, x, dx1, sc_m, g_pre_mix, tm):
    s_len, d = x.shape
    n_proj = w_in_t.shape[0]
    n_t = s_len // tm

    def body(du_ref, *refs):
        dref = refs[:9]
        rope_ref, w_ref, x_ref, dx1_ref, sc_ref, g_ref, dproj_ref, dx_ref, sums_ref = refs[9:]
        i = pl.program_id(0)
        cols = [du_ref[...].astype(BF16)]
        for kind in range(3):
            for grp in range(3):
                for s in range(2):
                    piece = dref[3 * grp + kind][s]
                    if kind < 2:
                        piece = _rope_bwd(piece, rope_ref)
                    if kind == 0:
                        piece = piece * (HEAD_DIM ** -0.5)
                    cols.append(piece.astype(BF16))
        dproj = jnp.concatenate(cols, axis=1)
        dproj_ref[...] = dproj
        dh = _dot(dproj, w_ref[...], NN)
        xv = x_ref[...]
        r = _rstd(xv)
        n = xv * r
        dng = dh * (1.0 + sc_ref[...])
        dx_ref[...] = dx1_ref[...] + _norm_bwd(dng * g_ref[...], n, r)
        sums = jnp.concatenate([jnp.sum(dh, axis=0, keepdims=True), jnp.sum(dh * (n * g_ref[...]), axis=0, keepdims=True),
                                jnp.sum(dng * n, axis=0, keepdims=True), jnp.zeros((5, d), F32)], axis=0)

        @pl.when(i == 0)
        def _():
            sums_ref[...] = sums

        @pl.when(i > 0)
        def _():
            sums_ref[...] += sums

    tile = lambda w: pl.BlockSpec((tm, w), lambda i: (i, 0))
    slab = pl.BlockSpec((2, tm, LANES), lambda i: (0, i, 0))
    vec = pl.BlockSpec((1, d), lambda i: (0, 0))
    return pl.pallas_call(
        body, name="inproj_bwd", grid=(n_t,),
        in_specs=[tile(256)] + [slab] * 9 + [pl.BlockSpec((3, tm, LANES), lambda i: (0, i, 0)),
                                             pl.BlockSpec((n_proj, d), lambda i: (0, 0)), tile(d), tile(d), vec, vec],
        out_specs=[tile(n_proj), tile(d), pl.BlockSpec((8, d), lambda i: (0, 0))],
        out_shape=[jax.ShapeDtypeStruct((s_len, n_proj), BF16), jax.ShapeDtypeStruct((s_len, d), F32),
                   jax.ShapeDtypeStruct((8, d), F32)],
        compiler_params=_params(("arbitrary",)),
    )(du, *dqkv, rope, w_in_t, x, dx1, sc_m, g_pre_mix)


def _wgrad(a, b, name, tk, tmm):
    s_len, m = a.shape
    n = b.shape[1]
    n_k = s_len // tk

    def body(a_ref, b_ref, o_ref, acc_ref):
        k = pl.program_id(1)
        part = _dot(a_ref[...], b_ref[...], TN)

        @pl.when(k == 0)
        def _():
            acc_ref[...] = part

        @pl.when(k > 0)
        def _():
            acc_ref[...] += part

        @pl.when(k == n_k - 1)
        def _():
            o_ref[...] = acc_ref[...].astype(BF16)

    return pl.pallas_call(
        body, name=name, grid=(m // tmm, n_k),
        in_specs=[pl.BlockSpec((tk, tmm), lambda j, k: (k, j)), pl.BlockSpec((tk, n), lambda j, k: (k, 0))],
        out_specs=pl.BlockSpec((tmm, n), lambda j, k: (j, 0)),
        out_shape=jax.ShapeDtypeStruct((m, n), BF16),
        scratch_shapes=[pltpu.VMEM((tmm, n), F32)],
        compiler_params=_params(("arbitrary", "arbitrary")),
    )(a, b)


def _place():
    return lax.axis_index("x"), lax.axis_index("y"), lax.axis_index("c")


def _peer(k):
    x, y, c = _place()
    bx, by, bc = (k >> 2) & 1, (k >> 1) & 1, k & 1
    return (x ^ bx if bx else x, y ^ by if by else y, c ^ bc if bc else c)


def _index(pos):
    return 4 * pos[0] + 2 * pos[1] + pos[2]


def _ada_exchange(c_rows, w_ada, b_ada_cols, taps):
    d = c_rows.shape[1]
    ncol = w_ada.shape[1]

    def body(c_ref, w_ref, b_ref, t_ref, call_ref, mod_ref, tall_ref, stage_ref, send_sems, recv_sems):
        me = _index(_place())
        call_ref[me] = c_ref[...]
        tall_ref[me] = t_ref[...]

        def gather(k):
            return pltpu.make_async_remote_copy(
                src_ref=c_ref, dst_ref=call_ref.at[me], send_sem=send_sems.at[0, k - 1], recv_sem=recv_sems.at[0, k - 1],
                device_id=_peer(k), device_id_type=MESH)

        def gather_taps(k):
            return pltpu.make_async_remote_copy(
                src_ref=t_ref, dst_ref=tall_ref.at[me], send_sem=send_sems.at[2, k - 1], recv_sem=recv_sems.at[2, k - 1],
                device_id=_peer(k), device_id_type=MESH)

        for k in range(1, N_DEV):
            gather(k).start()
        for k in range(1, N_DEV):
            gather_taps(k).start()
        for k in range(1, N_DEV):
            gather(k).wait_recv()
        cv = jnp.concatenate([call_ref[b, 0:1, :] for b in range(N_DEV)], axis=0)
        act = cv * jax.nn.sigmoid(cv)
        mod = lax.dot_general(act, w_ref[...], NN, preferred_element_type=F32,
                              precision=lax.Precision.HIGHEST) + b_ref[...]
        for b in range(N_DEV):
            stage_ref[b] = jnp.broadcast_to(mod[b:b + 1, :], (8, ncol))
        mod_ref[me] = stage_ref[me]

        def scatter(k):
            return pltpu.make_async_remote_copy(
                src_ref=stage_ref.at[_index(_peer(k))], dst_ref=mod_ref.at[me],
                send_sem=send_sems.at[1, k - 1], recv_sem=recv_sems.at[1, k - 1],
                device_id=_peer(k), device_id_type=MESH)

        for k in range(1, N_DEV):
            scatter(k).start()
        for k in range(1, N_DEV):
            scatter(k).wait_recv()
        for k in range(1, N_DEV):
            gather_taps(k).wait_recv()
        for k in range(1, N_DEV):
            gather(k).wait_send()
            scatter(k).wait_send()
            gather_taps(k).wait_send()

    vmem = pl.BlockSpec(memory_space=pltpu.VMEM)
    return pl.pallas_call(
        body, name="ada_exchange",
        in_specs=[vmem] * 4, out_specs=[vmem] * 3,
        out_shape=[jax.ShapeDtypeStruct((N_DEV, 8, d), F32), jax.ShapeDtypeStruct((N_DEV, 8, ncol), F32),
                   jax.ShapeDtypeStruct((N_DEV,) + taps.shape, F32)],
        scratch_shapes=[pltpu.VMEM((N_DEV, 8, ncol), F32), pltpu.SemaphoreType.DMA((3, N_DEV - 1)),
                        pltpu.SemaphoreType.DMA((3, N_DEV - 1))],
        compiler_params=_params(),
    )(c_rows, w_ada, b_ada_cols, taps)


def _gather_weights(shards):
    n_w = len(shards)

    def body(*refs):
        srcs, outs = refs[:n_w], refs[n_w:2 * n_w]
        send_sems, recv_sems, local_sems = refs[2 * n_w:]
        x, y, c = _place()
        me, sibling = (x, y, c), (x, y, 1 - c)
        chips = [(1 - x, y), (x, 1 - y), (1 - x, 1 - y)]

        def rows(w, pos):
            r = shards[w].shape[0]
            return outs[w].at[pl.ds(pl.multiple_of(_index(pos) * r, 16), r), :]

        def copy(k, w, block, to, own=False):
            return pltpu.make_async_remote_copy(
                src_ref=srcs[w] if own else rows(w, block), dst_ref=rows(w, block),
                send_sem=send_sems.at[k, w], recv_sem=recv_sems.at[k, w], device_id=to, device_id_type=MESH)

        mine = [pltpu.make_async_copy(srcs[w], rows(w, me), local_sems.at[w]) for w in range(n_w)]
        for cp in mine:
            cp.start()
        first = [copy(0, w, me, sibling, own=True) for w in range(n_w)]
        first += [copy(1 + j, w, me, (*chip, c), own=True) for j, chip in enumerate(chips) for w in range(n_w)]
        for cp in first:
            cp.start()
        passed = []
        for j, chip in enumerate(chips):
            for w in range(n_w):
                copy(1 + j, w, (*chip, c), me).wait_recv()
                fwd = copy(4 + j, w, (*chip, c), sibling)
                fwd.start()
                passed.append(fwd)
        for w in range(n_w):
            copy(0, w, sibling, me).wait_recv()
        for j, chip in enumerate(chips):
            for w in range(n_w):
                copy(4 + j, w, (*chip, 1 - c), me).wait_recv()
        for cp in first + passed:
            cp.wait_send()
        for cp in mine:
            cp.wait()

    hbm = pl.BlockSpec(memory_space=pltpu.HBM)
    return pl.pallas_call(
        body, name="gather_weights",
        in_specs=[hbm] * n_w, out_specs=[hbm] * n_w,
        out_shape=[jax.ShapeDtypeStruct((N_DEV * s.shape[0], s.shape[1]), s.dtype) for s in shards],
        scratch_shapes=[pltpu.SemaphoreType.DMA((N_DEV - 1, n_w)), pltpu.SemaphoreType.DMA((N_DEV - 1, n_w)),
                        pltpu.SemaphoreType.DMA((n_w,))],
        compiler_params=_params(),
    )(*shards)


def _scatter_grads(grads):
    n_w = len(grads)

    def body(*refs):
        srcs, outs = refs[:n_w], refs[n_w:2 * n_w]
        send_sems, recv_sems, local_sems = refs[2 * n_w:]
        me = _index(_place())

        def slab(w, dev):
            r = grads[w].shape[0] // N_DEV
            return srcs[w].at[pl.ds(pl.multiple_of(dev * r, 16), r), :]

        def copy(k, w):
            return pltpu.make_async_remote_copy(
                src_ref=slab(w, _index(_peer(k))), dst_ref=outs[w].at[me],
                send_sem=send_sems.at[k - 1, w], recv_sem=recv_sems.at[k - 1, w],
                device_id=_peer(k), device_id_type=MESH)

        mine = [pltpu.make_async_copy(slab(w, me), outs[w].at[me], local_sems.at[w]) for w in range(n_w)]
        for cp in mine:
            cp.start()
        sends = [copy(k, w) for k in range(1, N_DEV) for w in range(n_w)]
        for cp in sends:
            cp.start()
        for cp in sends:
            cp.wait_recv()
        for cp in sends:
            cp.wait_send()
        for cp in mine:
            cp.wait()

    hbm = pl.BlockSpec(memory_space=pltpu.HBM)
    return pl.pallas_call(
        body, name="scatter_grads",
        in_specs=[hbm] * n_w, out_specs=[hbm] * n_w,
        out_shape=[jax.ShapeDtypeStruct((N_DEV, g.shape[0] // N_DEV, g.shape[1]), g.dtype) for g in grads],
        scratch_shapes=[pltpu.SemaphoreType.DMA((N_DEV - 1, n_w)), pltpu.SemaphoreType.DMA((N_DEV - 1, n_w)),
                        pltpu.SemaphoreType.DMA((n_w,))],
        compiler_params=_params(),
    )(*grads)


def _peer_copies(mode, srcs, lands, send_sems, recv_sems):
    me = _index(_place())
    copies = []
    for k in range(1, N_DEV):
        peer = _peer(k)
        for w, (src, land) in enumerate(zip(srcs, lands)):
            if mode == "gather":
                r = src.shape[0]
                dst = land.at[pl.ds(pl.multiple_of(me * r, 16), r), :]
            else:
                r = src.shape[0] // N_DEV
                src = src.at[pl.ds(pl.multiple_of(_index(peer) * r, 16), r), :]
                dst = land.at[me]
            copies.append(pltpu.make_async_remote_copy(
                src_ref=src, dst_ref=dst, send_sem=send_sems.at[(k - 1) * len(srcs) + w],
                recv_sem=recv_sems.at[(k - 1) * len(srcs) + w],
                device_id=peer, device_id_type=MESH))
    return copies


def _exchange_start(mode, srcs, lands, name):
    n = len(srcs)

    def body(*refs):
        for cp in _peer_copies(mode, refs[:n], refs[n:2 * n], refs[2 * n], refs[2 * n + 1]):
            cp.start()
        refs[-1][...] = jnp.zeros_like(refs[-1])

    hbm, sem = pl.BlockSpec(memory_space=pltpu.HBM), pl.BlockSpec(memory_space=pltpu.SEMAPHORE)
    arrays = list(srcs) + list(lands)
    out = pl.pallas_call(
        body, name=name,
        out_shape=(pltpu.SemaphoreType.DMA(((N_DEV - 1) * n,)), pltpu.SemaphoreType.DMA(((N_DEV - 1) * n,)),
                   *[pltpu.HBM(a.shape, a.dtype) for a in arrays], jax.ShapeDtypeStruct((8, LANES), F32)),
        in_specs=[hbm] * (2 * n), out_specs=(sem, sem, *[hbm] * (2 * n), pl.BlockSpec(memory_space=pltpu.VMEM)),
        input_output_aliases={i: 2 + i for i in range(2 * n)},
        compiler_params=pltpu.CompilerParams(has_side_effects=pltpu.SideEffectType.DATAFLOW_SIDE_EFFECTING),
    )(*[pltpu.with_memory_space_constraint(a, pltpu.HBM) for a in arrays])
    return out[0], out[1], out[2:2 + n], out[2 + n:2 + 2 * n], out[-1]


def _exchange_wait(mode, send_sems, recv_sems, srcs, lands, after, name):
    n = len(srcs)

    def body(*refs):
        copies = _peer_copies(mode, refs[:n], refs[n:2 * n], refs[2 * n], refs[2 * n + 1])
        for cp in copies:
            cp.wait_send()
        for cp in copies:
            cp.wait_recv()

    hbm, sem = pl.BlockSpec(memory_space=pltpu.HBM), pl.BlockSpec(memory_space=pltpu.SEMAPHORE)
    arrays = list(srcs) + list(lands)
    out = pl.pallas_call(
        body, name=name, out_shape=tuple(pltpu.HBM(a.shape, a.dtype) for a in arrays),
        in_specs=[hbm] * (2 * n) + [sem, sem, pl.BlockSpec(memory_space=pl.ANY)], out_specs=tuple([hbm] * (2 * n)),
        input_output_aliases={i: i for i in range(2 * n)},
        compiler_params=pltpu.CompilerParams(has_side_effects=pltpu.SideEffectType.DATAFLOW_SIDE_EFFECTING),
    )(*arrays, send_sems, recv_sems, after)
    return out[n:]


SMALL_WEIGHTS = ("b_ada", "g_pre_mix", "g_post_mix", "g_pre_ffn", "g_post_ffn", "w_pool", "b_pool", "pool_scale", "conv_b")


def _small_reduce_adam(sums_in, sums_mix, sums_ffn, sums_pool, dw_blk, dconv, loss_loc, weights, moms, vels):
    locals_ = [sums_in, sums_mix, sums_ffn, sums_pool, dw_blk, dconv, loss_loc]
    n_l, n_w = len(locals_), len(weights)
    d = sums_in.shape[1]
    diag = (len(POOL_WINDOWS), HEAD_DIM, HEAD_DIM)

    def body(*refs):
        loc = refs[:n_l]
        w_refs, m_refs, v_refs = (refs[n_l + k * n_w:n_l + (k + 1) * n_w] for k in range(3))
        outs = refs[n_l + 3 * n_w:n_l + 7 * n_w]
        dmod_ref, dconv_ref, loss_ref = refs[n_l + 7 * n_w:n_l + 7 * n_w + 3]
        gathered = refs[n_l + 7 * n_w + 3:n_l + 7 * n_w + 3 + n_l]
        diag_ref, send_sems, recv_sems = refs[-3:]
        me = _index(_place())
        blk = loc[4][...]
        for gi in range(len(POOL_WINDOWS)):
            lo = gi * HEAD_DIM
            diag_ref[gi] = blk[lo:lo + HEAD_DIM, lo:lo + HEAD_DIM]
        loc = loc[:4] + (diag_ref,) + loc[5:]

        def copy(a, k):
            return pltpu.make_async_remote_copy(
                src_ref=loc[a], dst_ref=gathered[a].at[me], send_sem=send_sems.at[a, k - 1],
                recv_sem=recv_sems.at[a, k - 1], device_id=_peer(k), device_id_type=MESH)

        copies = [copy(a, k) for k in range(1, N_DEV) for a in range(n_l)]
        for cp in copies:
            cp.start()
        for a in range(n_l):
            gathered[a][me] = loc[a][...]
        for cp in copies:
            cp.wait_recv()

        def total(a):
            tot = gathered[a][0]
            for dev in range(1, N_DEV):
                tot = tot + gathered[a][dev]
            return tot

        t_in, t_mix, t_ffn, t_pool, t_blk, t_conv, t_loss = (total(a) for a in range(n_l))
        dconv_ref[...] = t_conv
        loss_ref[...] = t_loss
        mod_rows = ((0, 0), (0, 1), (1, 3), (1, 0), (1, 1), (2, 0))
        for dev in range(N_DEV):
            for k, (a, r) in enumerate(mod_rows):
                dmod_ref[dev:dev + 1, k * d:(k + 1) * d] = gathered[a][dev, r:r + 1, :]

        def update(idx, g, at=()):
            sel = lambda ref: ref.at[at] if at else ref
            delta, nm, nv = _adam_math(sel(w_refs[idx])[...], g, sel(m_refs[idx])[...], sel(v_refs[idx])[...])
            for k, val in enumerate((g, delta, nm, nv)):
                sel(outs[4 * idx + k])[...] = val

        tots = (t_in, t_mix, t_ffn)
        update(0, jnp.concatenate([tots[a][r:r + 1] for a, r in mod_rows], axis=1))
        update(1, t_in[2:3])
        update(2, t_mix[4:5])
        update(3, t_mix[2:3])
        update(4, t_ffn[1:2])
        for gi in range(len(POOL_WINDOWS)):
            update(5, t_blk[gi], at=(0, gi))
        update(6, jnp.concatenate([t_pool[0:1, gi * HEAD_DIM:(gi + 1) * HEAD_DIM] for gi in range(len(POOL_WINDOWS))], axis=0),
               at=(0,))
        update(7, t_pool[1:2])
        update(8, t_conv[3:4])
        for cp in copies:
            cp.wait_send()

    vmem = pl.BlockSpec(memory_space=pltpu.VMEM)
    shape = lambda a: jax.ShapeDtypeStruct(a.shape, F32)
    out = pl.pallas_call(
        body, name="small_reduce_adam",
        in_specs=[vmem] * (n_l + 3 * n_w), out_specs=[vmem] * (4 * n_w + 3),
        out_shape=[shape(w) for w in weights for _ in range(4)]
        + [jax.ShapeDtypeStruct((N_DEV, 6 * d), F32), shape(dconv), shape(loss_loc)],
        scratch_shapes=[pltpu.VMEM((N_DEV,) + (diag if a is dw_blk else a.shape), F32) for a in locals_]
        + [pltpu.VMEM(diag, F32), pltpu.SemaphoreType.DMA((n_l, N_DEV - 1)), pltpu.SemaphoreType.DMA((n_l, N_DEV - 1))],
        compiler_params=_params(),
    )(*locals_, *weights, *moms, *vels)
    return out[:4 * n_w], out[4 * n_w], out[4 * n_w + 1], out[4 * n_w + 2]


def _adam_math(w, g, m, v):
    m = ADAM_B1 * m + (1.0 - ADAM_B1) * g
    v = ADAM_B2 * v + (1.0 - ADAM_B2) * (g * g)
    m_hat = m / (1.0 - ADAM_B1 ** ADAM_STEP)
    v_hat = v / (1.0 - ADAM_B2 ** ADAM_STEP)
    delta = -ADAM_LR * (m_hat / (jnp.sqrt(v_hat) + ADAM_EPS) + ADAM_WD * w)
    return delta, m, v


def _adam(w, g, m, v, name, tr):
    rows, cols = w.shape

    def body(w_ref, g_ref, m_ref, v_ref, d_ref, nm_ref, nv_ref):
        d_ref[...], nm_ref[...], nv_ref[...] = _adam_math(w_ref[...], g_ref[...], m_ref[...], v_ref[...])

    spec = pl.BlockSpec((tr, cols), lambda i: (i, 0))
    shape = jax.ShapeDtypeStruct((rows, cols), F32)
    return pl.pallas_call(
        body, name=name, grid=(rows // tr,), in_specs=[spec] * 4, out_specs=[spec] * 3,
        out_shape=[shape] * 3, compiler_params=_params(("arbitrary",)),
    )(w, g, m, v)


def _sum_adam(parts, w, m, v, name, tr):
    _, rows, cols = parts.shape

    def body(p_ref, w_ref, m_ref, v_ref, g_ref, d_ref, nm_ref, nv_ref):
        g = p_ref[0].astype(F32)
        for dev in range(1, N_DEV):
            g = g + p_ref[dev].astype(F32)
        g_ref[...] = g
        d_ref[...], nm_ref[...], nv_ref[...] = _adam_math(w_ref[...], g, m_ref[...], v_ref[...])

    spec = pl.BlockSpec((tr, cols), lambda i: (i, 0))
    shape = jax.ShapeDtypeStruct((rows, cols), F32)
    return pl.pallas_call(
        body, name=name, grid=(rows // tr,),
        in_specs=[pl.BlockSpec((N_DEV, tr, cols), lambda i: (0, i, 0)), spec, spec, spec],
        out_specs=[spec] * 4, out_shape=[shape] * 4, compiler_params=_params(("arbitrary",)),
    )(parts, w, m, v)


def _ada_grad_adam(c_all, dmod_cols, w, m, v, tr):
    rows, cols = w.shape

    def body(c_ref, dm_ref, w_ref, m_ref, v_ref, g_ref, d_ref, nm_ref, nv_ref):
        cv = c_ref[...]
        act = cv * jax.nn.sigmoid(cv)
        g = lax.dot_general(act, dm_ref[...], TN, preferred_element_type=F32, precision=lax.Precision.HIGHEST)
        g_ref[...] = g
        d_ref[...], nm_ref[...], nv_ref[...] = _adam_math(w_ref[...], g, m_ref[...], v_ref[...])

    spec = pl.BlockSpec((tr, cols), lambda i: (i, 0))
    shape = jax.ShapeDtypeStruct((rows, cols), F32)
    return pl.pallas_call(
        body, name="ada_grad_adam", grid=(rows // tr,),
        in_specs=[pl.BlockSpec((N_DEV, tr), lambda i: (0, i)), pl.BlockSpec((N_DEV, cols), lambda i: (0, 0)), spec, spec, spec],
        out_specs=[spec] * 4, out_shape=[shape] * 4, compiler_params=_params(("arbitrary",)),
    )(c_all, dmod_cols, w, m, v)


def _rope_tables(positions):
    s_len = positions.shape[0]
    inv_freq = ROPE_THETA ** (-jnp.arange(0, 2 * ROT_HALF, 2, dtype=F32) / (2 * ROT_HALF))
    ang = positions.astype(F32)[:, None] * inv_freq
    cos, sin = jnp.cos(ang), jnp.sin(ang)
    rest = HEAD_DIM - 2 * ROT_HALF
    zero = lambda n: jnp.zeros((s_len, n), F32)
    head = jnp.stack([jnp.concatenate([cos, cos, jnp.ones((s_len, rest), F32)], axis=1),
                      jnp.concatenate([-sin, zero(HEAD_DIM - ROT_HALF)], axis=1),
                      jnp.concatenate([zero(ROT_HALF), sin, zero(rest)], axis=1)])
    return jnp.tile(head, (1, 1, LANES // HEAD_DIM))


def _pad_rows(a, rows):
    return jnp.pad(a, ((0, rows - a.shape[0]), (0, 0)))


def _as_rows(a, rows):
    flat = a.reshape(-1)
    return jnp.pad(flat, (0, rows * LANES - flat.shape[0])).reshape(rows, LANES)


def _sequence_step(xs, target, rope, mods, gains, w_in_t, w_out_t, fetch_ffn, send_ffn_grads, w_blk_b, b_pool_r,
                   pool_scale_r, conv_w_all, conv_b):
    sh_m, sc_m, gt_m, sh_f, sc_f, gt_f = mods
    g_pre_mix, g_post_mix, g_pre_ffn, g_post_ffn = gains
    h1, u_pool, qkv = _premix_inproj(xs, sh_m, sc_m, g_pre_mix, w_in_t, rope, tm=512)
    o_g, lse_g = [], []
    for gi, dil in enumerate(DILATIONS):
        o, lse = _attn_fwd(qkv, gi, dil)
        o_g.append(o)
        lse_g.append(lse)
    x1, y1, h2, cat, attn, lse_all = _mix_out(xs, u_pool, o_g, lse_g, w_blk_b, b_pool_r, pool_scale_r, w_out_t,
                                              gt_m, g_post_mix, g_pre_ffn, sc_f, sh_f, tm=256)
    w_up_t, w_down_f = fetch_ffn(x1)
    gate, val, dy2, dout, sums_ffn, loss_loc = _ffn_fwd_loss(h2, x1, target, w_up_t, w_down_f, conv_w_all, conv_b,
                                                              gt_f, g_post_ffn, tm=256, tf=2816, ck=256)

    dgc, dval, dw_down, dconv = _ffn_bwd_act(dy2, gate, val, w_down_f, conv_w_all, conv_b, tm=1024, tf=256)
    dup, dh2 = _ffn_bwd_up(dgc, dval, w_up_t, conv_w_all, tm=256)
    dw_up_t = _wgrad(dup, h2, "wgrad_up", tk=1024, tmm=1408)
    token = send_ffn_grads(dw_up_t, dw_down)
    if token is not None:
        sc_f = sc_f + token[0:1, 0:1]
    dx1, dpool, dattn, delta, dw_out_t, sums_mix = _mix_bwd(dh2, dout, x1, y1, cat, attn, w_out_t, sc_f, g_pre_ffn,
                                                           gt_m, g_post_mix, tm=256)
    du, dw_blk, sums_pool = _pool_bwd(dpool, u_pool, w_blk_b, b_pool_r, pool_scale_r, tm=512)
    dqkv = []
    for gi, dil in enumerate(DILATIONS):
        dqkv += list(_attn_bwd(qkv, dattn, lse_all, delta, gi, dil))
    dproj, grad_x, sums_in = _inproj_bwd(du, dqkv, rope, w_in_t, xs, dx1, sc_m, g_pre_mix, tm=256)
    dw_in_t = _wgrad(dproj, h1, "wgrad_in", tk=1024, tmm=1280)
    return (loss_loc, grad_x, dw_in_t, dw_out_t, dw_up_t, dw_down, dw_blk, dconv,
            sums_in, sums_mix, sums_ffn, sums_pool)


def kernel(x, c, positions, w_ada, b_ada, g_pre_mix, g_post_mix, g_pre_ffn, g_post_ffn, w_in, w_pool, b_pool, pool_scale, w_out, w_up, conv_w, conv_b, w_down, loss_target, m_w_ada, m_b_ada, m_g_pre_mix, m_g_post_mix, m_g_pre_ffn, m_g_post_ffn, m_w_in, m_w_pool, m_b_pool, m_pool_scale, m_w_out, m_w_up, m_conv_w, m_conv_b, m_w_down, v_w_ada, v_b_ada, v_g_pre_mix, v_g_post_mix, v_g_pre_ffn, v_g_post_ffn, v_w_in, v_w_pool, v_b_pool, v_pool_scale, v_w_out, v_w_up, v_conv_w, v_conv_b, v_w_down):
    s_len, d = x.shape[1], x.shape[2]
    d_ff = w_down.shape[1] * N_DEV
    me = _index(_place())
    xs, target = x[0], loss_target[0]

    ncol = w_ada.shape[2]
    b_cols = lax.dynamic_slice(b_ada, (0, me * ncol), (1, ncol))
    c_all, mod, taps_all = _ada_exchange(jnp.broadcast_to(c, (8, d)), w_ada[0], b_cols, _pad_rows(conv_w[0], 8))
    c_all = c_all[:, 0, :]
    conv_w_all = jnp.transpose(taps_all[:, :3, :], (1, 0, 2)).reshape(3, d_ff)
    sh_m, sc_m, gt_m, sh_f, sc_f, gt_f = [mod[:, 0, :].reshape(1, -1)[:, k * d:(k + 1) * d] for k in range(6)]

    w_in_t, w_out_t = _gather_weights([w_in[0].T.astype(BF16), w_out[0].T.astype(BF16)])

    rope = _rope_tables(positions[0])
    w_blk = jnp.zeros((256, 256), F32)
    for gi in range(4):
        w_blk = lax.dynamic_update_slice(w_blk, w_pool[0, gi], (gi * HEAD_DIM, gi * HEAD_DIM))
    w_blk_b = w_blk.astype(BF16)
    b_pool_r, pool_scale_r = b_pool.reshape(1, 256), pool_scale.reshape(1, 256)

    up_sh, down_sh = w_up[0].T.astype(BF16), w_down[0].astype(BF16)
    w_in_t, conv_w_all, up_sh, down_sh = lax.optimization_barrier((w_in_t, conv_w_all, up_sh, down_sh))
    lands = [lax.dynamic_update_slice(lax.empty((N_DEV * s.shape[0], s.shape[1]), BF16), s, (me * s.shape[0], 0))
             for s in (up_sh, down_sh)]
    w_send, w_recv, w_src, w_land, w_token = _exchange_start("gather", [up_sh, down_sh], lands, "ffn_weights_start")

    def fetch_ffn(after):
        return _exchange_wait("gather", w_send, w_recv, w_src, w_land, after, "ffn_weights_wait")

    flight = []

    def scatter_lands(*grads):
        lands = []
        for g in grads:
            r = g.shape[0] // N_DEV
            own = lax.dynamic_slice(g, (me * r, 0), (r, g.shape[1]))
            lands.append(lax.dynamic_update_slice(lax.empty((N_DEV, r, g.shape[1]), BF16), own[None], (me, 0, 0)))
        return lands

    def send_ffn_grads(dw_up_t, dw_down):
        flight.extend(_exchange_start("scatter", [dw_up_t, dw_down], scatter_lands(dw_up_t, dw_down), "ffn_grads_start"))
        return flight[4]

    (loss_loc, grad_x, dw_in_t, dw_out_t, _, _, dw_blk, dconv,
     sums_in, sums_mix, sums_ffn, sums_pool) = _sequence_step(
        xs, target, rope, (sh_m + w_token[0:1, 0:1], sc_m, gt_m, sh_f, sc_f, gt_f),
        (g_pre_mix, g_post_mix, g_pre_ffn, g_post_ffn),
        w_in_t, w_out_t, fetch_ffn, send_ffn_grads, w_blk_b, b_pool_r, pool_scale_r, conv_w_all, conv_b)

    mix_flight = _exchange_start("scatter", [dw_in_t, dw_out_t], scatter_lands(dw_in_t, dw_out_t), "mix_grads_start")
    parts_ffn = _exchange_wait("scatter", *flight[:4], mix_flight[4], "ffn_grads_wait")
    big = {
        "w_up": [a.T for a in _sum_adam(parts_ffn[0], w_up[0].T, m_w_up[0].T, v_w_up[0].T, "adam_w_up", 64)],
        "w_down": _sum_adam(parts_ffn[1], w_down[0], m_w_down[0], v_w_down[0], "adam_w_down", 32),
    }

    rep_w = [b_ada, g_pre_mix, g_post_mix, g_pre_ffn, g_post_ffn, w_pool, b_pool, pool_scale, conv_b]
    rep_m = [m_b_ada, m_g_pre_mix, m_g_post_mix, m_g_pre_ffn, m_g_post_ffn, m_w_pool, m_b_pool, m_pool_scale, m_conv_b]
    rep_v = [v_b_ada, v_g_pre_mix, v_g_post_mix, v_g_pre_ffn, v_g_post_ffn, v_w_pool, v_b_pool, v_pool_scale, v_conv_b]
    rep_out, dmod_all, dconv_tot, loss_tot = _small_reduce_adam(
        sums_in, sums_mix, sums_ffn, sums_pool, dw_blk, dconv, loss_loc, rep_w, rep_m, rep_v)
    g_rep, d_rep, nm_rep, nv_rep = (rep_out[k::4] for k in range(4))

    fcol = d_ff // N_DEV
    g_cw = lax.dynamic_slice(dconv_tot, (0, me * fcol), (3, fcol))
    d_cw, nm_cw, nv_cw = _adam(conv_w[0], g_cw, m_conv_w[0], v_conv_w[0], "adam_conv_w", 3)

    dmod_cols = lax.dynamic_slice(dmod_all, (0, me * ncol), (N_DEV, ncol))
    g_ada, d_ada, nm_ada, nv_ada = _ada_grad_adam(c_all, dmod_cols, w_ada[0], m_w_ada[0], v_w_ada[0], 256)

    parts_mix = _exchange_wait("scatter", *mix_flight[:4], g_ada, "mix_grads_wait")
    big["w_in"] = [a.T for a in _sum_adam(parts_mix[0], w_in[0].T, m_w_in[0].T, v_w_in[0].T, "adam_w_in", 64)]
    big["w_out"] = [a.T for a in _sum_adam(parts_mix[1], w_out[0].T, m_w_out[0].T, v_w_out[0].T, "adam_w_out", 128)]

    loss = loss_tot[0, 0]

    def group(k):
        rep = (g_rep, d_rep, nm_rep, nv_rep)[k]
        ada = (g_ada, d_ada, nm_ada, nv_ada)[k][None]
        cw = (g_cw, d_cw, nm_cw, nv_cw)[k][None]
        return [ada, rep[0], rep[1], rep[2], rep[3], rep[4], big["w_in"][k][None], rep[5], rep[6], rep[7],
                big["w_out"][k][None], big["w_up"][k][None], cw, rep[8], big["w_down"][k][None]]

    return (loss, grad_x[None], *group(0), *group(1), *group(2), *group(3))
```

```python
import functools
import math

import jax
import jax.numpy as jnp
from jax import lax
from jax.experimental import pallas as pl
from jax.experimental.pallas import tpu as pltpu

F32 = jnp.float32
BF16 = jnp.bfloat16
MESH = pl.DeviceIdType.MESH

N_DEV = 8
HEAD_DIM = 64
ROT_HALF = 8
ROPE_THETA = 500000.0
POOL_WINDOWS = (2, 4, 8, 16)
DILATIONS = (1, 4, 16)
BLOCK = 128
NORM_EPS = 1e-6
HALO = 16
MASKED = -1e30
ATTN_FWD_UNROLL = 4
ATTN_BWD_UNROLL = 2

ADAM_LR = 0.001
ADAM_B1 = 0.9
ADAM_B2 = 0.999
ADAM_EPS = 1e-08
ADAM_WD = 0.01
ADAM_STEP = 10

V7X_VMEM_LIMIT = 56 * 1024 * 1024
LANES = 128

NT = (((1,), (1,)), ((), ()))
NN = (((1,), (0,)), ((), ()))
TN = (((0,), (0,)), ((), ()))


def _dot(a, b, dims):
    return lax.dot_general(a, b, dims, preferred_element_type=F32)


def _params(sem=None, vmem=V7X_VMEM_LIMIT):
    if sem is None:
        return pltpu.CompilerParams(vmem_limit_bytes=vmem)
    return pltpu.CompilerParams(dimension_semantics=sem, vmem_limit_bytes=vmem)


def _rstd(v):
    return lax.rsqrt(jnp.mean(v * v, axis=-1, keepdims=True) + NORM_EPS)


def _norm_bwd(dn, n, rstd):
    return rstd * (dn - n * jnp.mean(dn * n, axis=-1, keepdims=True))


def _rope_fwd(p, rope_ref):
    return p * rope_ref[0] + pltpu.roll(p, LANES - ROT_HALF, 1) * rope_ref[1] + pltpu.roll(p, ROT_HALF, 1) * rope_ref[2]


def _rope_bwd(dp, rope_ref):
    return dp * rope_ref[0] + pltpu.roll(dp * rope_ref[1], ROT_HALF, 1) + pltpu.roll(dp * rope_ref[2], LANES - ROT_HALF, 1)


def _gelu_parts(v):
    k = math.sqrt(2.0 / math.pi)
    t = jnp.tanh(k * (v + 0.044715 * v * v * v))
    g = 0.5 * v * (1.0 + t)
    dg = 0.5 * (1.0 + t) + 0.5 * v * (1.0 - t * t) * k * (1.0 + 3.0 * 0.044715 * v * v)
    return g, dg


def _halo_before(i, tile):
    return jnp.maximum(i * (tile // HALO) - 1, 0)


def _premix_inproj(x, sh, sc, g, w_in_t, rope, tm):
    s_len, d = x.shape
    n_proj = w_in_t.shape[0]
    n_slab = (n_proj - 256) // LANES

    def body(x_ref, sh_ref, sc_ref, g_ref, w_ref, rope_ref, h_ref, up_ref, qkv_ref):
        xv = x_ref[...]
        h = (xv * _rstd(xv) * g_ref[...]) * (1.0 + sc_ref[...]) + sh_ref[...]
        hb = h.astype(BF16)
        h_ref[...] = hb
        up_ref[...] = _dot(hb, w_ref[0:256, :], NT)
        for pair in range(n_slab // 2):
            p = _dot(hb, w_ref[256 + 256 * pair:512 + 256 * pair, :], NT)
            for half in range(2):
                ph = p[:, half * LANES:(half + 1) * LANES]
                if pair < 6:
                    ph = _rope_fwd(ph, rope_ref)
                if pair < 3:
                    ph = ph * (HEAD_DIM ** -0.5)
                qkv_ref[2 * pair + half] = ph

    vec = pl.BlockSpec((1, d), lambda i: (0, 0))
    return pl.pallas_call(
        body, name="premix_inproj", grid=(s_len // tm,),
        in_specs=[pl.BlockSpec((tm, d), lambda i: (i, 0)), vec, vec, vec,
                  pl.BlockSpec((n_proj, d), lambda i: (0, 0)),
                  pl.BlockSpec((3, tm, LANES), lambda i: (0, i, 0))],
        out_specs=[pl.BlockSpec((tm, d), lambda i: (i, 0)),
                   pl.BlockSpec((tm, 256), lambda i: (i, 0)),
                   pl.BlockSpec((n_slab, tm, LANES), lambda i: (0, i, 0))],
        out_shape=[jax.ShapeDtypeStruct((s_len, d), BF16),
                   jax.ShapeDtypeStruct((s_len, 256), F32),
                   jax.ShapeDtypeStruct((n_slab, s_len, LANES), F32)],
        compiler_params=_params(("arbitrary",)),
    )(x, sh, sc, g, w_in_t, rope)


def _block_rows(n, r, dil):
    start = n * (BLOCK * dil) + r
    if dil == 1:
        return pl.ds(pl.multiple_of(start, BLOCK), BLOCK)
    return pl.ds(start, BLOCK, stride=dil)


def _band_mask(n):
    ri = lax.broadcasted_iota(jnp.int32, (BLOCK, 2 * BLOCK), 0)
    cj = lax.broadcasted_iota(jnp.int32, (BLOCK, 2 * BLOCK), 1)
    cur = (cj >= BLOCK) & (cj - BLOCK <= ri)
    prev = (cj < BLOCK) & (cj >= ri) & (n > 0)
    return cur | prev


def _attn_fwd(qkv, group, dil):
    s_len = qkv.shape[1]
    nb = s_len // (BLOCK * dil)

    def body(q_ref, k_ref, v_ref, o_ref, lse_ref):
        lane = lax.broadcasted_iota(jnp.int32, (BLOCK, LANES), 1)
        first = lane < HEAD_DIM

        def block(t, carry):
            r, n = t // nb, t % nb
            cur = _block_rows(n, r, dil)
            prev = _block_rows(jnp.maximum(n - 1, 0), r, dil)
            q = q_ref[0, cur, :]
            kcat = jnp.concatenate([k_ref[0, prev, :], k_ref[0, cur, :]], axis=0).astype(BF16)
            vcat = jnp.concatenate([v_ref[0, prev, :], v_ref[0, cur, :]], axis=0).astype(BF16)
            valid = _band_mask(n)
            q2 = jnp.concatenate([jnp.where(first, q, 0.0), jnp.where(first, 0.0, q)], axis=0).astype(BF16)
            s = jnp.where(jnp.concatenate([valid, valid], axis=0), _dot(q2, kcat, NT), MASKED)
            m = jnp.max(s, axis=-1, keepdims=True)
            p = jnp.exp(s - m)
            den = jnp.sum(p, axis=-1, keepdims=True)
            o2 = _dot(p.astype(BF16), vcat, NN) / den
            lse2 = m + jnp.log(den)
            o_ref[0, cur, :] = jnp.where(first, o2[:BLOCK], o2[BLOCK:])
            lse_ref[0, cur, :] = jnp.where(first, lse2[:BLOCK], lse2[BLOCK:])
            return carry

        lax.fori_loop(0, nb * dil, block, 0, unroll=ATTN_FWD_UNROLL)

    def slab(base):
        return pl.BlockSpec((1, s_len, LANES), lambda s: (base + 2 * group + s, 0, 0))

    out = pl.BlockSpec((1, s_len, LANES), lambda s: (s, 0, 0))
    shape = jax.ShapeDtypeStruct((2, s_len, LANES), F32)
    return pl.pallas_call(
        body, name=f"attn_fwd_d{dil}", grid=(2,),
        in_specs=[slab(0), slab(6), slab(12)], out_specs=[out, out], out_shape=[shape, shape],
        compiler_params=_params(("arbitrary",)),
    )(qkv, qkv, qkv)


def _pool_mixed(u, halo, i, tm):
    ue = jnp.concatenate([halo, u], axis=0)
    s2 = ue + pltpu.roll(ue, 1, 0)
    s4 = s2 + pltpu.roll(s2, 2, 0)
    s8 = s4 + pltpu.roll(s4, 4, 0)
    s16 = s8 + pltpu.roll(s8, 8, 0)
    grp = lax.broadcasted_iota(jnp.int32, (tm, 256), 1) // HEAD_DIM
    pick = lambda a, b, c, e: jnp.where(grp == 0, a, jnp.where(grp == 1, b, jnp.where(grp == 2, c, e)))
    win_sum = pick(s2[HALO:], s4[HALO:], s8[HALO:], s16[HALO:])
    pos = (i * tm + lax.broadcasted_iota(jnp.int32, (tm, 256), 0)).astype(F32)
    count = jnp.minimum(pos + 1.0, pick(*[float(w) for w in POOL_WINDOWS]))
    return win_sum / count - u, count


def _mix_out(x, u_pool, o_g, lse_g, w_blk, b_pool, pool_scale, w_out_t, gt_m, g_post_mix, g_pre_ffn, sc_f, sh_f, tm):
    s_len, d = x.shape

    def body(x_ref, u_ref, uh_ref, o0, o1, o2, l0, l1, l2, wb_ref, bp_ref, ps_ref, wo_ref,
             gt_ref, g1_ref, g2_ref, sc_ref, sh_ref,
             x1_ref, y1_ref, h2_ref, cat_ref, attn_ref, lall_ref):
        i = pl.program_id(0)
        u = u_ref[...]
        halo = uh_ref[...] * (i > 0).astype(F32)
        mixed, _ = _pool_mixed(u, halo, i, tm)
        y = _dot(mixed.astype(BF16), wb_ref[...], NN) + bp_ref[...]
        pool = y * ps_ref[...]
        attn = []
        for s in range(2):
            la, lb, lc = l0[s], l1[s], l2[s]
            mx = jnp.maximum(jnp.maximum(la, lb), lc)
            ea, eb, ec = jnp.exp(la - mx), jnp.exp(lb - mx), jnp.exp(lc - mx)
            den = ea + eb + ec
            lall_ref[s] = mx + jnp.log(den)
            attn.append((ea / den) * o0[s] + (eb / den) * o1[s] + (ec / den) * o2[s])
        attn = jnp.concatenate(attn, axis=1)
        attn_ref[...] = attn
        cat = jnp.concatenate([pool, attn], axis=1).astype(BF16)
        cat_ref[...] = cat
        y1 = _dot(cat, wo_ref[...], NT)
        y1_ref[...] = y1
        x1 = x_ref[...] + gt_ref[...] * (y1 * _rstd(y1) * g1_ref[...])
        x1_ref[...] = x1
        h2 = (x1 * _rstd(x1) * g2_ref[...]) * (1.0 + sc_ref[...]) + sh_ref[...]
        h2_ref[...] = h2.astype(BF16)

    tile = lambda w: pl.BlockSpec((tm, w), lambda i: (i, 0))
    slab = pl.BlockSpec((2, tm, LANES), lambda i: (0, i, 0))
    const = lambda a: pl.BlockSpec(a.shape, lambda i: (0,) * a.ndim)
    return pl.pallas_call(
        body, name="mix_out", grid=(s_len // tm,),
        in_specs=[tile(d), tile(256), pl.BlockSpec((HALO, 256), lambda i: (_halo_before(i, tm), 0)),
                  slab, slab, slab, slab, slab, slab,
                  const(w_blk), const(b_pool), const(pool_scale), const(w_out_t),
                  const(gt_m), const(g_post_mix), const(g_pre_ffn), const(sc_f), const(sh_f)],
        out_specs=[tile(d), tile(d), tile(d), tile(512), tile(256), slab],
        out_shape=[jax.ShapeDtypeStruct((s_len, d), F32), jax.ShapeDtypeStruct((s_len, d), F32),
                   jax.ShapeDtypeStruct((s_len, d), BF16), jax.ShapeDtypeStruct((s_len, 512), BF16),
                   jax.ShapeDtypeStruct((s_len, 256), F32), jax.ShapeDtypeStruct((2, s_len, LANES), F32)],
        compiler_params=_params(("arbitrary",)),
    )(x, u_pool, u_pool, *o_g, *lse_g, w_blk, b_pool, pool_scale, w_out_t, gt_m, g_post_mix, g_pre_ffn, sc_f, sh_f)


def _conv_gate(gate_ext, cw, cb):
    gc = gate_ext * cw[2:3, :] + pltpu.roll(gate_ext, 1, 0) * cw[1:2, :] + pltpu.roll(gate_ext, 2, 0) * cw[0:1, :]
    return gc[HALO:] + cb


def _ffn_fwd_loss(h2, x1, target, w_up_t, w_down, conv_w, conv_b, gt_f, g_post_ffn, tm, tf, ck):
    s_len, d = x1.shape
    d_ff = w_down.shape[0]
    n_f = d_ff // tf

    def body(h_ref, hh_ref, x1_ref, tgt_ref, wg_ref, wv_ref, wd_ref, cw_ref, cb_ref, gt_ref, g_ref,
             gate_ref, val_ref, dy2_ref, dout_ref, sums_ref, loss_ref, acc_ref):
        i, j = pl.program_id(0), pl.program_id(1)

        @pl.when((i == 0) & (j == 0))
        def _():
            sums_ref[...] = jnp.zeros_like(sums_ref)
            loss_ref[...] = jnp.zeros_like(loss_ref)

        h = h_ref[...]
        h_ext = jnp.concatenate([hh_ref[...], h], axis=0)
        row = lax.broadcasted_iota(jnp.int32, (tm + HALO, ck), 0)
        no_halo = (row < HALO) & (i == 0)
        part = None
        for c in range(tf // ck):
            cs = slice(c * ck, (c + 1) * ck)
            gate_ext = jnp.where(no_halo, 0.0, _dot(h_ext, wg_ref[cs, :], NT))
            val = _dot(h, wv_ref[cs, :], NT)
            act, _ = _gelu_parts(_conv_gate(gate_ext, cw_ref[:, cs], cb_ref[:, cs]))
            gate_ref[:, cs] = gate_ext[HALO:].astype(BF16)
            val_ref[:, cs] = val.astype(BF16)
            p = _dot((act * val).astype(BF16), wd_ref[cs, :], NN)
            part = p if part is None else part + p

        @pl.when(j == 0)
        def _():
            acc_ref[...] = part

        @pl.when(j > 0)
        def _():
            acc_ref[...] += part

        @pl.when(j == n_f - 1)
        def _():
            y2 = acc_ref[...]
            rstd = _rstd(y2)
            n = y2 * rstd
            rn = n * g_ref[...]
            err = x1_ref[...] + gt_ref[...] * rn - tgt_ref[...]
            loss_ref[...] += 0.5 * jnp.sum(jnp.mean(err * err, axis=-1, keepdims=True), axis=0, keepdims=True)
            dout = err * (1.0 / d)
            dout_ref[...] = dout
            drn = dout * gt_ref[...]
            sums_ref[0:1, :] += jnp.sum(dout * rn, axis=0, keepdims=True)
            sums_ref[1:2, :] += jnp.sum(drn * n, axis=0, keepdims=True)
            dy2_ref[...] = _norm_bwd(drn * g_ref[...], n, rstd).astype(BF16)

    tok = lambda w: pl.BlockSpec((tm, w), lambda i, j: (i, 0))
    tokf = pl.BlockSpec((tm, tf), lambda i, j: (i, j))
    vec = pl.BlockSpec((1, d), lambda i, j: (0, 0))
    once = {"pipeline_mode": pl.Buffered(1)} if n_f == 1 else {}
    return pl.pallas_call(
        body, name="ffn_fwd_loss", grid=(s_len // tm, n_f),
        in_specs=[tok(d), pl.BlockSpec((HALO, d), lambda i, j: (_halo_before(i, tm), 0)), tok(d), tok(d),
                  pl.BlockSpec((tf, d), lambda i, j: (j, 0), **once),
                  pl.BlockSpec((tf, d), lambda i, j: (j + n_f, 0), **once),
                  pl.BlockSpec((tf, d), lambda i, j: (j, 0), **once),
                  pl.BlockSpec((3, tf), lambda i, j: (0, j)), pl.BlockSpec((1, tf), lambda i, j: (0, j)), vec, vec],
        out_specs=[tokf, tokf, tok(d), tok(d), pl.BlockSpec((8, d), lambda i, j: (0, 0)),
                   pl.BlockSpec((8, LANES), lambda i, j: (0, 0))],
        out_shape=[jax.ShapeDtypeStruct((s_len, d_ff), BF16), jax.ShapeDtypeStruct((s_len, d_ff), BF16),
                   jax.ShapeDtypeStruct((s_len, d), BF16), jax.ShapeDtypeStruct((s_len, d), F32),
                   jax.ShapeDtypeStruct((8, d), F32), jax.ShapeDtypeStruct((8, LANES), F32)],
        scratch_shapes=[pltpu.VMEM((tm, d), F32)],
        compiler_params=_params(("arbitrary", "arbitrary")),
    )(h2, h2, x1, target, w_up_t, w_up_t, w_down, conv_w, conv_b, gt_f, g_post_ffn)


def _ffn_bwd_act(dy2, gate, val, w_down, conv_w, conv_b, tm, tf):
    s_len, d = dy2.shape
    d_ff = w_down.shape[0]
    n_t = s_len // tm

    def body(dy_ref, g_ref, gh_ref, v_ref, wd_ref, cw_ref, cb_ref, dgc_ref, dval_ref, dwd_ref, dconv_ref, acc_ref):
        i = pl.program_id(1)
        gate_ext = jnp.concatenate([gh_ref[...], g_ref[...]], axis=0).astype(F32)
        row = lax.broadcasted_iota(jnp.int32, gate_ext.shape, 0)
        gate_ext = jnp.where((row < HALO) & (i == 0), 0.0, gate_ext)
        act, dact = _gelu_parts(_conv_gate(gate_ext, cw_ref[...], cb_ref[...]))
        v = v_ref[...].astype(F32)
        da = _dot(dy_ref[...], wd_ref[...], NT)
        dgc = da * v * dact
        dgc_ref[...] = dgc.astype(BF16)
        dval_ref[...] = (da * act).astype(BF16)
        dwd = _dot((act * v).astype(BF16), dy_ref[...], TN)
        rows = [jnp.sum(dgc * pltpu.roll(gate_ext, 2 - k, 0)[HALO:], axis=0, keepdims=True) for k in range(2)]
        rows += [jnp.sum(dgc * gate_ext[HALO:], axis=0, keepdims=True), jnp.sum(dgc, axis=0, keepdims=True),
                 jnp.zeros((4, tf), F32)]
        dconv = jnp.concatenate(rows, axis=0)

        @pl.when(i == 0)
        def _():
            acc_ref[...] = dwd
            dconv_ref[...] = dconv

        @pl.when(i > 0)
        def _():
            acc_ref[...] += dwd
            dconv_ref[...] += dconv

        @pl.when(i == n_t - 1)
        def _():
            dwd_ref[...] = acc_ref[...].astype(BF16)

    tokf = pl.BlockSpec((tm, tf), lambda j, i: (i, j))
    return pl.pallas_call(
        body, name="ffn_bwd_act", grid=(d_ff // tf, n_t),
        in_specs=[pl.BlockSpec((tm, d), lambda j, i: (i, 0)), tokf,
                  pl.BlockSpec((HALO, tf), lambda j, i: (_halo_before(i, tm), j)), tokf,
                  pl.BlockSpec((tf, d), lambda j, i: (j, 0)),
                  pl.BlockSpec((3, tf), lambda j, i: (0, j)), pl.BlockSpec((1, tf), lambda j, i: (0, j))],
        out_specs=[tokf, tokf, pl.BlockSpec((tf, d), lambda j, i: (j, 0)), pl.BlockSpec((8, tf), lambda j, i: (0, j))],
        out_shape=[jax.ShapeDtypeStruct((s_len, d_ff), BF16), jax.ShapeDtypeStruct((s_len, d_ff), BF16),
                   jax.ShapeDtypeStruct((d_ff, d), BF16), jax.ShapeDtypeStruct((8, d_ff), F32)],
        scratch_shapes=[pltpu.VMEM((tf, d), F32)],
        compiler_params=_params(("arbitrary", "arbitrary")),
    )(dy2, gate, gate, val, w_down, conv_w, conv_b)


def _ffn_bwd_up(dgc, dval, w_up_t, conv_w, tm):
    s_len, d_ff = dgc.shape
    d = w_up_t.shape[1]
    n_t = s_len // tm

    def body(dg_ref, dgn_ref, dv_ref, cw_ref, w_ref, dup_ref, dh_ref):
        i = pl.program_id(0)
        nxt = dgn_ref[...].astype(F32) * (i < n_t - 1).astype(F32)
        ext = jnp.concatenate([dg_ref[...].astype(F32), nxt], axis=0)
        rows = tm + HALO
        dgate = (ext * cw_ref[2:3, :] + pltpu.roll(ext, rows - 1, 0) * cw_ref[1:2, :]
                 + pltpu.roll(ext, rows - 2, 0) * cw_ref[0:1, :])[:tm]
        dup = jnp.concatenate([dgate.astype(BF16), dv_ref[...]], axis=1)
        dup_ref[...] = dup
        dh_ref[...] = _dot(dup, w_ref[...], NN)

    tokf = pl.BlockSpec((tm, d_ff), lambda i: (i, 0))
    return pl.pallas_call(
        body, name="ffn_bwd_up", grid=(n_t,),
        in_specs=[tokf, pl.BlockSpec((HALO, d_ff), lambda i: (jnp.minimum((i + 1) * (tm // HALO), s_len // HALO - 1), 0)),
                  tokf, pl.BlockSpec((3, d_ff), lambda i: (0, 0)), pl.BlockSpec((2 * d_ff, d), lambda i: (0, 0))],
        out_specs=[pl.BlockSpec((tm, 2 * d_ff), lambda i: (i, 0)), pl.BlockSpec((tm, d), lambda i: (i, 0))],
        out_shape=[jax.ShapeDtypeStruct((s_len, 2 * d_ff), BF16), jax.ShapeDtypeStruct((s_len, d), F32)],
        compiler_params=_params(("arbitrary",)),
    )(dgc, dgc, dval, conv_w, w_up_t)


def _mix_bwd(dh2, dout, x1, y1, cat, attn, w_out_t, sc_f, g_pre_ffn, gt_m, g_post_mix, tm):
    s_len, d = x1.shape
    n_t = s_len // tm

    def body(dh_ref, do_ref, x1_ref, y1_ref, cat_ref, at_ref, wo_ref, sc_ref, g2_ref, gt_ref, g1_ref,
             dx1_ref, dpool_ref, dattn_ref, delta_ref, dwo_ref, sums_ref, acc_ref):
        i = pl.program_id(0)
        dh = dh_ref[...]
        x1 = x1_ref[...]
        r2 = _rstd(x1)
        n2 = x1 * r2
        ng = n2 * g2_ref[...]
        dng = dh * (1.0 + sc_ref[...])
        dx1 = do_ref[...] + _norm_bwd(dng * g2_ref[...], n2, r2)
        dx1_ref[...] = dx1
        y1 = y1_ref[...]
        r1 = _rstd(y1)
        n1 = y1 * r1
        drn = dx1 * gt_ref[...]
        dy1 = _norm_bwd(drn * g1_ref[...], n1, r1).astype(BF16)
        dcat = _dot(dy1, wo_ref[...], NN)
        dpool_ref[...] = dcat[:, 0:256]
        lane = lax.broadcasted_iota(jnp.int32, (tm, LANES), 1)
        first = lane < HEAD_DIM
        for s in range(2):
            da = dcat[:, 256 + s * LANES:256 + (s + 1) * LANES]
            dattn_ref[s] = da
            prod = da * at_ref[:, s * LANES:(s + 1) * LANES]
            tot = jnp.sum(prod, axis=-1, keepdims=True)
            lo = jnp.sum(jnp.where(first, prod, 0.0), axis=-1, keepdims=True)
            delta_ref[s] = jnp.where(first, lo, tot - lo)
        dwo = _dot(dy1, cat_ref[...], TN)
        sums = jnp.concatenate(
            [jnp.sum(dh, axis=0, keepdims=True), jnp.sum(dh * ng, axis=0, keepdims=True),
             jnp.sum(dng * n2, axis=0, keepdims=True), jnp.sum(dx1 * (n1 * g1_ref[...]), axis=0, keepdims=True),
             jnp.sum(drn * n1, axis=0, keepdims=True), jnp.zeros((3, d), F32)], axis=0)

        @pl.when(i == 0)
        def _():
            acc_ref[...] = dwo
            sums_ref[...] = sums

        @pl.when(i > 0)
        def _():
            acc_ref[...] += dwo
            sums_ref[...] += sums

        @pl.when(i == n_t - 1)
        def _():
            dwo_ref[...] = acc_ref[...].astype(BF16)

    tile = lambda w: pl.BlockSpec((tm, w), lambda i: (i, 0))
    slab = pl.BlockSpec((2, tm, LANES), lambda i: (0, i, 0))
    vec = pl.BlockSpec((1, d), lambda i: (0, 0))
    return pl.pallas_call(
        body, name="mix_bwd", grid=(n_t,),
        in_specs=[tile(d), tile(d), tile(d), tile(d), tile(512), tile(256),
                  pl.BlockSpec((d, 512), lambda i: (0, 0)), vec, vec, vec, vec],
        out_specs=[tile(d), tile(256), slab, slab, pl.BlockSpec((d, 512), lambda i: (0, 0)),
                   pl.BlockSpec((8, d), lambda i: (0, 0))],
        out_shape=[jax.ShapeDtypeStruct((s_len, d), F32), jax.ShapeDtypeStruct((s_len, 256), F32),
                   jax.ShapeDtypeStruct((2, s_len, LANES), F32), jax.ShapeDtypeStruct((2, s_len, LANES), F32),
                   jax.ShapeDtypeStruct((d, 512), BF16), jax.ShapeDtypeStruct((8, d), F32)],
        scratch_shapes=[pltpu.VMEM((d, 512), F32)],
        compiler_params=_params(("arbitrary",)),
    )(dh2, dout, x1, y1, cat, attn, w_out_t, sc_f, g_pre_ffn, gt_m, g_post_mix)


def _pool_bwd(dpool, u_pool, w_blk, b_pool, pool_scale, tm):
    s_len = dpool.shape[0]
    n_t = s_len // tm

    def body(dp_ref, dpn_ref, u_ref, uh_ref, wb_ref, bp_ref, ps_ref, du_ref, dwb_ref, sums_ref):
        i = pl.program_id(0)
        u = u_ref[...]
        mixed, _ = _pool_mixed(u, uh_ref[...] * (i > 0).astype(F32), i, tm)
        mixed_b = mixed.astype(BF16)
        y = _dot(mixed_b, wb_ref[...], NN) + bp_ref[...]
        dp = dp_ref[...]
        dy = dp * ps_ref[...]
        dwb = _dot(mixed_b, dy.astype(BF16), TN)
        sums = jnp.concatenate([jnp.sum(dy, axis=0, keepdims=True), jnp.sum(dp * y, axis=0, keepdims=True),
                                jnp.zeros((6, 256), F32)], axis=0)
        dp_ext = jnp.concatenate([dp, dpn_ref[...] * (i < n_t - 1).astype(F32)], axis=0)
        dmix = _dot((dp_ext * ps_ref[...]).astype(BF16), wb_ref[...], NT)
        rows = tm + HALO
        grp = lax.broadcasted_iota(jnp.int32, (rows, 256), 1) // HEAD_DIM
        pick = lambda a, b, c, e: jnp.where(grp == 0, a, jnp.where(grp == 1, b, jnp.where(grp == 2, c, e)))
        pos = (i * tm + lax.broadcasted_iota(jnp.int32, (rows, 256), 0)).astype(F32)
        z = dmix / jnp.minimum(pos + 1.0, pick(*[float(w) for w in POOL_WINDOWS]))
        f2 = z + pltpu.roll(z, rows - 1, 0)
        f4 = f2 + pltpu.roll(f2, rows - 2, 0)
        f8 = f4 + pltpu.roll(f4, rows - 4, 0)
        f16 = f8 + pltpu.roll(f8, rows - 8, 0)
        du_ref[...] = (pick(f2, f4, f8, f16) - dmix)[:tm]

        @pl.when(i == 0)
        def _():
            dwb_ref[...] = dwb
            sums_ref[...] = sums

        @pl.when(i > 0)
        def _():
            dwb_ref[...] += dwb
            sums_ref[...] += sums

    tile = pl.BlockSpec((tm, 256), lambda i: (i, 0))
    const = lambda a: pl.BlockSpec(a.shape, lambda i: (0,) * a.ndim)
    return pl.pallas_call(
        body, name="pool_bwd", grid=(n_t,),
        in_specs=[tile, pl.BlockSpec((HALO, 256), lambda i: (jnp.minimum((i + 1) * (tm // HALO), s_len // HALO - 1), 0)),
                  tile, pl.BlockSpec((HALO, 256), lambda i: (_halo_before(i, tm), 0)),
                  const(w_blk), const(b_pool), const(pool_scale)],
        out_specs=[tile, pl.BlockSpec((256, 256), lambda i: (0, 0)), pl.BlockSpec((8, 256), lambda i: (0, 0))],
        out_shape=[jax.ShapeDtypeStruct((s_len, 256), F32), jax.ShapeDtypeStruct((256, 256), F32),
                   jax.ShapeDtypeStruct((8, 256), F32)],
        compiler_params=_params(("arbitrary",)),
    )(dpool, dpool, u_pool, u_pool, w_blk, b_pool, pool_scale)


def _attn_bwd(qkv, dattn, lse_all, delta, group, dil):
    s_len = qkv.shape[1]
    nb = s_len // (BLOCK * dil)

    def body(q_ref, k_ref, v_ref, do_ref, l_ref, dl_ref, dq_ref, dk_ref, dv_ref):
        lane = lax.broadcasted_iota(jnp.int32, (BLOCK, LANES), 1)
        first = lane < HEAD_DIM

        def block(t, carry):
            dk_part, dv_part = carry
            r, n = t // nb, t % nb
            cur = _block_rows(n, r, dil)
            prev = _block_rows(jnp.maximum(n - 1, 0), r, dil)
            q = q_ref[0, cur, :]
            do = do_ref[0, cur, :]
            lse = l_ref[0, cur, :]
            dlt = dl_ref[0, cur, :]
            kcat = jnp.concatenate([k_ref[0, prev, :], k_ref[0, cur, :]], axis=0).astype(BF16)
            vcat = jnp.concatenate([v_ref[0, prev, :], v_ref[0, cur, :]], axis=0).astype(BF16)
            valid = _band_mask(n)
            stack = lambda a: jnp.concatenate([jnp.where(first, a, 0.0), jnp.where(first, 0.0, a)], axis=0)
            rows2 = lambda a: jnp.concatenate([a[:, 0:1], a[:, HEAD_DIM:HEAD_DIM + 1]], axis=0)
            q2, do2 = stack(q).astype(BF16), stack(do).astype(BF16)
            valid2 = jnp.concatenate([valid, valid], axis=0)
            p = jnp.where(valid2, jnp.exp(_dot(q2, kcat, NT) - rows2(lse)), 0.0)
            ds = (p * (_dot(do2, vcat, NT) - rows2(dlt))).astype(BF16)
            dq2 = _dot(ds, kcat, NN)
            dq_ref[0, cur, :] = jnp.where(first, dq2[:BLOCK], dq2[BLOCK:])
            dkc = _dot(ds, q2, TN)
            dvc = _dot(p.astype(BF16), do2, TN)
            dk_ref[0, prev, :] = dk_part + dkc[:BLOCK]
            dv_ref[0, prev, :] = dv_part + dvc[:BLOCK]
            dk_ref[0, cur, :] = dkc[BLOCK:]
            dv_ref[0, cur, :] = dvc[BLOCK:]
            return dkc[BLOCK:], dvc[BLOCK:]

        def blocks(tt, carry):
            for u in range(ATTN_BWD_UNROLL):
                carry = block(tt * ATTN_BWD_UNROLL + u, carry)
            return carry

        zero = jnp.zeros((BLOCK, LANES), F32)
        lax.fori_loop(0, nb * dil // ATTN_BWD_UNROLL, blocks, (zero, zero))

    def slab(base):
        return pl.BlockSpec((1, s_len, LANES), lambda s: (base + 2 * group + s, 0, 0))

    one = pl.BlockSpec((1, s_len, LANES), lambda s: (s, 0, 0))
    shape = jax.ShapeDtypeStruct((2, s_len, LANES), F32)
    return pl.pallas_call(
        body, name=f"attn_bwd_d{dil}", grid=(2,),
        in_specs=[slab(0), slab(6), slab(12), one, one, one],
        out_specs=[one, one, one], out_shape=[shape, shape, shape],
        compiler_params=_params(("arbitrary",)),
    )(qkv, qkv, qkv, dattn, lse_all, delta)


def _dproj_assemble(du, dqkv, rope, tm):
    s_len = du.shape[0]
    n_proj = 256 + 18 * LANES

    def body(du_ref, *refs):
        dref, rope_ref, dproj_ref = refs[:9], refs[9], refs[10]
        dproj_ref[:, 0:256] = du_ref[...].astype(BF16)
        col = 256
        for kind in range(3):
            for grp in range(3):
                for s in range(2):
                    piece = dref[3 * grp + kind][s]
                    if kind < 2:
                        piece = _rope_bwd(piece, rope_ref)
                    if kind == 0:
                        piece = piece * (HEAD_DIM ** -0.5)
                    dproj_ref[:, col:col + LANES] = piece.astype(BF16)
                    col += LANES

    slab = pl.BlockSpec((2, tm, LANES), lambda i: (0, i, 0))
    return pl.pallas_call(
        body, name="dproj_assemble", grid=(s_len // tm,),
        in_specs=[pl.BlockSpec((tm, 256), lambda i: (i, 0))] + [slab] * 9 + [pl.BlockSpec((3, tm, LANES), lambda i: (0, i, 0))],
        out_specs=pl.BlockSpec((tm, n_proj), lambda i: (i, 0)),
        out_shape=jax.ShapeDtypeStruct((s_len, n_proj), BF16),
        compiler_params=_params(("arbitrary",)),
    )(du, *dqkv, rope)


def _inproj_bwd(dproj, w_in_t, x, dx1, sc_m, g_pre_mix, tm):
    s_len, d = x.shape
    n_proj = w_in_t.shape[0]
    n_t = s_len // tm

    def body(dproj_ref, w_ref, x_ref, dx1_ref, sc_ref, g_ref, dx_ref, sums_ref):
        i = pl.program_id(0)
        dh = _dot(dproj_ref[...], w_ref[...], NN)
        xv = x_ref[...]
        r = _rstd(xv)
        n = xv * r
        dng = dh * (1.0 + sc_ref[...])
        dx_ref[...] = dx1_ref[...] + _norm_bwd(dng * g_ref[...], n, r)
        sums = jnp.concatenate([jnp.sum(dh, axis=0, keepdims=True), jnp.sum(dh * (n * g_ref[...]), axis=0, keepdims=True),
                                jnp.sum(dng * n, axis=0, keepdims=True), jnp.zeros((5, d), F32)], axis=0)

        @pl.when(i == 0)
        def _():
            sums_ref[...] = sums

        @pl.when(i > 0)
        def _():
            sums_ref[...] += sums

    tile = lambda w: pl.BlockSpec((tm, w), lambda i: (i, 0))
    vec = pl.BlockSpec((1, d), lambda i: (0, 0))
    return pl.pallas_call(
        body, name="inproj_bwd", grid=(n_t,),
        in_specs=[tile(n_proj), pl.BlockSpec((n_proj, d), lambda i: (0, 0)), tile(d), tile(d), vec, vec],
        out_specs=[tile(d), pl.BlockSpec((8, d), lambda i: (0, 0))],
        out_shape=[jax.ShapeDtypeStruct((s_len, d), F32), jax.ShapeDtypeStruct((8, d), F32)],
        compiler_params=_params(("arbitrary",)),
    )(dproj, w_in_t, x, dx1, sc_m, g_pre_mix)


def _wgrad(a, b, name, tk, tmm):
    s_len, m = a.shape
    n = b.shape[1]
    n_k = s_len // tk

    def body(a_ref, b_ref, o_ref, acc_ref):
        k = pl.program_id(1)
        part = _dot(a_ref[...], b_ref[...], TN)

        @pl.when(k == 0)
        def _():
            acc_ref[...] = part

        @pl.when(k > 0)
        def _():
            acc_ref[...] += part

        @pl.when(k == n_k - 1)
        def _():
            o_ref[...] = acc_ref[...].astype(BF16)

    return pl.pallas_call(
        body, name=name, grid=(m // tmm, n_k),
        in_specs=[pl.BlockSpec((tk, tmm), lambda j, k: (k, j)), pl.BlockSpec((tk, n), lambda j, k: (k, 0))],
        out_specs=pl.BlockSpec((tmm, n), lambda j, k: (j, 0)),
        out_shape=jax.ShapeDtypeStruct((m, n), BF16),
        scratch_shapes=[pltpu.VMEM((tmm, n), F32)],
        compiler_params=_params(("arbitrary", "arbitrary")),
    )(a, b)


def _place():
    return lax.axis_index("x"), lax.axis_index("y"), lax.axis_index("c")


def _peer(k):
    x, y, c = _place()
    bx, by, bc = (k >> 2) & 1, (k >> 1) & 1, k & 1
    return (x ^ bx if bx else x, y ^ by if by else y, c ^ bc if bc else c)


def _index(pos):
    return 4 * pos[0] + 2 * pos[1] + pos[2]


def _ada_exchange(c_rows, w_ada, b_ada_cols, taps):
    d = c_rows.shape[1]
    ncol = w_ada.shape[1]

    def body(c_ref, w_ref, b_ref, t_ref, call_ref, mod_ref, tall_ref, stage_ref, send_sems, recv_sems):
        me = _index(_place())
        call_ref[me] = c_ref[...]
        tall_ref[me] = t_ref[...]

        def gather(k):
            return pltpu.make_async_remote_copy(
                src_ref=c_ref, dst_ref=call_ref.at[me], send_sem=send_sems.at[0, k - 1], recv_sem=recv_sems.at[0, k - 1],
                device_id=_peer(k), device_id_type=MESH)

        def gather_taps(k):
            return pltpu.make_async_remote_copy(
                src_ref=t_ref, dst_ref=tall_ref.at[me], send_sem=send_sems.at[2, k - 1], recv_sem=recv_sems.at[2, k - 1],
                device_id=_peer(k), device_id_type=MESH)

        for k in range(1, N_DEV):
            gather(k).start()
        for k in range(1, N_DEV):
            gather_taps(k).start()
        for k in range(1, N_DEV):
            gather(k).wait_recv()
        cv = jnp.concatenate([call_ref[b, 0:1, :] for b in range(N_DEV)], axis=0)
        act = cv * jax.nn.sigmoid(cv)
        mod = lax.dot_general(act, w_ref[...], NN, preferred_element_type=F32,
                              precision=lax.Precision.HIGHEST) + b_ref[...]
        for b in range(N_DEV):
            stage_ref[b] = jnp.broadcast_to(mod[b:b + 1, :], (8, ncol))
        mod_ref[me] = stage_ref[me]

        def scatter(k):
            return pltpu.make_async_remote_copy(
                src_ref=stage_ref.at[_index(_peer(k))], dst_ref=mod_ref.at[me],
                send_sem=send_sems.at[1, k - 1], recv_sem=recv_sems.at[1, k - 1],
                device_id=_peer(k), device_id_type=MESH)

        for k in range(1, N_DEV):
            scatter(k).start()
        for k in range(1, N_DEV):
            scatter(k).wait_recv()
        for k in range(1, N_DEV):
            gather_taps(k).wait_recv()
        for k in range(1, N_DEV):
            gather(k).wait_send()
            scatter(k).wait_send()
            gather_taps(k).wait_send()

    vmem = pl.BlockSpec(memory_space=pltpu.VMEM)
    return pl.pallas_call(
        body, name="ada_exchange",
        in_specs=[vmem] * 4, out_specs=[vmem] * 3,
        out_shape=[jax.ShapeDtypeStruct((N_DEV, 8, d), F32), jax.ShapeDtypeStruct((N_DEV, 8, ncol), F32),
                   jax.ShapeDtypeStruct((N_DEV,) + taps.shape, F32)],
        scratch_shapes=[pltpu.VMEM((N_DEV, 8, ncol), F32), pltpu.SemaphoreType.DMA((3, N_DEV - 1)),
                        pltpu.SemaphoreType.DMA((3, N_DEV - 1))],
        compiler_params=_params(),
    )(c_rows, w_ada, b_ada_cols, taps)


def _gather_weights(shards):
    n_w = len(shards)

    def body(*refs):
        srcs, outs = refs[:n_w], refs[n_w:2 * n_w]
        send_sems, recv_sems, local_sems = refs[2 * n_w:]
        x, y, c = _place()
        me, sibling = (x, y, c), (x, y, 1 - c)
        chips = [(1 - x, y), (x, 1 - y), (1 - x, 1 - y)]

        def rows(w, pos):
            r = shards[w].shape[0]
            return outs[w].at[pl.ds(pl.multiple_of(_index(pos) * r, 16), r), :]

        def copy(k, w, block, to, own=False):
            return pltpu.make_async_remote_copy(
                src_ref=srcs[w] if own else rows(w, block), dst_ref=rows(w, block),
                send_sem=send_sems.at[k, w], recv_sem=recv_sems.at[k, w], device_id=to, device_id_type=MESH)

        mine = [pltpu.make_async_copy(srcs[w], rows(w, me), local_sems.at[w]) for w in range(n_w)]
        for cp in mine:
            cp.start()
        first = [copy(0, w, me, sibling, own=True) for w in range(n_w)]
        first += [copy(1 + j, w, me, (*chip, c), own=True) for j, chip in enumerate(chips) for w in range(n_w)]
        for cp in first:
            cp.start()
        passed = []
        for j, chip in enumerate(chips):
            for w in range(n_w):
                copy(1 + j, w, (*chip, c), me).wait_recv()
                fwd = copy(4 + j, w, (*chip, c), sibling)
                fwd.start()
                passed.append(fwd)
        for w in range(n_w):
            copy(0, w, sibling, me).wait_recv()
        for j, chip in enumerate(chips):
            for w in range(n_w):
                copy(4 + j, w, (*chip, 1 - c), me).wait_recv()
        for cp in first + passed:
            cp.wait_send()
        for cp in mine:
            cp.wait()

    hbm = pl.BlockSpec(memory_space=pltpu.HBM)
    return pl.pallas_call(
        body, name="gather_weights",
        in_specs=[hbm] * n_w, out_specs=[hbm] * n_w,
        out_shape=[jax.ShapeDtypeStruct((N_DEV * s.shape[0], s.shape[1]), s.dtype) for s in shards],
        scratch_shapes=[pltpu.SemaphoreType.DMA((N_DEV - 1, n_w)), pltpu.SemaphoreType.DMA((N_DEV - 1, n_w)),
                        pltpu.SemaphoreType.DMA((n_w,))],
        compiler_params=_params(),
    )(*shards)


def _scatter_grads(grads):
    n_w = len(grads)

    def body(*refs):
        srcs, outs = refs[:n_w], refs[n_w:2 * n_w]
        send_sems, recv_sems, local_sems = refs[2 * n_w:]
        me = _index(_place())

        def slab(w, dev):
            r = grads[w].shape[0] // N_DEV
            return srcs[w].at[pl.ds(pl.multiple_of(dev * r, 16), r), :]

        def copy(k, w):
            return pltpu.make_async_remote_copy(
                src_ref=slab(w, _index(_peer(k))), dst_ref=outs[w].at[me],
                send_sem=send_sems.at[k - 1, w], recv_sem=recv_sems.at[k - 1, w],
                device_id=_peer(k), device_id_type=MESH)

        mine = [pltpu.make_async_copy(slab(w, me), outs[w].at[me], local_sems.at[w]) for w in range(n_w)]
        for cp in mine:
            cp.start()
        sends = [copy(k, w) for k in range(1, N_DEV) for w in range(n_w)]
        for cp in sends:
            cp.start()
        for cp in sends:
            cp.wait_recv()
        for cp in sends:
            cp.wait_send()
        for cp in mine:
            cp.wait()

    hbm = pl.BlockSpec(memory_space=pltpu.HBM)
    return pl.pallas_call(
        body, name="scatter_grads",
        in_specs=[hbm] * n_w, out_specs=[hbm] * n_w,
        out_shape=[jax.ShapeDtypeStruct((N_DEV, g.shape[0] // N_DEV, g.shape[1]), g.dtype) for g in grads],
        scratch_shapes=[pltpu.SemaphoreType.DMA((N_DEV - 1, n_w)), pltpu.SemaphoreType.DMA((N_DEV - 1, n_w)),
                        pltpu.SemaphoreType.DMA((n_w,))],
        compiler_params=_params(),
    )(*grads)


def _peer_copies(mode, srcs, lands, send_sems, recv_sems):
    me = _index(_place())
    copies = []
    for k in range(1, N_DEV):
        peer = _peer(k)
        for w, (src, land) in enumerate(zip(srcs, lands)):
            if mode == "gather":
                r = src.shape[0]
                dst = land.at[pl.ds(pl.multiple_of(me * r, 16), r), :]
            else:
                r = src.shape[0] // N_DEV
                src = src.at[pl.ds(pl.multiple_of(_index(peer) * r, 16), r), :]
                dst = land.at[me]
            copies.append(pltpu.make_async_remote_copy(
                src_ref=src, dst_ref=dst, send_sem=send_sems.at[(k - 1) * len(srcs) + w],
                recv_sem=recv_sems.at[(k - 1) * len(srcs) + w],
                device_id=peer, device_id_type=MESH))
    return copies


def _exchange_start(mode, srcs, lands, name):
    n = len(srcs)

    def body(*refs):
        for cp in _peer_copies(mode, refs[:n], refs[n:2 * n], refs[2 * n], refs[2 * n + 1]):
            cp.start()
        refs[-1][...] = jnp.zeros_like(refs[-1])

    hbm, sem = pl.BlockSpec(memory_space=pltpu.HBM), pl.BlockSpec(memory_space=pltpu.SEMAPHORE)
    arrays = list(srcs) + list(lands)
    out = pl.pallas_call(
        body, name=name,
        out_shape=(pltpu.SemaphoreType.DMA(((N_DEV - 1) * n,)), pltpu.SemaphoreType.DMA(((N_DEV - 1) * n,)),
                   *[pltpu.HBM(a.shape, a.dtype) for a in arrays], jax.ShapeDtypeStruct((8, LANES), F32)),
        in_specs=[hbm] * (2 * n), out_specs=(sem, sem, *[hbm] * (2 * n), pl.BlockSpec(memory_space=pltpu.VMEM)),
        input_output_aliases={i: 2 + i for i in range(2 * n)},
        compiler_params=pltpu.CompilerParams(has_side_effects=pltpu.SideEffectType.DATAFLOW_SIDE_EFFECTING),
    )(*[pltpu.with_memory_space_constraint(a, pltpu.HBM) for a in arrays])
    return out[0], out[1], out[2:2 + n], out[2 + n:2 + 2 * n], out[-1]


def _exchange_wait(mode, send_sems, recv_sems, srcs, lands, after, name):
    n = len(srcs)

    def body(*refs):
        copies = _peer_copies(mode, refs[:n], refs[n:2 * n], refs[2 * n], refs[2 * n + 1])
        for cp in copies:
            cp.wait_send()
        for cp in copies:
            cp.wait_recv()

    hbm, sem = pl.BlockSpec(memory_space=pltpu.HBM), pl.BlockSpec(memory_space=pltpu.SEMAPHORE)
    arrays = list(srcs) + list(lands)
    out = pl.pallas_call(
        body, name=name, out_shape=tuple(pltpu.HBM(a.shape, a.dtype) for a in arrays),
        in_specs=[hbm] * (2 * n) + [sem, sem, pl.BlockSpec(memory_space=pl.ANY)], out_specs=tuple([hbm] * (2 * n)),
        input_output_aliases={i: i for i in range(2 * n)},
        compiler_params=pltpu.CompilerParams(has_side_effects=pltpu.SideEffectType.DATAFLOW_SIDE_EFFECTING),
    )(*arrays, send_sems, recv_sems, after)
    return out[n:]


SMALL_WEIGHTS = ("b_ada", "g_pre_mix", "g_post_mix", "g_pre_ffn", "g_post_ffn", "w_pool", "b_pool", "pool_scale", "conv_b")


def _small_reduce_adam(sums_in, sums_mix, sums_ffn, sums_pool, dw_blk, dconv, loss_loc, weights, moms, vels):
    locals_ = [sums_in, sums_mix, sums_ffn, sums_pool, dw_blk, dconv, loss_loc]
    n_l, n_w = len(locals_), len(weights)
    d = sums_in.shape[1]
    diag = (len(POOL_WINDOWS), HEAD_DIM, HEAD_DIM)

    def body(*refs):
        loc = refs[:n_l]
        w_refs, m_refs, v_refs = (refs[n_l + k * n_w:n_l + (k + 1) * n_w] for k in range(3))
        outs = refs[n_l + 3 * n_w:n_l + 7 * n_w]
        dmod_ref, dconv_ref, loss_ref = refs[n_l + 7 * n_w:n_l + 7 * n_w + 3]
        gathered = refs[n_l + 7 * n_w + 3:n_l + 7 * n_w + 3 + n_l]
        diag_ref, send_sems, recv_sems = refs[-3:]
        me = _index(_place())
        blk = loc[4][...]
        for gi in range(len(POOL_WINDOWS)):
            lo = gi * HEAD_DIM
            diag_ref[gi] = blk[lo:lo + HEAD_DIM, lo:lo + HEAD_DIM]
        loc = loc[:4] + (diag_ref,) + loc[5:]

        def copy(a, k):
            return pltpu.make_async_remote_copy(
                src_ref=loc[a], dst_ref=gathered[a].at[me], send_sem=send_sems.at[a, k - 1],
                recv_sem=recv_sems.at[a, k - 1], device_id=_peer(k), device_id_type=MESH)

        copies = [copy(a, k) for k in range(1, N_DEV) for a in range(n_l)]
        for cp in copies:
            cp.start()
        for a in range(n_l):
            gathered[a][me] = loc[a][...]
        for cp in copies:
            cp.wait_recv()

        def total(a):
            tot = gathered[a][0]
            for dev in range(1, N_DEV):
                tot = tot + gathered[a][dev]
            return tot

        t_in, t_mix, t_ffn, t_pool, t_blk, t_conv, t_loss = (total(a) for a in range(n_l))
        dconv_ref[...] = t_conv
        loss_ref[...] = t_loss
        mod_rows = ((0, 0), (0, 1), (1, 3), (1, 0), (1, 1), (2, 0))
        for dev in range(N_DEV):
            for k, (a, r) in enumerate(mod_rows):
                dmod_ref[dev:dev + 1, k * d:(k + 1) * d] = gathered[a][dev, r:r + 1, :]

        def update(idx, g, at=()):
            sel = lambda ref: ref.at[at] if at else ref
            delta, nm, nv = _adam_math(sel(w_refs[idx])[...], g, sel(m_refs[idx])[...], sel(v_refs[idx])[...])
            for k, val in enumerate((g, delta, nm, nv)):
                sel(outs[4 * idx + k])[...] = val

        tots = (t_in, t_mix, t_ffn)
        update(0, jnp.concatenate([tots[a][r:r + 1] for a, r in mod_rows], axis=1))
        update(1, t_in[2:3])
        update(2, t_mix[4:5])
        update(3, t_mix[2:3])
        update(4, t_ffn[1:2])
        for gi in range(len(POOL_WINDOWS)):
            update(5, t_blk[gi], at=(0, gi))
        update(6, jnp.concatenate([t_pool[0:1, gi * HEAD_DIM:(gi + 1) * HEAD_DIM] for gi in range(len(POOL_WINDOWS))], axis=0),
               at=(0,))
        update(7, t_pool[1:2])
        update(8, t_conv[3:4])
        for cp in copies:
            cp.wait_send()

    vmem = pl.BlockSpec(memory_space=pltpu.VMEM)
    shape = lambda a: jax.ShapeDtypeStruct(a.shape, F32)
    out = pl.pallas_call(
        body, name="small_reduce_adam",
        in_specs=[vmem] * (n_l + 3 * n_w), out_specs=[vmem] * (4 * n_w + 3),
        out_shape=[shape(w) for w in weights for _ in range(4)]
        + [jax.ShapeDtypeStruct((N_DEV, 6 * d), F32), shape(dconv), shape(loss_loc)],
        scratch_shapes=[pltpu.VMEM((N_DEV,) + (diag if a is dw_blk else a.shape), F32) for a in locals_]
        + [pltpu.VMEM(diag, F32), pltpu.SemaphoreType.DMA((n_l, N_DEV - 1)), pltpu.SemaphoreType.DMA((n_l, N_DEV - 1))],
        compiler_params=_params(),
    )(*locals_, *weights, *moms, *vels)
    return out[:4 * n_w], out[4 * n_w], out[4 * n_w + 1], out[4 * n_w + 2]


def _adam_math(w, g, m, v):
    m = ADAM_B1 * m + (1.0 - ADAM_B1) * g
    v = ADAM_B2 * v + (1.0 - ADAM_B2) * (g * g)
    m_hat = m / (1.0 - ADAM_B1 ** ADAM_STEP)
    v_hat = v / (1.0 - ADAM_B2 ** ADAM_STEP)
    delta = -ADAM_LR * (m_hat / (jnp.sqrt(v_hat) + ADAM_EPS) + ADAM_WD * w)
    return delta, m, v


def _adam(w, g, m, v, name, tr):
    rows, cols = w.shape

    def body(w_ref, g_ref, m_ref, v_ref, d_ref, nm_ref, nv_ref):
        d_ref[...], nm_ref[...], nv_ref[...] = _adam_math(w_ref[...], g_ref[...], m_ref[...], v_ref[...])

    spec = pl.BlockSpec((tr, cols), lambda i: (i, 0))
    shape = jax.ShapeDtypeStruct((rows, cols), F32)
    return pl.pallas_call(
        body, name=name, grid=(rows // tr,), in_specs=[spec] * 4, out_specs=[spec] * 3,
        out_shape=[shape] * 3, compiler_params=_params(("arbitrary",)),
    )(w, g, m, v)


def _sum_adam(parts, w, m, v, name, tr):
    _, rows, cols = parts.shape

    def body(p_ref, w_ref, m_ref, v_ref, g_ref, d_ref, nm_ref, nv_ref):
        g = p_ref[0].astype(F32)
        for dev in range(1, N_DEV):
            g = g + p_ref[dev].astype(F32)
        g_ref[...] = g
        d_ref[...], nm_ref[...], nv_ref[...] = _adam_math(w_ref[...], g, m_ref[...], v_ref[...])

    spec = pl.BlockSpec((tr, cols), lambda i: (i, 0))
    shape = jax.ShapeDtypeStruct((rows, cols), F32)
    return pl.pallas_call(
        body, name=name, grid=(rows // tr,),
        in_specs=[pl.BlockSpec((N_DEV, tr, cols), lambda i: (0, i, 0)), spec, spec, spec],
        out_specs=[spec] * 4, out_shape=[shape] * 4, compiler_params=_params(("arbitrary",)),
    )(parts, w, m, v)


def _ada_grad_adam(c_all, dmod_cols, w, m, v, tr):
    rows, cols = w.shape

    def body(c_ref, dm_ref, w_ref, m_ref, v_ref, g_ref, d_ref, nm_ref, nv_ref):
        cv = c_ref[...]
        act = cv * jax.nn.sigmoid(cv)
        g = lax.dot_general(act, dm_ref[...], TN, preferred_element_type=F32, precision=lax.Precision.HIGHEST)
        g_ref[...] = g
        d_ref[...], nm_ref[...], nv_ref[...] = _adam_math(w_ref[...], g, m_ref[...], v_ref[...])

    spec = pl.BlockSpec((tr, cols), lambda i: (i, 0))
    shape = jax.ShapeDtypeStruct((rows, cols), F32)
    return pl.pallas_call(
        body, name="ada_grad_adam", grid=(rows // tr,),
        in_specs=[pl.BlockSpec((N_DEV, tr), lambda i: (0, i)), pl.BlockSpec((N_DEV, cols), lambda i: (0, 0)), spec, spec, spec],
        out_specs=[spec] * 4, out_shape=[shape] * 4, compiler_params=_params(("arbitrary",)),
    )(c_all, dmod_cols, w, m, v)


def _rope_tables(positions):
    s_len = positions.shape[0]
    inv_freq = ROPE_THETA ** (-jnp.arange(0, 2 * ROT_HALF, 2, dtype=F32) / (2 * ROT_HALF))
    ang = positions.astype(F32)[:, None] * inv_freq
    cos, sin = jnp.cos(ang), jnp.sin(ang)
    rest = HEAD_DIM - 2 * ROT_HALF
    zero = lambda n: jnp.zeros((s_len, n), F32)
    head = jnp.stack([jnp.concatenate([cos, cos, jnp.ones((s_len, rest), F32)], axis=1),
                      jnp.concatenate([-sin, zero(HEAD_DIM - ROT_HALF)], axis=1),
                      jnp.concatenate([zero(ROT_HALF), sin, zero(rest)], axis=1)])
    return jnp.tile(head, (1, 1, LANES // HEAD_DIM))


def _pad_rows(a, rows):
    return jnp.pad(a, ((0, rows - a.shape[0]), (0, 0)))


def _as_rows(a, rows):
    flat = a.reshape(-1)
    return jnp.pad(flat, (0, rows * LANES - flat.shape[0])).reshape(rows, LANES)


def _sequence_step(xs, target, rope, mods, gains, w_in_t, w_out_t, fetch_ffn, send_ffn_grads, send_mix_grads, w_blk_b, b_pool_r,
                   pool_scale_r, conv_w_all, conv_b):
    sh_m, sc_m, gt_m, sh_f, sc_f, gt_f = mods
    g_pre_mix, g_post_mix, g_pre_ffn, g_post_ffn = gains
    h1, u_pool, qkv = _premix_inproj(xs, sh_m, sc_m, g_pre_mix, w_in_t, rope, tm=512)
    o_g, lse_g = [], []
    for gi, dil in enumerate(DILATIONS):
        o, lse = _attn_fwd(qkv, gi, dil)
        o_g.append(o)
        lse_g.append(lse)
    x1, y1, h2, cat, attn, lse_all = _mix_out(xs, u_pool, o_g, lse_g, w_blk_b, b_pool_r, pool_scale_r, w_out_t,
                                              gt_m, g_post_mix, g_pre_ffn, sc_f, sh_f, tm=256)
    w_up_t, w_down_f = fetch_ffn(x1)
    gate, val, dy2, dout, sums_ffn, loss_loc = _ffn_fwd_loss(h2, x1, target, w_up_t, w_down_f, conv_w_all, conv_b,
                                                              gt_f, g_post_ffn, tm=256, tf=2816, ck=256)

    dgc, dval, dw_down, dconv = _ffn_bwd_act(dy2, gate, val, w_down_f, conv_w_all, conv_b, tm=1024, tf=256)
    dup, dh2 = _ffn_bwd_up(dgc, dval, w_up_t, conv_w_all, tm=256)
    dw_up_t = _wgrad(dup, h2, "wgrad_up", tk=1024, tmm=1408)
    token = send_ffn_grads(dw_up_t, dw_down)
    if token is not None:
        sc_f = sc_f + token[0:1, 0:1]
    dx1, dpool, dattn, delta, dw_out_t, sums_mix = _mix_bwd(dh2, dout, x1, y1, cat, attn, w_out_t, sc_f, g_pre_ffn,
                                                           gt_m, g_post_mix, tm=256)
    du, dw_blk, sums_pool = _pool_bwd(dpool, u_pool, w_blk_b, b_pool_r, pool_scale_r, tm=512)
    dqkv = []
    for gi, dil in enumerate(DILATIONS):
        dqkv += list(_attn_bwd(qkv, dattn, lse_all, delta, gi, dil))
    dproj = _dproj_assemble(du, dqkv, rope, tm=512)
    dw_in_t = _wgrad(dproj, h1, "wgrad_in", tk=1024, tmm=1280)
    token = send_mix_grads(dw_in_t, dw_out_t)
    if token is not None:
        sc_m = sc_m + token[0:1, 0:1]
    grad_x, sums_in = _inproj_bwd(dproj, w_in_t, xs, dx1, sc_m, g_pre_mix, tm=256)
    return (loss_loc, grad_x, dw_in_t, dw_out_t, dw_up_t, dw_down, dw_blk, dconv,
            sums_in, sums_mix, sums_ffn, sums_pool)


def kernel(x, c, positions, w_ada, b_ada, g_pre_mix, g_post_mix, g_pre_ffn, g_post_ffn, w_in, w_pool, b_pool, pool_scale, w_out, w_up, conv_w, conv_b, w_down, loss_target, m_w_ada, m_b_ada, m_g_pre_mix, m_g_post_mix, m_g_pre_ffn, m_g_post_ffn, m_w_in, m_w_pool, m_b_pool, m_pool_scale, m_w_out, m_w_up, m_conv_w, m_conv_b, m_w_down, v_w_ada, v_b_ada, v_g_pre_mix, v_g_post_mix, v_g_pre_ffn, v_g_post_ffn, v_w_in, v_w_pool, v_b_pool, v_pool_scale, v_w_out, v_w_up, v_conv_w, v_conv_b, v_w_down):
    s_len, d = x.shape[1], x.shape[2]
    d_ff = w_down.shape[1] * N_DEV
    me = _index(_place())
    xs, target = x[0], loss_target[0]

    ncol = w_ada.shape[2]
    b_cols = lax.dynamic_slice(b_ada, (0, me * ncol), (1, ncol))
    c_all, mod, taps_all = _ada_exchange(jnp.broadcast_to(c, (8, d)), w_ada[0], b_cols, _pad_rows(conv_w[0], 8))
    c_all = c_all[:, 0, :]
    conv_w_all = jnp.transpose(taps_all[:, :3, :], (1, 0, 2)).reshape(3, d_ff)
    sh_m, sc_m, gt_m, sh_f, sc_f, gt_f = [mod[:, 0, :].reshape(1, -1)[:, k * d:(k + 1) * d] for k in range(6)]

    w_in_t, w_out_t = _gather_weights([w_in[0].T.astype(BF16), w_out[0].T.astype(BF16)])

    rope = _rope_tables(positions[0])
    w_blk = jnp.zeros((256, 256), F32)
    for gi in range(4):
        w_blk = lax.dynamic_update_slice(w_blk, w_pool[0, gi], (gi * HEAD_DIM, gi * HEAD_DIM))
    w_blk_b = w_blk.astype(BF16)
    b_pool_r, pool_scale_r = b_pool.reshape(1, 256), pool_scale.reshape(1, 256)

    up_sh, down_sh = w_up[0].T.astype(BF16), w_down[0].astype(BF16)
    w_in_t, conv_w_all, up_sh, down_sh = lax.optimization_barrier((w_in_t, conv_w_all, up_sh, down_sh))
    lands = [lax.dynamic_update_slice(lax.empty((N_DEV * s.shape[0], s.shape[1]), BF16), s, (me * s.shape[0], 0))
             for s in (up_sh, down_sh)]
    w_send, w_recv, w_src, w_land, w_token = _exchange_start("gather", [up_sh, down_sh], lands, "ffn_weights_start")

    def fetch_ffn(after):
        return _exchange_wait("gather", w_send, w_recv, w_src, w_land, after, "ffn_weights_wait")

    flight = []

    def scatter_lands(*grads):
        lands = []
        for g in grads:
            r = g.shape[0] // N_DEV
            own = lax.dynamic_slice(g, (me * r, 0), (r, g.shape[1]))
            lands.append(lax.dynamic_update_slice(lax.empty((N_DEV, r, g.shape[1]), BF16), own[None], (me, 0, 0)))
        return lands

    def send_ffn_grads(dw_up_t, dw_down):
        flight.extend(_exchange_start("scatter", [dw_up_t, dw_down], scatter_lands(dw_up_t, dw_down), "ffn_grads_start"))
        return flight[4]

    mix_flight = []

    def send_mix_grads(dw_in_t, dw_out_t):
        mix_flight.extend(_exchange_start("scatter", [dw_in_t, dw_out_t], scatter_lands(dw_in_t, dw_out_t), "mix_grads_start"))
        return mix_flight[4]

    (loss_loc, grad_x, dw_in_t, dw_out_t, _, _, dw_blk, dconv,
     sums_in, sums_mix, sums_ffn, sums_pool) = _sequence_step(
        xs, target, rope, (sh_m + w_token[0:1, 0:1], sc_m, gt_m, sh_f, sc_f, gt_f),
        (g_pre_mix, g_post_mix, g_pre_ffn, g_post_ffn),
        w_in_t, w_out_t, fetch_ffn, send_ffn_grads, send_mix_grads, w_blk_b, b_pool_r, pool_scale_r, conv_w_all, conv_b)

    parts_ffn = _exchange_wait("scatter", *flight[:4], grad_x, "ffn_grads_wait")
    big = {
        "w_up": [a.T for a in _sum_adam(parts_ffn[0], w_up[0].T, m_w_up[0].T, v_w_up[0].T, "adam_w_up", 64)],
        "w_down": _sum_adam(parts_ffn[1], w_down[0], m_w_down[0], v_w_down[0], "adam_w_down", 32),
    }

    rep_w = [b_ada, g_pre_mix, g_post_mix, g_pre_ffn, g_post_ffn, w_pool, b_pool, pool_scale, conv_b]
    rep_m = [m_b_ada, m_g_pre_mix, m_g_post_mix, m_g_pre_ffn, m_g_post_ffn, m_w_pool, m_b_pool, m_pool_scale, m_conv_b]
    rep_v = [v_b_ada, v_g_pre_mix, v_g_post_mix, v_g_pre_ffn, v_g_post_ffn, v_w_pool, v_b_pool, v_pool_scale, v_conv_b]
    rep_out, dmod_all, dconv_tot, loss_tot = _small_reduce_adam(
        sums_in, sums_mix, sums_ffn, sums_pool, dw_blk, dconv, loss_loc, rep_w, rep_m, rep_v)
    g_rep, d_rep, nm_rep, nv_rep = (rep_out[k::4] for k in range(4))

    fcol = d_ff // N_DEV
    g_cw = lax.dynamic_slice(dconv_tot, (0, me * fcol), (3, fcol))
    d_cw, nm_cw, nv_cw = _adam(conv_w[0], g_cw, m_conv_w[0], v_conv_w[0], "adam_conv_w", 3)

    dmod_cols = lax.dynamic_slice(dmod_all, (0, me * ncol), (N_DEV, ncol))
    g_ada, d_ada, nm_ada, nv_ada = _ada_grad_adam(c_all, dmod_cols, w_ada[0], m_w_ada[0], v_w_ada[0], 256)

    parts_mix = _exchange_wait("scatter", *mix_flight[:4], g_ada, "mix_grads_wait")
    big["w_in"] = [a.T for a in _sum_adam(parts_mix[0], w_in[0].T, m_w_in[0].T, v_w_in[0].T, "adam_w_in", 64)]
    big["w_out"] = [a.T for a in _sum_adam(parts_mix[1], w_out[0].T, m_w_out[0].T, v_w_out[0].T, "adam_w_out", 128)]

    loss = loss_tot[0, 0]

    def group(k):
        rep = (g_rep, d_rep, nm_rep, nv_rep)[k]
        ada = (g_ada, d_ada, nm_ada, nv_ada)[k][None]
        cw = (g_cw, d_cw, nm_cw, nv_cw)[k][None]
        return [ada, rep[0], rep[1], rep[2], rep[3], rep[4], big["w_in"][k][None], rep[5], rep[6], rep[7],
                big["w_out"][k][None], big["w_up"][k][None], cw, rep[8], big["w_down"][k][None]]

    return (loss, grad_x[None], *group(0), *group(1), *group(2), *group(3))
```

```python
import functools
import math

import jax
import jax.numpy as jnp
from jax import lax
from jax.experimental import pallas as pl
from jax.experimental.pallas import tpu as pltpu

F32 = jnp.float32
BF16 = jnp.bfloat16
MESH = pl.DeviceIdType.MESH

N_DEV = 8
HEAD_DIM = 64
ROT_HALF = 8
ROPE_THETA = 500000.0
POOL_WINDOWS = (2, 4, 8, 16)
DILATIONS = (1, 4, 16)
BLOCK = 128
NORM_EPS = 1e-6
HALO = 16
MASKED = -1e30
ATTN_FWD_UNROLL = 4
ATTN_BWD_UNROLL = 2

ADAM_LR = 0.001
ADAM_B1 = 0.9
ADAM_B2 = 0.999
ADAM_EPS = 1e-08
ADAM_WD = 0.01
ADAM_STEP = 10

V7X_VMEM_LIMIT = 56 * 1024 * 1024
LANES = 128

NT = (((1,), (1,)), ((), ()))
NN = (((1,), (0,)), ((), ()))
TN = (((0,), (0,)), ((), ()))


def _dot(a, b, dims):
    return lax.dot_general(a, b, dims, preferred_element_type=F32)


def _params(sem=None, vmem=V7X_VMEM_LIMIT):
    if sem is None:
        return pltpu.CompilerParams(vmem_limit_bytes=vmem)
    return pltpu.CompilerParams(dimension_semantics=sem, vmem_limit_bytes=vmem)


def _rstd(v):
    return lax.rsqrt(jnp.mean(v * v, axis=-1, keepdims=True) + NORM_EPS)


def _norm_bwd(dn, n, rstd):
    return rstd * (dn - n * jnp.mean(dn * n, axis=-1, keepdims=True))


def _rope_fwd(p, rope_ref):
    return p * rope_ref[0] + pltpu.roll(p, LANES - ROT_HALF, 1) * rope_ref[1] + pltpu.roll(p, ROT_HALF, 1) * rope_ref[2]


def _rope_bwd(dp, rope_ref):
    return dp * rope_ref[0] + pltpu.roll(dp * rope_ref[1], ROT_HALF, 1) + pltpu.roll(dp * rope_ref[2], LANES - ROT_HALF, 1)


def _gelu_parts(v):
    k2 = 2.0 * math.sqrt(2.0 / math.pi)
    c = 0.044715
    v2 = v * v
    s = jax.nn.sigmoid(v * (k2 + (k2 * c) * v2))
    g = v * s
    dg = s + g * (1.0 - s) * (k2 + (3.0 * k2 * c) * v2)
    return g, dg


def _halo_before(i, tile):
    return jnp.maximum(i * (tile // HALO) - 1, 0)


def _premix_inproj(x, sh, sc, g, w_in_t, rope, tm):
    s_len, d = x.shape
    n_proj = w_in_t.shape[0]
    n_slab = (n_proj - 256) // LANES

    def body(x_ref, sh_ref, sc_ref, g_ref, w_ref, rope_ref, h_ref, up_ref, qkv_ref):
        xv = x_ref[...]
        h = (xv * _rstd(xv) * g_ref[...]) * (1.0 + sc_ref[...]) + sh_ref[...]
        hb = h.astype(BF16)
        h_ref[...] = hb
        up_ref[...] = _dot(hb, w_ref[0:256, :], NT)
        for pair in range(n_slab // 2):
            p = _dot(hb, w_ref[256 + 256 * pair:512 + 256 * pair, :], NT)
            for half in range(2):
                ph = p[:, half * LANES:(half + 1) * LANES]
                if pair < 6:
                    ph = _rope_fwd(ph, rope_ref)
                if pair < 3:
                    ph = ph * (HEAD_DIM ** -0.5)
                qkv_ref[2 * pair + half] = ph

    vec = pl.BlockSpec((1, d), lambda i: (0, 0))
    return pl.pallas_call(
        body, name="premix_inproj", grid=(s_len // tm,),
        in_specs=[pl.BlockSpec((tm, d), lambda i: (i, 0)), vec, vec, vec,
                  pl.BlockSpec((n_proj, d), lambda i: (0, 0)),
                  pl.BlockSpec((3, tm, LANES), lambda i: (0, i, 0))],
        out_specs=[pl.BlockSpec((tm, d), lambda i: (i, 0)),
                   pl.BlockSpec((tm, 256), lambda i: (i, 0)),
                   pl.BlockSpec((n_slab, tm, LANES), lambda i: (0, i, 0))],
        out_shape=[jax.ShapeDtypeStruct((s_len, d), BF16),
                   jax.ShapeDtypeStruct((s_len, 256), F32),
                   jax.ShapeDtypeStruct((n_slab, s_len, LANES), F32)],
        compiler_params=_params(("arbitrary",)),
    )(x, sh, sc, g, w_in_t, rope)


def _block_rows(n, r, dil):
    start = n * (BLOCK * dil) + r
    if dil == 1:
        return pl.ds(pl.multiple_of(start, BLOCK), BLOCK)
    return pl.ds(start, BLOCK, stride=dil)


def _band_mask(n):
    ri = lax.broadcasted_iota(jnp.int32, (BLOCK, 2 * BLOCK), 0)
    cj = lax.broadcasted_iota(jnp.int32, (BLOCK, 2 * BLOCK), 1)
    cur = (cj >= BLOCK) & (cj - BLOCK <= ri)
    prev = (cj < BLOCK) & (cj >= ri) & (n > 0)
    return cur | prev


def _attn_fwd(qkv, group, dil):
    s_len = qkv.shape[1]
    nb = s_len // (BLOCK * dil)

    def body(q_ref, k_ref, v_ref, o_ref, lse_ref):
        lane = lax.broadcasted_iota(jnp.int32, (BLOCK, LANES), 1)
        first = lane < HEAD_DIM

        def block(t, carry):
            r, n = t // nb, t % nb
            cur = _block_rows(n, r, dil)
            prev = _block_rows(jnp.maximum(n - 1, 0), r, dil)
            q = q_ref[0, cur, :]
            kcat = jnp.concatenate([k_ref[0, prev, :], k_ref[0, cur, :]], axis=0).astype(BF16)
            vcat = jnp.concatenate([v_ref[0, prev, :], v_ref[0, cur, :]], axis=0).astype(BF16)
            valid = _band_mask(n)
            q2 = jnp.concatenate([jnp.where(first, q, 0.0), jnp.where(first, 0.0, q)], axis=0).astype(BF16)
            s = jnp.where(jnp.concatenate([valid, valid], axis=0), _dot(q2, kcat, NT), MASKED)
            m = jnp.max(s, axis=-1, keepdims=True)
            p = jnp.exp(s - m)
            den = jnp.sum(p, axis=-1, keepdims=True)
            o2 = _dot(p.astype(BF16), vcat, NN) / den
            lse2 = m + jnp.log(den)
            o_ref[0, cur, :] = jnp.where(first, o2[:BLOCK], o2[BLOCK:])
            lse_ref[0, cur, :] = jnp.where(first, lse2[:BLOCK], lse2[BLOCK:])
            return carry

        lax.fori_loop(0, nb * dil, block, 0, unroll=ATTN_FWD_UNROLL)

    def slab(base):
        return pl.BlockSpec((1, s_len, LANES), lambda s: (base + 2 * group + s, 0, 0))

    out = pl.BlockSpec((1, s_len, LANES), lambda s: (s, 0, 0))
    shape = jax.ShapeDtypeStruct((2, s_len, LANES), F32)
    return pl.pallas_call(
        body, name=f"attn_fwd_d{dil}", grid=(2,),
        in_specs=[slab(0), slab(6), slab(12)], out_specs=[out, out], out_shape=[shape, shape],
        compiler_params=_params(("arbitrary",)),
    )(qkv, qkv, qkv)


def _pool_mixed(u, halo, i, tm):
    ue = jnp.concatenate([halo, u], axis=0)
    s2 = ue + pltpu.roll(ue, 1, 0)
    s4 = s2 + pltpu.roll(s2, 2, 0)
    s8 = s4 + pltpu.roll(s4, 4, 0)
    s16 = s8 + pltpu.roll(s8, 8, 0)
    grp = lax.broadcasted_iota(jnp.int32, (tm, 256), 1) // HEAD_DIM
    pick = lambda a, b, c, e: jnp.where(grp == 0, a, jnp.where(grp == 1, b, jnp.where(grp == 2, c, e)))
    win_sum = pick(s2[HALO:], s4[HALO:], s8[HALO:], s16[HALO:])
    pos = (i * tm + lax.broadcasted_iota(jnp.int32, (tm, 256), 0)).astype(F32)
    count = jnp.minimum(pos + 1.0, pick(*[float(w) for w in POOL_WINDOWS]))
    return win_sum / count - u, count


def _mix_out(x, u_pool, o_g, lse_g, w_blk, b_pool, pool_scale, w_out_t, gt_m, g_post_mix, g_pre_ffn, sc_f, sh_f, tm):
    s_len, d = x.shape

    def body(x_ref, u_ref, uh_ref, o0, o1, o2, l0, l1, l2, wb_ref, bp_ref, ps_ref, wo_ref,
             gt_ref, g1_ref, g2_ref, sc_ref, sh_ref,
             x1_ref, y1_ref, h2_ref, cat_ref, attn_ref, lall_ref):
        i = pl.program_id(0)
        u = u_ref[...]
        halo = uh_ref[...] * (i > 0).astype(F32)
        mixed, _ = _pool_mixed(u, halo, i, tm)
        y = _dot(mixed.astype(BF16), wb_ref[...], NN) + bp_ref[...]
        pool = y * ps_ref[...]
        attn = []
        for s in range(2):
            la, lb, lc = l0[s], l1[s], l2[s]
            mx = jnp.maximum(jnp.maximum(la, lb), lc)
            ea, eb, ec = jnp.exp(la - mx), jnp.exp(lb - mx), jnp.exp(lc - mx)
            den = ea + eb + ec
            lall_ref[s] = mx + jnp.log(den)
            attn.append((ea / den) * o0[s] + (eb / den) * o1[s] + (ec / den) * o2[s])
        attn = jnp.concatenate(attn, axis=1)
        attn_ref[...] = attn
        cat = jnp.concatenate([pool, attn], axis=1).astype(BF16)
        cat_ref[...] = cat
        y1 = _dot(cat, wo_ref[...], NT)
        y1_ref[...] = y1
        x1 = x_ref[...] + gt_ref[...] * (y1 * _rstd(y1) * g1_ref[...])
        x1_ref[...] = x1
        h2 = (x1 * _rstd(x1) * g2_ref[...]) * (1.0 + sc_ref[...]) + sh_ref[...]
        h2_ref[...] = h2.astype(BF16)

    tile = lambda w: pl.BlockSpec((tm, w), lambda i: (i, 0))
    slab = pl.BlockSpec((2, tm, LANES), lambda i: (0, i, 0))
    const = lambda a: pl.BlockSpec(a.shape, lambda i: (0,) * a.ndim)
    return pl.pallas_call(
        body, name="mix_out", grid=(s_len // tm,),
        in_specs=[tile(d), tile(256), pl.BlockSpec((HALO, 256), lambda i: (_halo_before(i, tm), 0)),
                  slab, slab, slab, slab, slab, slab,
                  const(w_blk), const(b_pool), const(pool_scale), const(w_out_t),
                  const(gt_m), const(g_post_mix), const(g_pre_ffn), const(sc_f), const(sh_f)],
        out_specs=[tile(d), tile(d), tile(d), tile(512), tile(256), slab],
        out_shape=[jax.ShapeDtypeStruct((s_len, d), F32), jax.ShapeDtypeStruct((s_len, d), F32),
                   jax.ShapeDtypeStruct((s_len, d), BF16), jax.ShapeDtypeStruct((s_len, 512), BF16),
                   jax.ShapeDtypeStruct((s_len, 256), F32), jax.ShapeDtypeStruct((2, s_len, LANES), F32)],
        compiler_params=_params(("arbitrary",)),
    )(x, u_pool, u_pool, *o_g, *lse_g, w_blk, b_pool, pool_scale, w_out_t, gt_m, g_post_mix, g_pre_ffn, sc_f, sh_f)


def _conv_gate(gate_ext, cw, cb):
    gc = gate_ext * cw[2:3, :] + pltpu.roll(gate_ext, 1, 0) * cw[1:2, :] + pltpu.roll(gate_ext, 2, 0) * cw[0:1, :]
    return gc[HALO:] + cb


def _ffn_fwd_loss(h2, x1, target, w_up_t, w_down, conv_w, conv_b, gt_f, g_post_ffn, tm, tf, ck):
    s_len, d = x1.shape
    d_ff = w_down.shape[0]
    n_f = d_ff // tf

    def body(h_ref, hh_ref, x1_ref, tgt_ref, wg_ref, wv_ref, wd_ref, cw_ref, cb_ref, gt_ref, g_ref,
             gate_ref, a_ref, act_ref, vd_ref, dy2_ref, dout_ref, sums_ref, loss_ref, acc_ref):
        i, j = pl.program_id(0), pl.program_id(1)

        @pl.when((i == 0) & (j == 0))
        def _():
            sums_ref[...] = jnp.zeros_like(sums_ref)
            loss_ref[...] = jnp.zeros_like(loss_ref)

        h = h_ref[...]
        h_ext = jnp.concatenate([hh_ref[...], h], axis=0)
        row = lax.broadcasted_iota(jnp.int32, (tm + HALO, ck), 0)
        no_halo = (row < HALO) & (i == 0)
        part = None
        for c in range(tf // ck):
            cs = slice(c * ck, (c + 1) * ck)
            gate_ext = jnp.where(no_halo, 0.0, _dot(h_ext, wg_ref[cs, :], NT))
            val = _dot(h, wv_ref[cs, :], NT)
            act, dact = _gelu_parts(_conv_gate(gate_ext, cw_ref[:, cs], cb_ref[:, cs]))
            a = (act * val).astype(BF16)
            gate_ref[:, cs] = gate_ext[HALO:].astype(BF16)
            a_ref[:, cs] = a
            act_ref[:, cs] = act.astype(BF16)
            vd_ref[:, cs] = (val * dact).astype(BF16)
            p = _dot(a, wd_ref[cs, :], NN)
            part = p if part is None else part + p

        @pl.when(j == 0)
        def _():
            acc_ref[...] = part

        @pl.when(j > 0)
        def _():
            acc_ref[...] += part

        @pl.when(j == n_f - 1)
        def _():
            y2 = acc_ref[...]
            rstd = _rstd(y2)
            n = y2 * rstd
            rn = n * g_ref[...]
            err = x1_ref[...] + gt_ref[...] * rn - tgt_ref[...]
            loss_ref[...] += 0.5 * jnp.sum(jnp.mean(err * err, axis=-1, keepdims=True), axis=0, keepdims=True)
            dout = err * (1.0 / d)
            dout_ref[...] = dout
            drn = dout * gt_ref[...]
            sums_ref[0:1, :] += jnp.sum(dout * rn, axis=0, keepdims=True)
            sums_ref[1:2, :] += jnp.sum(drn * n, axis=0, keepdims=True)
            dy2_ref[...] = _norm_bwd(drn * g_ref[...], n, rstd).astype(BF16)

    tok = lambda w: pl.BlockSpec((tm, w), lambda i, j: (i, 0))
    tokf = pl.BlockSpec((tm, tf), lambda i, j: (i, j))
    vec = pl.BlockSpec((1, d), lambda i, j: (0, 0))
    once = {"pipeline_mode": pl.Buffered(1)} if n_f == 1 else {}
    return pl.pallas_call(
        body, name="ffn_fwd_loss", grid=(s_len // tm, n_f),
        in_specs=[tok(d), pl.BlockSpec((HALO, d), lambda i, j: (_halo_before(i, tm), 0)), tok(d), tok(d),
                  pl.BlockSpec((tf, d), lambda i, j: (j, 0), **once),
                  pl.BlockSpec((tf, d), lambda i, j: (j + n_f, 0), **once),
                  pl.BlockSpec((tf, d), lambda i, j: (j, 0), **once),
                  pl.BlockSpec((3, tf), lambda i, j: (0, j)), pl.BlockSpec((1, tf), lambda i, j: (0, j)), vec, vec],
        out_specs=[tokf, tokf, tokf, tokf, tok(d), tok(d), pl.BlockSpec((8, d), lambda i, j: (0, 0)),
                   pl.BlockSpec((8, LANES), lambda i, j: (0, 0))],
        out_shape=[jax.ShapeDtypeStruct((s_len, d_ff), BF16)] * 4
        + [jax.ShapeDtypeStruct((s_len, d), BF16), jax.ShapeDtypeStruct((s_len, d), F32),
                   jax.ShapeDtypeStruct((8, d), F32), jax.ShapeDtypeStruct((8, LANES), F32)],
        scratch_shapes=[pltpu.VMEM((tm, d), F32)],
        compiler_params=_params(("arbitrary", "arbitrary")),
    )(h2, h2, x1, target, w_up_t, w_up_t, w_down, conv_w, conv_b, gt_f, g_post_ffn)


def _ffn_bwd_act(dy2, gate, a, act, vd, w_down, tm, tf):
    s_len, d = dy2.shape
    d_ff = w_down.shape[0]
    n_t = s_len // tm

    def body(dy_ref, g_ref, gh_ref, a_ref, act_ref, vd_ref, wd_ref, dgc_ref, dval_ref, dwd_ref, dconv_ref, acc_ref):
        i = pl.program_id(1)
        gate_ext = jnp.concatenate([gh_ref[...], g_ref[...]], axis=0).astype(F32)
        row = lax.broadcasted_iota(jnp.int32, gate_ext.shape, 0)
        gate_ext = jnp.where((row < HALO) & (i == 0), 0.0, gate_ext)
        da = _dot(dy_ref[...], wd_ref[...], NT)
        dgc = da * vd_ref[...].astype(F32)
        dgc_ref[...] = dgc.astype(BF16)
        dval_ref[...] = (da * act_ref[...].astype(F32)).astype(BF16)
        dwd = _dot(a_ref[...], dy_ref[...], TN)
        rows = [jnp.sum(dgc * pltpu.roll(gate_ext, 2 - k, 0)[HALO:], axis=0, keepdims=True) for k in range(2)]
        rows += [jnp.sum(dgc * gate_ext[HALO:], axis=0, keepdims=True), jnp.sum(dgc, axis=0, keepdims=True),
                 jnp.zeros((4, tf), F32)]
        dconv = jnp.concatenate(rows, axis=0)

        @pl.when(i == 0)
        def _():
            acc_ref[...] = dwd
            dconv_ref[...] = dconv

        @pl.when(i > 0)
        def _():
            acc_ref[...] += dwd
            dconv_ref[...] += dconv

        @pl.when(i == n_t - 1)
        def _():
            dwd_ref[...] = acc_ref[...].astype(BF16)

    tokf = pl.BlockSpec((tm, tf), lambda j, i: (i, j))
    return pl.pallas_call(
        body, name="ffn_bwd_act", grid=(d_ff // tf, n_t),
        in_specs=[pl.BlockSpec((tm, d), lambda j, i: (i, 0)), tokf,
                  pl.BlockSpec((HALO, tf), lambda j, i: (_halo_before(i, tm), j)), tokf, tokf, tokf,
                  pl.BlockSpec((tf, d), lambda j, i: (j, 0))],
        out_specs=[tokf, tokf, pl.BlockSpec((tf, d), lambda j, i: (j, 0)), pl.BlockSpec((8, tf), lambda j, i: (0, j))],
        out_shape=[jax.ShapeDtypeStruct((s_len, d_ff), BF16), jax.ShapeDtypeStruct((s_len, d_ff), BF16),
                   jax.ShapeDtypeStruct((d_ff, d), BF16), jax.ShapeDtypeStruct((8, d_ff), F32)],
        scratch_shapes=[pltpu.VMEM((tf, d), F32)],
        compiler_params=_params(("arbitrary", "arbitrary")),
    )(dy2, gate, gate, a, act, vd, w_down)


def _ffn_bwd_up(dgc, dval, w_up_t, conv_w, tm):
    s_len, d_ff = dgc.shape
    d = w_up_t.shape[1]
    n_t = s_len // tm

    def body(dg_ref, dgn_ref, dv_ref, cw_ref, w_ref, dup_ref, dh_ref):
        i = pl.program_id(0)
        nxt = dgn_ref[...].astype(F32) * (i < n_t - 1).astype(F32)
        ext = jnp.concatenate([dg_ref[...].astype(F32), nxt], axis=0)
        rows = tm + HALO
        dgate = (ext * cw_ref[2:3, :] + pltpu.roll(ext, rows - 1, 0) * cw_ref[1:2, :]
                 + pltpu.roll(ext, rows - 2, 0) * cw_ref[0:1, :])[:tm]
        dup = jnp.concatenate([dgate.astype(BF16), dv_ref[...]], axis=1)
        dup_ref[...] = dup
        dh_ref[...] = _dot(dup, w_ref[...], NN)

    tokf = pl.BlockSpec((tm, d_ff), lambda i: (i, 0))
    return pl.pallas_call(
        body, name="ffn_bwd_up", grid=(n_t,),
        in_specs=[tokf, pl.BlockSpec((HALO, d_ff), lambda i: (jnp.minimum((i + 1) * (tm // HALO), s_len // HALO - 1), 0)),
                  tokf, pl.BlockSpec((3, d_ff), lambda i: (0, 0)), pl.BlockSpec((2 * d_ff, d), lambda i: (0, 0))],
        out_specs=[pl.BlockSpec((tm, 2 * d_ff), lambda i: (i, 0)), pl.BlockSpec((tm, d), lambda i: (i, 0))],
        out_shape=[jax.ShapeDtypeStruct((s_len, 2 * d_ff), BF16), jax.ShapeDtypeStruct((s_len, d), F32)],
        compiler_params=_params(("arbitrary",)),
    )(dgc, dgc, dval, conv_w, w_up_t)


def _mix_bwd(dh2, dout, x1, y1, cat, attn, w_out_t, sc_f, g_pre_ffn, gt_m, g_post_mix, tm):
    s_len, d = x1.shape
    n_t = s_len // tm

    def body(dh_ref, do_ref, x1_ref, y1_ref, cat_ref, at_ref, wo_ref, sc_ref, g2_ref, gt_ref, g1_ref,
             dx1_ref, dpool_ref, dattn_ref, delta_ref, dwo_ref, sums_ref, acc_ref):
        i = pl.program_id(0)
        dh = dh_ref[...]
        x1 = x1_ref[...]
        r2 = _rstd(x1)
        n2 = x1 * r2
        ng = n2 * g2_ref[...]
        dng = dh * (1.0 + sc_ref[...])
        dx1 = do_ref[...] + _norm_bwd(dng * g2_ref[...], n2, r2)
        dx1_ref[...] = dx1
        y1 = y1_ref[...]
        r1 = _rstd(y1)
        n1 = y1 * r1
        drn = dx1 * gt_ref[...]
        dy1 = _norm_bwd(drn * g1_ref[...], n1, r1).astype(BF16)
        dcat = _dot(dy1, wo_ref[...], NN)
        dpool_ref[...] = dcat[:, 0:256]
        lane = lax.broadcasted_iota(jnp.int32, (tm, LANES), 1)
        first = lane < HEAD_DIM
        for s in range(2):
            da = dcat[:, 256 + s * LANES:256 + (s + 1) * LANES]
            dattn_ref[s] = da
            prod = da * at_ref[:, s * LANES:(s + 1) * LANES]
            tot = jnp.sum(prod, axis=-1, keepdims=True)
            lo = jnp.sum(jnp.where(first, prod, 0.0), axis=-1, keepdims=True)
            delta_ref[s] = jnp.where(first, lo, tot - lo)
        dwo = _dot(dy1, cat_ref[...], TN)
        sums = jnp.concatenate(
            [jnp.sum(dh, axis=0, keepdims=True), jnp.sum(dh * ng, axis=0, keepdims=True),
             jnp.sum(dng * n2, axis=0, keepdims=True), jnp.sum(dx1 * (n1 * g1_ref[...]), axis=0, keepdims=True),
             jnp.sum(drn * n1, axis=0, keepdims=True), jnp.zeros((3, d), F32)], axis=0)

        @pl.when(i == 0)
        def _():
            acc_ref[...] = dwo
            sums_ref[...] = sums

        @pl.when(i > 0)
        def _():
            acc_ref[...] += dwo
            sums_ref[...] += sums

        @pl.when(i == n_t - 1)
        def _():
            dwo_ref[...] = acc_ref[...].astype(BF16)

    tile = lambda w: pl.BlockSpec((tm, w), lambda i: (i, 0))
    slab = pl.BlockSpec((2, tm, LANES), lambda i: (0, i, 0))
    vec = pl.BlockSpec((1, d), lambda i: (0, 0))
    return pl.pallas_call(
        body, name="mix_bwd", grid=(n_t,),
        in_specs=[tile(d), tile(d), tile(d), tile(d), tile(512), tile(256),
                  pl.BlockSpec((d, 512), lambda i: (0, 0)), vec, vec, vec, vec],
        out_specs=[tile(d), tile(256), slab, slab, pl.BlockSpec((d, 512), lambda i: (0, 0)),
                   pl.BlockSpec((8, d), lambda i: (0, 0))],
        out_shape=[jax.ShapeDtypeStruct((s_len, d), F32), jax.ShapeDtypeStruct((s_len, 256), F32),
                   jax.ShapeDtypeStruct((2, s_len, LANES), F32), jax.ShapeDtypeStruct((2, s_len, LANES), F32),
                   jax.ShapeDtypeStruct((d, 512), BF16), jax.ShapeDtypeStruct((8, d), F32)],
        scratch_shapes=[pltpu.VMEM((d, 512), F32)],
        compiler_params=_params(("arbitrary",)),
    )(dh2, dout, x1, y1, cat, attn, w_out_t, sc_f, g_pre_ffn, gt_m, g_post_mix)


def _pool_bwd(dpool, u_pool, w_blk, b_pool, pool_scale, tm):
    s_len = dpool.shape[0]
    n_t = s_len // tm

    def body(dp_ref, dpn_ref, u_ref, uh_ref, wb_ref, bp_ref, ps_ref, du_ref, dwb_ref, sums_ref):
        i = pl.program_id(0)
        u = u_ref[...]
        mixed, _ = _pool_mixed(u, uh_ref[...] * (i > 0).astype(F32), i, tm)
        mixed_b = mixed.astype(BF16)
        y = _dot(mixed_b, wb_ref[...], NN) + bp_ref[...]
        dp = dp_ref[...]
        dy = dp * ps_ref[...]
        dwb = _dot(mixed_b, dy.astype(BF16), TN)
        sums = jnp.concatenate([jnp.sum(dy, axis=0, keepdims=True), jnp.sum(dp * y, axis=0, keepdims=True),
                                jnp.zeros((6, 256), F32)], axis=0)
        dp_ext = jnp.concatenate([dp, dpn_ref[...] * (i < n_t - 1).astype(F32)], axis=0)
        dmix = _dot((dp_ext * ps_ref[...]).astype(BF16), wb_ref[...], NT)
        rows = tm + HALO
        grp = lax.broadcasted_iota(jnp.int32, (rows, 256), 1) // HEAD_DIM
        pick = lambda a, b, c, e: jnp.where(grp == 0, a, jnp.where(grp == 1, b, jnp.where(grp == 2, c, e)))
        pos = (i * tm + lax.broadcasted_iota(jnp.int32, (rows, 256), 0)).astype(F32)
        z = dmix / jnp.minimum(pos + 1.0, pick(*[float(w) for w in POOL_WINDOWS]))
        f2 = z + pltpu.roll(z, rows - 1, 0)
        f4 = f2 + pltpu.roll(f2, rows - 2, 0)
        f8 = f4 + pltpu.roll(f4, rows - 4, 0)
        f16 = f8 + pltpu.roll(f8, rows - 8, 0)
        du_ref[...] = (pick(f2, f4, f8, f16) - dmix)[:tm]

        @pl.when(i == 0)
        def _():
            dwb_ref[...] = dwb
            sums_ref[...] = sums

        @pl.when(i > 0)
        def _():
            dwb_ref[...] += dwb
            sums_ref[...] += sums

    tile = pl.BlockSpec((tm, 256), lambda i: (i, 0))
    const = lambda a: pl.BlockSpec(a.shape, lambda i: (0,) * a.ndim)
    return pl.pallas_call(
        body, name="pool_bwd", grid=(n_t,),
        in_specs=[tile, pl.BlockSpec((HALO, 256), lambda i: (jnp.minimum((i + 1) * (tm // HALO), s_len // HALO - 1), 0)),
                  tile, pl.BlockSpec((HALO, 256), lambda i: (_halo_before(i, tm), 0)),
                  const(w_blk), const(b_pool), const(pool_scale)],
        out_specs=[tile, pl.BlockSpec((256, 256), lambda i: (0, 0)), pl.BlockSpec((8, 256), lambda i: (0, 0))],
        out_shape=[jax.ShapeDtypeStruct((s_len, 256), F32), jax.ShapeDtypeStruct((256, 256), F32),
                   jax.ShapeDtypeStruct((8, 256), F32)],
        compiler_params=_params(("arbitrary",)),
    )(dpool, dpool, u_pool, u_pool, w_blk, b_pool, pool_scale)


def _attn_bwd(qkv, dattn, lse_all, delta, group, dil):
    s_len = qkv.shape[1]
    nb = s_len // (BLOCK * dil)

    def body(q_ref, k_ref, v_ref, do_ref, l_ref, dl_ref, dq_ref, dk_ref, dv_ref):
        lane = lax.broadcasted_iota(jnp.int32, (BLOCK, LANES), 1)
        first = lane < HEAD_DIM

        def block(t, carry):
            dk_part, dv_part = carry
            r, n = t // nb, t % nb
            cur = _block_rows(n, r, dil)
            prev = _block_rows(jnp.maximum(n - 1, 0), r, dil)
            q = q_ref[0, cur, :]
            do = do_ref[0, cur, :]
            lse = l_ref[0, cur, :]
            dlt = dl_ref[0, cur, :]
            kcat = jnp.concatenate([k_ref[0, prev, :], k_ref[0, cur, :]], axis=0).astype(BF16)
            vcat = jnp.concatenate([v_ref[0, prev, :], v_ref[0, cur, :]], axis=0).astype(BF16)
            valid = _band_mask(n)
            stack = lambda a: jnp.concatenate([jnp.where(first, a, 0.0), jnp.where(first, 0.0, a)], axis=0)
            rows2 = lambda a: jnp.concatenate([a[:, 0:1], a[:, HEAD_DIM:HEAD_DIM + 1]], axis=0)
            q2, do2 = stack(q).astype(BF16), stack(do).astype(BF16)
            valid2 = jnp.concatenate([valid, valid], axis=0)
            p = jnp.where(valid2, jnp.exp(_dot(q2, kcat, NT) - rows2(lse)), 0.0)
            ds = (p * (_dot(do2, vcat, NT) - rows2(dlt))).astype(BF16)
            dq2 = _dot(ds, kcat, NN)
            dq_ref[0, cur, :] = jnp.where(first, dq2[:BLOCK], dq2[BLOCK:])
            dkc = _dot(ds, q2, TN)
            dvc = _dot(p.astype(BF16), do2, TN)
            dk_ref[0, prev, :] = dk_part + dkc[:BLOCK]
            dv_ref[0, prev, :] = dv_part + dvc[:BLOCK]
            dk_ref[0, cur, :] = dkc[BLOCK:]
            dv_ref[0, cur, :] = dvc[BLOCK:]
            return dkc[BLOCK:], dvc[BLOCK:]

        def blocks(tt, carry):
            for u in range(ATTN_BWD_UNROLL):
                carry = block(tt * ATTN_BWD_UNROLL + u, carry)
            return carry

        zero = jnp.zeros((BLOCK, LANES), F32)
        lax.fori_loop(0, nb * dil // ATTN_BWD_UNROLL, blocks, (zero, zero))

    def slab(base):
        return pl.BlockSpec((1, s_len, LANES), lambda s: (base + 2 * group + s, 0, 0))

    one = pl.BlockSpec((1, s_len, LANES), lambda s: (s, 0, 0))
    shape = jax.ShapeDtypeStruct((2, s_len, LANES), F32)
    return pl.pallas_call(
        body, name=f"attn_bwd_d{dil}", grid=(2,),
        in_specs=[slab(0), slab(6), slab(12), one, one, one],
        out_specs=[one, one, one], out_shape=[shape, shape, shape],
        compiler_params=_params(("arbitrary",)),
    )(qkv, qkv, qkv, dattn, lse_all, delta)


def _dproj_assemble(du, dqkv, rope, tm):
    s_len = du.shape[0]
    n_proj = 256 + 18 * LANES

    def body(du_ref, *refs):
        dref, rope_ref, dproj_ref = refs[:9], refs[9], refs[10]
        dproj_ref[:, 0:256] = du_ref[...].astype(BF16)
        col = 256
        for kind in range(3):
            for grp in range(3):
                for s in range(2):
                    piece = dref[3 * grp + kind][s]
                    if kind < 2:
                        piece = _rope_bwd(piece, rope_ref)
                    if kind == 0:
                        piece = piece * (HEAD_DIM ** -0.5)
                    dproj_ref[:, col:col + LANES] = piece.astype(BF16)
                    col += LANES

    slab = pl.BlockSpec((2, tm, LANES), lambda i: (0, i, 0))
    return pl.pallas_call(
        body, name="dproj_assemble", grid=(s_len // tm,),
        in_specs=[pl.BlockSpec((tm, 256), lambda i: (i, 0))] + [slab] * 9 + [pl.BlockSpec((3, tm, LANES), lambda i: (0, i, 0))],
        out_specs=pl.BlockSpec((tm, n_proj), lambda i: (i, 0)),
        out_shape=jax.ShapeDtypeStruct((s_len, n_proj), BF16),
        compiler_params=_params(("arbitrary",)),
    )(du, *dqkv, rope)


def _inproj_bwd(dproj, w_in_t, x, dx1, sc_m, g_pre_mix, tm):
    s_len, d = x.shape
    n_proj = w_in_t.shape[0]
    n_t = s_len // tm

    def body(dproj_ref, w_ref, x_ref, dx1_ref, sc_ref, g_ref, dx_ref, sums_ref):
        i = pl.program_id(0)
        dh = _dot(dproj_ref[...], w_ref[...], NN)
        xv = x_ref[...]
        r = _rstd(xv)
        n = xv * r
        dng = dh * (1.0 + sc_ref[...])
        dx_ref[...] = dx1_ref[...] + _norm_bwd(dng * g_ref[...], n, r)
        sums = jnp.concatenate([jnp.sum(dh, axis=0, keepdims=True), jnp.sum(dh * (n * g_ref[...]), axis=0, keepdims=True),
                                jnp.sum(dng * n, axis=0, keepdims=True), jnp.zeros((5, d), F32)], axis=0)

        @pl.when(i == 0)
        def _():
            sums_ref[...] = sums

        @pl.when(i > 0)
        def _():
            sums_ref[...] += sums

    tile = lambda w: pl.BlockSpec((tm, w), lambda i: (i, 0))
    vec = pl.BlockSpec((1, d), lambda i: (0, 0))
    return pl.pallas_call(
        body, name="inproj_bwd", grid=(n_t,),
        in_specs=[tile(n_proj), pl.BlockSpec((n_proj, d), lambda i: (0, 0)), tile(d), tile(d), vec, vec],
        out_specs=[tile(d), pl.BlockSpec((8, d), lambda i: (0, 0))],
        out_shape=[jax.ShapeDtypeStruct((s_len, d), F32), jax.ShapeDtypeStruct((8, d), F32)],
        compiler_params=_params(("arbitrary",)),
    )(dproj, w_in_t, x, dx1, sc_m, g_pre_mix)


def _wgrad(a, b, name, tk, tmm):
    s_len, m = a.shape
    n = b.shape[1]
    n_k = s_len // tk

    def body(a_ref, b_ref, o_ref, acc_ref):
        k = pl.program_id(1)
        part = _dot(a_ref[...], b_ref[...], TN)

        @pl.when(k == 0)
        def _():
            acc_ref[...] = part

        @pl.when(k > 0)
        def _():
            acc_ref[...] += part

        @pl.when(k == n_k - 1)
        def _():
            o_ref[...] = acc_ref[...].astype(BF16)

    return pl.pallas_call(
        body, name=name, grid=(m // tmm, n_k),
        in_specs=[pl.BlockSpec((tk, tmm), lambda j, k: (k, j)), pl.BlockSpec((tk, n), lambda j, k: (k, 0))],
        out_specs=pl.BlockSpec((tmm, n), lambda j, k: (j, 0)),
        out_shape=jax.ShapeDtypeStruct((m, n), BF16),
        scratch_shapes=[pltpu.VMEM((tmm, n), F32)],
        compiler_params=_params(("arbitrary", "arbitrary")),
    )(a, b)


def _place():
    return lax.axis_index("x"), lax.axis_index("y"), lax.axis_index("c")


def _peer(k):
    x, y, c = _place()
    bx, by, bc = (k >> 2) & 1, (k >> 1) & 1, k & 1
    return (x ^ bx if bx else x, y ^ by if by else y, c ^ bc if bc else c)


def _index(pos):
    return 4 * pos[0] + 2 * pos[1] + pos[2]


def _ada_exchange(c_rows, w_ada, b_ada_cols, taps):
    d = c_rows.shape[1]
    ncol = w_ada.shape[1]

    def body(c_ref, w_ref, b_ref, t_ref, call_ref, mod_ref, tall_ref, stage_ref, send_sems, recv_sems):
        me = _index(_place())
        call_ref[me] = c_ref[...]
        tall_ref[me] = t_ref[...]

        def gather(k):
            return pltpu.make_async_remote_copy(
                src_ref=c_ref, dst_ref=call_ref.at[me], send_sem=send_sems.at[0, k - 1], recv_sem=recv_sems.at[0, k - 1],
                device_id=_peer(k), device_id_type=MESH)

        def gather_taps(k):
            return pltpu.make_async_remote_copy(
                src_ref=t_ref, dst_ref=tall_ref.at[me], send_sem=send_sems.at[2, k - 1], recv_sem=recv_sems.at[2, k - 1],
                device_id=_peer(k), device_id_type=MESH)

        for k in range(1, N_DEV):
            gather(k).start()
        for k in range(1, N_DEV):
            gather_taps(k).start()
        for k in range(1, N_DEV):
            gather(k).wait_recv()
        cv = jnp.concatenate([call_ref[b, 0:1, :] for b in range(N_DEV)], axis=0)
        act = cv * jax.nn.sigmoid(cv)
        mod = lax.dot_general(act, w_ref[...], NN, preferred_element_type=F32,
                              precision=lax.Precision.HIGHEST) + b_ref[...]
        for b in range(N_DEV):
            stage_ref[b] = jnp.broadcast_to(mod[b:b + 1, :], (8, ncol))
        mod_ref[me] = stage_ref[me]

        def scatter(k):
            return pltpu.make_async_remote_copy(
                src_ref=stage_ref.at[_index(_peer(k))], dst_ref=mod_ref.at[me],
                send_sem=send_sems.at[1, k - 1], recv_sem=recv_sems.at[1, k - 1],
                device_id=_peer(k), device_id_type=MESH)

        for k in range(1, N_DEV):
            scatter(k).start()
        for k in range(1, N_DEV):
            scatter(k).wait_recv()
        for k in range(1, N_DEV):
            gather_taps(k).wait_recv()
        for k in range(1, N_DEV):
            gather(k).wait_send()
            scatter(k).wait_send()
            gather_taps(k).wait_send()

    vmem = pl.BlockSpec(memory_space=pltpu.VMEM)
    return pl.pallas_call(
        body, name="ada_exchange",
        in_specs=[vmem] * 4, out_specs=[vmem] * 3,
        out_shape=[jax.ShapeDtypeStruct((N_DEV, 8, d), F32), jax.ShapeDtypeStruct((N_DEV, 8, ncol), F32),
                   jax.ShapeDtypeStruct((N_DEV,) + taps.shape, F32)],
        scratch_shapes=[pltpu.VMEM((N_DEV, 8, ncol), F32), pltpu.SemaphoreType.DMA((3, N_DEV - 1)),
                        pltpu.SemaphoreType.DMA((3, N_DEV - 1))],
        compiler_params=_params(),
    )(c_rows, w_ada, b_ada_cols, taps)


def _gather_weights(shards):
    n_w = len(shards)

    def body(*refs):
        srcs, outs = refs[:n_w], refs[n_w:2 * n_w]
        send_sems, recv_sems, local_sems = refs[2 * n_w:]
        x, y, c = _place()
        me, sibling = (x, y, c), (x, y, 1 - c)
        chips = [(1 - x, y), (x, 1 - y), (1 - x, 1 - y)]

        def rows(w, pos):
            r = shards[w].shape[0]
            return outs[w].at[pl.ds(pl.multiple_of(_index(pos) * r, 16), r), :]

        def copy(k, w, block, to, own=False):
            return pltpu.make_async_remote_copy(
                src_ref=srcs[w] if own else rows(w, block), dst_ref=rows(w, block),
                send_sem=send_sems.at[k, w], recv_sem=recv_sems.at[k, w], device_id=to, device_id_type=MESH)

        mine = [pltpu.make_async_copy(srcs[w], rows(w, me), local_sems.at[w]) for w in range(n_w)]
        for cp in mine:
            cp.start()
        first = [copy(0, w, me, sibling, own=True) for w in range(n_w)]
        first += [copy(1 + j, w, me, (*chip, c), own=True) for j, chip in enumerate(chips) for w in range(n_w)]
        for cp in first:
            cp.start()
        passed = []
        for j, chip in enumerate(chips):
            for w in range(n_w):
                copy(1 + j, w, (*chip, c), me).wait_recv()
                fwd = copy(4 + j, w, (*chip, c), sibling)
                fwd.start()
                passed.append(fwd)
        for w in range(n_w):
            copy(0, w, sibling, me).wait_recv()
        for j, chip in enumerate(chips):
            for w in range(n_w):
                copy(4 + j, w, (*chip, 1 - c), me).wait_recv()
        for cp in first + passed:
            cp.wait_send()
        for cp in mine:
            cp.wait()

    hbm = pl.BlockSpec(memory_space=pltpu.HBM)
    return pl.pallas_call(
        body, name="gather_weights",
        in_specs=[hbm] * n_w, out_specs=[hbm] * n_w,
        out_shape=[jax.ShapeDtypeStruct((N_DEV * s.shape[0], s.shape[1]), s.dtype) for s in shards],
        scratch_shapes=[pltpu.SemaphoreType.DMA((N_DEV - 1, n_w)), pltpu.SemaphoreType.DMA((N_DEV - 1, n_w)),
                        pltpu.SemaphoreType.DMA((n_w,))],
        compiler_params=_params(),
    )(*shards)


def _scatter_grads(grads):
    n_w = len(grads)

    def body(*refs):
        srcs, outs = refs[:n_w], refs[n_w:2 * n_w]
        send_sems, recv_sems, local_sems = refs[2 * n_w:]
        me = _index(_place())

        def slab(w, dev):
            r = grads[w].shape[0] // N_DEV
            return srcs[w].at[pl.ds(pl.multiple_of(dev * r, 16), r), :]

        def copy(k, w):
            return pltpu.make_async_remote_copy(
                src_ref=slab(w, _index(_peer(k))), dst_ref=outs[w].at[me],
                send_sem=send_sems.at[k - 1, w], recv_sem=recv_sems.at[k - 1, w],
                device_id=_peer(k), device_id_type=MESH)

        mine = [pltpu.make_async_copy(slab(w, me), outs[w].at[me], local_sems.at[w]) for w in range(n_w)]
        for cp in mine:
            cp.start()
        sends = [copy(k, w) for k in range(1, N_DEV) for w in range(n_w)]
        for cp in sends:
            cp.start()
        for cp in sends:
            cp.wait_recv()
        for cp in sends:
            cp.wait_send()
        for cp in mine:
            cp.wait()

    hbm = pl.BlockSpec(memory_space=pltpu.HBM)
    return pl.pallas_call(
        body, name="scatter_grads",
        in_specs=[hbm] * n_w, out_specs=[hbm] * n_w,
        out_shape=[jax.ShapeDtypeStruct((N_DEV, g.shape[0] // N_DEV, g.shape[1]), g.dtype) for g in grads],
        scratch_shapes=[pltpu.SemaphoreType.DMA((N_DEV - 1, n_w)), pltpu.SemaphoreType.DMA((N_DEV - 1, n_w)),
                        pltpu.SemaphoreType.DMA((n_w,))],
        compiler_params=_params(),
    )(*grads)


def _peer_copies(mode, srcs, lands, send_sems, recv_sems):
    me = _index(_place())
    copies = []
    for k in range(1, N_DEV):
        peer = _peer(k)
        for w, (src, land) in enumerate(zip(srcs, lands)):
            if mode == "gather":
                r = src.shape[0]
                dst = land.at[pl.ds(pl.multiple_of(me * r, 16), r), :]
            else:
                r = src.shape[0] // N_DEV
                src = src.at[pl.ds(pl.multiple_of(_index(peer) * r, 16), r), :]
                dst = land.at[me]
            copies.append(pltpu.make_async_remote_copy(
                src_ref=src, dst_ref=dst, send_sem=send_sems.at[(k - 1) * len(srcs) + w],
                recv_sem=recv_sems.at[(k - 1) * len(srcs) + w],
                device_id=peer, device_id_type=MESH))
    return copies


def _exchange_start(mode, srcs, lands, name):
    n = len(srcs)

    def body(*refs):
        for cp in _peer_copies(mode, refs[:n], refs[n:2 * n], refs[2 * n], refs[2 * n + 1]):
            cp.start()
        refs[-1][...] = jnp.zeros_like(refs[-1])

    hbm, sem = pl.BlockSpec(memory_space=pltpu.HBM), pl.BlockSpec(memory_space=pltpu.SEMAPHORE)
    arrays = list(srcs) + list(lands)
    out = pl.pallas_call(
        body, name=name,
        out_shape=(pltpu.SemaphoreType.DMA(((N_DEV - 1) * n,)), pltpu.SemaphoreType.DMA(((N_DEV - 1) * n,)),
                   *[pltpu.HBM(a.shape, a.dtype) for a in arrays], jax.ShapeDtypeStruct((8, LANES), F32)),
        in_specs=[hbm] * (2 * n), out_specs=(sem, sem, *[hbm] * (2 * n), pl.BlockSpec(memory_space=pltpu.VMEM)),
        input_output_aliases={i: 2 + i for i in range(2 * n)},
        compiler_params=pltpu.CompilerParams(has_side_effects=pltpu.SideEffectType.DATAFLOW_SIDE_EFFECTING),
    )(*[pltpu.with_memory_space_constraint(a, pltpu.HBM) for a in arrays])
    return out[0], out[1], out[2:2 + n], out[2 + n:2 + 2 * n], out[-1]


def _exchange_wait(mode, send_sems, recv_sems, srcs, lands, after, name):
    n = len(srcs)

    def body(*refs):
        copies = _peer_copies(mode, refs[:n], refs[n:2 * n], refs[2 * n], refs[2 * n + 1])
        for cp in copies:
            cp.wait_send()
        for cp in copies:
            cp.wait_recv()

    hbm, sem = pl.BlockSpec(memory_space=pltpu.HBM), pl.BlockSpec(memory_space=pltpu.SEMAPHORE)
    arrays = list(srcs) + list(lands)
    out = pl.pallas_call(
        body, name=name, out_shape=tuple(pltpu.HBM(a.shape, a.dtype) for a in arrays),
        in_specs=[hbm] * (2 * n) + [sem, sem, pl.BlockSpec(memory_space=pl.ANY)], out_specs=tuple([hbm] * (2 * n)),
        input_output_aliases={i: i for i in range(2 * n)},
        compiler_params=pltpu.CompilerParams(has_side_effects=pltpu.SideEffectType.DATAFLOW_SIDE_EFFECTING),
    )(*arrays, send_sems, recv_sems, after)
    return out[n:]


SMALL_WEIGHTS = ("b_ada", "g_pre_mix", "g_post_mix", "g_pre_ffn", "g_post_ffn", "w_pool", "b_pool", "pool_scale", "conv_b")


def _small_reduce_adam(sums_in, sums_mix, sums_ffn, sums_pool, dw_blk, dconv, loss_loc, weights, moms, vels):
    locals_ = [sums_in, sums_mix, sums_ffn, sums_pool, dw_blk, dconv, loss_loc]
    n_l, n_w = len(locals_), len(weights)
    d = sums_in.shape[1]
    diag = (len(POOL_WINDOWS), HEAD_DIM, HEAD_DIM)

    def body(*refs):
        loc = refs[:n_l]
        w_refs, m_refs, v_refs = (refs[n_l + k * n_w:n_l + (k + 1) * n_w] for k in range(3))
        outs = refs[n_l + 3 * n_w:n_l + 7 * n_w]
        dmod_ref, dconv_ref, loss_ref = refs[n_l + 7 * n_w:n_l + 7 * n_w + 3]
        gathered = refs[n_l + 7 * n_w + 3:n_l + 7 * n_w + 3 + n_l]
        diag_ref, send_sems, recv_sems = refs[-3:]
        me = _index(_place())
        blk = loc[4][...]
        for gi in range(len(POOL_WINDOWS)):
            lo = gi * HEAD_DIM
            diag_ref[gi] = blk[lo:lo + HEAD_DIM, lo:lo + HEAD_DIM]
        loc = loc[:4] + (diag_ref,) + loc[5:]

        def copy(a, k):
            return pltpu.make_async_remote_copy(
                src_ref=loc[a], dst_ref=gathered[a].at[me], send_sem=send_sems.at[a, k - 1],
                recv_sem=recv_sems.at[a, k - 1], device_id=_peer(k), device_id_type=MESH)

        copies = [copy(a, k) for k in range(1, N_DEV) for a in range(n_l)]
        for cp in copies:
            cp.start()
        for a in range(n_l):
            gathered[a][me] = loc[a][...]
        for cp in copies:
            cp.wait_recv()

        def total(a):
            tot = gathered[a][0]
            for dev in range(1, N_DEV):
                tot = tot + gathered[a][dev]
            return tot

        t_in, t_mix, t_ffn, t_pool, t_blk, t_conv, t_loss = (total(a) for a in range(n_l))
        dconv_ref[...] = t_conv
        loss_ref[...] = t_loss
        mod_rows = ((0, 0), (0, 1), (1, 3), (1, 0), (1, 1), (2, 0))
        for dev in range(N_DEV):
            for k, (a, r) in enumerate(mod_rows):
                dmod_ref[dev:dev + 1, k * d:(k + 1) * d] = gathered[a][dev, r:r + 1, :]

        def update(idx, g, at=()):
            sel = lambda ref: ref.at[at] if at else ref
            delta, nm, nv = _adam_math(sel(w_refs[idx])[...], g, sel(m_refs[idx])[...], sel(v_refs[idx])[...])
            for k, val in enumerate((g, delta, nm, nv)):
                sel(outs[4 * idx + k])[...] = val

        tots = (t_in, t_mix, t_ffn)
        update(0, jnp.concatenate([tots[a][r:r + 1] for a, r in mod_rows], axis=1))
        update(1, t_in[2:3])
        update(2, t_mix[4:5])
        update(3, t_mix[2:3])
        update(4, t_ffn[1:2])
        for gi in range(len(POOL_WINDOWS)):
            update(5, t_blk[gi], at=(0, gi))
        update(6, jnp.concatenate([t_pool[0:1, gi * HEAD_DIM:(gi + 1) * HEAD_DIM] for gi in range(len(POOL_WINDOWS))], axis=0),
               at=(0,))
        update(7, t_pool[1:2])
        update(8, t_conv[3:4])
        for cp in copies:
            cp.wait_send()

    vmem = pl.BlockSpec(memory_space=pltpu.VMEM)
    shape = lambda a: jax.ShapeDtypeStruct(a.shape, F32)
    out = pl.pallas_call(
        body, name="small_reduce_adam",
        in_specs=[vmem] * (n_l + 3 * n_w), out_specs=[vmem] * (4 * n_w + 3),
        out_shape=[shape(w) for w in weights for _ in range(4)]
        + [jax.ShapeDtypeStruct((N_DEV, 6 * d), F32), shape(dconv), shape(loss_loc)],
        scratch_shapes=[pltpu.VMEM((N_DEV,) + (diag if a is dw_blk else a.shape), F32) for a in locals_]
        + [pltpu.VMEM(diag, F32), pltpu.SemaphoreType.DMA((n_l, N_DEV - 1)), pltpu.SemaphoreType.DMA((n_l, N_DEV - 1))],
        compiler_params=_params(),
    )(*locals_, *weights, *moms, *vels)
    return out[:4 * n_w], out[4 * n_w], out[4 * n_w + 1], out[4 * n_w + 2]


def _adam_math(w, g, m, v):
    m = ADAM_B1 * m + (1.0 - ADAM_B1) * g
    v = ADAM_B2 * v + (1.0 - ADAM_B2) * (g * g)
    m_hat = m / (1.0 - ADAM_B1 ** ADAM_STEP)
    v_hat = v / (1.0 - ADAM_B2 ** ADAM_STEP)
    delta = -ADAM_LR * (m_hat / (jnp.sqrt(v_hat) + ADAM_EPS) + ADAM_WD * w)
    return delta, m, v


def _adam(w, g, m, v, name, tr):
    rows, cols = w.shape

    def body(w_ref, g_ref, m_ref, v_ref, d_ref, nm_ref, nv_ref):
        d_ref[...], nm_ref[...], nv_ref[...] = _adam_math(w_ref[...], g_ref[...], m_ref[...], v_ref[...])

    spec = pl.BlockSpec((tr, cols), lambda i: (i, 0))
    shape = jax.ShapeDtypeStruct((rows, cols), F32)
    return pl.pallas_call(
        body, name=name, grid=(rows // tr,), in_specs=[spec] * 4, out_specs=[spec] * 3,
        out_shape=[shape] * 3, compiler_params=_params(("arbitrary",)),
    )(w, g, m, v)


def _sum_adam(parts, w, m, v, name, tr):
    _, rows, cols = parts.shape

    def body(p_ref, w_ref, m_ref, v_ref, g_ref, d_ref, nm_ref, nv_ref):
        g = p_ref[0].astype(F32)
        for dev in range(1, N_DEV):
            g = g + p_ref[dev].astype(F32)
        g_ref[...] = g
        d_ref[...], nm_ref[...], nv_ref[...] = _adam_math(w_ref[...], g, m_ref[...], v_ref[...])

    spec = pl.BlockSpec((tr, cols), lambda i: (i, 0))
    shape = jax.ShapeDtypeStruct((rows, cols), F32)
    return pl.pallas_call(
        body, name=name, grid=(rows // tr,),
        in_specs=[pl.BlockSpec((N_DEV, tr, cols), lambda i: (0, i, 0)), spec, spec, spec],
        out_specs=[spec] * 4, out_shape=[shape] * 4, compiler_params=_params(("arbitrary",)),
    )(parts, w, m, v)


def _ada_grad_adam(c_all, dmod_cols, w, m, v, tr):
    rows, cols = w.shape

    def body(c_ref, dm_ref, w_ref, m_ref, v_ref, g_ref, d_ref, nm_ref, nv_ref):
        cv = c_ref[...]
        act = cv * jax.nn.sigmoid(cv)
        g = lax.dot_general(act, dm_ref[...], TN, preferred_element_type=F32, precision=lax.Precision.HIGHEST)
        g_ref[...] = g
        d_ref[...], nm_ref[...], nv_ref[...] = _adam_math(w_ref[...], g, m_ref[...], v_ref[...])

    spec = pl.BlockSpec((tr, cols), lambda i: (i, 0))
    shape = jax.ShapeDtypeStruct((rows, cols), F32)
    return pl.pallas_call(
        body, name="ada_grad_adam", grid=(rows // tr,),
        in_specs=[pl.BlockSpec((N_DEV, tr), lambda i: (0, i)), pl.BlockSpec((N_DEV, cols), lambda i: (0, 0)), spec, spec, spec],
        out_specs=[spec] * 4, out_shape=[shape] * 4, compiler_params=_params(("arbitrary",)),
    )(c_all, dmod_cols, w, m, v)


def _rope_tables(positions):
    s_len = positions.shape[0]
    inv_freq = ROPE_THETA ** (-jnp.arange(0, 2 * ROT_HALF, 2, dtype=F32) / (2 * ROT_HALF))
    ang = positions.astype(F32)[:, None] * inv_freq
    cos, sin = jnp.cos(ang), jnp.sin(ang)
    rest = HEAD_DIM - 2 * ROT_HALF
    zero = lambda n: jnp.zeros((s_len, n), F32)
    head = jnp.stack([jnp.concatenate([cos, cos, jnp.ones((s_len, rest), F32)], axis=1),
                      jnp.concatenate([-sin, zero(HEAD_DIM - ROT_HALF)], axis=1),
                      jnp.concatenate([zero(ROT_HALF), sin, zero(rest)], axis=1)])
    return jnp.tile(head, (1, 1, LANES // HEAD_DIM))


def _pad_rows(a, rows):
    return jnp.pad(a, ((0, rows - a.shape[0]), (0, 0)))


def _as_rows(a, rows):
    flat = a.reshape(-1)
    return jnp.pad(flat, (0, rows * LANES - flat.shape[0])).reshape(rows, LANES)


def _sequence_step(xs, target, rope, mods, gains, w_in_t, w_out_t, fetch_ffn, send_ffn_grads, send_mix_grads, w_blk_b, b_pool_r,
                   pool_scale_r, conv_w_all, conv_b):
    sh_m, sc_m, gt_m, sh_f, sc_f, gt_f = mods
    g_pre_mix, g_post_mix, g_pre_ffn, g_post_ffn = gains
    h1, u_pool, qkv = _premix_inproj(xs, sh_m, sc_m, g_pre_mix, w_in_t, rope, tm=512)
    o_g, lse_g = [], []
    for gi, dil in enumerate(DILATIONS):
        o, lse = _attn_fwd(qkv, gi, dil)
        o_g.append(o)
        lse_g.append(lse)
    x1, y1, h2, cat, attn, lse_all = _mix_out(xs, u_pool, o_g, lse_g, w_blk_b, b_pool_r, pool_scale_r, w_out_t,
                                              gt_m, g_post_mix, g_pre_ffn, sc_f, sh_f, tm=256)
    w_up_t, w_down_f = fetch_ffn(x1)
    gate, a_ffn, act, vd, dy2, dout, sums_ffn, loss_loc = _ffn_fwd_loss(h2, x1, target, w_up_t, w_down_f, conv_w_all, conv_b,
                                                              gt_f, g_post_ffn, tm=256, tf=2816, ck=256)

    dgc, dval, dw_down, dconv = _ffn_bwd_act(dy2, gate, a_ffn, act, vd, w_down_f, tm=1024, tf=256)
    dup, dh2 = _ffn_bwd_up(dgc, dval, w_up_t, conv_w_all, tm=256)
    dw_up_t = _wgrad(dup, h2, "wgrad_up", tk=1024, tmm=1408)
    token = send_ffn_grads(dw_up_t, dw_down)
    if token is not None:
        sc_f = sc_f + token[0:1, 0:1]
    dx1, dpool, dattn, delta, dw_out_t, sums_mix = _mix_bwd(dh2, dout, x1, y1, cat, attn, w_out_t, sc_f, g_pre_ffn,
                                                           gt_m, g_post_mix, tm=256)
    du, dw_blk, sums_pool = _pool_bwd(dpool, u_pool, w_blk_b, b_pool_r, pool_scale_r, tm=512)
    dqkv = []
    for gi, dil in enumerate(DILATIONS):
        dqkv += list(_attn_bwd(qkv, dattn, lse_all, delta, gi, dil))
    dproj = _dproj_assemble(du, dqkv, rope, tm=512)
    dw_in_t = _wgrad(dproj, h1, "wgrad_in", tk=1024, tmm=1280)
    token = send_mix_grads(dw_in_t, dw_out_t)
    if token is not None:
        sc_m = sc_m + token[0:1, 0:1]
    grad_x, sums_in = _inproj_bwd(dproj, w_in_t, xs, dx1, sc_m, g_pre_mix, tm=256)
    return (loss_loc, grad_x, dw_in_t, dw_out_t, dw_up_t, dw_down, dw_blk, dconv,
            sums_in, sums_mix, sums_ffn, sums_pool)


def kernel(x, c, positions, w_ada, b_ada, g_pre_mix, g_post_mix, g_pre_ffn, g_post_ffn, w_in, w_pool, b_pool, pool_scale, w_out, w_up, conv_w, conv_b, w_down, loss_target, m_w_ada, m_b_ada, m_g_pre_mix, m_g_post_mix, m_g_pre_ffn, m_g_post_ffn, m_w_in, m_w_pool, m_b_pool, m_pool_scale, m_w_out, m_w_up, m_conv_w, m_conv_b, m_w_down, v_w_ada, v_b_ada, v_g_pre_mix, v_g_post_mix, v_g_pre_ffn, v_g_post_ffn, v_w_in, v_w_pool, v_b_pool, v_pool_scale, v_w_out, v_w_up, v_conv_w, v_conv_b, v_w_down):
    s_len, d = x.shape[1], x.shape[2]
    d_ff = w_down.shape[1] * N_DEV
    me = _index(_place())
    xs, target = x[0], loss_target[0]

    ncol = w_ada.shape[2]
    b_cols = lax.dynamic_slice(b_ada, (0, me * ncol), (1, ncol))
    c_all, mod, taps_all = _ada_exchange(jnp.broadcast_to(c, (8, d)), w_ada[0], b_cols, _pad_rows(conv_w[0], 8))
    c_all = c_all[:, 0, :]
    conv_w_all = jnp.transpose(taps_all[:, :3, :], (1, 0, 2)).reshape(3, d_ff)
    sh_m, sc_m, gt_m, sh_f, sc_f, gt_f = [mod[:, 0, :].reshape(1, -1)[:, k * d:(k + 1) * d] for k in range(6)]

    w_in_t, w_out_t = _gather_weights([w_in[0].T.astype(BF16), w_out[0].T.astype(BF16)])

    rope = _rope_tables(positions[0])
    w_blk = jnp.zeros((256, 256), F32)
    for gi in range(4):
        w_blk = lax.dynamic_update_slice(w_blk, w_pool[0, gi], (gi * HEAD_DIM, gi * HEAD_DIM))
    w_blk_b = w_blk.astype(BF16)
    b_pool_r, pool_scale_r = b_pool.reshape(1, 256), pool_scale.reshape(1, 256)

    up_sh, down_sh = w_up[0].T.astype(BF16), w_down[0].astype(BF16)
    w_in_t, conv_w_all, up_sh, down_sh = lax.optimization_barrier((w_in_t, conv_w_all, up_sh, down_sh))
    lands = [lax.dynamic_update_slice(lax.empty((N_DEV * s.shape[0], s.shape[1]), BF16), s, (me * s.shape[0], 0))
             for s in (up_sh, down_sh)]
    w_send, w_recv, w_src, w_land, w_token = _exchange_start("gather", [up_sh, down_sh], lands, "ffn_weights_start")

    def fetch_ffn(after):
        return _exchange_wait("gather", w_send, w_recv, w_src, w_land, after, "ffn_weights_wait")

    flight = []

    def scatter_lands(*grads):
        lands = []
        for g in grads:
            r = g.shape[0] // N_DEV
            own = lax.dynamic_slice(g, (me * r, 0), (r, g.shape[1]))
            lands.append(lax.dynamic_update_slice(lax.empty((N_DEV, r, g.shape[1]), BF16), own[None], (me, 0, 0)))
        return lands

    def send_ffn_grads(dw_up_t, dw_down):
        flight.extend(_exchange_start("scatter", [dw_up_t, dw_down], scatter_lands(dw_up_t, dw_down), "ffn_grads_start"))
        return flight[4]

    mix_flight = []

    def send_mix_grads(dw_in_t, dw_out_t):
        mix_flight.extend(_exchange_start("scatter", [dw_in_t, dw_out_t], scatter_lands(dw_in_t, dw_out_t), "mix_grads_start"))
        return mix_flight[4]

    (loss_loc, grad_x, dw_in_t, dw_out_t, _, _, dw_blk, dconv,
     sums_in, sums_mix, sums_ffn, sums_pool) = _sequence_step(
        xs, target, rope, (sh_m + w_token[0:1, 0:1], sc_m, gt_m, sh_f, sc_f, gt_f),
        (g_pre_mix, g_post_mix, g_pre_ffn, g_post_ffn),
        w_in_t, w_out_t, fetch_ffn, send_ffn_grads, send_mix_grads, w_blk_b, b_pool_r, pool_scale_r, conv_w_all, conv_b)

    parts_ffn = _exchange_wait("scatter", *flight[:4], grad_x, "ffn_grads_wait")
    big = {
        "w_up": [a.T for a in _sum_adam(parts_ffn[0], w_up[0].T, m_w_up[0].T, v_w_up[0].T, "adam_w_up", 64)],
        "w_down": _sum_adam(parts_ffn[1], w_down[0], m_w_down[0], v_w_down[0], "adam_w_down", 32),
    }

    rep_w = [b_ada, g_pre_mix, g_post_mix, g_pre_ffn, g_post_ffn, w_pool, b_pool, pool_scale, conv_b]
    rep_m = [m_b_ada, m_g_pre_mix, m_g_post_mix, m_g_pre_ffn, m_g_post_ffn, m_w_pool, m_b_pool, m_pool_scale, m_conv_b]
    rep_v = [v_b_ada, v_g_pre_mix, v_g_post_mix, v_g_pre_ffn, v_g_post_ffn, v_w_pool, v_b_pool, v_pool_scale, v_conv_b]
    rep_out, dmod_all, dconv_tot, loss_tot = _small_reduce_adam(
        sums_in, sums_mix, sums_ffn, sums_pool, dw_blk, dconv, loss_loc, rep_w, rep_m, rep_v)
    g_rep, d_rep, nm_rep, nv_rep = (rep_out[k::4] for k in range(4))

    fcol = d_ff // N_DEV
    g_cw = lax.dynamic_slice(dconv_tot, (0, me * fcol), (3, fcol))
    d_cw, nm_cw, nv_cw = _adam(conv_w[0], g_cw, m_conv_w[0], v_conv_w[0], "adam_conv_w", 3)

    dmod_cols = lax.dynamic_slice(dmod_all, (0, me * ncol), (N_DEV, ncol))
    g_ada, d_ada, nm_ada, nv_ada = _ada_grad_adam(c_all, dmod_cols, w_ada[0], m_w_ada[0], v_w_ada[0], 256)

    parts_mix = _exchange_wait("scatter", *mix_flight[:4], g_ada, "mix_grads_wait")
    big["w_in"] = [a.T for a in _sum_adam(parts_mix[0], w_in[0].T, m_w_in[0].T, v_w_in[0].T, "adam_w_in", 64)]
    big["w_out"] = [a.T for a in _sum_adam(parts_mix[1], w_out[0].T, m_w_out[0].T, v_w_out[0].T, "adam_w_out", 128)]

    loss = loss_tot[0, 0]

    def group(k):
        rep = (g_rep, d_rep, nm_rep, nv_rep)[k]
        ada = (g_ada, d_ada, nm_ada, nv_ada)[k][None]
        cw = (g_cw, d_cw, nm_cw, nv_cw)[k][None]
        return [ada, rep[0], rep[1], rep[2], rep[3], rep[4], big["w_in"][k][None], rep[5], rep[6], rep[7],
                big["w_out"][k][None], big["w_up"][k][None], cw, rep[8], big["w_down"][k][None]]

    return (loss, grad_x[None], *group(0), *group(1), *group(2), *group(3))
```

```python
import functools
import math

import jax
import jax.numpy as jnp
from jax import lax
from jax.experimental import pallas as pl
from jax.experimental.pallas import tpu as pltpu

F32 = jnp.float32
BF16 = jnp.bfloat16
MESH = pl.DeviceIdType.MESH

N_DEV = 8
HEAD_DIM = 64
ROT_HALF = 8
ROPE_THETA = 500000.0
POOL_WINDOWS = (2, 4, 8, 16)
DILATIONS = (1, 4, 16)
BLOCK = 128
NORM_EPS = 1e-6
HALO = 16
MASKED = -1e30
ATTN_FWD_UNROLL = 4
ATTN_BWD_UNROLL = 2

ADAM_LR = 0.001
ADAM_B1 = 0.9
ADAM_B2 = 0.999
ADAM_EPS = 1e-08
ADAM_WD = 0.01
ADAM_STEP = 10

V7X_VMEM_LIMIT = 56 * 1024 * 1024
LANES = 128

NT = (((1,), (1,)), ((), ()))
NN = (((1,), (0,)), ((), ()))
TN = (((0,), (0,)), ((), ()))


def _dot(a, b, dims):
    return lax.dot_general(a, b, dims, preferred_element_type=F32)


def _params(sem=None, vmem=V7X_VMEM_LIMIT):
    if sem is None:
        return pltpu.CompilerParams(vmem_limit_bytes=vmem)
    return pltpu.CompilerParams(dimension_semantics=sem, vmem_limit_bytes=vmem)


def _rstd(v):
    return lax.rsqrt(jnp.mean(v * v, axis=-1, keepdims=True) + NORM_EPS)


def _norm_bwd(dn, n, rstd):
    return rstd * (dn - n * jnp.mean(dn * n, axis=-1, keepdims=True))


def _rope_fwd(p, rope_ref):
    return p * rope_ref[0] + pltpu.roll(p, LANES - ROT_HALF, 1) * rope_ref[1] + pltpu.roll(p, ROT_HALF, 1) * rope_ref[2]


def _rope_bwd(dp, rope_ref):
    return dp * rope_ref[0] + pltpu.roll(dp * rope_ref[1], ROT_HALF, 1) + pltpu.roll(dp * rope_ref[2], LANES - ROT_HALF, 1)


def _gelu_parts(v):
    k2 = 2.0 * math.sqrt(2.0 / math.pi)
    c = 0.044715
    v2 = v * v
    s = jax.nn.sigmoid(v * (k2 + (k2 * c) * v2))
    g = v * s
    dg = s + g * (1.0 - s) * (k2 + (3.0 * k2 * c) * v2)
    return g, dg


def _halo_before(i, tile):
    return jnp.maximum(i * (tile // HALO) - 1, 0)


def _premix_inproj(x, sh, sc, g, w_in_t, rope, tm):
    s_len, d = x.shape
    n_proj = w_in_t.shape[0]
    n_slab = (n_proj - 256) // LANES

    def body(x_ref, sh_ref, sc_ref, g_ref, w_ref, rope_ref, h_ref, up_ref, qkv_ref):
        xv = x_ref[...]
        h = (xv * _rstd(xv) * g_ref[...]) * (1.0 + sc_ref[...]) + sh_ref[...]
        hb = h.astype(BF16)
        h_ref[...] = hb
        up_ref[...] = _dot(hb, w_ref[0:256, :], NT)
        for pair in range(n_slab // 2):
            p = _dot(hb, w_ref[256 + 256 * pair:512 + 256 * pair, :], NT)
            for half in range(2):
                ph = p[:, half * LANES:(half + 1) * LANES]
                if pair < 6:
                    ph = _rope_fwd(ph, rope_ref)
                if pair < 3:
                    ph = ph * (HEAD_DIM ** -0.5)
                qkv_ref[2 * pair + half] = ph

    vec = pl.BlockSpec((1, d), lambda i: (0, 0))
    return pl.pallas_call(
        body, name="premix_inproj", grid=(s_len // tm,),
        in_specs=[pl.BlockSpec((tm, d), lambda i: (i, 0)), vec, vec, vec,
                  pl.BlockSpec((n_proj, d), lambda i: (0, 0)),
                  pl.BlockSpec((3, tm, LANES), lambda i: (0, i, 0))],
        out_specs=[pl.BlockSpec((tm, d), lambda i: (i, 0)),
                   pl.BlockSpec((tm, 256), lambda i: (i, 0)),
                   pl.BlockSpec((n_slab, tm, LANES), lambda i: (0, i, 0))],
        out_shape=[jax.ShapeDtypeStruct((s_len, d), BF16),
                   jax.ShapeDtypeStruct((s_len, 256), F32),
                   jax.ShapeDtypeStruct((n_slab, s_len, LANES), F32)],
        compiler_params=_params(("arbitrary",)),
    )(x, sh, sc, g, w_in_t, rope)


def _block_rows(n, r, dil):
    start = n * (BLOCK * dil) + r
    if dil == 1:
        return pl.ds(pl.multiple_of(start, BLOCK), BLOCK)
    return pl.ds(start, BLOCK, stride=dil)


def _band_mask(n):
    ri = lax.broadcasted_iota(jnp.int32, (BLOCK, 2 * BLOCK), 0)
    cj = lax.broadcasted_iota(jnp.int32, (BLOCK, 2 * BLOCK), 1)
    cur = (cj >= BLOCK) & (cj - BLOCK <= ri)
    prev = (cj < BLOCK) & (cj >= ri) & (n > 0)
    return cur | prev


def _attn_fwd(qkv, group, dil):
    s_len = qkv.shape[1]
    nb = s_len // (BLOCK * dil)

    def body(q_ref, k_ref, v_ref, o_ref, lse_ref):
        lane = lax.broadcasted_iota(jnp.int32, (BLOCK, LANES), 1)
        first = lane < HEAD_DIM

        def block(t, carry):
            r, n = t // nb, t % nb
            cur = _block_rows(n, r, dil)
            prev = _block_rows(jnp.maximum(n - 1, 0), r, dil)
            q = q_ref[0, cur, :]
            kcat = jnp.concatenate([k_ref[0, prev, :], k_ref[0, cur, :]], axis=0).astype(BF16)
            vcat = jnp.concatenate([v_ref[0, prev, :], v_ref[0, cur, :]], axis=0).astype(BF16)
            valid = _band_mask(n)
            q2 = jnp.concatenate([jnp.where(first, q, 0.0), jnp.where(first, 0.0, q)], axis=0).astype(BF16)
            s = jnp.where(jnp.concatenate([valid, valid], axis=0), _dot(q2, kcat, NT), MASKED)
            m = jnp.max(s, axis=-1, keepdims=True)
            p = jnp.exp(s - m)
            den = jnp.sum(p, axis=-1, keepdims=True)
            o2 = _dot(p.astype(BF16), vcat, NN) / den
            lse2 = m + jnp.log(den)
            o_ref[0, cur, :] = jnp.where(first, o2[:BLOCK], o2[BLOCK:])
            lse_ref[0, cur, :] = jnp.where(first, lse2[:BLOCK], lse2[BLOCK:])
            return carry

        lax.fori_loop(0, nb * dil, block, 0, unroll=ATTN_FWD_UNROLL)

    def slab(base):
        return pl.BlockSpec((1, s_len, LANES), lambda s: (base + 2 * group + s, 0, 0))

    out = pl.BlockSpec((1, s_len, LANES), lambda s: (s, 0, 0))
    shape = jax.ShapeDtypeStruct((2, s_len, LANES), F32)
    return pl.pallas_call(
        body, name=f"attn_fwd_d{dil}", grid=(2,),
        in_specs=[slab(0), slab(6), slab(12)], out_specs=[out, out], out_shape=[shape, shape],
        compiler_params=_params(("arbitrary",)),
    )(qkv, qkv, qkv)


def _pool_mixed(u, halo, i, tm):
    ue = jnp.concatenate([halo, u], axis=0)
    s2 = ue + pltpu.roll(ue, 1, 0)
    s4 = s2 + pltpu.roll(s2, 2, 0)
    s8 = s4 + pltpu.roll(s4, 4, 0)
    s16 = s8 + pltpu.roll(s8, 8, 0)
    grp = lax.broadcasted_iota(jnp.int32, (tm, 256), 1) // HEAD_DIM
    pick = lambda a, b, c, e: jnp.where(grp == 0, a, jnp.where(grp == 1, b, jnp.where(grp == 2, c, e)))
    win_sum = pick(s2[HALO:], s4[HALO:], s8[HALO:], s16[HALO:])
    pos = (i * tm + lax.broadcasted_iota(jnp.int32, (tm, 256), 0)).astype(F32)
    count = jnp.minimum(pos + 1.0, pick(*[float(w) for w in POOL_WINDOWS]))
    return win_sum / count - u, count


def _mix_out(x, u_pool, o_g, lse_g, w_blk, b_pool, pool_scale, w_out_t, gt_m, g_post_mix, g_pre_ffn, sc_f, sh_f, tm):
    s_len, d = x.shape

    def body(x_ref, u_ref, uh_ref, o0, o1, o2, l0, l1, l2, wb_ref, bp_ref, ps_ref, wo_ref,
             gt_ref, g1_ref, g2_ref, sc_ref, sh_ref,
             x1_ref, y1_ref, h2_ref, cat_ref, attn_ref, lall_ref):
        i = pl.program_id(0)
        u = u_ref[...]
        halo = uh_ref[...] * (i > 0).astype(F32)
        mixed, _ = _pool_mixed(u, halo, i, tm)
        y = _dot(mixed.astype(BF16), wb_ref[...], NN) + bp_ref[...]
        pool = y * ps_ref[...]
        attn = []
        for s in range(2):
            la, lb, lc = l0[s], l1[s], l2[s]
            mx = jnp.maximum(jnp.maximum(la, lb), lc)
            ea, eb, ec = jnp.exp(la - mx), jnp.exp(lb - mx), jnp.exp(lc - mx)
            den = ea + eb + ec
            lall_ref[s] = mx + jnp.log(den)
            attn.append((ea / den) * o0[s] + (eb / den) * o1[s] + (ec / den) * o2[s])
        attn = jnp.concatenate(attn, axis=1)
        attn_ref[...] = attn
        cat = jnp.concatenate([pool, attn], axis=1).astype(BF16)
        cat_ref[...] = cat
        y1 = _dot(cat, wo_ref[...], NT)
        y1_ref[...] = y1
        x1 = x_ref[...] + gt_ref[...] * (y1 * _rstd(y1) * g1_ref[...])
        x1_ref[...] = x1
        h2 = (x1 * _rstd(x1) * g2_ref[...]) * (1.0 + sc_ref[...]) + sh_ref[...]
        h2_ref[...] = h2.astype(BF16)

    tile = lambda w: pl.BlockSpec((tm, w), lambda i: (i, 0))
    slab = pl.BlockSpec((2, tm, LANES), lambda i: (0, i, 0))
    const = lambda a: pl.BlockSpec(a.shape, lambda i: (0,) * a.ndim)
    return pl.pallas_call(
        body, name="mix_out", grid=(s_len // tm,),
        in_specs=[tile(d), tile(256), pl.BlockSpec((HALO, 256), lambda i: (_halo_before(i, tm), 0)),
                  slab, slab, slab, slab, slab, slab,
                  const(w_blk), const(b_pool), const(pool_scale), const(w_out_t),
                  const(gt_m), const(g_post_mix), const(g_pre_ffn), const(sc_f), const(sh_f)],
        out_specs=[tile(d), tile(d), tile(d), tile(512), tile(256), slab],
        out_shape=[jax.ShapeDtypeStruct((s_len, d), F32), jax.ShapeDtypeStruct((s_len, d), F32),
                   jax.ShapeDtypeStruct((s_len, d), BF16), jax.ShapeDtypeStruct((s_len, 512), BF16),
                   jax.ShapeDtypeStruct((s_len, 256), F32), jax.ShapeDtypeStruct((2, s_len, LANES), F32)],
        compiler_params=_params(("arbitrary",)),
    )(x, u_pool, u_pool, *o_g, *lse_g, w_blk, b_pool, pool_scale, w_out_t, gt_m, g_post_mix, g_pre_ffn, sc_f, sh_f)


def _conv_gate(gate_ext, cw, cb):
    gc = gate_ext * cw[2:3, :] + pltpu.roll(gate_ext, 1, 0) * cw[1:2, :] + pltpu.roll(gate_ext, 2, 0) * cw[0:1, :]
    return gc[HALO:] + cb


def _ffn_fwd_loss(h2, x1, target, w_up_t, w_down, conv_w, conv_b, gt_f, g_post_ffn, tm, tf, ck):
    s_len, d = x1.shape
    d_ff = w_down.shape[0]
    n_f = d_ff // tf

    def body(h_ref, hh_ref, x1_ref, tgt_ref, wg_ref, wv_ref, wd_ref, cw_ref, cb_ref, gt_ref, g_ref,
             gate_ref, a_ref, act_ref, vd_ref, dy2_ref, dout_ref, sums_ref, loss_ref, acc_ref):
        i, j = pl.program_id(0), pl.program_id(1)

        @pl.when((i == 0) & (j == 0))
        def _():
            sums_ref[...] = jnp.zeros_like(sums_ref)
            loss_ref[...] = jnp.zeros_like(loss_ref)

        h = h_ref[...]
        h_ext = jnp.concatenate([hh_ref[...], h], axis=0)
        row = lax.broadcasted_iota(jnp.int32, (tm + HALO, ck), 0)
        no_halo = (row < HALO) & (i == 0)

        def up(c):
            cs = slice(c * ck, (c + 1) * ck)
            return jnp.where(no_halo, 0.0, _dot(h_ext, wg_ref[cs, :], NT)), _dot(h, wv_ref[cs, :], NT)

        part = None
        n_c = tf // ck
        nxt = up(0)
        for c in range(n_c):
            cs = slice(c * ck, (c + 1) * ck)
            gate_ext, val = nxt
            if c + 1 < n_c:
                nxt = up(c + 1)
            act, dact = _gelu_parts(_conv_gate(gate_ext, cw_ref[:, cs], cb_ref[:, cs]))
            a = (act * val).astype(BF16)
            gate_ref[:, cs] = gate_ext[HALO:].astype(BF16)
            a_ref[:, cs] = a
            act_ref[:, cs] = act.astype(BF16)
            vd_ref[:, cs] = (val * dact).astype(BF16)
            p = _dot(a, wd_ref[cs, :], NN)
            part = p if part is None else part + p

        @pl.when(j == 0)
        def _():
            acc_ref[...] = part

        @pl.when(j > 0)
        def _():
            acc_ref[...] += part

        @pl.when(j == n_f - 1)
        def _():
            y2 = acc_ref[...]
            rstd = _rstd(y2)
            n = y2 * rstd
            rn = n * g_ref[...]
            err = x1_ref[...] + gt_ref[...] * rn - tgt_ref[...]
            loss_ref[...] += 0.5 * jnp.sum(jnp.mean(err * err, axis=-1, keepdims=True), axis=0, keepdims=True)
            dout = err * (1.0 / d)
            dout_ref[...] = dout
            drn = dout * gt_ref[...]
            sums_ref[0:1, :] += jnp.sum(dout * rn, axis=0, keepdims=True)
            sums_ref[1:2, :] += jnp.sum(drn * n, axis=0, keepdims=True)
            dy2_ref[...] = _norm_bwd(drn * g_ref[...], n, rstd).astype(BF16)

    tok = lambda w: pl.BlockSpec((tm, w), lambda i, j: (i, 0))
    tokf = pl.BlockSpec((tm, tf), lambda i, j: (i, j))
    vec = pl.BlockSpec((1, d), lambda i, j: (0, 0))
    once = {"pipeline_mode": pl.Buffered(1)} if n_f == 1 else {}
    return pl.pallas_call(
        body, name="ffn_fwd_loss", grid=(s_len // tm, n_f),
        in_specs=[tok(d), pl.BlockSpec((HALO, d), lambda i, j: (_halo_before(i, tm), 0)), tok(d), tok(d),
                  pl.BlockSpec((tf, d), lambda i, j: (j, 0), **once),
                  pl.BlockSpec((tf, d), lambda i, j: (j + n_f, 0), **once),
                  pl.BlockSpec((tf, d), lambda i, j: (j, 0), **once),
                  pl.BlockSpec((3, tf), lambda i, j: (0, j)), pl.BlockSpec((1, tf), lambda i, j: (0, j)), vec, vec],
        out_specs=[tokf, tokf, tokf, tokf, tok(d), tok(d), pl.BlockSpec((8, d), lambda i, j: (0, 0)),
                   pl.BlockSpec((8, LANES), lambda i, j: (0, 0))],
        out_shape=[jax.ShapeDtypeStruct((s_len, d_ff), BF16)] * 4
        + [jax.ShapeDtypeStruct((s_len, d), BF16), jax.ShapeDtypeStruct((s_len, d), F32),
                   jax.ShapeDtypeStruct((8, d), F32), jax.ShapeDtypeStruct((8, LANES), F32)],
        scratch_shapes=[pltpu.VMEM((tm, d), F32)],
        compiler_params=_params(("arbitrary", "arbitrary")),
    )(h2, h2, x1, target, w_up_t, w_up_t, w_down, conv_w, conv_b, gt_f, g_post_ffn)


def _ffn_bwd_act(dy2, gate, a, act, vd, w_down, tm, tf):
    s_len, d = dy2.shape
    d_ff = w_down.shape[0]
    n_t = s_len // tm

    def body(dy_ref, g_ref, gh_ref, a_ref, act_ref, vd_ref, wd_ref, dgc_ref, dval_ref, dwd_ref, dconv_ref, acc_ref):
        i = pl.program_id(1)
        gate_ext = jnp.concatenate([gh_ref[...], g_ref[...]], axis=0).astype(F32)
        row = lax.broadcasted_iota(jnp.int32, gate_ext.shape, 0)
        gate_ext = jnp.where((row < HALO) & (i == 0), 0.0, gate_ext)
        da = _dot(dy_ref[...], wd_ref[...], NT)
        dwd = _dot(a_ref[...], dy_ref[...], TN)
        dgc = da * vd_ref[...].astype(F32)
        dgc_ref[...] = dgc.astype(BF16)
        dval_ref[...] = (da * act_ref[...].astype(F32)).astype(BF16)
        rows = [jnp.sum(dgc * pltpu.roll(gate_ext, 2 - k, 0)[HALO:], axis=0, keepdims=True) for k in range(2)]
        rows += [jnp.sum(dgc * gate_ext[HALO:], axis=0, keepdims=True), jnp.sum(dgc, axis=0, keepdims=True),
                 jnp.zeros((4, tf), F32)]
        dconv = jnp.concatenate(rows, axis=0)

        @pl.when(i == 0)
        def _():
            acc_ref[...] = dwd
            dconv_ref[...] = dconv

        @pl.when(i > 0)
        def _():
            acc_ref[...] += dwd
            dconv_ref[...] += dconv

        @pl.when(i == n_t - 1)
        def _():
            dwd_ref[...] = acc_ref[...].astype(BF16)

    tokf = pl.BlockSpec((tm, tf), lambda j, i: (i, j))
    return pl.pallas_call(
        body, name="ffn_bwd_act", grid=(d_ff // tf, n_t),
        in_specs=[pl.BlockSpec((tm, d), lambda j, i: (i, 0)), tokf,
                  pl.BlockSpec((HALO, tf), lambda j, i: (_halo_before(i, tm), j)), tokf, tokf, tokf,
                  pl.BlockSpec((tf, d), lambda j, i: (j, 0))],
        out_specs=[tokf, tokf, pl.BlockSpec((tf, d), lambda j, i: (j, 0)), pl.BlockSpec((8, tf), lambda j, i: (0, j))],
        out_shape=[jax.ShapeDtypeStruct((s_len, d_ff), BF16), jax.ShapeDtypeStruct((s_len, d_ff), BF16),
                   jax.ShapeDtypeStruct((d_ff, d), BF16), jax.ShapeDtypeStruct((8, d_ff), F32)],
        scratch_shapes=[pltpu.VMEM((tf, d), F32)],
        compiler_params=_params(("arbitrary", "arbitrary")),
    )(dy2, gate, gate, a, act, vd, w_down)


def _ffn_bwd_up(dgc, dval, w_up_t, conv_w, tm):
    s_len, d_ff = dgc.shape
    d = w_up_t.shape[1]
    n_t = s_len // tm

    def body(dg_ref, dgn_ref, dv_ref, cw_ref, w_ref, dup_ref, dh_ref):
        i = pl.program_id(0)
        nxt = dgn_ref[...].astype(F32) * (i < n_t - 1).astype(F32)
        ext = jnp.concatenate([dg_ref[...].astype(F32), nxt], axis=0)
        rows = tm + HALO
        dgate = (ext * cw_ref[2:3, :] + pltpu.roll(ext, rows - 1, 0) * cw_ref[1:2, :]
                 + pltpu.roll(ext, rows - 2, 0) * cw_ref[0:1, :])[:tm]
        dup = jnp.concatenate([dgate.astype(BF16), dv_ref[...]], axis=1)
        dup_ref[...] = dup
        dh_ref[...] = _dot(dup, w_ref[...], NN)

    tokf = pl.BlockSpec((tm, d_ff), lambda i: (i, 0))
    return pl.pallas_call(
        body, name="ffn_bwd_up", grid=(n_t,),
        in_specs=[tokf, pl.BlockSpec((HALO, d_ff), lambda i: (jnp.minimum((i + 1) * (tm // HALO), s_len // HALO - 1), 0)),
                  tokf, pl.BlockSpec((3, d_ff), lambda i: (0, 0)), pl.BlockSpec((2 * d_ff, d), lambda i: (0, 0))],
        out_specs=[pl.BlockSpec((tm, 2 * d_ff), lambda i: (i, 0)), pl.BlockSpec((tm, d), lambda i: (i, 0))],
        out_shape=[jax.ShapeDtypeStruct((s_len, 2 * d_ff), BF16), jax.ShapeDtypeStruct((s_len, d), F32)],
        compiler_params=_params(("arbitrary",)),
    )(dgc, dgc, dval, conv_w, w_up_t)


def _mix_bwd(dh2, dout, x1, y1, cat, attn, w_out_t, sc_f, g_pre_ffn, gt_m, g_post_mix, tm):
    s_len, d = x1.shape
    n_t = s_len // tm

    def body(dh_ref, do_ref, x1_ref, y1_ref, cat_ref, at_ref, wo_ref, sc_ref, g2_ref, gt_ref, g1_ref,
             dx1_ref, dpool_ref, dattn_ref, delta_ref, dwo_ref, sums_ref, acc_ref):
        i = pl.program_id(0)
        dh = dh_ref[...]
        x1 = x1_ref[...]
        r2 = _rstd(x1)
        n2 = x1 * r2
        ng = n2 * g2_ref[...]
        dng = dh * (1.0 + sc_ref[...])
        dx1 = do_ref[...] + _norm_bwd(dng * g2_ref[...], n2, r2)
        dx1_ref[...] = dx1
        y1 = y1_ref[...]
        r1 = _rstd(y1)
        n1 = y1 * r1
        drn = dx1 * gt_ref[...]
        dy1 = _norm_bwd(drn * g1_ref[...], n1, r1).astype(BF16)
        dcat = _dot(dy1, wo_ref[...], NN)
        dpool_ref[...] = dcat[:, 0:256]
        lane = lax.broadcasted_iota(jnp.int32, (tm, LANES), 1)
        first = lane < HEAD_DIM
        for s in range(2):
            da = dcat[:, 256 + s * LANES:256 + (s + 1) * LANES]
            dattn_ref[s] = da
            prod = da * at_ref[:, s * LANES:(s + 1) * LANES]
            tot = jnp.sum(prod, axis=-1, keepdims=True)
            lo = jnp.sum(jnp.where(first, prod, 0.0), axis=-1, keepdims=True)
            delta_ref[s] = jnp.where(first, lo, tot - lo)
        dwo = _dot(dy1, cat_ref[...], TN)
        sums = jnp.concatenate(
            [jnp.sum(dh, axis=0, keepdims=True), jnp.sum(dh * ng, axis=0, keepdims=True),
             jnp.sum(dng * n2, axis=0, keepdims=True), jnp.sum(dx1 * (n1 * g1_ref[...]), axis=0, keepdims=True),
             jnp.sum(drn * n1, axis=0, keepdims=True), jnp.zeros((3, d), F32)], axis=0)

        @pl.when(i == 0)
        def _():
            acc_ref[...] = dwo
            sums_ref[...] = sums

        @pl.when(i > 0)
        def _():
            acc_ref[...] += dwo
            sums_ref[...] += sums

        @pl.when(i == n_t - 1)
        def _():
            dwo_ref[...] = acc_ref[...].astype(BF16)

    tile = lambda w: pl.BlockSpec((tm, w), lambda i: (i, 0))
    slab = pl.BlockSpec((2, tm, LANES), lambda i: (0, i, 0))
    vec = pl.BlockSpec((1, d), lambda i: (0, 0))
    return pl.pallas_call(
        body, name="mix_bwd", grid=(n_t,),
        in_specs=[tile(d), tile(d), tile(d), tile(d), tile(512), tile(256),
                  pl.BlockSpec((d, 512), lambda i: (0, 0)), vec, vec, vec, vec],
        out_specs=[tile(d), tile(256), slab, slab, pl.BlockSpec((d, 512), lambda i: (0, 0)),
                   pl.BlockSpec((8, d), lambda i: (0, 0))],
        out_shape=[jax.ShapeDtypeStruct((s_len, d), F32), jax.ShapeDtypeStruct((s_len, 256), F32),
                   jax.ShapeDtypeStruct((2, s_len, LANES), F32), jax.ShapeDtypeStruct((2, s_len, LANES), F32),
                   jax.ShapeDtypeStruct((d, 512), BF16), jax.ShapeDtypeStruct((8, d), F32)],
        scratch_shapes=[pltpu.VMEM((d, 512), F32)],
        compiler_params=_params(("arbitrary",)),
    )(dh2, dout, x1, y1, cat, attn, w_out_t, sc_f, g_pre_ffn, gt_m, g_post_mix)


def _pool_bwd(dpool, u_pool, w_blk, b_pool, pool_scale, tm):
    s_len = dpool.shape[0]
    n_t = s_len // tm

    def body(dp_ref, dpn_ref, u_ref, uh_ref, wb_ref, bp_ref, ps_ref, du_ref, dwb_ref, sums_ref):
        i = pl.program_id(0)
        u = u_ref[...]
        mixed, _ = _pool_mixed(u, uh_ref[...] * (i > 0).astype(F32), i, tm)
        mixed_b = mixed.astype(BF16)
        y = _dot(mixed_b, wb_ref[...], NN) + bp_ref[...]
        dp = dp_ref[...]
        dy = dp * ps_ref[...]
        dwb = _dot(mixed_b, dy.astype(BF16), TN)
        sums = jnp.concatenate([jnp.sum(dy, axis=0, keepdims=True), jnp.sum(dp * y, axis=0, keepdims=True),
                                jnp.zeros((6, 256), F32)], axis=0)
        dp_ext = jnp.concatenate([dp, dpn_ref[...] * (i < n_t - 1).astype(F32)], axis=0)
        dmix = _dot((dp_ext * ps_ref[...]).astype(BF16), wb_ref[...], NT)
        rows = tm + HALO
        grp = lax.broadcasted_iota(jnp.int32, (rows, 256), 1) // HEAD_DIM
        pick = lambda a, b, c, e: jnp.where(grp == 0, a, jnp.where(grp == 1, b, jnp.where(grp == 2, c, e)))
        pos = (i * tm + lax.broadcasted_iota(jnp.int32, (rows, 256), 0)).astype(F32)
        z = dmix / jnp.minimum(pos + 1.0, pick(*[float(w) for w in POOL_WINDOWS]))
        f2 = z + pltpu.roll(z, rows - 1, 0)
        f4 = f2 + pltpu.roll(f2, rows - 2, 0)
        f8 = f4 + pltpu.roll(f4, rows - 4, 0)
        f16 = f8 + pltpu.roll(f8, rows - 8, 0)
        du_ref[...] = (pick(f2, f4, f8, f16) - dmix)[:tm]

        @pl.when(i == 0)
        def _():
            dwb_ref[...] = dwb
            sums_ref[...] = sums

        @pl.when(i > 0)
        def _():
            dwb_ref[...] += dwb
            sums_ref[...] += sums

    tile = pl.BlockSpec((tm, 256), lambda i: (i, 0))
    const = lambda a: pl.BlockSpec(a.shape, lambda i: (0,) * a.ndim)
    return pl.pallas_call(
        body, name="pool_bwd", grid=(n_t,),
        in_specs=[tile, pl.BlockSpec((HALO, 256), lambda i: (jnp.minimum((i + 1) * (tm // HALO), s_len // HALO - 1), 0)),
                  tile, pl.BlockSpec((HALO, 256), lambda i: (_halo_before(i, tm), 0)),
                  const(w_blk), const(b_pool), const(pool_scale)],
        out_specs=[tile, pl.BlockSpec((256, 256), lambda i: (0, 0)), pl.BlockSpec((8, 256), lambda i: (0, 0))],
        out_shape=[jax.ShapeDtypeStruct((s_len, 256), F32), jax.ShapeDtypeStruct((256, 256), F32),
                   jax.ShapeDtypeStruct((8, 256), F32)],
        compiler_params=_params(("arbitrary",)),
    )(dpool, dpool, u_pool, u_pool, w_blk, b_pool, pool_scale)


def _attn_bwd(qkv, dattn, lse_all, delta, group, dil):
    s_len = qkv.shape[1]
    nb = s_len // (BLOCK * dil)

    def body(q_ref, k_ref, v_ref, do_ref, l_ref, dl_ref, dq_ref, dk_ref, dv_ref):
        lane = lax.broadcasted_iota(jnp.int32, (BLOCK, LANES), 1)
        first = lane < HEAD_DIM

        def block(t, carry):
            dk_part, dv_part = carry
            r, n = t // nb, t % nb
            cur = _block_rows(n, r, dil)
            prev = _block_rows(jnp.maximum(n - 1, 0), r, dil)
            q = q_ref[0, cur, :]
            do = do_ref[0, cur, :]
            lse = l_ref[0, cur, :]
            dlt = dl_ref[0, cur, :]
            kcat = jnp.concatenate([k_ref[0, prev, :], k_ref[0, cur, :]], axis=0).astype(BF16)
            vcat = jnp.concatenate([v_ref[0, prev, :], v_ref[0, cur, :]], axis=0).astype(BF16)
            valid = _band_mask(n)
            stack = lambda a: jnp.concatenate([jnp.where(first, a, 0.0), jnp.where(first, 0.0, a)], axis=0)
            rows2 = lambda a: jnp.concatenate([a[:, 0:1], a[:, HEAD_DIM:HEAD_DIM + 1]], axis=0)
            q2, do2 = stack(q).astype(BF16), stack(do).astype(BF16)
            valid2 = jnp.concatenate([valid, valid], axis=0)
            p = jnp.where(valid2, jnp.exp(_dot(q2, kcat, NT) - rows2(lse)), 0.0)
            ds = (p * (_dot(do2, vcat, NT) - rows2(dlt))).astype(BF16)
            dq2 = _dot(ds, kcat, NN)
            dq_ref[0, cur, :] = jnp.where(first, dq2[:BLOCK], dq2[BLOCK:])
            dkc = _dot(ds, q2, TN)
            dvc = _dot(p.astype(BF16), do2, TN)
            dk_ref[0, prev, :] = dk_part + dkc[:BLOCK]
            dv_ref[0, prev, :] = dv_part + dvc[:BLOCK]
            dk_ref[0, cur, :] = dkc[BLOCK:]
            dv_ref[0, cur, :] = dvc[BLOCK:]
            return dkc[BLOCK:], dvc[BLOCK:]

        def blocks(tt, carry):
            for u in range(ATTN_BWD_UNROLL):
                carry = block(tt * ATTN_BWD_UNROLL + u, carry)
            return carry

        zero = jnp.zeros((BLOCK, LANES), F32)
        lax.fori_loop(0, nb * dil // ATTN_BWD_UNROLL, blocks, (zero, zero))

    def slab(base):
        return pl.BlockSpec((1, s_len, LANES), lambda s: (base + 2 * group + s, 0, 0))

    one = pl.BlockSpec((1, s_len, LANES), lambda s: (s, 0, 0))
    shape = jax.ShapeDtypeStruct((2, s_len, LANES), F32)
    return pl.pallas_call(
        body, name=f"attn_bwd_d{dil}", grid=(2,),
        in_specs=[slab(0), slab(6), slab(12), one, one, one],
        out_specs=[one, one, one], out_shape=[shape, shape, shape],
        compiler_params=_params(("arbitrary",)),
    )(qkv, qkv, qkv, dattn, lse_all, delta)


def _dproj_assemble(du, dqkv, rope, tm):
    s_len = du.shape[0]
    n_proj = 256 + 18 * LANES

    def body(du_ref, *refs):
        dref, rope_ref, dproj_ref = refs[:9], refs[9], refs[10]
        dproj_ref[:, 0:256] = du_ref[...].astype(BF16)
        col = 256
        for kind in range(3):
            for grp in range(3):
                for s in range(2):
                    piece = dref[3 * grp + kind][s]
                    if kind < 2:
                        piece = _rope_bwd(piece, rope_ref)
                    if kind == 0:
                        piece = piece * (HEAD_DIM ** -0.5)
                    dproj_ref[:, col:col + LANES] = piece.astype(BF16)
                    col += LANES

    slab = pl.BlockSpec((2, tm, LANES), lambda i: (0, i, 0))
    return pl.pallas_call(
        body, name="dproj_assemble", grid=(s_len // tm,),
        in_specs=[pl.BlockSpec((tm, 256), lambda i: (i, 0))] + [slab] * 9 + [pl.BlockSpec((3, tm, LANES), lambda i: (0, i, 0))],
        out_specs=pl.BlockSpec((tm, n_proj), lambda i: (i, 0)),
        out_shape=jax.ShapeDtypeStruct((s_len, n_proj), BF16),
        compiler_params=_params(("arbitrary",)),
    )(du, *dqkv, rope)


def _inproj_bwd(dproj, w_in_t, x, dx1, sc_m, g_pre_mix, tm):
    s_len, d = x.shape
    n_proj = w_in_t.shape[0]
    n_t = s_len // tm

    def body(dproj_ref, w_ref, x_ref, dx1_ref, sc_ref, g_ref, dx_ref, sums_ref):
        i = pl.program_id(0)
        dh = _dot(dproj_ref[...], w_ref[...], NN)
        xv = x_ref[...]
        r = _rstd(xv)
        n = xv * r
        dng = dh * (1.0 + sc_ref[...])
        dx_ref[...] = dx1_ref[...] + _norm_bwd(dng * g_ref[...], n, r)
        sums = jnp.concatenate([jnp.sum(dh, axis=0, keepdims=True), jnp.sum(dh * (n * g_ref[...]), axis=0, keepdims=True),
                                jnp.sum(dng * n, axis=0, keepdims=True), jnp.zeros((5, d), F32)], axis=0)

        @pl.when(i == 0)
        def _():
            sums_ref[...] = sums

        @pl.when(i > 0)
        def _():
            sums_ref[...] += sums

    tile = lambda w: pl.BlockSpec((tm, w), lambda i: (i, 0))
    vec = pl.BlockSpec((1, d), lambda i: (0, 0))
    return pl.pallas_call(
        body, name="inproj_bwd", grid=(n_t,),
        in_specs=[tile(n_proj), pl.BlockSpec((n_proj, d), lambda i: (0, 0)), tile(d), tile(d), vec, vec],
        out_specs=[tile(d), pl.BlockSpec((8, d), lambda i: (0, 0))],
        out_shape=[jax.ShapeDtypeStruct((s_len, d), F32), jax.ShapeDtypeStruct((8, d), F32)],
        compiler_params=_params(("arbitrary",)),
    )(dproj, w_in_t, x, dx1, sc_m, g_pre_mix)


def _wgrad(a, b, name, tk, tmm):
    s_len, m = a.shape
    n = b.shape[1]
    n_k = s_len // tk

    def body(a_ref, b_ref, o_ref, acc_ref):
        k = pl.program_id(1)
        part = _dot(a_ref[...], b_ref[...], TN)

        @pl.when(k == 0)
        def _():
            acc_ref[...] = part

        @pl.when(k > 0)
        def _():
            acc_ref[...] += part

        @pl.when(k == n_k - 1)
        def _():
            o_ref[...] = acc_ref[...].astype(BF16)

    return pl.pallas_call(
        body, name=name, grid=(m // tmm, n_k),
        in_specs=[pl.BlockSpec((tk, tmm), lambda j, k: (k, j)), pl.BlockSpec((tk, n), lambda j, k: (k, 0))],
        out_specs=pl.BlockSpec((tmm, n), lambda j, k: (j, 0)),
        out_shape=jax.ShapeDtypeStruct((m, n), BF16),
        scratch_shapes=[pltpu.VMEM((tmm, n), F32)],
        compiler_params=_params(("arbitrary", "arbitrary")),
    )(a, b)


def _place():
    return lax.axis_index("x"), lax.axis_index("y"), lax.axis_index("c")


def _peer(k):
    x, y, c = _place()
    bx, by, bc = (k >> 2) & 1, (k >> 1) & 1, k & 1
    return (x ^ bx if bx else x, y ^ by if by else y, c ^ bc if bc else c)


def _index(pos):
    return 4 * pos[0] + 2 * pos[1] + pos[2]


def _ada_exchange(c_rows, w_ada, b_ada_cols, taps):
    d = c_rows.shape[1]
    ncol = w_ada.shape[1]

    def body(c_ref, w_ref, b_ref, t_ref, call_ref, mod_ref, tall_ref, stage_ref, send_sems, recv_sems):
        me = _index(_place())
        call_ref[me] = c_ref[...]
        tall_ref[me] = t_ref[...]

        def gather(k):
            return pltpu.make_async_remote_copy(
                src_ref=c_ref, dst_ref=call_ref.at[me], send_sem=send_sems.at[0, k - 1], recv_sem=recv_sems.at[0, k - 1],
                device_id=_peer(k), device_id_type=MESH)

        def gather_taps(k):
            return pltpu.make_async_remote_copy(
                src_ref=t_ref, dst_ref=tall_ref.at[me], send_sem=send_sems.at[2, k - 1], recv_sem=recv_sems.at[2, k - 1],
                device_id=_peer(k), device_id_type=MESH)

        for k in range(1, N_DEV):
            gather(k).start()
        for k in range(1, N_DEV):
            gather_taps(k).start()
        for k in range(1, N_DEV):
            gather(k).wait_recv()
        cv = jnp.concatenate([call_ref[b, 0:1, :] for b in range(N_DEV)], axis=0)
        act = cv * jax.nn.sigmoid(cv)
        mod = lax.dot_general(act, w_ref[...], NN, preferred_element_type=F32,
                              precision=lax.Precision.HIGHEST) + b_ref[...]
        for b in range(N_DEV):
            stage_ref[b] = jnp.broadcast_to(mod[b:b + 1, :], (8, ncol))
        mod_ref[me] = stage_ref[me]

        def scatter(k):
            return pltpu.make_async_remote_copy(
                src_ref=stage_ref.at[_index(_peer(k))], dst_ref=mod_ref.at[me],
                send_sem=send_sems.at[1, k - 1], recv_sem=recv_sems.at[1, k - 1],
                device_id=_peer(k), device_id_type=MESH)

        for k in range(1, N_DEV):
            scatter(k).start()
        for k in range(1, N_DEV):
            scatter(k).wait_recv()
        for k in range(1, N_DEV):
            gather_taps(k).wait_recv()
        for k in range(1, N_DEV):
            gather(k).wait_send()
            scatter(k).wait_send()
            gather_taps(k).wait_send()

    vmem = pl.BlockSpec(memory_space=pltpu.VMEM)
    return pl.pallas_call(
        body, name="ada_exchange",
        in_specs=[vmem] * 4, out_specs=[vmem] * 3,
        out_shape=[jax.ShapeDtypeStruct((N_DEV, 8, d), F32), jax.ShapeDtypeStruct((N_DEV, 8, ncol), F32),
                   jax.ShapeDtypeStruct((N_DEV,) + taps.shape, F32)],
        scratch_shapes=[pltpu.VMEM((N_DEV, 8, ncol), F32), pltpu.SemaphoreType.DMA((3, N_DEV - 1)),
                        pltpu.SemaphoreType.DMA((3, N_DEV - 1))],
        compiler_params=_params(),
    )(c_rows, w_ada, b_ada_cols, taps)


def _gather_weights(shards):
    n_w = len(shards)

    def body(*refs):
        srcs, outs = refs[:n_w], refs[n_w:2 * n_w]
        send_sems, recv_sems, local_sems = refs[2 * n_w:]
        x, y, c = _place()
        me, sibling = (x, y, c), (x, y, 1 - c)
        chips = [(1 - x, y), (x, 1 - y), (1 - x, 1 - y)]

        def rows(w, pos):
            r = shards[w].shape[0]
            return outs[w].at[pl.ds(pl.multiple_of(_index(pos) * r, 16), r), :]

        def copy(k, w, block, to, own=False):
            return pltpu.make_async_remote_copy(
                src_ref=srcs[w] if own else rows(w, block), dst_ref=rows(w, block),
                send_sem=send_sems.at[k, w], recv_sem=recv_sems.at[k, w], device_id=to, device_id_type=MESH)

        mine = [pltpu.make_async_copy(srcs[w], rows(w, me), local_sems.at[w]) for w in range(n_w)]
        for cp in mine:
            cp.start()
        first = [copy(0, w, me, sibling, own=True) for w in range(n_w)]
        first += [copy(1 + j, w, me, (*chip, c), own=True) for j, chip in enumerate(chips) for w in range(n_w)]
        for cp in first:
            cp.start()
        passed = []
        for j, chip in enumerate(chips):
            for w in range(n_w):
                copy(1 + j, w, (*chip, c), me).wait_recv()
                fwd = copy(4 + j, w, (*chip, c), sibling)
                fwd.start()
                passed.append(fwd)
        for w in range(n_w):
            copy(0, w, sibling, me).wait_recv()
        for j, chip in enumerate(chips):
            for w in range(n_w):
                copy(4 + j, w, (*chip, 1 - c), me).wait_recv()
        for cp in first + passed:
            cp.wait_send()
        for cp in mine:
            cp.wait()

    hbm = pl.BlockSpec(memory_space=pltpu.HBM)
    return pl.pallas_call(
        body, name="gather_weights",
        in_specs=[hbm] * n_w, out_specs=[hbm] * n_w,
        out_shape=[jax.ShapeDtypeStruct((N_DEV * s.shape[0], s.shape[1]), s.dtype) for s in shards],
        scratch_shapes=[pltpu.SemaphoreType.DMA((N_DEV - 1, n_w)), pltpu.SemaphoreType.DMA((N_DEV - 1, n_w)),
                        pltpu.SemaphoreType.DMA((n_w,))],
        compiler_params=_params(),
    )(*shards)


def _scatter_grads(grads):
    n_w = len(grads)

    def body(*refs):
        srcs, outs = refs[:n_w], refs[n_w:2 * n_w]
        send_sems, recv_sems, local_sems = refs[2 * n_w:]
        me = _index(_place())

        def slab(w, dev):
            r = grads[w].shape[0] // N_DEV
            return srcs[w].at[pl.ds(pl.multiple_of(dev * r, 16), r), :]

        def copy(k, w):
            return pltpu.make_async_remote_copy(
                src_ref=slab(w, _index(_peer(k))), dst_ref=outs[w].at[me],
                send_sem=send_sems.at[k - 1, w], recv_sem=recv_sems.at[k - 1, w],
                device_id=_peer(k), device_id_type=MESH)

        mine = [pltpu.make_async_copy(slab(w, me), outs[w].at[me], local_sems.at[w]) for w in range(n_w)]
        for cp in mine:
            cp.start()
        sends = [copy(k, w) for k in range(1, N_DEV) for w in range(n_w)]
        for cp in sends:
            cp.start()
        for cp in sends:
            cp.wait_recv()
        for cp in sends:
            cp.wait_send()
        for cp in mine:
            cp.wait()

    hbm = pl.BlockSpec(memory_space=pltpu.HBM)
    return pl.pallas_call(
        body, name="scatter_grads",
        in_specs=[hbm] * n_w, out_specs=[hbm] * n_w,
        out_shape=[jax.ShapeDtypeStruct((N_DEV, g.shape[0] // N_DEV, g.shape[1]), g.dtype) for g in grads],
        scratch_shapes=[pltpu.SemaphoreType.DMA((N_DEV - 1, n_w)), pltpu.SemaphoreType.DMA((N_DEV - 1, n_w)),
                        pltpu.SemaphoreType.DMA((n_w,))],
        compiler_params=_params(),
    )(*grads)


def _peer_copies(mode, srcs, lands, send_sems, recv_sems):
    me = _index(_place())
    copies = []
    for k in range(1, N_DEV):
        peer = _peer(k)
        for w, (src, land) in enumerate(zip(srcs, lands)):
            if mode == "gather":
                r = src.shape[0]
                dst = land.at[pl.ds(pl.multiple_of(me * r, 16), r), :]
            else:
                r = src.shape[0] // N_DEV
                src = src.at[pl.ds(pl.multiple_of(_index(peer) * r, 16), r), :]
                dst = land.at[me]
            copies.append(pltpu.make_async_remote_copy(
                src_ref=src, dst_ref=dst, send_sem=send_sems.at[(k - 1) * len(srcs) + w],
                recv_sem=recv_sems.at[(k - 1) * len(srcs) + w],
                device_id=peer, device_id_type=MESH))
    return copies


def _landing_zone(mode, src, me, name):
    cols = src.shape[1]
    if mode == "gather":
        r = src.shape[0]
        in_spec = pl.BlockSpec((r, cols), lambda i, me_ref: (0, 0))
        out_spec = pl.BlockSpec((r, cols), lambda i, me_ref: (me_ref[0], 0))
        out_shape = jax.ShapeDtypeStruct((N_DEV * r, cols), src.dtype)
    else:
        r = src.shape[0] // N_DEV
        in_spec = pl.BlockSpec((r, cols), lambda i, me_ref: (me_ref[0], 0))
        out_spec = pl.BlockSpec((1, r, cols), lambda i, me_ref: (me_ref[0], 0, 0))
        out_shape = jax.ShapeDtypeStruct((N_DEV, r, cols), src.dtype)

    def body(me_ref, s_ref, o_ref):
        o_ref[...] = s_ref[...].reshape(o_ref.shape)

    return pl.pallas_call(
        body, name=name, out_shape=out_shape,
        grid_spec=pltpu.PrefetchScalarGridSpec(num_scalar_prefetch=1, grid=(1,), in_specs=[in_spec], out_specs=out_spec),
        compiler_params=_params(("arbitrary",)),
    )(me.reshape(1).astype(jnp.int32), src)


def _exchange_start(mode, srcs, lands, name):
    n = len(srcs)

    def body(*refs):
        for cp in _peer_copies(mode, refs[:n], refs[n:2 * n], refs[2 * n], refs[2 * n + 1]):
            cp.start()
        refs[-1][...] = jnp.zeros_like(refs[-1])

    hbm, sem = pl.BlockSpec(memory_space=pltpu.HBM), pl.BlockSpec(memory_space=pltpu.SEMAPHORE)
    arrays = list(srcs) + list(lands)
    out = pl.pallas_call(
        body, name=name,
        out_shape=(pltpu.SemaphoreType.DMA(((N_DEV - 1) * n,)), pltpu.SemaphoreType.DMA(((N_DEV - 1) * n,)),
                   *[pltpu.HBM(a.shape, a.dtype) for a in arrays], jax.ShapeDtypeStruct((8, LANES), F32)),
        in_specs=[hbm] * (2 * n), out_specs=(sem, sem, *[hbm] * (2 * n), pl.BlockSpec(memory_space=pltpu.VMEM)),
        input_output_aliases={i: 2 + i for i in range(2 * n)},
        compiler_params=pltpu.CompilerParams(has_side_effects=pltpu.SideEffectType.DATAFLOW_SIDE_EFFECTING),
    )(*[pltpu.with_memory_space_constraint(a, pltpu.HBM) for a in arrays])
    return out[0], out[1], out[2:2 + n], out[2 + n:2 + 2 * n], out[-1]


def _exchange_wait(mode, send_sems, recv_sems, srcs, lands, after, name):
    n = len(srcs)

    def body(*refs):
        copies = _peer_copies(mode, refs[:n], refs[n:2 * n], refs[2 * n], refs[2 * n + 1])
        for cp in copies:
            cp.wait_send()
        for cp in copies:
            cp.wait_recv()

    hbm, sem = pl.BlockSpec(memory_space=pltpu.HBM), pl.BlockSpec(memory_space=pltpu.SEMAPHORE)
    arrays = list(srcs) + list(lands)
    out = pl.pallas_call(
        body, name=name, out_shape=tuple(pltpu.HBM(a.shape, a.dtype) for a in arrays),
        in_specs=[hbm] * (2 * n) + [sem, sem, pl.BlockSpec(memory_space=pl.ANY)], out_specs=tuple([hbm] * (2 * n)),
        input_output_aliases={i: i for i in range(2 * n)},
        compiler_params=pltpu.CompilerParams(has_side_effects=pltpu.SideEffectType.DATAFLOW_SIDE_EFFECTING),
    )(*arrays, send_sems, recv_sems, after)
    return out[n:]


SMALL_WEIGHTS = ("b_ada", "g_pre_mix", "g_post_mix", "g_pre_ffn", "g_post_ffn", "w_pool", "b_pool", "pool_scale", "conv_b")


MOD_ROWS = ((0, 0), (0, 1), (1, 3), (1, 0), (1, 1), (2, 0))


def _small_reduce(sums_in, sums_mix, sums_ffn, sums_pool, dw_blk, dconv, loss_loc):
    locals_ = [sums_in, sums_mix, sums_ffn, sums_pool, dw_blk, dconv, loss_loc]
    n_l = len(locals_)
    d = sums_in.shape[1]
    diag = (len(POOL_WINDOWS), HEAD_DIM, HEAD_DIM)
    shapes = [diag if a is dw_blk else a.shape for a in locals_]

    def body(*refs):
        loc = refs[:n_l]
        tot_refs = refs[n_l:2 * n_l]
        dmod_ref = refs[2 * n_l]
        gathered = refs[2 * n_l + 1:3 * n_l + 1]
        diag_ref, send_sems, recv_sems = refs[-3:]
        me = _index(_place())
        blk = loc[4][...]
        for gi in range(len(POOL_WINDOWS)):
            lo = gi * HEAD_DIM
            diag_ref[gi] = blk[lo:lo + HEAD_DIM, lo:lo + HEAD_DIM]
        loc = loc[:4] + (diag_ref,) + loc[5:]

        def copy(a, k):
            return pltpu.make_async_remote_copy(
                src_ref=loc[a], dst_ref=gathered[a].at[me], send_sem=send_sems.at[a, k - 1],
                recv_sem=recv_sems.at[a, k - 1], device_id=_peer(k), device_id_type=MESH)

        copies = [copy(a, k) for k in range(1, N_DEV) for a in range(n_l)]
        for cp in copies:
            cp.start()
        for a in range(n_l):
            gathered[a][me] = loc[a][...]
        for cp in copies:
            cp.wait_recv()

        def total(a):
            tot = gathered[a][0]
            for dev in range(1, N_DEV):
                tot = tot + gathered[a][dev]
            return tot

        for a in range(n_l):
            tot_refs[a][...] = total(a)
        for dev in range(N_DEV):
            for k, (a, r) in enumerate(MOD_ROWS):
                dmod_ref[dev:dev + 1, k * d:(k + 1) * d] = gathered[a][dev, r:r + 1, :]
        for cp in copies:
            cp.wait_send()

    vmem = pl.BlockSpec(memory_space=pltpu.VMEM)
    out = pl.pallas_call(
        body, name="small_reduce",
        in_specs=[vmem] * n_l, out_specs=[vmem] * (n_l + 1),
        out_shape=[jax.ShapeDtypeStruct(s, F32) for s in shapes] + [jax.ShapeDtypeStruct((N_DEV, 6 * d), F32)],
        scratch_shapes=[pltpu.VMEM((N_DEV,) + s, F32) for s in shapes]
        + [pltpu.VMEM(diag, F32), pltpu.SemaphoreType.DMA((n_l, N_DEV - 1)), pltpu.SemaphoreType.DMA((n_l, N_DEV - 1))],
        compiler_params=_params(),
    )(*locals_)
    return out[:n_l], out[n_l]


def _small_adam(totals, weights, moms, vels):
    n_t, n_w = len(totals), len(weights)

    def body(*refs):
        t_in, t_mix, t_ffn, t_pool, t_blk, t_conv, _ = (r[...] for r in refs[:n_t])
        w_refs, m_refs, v_refs = (refs[n_t + k * n_w:n_t + (k + 1) * n_w] for k in range(3))
        outs = refs[n_t + 3 * n_w:]

        def update(idx, g, at=()):
            sel = lambda ref: ref.at[at] if at else ref
            delta, nm, nv = _adam_math(sel(w_refs[idx])[...], g, sel(m_refs[idx])[...], sel(v_refs[idx])[...])
            for k, val in enumerate((g, delta, nm, nv)):
                sel(outs[4 * idx + k])[...] = val

        tots = (t_in, t_mix, t_ffn)
        update(0, jnp.concatenate([tots[a][r:r + 1] for a, r in MOD_ROWS], axis=1))
        update(1, t_in[2:3])
        update(2, t_mix[4:5])
        update(3, t_mix[2:3])
        update(4, t_ffn[1:2])
        for gi in range(len(POOL_WINDOWS)):
            update(5, t_blk[gi], at=(0, gi))
        update(6, jnp.concatenate([t_pool[0:1, gi * HEAD_DIM:(gi + 1) * HEAD_DIM] for gi in range(len(POOL_WINDOWS))], axis=0),
               at=(0,))
        update(7, t_pool[1:2])
        update(8, t_conv[3:4])

    vmem = pl.BlockSpec(memory_space=pltpu.VMEM)
    return pl.pallas_call(
        body, name="small_adam", in_specs=[vmem] * (n_t + 3 * n_w), out_specs=[vmem] * (4 * n_w),
        out_shape=[jax.ShapeDtypeStruct(w.shape, F32) for w in weights for _ in range(4)],
        compiler_params=_params(),
    )(*totals, *weights, *moms, *vels)


def _adam_math(w, g, m, v):
    m = ADAM_B1 * m + (1.0 - ADAM_B1) * g
    v = ADAM_B2 * v + (1.0 - ADAM_B2) * (g * g)
    m_hat = m / (1.0 - ADAM_B1 ** ADAM_STEP)
    v_hat = v / (1.0 - ADAM_B2 ** ADAM_STEP)
    delta = -ADAM_LR * (m_hat / (jnp.sqrt(v_hat) + ADAM_EPS) + ADAM_WD * w)
    return delta, m, v


def _adam(w, g, m, v, name, tr):
    rows, cols = w.shape

    def body(w_ref, g_ref, m_ref, v_ref, d_ref, nm_ref, nv_ref):
        d_ref[...], nm_ref[...], nv_ref[...] = _adam_math(w_ref[...], g_ref[...], m_ref[...], v_ref[...])

    spec = pl.BlockSpec((tr, cols), lambda i: (i, 0))
    shape = jax.ShapeDtypeStruct((rows, cols), F32)
    return pl.pallas_call(
        body, name=name, grid=(rows // tr,), in_specs=[spec] * 4, out_specs=[spec] * 3,
        out_shape=[shape] * 3, compiler_params=_params(("arbitrary",)),
    )(w, g, m, v)


def _sum_adam(parts, w, m, v, name, tr):
    _, rows, cols = parts.shape

    def body(p_ref, w_ref, m_ref, v_ref, g_ref, d_ref, nm_ref, nv_ref):
        g = p_ref[0].astype(F32)
        for dev in range(1, N_DEV):
            g = g + p_ref[dev].astype(F32)
        g_ref[...] = g
        d_ref[...], nm_ref[...], nv_ref[...] = _adam_math(w_ref[...], g, m_ref[...], v_ref[...])

    spec = pl.BlockSpec((tr, cols), lambda i: (i, 0))
    shape = jax.ShapeDtypeStruct((rows, cols), F32)
    return pl.pallas_call(
        body, name=name, grid=(rows // tr,),
        in_specs=[pl.BlockSpec((N_DEV, tr, cols), lambda i: (0, i, 0)), spec, spec, spec],
        out_specs=[spec] * 4, out_shape=[shape] * 4, compiler_params=_params(("arbitrary",)),
    )(parts, w, m, v)


def _ada_grad_adam(c_all, dmod_cols, w, m, v, tr):
    rows, cols = w.shape

    def body(c_ref, dm_ref, w_ref, m_ref, v_ref, g_ref, d_ref, nm_ref, nv_ref):
        cv = c_ref[...]
        act = cv * jax.nn.sigmoid(cv)
        g = lax.dot_general(act, dm_ref[...], TN, preferred_element_type=F32, precision=lax.Precision.HIGHEST)
        g_ref[...] = g
        d_ref[...], nm_ref[...], nv_ref[...] = _adam_math(w_ref[...], g, m_ref[...], v_ref[...])

    spec = pl.BlockSpec((tr, cols), lambda i: (i, 0))
    shape = jax.ShapeDtypeStruct((rows, cols), F32)
    return pl.pallas_call(
        body, name="ada_grad_adam", grid=(rows // tr,),
        in_specs=[pl.BlockSpec((N_DEV, tr), lambda i: (0, i)), pl.BlockSpec((N_DEV, cols), lambda i: (0, 0)), spec, spec, spec],
        out_specs=[spec] * 4, out_shape=[shape] * 4, compiler_params=_params(("arbitrary",)),
    )(c_all, dmod_cols, w, m, v)


def _rope_tables(positions):
    s_len = positions.shape[0]
    inv_freq = ROPE_THETA ** (-jnp.arange(0, 2 * ROT_HALF, 2, dtype=F32) / (2 * ROT_HALF))
    ang = positions.astype(F32)[:, None] * inv_freq
    cos, sin = jnp.cos(ang), jnp.sin(ang)
    rest = HEAD_DIM - 2 * ROT_HALF
    zero = lambda n: jnp.zeros((s_len, n), F32)
    head = jnp.stack([jnp.concatenate([cos, cos, jnp.ones((s_len, rest), F32)], axis=1),
                      jnp.concatenate([-sin, zero(HEAD_DIM - ROT_HALF)], axis=1),
                      jnp.concatenate([zero(ROT_HALF), sin, zero(rest)], axis=1)])
    return jnp.tile(head, (1, 1, LANES // HEAD_DIM))


def _pad_rows(a, rows):
    return jnp.pad(a, ((0, rows - a.shape[0]), (0, 0)))


def _as_rows(a, rows):
    flat = a.reshape(-1)
    return jnp.pad(flat, (0, rows * LANES - flat.shape[0])).reshape(rows, LANES)


def _sequence_step(xs, target, rope, mods, gains, w_in_t, w_out_t, fetch_ffn, send_ffn_grads, send_mix_grads, w_blk_b, b_pool_r,
                   pool_scale_r, conv_w_all, conv_b):
    sh_m, sc_m, gt_m, sh_f, sc_f, gt_f = mods
    g_pre_mix, g_post_mix, g_pre_ffn, g_post_ffn = gains
    h1, u_pool, qkv = _premix_inproj(xs, sh_m, sc_m, g_pre_mix, w_in_t, rope, tm=512)
    o_g, lse_g = [], []
    for gi, dil in enumerate(DILATIONS):
        o, lse = _attn_fwd(qkv, gi, dil)
        o_g.append(o)
        lse_g.append(lse)
    x1, y1, h2, cat, attn, lse_all = _mix_out(xs, u_pool, o_g, lse_g, w_blk_b, b_pool_r, pool_scale_r, w_out_t,
                                              gt_m, g_post_mix, g_pre_ffn, sc_f, sh_f, tm=256)
    w_up_t, w_down_f = fetch_ffn(x1)
    gate, a_ffn, act, vd, dy2, dout, sums_ffn, loss_loc = _ffn_fwd_loss(h2, x1, target, w_up_t, w_down_f, conv_w_all, conv_b,
                                                              gt_f, g_post_ffn, tm=256, tf=2816, ck=256)

    dgc, dval, dw_down, dconv = _ffn_bwd_act(dy2, gate, a_ffn, act, vd, w_down_f, tm=1024, tf=256)
    dup, dh2 = _ffn_bwd_up(dgc, dval, w_up_t, conv_w_all, tm=256)
    dw_up_t = _wgrad(dup, h2, "wgrad_up", tk=1024, tmm=1408)
    token = send_ffn_grads(dw_up_t, dw_down)
    if token is not None:
        sc_f = sc_f + token[0:1, 0:1]
    dx1, dpool, dattn, delta, dw_out_t, sums_mix = _mix_bwd(dh2, dout, x1, y1, cat, attn, w_out_t, sc_f, g_pre_ffn,
                                                           gt_m, g_post_mix, tm=256)
    du, dw_blk, sums_pool = _pool_bwd(dpool, u_pool, w_blk_b, b_pool_r, pool_scale_r, tm=512)
    dqkv = []
    for gi, dil in enumerate(DILATIONS):
        dqkv += list(_attn_bwd(qkv, dattn, lse_all, delta, gi, dil))
    dproj = _dproj_assemble(du, dqkv, rope, tm=512)
    dw_in_t = _wgrad(dproj, h1, "wgrad_in", tk=1024, tmm=1280)
    token = send_mix_grads(dw_in_t, dw_out_t)
    if token is not None:
        sc_m = sc_m + token[0:1, 0:1]
    grad_x, sums_in = _inproj_bwd(dproj, w_in_t, xs, dx1, sc_m, g_pre_mix, tm=256)
    return (loss_loc, grad_x, dw_in_t, dw_out_t, dw_up_t, dw_down, dw_blk, dconv,
            sums_in, sums_mix, sums_ffn, sums_pool)


def kernel(x, c, positions, w_ada, b_ada, g_pre_mix, g_post_mix, g_pre_ffn, g_post_ffn, w_in, w_pool, b_pool, pool_scale, w_out, w_up, conv_w, conv_b, w_down, loss_target, m_w_ada, m_b_ada, m_g_pre_mix, m_g_post_mix, m_g_pre_ffn, m_g_post_ffn, m_w_in, m_w_pool, m_b_pool, m_pool_scale, m_w_out, m_w_up, m_conv_w, m_conv_b, m_w_down, v_w_ada, v_b_ada, v_g_pre_mix, v_g_post_mix, v_g_pre_ffn, v_g_post_ffn, v_w_in, v_w_pool, v_b_pool, v_pool_scale, v_w_out, v_w_up, v_conv_w, v_conv_b, v_w_down):
    s_len, d = x.shape[1], x.shape[2]
    d_ff = w_down.shape[1] * N_DEV
    me = _index(_place())
    xs, target = x[0], loss_target[0]

    ncol = w_ada.shape[2]
    b_cols = lax.dynamic_slice(b_ada, (0, me * ncol), (1, ncol))
    c_all, mod, taps_all = _ada_exchange(jnp.broadcast_to(c, (8, d)), w_ada[0], b_cols, _pad_rows(conv_w[0], 8))
    c_all = c_all[:, 0, :]
    conv_w_all = jnp.transpose(taps_all[:, :3, :], (1, 0, 2)).reshape(3, d_ff)
    sh_m, sc_m, gt_m, sh_f, sc_f, gt_f = [mod[:, 0, :].reshape(1, -1)[:, k * d:(k + 1) * d] for k in range(6)]

    w_in_t, w_out_t = _gather_weights([w_in[0].T.astype(BF16), w_out[0].T.astype(BF16)])

    rope = _rope_tables(positions[0])
    w_blk = jnp.zeros((256, 256), F32)
    for gi in range(4):
        w_blk = lax.dynamic_update_slice(w_blk, w_pool[0, gi], (gi * HEAD_DIM, gi * HEAD_DIM))
    w_blk_b = w_blk.astype(BF16)
    b_pool_r, pool_scale_r = b_pool.reshape(1, 256), pool_scale.reshape(1, 256)

    up_sh, down_sh = w_up[0].T.astype(BF16), w_down[0].astype(BF16)
    w_in_t, conv_w_all, up_sh, down_sh = lax.optimization_barrier((w_in_t, conv_w_all, up_sh, down_sh))
    lands = [_landing_zone("gather", s, me, "land_" + nm) for s, nm in ((up_sh, "w_up"), (down_sh, "w_down"))]
    w_send, w_recv, w_src, w_land, w_token = _exchange_start("gather", [up_sh, down_sh], lands, "ffn_weights_start")

    def fetch_ffn(after):
        return _exchange_wait("gather", w_send, w_recv, w_src, w_land, after, "ffn_weights_wait")

    flight = []

    def send_ffn_grads(dw_up_t, dw_down):
        lands = [_landing_zone("scatter", dw_up_t, me, "land_dw_up"), _landing_zone("scatter", dw_down, me, "land_dw_down")]
        flight.extend(_exchange_start("scatter", [dw_up_t, dw_down], lands, "ffn_grads_start"))
        return flight[4]

    mix_flight = []

    def send_mix_grads(dw_in_t, dw_out_t):
        lands = [_landing_zone("scatter", dw_in_t, me, "land_dw_in"), _landing_zone("scatter", dw_out_t, me, "land_dw_out")]
        mix_flight.extend(_exchange_start("scatter", [dw_in_t, dw_out_t], lands, "mix_grads_start"))
        return mix_flight[4]

    (loss_loc, grad_x, dw_in_t, dw_out_t, _, _, dw_blk, dconv,
     sums_in, sums_mix, sums_ffn, sums_pool) = _sequence_step(
        xs, target, rope, (sh_m + w_token[0:1, 0:1], sc_m, gt_m, sh_f, sc_f, gt_f),
        (g_pre_mix, g_post_mix, g_pre_ffn, g_post_ffn),
        w_in_t, w_out_t, fetch_ffn, send_ffn_grads, send_mix_grads, w_blk_b, b_pool_r, pool_scale_r, conv_w_all, conv_b)

    parts_ffn = _exchange_wait("scatter", *flight[:4], grad_x, "ffn_grads_wait")
    big = {
        "w_up": [a.T for a in _sum_adam(parts_ffn[0], w_up[0].T, m_w_up[0].T, v_w_up[0].T, "adam_w_up", 64)],
        "w_down": _sum_adam(parts_ffn[1], w_down[0], m_w_down[0], v_w_down[0], "adam_w_down", 32),
    }

    rep_w = [b_ada, g_pre_mix, g_post_mix, g_pre_ffn, g_post_ffn, w_pool, b_pool, pool_scale, conv_b]
    rep_m = [m_b_ada, m_g_pre_mix, m_g_post_mix, m_g_pre_ffn, m_g_post_ffn, m_w_pool, m_b_pool, m_pool_scale, m_conv_b]
    rep_v = [v_b_ada, v_g_pre_mix, v_g_post_mix, v_g_pre_ffn, v_g_post_ffn, v_w_pool, v_b_pool, v_pool_scale, v_conv_b]
    totals, dmod_all = _small_reduce(sums_in, sums_mix, sums_ffn, sums_pool, dw_blk, dconv, loss_loc)
    dconv_tot, loss_tot = totals[5], totals[6]
    rep_out = _small_adam(totals, rep_w, rep_m, rep_v)
    g_rep, d_rep, nm_rep, nv_rep = (rep_out[k::4] for k in range(4))

    fcol = d_ff // N_DEV
    g_cw = lax.dynamic_slice(dconv_tot, (0, me * fcol), (3, fcol))
    d_cw, nm_cw, nv_cw = _adam(conv_w[0], g_cw, m_conv_w[0], v_conv_w[0], "adam_conv_w", 3)

    dmod_cols = lax.dynamic_slice(dmod_all, (0, me * ncol), (N_DEV, ncol))
    g_ada, d_ada, nm_ada, nv_ada = _ada_grad_adam(c_all, dmod_cols, w_ada[0], m_w_ada[0], v_w_ada[0], 256)

    parts_mix = _exchange_wait("scatter", *mix_flight[:4], g_ada, "mix_grads_wait")
    big["w_in"] = [a.T for a in _sum_adam(parts_mix[0], w_in[0].T, m_w_in[0].T, v_w_in[0].T, "adam_w_in", 64)]
    big["w_out"] = [a.T for a in _sum_adam(parts_mix[1], w_out[0].T, m_w_out[0].T, v_w_out[0].T, "adam_w_out", 128)]

    loss = loss_tot[0, 0]

    def group(k):
        rep = (g_rep, d_rep, nm_rep, nv_rep)[k]
        ada = (g_ada, d_ada, nm_ada, nv_ada)[k][None]
        cw = (g_cw, d_cw, nm_cw, nv_cw)[k][None]
        return [ada, rep[0], rep[1], rep[2], rep[3], rep[4], big["w_in"][k][None], rep[5], rep[6], rep[7],
                big["w_out"][k][None], big["w_up"][k][None], cw, rep[8], big["w_down"][k][None]]

    return (loss, grad_x[None], *group(0), *group(1), *group(2), *group(3))
```

```python
import functools
import math

import jax
import jax.numpy as jnp
from jax import lax
from jax.experimental import pallas as pl
from jax.experimental.pallas import tpu as pltpu

F32 = jnp.float32
BF16 = jnp.bfloat16
MESH = pl.DeviceIdType.MESH

N_DEV = 8
HEAD_DIM = 64
ROT_HALF = 8
ROPE_THETA = 500000.0
POOL_WINDOWS = (2, 4, 8, 16)
DILATIONS = (1, 4, 16)
BLOCK = 128
NORM_EPS = 1e-6
HALO = 16
MASKED = -1e30
ATTN_FWD_UNROLL = 4
ATTN_BWD_UNROLL = 2

ADAM_LR = 0.001
ADAM_B1 = 0.9
ADAM_B2 = 0.999
ADAM_EPS = 1e-08
ADAM_WD = 0.01
ADAM_STEP = 10

V7X_VMEM_LIMIT = 56 * 1024 * 1024
LANES = 128

NT = (((1,), (1,)), ((), ()))
NN = (((1,), (0,)), ((), ()))
TN = (((0,), (0,)), ((), ()))


def _dot(a, b, dims):
    return lax.dot_general(a, b, dims, preferred_element_type=F32)


def _params(sem=None, vmem=V7X_VMEM_LIMIT):
    if sem is None:
        return pltpu.CompilerParams(vmem_limit_bytes=vmem)
    return pltpu.CompilerParams(dimension_semantics=sem, vmem_limit_bytes=vmem)


def _rstd(v):
    return lax.rsqrt(jnp.mean(v * v, axis=-1, keepdims=True) + NORM_EPS)


def _norm_bwd(dn, n, rstd):
    return rstd * (dn - n * jnp.mean(dn * n, axis=-1, keepdims=True))


def _rope_fwd(p, rope_ref):
    return p * rope_ref[0] + pltpu.roll(p, LANES - ROT_HALF, 1) * rope_ref[1] + pltpu.roll(p, ROT_HALF, 1) * rope_ref[2]


def _rope_bwd(dp, rope_ref):
    return dp * rope_ref[0] + pltpu.roll(dp * rope_ref[1], ROT_HALF, 1) + pltpu.roll(dp * rope_ref[2], LANES - ROT_HALF, 1)


def _gelu_parts(v):
    k2 = 2.0 * math.sqrt(2.0 / math.pi)
    c = 0.044715
    v2 = v * v
    s = jax.nn.sigmoid(v * (k2 + (k2 * c) * v2))
    g = v * s
    dg = s + g * (1.0 - s) * (k2 + (3.0 * k2 * c) * v2)
    return g, dg


def _halo_before(i, tile):
    return jnp.maximum(i * (tile // HALO) - 1, 0)


def _premix_inproj(x, sh, sc, g, w_in_t, rope, tm):
    s_len, d = x.shape
    n_proj = w_in_t.shape[0]
    n_slab = (n_proj - 256) // LANES

    def body(x_ref, sh_ref, sc_ref, g_ref, w_ref, rope_ref, h_ref, up_ref, qkv_ref):
        xv = x_ref[...]
        h = (xv * _rstd(xv) * g_ref[...]) * (1.0 + sc_ref[...]) + sh_ref[...]
        hb = h.astype(BF16)
        h_ref[...] = hb
        up_ref[...] = _dot(hb, w_ref[0:256, :], NT)
        for pair in range(n_slab // 2):
            p = _dot(hb, w_ref[256 + 256 * pair:512 + 256 * pair, :], NT)
            for half in range(2):
                ph = p[:, half * LANES:(half + 1) * LANES]
                if pair < 6:
                    ph = _rope_fwd(ph, rope_ref)
                if pair < 3:
                    ph = ph * (HEAD_DIM ** -0.5)
                qkv_ref[2 * pair + half] = ph

    vec = pl.BlockSpec((1, d), lambda i: (0, 0))
    return pl.pallas_call(
        body, name="premix_inproj", grid=(s_len // tm,),
        in_specs=[pl.BlockSpec((tm, d), lambda i: (i, 0)), vec, vec, vec,
                  pl.BlockSpec((n_proj, d), lambda i: (0, 0)),
                  pl.BlockSpec((3, tm, LANES), lambda i: (0, i, 0))],
        out_specs=[pl.BlockSpec((tm, d), lambda i: (i, 0)),
                   pl.BlockSpec((tm, 256), lambda i: (i, 0)),
                   pl.BlockSpec((n_slab, tm, LANES), lambda i: (0, i, 0))],
        out_shape=[jax.ShapeDtypeStruct((s_len, d), BF16),
                   jax.ShapeDtypeStruct((s_len, 256), F32),
                   jax.ShapeDtypeStruct((n_slab, s_len, LANES), F32)],
        compiler_params=_params(("arbitrary",)),
    )(x, sh, sc, g, w_in_t, rope)


def _block_rows(n, r, dil):
    start = n * (BLOCK * dil) + r
    if dil == 1:
        return pl.ds(pl.multiple_of(start, BLOCK), BLOCK)
    return pl.ds(start, BLOCK, stride=dil)


def _band_mask(n):
    ri = lax.broadcasted_iota(jnp.int32, (BLOCK, 2 * BLOCK), 0)
    cj = lax.broadcasted_iota(jnp.int32, (BLOCK, 2 * BLOCK), 1)
    cur = (cj >= BLOCK) & (cj - BLOCK <= ri)
    prev = (cj < BLOCK) & (cj >= ri) & (n > 0)
    return cur | prev


def _attn_fwd(qkv, group, dil):
    s_len = qkv.shape[1]
    nb = s_len // (BLOCK * dil)

    def body(q_ref, k_ref, v_ref, o_ref, lse_ref):
        lane = lax.broadcasted_iota(jnp.int32, (BLOCK, LANES), 1)
        first = lane < HEAD_DIM

        def block(t, carry):
            r, n = t // nb, t % nb
            cur = _block_rows(n, r, dil)
            prev = _block_rows(jnp.maximum(n - 1, 0), r, dil)
            q = q_ref[0, cur, :]
            kcat = jnp.concatenate([k_ref[0, prev, :], k_ref[0, cur, :]], axis=0).astype(BF16)
            vcat = jnp.concatenate([v_ref[0, prev, :], v_ref[0, cur, :]], axis=0).astype(BF16)
            valid = _band_mask(n)
            q2 = jnp.concatenate([jnp.where(first, q, 0.0), jnp.where(first, 0.0, q)], axis=0).astype(BF16)
            s = jnp.where(jnp.concatenate([valid, valid], axis=0), _dot(q2, kcat, NT), MASKED)
            m = jnp.max(s, axis=-1, keepdims=True)
            p = jnp.exp(s - m)
            den = jnp.sum(p, axis=-1, keepdims=True)
            o2 = _dot(p.astype(BF16), vcat, NN) / den
            lse2 = m + jnp.log(den)
            o_ref[0, cur, :] = jnp.where(first, o2[:BLOCK], o2[BLOCK:])
            lse_ref[0, cur, :] = jnp.where(first, lse2[:BLOCK], lse2[BLOCK:])
            return carry

        lax.fori_loop(0, nb * dil, block, 0, unroll=ATTN_FWD_UNROLL)

    def slab(base):
        return pl.BlockSpec((1, s_len, LANES), lambda s: (base + 2 * group + s, 0, 0))

    out = pl.BlockSpec((1, s_len, LANES), lambda s: (s, 0, 0))
    shape = jax.ShapeDtypeStruct((2, s_len, LANES), F32)
    return pl.pallas_call(
        body, name=f"attn_fwd_d{dil}", grid=(2,),
        in_specs=[slab(0), slab(6), slab(12)], out_specs=[out, out], out_shape=[shape, shape],
        compiler_params=_params(("arbitrary",)),
    )(qkv, qkv, qkv)


def _pool_mixed(u, halo, i, tm):
    ue = jnp.concatenate([halo, u], axis=0)
    s2 = ue + pltpu.roll(ue, 1, 0)
    s4 = s2 + pltpu.roll(s2, 2, 0)
    s8 = s4 + pltpu.roll(s4, 4, 0)
    s16 = s8 + pltpu.roll(s8, 8, 0)
    grp = lax.broadcasted_iota(jnp.int32, (tm, 256), 1) // HEAD_DIM
    pick = lambda a, b, c, e: jnp.where(grp == 0, a, jnp.where(grp == 1, b, jnp.where(grp == 2, c, e)))
    win_sum = pick(s2[HALO:], s4[HALO:], s8[HALO:], s16[HALO:])
    pos = (i * tm + lax.broadcasted_iota(jnp.int32, (tm, 256), 0)).astype(F32)
    count = jnp.minimum(pos + 1.0, pick(*[float(w) for w in POOL_WINDOWS]))
    return win_sum / count - u, count


def _mix_out(x, u_pool, o_g, lse_g, w_blk, b_pool, pool_scale, w_out_t, gt_m, g_post_mix, g_pre_ffn, sc_f, sh_f, tm):
    s_len, d = x.shape

    def body(x_ref, u_ref, uh_ref, o0, o1, o2, l0, l1, l2, wb_ref, bp_ref, ps_ref, wo_ref,
             gt_ref, g1_ref, g2_ref, sc_ref, sh_ref,
             x1_ref, y1_ref, h2_ref, cat_ref, attn_ref, lall_ref):
        i = pl.program_id(0)
        u = u_ref[...]
        halo = uh_ref[...] * (i > 0).astype(F32)
        mixed, _ = _pool_mixed(u, halo, i, tm)
        y = _dot(mixed.astype(BF16), wb_ref[...], NN) + bp_ref[...]
        pool = y * ps_ref[...]
        attn = []
        for s in range(2):
            la, lb, lc = l0[s], l1[s], l2[s]
            mx = jnp.maximum(jnp.maximum(la, lb), lc)
            ea, eb, ec = jnp.exp(la - mx), jnp.exp(lb - mx), jnp.exp(lc - mx)
            den = ea + eb + ec
            lall_ref[s] = mx + jnp.log(den)
            attn.append((ea / den) * o0[s] + (eb / den) * o1[s] + (ec / den) * o2[s])
        attn = jnp.concatenate(attn, axis=1)
        attn_ref[...] = attn
        cat = jnp.concatenate([pool, attn], axis=1).astype(BF16)
        cat_ref[...] = cat
        y1 = _dot(cat, wo_ref[...], NT)
        y1_ref[...] = y1
        x1 = x_ref[...] + gt_ref[...] * (y1 * _rstd(y1) * g1_ref[...])
        x1_ref[...] = x1
        h2 = (x1 * _rstd(x1) * g2_ref[...]) * (1.0 + sc_ref[...]) + sh_ref[...]
        h2_ref[...] = h2.astype(BF16)

    tile = lambda w: pl.BlockSpec((tm, w), lambda i: (i, 0))
    slab = pl.BlockSpec((2, tm, LANES), lambda i: (0, i, 0))
    const = lambda a: pl.BlockSpec(a.shape, lambda i: (0,) * a.ndim)
    return pl.pallas_call(
        body, name="mix_out", grid=(s_len // tm,),
        in_specs=[tile(d), tile(256), pl.BlockSpec((HALO, 256), lambda i: (_halo_before(i, tm), 0)),
                  slab, slab, slab, slab, slab, slab,
                  const(w_blk), const(b_pool), const(pool_scale), const(w_out_t),
                  const(gt_m), const(g_post_mix), const(g_pre_ffn), const(sc_f), const(sh_f)],
        out_specs=[tile(d), tile(d), tile(d), tile(512), tile(256), slab],
        out_shape=[jax.ShapeDtypeStruct((s_len, d), F32), jax.ShapeDtypeStruct((s_len, d), F32),
                   jax.ShapeDtypeStruct((s_len, d), BF16), jax.ShapeDtypeStruct((s_len, 512), BF16),
                   jax.ShapeDtypeStruct((s_len, 256), F32), jax.ShapeDtypeStruct((2, s_len, LANES), F32)],
        compiler_params=_params(("arbitrary",)),
    )(x, u_pool, u_pool, *o_g, *lse_g, w_blk, b_pool, pool_scale, w_out_t, gt_m, g_post_mix, g_pre_ffn, sc_f, sh_f)


def _conv_gate(gate_ext, cw, cb):
    gc = gate_ext * cw[2:3, :] + pltpu.roll(gate_ext, 1, 0) * cw[1:2, :] + pltpu.roll(gate_ext, 2, 0) * cw[0:1, :]
    return gc[HALO:] + cb


def _ffn_fwd_loss(h2, x1, target, w_up_t, w_down, conv_w, conv_b, gt_f, g_post_ffn, tm, tf, ck):
    s_len, d = x1.shape
    d_ff = w_down.shape[0]
    n_f = d_ff // tf

    def body(h_ref, hh_ref, x1_ref, tgt_ref, wg_ref, wv_ref, wd_ref, cw_ref, cb_ref, gt_ref, g_ref,
             gate_ref, a_ref, act_ref, vd_ref, dy2_ref, dout_ref, sums_ref, loss_ref, acc_ref):
        i, j = pl.program_id(0), pl.program_id(1)

        @pl.when((i == 0) & (j == 0))
        def _():
            sums_ref[...] = jnp.zeros_like(sums_ref)
            loss_ref[...] = jnp.zeros_like(loss_ref)

        h = h_ref[...]
        h_ext = jnp.concatenate([hh_ref[...], h], axis=0)
        row = lax.broadcasted_iota(jnp.int32, (tm + HALO, ck), 0)
        no_halo = (row < HALO) & (i == 0)

        def up(c):
            cs = slice(c * ck, (c + 1) * ck)
            return jnp.where(no_halo, 0.0, _dot(h_ext, wg_ref[cs, :], NT)), _dot(h, wv_ref[cs, :], NT)

        part = None
        n_c = tf // ck
        nxt = up(0)
        for c in range(n_c):
            cs = slice(c * ck, (c + 1) * ck)
            gate_ext, val = nxt
            if c + 1 < n_c:
                nxt = up(c + 1)
            act, dact = _gelu_parts(_conv_gate(gate_ext, cw_ref[:, cs], cb_ref[:, cs]))
            a = (act * val).astype(BF16)
            gate_ref[:, cs] = gate_ext[HALO:].astype(BF16)
            a_ref[:, cs] = a
            act_ref[:, cs] = act.astype(BF16)
            vd_ref[:, cs] = (val * dact).astype(BF16)
            p = _dot(a, wd_ref[cs, :], NN)
            part = p if part is None else part + p

        @pl.when(j == 0)
        def _():
            acc_ref[...] = part

        @pl.when(j > 0)
        def _():
            acc_ref[...] += part

        @pl.when(j == n_f - 1)
        def _():
            y2 = acc_ref[...]
            rstd = _rstd(y2)
            n = y2 * rstd
            rn = n * g_ref[...]
            err = x1_ref[...] + gt_ref[...] * rn - tgt_ref[...]
            loss_ref[...] += 0.5 * jnp.sum(jnp.mean(err * err, axis=-1, keepdims=True), axis=0, keepdims=True)
            dout = err * (1.0 / d)
            dout_ref[...] = dout
            drn = dout * gt_ref[...]
            sums_ref[0:1, :] += jnp.sum(dout * rn, axis=0, keepdims=True)
            sums_ref[1:2, :] += jnp.sum(drn * n, axis=0, keepdims=True)
            dy2_ref[...] = _norm_bwd(drn * g_ref[...], n, rstd).astype(BF16)

    tok = lambda w: pl.BlockSpec((tm, w), lambda i, j: (i, 0))
    tokf = pl.BlockSpec((tm, tf), lambda i, j: (i, j))
    vec = pl.BlockSpec((1, d), lambda i, j: (0, 0))
    once = {"pipeline_mode": pl.Buffered(1)} if n_f == 1 else {}
    return pl.pallas_call(
        body, name="ffn_fwd_loss", grid=(s_len // tm, n_f),
        in_specs=[tok(d), pl.BlockSpec((HALO, d), lambda i, j: (_halo_before(i, tm), 0)), tok(d), tok(d),
                  pl.BlockSpec((tf, d), lambda i, j: (j, 0), **once),
                  pl.BlockSpec((tf, d), lambda i, j: (j + n_f, 0), **once),
                  pl.BlockSpec((tf, d), lambda i, j: (j, 0), **once),
                  pl.BlockSpec((3, tf), lambda i, j: (0, j)), pl.BlockSpec((1, tf), lambda i, j: (0, j)), vec, vec],
        out_specs=[tokf, tokf, tokf, tokf, tok(d), tok(d), pl.BlockSpec((8, d), lambda i, j: (0, 0)),
                   pl.BlockSpec((8, LANES), lambda i, j: (0, 0))],
        out_shape=[jax.ShapeDtypeStruct((s_len, d_ff), BF16)] * 4
        + [jax.ShapeDtypeStruct((s_len, d), BF16), jax.ShapeDtypeStruct((s_len, d), F32),
                   jax.ShapeDtypeStruct((8, d), F32), jax.ShapeDtypeStruct((8, LANES), F32)],
        scratch_shapes=[pltpu.VMEM((tm, d), F32)],
        compiler_params=_params(("arbitrary", "arbitrary")),
    )(h2, h2, x1, target, w_up_t, w_up_t, w_down, conv_w, conv_b, gt_f, g_post_ffn)


def _ffn_bwd_act(dy2, gate, a, act, vd, w_down, tm, tf):
    s_len, d = dy2.shape
    d_ff = w_down.shape[0]
    n_t = s_len // tm

    def body(dy_ref, g_ref, gh_ref, a_ref, act_ref, vd_ref, wd_ref, dgc_ref, dval_ref, dwd_ref, dconv_ref, acc_ref):
        i = pl.program_id(1)
        gate_ext = jnp.concatenate([gh_ref[...], g_ref[...]], axis=0).astype(F32)
        row = lax.broadcasted_iota(jnp.int32, gate_ext.shape, 0)
        gate_ext = jnp.where((row < HALO) & (i == 0), 0.0, gate_ext)
        da = _dot(dy_ref[...], wd_ref[...], NT)
        dwd = _dot(a_ref[...], dy_ref[...], TN)
        dgc = da * vd_ref[...].astype(F32)
        dgc_ref[...] = dgc.astype(BF16)
        dval_ref[...] = (da * act_ref[...].astype(F32)).astype(BF16)
        rows = [jnp.sum(dgc * pltpu.roll(gate_ext, 2 - k, 0)[HALO:], axis=0, keepdims=True) for k in range(2)]
        rows += [jnp.sum(dgc * gate_ext[HALO:], axis=0, keepdims=True), jnp.sum(dgc, axis=0, keepdims=True),
                 jnp.zeros((4, tf), F32)]
        dconv = jnp.concatenate(rows, axis=0)

        @pl.when(i == 0)
        def _():
            acc_ref[...] = dwd
            dconv_ref[...] = dconv

        @pl.when(i > 0)
        def _():
            acc_ref[...] += dwd
            dconv_ref[...] += dconv

        @pl.when(i == n_t - 1)
        def _():
            dwd_ref[...] = acc_ref[...].astype(BF16)

    tokf = pl.BlockSpec((tm, tf), lambda j, i: (i, j))
    return pl.pallas_call(
        body, name="ffn_bwd_act", grid=(d_ff // tf, n_t),
        in_specs=[pl.BlockSpec((tm, d), lambda j, i: (i, 0)), tokf,
                  pl.BlockSpec((HALO, tf), lambda j, i: (_halo_before(i, tm), j)), tokf, tokf, tokf,
                  pl.BlockSpec((tf, d), lambda j, i: (j, 0))],
        out_specs=[tokf, tokf, pl.BlockSpec((tf, d), lambda j, i: (j, 0)), pl.BlockSpec((8, tf), lambda j, i: (0, j))],
        out_shape=[jax.ShapeDtypeStruct((s_len, d_ff), BF16), jax.ShapeDtypeStruct((s_len, d_ff), BF16),
                   jax.ShapeDtypeStruct((d_ff, d), BF16), jax.ShapeDtypeStruct((8, d_ff), F32)],
        scratch_shapes=[pltpu.VMEM((tf, d), F32)],
        compiler_params=_params(("arbitrary", "arbitrary")),
    )(dy2, gate, gate, a, act, vd, w_down)


def _ffn_bwd_up(dgc, dval, w_up_t, conv_w, tm):
    s_len, d_ff = dgc.shape
    d = w_up_t.shape[1]
    n_t = s_len // tm

    def body(dg_ref, dgn_ref, dv_ref, cw_ref, w_ref, dup_ref, dh_ref):
        i = pl.program_id(0)
        nxt = dgn_ref[...].astype(F32) * (i < n_t - 1).astype(F32)
        ext = jnp.concatenate([dg_ref[...].astype(F32), nxt], axis=0)
        rows = tm + HALO
        dgate = (ext * cw_ref[2:3, :] + pltpu.roll(ext, rows - 1, 0) * cw_ref[1:2, :]
                 + pltpu.roll(ext, rows - 2, 0) * cw_ref[0:1, :])[:tm]
        dup = jnp.concatenate([dgate.astype(BF16), dv_ref[...]], axis=1)
        dup_ref[...] = dup
        dh_ref[...] = _dot(dup, w_ref[...], NN)

    tokf = pl.BlockSpec((tm, d_ff), lambda i: (i, 0))
    return pl.pallas_call(
        body, name="ffn_bwd_up", grid=(n_t,),
        in_specs=[tokf, pl.BlockSpec((HALO, d_ff), lambda i: (jnp.minimum((i + 1) * (tm // HALO), s_len // HALO - 1), 0)),
                  tokf, pl.BlockSpec((3, d_ff), lambda i: (0, 0)), pl.BlockSpec((2 * d_ff, d), lambda i: (0, 0))],
        out_specs=[pl.BlockSpec((tm, 2 * d_ff), lambda i: (i, 0)), pl.BlockSpec((tm, d), lambda i: (i, 0))],
        out_shape=[jax.ShapeDtypeStruct((s_len, 2 * d_ff), BF16), jax.ShapeDtypeStruct((s_len, d), F32)],
        compiler_params=_params(("arbitrary",)),
    )(dgc, dgc, dval, conv_w, w_up_t)


def _mix_bwd(dh2, dout, x1, y1, cat, attn, w_out_t, sc_f, g_pre_ffn, gt_m, g_post_mix, tm):
    s_len, d = x1.shape
    n_t = s_len // tm

    def body(dh_ref, do_ref, x1_ref, y1_ref, cat_ref, at_ref, wo_ref, sc_ref, g2_ref, gt_ref, g1_ref,
             dx1_ref, dpool_ref, dattn_ref, delta_ref, dwo_ref, sums_ref, acc_ref):
        i = pl.program_id(0)
        dh = dh_ref[...]
        x1 = x1_ref[...]
        r2 = _rstd(x1)
        n2 = x1 * r2
        ng = n2 * g2_ref[...]
        dng = dh * (1.0 + sc_ref[...])
        dx1 = do_ref[...] + _norm_bwd(dng * g2_ref[...], n2, r2)
        dx1_ref[...] = dx1
        y1 = y1_ref[...]
        r1 = _rstd(y1)
        n1 = y1 * r1
        drn = dx1 * gt_ref[...]
        dy1 = _norm_bwd(drn * g1_ref[...], n1, r1).astype(BF16)
        dcat = _dot(dy1, wo_ref[...], NN)
        dpool_ref[...] = dcat[:, 0:256]
        lane = lax.broadcasted_iota(jnp.int32, (tm, LANES), 1)
        first = lane < HEAD_DIM
        for s in range(2):
            da = dcat[:, 256 + s * LANES:256 + (s + 1) * LANES]
            dattn_ref[s] = da
            prod = da * at_ref[:, s * LANES:(s + 1) * LANES]
            tot = jnp.sum(prod, axis=-1, keepdims=True)
            lo = jnp.sum(jnp.where(first, prod, 0.0), axis=-1, keepdims=True)
            delta_ref[s] = jnp.where(first, lo, tot - lo)
        dwo = _dot(dy1, cat_ref[...], TN)
        sums = jnp.concatenate(
            [jnp.sum(dh, axis=0, keepdims=True), jnp.sum(dh * ng, axis=0, keepdims=True),
             jnp.sum(dng * n2, axis=0, keepdims=True), jnp.sum(dx1 * (n1 * g1_ref[...]), axis=0, keepdims=True),
             jnp.sum(drn * n1, axis=0, keepdims=True), jnp.zeros((3, d), F32)], axis=0)

        @pl.when(i == 0)
        def _():
            acc_ref[...] = dwo
            sums_ref[...] = sums

        @pl.when(i > 0)
        def _():
            acc_ref[...] += dwo
            sums_ref[...] += sums

        @pl.when(i == n_t - 1)
        def _():
            dwo_ref[...] = acc_ref[...].astype(BF16)

    tile = lambda w: pl.BlockSpec((tm, w), lambda i: (i, 0))
    slab = pl.BlockSpec((2, tm, LANES), lambda i: (0, i, 0))
    vec = pl.BlockSpec((1, d), lambda i: (0, 0))
    return pl.pallas_call(
        body, name="mix_bwd", grid=(n_t,),
        in_specs=[tile(d), tile(d), tile(d), tile(d), tile(512), tile(256),
                  pl.BlockSpec((d, 512), lambda i: (0, 0)), vec, vec, vec, vec],
        out_specs=[tile(d), tile(256), slab, slab, pl.BlockSpec((d, 512), lambda i: (0, 0)),
                   pl.BlockSpec((8, d), lambda i: (0, 0))],
        out_shape=[jax.ShapeDtypeStruct((s_len, d), F32), jax.ShapeDtypeStruct((s_len, 256), F32),
                   jax.ShapeDtypeStruct((2, s_len, LANES), F32), jax.ShapeDtypeStruct((2, s_len, LANES), F32),
                   jax.ShapeDtypeStruct((d, 512), BF16), jax.ShapeDtypeStruct((8, d), F32)],
        scratch_shapes=[pltpu.VMEM((d, 512), F32)],
        compiler_params=_params(("arbitrary",)),
    )(dh2, dout, x1, y1, cat, attn, w_out_t, sc_f, g_pre_ffn, gt_m, g_post_mix)


def _pool_bwd(dpool, u_pool, w_blk, b_pool, pool_scale, tm):
    s_len = dpool.shape[0]
    n_t = s_len // tm

    def body(dp_ref, dpn_ref, u_ref, uh_ref, wb_ref, bp_ref, ps_ref, du_ref, dwp_ref, sums_ref, acc_ref):
        i = pl.program_id(0)
        u = u_ref[...]
        mixed, _ = _pool_mixed(u, uh_ref[...] * (i > 0).astype(F32), i, tm)
        mixed_b = mixed.astype(BF16)
        y = _dot(mixed_b, wb_ref[...], NN) + bp_ref[...]
        dp = dp_ref[...]
        dy = dp * ps_ref[...]
        dwb = _dot(mixed_b, dy.astype(BF16), TN)
        sums = jnp.concatenate([jnp.sum(dy, axis=0, keepdims=True), jnp.sum(dp * y, axis=0, keepdims=True),
                                jnp.zeros((6, 256), F32)], axis=0)
        dp_ext = jnp.concatenate([dp, dpn_ref[...] * (i < n_t - 1).astype(F32)], axis=0)
        dmix = _dot((dp_ext * ps_ref[...]).astype(BF16), wb_ref[...], NT)
        rows = tm + HALO
        grp = lax.broadcasted_iota(jnp.int32, (rows, 256), 1) // HEAD_DIM
        pick = lambda a, b, c, e: jnp.where(grp == 0, a, jnp.where(grp == 1, b, jnp.where(grp == 2, c, e)))
        pos = (i * tm + lax.broadcasted_iota(jnp.int32, (rows, 256), 0)).astype(F32)
        z = dmix / jnp.minimum(pos + 1.0, pick(*[float(w) for w in POOL_WINDOWS]))
        f2 = z + pltpu.roll(z, rows - 1, 0)
        f4 = f2 + pltpu.roll(f2, rows - 2, 0)
        f8 = f4 + pltpu.roll(f4, rows - 4, 0)
        f16 = f8 + pltpu.roll(f8, rows - 8, 0)
        du_ref[...] = (pick(f2, f4, f8, f16) - dmix)[:tm]

        @pl.when(i == 0)
        def _():
            acc_ref[...] = dwb
            sums_ref[...] = sums

        @pl.when(i > 0)
        def _():
            acc_ref[...] += dwb
            sums_ref[...] += sums

        @pl.when(i == n_t - 1)
        def _():
            full = acc_ref[...]
            for gi in range(len(POOL_WINDOWS)):
                lo = gi * HEAD_DIM
                dwp_ref[gi] = full[lo:lo + HEAD_DIM, lo:lo + HEAD_DIM]

    n_g = len(POOL_WINDOWS)
    tile = pl.BlockSpec((tm, 256), lambda i: (i, 0))
    const = lambda a: pl.BlockSpec(a.shape, lambda i: (0,) * a.ndim)
    return pl.pallas_call(
        body, name="pool_bwd", grid=(n_t,),
        in_specs=[tile, pl.BlockSpec((HALO, 256), lambda i: (jnp.minimum((i + 1) * (tm // HALO), s_len // HALO - 1), 0)),
                  tile, pl.BlockSpec((HALO, 256), lambda i: (_halo_before(i, tm), 0)),
                  const(w_blk), const(b_pool), const(pool_scale)],
        out_specs=[tile, pl.BlockSpec((n_g, HEAD_DIM, HEAD_DIM), lambda i: (0, 0, 0)), pl.BlockSpec((8, 256), lambda i: (0, 0))],
        out_shape=[jax.ShapeDtypeStruct((s_len, 256), F32), jax.ShapeDtypeStruct((n_g, HEAD_DIM, HEAD_DIM), F32),
                   jax.ShapeDtypeStruct((8, 256), F32)],
        scratch_shapes=[pltpu.VMEM((256, 256), F32)],
        compiler_params=_params(("arbitrary",)),
    )(dpool, dpool, u_pool, u_pool, w_blk, b_pool, pool_scale)


def _attn_bwd(qkv, dattn, lse_all, delta, group, dil):
    s_len = qkv.shape[1]
    nb = s_len // (BLOCK * dil)

    def body(q_ref, k_ref, v_ref, do_ref, l_ref, dl_ref, dq_ref, dk_ref, dv_ref):
        lane = lax.broadcasted_iota(jnp.int32, (BLOCK, LANES), 1)
        first = lane < HEAD_DIM

        def block(t, carry):
            dk_part, dv_part = carry
            r, n = t // nb, t % nb
            cur = _block_rows(n, r, dil)
            prev = _block_rows(jnp.maximum(n - 1, 0), r, dil)
            q = q_ref[0, cur, :]
            do = do_ref[0, cur, :]
            lse = l_ref[0, cur, :]
            dlt = dl_ref[0, cur, :]
            kcat = jnp.concatenate([k_ref[0, prev, :], k_ref[0, cur, :]], axis=0).astype(BF16)
            vcat = jnp.concatenate([v_ref[0, prev, :], v_ref[0, cur, :]], axis=0).astype(BF16)
            valid = _band_mask(n)
            stack = lambda a: jnp.concatenate([jnp.where(first, a, 0.0), jnp.where(first, 0.0, a)], axis=0)
            rows2 = lambda a: jnp.concatenate([a[:, 0:1], a[:, HEAD_DIM:HEAD_DIM + 1]], axis=0)
            q2, do2 = stack(q).astype(BF16), stack(do).astype(BF16)
            valid2 = jnp.concatenate([valid, valid], axis=0)
            p = jnp.where(valid2, jnp.exp(_dot(q2, kcat, NT) - rows2(lse)), 0.0)
            ds = (p * (_dot(do2, vcat, NT) - rows2(dlt))).astype(BF16)
            dq2 = _dot(ds, kcat, NN)
            dq_ref[0, cur, :] = jnp.where(first, dq2[:BLOCK], dq2[BLOCK:])
            dkc = _dot(ds, q2, TN)
            dvc = _dot(p.astype(BF16), do2, TN)
            dk_ref[0, prev, :] = dk_part + dkc[:BLOCK]
            dv_ref[0, prev, :] = dv_part + dvc[:BLOCK]
            dk_ref[0, cur, :] = dkc[BLOCK:]
            dv_ref[0, cur, :] = dvc[BLOCK:]
            return dkc[BLOCK:], dvc[BLOCK:]

        def blocks(tt, carry):
            for u in range(ATTN_BWD_UNROLL):
                carry = block(tt * ATTN_BWD_UNROLL + u, carry)
            return carry

        zero = jnp.zeros((BLOCK, LANES), F32)
        lax.fori_loop(0, nb * dil // ATTN_BWD_UNROLL, blocks, (zero, zero))

    def slab(base):
        return pl.BlockSpec((1, s_len, LANES), lambda s: (base + 2 * group + s, 0, 0))

    one = pl.BlockSpec((1, s_len, LANES), lambda s: (s, 0, 0))
    shape = jax.ShapeDtypeStruct((2, s_len, LANES), F32)
    return pl.pallas_call(
        body, name=f"attn_bwd_d{dil}", grid=(2,),
        in_specs=[slab(0), slab(6), slab(12), one, one, one],
        out_specs=[one, one, one], out_shape=[shape, shape, shape],
        compiler_params=_params(("arbitrary",)),
    )(qkv, qkv, qkv, dattn, lse_all, delta)


def _dproj_assemble(du, dqkv, rope, tm):
    s_len = du.shape[0]
    n_proj = 256 + 18 * LANES

    def body(du_ref, *refs):
        dref, rope_ref, dproj_ref = refs[:9], refs[9], refs[10]
        dproj_ref[:, 0:256] = du_ref[...].astype(BF16)
        col = 256
        for kind in range(3):
            for grp in range(3):
                for s in range(2):
                    piece = dref[3 * grp + kind][s]
                    if kind < 2:
                        piece = _rope_bwd(piece, rope_ref)
                    if kind == 0:
                        piece = piece * (HEAD_DIM ** -0.5)
                    dproj_ref[:, col:col + LANES] = piece.astype(BF16)
                    col += LANES

    slab = pl.BlockSpec((2, tm, LANES), lambda i: (0, i, 0))
    return pl.pallas_call(
        body, name="dproj_assemble", grid=(s_len // tm,),
        in_specs=[pl.BlockSpec((tm, 256), lambda i: (i, 0))] + [slab] * 9 + [pl.BlockSpec((3, tm, LANES), lambda i: (0, i, 0))],
        out_specs=pl.BlockSpec((tm, n_proj), lambda i: (i, 0)),
        out_shape=jax.ShapeDtypeStruct((s_len, n_proj), BF16),
        compiler_params=_params(("arbitrary",)),
    )(du, *dqkv, rope)


def _inproj_bwd(dproj, w_in_t, x, dx1, sc_m, g_pre_mix, tm):
    s_len, d = x.shape
    n_proj = w_in_t.shape[0]
    n_t = s_len // tm

    def body(dproj_ref, w_ref, x_ref, dx1_ref, sc_ref, g_ref, dx_ref, sums_ref):
        i = pl.program_id(0)
        dh = _dot(dproj_ref[...], w_ref[...], NN)
        xv = x_ref[...]
        r = _rstd(xv)
        n = xv * r
        dng = dh * (1.0 + sc_ref[...])
        dx_ref[...] = dx1_ref[...] + _norm_bwd(dng * g_ref[...], n, r)
        sums = jnp.concatenate([jnp.sum(dh, axis=0, keepdims=True), jnp.sum(dh * (n * g_ref[...]), axis=0, keepdims=True),
                                jnp.sum(dng * n, axis=0, keepdims=True), jnp.zeros((5, d), F32)], axis=0)

        @pl.when(i == 0)
        def _():
            sums_ref[...] = sums

        @pl.when(i > 0)
        def _():
            sums_ref[...] += sums

    tile = lambda w: pl.BlockSpec((tm, w), lambda i: (i, 0))
    vec = pl.BlockSpec((1, d), lambda i: (0, 0))
    return pl.pallas_call(
        body, name="inproj_bwd", grid=(n_t,),
        in_specs=[tile(n_proj), pl.BlockSpec((n_proj, d), lambda i: (0, 0)), tile(d), tile(d), vec, vec],
        out_specs=[tile(d), pl.BlockSpec((8, d), lambda i: (0, 0))],
        out_shape=[jax.ShapeDtypeStruct((s_len, d), F32), jax.ShapeDtypeStruct((8, d), F32)],
        compiler_params=_params(("arbitrary",)),
    )(dproj, w_in_t, x, dx1, sc_m, g_pre_mix)


def _wgrad(a, b, name, tk, tmm):
    s_len, m = a.shape
    n = b.shape[1]
    n_k = s_len // tk

    def body(a_ref, b_ref, o_ref, acc_ref):
        k = pl.program_id(1)
        part = _dot(a_ref[...], b_ref[...], TN)

        @pl.when(k == 0)
        def _():
            acc_ref[...] = part

        @pl.when(k > 0)
        def _():
            acc_ref[...] += part

        @pl.when(k == n_k - 1)
        def _():
            o_ref[...] = acc_ref[...].astype(BF16)

    return pl.pallas_call(
        body, name=name, grid=(m // tmm, n_k),
        in_specs=[pl.BlockSpec((tk, tmm), lambda j, k: (k, j)), pl.BlockSpec((tk, n), lambda j, k: (k, 0))],
        out_specs=pl.BlockSpec((tmm, n), lambda j, k: (j, 0)),
        out_shape=jax.ShapeDtypeStruct((m, n), BF16),
        scratch_shapes=[pltpu.VMEM((tmm, n), F32)],
        compiler_params=_params(("arbitrary", "arbitrary")),
    )(a, b)


def _place():
    return lax.axis_index("x"), lax.axis_index("y"), lax.axis_index("c")


def _peer(k):
    x, y, c = _place()
    bx, by, bc = (k >> 2) & 1, (k >> 1) & 1, k & 1
    return (x ^ bx if bx else x, y ^ by if by else y, c ^ bc if bc else c)


def _index(pos):
    return 4 * pos[0] + 2 * pos[1] + pos[2]


def _ada_exchange(c_rows, w_ada, b_ada_cols, taps):
    d = c_rows.shape[1]
    ncol = w_ada.shape[1]

    def body(c_ref, w_ref, b_ref, t_ref, call_ref, mod_ref, tall_ref, stage_ref, send_sems, recv_sems):
        me = _index(_place())
        call_ref[me] = c_ref[...]
        tall_ref[me] = t_ref[...]

        def gather(k):
            return pltpu.make_async_remote_copy(
                src_ref=c_ref, dst_ref=call_ref.at[me], send_sem=send_sems.at[0, k - 1], recv_sem=recv_sems.at[0, k - 1],
                device_id=_peer(k), device_id_type=MESH)

        def gather_taps(k):
            return pltpu.make_async_remote_copy(
                src_ref=t_ref, dst_ref=tall_ref.at[me], send_sem=send_sems.at[2, k - 1], recv_sem=recv_sems.at[2, k - 1],
                device_id=_peer(k), device_id_type=MESH)

        for k in range(1, N_DEV):
            gather(k).start()
        for k in range(1, N_DEV):
            gather_taps(k).start()
        for k in range(1, N_DEV):
            gather(k).wait_recv()
        cv = jnp.concatenate([call_ref[b, 0:1, :] for b in range(N_DEV)], axis=0)
        act = cv * jax.nn.sigmoid(cv)
        mod = lax.dot_general(act, w_ref[...], NN, preferred_element_type=F32,
                              precision=lax.Precision.HIGHEST) + b_ref[...]
        for b in range(N_DEV):
            stage_ref[b] = jnp.broadcast_to(mod[b:b + 1, :], (8, ncol))
        mod_ref[me] = stage_ref[me]

        def scatter(k):
            return pltpu.make_async_remote_copy(
                src_ref=stage_ref.at[_index(_peer(k))], dst_ref=mod_ref.at[me],
                send_sem=send_sems.at[1, k - 1], recv_sem=recv_sems.at[1, k - 1],
                device_id=_peer(k), device_id_type=MESH)

        for k in range(1, N_DEV):
            scatter(k).start()
        for k in range(1, N_DEV):
            scatter(k).wait_recv()
        for k in range(1, N_DEV):
            gather_taps(k).wait_recv()
        for k in range(1, N_DEV):
            gather(k).wait_send()
            scatter(k).wait_send()
            gather_taps(k).wait_send()

    vmem = pl.BlockSpec(memory_space=pltpu.VMEM)
    return pl.pallas_call(
        body, name="ada_exchange",
        in_specs=[vmem] * 4, out_specs=[vmem] * 3,
        out_shape=[jax.ShapeDtypeStruct((N_DEV, 8, d), F32), jax.ShapeDtypeStruct((N_DEV, 8, ncol), F32),
                   jax.ShapeDtypeStruct((N_DEV,) + taps.shape, F32)],
        scratch_shapes=[pltpu.VMEM((N_DEV, 8, ncol), F32), pltpu.SemaphoreType.DMA((3, N_DEV - 1)),
                        pltpu.SemaphoreType.DMA((3, N_DEV - 1))],
        compiler_params=_params(),
    )(c_rows, w_ada, b_ada_cols, taps)


def _gather_weights(shards):
    n_w = len(shards)

    def body(*refs):
        srcs, outs = refs[:n_w], refs[n_w:2 * n_w]
        send_sems, recv_sems, local_sems = refs[2 * n_w:]
        x, y, c = _place()
        me, sibling = (x, y, c), (x, y, 1 - c)
        chips = [(1 - x, y), (x, 1 - y), (1 - x, 1 - y)]

        def rows(w, pos):
            r = shards[w].shape[0]
            return outs[w].at[pl.ds(pl.multiple_of(_index(pos) * r, 16), r), :]

        def copy(k, w, block, to, own=False):
            return pltpu.make_async_remote_copy(
                src_ref=srcs[w] if own else rows(w, block), dst_ref=rows(w, block),
                send_sem=send_sems.at[k, w], recv_sem=recv_sems.at[k, w], device_id=to, device_id_type=MESH)

        mine = [pltpu.make_async_copy(srcs[w], rows(w, me), local_sems.at[w]) for w in range(n_w)]
        for cp in mine:
            cp.start()
        first = [copy(0, w, me, sibling, own=True) for w in range(n_w)]
        first += [copy(1 + j, w, me, (*chip, c), own=True) for j, chip in enumerate(chips) for w in range(n_w)]
        for cp in first:
            cp.start()
        passed = []
        for j, chip in enumerate(chips):
            for w in range(n_w):
                copy(1 + j, w, (*chip, c), me).wait_recv()
                fwd = copy(4 + j, w, (*chip, c), sibling)
                fwd.start()
                passed.append(fwd)
        for w in range(n_w):
            copy(0, w, sibling, me).wait_recv()
        for j, chip in enumerate(chips):
            for w in range(n_w):
                copy(4 + j, w, (*chip, 1 - c), me).wait_recv()
        for cp in first + passed:
            cp.wait_send()
        for cp in mine:
            cp.wait()

    hbm = pl.BlockSpec(memory_space=pltpu.HBM)
    return pl.pallas_call(
        body, name="gather_weights",
        in_specs=[hbm] * n_w, out_specs=[hbm] * n_w,
        out_shape=[jax.ShapeDtypeStruct((N_DEV * s.shape[0], s.shape[1]), s.dtype) for s in shards],
        scratch_shapes=[pltpu.SemaphoreType.DMA((N_DEV - 1, n_w)), pltpu.SemaphoreType.DMA((N_DEV - 1, n_w)),
                        pltpu.SemaphoreType.DMA((n_w,))],
        compiler_params=_params(),
    )(*shards)


def _scatter_grads(grads):
    n_w = len(grads)

    def body(*refs):
        srcs, outs = refs[:n_w], refs[n_w:2 * n_w]
        send_sems, recv_sems, local_sems = refs[2 * n_w:]
        me = _index(_place())

        def slab(w, dev):
            r = grads[w].shape[0] // N_DEV
            return srcs[w].at[pl.ds(pl.multiple_of(dev * r, 16), r), :]

        def copy(k, w):
            return pltpu.make_async_remote_copy(
                src_ref=slab(w, _index(_peer(k))), dst_ref=outs[w].at[me],
                send_sem=send_sems.at[k - 1, w], recv_sem=recv_sems.at[k - 1, w],
                device_id=_peer(k), device_id_type=MESH)

        mine = [pltpu.make_async_copy(slab(w, me), outs[w].at[me], local_sems.at[w]) for w in range(n_w)]
        for cp in mine:
            cp.start()
        sends = [copy(k, w) for k in range(1, N_DEV) for w in range(n_w)]
        for cp in sends:
            cp.start()
        for cp in sends:
            cp.wait_recv()
        for cp in sends:
            cp.wait_send()
        for cp in mine:
            cp.wait()

    hbm = pl.BlockSpec(memory_space=pltpu.HBM)
    return pl.pallas_call(
        body, name="scatter_grads",
        in_specs=[hbm] * n_w, out_specs=[hbm] * n_w,
        out_shape=[jax.ShapeDtypeStruct((N_DEV, g.shape[0] // N_DEV, g.shape[1]), g.dtype) for g in grads],
        scratch_shapes=[pltpu.SemaphoreType.DMA((N_DEV - 1, n_w)), pltpu.SemaphoreType.DMA((N_DEV - 1, n_w)),
                        pltpu.SemaphoreType.DMA((n_w,))],
        compiler_params=_params(),
    )(*grads)


def _peer_copies(mode, srcs, lands, send_sems, recv_sems):
    me = _index(_place())
    copies = []
    for k in range(1, N_DEV):
        peer = _peer(k)
        for w, (src, land) in enumerate(zip(srcs, lands)):
            if mode == "gather":
                r = src.shape[0]
                dst = land.at[pl.ds(pl.multiple_of(me * r, 16), r), :]
            elif mode == "allgather":
                dst = land.at[me]
            else:
                r = src.shape[0] // N_DEV
                src = src.at[pl.ds(pl.multiple_of(_index(peer) * r, 16), r), :]
                dst = land.at[me]
            copies.append(pltpu.make_async_remote_copy(
                src_ref=src, dst_ref=dst, send_sem=send_sems.at[(k - 1) * len(srcs) + w],
                recv_sem=recv_sems.at[(k - 1) * len(srcs) + w],
                device_id=peer, device_id_type=MESH))
    return copies


def _landing_zone(mode, src, me, name):
    cols = src.shape[1]
    if mode == "gather":
        r = src.shape[0]
        in_spec = pl.BlockSpec((r, cols), lambda i, me_ref: (0, 0))
        out_spec = pl.BlockSpec((r, cols), lambda i, me_ref: (me_ref[0], 0))
        out_shape = jax.ShapeDtypeStruct((N_DEV * r, cols), src.dtype)
    else:
        r = src.shape[0] // N_DEV
        in_spec = pl.BlockSpec((r, cols), lambda i, me_ref: (me_ref[0], 0))
        out_spec = pl.BlockSpec((1, r, cols), lambda i, me_ref: (me_ref[0], 0, 0))
        out_shape = jax.ShapeDtypeStruct((N_DEV, r, cols), src.dtype)

    def body(me_ref, s_ref, o_ref):
        o_ref[...] = s_ref[...].reshape(o_ref.shape)

    return pl.pallas_call(
        body, name=name, out_shape=out_shape,
        grid_spec=pltpu.PrefetchScalarGridSpec(num_scalar_prefetch=1, grid=(1,), in_specs=[in_spec], out_specs=out_spec),
        compiler_params=_params(("arbitrary",)),
    )(me.reshape(1).astype(jnp.int32), src)


def _exchange_start(mode, srcs, lands, name):
    n = len(srcs)

    def body(*refs):
        for cp in _peer_copies(mode, refs[:n], refs[n:2 * n], refs[2 * n], refs[2 * n + 1]):
            cp.start()
        refs[-1][...] = jnp.zeros_like(refs[-1])

    hbm, sem = pl.BlockSpec(memory_space=pltpu.HBM), pl.BlockSpec(memory_space=pltpu.SEMAPHORE)
    arrays = list(srcs) + list(lands)
    out = pl.pallas_call(
        body, name=name,
        out_shape=(pltpu.SemaphoreType.DMA(((N_DEV - 1) * n,)), pltpu.SemaphoreType.DMA(((N_DEV - 1) * n,)),
                   *[pltpu.HBM(a.shape, a.dtype) for a in arrays], jax.ShapeDtypeStruct((8, LANES), F32)),
        in_specs=[hbm] * (2 * n), out_specs=(sem, sem, *[hbm] * (2 * n), pl.BlockSpec(memory_space=pltpu.VMEM)),
        input_output_aliases={i: 2 + i for i in range(2 * n)},
        compiler_params=pltpu.CompilerParams(has_side_effects=pltpu.SideEffectType.DATAFLOW_SIDE_EFFECTING),
    )(*[pltpu.with_memory_space_constraint(a, pltpu.HBM) for a in arrays])
    return out[0], out[1], out[2:2 + n], out[2 + n:2 + 2 * n], out[-1]


def _exchange_wait(mode, send_sems, recv_sems, srcs, lands, after, name):
    n = len(srcs)

    def body(*refs):
        copies = _peer_copies(mode, refs[:n], refs[n:2 * n], refs[2 * n], refs[2 * n + 1])
        for cp in copies:
            cp.wait_send()
        for cp in copies:
            cp.wait_recv()

    hbm, sem = pl.BlockSpec(memory_space=pltpu.HBM), pl.BlockSpec(memory_space=pltpu.SEMAPHORE)
    arrays = list(srcs) + list(lands)
    out = pl.pallas_call(
        body, name=name, out_shape=tuple(pltpu.HBM(a.shape, a.dtype) for a in arrays),
        in_specs=[hbm] * (2 * n) + [sem, sem, pl.BlockSpec(memory_space=pl.ANY)], out_specs=tuple([hbm] * (2 * n)),
        input_output_aliases={i: i for i in range(2 * n)},
        compiler_params=pltpu.CompilerParams(has_side_effects=pltpu.SideEffectType.DATAFLOW_SIDE_EFFECTING),
    )(*arrays, send_sems, recv_sems, after)
    return out[n:]


SMALL_WEIGHTS = ("b_ada", "g_pre_mix", "g_post_mix", "g_pre_ffn", "g_post_ffn", "w_pool", "b_pool", "pool_scale", "conv_b")


MOD_ROWS = ((0, 0), (0, 1), (1, 3), (1, 0), (1, 1), (2, 0))


def _small_sum(mine, gathered):
    n_l = len(mine)
    d = mine[0].shape[1]

    def body(*refs):
        loc, got = refs[:n_l], refs[n_l:2 * n_l]
        tot_refs, dmod_ref = refs[2 * n_l:3 * n_l], refs[3 * n_l]
        me = _index(_place())
        part = lambda a, dev: jnp.where(dev == me, loc[a][...], got[a][dev])
        for a in range(n_l):
            tot = part(a, 0)
            for dev in range(1, N_DEV):
                tot = tot + part(a, dev)
            tot_refs[a][...] = tot
        for dev in range(N_DEV):
            for k, (a, r) in enumerate(MOD_ROWS):
                dmod_ref[dev:dev + 1, k * d:(k + 1) * d] = part(a, dev)[r:r + 1, :]

    vmem = pl.BlockSpec(memory_space=pltpu.VMEM)
    out = pl.pallas_call(
        body, name="small_sum", in_specs=[vmem] * (2 * n_l), out_specs=[vmem] * (n_l + 1),
        out_shape=[jax.ShapeDtypeStruct(a.shape, F32) for a in mine] + [jax.ShapeDtypeStruct((N_DEV, 6 * d), F32)],
        compiler_params=_params(),
    )(*mine, *gathered)
    return out[:n_l], out[n_l]


def _small_adam(totals, weights, moms, vels):
    n_t, n_w = len(totals), len(weights)

    def body(*refs):
        t_in, t_mix, t_ffn, t_pool, t_blk, t_conv, _ = (r[...] for r in refs[:n_t])
        w_refs, m_refs, v_refs = (refs[n_t + k * n_w:n_t + (k + 1) * n_w] for k in range(3))
        outs = refs[n_t + 3 * n_w:]

        def update(idx, g, at=()):
            sel = lambda ref: ref.at[at] if at else ref
            delta, nm, nv = _adam_math(sel(w_refs[idx])[...], g, sel(m_refs[idx])[...], sel(v_refs[idx])[...])
            for k, val in enumerate((g, delta, nm, nv)):
                sel(outs[4 * idx + k])[...] = val

        tots = (t_in, t_mix, t_ffn)
        update(0, jnp.concatenate([tots[a][r:r + 1] for a, r in MOD_ROWS], axis=1))
        update(1, t_in[2:3])
        update(2, t_mix[4:5])
        update(3, t_mix[2:3])
        update(4, t_ffn[1:2])
        for gi in range(len(POOL_WINDOWS)):
            update(5, t_blk[gi], at=(0, gi))
        update(6, jnp.concatenate([t_pool[0:1, gi * HEAD_DIM:(gi + 1) * HEAD_DIM] for gi in range(len(POOL_WINDOWS))], axis=0),
               at=(0,))
        update(7, t_pool[1:2])
        update(8, t_conv[3:4])

    vmem = pl.BlockSpec(memory_space=pltpu.VMEM)
    return pl.pallas_call(
        body, name="small_adam", in_specs=[vmem] * (n_t + 3 * n_w), out_specs=[vmem] * (4 * n_w),
        out_shape=[jax.ShapeDtypeStruct(w.shape, F32) for w in weights for _ in range(4)],
        compiler_params=_params(),
    )(*totals, *weights, *moms, *vels)


def _adam_math(w, g, m, v):
    m = ADAM_B1 * m + (1.0 - ADAM_B1) * g
    v = ADAM_B2 * v + (1.0 - ADAM_B2) * (g * g)
    m_hat = m / (1.0 - ADAM_B1 ** ADAM_STEP)
    v_hat = v / (1.0 - ADAM_B2 ** ADAM_STEP)
    delta = -ADAM_LR * (m_hat / (jnp.sqrt(v_hat) + ADAM_EPS) + ADAM_WD * w)
    return delta, m, v


def _adam(w, g, m, v, name, tr):
    rows, cols = w.shape

    def body(w_ref, g_ref, m_ref, v_ref, d_ref, nm_ref, nv_ref):
        d_ref[...], nm_ref[...], nv_ref[...] = _adam_math(w_ref[...], g_ref[...], m_ref[...], v_ref[...])

    spec = pl.BlockSpec((tr, cols), lambda i: (i, 0))
    shape = jax.ShapeDtypeStruct((rows, cols), F32)
    return pl.pallas_call(
        body, name=name, grid=(rows // tr,), in_specs=[spec] * 4, out_specs=[spec] * 3,
        out_shape=[shape] * 3, compiler_params=_params(("arbitrary",)),
    )(w, g, m, v)


def _sum_adam(parts, w, m, v, name, tr):
    _, rows, cols = parts.shape

    def body(p_ref, w_ref, m_ref, v_ref, g_ref, d_ref, nm_ref, nv_ref):
        g = p_ref[0].astype(F32)
        for dev in range(1, N_DEV):
            g = g + p_ref[dev].astype(F32)
        g_ref[...] = g
        d_ref[...], nm_ref[...], nv_ref[...] = _adam_math(w_ref[...], g, m_ref[...], v_ref[...])

    spec = pl.BlockSpec((tr, cols), lambda i: (i, 0))
    shape = jax.ShapeDtypeStruct((rows, cols), F32)
    return pl.pallas_call(
        body, name=name, grid=(rows // tr,),
        in_specs=[pl.BlockSpec((N_DEV, tr, cols), lambda i: (0, i, 0)), spec, spec, spec],
        out_specs=[spec] * 4, out_shape=[shape] * 4, compiler_params=_params(("arbitrary",)),
    )(parts, w, m, v)


def _ada_grad_adam(c_all, dmod_cols, w, m, v, tr):
    rows, cols = w.shape

    def body(c_ref, dm_ref, w_ref, m_ref, v_ref, g_ref, d_ref, nm_ref, nv_ref):
        cv = c_ref[...]
        act = cv * jax.nn.sigmoid(cv)
        g = lax.dot_general(act, dm_ref[...], TN, preferred_element_type=F32, precision=lax.Precision.HIGHEST)
        g_ref[...] = g
        d_ref[...], nm_ref[...], nv_ref[...] = _adam_math(w_ref[...], g, m_ref[...], v_ref[...])

    spec = pl.BlockSpec((tr, cols), lambda i: (i, 0))
    shape = jax.ShapeDtypeStruct((rows, cols), F32)
    return pl.pallas_call(
        body, name="ada_grad_adam", grid=(rows // tr,),
        in_specs=[pl.BlockSpec((N_DEV, tr), lambda i: (0, i)), pl.BlockSpec((N_DEV, cols), lambda i: (0, 0)), spec, spec, spec],
        out_specs=[spec] * 4, out_shape=[shape] * 4, compiler_params=_params(("arbitrary",)),
    )(c_all, dmod_cols, w, m, v)


def _rope_tables(positions):
    s_len = positions.shape[0]
    inv_freq = ROPE_THETA ** (-jnp.arange(0, 2 * ROT_HALF, 2, dtype=F32) / (2 * ROT_HALF))
    ang = positions.astype(F32)[:, None] * inv_freq
    cos, sin = jnp.cos(ang), jnp.sin(ang)
    rest = HEAD_DIM - 2 * ROT_HALF
    zero = lambda n: jnp.zeros((s_len, n), F32)
    head = jnp.stack([jnp.concatenate([cos, cos, jnp.ones((s_len, rest), F32)], axis=1),
                      jnp.concatenate([-sin, zero(HEAD_DIM - ROT_HALF)], axis=1),
                      jnp.concatenate([zero(ROT_HALF), sin, zero(rest)], axis=1)])
    return jnp.tile(head, (1, 1, LANES // HEAD_DIM))


def _pad_rows(a, rows):
    return jnp.pad(a, ((0, rows - a.shape[0]), (0, 0)))


def _as_rows(a, rows):
    flat = a.reshape(-1)
    return jnp.pad(flat, (0, rows * LANES - flat.shape[0])).reshape(rows, LANES)


def _sequence_step(xs, target, rope, mods, gains, w_in_t, w_out_t, fetch_ffn, send_ffn_grads, send_mix_grads, w_blk_b, b_pool_r,
                   pool_scale_r, conv_w_all, conv_b):
    sh_m, sc_m, gt_m, sh_f, sc_f, gt_f = mods
    g_pre_mix, g_post_mix, g_pre_ffn, g_post_ffn = gains
    h1, u_pool, qkv = _premix_inproj(xs, sh_m, sc_m, g_pre_mix, w_in_t, rope, tm=512)
    o_g, lse_g = [], []
    for gi, dil in enumerate(DILATIONS):
        o, lse = _attn_fwd(qkv, gi, dil)
        o_g.append(o)
        lse_g.append(lse)
    x1, y1, h2, cat, attn, lse_all = _mix_out(xs, u_pool, o_g, lse_g, w_blk_b, b_pool_r, pool_scale_r, w_out_t,
                                              gt_m, g_post_mix, g_pre_ffn, sc_f, sh_f, tm=256)
    w_up_t, w_down_f = fetch_ffn(x1)
    gate, a_ffn, act, vd, dy2, dout, sums_ffn, loss_loc = _ffn_fwd_loss(h2, x1, target, w_up_t, w_down_f, conv_w_all, conv_b,
                                                              gt_f, g_post_ffn, tm=256, tf=2816, ck=256)

    dgc, dval, dw_down, dconv = _ffn_bwd_act(dy2, gate, a_ffn, act, vd, w_down_f, tm=1024, tf=256)
    dup, dh2 = _ffn_bwd_up(dgc, dval, w_up_t, conv_w_all, tm=256)
    dw_up_t = _wgrad(dup, h2, "wgrad_up", tk=1024, tmm=1408)
    token = send_ffn_grads(dw_up_t, dw_down)
    if token is not None:
        sc_f = sc_f + token[0:1, 0:1]
    dx1, dpool, dattn, delta, dw_out_t, sums_mix = _mix_bwd(dh2, dout, x1, y1, cat, attn, w_out_t, sc_f, g_pre_ffn,
                                                           gt_m, g_post_mix, tm=256)
    du, dw_blk, sums_pool = _pool_bwd(dpool, u_pool, w_blk_b, b_pool_r, pool_scale_r, tm=512)
    dqkv = []
    for gi, dil in enumerate(DILATIONS):
        dqkv += list(_attn_bwd(qkv, dattn, lse_all, delta, gi, dil))
    dproj = _dproj_assemble(du, dqkv, rope, tm=512)
    dw_in_t = _wgrad(dproj, h1, "wgrad_in", tk=1024, tmm=1280)
    token = send_mix_grads(dw_in_t, dw_out_t)
    if token is not None:
        sc_m = sc_m + token[0:1, 0:1]
    grad_x, sums_in = _inproj_bwd(dproj, w_in_t, xs, dx1, sc_m, g_pre_mix, tm=256)
    return (loss_loc, grad_x, dw_in_t, dw_out_t, dw_up_t, dw_down, dw_blk, dconv,
            sums_in, sums_mix, sums_ffn, sums_pool)


def kernel(x, c, positions, w_ada, b_ada, g_pre_mix, g_post_mix, g_pre_ffn, g_post_ffn, w_in, w_pool, b_pool, pool_scale, w_out, w_up, conv_w, conv_b, w_down, loss_target, m_w_ada, m_b_ada, m_g_pre_mix, m_g_post_mix, m_g_pre_ffn, m_g_post_ffn, m_w_in, m_w_pool, m_b_pool, m_pool_scale, m_w_out, m_w_up, m_conv_w, m_conv_b, m_w_down, v_w_ada, v_b_ada, v_g_pre_mix, v_g_post_mix, v_g_pre_ffn, v_g_post_ffn, v_w_in, v_w_pool, v_b_pool, v_pool_scale, v_w_out, v_w_up, v_conv_w, v_conv_b, v_w_down):
    s_len, d = x.shape[1], x.shape[2]
    d_ff = w_down.shape[1] * N_DEV
    me = _index(_place())
    xs, target = x[0], loss_target[0]

    ncol = w_ada.shape[2]
    b_cols = lax.dynamic_slice(b_ada, (0, me * ncol), (1, ncol))
    c_all, mod, taps_all = _ada_exchange(jnp.broadcast_to(c, (8, d)), w_ada[0], b_cols, _pad_rows(conv_w[0], 8))
    c_all = c_all[:, 0, :]
    conv_w_all = jnp.transpose(taps_all[:, :3, :], (1, 0, 2)).reshape(3, d_ff)
    sh_m, sc_m, gt_m, sh_f, sc_f, gt_f = [mod[:, 0, :].reshape(1, -1)[:, k * d:(k + 1) * d] for k in range(6)]

    w_in_t, w_out_t = _gather_weights([w_in[0].T.astype(BF16), w_out[0].T.astype(BF16)])

    rope = _rope_tables(positions[0])
    w_blk = jnp.zeros((256, 256), F32)
    for gi in range(4):
        w_blk = lax.dynamic_update_slice(w_blk, w_pool[0, gi], (gi * HEAD_DIM, gi * HEAD_DIM))
    w_blk_b = w_blk.astype(BF16)
    b_pool_r, pool_scale_r = b_pool.reshape(1, 256), pool_scale.reshape(1, 256)

    up_sh, down_sh = w_up[0].T.astype(BF16), w_down[0].astype(BF16)
    w_in_t, conv_w_all, up_sh, down_sh = lax.optimization_barrier((w_in_t, conv_w_all, up_sh, down_sh))
    lands = [_landing_zone("gather", s, me, "land_" + nm) for s, nm in ((up_sh, "w_up"), (down_sh, "w_down"))]
    w_send, w_recv, w_src, w_land, w_token = _exchange_start("gather", [up_sh, down_sh], lands, "ffn_weights_start")

    def fetch_ffn(after):
        return _exchange_wait("gather", w_send, w_recv, w_src, w_land, after, "ffn_weights_wait")

    flight = []

    def send_ffn_grads(dw_up_t, dw_down):
        lands = [_landing_zone("scatter", dw_up_t, me, "land_dw_up"), _landing_zone("scatter", dw_down, me, "land_dw_down")]
        flight.extend(_exchange_start("scatter", [dw_up_t, dw_down], lands, "ffn_grads_start"))
        return flight[4]

    mix_flight = []

    def send_mix_grads(dw_in_t, dw_out_t):
        lands = [_landing_zone("scatter", dw_in_t, me, "land_dw_in"), _landing_zone("scatter", dw_out_t, me, "land_dw_out")]
        mix_flight.extend(_exchange_start("scatter", [dw_in_t, dw_out_t], lands, "mix_grads_start"))
        return mix_flight[4]

    (loss_loc, grad_x, dw_in_t, dw_out_t, _, _, dw_pool, dconv,
     sums_in, sums_mix, sums_ffn, sums_pool) = _sequence_step(
        xs, target, rope, (sh_m + w_token[0:1, 0:1], sc_m, gt_m, sh_f, sc_f, gt_f),
        (g_pre_mix, g_post_mix, g_pre_ffn, g_post_ffn),
        w_in_t, w_out_t, fetch_ffn, send_ffn_grads, send_mix_grads, w_blk_b, b_pool_r, pool_scale_r, conv_w_all, conv_b)

    small = [sums_in, sums_mix, sums_ffn, sums_pool, dw_pool, dconv, loss_loc]
    small_flight = _exchange_start("allgather", small, [lax.empty((N_DEV,) + a.shape, F32) for a in small], "small_start")

    parts_ffn = _exchange_wait("scatter", *flight[:4], small_flight[4], "ffn_grads_wait")
    big = {
        "w_up": [a.T for a in _sum_adam(parts_ffn[0], w_up[0].T, m_w_up[0].T, v_w_up[0].T, "adam_w_up", 64)],
        "w_down": _sum_adam(parts_ffn[1], w_down[0], m_w_down[0], v_w_down[0], "adam_w_down", 32),
    }

    rep_w = [b_ada, g_pre_mix, g_post_mix, g_pre_ffn, g_post_ffn, w_pool, b_pool, pool_scale, conv_b]
    rep_m = [m_b_ada, m_g_pre_mix, m_g_post_mix, m_g_pre_ffn, m_g_post_ffn, m_w_pool, m_b_pool, m_pool_scale, m_conv_b]
    rep_v = [v_b_ada, v_g_pre_mix, v_g_post_mix, v_g_pre_ffn, v_g_post_ffn, v_w_pool, v_b_pool, v_pool_scale, v_conv_b]
    gathered = _exchange_wait("allgather", *small_flight[:4], big["w_down"][0], "small_wait")
    totals, dmod_all = _small_sum(small, gathered)
    dconv_tot, loss_tot = totals[5], totals[6]
    rep_out = _small_adam(totals, rep_w, rep_m, rep_v)
    g_rep, d_rep, nm_rep, nv_rep = (rep_out[k::4] for k in range(4))

    fcol = d_ff // N_DEV
    g_cw = lax.dynamic_slice(dconv_tot, (0, me * fcol), (3, fcol))
    d_cw, nm_cw, nv_cw = _adam(conv_w[0], g_cw, m_conv_w[0], v_conv_w[0], "adam_conv_w", 3)

    dmod_cols = lax.dynamic_slice(dmod_all, (0, me * ncol), (N_DEV, ncol))
    g_ada, d_ada, nm_ada, nv_ada = _ada_grad_adam(c_all, dmod_cols, w_ada[0], m_w_ada[0], v_w_ada[0], 256)

    parts_mix = _exchange_wait("scatter", *mix_flight[:4], g_ada, "mix_grads_wait")
    big["w_in"] = [a.T for a in _sum_adam(parts_mix[0], w_in[0].T, m_w_in[0].T, v_w_in[0].T, "adam_w_in", 64)]
    big["w_out"] = [a.T for a in _sum_adam(parts_mix[1], w_out[0].T, m_w_out[0].T, v_w_out[0].T, "adam_w_out", 128)]

    loss = loss_tot[0, 0]

    def group(k):
        rep = (g_rep, d_rep, nm_rep, nv_rep)[k]
        ada = (g_ada, d_ada, nm_ada, nv_ada)[k][None]
        cw = (g_cw, d_cw, nm_cw, nv_cw)[k][None]
        return [ada, rep[0], rep[1], rep[2], rep[3], rep[4], big["w_in"][k][None], rep[5], rep[6], rep[7],
                big["w_out"][k][None], big["w_up"][k][None], cw, rep[8], big["w_down"][k][None]]

    return (loss, grad_x[None], *group(0), *group(1), *group(2), *group(3))
```

```python
import functools
import math

import jax
import jax.numpy as jnp
from jax import lax
from jax.experimental import pallas as pl
from jax.experimental.pallas import tpu as pltpu

F32 = jnp.float32
BF16 = jnp.bfloat16
MESH = pl.DeviceIdType.MESH

N_DEV = 8
HEAD_DIM = 64
ROT_HALF = 8
ROPE_THETA = 500000.0
POOL_WINDOWS = (2, 4, 8, 16)
DILATIONS = (1, 4, 16)
BLOCK = 128
NORM_EPS = 1e-6
HALO = 16
MASKED = -1e30
ATTN_FWD_UNROLL = 4
ATTN_BWD_UNROLL = 2

ADAM_LR = 0.001
ADAM_B1 = 0.9
ADAM_B2 = 0.999
ADAM_EPS = 1e-08
ADAM_WD = 0.01
ADAM_STEP = 10

V7X_VMEM_LIMIT = 56 * 1024 * 1024
LANES = 128

NT = (((1,), (1,)), ((), ()))
NN = (((1,), (0,)), ((), ()))
TN = (((0,), (0,)), ((), ()))


def _dot(a, b, dims):
    return lax.dot_general(a, b, dims, preferred_element_type=F32)


def _params(sem=None, vmem=V7X_VMEM_LIMIT):
    if sem is None:
        return pltpu.CompilerParams(vmem_limit_bytes=vmem)
    return pltpu.CompilerParams(dimension_semantics=sem, vmem_limit_bytes=vmem)


def _rstd(v):
    return lax.rsqrt(jnp.mean(v * v, axis=-1, keepdims=True) + NORM_EPS)


def _norm_bwd(dn, n, rstd):
    return rstd * (dn - n * jnp.mean(dn * n, axis=-1, keepdims=True))


def _rope_fwd(p, rope_ref):
    return p * rope_ref[0] + pltpu.roll(p, LANES - ROT_HALF, 1) * rope_ref[1] + pltpu.roll(p, ROT_HALF, 1) * rope_ref[2]


def _rope_bwd(dp, rope_ref):
    return dp * rope_ref[0] + pltpu.roll(dp * rope_ref[1], ROT_HALF, 1) + pltpu.roll(dp * rope_ref[2], LANES - ROT_HALF, 1)


def _gelu_parts(v):
    k2 = 2.0 * math.sqrt(2.0 / math.pi)
    c = 0.044715
    v2 = v * v
    s = jax.nn.sigmoid(v * (k2 + (k2 * c) * v2))
    g = v * s
    dg = s + g * (1.0 - s) * (k2 + (3.0 * k2 * c) * v2)
    return g, dg


def _halo_before(i, tile):
    return jnp.maximum(i * (tile // HALO) - 1, 0)


def _premix_inproj(x, sh, sc, g, w_in_t, rope, tm):
    s_len, d = x.shape
    n_proj = w_in_t.shape[0]
    n_slab = (n_proj - 256) // LANES

    def body(x_ref, sh_ref, sc_ref, g_ref, w_ref, rope_ref, h_ref, up_ref, qkv_ref):
        xv = x_ref[...]
        h = (xv * _rstd(xv) * g_ref[...]) * (1.0 + sc_ref[...]) + sh_ref[...]
        hb = h.astype(BF16)
        h_ref[...] = hb
        up_ref[...] = _dot(hb, w_ref[0:256, :], NT)
        for pair in range(n_slab // 2):
            p = _dot(hb, w_ref[256 + 256 * pair:512 + 256 * pair, :], NT)
            for half in range(2):
                ph = p[:, half * LANES:(half + 1) * LANES]
                if pair < 6:
                    ph = _rope_fwd(ph, rope_ref)
                if pair < 3:
                    ph = ph * (HEAD_DIM ** -0.5)
                qkv_ref[2 * pair + half] = ph

    vec = pl.BlockSpec((1, d), lambda i: (0, 0))
    return pl.pallas_call(
        body, name="premix_inproj", grid=(s_len // tm,),
        in_specs=[pl.BlockSpec((tm, d), lambda i: (i, 0)), vec, vec, vec,
                  pl.BlockSpec((n_proj, d), lambda i: (0, 0)),
                  pl.BlockSpec((3, tm, LANES), lambda i: (0, i, 0))],
        out_specs=[pl.BlockSpec((tm, d), lambda i: (i, 0)),
                   pl.BlockSpec((tm, 256), lambda i: (i, 0)),
                   pl.BlockSpec((n_slab, tm, LANES), lambda i: (0, i, 0))],
        out_shape=[jax.ShapeDtypeStruct((s_len, d), BF16),
                   jax.ShapeDtypeStruct((s_len, 256), F32),
                   jax.ShapeDtypeStruct((n_slab, s_len, LANES), F32)],
        compiler_params=_params(("arbitrary",)),
    )(x, sh, sc, g, w_in_t, rope)


def _block_rows(n, r, dil):
    start = n * (BLOCK * dil) + r
    if dil == 1:
        return pl.ds(pl.multiple_of(start, BLOCK), BLOCK)
    return pl.ds(start, BLOCK, stride=dil)


def _band_mask(n):
    ri = lax.broadcasted_iota(jnp.int32, (BLOCK, 2 * BLOCK), 0)
    cj = lax.broadcasted_iota(jnp.int32, (BLOCK, 2 * BLOCK), 1)
    cur = (cj >= BLOCK) & (cj - BLOCK <= ri)
    prev = (cj < BLOCK) & (cj >= ri) & (n > 0)
    return cur | prev


def _attn_fwd(qkv, group, dil):
    s_len = qkv.shape[1]
    nb = s_len // (BLOCK * dil)

    def body(q_ref, k_ref, v_ref, o_ref, lse_ref):
        lane = lax.broadcasted_iota(jnp.int32, (BLOCK, LANES), 1)
        first = lane < HEAD_DIM

        def block(t, carry):
            r, n = t // nb, t % nb
            cur = _block_rows(n, r, dil)
            prev = _block_rows(jnp.maximum(n - 1, 0), r, dil)
            q = q_ref[0, cur, :]
            kcat = jnp.concatenate([k_ref[0, prev, :], k_ref[0, cur, :]], axis=0).astype(BF16)
            vcat = jnp.concatenate([v_ref[0, prev, :], v_ref[0, cur, :]], axis=0).astype(BF16)
            valid = _band_mask(n)
            q2 = jnp.concatenate([jnp.where(first, q, 0.0), jnp.where(first, 0.0, q)], axis=0).astype(BF16)
            s = jnp.where(jnp.concatenate([valid, valid], axis=0), _dot(q2, kcat, NT), MASKED)
            m = jnp.max(s, axis=-1, keepdims=True)
            p = jnp.exp(s - m)
            den = jnp.sum(p, axis=-1, keepdims=True)
            o2 = _dot(p.astype(BF16), vcat, NN) / den
            lse2 = m + jnp.log(den)
            o_ref[0, cur, :] = jnp.where(first, o2[:BLOCK], o2[BLOCK:])
            lse_ref[0, cur, :] = jnp.where(first, lse2[:BLOCK], lse2[BLOCK:])
            return carry

        lax.fori_loop(0, nb * dil, block, 0, unroll=ATTN_FWD_UNROLL)

    def slab(base):
        return pl.BlockSpec((1, s_len, LANES), lambda s: (base + 2 * group + s, 0, 0))

    out = pl.BlockSpec((1, s_len, LANES), lambda s: (s, 0, 0))
    shape = jax.ShapeDtypeStruct((2, s_len, LANES), F32)
    return pl.pallas_call(
        body, name=f"attn_fwd_d{dil}", grid=(2,),
        in_specs=[slab(0), slab(6), slab(12)], out_specs=[out, out], out_shape=[shape, shape],
        compiler_params=_params(("arbitrary",)),
    )(qkv, qkv, qkv)


def _pool_mixed(u, halo, i, tm):
    ue = jnp.concatenate([halo, u], axis=0)
    s2 = ue + pltpu.roll(ue, 1, 0)
    s4 = s2 + pltpu.roll(s2, 2, 0)
    s8 = s4 + pltpu.roll(s4, 4, 0)
    s16 = s8 + pltpu.roll(s8, 8, 0)
    grp = lax.broadcasted_iota(jnp.int32, (tm, 256), 1) // HEAD_DIM
    pick = lambda a, b, c, e: jnp.where(grp == 0, a, jnp.where(grp == 1, b, jnp.where(grp == 2, c, e)))
    win_sum = pick(s2[HALO:], s4[HALO:], s8[HALO:], s16[HALO:])
    pos = (i * tm + lax.broadcasted_iota(jnp.int32, (tm, 256), 0)).astype(F32)
    count = jnp.minimum(pos + 1.0, pick(*[float(w) for w in POOL_WINDOWS]))
    return win_sum / count - u, count


def _mix_out(x, u_pool, o_g, lse_g, w_blk, b_pool, pool_scale, w_out_t, gt_m, g_post_mix, g_pre_ffn, sc_f, sh_f, tm):
    s_len, d = x.shape

    def body(x_ref, u_ref, uh_ref, o0, o1, o2, l0, l1, l2, wb_ref, bp_ref, ps_ref, wo_ref,
             gt_ref, g1_ref, g2_ref, sc_ref, sh_ref,
             x1_ref, y1_ref, h2_ref, cat_ref, attn_ref, lall_ref):
        i = pl.program_id(0)
        u = u_ref[...]
        halo = uh_ref[...] * (i > 0).astype(F32)
        mixed, _ = _pool_mixed(u, halo, i, tm)
        y = _dot(mixed.astype(BF16), wb_ref[...], NN) + bp_ref[...]
        pool = y * ps_ref[...]
        attn = []
        for s in range(2):
            la, lb, lc = l0[s], l1[s], l2[s]
            mx = jnp.maximum(jnp.maximum(la, lb), lc)
            ea, eb, ec = jnp.exp(la - mx), jnp.exp(lb - mx), jnp.exp(lc - mx)
            den = ea + eb + ec
            lall_ref[s] = mx + jnp.log(den)
            attn.append((ea / den) * o0[s] + (eb / den) * o1[s] + (ec / den) * o2[s])
        attn = jnp.concatenate(attn, axis=1)
        attn_ref[...] = attn
        cat = jnp.concatenate([pool, attn], axis=1).astype(BF16)
        cat_ref[...] = cat
        y1 = _dot(cat, wo_ref[...], NT)
        y1_ref[...] = y1
        x1 = x_ref[...] + gt_ref[...] * (y1 * _rstd(y1) * g1_ref[...])
        x1_ref[...] = x1
        h2 = (x1 * _rstd(x1) * g2_ref[...]) * (1.0 + sc_ref[...]) + sh_ref[...]
        h2_ref[...] = h2.astype(BF16)

    tile = lambda w: pl.BlockSpec((tm, w), lambda i: (i, 0))
    slab = pl.BlockSpec((2, tm, LANES), lambda i: (0, i, 0))
    const = lambda a: pl.BlockSpec(a.shape, lambda i: (0,) * a.ndim)
    return pl.pallas_call(
        body, name="mix_out", grid=(s_len // tm,),
        in_specs=[tile(d), tile(256), pl.BlockSpec((HALO, 256), lambda i: (_halo_before(i, tm), 0)),
                  slab, slab, slab, slab, slab, slab,
                  const(w_blk), const(b_pool), const(pool_scale), const(w_out_t),
                  const(gt_m), const(g_post_mix), const(g_pre_ffn), const(sc_f), const(sh_f)],
        out_specs=[tile(d), tile(d), tile(d), tile(512), tile(256), slab],
        out_shape=[jax.ShapeDtypeStruct((s_len, d), F32), jax.ShapeDtypeStruct((s_len, d), F32),
                   jax.ShapeDtypeStruct((s_len, d), BF16), jax.ShapeDtypeStruct((s_len, 512), BF16),
                   jax.ShapeDtypeStruct((s_len, 256), F32), jax.ShapeDtypeStruct((2, s_len, LANES), F32)],
        compiler_params=_params(("arbitrary",)),
    )(x, u_pool, u_pool, *o_g, *lse_g, w_blk, b_pool, pool_scale, w_out_t, gt_m, g_post_mix, g_pre_ffn, sc_f, sh_f)


def _conv_gate(gate_ext, cw, cb):
    gc = gate_ext * cw[2:3, :] + pltpu.roll(gate_ext, 1, 0) * cw[1:2, :] + pltpu.roll(gate_ext, 2, 0) * cw[0:1, :]
    return gc[HALO:] + cb


def _ffn_fwd_loss(h2, x1, target, w_up_t, w_down, conv_w, conv_b, gt_f, g_post_ffn, tm, tf, ck):
    s_len, d = x1.shape
    d_ff = w_down.shape[0]
    n_f = d_ff // tf

    def body(h_ref, hh_ref, x1_ref, tgt_ref, wg_ref, wv_ref, wd_ref, cw_ref, cb_ref, gt_ref, g_ref,
             gate_ref, a_ref, act_ref, vd_ref, dy2_ref, dout_ref, sums_ref, loss_ref, acc_ref):
        i, j = pl.program_id(0), pl.program_id(1)

        @pl.when((i == 0) & (j == 0))
        def _():
            sums_ref[...] = jnp.zeros_like(sums_ref)
            loss_ref[...] = jnp.zeros_like(loss_ref)

        h = h_ref[...]
        h_ext = jnp.concatenate([hh_ref[...], h], axis=0)
        row = lax.broadcasted_iota(jnp.int32, (tm + HALO, ck), 0)
        no_halo = (row < HALO) & (i == 0)

        def up(c):
            cs = slice(c * ck, (c + 1) * ck)
            return jnp.where(no_halo, 0.0, _dot(h_ext, wg_ref[cs, :], NT)), _dot(h, wv_ref[cs, :], NT)

        part = None
        n_c = tf // ck
        nxt = up(0)
        for c in range(n_c):
            cs = slice(c * ck, (c + 1) * ck)
            gate_ext, val = nxt
            if c + 1 < n_c:
                nxt = up(c + 1)
            act, dact = _gelu_parts(_conv_gate(gate_ext, cw_ref[:, cs], cb_ref[:, cs]))
            a = (act * val).astype(BF16)
            gate_ref[:, cs] = gate_ext[HALO:].astype(BF16)
            a_ref[:, cs] = a
            act_ref[:, cs] = act.astype(BF16)
            vd_ref[:, cs] = (val * dact).astype(BF16)
            p = _dot(a, wd_ref[cs, :], NN)
            part = p if part is None else part + p

        @pl.when(j == 0)
        def _():
            acc_ref[...] = part

        @pl.when(j > 0)
        def _():
            acc_ref[...] += part

        @pl.when(j == n_f - 1)
        def _():
            y2 = acc_ref[...]
            rstd = _rstd(y2)
            n = y2 * rstd
            rn = n * g_ref[...]
            err = x1_ref[...] + gt_ref[...] * rn - tgt_ref[...]
            loss_ref[...] += 0.5 * jnp.sum(jnp.mean(err * err, axis=-1, keepdims=True), axis=0, keepdims=True)
            dout = err * (1.0 / d)
            dout_ref[...] = dout
            drn = dout * gt_ref[...]
            sums_ref[0:1, :] += jnp.sum(dout * rn, axis=0, keepdims=True)
            sums_ref[1:2, :] += jnp.sum(drn * n, axis=0, keepdims=True)
            dy2_ref[...] = _norm_bwd(drn * g_ref[...], n, rstd).astype(BF16)

    tok = lambda w: pl.BlockSpec((tm, w), lambda i, j: (i, 0))
    tokf = pl.BlockSpec((tm, tf), lambda i, j: (i, j))
    vec = pl.BlockSpec((1, d), lambda i, j: (0, 0))
    once = {"pipeline_mode": pl.Buffered(1)} if n_f == 1 else {}
    return pl.pallas_call(
        body, name="ffn_fwd_loss", grid=(s_len // tm, n_f),
        in_specs=[tok(d), pl.BlockSpec((HALO, d), lambda i, j: (_halo_before(i, tm), 0)), tok(d), tok(d),
                  pl.BlockSpec((tf, d), lambda i, j: (j, 0), **once),
                  pl.BlockSpec((tf, d), lambda i, j: (j + n_f, 0), **once),
                  pl.BlockSpec((tf, d), lambda i, j: (j, 0), **once),
                  pl.BlockSpec((3, tf), lambda i, j: (0, j)), pl.BlockSpec((1, tf), lambda i, j: (0, j)), vec, vec],
        out_specs=[tokf, tokf, tokf, tokf, tok(d), tok(d), pl.BlockSpec((8, d), lambda i, j: (0, 0)),
                   pl.BlockSpec((8, LANES), lambda i, j: (0, 0))],
        out_shape=[jax.ShapeDtypeStruct((s_len, d_ff), BF16)] * 4
        + [jax.ShapeDtypeStruct((s_len, d), BF16), jax.ShapeDtypeStruct((s_len, d), F32),
                   jax.ShapeDtypeStruct((8, d), F32), jax.ShapeDtypeStruct((8, LANES), F32)],
        scratch_shapes=[pltpu.VMEM((tm, d), F32)],
        compiler_params=_params(("arbitrary", "arbitrary")),
    )(h2, h2, x1, target, w_up_t, w_up_t, w_down, conv_w, conv_b, gt_f, g_post_ffn)


def _ffn_bwd_act(dy2, gate, a, act, vd, w_down, tm, ck):
    s_len, d = dy2.shape
    d_ff = w_down.shape[0]
    n_t, n_c = s_len // tm, d_ff // ck

    def body(dy_ref, g_ref, gh_ref, a_ref, act_ref, vd_ref, wd_ref, dgc_ref, dval_ref, dwd_ref, dconv_ref, acc_ref):
        i = pl.program_id(0)

        @pl.when(i == 0)
        def _():
            acc_ref[...] = jnp.zeros_like(acc_ref)
            dconv_ref[...] = jnp.zeros_like(dconv_ref)

        dy = dy_ref[...]
        row = lax.broadcasted_iota(jnp.int32, (tm + HALO, ck), 0)
        no_halo = (row < HALO) & (i == 0)

        def down(c):
            return _dot(dy, wd_ref[c * ck:(c + 1) * ck, :], NT)

        nxt = down(0)
        for c in range(n_c):
            cs = slice(c * ck, (c + 1) * ck)
            da = nxt
            if c + 1 < n_c:
                nxt = down(c + 1)
            acc_ref[cs, :] += _dot(a_ref[:, cs], dy, TN)
            gate_ext = jnp.where(no_halo, 0.0, jnp.concatenate([gh_ref[:, cs], g_ref[:, cs]], axis=0).astype(F32))
            dgc = da * vd_ref[:, cs].astype(F32)
            dgc_ref[:, cs] = dgc.astype(BF16)
            dval_ref[:, cs] = (da * act_ref[:, cs].astype(F32)).astype(BF16)
            rows = [jnp.sum(dgc * pltpu.roll(gate_ext, 2 - k, 0)[HALO:], axis=0, keepdims=True) for k in range(2)]
            rows += [jnp.sum(dgc * gate_ext[HALO:], axis=0, keepdims=True), jnp.sum(dgc, axis=0, keepdims=True),
                     jnp.zeros((4, ck), F32)]
            dconv_ref[:, cs] += jnp.concatenate(rows, axis=0)

        @pl.when(i == n_t - 1)
        def _():
            dwd_ref[...] = acc_ref[...].astype(BF16)

    tokf = pl.BlockSpec((tm, d_ff), lambda i: (i, 0))
    return pl.pallas_call(
        body, name="ffn_bwd_act", grid=(n_t,),
        in_specs=[pl.BlockSpec((tm, d), lambda i: (i, 0)), tokf,
                  pl.BlockSpec((HALO, d_ff), lambda i: (_halo_before(i, tm), 0)), tokf, tokf, tokf,
                  pl.BlockSpec((d_ff, d), lambda i: (0, 0), pipeline_mode=pl.Buffered(1))],
        out_specs=[tokf, tokf, pl.BlockSpec((d_ff, d), lambda i: (0, 0)), pl.BlockSpec((8, d_ff), lambda i: (0, 0))],
        out_shape=[jax.ShapeDtypeStruct((s_len, d_ff), BF16), jax.ShapeDtypeStruct((s_len, d_ff), BF16),
                   jax.ShapeDtypeStruct((d_ff, d), BF16), jax.ShapeDtypeStruct((8, d_ff), F32)],
        scratch_shapes=[pltpu.VMEM((d_ff, d), F32)],
        compiler_params=_params(("arbitrary",)),
    )(dy2, gate, gate, a, act, vd, w_down)


def _ffn_bwd_up(dgc, dval, w_up_t, conv_w, tm):
    s_len, d_ff = dgc.shape
    d = w_up_t.shape[1]
    n_t = s_len // tm

    def body(dg_ref, dgn_ref, dv_ref, cw_ref, w_ref, dup_ref, dh_ref):
        i = pl.program_id(0)
        nxt = dgn_ref[...].astype(F32) * (i < n_t - 1).astype(F32)
        ext = jnp.concatenate([dg_ref[...].astype(F32), nxt], axis=0)
        rows = tm + HALO
        dgate = (ext * cw_ref[2:3, :] + pltpu.roll(ext, rows - 1, 0) * cw_ref[1:2, :]
                 + pltpu.roll(ext, rows - 2, 0) * cw_ref[0:1, :])[:tm]
        dup = jnp.concatenate([dgate.astype(BF16), dv_ref[...]], axis=1)
        dup_ref[...] = dup
        dh_ref[...] = _dot(dup, w_ref[...], NN)

    tokf = pl.BlockSpec((tm, d_ff), lambda i: (i, 0))
    return pl.pallas_call(
        body, name="ffn_bwd_up", grid=(n_t,),
        in_specs=[tokf, pl.BlockSpec((HALO, d_ff), lambda i: (jnp.minimum((i + 1) * (tm // HALO), s_len // HALO - 1), 0)),
                  tokf, pl.BlockSpec((3, d_ff), lambda i: (0, 0)), pl.BlockSpec((2 * d_ff, d), lambda i: (0, 0))],
        out_specs=[pl.BlockSpec((tm, 2 * d_ff), lambda i: (i, 0)), pl.BlockSpec((tm, d), lambda i: (i, 0))],
        out_shape=[jax.ShapeDtypeStruct((s_len, 2 * d_ff), BF16), jax.ShapeDtypeStruct((s_len, d), F32)],
        compiler_params=_params(("arbitrary",)),
    )(dgc, dgc, dval, conv_w, w_up_t)


def _mix_bwd(dh2, dout, x1, y1, cat, attn, w_out_t, sc_f, g_pre_ffn, gt_m, g_post_mix, tm):
    s_len, d = x1.shape
    n_t = s_len // tm

    def body(dh_ref, do_ref, x1_ref, y1_ref, cat_ref, at_ref, wo_ref, sc_ref, g2_ref, gt_ref, g1_ref,
             dx1_ref, dpool_ref, dattn_ref, delta_ref, dwo_ref, sums_ref, acc_ref):
        i = pl.program_id(0)
        dh = dh_ref[...]
        x1 = x1_ref[...]
        r2 = _rstd(x1)
        n2 = x1 * r2
        ng = n2 * g2_ref[...]
        dng = dh * (1.0 + sc_ref[...])
        dx1 = do_ref[...] + _norm_bwd(dng * g2_ref[...], n2, r2)
        dx1_ref[...] = dx1
        y1 = y1_ref[...]
        r1 = _rstd(y1)
        n1 = y1 * r1
        drn = dx1 * gt_ref[...]
        dy1 = _norm_bwd(drn * g1_ref[...], n1, r1).astype(BF16)
        dcat = _dot(dy1, wo_ref[...], NN)
        dpool_ref[...] = dcat[:, 0:256]
        lane = lax.broadcasted_iota(jnp.int32, (tm, LANES), 1)
        first = lane < HEAD_DIM
        for s in range(2):
            da = dcat[:, 256 + s * LANES:256 + (s + 1) * LANES]
            dattn_ref[s] = da
            prod = da * at_ref[:, s * LANES:(s + 1) * LANES]
            tot = jnp.sum(prod, axis=-1, keepdims=True)
            lo = jnp.sum(jnp.where(first, prod, 0.0), axis=-1, keepdims=True)
            delta_ref[s] = jnp.where(first, lo, tot - lo)
        dwo = _dot(dy1, cat_ref[...], TN)
        sums = jnp.concatenate(
            [jnp.sum(dh, axis=0, keepdims=True), jnp.sum(dh * ng, axis=0, keepdims=True),
             jnp.sum(dng * n2, axis=0, keepdims=True), jnp.sum(dx1 * (n1 * g1_ref[...]), axis=0, keepdims=True),
             jnp.sum(drn * n1, axis=0, keepdims=True), jnp.zeros((3, d), F32)], axis=0)

        @pl.when(i == 0)
        def _():
            acc_ref[...] = dwo
            sums_ref[...] = sums

        @pl.when(i > 0)
        def _():
            acc_ref[...] += dwo
            sums_ref[...] += sums

        @pl.when(i == n_t - 1)
        def _():
            dwo_ref[...] = acc_ref[...].astype(BF16)

    tile = lambda w: pl.BlockSpec((tm, w), lambda i: (i, 0))
    slab = pl.BlockSpec((2, tm, LANES), lambda i: (0, i, 0))
    vec = pl.BlockSpec((1, d), lambda i: (0, 0))
    return pl.pallas_call(
        body, name="mix_bwd", grid=(n_t,),
        in_specs=[tile(d), tile(d), tile(d), tile(d), tile(512), tile(256),
                  pl.BlockSpec((d, 512), lambda i: (0, 0)), vec, vec, vec, vec],
        out_specs=[tile(d), tile(256), slab, slab, pl.BlockSpec((d, 512), lambda i: (0, 0)),
                   pl.BlockSpec((8, d), lambda i: (0, 0))],
        out_shape=[jax.ShapeDtypeStruct((s_len, d), F32), jax.ShapeDtypeStruct((s_len, 256), F32),
                   jax.ShapeDtypeStruct((2, s_len, LANES), F32), jax.ShapeDtypeStruct((2, s_len, LANES), F32),
                   jax.ShapeDtypeStruct((d, 512), BF16), jax.ShapeDtypeStruct((8, d), F32)],
        scratch_shapes=[pltpu.VMEM((d, 512), F32)],
        compiler_params=_params(("arbitrary",)),
    )(dh2, dout, x1, y1, cat, attn, w_out_t, sc_f, g_pre_ffn, gt_m, g_post_mix)


def _pool_bwd(dpool, u_pool, w_blk, b_pool, pool_scale, tm):
    s_len = dpool.shape[0]
    n_t = s_len // tm

    def body(dp_ref, dpn_ref, u_ref, uh_ref, wb_ref, bp_ref, ps_ref, du_ref, dwp_ref, sums_ref, acc_ref):
        i = pl.program_id(0)
        u = u_ref[...]
        mixed, _ = _pool_mixed(u, uh_ref[...] * (i > 0).astype(F32), i, tm)
        mixed_b = mixed.astype(BF16)
        y = _dot(mixed_b, wb_ref[...], NN) + bp_ref[...]
        dp = dp_ref[...]
        dy = dp * ps_ref[...]
        dwb = _dot(mixed_b, dy.astype(BF16), TN)
        sums = jnp.concatenate([jnp.sum(dy, axis=0, keepdims=True), jnp.sum(dp * y, axis=0, keepdims=True),
                                jnp.zeros((6, 256), F32)], axis=0)
        dp_ext = jnp.concatenate([dp, dpn_ref[...] * (i < n_t - 1).astype(F32)], axis=0)
        dmix = _dot((dp_ext * ps_ref[...]).astype(BF16), wb_ref[...], NT)
        rows = tm + HALO
        grp = lax.broadcasted_iota(jnp.int32, (rows, 256), 1) // HEAD_DIM
        pick = lambda a, b, c, e: jnp.where(grp == 0, a, jnp.where(grp == 1, b, jnp.where(grp == 2, c, e)))
        pos = (i * tm + lax.broadcasted_iota(jnp.int32, (rows, 256), 0)).astype(F32)
        z = dmix / jnp.minimum(pos + 1.0, pick(*[float(w) for w in POOL_WINDOWS]))
        f2 = z + pltpu.roll(z, rows - 1, 0)
        f4 = f2 + pltpu.roll(f2, rows - 2, 0)
        f8 = f4 + pltpu.roll(f4, rows - 4, 0)
        f16 = f8 + pltpu.roll(f8, rows - 8, 0)
        du_ref[...] = (pick(f2, f4, f8, f16) - dmix)[:tm]

        @pl.when(i == 0)
        def _():
            acc_ref[...] = dwb
            sums_ref[...] = sums

        @pl.when(i > 0)
        def _():
            acc_ref[...] += dwb
            sums_ref[...] += sums

        @pl.when(i == n_t - 1)
        def _():
            full = acc_ref[...]
            for gi in range(len(POOL_WINDOWS)):
                lo = gi * HEAD_DIM
                dwp_ref[gi] = full[lo:lo + HEAD_DIM, lo:lo + HEAD_DIM]

    n_g = len(POOL_WINDOWS)
    tile = pl.BlockSpec((tm, 256), lambda i: (i, 0))
    const = lambda a: pl.BlockSpec(a.shape, lambda i: (0,) * a.ndim)
    return pl.pallas_call(
        body, name="pool_bwd", grid=(n_t,),
        in_specs=[tile, pl.BlockSpec((HALO, 256), lambda i: (jnp.minimum((i + 1) * (tm // HALO), s_len // HALO - 1), 0)),
                  tile, pl.BlockSpec((HALO, 256), lambda i: (_halo_before(i, tm), 0)),
                  const(w_blk), const(b_pool), const(pool_scale)],
        out_specs=[tile, pl.BlockSpec((n_g, HEAD_DIM, HEAD_DIM), lambda i: (0, 0, 0)), pl.BlockSpec((8, 256), lambda i: (0, 0))],
        out_shape=[jax.ShapeDtypeStruct((s_len, 256), F32), jax.ShapeDtypeStruct((n_g, HEAD_DIM, HEAD_DIM), F32),
                   jax.ShapeDtypeStruct((8, 256), F32)],
        scratch_shapes=[pltpu.VMEM((256, 256), F32)],
        compiler_params=_params(("arbitrary",)),
    )(dpool, dpool, u_pool, u_pool, w_blk, b_pool, pool_scale)


def _attn_bwd(qkv, dattn, lse_all, delta, group, dil):
    s_len = qkv.shape[1]
    nb = s_len // (BLOCK * dil)

    def body(q_ref, k_ref, v_ref, do_ref, l_ref, dl_ref, dq_ref, dk_ref, dv_ref):
        lane = lax.broadcasted_iota(jnp.int32, (BLOCK, LANES), 1)
        first = lane < HEAD_DIM

        def block(t, carry):
            dk_part, dv_part = carry
            r, n = t // nb, t % nb
            cur = _block_rows(n, r, dil)
            prev = _block_rows(jnp.maximum(n - 1, 0), r, dil)
            q = q_ref[0, cur, :]
            do = do_ref[0, cur, :]
            lse = l_ref[0, cur, :]
            dlt = dl_ref[0, cur, :]
            kcat = jnp.concatenate([k_ref[0, prev, :], k_ref[0, cur, :]], axis=0).astype(BF16)
            vcat = jnp.concatenate([v_ref[0, prev, :], v_ref[0, cur, :]], axis=0).astype(BF16)
            valid = _band_mask(n)
            stack = lambda a: jnp.concatenate([jnp.where(first, a, 0.0), jnp.where(first, 0.0, a)], axis=0)
            rows2 = lambda a: jnp.concatenate([a[:, 0:1], a[:, HEAD_DIM:HEAD_DIM + 1]], axis=0)
            q2, do2 = stack(q).astype(BF16), stack(do).astype(BF16)
            valid2 = jnp.concatenate([valid, valid], axis=0)
            p = jnp.where(valid2, jnp.exp(_dot(q2, kcat, NT) - rows2(lse)), 0.0)
            ds = (p * (_dot(do2, vcat, NT) - rows2(dlt))).astype(BF16)
            dq2 = _dot(ds, kcat, NN)
            dq_ref[0, cur, :] = jnp.where(first, dq2[:BLOCK], dq2[BLOCK:])
            dkc = _dot(ds, q2, TN)
            dvc = _dot(p.astype(BF16), do2, TN)
            dk_ref[0, prev, :] = dk_part + dkc[:BLOCK]
            dv_ref[0, prev, :] = dv_part + dvc[:BLOCK]
            dk_ref[0, cur, :] = dkc[BLOCK:]
            dv_ref[0, cur, :] = dvc[BLOCK:]
            return dkc[BLOCK:], dvc[BLOCK:]

        def blocks(tt, carry):
            for u in range(ATTN_BWD_UNROLL):
                carry = block(tt * ATTN_BWD_UNROLL + u, carry)
            return carry

        zero = jnp.zeros((BLOCK, LANES), F32)
        lax.fori_loop(0, nb * dil // ATTN_BWD_UNROLL, blocks, (zero, zero))

    def slab(base):
        return pl.BlockSpec((1, s_len, LANES), lambda s: (base + 2 * group + s, 0, 0))

    one = pl.BlockSpec((1, s_len, LANES), lambda s: (s, 0, 0))
    shape = jax.ShapeDtypeStruct((2, s_len, LANES), F32)
    return pl.pallas_call(
        body, name=f"attn_bwd_d{dil}", grid=(2,),
        in_specs=[slab(0), slab(6), slab(12), one, one, one],
        out_specs=[one, one, one], out_shape=[shape, shape, shape],
        compiler_params=_params(("arbitrary",)),
    )(qkv, qkv, qkv, dattn, lse_all, delta)


def _dproj_assemble(du, dqkv, rope, tm):
    s_len = du.shape[0]
    n_proj = 256 + 18 * LANES

    def body(du_ref, *refs):
        dref, rope_ref, dproj_ref = refs[:9], refs[9], refs[10]
        dproj_ref[:, 0:256] = du_ref[...].astype(BF16)
        col = 256
        for kind in range(3):
            for grp in range(3):
                for s in range(2):
                    piece = dref[3 * grp + kind][s]
                    if kind < 2:
                        piece = _rope_bwd(piece, rope_ref)
                    if kind == 0:
                        piece = piece * (HEAD_DIM ** -0.5)
                    dproj_ref[:, col:col + LANES] = piece.astype(BF16)
                    col += LANES

    slab = pl.BlockSpec((2, tm, LANES), lambda i: (0, i, 0))
    return pl.pallas_call(
        body, name="dproj_assemble", grid=(s_len // tm,),
        in_specs=[pl.BlockSpec((tm, 256), lambda i: (i, 0))] + [slab] * 9 + [pl.BlockSpec((3, tm, LANES), lambda i: (0, i, 0))],
        out_specs=pl.BlockSpec((tm, n_proj), lambda i: (i, 0)),
        out_shape=jax.ShapeDtypeStruct((s_len, n_proj), BF16),
        compiler_params=_params(("arbitrary",)),
    )(du, *dqkv, rope)


def _inproj_bwd(dproj, w_in_t, x, dx1, sc_m, g_pre_mix, tm):
    s_len, d = x.shape
    n_proj = w_in_t.shape[0]
    n_t = s_len // tm

    def body(dproj_ref, w_ref, x_ref, dx1_ref, sc_ref, g_ref, dx_ref, sums_ref):
        i = pl.program_id(0)
        dh = _dot(dproj_ref[...], w_ref[...], NN)
        xv = x_ref[...]
        r = _rstd(xv)
        n = xv * r
        dng = dh * (1.0 + sc_ref[...])
        dx_ref[...] = dx1_ref[...] + _norm_bwd(dng * g_ref[...], n, r)
        sums = jnp.concatenate([jnp.sum(dh, axis=0, keepdims=True), jnp.sum(dh * (n * g_ref[...]), axis=0, keepdims=True),
                                jnp.sum(dng * n, axis=0, keepdims=True), jnp.zeros((5, d), F32)], axis=0)

        @pl.when(i == 0)
        def _():
            sums_ref[...] = sums

        @pl.when(i > 0)
        def _():
            sums_ref[...] += sums

    tile = lambda w: pl.BlockSpec((tm, w), lambda i: (i, 0))
    vec = pl.BlockSpec((1, d), lambda i: (0, 0))
    return pl.pallas_call(
        body, name="inproj_bwd", grid=(n_t,),
        in_specs=[tile(n_proj), pl.BlockSpec((n_proj, d), lambda i: (0, 0)), tile(d), tile(d), vec, vec],
        out_specs=[tile(d), pl.BlockSpec((8, d), lambda i: (0, 0))],
        out_shape=[jax.ShapeDtypeStruct((s_len, d), F32), jax.ShapeDtypeStruct((8, d), F32)],
        compiler_params=_params(("arbitrary",)),
    )(dproj, w_in_t, x, dx1, sc_m, g_pre_mix)


def _wgrad(a, b, name, tk, tmm):
    s_len, m = a.shape
    n = b.shape[1]
    n_k = s_len // tk

    def body(a_ref, b_ref, o_ref, acc_ref):
        k = pl.program_id(1)
        part = _dot(a_ref[...], b_ref[...], TN)

        @pl.when(k == 0)
        def _():
            acc_ref[...] = part

        @pl.when(k > 0)
        def _():
            acc_ref[...] += part

        @pl.when(k == n_k - 1)
        def _():
            o_ref[...] = acc_ref[...].astype(BF16)

    return pl.pallas_call(
        body, name=name, grid=(m // tmm, n_k),
        in_specs=[pl.BlockSpec((tk, tmm), lambda j, k: (k, j)), pl.BlockSpec((tk, n), lambda j, k: (k, 0))],
        out_specs=pl.BlockSpec((tmm, n), lambda j, k: (j, 0)),
        out_shape=jax.ShapeDtypeStruct((m, n), BF16),
        scratch_shapes=[pltpu.VMEM((tmm, n), F32)],
        compiler_params=_params(("arbitrary", "arbitrary")),
    )(a, b)


def _place():
    return lax.axis_index("x"), lax.axis_index("y"), lax.axis_index("c")


def _peer(k):
    x, y, c = _place()
    bx, by, bc = (k >> 2) & 1, (k >> 1) & 1, k & 1
    return (x ^ bx if bx else x, y ^ by if by else y, c ^ bc if bc else c)


def _index(pos):
    return 4 * pos[0] + 2 * pos[1] + pos[2]


def _ada_exchange(c_rows, w_ada, b_ada_cols, taps):
    d = c_rows.shape[1]
    ncol = w_ada.shape[1]

    def body(c_ref, w_ref, b_ref, t_ref, call_ref, mod_ref, tall_ref, stage_ref, send_sems, recv_sems):
        me = _index(_place())
        call_ref[me] = c_ref[...]
        tall_ref[me] = t_ref[...]

        def gather(k):
            return pltpu.make_async_remote_copy(
                src_ref=c_ref, dst_ref=call_ref.at[me], send_sem=send_sems.at[0, k - 1], recv_sem=recv_sems.at[0, k - 1],
                device_id=_peer(k), device_id_type=MESH)

        def gather_taps(k):
            return pltpu.make_async_remote_copy(
                src_ref=t_ref, dst_ref=tall_ref.at[me], send_sem=send_sems.at[2, k - 1], recv_sem=recv_sems.at[2, k - 1],
                device_id=_peer(k), device_id_type=MESH)

        for k in range(1, N_DEV):
            gather(k).start()
        for k in range(1, N_DEV):
            gather_taps(k).start()
        for k in range(1, N_DEV):
            gather(k).wait_recv()
        cv = jnp.concatenate([call_ref[b, 0:1, :] for b in range(N_DEV)], axis=0)
        act = cv * jax.nn.sigmoid(cv)
        mod = lax.dot_general(act, w_ref[...], NN, preferred_element_type=F32,
                              precision=lax.Precision.HIGHEST) + b_ref[...]
        for b in range(N_DEV):
            stage_ref[b] = jnp.broadcast_to(mod[b:b + 1, :], (8, ncol))
        mod_ref[me] = stage_ref[me]

        def scatter(k):
            return pltpu.make_async_remote_copy(
                src_ref=stage_ref.at[_index(_peer(k))], dst_ref=mod_ref.at[me],
                send_sem=send_sems.at[1, k - 1], recv_sem=recv_sems.at[1, k - 1],
                device_id=_peer(k), device_id_type=MESH)

        for k in range(1, N_DEV):
            scatter(k).start()
        for k in range(1, N_DEV):
            scatter(k).wait_recv()
        for k in range(1, N_DEV):
            gather_taps(k).wait_recv()
        for k in range(1, N_DEV):
            gather(k).wait_send()
            scatter(k).wait_send()
            gather_taps(k).wait_send()

    vmem = pl.BlockSpec(memory_space=pltpu.VMEM)
    return pl.pallas_call(
        body, name="ada_exchange",
        in_specs=[vmem] * 4, out_specs=[vmem] * 3,
        out_shape=[jax.ShapeDtypeStruct((N_DEV, 8, d), F32), jax.ShapeDtypeStruct((N_DEV, 8, ncol), F32),
                   jax.ShapeDtypeStruct((N_DEV,) + taps.shape, F32)],
        scratch_shapes=[pltpu.VMEM((N_DEV, 8, ncol), F32), pltpu.SemaphoreType.DMA((3, N_DEV - 1)),
                        pltpu.SemaphoreType.DMA((3, N_DEV - 1))],
        compiler_params=_params(),
    )(c_rows, w_ada, b_ada_cols, taps)


def _gather_weights(shards):
    n_w = len(shards)

    def body(*refs):
        srcs, outs = refs[:n_w], refs[n_w:2 * n_w]
        send_sems, recv_sems, local_sems = refs[2 * n_w:]
        x, y, c = _place()
        me, sibling = (x, y, c), (x, y, 1 - c)
        chips = [(1 - x, y), (x, 1 - y), (1 - x, 1 - y)]

        def rows(w, pos):
            r = shards[w].shape[0]
            return outs[w].at[pl.ds(pl.multiple_of(_index(pos) * r, 16), r), :]

        def copy(k, w, block, to, own=False):
            return pltpu.make_async_remote_copy(
                src_ref=srcs[w] if own else rows(w, block), dst_ref=rows(w, block),
                send_sem=send_sems.at[k, w], recv_sem=recv_sems.at[k, w], device_id=to, device_id_type=MESH)

        mine = [pltpu.make_async_copy(srcs[w], rows(w, me), local_sems.at[w]) for w in range(n_w)]
        for cp in mine:
            cp.start()
        first = [copy(0, w, me, sibling, own=True) for w in range(n_w)]
        first += [copy(1 + j, w, me, (*chip, c), own=True) for j, chip in enumerate(chips) for w in range(n_w)]
        for cp in first:
            cp.start()
        passed = []
        for j, chip in enumerate(chips):
            for w in range(n_w):
                copy(1 + j, w, (*chip, c), me).wait_recv()
                fwd = copy(4 + j, w, (*chip, c), sibling)
                fwd.start()
                passed.append(fwd)
        for w in range(n_w):
            copy(0, w, sibling, me).wait_recv()
        for j, chip in enumerate(chips):
            for w in range(n_w):
                copy(4 + j, w, (*chip, 1 - c), me).wait_recv()
        for cp in first + passed:
            cp.wait_send()
        for cp in mine:
            cp.wait()

    hbm = pl.BlockSpec(memory_space=pltpu.HBM)
    return pl.pallas_call(
        body, name="gather_weights",
        in_specs=[hbm] * n_w, out_specs=[hbm] * n_w,
        out_shape=[jax.ShapeDtypeStruct((N_DEV * s.shape[0], s.shape[1]), s.dtype) for s in shards],
        scratch_shapes=[pltpu.SemaphoreType.DMA((N_DEV - 1, n_w)), pltpu.SemaphoreType.DMA((N_DEV - 1, n_w)),
                        pltpu.SemaphoreType.DMA((n_w,))],
        compiler_params=_params(),
    )(*shards)


def _scatter_grads(grads):
    n_w = len(grads)

    def body(*refs):
        srcs, outs = refs[:n_w], refs[n_w:2 * n_w]
        send_sems, recv_sems, local_sems = refs[2 * n_w:]
        me = _index(_place())

        def slab(w, dev):
            r = grads[w].shape[0] // N_DEV
            return srcs[w].at[pl.ds(pl.multiple_of(dev * r, 16), r), :]

        def copy(k, w):
            return pltpu.make_async_remote_copy(
                src_ref=slab(w, _index(_peer(k))), dst_ref=outs[w].at[me],
                send_sem=send_sems.at[k - 1, w], recv_sem=recv_sems.at[k - 1, w],
                device_id=_peer(k), device_id_type=MESH)

        mine = [pltpu.make_async_copy(slab(w, me), outs[w].at[me], local_sems.at[w]) for w in range(n_w)]
        for cp in mine:
            cp.start()
        sends = [copy(k, w) for k in range(1, N_DEV) for w in range(n_w)]
        for cp in sends:
            cp.start()
        for cp in sends:
            cp.wait_recv()
        for cp in sends:
            cp.wait_send()
        for cp in mine:
            cp.wait()

    hbm = pl.BlockSpec(memory_space=pltpu.HBM)
    return pl.pallas_call(
        body, name="scatter_grads",
        in_specs=[hbm] * n_w, out_specs=[hbm] * n_w,
        out_shape=[jax.ShapeDtypeStruct((N_DEV, g.shape[0] // N_DEV, g.shape[1]), g.dtype) for g in grads],
        scratch_shapes=[pltpu.SemaphoreType.DMA((N_DEV - 1, n_w)), pltpu.SemaphoreType.DMA((N_DEV - 1, n_w)),
                        pltpu.SemaphoreType.DMA((n_w,))],
        compiler_params=_params(),
    )(*grads)


def _peer_copies(mode, srcs, lands, send_sems, recv_sems):
    me = _index(_place())
    copies = []
    for k in range(1, N_DEV):
        peer = _peer(k)
        for w, (src, land) in enumerate(zip(srcs, lands)):
            if mode == "gather":
                r = src.shape[0]
                dst = land.at[pl.ds(pl.multiple_of(me * r, 16), r), :]
            elif mode == "allgather":
                dst = land.at[me]
            else:
                r = src.shape[0] // N_DEV
                src = src.at[pl.ds(pl.multiple_of(_index(peer) * r, 16), r), :]
                dst = land.at[me]
            copies.append(pltpu.make_async_remote_copy(
                src_ref=src, dst_ref=dst, send_sem=send_sems.at[(k - 1) * len(srcs) + w],
                recv_sem=recv_sems.at[(k - 1) * len(srcs) + w],
                device_id=peer, device_id_type=MESH))
    return copies


def _landing_zone(mode, src, me, name):
    cols = src.shape[1]
    if mode == "gather":
        r = src.shape[0]
        in_spec = pl.BlockSpec((r, cols), lambda i, me_ref: (0, 0))
        out_spec = pl.BlockSpec((r, cols), lambda i, me_ref: (me_ref[0], 0))
        out_shape = jax.ShapeDtypeStruct((N_DEV * r, cols), src.dtype)
    else:
        r = src.shape[0] // N_DEV
        in_spec = pl.BlockSpec((r, cols), lambda i, me_ref: (me_ref[0], 0))
        out_spec = pl.BlockSpec((1, r, cols), lambda i, me_ref: (me_ref[0], 0, 0))
        out_shape = jax.ShapeDtypeStruct((N_DEV, r, cols), src.dtype)

    def body(me_ref, s_ref, o_ref):
        o_ref[...] = s_ref[...].reshape(o_ref.shape)

    return pl.pallas_call(
        body, name=name, out_shape=out_shape,
        grid_spec=pltpu.PrefetchScalarGridSpec(num_scalar_prefetch=1, grid=(1,), in_specs=[in_spec], out_specs=out_spec),
        compiler_params=_params(("arbitrary",)),
    )(me.reshape(1).astype(jnp.int32), src)


def _exchange_start(mode, srcs, lands, name):
    n = len(srcs)

    def body(*refs):
        for cp in _peer_copies(mode, refs[:n], refs[n:2 * n], refs[2 * n], refs[2 * n + 1]):
            cp.start()
        refs[-1][...] = jnp.zeros_like(refs[-1])

    hbm, sem = pl.BlockSpec(memory_space=pltpu.HBM), pl.BlockSpec(memory_space=pltpu.SEMAPHORE)
    arrays = list(srcs) + list(lands)
    out = pl.pallas_call(
        body, name=name,
        out_shape=(pltpu.SemaphoreType.DMA(((N_DEV - 1) * n,)), pltpu.SemaphoreType.DMA(((N_DEV - 1) * n,)),
                   *[pltpu.HBM(a.shape, a.dtype) for a in arrays], jax.ShapeDtypeStruct((8, LANES), F32)),
        in_specs=[hbm] * (2 * n), out_specs=(sem, sem, *[hbm] * (2 * n), pl.BlockSpec(memory_space=pltpu.VMEM)),
        input_output_aliases={i: 2 + i for i in range(2 * n)},
        compiler_params=pltpu.CompilerParams(has_side_effects=pltpu.SideEffectType.DATAFLOW_SIDE_EFFECTING),
    )(*[pltpu.with_memory_space_constraint(a, pltpu.HBM) for a in arrays])
    return out[0], out[1], out[2:2 + n], out[2 + n:2 + 2 * n], out[-1]


def _exchange_wait(mode, send_sems, recv_sems, srcs, lands, after, name):
    n = len(srcs)

    def body(*refs):
        copies = _peer_copies(mode, refs[:n], refs[n:2 * n], refs[2 * n], refs[2 * n + 1])
        for cp in copies:
            cp.wait_send()
        for cp in copies:
            cp.wait_recv()

    hbm, sem = pl.BlockSpec(memory_space=pltpu.HBM), pl.BlockSpec(memory_space=pltpu.SEMAPHORE)
    arrays = list(srcs) + list(lands)
    out = pl.pallas_call(
        body, name=name, out_shape=tuple(pltpu.HBM(a.shape, a.dtype) for a in arrays),
        in_specs=[hbm] * (2 * n) + [sem, sem, pl.BlockSpec(memory_space=pl.ANY)], out_specs=tuple([hbm] * (2 * n)),
        input_output_aliases={i: i for i in range(2 * n)},
        compiler_params=pltpu.CompilerParams(has_side_effects=pltpu.SideEffectType.DATAFLOW_SIDE_EFFECTING),
    )(*arrays, send_sems, recv_sems, after)
    return out[n:]


SMALL_WEIGHTS = ("b_ada", "g_pre_mix", "g_post_mix", "g_pre_ffn", "g_post_ffn", "w_pool", "b_pool", "pool_scale", "conv_b")


MOD_ROWS = ((0, 0), (0, 1), (1, 3), (1, 0), (1, 1), (2, 0))


def _small_sum(mine, gathered):
    n_l = len(mine)
    d = mine[0].shape[1]

    def body(*refs):
        loc, got = refs[:n_l], refs[n_l:2 * n_l]
        tot_refs, dmod_ref = refs[2 * n_l:3 * n_l], refs[3 * n_l]
        me = _index(_place())
        part = lambda a, dev: jnp.where(dev == me, loc[a][...], got[a][dev])
        for a in range(n_l):
            tot = part(a, 0)
            for dev in range(1, N_DEV):
                tot = tot + part(a, dev)
            tot_refs[a][...] = tot
        for dev in range(N_DEV):
            for k, (a, r) in enumerate(MOD_ROWS):
                dmod_ref[dev:dev + 1, k * d:(k + 1) * d] = part(a, dev)[r:r + 1, :]

    vmem = pl.BlockSpec(memory_space=pltpu.VMEM)
    out = pl.pallas_call(
        body, name="small_sum", in_specs=[vmem] * (2 * n_l), out_specs=[vmem] * (n_l + 1),
        out_shape=[jax.ShapeDtypeStruct(a.shape, F32) for a in mine] + [jax.ShapeDtypeStruct((N_DEV, 6 * d), F32)],
        compiler_params=_params(),
    )(*mine, *gathered)
    return out[:n_l], out[n_l]


def _small_adam(totals, weights, moms, vels):
    n_t, n_w = len(totals), len(weights)

    def body(*refs):
        t_in, t_mix, t_ffn, t_pool, t_blk, t_conv, _ = (r[...] for r in refs[:n_t])
        w_refs, m_refs, v_refs = (refs[n_t + k * n_w:n_t + (k + 1) * n_w] for k in range(3))
        outs = refs[n_t + 3 * n_w:]

        def update(idx, g, at=()):
            sel = lambda ref: ref.at[at] if at else ref
            delta, nm, nv = _adam_math(sel(w_refs[idx])[...], g, sel(m_refs[idx])[...], sel(v_refs[idx])[...])
            for k, val in enumerate((g, delta, nm, nv)):
                sel(outs[4 * idx + k])[...] = val

        tots = (t_in, t_mix, t_ffn)
        update(0, jnp.concatenate([tots[a][r:r + 1] for a, r in MOD_ROWS], axis=1))
        update(1, t_in[2:3])
        update(2, t_mix[4:5])
        update(3, t_mix[2:3])
        update(4, t_ffn[1:2])
        for gi in range(len(POOL_WINDOWS)):
            update(5, t_blk[gi], at=(0, gi))
        update(6, jnp.concatenate([t_pool[0:1, gi * HEAD_DIM:(gi + 1) * HEAD_DIM] for gi in range(len(POOL_WINDOWS))], axis=0),
               at=(0,))
        update(7, t_pool[1:2])
        update(8, t_conv[3:4])

    vmem = pl.BlockSpec(memory_space=pltpu.VMEM)
    return pl.pallas_call(
        body, name="small_adam", in_specs=[vmem] * (n_t + 3 * n_w), out_specs=[vmem] * (4 * n_w),
        out_shape=[jax.ShapeDtypeStruct(w.shape, F32) for w in weights for _ in range(4)],
        compiler_params=_params(),
    )(*totals, *weights, *moms, *vels)


def _adam_math(w, g, m, v):
    m = ADAM_B1 * m + (1.0 - ADAM_B1) * g
    v = ADAM_B2 * v + (1.0 - ADAM_B2) * (g * g)
    m_hat = m / (1.0 - ADAM_B1 ** ADAM_STEP)
    v_hat = v / (1.0 - ADAM_B2 ** ADAM_STEP)
    delta = -ADAM_LR * (m_hat / (jnp.sqrt(v_hat) + ADAM_EPS) + ADAM_WD * w)
    return delta, m, v


def _adam(w, g, m, v, name, tr):
    rows, cols = w.shape

    def body(w_ref, g_ref, m_ref, v_ref, d_ref, nm_ref, nv_ref):
        d_ref[...], nm_ref[...], nv_ref[...] = _adam_math(w_ref[...], g_ref[...], m_ref[...], v_ref[...])

    spec = pl.BlockSpec((tr, cols), lambda i: (i, 0))
    shape = jax.ShapeDtypeStruct((rows, cols), F32)
    return pl.pallas_call(
        body, name=name, grid=(rows // tr,), in_specs=[spec] * 4, out_specs=[spec] * 3,
        out_shape=[shape] * 3, compiler_params=_params(("arbitrary",)),
    )(w, g, m, v)


def _sum_adam(parts, w, m, v, name, tr):
    _, rows, cols = parts.shape

    def body(p_ref, w_ref, m_ref, v_ref, g_ref, d_ref, nm_ref, nv_ref):
        g = p_ref[0].astype(F32)
        for dev in range(1, N_DEV):
            g = g + p_ref[dev].astype(F32)
        g_ref[...] = g
        d_ref[...], nm_ref[...], nv_ref[...] = _adam_math(w_ref[...], g, m_ref[...], v_ref[...])

    spec = pl.BlockSpec((tr, cols), lambda i: (i, 0))
    shape = jax.ShapeDtypeStruct((rows, cols), F32)
    return pl.pallas_call(
        body, name=name, grid=(rows // tr,),
        in_specs=[pl.BlockSpec((N_DEV, tr, cols), lambda i: (0, i, 0)), spec, spec, spec],
        out_specs=[spec] * 4, out_shape=[shape] * 4, compiler_params=_params(("arbitrary",)),
    )(parts, w, m, v)


def _ada_grad_adam(c_all, dmod_cols, w, m, v, tr):
    rows, cols = w.shape

    def body(c_ref, dm_ref, w_ref, m_ref, v_ref, g_ref, d_ref, nm_ref, nv_ref):
        cv = c_ref[...]
        act = cv * jax.nn.sigmoid(cv)
        g = lax.dot_general(act, dm_ref[...], TN, preferred_element_type=F32, precision=lax.Precision.HIGHEST)
        g_ref[...] = g
        d_ref[...], nm_ref[...], nv_ref[...] = _adam_math(w_ref[...], g, m_ref[...], v_ref[...])

    spec = pl.BlockSpec((tr, cols), lambda i: (i, 0))
    shape = jax.ShapeDtypeStruct((rows, cols), F32)
    return pl.pallas_call(
        body, name="ada_grad_adam", grid=(rows // tr,),
        in_specs=[pl.BlockSpec((N_DEV, tr), lambda i: (0, i)), pl.BlockSpec((N_DEV, cols), lambda i: (0, 0)), spec, spec, spec],
        out_specs=[spec] * 4, out_shape=[shape] * 4, compiler_params=_params(("arbitrary",)),
    )(c_all, dmod_cols, w, m, v)


def _rope_tables(positions):
    s_len = positions.shape[0]
    inv_freq = ROPE_THETA ** (-jnp.arange(0, 2 * ROT_HALF, 2, dtype=F32) / (2 * ROT_HALF))
    ang = positions.astype(F32)[:, None] * inv_freq
    cos, sin = jnp.cos(ang), jnp.sin(ang)
    rest = HEAD_DIM - 2 * ROT_HALF
    zero = lambda n: jnp.zeros((s_len, n), F32)
    head = jnp.stack([jnp.concatenate([cos, cos, jnp.ones((s_len, rest), F32)], axis=1),
                      jnp.concatenate([-sin, zero(HEAD_DIM - ROT_HALF)], axis=1),
                      jnp.concatenate([zero(ROT_HALF), sin, zero(rest)], axis=1)])
    return jnp.tile(head, (1, 1, LANES // HEAD_DIM))


def _pad_rows(a, rows):
    return jnp.pad(a, ((0, rows - a.shape[0]), (0, 0)))


def _as_rows(a, rows):
    flat = a.reshape(-1)
    return jnp.pad(flat, (0, rows * LANES - flat.shape[0])).reshape(rows, LANES)


def _sequence_step(xs, target, rope, mods, gains, w_in_t, w_out_t, fetch_ffn, send_ffn_grads, send_mix_grads, w_blk_b, b_pool_r,
                   pool_scale_r, conv_w_all, conv_b):
    sh_m, sc_m, gt_m, sh_f, sc_f, gt_f = mods
    g_pre_mix, g_post_mix, g_pre_ffn, g_post_ffn = gains
    h1, u_pool, qkv = _premix_inproj(xs, sh_m, sc_m, g_pre_mix, w_in_t, rope, tm=512)
    o_g, lse_g = [], []
    for gi, dil in enumerate(DILATIONS):
        o, lse = _attn_fwd(qkv, gi, dil)
        o_g.append(o)
        lse_g.append(lse)
    x1, y1, h2, cat, attn, lse_all = _mix_out(xs, u_pool, o_g, lse_g, w_blk_b, b_pool_r, pool_scale_r, w_out_t,
                                              gt_m, g_post_mix, g_pre_ffn, sc_f, sh_f, tm=256)
    w_up_t, w_down_f = fetch_ffn(x1)
    gate, a_ffn, act, vd, dy2, dout, sums_ffn, loss_loc = _ffn_fwd_loss(h2, x1, target, w_up_t, w_down_f, conv_w_all, conv_b,
                                                              gt_f, g_post_ffn, tm=256, tf=2816, ck=256)

    dgc, dval, dw_down, dconv = _ffn_bwd_act(dy2, gate, a_ffn, act, vd, w_down_f, tm=256, ck=256)
    dup, dh2 = _ffn_bwd_up(dgc, dval, w_up_t, conv_w_all, tm=256)
    dw_up_t = _wgrad(dup, h2, "wgrad_up", tk=1024, tmm=1408)
    token = send_ffn_grads(dw_up_t, dw_down)
    if token is not None:
        sc_f = sc_f + token[0:1, 0:1]
    dx1, dpool, dattn, delta, dw_out_t, sums_mix = _mix_bwd(dh2, dout, x1, y1, cat, attn, w_out_t, sc_f, g_pre_ffn,
                                                           gt_m, g_post_mix, tm=256)
    du, dw_blk, sums_pool = _pool_bwd(dpool, u_pool, w_blk_b, b_pool_r, pool_scale_r, tm=512)
    dqkv = []
    for gi, dil in enumerate(DILATIONS):
        dqkv += list(_attn_bwd(qkv, dattn, lse_all, delta, gi, dil))
    dproj = _dproj_assemble(du, dqkv, rope, tm=512)
    dw_in_t = _wgrad(dproj, h1, "wgrad_in", tk=1024, tmm=1280)
    token = send_mix_grads(dw_in_t, dw_out_t)
    if token is not None:
        sc_m = sc_m + token[0:1, 0:1]
    grad_x, sums_in = _inproj_bwd(dproj, w_in_t, xs, dx1, sc_m, g_pre_mix, tm=256)
    return (loss_loc, grad_x, dw_in_t, dw_out_t, dw_up_t, dw_down, dw_blk, dconv,
            sums_in, sums_mix, sums_ffn, sums_pool)


def kernel(x, c, positions, w_ada, b_ada, g_pre_mix, g_post_mix, g_pre_ffn, g_post_ffn, w_in, w_pool, b_pool, pool_scale, w_out, w_up, conv_w, conv_b, w_down, loss_target, m_w_ada, m_b_ada, m_g_pre_mix, m_g_post_mix, m_g_pre_ffn, m_g_post_ffn, m_w_in, m_w_pool, m_b_pool, m_pool_scale, m_w_out, m_w_up, m_conv_w, m_conv_b, m_w_down, v_w_ada, v_b_ada, v_g_pre_mix, v_g_post_mix, v_g_pre_ffn, v_g_post_ffn, v_w_in, v_w_pool, v_b_pool, v_pool_scale, v_w_out, v_w_up, v_conv_w, v_conv_b, v_w_down):
    s_len, d = x.shape[1], x.shape[2]
    d_ff = w_down.shape[1] * N_DEV
    me = _index(_place())
    xs, target = x[0], loss_target[0]

    ncol = w_ada.shape[2]
    b_cols = lax.dynamic_slice(b_ada, (0, me * ncol), (1, ncol))
    c_all, mod, taps_all = _ada_exchange(jnp.broadcast_to(c, (8, d)), w_ada[0], b_cols, _pad_rows(conv_w[0], 8))
    c_all = c_all[:, 0, :]
    conv_w_all = jnp.transpose(taps_all[:, :3, :], (1, 0, 2)).reshape(3, d_ff)
    sh_m, sc_m, gt_m, sh_f, sc_f, gt_f = [mod[:, 0, :].reshape(1, -1)[:, k * d:(k + 1) * d] for k in range(6)]

    w_in_t, w_out_t = _gather_weights([w_in[0].T.astype(BF16), w_out[0].T.astype(BF16)])

    rope = _rope_tables(positions[0])
    w_blk = jnp.zeros((256, 256), F32)
    for gi in range(4):
        w_blk = lax.dynamic_update_slice(w_blk, w_pool[0, gi], (gi * HEAD_DIM, gi * HEAD_DIM))
    w_blk_b = w_blk.astype(BF16)
    b_pool_r, pool_scale_r = b_pool.reshape(1, 256), pool_scale.reshape(1, 256)

    up_sh, down_sh = w_up[0].T.astype(BF16), w_down[0].astype(BF16)
    w_in_t, conv_w_all, up_sh, down_sh = lax.optimization_barrier((w_in_t, conv_w_all, up_sh, down_sh))
    lands = [_landing_zone("gather", s, me, "land_" + nm) for s, nm in ((up_sh, "w_up"), (down_sh, "w_down"))]
    w_send, w_recv, w_src, w_land, w_token = _exchange_start("gather", [up_sh, down_sh], lands, "ffn_weights_start")

    def fetch_ffn(after):
        return _exchange_wait("gather", w_send, w_recv, w_src, w_land, after, "ffn_weights_wait")

    flight = []

    def send_ffn_grads(dw_up_t, dw_down):
        lands = [_landing_zone("scatter", dw_up_t, me, "land_dw_up"), _landing_zone("scatter", dw_down, me, "land_dw_down")]
        flight.extend(_exchange_start("scatter", [dw_up_t, dw_down], lands, "ffn_grads_start"))
        return flight[4]

    mix_flight = []

    def send_mix_grads(dw_in_t, dw_out_t):
        lands = [_landing_zone("scatter", dw_in_t, me, "land_dw_in"), _landing_zone("scatter", dw_out_t, me, "land_dw_out")]
        mix_flight.extend(_exchange_start("scatter", [dw_in_t, dw_out_t], lands, "mix_grads_start"))
        return mix_flight[4]

    (loss_loc, grad_x, dw_in_t, dw_out_t, _, _, dw_pool, dconv,
     sums_in, sums_mix, sums_ffn, sums_pool) = _sequence_step(
        xs, target, rope, (sh_m + w_token[0:1, 0:1], sc_m, gt_m, sh_f, sc_f, gt_f),
        (g_pre_mix, g_post_mix, g_pre_ffn, g_post_ffn),
        w_in_t, w_out_t, fetch_ffn, send_ffn_grads, send_mix_grads, w_blk_b, b_pool_r, pool_scale_r, conv_w_all, conv_b)

    small = [sums_in, sums_mix, sums_ffn, sums_pool, dw_pool, dconv, loss_loc]
    small_flight = _exchange_start("allgather", small, [lax.empty((N_DEV,) + a.shape, F32) for a in small], "small_start")

    parts_ffn = _exchange_wait("scatter", *flight[:4], small_flight[4], "ffn_grads_wait")
    big = {
        "w_up": [a.T for a in _sum_adam(parts_ffn[0], w_up[0].T, m_w_up[0].T, v_w_up[0].T, "adam_w_up", 64)],
        "w_down": _sum_adam(parts_ffn[1], w_down[0], m_w_down[0], v_w_down[0], "adam_w_down", 32),
    }

    rep_w = [b_ada, g_pre_mix, g_post_mix, g_pre_ffn, g_post_ffn, w_pool, b_pool, pool_scale, conv_b]
    rep_m = [m_b_ada, m_g_pre_mix, m_g_post_mix, m_g_pre_ffn, m_g_post_ffn, m_w_pool, m_b_pool, m_pool_scale, m_conv_b]
    rep_v = [v_b_ada, v_g_pre_mix, v_g_post_mix, v_g_pre_ffn, v_g_post_ffn, v_w_pool, v_b_pool, v_pool_scale, v_conv_b]
    gathered = _exchange_wait("allgather", *small_flight[:4], big["w_down"][0], "small_wait")
    totals, dmod_all = _small_sum(small, gathered)
    dconv_tot, loss_tot = totals[5], totals[6]
    rep_out = _small_adam(totals, rep_w, rep_m, rep_v)
    g_rep, d_rep, nm_rep, nv_rep = (rep_out[k::4] for k in range(4))

    fcol = d_ff // N_DEV
    g_cw = lax.dynamic_slice(dconv_tot, (0, me * fcol), (3, fcol))
    d_cw, nm_cw, nv_cw = _adam(conv_w[0], g_cw, m_conv_w[0], v_conv_w[0], "adam_conv_w", 3)

    dmod_cols = lax.dynamic_slice(dmod_all, (0, me * ncol), (N_DEV, ncol))
    g_ada, d_ada, nm_ada, nv_ada = _ada_grad_adam(c_all, dmod_cols, w_ada[0], m_w_ada[0], v_w_ada[0], 256)

    parts_mix = _exchange_wait("scatter", *mix_flight[:4], g_ada, "mix_grads_wait")
    big["w_in"] = [a.T for a in _sum_adam(parts_mix[0], w_in[0].T, m_w_in[0].T, v_w_in[0].T, "adam_w_in", 64)]
    big["w_out"] = [a.T for a in _sum_adam(parts_mix[1], w_out[0].T, m_w_out[0].T, v_w_out[0].T, "adam_w_out", 128)]

    loss = loss_tot[0, 0]

    def group(k):
        rep = (g_rep, d_rep, nm_rep, nv_rep)[k]
        ada = (g_ada, d_ada, nm_ada, nv_ada)[k][None]
        cw = (g_cw, d_cw, nm_cw, nv_cw)[k][None]
        return [ada, rep[0], rep[1], rep[2], rep[3], rep[4], big["w_in"][k][None], rep[5], rep[6], rep[7],
                big["w_out"][k][None], big["w_up"][k][None], cw, rep[8], big["w_down"][k][None]]

    return (loss, grad_x[None], *group(0), *group(1), *group(2), *group(3))
```

```python
import functools
import math

import jax
import jax.numpy as jnp
from jax import lax
from jax.experimental import pallas as pl
from jax.experimental.pallas import tpu as pltpu

F32 = jnp.float32
BF16 = jnp.bfloat16
MESH = pl.DeviceIdType.MESH

N_DEV = 8
HEAD_DIM = 64
ROT_HALF = 8
ROPE_THETA = 500000.0
POOL_WINDOWS = (2, 4, 8, 16)
DILATIONS = (1, 4, 16)
BLOCK = 128
NORM_EPS = 1e-6
HALO = 16
MASKED = -1e30
ATTN_FWD_UNROLL = 4
ATTN_BWD_UNROLL = 2

ADAM_LR = 0.001
ADAM_B1 = 0.9
ADAM_B2 = 0.999
ADAM_EPS = 1e-08
ADAM_WD = 0.01
ADAM_STEP = 10

V7X_VMEM_LIMIT = 56 * 1024 * 1024
LANES = 128

NT = (((1,), (1,)), ((), ()))
NN = (((1,), (0,)), ((), ()))
TN = (((0,), (0,)), ((), ()))


def _dot(a, b, dims):
    return lax.dot_general(a, b, dims, preferred_element_type=F32)


def _params(sem=None, vmem=V7X_VMEM_LIMIT):
    if sem is None:
        return pltpu.CompilerParams(vmem_limit_bytes=vmem)
    return pltpu.CompilerParams(dimension_semantics=sem, vmem_limit_bytes=vmem)


def _rstd(v):
    return lax.rsqrt(jnp.mean(v * v, axis=-1, keepdims=True) + NORM_EPS)


def _norm_bwd(dn, n, rstd):
    return rstd * (dn - n * jnp.mean(dn * n, axis=-1, keepdims=True))


def _rope_fwd(p, rope_ref):
    return p * rope_ref[0] + pltpu.roll(p, LANES - ROT_HALF, 1) * rope_ref[1] + pltpu.roll(p, ROT_HALF, 1) * rope_ref[2]


def _rope_bwd(dp, rope_ref):
    return dp * rope_ref[0] + pltpu.roll(dp * rope_ref[1], ROT_HALF, 1) + pltpu.roll(dp * rope_ref[2], LANES - ROT_HALF, 1)


def _gelu_parts(v):
    k2 = 2.0 * math.sqrt(2.0 / math.pi)
    c = 0.044715
    v2 = v * v
    s = jax.nn.sigmoid(v * (k2 + (k2 * c) * v2))
    g = v * s
    dg = s + g * (1.0 - s) * (k2 + (3.0 * k2 * c) * v2)
    return g, dg


def _halo_before(i, tile):
    return jnp.maximum(i * (tile // HALO) - 1, 0)


def _premix_inproj(x, sh, sc, g, w_in_t, rope, tm):
    s_len, d = x.shape
    n_proj = w_in_t.shape[0]
    n_slab = (n_proj - 256) // LANES

    def body(x_ref, sh_ref, sc_ref, g_ref, w_ref, rope_ref, h_ref, up_ref, qkv_ref):
        xv = x_ref[...]
        h = (xv * _rstd(xv) * g_ref[...]) * (1.0 + sc_ref[...]) + sh_ref[...]
        hb = h.astype(BF16)
        h_ref[...] = hb
        up_ref[...] = _dot(hb, w_ref[0:256, :], NT)
        for pair in range(n_slab // 2):
            p = _dot(hb, w_ref[256 + 256 * pair:512 + 256 * pair, :], NT)
            for half in range(2):
                ph = p[:, half * LANES:(half + 1) * LANES]
                if pair < 6:
                    ph = _rope_fwd(ph, rope_ref)
                if pair < 3:
                    ph = ph * (HEAD_DIM ** -0.5)
                qkv_ref[2 * pair + half] = ph

    vec = pl.BlockSpec((1, d), lambda i: (0, 0))
    return pl.pallas_call(
        body, name="premix_inproj", grid=(s_len // tm,),
        in_specs=[pl.BlockSpec((tm, d), lambda i: (i, 0)), vec, vec, vec,
                  pl.BlockSpec((n_proj, d), lambda i: (0, 0)),
                  pl.BlockSpec((3, tm, LANES), lambda i: (0, i, 0))],
        out_specs=[pl.BlockSpec((tm, d), lambda i: (i, 0)),
                   pl.BlockSpec((tm, 256), lambda i: (i, 0)),
                   pl.BlockSpec((n_slab, tm, LANES), lambda i: (0, i, 0))],
        out_shape=[jax.ShapeDtypeStruct((s_len, d), BF16),
                   jax.ShapeDtypeStruct((s_len, 256), F32),
                   jax.ShapeDtypeStruct((n_slab, s_len, LANES), F32)],
        compiler_params=_params(("arbitrary",)),
    )(x, sh, sc, g, w_in_t, rope)


def _block_rows(n, r, dil):
    start = n * (BLOCK * dil) + r
    if dil == 1:
        return pl.ds(pl.multiple_of(start, BLOCK), BLOCK)
    return pl.ds(start, BLOCK, stride=dil)


def _band_mask(n):
    ri = lax.broadcasted_iota(jnp.int32, (BLOCK, 2 * BLOCK), 0)
    cj = lax.broadcasted_iota(jnp.int32, (BLOCK, 2 * BLOCK), 1)
    cur = (cj >= BLOCK) & (cj - BLOCK <= ri)
    prev = (cj < BLOCK) & (cj >= ri) & (n > 0)
    return cur | prev


def _attn_fwd(qkv, group, dil):
    s_len = qkv.shape[1]
    nb = s_len // (BLOCK * dil)

    def body(q_ref, k_ref, v_ref, o_ref, lse_ref):
        lane = lax.broadcasted_iota(jnp.int32, (BLOCK, LANES), 1)
        first = lane < HEAD_DIM

        def block(t, carry):
            r, n = t // nb, t % nb
            cur = _block_rows(n, r, dil)
            prev = _block_rows(jnp.maximum(n - 1, 0), r, dil)
            q = q_ref[0, cur, :]
            kcat = jnp.concatenate([k_ref[0, prev, :], k_ref[0, cur, :]], axis=0).astype(BF16)
            vcat = jnp.concatenate([v_ref[0, prev, :], v_ref[0, cur, :]], axis=0).astype(BF16)
            valid = _band_mask(n)
            q2 = jnp.concatenate([jnp.where(first, q, 0.0), jnp.where(first, 0.0, q)], axis=0).astype(BF16)
            s = jnp.where(jnp.concatenate([valid, valid], axis=0), _dot(q2, kcat, NT), MASKED)
            m = jnp.max(s, axis=-1, keepdims=True)
            p = jnp.exp(s - m)
            den = jnp.sum(p, axis=-1, keepdims=True)
            o2 = _dot(p.astype(BF16), vcat, NN) / den
            lse2 = m + jnp.log(den)
            o_ref[0, cur, :] = jnp.where(first, o2[:BLOCK], o2[BLOCK:])
            lse_ref[0, cur, :] = jnp.where(first, lse2[:BLOCK], lse2[BLOCK:])
            return carry

        lax.fori_loop(0, nb * dil, block, 0, unroll=ATTN_FWD_UNROLL)

    def slab(base):
        return pl.BlockSpec((1, s_len, LANES), lambda s: (base + 2 * group + s, 0, 0))

    out = pl.BlockSpec((1, s_len, LANES), lambda s: (s, 0, 0))
    shape = jax.ShapeDtypeStruct((2, s_len, LANES), F32)
    return pl.pallas_call(
        body, name=f"attn_fwd_d{dil}", grid=(2,),
        in_specs=[slab(0), slab(6), slab(12)], out_specs=[out, out], out_shape=[shape, shape],
        compiler_params=_params(("arbitrary",)),
    )(qkv, qkv, qkv)


def _pool_mixed(u, halo, i, tm):
    ue = jnp.concatenate([halo, u], axis=0)
    s2 = ue + pltpu.roll(ue, 1, 0)
    s4 = s2 + pltpu.roll(s2, 2, 0)
    s8 = s4 + pltpu.roll(s4, 4, 0)
    s16 = s8 + pltpu.roll(s8, 8, 0)
    grp = lax.broadcasted_iota(jnp.int32, (tm, 256), 1) // HEAD_DIM
    pick = lambda a, b, c, e: jnp.where(grp == 0, a, jnp.where(grp == 1, b, jnp.where(grp == 2, c, e)))
    win_sum = pick(s2[HALO:], s4[HALO:], s8[HALO:], s16[HALO:])
    pos = (i * tm + lax.broadcasted_iota(jnp.int32, (tm, 256), 0)).astype(F32)
    count = jnp.minimum(pos + 1.0, pick(*[float(w) for w in POOL_WINDOWS]))
    return win_sum / count - u, count


def _mix_out(x, u_pool, o_g, lse_g, w_blk, b_pool, pool_scale, w_out_t, gt_m, g_post_mix, g_pre_ffn, sc_f, sh_f, tm):
    s_len, d = x.shape

    def body(x_ref, u_ref, uh_ref, o0, o1, o2, l0, l1, l2, wb_ref, bp_ref, ps_ref, wo_ref,
             gt_ref, g1_ref, g2_ref, sc_ref, sh_ref,
             x1_ref, y1_ref, h2_ref, cat_ref, attn_ref, lall_ref):
        i = pl.program_id(0)
        u = u_ref[...]
        halo = uh_ref[...] * (i > 0).astype(F32)
        mixed, _ = _pool_mixed(u, halo, i, tm)
        y = _dot(mixed.astype(BF16), wb_ref[...], NN) + bp_ref[...]
        pool = y * ps_ref[...]
        attn = []
        for s in range(2):
            la, lb, lc = l0[s], l1[s], l2[s]
            mx = jnp.maximum(jnp.maximum(la, lb), lc)
            ea, eb, ec = jnp.exp(la - mx), jnp.exp(lb - mx), jnp.exp(lc - mx)
            den = ea + eb + ec
            lall_ref[s] = mx + jnp.log(den)
            attn.append((ea / den) * o0[s] + (eb / den) * o1[s] + (ec / den) * o2[s])
        attn = jnp.concatenate(attn, axis=1)
        attn_ref[...] = attn
        cat = jnp.concatenate([pool, attn], axis=1).astype(BF16)
        cat_ref[...] = cat
        y1 = _dot(cat, wo_ref[...], NT)
        y1_ref[...] = y1
        x1 = x_ref[...] + gt_ref[...] * (y1 * _rstd(y1) * g1_ref[...])
        x1_ref[...] = x1
        h2 = (x1 * _rstd(x1) * g2_ref[...]) * (1.0 + sc_ref[...]) + sh_ref[...]
        h2_ref[...] = h2.astype(BF16)

    tile = lambda w: pl.BlockSpec((tm, w), lambda i: (i, 0))
    slab = pl.BlockSpec((2, tm, LANES), lambda i: (0, i, 0))
    const = lambda a: pl.BlockSpec(a.shape, lambda i: (0,) * a.ndim)
    return pl.pallas_call(
        body, name="mix_out", grid=(s_len // tm,),
        in_specs=[tile(d), tile(256), pl.BlockSpec((HALO, 256), lambda i: (_halo_before(i, tm), 0)),
                  slab, slab, slab, slab, slab, slab,
                  const(w_blk), const(b_pool), const(pool_scale), const(w_out_t),
                  const(gt_m), const(g_post_mix), const(g_pre_ffn), const(sc_f), const(sh_f)],
        out_specs=[tile(d), tile(d), tile(d), tile(512), tile(256), slab],
        out_shape=[jax.ShapeDtypeStruct((s_len, d), F32), jax.ShapeDtypeStruct((s_len, d), F32),
                   jax.ShapeDtypeStruct((s_len, d), BF16), jax.ShapeDtypeStruct((s_len, 512), BF16),
                   jax.ShapeDtypeStruct((s_len, 256), F32), jax.ShapeDtypeStruct((2, s_len, LANES), F32)],
        compiler_params=_params(("arbitrary",)),
    )(x, u_pool, u_pool, *o_g, *lse_g, w_blk, b_pool, pool_scale, w_out_t, gt_m, g_post_mix, g_pre_ffn, sc_f, sh_f)


def _conv_gate(gate_ext, cw, cb):
    gc = gate_ext * cw[2:3, :] + pltpu.roll(gate_ext, 1, 0) * cw[1:2, :] + pltpu.roll(gate_ext, 2, 0) * cw[0:1, :]
    return gc[HALO:] + cb


def _ffn_fwd_loss(h2, x1, target, w_up_t, w_down, conv_w, conv_b, gt_f, g_post_ffn, tm, tf, ck):
    s_len, d = x1.shape
    d_ff = w_down.shape[0]
    n_f = d_ff // tf

    def body(h_ref, hh_ref, x1_ref, tgt_ref, wg_ref, wv_ref, wd_ref, cw_ref, cb_ref, gt_ref, g_ref,
             gate_ref, a_ref, act_ref, vd_ref, dy2_ref, dout_ref, sums_ref, loss_ref, acc_ref):
        i, j = pl.program_id(0), pl.program_id(1)

        @pl.when((i == 0) & (j == 0))
        def _():
            sums_ref[...] = jnp.zeros_like(sums_ref)
            loss_ref[...] = jnp.zeros_like(loss_ref)

        h = h_ref[...]
        h_ext = jnp.concatenate([hh_ref[...], h], axis=0)
        row = lax.broadcasted_iota(jnp.int32, (tm + HALO, ck), 0)
        no_halo = (row < HALO) & (i == 0)

        def up(c):
            cs = slice(c * ck, (c + 1) * ck)
            return jnp.where(no_halo, 0.0, _dot(h_ext, wg_ref[cs, :], NT)), _dot(h, wv_ref[cs, :], NT)

        part = None
        n_c = tf // ck
        nxt = up(0)
        for c in range(n_c):
            cs = slice(c * ck, (c + 1) * ck)
            gate_ext, val = nxt
            if c + 1 < n_c:
                nxt = up(c + 1)
            act, dact = _gelu_parts(_conv_gate(gate_ext, cw_ref[:, cs], cb_ref[:, cs]))
            a = (act * val).astype(BF16)
            gate_ref[:, cs] = gate_ext[HALO:].astype(BF16)
            a_ref[:, cs] = a
            act_ref[:, cs] = act.astype(BF16)
            vd_ref[:, cs] = (val * dact).astype(BF16)
            p = _dot(a, wd_ref[cs, :], NN)
            part = p if part is None else part + p

        @pl.when(j == 0)
        def _():
            acc_ref[...] = part

        @pl.when(j > 0)
        def _():
            acc_ref[...] += part

        @pl.when(j == n_f - 1)
        def _():
            y2 = acc_ref[...]
            rstd = _rstd(y2)
            n = y2 * rstd
            rn = n * g_ref[...]
            err = x1_ref[...] + gt_ref[...] * rn - tgt_ref[...]
            loss_ref[...] += 0.5 * jnp.sum(jnp.mean(err * err, axis=-1, keepdims=True), axis=0, keepdims=True)
            dout = err * (1.0 / d)
            dout_ref[...] = dout
            drn = dout * gt_ref[...]
            sums_ref[0:1, :] += jnp.sum(dout * rn, axis=0, keepdims=True)
            sums_ref[1:2, :] += jnp.sum(drn * n, axis=0, keepdims=True)
            dy2_ref[...] = _norm_bwd(drn * g_ref[...], n, rstd).astype(BF16)

    tok = lambda w: pl.BlockSpec((tm, w), lambda i, j: (i, 0))
    tokf = pl.BlockSpec((tm, tf), lambda i, j: (i, j))
    vec = pl.BlockSpec((1, d), lambda i, j: (0, 0))
    once = {"pipeline_mode": pl.Buffered(1)} if n_f == 1 else {}
    return pl.pallas_call(
        body, name="ffn_fwd_loss", grid=(s_len // tm, n_f),
        in_specs=[tok(d), pl.BlockSpec((HALO, d), lambda i, j: (_halo_before(i, tm), 0)), tok(d), tok(d),
                  pl.BlockSpec((tf, d), lambda i, j: (j, 0), **once),
                  pl.BlockSpec((tf, d), lambda i, j: (j + n_f, 0), **once),
                  pl.BlockSpec((tf, d), lambda i, j: (j, 0), **once),
                  pl.BlockSpec((3, tf), lambda i, j: (0, j)), pl.BlockSpec((1, tf), lambda i, j: (0, j)), vec, vec],
        out_specs=[tokf, tokf, tokf, tokf, tok(d), tok(d), pl.BlockSpec((8, d), lambda i, j: (0, 0)),
                   pl.BlockSpec((8, LANES), lambda i, j: (0, 0))],
        out_shape=[jax.ShapeDtypeStruct((s_len, d_ff), BF16)] * 4
        + [jax.ShapeDtypeStruct((s_len, d), BF16), jax.ShapeDtypeStruct((s_len, d), F32),
                   jax.ShapeDtypeStruct((8, d), F32), jax.ShapeDtypeStruct((8, LANES), F32)],
        scratch_shapes=[pltpu.VMEM((tm, d), F32)],
        compiler_params=_params(("arbitrary", "arbitrary")),
    )(h2, h2, x1, target, w_up_t, w_up_t, w_down, conv_w, conv_b, gt_f, g_post_ffn)


def _ffn_bwd_act(dy2, gate, a, act, vd, w_down, tm, ck):
    s_len, d = dy2.shape
    d_ff = w_down.shape[0]
    n_t, n_c = s_len // tm, d_ff // ck

    def body(dy_ref, g_ref, gh_ref, a_ref, act_ref, vd_ref, wd_ref, dgc_ref, dval_ref, dwd_ref, dconv_ref, acc_ref):
        i = pl.program_id(0)

        @pl.when(i == 0)
        def _():
            acc_ref[...] = jnp.zeros_like(acc_ref)
            dconv_ref[...] = jnp.zeros_like(dconv_ref)

        dy = dy_ref[...]
        row = lax.broadcasted_iota(jnp.int32, (tm + HALO, ck), 0)
        no_halo = (row < HALO) & (i == 0)

        def down(c):
            return _dot(dy, wd_ref[c * ck:(c + 1) * ck, :], NT)

        nxt = down(0)
        for c in range(n_c):
            cs = slice(c * ck, (c + 1) * ck)
            da = nxt
            if c + 1 < n_c:
                nxt = down(c + 1)
            acc_ref[cs, :] += _dot(a_ref[:, cs], dy, TN)
            gate_ext = jnp.where(no_halo, 0.0, jnp.concatenate([gh_ref[:, cs], g_ref[:, cs]], axis=0).astype(F32))
            dgc = da * vd_ref[:, cs].astype(F32)
            dgc_ref[:, cs] = dgc.astype(BF16)
            dval_ref[:, cs] = (da * act_ref[:, cs].astype(F32)).astype(BF16)
            rows = [jnp.sum(dgc * pltpu.roll(gate_ext, 2 - k, 0)[HALO:], axis=0, keepdims=True) for k in range(2)]
            rows += [jnp.sum(dgc * gate_ext[HALO:], axis=0, keepdims=True), jnp.sum(dgc, axis=0, keepdims=True),
                     jnp.zeros((4, ck), F32)]
            dconv_ref[:, cs] += jnp.concatenate(rows, axis=0)

        @pl.when(i == n_t - 1)
        def _():
            dwd_ref[...] = acc_ref[...].astype(BF16)

    tokf = pl.BlockSpec((tm, d_ff), lambda i: (i, 0))
    return pl.pallas_call(
        body, name="ffn_bwd_act", grid=(n_t,),
        in_specs=[pl.BlockSpec((tm, d), lambda i: (i, 0)), tokf,
                  pl.BlockSpec((HALO, d_ff), lambda i: (_halo_before(i, tm), 0)), tokf, tokf, tokf,
                  pl.BlockSpec((d_ff, d), lambda i: (0, 0), pipeline_mode=pl.Buffered(1))],
        out_specs=[tokf, tokf, pl.BlockSpec((d_ff, d), lambda i: (0, 0)), pl.BlockSpec((8, d_ff), lambda i: (0, 0))],
        out_shape=[jax.ShapeDtypeStruct((s_len, d_ff), BF16), jax.ShapeDtypeStruct((s_len, d_ff), BF16),
                   jax.ShapeDtypeStruct((d_ff, d), BF16), jax.ShapeDtypeStruct((8, d_ff), F32)],
        scratch_shapes=[pltpu.VMEM((d_ff, d), F32)],
        compiler_params=_params(("arbitrary",)),
    )(dy2, gate, gate, a, act, vd, w_down)


def _ffn_bwd_up(dgc, dval, w_up_t, conv_w, tm):
    s_len, d_ff = dgc.shape
    d = w_up_t.shape[1]
    n_t = s_len // tm

    def body(dg_ref, dgn_ref, dv_ref, cw_ref, w_ref, dup_ref, dh_ref):
        i = pl.program_id(0)
        nxt = dgn_ref[...].astype(F32) * (i < n_t - 1).astype(F32)
        ext = jnp.concatenate([dg_ref[...].astype(F32), nxt], axis=0)
        rows = tm + HALO
        dgate = (ext * cw_ref[2:3, :] + pltpu.roll(ext, rows - 1, 0) * cw_ref[1:2, :]
                 + pltpu.roll(ext, rows - 2, 0) * cw_ref[0:1, :])[:tm]
        dup = jnp.concatenate([dgate.astype(BF16), dv_ref[...]], axis=1)
        dup_ref[...] = dup
        dh_ref[...] = _dot(dup, w_ref[...], NN)

    tokf = pl.BlockSpec((tm, d_ff), lambda i: (i, 0))
    return pl.pallas_call(
        body, name="ffn_bwd_up", grid=(n_t,),
        in_specs=[tokf, pl.BlockSpec((HALO, d_ff), lambda i: (jnp.minimum((i + 1) * (tm // HALO), s_len // HALO - 1), 0)),
                  tokf, pl.BlockSpec((3, d_ff), lambda i: (0, 0)), pl.BlockSpec((2 * d_ff, d), lambda i: (0, 0))],
        out_specs=[pl.BlockSpec((tm, 2 * d_ff), lambda i: (i, 0)), pl.BlockSpec((tm, d), lambda i: (i, 0))],
        out_shape=[jax.ShapeDtypeStruct((s_len, 2 * d_ff), BF16), jax.ShapeDtypeStruct((s_len, d), F32)],
        compiler_params=_params(("arbitrary",)),
    )(dgc, dgc, dval, conv_w, w_up_t)


def _mix_bwd(dh2, dout, x1, y1, cat, attn, w_out_t, sc_f, g_pre_ffn, gt_m, g_post_mix, tm):
    s_len, d = x1.shape
    n_t = s_len // tm

    def body(dh_ref, do_ref, x1_ref, y1_ref, cat_ref, at_ref, wo_ref, sc_ref, g2_ref, gt_ref, g1_ref,
             dx1_ref, dpool_ref, dattn_ref, delta_ref, dwo_ref, sums_ref, acc_ref):
        i = pl.program_id(0)
        dh = dh_ref[...]
        x1 = x1_ref[...]
        r2 = _rstd(x1)
        n2 = x1 * r2
        ng = n2 * g2_ref[...]
        dng = dh * (1.0 + sc_ref[...])
        dx1 = do_ref[...] + _norm_bwd(dng * g2_ref[...], n2, r2)
        dx1_ref[...] = dx1
        y1 = y1_ref[...]
        r1 = _rstd(y1)
        n1 = y1 * r1
        drn = dx1 * gt_ref[...]
        dy1 = _norm_bwd(drn * g1_ref[...], n1, r1).astype(BF16)
        dcat = _dot(dy1, wo_ref[...], NN)
        dpool_ref[...] = dcat[:, 0:256]
        lane = lax.broadcasted_iota(jnp.int32, (tm, LANES), 1)
        first = lane < HEAD_DIM
        for s in range(2):
            da = dcat[:, 256 + s * LANES:256 + (s + 1) * LANES]
            dattn_ref[s] = da
            prod = da * at_ref[:, s * LANES:(s + 1) * LANES]
            tot = jnp.sum(prod, axis=-1, keepdims=True)
            lo = jnp.sum(jnp.where(first, prod, 0.0), axis=-1, keepdims=True)
            delta_ref[s] = jnp.where(first, lo, tot - lo)
        dwo = _dot(dy1, cat_ref[...], TN)
        sums = jnp.concatenate(
            [jnp.sum(dh, axis=0, keepdims=True), jnp.sum(dh * ng, axis=0, keepdims=True),
             jnp.sum(dng * n2, axis=0, keepdims=True), jnp.sum(dx1 * (n1 * g1_ref[...]), axis=0, keepdims=True),
             jnp.sum(drn * n1, axis=0, keepdims=True), jnp.zeros((3, d), F32)], axis=0)

        @pl.when(i == 0)
        def _():
            acc_ref[...] = dwo
            sums_ref[...] = sums

        @pl.when(i > 0)
        def _():
            acc_ref[...] += dwo
            sums_ref[...] += sums

        @pl.when(i == n_t - 1)
        def _():
            dwo_ref[...] = acc_ref[...].astype(BF16)

    tile = lambda w: pl.BlockSpec((tm, w), lambda i: (i, 0))
    slab = pl.BlockSpec((2, tm, LANES), lambda i: (0, i, 0))
    vec = pl.BlockSpec((1, d), lambda i: (0, 0))
    return pl.pallas_call(
        body, name="mix_bwd", grid=(n_t,),
        in_specs=[tile(d), tile(d), tile(d), tile(d), tile(512), tile(256),
                  pl.BlockSpec((d, 512), lambda i: (0, 0)), vec, vec, vec, vec],
        out_specs=[tile(d), tile(256), slab, slab, pl.BlockSpec((d, 512), lambda i: (0, 0)),
                   pl.BlockSpec((8, d), lambda i: (0, 0))],
        out_shape=[jax.ShapeDtypeStruct((s_len, d), F32), jax.ShapeDtypeStruct((s_len, 256), F32),
                   jax.ShapeDtypeStruct((2, s_len, LANES), F32), jax.ShapeDtypeStruct((2, s_len, LANES), F32),
                   jax.ShapeDtypeStruct((d, 512), BF16), jax.ShapeDtypeStruct((8, d), F32)],
        scratch_shapes=[pltpu.VMEM((d, 512), F32)],
        compiler_params=_params(("arbitrary",)),
    )(dh2, dout, x1, y1, cat, attn, w_out_t, sc_f, g_pre_ffn, gt_m, g_post_mix)


def _pool_bwd(dpool, u_pool, w_blk, b_pool, pool_scale, tm):
    s_len = dpool.shape[0]
    n_t = s_len // tm

    def body(dp_ref, dpn_ref, u_ref, uh_ref, wb_ref, bp_ref, ps_ref, du_ref, dwp_ref, sums_ref, acc_ref):
        i = pl.program_id(0)
        u = u_ref[...]
        mixed, _ = _pool_mixed(u, uh_ref[...] * (i > 0).astype(F32), i, tm)
        mixed_b = mixed.astype(BF16)
        y = _dot(mixed_b, wb_ref[...], NN) + bp_ref[...]
        dp = dp_ref[...]
        dy = dp * ps_ref[...]
        dwb = _dot(mixed_b, dy.astype(BF16), TN)
        sums = jnp.concatenate([jnp.sum(dy, axis=0, keepdims=True), jnp.sum(dp * y, axis=0, keepdims=True),
                                jnp.zeros((6, 256), F32)], axis=0)
        dp_ext = jnp.concatenate([dp, dpn_ref[...] * (i < n_t - 1).astype(F32)], axis=0)
        dmix = _dot((dp_ext * ps_ref[...]).astype(BF16), wb_ref[...], NT)
        rows = tm + HALO
        grp = lax.broadcasted_iota(jnp.int32, (rows, 256), 1) // HEAD_DIM
        pick = lambda a, b, c, e: jnp.where(grp == 0, a, jnp.where(grp == 1, b, jnp.where(grp == 2, c, e)))
        pos = (i * tm + lax.broadcasted_iota(jnp.int32, (rows, 256), 0)).astype(F32)
        z = dmix / jnp.minimum(pos + 1.0, pick(*[float(w) for w in POOL_WINDOWS]))
        f2 = z + pltpu.roll(z, rows - 1, 0)
        f4 = f2 + pltpu.roll(f2, rows - 2, 0)
        f8 = f4 + pltpu.roll(f4, rows - 4, 0)
        f16 = f8 + pltpu.roll(f8, rows - 8, 0)
        du_ref[...] = (pick(f2, f4, f8, f16) - dmix)[:tm]

        @pl.when(i == 0)
        def _():
            acc_ref[...] = dwb
            sums_ref[...] = sums

        @pl.when(i > 0)
        def _():
            acc_ref[...] += dwb
            sums_ref[...] += sums

        @pl.when(i == n_t - 1)
        def _():
            full = acc_ref[...]
            for gi in range(len(POOL_WINDOWS)):
                lo = gi * HEAD_DIM
                dwp_ref[gi] = full[lo:lo + HEAD_DIM, lo:lo + HEAD_DIM]

    n_g = len(POOL_WINDOWS)
    tile = pl.BlockSpec((tm, 256), lambda i: (i, 0))
    const = lambda a: pl.BlockSpec(a.shape, lambda i: (0,) * a.ndim)
    return pl.pallas_call(
        body, name="pool_bwd", grid=(n_t,),
        in_specs=[tile, pl.BlockSpec((HALO, 256), lambda i: (jnp.minimum((i + 1) * (tm // HALO), s_len // HALO - 1), 0)),
                  tile, pl.BlockSpec((HALO, 256), lambda i: (_halo_before(i, tm), 0)),
                  const(w_blk), const(b_pool), const(pool_scale)],
        out_specs=[tile, pl.BlockSpec((n_g, HEAD_DIM, HEAD_DIM), lambda i: (0, 0, 0)), pl.BlockSpec((8, 256), lambda i: (0, 0))],
        out_shape=[jax.ShapeDtypeStruct((s_len, 256), F32), jax.ShapeDtypeStruct((n_g, HEAD_DIM, HEAD_DIM), F32),
                   jax.ShapeDtypeStruct((8, 256), F32)],
        scratch_shapes=[pltpu.VMEM((256, 256), F32)],
        compiler_params=_params(("arbitrary",)),
    )(dpool, dpool, u_pool, u_pool, w_blk, b_pool, pool_scale)


def _attn_bwd(qkv, dattn, lse_all, delta, group, dil):
    s_len = qkv.shape[1]
    nb = s_len // (BLOCK * dil)

    def body(q_ref, k_ref, v_ref, do_ref, l_ref, dl_ref, dq_ref, dk_ref, dv_ref):
        lane = lax.broadcasted_iota(jnp.int32, (BLOCK, LANES), 1)
        first = lane < HEAD_DIM

        def block(t, carry):
            dk_part, dv_part = carry
            r, n = t // nb, t % nb
            cur = _block_rows(n, r, dil)
            prev = _block_rows(jnp.maximum(n - 1, 0), r, dil)
            q = q_ref[0, cur, :]
            do = do_ref[0, cur, :]
            lse = l_ref[0, cur, :]
            dlt = dl_ref[0, cur, :]
            kcat = jnp.concatenate([k_ref[0, prev, :], k_ref[0, cur, :]], axis=0).astype(BF16)
            vcat = jnp.concatenate([v_ref[0, prev, :], v_ref[0, cur, :]], axis=0).astype(BF16)
            valid = _band_mask(n)
            stack = lambda a: jnp.concatenate([jnp.where(first, a, 0.0), jnp.where(first, 0.0, a)], axis=0)
            rows2 = lambda a: jnp.concatenate([a[:, 0:1], a[:, HEAD_DIM:HEAD_DIM + 1]], axis=0)
            q2, do2 = stack(q).astype(BF16), stack(do).astype(BF16)
            valid2 = jnp.concatenate([valid, valid], axis=0)
            p = jnp.where(valid2, jnp.exp(_dot(q2, kcat, NT) - rows2(lse)), 0.0)
            ds = (p * (_dot(do2, vcat, NT) - rows2(dlt))).astype(BF16)
            dq2 = _dot(ds, kcat, NN)
            dq_ref[0, cur, :] = jnp.where(first, dq2[:BLOCK], dq2[BLOCK:])
            dkc = _dot(ds, q2, TN)
            dvc = _dot(p.astype(BF16), do2, TN)
            dk_ref[0, prev, :] = dk_part + dkc[:BLOCK]
            dv_ref[0, prev, :] = dv_part + dvc[:BLOCK]
            dk_ref[0, cur, :] = dkc[BLOCK:]
            dv_ref[0, cur, :] = dvc[BLOCK:]
            return dkc[BLOCK:], dvc[BLOCK:]

        def blocks(tt, carry):
            for u in range(ATTN_BWD_UNROLL):
                carry = block(tt * ATTN_BWD_UNROLL + u, carry)
            return carry

        zero = jnp.zeros((BLOCK, LANES), F32)
        lax.fori_loop(0, nb * dil // ATTN_BWD_UNROLL, blocks, (zero, zero))

    def slab(base):
        return pl.BlockSpec((1, s_len, LANES), lambda s: (base + 2 * group + s, 0, 0))

    one = pl.BlockSpec((1, s_len, LANES), lambda s: (s, 0, 0))
    shape = jax.ShapeDtypeStruct((2, s_len, LANES), F32)
    return pl.pallas_call(
        body, name=f"attn_bwd_d{dil}", grid=(2,),
        in_specs=[slab(0), slab(6), slab(12), one, one, one],
        out_specs=[one, one, one], out_shape=[shape, shape, shape],
        compiler_params=_params(("arbitrary",)),
    )(qkv, qkv, qkv, dattn, lse_all, delta)


def _dproj_assemble(du, dqkv, rope, tm):
    s_len = du.shape[0]
    n_proj = 256 + 18 * LANES

    def body(du_ref, *refs):
        dref, rope_ref, dproj_ref = refs[:9], refs[9], refs[10]
        dproj_ref[:, 0:256] = du_ref[...].astype(BF16)
        col = 256
        for kind in range(3):
            for grp in range(3):
                for s in range(2):
                    piece = dref[3 * grp + kind][s]
                    if kind < 2:
                        piece = _rope_bwd(piece, rope_ref)
                    if kind == 0:
                        piece = piece * (HEAD_DIM ** -0.5)
                    dproj_ref[:, col:col + LANES] = piece.astype(BF16)
                    col += LANES

    slab = pl.BlockSpec((2, tm, LANES), lambda i: (0, i, 0))
    return pl.pallas_call(
        body, name="dproj_assemble", grid=(s_len // tm,),
        in_specs=[pl.BlockSpec((tm, 256), lambda i: (i, 0))] + [slab] * 9 + [pl.BlockSpec((3, tm, LANES), lambda i: (0, i, 0))],
        out_specs=pl.BlockSpec((tm, n_proj), lambda i: (i, 0)),
        out_shape=jax.ShapeDtypeStruct((s_len, n_proj), BF16),
        compiler_params=_params(("arbitrary",)),
    )(du, *dqkv, rope)


def _inproj_bwd(dproj, w_in_t, x, dx1, sc_m, g_pre_mix, tm):
    s_len, d = x.shape
    n_proj = w_in_t.shape[0]
    n_t = s_len // tm

    def body(dproj_ref, w_ref, x_ref, dx1_ref, sc_ref, g_ref, dx_ref, sums_ref):
        i = pl.program_id(0)
        dh = _dot(dproj_ref[...], w_ref[...], NN)
        xv = x_ref[...]
        r = _rstd(xv)
        n = xv * r
        dng = dh * (1.0 + sc_ref[...])
        dx_ref[...] = dx1_ref[...] + _norm_bwd(dng * g_ref[...], n, r)
        sums = jnp.concatenate([jnp.sum(dh, axis=0, keepdims=True), jnp.sum(dh * (n * g_ref[...]), axis=0, keepdims=True),
                                jnp.sum(dng * n, axis=0, keepdims=True), jnp.zeros((5, d), F32)], axis=0)

        @pl.when(i == 0)
        def _():
            sums_ref[...] = sums

        @pl.when(i > 0)
        def _():
            sums_ref[...] += sums

    tile = lambda w: pl.BlockSpec((tm, w), lambda i: (i, 0))
    vec = pl.BlockSpec((1, d), lambda i: (0, 0))
    return pl.pallas_call(
        body, name="inproj_bwd", grid=(n_t,),
        in_specs=[tile(n_proj), pl.BlockSpec((n_proj, d), lambda i: (0, 0)), tile(d), tile(d), vec, vec],
        out_specs=[tile(d), pl.BlockSpec((8, d), lambda i: (0, 0))],
        out_shape=[jax.ShapeDtypeStruct((s_len, d), F32), jax.ShapeDtypeStruct((8, d), F32)],
        compiler_params=_params(("arbitrary",)),
    )(dproj, w_in_t, x, dx1, sc_m, g_pre_mix)


def _wgrad(a, b, name, tk, tmm):
    s_len, m = a.shape
    n = b.shape[1]
    n_k = s_len // tk

    def body(a_ref, b_ref, o_ref, acc_ref):
        k = pl.program_id(1)
        part = _dot(a_ref[...], b_ref[...], TN)

        @pl.when(k == 0)
        def _():
            acc_ref[...] = part

        @pl.when(k > 0)
        def _():
            acc_ref[...] += part

        @pl.when(k == n_k - 1)
        def _():
            o_ref[...] = acc_ref[...].astype(BF16)

    return pl.pallas_call(
        body, name=name, grid=(m // tmm, n_k),
        in_specs=[pl.BlockSpec((tk, tmm), lambda j, k: (k, j)), pl.BlockSpec((tk, n), lambda j, k: (k, 0))],
        out_specs=pl.BlockSpec((tmm, n), lambda j, k: (j, 0)),
        out_shape=jax.ShapeDtypeStruct((m, n), BF16),
        scratch_shapes=[pltpu.VMEM((tmm, n), F32)],
        compiler_params=_params(("arbitrary", "arbitrary")),
    )(a, b)


def _place():
    return lax.axis_index("x"), lax.axis_index("y"), lax.axis_index("c")


def _peer(k):
    x, y, c = _place()
    bx, by, bc = (k >> 2) & 1, (k >> 1) & 1, k & 1
    return (x ^ bx if bx else x, y ^ by if by else y, c ^ bc if bc else c)


def _index(pos):
    return 4 * pos[0] + 2 * pos[1] + pos[2]


def _ada_exchange(c_rows, w_ada, b_ada_cols, taps):
    d = c_rows.shape[1]
    ncol = w_ada.shape[1]

    def body(c_ref, w_ref, b_ref, t_ref, call_ref, mod_ref, tall_ref, stage_ref, send_sems, recv_sems):
        me = _index(_place())
        call_ref[me] = c_ref[...]
        tall_ref[me] = t_ref[...]

        def gather(k):
            return pltpu.make_async_remote_copy(
                src_ref=c_ref, dst_ref=call_ref.at[me], send_sem=send_sems.at[0, k - 1], recv_sem=recv_sems.at[0, k - 1],
                device_id=_peer(k), device_id_type=MESH)

        def gather_taps(k):
            return pltpu.make_async_remote_copy(
                src_ref=t_ref, dst_ref=tall_ref.at[me], send_sem=send_sems.at[2, k - 1], recv_sem=recv_sems.at[2, k - 1],
                device_id=_peer(k), device_id_type=MESH)

        for k in range(1, N_DEV):
            gather(k).start()
        for k in range(1, N_DEV):
            gather_taps(k).start()
        for k in range(1, N_DEV):
            gather(k).wait_recv()
        cv = jnp.concatenate([call_ref[b, 0:1, :] for b in range(N_DEV)], axis=0)
        act = cv * jax.nn.sigmoid(cv)
        mod = lax.dot_general(act, w_ref[...], NN, preferred_element_type=F32,
                              precision=lax.Precision.HIGHEST) + b_ref[...]
        for b in range(N_DEV):
            stage_ref[b] = jnp.broadcast_to(mod[b:b + 1, :], (8, ncol))
        mod_ref[me] = stage_ref[me]

        def scatter(k):
            return pltpu.make_async_remote_copy(
                src_ref=stage_ref.at[_index(_peer(k))], dst_ref=mod_ref.at[me],
                send_sem=send_sems.at[1, k - 1], recv_sem=recv_sems.at[1, k - 1],
                device_id=_peer(k), device_id_type=MESH)

        for k in range(1, N_DEV):
            scatter(k).start()
        for k in range(1, N_DEV):
            scatter(k).wait_recv()
        for k in range(1, N_DEV):
            gather_taps(k).wait_recv()
        for k in range(1, N_DEV):
            gather(k).wait_send()
            scatter(k).wait_send()
            gather_taps(k).wait_send()

    vmem = pl.BlockSpec(memory_space=pltpu.VMEM)
    return pl.pallas_call(
        body, name="ada_exchange",
        in_specs=[vmem] * 4, out_specs=[vmem] * 3,
        out_shape=[jax.ShapeDtypeStruct((N_DEV, 8, d), F32), jax.ShapeDtypeStruct((N_DEV, 8, ncol), F32),
                   jax.ShapeDtypeStruct((N_DEV,) + taps.shape, F32)],
        scratch_shapes=[pltpu.VMEM((N_DEV, 8, ncol), F32), pltpu.SemaphoreType.DMA((3, N_DEV - 1)),
                        pltpu.SemaphoreType.DMA((3, N_DEV - 1))],
        compiler_params=_params(),
    )(c_rows, w_ada, b_ada_cols, taps)


def _gather_weights(shards):
    n_w = len(shards)

    def body(*refs):
        srcs, outs = refs[:n_w], refs[n_w:2 * n_w]
        send_sems, recv_sems, local_sems = refs[2 * n_w:]
        x, y, c = _place()
        me, sibling = (x, y, c), (x, y, 1 - c)
        chips = [(1 - x, y), (x, 1 - y), (1 - x, 1 - y)]

        def rows(w, pos):
            r = shards[w].shape[0]
            return outs[w].at[pl.ds(pl.multiple_of(_index(pos) * r, 16), r), :]

        def copy(k, w, block, to, own=False):
            return pltpu.make_async_remote_copy(
                src_ref=srcs[w] if own else rows(w, block), dst_ref=rows(w, block),
                send_sem=send_sems.at[k, w], recv_sem=recv_sems.at[k, w], device_id=to, device_id_type=MESH)

        mine = [pltpu.make_async_copy(srcs[w], rows(w, me), local_sems.at[w]) for w in range(n_w)]
        for cp in mine:
            cp.start()
        first = [copy(0, w, me, sibling, own=True) for w in range(n_w)]
        first += [copy(1 + j, w, me, (*chip, c), own=True) for j, chip in enumerate(chips) for w in range(n_w)]
        for cp in first:
            cp.start()
        passed = []
        for j, chip in enumerate(chips):
            for w in range(n_w):
                copy(1 + j, w, (*chip, c), me).wait_recv()
                fwd = copy(4 + j, w, (*chip, c), sibling)
                fwd.start()
                passed.append(fwd)
        for w in range(n_w):
            copy(0, w, sibling, me).wait_recv()
        for j, chip in enumerate(chips):
            for w in range(n_w):
                copy(4 + j, w, (*chip, 1 - c), me).wait_recv()
        for cp in first + passed:
            cp.wait_send()
        for cp in mine:
            cp.wait()

    hbm = pl.BlockSpec(memory_space=pltpu.HBM)
    return pl.pallas_call(
        body, name="gather_weights",
        in_specs=[hbm] * n_w, out_specs=[hbm] * n_w,
        out_shape=[jax.ShapeDtypeStruct((N_DEV * s.shape[0], s.shape[1]), s.dtype) for s in shards],
        scratch_shapes=[pltpu.SemaphoreType.DMA((N_DEV - 1, n_w)), pltpu.SemaphoreType.DMA((N_DEV - 1, n_w)),
                        pltpu.SemaphoreType.DMA((n_w,))],
        compiler_params=_params(),
    )(*shards)


def _scatter_grads(grads):
    n_w = len(grads)

    def body(*refs):
        srcs, outs = refs[:n_w], refs[n_w:2 * n_w]
        send_sems, recv_sems, local_sems = refs[2 * n_w:]
        me = _index(_place())

        def slab(w, dev):
            r = grads[w].shape[0] // N_DEV
            return srcs[w].at[pl.ds(pl.multiple_of(dev * r, 16), r), :]

        def copy(k, w):
            return pltpu.make_async_remote_copy(
                src_ref=slab(w, _index(_peer(k))), dst_ref=outs[w].at[me],
                send_sem=send_sems.at[k - 1, w], recv_sem=recv_sems.at[k - 1, w],
                device_id=_peer(k), device_id_type=MESH)

        mine = [pltpu.make_async_copy(slab(w, me), outs[w].at[me], local_sems.at[w]) for w in range(n_w)]
        for cp in mine:
            cp.start()
        sends = [copy(k, w) for k in range(1, N_DEV) for w in range(n_w)]
        for cp in sends:
            cp.start()
        for cp in sends:
            cp.wait_recv()
        for cp in sends:
            cp.wait_send()
        for cp in mine:
            cp.wait()

    hbm = pl.BlockSpec(memory_space=pltpu.HBM)
    return pl.pallas_call(
        body, name="scatter_grads",
        in_specs=[hbm] * n_w, out_specs=[hbm] * n_w,
        out_shape=[jax.ShapeDtypeStruct((N_DEV, g.shape[0] // N_DEV, g.shape[1]), g.dtype) for g in grads],
        scratch_shapes=[pltpu.SemaphoreType.DMA((N_DEV - 1, n_w)), pltpu.SemaphoreType.DMA((N_DEV - 1, n_w)),
                        pltpu.SemaphoreType.DMA((n_w,))],
        compiler_params=_params(),
    )(*grads)


def _peer_copies(mode, srcs, lands, send_sems, recv_sems):
    if mode in ("gather_ici", "gather_d2d"):
        x, y, c = _place()
        sibling = (x, y, 1 - c)
        chips = [(1 - x, y), (x, 1 - y), (1 - x, 1 - y)]
        n = len(lands)

        def rows(w, pos):
            r = lands[w].shape[0] // N_DEV
            return lands[w].at[pl.ds(pl.multiple_of(_index(pos) * r, 16), r), :]

        def copy(k, w, src, dst, to):
            return pltpu.make_async_remote_copy(src_ref=src, dst_ref=dst, send_sem=send_sems.at[k * n + w],
                                                recv_sem=recv_sems.at[k * n + w], device_id=to, device_id_type=MESH)

        if mode == "gather_ici":
            targets = [sibling] + [(*chip, c) for chip in chips]
            return [copy(k, w, srcs[w], rows(w, (x, y, c)), to) for k, to in enumerate(targets) for w in range(n)]
        return [copy(j, w, rows(w, (*chip, c)), rows(w, (*chip, c)), sibling)
                for j, chip in enumerate(chips) for w in range(n)]
    me = _index(_place())
    copies = []
    for k in range(1, N_DEV):
        peer = _peer(k)
        for w, (src, land) in enumerate(zip(srcs, lands)):
            if mode == "gather":
                r = src.shape[0]
                dst = land.at[pl.ds(pl.multiple_of(me * r, 16), r), :]
            elif mode == "allgather":
                dst = land.at[me]
            else:
                r = src.shape[0] // N_DEV
                src = src.at[pl.ds(pl.multiple_of(_index(peer) * r, 16), r), :]
                dst = land.at[me]
            copies.append(pltpu.make_async_remote_copy(
                src_ref=src, dst_ref=dst, send_sem=send_sems.at[(k - 1) * len(srcs) + w],
                recv_sem=recv_sems.at[(k - 1) * len(srcs) + w],
                device_id=peer, device_id_type=MESH))
    return copies


def _landing_zone(mode, src, me, name):
    cols = src.shape[1]
    if mode == "gather":
        r = src.shape[0]
        in_spec = pl.BlockSpec((r, cols), lambda i, me_ref: (0, 0))
        out_spec = pl.BlockSpec((r, cols), lambda i, me_ref: (me_ref[0], 0))
        out_shape = jax.ShapeDtypeStruct((N_DEV * r, cols), src.dtype)
    else:
        r = src.shape[0] // N_DEV
        in_spec = pl.BlockSpec((r, cols), lambda i, me_ref: (me_ref[0], 0))
        out_spec = pl.BlockSpec((1, r, cols), lambda i, me_ref: (me_ref[0], 0, 0))
        out_shape = jax.ShapeDtypeStruct((N_DEV, r, cols), src.dtype)

    def body(me_ref, s_ref, o_ref):
        o_ref[...] = s_ref[...].reshape(o_ref.shape)

    return pl.pallas_call(
        body, name=name, out_shape=out_shape,
        grid_spec=pltpu.PrefetchScalarGridSpec(num_scalar_prefetch=1, grid=(1,), in_specs=[in_spec], out_specs=out_spec),
        compiler_params=_params(("arbitrary",)),
    )(me.reshape(1).astype(jnp.int32), src)


def _exchange_start(mode, srcs, lands, name):
    n_s, n_a = len(srcs), len(srcs) + len(lands)
    n_cp = _COPIES_PER_ARRAY.get(mode, N_DEV - 1) * len(lands)

    def body(*refs):
        for cp in _peer_copies(mode, refs[:n_s], refs[n_s:n_a], refs[n_a], refs[n_a + 1]):
            cp.start()
        refs[-1][...] = jnp.zeros_like(refs[-1])

    hbm, sem = pl.BlockSpec(memory_space=pltpu.HBM), pl.BlockSpec(memory_space=pltpu.SEMAPHORE)
    arrays = list(srcs) + list(lands)
    out = pl.pallas_call(
        body, name=name,
        out_shape=(pltpu.SemaphoreType.DMA((n_cp,)), pltpu.SemaphoreType.DMA((n_cp,)),
                   *[pltpu.HBM(a.shape, a.dtype) for a in arrays], jax.ShapeDtypeStruct((8, LANES), F32)),
        in_specs=[hbm] * n_a, out_specs=(sem, sem, *[hbm] * n_a, pl.BlockSpec(memory_space=pltpu.VMEM)),
        input_output_aliases={i: 2 + i for i in range(n_a)},
        compiler_params=pltpu.CompilerParams(has_side_effects=pltpu.SideEffectType.DATAFLOW_SIDE_EFFECTING),
    )(*[pltpu.with_memory_space_constraint(a, pltpu.HBM) for a in arrays])
    return out[0], out[1], out[2:2 + n_s], out[2 + n_s:2 + n_a], out[-1]


_COPIES_PER_ARRAY = {"gather_ici": 4, "gather_d2d": 3}


def _exchange_wait(mode, send_sems, recv_sems, srcs, lands, after, name):
    n_s, n_a = len(srcs), len(srcs) + len(lands)

    def body(*refs):
        copies = _peer_copies(mode, refs[:n_s], refs[n_s:n_a], refs[n_a], refs[n_a + 1])
        for cp in copies:
            cp.wait_send()
        for cp in copies:
            cp.wait_recv()

    hbm, sem = pl.BlockSpec(memory_space=pltpu.HBM), pl.BlockSpec(memory_space=pltpu.SEMAPHORE)
    arrays = list(srcs) + list(lands)
    out = pl.pallas_call(
        body, name=name, out_shape=tuple(pltpu.HBM(a.shape, a.dtype) for a in arrays),
        in_specs=[hbm] * n_a + [sem, sem, pl.BlockSpec(memory_space=pl.ANY)], out_specs=tuple([hbm] * n_a),
        input_output_aliases={i: i for i in range(n_a)},
        compiler_params=pltpu.CompilerParams(has_side_effects=pltpu.SideEffectType.DATAFLOW_SIDE_EFFECTING),
    )(*arrays, send_sems, recv_sems, after)
    return out[n_s:]


SMALL_WEIGHTS = ("b_ada", "g_pre_mix", "g_post_mix", "g_pre_ffn", "g_post_ffn", "w_pool", "b_pool", "pool_scale", "conv_b")


MOD_ROWS = ((0, 0), (0, 1), (1, 3), (1, 0), (1, 1), (2, 0))


def _small_sum(mine, gathered):
    n_l = len(mine)
    d = mine[0].shape[1]

    def body(*refs):
        loc, got = refs[:n_l], refs[n_l:2 * n_l]
        tot_refs, dmod_ref = refs[2 * n_l:3 * n_l], refs[3 * n_l]
        me = _index(_place())
        part = lambda a, dev: jnp.where(dev == me, loc[a][...], got[a][dev])
        for a in range(n_l):
            tot = part(a, 0)
            for dev in range(1, N_DEV):
                tot = tot + part(a, dev)
            tot_refs[a][...] = tot
        for dev in range(N_DEV):
            for k, (a, r) in enumerate(MOD_ROWS):
                dmod_ref[dev:dev + 1, k * d:(k + 1) * d] = part(a, dev)[r:r + 1, :]

    vmem = pl.BlockSpec(memory_space=pltpu.VMEM)
    out = pl.pallas_call(
        body, name="small_sum", in_specs=[vmem] * (2 * n_l), out_specs=[vmem] * (n_l + 1),
        out_shape=[jax.ShapeDtypeStruct(a.shape, F32) for a in mine] + [jax.ShapeDtypeStruct((N_DEV, 6 * d), F32)],
        compiler_params=_params(),
    )(*mine, *gathered)
    return out[:n_l], out[n_l]


def _small_adam(totals, weights, moms, vels):
    n_t, n_w = len(totals), len(weights)

    def body(*refs):
        t_in, t_mix, t_ffn, t_pool, t_blk, t_conv, _ = (r[...] for r in refs[:n_t])
        w_refs, m_refs, v_refs = (refs[n_t + k * n_w:n_t + (k + 1) * n_w] for k in range(3))
        outs = refs[n_t + 3 * n_w:]

        def update(idx, g, at=()):
            sel = lambda ref: ref.at[at] if at else ref
            delta, nm, nv = _adam_math(sel(w_refs[idx])[...], g, sel(m_refs[idx])[...], sel(v_refs[idx])[...])
            for k, val in enumerate((g, delta, nm, nv)):
                sel(outs[4 * idx + k])[...] = val

        tots = (t_in, t_mix, t_ffn)
        update(0, jnp.concatenate([tots[a][r:r + 1] for a, r in MOD_ROWS], axis=1))
        update(1, t_in[2:3])
        update(2, t_mix[4:5])
        update(3, t_mix[2:3])
        update(4, t_ffn[1:2])
        for gi in range(len(POOL_WINDOWS)):
            update(5, t_blk[gi], at=(0, gi))
        update(6, jnp.concatenate([t_pool[0:1, gi * HEAD_DIM:(gi + 1) * HEAD_DIM] for gi in range(len(POOL_WINDOWS))], axis=0),
               at=(0,))
        update(7, t_pool[1:2])
        update(8, t_conv[3:4])

    vmem = pl.BlockSpec(memory_space=pltpu.VMEM)
    return pl.pallas_call(
        body, name="small_adam", in_specs=[vmem] * (n_t + 3 * n_w), out_specs=[vmem] * (4 * n_w),
        out_shape=[jax.ShapeDtypeStruct(w.shape, F32) for w in weights for _ in range(4)],
        compiler_params=_params(),
    )(*totals, *weights, *moms, *vels)


def _adam_math(w, g, m, v):
    m = ADAM_B1 * m + (1.0 - ADAM_B1) * g
    v = ADAM_B2 * v + (1.0 - ADAM_B2) * (g * g)
    m_hat = m / (1.0 - ADAM_B1 ** ADAM_STEP)
    v_hat = v / (1.0 - ADAM_B2 ** ADAM_STEP)
    delta = -ADAM_LR * (m_hat / (jnp.sqrt(v_hat) + ADAM_EPS) + ADAM_WD * w)
    return delta, m, v


def _adam(w, g, m, v, name, tr):
    rows, cols = w.shape

    def body(w_ref, g_ref, m_ref, v_ref, d_ref, nm_ref, nv_ref):
        d_ref[...], nm_ref[...], nv_ref[...] = _adam_math(w_ref[...], g_ref[...], m_ref[...], v_ref[...])

    spec = pl.BlockSpec((tr, cols), lambda i: (i, 0))
    shape = jax.ShapeDtypeStruct((rows, cols), F32)
    return pl.pallas_call(
        body, name=name, grid=(rows // tr,), in_specs=[spec] * 4, out_specs=[spec] * 3,
        out_shape=[shape] * 3, compiler_params=_params(("arbitrary",)),
    )(w, g, m, v)


def _sum_adam(parts, w, m, v, name, tr):
    _, rows, cols = parts.shape

    def body(p_ref, w_ref, m_ref, v_ref, g_ref, d_ref, nm_ref, nv_ref):
        g = p_ref[0].astype(F32)
        for dev in range(1, N_DEV):
            g = g + p_ref[dev].astype(F32)
        g_ref[...] = g
        d_ref[...], nm_ref[...], nv_ref[...] = _adam_math(w_ref[...], g, m_ref[...], v_ref[...])

    spec = pl.BlockSpec((tr, cols), lambda i: (i, 0))
    shape = jax.ShapeDtypeStruct((rows, cols), F32)
    return pl.pallas_call(
        body, name=name, grid=(rows // tr,),
        in_specs=[pl.BlockSpec((N_DEV, tr, cols), lambda i: (0, i, 0)), spec, spec, spec],
        out_specs=[spec] * 4, out_shape=[shape] * 4, compiler_params=_params(("arbitrary",)),
    )(parts, w, m, v)


def _ada_grad_adam(c_all, dmod_cols, w, m, v, tr):
    rows, cols = w.shape

    def body(c_ref, dm_ref, w_ref, m_ref, v_ref, g_ref, d_ref, nm_ref, nv_ref):
        cv = c_ref[...]
        act = cv * jax.nn.sigmoid(cv)
        g = lax.dot_general(act, dm_ref[...], TN, preferred_element_type=F32, precision=lax.Precision.HIGHEST)
        g_ref[...] = g
        d_ref[...], nm_ref[...], nv_ref[...] = _adam_math(w_ref[...], g, m_ref[...], v_ref[...])

    spec = pl.BlockSpec((tr, cols), lambda i: (i, 0))
    shape = jax.ShapeDtypeStruct((rows, cols), F32)
    return pl.pallas_call(
        body, name="ada_grad_adam", grid=(rows // tr,),
        in_specs=[pl.BlockSpec((N_DEV, tr), lambda i: (0, i)), pl.BlockSpec((N_DEV, cols), lambda i: (0, 0)), spec, spec, spec],
        out_specs=[spec] * 4, out_shape=[shape] * 4, compiler_params=_params(("arbitrary",)),
    )(c_all, dmod_cols, w, m, v)


def _rope_tables(positions):
    s_len = positions.shape[0]
    inv_freq = ROPE_THETA ** (-jnp.arange(0, 2 * ROT_HALF, 2, dtype=F32) / (2 * ROT_HALF))
    ang = positions.astype(F32)[:, None] * inv_freq
    cos, sin = jnp.cos(ang), jnp.sin(ang)
    rest = HEAD_DIM - 2 * ROT_HALF
    zero = lambda n: jnp.zeros((s_len, n), F32)
    head = jnp.stack([jnp.concatenate([cos, cos, jnp.ones((s_len, rest), F32)], axis=1),
                      jnp.concatenate([-sin, zero(HEAD_DIM - ROT_HALF)], axis=1),
                      jnp.concatenate([zero(ROT_HALF), sin, zero(rest)], axis=1)])
    return jnp.tile(head, (1, 1, LANES // HEAD_DIM))


def _pad_rows(a, rows):
    return jnp.pad(a, ((0, rows - a.shape[0]), (0, 0)))


def _as_rows(a, rows):
    flat = a.reshape(-1)
    return jnp.pad(flat, (0, rows * LANES - flat.shape[0])).reshape(rows, LANES)


def _sequence_step(xs, target, rope, mods, gains, w_in_t, w_out_t, relay_ffn, fetch_ffn, send_ffn_grads, send_mix_grads, w_blk_b, b_pool_r,
                   pool_scale_r, conv_w_all, conv_b):
    sh_m, sc_m, gt_m, sh_f, sc_f, gt_f = mods
    g_pre_mix, g_post_mix, g_pre_ffn, g_post_ffn = gains
    h1, u_pool, qkv = _premix_inproj(xs, sh_m, sc_m, g_pre_mix, w_in_t, rope, tm=512)
    o_g, lse_g = [], []
    for gi, dil in enumerate(DILATIONS):
        o, lse = _attn_fwd(qkv, gi, dil)
        o_g.append(o)
        lse_g.append(lse)
    token = relay_ffn(lse_g[-1])
    x1, y1, h2, cat, attn, lse_all = _mix_out(xs, u_pool, o_g, lse_g, w_blk_b, b_pool_r, pool_scale_r, w_out_t,
                                              gt_m if token is None else gt_m + token[0:1, 0:1],
                                              g_post_mix, g_pre_ffn, sc_f, sh_f, tm=256)
    w_up_t, w_down_f = fetch_ffn(x1)
    gate, a_ffn, act, vd, dy2, dout, sums_ffn, loss_loc = _ffn_fwd_loss(h2, x1, target, w_up_t, w_down_f, conv_w_all, conv_b,
                                                              gt_f, g_post_ffn, tm=256, tf=2816, ck=256)

    dgc, dval, dw_down, dconv = _ffn_bwd_act(dy2, gate, a_ffn, act, vd, w_down_f, tm=256, ck=256)
    dup, dh2 = _ffn_bwd_up(dgc, dval, w_up_t, conv_w_all, tm=256)
    dw_up_t = _wgrad(dup, h2, "wgrad_up", tk=1024, tmm=1408)
    token = send_ffn_grads(dw_up_t, dw_down)
    if token is not None:
        sc_f = sc_f + token[0:1, 0:1]
    dx1, dpool, dattn, delta, dw_out_t, sums_mix = _mix_bwd(dh2, dout, x1, y1, cat, attn, w_out_t, sc_f, g_pre_ffn,
                                                           gt_m, g_post_mix, tm=256)
    du, dw_blk, sums_pool = _pool_bwd(dpool, u_pool, w_blk_b, b_pool_r, pool_scale_r, tm=512)
    dqkv = []
    for gi, dil in enumerate(DILATIONS):
        dqkv += list(_attn_bwd(qkv, dattn, lse_all, delta, gi, dil))
    dproj = _dproj_assemble(du, dqkv, rope, tm=512)
    dw_in_t = _wgrad(dproj, h1, "wgrad_in", tk=1024, tmm=1280)
    token = send_mix_grads(dw_in_t, dw_out_t)
    if token is not None:
        sc_m = sc_m + token[0:1, 0:1]
    grad_x, sums_in = _inproj_bwd(dproj, w_in_t, xs, dx1, sc_m, g_pre_mix, tm=256)
    return (loss_loc, grad_x, dw_in_t, dw_out_t, dw_up_t, dw_down, dw_blk, dconv,
            sums_in, sums_mix, sums_ffn, sums_pool)


def kernel(x, c, positions, w_ada, b_ada, g_pre_mix, g_post_mix, g_pre_ffn, g_post_ffn, w_in, w_pool, b_pool, pool_scale, w_out, w_up, conv_w, conv_b, w_down, loss_target, m_w_ada, m_b_ada, m_g_pre_mix, m_g_post_mix, m_g_pre_ffn, m_g_post_ffn, m_w_in, m_w_pool, m_b_pool, m_pool_scale, m_w_out, m_w_up, m_conv_w, m_conv_b, m_w_down, v_w_ada, v_b_ada, v_g_pre_mix, v_g_post_mix, v_g_pre_ffn, v_g_post_ffn, v_w_in, v_w_pool, v_b_pool, v_pool_scale, v_w_out, v_w_up, v_conv_w, v_conv_b, v_w_down):
    s_len, d = x.shape[1], x.shape[2]
    d_ff = w_down.shape[1] * N_DEV
    me = _index(_place())
    xs, target = x[0], loss_target[0]

    ncol = w_ada.shape[2]
    b_cols = lax.dynamic_slice(b_ada, (0, me * ncol), (1, ncol))
    c_all, mod, taps_all = _ada_exchange(jnp.broadcast_to(c, (8, d)), w_ada[0], b_cols, _pad_rows(conv_w[0], 8))
    c_all = c_all[:, 0, :]
    conv_w_all = jnp.transpose(taps_all[:, :3, :], (1, 0, 2)).reshape(3, d_ff)
    sh_m, sc_m, gt_m, sh_f, sc_f, gt_f = [mod[:, 0, :].reshape(1, -1)[:, k * d:(k + 1) * d] for k in range(6)]

    w_in_t, w_out_t = _gather_weights([w_in[0].T.astype(BF16), w_out[0].T.astype(BF16)])

    rope = _rope_tables(positions[0])
    w_blk = jnp.zeros((256, 256), F32)
    for gi in range(4):
        w_blk = lax.dynamic_update_slice(w_blk, w_pool[0, gi], (gi * HEAD_DIM, gi * HEAD_DIM))
    w_blk_b = w_blk.astype(BF16)
    b_pool_r, pool_scale_r = b_pool.reshape(1, 256), pool_scale.reshape(1, 256)

    up_sh, down_sh = w_up[0].T.astype(BF16), w_down[0].astype(BF16)
    w_in_t, conv_w_all, up_sh, down_sh = lax.optimization_barrier((w_in_t, conv_w_all, up_sh, down_sh))
    lands = [_landing_zone("gather", s, me, "land_" + nm) for s, nm in ((up_sh, "w_up"), (down_sh, "w_down"))]
    w_send, w_recv, w_src, w_land, w_token = _exchange_start("gather_ici", [up_sh, down_sh], lands, "ffn_weights_ici_start")
    relay = []

    def relay_ffn(after):
        arrived = _exchange_wait("gather_ici", w_send, w_recv, w_src, w_land, after, "ffn_weights_ici_wait")
        relay.extend(_exchange_start("gather_d2d", [], arrived, "ffn_weights_d2d_start"))
        return relay[4]

    def fetch_ffn(after):
        return _exchange_wait("gather_d2d", relay[0], relay[1], [], relay[3], after, "ffn_weights_d2d_wait")

    flight = []

    def send_ffn_grads(dw_up_t, dw_down):
        lands = [_landing_zone("scatter", dw_up_t, me, "land_dw_up"), _landing_zone("scatter", dw_down, me, "land_dw_down")]
        flight.extend(_exchange_start("scatter", [dw_up_t, dw_down], lands, "ffn_grads_start"))
        return flight[4]

    mix_flight = []

    def send_mix_grads(dw_in_t, dw_out_t):
        lands = [_landing_zone("scatter", dw_in_t, me, "land_dw_in"), _landing_zone("scatter", dw_out_t, me, "land_dw_out")]
        mix_flight.extend(_exchange_start("scatter", [dw_in_t, dw_out_t], lands, "mix_grads_start"))
        return mix_flight[4]

    (loss_loc, grad_x, dw_in_t, dw_out_t, _, _, dw_pool, dconv,
     sums_in, sums_mix, sums_ffn, sums_pool) = _sequence_step(
        xs, target, rope, (sh_m + w_token[0:1, 0:1], sc_m, gt_m, sh_f, sc_f, gt_f),
        (g_pre_mix, g_post_mix, g_pre_ffn, g_post_ffn),
        w_in_t, w_out_t, relay_ffn, fetch_ffn, send_ffn_grads, send_mix_grads, w_blk_b, b_pool_r, pool_scale_r, conv_w_all, conv_b)

    small = [sums_in, sums_mix, sums_ffn, sums_pool, dw_pool, dconv, loss_loc]
    small_flight = _exchange_start("allgather", small, [lax.empty((N_DEV,) + a.shape, F32) for a in small], "small_start")

    parts_ffn = _exchange_wait("scatter", *flight[:4], small_flight[4], "ffn_grads_wait")
    big = {
        "w_up": [a.T for a in _sum_adam(parts_ffn[0], w_up[0].T, m_w_up[0].T, v_w_up[0].T, "adam_w_up", 64)],
        "w_down": _sum_adam(parts_ffn[1], w_down[0], m_w_down[0], v_w_down[0], "adam_w_down", 32),
    }

    rep_w = [b_ada, g_pre_mix, g_post_mix, g_pre_ffn, g_post_ffn, w_pool, b_pool, pool_scale, conv_b]
    rep_m = [m_b_ada, m_g_pre_mix, m_g_post_mix, m_g_pre_ffn, m_g_post_ffn, m_w_pool, m_b_pool, m_pool_scale, m_conv_b]
    rep_v = [v_b_ada, v_g_pre_mix, v_g_post_mix, v_g_pre_ffn, v_g_post_ffn, v_w_pool, v_b_pool, v_pool_scale, v_conv_b]
    gathered = _exchange_wait("allgather", *small_flight[:4], big["w_down"][0], "small_wait")
    totals, dmod_all = _small_sum(small, gathered)
    dconv_tot, loss_tot = totals[5], totals[6]
    rep_out = _small_adam(totals, rep_w, rep_m, rep_v)
    g_rep, d_rep, nm_rep, nv_rep = (rep_out[k::4] for k in range(4))

    fcol = d_ff // N_DEV
    g_cw = lax.dynamic_slice(dconv_tot, (0, me * fcol), (3, fcol))
    d_cw, nm_cw, nv_cw = _adam(conv_w[0], g_cw, m_conv_w[0], v_conv_w[0], "adam_conv_w", 3)

    dmod_cols = lax.dynamic_slice(dmod_all, (0, me * ncol), (N_DEV, ncol))
    g_ada, d_ada, nm_ada, nv_ada = _ada_grad_adam(c_all, dmod_cols, w_ada[0], m_w_ada[0], v_w_ada[0], 256)

    parts_mix = _exchange_wait("scatter", *mix_flight[:4], g_ada, "mix_grads_wait")
    big["w_in"] = [a.T for a in _sum_adam(parts_mix[0], w_in[0].T, m_w_in[0].T, v_w_in[0].T, "adam_w_in", 64)]
    big["w_out"] = [a.T for a in _sum_adam(parts_mix[1], w_out[0].T, m_w_out[0].T, v_w_out[0].T, "adam_w_out", 128)]

    loss = loss_tot[0, 0]

    def group(k):
        rep = (g_rep, d_rep, nm_rep, nv_rep)[k]
        ada = (g_ada, d_ada, nm_ada, nv_ada)[k][None]
        cw = (g_cw, d_cw, nm_cw, nv_cw)[k][None]
        return [ada, rep[0], rep[1], rep[2], rep[3], rep[4], big["w_in"][k][None], rep[5], rep[6], rep[7],
                big["w_out"][k][None], big["w_up"][k][None], cw, rep[8], big["w_down"][k][None]]

    return (loss, grad_x[None], *group(0), *group(1), *group(2), *group(3))
```

```python
import functools
import math

import jax
import jax.numpy as jnp
from jax import lax
from jax.experimental import pallas as pl
from jax.experimental.pallas import tpu as pltpu

F32 = jnp.float32
BF16 = jnp.bfloat16
MESH = pl.DeviceIdType.MESH

N_DEV = 8
HEAD_DIM = 64
ROT_HALF = 8
ROPE_THETA = 500000.0
POOL_WINDOWS = (2, 4, 8, 16)
DILATIONS = (1, 4, 16)
BLOCK = 128
NORM_EPS = 1e-6
HALO = 16
MASKED = -1e30
ATTN_FWD_UNROLL = 4
ATTN_BWD_UNROLL = 2

ADAM_LR = 0.001
ADAM_B1 = 0.9
ADAM_B2 = 0.999
ADAM_EPS = 1e-08
ADAM_WD = 0.01
ADAM_STEP = 10

V7X_VMEM_LIMIT = 56 * 1024 * 1024
LANES = 128

NT = (((1,), (1,)), ((), ()))
NN = (((1,), (0,)), ((), ()))
TN = (((0,), (0,)), ((), ()))


def _dot(a, b, dims):
    return lax.dot_general(a, b, dims, preferred_element_type=F32)


def _params(sem=None, vmem=V7X_VMEM_LIMIT):
    if sem is None:
        return pltpu.CompilerParams(vmem_limit_bytes=vmem)
    return pltpu.CompilerParams(dimension_semantics=sem, vmem_limit_bytes=vmem)


def _rstd(v):
    return lax.rsqrt(jnp.mean(v * v, axis=-1, keepdims=True) + NORM_EPS)


def _norm_bwd(dn, n, rstd):
    return rstd * (dn - n * jnp.mean(dn * n, axis=-1, keepdims=True))


def _rope_fwd(p, rope_ref):
    return p * rope_ref[0] + pltpu.roll(p, LANES - ROT_HALF, 1) * rope_ref[1] + pltpu.roll(p, ROT_HALF, 1) * rope_ref[2]


def _rope_bwd(dp, rope_ref):
    return dp * rope_ref[0] + pltpu.roll(dp * rope_ref[1], ROT_HALF, 1) + pltpu.roll(dp * rope_ref[2], LANES - ROT_HALF, 1)


def _gelu_parts(v):
    k2 = 2.0 * math.sqrt(2.0 / math.pi)
    c = 0.044715
    v2 = v * v
    s = jax.nn.sigmoid(v * (k2 + (k2 * c) * v2))
    g = v * s
    dg = s + g * (1.0 - s) * (k2 + (3.0 * k2 * c) * v2)
    return g, dg


def _halo_before(i, tile):
    return jnp.maximum(i * (tile // HALO) - 1, 0)


def _premix_inproj(x, sh, sc, g, w_in_t, rope, tm):
    s_len, d = x.shape
    n_proj = w_in_t.shape[0]
    n_slab = (n_proj - 256) // LANES

    def body(x_ref, sh_ref, sc_ref, g_ref, w_ref, rope_ref, h_ref, up_ref, qkv_ref):
        xv = x_ref[...]
        h = (xv * _rstd(xv) * g_ref[...]) * (1.0 + sc_ref[...]) + sh_ref[...]
        hb = h.astype(BF16)
        h_ref[...] = hb
        up_ref[...] = _dot(hb, w_ref[0:256, :], NT)
        for pair in range(n_slab // 2):
            p = _dot(hb, w_ref[256 + 256 * pair:512 + 256 * pair, :], NT)
            for half in range(2):
                ph = p[:, half * LANES:(half + 1) * LANES]
                if pair < 6:
                    ph = _rope_fwd(ph, rope_ref)
                if pair < 3:
                    ph = ph * (HEAD_DIM ** -0.5)
                qkv_ref[2 * pair + half] = ph

    vec = pl.BlockSpec((1, d), lambda i: (0, 0))
    return pl.pallas_call(
        body, name="premix_inproj", grid=(s_len // tm,),
        in_specs=[pl.BlockSpec((tm, d), lambda i: (i, 0)), vec, vec, vec,
                  pl.BlockSpec((n_proj, d), lambda i: (0, 0)),
                  pl.BlockSpec((3, tm, LANES), lambda i: (0, i, 0))],
        out_specs=[pl.BlockSpec((tm, d), lambda i: (i, 0)),
                   pl.BlockSpec((tm, 256), lambda i: (i, 0)),
                   pl.BlockSpec((n_slab, tm, LANES), lambda i: (0, i, 0))],
        out_shape=[jax.ShapeDtypeStruct((s_len, d), BF16),
                   jax.ShapeDtypeStruct((s_len, 256), F32),
                   jax.ShapeDtypeStruct((n_slab, s_len, LANES), F32)],
        compiler_params=_params(("arbitrary",)),
    )(x, sh, sc, g, w_in_t, rope)


def _block_rows(n, r, dil):
    start = n * (BLOCK * dil) + r
    if dil == 1:
        return pl.ds(pl.multiple_of(start, BLOCK), BLOCK)
    return pl.ds(start, BLOCK, stride=dil)


def _band_mask(n):
    ri = lax.broadcasted_iota(jnp.int32, (BLOCK, 2 * BLOCK), 0)
    cj = lax.broadcasted_iota(jnp.int32, (BLOCK, 2 * BLOCK), 1)
    cur = (cj >= BLOCK) & (cj - BLOCK <= ri)
    prev = (cj < BLOCK) & (cj >= ri) & (n > 0)
    return cur | prev


def _attn_fwd(qkv, group, dil):
    s_len = qkv.shape[1]
    nb = s_len // (BLOCK * dil)

    def body(q_ref, k_ref, v_ref, o_ref, lse_ref):
        lane = lax.broadcasted_iota(jnp.int32, (BLOCK, LANES), 1)
        first = lane < HEAD_DIM

        def block(t, carry):
            r, n = t // nb, t % nb
            cur = _block_rows(n, r, dil)
            prev = _block_rows(jnp.maximum(n - 1, 0), r, dil)
            q = q_ref[0, cur, :]
            kcat = jnp.concatenate([k_ref[0, prev, :], k_ref[0, cur, :]], axis=0).astype(BF16)
            vcat = jnp.concatenate([v_ref[0, prev, :], v_ref[0, cur, :]], axis=0).astype(BF16)
            valid = _band_mask(n)
            q2 = jnp.concatenate([jnp.where(first, q, 0.0), jnp.where(first, 0.0, q)], axis=0).astype(BF16)
            s = jnp.where(jnp.concatenate([valid, valid], axis=0), _dot(q2, kcat, NT), MASKED)
            m = jnp.max(s, axis=-1, keepdims=True)
            p = jnp.exp(s - m)
            den = jnp.sum(p, axis=-1, keepdims=True)
            o2 = _dot(p.astype(BF16), vcat, NN) / den
            lse2 = m + jnp.log(den)
            o_ref[0, cur, :] = jnp.where(first, o2[:BLOCK], o2[BLOCK:])
            lse_ref[0, cur, :] = jnp.where(first, lse2[:BLOCK], lse2[BLOCK:])
            return carry

        lax.fori_loop(0, nb * dil, block, 0, unroll=ATTN_FWD_UNROLL)

    def slab(base):
        return pl.BlockSpec((1, s_len, LANES), lambda s: (base + 2 * group + s, 0, 0))

    out = pl.BlockSpec((1, s_len, LANES), lambda s: (s, 0, 0))
    shape = jax.ShapeDtypeStruct((2, s_len, LANES), F32)
    return pl.pallas_call(
        body, name=f"attn_fwd_d{dil}", grid=(2,),
        in_specs=[slab(0), slab(6), slab(12)], out_specs=[out, out], out_shape=[shape, shape],
        compiler_params=_params(("arbitrary",)),
    )(qkv, qkv, qkv)


def _pool_mixed(u, halo, i, tm):
    ue = jnp.concatenate([halo, u], axis=0)
    s2 = ue + pltpu.roll(ue, 1, 0)
    s4 = s2 + pltpu.roll(s2, 2, 0)
    s8 = s4 + pltpu.roll(s4, 4, 0)
    s16 = s8 + pltpu.roll(s8, 8, 0)
    grp = lax.broadcasted_iota(jnp.int32, (tm, 256), 1) // HEAD_DIM
    pick = lambda a, b, c, e: jnp.where(grp == 0, a, jnp.where(grp == 1, b, jnp.where(grp == 2, c, e)))
    win_sum = pick(s2[HALO:], s4[HALO:], s8[HALO:], s16[HALO:])
    pos = (i * tm + lax.broadcasted_iota(jnp.int32, (tm, 256), 0)).astype(F32)
    count = jnp.minimum(pos + 1.0, pick(*[float(w) for w in POOL_WINDOWS]))
    return win_sum / count - u, count


def _mix_out(x, u_pool, o_g, lse_g, w_blk, b_pool, pool_scale, w_out_t, gt_m, g_post_mix, g_pre_ffn, sc_f, sh_f, tm):
    s_len, d = x.shape

    def body(x_ref, u_ref, uh_ref, o0, o1, o2, l0, l1, l2, wb_ref, bp_ref, ps_ref, wo_ref,
             gt_ref, g1_ref, g2_ref, sc_ref, sh_ref,
             x1_ref, y1_ref, h2_ref, cat_ref, attn_ref, lall_ref):
        i = pl.program_id(0)
        u = u_ref[...]
        halo = uh_ref[...] * (i > 0).astype(F32)
        mixed, _ = _pool_mixed(u, halo, i, tm)
        y = _dot(mixed.astype(BF16), wb_ref[...], NN) + bp_ref[...]
        pool = y * ps_ref[...]
        attn = []
        for s in range(2):
            la, lb, lc = l0[s], l1[s], l2[s]
            mx = jnp.maximum(jnp.maximum(la, lb), lc)
            ea, eb, ec = jnp.exp(la - mx), jnp.exp(lb - mx), jnp.exp(lc - mx)
            den = ea + eb + ec
            lall_ref[s] = mx + jnp.log(den)
            attn.append((ea / den) * o0[s] + (eb / den) * o1[s] + (ec / den) * o2[s])
        attn = jnp.concatenate(attn, axis=1)
        attn_ref[...] = attn
        cat = jnp.concatenate([pool, attn], axis=1).astype(BF16)
        cat_ref[...] = cat
        y1 = _dot(cat, wo_ref[...], NT)
        y1_ref[...] = y1
        x1 = x_ref[...] + gt_ref[...] * (y1 * _rstd(y1) * g1_ref[...])
        x1_ref[...] = x1
        h2 = (x1 * _rstd(x1) * g2_ref[...]) * (1.0 + sc_ref[...]) + sh_ref[...]
        h2_ref[...] = h2.astype(BF16)

    tile = lambda w: pl.BlockSpec((tm, w), lambda i: (i, 0))
    slab = pl.BlockSpec((2, tm, LANES), lambda i: (0, i, 0))
    const = lambda a: pl.BlockSpec(a.shape, lambda i: (0,) * a.ndim)
    return pl.pallas_call(
        body, name="mix_out", grid=(s_len // tm,),
        in_specs=[tile(d), tile(256), pl.BlockSpec((HALO, 256), lambda i: (_halo_before(i, tm), 0)),
                  slab, slab, slab, slab, slab, slab,
                  const(w_blk), const(b_pool), const(pool_scale), const(w_out_t),
                  const(gt_m), const(g_post_mix), const(g_pre_ffn), const(sc_f), const(sh_f)],
        out_specs=[tile(d), tile(d), tile(d), tile(512), tile(256), slab],
        out_shape=[jax.ShapeDtypeStruct((s_len, d), F32), jax.ShapeDtypeStruct((s_len, d), F32),
                   jax.ShapeDtypeStruct((s_len, d), BF16), jax.ShapeDtypeStruct((s_len, 512), BF16),
                   jax.ShapeDtypeStruct((s_len, 256), F32), jax.ShapeDtypeStruct((2, s_len, LANES), F32)],
        compiler_params=_params(("arbitrary",)),
    )(x, u_pool, u_pool, *o_g, *lse_g, w_blk, b_pool, pool_scale, w_out_t, gt_m, g_post_mix, g_pre_ffn, sc_f, sh_f)


def _conv_gate(gate_ext, cw, cb):
    gc = gate_ext * cw[2:3, :] + pltpu.roll(gate_ext, 1, 0) * cw[1:2, :] + pltpu.roll(gate_ext, 2, 0) * cw[0:1, :]
    return gc[HALO:] + cb


def _ffn_fwd_loss(h2, x1, target, w_up_t, w_down, conv_w, conv_b, gt_f, g_post_ffn, tm, tf, ck):
    s_len, d = x1.shape
    d_ff = w_down.shape[0]
    n_f = d_ff // tf

    def body(h_ref, hh_ref, x1_ref, tgt_ref, wg_ref, wv_ref, wd_ref, cw_ref, cb_ref, gt_ref, g_ref,
             gate_ref, a_ref, act_ref, vd_ref, dy2_ref, dout_ref, sums_ref, loss_ref, acc_ref):
        i, j = pl.program_id(0), pl.program_id(1)

        @pl.when((i == 0) & (j == 0))
        def _():
            sums_ref[...] = jnp.zeros_like(sums_ref)
            loss_ref[...] = jnp.zeros_like(loss_ref)

        h = h_ref[...]
        h_ext = jnp.concatenate([hh_ref[...], h], axis=0)
        row = lax.broadcasted_iota(jnp.int32, (tm + HALO, ck), 0)
        no_halo = (row < HALO) & (i == 0)

        def up(c):
            cs = slice(c * ck, (c + 1) * ck)
            return jnp.where(no_halo, 0.0, _dot(h_ext, wg_ref[cs, :], NT)), _dot(h, wv_ref[cs, :], NT)

        part = None
        n_c = tf // ck
        nxt = up(0)
        for c in range(n_c):
            cs = slice(c * ck, (c + 1) * ck)
            gate_ext, val = nxt
            if c + 1 < n_c:
                nxt = up(c + 1)
            act, dact = _gelu_parts(_conv_gate(gate_ext, cw_ref[:, cs], cb_ref[:, cs]))
            a = (act * val).astype(BF16)
            gate_ref[:, cs] = gate_ext[HALO:].astype(BF16)
            a_ref[:, cs] = a
            act_ref[:, cs] = act.astype(BF16)
            vd_ref[:, cs] = (val * dact).astype(BF16)
            p = _dot(a, wd_ref[cs, :], NN)
            part = p if part is None else part + p

        @pl.when(j == 0)
        def _():
            acc_ref[...] = part

        @pl.when(j > 0)
        def _():
            acc_ref[...] += part

        @pl.when(j == n_f - 1)
        def _():
            y2 = acc_ref[...]
            rstd = _rstd(y2)
            n = y2 * rstd
            rn = n * g_ref[...]
            err = x1_ref[...] + gt_ref[...] * rn - tgt_ref[...]
            loss_ref[...] += 0.5 * jnp.sum(jnp.mean(err * err, axis=-1, keepdims=True), axis=0, keepdims=True)
            dout = err * (1.0 / d)
            dout_ref[...] = dout
            drn = dout * gt_ref[...]
            sums_ref[0:1, :] += jnp.sum(dout * rn, axis=0, keepdims=True)
            sums_ref[1:2, :] += jnp.sum(drn * n, axis=0, keepdims=True)
            dy2_ref[...] = _norm_bwd(drn * g_ref[...], n, rstd).astype(BF16)

    tok = lambda w: pl.BlockSpec((tm, w), lambda i, j: (i, 0))
    tokf = pl.BlockSpec((tm, tf), lambda i, j: (i, j))
    vec = pl.BlockSpec((1, d), lambda i, j: (0, 0))
    once = {"pipeline_mode": pl.Buffered(1)} if n_f == 1 else {}
    return pl.pallas_call(
        body, name="ffn_fwd_loss", grid=(s_len // tm, n_f),
        in_specs=[tok(d), pl.BlockSpec((HALO, d), lambda i, j: (_halo_before(i, tm), 0)), tok(d), tok(d),
                  pl.BlockSpec((tf, d), lambda i, j: (j, 0), **once),
                  pl.BlockSpec((tf, d), lambda i, j: (j + n_f, 0), **once),
                  pl.BlockSpec((tf, d), lambda i, j: (j, 0), **once),
                  pl.BlockSpec((3, tf), lambda i, j: (0, j)), pl.BlockSpec((1, tf), lambda i, j: (0, j)), vec, vec],
        out_specs=[tokf, tokf, tokf, tokf, tok(d), tok(d), pl.BlockSpec((8, d), lambda i, j: (0, 0)),
                   pl.BlockSpec((8, LANES), lambda i, j: (0, 0))],
        out_shape=[jax.ShapeDtypeStruct((s_len, d_ff), BF16)] * 4
        + [jax.ShapeDtypeStruct((s_len, d), BF16), jax.ShapeDtypeStruct((s_len, d), F32),
                   jax.ShapeDtypeStruct((8, d), F32), jax.ShapeDtypeStruct((8, LANES), F32)],
        scratch_shapes=[pltpu.VMEM((tm, d), F32)],
        compiler_params=_params(("arbitrary", "arbitrary")),
    )(h2, h2, x1, target, w_up_t, w_up_t, w_down, conv_w, conv_b, gt_f, g_post_ffn)


def _ffn_bwd_act(dy2, gate, a, act, vd, w_down, tm, ck):
    s_len, d = dy2.shape
    d_ff = w_down.shape[0]
    n_t, n_c = s_len // tm, d_ff // ck

    def body(dy_ref, g_ref, gh_ref, a_ref, act_ref, vd_ref, wd_ref, dgc_ref, dval_ref, dwd_ref, dconv_ref, acc_ref):
        i = pl.program_id(0)

        @pl.when(i == 0)
        def _():
            acc_ref[...] = jnp.zeros_like(acc_ref)
            dconv_ref[...] = jnp.zeros_like(dconv_ref)

        dy = dy_ref[...]
        row = lax.broadcasted_iota(jnp.int32, (tm + HALO, ck), 0)
        no_halo = (row < HALO) & (i == 0)

        def down(c):
            return _dot(dy, wd_ref[c * ck:(c + 1) * ck, :], NT)

        nxt = down(0)
        for c in range(n_c):
            cs = slice(c * ck, (c + 1) * ck)
            da = nxt
            if c + 1 < n_c:
                nxt = down(c + 1)
            acc_ref[cs, :] += _dot(a_ref[:, cs], dy, TN)
            gate_ext = jnp.where(no_halo, 0.0, jnp.concatenate([gh_ref[:, cs], g_ref[:, cs]], axis=0).astype(F32))
            dgc = da * vd_ref[:, cs].astype(F32)
            dgc_ref[:, cs] = dgc.astype(BF16)
            dval_ref[:, cs] = (da * act_ref[:, cs].astype(F32)).astype(BF16)
            rows = [jnp.sum(dgc * pltpu.roll(gate_ext, 2 - k, 0)[HALO:], axis=0, keepdims=True) for k in range(2)]
            rows += [jnp.sum(dgc * gate_ext[HALO:], axis=0, keepdims=True), jnp.sum(dgc, axis=0, keepdims=True),
                     jnp.zeros((4, ck), F32)]
            dconv_ref[:, cs] += jnp.concatenate(rows, axis=0)

        @pl.when(i == n_t - 1)
        def _():
            dwd_ref[...] = acc_ref[...].astype(BF16)

    tokf = pl.BlockSpec((tm, d_ff), lambda i: (i, 0))
    return pl.pallas_call(
        body, name="ffn_bwd_act", grid=(n_t,),
        in_specs=[pl.BlockSpec((tm, d), lambda i: (i, 0)), tokf,
                  pl.BlockSpec((HALO, d_ff), lambda i: (_halo_before(i, tm), 0)), tokf, tokf, tokf,
                  pl.BlockSpec((d_ff, d), lambda i: (0, 0), pipeline_mode=pl.Buffered(1))],
        out_specs=[tokf, tokf, pl.BlockSpec((d_ff, d), lambda i: (0, 0)), pl.BlockSpec((8, d_ff), lambda i: (0, 0))],
        out_shape=[jax.ShapeDtypeStruct((s_len, d_ff), BF16), jax.ShapeDtypeStruct((s_len, d_ff), BF16),
                   jax.ShapeDtypeStruct((d_ff, d), BF16), jax.ShapeDtypeStruct((8, d_ff), F32)],
        scratch_shapes=[pltpu.VMEM((d_ff, d), F32)],
        compiler_params=_params(("arbitrary",)),
    )(dy2, gate, gate, a, act, vd, w_down)


def _ffn_bwd_up(dgc, dval, w_up_t, conv_w, tm):
    s_len, d_ff = dgc.shape
    d = w_up_t.shape[1]
    n_t = s_len // tm

    def body(dg_ref, dgn_ref, dv_ref, cw_ref, w_ref, dup_ref, dh_ref):
        i = pl.program_id(0)
        nxt = dgn_ref[...].astype(F32) * (i < n_t - 1).astype(F32)
        ext = jnp.concatenate([dg_ref[...].astype(F32), nxt], axis=0)
        rows = tm + HALO
        dgate = (ext * cw_ref[2:3, :] + pltpu.roll(ext, rows - 1, 0) * cw_ref[1:2, :]
                 + pltpu.roll(ext, rows - 2, 0) * cw_ref[0:1, :])[:tm]
        dup = jnp.concatenate([dgate.astype(BF16), dv_ref[...]], axis=1)
        dup_ref[...] = dup
        dh_ref[...] = _dot(dup, w_ref[...], NN)

    tokf = pl.BlockSpec((tm, d_ff), lambda i: (i, 0))
    return pl.pallas_call(
        body, name="ffn_bwd_up", grid=(n_t,),
        in_specs=[tokf, pl.BlockSpec((HALO, d_ff), lambda i: (jnp.minimum((i + 1) * (tm // HALO), s_len // HALO - 1), 0)),
                  tokf, pl.BlockSpec((3, d_ff), lambda i: (0, 0)), pl.BlockSpec((2 * d_ff, d), lambda i: (0, 0))],
        out_specs=[pl.BlockSpec((tm, 2 * d_ff), lambda i: (i, 0)), pl.BlockSpec((tm, d), lambda i: (i, 0))],
        out_shape=[jax.ShapeDtypeStruct((s_len, 2 * d_ff), BF16), jax.ShapeDtypeStruct((s_len, d), F32)],
        compiler_params=_params(("arbitrary",)),
    )(dgc, dgc, dval, conv_w, w_up_t)


def _mix_bwd(dh2, dout, x1, y1, cat, attn, w_out_t, sc_f, g_pre_ffn, gt_m, g_post_mix, tm):
    s_len, d = x1.shape
    n_t = s_len // tm

    def body(dh_ref, do_ref, x1_ref, y1_ref, cat_ref, at_ref, wo_ref, sc_ref, g2_ref, gt_ref, g1_ref,
             dx1_ref, dpool_ref, dattn_ref, delta_ref, dwo_ref, sums_ref, acc_ref):
        i = pl.program_id(0)
        dh = dh_ref[...]
        x1 = x1_ref[...]
        r2 = _rstd(x1)
        n2 = x1 * r2
        ng = n2 * g2_ref[...]
        dng = dh * (1.0 + sc_ref[...])
        dx1 = do_ref[...] + _norm_bwd(dng * g2_ref[...], n2, r2)
        dx1_ref[...] = dx1
        y1 = y1_ref[...]
        r1 = _rstd(y1)
        n1 = y1 * r1
        drn = dx1 * gt_ref[...]
        dy1 = _norm_bwd(drn * g1_ref[...], n1, r1).astype(BF16)
        dcat = _dot(dy1, wo_ref[...], NN)
        dpool_ref[...] = dcat[:, 0:256]
        lane = lax.broadcasted_iota(jnp.int32, (tm, LANES), 1)
        first = lane < HEAD_DIM
        for s in range(2):
            da = dcat[:, 256 + s * LANES:256 + (s + 1) * LANES]
            dattn_ref[s] = da
            prod = da * at_ref[:, s * LANES:(s + 1) * LANES]
            tot = jnp.sum(prod, axis=-1, keepdims=True)
            lo = jnp.sum(jnp.where(first, prod, 0.0), axis=-1, keepdims=True)
            delta_ref[s] = jnp.where(first, lo, tot - lo)
        dwo = _dot(dy1, cat_ref[...], TN)
        sums = jnp.concatenate(
            [jnp.sum(dh, axis=0, keepdims=True), jnp.sum(dh * ng, axis=0, keepdims=True),
             jnp.sum(dng * n2, axis=0, keepdims=True), jnp.sum(dx1 * (n1 * g1_ref[...]), axis=0, keepdims=True),
             jnp.sum(drn * n1, axis=0, keepdims=True), jnp.zeros((3, d), F32)], axis=0)

        @pl.when(i == 0)
        def _():
            acc_ref[...] = dwo
            sums_ref[...] = sums

        @pl.when(i > 0)
        def _():
            acc_ref[...] += dwo
            sums_ref[...] += sums

        @pl.when(i == n_t - 1)
        def _():
            dwo_ref[...] = acc_ref[...].astype(BF16)

    tile = lambda w: pl.BlockSpec((tm, w), lambda i: (i, 0))
    slab = pl.BlockSpec((2, tm, LANES), lambda i: (0, i, 0))
    vec = pl.BlockSpec((1, d), lambda i: (0, 0))
    return pl.pallas_call(
        body, name="mix_bwd", grid=(n_t,),
        in_specs=[tile(d), tile(d), tile(d), tile(d), tile(512), tile(256),
                  pl.BlockSpec((d, 512), lambda i: (0, 0)), vec, vec, vec, vec],
        out_specs=[tile(d), tile(256), slab, slab, pl.BlockSpec((d, 512), lambda i: (0, 0)),
                   pl.BlockSpec((8, d), lambda i: (0, 0))],
        out_shape=[jax.ShapeDtypeStruct((s_len, d), F32), jax.ShapeDtypeStruct((s_len, 256), F32),
                   jax.ShapeDtypeStruct((2, s_len, LANES), F32), jax.ShapeDtypeStruct((2, s_len, LANES), F32),
                   jax.ShapeDtypeStruct((d, 512), BF16), jax.ShapeDtypeStruct((8, d), F32)],
        scratch_shapes=[pltpu.VMEM((d, 512), F32)],
        compiler_params=_params(("arbitrary",)),
    )(dh2, dout, x1, y1, cat, attn, w_out_t, sc_f, g_pre_ffn, gt_m, g_post_mix)


def _pool_bwd(dpool, u_pool, w_blk, b_pool, pool_scale, tm):
    s_len = dpool.shape[0]
    n_t = s_len // tm

    def body(dp_ref, dpn_ref, u_ref, uh_ref, wb_ref, bp_ref, ps_ref, du_ref, dwp_ref, sums_ref, acc_ref):
        i = pl.program_id(0)
        u = u_ref[...]
        mixed, _ = _pool_mixed(u, uh_ref[...] * (i > 0).astype(F32), i, tm)
        mixed_b = mixed.astype(BF16)
        y = _dot(mixed_b, wb_ref[...], NN) + bp_ref[...]
        dp = dp_ref[...]
        dy = dp * ps_ref[...]
        dwb = _dot(mixed_b, dy.astype(BF16), TN)
        sums = jnp.concatenate([jnp.sum(dy, axis=0, keepdims=True), jnp.sum(dp * y, axis=0, keepdims=True),
                                jnp.zeros((6, 256), F32)], axis=0)
        dp_ext = jnp.concatenate([dp, dpn_ref[...] * (i < n_t - 1).astype(F32)], axis=0)
        dmix = _dot((dp_ext * ps_ref[...]).astype(BF16), wb_ref[...], NT)
        rows = tm + HALO
        grp = lax.broadcasted_iota(jnp.int32, (rows, 256), 1) // HEAD_DIM
        pick = lambda a, b, c, e: jnp.where(grp == 0, a, jnp.where(grp == 1, b, jnp.where(grp == 2, c, e)))
        pos = (i * tm + lax.broadcasted_iota(jnp.int32, (rows, 256), 0)).astype(F32)
        z = dmix / jnp.minimum(pos + 1.0, pick(*[float(w) for w in POOL_WINDOWS]))
        f2 = z + pltpu.roll(z, rows - 1, 0)
        f4 = f2 + pltpu.roll(f2, rows - 2, 0)
        f8 = f4 + pltpu.roll(f4, rows - 4, 0)
        f16 = f8 + pltpu.roll(f8, rows - 8, 0)
        du_ref[...] = (pick(f2, f4, f8, f16) - dmix)[:tm]

        @pl.when(i == 0)
        def _():
            acc_ref[...] = dwb
            sums_ref[...] = sums

        @pl.when(i > 0)
        def _():
            acc_ref[...] += dwb
            sums_ref[...] += sums

        @pl.when(i == n_t - 1)
        def _():
            full = acc_ref[...]
            for gi in range(len(POOL_WINDOWS)):
                lo = gi * HEAD_DIM
                dwp_ref[gi] = full[lo:lo + HEAD_DIM, lo:lo + HEAD_DIM]

    n_g = len(POOL_WINDOWS)
    tile = pl.BlockSpec((tm, 256), lambda i: (i, 0))
    const = lambda a: pl.BlockSpec(a.shape, lambda i: (0,) * a.ndim)
    return pl.pallas_call(
        body, name="pool_bwd", grid=(n_t,),
        in_specs=[tile, pl.BlockSpec((HALO, 256), lambda i: (jnp.minimum((i + 1) * (tm // HALO), s_len // HALO - 1), 0)),
                  tile, pl.BlockSpec((HALO, 256), lambda i: (_halo_before(i, tm), 0)),
                  const(w_blk), const(b_pool), const(pool_scale)],
        out_specs=[tile, pl.BlockSpec((n_g, HEAD_DIM, HEAD_DIM), lambda i: (0, 0, 0)), pl.BlockSpec((8, 256), lambda i: (0, 0))],
        out_shape=[jax.ShapeDtypeStruct((s_len, 256), F32), jax.ShapeDtypeStruct((n_g, HEAD_DIM, HEAD_DIM), F32),
                   jax.ShapeDtypeStruct((8, 256), F32)],
        scratch_shapes=[pltpu.VMEM((256, 256), F32)],
        compiler_params=_params(("arbitrary",)),
    )(dpool, dpool, u_pool, u_pool, w_blk, b_pool, pool_scale)


def _attn_bwd(qkv, dattn, lse_all, delta, group, dil):
    s_len = qkv.shape[1]
    nb = s_len // (BLOCK * dil)

    def body(q_ref, k_ref, v_ref, do_ref, l_ref, dl_ref, dq_ref, dk_ref, dv_ref):
        lane = lax.broadcasted_iota(jnp.int32, (BLOCK, LANES), 1)
        first = lane < HEAD_DIM

        def block(t, carry):
            dk_part, dv_part = carry
            r, n = t // nb, t % nb
            cur = _block_rows(n, r, dil)
            prev = _block_rows(jnp.maximum(n - 1, 0), r, dil)
            q = q_ref[0, cur, :]
            do = do_ref[0, cur, :]
            lse = l_ref[0, cur, :]
            dlt = dl_ref[0, cur, :]
            kcat = jnp.concatenate([k_ref[0, prev, :], k_ref[0, cur, :]], axis=0).astype(BF16)
            vcat = jnp.concatenate([v_ref[0, prev, :], v_ref[0, cur, :]], axis=0).astype(BF16)
            valid = _band_mask(n)
            stack = lambda a: jnp.concatenate([jnp.where(first, a, 0.0), jnp.where(first, 0.0, a)], axis=0)
            rows2 = lambda a: jnp.concatenate([a[:, 0:1], a[:, HEAD_DIM:HEAD_DIM + 1]], axis=0)
            q2, do2 = stack(q).astype(BF16), stack(do).astype(BF16)
            valid2 = jnp.concatenate([valid, valid], axis=0)
            p = jnp.where(valid2, jnp.exp(_dot(q2, kcat, NT) - rows2(lse)), 0.0)
            ds = (p * (_dot(do2, vcat, NT) - rows2(dlt))).astype(BF16)
            dq2 = _dot(ds, kcat, NN)
            dq_ref[0, cur, :] = jnp.where(first, dq2[:BLOCK], dq2[BLOCK:])
            dkc = _dot(ds, q2, TN)
            dvc = _dot(p.astype(BF16), do2, TN)
            dk_ref[0, prev, :] = dk_part + dkc[:BLOCK]
            dv_ref[0, prev, :] = dv_part + dvc[:BLOCK]
            dk_ref[0, cur, :] = dkc[BLOCK:]
            dv_ref[0, cur, :] = dvc[BLOCK:]
            return dkc[BLOCK:], dvc[BLOCK:]

        def blocks(tt, carry):
            for u in range(ATTN_BWD_UNROLL):
                carry = block(tt * ATTN_BWD_UNROLL + u, carry)
            return carry

        zero = jnp.zeros((BLOCK, LANES), F32)
        lax.fori_loop(0, nb * dil // ATTN_BWD_UNROLL, blocks, (zero, zero))

    def slab(base):
        return pl.BlockSpec((1, s_len, LANES), lambda s: (base + 2 * group + s, 0, 0))

    one = pl.BlockSpec((1, s_len, LANES), lambda s: (s, 0, 0))
    shape = jax.ShapeDtypeStruct((2, s_len, LANES), F32)
    return pl.pallas_call(
        body, name=f"attn_bwd_d{dil}", grid=(2,),
        in_specs=[slab(0), slab(6), slab(12), one, one, one],
        out_specs=[one, one, one], out_shape=[shape, shape, shape],
        compiler_params=_params(("arbitrary",)),
    )(qkv, qkv, qkv, dattn, lse_all, delta)


def _dproj_assemble(du, dqkv, rope, tm):
    s_len = du.shape[0]
    n_proj = 256 + 18 * LANES

    def body(du_ref, *refs):
        dref, rope_ref, dproj_ref = refs[:9], refs[9], refs[10]
        dproj_ref[:, 0:256] = du_ref[...].astype(BF16)
        col = 256
        for kind in range(3):
            for grp in range(3):
                for s in range(2):
                    piece = dref[3 * grp + kind][s]
                    if kind < 2:
                        piece = _rope_bwd(piece, rope_ref)
                    if kind == 0:
                        piece = piece * (HEAD_DIM ** -0.5)
                    dproj_ref[:, col:col + LANES] = piece.astype(BF16)
                    col += LANES

    slab = pl.BlockSpec((2, tm, LANES), lambda i: (0, i, 0))
    return pl.pallas_call(
        body, name="dproj_assemble", grid=(s_len // tm,),
        in_specs=[pl.BlockSpec((tm, 256), lambda i: (i, 0))] + [slab] * 9 + [pl.BlockSpec((3, tm, LANES), lambda i: (0, i, 0))],
        out_specs=pl.BlockSpec((tm, n_proj), lambda i: (i, 0)),
        out_shape=jax.ShapeDtypeStruct((s_len, n_proj), BF16),
        compiler_params=_params(("arbitrary",)),
    )(du, *dqkv, rope)


def _inproj_bwd(dproj, w_in_t, x, dx1, sc_m, g_pre_mix, tm):
    s_len, d = x.shape
    n_proj = w_in_t.shape[0]
    n_t = s_len // tm

    def body(dproj_ref, w_ref, x_ref, dx1_ref, sc_ref, g_ref, dx_ref, sums_ref):
        i = pl.program_id(0)
        halves = [slice(0, tm // 2), slice(tm // 2, tm)]
        dhs = [_dot(dproj_ref[rs, :], w_ref[...], NN) for rs in halves]
        sums = None
        for rs, dh in zip(halves, dhs):
            xv = x_ref[rs, :]
            r = _rstd(xv)
            n = xv * r
            dng = dh * (1.0 + sc_ref[...])
            dx_ref[rs, :] = dx1_ref[rs, :] + _norm_bwd(dng * g_ref[...], n, r)
            part = jnp.concatenate([jnp.sum(dh, axis=0, keepdims=True), jnp.sum(dh * (n * g_ref[...]), axis=0, keepdims=True),
                                    jnp.sum(dng * n, axis=0, keepdims=True), jnp.zeros((5, d), F32)], axis=0)
            sums = part if sums is None else sums + part

        @pl.when(i == 0)
        def _():
            sums_ref[...] = sums

        @pl.when(i > 0)
        def _():
            sums_ref[...] += sums

    tile = lambda w: pl.BlockSpec((tm, w), lambda i: (i, 0))
    vec = pl.BlockSpec((1, d), lambda i: (0, 0))
    return pl.pallas_call(
        body, name="inproj_bwd", grid=(n_t,),
        in_specs=[tile(n_proj), pl.BlockSpec((n_proj, d), lambda i: (0, 0)), tile(d), tile(d), vec, vec],
        out_specs=[tile(d), pl.BlockSpec((8, d), lambda i: (0, 0))],
        out_shape=[jax.ShapeDtypeStruct((s_len, d), F32), jax.ShapeDtypeStruct((8, d), F32)],
        compiler_params=_params(("arbitrary",)),
    )(dproj, w_in_t, x, dx1, sc_m, g_pre_mix)


def _wgrad(a, b, name, tk, tmm):
    s_len, m = a.shape
    n = b.shape[1]
    n_k = s_len // tk

    def body(a_ref, b_ref, o_ref, acc_ref):
        k = pl.program_id(1)
        part = _dot(a_ref[...], b_ref[...], TN)

        @pl.when(k == 0)
        def _():
            acc_ref[...] = part

        @pl.when(k > 0)
        def _():
            acc_ref[...] += part

        @pl.when(k == n_k - 1)
        def _():
            o_ref[...] = acc_ref[...].astype(BF16)

    return pl.pallas_call(
        body, name=name, grid=(m // tmm, n_k),
        in_specs=[pl.BlockSpec((tk, tmm), lambda j, k: (k, j)), pl.BlockSpec((tk, n), lambda j, k: (k, 0))],
        out_specs=pl.BlockSpec((tmm, n), lambda j, k: (j, 0)),
        out_shape=jax.ShapeDtypeStruct((m, n), BF16),
        scratch_shapes=[pltpu.VMEM((tmm, n), F32)],
        compiler_params=_params(("arbitrary", "arbitrary")),
    )(a, b)


def _place():
    return lax.axis_index("x"), lax.axis_index("y"), lax.axis_index("c")


def _peer(k):
    x, y, c = _place()
    bx, by, bc = (k >> 2) & 1, (k >> 1) & 1, k & 1
    return (x ^ bx if bx else x, y ^ by if by else y, c ^ bc if bc else c)


def _index(pos):
    return 4 * pos[0] + 2 * pos[1] + pos[2]


def _ada_exchange(c_rows, w_ada, b_ada_cols, taps):
    d = c_rows.shape[1]
    ncol = w_ada.shape[1]

    def body(c_ref, w_ref, b_ref, t_ref, call_ref, mod_ref, tall_ref, stage_ref, send_sems, recv_sems):
        me = _index(_place())
        call_ref[me] = c_ref[...]
        tall_ref[me] = t_ref[...]

        def gather(k):
            return pltpu.make_async_remote_copy(
                src_ref=c_ref, dst_ref=call_ref.at[me], send_sem=send_sems.at[0, k - 1], recv_sem=recv_sems.at[0, k - 1],
                device_id=_peer(k), device_id_type=MESH)

        def gather_taps(k):
            return pltpu.make_async_remote_copy(
                src_ref=t_ref, dst_ref=tall_ref.at[me], send_sem=send_sems.at[2, k - 1], recv_sem=recv_sems.at[2, k - 1],
                device_id=_peer(k), device_id_type=MESH)

        for k in range(1, N_DEV):
            gather(k).start()
        for k in range(1, N_DEV):
            gather_taps(k).start()
        for k in range(1, N_DEV):
            gather(k).wait_recv()
        cv = jnp.concatenate([call_ref[b, 0:1, :] for b in range(N_DEV)], axis=0)
        act = cv * jax.nn.sigmoid(cv)
        mod = lax.dot_general(act, w_ref[...], NN, preferred_element_type=F32,
                              precision=lax.Precision.HIGHEST) + b_ref[...]
        for b in range(N_DEV):
            stage_ref[b] = jnp.broadcast_to(mod[b:b + 1, :], (8, ncol))
        mod_ref[me] = stage_ref[me]

        def scatter(k):
            return pltpu.make_async_remote_copy(
                src_ref=stage_ref.at[_index(_peer(k))], dst_ref=mod_ref.at[me],
                send_sem=send_sems.at[1, k - 1], recv_sem=recv_sems.at[1, k - 1],
                device_id=_peer(k), device_id_type=MESH)

        for k in range(1, N_DEV):
            scatter(k).start()
        for k in range(1, N_DEV):
            scatter(k).wait_recv()
        for k in range(1, N_DEV):
            gather_taps(k).wait_recv()
        for k in range(1, N_DEV):
            gather(k).wait_send()
            scatter(k).wait_send()
            gather_taps(k).wait_send()

    vmem = pl.BlockSpec(memory_space=pltpu.VMEM)
    return pl.pallas_call(
        body, name="ada_exchange",
        in_specs=[vmem] * 4, out_specs=[vmem] * 3,
        out_shape=[jax.ShapeDtypeStruct((N_DEV, 8, d), F32), jax.ShapeDtypeStruct((N_DEV, 8, ncol), F32),
                   jax.ShapeDtypeStruct((N_DEV,) + taps.shape, F32)],
        scratch_shapes=[pltpu.VMEM((N_DEV, 8, ncol), F32), pltpu.SemaphoreType.DMA((3, N_DEV - 1)),
                        pltpu.SemaphoreType.DMA((3, N_DEV - 1))],
        compiler_params=_params(),
    )(c_rows, w_ada, b_ada_cols, taps)


def _gather_weights(shards):
    n_w = len(shards)

    def body(*refs):
        srcs, outs = refs[:n_w], refs[n_w:2 * n_w]
        send_sems, recv_sems, local_sems = refs[2 * n_w:]
        x, y, c = _place()
        me, sibling = (x, y, c), (x, y, 1 - c)
        chips = [(1 - x, y), (x, 1 - y), (1 - x, 1 - y)]

        def rows(w, pos):
            r = shards[w].shape[0]
            return outs[w].at[pl.ds(pl.multiple_of(_index(pos) * r, 16), r), :]

        def copy(k, w, block, to, own=False):
            return pltpu.make_async_remote_copy(
                src_ref=srcs[w] if own else rows(w, block), dst_ref=rows(w, block),
                send_sem=send_sems.at[k, w], recv_sem=recv_sems.at[k, w], device_id=to, device_id_type=MESH)

        mine = [pltpu.make_async_copy(srcs[w], rows(w, me), local_sems.at[w]) for w in range(n_w)]
        for cp in mine:
            cp.start()
        first = [copy(0, w, me, sibling, own=True) for w in range(n_w)]
        first += [copy(1 + j, w, me, (*chip, c), own=True) for j, chip in enumerate(chips) for w in range(n_w)]
        for cp in first:
            cp.start()
        passed = []
        for j, chip in enumerate(chips):
            for w in range(n_w):
                copy(1 + j, w, (*chip, c), me).wait_recv()
                fwd = copy(4 + j, w, (*chip, c), sibling)
                fwd.start()
                passed.append(fwd)
        for w in range(n_w):
            copy(0, w, sibling, me).wait_recv()
        for j, chip in enumerate(chips):
            for w in range(n_w):
                copy(4 + j, w, (*chip, 1 - c), me).wait_recv()
        for cp in first + passed:
            cp.wait_send()
        for cp in mine:
            cp.wait()

    hbm = pl.BlockSpec(memory_space=pltpu.HBM)
    return pl.pallas_call(
        body, name="gather_weights",
        in_specs=[hbm] * n_w, out_specs=[hbm] * n_w,
        out_shape=[jax.ShapeDtypeStruct((N_DEV * s.shape[0], s.shape[1]), s.dtype) for s in shards],
        scratch_shapes=[pltpu.SemaphoreType.DMA((N_DEV - 1, n_w)), pltpu.SemaphoreType.DMA((N_DEV - 1, n_w)),
                        pltpu.SemaphoreType.DMA((n_w,))],
        compiler_params=_params(),
    )(*shards)


def _scatter_grads(grads):
    n_w = len(grads)

    def body(*refs):
        srcs, outs = refs[:n_w], refs[n_w:2 * n_w]
        send_sems, recv_sems, local_sems = refs[2 * n_w:]
        me = _index(_place())

        def slab(w, dev):
            r = grads[w].shape[0] // N_DEV
            return srcs[w].at[pl.ds(pl.multiple_of(dev * r, 16), r), :]

        def copy(k, w):
            return pltpu.make_async_remote_copy(
                src_ref=slab(w, _index(_peer(k))), dst_ref=outs[w].at[me],
                send_sem=send_sems.at[k - 1, w], recv_sem=recv_sems.at[k - 1, w],
                device_id=_peer(k), device_id_type=MESH)

        mine = [pltpu.make_async_copy(slab(w, me), outs[w].at[me], local_sems.at[w]) for w in range(n_w)]
        for cp in mine:
            cp.start()
        sends = [copy(k, w) for k in range(1, N_DEV) for w in range(n_w)]
        for cp in sends:
            cp.start()
        for cp in sends:
            cp.wait_recv()
        for cp in sends:
            cp.wait_send()
        for cp in mine:
            cp.wait()

    hbm = pl.BlockSpec(memory_space=pltpu.HBM)
    return pl.pallas_call(
        body, name="scatter_grads",
        in_specs=[hbm] * n_w, out_specs=[hbm] * n_w,
        out_shape=[jax.ShapeDtypeStruct((N_DEV, g.shape[0] // N_DEV, g.shape[1]), g.dtype) for g in grads],
        scratch_shapes=[pltpu.SemaphoreType.DMA((N_DEV - 1, n_w)), pltpu.SemaphoreType.DMA((N_DEV - 1, n_w)),
                        pltpu.SemaphoreType.DMA((n_w,))],
        compiler_params=_params(),
    )(*grads)


def _peer_copies(mode, srcs, lands, send_sems, recv_sems):
    if mode in ("gather_ici", "gather_d2d"):
        x, y, c = _place()
        sibling = (x, y, 1 - c)
        chips = [(1 - x, y), (x, 1 - y), (1 - x, 1 - y)]
        n = len(lands)

        def rows(w, pos):
            r = lands[w].shape[0] // N_DEV
            return lands[w].at[pl.ds(pl.multiple_of(_index(pos) * r, 16), r), :]

        def copy(k, w, src, dst, to):
            return pltpu.make_async_remote_copy(src_ref=src, dst_ref=dst, send_sem=send_sems.at[k * n + w],
                                                recv_sem=recv_sems.at[k * n + w], device_id=to, device_id_type=MESH)

        if mode == "gather_ici":
            targets = [sibling] + [(*chip, c) for chip in chips]
            return [copy(k, w, srcs[w], rows(w, (x, y, c)), to) for k, to in enumerate(targets) for w in range(n)]
        return [copy(j, w, rows(w, (*chip, c)), rows(w, (*chip, c)), sibling)
                for j, chip in enumerate(chips) for w in range(n)]
    me = _index(_place())
    copies = []
    for k in range(1, N_DEV):
        peer = _peer(k)
        for w, (src, land) in enumerate(zip(srcs, lands)):
            if mode == "gather":
                r = src.shape[0]
                dst = land.at[pl.ds(pl.multiple_of(me * r, 16), r), :]
            elif mode == "allgather":
                dst = land.at[me]
            else:
                r = src.shape[0] // N_DEV
                src = src.at[pl.ds(pl.multiple_of(_index(peer) * r, 16), r), :]
                dst = land.at[me]
            copies.append(pltpu.make_async_remote_copy(
                src_ref=src, dst_ref=dst, send_sem=send_sems.at[(k - 1) * len(srcs) + w],
                recv_sem=recv_sems.at[(k - 1) * len(srcs) + w],
                device_id=peer, device_id_type=MESH))
    return copies


def _landing_zone(mode, src, me, name):
    cols = src.shape[1]
    if mode == "gather":
        r = src.shape[0]
        in_spec = pl.BlockSpec((r, cols), lambda i, me_ref: (0, 0))
        out_spec = pl.BlockSpec((r, cols), lambda i, me_ref: (me_ref[0], 0))
        out_shape = jax.ShapeDtypeStruct((N_DEV * r, cols), src.dtype)
    else:
        r = src.shape[0] // N_DEV
        in_spec = pl.BlockSpec((r, cols), lambda i, me_ref: (me_ref[0], 0))
        out_spec = pl.BlockSpec((1, r, cols), lambda i, me_ref: (me_ref[0], 0, 0))
        out_shape = jax.ShapeDtypeStruct((N_DEV, r, cols), src.dtype)

    def body(me_ref, s_ref, o_ref):
        o_ref[...] = s_ref[...].reshape(o_ref.shape)

    return pl.pallas_call(
        body, name=name, out_shape=out_shape,
        grid_spec=pltpu.PrefetchScalarGridSpec(num_scalar_prefetch=1, grid=(1,), in_specs=[in_spec], out_specs=out_spec),
        compiler_params=_params(("arbitrary",)),
    )(me.reshape(1).astype(jnp.int32), src)


def _exchange_start(mode, srcs, lands, name):
    n_s, n_a = len(srcs), len(srcs) + len(lands)
    n_cp = _COPIES_PER_ARRAY.get(mode, N_DEV - 1) * len(lands)

    def body(*refs):
        for cp in _peer_copies(mode, refs[:n_s], refs[n_s:n_a], refs[n_a], refs[n_a + 1]):
            cp.start()
        refs[-1][...] = jnp.zeros_like(refs[-1])

    hbm, sem = pl.BlockSpec(memory_space=pltpu.HBM), pl.BlockSpec(memory_space=pltpu.SEMAPHORE)
    arrays = list(srcs) + list(lands)
    out = pl.pallas_call(
        body, name=name,
        out_shape=(pltpu.SemaphoreType.DMA((n_cp,)), pltpu.SemaphoreType.DMA((n_cp,)),
                   *[pltpu.HBM(a.shape, a.dtype) for a in arrays], jax.ShapeDtypeStruct((8, LANES), F32)),
        in_specs=[hbm] * n_a, out_specs=(sem, sem, *[hbm] * n_a, pl.BlockSpec(memory_space=pltpu.VMEM)),
        input_output_aliases={i: 2 + i for i in range(n_a)},
        compiler_params=pltpu.CompilerParams(has_side_effects=pltpu.SideEffectType.DATAFLOW_SIDE_EFFECTING),
    )(*[pltpu.with_memory_space_constraint(a, pltpu.HBM) for a in arrays])
    return out[0], out[1], out[2:2 + n_s], out[2 + n_s:2 + n_a], out[-1]


_COPIES_PER_ARRAY = {"gather_ici": 4, "gather_d2d": 3}


def _exchange_wait(mode, send_sems, recv_sems, srcs, lands, after, name):
    n_s, n_a = len(srcs), len(srcs) + len(lands)

    def body(*refs):
        copies = _peer_copies(mode, refs[:n_s], refs[n_s:n_a], refs[n_a], refs[n_a + 1])
        for cp in copies:
            cp.wait_send()
        for cp in copies:
            cp.wait_recv()

    hbm, sem = pl.BlockSpec(memory_space=pltpu.HBM), pl.BlockSpec(memory_space=pltpu.SEMAPHORE)
    arrays = list(srcs) + list(lands)
    out = pl.pallas_call(
        body, name=name, out_shape=tuple(pltpu.HBM(a.shape, a.dtype) for a in arrays),
        in_specs=[hbm] * n_a + [sem, sem, pl.BlockSpec(memory_space=pl.ANY)], out_specs=tuple([hbm] * n_a),
        input_output_aliases={i: i for i in range(n_a)},
        compiler_params=pltpu.CompilerParams(has_side_effects=pltpu.SideEffectType.DATAFLOW_SIDE_EFFECTING),
    )(*arrays, send_sems, recv_sems, after)
    return out[n_s:]


SMALL_WEIGHTS = ("b_ada", "g_pre_mix", "g_post_mix", "g_pre_ffn", "g_post_ffn", "w_pool", "b_pool", "pool_scale", "conv_b")


MOD_ROWS = ((0, 0), (0, 1), (1, 3), (1, 0), (1, 1), (2, 0))


def _small_sum(mine, gathered):
    n_l = len(mine)
    d = mine[0].shape[1]

    def body(*refs):
        loc, got = refs[:n_l], refs[n_l:2 * n_l]
        tot_refs, dmod_ref = refs[2 * n_l:3 * n_l], refs[3 * n_l]
        me = _index(_place())
        part = lambda a, dev: jnp.where(dev == me, loc[a][...], got[a][dev])
        for a in range(n_l):
            tot = part(a, 0)
            for dev in range(1, N_DEV):
                tot = tot + part(a, dev)
            tot_refs[a][...] = tot
        for dev in range(N_DEV):
            for k, (a, r) in enumerate(MOD_ROWS):
                dmod_ref[dev:dev + 1, k * d:(k + 1) * d] = part(a, dev)[r:r + 1, :]

    vmem = pl.BlockSpec(memory_space=pltpu.VMEM)
    out = pl.pallas_call(
        body, name="small_sum", in_specs=[vmem] * (2 * n_l), out_specs=[vmem] * (n_l + 1),
        out_shape=[jax.ShapeDtypeStruct(a.shape, F32) for a in mine] + [jax.ShapeDtypeStruct((N_DEV, 6 * d), F32)],
        compiler_params=_params(),
    )(*mine, *gathered)
    return out[:n_l], out[n_l]


def _small_adam(totals, weights, moms, vels):
    n_t, n_w = len(totals), len(weights)

    def body(*refs):
        t_in, t_mix, t_ffn, t_pool, t_blk, t_conv, _ = (r[...] for r in refs[:n_t])
        w_refs, m_refs, v_refs = (refs[n_t + k * n_w:n_t + (k + 1) * n_w] for k in range(3))
        outs = refs[n_t + 3 * n_w:]

        def update(idx, g, at=()):
            sel = lambda ref: ref.at[at] if at else ref
            delta, nm, nv = _adam_math(sel(w_refs[idx])[...], g, sel(m_refs[idx])[...], sel(v_refs[idx])[...])
            for k, val in enumerate((g, delta, nm, nv)):
                sel(outs[4 * idx + k])[...] = val

        tots = (t_in, t_mix, t_ffn)
        update(0, jnp.concatenate([tots[a][r:r + 1] for a, r in MOD_ROWS], axis=1))
        update(1, t_in[2:3])
        update(2, t_mix[4:5])
        update(3, t_mix[2:3])
        update(4, t_ffn[1:2])
        for gi in range(len(POOL_WINDOWS)):
            update(5, t_blk[gi], at=(0, gi))
        update(6, jnp.concatenate([t_pool[0:1, gi * HEAD_DIM:(gi + 1) * HEAD_DIM] for gi in range(len(POOL_WINDOWS))], axis=0),
               at=(0,))
        update(7, t_pool[1:2])
        update(8, t_conv[3:4])

    vmem = pl.BlockSpec(memory_space=pltpu.VMEM)
    return pl.pallas_call(
        body, name="small_adam", in_specs=[vmem] * (n_t + 3 * n_w), out_specs=[vmem] * (4 * n_w),
        out_shape=[jax.ShapeDtypeStruct(w.shape, F32) for w in weights for _ in range(4)],
        compiler_params=_params(),
    )(*totals, *weights, *moms, *vels)


def _adam_math(w, g, m, v):
    m = ADAM_B1 * m + (1.0 - ADAM_B1) * g
    v = ADAM_B2 * v + (1.0 - ADAM_B2) * (g * g)
    m_hat = m / (1.0 - ADAM_B1 ** ADAM_STEP)
    v_hat = v / (1.0 - ADAM_B2 ** ADAM_STEP)
    delta = -ADAM_LR * (m_hat / (jnp.sqrt(v_hat) + ADAM_EPS) + ADAM_WD * w)
    return delta, m, v


def _adam(w, g, m, v, name, tr):
    rows, cols = w.shape

    def body(w_ref, g_ref, m_ref, v_ref, d_ref, nm_ref, nv_ref):
        d_ref[...], nm_ref[...], nv_ref[...] = _adam_math(w_ref[...], g_ref[...], m_ref[...], v_ref[...])

    spec = pl.BlockSpec((tr, cols), lambda i: (i, 0))
    shape = jax.ShapeDtypeStruct((rows, cols), F32)
    return pl.pallas_call(
        body, name=name, grid=(rows // tr,), in_specs=[spec] * 4, out_specs=[spec] * 3,
        out_shape=[shape] * 3, compiler_params=_params(("arbitrary",)),
    )(w, g, m, v)


def _sum_adam(parts, w, m, v, name, tr):
    _, rows, cols = parts.shape

    def body(p_ref, w_ref, m_ref, v_ref, g_ref, d_ref, nm_ref, nv_ref):
        g = p_ref[0].astype(F32)
        for dev in range(1, N_DEV):
            g = g + p_ref[dev].astype(F32)
        g_ref[...] = g
        d_ref[...], nm_ref[...], nv_ref[...] = _adam_math(w_ref[...], g, m_ref[...], v_ref[...])

    spec = pl.BlockSpec((tr, cols), lambda i: (i, 0))
    shape = jax.ShapeDtypeStruct((rows, cols), F32)
    return pl.pallas_call(
        body, name=name, grid=(rows // tr,),
        in_specs=[pl.BlockSpec((N_DEV, tr, cols), lambda i: (0, i, 0)), spec, spec, spec],
        out_specs=[spec] * 4, out_shape=[shape] * 4, compiler_params=_params(("arbitrary",)),
    )(parts, w, m, v)


def _ada_grad_adam(c_all, dmod_cols, w, m, v, tr):
    rows, cols = w.shape

    def body(c_ref, dm_ref, w_ref, m_ref, v_ref, g_ref, d_ref, nm_ref, nv_ref):
        cv = c_ref[...]
        act = cv * jax.nn.sigmoid(cv)
        g = lax.dot_general(act, dm_ref[...], TN, preferred_element_type=F32, precision=lax.Precision.HIGHEST)
        g_ref[...] = g
        d_ref[...], nm_ref[...], nv_ref[...] = _adam_math(w_ref[...], g, m_ref[...], v_ref[...])

    spec = pl.BlockSpec((tr, cols), lambda i: (i, 0))
    shape = jax.ShapeDtypeStruct((rows, cols), F32)
    return pl.pallas_call(
        body, name="ada_grad_adam", grid=(rows // tr,),
        in_specs=[pl.BlockSpec((N_DEV, tr), lambda i: (0, i)), pl.BlockSpec((N_DEV, cols), lambda i: (0, 0)), spec, spec, spec],
        out_specs=[spec] * 4, out_shape=[shape] * 4, compiler_params=_params(("arbitrary",)),
    )(c_all, dmod_cols, w, m, v)


def _rope_tables(positions):
    s_len = positions.shape[0]
    inv_freq = ROPE_THETA ** (-jnp.arange(0, 2 * ROT_HALF, 2, dtype=F32) / (2 * ROT_HALF))
    ang = positions.astype(F32)[:, None] * inv_freq
    cos, sin = jnp.cos(ang), jnp.sin(ang)
    rest = HEAD_DIM - 2 * ROT_HALF
    zero = lambda n: jnp.zeros((s_len, n), F32)
    head = jnp.stack([jnp.concatenate([cos, cos, jnp.ones((s_len, rest), F32)], axis=1),
                      jnp.concatenate([-sin, zero(HEAD_DIM - ROT_HALF)], axis=1),
                      jnp.concatenate([zero(ROT_HALF), sin, zero(rest)], axis=1)])
    return jnp.tile(head, (1, 1, LANES // HEAD_DIM))


def _pad_rows(a, rows):
    return jnp.pad(a, ((0, rows - a.shape[0]), (0, 0)))


def _as_rows(a, rows):
    flat = a.reshape(-1)
    return jnp.pad(flat, (0, rows * LANES - flat.shape[0])).reshape(rows, LANES)


def _sequence_step(xs, target, rope, mods, gains, w_in_t, w_out_t, relay_ffn, fetch_ffn, send_ffn_grads, send_mix_grads, w_blk_b, b_pool_r,
                   pool_scale_r, conv_w_all, conv_b):
    sh_m, sc_m, gt_m, sh_f, sc_f, gt_f = mods
    g_pre_mix, g_post_mix, g_pre_ffn, g_post_ffn = gains
    h1, u_pool, qkv = _premix_inproj(xs, sh_m, sc_m, g_pre_mix, w_in_t, rope, tm=512)
    o_g, lse_g = [], []
    for gi, dil in enumerate(DILATIONS):
        o, lse = _attn_fwd(qkv, gi, dil)
        o_g.append(o)
        lse_g.append(lse)
    token = relay_ffn(lse_g[-1])
    x1, y1, h2, cat, attn, lse_all = _mix_out(xs, u_pool, o_g, lse_g, w_blk_b, b_pool_r, pool_scale_r, w_out_t,
                                              gt_m if token is None else gt_m + token[0:1, 0:1],
                                              g_post_mix, g_pre_ffn, sc_f, sh_f, tm=256)
    w_up_t, w_down_f = fetch_ffn(x1)
    gate, a_ffn, act, vd, dy2, dout, sums_ffn, loss_loc = _ffn_fwd_loss(h2, x1, target, w_up_t, w_down_f, conv_w_all, conv_b,
                                                              gt_f, g_post_ffn, tm=256, tf=2816, ck=256)

    dgc, dval, dw_down, dconv = _ffn_bwd_act(dy2, gate, a_ffn, act, vd, w_down_f, tm=256, ck=256)
    dup, dh2 = _ffn_bwd_up(dgc, dval, w_up_t, conv_w_all, tm=256)
    dw_up_t = _wgrad(dup, h2, "wgrad_up", tk=2048, tmm=1408)
    token = send_ffn_grads(dw_up_t, dw_down)
    if token is not None:
        sc_f = sc_f + token[0:1, 0:1]
    dx1, dpool, dattn, delta, dw_out_t, sums_mix = _mix_bwd(dh2, dout, x1, y1, cat, attn, w_out_t, sc_f, g_pre_ffn,
                                                           gt_m, g_post_mix, tm=256)
    du, dw_blk, sums_pool = _pool_bwd(dpool, u_pool, w_blk_b, b_pool_r, pool_scale_r, tm=512)
    dqkv = []
    for gi, dil in enumerate(DILATIONS):
        dqkv += list(_attn_bwd(qkv, dattn, lse_all, delta, gi, dil))
    dproj = _dproj_assemble(du, dqkv, rope, tm=512)
    dw_in_t = _wgrad(dproj, h1, "wgrad_in", tk=2048, tmm=1280)
    token = send_mix_grads(dw_in_t, dw_out_t)
    if token is not None:
        sc_m = sc_m + token[0:1, 0:1]
    grad_x, sums_in = _inproj_bwd(dproj, w_in_t, xs, dx1, sc_m, g_pre_mix, tm=256)
    return (loss_loc, grad_x, dw_in_t, dw_out_t, dw_up_t, dw_down, dw_blk, dconv,
            sums_in, sums_mix, sums_ffn, sums_pool)


def kernel(x, c, positions, w_ada, b_ada, g_pre_mix, g_post_mix, g_pre_ffn, g_post_ffn, w_in, w_pool, b_pool, pool_scale, w_out, w_up, conv_w, conv_b, w_down, loss_target, m_w_ada, m_b_ada, m_g_pre_mix, m_g_post_mix, m_g_pre_ffn, m_g_post_ffn, m_w_in, m_w_pool, m_b_pool, m_pool_scale, m_w_out, m_w_up, m_conv_w, m_conv_b, m_w_down, v_w_ada, v_b_ada, v_g_pre_mix, v_g_post_mix, v_g_pre_ffn, v_g_post_ffn, v_w_in, v_w_pool, v_b_pool, v_pool_scale, v_w_out, v_w_up, v_conv_w, v_conv_b, v_w_down):
    s_len, d = x.shape[1], x.shape[2]
    d_ff = w_down.shape[1] * N_DEV
    me = _index(_place())
    xs, target = x[0], loss_target[0]

    ncol = w_ada.shape[2]
    b_cols = lax.dynamic_slice(b_ada, (0, me * ncol), (1, ncol))
    c_all, mod, taps_all = _ada_exchange(jnp.broadcast_to(c, (8, d)), w_ada[0], b_cols, _pad_rows(conv_w[0], 8))
    c_all = c_all[:, 0, :]
    conv_w_all = jnp.transpose(taps_all[:, :3, :], (1, 0, 2)).reshape(3, d_ff)
    sh_m, sc_m, gt_m, sh_f, sc_f, gt_f = [mod[:, 0, :].reshape(1, -1)[:, k * d:(k + 1) * d] for k in range(6)]

    w_in_t, w_out_t = _gather_weights([w_in[0].T.astype(BF16), w_out[0].T.astype(BF16)])

    rope = _rope_tables(positions[0])
    w_blk = jnp.zeros((256, 256), F32)
    for gi in range(4):
        w_blk = lax.dynamic_update_slice(w_blk, w_pool[0, gi], (gi * HEAD_DIM, gi * HEAD_DIM))
    w_blk_b = w_blk.astype(BF16)
    b_pool_r, pool_scale_r = b_pool.reshape(1, 256), pool_scale.reshape(1, 256)

    up_sh, down_sh = w_up[0].T.astype(BF16), w_down[0].astype(BF16)
    w_in_t, conv_w_all, up_sh, down_sh = lax.optimization_barrier((w_in_t, conv_w_all, up_sh, down_sh))
    lands = [_landing_zone("gather", s, me, "land_" + nm) for s, nm in ((up_sh, "w_up"), (down_sh, "w_down"))]
    w_send, w_recv, w_src, w_land, w_token = _exchange_start("gather_ici", [up_sh, down_sh], lands, "ffn_weights_ici_start")
    relay = []

    def relay_ffn(after):
        arrived = _exchange_wait("gather_ici", w_send, w_recv, w_src, w_land, after, "ffn_weights_ici_wait")
        relay.extend(_exchange_start("gather_d2d", [], arrived, "ffn_weights_d2d_start"))
        return relay[4]

    def fetch_ffn(after):
        return _exchange_wait("gather_d2d", relay[0], relay[1], [], relay[3], after, "ffn_weights_d2d_wait")

    flight = []

    def send_ffn_grads(dw_up_t, dw_down):
        lands = [_landing_zone("scatter", dw_up_t, me, "land_dw_up"), _landing_zone("scatter", dw_down, me, "land_dw_down")]
        flight.extend(_exchange_start("scatter", [dw_up_t, dw_down], lands, "ffn_grads_start"))
        return flight[4]

    mix_flight = []

    def send_mix_grads(dw_in_t, dw_out_t):
        lands = [_landing_zone("scatter", dw_in_t, me, "land_dw_in"), _landing_zone("scatter", dw_out_t, me, "land_dw_out")]
        mix_flight.extend(_exchange_start("scatter", [dw_in_t, dw_out_t], lands, "mix_grads_start"))
        return mix_flight[4]

    (loss_loc, grad_x, dw_in_t, dw_out_t, _, _, dw_pool, dconv,
     sums_in, sums_mix, sums_ffn, sums_pool) = _sequence_step(
        xs, target, rope, (sh_m + w_token[0:1, 0:1], sc_m, gt_m, sh_f, sc_f, gt_f),
        (g_pre_mix, g_post_mix, g_pre_ffn, g_post_ffn),
        w_in_t, w_out_t, relay_ffn, fetch_ffn, send_ffn_grads, send_mix_grads, w_blk_b, b_pool_r, pool_scale_r, conv_w_all, conv_b)

    small = [sums_in, sums_mix, sums_ffn, sums_pool, dw_pool, dconv, loss_loc]
    small_flight = _exchange_start("allgather", small, [lax.empty((N_DEV,) + a.shape, F32) for a in small], "small_start")

    parts_ffn = _exchange_wait("scatter", *flight[:4], small_flight[4], "ffn_grads_wait")
    big = {
        "w_up": [a.T for a in _sum_adam(parts_ffn[0], w_up[0].T, m_w_up[0].T, v_w_up[0].T, "adam_w_up", 64)],
        "w_down": _sum_adam(parts_ffn[1], w_down[0], m_w_down[0], v_w_down[0], "adam_w_down", 32),
    }

    rep_w = [b_ada, g_pre_mix, g_post_mix, g_pre_ffn, g_post_ffn, w_pool, b_pool, pool_scale, conv_b]
    rep_m = [m_b_ada, m_g_pre_mix, m_g_post_mix, m_g_pre_ffn, m_g_post_ffn, m_w_pool, m_b_pool, m_pool_scale, m_conv_b]
    rep_v = [v_b_ada, v_g_pre_mix, v_g_post_mix, v_g_pre_ffn, v_g_post_ffn, v_w_pool, v_b_pool, v_pool_scale, v_conv_b]
    gathered = _exchange_wait("allgather", *small_flight[:4], big["w_down"][0], "small_wait")
    totals, dmod_all = _small_sum(small, gathered)
    dconv_tot, loss_tot = totals[5], totals[6]
    rep_out = _small_adam(totals, rep_w, rep_m, rep_v)
    g_rep, d_rep, nm_rep, nv_rep = (rep_out[k::4] for k in range(4))

    fcol = d_ff // N_DEV
    g_cw = lax.dynamic_slice(dconv_tot, (0, me * fcol), (3, fcol))
    d_cw, nm_cw, nv_cw = _adam(conv_w[0], g_cw, m_conv_w[0], v_conv_w[0], "adam_conv_w", 3)

    dmod_cols = lax.dynamic_slice(dmod_all, (0, me * ncol), (N_DEV, ncol))
    g_ada, d_ada, nm_ada, nv_ada = _ada_grad_adam(c_all, dmod_cols, w_ada[0], m_w_ada[0], v_w_ada[0], 256)

    parts_mix = _exchange_wait("scatter", *mix_flight[:4], g_ada, "mix_grads_wait")
    big["w_in"] = [a.T for a in _sum_adam(parts_mix[0], w_in[0].T, m_w_in[0].T, v_w_in[0].T, "adam_w_in", 64)]
    big["w_out"] = [a.T for a in _sum_adam(parts_mix[1], w_out[0].T, m_w_out[0].T, v_w_out[0].T, "adam_w_out", 128)]

    loss = loss_tot[0, 0]

    def group(k):
        rep = (g_rep, d_rep, nm_rep, nv_rep)[k]
        ada = (g_ada, d_ada, nm_ada, nv_ada)[k][None]
        cw = (g_cw, d_cw, nm_cw, nv_cw)[k][None]
        return [ada, rep[0], rep[1], rep[2], rep[3], rep[4], big["w_in"][k][None], rep[5], rep[6], rep[7],
                big["w_out"][k][None], big["w_up"][k][None], cw, rep[8], big["w_down"][k][None]]

    return (loss, grad_x[None], *group(0), *group(1), *group(2), *group(3))
```

```python
import functools
import math

import jax
import jax.numpy as jnp
from jax import lax
from jax.experimental import pallas as pl
from jax.experimental.pallas import tpu as pltpu

F32 = jnp.float32
BF16 = jnp.bfloat16
MESH = pl.DeviceIdType.MESH

N_DEV = 8
HEAD_DIM = 64
ROT_HALF = 8
ROPE_THETA = 500000.0
POOL_WINDOWS = (2, 4, 8, 16)
DILATIONS = (1, 4, 16)
BLOCK = 128
NORM_EPS = 1e-6
HALO = 16
MASKED = -1e30
ATTN_FWD_UNROLL = 8
ATTN_BWD_UNROLL = 4

ADAM_LR = 0.001
ADAM_B1 = 0.9
ADAM_B2 = 0.999
ADAM_EPS = 1e-08
ADAM_WD = 0.01
ADAM_STEP = 10

V7X_VMEM_LIMIT = 56 * 1024 * 1024
LANES = 128

NT = (((1,), (1,)), ((), ()))
NN = (((1,), (0,)), ((), ()))
TN = (((0,), (0,)), ((), ()))


def _dot(a, b, dims):
    return lax.dot_general(a, b, dims, preferred_element_type=F32)


def _params(sem=None, vmem=V7X_VMEM_LIMIT):
    if sem is None:
        return pltpu.CompilerParams(vmem_limit_bytes=vmem)
    return pltpu.CompilerParams(dimension_semantics=sem, vmem_limit_bytes=vmem)


def _rstd(v):
    return lax.rsqrt(jnp.mean(v * v, axis=-1, keepdims=True) + NORM_EPS)


def _norm_bwd(dn, n, rstd):
    return rstd * (dn - n * jnp.mean(dn * n, axis=-1, keepdims=True))


def _rope_fwd(p, rope_ref):
    return p * rope_ref[0] + pltpu.roll(p, LANES - ROT_HALF, 1) * rope_ref[1] + pltpu.roll(p, ROT_HALF, 1) * rope_ref[2]


def _rope_bwd(dp, rope_ref):
    return dp * rope_ref[0] + pltpu.roll(dp * rope_ref[1], ROT_HALF, 1) + pltpu.roll(dp * rope_ref[2], LANES - ROT_HALF, 1)


def _gelu_parts(v):
    k2 = 2.0 * math.sqrt(2.0 / math.pi)
    c = 0.044715
    v2 = v * v
    s = jax.nn.sigmoid(v * (k2 + (k2 * c) * v2))
    g = v * s
    dg = s + g * (1.0 - s) * (k2 + (3.0 * k2 * c) * v2)
    return g, dg


def _halo_before(i, tile):
    return jnp.maximum(i * (tile // HALO) - 1, 0)


def _premix_inproj(x, sh, sc, g, w_in_t, rope, tm):
    s_len, d = x.shape
    n_proj = w_in_t.shape[0]
    n_slab = (n_proj - 256) // LANES

    def body(x_ref, sh_ref, sc_ref, g_ref, w_ref, rope_ref, h_ref, up_ref, qkv_ref):
        xv = x_ref[...]
        h = (xv * _rstd(xv) * g_ref[...]) * (1.0 + sc_ref[...]) + sh_ref[...]
        hb = h.astype(BF16)
        h_ref[...] = hb
        up_ref[...] = _dot(hb, w_ref[0:256, :], NT)
        for pair in range(n_slab // 2):
            p = _dot(hb, w_ref[256 + 256 * pair:512 + 256 * pair, :], NT)
            for half in range(2):
                ph = p[:, half * LANES:(half + 1) * LANES]
                if pair < 6:
                    ph = _rope_fwd(ph, rope_ref)
                if pair < 3:
                    ph = ph * (HEAD_DIM ** -0.5)
                qkv_ref[2 * pair + half] = ph

    vec = pl.BlockSpec((1, d), lambda i: (0, 0))
    return pl.pallas_call(
        body, name="premix_inproj", grid=(s_len // tm,),
        in_specs=[pl.BlockSpec((tm, d), lambda i: (i, 0)), vec, vec, vec,
                  pl.BlockSpec((n_proj, d), lambda i: (0, 0)),
                  pl.BlockSpec((3, tm, LANES), lambda i: (0, i, 0))],
        out_specs=[pl.BlockSpec((tm, d), lambda i: (i, 0)),
                   pl.BlockSpec((tm, 256), lambda i: (i, 0)),
                   pl.BlockSpec((n_slab, tm, LANES), lambda i: (0, i, 0))],
        out_shape=[jax.ShapeDtypeStruct((s_len, d), BF16),
                   jax.ShapeDtypeStruct((s_len, 256), F32),
                   jax.ShapeDtypeStruct((n_slab, s_len, LANES), F32)],
        compiler_params=_params(("arbitrary",)),
    )(x, sh, sc, g, w_in_t, rope)


def _block_rows(n, r, dil):
    start = n * (BLOCK * dil) + r
    if dil == 1:
        return pl.ds(pl.multiple_of(start, BLOCK), BLOCK)
    return pl.ds(start, BLOCK, stride=dil)


def _band_mask(n):
    ri = lax.broadcasted_iota(jnp.int32, (BLOCK, 2 * BLOCK), 0)
    cj = lax.broadcasted_iota(jnp.int32, (BLOCK, 2 * BLOCK), 1)
    cur = (cj >= BLOCK) & (cj - BLOCK <= ri)
    prev = (cj < BLOCK) & (cj >= ri) & (n > 0)
    return cur | prev


def _attn_fwd(qkv, group, dil):
    s_len = qkv.shape[1]
    nb = s_len // (BLOCK * dil)

    def body(q_ref, k_ref, v_ref, o_ref, lse_ref):
        lane = lax.broadcasted_iota(jnp.int32, (BLOCK, LANES), 1)
        first = lane < HEAD_DIM

        def block(t, carry):
            r, n = t // nb, t % nb
            cur = _block_rows(n, r, dil)
            prev = _block_rows(jnp.maximum(n - 1, 0), r, dil)
            q = q_ref[0, cur, :]
            kcat = jnp.concatenate([k_ref[0, prev, :], k_ref[0, cur, :]], axis=0).astype(BF16)
            vcat = jnp.concatenate([v_ref[0, prev, :], v_ref[0, cur, :]], axis=0).astype(BF16)
            valid = _band_mask(n)
            q2 = jnp.concatenate([jnp.where(first, q, 0.0), jnp.where(first, 0.0, q)], axis=0).astype(BF16)
            s = jnp.where(jnp.concatenate([valid, valid], axis=0), _dot(q2, kcat, NT), MASKED)
            m = jnp.max(s, axis=-1, keepdims=True)
            p = jnp.exp(s - m)
            den = jnp.sum(p, axis=-1, keepdims=True)
            o2 = _dot(p.astype(BF16), vcat, NN) / den
            lse2 = m + jnp.log(den)
            o_ref[0, cur, :] = jnp.where(first, o2[:BLOCK], o2[BLOCK:])
            lse_ref[0, cur, :] = jnp.where(first, lse2[:BLOCK], lse2[BLOCK:])
            return carry

        lax.fori_loop(0, nb * dil, block, 0, unroll=ATTN_FWD_UNROLL)

    def slab(base):
        return pl.BlockSpec((1, s_len, LANES), lambda s: (base + 2 * group + s, 0, 0))

    out = pl.BlockSpec((1, s_len, LANES), lambda s: (s, 0, 0))
    shape = jax.ShapeDtypeStruct((2, s_len, LANES), F32)
    return pl.pallas_call(
        body, name=f"attn_fwd_d{dil}", grid=(2,),
        in_specs=[slab(0), slab(6), slab(12)], out_specs=[out, out], out_shape=[shape, shape],
        compiler_params=_params(("arbitrary",)),
    )(qkv, qkv, qkv)


def _pool_mixed(u, halo, i, tm):
    ue = jnp.concatenate([halo, u], axis=0)
    s2 = ue + pltpu.roll(ue, 1, 0)
    s4 = s2 + pltpu.roll(s2, 2, 0)
    s8 = s4 + pltpu.roll(s4, 4, 0)
    s16 = s8 + pltpu.roll(s8, 8, 0)
    grp = lax.broadcasted_iota(jnp.int32, (tm, 256), 1) // HEAD_DIM
    pick = lambda a, b, c, e: jnp.where(grp == 0, a, jnp.where(grp == 1, b, jnp.where(grp == 2, c, e)))
    win_sum = pick(s2[HALO:], s4[HALO:], s8[HALO:], s16[HALO:])
    pos = (i * tm + lax.broadcasted_iota(jnp.int32, (tm, 256), 0)).astype(F32)
    count = jnp.minimum(pos + 1.0, pick(*[float(w) for w in POOL_WINDOWS]))
    return win_sum / count - u, count


def _mix_out(x, u_pool, o_g, lse_g, w_blk, b_pool, pool_scale, w_out_t, gt_m, g_post_mix, g_pre_ffn, sc_f, sh_f, tm):
    s_len, d = x.shape

    def body(x_ref, u_ref, uh_ref, o0, o1, o2, l0, l1, l2, wb_ref, bp_ref, ps_ref, wo_ref,
             gt_ref, g1_ref, g2_ref, sc_ref, sh_ref,
             x1_ref, y1_ref, h2_ref, cat_ref, attn_ref, lall_ref):
        i = pl.program_id(0)
        u = u_ref[...]
        halo = uh_ref[...] * (i > 0).astype(F32)
        mixed, _ = _pool_mixed(u, halo, i, tm)
        y = _dot(mixed.astype(BF16), wb_ref[...], NN) + bp_ref[...]
        pool = y * ps_ref[...]
        attn = []
        for s in range(2):
            la, lb, lc = l0[s], l1[s], l2[s]
            mx = jnp.maximum(jnp.maximum(la, lb), lc)
            ea, eb, ec = jnp.exp(la - mx), jnp.exp(lb - mx), jnp.exp(lc - mx)
            den = ea + eb + ec
            lall_ref[s] = mx + jnp.log(den)
            attn.append((ea / den) * o0[s] + (eb / den) * o1[s] + (ec / den) * o2[s])
        attn = jnp.concatenate(attn, axis=1)
        attn_ref[...] = attn
        cat = jnp.concatenate([pool, attn], axis=1).astype(BF16)
        cat_ref[...] = cat
        y1 = _dot(cat, wo_ref[...], NT)
        y1_ref[...] = y1.astype(BF16)
        x1 = x_ref[...] + gt_ref[...] * (y1 * _rstd(y1) * g1_ref[...])
        x1_ref[...] = x1
        h2 = (x1 * _rstd(x1) * g2_ref[...]) * (1.0 + sc_ref[...]) + sh_ref[...]
        h2_ref[...] = h2.astype(BF16)

    tile = lambda w: pl.BlockSpec((tm, w), lambda i: (i, 0))
    slab = pl.BlockSpec((2, tm, LANES), lambda i: (0, i, 0))
    const = lambda a: pl.BlockSpec(a.shape, lambda i: (0,) * a.ndim)
    return pl.pallas_call(
        body, name="mix_out", grid=(s_len // tm,),
        in_specs=[tile(d), tile(256), pl.BlockSpec((HALO, 256), lambda i: (_halo_before(i, tm), 0)),
                  slab, slab, slab, slab, slab, slab,
                  const(w_blk), const(b_pool), const(pool_scale), const(w_out_t),
                  const(gt_m), const(g_post_mix), const(g_pre_ffn), const(sc_f), const(sh_f)],
        out_specs=[tile(d), tile(d), tile(d), tile(512), tile(256), slab],
        out_shape=[jax.ShapeDtypeStruct((s_len, d), F32), jax.ShapeDtypeStruct((s_len, d), BF16),
                   jax.ShapeDtypeStruct((s_len, d), BF16), jax.ShapeDtypeStruct((s_len, 512), BF16),
                   jax.ShapeDtypeStruct((s_len, 256), F32), jax.ShapeDtypeStruct((2, s_len, LANES), F32)],
        compiler_params=_params(("arbitrary",)),
    )(x, u_pool, u_pool, *o_g, *lse_g, w_blk, b_pool, pool_scale, w_out_t, gt_m, g_post_mix, g_pre_ffn, sc_f, sh_f)


def _conv_gate(gate_ext, cw, cb):
    gc = gate_ext * cw[2:3, :] + pltpu.roll(gate_ext, 1, 0) * cw[1:2, :] + pltpu.roll(gate_ext, 2, 0) * cw[0:1, :]
    return gc[HALO:] + cb


def _ffn_fwd_loss(h2, x1, target, w_up_t, w_down, conv_w, conv_b, gt_f, g_post_ffn, tm, tf, ck):
    s_len, d = x1.shape
    d_ff = w_down.shape[0]
    n_f = d_ff // tf

    def body(h_ref, hh_ref, x1_ref, tgt_ref, wg_ref, wv_ref, wd_ref, cw_ref, cb_ref, gt_ref, g_ref,
             gate_ref, a_ref, act_ref, vd_ref, dy2_ref, dout_ref, sums_ref, loss_ref, acc_ref):
        i, j = pl.program_id(0), pl.program_id(1)

        @pl.when((i == 0) & (j == 0))
        def _():
            sums_ref[...] = jnp.zeros_like(sums_ref)
            loss_ref[...] = jnp.zeros_like(loss_ref)

        h = h_ref[...]
        h_ext = jnp.concatenate([hh_ref[...], h], axis=0)
        row = lax.broadcasted_iota(jnp.int32, (tm + HALO, ck), 0)
        no_halo = (row < HALO) & (i == 0)

        def up(c):
            cs = slice(c * ck, (c + 1) * ck)
            return jnp.where(no_halo, 0.0, _dot(h_ext, wg_ref[cs, :], NT)), _dot(h, wv_ref[cs, :], NT)

        part = None
        n_c = tf // ck
        nxt = up(0)
        for c in range(n_c):
            cs = slice(c * ck, (c + 1) * ck)
            gate_ext, val = nxt
            if c + 1 < n_c:
                nxt = up(c + 1)
            act, dact = _gelu_parts(_conv_gate(gate_ext, cw_ref[:, cs], cb_ref[:, cs]))
            a = (act * val).astype(BF16)
            gate_ref[:, cs] = gate_ext[HALO:].astype(BF16)
            a_ref[:, cs] = a
            act_ref[:, cs] = act.astype(BF16)
            vd_ref[:, cs] = (val * dact).astype(BF16)
            p = _dot(a, wd_ref[cs, :], NN)
            part = p if part is None else part + p

        @pl.when(j == 0)
        def _():
            acc_ref[...] = part

        @pl.when(j > 0)
        def _():
            acc_ref[...] += part

        @pl.when(j == n_f - 1)
        def _():
            y2 = acc_ref[...]
            rstd = _rstd(y2)
            n = y2 * rstd
            rn = n * g_ref[...]
            err = x1_ref[...] + gt_ref[...] * rn - tgt_ref[...]
            loss_ref[...] += 0.5 * jnp.sum(jnp.mean(err * err, axis=-1, keepdims=True), axis=0, keepdims=True)
            dout = err * (1.0 / d)
            dout_ref[...] = dout
            drn = dout * gt_ref[...]
            sums_ref[0:1, :] += jnp.sum(dout * rn, axis=0, keepdims=True)
            sums_ref[1:2, :] += jnp.sum(drn * n, axis=0, keepdims=True)
            dy2_ref[...] = _norm_bwd(drn * g_ref[...], n, rstd).astype(BF16)

    tok = lambda w: pl.BlockSpec((tm, w), lambda i, j: (i, 0))
    tokf = pl.BlockSpec((tm, tf), lambda i, j: (i, j))
    vec = pl.BlockSpec((1, d), lambda i, j: (0, 0))
    once = {"pipeline_mode": pl.Buffered(1)} if n_f == 1 else {}
    return pl.pallas_call(
        body, name="ffn_fwd_loss", grid=(s_len // tm, n_f),
        in_specs=[tok(d), pl.BlockSpec((HALO, d), lambda i, j: (_halo_before(i, tm), 0)), tok(d), tok(d),
                  pl.BlockSpec((tf, d), lambda i, j: (j, 0), **once),
                  pl.BlockSpec((tf, d), lambda i, j: (j + n_f, 0), **once),
                  pl.BlockSpec((tf, d), lambda i, j: (j, 0), **once),
                  pl.BlockSpec((3, tf), lambda i, j: (0, j)), pl.BlockSpec((1, tf), lambda i, j: (0, j)), vec, vec],
        out_specs=[tokf, tokf, tokf, tokf, tok(d), tok(d), pl.BlockSpec((8, d), lambda i, j: (0, 0)),
                   pl.BlockSpec((8, LANES), lambda i, j: (0, 0))],
        out_shape=[jax.ShapeDtypeStruct((s_len, d_ff), BF16)] * 4
        + [jax.ShapeDtypeStruct((s_len, d), BF16), jax.ShapeDtypeStruct((s_len, d), F32),
                   jax.ShapeDtypeStruct((8, d), F32), jax.ShapeDtypeStruct((8, LANES), F32)],
        scratch_shapes=[pltpu.VMEM((tm, d), F32)],
        compiler_params=_params(("arbitrary", "arbitrary")),
    )(h2, h2, x1, target, w_up_t, w_up_t, w_down, conv_w, conv_b, gt_f, g_post_ffn)


def _ffn_bwd_act(dy2, gate, a, act, vd, w_down, tm, ck):
    s_len, d = dy2.shape
    d_ff = w_down.shape[0]
    n_t, n_c = s_len // tm, d_ff // ck

    def body(dy_ref, g_ref, gh_ref, a_ref, act_ref, vd_ref, wd_ref, dgc_ref, dval_ref, dwd_ref, dconv_ref, acc_ref):
        i = pl.program_id(0)

        @pl.when(i == 0)
        def _():
            acc_ref[...] = jnp.zeros_like(acc_ref)
            dconv_ref[...] = jnp.zeros_like(dconv_ref)

        dy = dy_ref[...]
        row = lax.broadcasted_iota(jnp.int32, (tm + HALO, ck), 0)
        no_halo = (row < HALO) & (i == 0)

        def down(c):
            return _dot(dy, wd_ref[c * ck:(c + 1) * ck, :], NT)

        nxt = down(0)
        for c in range(n_c):
            cs = slice(c * ck, (c + 1) * ck)
            da = nxt
            if c + 1 < n_c:
                nxt = down(c + 1)
            acc_ref[cs, :] += _dot(a_ref[:, cs], dy, TN)
            gate_ext = jnp.where(no_halo, 0.0, jnp.concatenate([gh_ref[:, cs], g_ref[:, cs]], axis=0).astype(F32))
            dgc = da * vd_ref[:, cs].astype(F32)
            dgc_ref[:, cs] = dgc.astype(BF16)
            dval_ref[:, cs] = (da * act_ref[:, cs].astype(F32)).astype(BF16)
            rows = [jnp.sum(dgc * pltpu.roll(gate_ext, 2 - k, 0)[HALO:], axis=0, keepdims=True) for k in range(2)]
            rows += [jnp.sum(dgc * gate_ext[HALO:], axis=0, keepdims=True), jnp.sum(dgc, axis=0, keepdims=True),
                     jnp.zeros((4, ck), F32)]
            dconv_ref[:, cs] += jnp.concatenate(rows, axis=0)

        @pl.when(i == n_t - 1)
        def _():
            dwd_ref[...] = acc_ref[...].astype(BF16)

    tokf = pl.BlockSpec((tm, d_ff), lambda i: (i, 0))
    return pl.pallas_call(
        body, name="ffn_bwd_act", grid=(n_t,),
        in_specs=[pl.BlockSpec((tm, d), lambda i: (i, 0)), tokf,
                  pl.BlockSpec((HALO, d_ff), lambda i: (_halo_before(i, tm), 0)), tokf, tokf, tokf,
                  pl.BlockSpec((d_ff, d), lambda i: (0, 0), pipeline_mode=pl.Buffered(1))],
        out_specs=[tokf, tokf, pl.BlockSpec((d_ff, d), lambda i: (0, 0)), pl.BlockSpec((8, d_ff), lambda i: (0, 0))],
        out_shape=[jax.ShapeDtypeStruct((s_len, d_ff), BF16), jax.ShapeDtypeStruct((s_len, d_ff), BF16),
                   jax.ShapeDtypeStruct((d_ff, d), BF16), jax.ShapeDtypeStruct((8, d_ff), F32)],
        scratch_shapes=[pltpu.VMEM((d_ff, d), F32)],
        compiler_params=_params(("arbitrary",)),
    )(dy2, gate, gate, a, act, vd, w_down)


def _ffn_bwd_up(dgc, dval, w_up_t, conv_w, tm):
    s_len, d_ff = dgc.shape
    d = w_up_t.shape[1]
    n_t = s_len // tm

    def body(dg_ref, dgn_ref, dv_ref, cw_ref, w_ref, dup_ref, dh_ref):
        i = pl.program_id(0)
        nxt = dgn_ref[...].astype(F32) * (i < n_t - 1).astype(F32)
        ext = jnp.concatenate([dg_ref[...].astype(F32), nxt], axis=0)
        rows = tm + HALO
        dgate = (ext * cw_ref[2:3, :] + pltpu.roll(ext, rows - 1, 0) * cw_ref[1:2, :]
                 + pltpu.roll(ext, rows - 2, 0) * cw_ref[0:1, :])[:tm]
        dup = jnp.concatenate([dgate.astype(BF16), dv_ref[...]], axis=1)
        dup_ref[...] = dup
        dh_ref[...] = _dot(dup, w_ref[...], NN).astype(BF16)

    tokf = pl.BlockSpec((tm, d_ff), lambda i: (i, 0))
    return pl.pallas_call(
        body, name="ffn_bwd_up", grid=(n_t,),
        in_specs=[tokf, pl.BlockSpec((HALO, d_ff), lambda i: (jnp.minimum((i + 1) * (tm // HALO), s_len // HALO - 1), 0)),
                  tokf, pl.BlockSpec((3, d_ff), lambda i: (0, 0)), pl.BlockSpec((2 * d_ff, d), lambda i: (0, 0))],
        out_specs=[pl.BlockSpec((tm, 2 * d_ff), lambda i: (i, 0)), pl.BlockSpec((tm, d), lambda i: (i, 0))],
        out_shape=[jax.ShapeDtypeStruct((s_len, 2 * d_ff), BF16), jax.ShapeDtypeStruct((s_len, d), BF16)],
        compiler_params=_params(("arbitrary",)),
    )(dgc, dgc, dval, conv_w, w_up_t)


def _mix_bwd(dh2, dout, x1, y1, cat, attn, w_out_t, sc_f, g_pre_ffn, gt_m, g_post_mix, tm):
    s_len, d = x1.shape
    n_t = s_len // tm

    def body(dh_ref, do_ref, x1_ref, y1_ref, cat_ref, at_ref, wo_ref, sc_ref, g2_ref, gt_ref, g1_ref,
             dx1_ref, dpool_ref, dattn_ref, delta_ref, dwo_ref, sums_ref, acc_ref):
        i = pl.program_id(0)
        dh = dh_ref[...].astype(F32)
        x1 = x1_ref[...]
        r2 = _rstd(x1)
        n2 = x1 * r2
        ng = n2 * g2_ref[...]
        dng = dh * (1.0 + sc_ref[...])
        dx1 = do_ref[...] + _norm_bwd(dng * g2_ref[...], n2, r2)
        dx1_ref[...] = dx1
        y1 = y1_ref[...].astype(F32)
        r1 = _rstd(y1)
        n1 = y1 * r1
        drn = dx1 * gt_ref[...]
        dy1 = _norm_bwd(drn * g1_ref[...], n1, r1).astype(BF16)
        dcat = _dot(dy1, wo_ref[...], NN)
        dpool_ref[...] = dcat[:, 0:256]
        lane = lax.broadcasted_iota(jnp.int32, (tm, LANES), 1)
        first = lane < HEAD_DIM
        for s in range(2):
            da = dcat[:, 256 + s * LANES:256 + (s + 1) * LANES]
            dattn_ref[s] = da
            prod = da * at_ref[:, s * LANES:(s + 1) * LANES]
            tot = jnp.sum(prod, axis=-1, keepdims=True)
            lo = jnp.sum(jnp.where(first, prod, 0.0), axis=-1, keepdims=True)
            delta_ref[s] = jnp.where(first, lo, tot - lo)
        dwo = _dot(dy1, cat_ref[...], TN)
        sums = jnp.concatenate(
            [jnp.sum(dh, axis=0, keepdims=True), jnp.sum(dh * ng, axis=0, keepdims=True),
             jnp.sum(dng * n2, axis=0, keepdims=True), jnp.sum(dx1 * (n1 * g1_ref[...]), axis=0, keepdims=True),
             jnp.sum(drn * n1, axis=0, keepdims=True), jnp.zeros((3, d), F32)], axis=0)

        @pl.when(i == 0)
        def _():
            acc_ref[...] = dwo
            sums_ref[...] = sums

        @pl.when(i > 0)
        def _():
            acc_ref[...] += dwo
            sums_ref[...] += sums

        @pl.when(i == n_t - 1)
        def _():
            dwo_ref[...] = acc_ref[...].astype(BF16)

    tile = lambda w: pl.BlockSpec((tm, w), lambda i: (i, 0))
    slab = pl.BlockSpec((2, tm, LANES), lambda i: (0, i, 0))
    vec = pl.BlockSpec((1, d), lambda i: (0, 0))
    return pl.pallas_call(
        body, name="mix_bwd", grid=(n_t,),
        in_specs=[tile(d), tile(d), tile(d), tile(d), tile(512), tile(256),
                  pl.BlockSpec((d, 512), lambda i: (0, 0)), vec, vec, vec, vec],
        out_specs=[tile(d), tile(256), slab, slab, pl.BlockSpec((d, 512), lambda i: (0, 0)),
                   pl.BlockSpec((8, d), lambda i: (0, 0))],
        out_shape=[jax.ShapeDtypeStruct((s_len, d), F32), jax.ShapeDtypeStruct((s_len, 256), F32),
                   jax.ShapeDtypeStruct((2, s_len, LANES), F32), jax.ShapeDtypeStruct((2, s_len, LANES), F32),
                   jax.ShapeDtypeStruct((d, 512), BF16), jax.ShapeDtypeStruct((8, d), F32)],
        scratch_shapes=[pltpu.VMEM((d, 512), F32)],
        compiler_params=_params(("arbitrary",)),
    )(dh2, dout, x1, y1, cat, attn, w_out_t, sc_f, g_pre_ffn, gt_m, g_post_mix)


def _pool_bwd(dpool, u_pool, w_blk, b_pool, pool_scale, tm):
    s_len = dpool.shape[0]
    n_t = s_len // tm

    def body(dp_ref, dpn_ref, u_ref, uh_ref, wb_ref, bp_ref, ps_ref, du_ref, dwp_ref, sums_ref, acc_ref):
        i = pl.program_id(0)
        u = u_ref[...]
        mixed, _ = _pool_mixed(u, uh_ref[...] * (i > 0).astype(F32), i, tm)
        mixed_b = mixed.astype(BF16)
        y = _dot(mixed_b, wb_ref[...], NN) + bp_ref[...]
        dp = dp_ref[...]
        dy = dp * ps_ref[...]
        dwb = _dot(mixed_b, dy.astype(BF16), TN)
        sums = jnp.concatenate([jnp.sum(dy, axis=0, keepdims=True), jnp.sum(dp * y, axis=0, keepdims=True),
                                jnp.zeros((6, 256), F32)], axis=0)
        dp_ext = jnp.concatenate([dp, dpn_ref[...] * (i < n_t - 1).astype(F32)], axis=0)
        dmix = _dot((dp_ext * ps_ref[...]).astype(BF16), wb_ref[...], NT)
        rows = tm + HALO
        grp = lax.broadcasted_iota(jnp.int32, (rows, 256), 1) // HEAD_DIM
        pick = lambda a, b, c, e: jnp.where(grp == 0, a, jnp.where(grp == 1, b, jnp.where(grp == 2, c, e)))
        pos = (i * tm + lax.broadcasted_iota(jnp.int32, (rows, 256), 0)).astype(F32)
        z = dmix / jnp.minimum(pos + 1.0, pick(*[float(w) for w in POOL_WINDOWS]))
        f2 = z + pltpu.roll(z, rows - 1, 0)
        f4 = f2 + pltpu.roll(f2, rows - 2, 0)
        f8 = f4 + pltpu.roll(f4, rows - 4, 0)
        f16 = f8 + pltpu.roll(f8, rows - 8, 0)
        du_ref[...] = (pick(f2, f4, f8, f16) - dmix)[:tm]

        @pl.when(i == 0)
        def _():
            acc_ref[...] = dwb
            sums_ref[...] = sums

        @pl.when(i > 0)
        def _():
            acc_ref[...] += dwb
            sums_ref[...] += sums

        @pl.when(i == n_t - 1)
        def _():
            full = acc_ref[...]
            for gi in range(len(POOL_WINDOWS)):
                lo = gi * HEAD_DIM
                dwp_ref[gi] = full[lo:lo + HEAD_DIM, lo:lo + HEAD_DIM]

    n_g = len(POOL_WINDOWS)
    tile = pl.BlockSpec((tm, 256), lambda i: (i, 0))
    const = lambda a: pl.BlockSpec(a.shape, lambda i: (0,) * a.ndim)
    return pl.pallas_call(
        body, name="pool_bwd", grid=(n_t,),
        in_specs=[tile, pl.BlockSpec((HALO, 256), lambda i: (jnp.minimum((i + 1) * (tm // HALO), s_len // HALO - 1), 0)),
                  tile, pl.BlockSpec((HALO, 256), lambda i: (_halo_before(i, tm), 0)),
                  const(w_blk), const(b_pool), const(pool_scale)],
        out_specs=[tile, pl.BlockSpec((n_g, HEAD_DIM, HEAD_DIM), lambda i: (0, 0, 0)), pl.BlockSpec((8, 256), lambda i: (0, 0))],
        out_shape=[jax.ShapeDtypeStruct((s_len, 256), F32), jax.ShapeDtypeStruct((n_g, HEAD_DIM, HEAD_DIM), F32),
                   jax.ShapeDtypeStruct((8, 256), F32)],
        scratch_shapes=[pltpu.VMEM((256, 256), F32)],
        compiler_params=_params(("arbitrary",)),
    )(dpool, dpool, u_pool, u_pool, w_blk, b_pool, pool_scale)


def _attn_bwd(qkv, dattn, lse_all, delta, group, dil):
    s_len = qkv.shape[1]
    nb = s_len // (BLOCK * dil)

    def body(q_ref, k_ref, v_ref, do_ref, l_ref, dl_ref, dq_ref, dk_ref, dv_ref):
        lane = lax.broadcasted_iota(jnp.int32, (BLOCK, LANES), 1)
        first = lane < HEAD_DIM

        def block(t, carry):
            dk_part, dv_part = carry
            r, n = t // nb, t % nb
            cur = _block_rows(n, r, dil)
            prev = _block_rows(jnp.maximum(n - 1, 0), r, dil)
            q = q_ref[0, cur, :]
            do = do_ref[0, cur, :]
            lse = l_ref[0, cur, :]
            dlt = dl_ref[0, cur, :]
            kcat = jnp.concatenate([k_ref[0, prev, :], k_ref[0, cur, :]], axis=0).astype(BF16)
            vcat = jnp.concatenate([v_ref[0, prev, :], v_ref[0, cur, :]], axis=0).astype(BF16)
            valid = _band_mask(n)
            stack = lambda a: jnp.concatenate([jnp.where(first, a, 0.0), jnp.where(first, 0.0, a)], axis=0)
            rows2 = lambda a: jnp.concatenate([a[:, 0:1], a[:, HEAD_DIM:HEAD_DIM + 1]], axis=0)
            q2, do2 = stack(q).astype(BF16), stack(do).astype(BF16)
            valid2 = jnp.concatenate([valid, valid], axis=0)
            p = jnp.where(valid2, jnp.exp(_dot(q2, kcat, NT) - rows2(lse)), 0.0)
            ds = (p * (_dot(do2, vcat, NT) - rows2(dlt))).astype(BF16)
            dq2 = _dot(ds, kcat, NN)
            dq_ref[0, cur, :] = jnp.where(first, dq2[:BLOCK], dq2[BLOCK:])
            dkc = _dot(ds, q2, TN)
            dvc = _dot(p.astype(BF16), do2, TN)
            dk_ref[0, prev, :] = dk_part + dkc[:BLOCK]
            dv_ref[0, prev, :] = dv_part + dvc[:BLOCK]
            dk_ref[0, cur, :] = dkc[BLOCK:]
            dv_ref[0, cur, :] = dvc[BLOCK:]
            return dkc[BLOCK:], dvc[BLOCK:]

        def blocks(tt, carry):
            for u in range(ATTN_BWD_UNROLL):
                carry = block(tt * ATTN_BWD_UNROLL + u, carry)
            return carry

        zero = jnp.zeros((BLOCK, LANES), F32)
        lax.fori_loop(0, nb * dil // ATTN_BWD_UNROLL, blocks, (zero, zero))

    def slab(base):
        return pl.BlockSpec((1, s_len, LANES), lambda s: (base + 2 * group + s, 0, 0))

    one = pl.BlockSpec((1, s_len, LANES), lambda s: (s, 0, 0))
    shape = jax.ShapeDtypeStruct((2, s_len, LANES), F32)
    return pl.pallas_call(
        body, name=f"attn_bwd_d{dil}", grid=(2,),
        in_specs=[slab(0), slab(6), slab(12), one, one, one],
        out_specs=[one, one, one], out_shape=[shape, shape, shape],
        compiler_params=_params(("arbitrary",)),
    )(qkv, qkv, qkv, dattn, lse_all, delta)


def _dproj_assemble(du, dqkv, rope, tm):
    s_len = du.shape[0]
    n_proj = 256 + 18 * LANES

    def body(du_ref, *refs):
        dref, rope_ref, dproj_ref = refs[:9], refs[9], refs[10]
        dproj_ref[:, 0:256] = du_ref[...].astype(BF16)
        col = 256
        for kind in range(3):
            for grp in range(3):
                for s in range(2):
                    piece = dref[3 * grp + kind][s]
                    if kind < 2:
                        piece = _rope_bwd(piece, rope_ref)
                    if kind == 0:
                        piece = piece * (HEAD_DIM ** -0.5)
                    dproj_ref[:, col:col + LANES] = piece.astype(BF16)
                    col += LANES

    slab = pl.BlockSpec((2, tm, LANES), lambda i: (0, i, 0))
    return pl.pallas_call(
        body, name="dproj_assemble", grid=(s_len // tm,),
        in_specs=[pl.BlockSpec((tm, 256), lambda i: (i, 0))] + [slab] * 9 + [pl.BlockSpec((3, tm, LANES), lambda i: (0, i, 0))],
        out_specs=pl.BlockSpec((tm, n_proj), lambda i: (i, 0)),
        out_shape=jax.ShapeDtypeStruct((s_len, n_proj), BF16),
        compiler_params=_params(("arbitrary",)),
    )(du, *dqkv, rope)


def _inproj_bwd(dproj, w_in_t, x, dx1, sc_m, g_pre_mix, tm):
    s_len, d = x.shape
    n_proj = w_in_t.shape[0]
    n_t = s_len // tm

    def body(dproj_ref, w_ref, x_ref, dx1_ref, sc_ref, g_ref, dx_ref, sums_ref):
        i = pl.program_id(0)
        halves = [slice(0, tm // 2), slice(tm // 2, tm)]
        dhs = [_dot(dproj_ref[rs, :], w_ref[...], NN) for rs in halves]
        sums = None
        for rs, dh in zip(halves, dhs):
            xv = x_ref[rs, :]
            r = _rstd(xv)
            n = xv * r
            dng = dh * (1.0 + sc_ref[...])
            dx_ref[rs, :] = dx1_ref[rs, :] + _norm_bwd(dng * g_ref[...], n, r)
            part = jnp.concatenate([jnp.sum(dh, axis=0, keepdims=True), jnp.sum(dh * (n * g_ref[...]), axis=0, keepdims=True),
                                    jnp.sum(dng * n, axis=0, keepdims=True), jnp.zeros((5, d), F32)], axis=0)
            sums = part if sums is None else sums + part

        @pl.when(i == 0)
        def _():
            sums_ref[...] = sums

        @pl.when(i > 0)
        def _():
            sums_ref[...] += sums

    tile = lambda w: pl.BlockSpec((tm, w), lambda i: (i, 0))
    vec = pl.BlockSpec((1, d), lambda i: (0, 0))
    return pl.pallas_call(
        body, name="inproj_bwd", grid=(n_t,),
        in_specs=[tile(n_proj), pl.BlockSpec((n_proj, d), lambda i: (0, 0)), tile(d), tile(d), vec, vec],
        out_specs=[tile(d), pl.BlockSpec((8, d), lambda i: (0, 0))],
        out_shape=[jax.ShapeDtypeStruct((s_len, d), F32), jax.ShapeDtypeStruct((8, d), F32)],
        compiler_params=_params(("arbitrary",)),
    )(dproj, w_in_t, x, dx1, sc_m, g_pre_mix)


def _wgrad(a, b, name, tk, tmm):
    s_len, m = a.shape
    n = b.shape[1]
    n_k = s_len // tk

    def body(a_ref, b_ref, o_ref, acc_ref):
        k = pl.program_id(1)
        part = _dot(a_ref[...], b_ref[...], TN)

        @pl.when(k == 0)
        def _():
            acc_ref[...] = part

        @pl.when(k > 0)
        def _():
            acc_ref[...] += part

        @pl.when(k == n_k - 1)
        def _():
            o_ref[...] = acc_ref[...].astype(BF16)

    return pl.pallas_call(
        body, name=name, grid=(m // tmm, n_k),
        in_specs=[pl.BlockSpec((tk, tmm), lambda j, k: (k, j)), pl.BlockSpec((tk, n), lambda j, k: (k, 0))],
        out_specs=pl.BlockSpec((tmm, n), lambda j, k: (j, 0)),
        out_shape=jax.ShapeDtypeStruct((m, n), BF16),
        scratch_shapes=[pltpu.VMEM((tmm, n), F32)],
        compiler_params=_params(("arbitrary", "arbitrary")),
    )(a, b)


def _place():
    return lax.axis_index("x"), lax.axis_index("y"), lax.axis_index("c")


def _peer(k):
    x, y, c = _place()
    bx, by, bc = (k >> 2) & 1, (k >> 1) & 1, k & 1
    return (x ^ bx if bx else x, y ^ by if by else y, c ^ bc if bc else c)


def _index(pos):
    return 4 * pos[0] + 2 * pos[1] + pos[2]


def _ada_exchange(c_rows, w_ada, b_ada_cols, taps):
    d = c_rows.shape[1]
    ncol = w_ada.shape[1]

    def body(c_ref, w_ref, b_ref, t_ref, call_ref, mod_ref, tall_ref, stage_ref, send_sems, recv_sems):
        me = _index(_place())
        call_ref[me] = c_ref[...]
        tall_ref[me] = t_ref[...]

        def gather(k):
            return pltpu.make_async_remote_copy(
                src_ref=c_ref, dst_ref=call_ref.at[me], send_sem=send_sems.at[0, k - 1], recv_sem=recv_sems.at[0, k - 1],
                device_id=_peer(k), device_id_type=MESH)

        def gather_taps(k):
            return pltpu.make_async_remote_copy(
                src_ref=t_ref, dst_ref=tall_ref.at[me], send_sem=send_sems.at[2, k - 1], recv_sem=recv_sems.at[2, k - 1],
                device_id=_peer(k), device_id_type=MESH)

        for k in range(1, N_DEV):
            gather(k).start()
        for k in range(1, N_DEV):
            gather_taps(k).start()
        for k in range(1, N_DEV):
            gather(k).wait_recv()
        cv = jnp.concatenate([call_ref[b, 0:1, :] for b in range(N_DEV)], axis=0)
        act = cv * jax.nn.sigmoid(cv)
        mod = lax.dot_general(act, w_ref[...], NN, preferred_element_type=F32,
                              precision=lax.Precision.HIGHEST) + b_ref[...]
        for b in range(N_DEV):
            stage_ref[b] = jnp.broadcast_to(mod[b:b + 1, :], (8, ncol))
        mod_ref[me] = stage_ref[me]

        def scatter(k):
            return pltpu.make_async_remote_copy(
                src_ref=stage_ref.at[_index(_peer(k))], dst_ref=mod_ref.at[me],
                send_sem=send_sems.at[1, k - 1], recv_sem=recv_sems.at[1, k - 1],
                device_id=_peer(k), device_id_type=MESH)

        for k in range(1, N_DEV):
            scatter(k).start()
        for k in range(1, N_DEV):
            scatter(k).wait_recv()
        for k in range(1, N_DEV):
            gather_taps(k).wait_recv()
        for k in range(1, N_DEV):
            gather(k).wait_send()
            scatter(k).wait_send()
            gather_taps(k).wait_send()

    vmem = pl.BlockSpec(memory_space=pltpu.VMEM)
    return pl.pallas_call(
        body, name="ada_exchange",
        in_specs=[vmem] * 4, out_specs=[vmem] * 3,
        out_shape=[jax.ShapeDtypeStruct((N_DEV, 8, d), F32), jax.ShapeDtypeStruct((N_DEV, 8, ncol), F32),
                   jax.ShapeDtypeStruct((N_DEV,) + taps.shape, F32)],
        scratch_shapes=[pltpu.VMEM((N_DEV, 8, ncol), F32), pltpu.SemaphoreType.DMA((3, N_DEV - 1)),
                        pltpu.SemaphoreType.DMA((3, N_DEV - 1))],
        compiler_params=_params(),
    )(c_rows, w_ada, b_ada_cols, taps)


def _gather_weights(shards):
    n_w = len(shards)

    def body(*refs):
        srcs, outs = refs[:n_w], refs[n_w:2 * n_w]
        send_sems, recv_sems, local_sems = refs[2 * n_w:]
        x, y, c = _place()
        me, sibling = (x, y, c), (x, y, 1 - c)
        chips = [(1 - x, y), (x, 1 - y), (1 - x, 1 - y)]

        def rows(w, pos):
            r = shards[w].shape[0]
            return outs[w].at[pl.ds(pl.multiple_of(_index(pos) * r, 16), r), :]

        def copy(k, w, block, to, own=False):
            return pltpu.make_async_remote_copy(
                src_ref=srcs[w] if own else rows(w, block), dst_ref=rows(w, block),
                send_sem=send_sems.at[k, w], recv_sem=recv_sems.at[k, w], device_id=to, device_id_type=MESH)

        mine = [pltpu.make_async_copy(srcs[w], rows(w, me), local_sems.at[w]) for w in range(n_w)]
        for cp in mine:
            cp.start()
        first = [copy(0, w, me, sibling, own=True) for w in range(n_w)]
        first += [copy(1 + j, w, me, (*chip, c), own=True) for j, chip in enumerate(chips) for w in range(n_w)]
        for cp in first:
            cp.start()
        passed = []
        for j, chip in enumerate(chips):
            for w in range(n_w):
                copy(1 + j, w, (*chip, c), me).wait_recv()
                fwd = copy(4 + j, w, (*chip, c), sibling)
                fwd.start()
                passed.append(fwd)
        for w in range(n_w):
            copy(0, w, sibling, me).wait_recv()
        for j, chip in enumerate(chips):
            for w in range(n_w):
                copy(4 + j, w, (*chip, 1 - c), me).wait_recv()
        for cp in first + passed:
            cp.wait_send()
        for cp in mine:
            cp.wait()

    hbm = pl.BlockSpec(memory_space=pltpu.HBM)
    return pl.pallas_call(
        body, name="gather_weights",
        in_specs=[hbm] * n_w, out_specs=[hbm] * n_w,
        out_shape=[jax.ShapeDtypeStruct((N_DEV * s.shape[0], s.shape[1]), s.dtype) for s in shards],
        scratch_shapes=[pltpu.SemaphoreType.DMA((N_DEV - 1, n_w)), pltpu.SemaphoreType.DMA((N_DEV - 1, n_w)),
                        pltpu.SemaphoreType.DMA((n_w,))],
        compiler_params=_params(),
    )(*shards)


def _peer_copies(mode, srcs, lands, send_sems, recv_sems):
    if mode in ("gather_ici", "gather_d2d"):
        x, y, c = _place()
        sibling = (x, y, 1 - c)
        chips = [(1 - x, y), (x, 1 - y), (1 - x, 1 - y)]
        n = len(lands)

        def rows(w, pos):
            r = lands[w].shape[0] // N_DEV
            return lands[w].at[pl.ds(pl.multiple_of(_index(pos) * r, 16), r), :]

        def copy(k, w, src, dst, to):
            return pltpu.make_async_remote_copy(src_ref=src, dst_ref=dst, send_sem=send_sems.at[k * n + w],
                                                recv_sem=recv_sems.at[k * n + w], device_id=to, device_id_type=MESH)

        if mode == "gather_ici":
            targets = [sibling] + [(*chip, c) for chip in chips]
            return [copy(k, w, srcs[w], rows(w, (x, y, c)), to) for k, to in enumerate(targets) for w in range(n)]
        return [copy(j, w, rows(w, (*chip, c)), rows(w, (*chip, c)), sibling)
                for j, chip in enumerate(chips) for w in range(n)]
    me = _index(_place())
    copies = []
    for k in range(1, N_DEV):
        peer = _peer(k)
        for w, (src, land) in enumerate(zip(srcs, lands)):
            if mode == "gather":
                r = src.shape[0]
                dst = land.at[pl.ds(pl.multiple_of(me * r, 16), r), :]
            elif mode == "allgather":
                dst = land.at[me]
            else:
                r = src.shape[0] // N_DEV
                src = src.at[pl.ds(pl.multiple_of(_index(peer) * r, 16), r), :]
                dst = land.at[me]
            copies.append(pltpu.make_async_remote_copy(
                src_ref=src, dst_ref=dst, send_sem=send_sems.at[(k - 1) * len(srcs) + w],
                recv_sem=recv_sems.at[(k - 1) * len(srcs) + w],
                device_id=peer, device_id_type=MESH))
    return copies


def _landing_zone(mode, src, me, name):
    cols = src.shape[1]
    if mode == "gather":
        r = src.shape[0]
        in_spec = pl.BlockSpec((r, cols), lambda i, me_ref: (0, 0))
        out_spec = pl.BlockSpec((r, cols), lambda i, me_ref: (me_ref[0], 0))
        out_shape = jax.ShapeDtypeStruct((N_DEV * r, cols), src.dtype)
    else:
        r = src.shape[0] // N_DEV
        in_spec = pl.BlockSpec((r, cols), lambda i, me_ref: (me_ref[0], 0))
        out_spec = pl.BlockSpec((1, r, cols), lambda i, me_ref: (me_ref[0], 0, 0))
        out_shape = jax.ShapeDtypeStruct((N_DEV, r, cols), src.dtype)

    def body(me_ref, s_ref, o_ref):
        o_ref[...] = s_ref[...].reshape(o_ref.shape)

    return pl.pallas_call(
        body, name=name, out_shape=out_shape,
        grid_spec=pltpu.PrefetchScalarGridSpec(num_scalar_prefetch=1, grid=(1,), in_specs=[in_spec], out_specs=out_spec),
        compiler_params=_params(("arbitrary",)),
    )(me.reshape(1).astype(jnp.int32), src)


def _exchange_start(mode, srcs, lands, name):
    n_s, n_a = len(srcs), len(srcs) + len(lands)
    n_cp = _COPIES_PER_ARRAY.get(mode, N_DEV - 1) * len(lands)

    def body(*refs):
        for cp in _peer_copies(mode, refs[:n_s], refs[n_s:n_a], refs[n_a], refs[n_a + 1]):
            cp.start()
        refs[-1][...] = jnp.zeros_like(refs[-1])

    hbm, sem = pl.BlockSpec(memory_space=pltpu.HBM), pl.BlockSpec(memory_space=pltpu.SEMAPHORE)
    arrays = list(srcs) + list(lands)
    out = pl.pallas_call(
        body, name=name,
        out_shape=(pltpu.SemaphoreType.DMA((n_cp,)), pltpu.SemaphoreType.DMA((n_cp,)),
                   *[pltpu.HBM(a.shape, a.dtype) for a in arrays], jax.ShapeDtypeStruct((8, LANES), F32)),
        in_specs=[hbm] * n_a, out_specs=(sem, sem, *[hbm] * n_a, pl.BlockSpec(memory_space=pltpu.VMEM)),
        input_output_aliases={i: 2 + i for i in range(n_a)},
        compiler_params=pltpu.CompilerParams(has_side_effects=pltpu.SideEffectType.DATAFLOW_SIDE_EFFECTING),
    )(*[pltpu.with_memory_space_constraint(a, pltpu.HBM) for a in arrays])
    return out[0], out[1], out[2:2 + n_s], out[2 + n_s:2 + n_a], out[-1]


_COPIES_PER_ARRAY = {"gather_ici": 4, "gather_d2d": 3}


def _exchange_wait(mode, send_sems, recv_sems, srcs, lands, after, name):
    n_s, n_a = len(srcs), len(srcs) + len(lands)

    def body(*refs):
        copies = _peer_copies(mode, refs[:n_s], refs[n_s:n_a], refs[n_a], refs[n_a + 1])
        for cp in copies:
            cp.wait_send()
        for cp in copies:
            cp.wait_recv()

    hbm, sem = pl.BlockSpec(memory_space=pltpu.HBM), pl.BlockSpec(memory_space=pltpu.SEMAPHORE)
    arrays = list(srcs) + list(lands)
    out = pl.pallas_call(
        body, name=name, out_shape=tuple(pltpu.HBM(a.shape, a.dtype) for a in arrays),
        in_specs=[hbm] * n_a + [sem, sem, pl.BlockSpec(memory_space=pl.ANY)], out_specs=tuple([hbm] * n_a),
        input_output_aliases={i: i for i in range(n_a)},
        compiler_params=pltpu.CompilerParams(has_side_effects=pltpu.SideEffectType.DATAFLOW_SIDE_EFFECTING),
    )(*arrays, send_sems, recv_sems, after)
    return out[n_s:]


SMALL_WEIGHTS = ("b_ada", "g_pre_mix", "g_post_mix", "g_pre_ffn", "g_post_ffn", "w_pool", "b_pool", "pool_scale", "conv_b")


MOD_ROWS = ((0, 0), (0, 1), (1, 3), (1, 0), (1, 1), (2, 0))


def _small_sum(mine, gathered):
    n_l = len(mine)
    d = mine[0].shape[1]

    def body(*refs):
        loc, got = refs[:n_l], refs[n_l:2 * n_l]
        tot_refs, dmod_ref = refs[2 * n_l:3 * n_l], refs[3 * n_l]
        me = _index(_place())
        part = lambda a, dev: jnp.where(dev == me, loc[a][...], got[a][dev])
        for a in range(n_l):
            tot = part(a, 0)
            for dev in range(1, N_DEV):
                tot = tot + part(a, dev)
            tot_refs[a][...] = tot
        for dev in range(N_DEV):
            for k, (a, r) in enumerate(MOD_ROWS):
                dmod_ref[dev:dev + 1, k * d:(k + 1) * d] = part(a, dev)[r:r + 1, :]

    vmem = pl.BlockSpec(memory_space=pltpu.VMEM)
    out = pl.pallas_call(
        body, name="small_sum", in_specs=[vmem] * (2 * n_l), out_specs=[vmem] * (n_l + 1),
        out_shape=[jax.ShapeDtypeStruct(a.shape, F32) for a in mine] + [jax.ShapeDtypeStruct((N_DEV, 6 * d), F32)],
        compiler_params=_params(),
    )(*mine, *gathered)
    return out[:n_l], out[n_l]


def _small_adam(totals, weights, moms, vels):
    n_t, n_w = len(totals), len(weights)

    def body(*refs):
        t_in, t_mix, t_ffn, t_pool, t_blk, t_conv, _ = (r[...] for r in refs[:n_t])
        w_refs, m_refs, v_refs = (refs[n_t + k * n_w:n_t + (k + 1) * n_w] for k in range(3))
        outs = refs[n_t + 3 * n_w:]

        def update(idx, g, at=()):
            sel = lambda ref: ref.at[at] if at else ref
            delta, nm, nv = _adam_math(sel(w_refs[idx])[...], g, sel(m_refs[idx])[...], sel(v_refs[idx])[...])
            for k, val in enumerate((g, delta, nm, nv)):
                sel(outs[4 * idx + k])[...] = val

        tots = (t_in, t_mix, t_ffn)
        update(0, jnp.concatenate([tots[a][r:r + 1] for a, r in MOD_ROWS], axis=1))
        update(1, t_in[2:3])
        update(2, t_mix[4:5])
        update(3, t_mix[2:3])
        update(4, t_ffn[1:2])
        for gi in range(len(POOL_WINDOWS)):
            update(5, t_blk[gi], at=(0, gi))
        update(6, jnp.concatenate([t_pool[0:1, gi * HEAD_DIM:(gi + 1) * HEAD_DIM] for gi in range(len(POOL_WINDOWS))], axis=0),
               at=(0,))
        update(7, t_pool[1:2])
        update(8, t_conv[3:4])

    vmem = pl.BlockSpec(memory_space=pltpu.VMEM)
    return pl.pallas_call(
        body, name="small_adam", in_specs=[vmem] * (n_t + 3 * n_w), out_specs=[vmem] * (4 * n_w),
        out_shape=[jax.ShapeDtypeStruct(w.shape, F32) for w in weights for _ in range(4)],
        compiler_params=_params(),
    )(*totals, *weights, *moms, *vels)


def _adam_math(w, g, m, v):
    m = ADAM_B1 * m + (1.0 - ADAM_B1) * g
    v = ADAM_B2 * v + (1.0 - ADAM_B2) * (g * g)
    m_hat = m / (1.0 - ADAM_B1 ** ADAM_STEP)
    v_hat = v / (1.0 - ADAM_B2 ** ADAM_STEP)
    delta = -ADAM_LR * (m_hat / (jnp.sqrt(v_hat) + ADAM_EPS) + ADAM_WD * w)
    return delta, m, v


def _adam(w, g, m, v, name, tr):
    rows, cols = w.shape

    def body(w_ref, g_ref, m_ref, v_ref, d_ref, nm_ref, nv_ref):
        d_ref[...], nm_ref[...], nv_ref[...] = _adam_math(w_ref[...], g_ref[...], m_ref[...], v_ref[...])

    spec = pl.BlockSpec((tr, cols), lambda i: (i, 0))
    shape = jax.ShapeDtypeStruct((rows, cols), F32)
    return pl.pallas_call(
        body, name=name, grid=(rows // tr,), in_specs=[spec] * 4, out_specs=[spec] * 3,
        out_shape=[shape] * 3, compiler_params=_params(("arbitrary",)),
    )(w, g, m, v)


def _sum_adam(parts, w, m, v, name, tr):
    _, rows, cols = parts.shape

    def body(p_ref, w_ref, m_ref, v_ref, g_ref, d_ref, nm_ref, nv_ref):
        g = p_ref[0].astype(F32)
        for dev in range(1, N_DEV):
            g = g + p_ref[dev].astype(F32)
        g_ref[...] = g
        d_ref[...], nm_ref[...], nv_ref[...] = _adam_math(w_ref[...], g, m_ref[...], v_ref[...])

    spec = pl.BlockSpec((tr, cols), lambda i: (i, 0))
    shape = jax.ShapeDtypeStruct((rows, cols), F32)
    return pl.pallas_call(
        body, name=name, grid=(rows // tr,),
        in_specs=[pl.BlockSpec((N_DEV, tr, cols), lambda i: (0, i, 0)), spec, spec, spec],
        out_specs=[spec] * 4, out_shape=[shape] * 4, compiler_params=_params(("arbitrary",)),
    )(parts, w, m, v)


def _ada_grad_adam(c_all, dmod_cols, w, m, v, tr):
    rows, cols = w.shape

    def body(c_ref, dm_ref, w_ref, m_ref, v_ref, g_ref, d_ref, nm_ref, nv_ref):
        cv = c_ref[...]
        act = cv * jax.nn.sigmoid(cv)
        g = lax.dot_general(act, dm_ref[...], TN, preferred_element_type=F32, precision=lax.Precision.HIGHEST)
        g_ref[...] = g
        d_ref[...], nm_ref[...], nv_ref[...] = _adam_math(w_ref[...], g, m_ref[...], v_ref[...])

    spec = pl.BlockSpec((tr, cols), lambda i: (i, 0))
    shape = jax.ShapeDtypeStruct((rows, cols), F32)
    return pl.pallas_call(
        body, name="ada_grad_adam", grid=(rows // tr,),
        in_specs=[pl.BlockSpec((N_DEV, tr), lambda i: (0, i)), pl.BlockSpec((N_DEV, cols), lambda i: (0, 0)), spec, spec, spec],
        out_specs=[spec] * 4, out_shape=[shape] * 4, compiler_params=_params(("arbitrary",)),
    )(c_all, dmod_cols, w, m, v)


def _rope_tables(positions):
    s_len = positions.shape[0]
    inv_freq = ROPE_THETA ** (-jnp.arange(0, 2 * ROT_HALF, 2, dtype=F32) / (2 * ROT_HALF))
    ang = positions.astype(F32)[:, None] * inv_freq
    cos, sin = jnp.cos(ang), jnp.sin(ang)
    rest = HEAD_DIM - 2 * ROT_HALF
    zero = lambda n: jnp.zeros((s_len, n), F32)
    head = jnp.stack([jnp.concatenate([cos, cos, jnp.ones((s_len, rest), F32)], axis=1),
                      jnp.concatenate([-sin, zero(HEAD_DIM - ROT_HALF)], axis=1),
                      jnp.concatenate([zero(ROT_HALF), sin, zero(rest)], axis=1)])
    return jnp.tile(head, (1, 1, LANES // HEAD_DIM))


def _pad_rows(a, rows):
    return jnp.pad(a, ((0, rows - a.shape[0]), (0, 0)))


def _sequence_step(xs, target, rope, mods, gains, w_in_t, w_out_t, relay_ffn, fetch_ffn, send_ffn_grads, send_mix_grads, w_blk_b, b_pool_r,
                   pool_scale_r, conv_w_all, conv_b):
    sh_m, sc_m, gt_m, sh_f, sc_f, gt_f = mods
    g_pre_mix, g_post_mix, g_pre_ffn, g_post_ffn = gains
    h1, u_pool, qkv = _premix_inproj(xs, sh_m, sc_m, g_pre_mix, w_in_t, rope, tm=512)
    o_g, lse_g = [], []
    for gi, dil in enumerate(DILATIONS):
        o, lse = _attn_fwd(qkv, gi, dil)
        o_g.append(o)
        lse_g.append(lse)
    token = relay_ffn(lse_g[-1])
    x1, y1, h2, cat, attn, lse_all = _mix_out(xs, u_pool, o_g, lse_g, w_blk_b, b_pool_r, pool_scale_r, w_out_t,
                                              gt_m if token is None else gt_m + token[0:1, 0:1],
                                              g_post_mix, g_pre_ffn, sc_f, sh_f, tm=256)
    w_up_t, w_down_f = fetch_ffn(x1)
    gate, a_ffn, act, vd, dy2, dout, sums_ffn, loss_loc = _ffn_fwd_loss(h2, x1, target, w_up_t, w_down_f, conv_w_all, conv_b,
                                                              gt_f, g_post_ffn, tm=256, tf=2816, ck=256)

    dgc, dval, dw_down, dconv = _ffn_bwd_act(dy2, gate, a_ffn, act, vd, w_down_f, tm=256, ck=256)
    dup, dh2 = _ffn_bwd_up(dgc, dval, w_up_t, conv_w_all, tm=256)
    dw_up_t = _wgrad(dup, h2, "wgrad_up", tk=2048, tmm=1408)
    token = send_ffn_grads(dw_up_t, dw_down)
    if token is not None:
        sc_f = sc_f + token[0:1, 0:1]
    dx1, dpool, dattn, delta, dw_out_t, sums_mix = _mix_bwd(dh2, dout, x1, y1, cat, attn, w_out_t, sc_f, g_pre_ffn,
                                                           gt_m, g_post_mix, tm=256)
    du, dw_blk, sums_pool = _pool_bwd(dpool, u_pool, w_blk_b, b_pool_r, pool_scale_r, tm=512)
    dqkv = []
    for gi, dil in enumerate(DILATIONS):
        dqkv += list(_attn_bwd(qkv, dattn, lse_all, delta, gi, dil))
    dproj = _dproj_assemble(du, dqkv, rope, tm=512)
    dw_in_t = _wgrad(dproj, h1, "wgrad_in", tk=2048, tmm=1280)
    token = send_mix_grads(dw_in_t, dw_out_t)
    if token is not None:
        sc_m = sc_m + token[0:1, 0:1]
    grad_x, sums_in = _inproj_bwd(dproj, w_in_t, xs, dx1, sc_m, g_pre_mix, tm=256)
    return (loss_loc, grad_x, dw_in_t, dw_out_t, dw_up_t, dw_down, dw_blk, dconv,
            sums_in, sums_mix, sums_ffn, sums_pool)


def kernel(x, c, positions, w_ada, b_ada, g_pre_mix, g_post_mix, g_pre_ffn, g_post_ffn, w_in, w_pool, b_pool, pool_scale, w_out, w_up, conv_w, conv_b, w_down, loss_target, m_w_ada, m_b_ada, m_g_pre_mix, m_g_post_mix, m_g_pre_ffn, m_g_post_ffn, m_w_in, m_w_pool, m_b_pool, m_pool_scale, m_w_out, m_w_up, m_conv_w, m_conv_b, m_w_down, v_w_ada, v_b_ada, v_g_pre_mix, v_g_post_mix, v_g_pre_ffn, v_g_post_ffn, v_w_in, v_w_pool, v_b_pool, v_pool_scale, v_w_out, v_w_up, v_conv_w, v_conv_b, v_w_down):
    s_len, d = x.shape[1], x.shape[2]
    d_ff = w_down.shape[1] * N_DEV
    me = _index(_place())
    xs, target = x[0], loss_target[0]

    ncol = w_ada.shape[2]
    b_cols = lax.dynamic_slice(b_ada, (0, me * ncol), (1, ncol))
    c_all, mod, taps_all = _ada_exchange(jnp.broadcast_to(c, (8, d)), w_ada[0], b_cols, _pad_rows(conv_w[0], 8))
    c_all = c_all[:, 0, :]
    conv_w_all = jnp.transpose(taps_all[:, :3, :], (1, 0, 2)).reshape(3, d_ff)
    sh_m, sc_m, gt_m, sh_f, sc_f, gt_f = [mod[:, 0, :].reshape(1, -1)[:, k * d:(k + 1) * d] for k in range(6)]

    w_in_t, w_out_t = _gather_weights([w_in[0].T.astype(BF16), w_out[0].T.astype(BF16)])

    rope = _rope_tables(positions[0])
    w_blk = jnp.zeros((256, 256), F32)
    for gi in range(4):
        w_blk = lax.dynamic_update_slice(w_blk, w_pool[0, gi], (gi * HEAD_DIM, gi * HEAD_DIM))
    w_blk_b = w_blk.astype(BF16)
    b_pool_r, pool_scale_r = b_pool.reshape(1, 256), pool_scale.reshape(1, 256)

    up_sh, down_sh = w_up[0].T.astype(BF16), w_down[0].astype(BF16)
    w_in_t, conv_w_all, up_sh, down_sh = lax.optimization_barrier((w_in_t, conv_w_all, up_sh, down_sh))
    lands = [_landing_zone("gather", s, me, "land_" + nm) for s, nm in ((up_sh, "w_up"), (down_sh, "w_down"))]
    w_send, w_recv, w_src, w_land, w_token = _exchange_start("gather_ici", [up_sh, down_sh], lands, "ffn_weights_ici_start")
    relay = []

    def relay_ffn(after):
        arrived = _exchange_wait("gather_ici", w_send, w_recv, w_src, w_land, after, "ffn_weights_ici_wait")
        relay.extend(_exchange_start("gather_d2d", [], arrived, "ffn_weights_d2d_start"))
        return relay[4]

    def fetch_ffn(after):
        return _exchange_wait("gather_d2d", relay[0], relay[1], [], relay[3], after, "ffn_weights_d2d_wait")

    flight = []

    def send_ffn_grads(dw_up_t, dw_down):
        lands = [_landing_zone("scatter", dw_up_t, me, "land_dw_up"), _landing_zone("scatter", dw_down, me, "land_dw_down")]
        flight.extend(_exchange_start("scatter", [dw_up_t, dw_down], lands, "ffn_grads_start"))
        return flight[4]

    mix_flight = []

    def send_mix_grads(dw_in_t, dw_out_t):
        lands = [_landing_zone("scatter", dw_in_t, me, "land_dw_in"), _landing_zone("scatter", dw_out_t, me, "land_dw_out")]
        mix_flight.extend(_exchange_start("scatter", [dw_in_t, dw_out_t], lands, "mix_grads_start"))
        return mix_flight[4]

    (loss_loc, grad_x, dw_in_t, dw_out_t, _, _, dw_pool, dconv,
     sums_in, sums_mix, sums_ffn, sums_pool) = _sequence_step(
        xs, target, rope, (sh_m + w_token[0:1, 0:1], sc_m, gt_m, sh_f, sc_f, gt_f),
        (g_pre_mix, g_post_mix, g_pre_ffn, g_post_ffn),
        w_in_t, w_out_t, relay_ffn, fetch_ffn, send_ffn_grads, send_mix_grads, w_blk_b, b_pool_r, pool_scale_r, conv_w_all, conv_b)

    small = [sums_in, sums_mix, sums_ffn, sums_pool, dw_pool, dconv, loss_loc]
    small_flight = _exchange_start("allgather", small, [lax.empty((N_DEV,) + a.shape, F32) for a in small], "small_start")

    parts_ffn = _exchange_wait("scatter", *flight[:4], small_flight[4], "ffn_grads_wait")
    big = {
        "w_up": [a.T for a in _sum_adam(parts_ffn[0], w_up[0].T, m_w_up[0].T, v_w_up[0].T, "adam_w_up", 64)],
        "w_down": _sum_adam(parts_ffn[1], w_down[0], m_w_down[0], v_w_down[0], "adam_w_down", 32),
    }

    rep_w = [b_ada, g_pre_mix, g_post_mix, g_pre_ffn, g_post_ffn, w_pool, b_pool, pool_scale, conv_b]
    rep_m = [m_b_ada, m_g_pre_mix, m_g_post_mix, m_g_pre_ffn, m_g_post_ffn, m_w_pool, m_b_pool, m_pool_scale, m_conv_b]
    rep_v = [v_b_ada, v_g_pre_mix, v_g_post_mix, v_g_pre_ffn, v_g_post_ffn, v_w_pool, v_b_pool, v_pool_scale, v_conv_b]
    gathered = _exchange_wait("allgather", *small_flight[:4], big["w_down"][0], "small_wait")
    totals, dmod_all = _small_sum(small, gathered)
    dconv_tot, loss_tot = totals[5], totals[6]
    rep_out = _small_adam(totals, rep_w, rep_m, rep_v)
    g_rep, d_rep, nm_rep, nv_rep = (rep_out[k::4] for k in range(4))

    fcol = d_ff // N_DEV
    g_cw = lax.dynamic_slice(dconv_tot, (0, me * fcol), (3, fcol))
    d_cw, nm_cw, nv_cw = _adam(conv_w[0], g_cw, m_conv_w[0], v_conv_w[0], "adam_conv_w", 3)

    dmod_cols = lax.dynamic_slice(dmod_all, (0, me * ncol), (N_DEV, ncol))
    g_ada, d_ada, nm_ada, nv_ada = _ada_grad_adam(c_all, dmod_cols, w_ada[0], m_w_ada[0], v_w_ada[0], 256)

    parts_mix = _exchange_wait("scatter", *mix_flight[:4], g_ada, "mix_grads_wait")
    big["w_in"] = [a.T for a in _sum_adam(parts_mix[0], w_in[0].T, m_w_in[0].T, v_w_in[0].T, "adam_w_in", 64)]
    big["w_out"] = [a.T for a in _sum_adam(parts_mix[1], w_out[0].T, m_w_out[0].T, v_w_out[0].T, "adam_w_out", 128)]

    loss = loss_tot[0, 0]

    def group(k):
        rep = (g_rep, d_rep, nm_rep, nv_rep)[k]
        ada = (g_ada, d_ada, nm_ada, nv_ada)[k][None]
        cw = (g_cw, d_cw, nm_cw, nv_cw)[k][None]
        return [ada, rep[0], rep[1], rep[2], rep[3], rep[4], big["w_in"][k][None], rep[5], rep[6], rep[7],
                big["w_out"][k][None], big["w_up"][k][None], cw, rep[8], big["w_down"][k][None]]

    return (loss, grad_x[None], *group(0), *group(1), *group(2), *group(3))
```

```python
import functools
import math

import jax
import jax.numpy as jnp
from jax import lax
from jax.experimental import pallas as pl
from jax.experimental.pallas import tpu as pltpu

F32 = jnp.float32
BF16 = jnp.bfloat16
MESH = pl.DeviceIdType.MESH

N_DEV = 8
HEAD_DIM = 64
ROT_HALF = 8
ROPE_THETA = 500000.0
POOL_WINDOWS = (2, 4, 8, 16)
DILATIONS = (1, 4, 16)
BLOCK = 128
NORM_EPS = 1e-6
HALO = 16
MASKED = -1e30
ATTN_FWD_UNROLL = 8
ATTN_BWD_UNROLL = 4

ADAM_LR = 0.001
ADAM_B1 = 0.9
ADAM_B2 = 0.999
ADAM_EPS = 1e-08
ADAM_WD = 0.01
ADAM_STEP = 10

V7X_VMEM_LIMIT = 56 * 1024 * 1024
LANES = 128

NT = (((1,), (1,)), ((), ()))
NN = (((1,), (0,)), ((), ()))
TN = (((0,), (0,)), ((), ()))


def _dot(a, b, dims):
    return lax.dot_general(a, b, dims, preferred_element_type=F32)


def _params(sem=None, vmem=V7X_VMEM_LIMIT):
    if sem is None:
        return pltpu.CompilerParams(vmem_limit_bytes=vmem)
    return pltpu.CompilerParams(dimension_semantics=sem, vmem_limit_bytes=vmem)


def _rstd(v):
    return lax.rsqrt(jnp.mean(v * v, axis=-1, keepdims=True) + NORM_EPS)


def _norm_bwd(dn, n, rstd):
    return rstd * (dn - n * jnp.mean(dn * n, axis=-1, keepdims=True))


def _rope_fwd(p, rope_ref):
    return p * rope_ref[0] + pltpu.roll(p, LANES - ROT_HALF, 1) * rope_ref[1] + pltpu.roll(p, ROT_HALF, 1) * rope_ref[2]


def _rope_bwd(dp, rope_ref):
    return dp * rope_ref[0] + pltpu.roll(dp * rope_ref[1], ROT_HALF, 1) + pltpu.roll(dp * rope_ref[2], LANES - ROT_HALF, 1)


def _gelu_parts(v):
    k2 = 2.0 * math.sqrt(2.0 / math.pi)
    c = 0.044715
    v2 = v * v
    s = jax.nn.sigmoid(v * (k2 + (k2 * c) * v2))
    g = v * s
    dg = s + g * (1.0 - s) * (k2 + (3.0 * k2 * c) * v2)
    return g, dg


def _halo_before(i, tile):
    return jnp.maximum(i * (tile // HALO) - 1, 0)


def _premix_inproj(x, sh, sc, g, w_in_t, rope, tm):
    s_len, d = x.shape
    n_proj = w_in_t.shape[0]
    n_slab = (n_proj - 256) // LANES

    def body(x_ref, sh_ref, sc_ref, g_ref, w_ref, rope_ref, h_ref, up_ref, qkv_ref):
        xv = x_ref[...]
        h = (xv * _rstd(xv) * g_ref[...]) * (1.0 + sc_ref[...]) + sh_ref[...]
        hb = h.astype(BF16)
        h_ref[...] = hb
        up_ref[...] = _dot(hb, w_ref[0:256, :], NT)
        for pair in range(n_slab // 2):
            p = _dot(hb, w_ref[256 + 256 * pair:512 + 256 * pair, :], NT)
            for half in range(2):
                ph = p[:, half * LANES:(half + 1) * LANES]
                if pair < 6:
                    ph = _rope_fwd(ph, rope_ref)
                if pair < 3:
                    ph = ph * (HEAD_DIM ** -0.5)
                qkv_ref[2 * pair + half] = ph

    vec = pl.BlockSpec((1, d), lambda i: (0, 0))
    return pl.pallas_call(
        body, name="premix_inproj", grid=(s_len // tm,),
        in_specs=[pl.BlockSpec((tm, d), lambda i: (i, 0)), vec, vec, vec,
                  pl.BlockSpec((n_proj, d), lambda i: (0, 0)),
                  pl.BlockSpec((3, tm, LANES), lambda i: (0, i, 0))],
        out_specs=[pl.BlockSpec((tm, d), lambda i: (i, 0)),
                   pl.BlockSpec((tm, 256), lambda i: (i, 0)),
                   pl.BlockSpec((n_slab, tm, LANES), lambda i: (0, i, 0))],
        out_shape=[jax.ShapeDtypeStruct((s_len, d), BF16),
                   jax.ShapeDtypeStruct((s_len, 256), F32),
                   jax.ShapeDtypeStruct((n_slab, s_len, LANES), F32)],
        compiler_params=_params(("arbitrary",)),
    )(x, sh, sc, g, w_in_t, rope)


def _block_rows(n, r, dil):
    start = n * (BLOCK * dil) + r
    if dil == 1:
        return pl.ds(pl.multiple_of(start, BLOCK), BLOCK)
    return pl.ds(start, BLOCK, stride=dil)


def _band_mask(n):
    ri = lax.broadcasted_iota(jnp.int32, (BLOCK, 2 * BLOCK), 0)
    cj = lax.broadcasted_iota(jnp.int32, (BLOCK, 2 * BLOCK), 1)
    cur = (cj >= BLOCK) & (cj - BLOCK <= ri)
    prev = (cj < BLOCK) & (cj >= ri) & (n > 0)
    return cur | prev


def _attn_fwd(qkv, group, dil):
    s_len = qkv.shape[1]
    nb = s_len // (BLOCK * dil)

    def body(q_ref, k_ref, v_ref, o_ref, lse_ref):
        lane = lax.broadcasted_iota(jnp.int32, (BLOCK, LANES), 1)
        first = lane < HEAD_DIM

        def block(t, carry):
            r, n = t // nb, t % nb
            cur = _block_rows(n, r, dil)
            prev = _block_rows(jnp.maximum(n - 1, 0), r, dil)
            q = q_ref[0, cur, :]
            kcat = jnp.concatenate([k_ref[0, prev, :], k_ref[0, cur, :]], axis=0).astype(BF16)
            vcat = jnp.concatenate([v_ref[0, prev, :], v_ref[0, cur, :]], axis=0).astype(BF16)
            valid = _band_mask(n)
            q2 = jnp.concatenate([jnp.where(first, q, 0.0), jnp.where(first, 0.0, q)], axis=0).astype(BF16)
            s = jnp.where(jnp.concatenate([valid, valid], axis=0), _dot(q2, kcat, NT), MASKED)
            m = jnp.max(s, axis=-1, keepdims=True)
            p = jnp.exp(s - m)
            den = jnp.sum(p, axis=-1, keepdims=True)
            o2 = _dot(p.astype(BF16), vcat, NN) / den
            lse2 = m + jnp.log(den)
            o_ref[0, cur, :] = jnp.where(first, o2[:BLOCK], o2[BLOCK:])
            lse_ref[0, cur, :] = jnp.where(first, lse2[:BLOCK], lse2[BLOCK:])
            return carry

        lax.fori_loop(0, nb * dil, block, 0, unroll=ATTN_FWD_UNROLL)

    def slab(base):
        return pl.BlockSpec((1, s_len, LANES), lambda s: (base + 2 * group + s, 0, 0))

    out = pl.BlockSpec((1, s_len, LANES), lambda s: (s, 0, 0))
    shape = jax.ShapeDtypeStruct((2, s_len, LANES), F32)
    return pl.pallas_call(
        body, name=f"attn_fwd_d{dil}", grid=(2,),
        in_specs=[slab(0), slab(6), slab(12)], out_specs=[out, out], out_shape=[shape, shape],
        compiler_params=_params(("arbitrary",)),
    )(qkv, qkv, qkv)


def _pool_mixed(u, halo, i, tm):
    ue = jnp.concatenate([halo, u], axis=0)
    s2 = ue + pltpu.roll(ue, 1, 0)
    s4 = s2 + pltpu.roll(s2, 2, 0)
    s8 = s4 + pltpu.roll(s4, 4, 0)
    s16 = s8 + pltpu.roll(s8, 8, 0)
    grp = lax.broadcasted_iota(jnp.int32, (tm, 256), 1) // HEAD_DIM
    pick = lambda a, b, c, e: jnp.where(grp == 0, a, jnp.where(grp == 1, b, jnp.where(grp == 2, c, e)))
    win_sum = pick(s2[HALO:], s4[HALO:], s8[HALO:], s16[HALO:])
    pos = (i * tm + lax.broadcasted_iota(jnp.int32, (tm, 256), 0)).astype(F32)
    count = jnp.minimum(pos + 1.0, pick(*[float(w) for w in POOL_WINDOWS]))
    return win_sum / count - u, count


def _mix_out(x, u_pool, o_g, lse_g, w_blk, b_pool, pool_scale, w_out_t, gt_m, g_post_mix, g_pre_ffn, sc_f, sh_f, tm):
    s_len, d = x.shape

    def body(x_ref, u_ref, uh_ref, o0, o1, o2, l0, l1, l2, wb_ref, bp_ref, ps_ref, wo_ref,
             gt_ref, g1_ref, g2_ref, sc_ref, sh_ref,
             x1_ref, y1_ref, h2_ref, cat_ref, attn_ref, lall_ref):
        i = pl.program_id(0)
        u = u_ref[...]
        halo = uh_ref[...] * (i > 0).astype(F32)
        mixed, _ = _pool_mixed(u, halo, i, tm)
        y = _dot(mixed.astype(BF16), wb_ref[...], NN) + bp_ref[...]
        pool = y * ps_ref[...]
        attn = []
        for s in range(2):
            la, lb, lc = l0[s], l1[s], l2[s]
            mx = jnp.maximum(jnp.maximum(la, lb), lc)
            ea, eb, ec = jnp.exp(la - mx), jnp.exp(lb - mx), jnp.exp(lc - mx)
            den = ea + eb + ec
            lall_ref[s] = mx + jnp.log(den)
            attn.append((ea / den) * o0[s] + (eb / den) * o1[s] + (ec / den) * o2[s])
        attn = jnp.concatenate(attn, axis=1)
        attn_ref[...] = attn
        cat = jnp.concatenate([pool, attn], axis=1).astype(BF16)
        cat_ref[...] = cat
        y1 = _dot(cat, wo_ref[...], NT)
        y1_ref[...] = y1.astype(BF16)
        x1 = x_ref[...] + gt_ref[...] * (y1 * _rstd(y1) * g1_ref[...])
        x1_ref[...] = x1
        h2 = (x1 * _rstd(x1) * g2_ref[...]) * (1.0 + sc_ref[...]) + sh_ref[...]
        h2_ref[...] = h2.astype(BF16)

    tile = lambda w: pl.BlockSpec((tm, w), lambda i: (i, 0))
    slab = pl.BlockSpec((2, tm, LANES), lambda i: (0, i, 0))
    const = lambda a: pl.BlockSpec(a.shape, lambda i: (0,) * a.ndim)
    return pl.pallas_call(
        body, name="mix_out", grid=(s_len // tm,),
        in_specs=[tile(d), tile(256), pl.BlockSpec((HALO, 256), lambda i: (_halo_before(i, tm), 0)),
                  slab, slab, slab, slab, slab, slab,
                  const(w_blk), const(b_pool), const(pool_scale), const(w_out_t),
                  const(gt_m), const(g_post_mix), const(g_pre_ffn), const(sc_f), const(sh_f)],
        out_specs=[tile(d), tile(d), tile(d), tile(512), tile(256), slab],
        out_shape=[jax.ShapeDtypeStruct((s_len, d), F32), jax.ShapeDtypeStruct((s_len, d), BF16),
                   jax.ShapeDtypeStruct((s_len, d), BF16), jax.ShapeDtypeStruct((s_len, 512), BF16),
                   jax.ShapeDtypeStruct((s_len, 256), F32), jax.ShapeDtypeStruct((2, s_len, LANES), F32)],
        compiler_params=_params(("arbitrary",)),
    )(x, u_pool, u_pool, *o_g, *lse_g, w_blk, b_pool, pool_scale, w_out_t, gt_m, g_post_mix, g_pre_ffn, sc_f, sh_f)


def _conv_gate(gate_ext, cw, cb):
    gc = gate_ext * cw[2:3, :] + pltpu.roll(gate_ext, 1, 0) * cw[1:2, :] + pltpu.roll(gate_ext, 2, 0) * cw[0:1, :]
    return gc[HALO:] + cb


def _ffn_fwd_loss(h2, x1, target, w_up_t, w_down, conv_w, conv_b, gt_f, g_post_ffn, tm, tf, ck):
    s_len, d = x1.shape
    d_ff = w_down.shape[0]
    n_f = d_ff // tf

    def body(h_ref, hh_ref, x1_ref, tgt_ref, wg_ref, wv_ref, wd_ref, cw_ref, cb_ref, gt_ref, g_ref,
             gate_ref, a_ref, act_ref, vd_ref, dy2_ref, dout_ref, sums_ref, loss_ref, acc_ref):
        i, j = pl.program_id(0), pl.program_id(1)

        @pl.when((i == 0) & (j == 0))
        def _():
            sums_ref[...] = jnp.zeros_like(sums_ref)
            loss_ref[...] = jnp.zeros_like(loss_ref)

        h = h_ref[...]
        h_ext = jnp.concatenate([hh_ref[...], h], axis=0)
        row = lax.broadcasted_iota(jnp.int32, (tm + HALO, ck), 0)
        no_halo = (row < HALO) & (i == 0)

        def up(c):
            cs = slice(c * ck, (c + 1) * ck)
            return jnp.where(no_halo, 0.0, _dot(h_ext, wg_ref[cs, :], NT)), _dot(h, wv_ref[cs, :], NT)

        part = None
        n_c = tf // ck
        nxt = up(0)
        for c in range(n_c):
            cs = slice(c * ck, (c + 1) * ck)
            gate_ext, val = nxt
            if c + 1 < n_c:
                nxt = up(c + 1)
            act, dact = _gelu_parts(_conv_gate(gate_ext, cw_ref[:, cs], cb_ref[:, cs]))
            a = (act * val).astype(BF16)
            gate_ref[:, cs] = gate_ext[HALO:].astype(BF16)
            a_ref[:, cs] = a
            act_ref[:, cs] = act.astype(BF16)
            vd_ref[:, cs] = (val * dact).astype(BF16)
            p = _dot(a, wd_ref[cs, :], NN)
            part = p if part is None else part + p

        @pl.when(j == 0)
        def _():
            acc_ref[...] = part

        @pl.when(j > 0)
        def _():
            acc_ref[...] += part

        @pl.when(j == n_f - 1)
        def _():
            y2 = acc_ref[...]
            rstd = _rstd(y2)
            n = y2 * rstd
            rn = n * g_ref[...]
            err = x1_ref[...] + gt_ref[...] * rn - tgt_ref[...]
            loss_ref[...] += 0.5 * jnp.sum(jnp.mean(err * err, axis=-1, keepdims=True), axis=0, keepdims=True)
            dout = err * (1.0 / d)
            dout_ref[...] = dout
            drn = dout * gt_ref[...]
            sums_ref[0:1, :] += jnp.sum(dout * rn, axis=0, keepdims=True)
            sums_ref[1:2, :] += jnp.sum(drn * n, axis=0, keepdims=True)
            dy2_ref[...] = _norm_bwd(drn * g_ref[...], n, rstd).astype(BF16)

    tok = lambda w: pl.BlockSpec((tm, w), lambda i, j: (i, 0))
    tokf = pl.BlockSpec((tm, tf), lambda i, j: (i, j))
    vec = pl.BlockSpec((1, d), lambda i, j: (0, 0))
    once = {"pipeline_mode": pl.Buffered(1)} if n_f == 1 else {}
    return pl.pallas_call(
        body, name="ffn_fwd_loss", grid=(s_len // tm, n_f),
        in_specs=[tok(d), pl.BlockSpec((HALO, d), lambda i, j: (_halo_before(i, tm), 0)), tok(d), tok(d),
                  pl.BlockSpec((tf, d), lambda i, j: (j, 0), **once),
                  pl.BlockSpec((tf, d), lambda i, j: (j + n_f, 0), **once),
                  pl.BlockSpec((tf, d), lambda i, j: (j, 0), **once),
                  pl.BlockSpec((3, tf), lambda i, j: (0, j)), pl.BlockSpec((1, tf), lambda i, j: (0, j)), vec, vec],
        out_specs=[tokf, tokf, tokf, tokf, tok(d), tok(d), pl.BlockSpec((8, d), lambda i, j: (0, 0)),
                   pl.BlockSpec((8, LANES), lambda i, j: (0, 0))],
        out_shape=[jax.ShapeDtypeStruct((s_len, d_ff), BF16)] * 4
        + [jax.ShapeDtypeStruct((s_len, d), BF16), jax.ShapeDtypeStruct((s_len, d), F32),
                   jax.ShapeDtypeStruct((8, d), F32), jax.ShapeDtypeStruct((8, LANES), F32)],
        scratch_shapes=[pltpu.VMEM((tm, d), F32)],
        compiler_params=_params(("arbitrary", "arbitrary")),
    )(h2, h2, x1, target, w_up_t, w_up_t, w_down, conv_w, conv_b, gt_f, g_post_ffn)


def _ffn_bwd_act(dy2, gate, a, act, vd, w_down, tm, ck):
    s_len, d = dy2.shape
    d_ff = w_down.shape[0]
    n_t, n_c = s_len // tm, d_ff // ck

    def body(dy_ref, g_ref, gh_ref, a_ref, act_ref, vd_ref, wd_ref, dgc_ref, dval_ref, dwd_ref, dconv_ref, acc_ref):
        i = pl.program_id(0)

        @pl.when(i == 0)
        def _():
            acc_ref[...] = jnp.zeros_like(acc_ref)
            dconv_ref[...] = jnp.zeros_like(dconv_ref)

        dy = dy_ref[...]
        row = lax.broadcasted_iota(jnp.int32, (tm + HALO, ck), 0)
        no_halo = (row < HALO) & (i == 0)

        def down(c):
            return _dot(dy, wd_ref[c * ck:(c + 1) * ck, :], NT)

        nxt = down(0)
        for c in range(n_c):
            cs = slice(c * ck, (c + 1) * ck)
            da = nxt
            if c + 1 < n_c:
                nxt = down(c + 1)
            acc_ref[cs, :] += _dot(a_ref[:, cs], dy, TN)
            gate_ext = jnp.where(no_halo, 0.0, jnp.concatenate([gh_ref[:, cs], g_ref[:, cs]], axis=0).astype(F32))
            dgc = da * vd_ref[:, cs].astype(F32)
            dgc_ref[:, cs] = dgc.astype(BF16)
            dval_ref[:, cs] = (da * act_ref[:, cs].astype(F32)).astype(BF16)
            rows = [jnp.sum(dgc * pltpu.roll(gate_ext, 2 - k, 0)[HALO:], axis=0, keepdims=True) for k in range(2)]
            rows += [jnp.sum(dgc * gate_ext[HALO:], axis=0, keepdims=True), jnp.sum(dgc, axis=0, keepdims=True),
                     jnp.zeros((4, ck), F32)]
            dconv_ref[:, cs] += jnp.concatenate(rows, axis=0)

        @pl.when(i == n_t - 1)
        def _():
            dwd_ref[...] = acc_ref[...].astype(BF16)

    tokf = pl.BlockSpec((tm, d_ff), lambda i: (i, 0))
    return pl.pallas_call(
        body, name="ffn_bwd_act", grid=(n_t,),
        in_specs=[pl.BlockSpec((tm, d), lambda i: (i, 0)), tokf,
                  pl.BlockSpec((HALO, d_ff), lambda i: (_halo_before(i, tm), 0)), tokf, tokf, tokf,
                  pl.BlockSpec((d_ff, d), lambda i: (0, 0), pipeline_mode=pl.Buffered(1))],
        out_specs=[tokf, tokf, pl.BlockSpec((d_ff, d), lambda i: (0, 0)), pl.BlockSpec((8, d_ff), lambda i: (0, 0))],
        out_shape=[jax.ShapeDtypeStruct((s_len, d_ff), BF16), jax.ShapeDtypeStruct((s_len, d_ff), BF16),
                   jax.ShapeDtypeStruct((d_ff, d), BF16), jax.ShapeDtypeStruct((8, d_ff), F32)],
        scratch_shapes=[pltpu.VMEM((d_ff, d), F32)],
        compiler_params=_params(("arbitrary",)),
    )(dy2, gate, gate, a, act, vd, w_down)


def _ffn_bwd_up(dgc, dval, w_up_t, conv_w, tm):
    s_len, d_ff = dgc.shape
    d = w_up_t.shape[1]
    n_t = s_len // tm

    def body(dg_ref, dgn_ref, dv_ref, cw_ref, w_ref, dup_ref, dh_ref):
        i = pl.program_id(0)
        nxt = dgn_ref[...].astype(F32) * (i < n_t - 1).astype(F32)
        ext = jnp.concatenate([dg_ref[...].astype(F32), nxt], axis=0)
        rows = tm + HALO
        dgate = (ext * cw_ref[2:3, :] + pltpu.roll(ext, rows - 1, 0) * cw_ref[1:2, :]
                 + pltpu.roll(ext, rows - 2, 0) * cw_ref[0:1, :])[:tm]
        dup = jnp.concatenate([dgate.astype(BF16), dv_ref[...]], axis=1)
        dup_ref[...] = dup
        dh_ref[...] = _dot(dup, w_ref[...], NN).astype(BF16)

    tokf = pl.BlockSpec((tm, d_ff), lambda i: (i, 0))
    return pl.pallas_call(
        body, name="ffn_bwd_up", grid=(n_t,),
        in_specs=[tokf, pl.BlockSpec((HALO, d_ff), lambda i: (jnp.minimum((i + 1) * (tm // HALO), s_len // HALO - 1), 0)),
                  tokf, pl.BlockSpec((3, d_ff), lambda i: (0, 0)), pl.BlockSpec((2 * d_ff, d), lambda i: (0, 0))],
        out_specs=[pl.BlockSpec((tm, 2 * d_ff), lambda i: (i, 0)), pl.BlockSpec((tm, d), lambda i: (i, 0))],
        out_shape=[jax.ShapeDtypeStruct((s_len, 2 * d_ff), BF16), jax.ShapeDtypeStruct((s_len, d), BF16)],
        compiler_params=_params(("arbitrary",)),
    )(dgc, dgc, dval, conv_w, w_up_t)


def _mix_bwd(dh2, dout, x1, y1, cat, attn, w_out_t, sc_f, g_pre_ffn, gt_m, g_post_mix, tm):
    s_len, d = x1.shape
    n_t = s_len // tm

    def body(dh_ref, do_ref, x1_ref, y1_ref, cat_ref, at_ref, wo_ref, sc_ref, g2_ref, gt_ref, g1_ref,
             dx1_ref, dpool_ref, dattn_ref, delta_ref, dwo_ref, sums_ref, acc_ref):
        i = pl.program_id(0)
        dh = dh_ref[...].astype(F32)
        x1 = x1_ref[...]
        r2 = _rstd(x1)
        n2 = x1 * r2
        ng = n2 * g2_ref[...]
        dng = dh * (1.0 + sc_ref[...])
        dx1 = do_ref[...] + _norm_bwd(dng * g2_ref[...], n2, r2)
        dx1_ref[...] = dx1
        y1 = y1_ref[...].astype(F32)
        r1 = _rstd(y1)
        n1 = y1 * r1
        drn = dx1 * gt_ref[...]
        dy1 = _norm_bwd(drn * g1_ref[...], n1, r1).astype(BF16)
        dcat = _dot(dy1, wo_ref[...], NN)
        dpool_ref[...] = dcat[:, 0:256]
        lane = lax.broadcasted_iota(jnp.int32, (tm, LANES), 1)
        first = lane < HEAD_DIM
        for s in range(2):
            da = dcat[:, 256 + s * LANES:256 + (s + 1) * LANES]
            dattn_ref[s] = da
            prod = da * at_ref[:, s * LANES:(s + 1) * LANES]
            tot = jnp.sum(prod, axis=-1, keepdims=True)
            lo = jnp.sum(jnp.where(first, prod, 0.0), axis=-1, keepdims=True)
            delta_ref[s] = jnp.where(first, lo, tot - lo)
        dwo = _dot(dy1, cat_ref[...], TN)
        sums = jnp.concatenate(
            [jnp.sum(dh, axis=0, keepdims=True), jnp.sum(dh * ng, axis=0, keepdims=True),
             jnp.sum(dng * n2, axis=0, keepdims=True), jnp.sum(dx1 * (n1 * g1_ref[...]), axis=0, keepdims=True),
             jnp.sum(drn * n1, axis=0, keepdims=True), jnp.zeros((3, d), F32)], axis=0)

        @pl.when(i == 0)
        def _():
            acc_ref[...] = dwo
            sums_ref[...] = sums

        @pl.when(i > 0)
        def _():
            acc_ref[...] += dwo
            sums_ref[...] += sums

        @pl.when(i == n_t - 1)
        def _():
            dwo_ref[...] = acc_ref[...].astype(BF16)

    tile = lambda w: pl.BlockSpec((tm, w), lambda i: (i, 0))
    slab = pl.BlockSpec((2, tm, LANES), lambda i: (0, i, 0))
    vec = pl.BlockSpec((1, d), lambda i: (0, 0))
    return pl.pallas_call(
        body, name="mix_bwd", grid=(n_t,),
        in_specs=[tile(d), tile(d), tile(d), tile(d), tile(512), tile(256),
                  pl.BlockSpec((d, 512), lambda i: (0, 0)), vec, vec, vec, vec],
        out_specs=[tile(d), tile(256), slab, slab, pl.BlockSpec((d, 512), lambda i: (0, 0)),
                   pl.BlockSpec((8, d), lambda i: (0, 0))],
        out_shape=[jax.ShapeDtypeStruct((s_len, d), F32), jax.ShapeDtypeStruct((s_len, 256), F32),
                   jax.ShapeDtypeStruct((2, s_len, LANES), F32), jax.ShapeDtypeStruct((2, s_len, LANES), F32),
                   jax.ShapeDtypeStruct((d, 512), BF16), jax.ShapeDtypeStruct((8, d), F32)],
        scratch_shapes=[pltpu.VMEM((d, 512), F32)],
        compiler_params=_params(("arbitrary",)),
    )(dh2, dout, x1, y1, cat, attn, w_out_t, sc_f, g_pre_ffn, gt_m, g_post_mix)


def _pool_bwd(dpool, u_pool, w_blk, b_pool, pool_scale, tm):
    s_len = dpool.shape[0]
    n_t = s_len // tm

    def body(dp_ref, dpn_ref, u_ref, uh_ref, wb_ref, bp_ref, ps_ref, du_ref, dwp_ref, sums_ref, acc_ref):
        i = pl.program_id(0)
        u = u_ref[...]
        mixed, _ = _pool_mixed(u, uh_ref[...] * (i > 0).astype(F32), i, tm)
        mixed_b = mixed.astype(BF16)
        y = _dot(mixed_b, wb_ref[...], NN) + bp_ref[...]
        dp = dp_ref[...]
        dy = dp * ps_ref[...]
        dwb = _dot(mixed_b, dy.astype(BF16), TN)
        sums = jnp.concatenate([jnp.sum(dy, axis=0, keepdims=True), jnp.sum(dp * y, axis=0, keepdims=True),
                                jnp.zeros((6, 256), F32)], axis=0)
        dp_ext = jnp.concatenate([dp, dpn_ref[...] * (i < n_t - 1).astype(F32)], axis=0)
        dmix = _dot((dp_ext * ps_ref[...]).astype(BF16), wb_ref[...], NT)
        rows = tm + HALO
        grp = lax.broadcasted_iota(jnp.int32, (rows, 256), 1) // HEAD_DIM
        pick = lambda a, b, c, e: jnp.where(grp == 0, a, jnp.where(grp == 1, b, jnp.where(grp == 2, c, e)))
        pos = (i * tm + lax.broadcasted_iota(jnp.int32, (rows, 256), 0)).astype(F32)
        z = dmix / jnp.minimum(pos + 1.0, pick(*[float(w) for w in POOL_WINDOWS]))
        f2 = z + pltpu.roll(z, rows - 1, 0)
        f4 = f2 + pltpu.roll(f2, rows - 2, 0)
        f8 = f4 + pltpu.roll(f4, rows - 4, 0)
        f16 = f8 + pltpu.roll(f8, rows - 8, 0)
        du_ref[...] = (pick(f2, f4, f8, f16) - dmix)[:tm]

        @pl.when(i == 0)
        def _():
            acc_ref[...] = dwb
            sums_ref[...] = sums

        @pl.when(i > 0)
        def _():
            acc_ref[...] += dwb
            sums_ref[...] += sums

        @pl.when(i == n_t - 1)
        def _():
            full = acc_ref[...]
            for gi in range(len(POOL_WINDOWS)):
                lo = gi * HEAD_DIM
                dwp_ref[gi] = full[lo:lo + HEAD_DIM, lo:lo + HEAD_DIM]

    n_g = len(POOL_WINDOWS)
    tile = pl.BlockSpec((tm, 256), lambda i: (i, 0))
    const = lambda a: pl.BlockSpec(a.shape, lambda i: (0,) * a.ndim)
    return pl.pallas_call(
        body, name="pool_bwd", grid=(n_t,),
        in_specs=[tile, pl.BlockSpec((HALO, 256), lambda i: (jnp.minimum((i + 1) * (tm // HALO), s_len // HALO - 1), 0)),
                  tile, pl.BlockSpec((HALO, 256), lambda i: (_halo_before(i, tm), 0)),
                  const(w_blk), const(b_pool), const(pool_scale)],
        out_specs=[tile, pl.BlockSpec((n_g, HEAD_DIM, HEAD_DIM), lambda i: (0, 0, 0)), pl.BlockSpec((8, 256), lambda i: (0, 0))],
        out_shape=[jax.ShapeDtypeStruct((s_len, 256), F32), jax.ShapeDtypeStruct((n_g, HEAD_DIM, HEAD_DIM), F32),
                   jax.ShapeDtypeStruct((8, 256), F32)],
        scratch_shapes=[pltpu.VMEM((256, 256), F32)],
        compiler_params=_params(("arbitrary",)),
    )(dpool, dpool, u_pool, u_pool, w_blk, b_pool, pool_scale)


def _attn_bwd(qkv, dattn, lse_all, delta, group, dil):
    s_len = qkv.shape[1]
    nb = s_len // (BLOCK * dil)

    def body(q_ref, k_ref, v_ref, do_ref, l_ref, dl_ref, dq_ref, dk_ref, dv_ref):
        lane = lax.broadcasted_iota(jnp.int32, (BLOCK, LANES), 1)
        first = lane < HEAD_DIM

        def block(t, carry):
            dk_part, dv_part = carry
            r, n = t // nb, t % nb
            cur = _block_rows(n, r, dil)
            prev = _block_rows(jnp.maximum(n - 1, 0), r, dil)
            q = q_ref[0, cur, :]
            do = do_ref[0, cur, :]
            lse = l_ref[0, cur, :]
            dlt = dl_ref[0, cur, :]
            kcat = jnp.concatenate([k_ref[0, prev, :], k_ref[0, cur, :]], axis=0).astype(BF16)
            vcat = jnp.concatenate([v_ref[0, prev, :], v_ref[0, cur, :]], axis=0).astype(BF16)
            valid = _band_mask(n)
            stack = lambda a: jnp.concatenate([jnp.where(first, a, 0.0), jnp.where(first, 0.0, a)], axis=0)
            rows2 = lambda a: jnp.concatenate([a[:, 0:1], a[:, HEAD_DIM:HEAD_DIM + 1]], axis=0)
            q2, do2 = stack(q).astype(BF16), stack(do).astype(BF16)
            valid2 = jnp.concatenate([valid, valid], axis=0)
            p = jnp.where(valid2, jnp.exp(_dot(q2, kcat, NT) - rows2(lse)), 0.0)
            ds = (p * (_dot(do2, vcat, NT) - rows2(dlt))).astype(BF16)
            dq2 = _dot(ds, kcat, NN)
            dq_ref[0, cur, :] = jnp.where(first, dq2[:BLOCK], dq2[BLOCK:])
            dkc = _dot(ds, q2, TN)
            dvc = _dot(p.astype(BF16), do2, TN)
            dk_ref[0, prev, :] = dk_part + dkc[:BLOCK]
            dv_ref[0, prev, :] = dv_part + dvc[:BLOCK]
            dk_ref[0, cur, :] = dkc[BLOCK:]
            dv_ref[0, cur, :] = dvc[BLOCK:]
            return dkc[BLOCK:], dvc[BLOCK:]

        def blocks(tt, carry):
            for u in range(ATTN_BWD_UNROLL):
                carry = block(tt * ATTN_BWD_UNROLL + u, carry)
            return carry

        zero = jnp.zeros((BLOCK, LANES), F32)
        lax.fori_loop(0, nb * dil // ATTN_BWD_UNROLL, blocks, (zero, zero))

    def slab(base):
        return pl.BlockSpec((1, s_len, LANES), lambda s: (base + 2 * group + s, 0, 0))

    one = pl.BlockSpec((1, s_len, LANES), lambda s: (s, 0, 0))
    shape = jax.ShapeDtypeStruct((2, s_len, LANES), F32)
    return pl.pallas_call(
        body, name=f"attn_bwd_d{dil}", grid=(2,),
        in_specs=[slab(0), slab(6), slab(12), one, one, one],
        out_specs=[one, one, one], out_shape=[shape, shape, shape],
        compiler_params=_params(("arbitrary",)),
    )(qkv, qkv, qkv, dattn, lse_all, delta)


def _dproj_assemble(du, dqkv, rope, tm):
    s_len = du.shape[0]
    n_proj = 256 + 18 * LANES

    def body(du_ref, *refs):
        dref, rope_ref, dproj_ref = refs[:9], refs[9], refs[10]
        dproj_ref[:, 0:256] = du_ref[...].astype(BF16)
        col = 256
        for kind in range(3):
            for grp in range(3):
                for s in range(2):
                    piece = dref[3 * grp + kind][s]
                    if kind < 2:
                        piece = _rope_bwd(piece, rope_ref)
                    if kind == 0:
                        piece = piece * (HEAD_DIM ** -0.5)
                    dproj_ref[:, col:col + LANES] = piece.astype(BF16)
                    col += LANES

    slab = pl.BlockSpec((2, tm, LANES), lambda i: (0, i, 0))
    return pl.pallas_call(
        body, name="dproj_assemble", grid=(s_len // tm,),
        in_specs=[pl.BlockSpec((tm, 256), lambda i: (i, 0))] + [slab] * 9 + [pl.BlockSpec((3, tm, LANES), lambda i: (0, i, 0))],
        out_specs=pl.BlockSpec((tm, n_proj), lambda i: (i, 0)),
        out_shape=jax.ShapeDtypeStruct((s_len, n_proj), BF16),
        compiler_params=_params(("arbitrary",)),
    )(du, *dqkv, rope)


def _inproj_bwd(dproj, w_in_t, x, dx1, sc_m, g_pre_mix, tm):
    s_len, d = x.shape
    n_proj = w_in_t.shape[0]
    n_t = s_len // tm

    def body(dproj_ref, w_ref, x_ref, dx1_ref, sc_ref, g_ref, dx_ref, sums_ref):
        i = pl.program_id(0)
        halves = [slice(0, tm // 2), slice(tm // 2, tm)]
        dhs = [_dot(dproj_ref[rs, :], w_ref[...], NN) for rs in halves]
        sums = None
        for rs, dh in zip(halves, dhs):
            xv = x_ref[rs, :]
            r = _rstd(xv)
            n = xv * r
            dng = dh * (1.0 + sc_ref[...])
            dx_ref[rs, :] = dx1_ref[rs, :] + _norm_bwd(dng * g_ref[...], n, r)
            part = jnp.concatenate([jnp.sum(dh, axis=0, keepdims=True), jnp.sum(dh * (n * g_ref[...]), axis=0, keepdims=True),
                                    jnp.sum(dng * n, axis=0, keepdims=True), jnp.zeros((5, d), F32)], axis=0)
            sums = part if sums is None else sums + part

        @pl.when(i == 0)
        def _():
            sums_ref[...] = sums

        @pl.when(i > 0)
        def _():
            sums_ref[...] += sums

    tile = lambda w: pl.BlockSpec((tm, w), lambda i: (i, 0))
    vec = pl.BlockSpec((1, d), lambda i: (0, 0))
    return pl.pallas_call(
        body, name="inproj_bwd", grid=(n_t,),
        in_specs=[tile(n_proj), pl.BlockSpec((n_proj, d), lambda i: (0, 0)), tile(d), tile(d), vec, vec],
        out_specs=[tile(d), pl.BlockSpec((8, d), lambda i: (0, 0))],
        out_shape=[jax.ShapeDtypeStruct((s_len, d), F32), jax.ShapeDtypeStruct((8, d), F32)],
        compiler_params=_params(("arbitrary",)),
    )(dproj, w_in_t, x, dx1, sc_m, g_pre_mix)


def _wgrad(a, b, name, tk, tmm):
    s_len, m = a.shape
    n = b.shape[1]
    n_k = s_len // tk

    def body(a_ref, b_ref, o_ref, acc_ref):
        k = pl.program_id(1)
        part = _dot(a_ref[...], b_ref[...], TN)

        @pl.when(k == 0)
        def _():
            acc_ref[...] = part

        @pl.when(k > 0)
        def _():
            acc_ref[...] += part

        @pl.when(k == n_k - 1)
        def _():
            o_ref[...] = acc_ref[...].astype(BF16)

    return pl.pallas_call(
        body, name=name, grid=(m // tmm, n_k),
        in_specs=[pl.BlockSpec((tk, tmm), lambda j, k: (k, j)), pl.BlockSpec((tk, n), lambda j, k: (k, 0))],
        out_specs=pl.BlockSpec((tmm, n), lambda j, k: (j, 0)),
        out_shape=jax.ShapeDtypeStruct((m, n), BF16),
        scratch_shapes=[pltpu.VMEM((tmm, n), F32)],
        compiler_params=_params(("arbitrary", "arbitrary")),
    )(a, b)


def _place():
    return lax.axis_index("x"), lax.axis_index("y"), lax.axis_index("c")


def _peer(k):
    x, y, c = _place()
    bx, by, bc = (k >> 2) & 1, (k >> 1) & 1, k & 1
    return (x ^ bx if bx else x, y ^ by if by else y, c ^ bc if bc else c)


def _index(pos):
    return 4 * pos[0] + 2 * pos[1] + pos[2]


def _ada_exchange(c_rows, w_ada, b_ada_cols, taps):
    d = c_rows.shape[1]
    ncol = w_ada.shape[1]

    def body(c_ref, w_ref, b_ref, t_ref, call_ref, mod_ref, tall_ref, stage_ref, send_sems, recv_sems):
        me = _index(_place())
        call_ref[me] = c_ref[...]
        tall_ref[me] = t_ref[...]

        def gather(k):
            return pltpu.make_async_remote_copy(
                src_ref=c_ref, dst_ref=call_ref.at[me], send_sem=send_sems.at[0, k - 1], recv_sem=recv_sems.at[0, k - 1],
                device_id=_peer(k), device_id_type=MESH)

        def gather_taps(k):
            return pltpu.make_async_remote_copy(
                src_ref=t_ref, dst_ref=tall_ref.at[me], send_sem=send_sems.at[2, k - 1], recv_sem=recv_sems.at[2, k - 1],
                device_id=_peer(k), device_id_type=MESH)

        for k in range(1, N_DEV):
            gather(k).start()
        for k in range(1, N_DEV):
            gather_taps(k).start()
        for k in range(1, N_DEV):
            gather(k).wait_recv()
        cv = jnp.concatenate([call_ref[b, 0:1, :] for b in range(N_DEV)], axis=0)
        act = cv * jax.nn.sigmoid(cv)
        mod = lax.dot_general(act, w_ref[...], NN, preferred_element_type=F32,
                              precision=lax.Precision.HIGHEST) + b_ref[...]
        for b in range(N_DEV):
            stage_ref[b] = jnp.broadcast_to(mod[b:b + 1, :], (8, ncol))
        mod_ref[me] = stage_ref[me]

        def scatter(k):
            return pltpu.make_async_remote_copy(
                src_ref=stage_ref.at[_index(_peer(k))], dst_ref=mod_ref.at[me],
                send_sem=send_sems.at[1, k - 1], recv_sem=recv_sems.at[1, k - 1],
                device_id=_peer(k), device_id_type=MESH)

        for k in range(1, N_DEV):
            scatter(k).start()
        for k in range(1, N_DEV):
            scatter(k).wait_recv()
        for k in range(1, N_DEV):
            gather_taps(k).wait_recv()
        for k in range(1, N_DEV):
            gather(k).wait_send()
            scatter(k).wait_send()
            gather_taps(k).wait_send()

    vmem = pl.BlockSpec(memory_space=pltpu.VMEM)
    return pl.pallas_call(
        body, name="ada_exchange",
        in_specs=[vmem] * 4, out_specs=[vmem] * 3,
        out_shape=[jax.ShapeDtypeStruct((N_DEV, 8, d), F32), jax.ShapeDtypeStruct((N_DEV, 8, ncol), F32),
                   jax.ShapeDtypeStruct((N_DEV,) + taps.shape, F32)],
        scratch_shapes=[pltpu.VMEM((N_DEV, 8, ncol), F32), pltpu.SemaphoreType.DMA((3, N_DEV - 1)),
                        pltpu.SemaphoreType.DMA((3, N_DEV - 1))],
        compiler_params=_params(),
    )(c_rows, w_ada, b_ada_cols, taps)


def _entry_exchange(c_rows, w_ada, b_ada_cols, taps, shards):
    d = c_rows.shape[1]
    ncol = w_ada.shape[1]
    n_w = len(shards)

    def body(c_ref, w_ref, b_ref, t_ref, *rest):
        srcs = rest[:n_w]
        call_ref, mod_ref, tall_ref = rest[n_w:n_w + 3]
        outs = rest[n_w + 3:2 * n_w + 3]
        stage_ref, s_send, s_recv, w_send, w_recv, local_sems = rest[2 * n_w + 3:]
        x, y, c = _place()
        here, sibling = (x, y, c), (x, y, 1 - c)
        chips = [(1 - x, y), (x, 1 - y), (1 - x, 1 - y)]
        me = _index(here)

        def small(kind, src, dst, k):
            return pltpu.make_async_remote_copy(src_ref=src, dst_ref=dst, send_sem=s_send.at[kind, k - 1],
                                                recv_sem=s_recv.at[kind, k - 1], device_id=_peer(k), device_id_type=MESH)

        gather = lambda k: small(0, c_ref, call_ref.at[me], k)
        scatter = lambda k: small(1, stage_ref.at[_index(_peer(k))], mod_ref.at[me], k)
        gather_taps = lambda k: small(2, t_ref, tall_ref.at[me], k)

        def rows(w, pos):
            r = shards[w].shape[0]
            return outs[w].at[pl.ds(pl.multiple_of(_index(pos) * r, 16), r), :]

        def block(k, w, pos, to, own=False):
            return pltpu.make_async_remote_copy(
                src_ref=srcs[w] if own else rows(w, pos), dst_ref=rows(w, pos),
                send_sem=w_send.at[k, w], recv_sem=w_recv.at[k, w], device_id=to, device_id_type=MESH)

        call_ref[me] = c_ref[...]
        tall_ref[me] = t_ref[...]
        for k in range(1, N_DEV):
            gather(k).start()
        for k in range(1, N_DEV):
            gather_taps(k).start()
        mine = [pltpu.make_async_copy(srcs[w], rows(w, here), local_sems.at[w]) for w in range(n_w)]
        for cp in mine:
            cp.start()
        first = [block(0, w, here, sibling, own=True) for w in range(n_w)]
        first += [block(1 + j, w, here, (*chip, c), own=True) for j, chip in enumerate(chips) for w in range(n_w)]
        for cp in first:
            cp.start()

        for k in range(1, N_DEV):
            gather(k).wait_recv()
        cv = jnp.concatenate([call_ref[b, 0:1, :] for b in range(N_DEV)], axis=0)
        act = cv * jax.nn.sigmoid(cv)
        mod = lax.dot_general(act, w_ref[...], NN, preferred_element_type=F32,
                              precision=lax.Precision.HIGHEST) + b_ref[...]
        for b in range(N_DEV):
            stage_ref[b] = jnp.broadcast_to(mod[b:b + 1, :], (8, ncol))
        mod_ref[me] = stage_ref[me]
        for k in range(1, N_DEV):
            scatter(k).start()

        passed = []
        for j, chip in enumerate(chips):
            for w in range(n_w):
                block(1 + j, w, (*chip, c), here).wait_recv()
                fwd = block(4 + j, w, (*chip, c), sibling)
                fwd.start()
                passed.append(fwd)
        for w in range(n_w):
            block(0, w, sibling, here).wait_recv()
        for j, chip in enumerate(chips):
            for w in range(n_w):
                block(4 + j, w, (*chip, 1 - c), here).wait_recv()
        for k in range(1, N_DEV):
            scatter(k).wait_recv()
            gather_taps(k).wait_recv()
        for cp in first + passed:
            cp.wait_send()
        for k in range(1, N_DEV):
            gather(k).wait_send()
            scatter(k).wait_send()
            gather_taps(k).wait_send()
        for cp in mine:
            cp.wait()

    vmem, hbm = pl.BlockSpec(memory_space=pltpu.VMEM), pl.BlockSpec(memory_space=pltpu.HBM)
    out = pl.pallas_call(
        body, name="entry_exchange",
        in_specs=[vmem] * 4 + [hbm] * n_w, out_specs=[vmem] * 3 + [hbm] * n_w,
        out_shape=[jax.ShapeDtypeStruct((N_DEV, 8, d), F32), jax.ShapeDtypeStruct((N_DEV, 8, ncol), F32),
                   jax.ShapeDtypeStruct((N_DEV,) + taps.shape, F32)]
        + [jax.ShapeDtypeStruct((N_DEV * s.shape[0], s.shape[1]), s.dtype) for s in shards],
        scratch_shapes=[pltpu.VMEM((N_DEV, 8, ncol), F32), pltpu.SemaphoreType.DMA((3, N_DEV - 1)),
                        pltpu.SemaphoreType.DMA((3, N_DEV - 1)), pltpu.SemaphoreType.DMA((N_DEV - 1, n_w)),
                        pltpu.SemaphoreType.DMA((N_DEV - 1, n_w)), pltpu.SemaphoreType.DMA((n_w,))],
        compiler_params=_params(),
    )(c_rows, w_ada, b_ada_cols, taps, *shards)
    return out[0], out[1], out[2], out[3:]


def _gather_weights(shards):
    n_w = len(shards)

    def body(*refs):
        srcs, outs = refs[:n_w], refs[n_w:2 * n_w]
        send_sems, recv_sems, local_sems = refs[2 * n_w:]
        x, y, c = _place()
        me, sibling = (x, y, c), (x, y, 1 - c)
        chips = [(1 - x, y), (x, 1 - y), (1 - x, 1 - y)]

        def rows(w, pos):
            r = shards[w].shape[0]
            return outs[w].at[pl.ds(pl.multiple_of(_index(pos) * r, 16), r), :]

        def copy(k, w, block, to, own=False):
            return pltpu.make_async_remote_copy(
                src_ref=srcs[w] if own else rows(w, block), dst_ref=rows(w, block),
                send_sem=send_sems.at[k, w], recv_sem=recv_sems.at[k, w], device_id=to, device_id_type=MESH)

        mine = [pltpu.make_async_copy(srcs[w], rows(w, me), local_sems.at[w]) for w in range(n_w)]
        for cp in mine:
            cp.start()
        first = [copy(0, w, me, sibling, own=True) for w in range(n_w)]
        first += [copy(1 + j, w, me, (*chip, c), own=True) for j, chip in enumerate(chips) for w in range(n_w)]
        for cp in first:
            cp.start()
        passed = []
        for j, chip in enumerate(chips):
            for w in range(n_w):
                copy(1 + j, w, (*chip, c), me).wait_recv()
                fwd = copy(4 + j, w, (*chip, c), sibling)
                fwd.start()
                passed.append(fwd)
        for w in range(n_w):
            copy(0, w, sibling, me).wait_recv()
        for j, chip in enumerate(chips):
            for w in range(n_w):
                copy(4 + j, w, (*chip, 1 - c), me).wait_recv()
        for cp in first + passed:
            cp.wait_send()
        for cp in mine:
            cp.wait()

    hbm = pl.BlockSpec(memory_space=pltpu.HBM)
    return pl.pallas_call(
        body, name="gather_weights",
        in_specs=[hbm] * n_w, out_specs=[hbm] * n_w,
        out_shape=[jax.ShapeDtypeStruct((N_DEV * s.shape[0], s.shape[1]), s.dtype) for s in shards],
        scratch_shapes=[pltpu.SemaphoreType.DMA((N_DEV - 1, n_w)), pltpu.SemaphoreType.DMA((N_DEV - 1, n_w)),
                        pltpu.SemaphoreType.DMA((n_w,))],
        compiler_params=_params(),
    )(*shards)


def _peer_copies(mode, srcs, lands, send_sems, recv_sems):
    if mode in ("gather_ici", "gather_d2d"):
        x, y, c = _place()
        sibling = (x, y, 1 - c)
        chips = [(1 - x, y), (x, 1 - y), (1 - x, 1 - y)]
        n = len(lands)

        def rows(w, pos):
            r = lands[w].shape[0] // N_DEV
            return lands[w].at[pl.ds(pl.multiple_of(_index(pos) * r, 16), r), :]

        def copy(k, w, src, dst, to):
            return pltpu.make_async_remote_copy(src_ref=src, dst_ref=dst, send_sem=send_sems.at[k * n + w],
                                                recv_sem=recv_sems.at[k * n + w], device_id=to, device_id_type=MESH)

        if mode == "gather_ici":
            targets = [sibling] + [(*chip, c) for chip in chips]
            return [copy(k, w, srcs[w], rows(w, (x, y, c)), to) for k, to in enumerate(targets) for w in range(n)]
        return [copy(j, w, rows(w, (*chip, c)), rows(w, (*chip, c)), sibling)
                for j, chip in enumerate(chips) for w in range(n)]
    me = _index(_place())
    copies = []
    for k in range(1, N_DEV):
        peer = _peer(k)
        for w, (src, land) in enumerate(zip(srcs, lands)):
            if mode == "gather":
                r = src.shape[0]
                dst = land.at[pl.ds(pl.multiple_of(me * r, 16), r), :]
            elif mode == "allgather":
                dst = land.at[me]
            else:
                r = src.shape[0] // N_DEV
                src = src.at[pl.ds(pl.multiple_of(_index(peer) * r, 16), r), :]
                dst = land.at[me]
            copies.append(pltpu.make_async_remote_copy(
                src_ref=src, dst_ref=dst, send_sem=send_sems.at[(k - 1) * len(srcs) + w],
                recv_sem=recv_sems.at[(k - 1) * len(srcs) + w],
                device_id=peer, device_id_type=MESH))
    return copies


def _landing_zone(mode, src, me, name):
    cols = src.shape[1]
    if mode == "gather":
        r = src.shape[0]
        in_spec = pl.BlockSpec((r, cols), lambda i, me_ref: (0, 0))
        out_spec = pl.BlockSpec((r, cols), lambda i, me_ref: (me_ref[0], 0))
        out_shape = jax.ShapeDtypeStruct((N_DEV * r, cols), src.dtype)
    else:
        r = src.shape[0] // N_DEV
        in_spec = pl.BlockSpec((r, cols), lambda i, me_ref: (me_ref[0], 0))
        out_spec = pl.BlockSpec((1, r, cols), lambda i, me_ref: (me_ref[0], 0, 0))
        out_shape = jax.ShapeDtypeStruct((N_DEV, r, cols), src.dtype)

    def body(me_ref, s_ref, o_ref):
        o_ref[...] = s_ref[...].reshape(o_ref.shape)

    return pl.pallas_call(
        body, name=name, out_shape=out_shape,
        grid_spec=pltpu.PrefetchScalarGridSpec(num_scalar_prefetch=1, grid=(1,), in_specs=[in_spec], out_specs=out_spec),
        compiler_params=_params(("arbitrary",)),
    )(me.reshape(1).astype(jnp.int32), src)


def _exchange_start(mode, srcs, lands, name):
    n_s, n_a = len(srcs), len(srcs) + len(lands)
    n_cp = _COPIES_PER_ARRAY.get(mode, N_DEV - 1) * len(lands)

    def body(*refs):
        for cp in _peer_copies(mode, refs[:n_s], refs[n_s:n_a], refs[n_a], refs[n_a + 1]):
            cp.start()
        refs[-1][...] = jnp.zeros_like(refs[-1])

    hbm, sem = pl.BlockSpec(memory_space=pltpu.HBM), pl.BlockSpec(memory_space=pltpu.SEMAPHORE)
    arrays = list(srcs) + list(lands)
    out = pl.pallas_call(
        body, name=name,
        out_shape=(pltpu.SemaphoreType.DMA((n_cp,)), pltpu.SemaphoreType.DMA((n_cp,)),
                   *[pltpu.HBM(a.shape, a.dtype) for a in arrays], jax.ShapeDtypeStruct((8, LANES), F32)),
        in_specs=[hbm] * n_a, out_specs=(sem, sem, *[hbm] * n_a, pl.BlockSpec(memory_space=pltpu.VMEM)),
        input_output_aliases={i: 2 + i for i in range(n_a)},
        compiler_params=pltpu.CompilerParams(has_side_effects=pltpu.SideEffectType.DATAFLOW_SIDE_EFFECTING),
    )(*[pltpu.with_memory_space_constraint(a, pltpu.HBM) for a in arrays])
    return out[0], out[1], out[2:2 + n_s], out[2 + n_s:2 + n_a], out[-1]


_COPIES_PER_ARRAY = {"gather_ici": 4, "gather_d2d": 3}


def _exchange_wait(mode, send_sems, recv_sems, srcs, lands, after, name):
    n_s, n_a = len(srcs), len(srcs) + len(lands)

    def body(*refs):
        copies = _peer_copies(mode, refs[:n_s], refs[n_s:n_a], refs[n_a], refs[n_a + 1])
        for cp in copies:
            cp.wait_send()
        for cp in copies:
            cp.wait_recv()

    hbm, sem = pl.BlockSpec(memory_space=pltpu.HBM), pl.BlockSpec(memory_space=pltpu.SEMAPHORE)
    arrays = list(srcs) + list(lands)
    out = pl.pallas_call(
        body, name=name, out_shape=tuple(pltpu.HBM(a.shape, a.dtype) for a in arrays),
        in_specs=[hbm] * n_a + [sem, sem, pl.BlockSpec(memory_space=pl.ANY)], out_specs=tuple([hbm] * n_a),
        input_output_aliases={i: i for i in range(n_a)},
        compiler_params=pltpu.CompilerParams(has_side_effects=pltpu.SideEffectType.DATAFLOW_SIDE_EFFECTING),
    )(*arrays, send_sems, recv_sems, after)
    return out[n_s:]


SMALL_WEIGHTS = ("b_ada", "g_pre_mix", "g_post_mix", "g_pre_ffn", "g_post_ffn", "w_pool", "b_pool", "pool_scale", "conv_b")


MOD_ROWS = ((0, 0), (0, 1), (1, 3), (1, 0), (1, 1), (2, 0))


def _small_sum(mine, gathered):
    n_l = len(mine)
    d = mine[0].shape[1]

    def body(*refs):
        loc, got = refs[:n_l], refs[n_l:2 * n_l]
        tot_refs, dmod_ref = refs[2 * n_l:3 * n_l], refs[3 * n_l]
        me = _index(_place())
        part = lambda a, dev: jnp.where(dev == me, loc[a][...], got[a][dev])
        for a in range(n_l):
            tot = part(a, 0)
            for dev in range(1, N_DEV):
                tot = tot + part(a, dev)
            tot_refs[a][...] = tot
        for dev in range(N_DEV):
            for k, (a, r) in enumerate(MOD_ROWS):
                dmod_ref[dev:dev + 1, k * d:(k + 1) * d] = part(a, dev)[r:r + 1, :]

    vmem = pl.BlockSpec(memory_space=pltpu.VMEM)
    out = pl.pallas_call(
        body, name="small_sum", in_specs=[vmem] * (2 * n_l), out_specs=[vmem] * (n_l + 1),
        out_shape=[jax.ShapeDtypeStruct(a.shape, F32) for a in mine] + [jax.ShapeDtypeStruct((N_DEV, 6 * d), F32)],
        compiler_params=_params(),
    )(*mine, *gathered)
    return out[:n_l], out[n_l]


def _small_adam(totals, weights, moms, vels):
    n_t, n_w = len(totals), len(weights)

    def body(*refs):
        t_in, t_mix, t_ffn, t_pool, t_blk, t_conv, _ = (r[...] for r in refs[:n_t])
        w_refs, m_refs, v_refs = (refs[n_t + k * n_w:n_t + (k + 1) * n_w] for k in range(3))
        outs = refs[n_t + 3 * n_w:]

        def update(idx, g, at=()):
            sel = lambda ref: ref.at[at] if at else ref
            delta, nm, nv = _adam_math(sel(w_refs[idx])[...], g, sel(m_refs[idx])[...], sel(v_refs[idx])[...])
            for k, val in enumerate((g, delta, nm, nv)):
                sel(outs[4 * idx + k])[...] = val

        tots = (t_in, t_mix, t_ffn)
        update(0, jnp.concatenate([tots[a][r:r + 1] for a, r in MOD_ROWS], axis=1))
        update(1, t_in[2:3])
        update(2, t_mix[4:5])
        update(3, t_mix[2:3])
        update(4, t_ffn[1:2])
        for gi in range(len(POOL_WINDOWS)):
            update(5, t_blk[gi], at=(0, gi))
        update(6, jnp.concatenate([t_pool[0:1, gi * HEAD_DIM:(gi + 1) * HEAD_DIM] for gi in range(len(POOL_WINDOWS))], axis=0),
               at=(0,))
        update(7, t_pool[1:2])
        update(8, t_conv[3:4])

    vmem = pl.BlockSpec(memory_space=pltpu.VMEM)
    return pl.pallas_call(
        body, name="small_adam", in_specs=[vmem] * (n_t + 3 * n_w), out_specs=[vmem] * (4 * n_w),
        out_shape=[jax.ShapeDtypeStruct(w.shape, F32) for w in weights for _ in range(4)],
        compiler_params=_params(),
    )(*totals, *weights, *moms, *vels)


def _adam_math(w, g, m, v):
    m = ADAM_B1 * m + (1.0 - ADAM_B1) * g
    v = ADAM_B2 * v + (1.0 - ADAM_B2) * (g * g)
    m_hat = m / (1.0 - ADAM_B1 ** ADAM_STEP)
    v_hat = v / (1.0 - ADAM_B2 ** ADAM_STEP)
    delta = -ADAM_LR * (m_hat / (jnp.sqrt(v_hat) + ADAM_EPS) + ADAM_WD * w)
    return delta, m, v


def _adam(w, g, m, v, name, tr):
    rows, cols = w.shape

    def body(w_ref, g_ref, m_ref, v_ref, d_ref, nm_ref, nv_ref):
        d_ref[...], nm_ref[...], nv_ref[...] = _adam_math(w_ref[...], g_ref[...], m_ref[...], v_ref[...])

    spec = pl.BlockSpec((tr, cols), lambda i: (i, 0))
    shape = jax.ShapeDtypeStruct((rows, cols), F32)
    return pl.pallas_call(
        body, name=name, grid=(rows // tr,), in_specs=[spec] * 4, out_specs=[spec] * 3,
        out_shape=[shape] * 3, compiler_params=_params(("arbitrary",)),
    )(w, g, m, v)


def _sum_adam(parts, w, m, v, name, tr):
    _, rows, cols = parts.shape

    def body(p_ref, w_ref, m_ref, v_ref, g_ref, d_ref, nm_ref, nv_ref):
        g = p_ref[0].astype(F32)
        for dev in range(1, N_DEV):
            g = g + p_ref[dev].astype(F32)
        g_ref[...] = g
        d_ref[...], nm_ref[...], nv_ref[...] = _adam_math(w_ref[...], g, m_ref[...], v_ref[...])

    spec = pl.BlockSpec((tr, cols), lambda i: (i, 0))
    shape = jax.ShapeDtypeStruct((rows, cols), F32)
    return pl.pallas_call(
        body, name=name, grid=(rows // tr,),
        in_specs=[pl.BlockSpec((N_DEV, tr, cols), lambda i: (0, i, 0)), spec, spec, spec],
        out_specs=[spec] * 4, out_shape=[shape] * 4, compiler_params=_params(("arbitrary",)),
    )(parts, w, m, v)


def _ada_grad_adam(c_all, dmod_cols, w, m, v, tr):
    rows, cols = w.shape

    def body(c_ref, dm_ref, w_ref, m_ref, v_ref, g_ref, d_ref, nm_ref, nv_ref):
        cv = c_ref[...]
        act = cv * jax.nn.sigmoid(cv)
        g = lax.dot_general(act, dm_ref[...], TN, preferred_element_type=F32, precision=lax.Precision.HIGHEST)
        g_ref[...] = g
        d_ref[...], nm_ref[...], nv_ref[...] = _adam_math(w_ref[...], g, m_ref[...], v_ref[...])

    spec = pl.BlockSpec((tr, cols), lambda i: (i, 0))
    shape = jax.ShapeDtypeStruct((rows, cols), F32)
    return pl.pallas_call(
        body, name="ada_grad_adam", grid=(rows // tr,),
        in_specs=[pl.BlockSpec((N_DEV, tr), lambda i: (0, i)), pl.BlockSpec((N_DEV, cols), lambda i: (0, 0)), spec, spec, spec],
        out_specs=[spec] * 4, out_shape=[shape] * 4, compiler_params=_params(("arbitrary",)),
    )(c_all, dmod_cols, w, m, v)


def _rope_tables(positions):
    s_len = positions.shape[0]
    inv_freq = ROPE_THETA ** (-jnp.arange(0, 2 * ROT_HALF, 2, dtype=F32) / (2 * ROT_HALF))
    ang = positions.astype(F32)[:, None] * inv_freq
    cos, sin = jnp.cos(ang), jnp.sin(ang)
    rest = HEAD_DIM - 2 * ROT_HALF
    zero = lambda n: jnp.zeros((s_len, n), F32)
    head = jnp.stack([jnp.concatenate([cos, cos, jnp.ones((s_len, rest), F32)], axis=1),
                      jnp.concatenate([-sin, zero(HEAD_DIM - ROT_HALF)], axis=1),
                      jnp.concatenate([zero(ROT_HALF), sin, zero(rest)], axis=1)])
    return jnp.tile(head, (1, 1, LANES // HEAD_DIM))


def _pad_rows(a, rows):
    return jnp.pad(a, ((0, rows - a.shape[0]), (0, 0)))


def _sequence_step(xs, target, rope, mods, gains, w_in_t, w_out_t, relay_ffn, fetch_ffn, send_ffn_grads, send_mix_grads, w_blk_b, b_pool_r,
                   pool_scale_r, conv_w_all, conv_b):
    sh_m, sc_m, gt_m, sh_f, sc_f, gt_f = mods
    g_pre_mix, g_post_mix, g_pre_ffn, g_post_ffn = gains
    h1, u_pool, qkv = _premix_inproj(xs, sh_m, sc_m, g_pre_mix, w_in_t, rope, tm=512)
    o_g, lse_g = [], []
    for gi, dil in enumerate(DILATIONS):
        o, lse = _attn_fwd(qkv, gi, dil)
        o_g.append(o)
        lse_g.append(lse)
    token = relay_ffn(lse_g[-1])
    x1, y1, h2, cat, attn, lse_all = _mix_out(xs, u_pool, o_g, lse_g, w_blk_b, b_pool_r, pool_scale_r, w_out_t,
                                              gt_m if token is None else gt_m + token[0:1, 0:1],
                                              g_post_mix, g_pre_ffn, sc_f, sh_f, tm=256)
    w_up_t, w_down_f = fetch_ffn(x1)
    gate, a_ffn, act, vd, dy2, dout, sums_ffn, loss_loc = _ffn_fwd_loss(h2, x1, target, w_up_t, w_down_f, conv_w_all, conv_b,
                                                              gt_f, g_post_ffn, tm=256, tf=2816, ck=256)

    dgc, dval, dw_down, dconv = _ffn_bwd_act(dy2, gate, a_ffn, act, vd, w_down_f, tm=256, ck=256)
    dup, dh2 = _ffn_bwd_up(dgc, dval, w_up_t, conv_w_all, tm=256)
    dw_up_t = _wgrad(dup, h2, "wgrad_up", tk=2048, tmm=1408)
    token = send_ffn_grads(dw_up_t, dw_down)
    if token is not None:
        sc_f = sc_f + token[0:1, 0:1]
    dx1, dpool, dattn, delta, dw_out_t, sums_mix = _mix_bwd(dh2, dout, x1, y1, cat, attn, w_out_t, sc_f, g_pre_ffn,
                                                           gt_m, g_post_mix, tm=256)
    du, dw_blk, sums_pool = _pool_bwd(dpool, u_pool, w_blk_b, b_pool_r, pool_scale_r, tm=512)
    dqkv = []
    for gi, dil in enumerate(DILATIONS):
        dqkv += list(_attn_bwd(qkv, dattn, lse_all, delta, gi, dil))
    dproj = _dproj_assemble(du, dqkv, rope, tm=512)
    dw_in_t = _wgrad(dproj, h1, "wgrad_in", tk=2048, tmm=1280)
    token = send_mix_grads(dw_in_t, dw_out_t)
    if token is not None:
        sc_m = sc_m + token[0:1, 0:1]
    grad_x, sums_in = _inproj_bwd(dproj, w_in_t, xs, dx1, sc_m, g_pre_mix, tm=256)
    return (loss_loc, grad_x, dw_in_t, dw_out_t, dw_up_t, dw_down, dw_blk, dconv,
            sums_in, sums_mix, sums_ffn, sums_pool)


def kernel(x, c, positions, w_ada, b_ada, g_pre_mix, g_post_mix, g_pre_ffn, g_post_ffn, w_in, w_pool, b_pool, pool_scale, w_out, w_up, conv_w, conv_b, w_down, loss_target, m_w_ada, m_b_ada, m_g_pre_mix, m_g_post_mix, m_g_pre_ffn, m_g_post_ffn, m_w_in, m_w_pool, m_b_pool, m_pool_scale, m_w_out, m_w_up, m_conv_w, m_conv_b, m_w_down, v_w_ada, v_b_ada, v_g_pre_mix, v_g_post_mix, v_g_pre_ffn, v_g_post_ffn, v_w_in, v_w_pool, v_b_pool, v_pool_scale, v_w_out, v_w_up, v_conv_w, v_conv_b, v_w_down):
    s_len, d = x.shape[1], x.shape[2]
    d_ff = w_down.shape[1] * N_DEV
    me = _index(_place())
    xs, target = x[0], loss_target[0]

    ncol = w_ada.shape[2]
    b_cols = lax.dynamic_slice(b_ada, (0, me * ncol), (1, ncol))
    c_all, mod, taps_all, (w_in_t, w_out_t) = _entry_exchange(
        jnp.broadcast_to(c, (8, d)), w_ada[0], b_cols, _pad_rows(conv_w[0], 8),
        [w_in[0].T.astype(BF16), w_out[0].T.astype(BF16)])
    c_all = c_all[:, 0, :]
    conv_w_all = jnp.transpose(taps_all[:, :3, :], (1, 0, 2)).reshape(3, d_ff)
    sh_m, sc_m, gt_m, sh_f, sc_f, gt_f = [mod[:, 0, :].reshape(1, -1)[:, k * d:(k + 1) * d] for k in range(6)]

    rope = _rope_tables(positions[0])
    w_blk = jnp.zeros((256, 256), F32)
    for gi in range(4):
        w_blk = lax.dynamic_update_slice(w_blk, w_pool[0, gi], (gi * HEAD_DIM, gi * HEAD_DIM))
    w_blk_b = w_blk.astype(BF16)
    b_pool_r, pool_scale_r = b_pool.reshape(1, 256), pool_scale.reshape(1, 256)

    up_sh, down_sh = w_up[0].T.astype(BF16), w_down[0].astype(BF16)
    w_in_t, conv_w_all, up_sh, down_sh = lax.optimization_barrier((w_in_t, conv_w_all, up_sh, down_sh))
    lands = [_landing_zone("gather", s, me, "land_" + nm) for s, nm in ((up_sh, "w_up"), (down_sh, "w_down"))]
    w_send, w_recv, w_src, w_land, w_token = _exchange_start("gather_ici", [up_sh, down_sh], lands, "ffn_weights_ici_start")
    relay = []

    def relay_ffn(after):
        arrived = _exchange_wait("gather_ici", w_send, w_recv, w_src, w_land, after, "ffn_weights_ici_wait")
        relay.extend(_exchange_start("gather_d2d", [], arrived, "ffn_weights_d2d_start"))
        return relay[4]

    def fetch_ffn(after):
        return _exchange_wait("gather_d2d", relay[0], relay[1], [], relay[3], after, "ffn_weights_d2d_wait")

    flight = []

    def send_ffn_grads(dw_up_t, dw_down):
        lands = [_landing_zone("scatter", dw_up_t, me, "land_dw_up"), _landing_zone("scatter", dw_down, me, "land_dw_down")]
        flight.extend(_exchange_start("scatter", [dw_up_t, dw_down], lands, "ffn_grads_start"))
        return flight[4]

    mix_flight = []

    def send_mix_grads(dw_in_t, dw_out_t):
        lands = [_landing_zone("scatter", dw_in_t, me, "land_dw_in"), _landing_zone("scatter", dw_out_t, me, "land_dw_out")]
        mix_flight.extend(_exchange_start("scatter", [dw_in_t, dw_out_t], lands, "mix_grads_start"))
        return mix_flight[4]

    (loss_loc, grad_x, dw_in_t, dw_out_t, _, _, dw_pool, dconv,
     sums_in, sums_mix, sums_ffn, sums_pool) = _sequence_step(
        xs, target, rope, (sh_m + w_token[0:1, 0:1], sc_m, gt_m, sh_f, sc_f, gt_f),
        (g_pre_mix, g_post_mix, g_pre_ffn, g_post_ffn),
        w_in_t, w_out_t, relay_ffn, fetch_ffn, send_ffn_grads, send_mix_grads, w_blk_b, b_pool_r, pool_scale_r, conv_w_all, conv_b)

    small = [sums_in, sums_mix, sums_ffn, sums_pool, dw_pool, dconv, loss_loc]
    small_flight = _exchange_start("allgather", small, [lax.empty((N_DEV,) + a.shape, F32) for a in small], "small_start")

    parts_ffn = _exchange_wait("scatter", *flight[:4], small_flight[4], "ffn_grads_wait")
    big = {
        "w_up": [a.T for a in _sum_adam(parts_ffn[0], w_up[0].T, m_w_up[0].T, v_w_up[0].T, "adam_w_up", 64)],
        "w_down": _sum_adam(parts_ffn[1], w_down[0], m_w_down[0], v_w_down[0], "adam_w_down", 32),
    }

    rep_w = [b_ada, g_pre_mix, g_post_mix, g_pre_ffn, g_post_ffn, w_pool, b_pool, pool_scale, conv_b]
    rep_m = [m_b_ada, m_g_pre_mix, m_g_post_mix, m_g_pre_ffn, m_g_post_ffn, m_w_pool, m_b_pool, m_pool_scale, m_conv_b]
    rep_v = [v_b_ada, v_g_pre_mix, v_g_post_mix, v_g_pre_ffn, v_g_post_ffn, v_w_pool, v_b_pool, v_pool_scale, v_conv_b]
    parts_mix = _exchange_wait("scatter", *mix_flight[:4], big["w_down"][0], "mix_grads_wait")
    big["w_in"] = [a.T for a in _sum_adam(parts_mix[0], w_in[0].T, m_w_in[0].T, v_w_in[0].T, "adam_w_in", 64)]
    big["w_out"] = [a.T for a in _sum_adam(parts_mix[1], w_out[0].T, m_w_out[0].T, v_w_out[0].T, "adam_w_out", 128)]
    gathered = _exchange_wait("allgather", *small_flight[:4], big["w_out"][0], "small_wait")
    totals, dmod_all = _small_sum(small, gathered)
    dconv_tot, loss_tot = totals[5], totals[6]
    rep_out = _small_adam(totals, rep_w, rep_m, rep_v)
    g_rep, d_rep, nm_rep, nv_rep = (rep_out[k::4] for k in range(4))

    fcol = d_ff // N_DEV
    g_cw = lax.dynamic_slice(dconv_tot, (0, me * fcol), (3, fcol))
    d_cw, nm_cw, nv_cw = _adam(conv_w[0], g_cw, m_conv_w[0], v_conv_w[0], "adam_conv_w", 3)

    dmod_cols = lax.dynamic_slice(dmod_all, (0, me * ncol), (N_DEV, ncol))
    g_ada, d_ada, nm_ada, nv_ada = _ada_grad_adam(c_all, dmod_cols, w_ada[0], m_w_ada[0], v_w_ada[0], 256)

    loss = loss_tot[0, 0]

    def group(k):
        rep = (g_rep, d_rep, nm_rep, nv_rep)[k]
        ada = (g_ada, d_ada, nm_ada, nv_ada)[k][None]
        cw = (g_cw, d_cw, nm_cw, nv_cw)[k][None]
        return [ada, rep[0], rep[1], rep[2], rep[3], rep[4], big["w_in"][k][None], rep[5], rep[6], rep[7],
                big["w_out"][k][None], big["w_up"][k][None], cw, rep[8], big["w_down"][k][None]]

    return (loss, grad_x[None], *group(0), *group(1), *group(2), *group(3))
```

```python
import functools
import math

import jax
import jax.numpy as jnp
from jax import lax
from jax.experimental import pallas as pl
from jax.experimental.pallas import tpu as pltpu

F32 = jnp.float32
BF16 = jnp.bfloat16
MESH = pl.DeviceIdType.MESH

N_DEV = 8
HEAD_DIM = 64
ROT_HALF = 8
ROPE_THETA = 500000.0
POOL_WINDOWS = (2, 4, 8, 16)
DILATIONS = (1, 4, 16)
BLOCK = 128
NORM_EPS = 1e-6
HALO = 16
MASKED = -1e30
ATTN_FWD_UNROLL = 8
ATTN_BWD_UNROLL = 4

ADAM_LR = 0.001
ADAM_B1 = 0.9
ADAM_B2 = 0.999
ADAM_EPS = 1e-08
ADAM_WD = 0.01
ADAM_STEP = 10

V7X_VMEM_LIMIT = 56 * 1024 * 1024
LANES = 128

NT = (((1,), (1,)), ((), ()))
NN = (((1,), (0,)), ((), ()))
TN = (((0,), (0,)), ((), ()))


def _dot(a, b, dims):
    return lax.dot_general(a, b, dims, preferred_element_type=F32)


def _params(sem=None, vmem=V7X_VMEM_LIMIT):
    if sem is None:
        return pltpu.CompilerParams(vmem_limit_bytes=vmem)
    return pltpu.CompilerParams(dimension_semantics=sem, vmem_limit_bytes=vmem)


def _rstd(v):
    return lax.rsqrt(jnp.mean(v * v, axis=-1, keepdims=True) + NORM_EPS)


def _norm_bwd(dn, n, rstd):
    return rstd * (dn - n * jnp.mean(dn * n, axis=-1, keepdims=True))


def _rope_fwd(p, rope_ref):
    return p * rope_ref[0] + pltpu.roll(p, LANES - ROT_HALF, 1) * rope_ref[1] + pltpu.roll(p, ROT_HALF, 1) * rope_ref[2]


def _rope_bwd(dp, rope_ref):
    return dp * rope_ref[0] + pltpu.roll(dp * rope_ref[1], ROT_HALF, 1) + pltpu.roll(dp * rope_ref[2], LANES - ROT_HALF, 1)


def _gelu_parts(v):
    k2 = 2.0 * math.sqrt(2.0 / math.pi)
    c = 0.044715
    v2 = v * v
    s = jax.nn.sigmoid(v * (k2 + (k2 * c) * v2))
    g = v * s
    dg = s + g * (1.0 - s) * (k2 + (3.0 * k2 * c) * v2)
    return g, dg


def _halo_before(i, tile):
    return jnp.maximum(i * (tile // HALO) - 1, 0)


def _premix_inproj(x, sh, sc, g, w_in_t, rope, tm):
    s_len, d = x.shape
    n_proj = w_in_t.shape[0]
    n_slab = (n_proj - 256) // LANES

    def body(x_ref, sh_ref, sc_ref, g_ref, w_ref, rope_ref, h_ref, up_ref, qkv_ref):
        xv = x_ref[...]
        h = (xv * _rstd(xv) * g_ref[...]) * (1.0 + sc_ref[...]) + sh_ref[...]
        hb = h.astype(BF16)
        h_ref[...] = hb
        up_ref[...] = _dot(hb, w_ref[0:256, :], NT)
        for pair in range(n_slab // 2):
            p = _dot(hb, w_ref[256 + 256 * pair:512 + 256 * pair, :], NT)
            for half in range(2):
                ph = p[:, half * LANES:(half + 1) * LANES]
                if pair < 6:
                    ph = _rope_fwd(ph, rope_ref)
                if pair < 3:
                    ph = ph * (HEAD_DIM ** -0.5)
                qkv_ref[2 * pair + half] = ph

    vec = pl.BlockSpec((1, d), lambda i: (0, 0))
    return pl.pallas_call(
        body, name="premix_inproj", grid=(s_len // tm,),
        in_specs=[pl.BlockSpec((tm, d), lambda i: (i, 0)), vec, vec, vec,
                  pl.BlockSpec((n_proj, d), lambda i: (0, 0)),
                  pl.BlockSpec((3, tm, LANES), lambda i: (0, i, 0))],
        out_specs=[pl.BlockSpec((tm, d), lambda i: (i, 0)),
                   pl.BlockSpec((tm, 256), lambda i: (i, 0)),
                   pl.BlockSpec((n_slab, tm, LANES), lambda i: (0, i, 0))],
        out_shape=[jax.ShapeDtypeStruct((s_len, d), BF16),
                   jax.ShapeDtypeStruct((s_len, 256), F32),
                   jax.ShapeDtypeStruct((n_slab, s_len, LANES), F32)],
        compiler_params=_params(("arbitrary",)),
    )(x, sh, sc, g, w_in_t, rope)


def _block_rows(n, r, dil):
    start = n * (BLOCK * dil) + r
    if dil == 1:
        return pl.ds(pl.multiple_of(start, BLOCK), BLOCK)
    return pl.ds(start, BLOCK, stride=dil)


def _band_mask(n):
    ri = lax.broadcasted_iota(jnp.int32, (BLOCK, 2 * BLOCK), 0)
    cj = lax.broadcasted_iota(jnp.int32, (BLOCK, 2 * BLOCK), 1)
    cur = (cj >= BLOCK) & (cj - BLOCK <= ri)
    prev = (cj < BLOCK) & (cj >= ri) & (n > 0)
    return cur | prev


def _attn_fwd(qkv):
    s_len = qkv.shape[1]
    n_g = len(DILATIONS)

    def body(q_ref, k_ref, v_ref, o_ref, lse_ref):
        lane = lax.broadcasted_iota(jnp.int32, (BLOCK, LANES), 1)
        first = lane < HEAD_DIM

        def group(dil):
            nb = s_len // (BLOCK * dil)

            def block(t, carry):
                r, n = t // nb, t % nb
                cur = _block_rows(n, r, dil)
                prev = _block_rows(jnp.maximum(n - 1, 0), r, dil)
                q = q_ref[0, cur, :]
                kcat = jnp.concatenate([k_ref[0, prev, :], k_ref[0, cur, :]], axis=0).astype(BF16)
                vcat = jnp.concatenate([v_ref[0, prev, :], v_ref[0, cur, :]], axis=0).astype(BF16)
                valid = _band_mask(n)
                q2 = jnp.concatenate([jnp.where(first, q, 0.0), jnp.where(first, 0.0, q)], axis=0).astype(BF16)
                s = jnp.where(jnp.concatenate([valid, valid], axis=0), _dot(q2, kcat, NT), MASKED)
                m = jnp.max(s, axis=-1, keepdims=True)
                p = jnp.exp(s - m)
                den = jnp.sum(p, axis=-1, keepdims=True)
                o2 = _dot(p.astype(BF16), vcat, NN) / den
                lse2 = m + jnp.log(den)
                o_ref[0, 0, cur, :] = jnp.where(first, o2[:BLOCK], o2[BLOCK:])
                lse_ref[0, 0, cur, :] = jnp.where(first, lse2[:BLOCK], lse2[BLOCK:])
                return carry

            lax.fori_loop(0, nb * dil, block, 0, unroll=ATTN_FWD_UNROLL)

        for gi, dil in enumerate(DILATIONS):
            pl.when(pl.program_id(0) == gi)(functools.partial(group, dil))

    def slab(base):
        return pl.BlockSpec((1, s_len, LANES), lambda g, s: (base + 2 * g + s, 0, 0))

    out = pl.BlockSpec((1, 1, s_len, LANES), lambda g, s: (g, s, 0, 0))
    shape = jax.ShapeDtypeStruct((n_g, 2, s_len, LANES), F32)
    return pl.pallas_call(
        body, name="attn_fwd", grid=(n_g, 2),
        in_specs=[slab(0), slab(6), slab(12)], out_specs=[out, out], out_shape=[shape, shape],
        compiler_params=_params(("arbitrary", "arbitrary")),
    )(qkv, qkv, qkv)


def _pool_mixed(u, halo, i, tm):
    ue = jnp.concatenate([halo, u], axis=0)
    s2 = ue + pltpu.roll(ue, 1, 0)
    s4 = s2 + pltpu.roll(s2, 2, 0)
    s8 = s4 + pltpu.roll(s4, 4, 0)
    s16 = s8 + pltpu.roll(s8, 8, 0)
    grp = lax.broadcasted_iota(jnp.int32, (tm, 256), 1) // HEAD_DIM
    pick = lambda a, b, c, e: jnp.where(grp == 0, a, jnp.where(grp == 1, b, jnp.where(grp == 2, c, e)))
    win_sum = pick(s2[HALO:], s4[HALO:], s8[HALO:], s16[HALO:])
    pos = (i * tm + lax.broadcasted_iota(jnp.int32, (tm, 256), 0)).astype(F32)
    count = jnp.minimum(pos + 1.0, pick(*[float(w) for w in POOL_WINDOWS]))
    return win_sum / count - u, count


def _mix_out(x, u_pool, o_g, lse_g, w_blk, b_pool, pool_scale, w_out_t, gt_m, g_post_mix, g_pre_ffn, sc_f, sh_f, tm):
    s_len, d = x.shape

    def body(x_ref, u_ref, uh_ref, o_ref, l_ref, wb_ref, bp_ref, ps_ref, wo_ref,
             gt_ref, g1_ref, g2_ref, sc_ref, sh_ref,
             x1_ref, y1_ref, h2_ref, cat_ref, attn_ref, lall_ref):
        (o0, o1, o2), (l0, l1, l2) = (o_ref.at[g] for g in range(3)), (l_ref.at[g] for g in range(3))
        i = pl.program_id(0)
        u = u_ref[...]
        halo = uh_ref[...] * (i > 0).astype(F32)
        mixed, _ = _pool_mixed(u, halo, i, tm)
        y = _dot(mixed.astype(BF16), wb_ref[...], NN) + bp_ref[...]
        pool = y * ps_ref[...]
        attn = []
        for s in range(2):
            la, lb, lc = l0[s], l1[s], l2[s]
            mx = jnp.maximum(jnp.maximum(la, lb), lc)
            ea, eb, ec = jnp.exp(la - mx), jnp.exp(lb - mx), jnp.exp(lc - mx)
            den = ea + eb + ec
            lall_ref[s] = mx + jnp.log(den)
            attn.append((ea / den) * o0[s] + (eb / den) * o1[s] + (ec / den) * o2[s])
        attn = jnp.concatenate(attn, axis=1)
        attn_ref[...] = attn
        cat = jnp.concatenate([pool, attn], axis=1).astype(BF16)
        cat_ref[...] = cat
        y1 = _dot(cat, wo_ref[...], NT)
        y1_ref[...] = y1.astype(BF16)
        x1 = x_ref[...] + gt_ref[...] * (y1 * _rstd(y1) * g1_ref[...])
        x1_ref[...] = x1
        h2 = (x1 * _rstd(x1) * g2_ref[...]) * (1.0 + sc_ref[...]) + sh_ref[...]
        h2_ref[...] = h2.astype(BF16)

    tile = lambda w: pl.BlockSpec((tm, w), lambda i: (i, 0))
    slab = pl.BlockSpec((2, tm, LANES), lambda i: (0, i, 0))
    groups = pl.BlockSpec((len(DILATIONS), 2, tm, LANES), lambda i: (0, 0, i, 0))
    const = lambda a: pl.BlockSpec(a.shape, lambda i: (0,) * a.ndim)
    return pl.pallas_call(
        body, name="mix_out", grid=(s_len // tm,),
        in_specs=[tile(d), tile(256), pl.BlockSpec((HALO, 256), lambda i: (_halo_before(i, tm), 0)),
                  groups, groups,
                  const(w_blk), const(b_pool), const(pool_scale), const(w_out_t),
                  const(gt_m), const(g_post_mix), const(g_pre_ffn), const(sc_f), const(sh_f)],
        out_specs=[tile(d), tile(d), tile(d), tile(512), tile(256), slab],
        out_shape=[jax.ShapeDtypeStruct((s_len, d), F32), jax.ShapeDtypeStruct((s_len, d), BF16),
                   jax.ShapeDtypeStruct((s_len, d), BF16), jax.ShapeDtypeStruct((s_len, 512), BF16),
                   jax.ShapeDtypeStruct((s_len, 256), F32), jax.ShapeDtypeStruct((2, s_len, LANES), F32)],
        compiler_params=_params(("arbitrary",)),
    )(x, u_pool, u_pool, o_g, lse_g, w_blk, b_pool, pool_scale, w_out_t, gt_m, g_post_mix, g_pre_ffn, sc_f, sh_f)


def _conv_gate(gate_ext, cw, cb):
    gc = gate_ext * cw[2:3, :] + pltpu.roll(gate_ext, 1, 0) * cw[1:2, :] + pltpu.roll(gate_ext, 2, 0) * cw[0:1, :]
    return gc[HALO:] + cb


def _ffn_fwd_loss(h2, x1, target, w_up_t, w_down, conv_w, conv_b, gt_f, g_post_ffn, tm, tf, ck):
    s_len, d = x1.shape
    d_ff = w_down.shape[0]
    n_f = d_ff // tf

    def body(h_ref, hh_ref, x1_ref, tgt_ref, wg_ref, wv_ref, wd_ref, cw_ref, cb_ref, gt_ref, g_ref,
             gate_ref, a_ref, act_ref, vd_ref, dy2_ref, dout_ref, sums_ref, loss_ref, acc_ref):
        i, j = pl.program_id(0), pl.program_id(1)

        @pl.when((i == 0) & (j == 0))
        def _():
            sums_ref[...] = jnp.zeros_like(sums_ref)
            loss_ref[...] = jnp.zeros_like(loss_ref)

        h = h_ref[...]
        h_ext = jnp.concatenate([hh_ref[...], h], axis=0)
        row = lax.broadcasted_iota(jnp.int32, (tm + HALO, ck), 0)
        no_halo = (row < HALO) & (i == 0)

        def up(c):
            cs = slice(c * ck, (c + 1) * ck)
            return jnp.where(no_halo, 0.0, _dot(h_ext, wg_ref[cs, :], NT)), _dot(h, wv_ref[cs, :], NT)

        part = None
        n_c = tf // ck
        nxt = up(0)
        for c in range(n_c):
            cs = slice(c * ck, (c + 1) * ck)
            gate_ext, val = nxt
            if c + 1 < n_c:
                nxt = up(c + 1)
            act, dact = _gelu_parts(_conv_gate(gate_ext, cw_ref[:, cs], cb_ref[:, cs]))
            a = (act * val).astype(BF16)
            gate_ref[:, cs] = gate_ext[HALO:].astype(BF16)
            a_ref[:, cs] = a
            act_ref[:, cs] = act.astype(BF16)
            vd_ref[:, cs] = (val * dact).astype(BF16)
            p = _dot(a, wd_ref[cs, :], NN)
            part = p if part is None else part + p

        @pl.when(j == 0)
        def _():
            acc_ref[...] = part

        @pl.when(j > 0)
        def _():
            acc_ref[...] += part

        @pl.when(j == n_f - 1)
        def _():
            y2 = acc_ref[...]
            rstd = _rstd(y2)
            n = y2 * rstd
            rn = n * g_ref[...]
            err = x1_ref[...] + gt_ref[...] * rn - tgt_ref[...]
            loss_ref[...] += 0.5 * jnp.sum(jnp.mean(err * err, axis=-1, keepdims=True), axis=0, keepdims=True)
            dout = err * (1.0 / d)
            dout_ref[...] = dout
            drn = dout * gt_ref[...]
            sums_ref[0:1, :] += jnp.sum(dout * rn, axis=0, keepdims=True)
            sums_ref[1:2, :] += jnp.sum(drn * n, axis=0, keepdims=True)
            dy2_ref[...] = _norm_bwd(drn * g_ref[...], n, rstd).astype(BF16)

    tok = lambda w: pl.BlockSpec((tm, w), lambda i, j: (i, 0))
    tokf = pl.BlockSpec((tm, tf), lambda i, j: (i, j))
    vec = pl.BlockSpec((1, d), lambda i, j: (0, 0))
    once = {"pipeline_mode": pl.Buffered(1)} if n_f == 1 else {}
    return pl.pallas_call(
        body, name="ffn_fwd_loss", grid=(s_len // tm, n_f),
        in_specs=[tok(d), pl.BlockSpec((HALO, d), lambda i, j: (_halo_before(i, tm), 0)), tok(d), tok(d),
                  pl.BlockSpec((tf, d), lambda i, j: (j, 0), **once),
                  pl.BlockSpec((tf, d), lambda i, j: (j + n_f, 0), **once),
                  pl.BlockSpec((tf, d), lambda i, j: (j, 0), **once),
                  pl.BlockSpec((3, tf), lambda i, j: (0, j)), pl.BlockSpec((1, tf), lambda i, j: (0, j)), vec, vec],
        out_specs=[tokf, tokf, tokf, tokf, tok(d), tok(d), pl.BlockSpec((8, d), lambda i, j: (0, 0)),
                   pl.BlockSpec((8, LANES), lambda i, j: (0, 0))],
        out_shape=[jax.ShapeDtypeStruct((s_len, d_ff), BF16)] * 4
        + [jax.ShapeDtypeStruct((s_len, d), BF16), jax.ShapeDtypeStruct((s_len, d), F32),
                   jax.ShapeDtypeStruct((8, d), F32), jax.ShapeDtypeStruct((8, LANES), F32)],
        scratch_shapes=[pltpu.VMEM((tm, d), F32)],
        compiler_params=_params(("arbitrary", "arbitrary")),
    )(h2, h2, x1, target, w_up_t, w_up_t, w_down, conv_w, conv_b, gt_f, g_post_ffn)


def _ffn_bwd_act(dy2, gate, a, act, vd, w_down, tm, ck):
    s_len, d = dy2.shape
    d_ff = w_down.shape[0]
    n_t, n_c = s_len // tm, d_ff // ck

    def body(dy_ref, g_ref, gh_ref, a_ref, act_ref, vd_ref, wd_ref, dgc_ref, dval_ref, dwd_ref, dconv_ref, acc_ref):
        i = pl.program_id(0)

        @pl.when(i == 0)
        def _():
            acc_ref[...] = jnp.zeros_like(acc_ref)
            dconv_ref[...] = jnp.zeros_like(dconv_ref)

        dy = dy_ref[...]
        row = lax.broadcasted_iota(jnp.int32, (tm + HALO, ck), 0)
        no_halo = (row < HALO) & (i == 0)

        def down(c):
            return _dot(dy, wd_ref[c * ck:(c + 1) * ck, :], NT)

        nxt = down(0)
        for c in range(n_c):
            cs = slice(c * ck, (c + 1) * ck)
            da = nxt
            if c + 1 < n_c:
                nxt = down(c + 1)
            acc_ref[cs, :] += _dot(a_ref[:, cs], dy, TN)
            gate_ext = jnp.where(no_halo, 0.0, jnp.concatenate([gh_ref[:, cs], g_ref[:, cs]], axis=0).astype(F32))
            dgc = da * vd_ref[:, cs].astype(F32)
            dgc_ref[:, cs] = dgc.astype(BF16)
            dval_ref[:, cs] = (da * act_ref[:, cs].astype(F32)).astype(BF16)
            rows = [jnp.sum(dgc * pltpu.roll(gate_ext, 2 - k, 0)[HALO:], axis=0, keepdims=True) for k in range(2)]
            rows += [jnp.sum(dgc * gate_ext[HALO:], axis=0, keepdims=True), jnp.sum(dgc, axis=0, keepdims=True),
                     jnp.zeros((4, ck), F32)]
            dconv_ref[:, cs] += jnp.concatenate(rows, axis=0)

        @pl.when(i == n_t - 1)
        def _():
            dwd_ref[...] = acc_ref[...].astype(BF16)

    tokf = pl.BlockSpec((tm, d_ff), lambda i: (i, 0))
    return pl.pallas_call(
        body, name="ffn_bwd_act", grid=(n_t,),
        in_specs=[pl.BlockSpec((tm, d), lambda i: (i, 0)), tokf,
                  pl.BlockSpec((HALO, d_ff), lambda i: (_halo_before(i, tm), 0)), tokf, tokf, tokf,
                  pl.BlockSpec((d_ff, d), lambda i: (0, 0), pipeline_mode=pl.Buffered(1))],
        out_specs=[tokf, tokf, pl.BlockSpec((d_ff, d), lambda i: (0, 0)), pl.BlockSpec((8, d_ff), lambda i: (0, 0))],
        out_shape=[jax.ShapeDtypeStruct((s_len, d_ff), BF16), jax.ShapeDtypeStruct((s_len, d_ff), BF16),
                   jax.ShapeDtypeStruct((d_ff, d), BF16), jax.ShapeDtypeStruct((8, d_ff), F32)],
        scratch_shapes=[pltpu.VMEM((d_ff, d), F32)],
        compiler_params=_params(("arbitrary",)),
    )(dy2, gate, gate, a, act, vd, w_down)


def _ffn_bwd_up(dgc, dval, w_up_t, conv_w, tm):
    s_len, d_ff = dgc.shape
    d = w_up_t.shape[1]
    n_t = s_len // tm

    def body(dg_ref, dgn_ref, dv_ref, cw_ref, w_ref, dup_ref, dh_ref):
        i = pl.program_id(0)
        nxt = dgn_ref[...].astype(F32) * (i < n_t - 1).astype(F32)
        ext = jnp.concatenate([dg_ref[...].astype(F32), nxt], axis=0)
        rows = tm + HALO
        dgate = (ext * cw_ref[2:3, :] + pltpu.roll(ext, rows - 1, 0) * cw_ref[1:2, :]
                 + pltpu.roll(ext, rows - 2, 0) * cw_ref[0:1, :])[:tm]
        dup = jnp.concatenate([dgate.astype(BF16), dv_ref[...]], axis=1)
        dup_ref[...] = dup
        dh_ref[...] = _dot(dup, w_ref[...], NN).astype(BF16)

    tokf = pl.BlockSpec((tm, d_ff), lambda i: (i, 0))
    return pl.pallas_call(
        body, name="ffn_bwd_up", grid=(n_t,),
        in_specs=[tokf, pl.BlockSpec((HALO, d_ff), lambda i: (jnp.minimum((i + 1) * (tm // HALO), s_len // HALO - 1), 0)),
                  tokf, pl.BlockSpec((3, d_ff), lambda i: (0, 0)), pl.BlockSpec((2 * d_ff, d), lambda i: (0, 0))],
        out_specs=[pl.BlockSpec((tm, 2 * d_ff), lambda i: (i, 0)), pl.BlockSpec((tm, d), lambda i: (i, 0))],
        out_shape=[jax.ShapeDtypeStruct((s_len, 2 * d_ff), BF16), jax.ShapeDtypeStruct((s_len, d), BF16)],
        compiler_params=_params(("arbitrary",)),
    )(dgc, dgc, dval, conv_w, w_up_t)


def _mix_bwd(dh2, dout, x1, y1, cat, attn, w_out_t, sc_f, g_pre_ffn, gt_m, g_post_mix, tm):
    s_len, d = x1.shape
    n_t = s_len // tm

    def body(dh_ref, do_ref, x1_ref, y1_ref, cat_ref, at_ref, wo_ref, sc_ref, g2_ref, gt_ref, g1_ref,
             dx1_ref, dpool_ref, dattn_ref, delta_ref, dwo_ref, sums_ref, acc_ref):
        i = pl.program_id(0)
        dh = dh_ref[...].astype(F32)
        x1 = x1_ref[...]
        r2 = _rstd(x1)
        n2 = x1 * r2
        ng = n2 * g2_ref[...]
        dng = dh * (1.0 + sc_ref[...])
        dx1 = do_ref[...] + _norm_bwd(dng * g2_ref[...], n2, r2)
        dx1_ref[...] = dx1
        y1 = y1_ref[...].astype(F32)
        r1 = _rstd(y1)
        n1 = y1 * r1
        drn = dx1 * gt_ref[...]
        dy1 = _norm_bwd(drn * g1_ref[...], n1, r1).astype(BF16)
        dcat = _dot(dy1, wo_ref[...], NN)
        dpool_ref[...] = dcat[:, 0:256]
        lane = lax.broadcasted_iota(jnp.int32, (tm, LANES), 1)
        first = lane < HEAD_DIM
        for s in range(2):
            da = dcat[:, 256 + s * LANES:256 + (s + 1) * LANES]
            dattn_ref[s] = da
            prod = da * at_ref[:, s * LANES:(s + 1) * LANES]
            tot = jnp.sum(prod, axis=-1, keepdims=True)
            lo = jnp.sum(jnp.where(first, prod, 0.0), axis=-1, keepdims=True)
            delta_ref[s] = jnp.where(first, lo, tot - lo)
        dwo = _dot(dy1, cat_ref[...], TN)
        sums = jnp.concatenate(
            [jnp.sum(dh, axis=0, keepdims=True), jnp.sum(dh * ng, axis=0, keepdims=True),
             jnp.sum(dng * n2, axis=0, keepdims=True), jnp.sum(dx1 * (n1 * g1_ref[...]), axis=0, keepdims=True),
             jnp.sum(drn * n1, axis=0, keepdims=True), jnp.zeros((3, d), F32)], axis=0)

        @pl.when(i == 0)
        def _():
            acc_ref[...] = dwo
            sums_ref[...] = sums

        @pl.when(i > 0)
        def _():
            acc_ref[...] += dwo
            sums_ref[...] += sums

        @pl.when(i == n_t - 1)
        def _():
            dwo_ref[...] = acc_ref[...].astype(BF16)

    tile = lambda w: pl.BlockSpec((tm, w), lambda i: (i, 0))
    slab = pl.BlockSpec((2, tm, LANES), lambda i: (0, i, 0))
    vec = pl.BlockSpec((1, d), lambda i: (0, 0))
    return pl.pallas_call(
        body, name="mix_bwd", grid=(n_t,),
        in_specs=[tile(d), tile(d), tile(d), tile(d), tile(512), tile(256),
                  pl.BlockSpec((d, 512), lambda i: (0, 0)), vec, vec, vec, vec],
        out_specs=[tile(d), tile(256), slab, slab, pl.BlockSpec((d, 512), lambda i: (0, 0)),
                   pl.BlockSpec((8, d), lambda i: (0, 0))],
        out_shape=[jax.ShapeDtypeStruct((s_len, d), F32), jax.ShapeDtypeStruct((s_len, 256), F32),
                   jax.ShapeDtypeStruct((2, s_len, LANES), F32), jax.ShapeDtypeStruct((2, s_len, LANES), F32),
                   jax.ShapeDtypeStruct((d, 512), BF16), jax.ShapeDtypeStruct((8, d), F32)],
        scratch_shapes=[pltpu.VMEM((d, 512), F32)],
        compiler_params=_params(("arbitrary",)),
    )(dh2, dout, x1, y1, cat, attn, w_out_t, sc_f, g_pre_ffn, gt_m, g_post_mix)


def _pool_bwd(dpool, u_pool, w_blk, b_pool, pool_scale, tm):
    s_len = dpool.shape[0]
    n_t = s_len // tm

    def body(dp_ref, dpn_ref, u_ref, uh_ref, wb_ref, bp_ref, ps_ref, du_ref, dwp_ref, sums_ref, acc_ref):
        i = pl.program_id(0)
        u = u_ref[...]
        mixed, _ = _pool_mixed(u, uh_ref[...] * (i > 0).astype(F32), i, tm)
        mixed_b = mixed.astype(BF16)
        y = _dot(mixed_b, wb_ref[...], NN) + bp_ref[...]
        dp = dp_ref[...]
        dy = dp * ps_ref[...]
        dwb = _dot(mixed_b, dy.astype(BF16), TN)
        sums = jnp.concatenate([jnp.sum(dy, axis=0, keepdims=True), jnp.sum(dp * y, axis=0, keepdims=True),
                                jnp.zeros((6, 256), F32)], axis=0)
        dp_ext = jnp.concatenate([dp, dpn_ref[...] * (i < n_t - 1).astype(F32)], axis=0)
        dmix = _dot((dp_ext * ps_ref[...]).astype(BF16), wb_ref[...], NT)
        rows = tm + HALO
        grp = lax.broadcasted_iota(jnp.int32, (rows, 256), 1) // HEAD_DIM
        pick = lambda a, b, c, e: jnp.where(grp == 0, a, jnp.where(grp == 1, b, jnp.where(grp == 2, c, e)))
        pos = (i * tm + lax.broadcasted_iota(jnp.int32, (rows, 256), 0)).astype(F32)
        z = dmix / jnp.minimum(pos + 1.0, pick(*[float(w) for w in POOL_WINDOWS]))
        f2 = z + pltpu.roll(z, rows - 1, 0)
        f4 = f2 + pltpu.roll(f2, rows - 2, 0)
        f8 = f4 + pltpu.roll(f4, rows - 4, 0)
        f16 = f8 + pltpu.roll(f8, rows - 8, 0)
        du_ref[...] = (pick(f2, f4, f8, f16) - dmix)[:tm]

        @pl.when(i == 0)
        def _():
            acc_ref[...] = dwb
            sums_ref[...] = sums

        @pl.when(i > 0)
        def _():
            acc_ref[...] += dwb
            sums_ref[...] += sums

        @pl.when(i == n_t - 1)
        def _():
            full = acc_ref[...]
            for gi in range(len(POOL_WINDOWS)):
                lo = gi * HEAD_DIM
                dwp_ref[gi] = full[lo:lo + HEAD_DIM, lo:lo + HEAD_DIM]

    n_g = len(POOL_WINDOWS)
    tile = pl.BlockSpec((tm, 256), lambda i: (i, 0))
    const = lambda a: pl.BlockSpec(a.shape, lambda i: (0,) * a.ndim)
    return pl.pallas_call(
        body, name="pool_bwd", grid=(n_t,),
        in_specs=[tile, pl.BlockSpec((HALO, 256), lambda i: (jnp.minimum((i + 1) * (tm // HALO), s_len // HALO - 1), 0)),
                  tile, pl.BlockSpec((HALO, 256), lambda i: (_halo_before(i, tm), 0)),
                  const(w_blk), const(b_pool), const(pool_scale)],
        out_specs=[tile, pl.BlockSpec((n_g, HEAD_DIM, HEAD_DIM), lambda i: (0, 0, 0)), pl.BlockSpec((8, 256), lambda i: (0, 0))],
        out_shape=[jax.ShapeDtypeStruct((s_len, 256), F32), jax.ShapeDtypeStruct((n_g, HEAD_DIM, HEAD_DIM), F32),
                   jax.ShapeDtypeStruct((8, 256), F32)],
        scratch_shapes=[pltpu.VMEM((256, 256), F32)],
        compiler_params=_params(("arbitrary",)),
    )(dpool, dpool, u_pool, u_pool, w_blk, b_pool, pool_scale)


def _attn_bwd(qkv, dattn, lse_all, delta):
    s_len = qkv.shape[1]
    n_g = len(DILATIONS)

    def body(q_ref, k_ref, v_ref, do_ref, l_ref, dl_ref, dq_ref, dk_ref, dv_ref):
        lane = lax.broadcasted_iota(jnp.int32, (BLOCK, LANES), 1)
        first = lane < HEAD_DIM

        def group(dil):
            nb = s_len // (BLOCK * dil)

            def block(t, carry):
                dk_part, dv_part = carry
                r, n = t // nb, t % nb
                cur = _block_rows(n, r, dil)
                prev = _block_rows(jnp.maximum(n - 1, 0), r, dil)
                q = q_ref[0, cur, :]
                do = do_ref[0, cur, :]
                lse = l_ref[0, cur, :]
                dlt = dl_ref[0, cur, :]
                kcat = jnp.concatenate([k_ref[0, prev, :], k_ref[0, cur, :]], axis=0).astype(BF16)
                vcat = jnp.concatenate([v_ref[0, prev, :], v_ref[0, cur, :]], axis=0).astype(BF16)
                valid = _band_mask(n)
                stack = lambda a: jnp.concatenate([jnp.where(first, a, 0.0), jnp.where(first, 0.0, a)], axis=0)
                rows2 = lambda a: jnp.concatenate([a[:, 0:1], a[:, HEAD_DIM:HEAD_DIM + 1]], axis=0)
                q2, do2 = stack(q).astype(BF16), stack(do).astype(BF16)
                valid2 = jnp.concatenate([valid, valid], axis=0)
                p = jnp.where(valid2, jnp.exp(_dot(q2, kcat, NT) - rows2(lse)), 0.0)
                ds = (p * (_dot(do2, vcat, NT) - rows2(dlt))).astype(BF16)
                dq2 = _dot(ds, kcat, NN)
                dq_ref[0, 0, cur, :] = jnp.where(first, dq2[:BLOCK], dq2[BLOCK:])
                dkc = _dot(ds, q2, TN)
                dvc = _dot(p.astype(BF16), do2, TN)
                dk_ref[0, 0, prev, :] = dk_part + dkc[:BLOCK]
                dv_ref[0, 0, prev, :] = dv_part + dvc[:BLOCK]
                dk_ref[0, 0, cur, :] = dkc[BLOCK:]
                dv_ref[0, 0, cur, :] = dvc[BLOCK:]
                return dkc[BLOCK:], dvc[BLOCK:]

            def blocks(tt, carry):
                for u in range(ATTN_BWD_UNROLL):
                    carry = block(tt * ATTN_BWD_UNROLL + u, carry)
                return carry

            zero = jnp.zeros((BLOCK, LANES), F32)
            lax.fori_loop(0, nb * dil // ATTN_BWD_UNROLL, blocks, (zero, zero))

        for gi, dil in enumerate(DILATIONS):
            pl.when(pl.program_id(1) == gi)(functools.partial(group, dil))

    def slab(base):
        return pl.BlockSpec((1, s_len, LANES), lambda s, g: (base + 2 * g + s, 0, 0))

    one = pl.BlockSpec((1, s_len, LANES), lambda s, g: (s, 0, 0))
    out = pl.BlockSpec((1, 1, s_len, LANES), lambda s, g: (g, s, 0, 0))
    shape = jax.ShapeDtypeStruct((n_g, 2, s_len, LANES), F32)
    return pl.pallas_call(
        body, name="attn_bwd", grid=(2, n_g),
        in_specs=[slab(0), slab(6), slab(12), one, one, one],
        out_specs=[out, out, out], out_shape=[shape, shape, shape],
        compiler_params=_params(("arbitrary", "arbitrary")),
    )(qkv, qkv, qkv, dattn, lse_all, delta)


def _dproj_assemble(du, dqkv, rope, tm):
    s_len = du.shape[0]
    n_proj = 256 + 18 * LANES

    def body(du_ref, dq_ref, dk_ref, dv_ref, rope_ref, dproj_ref):
        dproj_ref[:, 0:256] = du_ref[...].astype(BF16)
        col = 256
        for kind, dref in enumerate((dq_ref, dk_ref, dv_ref)):
            for grp in range(3):
                for s in range(2):
                    piece = dref[grp, s]
                    if kind < 2:
                        piece = _rope_bwd(piece, rope_ref)
                    if kind == 0:
                        piece = piece * (HEAD_DIM ** -0.5)
                    dproj_ref[:, col:col + LANES] = piece.astype(BF16)
                    col += LANES

    groups = pl.BlockSpec((len(DILATIONS), 2, tm, LANES), lambda i: (0, 0, i, 0))
    return pl.pallas_call(
        body, name="dproj_assemble", grid=(s_len // tm,),
        in_specs=[pl.BlockSpec((tm, 256), lambda i: (i, 0))] + [groups] * 3 + [pl.BlockSpec((3, tm, LANES), lambda i: (0, i, 0))],
        out_specs=pl.BlockSpec((tm, n_proj), lambda i: (i, 0)),
        out_shape=jax.ShapeDtypeStruct((s_len, n_proj), BF16),
        compiler_params=_params(("arbitrary",)),
    )(du, *dqkv, rope)


def _inproj_bwd(dproj, w_in_t, x, dx1, sc_m, g_pre_mix, tm):
    s_len, d = x.shape
    n_proj = w_in_t.shape[0]
    n_t = s_len // tm

    def body(dproj_ref, w_ref, x_ref, dx1_ref, sc_ref, g_ref, dx_ref, sums_ref):
        i = pl.program_id(0)
        halves = [slice(0, tm // 2), slice(tm // 2, tm)]
        dhs = [_dot(dproj_ref[rs, :], w_ref[...], NN) for rs in halves]
        sums = None
        for rs, dh in zip(halves, dhs):
            xv = x_ref[rs, :]
            r = _rstd(xv)
            n = xv * r
            dng = dh * (1.0 + sc_ref[...])
            dx_ref[rs, :] = dx1_ref[rs, :] + _norm_bwd(dng * g_ref[...], n, r)
            part = jnp.concatenate([jnp.sum(dh, axis=0, keepdims=True), jnp.sum(dh * (n * g_ref[...]), axis=0, keepdims=True),
                                    jnp.sum(dng * n, axis=0, keepdims=True), jnp.zeros((5, d), F32)], axis=0)
            sums = part if sums is None else sums + part

        @pl.when(i == 0)
        def _():
            sums_ref[...] = sums

        @pl.when(i > 0)
        def _():
            sums_ref[...] += sums

    tile = lambda w: pl.BlockSpec((tm, w), lambda i: (i, 0))
    vec = pl.BlockSpec((1, d), lambda i: (0, 0))
    return pl.pallas_call(
        body, name="inproj_bwd", grid=(n_t,),
        in_specs=[tile(n_proj), pl.BlockSpec((n_proj, d), lambda i: (0, 0)), tile(d), tile(d), vec, vec],
        out_specs=[tile(d), pl.BlockSpec((8, d), lambda i: (0, 0))],
        out_shape=[jax.ShapeDtypeStruct((s_len, d), F32), jax.ShapeDtypeStruct((8, d), F32)],
        compiler_params=_params(("arbitrary",)),
    )(dproj, w_in_t, x, dx1, sc_m, g_pre_mix)


def _wgrad(a, b, name, tk, tmm):
    s_len, m = a.shape
    n = b.shape[1]
    n_k = s_len // tk

    def body(a_ref, b_ref, o_ref, acc_ref):
        k = pl.program_id(1)
        part = _dot(a_ref[...], b_ref[...], TN)

        @pl.when(k == 0)
        def _():
            acc_ref[...] = part

        @pl.when(k > 0)
        def _():
            acc_ref[...] += part

        @pl.when(k == n_k - 1)
        def _():
            o_ref[...] = acc_ref[...].astype(BF16)

    return pl.pallas_call(
        body, name=name, grid=(m // tmm, n_k),
        in_specs=[pl.BlockSpec((tk, tmm), lambda j, k: (k, j)), pl.BlockSpec((tk, n), lambda j, k: (k, 0))],
        out_specs=pl.BlockSpec((tmm, n), lambda j, k: (j, 0)),
        out_shape=jax.ShapeDtypeStruct((m, n), BF16),
        scratch_shapes=[pltpu.VMEM((tmm, n), F32)],
        compiler_params=_params(("arbitrary", "arbitrary")),
    )(a, b)


def _place():
    return lax.axis_index("x"), lax.axis_index("y"), lax.axis_index("c")


def _peer(k):
    x, y, c = _place()
    bx, by, bc = (k >> 2) & 1, (k >> 1) & 1, k & 1
    return (x ^ bx if bx else x, y ^ by if by else y, c ^ bc if bc else c)


def _index(pos):
    return 4 * pos[0] + 2 * pos[1] + pos[2]


def _ada_exchange(c_rows, w_ada, b_ada_cols, taps):
    d = c_rows.shape[1]
    ncol = w_ada.shape[1]

    def body(c_ref, w_ref, b_ref, t_ref, call_ref, mod_ref, tall_ref, stage_ref, send_sems, recv_sems):
        me = _index(_place())
        call_ref[me] = c_ref[...]
        tall_ref[me] = t_ref[...]

        def gather(k):
            return pltpu.make_async_remote_copy(
                src_ref=c_ref, dst_ref=call_ref.at[me], send_sem=send_sems.at[0, k - 1], recv_sem=recv_sems.at[0, k - 1],
                device_id=_peer(k), device_id_type=MESH)

        def gather_taps(k):
            return pltpu.make_async_remote_copy(
                src_ref=t_ref, dst_ref=tall_ref.at[me], send_sem=send_sems.at[2, k - 1], recv_sem=recv_sems.at[2, k - 1],
                device_id=_peer(k), device_id_type=MESH)

        for k in range(1, N_DEV):
            gather(k).start()
        for k in range(1, N_DEV):
            gather_taps(k).start()
        for k in range(1, N_DEV):
            gather(k).wait_recv()
        cv = jnp.concatenate([call_ref[b, 0:1, :] for b in range(N_DEV)], axis=0)
        act = cv * jax.nn.sigmoid(cv)
        mod = lax.dot_general(act, w_ref[...], NN, preferred_element_type=F32,
                              precision=lax.Precision.HIGHEST) + b_ref[...]
        for b in range(N_DEV):
            stage_ref[b] = jnp.broadcast_to(mod[b:b + 1, :], (8, ncol))
        mod_ref[me] = stage_ref[me]

        def scatter(k):
            return pltpu.make_async_remote_copy(
                src_ref=stage_ref.at[_index(_peer(k))], dst_ref=mod_ref.at[me],
                send_sem=send_sems.at[1, k - 1], recv_sem=recv_sems.at[1, k - 1],
                device_id=_peer(k), device_id_type=MESH)

        for k in range(1, N_DEV):
            scatter(k).start()
        for k in range(1, N_DEV):
            scatter(k).wait_recv()
        for k in range(1, N_DEV):
            gather_taps(k).wait_recv()
        for k in range(1, N_DEV):
            gather(k).wait_send()
            scatter(k).wait_send()
            gather_taps(k).wait_send()

    vmem = pl.BlockSpec(memory_space=pltpu.VMEM)
    return pl.pallas_call(
        body, name="ada_exchange",
        in_specs=[vmem] * 4, out_specs=[vmem] * 3,
        out_shape=[jax.ShapeDtypeStruct((N_DEV, 8, d), F32), jax.ShapeDtypeStruct((N_DEV, 8, ncol), F32),
                   jax.ShapeDtypeStruct((N_DEV,) + taps.shape, F32)],
        scratch_shapes=[pltpu.VMEM((N_DEV, 8, ncol), F32), pltpu.SemaphoreType.DMA((3, N_DEV - 1)),
                        pltpu.SemaphoreType.DMA((3, N_DEV - 1))],
        compiler_params=_params(),
    )(c_rows, w_ada, b_ada_cols, taps)


def _entry_exchange(c_rows, w_ada, b_ada_cols, taps, shards):
    d = c_rows.shape[1]
    ncol = w_ada.shape[1]
    n_w = len(shards)

    def body(c_ref, w_ref, b_ref, t_ref, *rest):
        srcs = rest[:n_w]
        call_ref, mod_ref, tall_ref = rest[n_w:n_w + 3]
        outs = rest[n_w + 3:2 * n_w + 3]
        stage_ref, s_send, s_recv, w_send, w_recv, local_sems = rest[2 * n_w + 3:]
        x, y, c = _place()
        here, sibling = (x, y, c), (x, y, 1 - c)
        chips = [(1 - x, y), (x, 1 - y), (1 - x, 1 - y)]
        me = _index(here)

        def small(kind, src, dst, k):
            return pltpu.make_async_remote_copy(src_ref=src, dst_ref=dst, send_sem=s_send.at[kind, k - 1],
                                                recv_sem=s_recv.at[kind, k - 1], device_id=_peer(k), device_id_type=MESH)

        gather = lambda k: small(0, c_ref, call_ref.at[me], k)
        scatter = lambda k: small(1, stage_ref.at[_index(_peer(k))], mod_ref.at[me], k)
        gather_taps = lambda k: small(2, t_ref, tall_ref.at[me], k)

        def rows(w, pos):
            r = shards[w].shape[0]
            return outs[w].at[pl.ds(pl.multiple_of(_index(pos) * r, 16), r), :]

        def block(k, w, pos, to, own=False):
            return pltpu.make_async_remote_copy(
                src_ref=srcs[w] if own else rows(w, pos), dst_ref=rows(w, pos),
                send_sem=w_send.at[k, w], recv_sem=w_recv.at[k, w], device_id=to, device_id_type=MESH)

        call_ref[me] = c_ref[...]
        tall_ref[me] = t_ref[...]
        for k in range(1, N_DEV):
            gather(k).start()
        for k in range(1, N_DEV):
            gather_taps(k).start()
        mine = [pltpu.make_async_copy(srcs[w], rows(w, here), local_sems.at[w]) for w in range(n_w)]
        for cp in mine:
            cp.start()
        first = [block(0, w, here, sibling, own=True) for w in range(n_w)]
        first += [block(1 + j, w, here, (*chip, c), own=True) for j, chip in enumerate(chips) for w in range(n_w)]
        for cp in first:
            cp.start()

        for k in range(1, N_DEV):
            gather(k).wait_recv()
        cv = jnp.concatenate([call_ref[b, 0:1, :] for b in range(N_DEV)], axis=0)
        act = cv * jax.nn.sigmoid(cv)
        mod = lax.dot_general(act, w_ref[...], NN, preferred_element_type=F32,
                              precision=lax.Precision.HIGHEST) + b_ref[...]
        for b in range(N_DEV):
            stage_ref[b] = jnp.broadcast_to(mod[b:b + 1, :], (8, ncol))
        mod_ref[me] = stage_ref[me]
        for k in range(1, N_DEV):
            scatter(k).start()

        passed = []
        for j, chip in enumerate(chips):
            for w in range(n_w):
                block(1 + j, w, (*chip, c), here).wait_recv()
                fwd = block(4 + j, w, (*chip, c), sibling)
                fwd.start()
                passed.append(fwd)
        for w in range(n_w):
            block(0, w, sibling, here).wait_recv()
        for j, chip in enumerate(chips):
            for w in range(n_w):
                block(4 + j, w, (*chip, 1 - c), here).wait_recv()
        for k in range(1, N_DEV):
            scatter(k).wait_recv()
            gather_taps(k).wait_recv()
        for cp in first + passed:
            cp.wait_send()
        for k in range(1, N_DEV):
            gather(k).wait_send()
            scatter(k).wait_send()
            gather_taps(k).wait_send()
        for cp in mine:
            cp.wait()

    vmem, hbm = pl.BlockSpec(memory_space=pltpu.VMEM), pl.BlockSpec(memory_space=pltpu.HBM)
    out = pl.pallas_call(
        body, name="entry_exchange",
        in_specs=[vmem] * 4 + [hbm] * n_w, out_specs=[vmem] * 3 + [hbm] * n_w,
        out_shape=[jax.ShapeDtypeStruct((N_DEV, 8, d), F32), jax.ShapeDtypeStruct((N_DEV, 8, ncol), F32),
                   jax.ShapeDtypeStruct((N_DEV,) + taps.shape, F32)]
        + [jax.ShapeDtypeStruct((N_DEV * s.shape[0], s.shape[1]), s.dtype) for s in shards],
        scratch_shapes=[pltpu.VMEM((N_DEV, 8, ncol), F32), pltpu.SemaphoreType.DMA((3, N_DEV - 1)),
                        pltpu.SemaphoreType.DMA((3, N_DEV - 1)), pltpu.SemaphoreType.DMA((N_DEV - 1, n_w)),
                        pltpu.SemaphoreType.DMA((N_DEV - 1, n_w)), pltpu.SemaphoreType.DMA((n_w,))],
        compiler_params=_params(),
    )(c_rows, w_ada, b_ada_cols, taps, *shards)
    return out[0], out[1], out[2], out[3:]


def _gather_weights(shards):
    n_w = len(shards)

    def body(*refs):
        srcs, outs = refs[:n_w], refs[n_w:2 * n_w]
        send_sems, recv_sems, local_sems = refs[2 * n_w:]
        x, y, c = _place()
        me, sibling = (x, y, c), (x, y, 1 - c)
        chips = [(1 - x, y), (x, 1 - y), (1 - x, 1 - y)]

        def rows(w, pos):
            r = shards[w].shape[0]
            return outs[w].at[pl.ds(pl.multiple_of(_index(pos) * r, 16), r), :]

        def copy(k, w, block, to, own=False):
            return pltpu.make_async_remote_copy(
                src_ref=srcs[w] if own else rows(w, block), dst_ref=rows(w, block),
                send_sem=send_sems.at[k, w], recv_sem=recv_sems.at[k, w], device_id=to, device_id_type=MESH)

        mine = [pltpu.make_async_copy(srcs[w], rows(w, me), local_sems.at[w]) for w in range(n_w)]
        for cp in mine:
            cp.start()
        first = [copy(0, w, me, sibling, own=True) for w in range(n_w)]
        first += [copy(1 + j, w, me, (*chip, c), own=True) for j, chip in enumerate(chips) for w in range(n_w)]
        for cp in first:
            cp.start()
        passed = []
        for j, chip in enumerate(chips):
            for w in range(n_w):
                copy(1 + j, w, (*chip, c), me).wait_recv()
                fwd = copy(4 + j, w, (*chip, c), sibling)
                fwd.start()
                passed.append(fwd)
        for w in range(n_w):
            copy(0, w, sibling, me).wait_recv()
        for j, chip in enumerate(chips):
            for w in range(n_w):
                copy(4 + j, w, (*chip, 1 - c), me).wait_recv()
        for cp in first + passed:
            cp.wait_send()
        for cp in mine:
            cp.wait()

    hbm = pl.BlockSpec(memory_space=pltpu.HBM)
    return pl.pallas_call(
        body, name="gather_weights",
        in_specs=[hbm] * n_w, out_specs=[hbm] * n_w,
        out_shape=[jax.ShapeDtypeStruct((N_DEV * s.shape[0], s.shape[1]), s.dtype) for s in shards],
        scratch_shapes=[pltpu.SemaphoreType.DMA((N_DEV - 1, n_w)), pltpu.SemaphoreType.DMA((N_DEV - 1, n_w)),
                        pltpu.SemaphoreType.DMA((n_w,))],
        compiler_params=_params(),
    )(*shards)


def _peer_copies(mode, srcs, lands, send_sems, recv_sems):
    if mode in ("gather_ici", "gather_d2d"):
        x, y, c = _place()
        sibling = (x, y, 1 - c)
        chips = [(1 - x, y), (x, 1 - y), (1 - x, 1 - y)]
        n = len(lands)

        def rows(w, pos):
            r = lands[w].shape[0] // N_DEV
            return lands[w].at[pl.ds(pl.multiple_of(_index(pos) * r, 16), r), :]

        def copy(k, w, src, dst, to):
            return pltpu.make_async_remote_copy(src_ref=src, dst_ref=dst, send_sem=send_sems.at[k * n + w],
                                                recv_sem=recv_sems.at[k * n + w], device_id=to, device_id_type=MESH)

        if mode == "gather_ici":
            targets = [sibling] + [(*chip, c) for chip in chips]
            return [copy(k, w, srcs[w], rows(w, (x, y, c)), to) for k, to in enumerate(targets) for w in range(n)]
        return [copy(j, w, rows(w, (*chip, c)), rows(w, (*chip, c)), sibling)
                for j, chip in enumerate(chips) for w in range(n)]
    me = _index(_place())
    copies = []
    for k in range(1, N_DEV):
        peer = _peer(k)
        for w, (src, land) in enumerate(zip(srcs, lands)):
            if mode == "gather":
                r = src.shape[0]
                dst = land.at[pl.ds(pl.multiple_of(me * r, 16), r), :]
            elif mode == "allgather":
                dst = land.at[me]
            else:
                r = src.shape[0] // N_DEV
                src = src.at[pl.ds(pl.multiple_of(_index(peer) * r, 16), r), :]
                dst = land.at[me]
            copies.append(pltpu.make_async_remote_copy(
                src_ref=src, dst_ref=dst, send_sem=send_sems.at[(k - 1) * len(srcs) + w],
                recv_sem=recv_sems.at[(k - 1) * len(srcs) + w],
                device_id=peer, device_id_type=MESH))
    return copies


def _landing_zone(mode, src, me, name):
    cols = src.shape[1]
    if mode == "gather":
        r = src.shape[0]
        in_spec = pl.BlockSpec((r, cols), lambda i, me_ref: (0, 0))
        out_spec = pl.BlockSpec((r, cols), lambda i, me_ref: (me_ref[0], 0))
        out_shape = jax.ShapeDtypeStruct((N_DEV * r, cols), src.dtype)
    else:
        r = src.shape[0] // N_DEV
        in_spec = pl.BlockSpec((r, cols), lambda i, me_ref: (me_ref[0], 0))
        out_spec = pl.BlockSpec((1, r, cols), lambda i, me_ref: (me_ref[0], 0, 0))
        out_shape = jax.ShapeDtypeStruct((N_DEV, r, cols), src.dtype)

    def body(me_ref, s_ref, o_ref):
        o_ref[...] = s_ref[...].reshape(o_ref.shape)

    return pl.pallas_call(
        body, name=name, out_shape=out_shape,
        grid_spec=pltpu.PrefetchScalarGridSpec(num_scalar_prefetch=1, grid=(1,), in_specs=[in_spec], out_specs=out_spec),
        compiler_params=_params(("arbitrary",)),
    )(me.reshape(1).astype(jnp.int32), src)


def _exchange_start(mode, srcs, lands, name):
    n_s, n_a = len(srcs), len(srcs) + len(lands)
    n_cp = _COPIES_PER_ARRAY.get(mode, N_DEV - 1) * len(lands)

    def body(*refs):
        for cp in _peer_copies(mode, refs[:n_s], refs[n_s:n_a], refs[n_a], refs[n_a + 1]):
            cp.start()
        refs[-1][...] = jnp.zeros_like(refs[-1])

    hbm, sem = pl.BlockSpec(memory_space=pltpu.HBM), pl.BlockSpec(memory_space=pltpu.SEMAPHORE)
    arrays = list(srcs) + list(lands)
    out = pl.pallas_call(
        body, name=name,
        out_shape=(pltpu.SemaphoreType.DMA((n_cp,)), pltpu.SemaphoreType.DMA((n_cp,)),
                   *[pltpu.HBM(a.shape, a.dtype) for a in arrays], jax.ShapeDtypeStruct((8, LANES), F32)),
        in_specs=[hbm] * n_a, out_specs=(sem, sem, *[hbm] * n_a, pl.BlockSpec(memory_space=pltpu.VMEM)),
        input_output_aliases={i: 2 + i for i in range(n_a)},
        compiler_params=pltpu.CompilerParams(has_side_effects=pltpu.SideEffectType.DATAFLOW_SIDE_EFFECTING),
    )(*[pltpu.with_memory_space_constraint(a, pltpu.HBM) for a in arrays])
    return out[0], out[1], out[2:2 + n_s], out[2 + n_s:2 + n_a], out[-1]


_COPIES_PER_ARRAY = {"gather_ici": 4, "gather_d2d": 3}


def _exchange_wait(mode, send_sems, recv_sems, srcs, lands, after, name):
    n_s, n_a = len(srcs), len(srcs) + len(lands)

    def body(*refs):
        copies = _peer_copies(mode, refs[:n_s], refs[n_s:n_a], refs[n_a], refs[n_a + 1])
        for cp in copies:
            cp.wait_send()
        for cp in copies:
            cp.wait_recv()

    hbm, sem = pl.BlockSpec(memory_space=pltpu.HBM), pl.BlockSpec(memory_space=pltpu.SEMAPHORE)
    arrays = list(srcs) + list(lands)
    out = pl.pallas_call(
        body, name=name, out_shape=tuple(pltpu.HBM(a.shape, a.dtype) for a in arrays),
        in_specs=[hbm] * n_a + [sem, sem, pl.BlockSpec(memory_space=pl.ANY)], out_specs=tuple([hbm] * n_a),
        input_output_aliases={i: i for i in range(n_a)},
        compiler_params=pltpu.CompilerParams(has_side_effects=pltpu.SideEffectType.DATAFLOW_SIDE_EFFECTING),
    )(*arrays, send_sems, recv_sems, after)
    return out[n_s:]


SMALL_WEIGHTS = ("b_ada", "g_pre_mix", "g_post_mix", "g_pre_ffn", "g_post_ffn", "w_pool", "b_pool", "pool_scale", "conv_b")


MOD_ROWS = ((0, 0), (0, 1), (1, 3), (1, 0), (1, 1), (2, 0))


def _small_sum(mine, gathered):
    n_l = len(mine)
    d = mine[0].shape[1]

    def body(*refs):
        loc, got = refs[:n_l], refs[n_l:2 * n_l]
        tot_refs, dmod_ref = refs[2 * n_l:3 * n_l], refs[3 * n_l]
        me = _index(_place())
        part = lambda a, dev: jnp.where(dev == me, loc[a][...], got[a][dev])
        for a in range(n_l):
            tot = part(a, 0)
            for dev in range(1, N_DEV):
                tot = tot + part(a, dev)
            tot_refs[a][...] = tot
        for dev in range(N_DEV):
            for k, (a, r) in enumerate(MOD_ROWS):
                dmod_ref[dev:dev + 1, k * d:(k + 1) * d] = part(a, dev)[r:r + 1, :]

    vmem = pl.BlockSpec(memory_space=pltpu.VMEM)
    out = pl.pallas_call(
        body, name="small_sum", in_specs=[vmem] * (2 * n_l), out_specs=[vmem] * (n_l + 1),
        out_shape=[jax.ShapeDtypeStruct(a.shape, F32) for a in mine] + [jax.ShapeDtypeStruct((N_DEV, 6 * d), F32)],
        compiler_params=_params(),
    )(*mine, *gathered)
    return out[:n_l], out[n_l]


def _small_adam(totals, weights, moms, vels):
    n_t, n_w = len(totals), len(weights)

    def body(*refs):
        t_in, t_mix, t_ffn, t_pool, t_blk, t_conv, _ = (r[...] for r in refs[:n_t])
        w_refs, m_refs, v_refs = (refs[n_t + k * n_w:n_t + (k + 1) * n_w] for k in range(3))
        outs = refs[n_t + 3 * n_w:]

        def update(idx, g, at=()):
            sel = lambda ref: ref.at[at] if at else ref
            delta, nm, nv = _adam_math(sel(w_refs[idx])[...], g, sel(m_refs[idx])[...], sel(v_refs[idx])[...])
            for k, val in enumerate((g, delta, nm, nv)):
                sel(outs[4 * idx + k])[...] = val

        tots = (t_in, t_mix, t_ffn)
        update(0, jnp.concatenate([tots[a][r:r + 1] for a, r in MOD_ROWS], axis=1))
        update(1, t_in[2:3])
        update(2, t_mix[4:5])
        update(3, t_mix[2:3])
        update(4, t_ffn[1:2])
        for gi in range(len(POOL_WINDOWS)):
            update(5, t_blk[gi], at=(0, gi))
        update(6, jnp.concatenate([t_pool[0:1, gi * HEAD_DIM:(gi + 1) * HEAD_DIM] for gi in range(len(POOL_WINDOWS))], axis=0),
               at=(0,))
        update(7, t_pool[1:2])
        update(8, t_conv[3:4])

    vmem = pl.BlockSpec(memory_space=pltpu.VMEM)
    return pl.pallas_call(
        body, name="small_adam", in_specs=[vmem] * (n_t + 3 * n_w), out_specs=[vmem] * (4 * n_w),
        out_shape=[jax.ShapeDtypeStruct(w.shape, F32) for w in weights for _ in range(4)],
        compiler_params=_params(),
    )(*totals, *weights, *moms, *vels)


def _adam_math(w, g, m, v):
    m = ADAM_B1 * m + (1.0 - ADAM_B1) * g
    v = ADAM_B2 * v + (1.0 - ADAM_B2) * (g * g)
    m_hat = m / (1.0 - ADAM_B1 ** ADAM_STEP)
    v_hat = v / (1.0 - ADAM_B2 ** ADAM_STEP)
    delta = -ADAM_LR * (m_hat / (jnp.sqrt(v_hat) + ADAM_EPS) + ADAM_WD * w)
    return delta, m, v


def _adam(w, g, m, v, name, tr):
    rows, cols = w.shape

    def body(w_ref, g_ref, m_ref, v_ref, d_ref, nm_ref, nv_ref):
        d_ref[...], nm_ref[...], nv_ref[...] = _adam_math(w_ref[...], g_ref[...], m_ref[...], v_ref[...])

    spec = pl.BlockSpec((tr, cols), lambda i: (i, 0))
    shape = jax.ShapeDtypeStruct((rows, cols), F32)
    return pl.pallas_call(
        body, name=name, grid=(rows // tr,), in_specs=[spec] * 4, out_specs=[spec] * 3,
        out_shape=[shape] * 3, compiler_params=_params(("arbitrary",)),
    )(w, g, m, v)


def _sum_adam(parts, w, m, v, name, tr):
    _, rows, cols = parts.shape

    def body(p_ref, w_ref, m_ref, v_ref, g_ref, d_ref, nm_ref, nv_ref):
        g = p_ref[0].astype(F32)
        for dev in range(1, N_DEV):
            g = g + p_ref[dev].astype(F32)
        g_ref[...] = g
        d_ref[...], nm_ref[...], nv_ref[...] = _adam_math(w_ref[...], g, m_ref[...], v_ref[...])

    spec = pl.BlockSpec((tr, cols), lambda i: (i, 0))
    shape = jax.ShapeDtypeStruct((rows, cols), F32)
    return pl.pallas_call(
        body, name=name, grid=(rows // tr,),
        in_specs=[pl.BlockSpec((N_DEV, tr, cols), lambda i: (0, i, 0)), spec, spec, spec],
        out_specs=[spec] * 4, out_shape=[shape] * 4, compiler_params=_params(("arbitrary",)),
    )(parts, w, m, v)


def _ada_grad_adam(c_all, dmod_cols, w, m, v, tr):
    rows, cols = w.shape

    def body(c_ref, dm_ref, w_ref, m_ref, v_ref, g_ref, d_ref, nm_ref, nv_ref):
        cv = c_ref[...]
        act = cv * jax.nn.sigmoid(cv)
        g = lax.dot_general(act, dm_ref[...], TN, preferred_element_type=F32, precision=lax.Precision.HIGHEST)
        g_ref[...] = g
        d_ref[...], nm_ref[...], nv_ref[...] = _adam_math(w_ref[...], g, m_ref[...], v_ref[...])

    spec = pl.BlockSpec((tr, cols), lambda i: (i, 0))
    shape = jax.ShapeDtypeStruct((rows, cols), F32)
    return pl.pallas_call(
        body, name="ada_grad_adam", grid=(rows // tr,),
        in_specs=[pl.BlockSpec((N_DEV, tr), lambda i: (0, i)), pl.BlockSpec((N_DEV, cols), lambda i: (0, 0)), spec, spec, spec],
        out_specs=[spec] * 4, out_shape=[shape] * 4, compiler_params=_params(("arbitrary",)),
    )(c_all, dmod_cols, w, m, v)


def _rope_tables(positions):
    s_len = positions.shape[0]
    inv_freq = ROPE_THETA ** (-jnp.arange(0, 2 * ROT_HALF, 2, dtype=F32) / (2 * ROT_HALF))
    ang = positions.astype(F32)[:, None] * inv_freq
    cos, sin = jnp.cos(ang), jnp.sin(ang)
    rest = HEAD_DIM - 2 * ROT_HALF
    zero = lambda n: jnp.zeros((s_len, n), F32)
    head = jnp.stack([jnp.concatenate([cos, cos, jnp.ones((s_len, rest), F32)], axis=1),
                      jnp.concatenate([-sin, zero(HEAD_DIM - ROT_HALF)], axis=1),
                      jnp.concatenate([zero(ROT_HALF), sin, zero(rest)], axis=1)])
    return jnp.tile(head, (1, 1, LANES // HEAD_DIM))


def _pad_rows(a, rows):
    return jnp.pad(a, ((0, rows - a.shape[0]), (0, 0)))


def _sequence_step(xs, target, rope, mods, gains, w_in_t, w_out_t, relay_ffn, fetch_ffn, send_ffn_grads, send_mix_grads, w_blk_b, b_pool_r,
                   pool_scale_r, conv_w_all, conv_b):
    sh_m, sc_m, gt_m, sh_f, sc_f, gt_f = mods
    g_pre_mix, g_post_mix, g_pre_ffn, g_post_ffn = gains
    h1, u_pool, qkv = _premix_inproj(xs, sh_m, sc_m, g_pre_mix, w_in_t, rope, tm=512)
    o_g, lse_g = _attn_fwd(qkv)
    x1, y1, h2, cat, attn, lse_all = _mix_out(xs, u_pool, o_g, lse_g, w_blk_b, b_pool_r, pool_scale_r, w_out_t,
                                              gt_m, g_post_mix, g_pre_ffn, sc_f, sh_f, tm=256)
    token = relay_ffn(x1)
    w_up_t, w_down_f = fetch_ffn(x1 if token is None else token)
    gate, a_ffn, act, vd, dy2, dout, sums_ffn, loss_loc = _ffn_fwd_loss(h2, x1, target, w_up_t, w_down_f, conv_w_all, conv_b,
                                                              gt_f, g_post_ffn, tm=256, tf=2816, ck=256)

    dgc, dval, dw_down, dconv = _ffn_bwd_act(dy2, gate, a_ffn, act, vd, w_down_f, tm=256, ck=256)
    dup, dh2 = _ffn_bwd_up(dgc, dval, w_up_t, conv_w_all, tm=256)
    dw_up_t = _wgrad(dup, h2, "wgrad_up", tk=2048, tmm=1408)
    token = send_ffn_grads(dw_up_t, dw_down)
    if token is not None:
        sc_f = sc_f + token[0:1, 0:1]
    dx1, dpool, dattn, delta, dw_out_t, sums_mix = _mix_bwd(dh2, dout, x1, y1, cat, attn, w_out_t, sc_f, g_pre_ffn,
                                                           gt_m, g_post_mix, tm=256)
    du, dw_blk, sums_pool = _pool_bwd(dpool, u_pool, w_blk_b, b_pool_r, pool_scale_r, tm=512)
    dproj = _dproj_assemble(du, _attn_bwd(qkv, dattn, lse_all, delta), rope, tm=512)
    dw_in_t = _wgrad(dproj, h1, "wgrad_in", tk=2048, tmm=1280)
    token = send_mix_grads(dw_in_t, dw_out_t)
    if token is not None:
        sc_m = sc_m + token[0:1, 0:1]
    grad_x, sums_in = _inproj_bwd(dproj, w_in_t, xs, dx1, sc_m, g_pre_mix, tm=256)
    return (loss_loc, grad_x, dw_in_t, dw_out_t, dw_up_t, dw_down, dw_blk, dconv,
            sums_in, sums_mix, sums_ffn, sums_pool)


def kernel(x, c, positions, w_ada, b_ada, g_pre_mix, g_post_mix, g_pre_ffn, g_post_ffn, w_in, w_pool, b_pool, pool_scale, w_out, w_up, conv_w, conv_b, w_down, loss_target, m_w_ada, m_b_ada, m_g_pre_mix, m_g_post_mix, m_g_pre_ffn, m_g_post_ffn, m_w_in, m_w_pool, m_b_pool, m_pool_scale, m_w_out, m_w_up, m_conv_w, m_conv_b, m_w_down, v_w_ada, v_b_ada, v_g_pre_mix, v_g_post_mix, v_g_pre_ffn, v_g_post_ffn, v_w_in, v_w_pool, v_b_pool, v_pool_scale, v_w_out, v_w_up, v_conv_w, v_conv_b, v_w_down):
    s_len, d = x.shape[1], x.shape[2]
    d_ff = w_down.shape[1] * N_DEV
    me = _index(_place())
    xs, target = x[0], loss_target[0]

    ncol = w_ada.shape[2]
    b_cols = lax.dynamic_slice(b_ada, (0, me * ncol), (1, ncol))
    c_all, mod, taps_all, (w_in_t, w_out_t) = _entry_exchange(
        jnp.broadcast_to(c, (8, d)), w_ada[0], b_cols, _pad_rows(conv_w[0], 8),
        [w_in[0].T.astype(BF16), w_out[0].T.astype(BF16)])
    c_all = c_all[:, 0, :]
    conv_w_all = jnp.transpose(taps_all[:, :3, :], (1, 0, 2)).reshape(3, d_ff)
    sh_m, sc_m, gt_m, sh_f, sc_f, gt_f = [mod[:, 0, :].reshape(1, -1)[:, k * d:(k + 1) * d] for k in range(6)]

    rope = _rope_tables(positions[0])
    w_blk = jnp.zeros((256, 256), F32)
    for gi in range(4):
        w_blk = lax.dynamic_update_slice(w_blk, w_pool[0, gi], (gi * HEAD_DIM, gi * HEAD_DIM))
    w_blk_b = w_blk.astype(BF16)
    b_pool_r, pool_scale_r = b_pool.reshape(1, 256), pool_scale.reshape(1, 256)

    up_sh, down_sh = w_up[0].T.astype(BF16), w_down[0].astype(BF16)
    w_in_t, conv_w_all, up_sh, down_sh = lax.optimization_barrier((w_in_t, conv_w_all, up_sh, down_sh))
    lands = [_landing_zone("gather", s, me, "land_" + nm) for s, nm in ((up_sh, "w_up"), (down_sh, "w_down"))]
    w_send, w_recv, w_src, w_land, w_token = _exchange_start("gather_ici", [up_sh, down_sh], lands, "ffn_weights_ici_start")
    relay = []

    def relay_ffn(after):
        arrived = _exchange_wait("gather_ici", w_send, w_recv, w_src, w_land, after, "ffn_weights_ici_wait")
        relay.extend(_exchange_start("gather_d2d", [], arrived, "ffn_weights_d2d_start"))
        return relay[4]

    def fetch_ffn(after):
        return _exchange_wait("gather_d2d", relay[0], relay[1], [], relay[3], after, "ffn_weights_d2d_wait")

    flight = []

    def send_ffn_grads(dw_up_t, dw_down):
        lands = [_landing_zone("scatter", dw_up_t, me, "land_dw_up"), _landing_zone("scatter", dw_down, me, "land_dw_down")]
        flight.extend(_exchange_start("scatter", [dw_up_t, dw_down], lands, "ffn_grads_start"))
        return flight[4]

    mix_flight = []

    def send_mix_grads(dw_in_t, dw_out_t):
        lands = [_landing_zone("scatter", dw_in_t, me, "land_dw_in"), _landing_zone("scatter", dw_out_t, me, "land_dw_out")]
        mix_flight.extend(_exchange_start("scatter", [dw_in_t, dw_out_t], lands, "mix_grads_start"))
        return mix_flight[4]

    (loss_loc, grad_x, dw_in_t, dw_out_t, _, _, dw_pool, dconv,
     sums_in, sums_mix, sums_ffn, sums_pool) = _sequence_step(
        xs, target, rope, (sh_m + w_token[0:1, 0:1], sc_m, gt_m, sh_f, sc_f, gt_f),
        (g_pre_mix, g_post_mix, g_pre_ffn, g_post_ffn),
        w_in_t, w_out_t, relay_ffn, fetch_ffn, send_ffn_grads, send_mix_grads, w_blk_b, b_pool_r, pool_scale_r, conv_w_all, conv_b)

    small = [sums_in, sums_mix, sums_ffn, sums_pool, dw_pool, dconv, loss_loc]
    small_flight = _exchange_start("allgather", small, [lax.empty((N_DEV,) + a.shape, F32) for a in small], "small_start")

    parts_ffn = _exchange_wait("scatter", *flight[:4], small_flight[4], "ffn_grads_wait")
    big = {
        "w_up": [a.T for a in _sum_adam(parts_ffn[0], w_up[0].T, m_w_up[0].T, v_w_up[0].T, "adam_w_up", 64)],
        "w_down": _sum_adam(parts_ffn[1], w_down[0], m_w_down[0], v_w_down[0], "adam_w_down", 32),
    }

    rep_w = [b_ada, g_pre_mix, g_post_mix, g_pre_ffn, g_post_ffn, w_pool, b_pool, pool_scale, conv_b]
    rep_m = [m_b_ada, m_g_pre_mix, m_g_post_mix, m_g_pre_ffn, m_g_post_ffn, m_w_pool, m_b_pool, m_pool_scale, m_conv_b]
    rep_v = [v_b_ada, v_g_pre_mix, v_g_post_mix, v_g_pre_ffn, v_g_post_ffn, v_w_pool, v_b_pool, v_pool_scale, v_conv_b]
    parts_mix = _exchange_wait("scatter", *mix_flight[:4], big["w_down"][0], "mix_grads_wait")
    big["w_in"] = [a.T for a in _sum_adam(parts_mix[0], w_in[0].T, m_w_in[0].T, v_w_in[0].T, "adam_w_in", 64)]
    big["w_out"] = [a.T for a in _sum_adam(parts_mix[1], w_out[0].T, m_w_out[0].T, v_w_out[0].T, "adam_w_out", 128)]
    gathered = _exchange_wait("allgather", *small_flight[:4], big["w_out"][0], "small_wait")
    totals, dmod_all = _small_sum(small, gathered)
    dconv_tot, loss_tot = totals[5], totals[6]
    rep_out = _small_adam(totals, rep_w, rep_m, rep_v)
    g_rep, d_rep, nm_rep, nv_rep = (rep_out[k::4] for k in range(4))

    fcol = d_ff // N_DEV
    g_cw = lax.dynamic_slice(dconv_tot, (0, me * fcol), (3, fcol))
    d_cw, nm_cw, nv_cw = _adam(conv_w[0], g_cw, m_conv_w[0], v_conv_w[0], "adam_conv_w", 3)

    dmod_cols = lax.dynamic_slice(dmod_all, (0, me * ncol), (N_DEV, ncol))
    g_ada, d_ada, nm_ada, nv_ada = _ada_grad_adam(c_all, dmod_cols, w_ada[0], m_w_ada[0], v_w_ada[0], 256)

    loss = loss_tot[0, 0]

    def group(k):
        rep = (g_rep, d_rep, nm_rep, nv_rep)[k]
        ada = (g_ada, d_ada, nm_ada, nv_ada)[k][None]
        cw = (g_cw, d_cw, nm_cw, nv_cw)[k][None]
        return [ada, rep[0], rep[1], rep[2], rep[3], rep[4], big["w_in"][k][None], rep[5], rep[6], rep[7],
                big["w_out"][k][None], big["w_up"][k][None], cw, rep[8], big["w_down"][k][None]]

    return (loss, grad_x[None], *group(0), *group(1), *group(2), *group(3))
```

```python
import functools
import math

import jax
import jax.numpy as jnp
from jax import lax
from jax.experimental import pallas as pl
from jax.experimental.pallas import tpu as pltpu

F32 = jnp.float32
BF16 = jnp.bfloat16
MESH = pl.DeviceIdType.MESH

N_DEV = 8
HEAD_DIM = 64
ROT_HALF = 8
ROPE_THETA = 500000.0
POOL_WINDOWS = (2, 4, 8, 16)
DILATIONS = (1, 4, 16)
BLOCK = 128
NORM_EPS = 1e-6
HALO = 16
MASKED = -1e30
ATTN_FWD_UNROLL = 8
ATTN_BWD_UNROLL = 4

ADAM_LR = 0.001
ADAM_B1 = 0.9
ADAM_B2 = 0.999
ADAM_EPS = 1e-08
ADAM_WD = 0.01
ADAM_STEP = 10

V7X_VMEM_LIMIT = 56 * 1024 * 1024
LANES = 128

NT = (((1,), (1,)), ((), ()))
NN = (((1,), (0,)), ((), ()))
TN = (((0,), (0,)), ((), ()))


def _dot(a, b, dims):
    return lax.dot_general(a, b, dims, preferred_element_type=F32)


def _params(sem=None, vmem=V7X_VMEM_LIMIT):
    if sem is None:
        return pltpu.CompilerParams(vmem_limit_bytes=vmem)
    return pltpu.CompilerParams(dimension_semantics=sem, vmem_limit_bytes=vmem)


def _rstd(v):
    return lax.rsqrt(jnp.mean(v * v, axis=-1, keepdims=True) + NORM_EPS)


def _norm_bwd(dn, n, rstd):
    return rstd * (dn - n * jnp.mean(dn * n, axis=-1, keepdims=True))


def _rope_lanes(cs_ref, spread_ref):
    return [lax.dot_general(cs_ref[...], spread_ref[k], NN, preferred_element_type=F32, precision=lax.Precision.HIGHEST)
            for k in range(3)]


def _rope_fwd(p, lanes):
    return p * lanes[0] + pltpu.roll(p, LANES - ROT_HALF, 1) * lanes[1] + pltpu.roll(p, ROT_HALF, 1) * lanes[2]


def _rope_bwd(dp, lanes):
    return dp * lanes[0] + pltpu.roll(dp * lanes[1], ROT_HALF, 1) + pltpu.roll(dp * lanes[2], LANES - ROT_HALF, 1)


def _gelu_parts(v):
    k2 = 2.0 * math.sqrt(2.0 / math.pi)
    c = 0.044715
    v2 = v * v
    s = jax.nn.sigmoid(v * (k2 + (k2 * c) * v2))
    g = v * s
    dg = s + g * (1.0 - s) * (k2 + (3.0 * k2 * c) * v2)
    return g, dg


def _halo_before(i, tile):
    return jnp.maximum(i * (tile // HALO) - 1, 0)


def _premix_inproj(x, sh, sc, g, w_in_t, rope, tm):
    s_len, d = x.shape
    n_proj = w_in_t.shape[0]
    n_slab = (n_proj - 256) // LANES

    def body(x_ref, sh_ref, sc_ref, g_ref, w_ref, cs_ref, spread_ref, h_ref, up_ref, qkv_ref):
        xv = x_ref[...]
        h = (xv * _rstd(xv) * g_ref[...]) * (1.0 + sc_ref[...]) + sh_ref[...]
        hb = h.astype(BF16)
        h_ref[...] = hb
        up_ref[...] = _dot(hb, w_ref[0:256, :], NT)
        lanes = _rope_lanes(cs_ref, spread_ref)
        for pair in range(n_slab // 2):
            p = _dot(hb, w_ref[256 + 256 * pair:512 + 256 * pair, :], NT)
            for half in range(2):
                ph = p[:, half * LANES:(half + 1) * LANES]
                if pair < 6:
                    ph = _rope_fwd(ph, lanes)
                if pair < 3:
                    ph = ph * (HEAD_DIM ** -0.5)
                qkv_ref[2 * pair + half] = ph

    vec = pl.BlockSpec((1, d), lambda i: (0, 0))
    return pl.pallas_call(
        body, name="premix_inproj", grid=(s_len // tm,),
        in_specs=[pl.BlockSpec((tm, d), lambda i: (i, 0)), vec, vec, vec,
                  pl.BlockSpec((n_proj, d), lambda i: (0, 0)),
                  pl.BlockSpec((tm, rope[0].shape[1]), lambda i: (i, 0)), pl.BlockSpec(rope[1].shape, lambda i: (0, 0, 0))],
        out_specs=[pl.BlockSpec((tm, d), lambda i: (i, 0)),
                   pl.BlockSpec((tm, 256), lambda i: (i, 0)),
                   pl.BlockSpec((n_slab, tm, LANES), lambda i: (0, i, 0))],
        out_shape=[jax.ShapeDtypeStruct((s_len, d), BF16),
                   jax.ShapeDtypeStruct((s_len, 256), F32),
                   jax.ShapeDtypeStruct((n_slab, s_len, LANES), F32)],
        compiler_params=_params(("arbitrary",)),
    )(x, sh, sc, g, w_in_t, *rope)


def _block_rows(n, r, dil):
    start = n * (BLOCK * dil) + r
    if dil == 1:
        return pl.ds(pl.multiple_of(start, BLOCK), BLOCK)
    return pl.ds(start, BLOCK, stride=dil)


def _band_mask(n):
    ri = lax.broadcasted_iota(jnp.int32, (BLOCK, 2 * BLOCK), 0)
    cj = lax.broadcasted_iota(jnp.int32, (BLOCK, 2 * BLOCK), 1)
    cur = (cj >= BLOCK) & (cj - BLOCK <= ri)
    prev = (cj < BLOCK) & (cj >= ri) & (n > 0)
    return cur | prev


def _attn_fwd(qkv):
    s_len = qkv.shape[1]
    n_g = len(DILATIONS)

    def body(q_ref, k_ref, v_ref, o_ref, lse_ref):
        lane = lax.broadcasted_iota(jnp.int32, (BLOCK, LANES), 1)
        first = lane < HEAD_DIM

        def group(dil):
            nb = s_len // (BLOCK * dil)

            def block(t, carry):
                r, n = t // nb, t % nb
                cur = _block_rows(n, r, dil)
                prev = _block_rows(jnp.maximum(n - 1, 0), r, dil)
                q = q_ref[0, cur, :]
                kcat = jnp.concatenate([k_ref[0, prev, :], k_ref[0, cur, :]], axis=0).astype(BF16)
                vcat = jnp.concatenate([v_ref[0, prev, :], v_ref[0, cur, :]], axis=0).astype(BF16)
                valid = _band_mask(n)
                q2 = jnp.concatenate([jnp.where(first, q, 0.0), jnp.where(first, 0.0, q)], axis=0).astype(BF16)
                s = jnp.where(jnp.concatenate([valid, valid], axis=0), _dot(q2, kcat, NT), MASKED)
                m = jnp.max(s, axis=-1, keepdims=True)
                p = jnp.exp(s - m)
                den = jnp.sum(p, axis=-1, keepdims=True)
                o2 = _dot(p.astype(BF16), vcat, NN) / den
                lse2 = m + jnp.log(den)
                o_ref[0, 0, cur, :] = jnp.where(first, o2[:BLOCK], o2[BLOCK:])
                lse_ref[0, 0, cur, :] = jnp.where(first, lse2[:BLOCK], lse2[BLOCK:])
                return carry

            lax.fori_loop(0, nb * dil, block, 0, unroll=ATTN_FWD_UNROLL)

        for gi, dil in enumerate(DILATIONS):
            pl.when(pl.program_id(0) == gi)(functools.partial(group, dil))

    def slab(base):
        return pl.BlockSpec((1, s_len, LANES), lambda g, s: (base + 2 * g + s, 0, 0))

    out = pl.BlockSpec((1, 1, s_len, LANES), lambda g, s: (g, s, 0, 0))
    shape = jax.ShapeDtypeStruct((n_g, 2, s_len, LANES), F32)
    return pl.pallas_call(
        body, name="attn_fwd", grid=(n_g, 2),
        in_specs=[slab(0), slab(6), slab(12)], out_specs=[out, out], out_shape=[shape, shape],
        compiler_params=_params(("arbitrary", "arbitrary")),
    )(qkv, qkv, qkv)


def _pool_mixed(u, halo, i, tm):
    ue = jnp.concatenate([halo, u], axis=0)
    s2 = ue + pltpu.roll(ue, 1, 0)
    s4 = s2 + pltpu.roll(s2, 2, 0)
    s8 = s4 + pltpu.roll(s4, 4, 0)
    s16 = s8 + pltpu.roll(s8, 8, 0)
    grp = lax.broadcasted_iota(jnp.int32, (tm, 256), 1) // HEAD_DIM
    pick = lambda a, b, c, e: jnp.where(grp == 0, a, jnp.where(grp == 1, b, jnp.where(grp == 2, c, e)))
    win_sum = pick(s2[HALO:], s4[HALO:], s8[HALO:], s16[HALO:])
    pos = (i * tm + lax.broadcasted_iota(jnp.int32, (tm, 256), 0)).astype(F32)
    count = jnp.minimum(pos + 1.0, pick(*[float(w) for w in POOL_WINDOWS]))
    return win_sum / count - u, count


def _mix_out(x, u_pool, o_g, lse_g, w_blk, b_pool, pool_scale, w_out_t, gt_m, g_post_mix, g_pre_ffn, sc_f, sh_f, tm):
    s_len, d = x.shape

    def body(x_ref, u_ref, uh_ref, o_ref, l_ref, wb_ref, bp_ref, ps_ref, wo_ref,
             gt_ref, g1_ref, g2_ref, sc_ref, sh_ref,
             x1_ref, y1_ref, h2_ref, cat_ref, attn_ref, lall_ref):
        (o0, o1, o2), (l0, l1, l2) = (o_ref.at[g] for g in range(3)), (l_ref.at[g] for g in range(3))
        i = pl.program_id(0)
        u = u_ref[...]
        halo = uh_ref[...] * (i > 0).astype(F32)
        mixed, _ = _pool_mixed(u, halo, i, tm)
        y = _dot(mixed.astype(BF16), wb_ref[...], NN) + bp_ref[...]
        pool = y * ps_ref[...]
        attn = []
        for s in range(2):
            la, lb, lc = l0[s], l1[s], l2[s]
            mx = jnp.maximum(jnp.maximum(la, lb), lc)
            ea, eb, ec = jnp.exp(la - mx), jnp.exp(lb - mx), jnp.exp(lc - mx)
            den = ea + eb + ec
            lall_ref[s] = mx + jnp.log(den)
            attn.append((ea / den) * o0[s] + (eb / den) * o1[s] + (ec / den) * o2[s])
        attn = jnp.concatenate(attn, axis=1)
        attn_ref[...] = attn
        cat = jnp.concatenate([pool, attn], axis=1).astype(BF16)
        cat_ref[...] = cat
        y1 = _dot(cat, wo_ref[...], NT)
        y1_ref[...] = y1.astype(BF16)
        x1 = x_ref[...] + gt_ref[...] * (y1 * _rstd(y1) * g1_ref[...])
        x1_ref[...] = x1
        h2 = (x1 * _rstd(x1) * g2_ref[...]) * (1.0 + sc_ref[...]) + sh_ref[...]
        h2_ref[...] = h2.astype(BF16)

    tile = lambda w: pl.BlockSpec((tm, w), lambda i: (i, 0))
    slab = pl.BlockSpec((2, tm, LANES), lambda i: (0, i, 0))
    groups = pl.BlockSpec((len(DILATIONS), 2, tm, LANES), lambda i: (0, 0, i, 0))
    const = lambda a: pl.BlockSpec(a.shape, lambda i: (0,) * a.ndim)
    return pl.pallas_call(
        body, name="mix_out", grid=(s_len // tm,),
        in_specs=[tile(d), tile(256), pl.BlockSpec((HALO, 256), lambda i: (_halo_before(i, tm), 0)),
                  groups, groups,
                  const(w_blk), const(b_pool), const(pool_scale), const(w_out_t),
                  const(gt_m), const(g_post_mix), const(g_pre_ffn), const(sc_f), const(sh_f)],
        out_specs=[tile(d), tile(d), tile(d), tile(512), tile(256), slab],
        out_shape=[jax.ShapeDtypeStruct((s_len, d), F32), jax.ShapeDtypeStruct((s_len, d), BF16),
                   jax.ShapeDtypeStruct((s_len, d), BF16), jax.ShapeDtypeStruct((s_len, 512), BF16),
                   jax.ShapeDtypeStruct((s_len, 256), F32), jax.ShapeDtypeStruct((2, s_len, LANES), F32)],
        compiler_params=_params(("arbitrary",)),
    )(x, u_pool, u_pool, o_g, lse_g, w_blk, b_pool, pool_scale, w_out_t, gt_m, g_post_mix, g_pre_ffn, sc_f, sh_f)


def _conv_gate(gate_ext, cw, cb):
    gc = gate_ext * cw[2:3, :] + pltpu.roll(gate_ext, 1, 0) * cw[1:2, :] + pltpu.roll(gate_ext, 2, 0) * cw[0:1, :]
    return gc[HALO:] + cb


def _ffn_fwd_loss(h2, x1, target, w_up_t, w_down, conv_w, conv_b, gt_f, g_post_ffn, tm, tf, ck):
    s_len, d = x1.shape
    d_ff = w_down.shape[0]
    n_f = d_ff // tf

    def body(h_ref, hh_ref, x1_ref, tgt_ref, wg_ref, wv_ref, wd_ref, cw_ref, cb_ref, gt_ref, g_ref,
             gate_ref, a_ref, act_ref, vd_ref, dy2_ref, dout_ref, sums_ref, loss_ref, acc_ref):
        i, j = pl.program_id(0), pl.program_id(1)

        @pl.when((i == 0) & (j == 0))
        def _():
            sums_ref[...] = jnp.zeros_like(sums_ref)
            loss_ref[...] = jnp.zeros_like(loss_ref)

        h = h_ref[...]
        h_ext = jnp.concatenate([hh_ref[...], h], axis=0)
        row = lax.broadcasted_iota(jnp.int32, (tm + HALO, ck), 0)
        no_halo = (row < HALO) & (i == 0)

        def up(c):
            cs = slice(c * ck, (c + 1) * ck)
            return jnp.where(no_halo, 0.0, _dot(h_ext, wg_ref[cs, :], NT)), _dot(h, wv_ref[cs, :], NT)

        part = None
        n_c = tf // ck
        nxt = up(0)
        for c in range(n_c):
            cs = slice(c * ck, (c + 1) * ck)
            gate_ext, val = nxt
            if c + 1 < n_c:
                nxt = up(c + 1)
            act, dact = _gelu_parts(_conv_gate(gate_ext, cw_ref[:, cs], cb_ref[:, cs]))
            a = (act * val).astype(BF16)
            gate_ref[:, cs] = gate_ext[HALO:].astype(BF16)
            a_ref[:, cs] = a
            act_ref[:, cs] = act.astype(BF16)
            vd_ref[:, cs] = (val * dact).astype(BF16)
            p = _dot(a, wd_ref[cs, :], NN)
            part = p if part is None else part + p

        @pl.when(j == 0)
        def _():
            acc_ref[...] = part

        @pl.when(j > 0)
        def _():
            acc_ref[...] += part

        @pl.when(j == n_f - 1)
        def _():
            y2 = acc_ref[...]
            rstd = _rstd(y2)
            n = y2 * rstd
            rn = n * g_ref[...]
            err = x1_ref[...] + gt_ref[...] * rn - tgt_ref[...]
            loss_ref[...] += 0.5 * jnp.sum(jnp.mean(err * err, axis=-1, keepdims=True), axis=0, keepdims=True)
            dout = err * (1.0 / d)
            dout_ref[...] = dout
            drn = dout * gt_ref[...]
            sums_ref[0:1, :] += jnp.sum(dout * rn, axis=0, keepdims=True)
            sums_ref[1:2, :] += jnp.sum(drn * n, axis=0, keepdims=True)
            dy2_ref[...] = _norm_bwd(drn * g_ref[...], n, rstd).astype(BF16)

    tok = lambda w: pl.BlockSpec((tm, w), lambda i, j: (i, 0))
    tokf = pl.BlockSpec((tm, tf), lambda i, j: (i, j))
    vec = pl.BlockSpec((1, d), lambda i, j: (0, 0))
    once = {"pipeline_mode": pl.Buffered(1)} if n_f == 1 else {}
    return pl.pallas_call(
        body, name="ffn_fwd_loss", grid=(s_len // tm, n_f),
        in_specs=[tok(d), pl.BlockSpec((HALO, d), lambda i, j: (_halo_before(i, tm), 0)), tok(d), tok(d),
                  pl.BlockSpec((tf, d), lambda i, j: (j, 0), **once),
                  pl.BlockSpec((tf, d), lambda i, j: (j + n_f, 0), **once),
                  pl.BlockSpec((tf, d), lambda i, j: (j, 0), **once),
                  pl.BlockSpec((3, tf), lambda i, j: (0, j)), pl.BlockSpec((1, tf), lambda i, j: (0, j)), vec, vec],
        out_specs=[tokf, tokf, tokf, tokf, tok(d), tok(d), pl.BlockSpec((8, d), lambda i, j: (0, 0)),
                   pl.BlockSpec((8, LANES), lambda i, j: (0, 0))],
        out_shape=[jax.ShapeDtypeStruct((s_len, d_ff), BF16)] * 4
        + [jax.ShapeDtypeStruct((s_len, d), BF16), jax.ShapeDtypeStruct((s_len, d), F32),
                   jax.ShapeDtypeStruct((8, d), F32), jax.ShapeDtypeStruct((8, LANES), F32)],
        scratch_shapes=[pltpu.VMEM((tm, d), F32)],
        compiler_params=_params(("arbitrary", "arbitrary")),
    )(h2, h2, x1, target, w_up_t, w_up_t, w_down, conv_w, conv_b, gt_f, g_post_ffn)


def _ffn_bwd_act(dy2, gate, a, act, vd, w_down, tm, ck):
    s_len, d = dy2.shape
    d_ff = w_down.shape[0]
    n_t, n_c = s_len // tm, d_ff // ck

    def body(dy_ref, g_ref, gh_ref, a_ref, act_ref, vd_ref, wd_ref, dgc_ref, dval_ref, dwd_ref, dconv_ref, acc_ref):
        i = pl.program_id(0)

        @pl.when(i == 0)
        def _():
            acc_ref[...] = jnp.zeros_like(acc_ref)
            dconv_ref[...] = jnp.zeros_like(dconv_ref)

        dy = dy_ref[...]
        row = lax.broadcasted_iota(jnp.int32, (tm + HALO, ck), 0)
        no_halo = (row < HALO) & (i == 0)

        def down(c):
            return _dot(dy, wd_ref[c * ck:(c + 1) * ck, :], NT)

        nxt = down(0)
        for c in range(n_c):
            cs = slice(c * ck, (c + 1) * ck)
            da = nxt
            if c + 1 < n_c:
                nxt = down(c + 1)
            acc_ref[cs, :] += _dot(a_ref[:, cs], dy, TN)
            gate_ext = jnp.where(no_halo, 0.0, jnp.concatenate([gh_ref[:, cs], g_ref[:, cs]], axis=0).astype(F32))
            dgc = da * vd_ref[:, cs].astype(F32)
            dgc_ref[:, cs] = dgc.astype(BF16)
            dval_ref[:, cs] = (da * act_ref[:, cs].astype(F32)).astype(BF16)
            rows = [jnp.sum(dgc * pltpu.roll(gate_ext, 2 - k, 0)[HALO:], axis=0, keepdims=True) for k in range(2)]
            rows += [jnp.sum(dgc * gate_ext[HALO:], axis=0, keepdims=True), jnp.sum(dgc, axis=0, keepdims=True),
                     jnp.zeros((4, ck), F32)]
            dconv_ref[:, cs] += jnp.concatenate(rows, axis=0)

        @pl.when(i == n_t - 1)
        def _():
            dwd_ref[...] = acc_ref[...].astype(BF16)

    tokf = pl.BlockSpec((tm, d_ff), lambda i: (i, 0))
    return pl.pallas_call(
        body, name="ffn_bwd_act", grid=(n_t,),
        in_specs=[pl.BlockSpec((tm, d), lambda i: (i, 0)), tokf,
                  pl.BlockSpec((HALO, d_ff), lambda i: (_halo_before(i, tm), 0)), tokf, tokf, tokf,
                  pl.BlockSpec((d_ff, d), lambda i: (0, 0), pipeline_mode=pl.Buffered(1))],
        out_specs=[tokf, tokf, pl.BlockSpec((d_ff, d), lambda i: (0, 0)), pl.BlockSpec((8, d_ff), lambda i: (0, 0))],
        out_shape=[jax.ShapeDtypeStruct((s_len, d_ff), BF16), jax.ShapeDtypeStruct((s_len, d_ff), BF16),
                   jax.ShapeDtypeStruct((d_ff, d), BF16), jax.ShapeDtypeStruct((8, d_ff), F32)],
        scratch_shapes=[pltpu.VMEM((d_ff, d), F32)],
        compiler_params=_params(("arbitrary",)),
    )(dy2, gate, gate, a, act, vd, w_down)


def _ffn_bwd_up(dgc, dval, w_up_t, conv_w, tm):
    s_len, d_ff = dgc.shape
    d = w_up_t.shape[1]
    n_t = s_len // tm

    def body(dg_ref, dgn_ref, dv_ref, cw_ref, w_ref, dup_ref, dh_ref):
        i = pl.program_id(0)
        nxt = dgn_ref[...].astype(F32) * (i < n_t - 1).astype(F32)
        ext = jnp.concatenate([dg_ref[...].astype(F32), nxt], axis=0)
        rows = tm + HALO
        dgate = (ext * cw_ref[2:3, :] + pltpu.roll(ext, rows - 1, 0) * cw_ref[1:2, :]
                 + pltpu.roll(ext, rows - 2, 0) * cw_ref[0:1, :])[:tm]
        dup = jnp.concatenate([dgate.astype(BF16), dv_ref[...]], axis=1)
        dup_ref[...] = dup
        dh_ref[...] = _dot(dup, w_ref[...], NN).astype(BF16)

    tokf = pl.BlockSpec((tm, d_ff), lambda i: (i, 0))
    return pl.pallas_call(
        body, name="ffn_bwd_up", grid=(n_t,),
        in_specs=[tokf, pl.BlockSpec((HALO, d_ff), lambda i: (jnp.minimum((i + 1) * (tm // HALO), s_len // HALO - 1), 0)),
                  tokf, pl.BlockSpec((3, d_ff), lambda i: (0, 0)), pl.BlockSpec((2 * d_ff, d), lambda i: (0, 0))],
        out_specs=[pl.BlockSpec((tm, 2 * d_ff), lambda i: (i, 0)), pl.BlockSpec((tm, d), lambda i: (i, 0))],
        out_shape=[jax.ShapeDtypeStruct((s_len, 2 * d_ff), BF16), jax.ShapeDtypeStruct((s_len, d), BF16)],
        compiler_params=_params(("arbitrary",)),
    )(dgc, dgc, dval, conv_w, w_up_t)


def _mix_bwd(dh2, dout, x1, y1, cat, attn, w_out_t, sc_f, g_pre_ffn, gt_m, g_post_mix, tm):
    s_len, d = x1.shape
    n_t = s_len // tm

    def body(dh_ref, do_ref, x1_ref, y1_ref, cat_ref, at_ref, wo_ref, sc_ref, g2_ref, gt_ref, g1_ref,
             dx1_ref, dpool_ref, dattn_ref, delta_ref, dwo_ref, sums_ref, acc_ref):
        i = pl.program_id(0)
        dh = dh_ref[...].astype(F32)
        x1 = x1_ref[...]
        r2 = _rstd(x1)
        n2 = x1 * r2
        ng = n2 * g2_ref[...]
        dng = dh * (1.0 + sc_ref[...])
        dx1 = do_ref[...] + _norm_bwd(dng * g2_ref[...], n2, r2)
        dx1_ref[...] = dx1
        y1 = y1_ref[...].astype(F32)
        r1 = _rstd(y1)
        n1 = y1 * r1
        drn = dx1 * gt_ref[...]
        dy1 = _norm_bwd(drn * g1_ref[...], n1, r1).astype(BF16)
        dcat = _dot(dy1, wo_ref[...], NN)
        dpool_ref[...] = dcat[:, 0:256]
        lane = lax.broadcasted_iota(jnp.int32, (tm, LANES), 1)
        first = lane < HEAD_DIM
        for s in range(2):
            da = dcat[:, 256 + s * LANES:256 + (s + 1) * LANES]
            dattn_ref[s] = da
            prod = da * at_ref[:, s * LANES:(s + 1) * LANES]
            tot = jnp.sum(prod, axis=-1, keepdims=True)
            lo = jnp.sum(jnp.where(first, prod, 0.0), axis=-1, keepdims=True)
            delta_ref[s] = jnp.where(first, lo, tot - lo)
        dwo = _dot(dy1, cat_ref[...], TN)
        sums = jnp.concatenate(
            [jnp.sum(dh, axis=0, keepdims=True), jnp.sum(dh * ng, axis=0, keepdims=True),
             jnp.sum(dng * n2, axis=0, keepdims=True), jnp.sum(dx1 * (n1 * g1_ref[...]), axis=0, keepdims=True),
             jnp.sum(drn * n1, axis=0, keepdims=True), jnp.zeros((3, d), F32)], axis=0)

        @pl.when(i == 0)
        def _():
            acc_ref[...] = dwo
            sums_ref[...] = sums

        @pl.when(i > 0)
        def _():
            acc_ref[...] += dwo
            sums_ref[...] += sums

        @pl.when(i == n_t - 1)
        def _():
            dwo_ref[...] = acc_ref[...].astype(BF16)

    tile = lambda w: pl.BlockSpec((tm, w), lambda i: (i, 0))
    slab = pl.BlockSpec((2, tm, LANES), lambda i: (0, i, 0))
    vec = pl.BlockSpec((1, d), lambda i: (0, 0))
    return pl.pallas_call(
        body, name="mix_bwd", grid=(n_t,),
        in_specs=[tile(d), tile(d), tile(d), tile(d), tile(512), tile(256),
                  pl.BlockSpec((d, 512), lambda i: (0, 0)), vec, vec, vec, vec],
        out_specs=[tile(d), tile(256), slab, slab, pl.BlockSpec((d, 512), lambda i: (0, 0)),
                   pl.BlockSpec((8, d), lambda i: (0, 0))],
        out_shape=[jax.ShapeDtypeStruct((s_len, d), F32), jax.ShapeDtypeStruct((s_len, 256), F32),
                   jax.ShapeDtypeStruct((2, s_len, LANES), F32), jax.ShapeDtypeStruct((2, s_len, LANES), F32),
                   jax.ShapeDtypeStruct((d, 512), BF16), jax.ShapeDtypeStruct((8, d), F32)],
        scratch_shapes=[pltpu.VMEM((d, 512), F32)],
        compiler_params=_params(("arbitrary",)),
    )(dh2, dout, x1, y1, cat, attn, w_out_t, sc_f, g_pre_ffn, gt_m, g_post_mix)


def _pool_bwd(dpool, u_pool, w_blk, b_pool, pool_scale, tm):
    s_len = dpool.shape[0]
    n_t = s_len // tm

    def body(dp_ref, dpn_ref, u_ref, uh_ref, wb_ref, bp_ref, ps_ref, du_ref, dwp_ref, sums_ref, acc_ref):
        i = pl.program_id(0)
        u = u_ref[...]
        mixed, _ = _pool_mixed(u, uh_ref[...] * (i > 0).astype(F32), i, tm)
        mixed_b = mixed.astype(BF16)
        y = _dot(mixed_b, wb_ref[...], NN) + bp_ref[...]
        dp = dp_ref[...]
        dy = dp * ps_ref[...]
        dwb = _dot(mixed_b, dy.astype(BF16), TN)
        sums = jnp.concatenate([jnp.sum(dy, axis=0, keepdims=True), jnp.sum(dp * y, axis=0, keepdims=True),
                                jnp.zeros((6, 256), F32)], axis=0)
        dp_ext = jnp.concatenate([dp, dpn_ref[...] * (i < n_t - 1).astype(F32)], axis=0)
        dmix = _dot((dp_ext * ps_ref[...]).astype(BF16), wb_ref[...], NT)
        rows = tm + HALO
        grp = lax.broadcasted_iota(jnp.int32, (rows, 256), 1) // HEAD_DIM
        pick = lambda a, b, c, e: jnp.where(grp == 0, a, jnp.where(grp == 1, b, jnp.where(grp == 2, c, e)))
        pos = (i * tm + lax.broadcasted_iota(jnp.int32, (rows, 256), 0)).astype(F32)
        z = dmix / jnp.minimum(pos + 1.0, pick(*[float(w) for w in POOL_WINDOWS]))
        f2 = z + pltpu.roll(z, rows - 1, 0)
        f4 = f2 + pltpu.roll(f2, rows - 2, 0)
        f8 = f4 + pltpu.roll(f4, rows - 4, 0)
        f16 = f8 + pltpu.roll(f8, rows - 8, 0)
        du_ref[...] = (pick(f2, f4, f8, f16) - dmix)[:tm]

        @pl.when(i == 0)
        def _():
            acc_ref[...] = dwb
            sums_ref[...] = sums

        @pl.when(i > 0)
        def _():
            acc_ref[...] += dwb
            sums_ref[...] += sums

        @pl.when(i == n_t - 1)
        def _():
            full = acc_ref[...]
            for gi in range(len(POOL_WINDOWS)):
                lo = gi * HEAD_DIM
                dwp_ref[gi] = full[lo:lo + HEAD_DIM, lo:lo + HEAD_DIM]

    n_g = len(POOL_WINDOWS)
    tile = pl.BlockSpec((tm, 256), lambda i: (i, 0))
    const = lambda a: pl.BlockSpec(a.shape, lambda i: (0,) * a.ndim)
    return pl.pallas_call(
        body, name="pool_bwd", grid=(n_t,),
        in_specs=[tile, pl.BlockSpec((HALO, 256), lambda i: (jnp.minimum((i + 1) * (tm // HALO), s_len // HALO - 1), 0)),
                  tile, pl.BlockSpec((HALO, 256), lambda i: (_halo_before(i, tm), 0)),
                  const(w_blk), const(b_pool), const(pool_scale)],
        out_specs=[tile, pl.BlockSpec((n_g, HEAD_DIM, HEAD_DIM), lambda i: (0, 0, 0)), pl.BlockSpec((8, 256), lambda i: (0, 0))],
        out_shape=[jax.ShapeDtypeStruct((s_len, 256), F32), jax.ShapeDtypeStruct((n_g, HEAD_DIM, HEAD_DIM), F32),
                   jax.ShapeDtypeStruct((8, 256), F32)],
        scratch_shapes=[pltpu.VMEM((256, 256), F32)],
        compiler_params=_params(("arbitrary",)),
    )(dpool, dpool, u_pool, u_pool, w_blk, b_pool, pool_scale)


def _attn_bwd(qkv, dattn, lse_all, delta):
    s_len = qkv.shape[1]
    n_g = len(DILATIONS)

    def body(q_ref, k_ref, v_ref, do_ref, l_ref, dl_ref, dq_ref, dk_ref, dv_ref):
        lane = lax.broadcasted_iota(jnp.int32, (BLOCK, LANES), 1)
        first = lane < HEAD_DIM

        def group(dil):
            nb = s_len // (BLOCK * dil)

            def block(t, carry):
                dk_part, dv_part = carry
                r, n = t // nb, t % nb
                cur = _block_rows(n, r, dil)
                prev = _block_rows(jnp.maximum(n - 1, 0), r, dil)
                q = q_ref[0, cur, :]
                do = do_ref[0, cur, :]
                lse = l_ref[0, cur, :]
                dlt = dl_ref[0, cur, :]
                kcat = jnp.concatenate([k_ref[0, prev, :], k_ref[0, cur, :]], axis=0).astype(BF16)
                vcat = jnp.concatenate([v_ref[0, prev, :], v_ref[0, cur, :]], axis=0).astype(BF16)
                valid = _band_mask(n)
                stack = lambda a: jnp.concatenate([jnp.where(first, a, 0.0), jnp.where(first, 0.0, a)], axis=0)
                rows2 = lambda a: jnp.concatenate([a[:, 0:1], a[:, HEAD_DIM:HEAD_DIM + 1]], axis=0)
                q2, do2 = stack(q).astype(BF16), stack(do).astype(BF16)
                valid2 = jnp.concatenate([valid, valid], axis=0)
                p = jnp.where(valid2, jnp.exp(_dot(q2, kcat, NT) - rows2(lse)), 0.0)
                ds = (p * (_dot(do2, vcat, NT) - rows2(dlt))).astype(BF16)
                dq2 = _dot(ds, kcat, NN)
                dq_ref[0, 0, cur, :] = jnp.where(first, dq2[:BLOCK], dq2[BLOCK:])
                dkc = _dot(ds, q2, TN)
                dvc = _dot(p.astype(BF16), do2, TN)
                dk_ref[0, 0, prev, :] = dk_part + dkc[:BLOCK]
                dv_ref[0, 0, prev, :] = dv_part + dvc[:BLOCK]
                dk_ref[0, 0, cur, :] = dkc[BLOCK:]
                dv_ref[0, 0, cur, :] = dvc[BLOCK:]
                return dkc[BLOCK:], dvc[BLOCK:]

            def blocks(tt, carry):
                for u in range(ATTN_BWD_UNROLL):
                    carry = block(tt * ATTN_BWD_UNROLL + u, carry)
                return carry

            zero = jnp.zeros((BLOCK, LANES), F32)
            lax.fori_loop(0, nb * dil // ATTN_BWD_UNROLL, blocks, (zero, zero))

        for gi, dil in enumerate(DILATIONS):
            pl.when(pl.program_id(1) == gi)(functools.partial(group, dil))

    def slab(base):
        return pl.BlockSpec((1, s_len, LANES), lambda s, g: (base + 2 * g + s, 0, 0))

    one = pl.BlockSpec((1, s_len, LANES), lambda s, g: (s, 0, 0))
    out = pl.BlockSpec((1, 1, s_len, LANES), lambda s, g: (g, s, 0, 0))
    shape = jax.ShapeDtypeStruct((n_g, 2, s_len, LANES), F32)
    return pl.pallas_call(
        body, name="attn_bwd", grid=(2, n_g),
        in_specs=[slab(0), slab(6), slab(12), one, one, one],
        out_specs=[out, out, out], out_shape=[shape, shape, shape],
        compiler_params=_params(("arbitrary", "arbitrary")),
    )(qkv, qkv, qkv, dattn, lse_all, delta)


def _dproj_assemble(du, dqkv, rope, tm):
    s_len = du.shape[0]
    n_proj = 256 + 18 * LANES

    def body(du_ref, dq_ref, dk_ref, dv_ref, cs_ref, spread_ref, dproj_ref):
        dproj_ref[:, 0:256] = du_ref[...].astype(BF16)
        lanes = _rope_lanes(cs_ref, spread_ref)
        col = 256
        for kind, dref in enumerate((dq_ref, dk_ref, dv_ref)):
            for grp in range(3):
                for s in range(2):
                    piece = dref[grp, s]
                    if kind < 2:
                        piece = _rope_bwd(piece, lanes)
                    if kind == 0:
                        piece = piece * (HEAD_DIM ** -0.5)
                    dproj_ref[:, col:col + LANES] = piece.astype(BF16)
                    col += LANES

    groups = pl.BlockSpec((len(DILATIONS), 2, tm, LANES), lambda i: (0, 0, i, 0))
    return pl.pallas_call(
        body, name="dproj_assemble", grid=(s_len // tm,),
        in_specs=[pl.BlockSpec((tm, 256), lambda i: (i, 0))] + [groups] * 3
        + [pl.BlockSpec((tm, rope[0].shape[1]), lambda i: (i, 0)), pl.BlockSpec(rope[1].shape, lambda i: (0, 0, 0))],
        out_specs=pl.BlockSpec((tm, n_proj), lambda i: (i, 0)),
        out_shape=jax.ShapeDtypeStruct((s_len, n_proj), BF16),
        compiler_params=_params(("arbitrary",)),
    )(du, *dqkv, *rope)


def _inproj_bwd(dproj, w_in_t, x, dx1, sc_m, g_pre_mix, tm):
    s_len, d = x.shape
    n_proj = w_in_t.shape[0]
    n_t = s_len // tm

    def body(dproj_ref, w_ref, x_ref, dx1_ref, sc_ref, g_ref, dx_ref, sums_ref):
        i = pl.program_id(0)
        halves = [slice(0, tm // 2), slice(tm // 2, tm)]
        dhs = [_dot(dproj_ref[rs, :], w_ref[...], NN) for rs in halves]
        sums = None
        for rs, dh in zip(halves, dhs):
            xv = x_ref[rs, :]
            r = _rstd(xv)
            n = xv * r
            dng = dh * (1.0 + sc_ref[...])
            dx_ref[rs, :] = dx1_ref[rs, :] + _norm_bwd(dng * g_ref[...], n, r)
            part = jnp.concatenate([jnp.sum(dh, axis=0, keepdims=True), jnp.sum(dh * (n * g_ref[...]), axis=0, keepdims=True),
                                    jnp.sum(dng * n, axis=0, keepdims=True), jnp.zeros((5, d), F32)], axis=0)
            sums = part if sums is None else sums + part

        @pl.when(i == 0)
        def _():
            sums_ref[...] = sums

        @pl.when(i > 0)
        def _():
            sums_ref[...] += sums

    tile = lambda w: pl.BlockSpec((tm, w), lambda i: (i, 0))
    vec = pl.BlockSpec((1, d), lambda i: (0, 0))
    return pl.pallas_call(
        body, name="inproj_bwd", grid=(n_t,),
        in_specs=[tile(n_proj), pl.BlockSpec((n_proj, d), lambda i: (0, 0)), tile(d), tile(d), vec, vec],
        out_specs=[tile(d), pl.BlockSpec((8, d), lambda i: (0, 0))],
        out_shape=[jax.ShapeDtypeStruct((s_len, d), F32), jax.ShapeDtypeStruct((8, d), F32)],
        compiler_params=_params(("arbitrary",)),
    )(dproj, w_in_t, x, dx1, sc_m, g_pre_mix)


def _wgrad(a, b, name, tk, tmm):
    s_len, m = a.shape
    n = b.shape[1]
    n_k = s_len // tk

    def body(a_ref, b_ref, o_ref, acc_ref):
        k = pl.program_id(1)
        part = _dot(a_ref[...], b_ref[...], TN)

        @pl.when(k == 0)
        def _():
            acc_ref[...] = part

        @pl.when(k > 0)
        def _():
            acc_ref[...] += part

        @pl.when(k == n_k - 1)
        def _():
            o_ref[...] = acc_ref[...].astype(BF16)

    return pl.pallas_call(
        body, name=name, grid=(m // tmm, n_k),
        in_specs=[pl.BlockSpec((tk, tmm), lambda j, k: (k, j)), pl.BlockSpec((tk, n), lambda j, k: (k, 0))],
        out_specs=pl.BlockSpec((tmm, n), lambda j, k: (j, 0)),
        out_shape=jax.ShapeDtypeStruct((m, n), BF16),
        scratch_shapes=[pltpu.VMEM((tmm, n), F32)],
        compiler_params=_params(("arbitrary", "arbitrary")),
    )(a, b)


def _place():
    return lax.axis_index("x"), lax.axis_index("y"), lax.axis_index("c")


def _peer(k):
    x, y, c = _place()
    bx, by, bc = (k >> 2) & 1, (k >> 1) & 1, k & 1
    return (x ^ bx if bx else x, y ^ by if by else y, c ^ bc if bc else c)


def _index(pos):
    return 4 * pos[0] + 2 * pos[1] + pos[2]


def _entry_exchange(c_rows, w_ada, b_ada_cols, taps, shards):
    d = c_rows.shape[1]
    ncol = w_ada.shape[1]
    n_w = len(shards)

    def body(c_ref, w_ref, b_ref, t_ref, *rest):
        srcs = rest[:n_w]
        call_ref, mod_ref, tall_ref = rest[n_w:n_w + 3]
        outs = rest[n_w + 3:2 * n_w + 3]
        stage_ref, s_send, s_recv, w_send, w_recv, local_sems = rest[2 * n_w + 3:]
        x, y, c = _place()
        here, sibling = (x, y, c), (x, y, 1 - c)
        chips = [(1 - x, y), (x, 1 - y), (1 - x, 1 - y)]
        me = _index(here)

        def small(kind, src, dst, k):
            return pltpu.make_async_remote_copy(src_ref=src, dst_ref=dst, send_sem=s_send.at[kind, k - 1],
                                                recv_sem=s_recv.at[kind, k - 1], device_id=_peer(k), device_id_type=MESH)

        gather = lambda k: small(0, c_ref, call_ref.at[me], k)
        scatter = lambda k: small(1, stage_ref.at[_index(_peer(k))], mod_ref.at[me], k)
        gather_taps = lambda k: small(2, t_ref, tall_ref.at[me], k)

        def rows(w, pos):
            r = shards[w].shape[0]
            return outs[w].at[pl.ds(pl.multiple_of(_index(pos) * r, 16), r), :]

        def block(k, w, pos, to, own=False):
            return pltpu.make_async_remote_copy(
                src_ref=srcs[w] if own else rows(w, pos), dst_ref=rows(w, pos),
                send_sem=w_send.at[k, w], recv_sem=w_recv.at[k, w], device_id=to, device_id_type=MESH)

        call_ref[me] = c_ref[...]
        tall_ref[me] = t_ref[...]
        for k in range(1, N_DEV):
            gather(k).start()
        for k in range(1, N_DEV):
            gather_taps(k).start()
        mine = [pltpu.make_async_copy(srcs[w], rows(w, here), local_sems.at[w]) for w in range(n_w)]
        for cp in mine:
            cp.start()
        first = [block(0, w, here, sibling, own=True) for w in range(n_w)]
        first += [block(1 + j, w, here, (*chip, c), own=True) for j, chip in enumerate(chips) for w in range(n_w)]
        for cp in first:
            cp.start()

        for k in range(1, N_DEV):
            gather(k).wait_recv()
        cv = jnp.concatenate([call_ref[b, 0:1, :] for b in range(N_DEV)], axis=0)
        act = cv * jax.nn.sigmoid(cv)
        mod = lax.dot_general(act, w_ref[...], NN, preferred_element_type=F32,
                              precision=lax.Precision.HIGHEST) + b_ref[...]
        for b in range(N_DEV):
            stage_ref[b] = jnp.broadcast_to(mod[b:b + 1, :], (8, ncol))
        mod_ref[me] = stage_ref[me]
        for k in range(1, N_DEV):
            scatter(k).start()

        passed = []
        for j, chip in enumerate(chips):
            for w in range(n_w):
                block(1 + j, w, (*chip, c), here).wait_recv()
                fwd = block(4 + j, w, (*chip, c), sibling)
                fwd.start()
                passed.append(fwd)
        for w in range(n_w):
            block(0, w, sibling, here).wait_recv()
        for j, chip in enumerate(chips):
            for w in range(n_w):
                block(4 + j, w, (*chip, 1 - c), here).wait_recv()
        for k in range(1, N_DEV):
            scatter(k).wait_recv()
            gather_taps(k).wait_recv()
        for cp in first + passed:
            cp.wait_send()
        for k in range(1, N_DEV):
            gather(k).wait_send()
            scatter(k).wait_send()
            gather_taps(k).wait_send()
        for cp in mine:
            cp.wait()

    vmem, hbm = pl.BlockSpec(memory_space=pltpu.VMEM), pl.BlockSpec(memory_space=pltpu.HBM)
    out = pl.pallas_call(
        body, name="entry_exchange",
        in_specs=[vmem] * 4 + [hbm] * n_w, out_specs=[vmem] * 3 + [hbm] * n_w,
        out_shape=[jax.ShapeDtypeStruct((N_DEV, 8, d), F32), jax.ShapeDtypeStruct((N_DEV, 8, ncol), F32),
                   jax.ShapeDtypeStruct((N_DEV,) + taps.shape, F32)]
        + [jax.ShapeDtypeStruct((N_DEV * s.shape[0], s.shape[1]), s.dtype) for s in shards],
        scratch_shapes=[pltpu.VMEM((N_DEV, 8, ncol), F32), pltpu.SemaphoreType.DMA((3, N_DEV - 1)),
                        pltpu.SemaphoreType.DMA((3, N_DEV - 1)), pltpu.SemaphoreType.DMA((N_DEV - 1, n_w)),
                        pltpu.SemaphoreType.DMA((N_DEV - 1, n_w)), pltpu.SemaphoreType.DMA((n_w,))],
        compiler_params=_params(),
    )(c_rows, w_ada, b_ada_cols, taps, *shards)
    return out[0], out[1], out[2], out[3:]


def _peer_copies(mode, srcs, lands, send_sems, recv_sems):
    if mode in ("gather_ici", "gather_d2d"):
        x, y, c = _place()
        sibling = (x, y, 1 - c)
        chips = [(1 - x, y), (x, 1 - y), (1 - x, 1 - y)]
        n = len(lands)

        def rows(w, pos):
            r = lands[w].shape[0] // N_DEV
            return lands[w].at[pl.ds(pl.multiple_of(_index(pos) * r, 16), r), :]

        def copy(k, w, src, dst, to):
            return pltpu.make_async_remote_copy(src_ref=src, dst_ref=dst, send_sem=send_sems.at[k * n + w],
                                                recv_sem=recv_sems.at[k * n + w], device_id=to, device_id_type=MESH)

        if mode == "gather_ici":
            targets = [sibling] + [(*chip, c) for chip in chips]
            return [copy(k, w, srcs[w], rows(w, (x, y, c)), to) for k, to in enumerate(targets) for w in range(n)]
        return [copy(j, w, rows(w, (*chip, c)), rows(w, (*chip, c)), sibling)
                for j, chip in enumerate(chips) for w in range(n)]
    me = _index(_place())
    copies = []
    for k in range(1, N_DEV):
        peer = _peer(k)
        for w, (src, land) in enumerate(zip(srcs, lands)):
            if mode == "gather":
                r = src.shape[0]
                dst = land.at[pl.ds(pl.multiple_of(me * r, 16), r), :]
            elif mode == "allgather":
                dst = land.at[me]
            else:
                r = src.shape[0] // N_DEV
                src = src.at[pl.ds(pl.multiple_of(_index(peer) * r, 16), r), :]
                dst = land.at[me]
            copies.append(pltpu.make_async_remote_copy(
                src_ref=src, dst_ref=dst, send_sem=send_sems.at[(k - 1) * len(srcs) + w],
                recv_sem=recv_sems.at[(k - 1) * len(srcs) + w],
                device_id=peer, device_id_type=MESH))
    return copies


def _landing_zone(mode, src, me, name):
    cols = src.shape[1]
    if mode == "gather":
        r = src.shape[0]
        in_spec = pl.BlockSpec((r, cols), lambda i, me_ref: (0, 0))
        out_spec = pl.BlockSpec((r, cols), lambda i, me_ref: (me_ref[0], 0))
        out_shape = jax.ShapeDtypeStruct((N_DEV * r, cols), src.dtype)
    else:
        r = src.shape[0] // N_DEV
        in_spec = pl.BlockSpec((r, cols), lambda i, me_ref: (me_ref[0], 0))
        out_spec = pl.BlockSpec((1, r, cols), lambda i, me_ref: (me_ref[0], 0, 0))
        out_shape = jax.ShapeDtypeStruct((N_DEV, r, cols), src.dtype)

    def body(me_ref, s_ref, o_ref):
        o_ref[...] = s_ref[...].reshape(o_ref.shape)

    return pl.pallas_call(
        body, name=name, out_shape=out_shape,
        grid_spec=pltpu.PrefetchScalarGridSpec(num_scalar_prefetch=1, grid=(1,), in_specs=[in_spec], out_specs=out_spec),
        compiler_params=_params(("arbitrary",)),
    )(me.reshape(1).astype(jnp.int32), src)


def _exchange_start(mode, srcs, lands, name):
    n_s, n_a = len(srcs), len(srcs) + len(lands)
    n_cp = _COPIES_PER_ARRAY.get(mode, N_DEV - 1) * len(lands)

    def body(*refs):
        for cp in _peer_copies(mode, refs[:n_s], refs[n_s:n_a], refs[n_a], refs[n_a + 1]):
            cp.start()
        refs[-1][...] = jnp.zeros_like(refs[-1])

    hbm, sem = pl.BlockSpec(memory_space=pltpu.HBM), pl.BlockSpec(memory_space=pltpu.SEMAPHORE)
    arrays = list(srcs) + list(lands)
    out = pl.pallas_call(
        body, name=name,
        out_shape=(pltpu.SemaphoreType.DMA((n_cp,)), pltpu.SemaphoreType.DMA((n_cp,)),
                   *[pltpu.HBM(a.shape, a.dtype) for a in arrays], jax.ShapeDtypeStruct((8, LANES), F32)),
        in_specs=[hbm] * n_a, out_specs=(sem, sem, *[hbm] * n_a, pl.BlockSpec(memory_space=pltpu.VMEM)),
        input_output_aliases={i: 2 + i for i in range(n_a)},
        compiler_params=pltpu.CompilerParams(has_side_effects=pltpu.SideEffectType.DATAFLOW_SIDE_EFFECTING),
    )(*[pltpu.with_memory_space_constraint(a, pltpu.HBM) for a in arrays])
    return out[0], out[1], out[2:2 + n_s], out[2 + n_s:2 + n_a], out[-1]


_COPIES_PER_ARRAY = {"gather_ici": 4, "gather_d2d": 3}


def _exchange_wait(mode, send_sems, recv_sems, srcs, lands, after, name):
    n_s, n_a = len(srcs), len(srcs) + len(lands)

    def body(*refs):
        copies = _peer_copies(mode, refs[:n_s], refs[n_s:n_a], refs[n_a], refs[n_a + 1])
        for cp in copies:
            cp.wait_send()
        for cp in copies:
            cp.wait_recv()

    hbm, sem = pl.BlockSpec(memory_space=pltpu.HBM), pl.BlockSpec(memory_space=pltpu.SEMAPHORE)
    arrays = list(srcs) + list(lands)
    out = pl.pallas_call(
        body, name=name, out_shape=tuple(pltpu.HBM(a.shape, a.dtype) for a in arrays),
        in_specs=[hbm] * n_a + [sem, sem, pl.BlockSpec(memory_space=pl.ANY)], out_specs=tuple([hbm] * n_a),
        input_output_aliases={i: i for i in range(n_a)},
        compiler_params=pltpu.CompilerParams(has_side_effects=pltpu.SideEffectType.DATAFLOW_SIDE_EFFECTING),
    )(*arrays, send_sems, recv_sems, after)
    return out[n_s:]


SMALL_WEIGHTS = ("b_ada", "g_pre_mix", "g_post_mix", "g_pre_ffn", "g_post_ffn", "w_pool", "b_pool", "pool_scale", "conv_b")


MOD_ROWS = ((0, 0), (0, 1), (1, 3), (1, 0), (1, 1), (2, 0))


def _small_sum(mine, gathered):
    n_l = len(mine)
    d = mine[0].shape[1]

    def body(*refs):
        loc, got = refs[:n_l], refs[n_l:2 * n_l]
        tot_refs, dmod_ref = refs[2 * n_l:3 * n_l], refs[3 * n_l]
        me = _index(_place())
        part = lambda a, dev: jnp.where(dev == me, loc[a][...], got[a][dev])
        for a in range(n_l):
            tot = part(a, 0)
            for dev in range(1, N_DEV):
                tot = tot + part(a, dev)
            tot_refs[a][...] = tot
        for dev in range(N_DEV):
            for k, (a, r) in enumerate(MOD_ROWS):
                dmod_ref[dev:dev + 1, k * d:(k + 1) * d] = part(a, dev)[r:r + 1, :]

    vmem = pl.BlockSpec(memory_space=pltpu.VMEM)
    out = pl.pallas_call(
        body, name="small_sum", in_specs=[vmem] * (2 * n_l), out_specs=[vmem] * (n_l + 1),
        out_shape=[jax.ShapeDtypeStruct(a.shape, F32) for a in mine] + [jax.ShapeDtypeStruct((N_DEV, 6 * d), F32)],
        compiler_params=_params(),
    )(*mine, *gathered)
    return out[:n_l], out[n_l]


def _small_adam(totals, weights, moms, vels):
    n_t, n_w = len(totals), len(weights)

    def body(*refs):
        t_in, t_mix, t_ffn, t_pool, t_blk, t_conv, _ = (r[...] for r in refs[:n_t])
        w_refs, m_refs, v_refs = (refs[n_t + k * n_w:n_t + (k + 1) * n_w] for k in range(3))
        outs = refs[n_t + 3 * n_w:]

        def update(idx, g, at=()):
            sel = lambda ref: ref.at[at] if at else ref
            delta, nm, nv = _adam_math(sel(w_refs[idx])[...], g, sel(m_refs[idx])[...], sel(v_refs[idx])[...])
            for k, val in enumerate((g, delta, nm, nv)):
                sel(outs[4 * idx + k])[...] = val

        tots = (t_in, t_mix, t_ffn)
        update(0, jnp.concatenate([tots[a][r:r + 1] for a, r in MOD_ROWS], axis=1))
        update(1, t_in[2:3])
        update(2, t_mix[4:5])
        update(3, t_mix[2:3])
        update(4, t_ffn[1:2])
        for gi in range(len(POOL_WINDOWS)):
            update(5, t_blk[gi], at=(0, gi))
        update(6, jnp.concatenate([t_pool[0:1, gi * HEAD_DIM:(gi + 1) * HEAD_DIM] for gi in range(len(POOL_WINDOWS))], axis=0),
               at=(0,))
        update(7, t_pool[1:2])
        update(8, t_conv[3:4])

    vmem = pl.BlockSpec(memory_space=pltpu.VMEM)
    return pl.pallas_call(
        body, name="small_adam", in_specs=[vmem] * (n_t + 3 * n_w), out_specs=[vmem] * (4 * n_w),
        out_shape=[jax.ShapeDtypeStruct(w.shape, F32) for w in weights for _ in range(4)],
        compiler_params=_params(),
    )(*totals, *weights, *moms, *vels)


def _adam_math(w, g, m, v):
    m = ADAM_B1 * m + (1.0 - ADAM_B1) * g
    v = ADAM_B2 * v + (1.0 - ADAM_B2) * (g * g)
    m_hat = m / (1.0 - ADAM_B1 ** ADAM_STEP)
    v_hat = v / (1.0 - ADAM_B2 ** ADAM_STEP)
    delta = -ADAM_LR * (m_hat / (jnp.sqrt(v_hat) + ADAM_EPS) + ADAM_WD * w)
    return delta, m, v


def _adam(w, g, m, v, name, tr):
    rows, cols = w.shape

    def body(w_ref, g_ref, m_ref, v_ref, d_ref, nm_ref, nv_ref):
        d_ref[...], nm_ref[...], nv_ref[...] = _adam_math(w_ref[...], g_ref[...], m_ref[...], v_ref[...])

    spec = pl.BlockSpec((tr, cols), lambda i: (i, 0))
    shape = jax.ShapeDtypeStruct((rows, cols), F32)
    return pl.pallas_call(
        body, name=name, grid=(rows // tr,), in_specs=[spec] * 4, out_specs=[spec] * 3,
        out_shape=[shape] * 3, compiler_params=_params(("arbitrary",)),
    )(w, g, m, v)


def _sum_adam(parts, w, m, v, name, tr):
    _, rows, cols = parts.shape

    def body(p_ref, w_ref, m_ref, v_ref, g_ref, d_ref, nm_ref, nv_ref):
        g = p_ref[0].astype(F32)
        for dev in range(1, N_DEV):
            g = g + p_ref[dev].astype(F32)
        g_ref[...] = g
        d_ref[...], nm_ref[...], nv_ref[...] = _adam_math(w_ref[...], g, m_ref[...], v_ref[...])

    spec = pl.BlockSpec((tr, cols), lambda i: (i, 0))
    shape = jax.ShapeDtypeStruct((rows, cols), F32)
    return pl.pallas_call(
        body, name=name, grid=(rows // tr,),
        in_specs=[pl.BlockSpec((N_DEV, tr, cols), lambda i: (0, i, 0)), spec, spec, spec],
        out_specs=[spec] * 4, out_shape=[shape] * 4, compiler_params=_params(("arbitrary",)),
    )(parts, w, m, v)


def _ada_grad_adam(c_all, dmod_cols, w, m, v, tr):
    rows, cols = w.shape

    def body(c_ref, dm_ref, w_ref, m_ref, v_ref, g_ref, d_ref, nm_ref, nv_ref):
        cv = c_ref[...]
        act = cv * jax.nn.sigmoid(cv)
        g = lax.dot_general(act, dm_ref[...], TN, preferred_element_type=F32, precision=lax.Precision.HIGHEST)
        g_ref[...] = g
        d_ref[...], nm_ref[...], nv_ref[...] = _adam_math(w_ref[...], g, m_ref[...], v_ref[...])

    spec = pl.BlockSpec((tr, cols), lambda i: (i, 0))
    shape = jax.ShapeDtypeStruct((rows, cols), F32)
    return pl.pallas_call(
        body, name="ada_grad_adam", grid=(rows // tr,),
        in_specs=[pl.BlockSpec((N_DEV, tr), lambda i: (0, i)), pl.BlockSpec((N_DEV, cols), lambda i: (0, 0)), spec, spec, spec],
        out_specs=[spec] * 4, out_shape=[shape] * 4, compiler_params=_params(("arbitrary",)),
    )(c_all, dmod_cols, w, m, v)


def _rope_tables(positions):
    inv_freq = ROPE_THETA ** (-jnp.arange(0, 2 * ROT_HALF, 2, dtype=F32) / (2 * ROT_HALF))
    ang = positions.astype(F32)[:, None] * inv_freq
    rows = jnp.concatenate([jnp.cos(ang), jnp.sin(ang), jnp.ones_like(ang)], axis=1)
    spread = [[[0.0] * LANES for _ in range(3 * ROT_HALF)] for _ in range(3)]
    for lane in range(LANES):
        p, j = lane % HEAD_DIM, lane % ROT_HALF
        if p < ROT_HALF:
            spread[0][j][lane] = 1.0
            spread[1][ROT_HALF + j][lane] = -1.0
        elif p < 2 * ROT_HALF:
            spread[0][j][lane] = 1.0
            spread[2][ROT_HALF + j][lane] = 1.0
        else:
            spread[0][2 * ROT_HALF][lane] = 1.0
    return rows, jnp.array(spread, F32)


def _pad_rows(a, rows):
    return jnp.pad(a, ((0, rows - a.shape[0]), (0, 0)))


def _sequence_step(xs, target, rope, mods, gains, w_in_t, w_out_t, relay_ffn, fetch_ffn, send_ffn_grads, send_mix_grads, w_blk_b, b_pool_r,
                   pool_scale_r, conv_w_all, conv_b):
    sh_m, sc_m, gt_m, sh_f, sc_f, gt_f = mods
    g_pre_mix, g_post_mix, g_pre_ffn, g_post_ffn = gains
    h1, u_pool, qkv = _premix_inproj(xs, sh_m, sc_m, g_pre_mix, w_in_t, rope, tm=512)
    o_g, lse_g = _attn_fwd(qkv)
    x1, y1, h2, cat, attn, lse_all = _mix_out(xs, u_pool, o_g, lse_g, w_blk_b, b_pool_r, pool_scale_r, w_out_t,
                                              gt_m, g_post_mix, g_pre_ffn, sc_f, sh_f, tm=256)
    token = relay_ffn(x1)
    w_up_t, w_down_f = fetch_ffn(x1 if token is None else token)
    gate, a_ffn, act, vd, dy2, dout, sums_ffn, loss_loc = _ffn_fwd_loss(h2, x1, target, w_up_t, w_down_f, conv_w_all, conv_b,
                                                              gt_f, g_post_ffn, tm=256, tf=2816, ck=256)

    dgc, dval, dw_down, dconv = _ffn_bwd_act(dy2, gate, a_ffn, act, vd, w_down_f, tm=256, ck=256)
    dup, dh2 = _ffn_bwd_up(dgc, dval, w_up_t, conv_w_all, tm=256)
    dw_up_t = _wgrad(dup, h2, "wgrad_up", tk=2048, tmm=1408)
    token = send_ffn_grads(dw_up_t, dw_down)
    if token is not None:
        sc_f = sc_f + token[0:1, 0:1]
    dx1, dpool, dattn, delta, dw_out_t, sums_mix = _mix_bwd(dh2, dout, x1, y1, cat, attn, w_out_t, sc_f, g_pre_ffn,
                                                           gt_m, g_post_mix, tm=256)
    du, dw_blk, sums_pool = _pool_bwd(dpool, u_pool, w_blk_b, b_pool_r, pool_scale_r, tm=512)
    dproj = _dproj_assemble(du, _attn_bwd(qkv, dattn, lse_all, delta), rope, tm=512)
    dw_in_t = _wgrad(dproj, h1, "wgrad_in", tk=2048, tmm=1280)
    token = send_mix_grads(dw_in_t, dw_out_t)
    if token is not None:
        sc_m = sc_m + token[0:1, 0:1]
    grad_x, sums_in = _inproj_bwd(dproj, w_in_t, xs, dx1, sc_m, g_pre_mix, tm=256)
    return (loss_loc, grad_x, dw_in_t, dw_out_t, dw_up_t, dw_down, dw_blk, dconv,
            sums_in, sums_mix, sums_ffn, sums_pool)


def kernel(x, c, positions, w_ada, b_ada, g_pre_mix, g_post_mix, g_pre_ffn, g_post_ffn, w_in, w_pool, b_pool, pool_scale, w_out, w_up, conv_w, conv_b, w_down, loss_target, m_w_ada, m_b_ada, m_g_pre_mix, m_g_post_mix, m_g_pre_ffn, m_g_post_ffn, m_w_in, m_w_pool, m_b_pool, m_pool_scale, m_w_out, m_w_up, m_conv_w, m_conv_b, m_w_down, v_w_ada, v_b_ada, v_g_pre_mix, v_g_post_mix, v_g_pre_ffn, v_g_post_ffn, v_w_in, v_w_pool, v_b_pool, v_pool_scale, v_w_out, v_w_up, v_conv_w, v_conv_b, v_w_down):
    s_len, d = x.shape[1], x.shape[2]
    d_ff = w_down.shape[1] * N_DEV
    me = _index(_place())
    xs, target = x[0], loss_target[0]

    ncol = w_ada.shape[2]
    b_cols = lax.dynamic_slice(b_ada, (0, me * ncol), (1, ncol))
    c_all, mod, taps_all, (w_in_t, w_out_t) = _entry_exchange(
        jnp.broadcast_to(c, (8, d)), w_ada[0], b_cols, _pad_rows(conv_w[0], 8),
        [w_in[0].T.astype(BF16), w_out[0].T.astype(BF16)])
    c_all = c_all[:, 0, :]
    conv_w_all = jnp.transpose(taps_all[:, :3, :], (1, 0, 2)).reshape(3, d_ff)
    sh_m, sc_m, gt_m, sh_f, sc_f, gt_f = [mod[:, 0, :].reshape(1, -1)[:, k * d:(k + 1) * d] for k in range(6)]

    rope = _rope_tables(positions[0])
    w_blk = jnp.zeros((256, 256), F32)
    for gi in range(4):
        w_blk = lax.dynamic_update_slice(w_blk, w_pool[0, gi], (gi * HEAD_DIM, gi * HEAD_DIM))
    w_blk_b = w_blk.astype(BF16)
    b_pool_r, pool_scale_r = b_pool.reshape(1, 256), pool_scale.reshape(1, 256)

    up_sh, down_sh = w_up[0].T.astype(BF16), w_down[0].astype(BF16)
    w_in_t, conv_w_all, up_sh, down_sh = lax.optimization_barrier((w_in_t, conv_w_all, up_sh, down_sh))
    lands = [_landing_zone("gather", s, me, "land_" + nm) for s, nm in ((up_sh, "w_up"), (down_sh, "w_down"))]
    w_send, w_recv, w_src, w_land, w_token = _exchange_start("gather_ici", [up_sh, down_sh], lands, "ffn_weights_ici_start")
    relay = []

    def relay_ffn(after):
        arrived = _exchange_wait("gather_ici", w_send, w_recv, w_src, w_land, after, "ffn_weights_ici_wait")
        relay.extend(_exchange_start("gather_d2d", [], arrived, "ffn_weights_d2d_start"))
        return relay[4]

    def fetch_ffn(after):
        return _exchange_wait("gather_d2d", relay[0], relay[1], [], relay[3], after, "ffn_weights_d2d_wait")

    flight = []

    def send_ffn_grads(dw_up_t, dw_down):
        lands = [_landing_zone("scatter", dw_up_t, me, "land_dw_up"), _landing_zone("scatter", dw_down, me, "land_dw_down")]
        flight.extend(_exchange_start("scatter", [dw_up_t, dw_down], lands, "ffn_grads_start"))
        return flight[4]

    mix_flight = []

    def send_mix_grads(dw_in_t, dw_out_t):
        lands = [_landing_zone("scatter", dw_in_t, me, "land_dw_in"), _landing_zone("scatter", dw_out_t, me, "land_dw_out")]
        mix_flight.extend(_exchange_start("scatter", [dw_in_t, dw_out_t], lands, "mix_grads_start"))
        return mix_flight[4]

    (loss_loc, grad_x, dw_in_t, dw_out_t, _, _, dw_pool, dconv,
     sums_in, sums_mix, sums_ffn, sums_pool) = _sequence_step(
        xs, target, rope, (sh_m + w_token[0:1, 0:1], sc_m, gt_m, sh_f, sc_f, gt_f),
        (g_pre_mix, g_post_mix, g_pre_ffn, g_post_ffn),
        w_in_t, w_out_t, relay_ffn, fetch_ffn, send_ffn_grads, send_mix_grads, w_blk_b, b_pool_r, pool_scale_r, conv_w_all, conv_b)

    small = [sums_in, sums_mix, sums_ffn, sums_pool, dw_pool, dconv, loss_loc]
    small_flight = _exchange_start("allgather", small, [lax.empty((N_DEV,) + a.shape, F32) for a in small], "small_start")

    parts_ffn = _exchange_wait("scatter", *flight[:4], small_flight[4], "ffn_grads_wait")
    big = {
        "w_up": [a.T for a in _sum_adam(parts_ffn[0], w_up[0].T, m_w_up[0].T, v_w_up[0].T, "adam_w_up", 64)],
        "w_down": _sum_adam(parts_ffn[1], w_down[0], m_w_down[0], v_w_down[0], "adam_w_down", 32),
    }

    rep_w = [b_ada, g_pre_mix, g_post_mix, g_pre_ffn, g_post_ffn, w_pool, b_pool, pool_scale, conv_b]
    rep_m = [m_b_ada, m_g_pre_mix, m_g_post_mix, m_g_pre_ffn, m_g_post_ffn, m_w_pool, m_b_pool, m_pool_scale, m_conv_b]
    rep_v = [v_b_ada, v_g_pre_mix, v_g_post_mix, v_g_pre_ffn, v_g_post_ffn, v_w_pool, v_b_pool, v_pool_scale, v_conv_b]
    parts_mix = _exchange_wait("scatter", *mix_flight[:4], big["w_down"][0], "mix_grads_wait")
    big["w_in"] = [a.T for a in _sum_adam(parts_mix[0], w_in[0].T, m_w_in[0].T, v_w_in[0].T, "adam_w_in", 64)]
    big["w_out"] = [a.T for a in _sum_adam(parts_mix[1], w_out[0].T, m_w_out[0].T, v_w_out[0].T, "adam_w_out", 128)]
    gathered = _exchange_wait("allgather", *small_flight[:4], big["w_out"][0], "small_wait")
    totals, dmod_all = _small_sum(small, gathered)
    dconv_tot, loss_tot = totals[5], totals[6]
    rep_out = _small_adam(totals, rep_w, rep_m, rep_v)
    g_rep, d_rep, nm_rep, nv_rep = (rep_out[k::4] for k in range(4))

    fcol = d_ff // N_DEV
    g_cw = lax.dynamic_slice(dconv_tot, (0, me * fcol), (3, fcol))
    d_cw, nm_cw, nv_cw = _adam(conv_w[0], g_cw, m_conv_w[0], v_conv_w[0], "adam_conv_w", 3)

    dmod_cols = lax.dynamic_slice(dmod_all, (0, me * ncol), (N_DEV, ncol))
    g_ada, d_ada, nm_ada, nv_ada = _ada_grad_adam(c_all, dmod_cols, w_ada[0], m_w_ada[0], v_w_ada[0], 256)

    loss = loss_tot[0, 0]

    def group(k):
        rep = (g_rep, d_rep, nm_rep, nv_rep)[k]
        ada = (g_ada, d_ada, nm_ada, nv_ada)[k][None]
        cw = (g_cw, d_cw, nm_cw, nv_cw)[k][None]
        return [ada, rep[0], rep[1], rep[2], rep[3], rep[4], big["w_in"][k][None], rep[5], rep[6], rep[7],
                big["w_out"][k][None], big["w_up"][k][None], cw, rep[8], big["w_down"][k][None]]

    return (loss, grad_x[None], *group(0), *group(1), *group(2), *group(3))
```

```python
import functools
import math

import jax
import jax.numpy as jnp
from jax import lax
from jax.experimental import pallas as pl
from jax.experimental.pallas import tpu as pltpu

F32 = jnp.float32
BF16 = jnp.bfloat16
MESH = pl.DeviceIdType.MESH

N_DEV = 8
HEAD_DIM = 64
ROT_HALF = 8
ROPE_THETA = 500000.0
POOL_WINDOWS = (2, 4, 8, 16)
DILATIONS = (1, 4, 16)
BLOCK = 128
NORM_EPS = 1e-6
HALO = 16
MASKED = -1e30
ATTN_FWD_UNROLL = 8
ATTN_BWD_UNROLL = 4

ADAM_LR = 0.001
ADAM_B1 = 0.9
ADAM_B2 = 0.999
ADAM_EPS = 1e-08
ADAM_WD = 0.01
ADAM_STEP = 10

V7X_VMEM_LIMIT = 56 * 1024 * 1024
LANES = 128

NT = (((1,), (1,)), ((), ()))
NN = (((1,), (0,)), ((), ()))
TN = (((0,), (0,)), ((), ()))


def _dot(a, b, dims):
    return lax.dot_general(a, b, dims, preferred_element_type=F32)


def _params(sem=None, vmem=V7X_VMEM_LIMIT):
    if sem is None:
        return pltpu.CompilerParams(vmem_limit_bytes=vmem)
    return pltpu.CompilerParams(dimension_semantics=sem, vmem_limit_bytes=vmem)


def _rstd(v):
    return lax.rsqrt(jnp.mean(v * v, axis=-1, keepdims=True) + NORM_EPS)


def _norm_bwd(dn, n, rstd):
    return rstd * (dn - n * jnp.mean(dn * n, axis=-1, keepdims=True))


def _rope_lanes(cs_ref, spread_ref):
    return [lax.dot_general(cs_ref[...], spread_ref[k], NN, preferred_element_type=F32, precision=lax.Precision.HIGHEST)
            for k in range(3)]


def _rope_fwd(p, lanes):
    return p * lanes[0] + pltpu.roll(p, LANES - ROT_HALF, 1) * lanes[1] + pltpu.roll(p, ROT_HALF, 1) * lanes[2]


def _rope_bwd(dp, lanes):
    return dp * lanes[0] + pltpu.roll(dp * lanes[1], ROT_HALF, 1) + pltpu.roll(dp * lanes[2], LANES - ROT_HALF, 1)


def _gelu_parts(v):
    k2 = 2.0 * math.sqrt(2.0 / math.pi)
    c = 0.044715
    v2 = v * v
    s = jax.nn.sigmoid(v * (k2 + (k2 * c) * v2))
    g = v * s
    dg = s + g * (1.0 - s) * (k2 + (3.0 * k2 * c) * v2)
    return g, dg


def _halo_before(i, tile):
    return jnp.maximum(i * (tile // HALO) - 1, 0)


def _premix_inproj(x, sh, sc, g, w_in_t, rope, tm):
    s_len, d = x.shape
    n_proj = w_in_t.shape[0]
    n_slab = (n_proj - 256) // LANES

    def body(x_ref, sh_ref, sc_ref, g_ref, w_ref, cs_ref, spread_ref, h_ref, up_ref, qkv_ref):
        xv = x_ref[...]
        h = (xv * _rstd(xv) * g_ref[...]) * (1.0 + sc_ref[...]) + sh_ref[...]
        hb = h.astype(BF16)
        h_ref[...] = hb
        up_ref[...] = _dot(hb, w_ref[0:256, :], NT)
        lanes = _rope_lanes(cs_ref, spread_ref)
        for pair in range(n_slab // 2):
            p = _dot(hb, w_ref[256 + 256 * pair:512 + 256 * pair, :], NT)
            for half in range(2):
                ph = p[:, half * LANES:(half + 1) * LANES]
                if pair < 6:
                    ph = _rope_fwd(ph, lanes)
                if pair < 3:
                    ph = ph * (HEAD_DIM ** -0.5)
                qkv_ref[2 * pair + half] = ph

    vec = pl.BlockSpec((1, d), lambda i: (0, 0))
    return pl.pallas_call(
        body, name="premix_inproj", grid=(s_len // tm,),
        in_specs=[pl.BlockSpec((tm, d), lambda i: (i, 0)), vec, vec, vec,
                  pl.BlockSpec((n_proj, d), lambda i: (0, 0)),
                  pl.BlockSpec((tm, rope[0].shape[1]), lambda i: (i, 0)), pl.BlockSpec(rope[1].shape, lambda i: (0, 0, 0))],
        out_specs=[pl.BlockSpec((tm, d), lambda i: (i, 0)),
                   pl.BlockSpec((tm, 256), lambda i: (i, 0)),
                   pl.BlockSpec((n_slab, tm, LANES), lambda i: (0, i, 0))],
        out_shape=[jax.ShapeDtypeStruct((s_len, d), BF16),
                   jax.ShapeDtypeStruct((s_len, 256), F32),
                   jax.ShapeDtypeStruct((n_slab, s_len, LANES), F32)],
        compiler_params=_params(("arbitrary",)),
    )(x, sh, sc, g, w_in_t, *rope)


def _block_rows(n, r, dil):
    start = n * (BLOCK * dil) + r
    if dil == 1:
        return pl.ds(pl.multiple_of(start, BLOCK), BLOCK)
    return pl.ds(start, BLOCK, stride=dil)


def _band_mask(n):
    ri = lax.broadcasted_iota(jnp.int32, (BLOCK, 2 * BLOCK), 0)
    cj = lax.broadcasted_iota(jnp.int32, (BLOCK, 2 * BLOCK), 1)
    cur = (cj >= BLOCK) & (cj - BLOCK <= ri)
    prev = (cj < BLOCK) & (cj >= ri) & (n > 0)
    return cur | prev


def _attn_fwd(qkv):
    s_len = qkv.shape[1]
    n_g = len(DILATIONS)

    def body(q_ref, k_ref, v_ref, o_ref, lse_ref):
        lane = lax.broadcasted_iota(jnp.int32, (BLOCK, LANES), 1)
        first = lane < HEAD_DIM

        def group(dil):
            nb = s_len // (BLOCK * dil)

            def block(t, carry):
                r, n = t // nb, t % nb
                cur = _block_rows(n, r, dil)
                prev = _block_rows(jnp.maximum(n - 1, 0), r, dil)
                q = q_ref[0, cur, :]
                kcat = jnp.concatenate([k_ref[0, prev, :], k_ref[0, cur, :]], axis=0).astype(BF16)
                vcat = jnp.concatenate([v_ref[0, prev, :], v_ref[0, cur, :]], axis=0).astype(BF16)
                valid = _band_mask(n)
                q2 = jnp.concatenate([jnp.where(first, q, 0.0), jnp.where(first, 0.0, q)], axis=0).astype(BF16)
                s = jnp.where(jnp.concatenate([valid, valid], axis=0), _dot(q2, kcat, NT), MASKED)
                m = jnp.max(s, axis=-1, keepdims=True)
                p = jnp.exp(s - m)
                den = jnp.sum(p, axis=-1, keepdims=True)
                o2 = _dot(p.astype(BF16), vcat, NN) / den
                lse2 = m + jnp.log(den)
                o_ref[0, 0, cur, :] = jnp.where(first, o2[:BLOCK], o2[BLOCK:])
                lse_ref[0, 0, cur, :] = jnp.where(first, lse2[:BLOCK], lse2[BLOCK:])
                return carry

            lax.fori_loop(0, nb * dil, block, 0, unroll=ATTN_FWD_UNROLL)

        for gi, dil in enumerate(DILATIONS):
            pl.when(pl.program_id(0) == gi)(functools.partial(group, dil))

    def slab(base):
        return pl.BlockSpec((1, s_len, LANES), lambda g, s: (base + 2 * g + s, 0, 0))

    out = pl.BlockSpec((1, 1, s_len, LANES), lambda g, s: (g, s, 0, 0))
    shape = jax.ShapeDtypeStruct((n_g, 2, s_len, LANES), F32)
    return pl.pallas_call(
        body, name="attn_fwd", grid=(n_g, 2),
        in_specs=[slab(0), slab(6), slab(12)], out_specs=[out, out], out_shape=[shape, shape],
        compiler_params=_params(("arbitrary", "arbitrary")),
    )(qkv, qkv, qkv)


def _pool_mixed(u, halo, i, tm):
    ue = jnp.concatenate([halo, u], axis=0)
    s2 = ue + pltpu.roll(ue, 1, 0)
    s4 = s2 + pltpu.roll(s2, 2, 0)
    s8 = s4 + pltpu.roll(s4, 4, 0)
    s16 = s8 + pltpu.roll(s8, 8, 0)
    grp = lax.broadcasted_iota(jnp.int32, (tm, 256), 1) // HEAD_DIM
    pick = lambda a, b, c, e: jnp.where(grp == 0, a, jnp.where(grp == 1, b, jnp.where(grp == 2, c, e)))
    win_sum = pick(s2[HALO:], s4[HALO:], s8[HALO:], s16[HALO:])
    pos = (i * tm + lax.broadcasted_iota(jnp.int32, (tm, 256), 0)).astype(F32)
    count = jnp.minimum(pos + 1.0, pick(*[float(w) for w in POOL_WINDOWS]))
    return win_sum / count - u, count


def _mix_out(x, u_pool, o_g, lse_g, w_blk, b_pool, pool_scale, w_out_t, gt_m, g_post_mix, g_pre_ffn, sc_f, sh_f, tm):
    s_len, d = x.shape

    def body(x_ref, u_ref, uh_ref, o_ref, l_ref, wb_ref, bp_ref, ps_ref, wo_ref,
             gt_ref, g1_ref, g2_ref, sc_ref, sh_ref,
             x1_ref, y1_ref, h2_ref, cat_ref, attn_ref, lall_ref):
        (o0, o1, o2), (l0, l1, l2) = (o_ref.at[g] for g in range(3)), (l_ref.at[g] for g in range(3))
        i = pl.program_id(0)
        u = u_ref[...]
        halo = uh_ref[...] * (i > 0).astype(F32)
        mixed, _ = _pool_mixed(u, halo, i, tm)
        y = _dot(mixed.astype(BF16), wb_ref[...], NN) + bp_ref[...]
        pool = y * ps_ref[...]
        attn = []
        for s in range(2):
            la, lb, lc = l0[s], l1[s], l2[s]
            mx = jnp.maximum(jnp.maximum(la, lb), lc)
            ea, eb, ec = jnp.exp(la - mx), jnp.exp(lb - mx), jnp.exp(lc - mx)
            den = ea + eb + ec
            lall_ref[s] = mx + jnp.log(den)
            attn.append((ea / den) * o0[s] + (eb / den) * o1[s] + (ec / den) * o2[s])
        attn = jnp.concatenate(attn, axis=1)
        attn_ref[...] = attn
        cat = jnp.concatenate([pool, attn], axis=1).astype(BF16)
        cat_ref[...] = cat
        y1 = _dot(cat, wo_ref[...], NT)
        y1_ref[...] = y1.astype(BF16)
        x1 = x_ref[...] + gt_ref[...] * (y1 * _rstd(y1) * g1_ref[...])
        x1_ref[...] = x1
        h2 = (x1 * _rstd(x1) * g2_ref[...]) * (1.0 + sc_ref[...]) + sh_ref[...]
        h2_ref[...] = h2.astype(BF16)

    tile = lambda w: pl.BlockSpec((tm, w), lambda i: (i, 0))
    slab = pl.BlockSpec((2, tm, LANES), lambda i: (0, i, 0))
    groups = pl.BlockSpec((len(DILATIONS), 2, tm, LANES), lambda i: (0, 0, i, 0))
    const = lambda a: pl.BlockSpec(a.shape, lambda i: (0,) * a.ndim)
    return pl.pallas_call(
        body, name="mix_out", grid=(s_len // tm,),
        in_specs=[tile(d), tile(256), pl.BlockSpec((HALO, 256), lambda i: (_halo_before(i, tm), 0)),
                  groups, groups,
                  const(w_blk), const(b_pool), const(pool_scale), const(w_out_t),
                  const(gt_m), const(g_post_mix), const(g_pre_ffn), const(sc_f), const(sh_f)],
        out_specs=[tile(d), tile(d), tile(d), tile(512), tile(256), slab],
        out_shape=[jax.ShapeDtypeStruct((s_len, d), F32), jax.ShapeDtypeStruct((s_len, d), BF16),
                   jax.ShapeDtypeStruct((s_len, d), BF16), jax.ShapeDtypeStruct((s_len, 512), BF16),
                   jax.ShapeDtypeStruct((s_len, 256), F32), jax.ShapeDtypeStruct((2, s_len, LANES), F32)],
        compiler_params=_params(("arbitrary",)),
    )(x, u_pool, u_pool, o_g, lse_g, w_blk, b_pool, pool_scale, w_out_t, gt_m, g_post_mix, g_pre_ffn, sc_f, sh_f)


def _conv_gate(gate_ext, cw, cb):
    gc = gate_ext * cw[2:3, :] + pltpu.roll(gate_ext, 1, 0) * cw[1:2, :] + pltpu.roll(gate_ext, 2, 0) * cw[0:1, :]
    return gc[HALO:] + cb


def _ffn_fwd_loss(h2, x1, target, w_up_t, w_down, conv_w, conv_b, gt_f, g_post_ffn, tm, tf, ck):
    s_len, d = x1.shape
    d_ff = w_down.shape[0]
    n_f = d_ff // tf

    def body(h_ref, hh_ref, x1_ref, tgt_ref, wg_ref, wv_ref, wd_ref, cw_ref, cb_ref, gt_ref, g_ref,
             gate_ref, a_ref, act_ref, vd_ref, dy2_ref, dout_ref, sums_ref, loss_ref, acc_ref):
        i, j = pl.program_id(0), pl.program_id(1)

        @pl.when((i == 0) & (j == 0))
        def _():
            sums_ref[...] = jnp.zeros_like(sums_ref)
            loss_ref[...] = jnp.zeros_like(loss_ref)

        h = h_ref[...]
        h_ext = jnp.concatenate([hh_ref[...], h], axis=0)
        row = lax.broadcasted_iota(jnp.int32, (tm + HALO, ck), 0)
        no_halo = (row < HALO) & (i == 0)

        def up(c):
            cs = slice(c * ck, (c + 1) * ck)
            return jnp.where(no_halo, 0.0, _dot(h_ext, wg_ref[cs, :], NT)), _dot(h, wv_ref[cs, :], NT)

        part = None
        n_c = tf // ck
        nxt = up(0)
        for c in range(n_c):
            cs = slice(c * ck, (c + 1) * ck)
            gate_ext, val = nxt
            if c + 1 < n_c:
                nxt = up(c + 1)
            act, dact = _gelu_parts(_conv_gate(gate_ext, cw_ref[:, cs], cb_ref[:, cs]))
            a = (act * val).astype(BF16)
            gate_ref[:, cs] = gate_ext[HALO:].astype(BF16)
            a_ref[:, cs] = a
            act_ref[:, cs] = act.astype(BF16)
            vd_ref[:, cs] = (val * dact).astype(BF16)
            p = _dot(a, wd_ref[cs, :], NN)
            part = p if part is None else part + p

        @pl.when(j == 0)
        def _():
            acc_ref[...] = part

        @pl.when(j > 0)
        def _():
            acc_ref[...] += part

        @pl.when(j == n_f - 1)
        def _():
            y2 = acc_ref[...]
            rstd = _rstd(y2)
            n = y2 * rstd
            rn = n * g_ref[...]
            err = x1_ref[...] + gt_ref[...] * rn - tgt_ref[...]
            loss_ref[...] += 0.5 * jnp.sum(jnp.mean(err * err, axis=-1, keepdims=True), axis=0, keepdims=True)
            dout = err * (1.0 / d)
            dout_ref[...] = dout
            drn = dout * gt_ref[...]
            sums_ref[0:1, :] += jnp.sum(dout * rn, axis=0, keepdims=True)
            sums_ref[1:2, :] += jnp.sum(drn * n, axis=0, keepdims=True)
            dy2_ref[...] = _norm_bwd(drn * g_ref[...], n, rstd).astype(BF16)

    tok = lambda w: pl.BlockSpec((tm, w), lambda i, j: (i, 0))
    tokf = pl.BlockSpec((tm, tf), lambda i, j: (i, j))
    vec = pl.BlockSpec((1, d), lambda i, j: (0, 0))
    once = {"pipeline_mode": pl.Buffered(1)} if n_f == 1 else {}
    return pl.pallas_call(
        body, name="ffn_fwd_loss", grid=(s_len // tm, n_f),
        in_specs=[tok(d), pl.BlockSpec((HALO, d), lambda i, j: (_halo_before(i, tm), 0)), tok(d), tok(d),
                  pl.BlockSpec((tf, d), lambda i, j: (j, 0), **once),
                  pl.BlockSpec((tf, d), lambda i, j: (j + n_f, 0), **once),
                  pl.BlockSpec((tf, d), lambda i, j: (j, 0), **once),
                  pl.BlockSpec((3, tf), lambda i, j: (0, j)), pl.BlockSpec((1, tf), lambda i, j: (0, j)), vec, vec],
        out_specs=[tokf, tokf, tokf, tokf, tok(d), tok(d), pl.BlockSpec((8, d), lambda i, j: (0, 0)),
                   pl.BlockSpec((8, LANES), lambda i, j: (0, 0))],
        out_shape=[jax.ShapeDtypeStruct((s_len, d_ff), BF16)] * 4
        + [jax.ShapeDtypeStruct((s_len, d), BF16), jax.ShapeDtypeStruct((s_len, d), F32),
                   jax.ShapeDtypeStruct((8, d), F32), jax.ShapeDtypeStruct((8, LANES), F32)],
        scratch_shapes=[pltpu.VMEM((tm, d), F32)],
        compiler_params=_params(("arbitrary", "arbitrary")),
    )(h2, h2, x1, target, w_up_t, w_up_t, w_down, conv_w, conv_b, gt_f, g_post_ffn)


def _ffn_bwd_act(dy2, gate, a, act, vd, w_down, tm, tf, ck):
    s_len, d = dy2.shape
    d_ff = w_down.shape[0]
    n_t = s_len // tm
    chunks = [slice(lo, min(lo + ck, tf)) for lo in range(0, tf, ck)]

    def body(dy_ref, g_ref, gh_ref, a_ref, act_ref, vd_ref, wd_ref, dgc_ref, dval_ref, dwd_ref, dconv_ref, acc_ref):
        i = pl.program_id(1)

        @pl.when(i == 0)
        def _():
            acc_ref[...] = jnp.zeros_like(acc_ref)
            dconv_ref[...] = jnp.zeros_like(dconv_ref)

        dy = dy_ref[...]

        def down(cs):
            return _dot(dy, wd_ref[cs, :], NT)

        nxt = down(chunks[0])
        for c, cs in enumerate(chunks):
            width = cs.stop - cs.start
            da = nxt
            if c + 1 < len(chunks):
                nxt = down(chunks[c + 1])
            acc_ref[cs, :] += _dot(a_ref[:, cs], dy, TN)
            row = lax.broadcasted_iota(jnp.int32, (tm + HALO, width), 0)
            gate_ext = jnp.where((row < HALO) & (i == 0), 0.0,
                                 jnp.concatenate([gh_ref[:, cs], g_ref[:, cs]], axis=0).astype(F32))
            dgc = da * vd_ref[:, cs].astype(F32)
            dgc_ref[:, cs] = dgc.astype(BF16)
            dval_ref[:, cs] = (da * act_ref[:, cs].astype(F32)).astype(BF16)
            rows = [jnp.sum(dgc * pltpu.roll(gate_ext, 2 - k, 0)[HALO:], axis=0, keepdims=True) for k in range(2)]
            rows += [jnp.sum(dgc * gate_ext[HALO:], axis=0, keepdims=True), jnp.sum(dgc, axis=0, keepdims=True),
                     jnp.zeros((4, width), F32)]
            dconv_ref[:, cs] += jnp.concatenate(rows, axis=0)

        @pl.when(i == n_t - 1)
        def _():
            dwd_ref[...] = acc_ref[...].astype(BF16)

    tokf = pl.BlockSpec((tm, tf), lambda j, i: (i, j))
    return pl.pallas_call(
        body, name="ffn_bwd_act", grid=(d_ff // tf, n_t),
        in_specs=[pl.BlockSpec((tm, d), lambda j, i: (i, 0)), tokf,
                  pl.BlockSpec((HALO, tf), lambda j, i: (_halo_before(i, tm), j)), tokf, tokf, tokf,
                  pl.BlockSpec((tf, d), lambda j, i: (j, 0))],
        out_specs=[tokf, tokf, pl.BlockSpec((tf, d), lambda j, i: (j, 0)), pl.BlockSpec((8, tf), lambda j, i: (0, j))],
        out_shape=[jax.ShapeDtypeStruct((s_len, d_ff), BF16), jax.ShapeDtypeStruct((s_len, d_ff), BF16),
                   jax.ShapeDtypeStruct((d_ff, d), BF16), jax.ShapeDtypeStruct((8, d_ff), F32)],
        scratch_shapes=[pltpu.VMEM((tf, d), F32)],
        compiler_params=_params(("arbitrary", "arbitrary")),
    )(dy2, gate, gate, a, act, vd, w_down)


def _ffn_bwd_up(dgc, dval, w_up_t, conv_w, tm):
    s_len, d_ff = dgc.shape
    d = w_up_t.shape[1]
    n_t = s_len // tm

    def body(dg_ref, dgn_ref, dv_ref, cw_ref, w_ref, dup_ref, dh_ref):
        i = pl.program_id(0)
        nxt = dgn_ref[...].astype(F32) * (i < n_t - 1).astype(F32)
        ext = jnp.concatenate([dg_ref[...].astype(F32), nxt], axis=0)
        rows = tm + HALO
        dgate = (ext * cw_ref[2:3, :] + pltpu.roll(ext, rows - 1, 0) * cw_ref[1:2, :]
                 + pltpu.roll(ext, rows - 2, 0) * cw_ref[0:1, :])[:tm]
        dup = jnp.concatenate([dgate.astype(BF16), dv_ref[...]], axis=1)
        dup_ref[...] = dup
        dh_ref[...] = _dot(dup, w_ref[...], NN).astype(BF16)

    tokf = pl.BlockSpec((tm, d_ff), lambda i: (i, 0))
    return pl.pallas_call(
        body, name="ffn_bwd_up", grid=(n_t,),
        in_specs=[tokf, pl.BlockSpec((HALO, d_ff), lambda i: (jnp.minimum((i + 1) * (tm // HALO), s_len // HALO - 1), 0)),
                  tokf, pl.BlockSpec((3, d_ff), lambda i: (0, 0)), pl.BlockSpec((2 * d_ff, d), lambda i: (0, 0))],
        out_specs=[pl.BlockSpec((tm, 2 * d_ff), lambda i: (i, 0)), pl.BlockSpec((tm, d), lambda i: (i, 0))],
        out_shape=[jax.ShapeDtypeStruct((s_len, 2 * d_ff), BF16), jax.ShapeDtypeStruct((s_len, d), BF16)],
        compiler_params=_params(("arbitrary",)),
    )(dgc, dgc, dval, conv_w, w_up_t)


def _mix_bwd(dh2, dout, x1, y1, cat, attn, w_out_t, sc_f, g_pre_ffn, gt_m, g_post_mix, tm):
    s_len, d = x1.shape
    n_t = s_len // tm

    def body(dh_ref, do_ref, x1_ref, y1_ref, cat_ref, at_ref, wo_ref, sc_ref, g2_ref, gt_ref, g1_ref,
             dx1_ref, dpool_ref, dattn_ref, delta_ref, dwo_ref, sums_ref, acc_ref):
        i = pl.program_id(0)
        dh = dh_ref[...].astype(F32)
        x1 = x1_ref[...]
        r2 = _rstd(x1)
        n2 = x1 * r2
        ng = n2 * g2_ref[...]
        dng = dh * (1.0 + sc_ref[...])
        dx1 = do_ref[...] + _norm_bwd(dng * g2_ref[...], n2, r2)
        dx1_ref[...] = dx1
        y1 = y1_ref[...].astype(F32)
        r1 = _rstd(y1)
        n1 = y1 * r1
        drn = dx1 * gt_ref[...]
        dy1 = _norm_bwd(drn * g1_ref[...], n1, r1).astype(BF16)
        dcat = _dot(dy1, wo_ref[...], NN)
        dpool_ref[...] = dcat[:, 0:256]
        lane = lax.broadcasted_iota(jnp.int32, (tm, LANES), 1)
        first = lane < HEAD_DIM
        for s in range(2):
            da = dcat[:, 256 + s * LANES:256 + (s + 1) * LANES]
            dattn_ref[s] = da
            prod = da * at_ref[:, s * LANES:(s + 1) * LANES]
            tot = jnp.sum(prod, axis=-1, keepdims=True)
            lo = jnp.sum(jnp.where(first, prod, 0.0), axis=-1, keepdims=True)
            delta_ref[s] = jnp.where(first, lo, tot - lo)
        dwo = _dot(dy1, cat_ref[...], TN)
        sums = jnp.concatenate(
            [jnp.sum(dh, axis=0, keepdims=True), jnp.sum(dh * ng, axis=0, keepdims=True),
             jnp.sum(dng * n2, axis=0, keepdims=True), jnp.sum(dx1 * (n1 * g1_ref[...]), axis=0, keepdims=True),
             jnp.sum(drn * n1, axis=0, keepdims=True), jnp.zeros((3, d), F32)], axis=0)

        @pl.when(i == 0)
        def _():
            acc_ref[...] = dwo
            sums_ref[...] = sums

        @pl.when(i > 0)
        def _():
            acc_ref[...] += dwo
            sums_ref[...] += sums

        @pl.when(i == n_t - 1)
        def _():
            dwo_ref[...] = acc_ref[...].astype(BF16)

    tile = lambda w: pl.BlockSpec((tm, w), lambda i: (i, 0))
    slab = pl.BlockSpec((2, tm, LANES), lambda i: (0, i, 0))
    vec = pl.BlockSpec((1, d), lambda i: (0, 0))
    return pl.pallas_call(
        body, name="mix_bwd", grid=(n_t,),
        in_specs=[tile(d), tile(d), tile(d), tile(d), tile(512), tile(256),
                  pl.BlockSpec((d, 512), lambda i: (0, 0)), vec, vec, vec, vec],
        out_specs=[tile(d), tile(256), slab, slab, pl.BlockSpec((d, 512), lambda i: (0, 0)),
                   pl.BlockSpec((8, d), lambda i: (0, 0))],
        out_shape=[jax.ShapeDtypeStruct((s_len, d), F32), jax.ShapeDtypeStruct((s_len, 256), F32),
                   jax.ShapeDtypeStruct((2, s_len, LANES), F32), jax.ShapeDtypeStruct((2, s_len, LANES), F32),
                   jax.ShapeDtypeStruct((d, 512), BF16), jax.ShapeDtypeStruct((8, d), F32)],
        scratch_shapes=[pltpu.VMEM((d, 512), F32)],
        compiler_params=_params(("arbitrary",)),
    )(dh2, dout, x1, y1, cat, attn, w_out_t, sc_f, g_pre_ffn, gt_m, g_post_mix)


def _pool_bwd(dpool, u_pool, w_blk, b_pool, pool_scale, tm):
    s_len = dpool.shape[0]
    n_t = s_len // tm

    def body(dp_ref, dpn_ref, u_ref, uh_ref, wb_ref, bp_ref, ps_ref, du_ref, dwp_ref, sums_ref, acc_ref):
        i = pl.program_id(0)
        u = u_ref[...]
        mixed, _ = _pool_mixed(u, uh_ref[...] * (i > 0).astype(F32), i, tm)
        mixed_b = mixed.astype(BF16)
        y = _dot(mixed_b, wb_ref[...], NN) + bp_ref[...]
        dp = dp_ref[...]
        dy = dp * ps_ref[...]
        dwb = _dot(mixed_b, dy.astype(BF16), TN)
        sums = jnp.concatenate([jnp.sum(dy, axis=0, keepdims=True), jnp.sum(dp * y, axis=0, keepdims=True),
                                jnp.zeros((6, 256), F32)], axis=0)
        dp_ext = jnp.concatenate([dp, dpn_ref[...] * (i < n_t - 1).astype(F32)], axis=0)
        dmix = _dot((dp_ext * ps_ref[...]).astype(BF16), wb_ref[...], NT)
        rows = tm + HALO
        grp = lax.broadcasted_iota(jnp.int32, (rows, 256), 1) // HEAD_DIM
        pick = lambda a, b, c, e: jnp.where(grp == 0, a, jnp.where(grp == 1, b, jnp.where(grp == 2, c, e)))
        pos = (i * tm + lax.broadcasted_iota(jnp.int32, (rows, 256), 0)).astype(F32)
        z = dmix / jnp.minimum(pos + 1.0, pick(*[float(w) for w in POOL_WINDOWS]))
        f2 = z + pltpu.roll(z, rows - 1, 0)
        f4 = f2 + pltpu.roll(f2, rows - 2, 0)
        f8 = f4 + pltpu.roll(f4, rows - 4, 0)
        f16 = f8 + pltpu.roll(f8, rows - 8, 0)
        du_ref[...] = (pick(f2, f4, f8, f16) - dmix)[:tm]

        @pl.when(i == 0)
        def _():
            acc_ref[...] = dwb
            sums_ref[...] = sums

        @pl.when(i > 0)
        def _():
            acc_ref[...] += dwb
            sums_ref[...] += sums

        @pl.when(i == n_t - 1)
        def _():
            full = acc_ref[...]
            for gi in range(len(POOL_WINDOWS)):
                lo = gi * HEAD_DIM
                dwp_ref[gi] = full[lo:lo + HEAD_DIM, lo:lo + HEAD_DIM]

    n_g = len(POOL_WINDOWS)
    tile = pl.BlockSpec((tm, 256), lambda i: (i, 0))
    const = lambda a: pl.BlockSpec(a.shape, lambda i: (0,) * a.ndim)
    return pl.pallas_call(
        body, name="pool_bwd", grid=(n_t,),
        in_specs=[tile, pl.BlockSpec((HALO, 256), lambda i: (jnp.minimum((i + 1) * (tm // HALO), s_len // HALO - 1), 0)),
                  tile, pl.BlockSpec((HALO, 256), lambda i: (_halo_before(i, tm), 0)),
                  const(w_blk), const(b_pool), const(pool_scale)],
        out_specs=[tile, pl.BlockSpec((n_g, HEAD_DIM, HEAD_DIM), lambda i: (0, 0, 0)), pl.BlockSpec((8, 256), lambda i: (0, 0))],
        out_shape=[jax.ShapeDtypeStruct((s_len, 256), F32), jax.ShapeDtypeStruct((n_g, HEAD_DIM, HEAD_DIM), F32),
                   jax.ShapeDtypeStruct((8, 256), F32)],
        scratch_shapes=[pltpu.VMEM((256, 256), F32)],
        compiler_params=_params(("arbitrary",)),
    )(dpool, dpool, u_pool, u_pool, w_blk, b_pool, pool_scale)


def _attn_bwd(qkv, dattn, lse_all, delta):
    s_len = qkv.shape[1]
    n_g = len(DILATIONS)

    def body(q_ref, k_ref, v_ref, do_ref, l_ref, dl_ref, dq_ref, dk_ref, dv_ref):
        lane = lax.broadcasted_iota(jnp.int32, (BLOCK, LANES), 1)
        first = lane < HEAD_DIM

        def group(dil):
            nb = s_len // (BLOCK * dil)

            def block(t, carry):
                dk_part, dv_part = carry
                r, n = t // nb, t % nb
                cur = _block_rows(n, r, dil)
                prev = _block_rows(jnp.maximum(n - 1, 0), r, dil)
                q = q_ref[0, cur, :]
                do = do_ref[0, cur, :]
                lse = l_ref[0, cur, :]
                dlt = dl_ref[0, cur, :]
                kcat = jnp.concatenate([k_ref[0, prev, :], k_ref[0, cur, :]], axis=0).astype(BF16)
                vcat = jnp.concatenate([v_ref[0, prev, :], v_ref[0, cur, :]], axis=0).astype(BF16)
                valid = _band_mask(n)
                stack = lambda a: jnp.concatenate([jnp.where(first, a, 0.0), jnp.where(first, 0.0, a)], axis=0)
                rows2 = lambda a: jnp.concatenate([a[:, 0:1], a[:, HEAD_DIM:HEAD_DIM + 1]], axis=0)
                q2, do2 = stack(q).astype(BF16), stack(do).astype(BF16)
                valid2 = jnp.concatenate([valid, valid], axis=0)
                p = jnp.where(valid2, jnp.exp(_dot(q2, kcat, NT) - rows2(lse)), 0.0)
                ds = (p * (_dot(do2, vcat, NT) - rows2(dlt))).astype(BF16)
                dq2 = _dot(ds, kcat, NN)
                dq_ref[0, 0, cur, :] = jnp.where(first, dq2[:BLOCK], dq2[BLOCK:])
                dkc = _dot(ds, q2, TN)
                dvc = _dot(p.astype(BF16), do2, TN)
                dk_ref[0, 0, prev, :] = dk_part + dkc[:BLOCK]
                dv_ref[0, 0, prev, :] = dv_part + dvc[:BLOCK]
                dk_ref[0, 0, cur, :] = dkc[BLOCK:]
                dv_ref[0, 0, cur, :] = dvc[BLOCK:]
                return dkc[BLOCK:], dvc[BLOCK:]

            def blocks(tt, carry):
                for u in range(ATTN_BWD_UNROLL):
                    carry = block(tt * ATTN_BWD_UNROLL + u, carry)
                return carry

            zero = jnp.zeros((BLOCK, LANES), F32)
            lax.fori_loop(0, nb * dil // ATTN_BWD_UNROLL, blocks, (zero, zero))

        for gi, dil in enumerate(DILATIONS):
            pl.when(pl.program_id(1) == gi)(functools.partial(group, dil))

    def slab(base):
        return pl.BlockSpec((1, s_len, LANES), lambda s, g: (base + 2 * g + s, 0, 0))

    one = pl.BlockSpec((1, s_len, LANES), lambda s, g: (s, 0, 0))
    out = pl.BlockSpec((1, 1, s_len, LANES), lambda s, g: (g, s, 0, 0))
    shape = jax.ShapeDtypeStruct((n_g, 2, s_len, LANES), F32)
    return pl.pallas_call(
        body, name="attn_bwd", grid=(2, n_g),
        in_specs=[slab(0), slab(6), slab(12), one, one, one],
        out_specs=[out, out, out], out_shape=[shape, shape, shape],
        compiler_params=_params(("arbitrary", "arbitrary")),
    )(qkv, qkv, qkv, dattn, lse_all, delta)


def _dproj_assemble(du, dqkv, rope, tm):
    s_len = du.shape[0]
    n_proj = 256 + 18 * LANES

    def body(du_ref, dq_ref, dk_ref, dv_ref, cs_ref, spread_ref, dproj_ref):
        dproj_ref[:, 0:256] = du_ref[...].astype(BF16)
        lanes = _rope_lanes(cs_ref, spread_ref)
        col = 256
        for kind, dref in enumerate((dq_ref, dk_ref, dv_ref)):
            for grp in range(3):
                for s in range(2):
                    piece = dref[grp, s]
                    if kind < 2:
                        piece = _rope_bwd(piece, lanes)
                    if kind == 0:
                        piece = piece * (HEAD_DIM ** -0.5)
                    dproj_ref[:, col:col + LANES] = piece.astype(BF16)
                    col += LANES

    groups = pl.BlockSpec((len(DILATIONS), 2, tm, LANES), lambda i: (0, 0, i, 0))
    return pl.pallas_call(
        body, name="dproj_assemble", grid=(s_len // tm,),
        in_specs=[pl.BlockSpec((tm, 256), lambda i: (i, 0))] + [groups] * 3
        + [pl.BlockSpec((tm, rope[0].shape[1]), lambda i: (i, 0)), pl.BlockSpec(rope[1].shape, lambda i: (0, 0, 0))],
        out_specs=pl.BlockSpec((tm, n_proj), lambda i: (i, 0)),
        out_shape=jax.ShapeDtypeStruct((s_len, n_proj), BF16),
        compiler_params=_params(("arbitrary",)),
    )(du, *dqkv, *rope)


def _inproj_bwd(dproj, w_in_t, x, dx1, sc_m, g_pre_mix, tm):
    s_len, d = x.shape
    n_proj = w_in_t.shape[0]
    n_t = s_len // tm

    def body(dproj_ref, w_ref, x_ref, dx1_ref, sc_ref, g_ref, dx_ref, sums_ref):
        i = pl.program_id(0)
        halves = [slice(0, tm // 2), slice(tm // 2, tm)]
        dhs = [_dot(dproj_ref[rs, :], w_ref[...], NN) for rs in halves]
        sums = None
        for rs, dh in zip(halves, dhs):
            xv = x_ref[rs, :]
            r = _rstd(xv)
            n = xv * r
            dng = dh * (1.0 + sc_ref[...])
            dx_ref[rs, :] = dx1_ref[rs, :] + _norm_bwd(dng * g_ref[...], n, r)
            part = jnp.concatenate([jnp.sum(dh, axis=0, keepdims=True), jnp.sum(dh * (n * g_ref[...]), axis=0, keepdims=True),
                                    jnp.sum(dng * n, axis=0, keepdims=True), jnp.zeros((5, d), F32)], axis=0)
            sums = part if sums is None else sums + part

        @pl.when(i == 0)
        def _():
            sums_ref[...] = sums

        @pl.when(i > 0)
        def _():
            sums_ref[...] += sums

    tile = lambda w: pl.BlockSpec((tm, w), lambda i: (i, 0))
    vec = pl.BlockSpec((1, d), lambda i: (0, 0))
    return pl.pallas_call(
        body, name="inproj_bwd", grid=(n_t,),
        in_specs=[tile(n_proj), pl.BlockSpec((n_proj, d), lambda i: (0, 0)), tile(d), tile(d), vec, vec],
        out_specs=[tile(d), pl.BlockSpec((8, d), lambda i: (0, 0))],
        out_shape=[jax.ShapeDtypeStruct((s_len, d), F32), jax.ShapeDtypeStruct((8, d), F32)],
        compiler_params=_params(("arbitrary",)),
    )(dproj, w_in_t, x, dx1, sc_m, g_pre_mix)


def _wgrad(a, b, name, tk, tmm):
    s_len, m = a.shape
    n = b.shape[1]
    n_k = s_len // tk

    def body(a_ref, b_ref, o_ref, acc_ref):
        k = pl.program_id(1)
        part = _dot(a_ref[...], b_ref[...], TN)

        @pl.when(k == 0)
        def _():
            acc_ref[...] = part

        @pl.when(k > 0)
        def _():
            acc_ref[...] += part

        @pl.when(k == n_k - 1)
        def _():
            o_ref[...] = acc_ref[...].astype(BF16)

    return pl.pallas_call(
        body, name=name, grid=(m // tmm, n_k),
        in_specs=[pl.BlockSpec((tk, tmm), lambda j, k: (k, j)), pl.BlockSpec((tk, n), lambda j, k: (k, 0))],
        out_specs=pl.BlockSpec((tmm, n), lambda j, k: (j, 0)),
        out_shape=jax.ShapeDtypeStruct((m, n), BF16),
        scratch_shapes=[pltpu.VMEM((tmm, n), F32)],
        compiler_params=_params(("arbitrary", "arbitrary")),
    )(a, b)


def _place():
    return lax.axis_index("x"), lax.axis_index("y"), lax.axis_index("c")


def _peer(k):
    x, y, c = _place()
    bx, by, bc = (k >> 2) & 1, (k >> 1) & 1, k & 1
    return (x ^ bx if bx else x, y ^ by if by else y, c ^ bc if bc else c)


def _index(pos):
    return 4 * pos[0] + 2 * pos[1] + pos[2]


def _entry_exchange(c_rows, w_ada, b_ada_cols, taps, shards):
    d = c_rows.shape[1]
    ncol = w_ada.shape[1]
    n_w = len(shards)

    def body(c_ref, w_ref, b_ref, t_ref, *rest):
        srcs = rest[:n_w]
        call_ref, mod_ref, tall_ref = rest[n_w:n_w + 3]
        outs = rest[n_w + 3:2 * n_w + 3]
        stage_ref, s_send, s_recv, w_send, w_recv, local_sems = rest[2 * n_w + 3:]
        x, y, c = _place()
        here, sibling = (x, y, c), (x, y, 1 - c)
        chips = [(1 - x, y), (x, 1 - y), (1 - x, 1 - y)]
        me = _index(here)

        def small(kind, src, dst, k):
            return pltpu.make_async_remote_copy(src_ref=src, dst_ref=dst, send_sem=s_send.at[kind, k - 1],
                                                recv_sem=s_recv.at[kind, k - 1], device_id=_peer(k), device_id_type=MESH)

        gather = lambda k: small(0, c_ref, call_ref.at[me], k)
        scatter = lambda k: small(1, stage_ref.at[_index(_peer(k))], mod_ref.at[me], k)
        gather_taps = lambda k: small(2, t_ref, tall_ref.at[me], k)

        def rows(w, pos):
            r = shards[w].shape[0]
            return outs[w].at[pl.ds(pl.multiple_of(_index(pos) * r, 16), r), :]

        def block(k, w, pos, to, own=False):
            return pltpu.make_async_remote_copy(
                src_ref=srcs[w] if own else rows(w, pos), dst_ref=rows(w, pos),
                send_sem=w_send.at[k, w], recv_sem=w_recv.at[k, w], device_id=to, device_id_type=MESH)

        call_ref[me] = c_ref[...]
        tall_ref[me] = t_ref[...]
        for k in range(1, N_DEV):
            gather(k).start()
        for k in range(1, N_DEV):
            gather_taps(k).start()
        mine = [pltpu.make_async_copy(srcs[w], rows(w, here), local_sems.at[w]) for w in range(n_w)]
        for cp in mine:
            cp.start()
        first = [block(0, w, here, sibling, own=True) for w in range(n_w)]
        first += [block(1 + j, w, here, (*chip, c), own=True) for j, chip in enumerate(chips) for w in range(n_w)]
        for cp in first:
            cp.start()

        for k in range(1, N_DEV):
            gather(k).wait_recv()
        cv = jnp.concatenate([call_ref[b, 0:1, :] for b in range(N_DEV)], axis=0)
        act = cv * jax.nn.sigmoid(cv)
        mod = lax.dot_general(act, w_ref[...], NN, preferred_element_type=F32,
                              precision=lax.Precision.HIGHEST) + b_ref[...]
        for b in range(N_DEV):
            stage_ref[b] = jnp.broadcast_to(mod[b:b + 1, :], (8, ncol))
        mod_ref[me] = stage_ref[me]
        for k in range(1, N_DEV):
            scatter(k).start()

        passed = []
        for j, chip in enumerate(chips):
            for w in range(n_w):
                block(1 + j, w, (*chip, c), here).wait_recv()
                fwd = block(4 + j, w, (*chip, c), sibling)
                fwd.start()
                passed.append(fwd)
        for w in range(n_w):
            block(0, w, sibling, here).wait_recv()
        for j, chip in enumerate(chips):
            for w in range(n_w):
                block(4 + j, w, (*chip, 1 - c), here).wait_recv()
        for k in range(1, N_DEV):
            scatter(k).wait_recv()
            gather_taps(k).wait_recv()
        for cp in first + passed:
            cp.wait_send()
        for k in range(1, N_DEV):
            gather(k).wait_send()
            scatter(k).wait_send()
            gather_taps(k).wait_send()
        for cp in mine:
            cp.wait()

    vmem, hbm = pl.BlockSpec(memory_space=pltpu.VMEM), pl.BlockSpec(memory_space=pltpu.HBM)
    out = pl.pallas_call(
        body, name="entry_exchange",
        in_specs=[vmem] * 4 + [hbm] * n_w, out_specs=[vmem] * 3 + [hbm] * n_w,
        out_shape=[jax.ShapeDtypeStruct((N_DEV, 8, d), F32), jax.ShapeDtypeStruct((N_DEV, 8, ncol), F32),
                   jax.ShapeDtypeStruct((N_DEV,) + taps.shape, F32)]
        + [jax.ShapeDtypeStruct((N_DEV * s.shape[0], s.shape[1]), s.dtype) for s in shards],
        scratch_shapes=[pltpu.VMEM((N_DEV, 8, ncol), F32), pltpu.SemaphoreType.DMA((3, N_DEV - 1)),
                        pltpu.SemaphoreType.DMA((3, N_DEV - 1)), pltpu.SemaphoreType.DMA((N_DEV - 1, n_w)),
                        pltpu.SemaphoreType.DMA((N_DEV - 1, n_w)), pltpu.SemaphoreType.DMA((n_w,))],
        compiler_params=_params(),
    )(c_rows, w_ada, b_ada_cols, taps, *shards)
    return out[0], out[1], out[2], out[3:]


def _peer_copies(mode, srcs, lands, send_sems, recv_sems):
    if mode in ("gather_ici", "gather_d2d"):
        x, y, c = _place()
        sibling = (x, y, 1 - c)
        chips = [(1 - x, y), (x, 1 - y), (1 - x, 1 - y)]
        n = len(lands)

        def rows(w, pos):
            r = lands[w].shape[0] // N_DEV
            return lands[w].at[pl.ds(pl.multiple_of(_index(pos) * r, 16), r), :]

        def copy(k, w, src, dst, to):
            return pltpu.make_async_remote_copy(src_ref=src, dst_ref=dst, send_sem=send_sems.at[k * n + w],
                                                recv_sem=recv_sems.at[k * n + w], device_id=to, device_id_type=MESH)

        if mode == "gather_ici":
            targets = [sibling] + [(*chip, c) for chip in chips]
            return [copy(k, w, srcs[w], rows(w, (x, y, c)), to) for k, to in enumerate(targets) for w in range(n)]
        return [copy(j, w, rows(w, (*chip, c)), rows(w, (*chip, c)), sibling)
                for j, chip in enumerate(chips) for w in range(n)]
    me = _index(_place())
    copies = []
    for k in range(1, N_DEV):
        peer = _peer(k)
        for w, (src, land) in enumerate(zip(srcs, lands)):
            if mode == "gather":
                r = src.shape[0]
                dst = land.at[pl.ds(pl.multiple_of(me * r, 16), r), :]
            elif mode == "allgather":
                dst = land.at[me]
            else:
                r = src.shape[0] // N_DEV
                src = src.at[pl.ds(pl.multiple_of(_index(peer) * r, 16), r), :]
                dst = land.at[me]
            copies.append(pltpu.make_async_remote_copy(
                src_ref=src, dst_ref=dst, send_sem=send_sems.at[(k - 1) * len(srcs) + w],
                recv_sem=recv_sems.at[(k - 1) * len(srcs) + w],
                device_id=peer, device_id_type=MESH))
    return copies


def _landing_zone(mode, src, me, name):
    cols = src.shape[1]
    if mode == "gather":
        r = src.shape[0]
        in_spec = pl.BlockSpec((r, cols), lambda i, me_ref: (0, 0))
        out_spec = pl.BlockSpec((r, cols), lambda i, me_ref: (me_ref[0], 0))
        out_shape = jax.ShapeDtypeStruct((N_DEV * r, cols), src.dtype)
    else:
        r = src.shape[0] // N_DEV
        in_spec = pl.BlockSpec((r, cols), lambda i, me_ref: (me_ref[0], 0))
        out_spec = pl.BlockSpec((1, r, cols), lambda i, me_ref: (me_ref[0], 0, 0))
        out_shape = jax.ShapeDtypeStruct((N_DEV, r, cols), src.dtype)

    def body(me_ref, s_ref, o_ref):
        o_ref[...] = s_ref[...].reshape(o_ref.shape)

    return pl.pallas_call(
        body, name=name, out_shape=out_shape,
        grid_spec=pltpu.PrefetchScalarGridSpec(num_scalar_prefetch=1, grid=(1,), in_specs=[in_spec], out_specs=out_spec),
        compiler_params=_params(("arbitrary",)),
    )(me.reshape(1).astype(jnp.int32), src)


def _exchange_start(mode, srcs, lands, name):
    n_s, n_a = len(srcs), len(srcs) + len(lands)
    n_cp = _COPIES_PER_ARRAY.get(mode, N_DEV - 1) * len(lands)

    def body(*refs):
        for cp in _peer_copies(mode, refs[:n_s], refs[n_s:n_a], refs[n_a], refs[n_a + 1]):
            cp.start()
        refs[-1][...] = jnp.zeros_like(refs[-1])

    hbm, sem = pl.BlockSpec(memory_space=pltpu.HBM), pl.BlockSpec(memory_space=pltpu.SEMAPHORE)
    arrays = list(srcs) + list(lands)
    out = pl.pallas_call(
        body, name=name,
        out_shape=(pltpu.SemaphoreType.DMA((n_cp,)), pltpu.SemaphoreType.DMA((n_cp,)),
                   *[pltpu.HBM(a.shape, a.dtype) for a in arrays], jax.ShapeDtypeStruct((8, LANES), F32)),
        in_specs=[hbm] * n_a, out_specs=(sem, sem, *[hbm] * n_a, pl.BlockSpec(memory_space=pltpu.VMEM)),
        input_output_aliases={i: 2 + i for i in range(n_a)},
        compiler_params=pltpu.CompilerParams(has_side_effects=pltpu.SideEffectType.DATAFLOW_SIDE_EFFECTING),
    )(*[pltpu.with_memory_space_constraint(a, pltpu.HBM) for a in arrays])
    return out[0], out[1], out[2:2 + n_s], out[2 + n_s:2 + n_a], out[-1]


_COPIES_PER_ARRAY = {"gather_ici": 4, "gather_d2d": 3}


def _exchange_wait(mode, send_sems, recv_sems, srcs, lands, after, name):
    n_s, n_a = len(srcs), len(srcs) + len(lands)

    def body(*refs):
        copies = _peer_copies(mode, refs[:n_s], refs[n_s:n_a], refs[n_a], refs[n_a + 1])
        for cp in copies:
            cp.wait_send()
        for cp in copies:
            cp.wait_recv()

    hbm, sem = pl.BlockSpec(memory_space=pltpu.HBM), pl.BlockSpec(memory_space=pltpu.SEMAPHORE)
    arrays = list(srcs) + list(lands)
    out = pl.pallas_call(
        body, name=name, out_shape=tuple(pltpu.HBM(a.shape, a.dtype) for a in arrays),
        in_specs=[hbm] * n_a + [sem, sem, pl.BlockSpec(memory_space=pl.ANY)], out_specs=tuple([hbm] * n_a),
        input_output_aliases={i: i for i in range(n_a)},
        compiler_params=pltpu.CompilerParams(has_side_effects=pltpu.SideEffectType.DATAFLOW_SIDE_EFFECTING),
    )(*arrays, send_sems, recv_sems, after)
    return out[n_s:]


SMALL_WEIGHTS = ("b_ada", "g_pre_mix", "g_post_mix", "g_pre_ffn", "g_post_ffn", "w_pool", "b_pool", "pool_scale", "conv_b")


MOD_ROWS = ((0, 0), (0, 1), (1, 3), (1, 0), (1, 1), (2, 0))


def _small_sum(mine, gathered):
    n_l = len(mine)
    d = mine[0].shape[1]

    def body(*refs):
        loc, got = refs[:n_l], refs[n_l:2 * n_l]
        tot_refs, dmod_ref = refs[2 * n_l:3 * n_l], refs[3 * n_l]
        me = _index(_place())
        part = lambda a, dev: jnp.where(dev == me, loc[a][...], got[a][dev])
        for a in range(n_l):
            tot = part(a, 0)
            for dev in range(1, N_DEV):
                tot = tot + part(a, dev)
            tot_refs[a][...] = tot
        for dev in range(N_DEV):
            for k, (a, r) in enumerate(MOD_ROWS):
                dmod_ref[dev:dev + 1, k * d:(k + 1) * d] = part(a, dev)[r:r + 1, :]

    vmem = pl.BlockSpec(memory_space=pltpu.VMEM)
    out = pl.pallas_call(
        body, name="small_sum", in_specs=[vmem] * (2 * n_l), out_specs=[vmem] * (n_l + 1),
        out_shape=[jax.ShapeDtypeStruct(a.shape, F32) for a in mine] + [jax.ShapeDtypeStruct((N_DEV, 6 * d), F32)],
        compiler_params=_params(),
    )(*mine, *gathered)
    return out[:n_l], out[n_l]


def _small_adam(totals, weights, moms, vels):
    n_t, n_w = len(totals), len(weights)

    def body(*refs):
        t_in, t_mix, t_ffn, t_pool, t_blk, t_conv, _ = (r[...] for r in refs[:n_t])
        w_refs, m_refs, v_refs = (refs[n_t + k * n_w:n_t + (k + 1) * n_w] for k in range(3))
        outs = refs[n_t + 3 * n_w:]

        def update(idx, g, at=()):
            sel = lambda ref: ref.at[at] if at else ref
            delta, nm, nv = _adam_math(sel(w_refs[idx])[...], g, sel(m_refs[idx])[...], sel(v_refs[idx])[...])
            for k, val in enumerate((g, delta, nm, nv)):
                sel(outs[4 * idx + k])[...] = val

        tots = (t_in, t_mix, t_ffn)
        update(0, jnp.concatenate([tots[a][r:r + 1] for a, r in MOD_ROWS], axis=1))
        update(1, t_in[2:3])
        update(2, t_mix[4:5])
        update(3, t_mix[2:3])
        update(4, t_ffn[1:2])
        for gi in range(len(POOL_WINDOWS)):
            update(5, t_blk[gi], at=(0, gi))
        update(6, jnp.concatenate([t_pool[0:1, gi * HEAD_DIM:(gi + 1) * HEAD_DIM] for gi in range(len(POOL_WINDOWS))], axis=0),
               at=(0,))
        update(7, t_pool[1:2])
        update(8, t_conv[3:4])

    vmem = pl.BlockSpec(memory_space=pltpu.VMEM)
    return pl.pallas_call(
        body, name="small_adam", in_specs=[vmem] * (n_t + 3 * n_w), out_specs=[vmem] * (4 * n_w),
        out_shape=[jax.ShapeDtypeStruct(w.shape, F32) for w in weights for _ in range(4)],
        compiler_params=_params(),
    )(*totals, *weights, *moms, *vels)


def _adam_math(w, g, m, v):
    m = ADAM_B1 * m + (1.0 - ADAM_B1) * g
    v = ADAM_B2 * v + (1.0 - ADAM_B2) * (g * g)
    m_hat = m / (1.0 - ADAM_B1 ** ADAM_STEP)
    v_hat = v / (1.0 - ADAM_B2 ** ADAM_STEP)
    delta = -ADAM_LR * (m_hat / (jnp.sqrt(v_hat) + ADAM_EPS) + ADAM_WD * w)
    return delta, m, v


def _adam(w, g, m, v, name, tr):
    rows, cols = w.shape

    def body(w_ref, g_ref, m_ref, v_ref, d_ref, nm_ref, nv_ref):
        d_ref[...], nm_ref[...], nv_ref[...] = _adam_math(w_ref[...], g_ref[...], m_ref[...], v_ref[...])

    spec = pl.BlockSpec((tr, cols), lambda i: (i, 0))
    shape = jax.ShapeDtypeStruct((rows, cols), F32)
    return pl.pallas_call(
        body, name=name, grid=(rows // tr,), in_specs=[spec] * 4, out_specs=[spec] * 3,
        out_shape=[shape] * 3, compiler_params=_params(("arbitrary",)),
    )(w, g, m, v)


def _sum_adam(parts, w, m, v, name, tr):
    _, rows, cols = parts.shape

    def body(p_ref, w_ref, m_ref, v_ref, g_ref, d_ref, nm_ref, nv_ref):
        g = p_ref[0].astype(F32)
        for dev in range(1, N_DEV):
            g = g + p_ref[dev].astype(F32)
        g_ref[...] = g
        d_ref[...], nm_ref[...], nv_ref[...] = _adam_math(w_ref[...], g, m_ref[...], v_ref[...])

    spec = pl.BlockSpec((tr, cols), lambda i: (i, 0))
    shape = jax.ShapeDtypeStruct((rows, cols), F32)
    return pl.pallas_call(
        body, name=name, grid=(rows // tr,),
        in_specs=[pl.BlockSpec((N_DEV, tr, cols), lambda i: (0, i, 0)), spec, spec, spec],
        out_specs=[spec] * 4, out_shape=[shape] * 4, compiler_params=_params(("arbitrary",)),
    )(parts, w, m, v)


def _ada_grad_adam(c_all, dmod_cols, w, m, v, tr):
    rows, cols = w.shape

    def body(c_ref, dm_ref, w_ref, m_ref, v_ref, g_ref, d_ref, nm_ref, nv_ref):
        cv = c_ref[...]
        act = cv * jax.nn.sigmoid(cv)
        g = lax.dot_general(act, dm_ref[...], TN, preferred_element_type=F32, precision=lax.Precision.HIGHEST)
        g_ref[...] = g
        d_ref[...], nm_ref[...], nv_ref[...] = _adam_math(w_ref[...], g, m_ref[...], v_ref[...])

    spec = pl.BlockSpec((tr, cols), lambda i: (i, 0))
    shape = jax.ShapeDtypeStruct((rows, cols), F32)
    return pl.pallas_call(
        body, name="ada_grad_adam", grid=(rows // tr,),
        in_specs=[pl.BlockSpec((N_DEV, tr), lambda i: (0, i)), pl.BlockSpec((N_DEV, cols), lambda i: (0, 0)), spec, spec, spec],
        out_specs=[spec] * 4, out_shape=[shape] * 4, compiler_params=_params(("arbitrary",)),
    )(c_all, dmod_cols, w, m, v)


def _rope_tables(positions):
    inv_freq = ROPE_THETA ** (-jnp.arange(0, 2 * ROT_HALF, 2, dtype=F32) / (2 * ROT_HALF))
    ang = positions.astype(F32)[:, None] * inv_freq
    rows = jnp.concatenate([jnp.cos(ang), jnp.sin(ang), jnp.ones_like(ang)], axis=1)
    spread = [[[0.0] * LANES for _ in range(3 * ROT_HALF)] for _ in range(3)]
    for lane in range(LANES):
        p, j = lane % HEAD_DIM, lane % ROT_HALF
        if p < ROT_HALF:
            spread[0][j][lane] = 1.0
            spread[1][ROT_HALF + j][lane] = -1.0
        elif p < 2 * ROT_HALF:
            spread[0][j][lane] = 1.0
            spread[2][ROT_HALF + j][lane] = 1.0
        else:
            spread[0][2 * ROT_HALF][lane] = 1.0
    return rows, jnp.array(spread, F32)


def _pad_rows(a, rows):
    return jnp.pad(a, ((0, rows - a.shape[0]), (0, 0)))


def _sequence_step(xs, target, rope, mods, gains, w_in_t, w_out_t, relay_ffn, fetch_ffn, send_ffn_grads, send_mix_grads, w_blk_b, b_pool_r,
                   pool_scale_r, conv_w_all, conv_b):
    sh_m, sc_m, gt_m, sh_f, sc_f, gt_f = mods
    g_pre_mix, g_post_mix, g_pre_ffn, g_post_ffn = gains
    h1, u_pool, qkv = _premix_inproj(xs, sh_m, sc_m, g_pre_mix, w_in_t, rope, tm=512)
    o_g, lse_g = _attn_fwd(qkv)
    x1, y1, h2, cat, attn, lse_all = _mix_out(xs, u_pool, o_g, lse_g, w_blk_b, b_pool_r, pool_scale_r, w_out_t,
                                              gt_m, g_post_mix, g_pre_ffn, sc_f, sh_f, tm=256)
    token = relay_ffn(x1)
    w_up_t, w_down_f = fetch_ffn(x1 if token is None else token)
    gate, a_ffn, act, vd, dy2, dout, sums_ffn, loss_loc = _ffn_fwd_loss(h2, x1, target, w_up_t, w_down_f, conv_w_all, conv_b,
                                                              gt_f, g_post_ffn, tm=256, tf=2816, ck=256)

    dgc, dval, dw_down, dconv = _ffn_bwd_act(dy2, gate, a_ffn, act, vd, w_down_f, tm=512, tf=1408, ck=256)
    dup, dh2 = _ffn_bwd_up(dgc, dval, w_up_t, conv_w_all, tm=256)
    dw_up_t = _wgrad(dup, h2, "wgrad_up", tk=2048, tmm=1408)
    token = send_ffn_grads(dw_up_t, dw_down)
    if token is not None:
        sc_f = sc_f + token[0:1, 0:1]
    dx1, dpool, dattn, delta, dw_out_t, sums_mix = _mix_bwd(dh2, dout, x1, y1, cat, attn, w_out_t, sc_f, g_pre_ffn,
                                                           gt_m, g_post_mix, tm=256)
    du, dw_blk, sums_pool = _pool_bwd(dpool, u_pool, w_blk_b, b_pool_r, pool_scale_r, tm=512)
    dproj = _dproj_assemble(du, _attn_bwd(qkv, dattn, lse_all, delta), rope, tm=512)
    dw_in_t = _wgrad(dproj, h1, "wgrad_in", tk=2048, tmm=1280)
    token = send_mix_grads(dw_in_t, dw_out_t)
    if token is not None:
        sc_m = sc_m + token[0:1, 0:1]
    grad_x, sums_in = _inproj_bwd(dproj, w_in_t, xs, dx1, sc_m, g_pre_mix, tm=256)
    return (loss_loc, grad_x, dw_in_t, dw_out_t, dw_up_t, dw_down, dw_blk, dconv,
            sums_in, sums_mix, sums_ffn, sums_pool)


def kernel(x, c, positions, w_ada, b_ada, g_pre_mix, g_post_mix, g_pre_ffn, g_post_ffn, w_in, w_pool, b_pool, pool_scale, w_out, w_up, conv_w, conv_b, w_down, loss_target, m_w_ada, m_b_ada, m_g_pre_mix, m_g_post_mix, m_g_pre_ffn, m_g_post_ffn, m_w_in, m_w_pool, m_b_pool, m_pool_scale, m_w_out, m_w_up, m_conv_w, m_conv_b, m_w_down, v_w_ada, v_b_ada, v_g_pre_mix, v_g_post_mix, v_g_pre_ffn, v_g_post_ffn, v_w_in, v_w_pool, v_b_pool, v_pool_scale, v_w_out, v_w_up, v_conv_w, v_conv_b, v_w_down):
    s_len, d = x.shape[1], x.shape[2]
    d_ff = w_down.shape[1] * N_DEV
    me = _index(_place())
    xs, target = x[0], loss_target[0]

    ncol = w_ada.shape[2]
    b_cols = lax.dynamic_slice(b_ada, (0, me * ncol), (1, ncol))
    c_all, mod, taps_all, (w_in_t, w_out_t) = _entry_exchange(
        jnp.broadcast_to(c, (8, d)), w_ada[0], b_cols, _pad_rows(conv_w[0], 8),
        [w_in[0].T.astype(BF16), w_out[0].T.astype(BF16)])
    c_all = c_all[:, 0, :]
    conv_w_all = jnp.transpose(taps_all[:, :3, :], (1, 0, 2)).reshape(3, d_ff)
    sh_m, sc_m, gt_m, sh_f, sc_f, gt_f = [mod[:, 0, :].reshape(1, -1)[:, k * d:(k + 1) * d] for k in range(6)]

    rope = _rope_tables(positions[0])
    w_blk = jnp.zeros((256, 256), F32)
    for gi in range(4):
        w_blk = lax.dynamic_update_slice(w_blk, w_pool[0, gi], (gi * HEAD_DIM, gi * HEAD_DIM))
    w_blk_b = w_blk.astype(BF16)
    b_pool_r, pool_scale_r = b_pool.reshape(1, 256), pool_scale.reshape(1, 256)

    up_sh, down_sh = w_up[0].T.astype(BF16), w_down[0].astype(BF16)
    w_in_t, conv_w_all, up_sh, down_sh = lax.optimization_barrier((w_in_t, conv_w_all, up_sh, down_sh))
    lands = [_landing_zone("gather", s, me, "land_" + nm) for s, nm in ((up_sh, "w_up"), (down_sh, "w_down"))]
    w_send, w_recv, w_src, w_land, w_token = _exchange_start("gather_ici", [up_sh, down_sh], lands, "ffn_weights_ici_start")
    relay = []

    def relay_ffn(after):
        arrived = _exchange_wait("gather_ici", w_send, w_recv, w_src, w_land, after, "ffn_weights_ici_wait")
        relay.extend(_exchange_start("gather_d2d", [], arrived, "ffn_weights_d2d_start"))
        return relay[4]

    def fetch_ffn(after):
        return _exchange_wait("gather_d2d", relay[0], relay[1], [], relay[3], after, "ffn_weights_d2d_wait")

    flight = []

    def send_ffn_grads(dw_up_t, dw_down):
        lands = [_landing_zone("scatter", dw_up_t, me, "land_dw_up"), _landing_zone("scatter", dw_down, me, "land_dw_down")]
        flight.extend(_exchange_start("scatter", [dw_up_t, dw_down], lands, "ffn_grads_start"))
        return flight[4]

    mix_flight = []

    def send_mix_grads(dw_in_t, dw_out_t):
        lands = [_landing_zone("scatter", dw_in_t, me, "land_dw_in"), _landing_zone("scatter", dw_out_t, me, "land_dw_out")]
        mix_flight.extend(_exchange_start("scatter", [dw_in_t, dw_out_t], lands, "mix_grads_start"))
        return mix_flight[4]

    (loss_loc, grad_x, dw_in_t, dw_out_t, _, _, dw_pool, dconv,
     sums_in, sums_mix, sums_ffn, sums_pool) = _sequence_step(
        xs, target, rope, (sh_m + w_token[0:1, 0:1], sc_m, gt_m, sh_f, sc_f, gt_f),
        (g_pre_mix, g_post_mix, g_pre_ffn, g_post_ffn),
        w_in_t, w_out_t, relay_ffn, fetch_ffn, send_ffn_grads, send_mix_grads, w_blk_b, b_pool_r, pool_scale_r, conv_w_all, conv_b)

    small = [sums_in, sums_mix, sums_ffn, sums_pool, dw_pool, dconv, loss_loc]
    small_flight = _exchange_start("allgather", small, [lax.empty((N_DEV,) + a.shape, F32) for a in small], "small_start")

    parts_ffn = _exchange_wait("scatter", *flight[:4], small_flight[4], "ffn_grads_wait")
    big = {
        "w_up": [a.T for a in _sum_adam(parts_ffn[0], w_up[0].T, m_w_up[0].T, v_w_up[0].T, "adam_w_up", 64)],
        "w_down": _sum_adam(parts_ffn[1], w_down[0], m_w_down[0], v_w_down[0], "adam_w_down", 32),
    }

    rep_w = [b_ada, g_pre_mix, g_post_mix, g_pre_ffn, g_post_ffn, w_pool, b_pool, pool_scale, conv_b]
    rep_m = [m_b_ada, m_g_pre_mix, m_g_post_mix, m_g_pre_ffn, m_g_post_ffn, m_w_pool, m_b_pool, m_pool_scale, m_conv_b]
    rep_v = [v_b_ada, v_g_pre_mix, v_g_post_mix, v_g_pre_ffn, v_g_post_ffn, v_w_pool, v_b_pool, v_pool_scale, v_conv_b]
    parts_mix = _exchange_wait("scatter", *mix_flight[:4], big["w_down"][0], "mix_grads_wait")
    big["w_in"] = [a.T for a in _sum_adam(parts_mix[0], w_in[0].T, m_w_in[0].T, v_w_in[0].T, "adam_w_in", 64)]
    big["w_out"] = [a.T for a in _sum_adam(parts_mix[1], w_out[0].T, m_w_out[0].T, v_w_out[0].T, "adam_w_out", 128)]
    gathered = _exchange_wait("allgather", *small_flight[:4], big["w_out"][0], "small_wait")
    totals, dmod_all = _small_sum(small, gathered)
    dconv_tot, loss_tot = totals[5], totals[6]
    rep_out = _small_adam(totals, rep_w, rep_m, rep_v)
    g_rep, d_rep, nm_rep, nv_rep = (rep_out[k::4] for k in range(4))

    fcol = d_ff // N_DEV
    g_cw = lax.dynamic_slice(dconv_tot, (0, me * fcol), (3, fcol))
    d_cw, nm_cw, nv_cw = _adam(conv_w[0], g_cw, m_conv_w[0], v_conv_w[0], "adam_conv_w", 3)

    dmod_cols = lax.dynamic_slice(dmod_all, (0, me * ncol), (N_DEV, ncol))
    g_ada, d_ada, nm_ada, nv_ada = _ada_grad_adam(c_all, dmod_cols, w_ada[0], m_w_ada[0], v_w_ada[0], 256)

    loss = loss_tot[0, 0]

    def group(k):
        rep = (g_rep, d_rep, nm_rep, nv_rep)[k]
        ada = (g_ada, d_ada, nm_ada, nv_ada)[k][None]
        cw = (g_cw, d_cw, nm_cw, nv_cw)[k][None]
        return [ada, rep[0], rep[1], rep[2], rep[3], rep[4], big["w_in"][k][None], rep[5], rep[6], rep[7],
                big["w_out"][k][None], big["w_up"][k][None], cw, rep[8], big["w_down"][k][None]]

    return (loss, grad_x[None], *group(0), *group(1), *group(2), *group(3))
```

```python
import functools
import math

import jax
import jax.numpy as jnp
from jax import lax
from jax.experimental import pallas as pl
from jax.experimental.pallas import tpu as pltpu

F32 = jnp.float32
BF16 = jnp.bfloat16
MESH = pl.DeviceIdType.MESH

N_DEV = 8
HEAD_DIM = 64
ROT_HALF = 8
ROPE_THETA = 500000.0
POOL_WINDOWS = (2, 4, 8, 16)
DILATIONS = (1, 4, 16)
BLOCK = 128
NORM_EPS = 1e-6
HALO = 16
MASKED = -1e30
ATTN_FWD_UNROLL = 8
ATTN_BWD_UNROLL = 8

ADAM_LR = 0.001
ADAM_B1 = 0.9
ADAM_B2 = 0.999
ADAM_EPS = 1e-08
ADAM_WD = 0.01
ADAM_STEP = 10

V7X_VMEM_LIMIT = 56 * 1024 * 1024
LANES = 128

NT = (((1,), (1,)), ((), ()))
NN = (((1,), (0,)), ((), ()))
TN = (((0,), (0,)), ((), ()))


def _dot(a, b, dims):
    return lax.dot_general(a, b, dims, preferred_element_type=F32)


def _params(sem=None, vmem=V7X_VMEM_LIMIT):
    if sem is None:
        return pltpu.CompilerParams(vmem_limit_bytes=vmem)
    return pltpu.CompilerParams(dimension_semantics=sem, vmem_limit_bytes=vmem)


def _rstd(v):
    return lax.rsqrt(jnp.mean(v * v, axis=-1, keepdims=True) + NORM_EPS)


def _norm_bwd(dn, n, rstd):
    return rstd * (dn - n * jnp.mean(dn * n, axis=-1, keepdims=True))


def _rope_lanes(cs_ref, spread_ref):
    return [lax.dot_general(cs_ref[...], spread_ref[k], NN, preferred_element_type=F32, precision=lax.Precision.HIGHEST)
            for k in range(3)]


def _rope_fwd(p, lanes):
    return p * lanes[0] + pltpu.roll(p, LANES - ROT_HALF, 1) * lanes[1] + pltpu.roll(p, ROT_HALF, 1) * lanes[2]


def _rope_bwd(dp, lanes):
    return dp * lanes[0] + pltpu.roll(dp * lanes[1], ROT_HALF, 1) + pltpu.roll(dp * lanes[2], LANES - ROT_HALF, 1)


def _gelu_parts(v):
    k2 = 2.0 * math.sqrt(2.0 / math.pi)
    c = 0.044715
    v2 = v * v
    s = jax.nn.sigmoid(v * (k2 + (k2 * c) * v2))
    g = v * s
    dg = s + g * (1.0 - s) * (k2 + (3.0 * k2 * c) * v2)
    return g, dg


def _halo_before(i, tile):
    return jnp.maximum(i * (tile // HALO) - 1, 0)


def _premix_inproj(x, sh, sc, g, w_in_t, rope, tm):
    s_len, d = x.shape
    n_proj = w_in_t.shape[0]
    n_slab = (n_proj - 256) // LANES

    def body(x_ref, sh_ref, sc_ref, g_ref, w_ref, cs_ref, spread_ref, h_ref, up_ref, qkv_ref):
        xv = x_ref[...]
        h = (xv * _rstd(xv) * g_ref[...]) * (1.0 + sc_ref[...]) + sh_ref[...]
        hb = h.astype(BF16)
        h_ref[...] = hb
        up_ref[...] = _dot(hb, w_ref[0:256, :], NT)
        lanes = _rope_lanes(cs_ref, spread_ref)
        for pair in range(n_slab // 2):
            p = _dot(hb, w_ref[256 + 256 * pair:512 + 256 * pair, :], NT)
            for half in range(2):
                ph = p[:, half * LANES:(half + 1) * LANES]
                if pair < 6:
                    ph = _rope_fwd(ph, lanes)
                if pair < 3:
                    ph = ph * (HEAD_DIM ** -0.5)
                qkv_ref[2 * pair + half] = ph

    vec = pl.BlockSpec((1, d), lambda i: (0, 0))
    return pl.pallas_call(
        body, name="premix_inproj", grid=(s_len // tm,),
        in_specs=[pl.BlockSpec((tm, d), lambda i: (i, 0)), vec, vec, vec,
                  pl.BlockSpec((n_proj, d), lambda i: (0, 0)),
                  pl.BlockSpec((tm, rope[0].shape[1]), lambda i: (i, 0)), pl.BlockSpec(rope[1].shape, lambda i: (0, 0, 0))],
        out_specs=[pl.BlockSpec((tm, d), lambda i: (i, 0)),
                   pl.BlockSpec((tm, 256), lambda i: (i, 0)),
                   pl.BlockSpec((n_slab, tm, LANES), lambda i: (0, i, 0))],
        out_shape=[jax.ShapeDtypeStruct((s_len, d), BF16),
                   jax.ShapeDtypeStruct((s_len, 256), F32),
                   jax.ShapeDtypeStruct((n_slab, s_len, LANES), F32)],
        compiler_params=_params(("arbitrary",)),
    )(x, sh, sc, g, w_in_t, *rope)


def _block_rows(n, r, dil):
    start = n * (BLOCK * dil) + r
    if dil == 1:
        return pl.ds(pl.multiple_of(start, BLOCK), BLOCK)
    return pl.ds(start, BLOCK, stride=dil)


def _band_mask(n):
    ri = lax.broadcasted_iota(jnp.int32, (BLOCK, 2 * BLOCK), 0)
    cj = lax.broadcasted_iota(jnp.int32, (BLOCK, 2 * BLOCK), 1)
    cur = (cj >= BLOCK) & (cj - BLOCK <= ri)
    prev = (cj < BLOCK) & (cj >= ri) & (n > 0)
    return cur | prev


def _attn_fwd(qkv):
    s_len = qkv.shape[1]
    n_g = len(DILATIONS)

    def body(q_ref, k_ref, v_ref, o_ref, lse_ref):
        lane = lax.broadcasted_iota(jnp.int32, (BLOCK, LANES), 1)
        first = lane < HEAD_DIM

        def group(dil):
            nb = s_len // (BLOCK * dil)

            def block(t, carry):
                r, n = t // nb, t % nb
                cur = _block_rows(n, r, dil)
                prev = _block_rows(jnp.maximum(n - 1, 0), r, dil)
                q = q_ref[0, cur, :]
                kcat = jnp.concatenate([k_ref[0, prev, :], k_ref[0, cur, :]], axis=0).astype(BF16)
                vcat = jnp.concatenate([v_ref[0, prev, :], v_ref[0, cur, :]], axis=0).astype(BF16)
                valid = _band_mask(n)
                q2 = jnp.concatenate([jnp.where(first, q, 0.0), jnp.where(first, 0.0, q)], axis=0).astype(BF16)
                s = jnp.where(jnp.concatenate([valid, valid], axis=0), _dot(q2, kcat, NT), MASKED)
                m = jnp.max(s, axis=-1, keepdims=True)
                p = jnp.exp(s - m)
                den = jnp.sum(p, axis=-1, keepdims=True)
                o2 = _dot(p.astype(BF16), vcat, NN) / den
                lse2 = m + jnp.log(den)
                o_ref[0, 0, cur, :] = jnp.where(first, o2[:BLOCK], o2[BLOCK:])
                lse_ref[0, 0, cur, :] = jnp.where(first, lse2[:BLOCK], lse2[BLOCK:])
                return carry

            lax.fori_loop(0, nb * dil, block, 0, unroll=ATTN_FWD_UNROLL)

        for gi, dil in enumerate(DILATIONS):
            pl.when(pl.program_id(0) == gi)(functools.partial(group, dil))

    def slab(base):
        return pl.BlockSpec((1, s_len, LANES), lambda g, s: (base + 2 * g + s, 0, 0))

    out = pl.BlockSpec((1, 1, s_len, LANES), lambda g, s: (g, s, 0, 0))
    shape = jax.ShapeDtypeStruct((n_g, 2, s_len, LANES), F32)
    return pl.pallas_call(
        body, name="attn_fwd", grid=(n_g, 2),
        in_specs=[slab(0), slab(6), slab(12)], out_specs=[out, out], out_shape=[shape, shape],
        compiler_params=_params(("arbitrary", "arbitrary")),
    )(qkv, qkv, qkv)


def _pool_mixed(u, halo, i, tm):
    ue = jnp.concatenate([halo, u], axis=0)
    s2 = ue + pltpu.roll(ue, 1, 0)
    s4 = s2 + pltpu.roll(s2, 2, 0)
    s8 = s4 + pltpu.roll(s4, 4, 0)
    s16 = s8 + pltpu.roll(s8, 8, 0)
    grp = lax.broadcasted_iota(jnp.int32, (tm, 256), 1) // HEAD_DIM
    pick = lambda a, b, c, e: jnp.where(grp == 0, a, jnp.where(grp == 1, b, jnp.where(grp == 2, c, e)))
    win_sum = pick(s2[HALO:], s4[HALO:], s8[HALO:], s16[HALO:])
    pos = (i * tm + lax.broadcasted_iota(jnp.int32, (tm, 256), 0)).astype(F32)
    count = jnp.minimum(pos + 1.0, pick(*[float(w) for w in POOL_WINDOWS]))
    return win_sum / count - u, count


def _mix_out(x, u_pool, o_g, lse_g, w_blk, b_pool, pool_scale, w_out_t, gt_m, g_post_mix, g_pre_ffn, sc_f, sh_f, tm):
    s_len, d = x.shape

    def body(x_ref, u_ref, uh_ref, o_ref, l_ref, wb_ref, bp_ref, ps_ref, wo_ref,
             gt_ref, g1_ref, g2_ref, sc_ref, sh_ref,
             x1_ref, y1_ref, h2_ref, cat_ref, attn_ref, lall_ref):
        (o0, o1, o2), (l0, l1, l2) = (o_ref.at[g] for g in range(3)), (l_ref.at[g] for g in range(3))
        i = pl.program_id(0)
        u = u_ref[...]
        halo = uh_ref[...] * (i > 0).astype(F32)
        mixed, _ = _pool_mixed(u, halo, i, tm)
        y = _dot(mixed.astype(BF16), wb_ref[...], NN) + bp_ref[...]
        pool = y * ps_ref[...]
        attn = []
        for s in range(2):
            la, lb, lc = l0[s], l1[s], l2[s]
            mx = jnp.maximum(jnp.maximum(la, lb), lc)
            ea, eb, ec = jnp.exp(la - mx), jnp.exp(lb - mx), jnp.exp(lc - mx)
            den = ea + eb + ec
            lall_ref[s] = mx + jnp.log(den)
            attn.append((ea / den) * o0[s] + (eb / den) * o1[s] + (ec / den) * o2[s])
        attn = jnp.concatenate(attn, axis=1)
        attn_ref[...] = attn
        cat = jnp.concatenate([pool, attn], axis=1).astype(BF16)
        cat_ref[...] = cat
        y1 = _dot(cat, wo_ref[...], NT)
        y1_ref[...] = y1.astype(BF16)
        x1 = x_ref[...] + gt_ref[...] * (y1 * _rstd(y1) * g1_ref[...])
        x1_ref[...] = x1
        h2 = (x1 * _rstd(x1) * g2_ref[...]) * (1.0 + sc_ref[...]) + sh_ref[...]
        h2_ref[...] = h2.astype(BF16)

    tile = lambda w: pl.BlockSpec((tm, w), lambda i: (i, 0))
    slab = pl.BlockSpec((2, tm, LANES), lambda i: (0, i, 0))
    groups = pl.BlockSpec((len(DILATIONS), 2, tm, LANES), lambda i: (0, 0, i, 0))
    const = lambda a: pl.BlockSpec(a.shape, lambda i: (0,) * a.ndim)
    return pl.pallas_call(
        body, name="mix_out", grid=(s_len // tm,),
        in_specs=[tile(d), tile(256), pl.BlockSpec((HALO, 256), lambda i: (_halo_before(i, tm), 0)),
                  groups, groups,
                  const(w_blk), const(b_pool), const(pool_scale), const(w_out_t),
                  const(gt_m), const(g_post_mix), const(g_pre_ffn), const(sc_f), const(sh_f)],
        out_specs=[tile(d), tile(d), tile(d), tile(512), tile(256), slab],
        out_shape=[jax.ShapeDtypeStruct((s_len, d), F32), jax.ShapeDtypeStruct((s_len, d), BF16),
                   jax.ShapeDtypeStruct((s_len, d), BF16), jax.ShapeDtypeStruct((s_len, 512), BF16),
                   jax.ShapeDtypeStruct((s_len, 256), F32), jax.ShapeDtypeStruct((2, s_len, LANES), F32)],
        compiler_params=_params(("arbitrary",)),
    )(x, u_pool, u_pool, o_g, lse_g, w_blk, b_pool, pool_scale, w_out_t, gt_m, g_post_mix, g_pre_ffn, sc_f, sh_f)


def _conv_gate(gate_ext, cw, cb):
    gc = gate_ext * cw[2:3, :] + pltpu.roll(gate_ext, 1, 0) * cw[1:2, :] + pltpu.roll(gate_ext, 2, 0) * cw[0:1, :]
    return gc[HALO:] + cb


def _ffn_fwd_loss(h2, x1, target, w_up_t, w_down, conv_w, conv_b, gt_f, g_post_ffn, tm, ck):
    s_len, d = x1.shape
    d_ff = w_down.shape[0]
    n_t, n_c = s_len // tm, d_ff // ck

    def body(h_ref, hh_ref, x1_ref, tgt_ref, wg_ref, wv_ref, wd_ref, cw_ref, cb_ref, gt_ref, g_ref,
             gate_ref, a_ref, act_ref, vd_ref, dy2_ref, dout_ref, sums_ref, loss_ref, acc_ref):
        i = pl.program_id(0)

        @pl.when(i == 0)
        def _():
            sums_ref[...] = jnp.zeros_like(sums_ref)
            loss_ref[...] = jnp.zeros_like(loss_ref)
            acc_ref[...] = jnp.zeros_like(acc_ref)

        def finish(live):
            y2 = acc_ref[...]
            rstd = _rstd(y2)
            n = y2 * rstd
            rn = n * g_ref[...]
            err = x1_ref[...] + gt_ref[...] * rn - tgt_ref[...]
            keep = lambda v: jnp.where(live, v, 0.0)
            loss_ref[...] += keep(0.5 * jnp.sum(jnp.mean(err * err, axis=-1, keepdims=True), axis=0, keepdims=True))
            dout = err * (1.0 / d)
            dout_ref[...] = dout
            drn = dout * gt_ref[...]
            sums_ref[0:1, :] += keep(jnp.sum(dout * rn, axis=0, keepdims=True))
            sums_ref[1:2, :] += keep(jnp.sum(drn * n, axis=0, keepdims=True))
            dy2_ref[...] = _norm_bwd(drn * g_ref[...], n, rstd).astype(BF16)

        @pl.when(i < n_t)
        def _():
            h = h_ref[...]
            h_ext = jnp.concatenate([hh_ref[...], h], axis=0)
            row = lax.broadcasted_iota(jnp.int32, (tm + HALO, ck), 0)
            no_halo = (row < HALO) & (i == 0)

            def up(c):
                cs = slice(c * ck, (c + 1) * ck)
                return jnp.where(no_halo, 0.0, _dot(h_ext, wg_ref[cs, :], NT)), _dot(h, wv_ref[cs, :], NT)

            part = None
            nxt = up(0)
            finish(i > 0)
            for c in range(n_c):
                cs = slice(c * ck, (c + 1) * ck)
                gate_ext, val = nxt
                if c + 1 < n_c:
                    nxt = up(c + 1)
                act, dact = _gelu_parts(_conv_gate(gate_ext, cw_ref[:, cs], cb_ref[:, cs]))
                a = (act * val).astype(BF16)
                gate_ref[:, cs] = gate_ext[HALO:].astype(BF16)
                a_ref[:, cs] = a
                act_ref[:, cs] = act.astype(BF16)
                vd_ref[:, cs] = (val * dact).astype(BF16)
                p = _dot(a, wd_ref[cs, :], NN)
                part = p if part is None else part + p
            acc_ref[...] = part

        @pl.when(i == n_t)
        def _():
            finish(True)

    this = lambda i: jnp.minimum(i, n_t - 1)
    before = lambda i: jnp.maximum(i - 1, 0)
    tok = lambda w, at: pl.BlockSpec((tm, w), lambda i: (at(i), 0))
    vec = pl.BlockSpec((1, d), lambda i: (0, 0))
    once = lambda shape, imap: pl.BlockSpec(shape, imap, pipeline_mode=pl.Buffered(1))
    return pl.pallas_call(
        body, name="ffn_fwd_loss", grid=(n_t + 1,),
        in_specs=[tok(d, this), pl.BlockSpec((HALO, d), lambda i: (_halo_before(this(i), tm), 0)),
                  tok(d, before), tok(d, before),
                  once((d_ff, d), lambda i: (0, 0)), once((d_ff, d), lambda i: (1, 0)), once((d_ff, d), lambda i: (0, 0)),
                  pl.BlockSpec((3, d_ff), lambda i: (0, 0)), pl.BlockSpec((1, d_ff), lambda i: (0, 0)), vec, vec],
        out_specs=[tok(d_ff, this)] * 4 + [tok(d, before), tok(d, before), pl.BlockSpec((8, d), lambda i: (0, 0)),
                                          pl.BlockSpec((8, LANES), lambda i: (0, 0))],
        out_shape=[jax.ShapeDtypeStruct((s_len, d_ff), BF16)] * 4
        + [jax.ShapeDtypeStruct((s_len, d), BF16), jax.ShapeDtypeStruct((s_len, d), F32),
           jax.ShapeDtypeStruct((8, d), F32), jax.ShapeDtypeStruct((8, LANES), F32)],
        scratch_shapes=[pltpu.VMEM((tm, d), F32)],
        compiler_params=_params(("arbitrary",)),
    )(h2, h2, x1, target, w_up_t, w_up_t, w_down, conv_w, conv_b, gt_f, g_post_ffn)


def _ffn_bwd_act(dy2, gate, a, act, vd, w_down, tm, tf, ck):
    s_len, d = dy2.shape
    d_ff = w_down.shape[0]
    n_t = s_len // tm
    chunks = [slice(lo, min(lo + ck, tf)) for lo in range(0, tf, ck)]

    def body(dy_ref, g_ref, gh_ref, a_ref, act_ref, vd_ref, wd_ref, dgc_ref, dval_ref, dwd_ref, dconv_ref, acc_ref):
        i = pl.program_id(1)

        @pl.when(i == 0)
        def _():
            acc_ref[...] = jnp.zeros_like(acc_ref)
            dconv_ref[...] = jnp.zeros_like(dconv_ref)

        dy = dy_ref[...]

        def down(cs):
            return _dot(dy, wd_ref[cs, :], NT)

        nxt = down(chunks[0])
        for c, cs in enumerate(chunks):
            width = cs.stop - cs.start
            da = nxt
            if c + 1 < len(chunks):
                nxt = down(chunks[c + 1])
            acc_ref[cs, :] += _dot(a_ref[:, cs], dy, TN)
            row = lax.broadcasted_iota(jnp.int32, (tm + HALO, width), 0)
            gate_ext = jnp.where((row < HALO) & (i == 0), 0.0,
                                 jnp.concatenate([gh_ref[:, cs], g_ref[:, cs]], axis=0).astype(F32))
            dgc = da * vd_ref[:, cs].astype(F32)
            dgc_ref[:, cs] = dgc.astype(BF16)
            dval_ref[:, cs] = (da * act_ref[:, cs].astype(F32)).astype(BF16)
            rows = [jnp.sum(dgc * pltpu.roll(gate_ext, 2 - k, 0)[HALO:], axis=0, keepdims=True) for k in range(2)]
            rows += [jnp.sum(dgc * gate_ext[HALO:], axis=0, keepdims=True), jnp.sum(dgc, axis=0, keepdims=True),
                     jnp.zeros((4, width), F32)]
            dconv_ref[:, cs] += jnp.concatenate(rows, axis=0)

        @pl.when(i == n_t - 1)
        def _():
            dwd_ref[...] = acc_ref[...].astype(BF16)

    tokf = pl.BlockSpec((tm, tf), lambda j, i: (i, j))
    return pl.pallas_call(
        body, name="ffn_bwd_act", grid=(d_ff // tf, n_t),
        in_specs=[pl.BlockSpec((tm, d), lambda j, i: (i, 0)), tokf,
                  pl.BlockSpec((HALO, tf), lambda j, i: (_halo_before(i, tm), j)), tokf, tokf, tokf,
                  pl.BlockSpec((tf, d), lambda j, i: (j, 0))],
        out_specs=[tokf, tokf, pl.BlockSpec((tf, d), lambda j, i: (j, 0)), pl.BlockSpec((8, tf), lambda j, i: (0, j))],
        out_shape=[jax.ShapeDtypeStruct((s_len, d_ff), BF16), jax.ShapeDtypeStruct((s_len, d_ff), BF16),
                   jax.ShapeDtypeStruct((d_ff, d), BF16), jax.ShapeDtypeStruct((8, d_ff), F32)],
        scratch_shapes=[pltpu.VMEM((tf, d), F32)],
        compiler_params=_params(("arbitrary", "arbitrary")),
    )(dy2, gate, gate, a, act, vd, w_down)


def _ffn_bwd_up(dgc, dval, w_up_t, conv_w, tm):
    s_len, d_ff = dgc.shape
    d = w_up_t.shape[1]
    n_t = s_len // tm

    def body(dg_ref, dgn_ref, dv_ref, cw_ref, w_ref, dup_ref, dh_ref):
        i = pl.program_id(0)
        nxt = dgn_ref[...].astype(F32) * (i < n_t - 1).astype(F32)
        ext = jnp.concatenate([dg_ref[...].astype(F32), nxt], axis=0)
        rows = tm + HALO
        dgate = (ext * cw_ref[2:3, :] + pltpu.roll(ext, rows - 1, 0) * cw_ref[1:2, :]
                 + pltpu.roll(ext, rows - 2, 0) * cw_ref[0:1, :])[:tm]
        dup = jnp.concatenate([dgate.astype(BF16), dv_ref[...]], axis=1)
        dup_ref[...] = dup
        dh_ref[...] = _dot(dup, w_ref[...], NN).astype(BF16)

    tokf = pl.BlockSpec((tm, d_ff), lambda i: (i, 0))
    return pl.pallas_call(
        body, name="ffn_bwd_up", grid=(n_t,),
        in_specs=[tokf, pl.BlockSpec((HALO, d_ff), lambda i: (jnp.minimum((i + 1) * (tm // HALO), s_len // HALO - 1), 0)),
                  tokf, pl.BlockSpec((3, d_ff), lambda i: (0, 0)), pl.BlockSpec((2 * d_ff, d), lambda i: (0, 0))],
        out_specs=[pl.BlockSpec((tm, 2 * d_ff), lambda i: (i, 0)), pl.BlockSpec((tm, d), lambda i: (i, 0))],
        out_shape=[jax.ShapeDtypeStruct((s_len, 2 * d_ff), BF16), jax.ShapeDtypeStruct((s_len, d), BF16)],
        compiler_params=_params(("arbitrary",)),
    )(dgc, dgc, dval, conv_w, w_up_t)


def _mix_bwd(dh2, dout, x1, y1, cat, attn, w_out_t, sc_f, g_pre_ffn, gt_m, g_post_mix, tm):
    s_len, d = x1.shape
    n_t = s_len // tm

    def body(dh_ref, do_ref, x1_ref, y1_ref, cat_ref, at_ref, wo_ref, sc_ref, g2_ref, gt_ref, g1_ref,
             dx1_ref, dpool_ref, dattn_ref, delta_ref, dwo_ref, sums_ref, acc_ref):
        i = pl.program_id(0)
        dh = dh_ref[...].astype(F32)
        x1 = x1_ref[...]
        r2 = _rstd(x1)
        n2 = x1 * r2
        ng = n2 * g2_ref[...]
        dng = dh * (1.0 + sc_ref[...])
        dx1 = do_ref[...] + _norm_bwd(dng * g2_ref[...], n2, r2)
        dx1_ref[...] = dx1
        y1 = y1_ref[...].astype(F32)
        r1 = _rstd(y1)
        n1 = y1 * r1
        drn = dx1 * gt_ref[...]
        dy1 = _norm_bwd(drn * g1_ref[...], n1, r1).astype(BF16)
        dcat = _dot(dy1, wo_ref[...], NN)
        dpool_ref[...] = dcat[:, 0:256]
        lane = lax.broadcasted_iota(jnp.int32, (tm, LANES), 1)
        first = lane < HEAD_DIM
        for s in range(2):
            da = dcat[:, 256 + s * LANES:256 + (s + 1) * LANES]
            dattn_ref[s] = da
            prod = da * at_ref[:, s * LANES:(s + 1) * LANES]
            tot = jnp.sum(prod, axis=-1, keepdims=True)
            lo = jnp.sum(jnp.where(first, prod, 0.0), axis=-1, keepdims=True)
            delta_ref[s] = jnp.where(first, lo, tot - lo)
        dwo = _dot(dy1, cat_ref[...], TN)
        sums = jnp.concatenate(
            [jnp.sum(dh, axis=0, keepdims=True), jnp.sum(dh * ng, axis=0, keepdims=True),
             jnp.sum(dng * n2, axis=0, keepdims=True), jnp.sum(dx1 * (n1 * g1_ref[...]), axis=0, keepdims=True),
             jnp.sum(drn * n1, axis=0, keepdims=True), jnp.zeros((3, d), F32)], axis=0)

        @pl.when(i == 0)
        def _():
            acc_ref[...] = dwo
            sums_ref[...] = sums

        @pl.when(i > 0)
        def _():
            acc_ref[...] += dwo
            sums_ref[...] += sums

        @pl.when(i == n_t - 1)
        def _():
            dwo_ref[...] = acc_ref[...].astype(BF16)

    tile = lambda w: pl.BlockSpec((tm, w), lambda i: (i, 0))
    slab = pl.BlockSpec((2, tm, LANES), lambda i: (0, i, 0))
    vec = pl.BlockSpec((1, d), lambda i: (0, 0))
    return pl.pallas_call(
        body, name="mix_bwd", grid=(n_t,),
        in_specs=[tile(d), tile(d), tile(d), tile(d), tile(512), tile(256),
                  pl.BlockSpec((d, 512), lambda i: (0, 0)), vec, vec, vec, vec],
        out_specs=[tile(d), tile(256), slab, slab, pl.BlockSpec((d, 512), lambda i: (0, 0)),
                   pl.BlockSpec((8, d), lambda i: (0, 0))],
        out_shape=[jax.ShapeDtypeStruct((s_len, d), F32), jax.ShapeDtypeStruct((s_len, 256), F32),
                   jax.ShapeDtypeStruct((2, s_len, LANES), F32), jax.ShapeDtypeStruct((2, s_len, LANES), F32),
                   jax.ShapeDtypeStruct((d, 512), BF16), jax.ShapeDtypeStruct((8, d), F32)],
        scratch_shapes=[pltpu.VMEM((d, 512), F32)],
        compiler_params=_params(("arbitrary",)),
    )(dh2, dout, x1, y1, cat, attn, w_out_t, sc_f, g_pre_ffn, gt_m, g_post_mix)


def _pool_bwd(dpool, u_pool, w_blk, b_pool, pool_scale, tm):
    s_len = dpool.shape[0]
    n_t = s_len // tm

    def body(dp_ref, dpn_ref, u_ref, uh_ref, wb_ref, bp_ref, ps_ref, du_ref, dwp_ref, sums_ref, acc_ref):
        i = pl.program_id(0)
        u = u_ref[...]
        mixed, _ = _pool_mixed(u, uh_ref[...] * (i > 0).astype(F32), i, tm)
        mixed_b = mixed.astype(BF16)
        y = _dot(mixed_b, wb_ref[...], NN) + bp_ref[...]
        dp = dp_ref[...]
        dy = dp * ps_ref[...]
        dwb = _dot(mixed_b, dy.astype(BF16), TN)
        sums = jnp.concatenate([jnp.sum(dy, axis=0, keepdims=True), jnp.sum(dp * y, axis=0, keepdims=True),
                                jnp.zeros((6, 256), F32)], axis=0)
        dp_ext = jnp.concatenate([dp, dpn_ref[...] * (i < n_t - 1).astype(F32)], axis=0)
        dmix = _dot((dp_ext * ps_ref[...]).astype(BF16), wb_ref[...], NT)
        rows = tm + HALO
        grp = lax.broadcasted_iota(jnp.int32, (rows, 256), 1) // HEAD_DIM
        pick = lambda a, b, c, e: jnp.where(grp == 0, a, jnp.where(grp == 1, b, jnp.where(grp == 2, c, e)))
        pos = (i * tm + lax.broadcasted_iota(jnp.int32, (rows, 256), 0)).astype(F32)
        z = dmix / jnp.minimum(pos + 1.0, pick(*[float(w) for w in POOL_WINDOWS]))
        f2 = z + pltpu.roll(z, rows - 1, 0)
        f4 = f2 + pltpu.roll(f2, rows - 2, 0)
        f8 = f4 + pltpu.roll(f4, rows - 4, 0)
        f16 = f8 + pltpu.roll(f8, rows - 8, 0)
        du_ref[...] = (pick(f2, f4, f8, f16) - dmix)[:tm]

        @pl.when(i == 0)
        def _():
            acc_ref[...] = dwb
            sums_ref[...] = sums

        @pl.when(i > 0)
        def _():
            acc_ref[...] += dwb
            sums_ref[...] += sums

        @pl.when(i == n_t - 1)
        def _():
            full = acc_ref[...]
            for gi in range(len(POOL_WINDOWS)):
                lo = gi * HEAD_DIM
                dwp_ref[gi] = full[lo:lo + HEAD_DIM, lo:lo + HEAD_DIM]

    n_g = len(POOL_WINDOWS)
    tile = pl.BlockSpec((tm, 256), lambda i: (i, 0))
    const = lambda a: pl.BlockSpec(a.shape, lambda i: (0,) * a.ndim)
    return pl.pallas_call(
        body, name="pool_bwd", grid=(n_t,),
        in_specs=[tile, pl.BlockSpec((HALO, 256), lambda i: (jnp.minimum((i + 1) * (tm // HALO), s_len // HALO - 1), 0)),
                  tile, pl.BlockSpec((HALO, 256), lambda i: (_halo_before(i, tm), 0)),
                  const(w_blk), const(b_pool), const(pool_scale)],
        out_specs=[tile, pl.BlockSpec((n_g, HEAD_DIM, HEAD_DIM), lambda i: (0, 0, 0)), pl.BlockSpec((8, 256), lambda i: (0, 0))],
        out_shape=[jax.ShapeDtypeStruct((s_len, 256), F32), jax.ShapeDtypeStruct((n_g, HEAD_DIM, HEAD_DIM), F32),
                   jax.ShapeDtypeStruct((8, 256), F32)],
        scratch_shapes=[pltpu.VMEM((256, 256), F32)],
        compiler_params=_params(("arbitrary",)),
    )(dpool, dpool, u_pool, u_pool, w_blk, b_pool, pool_scale)


def _attn_bwd(qkv, dattn, lse_all, delta):
    s_len = qkv.shape[1]
    n_g = len(DILATIONS)

    def body(q_ref, k_ref, v_ref, do_ref, l_ref, dl_ref, dq_ref, dk_ref, dv_ref):
        lane = lax.broadcasted_iota(jnp.int32, (BLOCK, LANES), 1)
        first = lane < HEAD_DIM

        def group(dil):
            nb = s_len // (BLOCK * dil)

            def block(t, carry):
                dk_part, dv_part = carry
                r, n = t // nb, t % nb
                cur = _block_rows(n, r, dil)
                prev = _block_rows(jnp.maximum(n - 1, 0), r, dil)
                q = q_ref[0, cur, :]
                do = do_ref[0, cur, :]
                lse = l_ref[0, cur, :]
                dlt = dl_ref[0, cur, :]
                kcat = jnp.concatenate([k_ref[0, prev, :], k_ref[0, cur, :]], axis=0).astype(BF16)
                vcat = jnp.concatenate([v_ref[0, prev, :], v_ref[0, cur, :]], axis=0).astype(BF16)
                valid = _band_mask(n)
                stack = lambda a: jnp.concatenate([jnp.where(first, a, 0.0), jnp.where(first, 0.0, a)], axis=0)
                rows2 = lambda a: jnp.concatenate([a[:, 0:1], a[:, HEAD_DIM:HEAD_DIM + 1]], axis=0)
                q2, do2 = stack(q).astype(BF16), stack(do).astype(BF16)
                valid2 = jnp.concatenate([valid, valid], axis=0)
                p = jnp.where(valid2, jnp.exp(_dot(q2, kcat, NT) - rows2(lse)), 0.0)
                ds = (p * (_dot(do2, vcat, NT) - rows2(dlt))).astype(BF16)
                dq2 = _dot(ds, kcat, NN)
                dq_ref[0, 0, cur, :] = jnp.where(first, dq2[:BLOCK], dq2[BLOCK:])
                dkc = _dot(ds, q2, TN)
                dvc = _dot(p.astype(BF16), do2, TN)
                dk_ref[0, 0, prev, :] = dk_part + dkc[:BLOCK]
                dv_ref[0, 0, prev, :] = dv_part + dvc[:BLOCK]
                dk_ref[0, 0, cur, :] = dkc[BLOCK:]
                dv_ref[0, 0, cur, :] = dvc[BLOCK:]
                return dkc[BLOCK:], dvc[BLOCK:]

            def blocks(tt, carry):
                for u in range(ATTN_BWD_UNROLL):
                    carry = block(tt * ATTN_BWD_UNROLL + u, carry)
                return carry

            zero = jnp.zeros((BLOCK, LANES), F32)
            lax.fori_loop(0, nb * dil // ATTN_BWD_UNROLL, blocks, (zero, zero))

        for gi, dil in enumerate(DILATIONS):
            pl.when(pl.program_id(1) == gi)(functools.partial(group, dil))

    def slab(base):
        return pl.BlockSpec((1, s_len, LANES), lambda s, g: (base + 2 * g + s, 0, 0))

    one = pl.BlockSpec((1, s_len, LANES), lambda s, g: (s, 0, 0))
    out = pl.BlockSpec((1, 1, s_len, LANES), lambda s, g: (g, s, 0, 0))
    shape = jax.ShapeDtypeStruct((n_g, 2, s_len, LANES), F32)
    return pl.pallas_call(
        body, name="attn_bwd", grid=(2, n_g),
        in_specs=[slab(0), slab(6), slab(12), one, one, one],
        out_specs=[out, out, out], out_shape=[shape, shape, shape],
        compiler_params=_params(("arbitrary", "arbitrary")),
    )(qkv, qkv, qkv, dattn, lse_all, delta)


def _dproj_assemble(du, dqkv, rope, tm):
    s_len = du.shape[0]
    n_proj = 256 + 18 * LANES

    def body(du_ref, dq_ref, dk_ref, dv_ref, cs_ref, spread_ref, dproj_ref):
        dproj_ref[:, 0:256] = du_ref[...].astype(BF16)
        lanes = _rope_lanes(cs_ref, spread_ref)
        col = 256
        for kind, dref in enumerate((dq_ref, dk_ref, dv_ref)):
            for grp in range(3):
                for s in range(2):
                    piece = dref[grp, s]
                    if kind < 2:
                        piece = _rope_bwd(piece, lanes)
                    if kind == 0:
                        piece = piece * (HEAD_DIM ** -0.5)
                    dproj_ref[:, col:col + LANES] = piece.astype(BF16)
                    col += LANES

    groups = pl.BlockSpec((len(DILATIONS), 2, tm, LANES), lambda i: (0, 0, i, 0))
    return pl.pallas_call(
        body, name="dproj_assemble", grid=(s_len // tm,),
        in_specs=[pl.BlockSpec((tm, 256), lambda i: (i, 0))] + [groups] * 3
        + [pl.BlockSpec((tm, rope[0].shape[1]), lambda i: (i, 0)), pl.BlockSpec(rope[1].shape, lambda i: (0, 0, 0))],
        out_specs=pl.BlockSpec((tm, n_proj), lambda i: (i, 0)),
        out_shape=jax.ShapeDtypeStruct((s_len, n_proj), BF16),
        compiler_params=_params(("arbitrary",)),
    )(du, *dqkv, *rope)


def _inproj_bwd(dproj, w_in_t, x, dx1, sc_m, g_pre_mix, tm):
    s_len, d = x.shape
    n_proj = w_in_t.shape[0]
    n_t = s_len // tm

    def body(dproj_ref, w_ref, x_ref, dx1_ref, sc_ref, g_ref, dx_ref, sums_ref):
        i = pl.program_id(0)
        halves = [slice(0, tm // 2), slice(tm // 2, tm)]
        dhs = [_dot(dproj_ref[rs, :], w_ref[...], NN) for rs in halves]
        sums = None
        for rs, dh in zip(halves, dhs):
            xv = x_ref[rs, :]
            r = _rstd(xv)
            n = xv * r
            dng = dh * (1.0 + sc_ref[...])
            dx_ref[rs, :] = dx1_ref[rs, :] + _norm_bwd(dng * g_ref[...], n, r)
            part = jnp.concatenate([jnp.sum(dh, axis=0, keepdims=True), jnp.sum(dh * (n * g_ref[...]), axis=0, keepdims=True),
                                    jnp.sum(dng * n, axis=0, keepdims=True), jnp.zeros((5, d), F32)], axis=0)
            sums = part if sums is None else sums + part

        @pl.when(i == 0)
        def _():
            sums_ref[...] = sums

        @pl.when(i > 0)
        def _():
            sums_ref[...] += sums

    tile = lambda w: pl.BlockSpec((tm, w), lambda i: (i, 0))
    vec = pl.BlockSpec((1, d), lambda i: (0, 0))
    return pl.pallas_call(
        body, name="inproj_bwd", grid=(n_t,),
        in_specs=[tile(n_proj), pl.BlockSpec((n_proj, d), lambda i: (0, 0)), tile(d), tile(d), vec, vec],
        out_specs=[tile(d), pl.BlockSpec((8, d), lambda i: (0, 0))],
        out_shape=[jax.ShapeDtypeStruct((s_len, d), F32), jax.ShapeDtypeStruct((8, d), F32)],
        compiler_params=_params(("arbitrary",)),
    )(dproj, w_in_t, x, dx1, sc_m, g_pre_mix)


def _wgrad(a, b, name, tk, tmm):
    s_len, m = a.shape
    n = b.shape[1]
    n_k = s_len // tk

    def body(a_ref, b_ref, o_ref, acc_ref):
        k = pl.program_id(1)
        part = _dot(a_ref[...], b_ref[...], TN)

        @pl.when(k == 0)
        def _():
            acc_ref[...] = part

        @pl.when(k > 0)
        def _():
            acc_ref[...] += part

        @pl.when(k == n_k - 1)
        def _():
            o_ref[...] = acc_ref[...].astype(BF16)

    return pl.pallas_call(
        body, name=name, grid=(m // tmm, n_k),
        in_specs=[pl.BlockSpec((tk, tmm), lambda j, k: (k, j)), pl.BlockSpec((tk, n), lambda j, k: (k, 0))],
        out_specs=pl.BlockSpec((tmm, n), lambda j, k: (j, 0)),
        out_shape=jax.ShapeDtypeStruct((m, n), BF16),
        scratch_shapes=[pltpu.VMEM((tmm, n), F32)],
        compiler_params=_params(("arbitrary", "arbitrary")),
    )(a, b)


def _place():
    return lax.axis_index("x"), lax.axis_index("y"), lax.axis_index("c")


def _peer(k):
    x, y, c = _place()
    bx, by, bc = (k >> 2) & 1, (k >> 1) & 1, k & 1
    return (x ^ bx if bx else x, y ^ by if by else y, c ^ bc if bc else c)


def _index(pos):
    return 4 * pos[0] + 2 * pos[1] + pos[2]


def _entry_exchange(c_rows, w_ada, b_ada_cols, taps, shards):
    d = c_rows.shape[1]
    ncol = w_ada.shape[1]
    n_w = len(shards)

    def body(c_ref, w_ref, b_ref, t_ref, *rest):
        srcs = rest[:n_w]
        call_ref, mod_ref, tall_ref = rest[n_w:n_w + 3]
        outs = rest[n_w + 3:2 * n_w + 3]
        stage_ref, s_send, s_recv, w_send, w_recv, local_sems = rest[2 * n_w + 3:]
        x, y, c = _place()
        here, sibling = (x, y, c), (x, y, 1 - c)
        chips = [(1 - x, y), (x, 1 - y), (1 - x, 1 - y)]
        me = _index(here)

        def small(kind, src, dst, k):
            return pltpu.make_async_remote_copy(src_ref=src, dst_ref=dst, send_sem=s_send.at[kind, k - 1],
                                                recv_sem=s_recv.at[kind, k - 1], device_id=_peer(k), device_id_type=MESH)

        gather = lambda k: small(0, c_ref, call_ref.at[me], k)
        scatter = lambda k: small(1, stage_ref.at[_index(_peer(k))], mod_ref.at[me], k)
        gather_taps = lambda k: small(2, t_ref, tall_ref.at[me], k)

        def rows(w, pos):
            r = shards[w].shape[0]
            return outs[w].at[pl.ds(pl.multiple_of(_index(pos) * r, 16), r), :]

        def block(k, w, pos, to, own=False):
            return pltpu.make_async_remote_copy(
                src_ref=srcs[w] if own else rows(w, pos), dst_ref=rows(w, pos),
                send_sem=w_send.at[k, w], recv_sem=w_recv.at[k, w], device_id=to, device_id_type=MESH)

        call_ref[me] = c_ref[...]
        tall_ref[me] = t_ref[...]
        for k in range(1, N_DEV):
            gather(k).start()
        for k in range(1, N_DEV):
            gather_taps(k).start()
        mine = [pltpu.make_async_copy(srcs[w], rows(w, here), local_sems.at[w]) for w in range(n_w)]
        for cp in mine:
            cp.start()
        first = [block(0, w, here, sibling, own=True) for w in range(n_w)]
        first += [block(1 + j, w, here, (*chip, c), own=True) for j, chip in enumerate(chips) for w in range(n_w)]
        for cp in first:
            cp.start()

        for k in range(1, N_DEV):
            gather(k).wait_recv()
        cv = jnp.concatenate([call_ref[b, 0:1, :] for b in range(N_DEV)], axis=0)
        act = cv * jax.nn.sigmoid(cv)
        mod = lax.dot_general(act, w_ref[...], NN, preferred_element_type=F32,
                              precision=lax.Precision.HIGHEST) + b_ref[...]
        for b in range(N_DEV):
            stage_ref[b] = jnp.broadcast_to(mod[b:b + 1, :], (8, ncol))
        mod_ref[me] = stage_ref[me]
        for k in range(1, N_DEV):
            scatter(k).start()

        passed = []
        for j, chip in enumerate(chips):
            for w in range(n_w):
                block(1 + j, w, (*chip, c), here).wait_recv()
                fwd = block(4 + j, w, (*chip, c), sibling)
                fwd.start()
                passed.append(fwd)
        for w in range(n_w):
            block(0, w, sibling, here).wait_recv()
        for j, chip in enumerate(chips):
            for w in range(n_w):
                block(4 + j, w, (*chip, 1 - c), here).wait_recv()
        for k in range(1, N_DEV):
            scatter(k).wait_recv()
            gather_taps(k).wait_recv()
        for cp in first + passed:
            cp.wait_send()
        for k in range(1, N_DEV):
            gather(k).wait_send()
            scatter(k).wait_send()
            gather_taps(k).wait_send()
        for cp in mine:
            cp.wait()

    vmem, hbm = pl.BlockSpec(memory_space=pltpu.VMEM), pl.BlockSpec(memory_space=pltpu.HBM)
    out = pl.pallas_call(
        body, name="entry_exchange",
        in_specs=[vmem] * 4 + [hbm] * n_w, out_specs=[vmem] * 3 + [hbm] * n_w,
        out_shape=[jax.ShapeDtypeStruct((N_DEV, 8, d), F32), jax.ShapeDtypeStruct((N_DEV, 8, ncol), F32),
                   jax.ShapeDtypeStruct((N_DEV,) + taps.shape, F32)]
        + [jax.ShapeDtypeStruct((N_DEV * s.shape[0], s.shape[1]), s.dtype) for s in shards],
        scratch_shapes=[pltpu.VMEM((N_DEV, 8, ncol), F32), pltpu.SemaphoreType.DMA((3, N_DEV - 1)),
                        pltpu.SemaphoreType.DMA((3, N_DEV - 1)), pltpu.SemaphoreType.DMA((N_DEV - 1, n_w)),
                        pltpu.SemaphoreType.DMA((N_DEV - 1, n_w)), pltpu.SemaphoreType.DMA((n_w,))],
        compiler_params=_params(),
    )(c_rows, w_ada, b_ada_cols, taps, *shards)
    return out[0], out[1], out[2], out[3:]


def _peer_copies(mode, srcs, lands, send_sems, recv_sems):
    if mode in ("gather_ici", "gather_d2d"):
        x, y, c = _place()
        sibling = (x, y, 1 - c)
        chips = [(1 - x, y), (x, 1 - y), (1 - x, 1 - y)]
        n = len(lands)

        def rows(w, pos):
            r = lands[w].shape[0] // N_DEV
            return lands[w].at[pl.ds(pl.multiple_of(_index(pos) * r, 16), r), :]

        def copy(k, w, src, dst, to):
            return pltpu.make_async_remote_copy(src_ref=src, dst_ref=dst, send_sem=send_sems.at[k * n + w],
                                                recv_sem=recv_sems.at[k * n + w], device_id=to, device_id_type=MESH)

        if mode == "gather_ici":
            targets = [sibling] + [(*chip, c) for chip in chips]
            return [copy(k, w, srcs[w], rows(w, (x, y, c)), to) for k, to in enumerate(targets) for w in range(n)]
        return [copy(j, w, rows(w, (*chip, c)), rows(w, (*chip, c)), sibling)
                for j, chip in enumerate(chips) for w in range(n)]
    me = _index(_place())
    copies = []
    for k in range(1, N_DEV):
        peer = _peer(k)
        for w, (src, land) in enumerate(zip(srcs, lands)):
            if mode == "gather":
                r = src.shape[0]
                dst = land.at[pl.ds(pl.multiple_of(me * r, 16), r), :]
            elif mode == "allgather":
                dst = land.at[me]
            else:
                r = src.shape[0] // N_DEV
                src = src.at[pl.ds(pl.multiple_of(_index(peer) * r, 16), r), :]
                dst = land.at[me]
            copies.append(pltpu.make_async_remote_copy(
                src_ref=src, dst_ref=dst, send_sem=send_sems.at[(k - 1) * len(srcs) + w],
                recv_sem=recv_sems.at[(k - 1) * len(srcs) + w],
                device_id=peer, device_id_type=MESH))
    return copies


def _landing_zone(mode, src, me, name):
    cols = src.shape[1]
    if mode == "gather":
        r = src.shape[0]
        in_spec = pl.BlockSpec((r, cols), lambda i, me_ref: (0, 0))
        out_spec = pl.BlockSpec((r, cols), lambda i, me_ref: (me_ref[0], 0))
        out_shape = jax.ShapeDtypeStruct((N_DEV * r, cols), src.dtype)
    else:
        r = src.shape[0] // N_DEV
        in_spec = pl.BlockSpec((r, cols), lambda i, me_ref: (me_ref[0], 0))
        out_spec = pl.BlockSpec((1, r, cols), lambda i, me_ref: (me_ref[0], 0, 0))
        out_shape = jax.ShapeDtypeStruct((N_DEV, r, cols), src.dtype)

    def body(me_ref, s_ref, o_ref):
        o_ref[...] = s_ref[...].reshape(o_ref.shape)

    return pl.pallas_call(
        body, name=name, out_shape=out_shape,
        grid_spec=pltpu.PrefetchScalarGridSpec(num_scalar_prefetch=1, grid=(1,), in_specs=[in_spec], out_specs=out_spec),
        compiler_params=_params(("arbitrary",)),
    )(me.reshape(1).astype(jnp.int32), src)


def _exchange_start(mode, srcs, lands, name):
    n_s, n_a = len(srcs), len(srcs) + len(lands)
    n_cp = _COPIES_PER_ARRAY.get(mode, N_DEV - 1) * len(lands)

    def body(*refs):
        for cp in _peer_copies(mode, refs[:n_s], refs[n_s:n_a], refs[n_a], refs[n_a + 1]):
            cp.start()
        refs[-1][...] = jnp.zeros_like(refs[-1])

    hbm, sem = pl.BlockSpec(memory_space=pltpu.HBM), pl.BlockSpec(memory_space=pltpu.SEMAPHORE)
    arrays = list(srcs) + list(lands)
    out = pl.pallas_call(
        body, name=name,
        out_shape=(pltpu.SemaphoreType.DMA((n_cp,)), pltpu.SemaphoreType.DMA((n_cp,)),
                   *[pltpu.HBM(a.shape, a.dtype) for a in arrays], jax.ShapeDtypeStruct((8, LANES), F32)),
        in_specs=[hbm] * n_a, out_specs=(sem, sem, *[hbm] * n_a, pl.BlockSpec(memory_space=pltpu.VMEM)),
        input_output_aliases={i: 2 + i for i in range(n_a)},
        compiler_params=pltpu.CompilerParams(has_side_effects=pltpu.SideEffectType.DATAFLOW_SIDE_EFFECTING),
    )(*[pltpu.with_memory_space_constraint(a, pltpu.HBM) for a in arrays])
    return out[0], out[1], out[2:2 + n_s], out[2 + n_s:2 + n_a], out[-1]


_COPIES_PER_ARRAY = {"gather_ici": 4, "gather_d2d": 3}


def _exchange_wait(mode, send_sems, recv_sems, srcs, lands, after, name):
    n_s, n_a = len(srcs), len(srcs) + len(lands)

    def body(*refs):
        copies = _peer_copies(mode, refs[:n_s], refs[n_s:n_a], refs[n_a], refs[n_a + 1])
        for cp in copies:
            cp.wait_send()
        for cp in copies:
            cp.wait_recv()

    hbm, sem = pl.BlockSpec(memory_space=pltpu.HBM), pl.BlockSpec(memory_space=pltpu.SEMAPHORE)
    arrays = list(srcs) + list(lands)
    out = pl.pallas_call(
        body, name=name, out_shape=tuple(pltpu.HBM(a.shape, a.dtype) for a in arrays),
        in_specs=[hbm] * n_a + [sem, sem, pl.BlockSpec(memory_space=pl.ANY)], out_specs=tuple([hbm] * n_a),
        input_output_aliases={i: i for i in range(n_a)},
        compiler_params=pltpu.CompilerParams(has_side_effects=pltpu.SideEffectType.DATAFLOW_SIDE_EFFECTING),
    )(*arrays, send_sems, recv_sems, after)
    return out[n_s:]


SMALL_WEIGHTS = ("b_ada", "g_pre_mix", "g_post_mix", "g_pre_ffn", "g_post_ffn", "w_pool", "b_pool", "pool_scale", "conv_b")


MOD_ROWS = ((0, 0), (0, 1), (1, 3), (1, 0), (1, 1), (2, 0))


def _small_sum(mine, gathered):
    n_l = len(mine)
    d = mine[0].shape[1]

    def body(*refs):
        loc, got = refs[:n_l], refs[n_l:2 * n_l]
        tot_refs, dmod_ref = refs[2 * n_l:3 * n_l], refs[3 * n_l]
        me = _index(_place())
        part = lambda a, dev: jnp.where(dev == me, loc[a][...], got[a][dev])
        for a in range(n_l):
            tot = part(a, 0)
            for dev in range(1, N_DEV):
                tot = tot + part(a, dev)
            tot_refs[a][...] = tot
        for dev in range(N_DEV):
            for k, (a, r) in enumerate(MOD_ROWS):
                dmod_ref[dev:dev + 1, k * d:(k + 1) * d] = part(a, dev)[r:r + 1, :]

    vmem = pl.BlockSpec(memory_space=pltpu.VMEM)
    out = pl.pallas_call(
        body, name="small_sum", in_specs=[vmem] * (2 * n_l), out_specs=[vmem] * (n_l + 1),
        out_shape=[jax.ShapeDtypeStruct(a.shape, F32) for a in mine] + [jax.ShapeDtypeStruct((N_DEV, 6 * d), F32)],
        compiler_params=_params(),
    )(*mine, *gathered)
    return out[:n_l], out[n_l]


def _small_adam(totals, weights, moms, vels):
    n_t, n_w = len(totals), len(weights)

    def body(*refs):
        t_in, t_mix, t_ffn, t_pool, t_blk, t_conv, _ = (r[...] for r in refs[:n_t])
        w_refs, m_refs, v_refs = (refs[n_t + k * n_w:n_t + (k + 1) * n_w] for k in range(3))
        outs = refs[n_t + 3 * n_w:]

        def update(idx, g, at=()):
            sel = lambda ref: ref.at[at] if at else ref
            delta, nm, nv = _adam_math(sel(w_refs[idx])[...], g, sel(m_refs[idx])[...], sel(v_refs[idx])[...])
            for k, val in enumerate((g, delta, nm, nv)):
                sel(outs[4 * idx + k])[...] = val

        tots = (t_in, t_mix, t_ffn)
        update(0, jnp.concatenate([tots[a][r:r + 1] for a, r in MOD_ROWS], axis=1))
        update(1, t_in[2:3])
        update(2, t_mix[4:5])
        update(3, t_mix[2:3])
        update(4, t_ffn[1:2])
        for gi in range(len(POOL_WINDOWS)):
            update(5, t_blk[gi], at=(0, gi))
        update(6, jnp.concatenate([t_pool[0:1, gi * HEAD_DIM:(gi + 1) * HEAD_DIM] for gi in range(len(POOL_WINDOWS))], axis=0),
               at=(0,))
        update(7, t_pool[1:2])
        update(8, t_conv[3:4])

    vmem = pl.BlockSpec(memory_space=pltpu.VMEM)
    return pl.pallas_call(
        body, name="small_adam", in_specs=[vmem] * (n_t + 3 * n_w), out_specs=[vmem] * (4 * n_w),
        out_shape=[jax.ShapeDtypeStruct(w.shape, F32) for w in weights for _ in range(4)],
        compiler_params=_params(),
    )(*totals, *weights, *moms, *vels)


def _adam_math(w, g, m, v):
    m = ADAM_B1 * m + (1.0 - ADAM_B1) * g
    v = ADAM_B2 * v + (1.0 - ADAM_B2) * (g * g)
    m_hat = m / (1.0 - ADAM_B1 ** ADAM_STEP)
    v_hat = v / (1.0 - ADAM_B2 ** ADAM_STEP)
    delta = -ADAM_LR * (m_hat / (jnp.sqrt(v_hat) + ADAM_EPS) + ADAM_WD * w)
    return delta, m, v


def _adam(w, g, m, v, name, tr):
    rows, cols = w.shape

    def body(w_ref, g_ref, m_ref, v_ref, d_ref, nm_ref, nv_ref):
        d_ref[...], nm_ref[...], nv_ref[...] = _adam_math(w_ref[...], g_ref[...], m_ref[...], v_ref[...])

    spec = pl.BlockSpec((tr, cols), lambda i: (i, 0))
    shape = jax.ShapeDtypeStruct((rows, cols), F32)
    return pl.pallas_call(
        body, name=name, grid=(rows // tr,), in_specs=[spec] * 4, out_specs=[spec] * 3,
        out_shape=[shape] * 3, compiler_params=_params(("arbitrary",)),
    )(w, g, m, v)


def _sum_adam(parts, w, m, v, name, tr):
    _, rows, cols = parts.shape

    def body(p_ref, w_ref, m_ref, v_ref, g_ref, d_ref, nm_ref, nv_ref):
        g = p_ref[0].astype(F32)
        for dev in range(1, N_DEV):
            g = g + p_ref[dev].astype(F32)
        g_ref[...] = g
        d_ref[...], nm_ref[...], nv_ref[...] = _adam_math(w_ref[...], g, m_ref[...], v_ref[...])

    spec = pl.BlockSpec((tr, cols), lambda i: (i, 0))
    shape = jax.ShapeDtypeStruct((rows, cols), F32)
    return pl.pallas_call(
        body, name=name, grid=(rows // tr,),
        in_specs=[pl.BlockSpec((N_DEV, tr, cols), lambda i: (0, i, 0)), spec, spec, spec],
        out_specs=[spec] * 4, out_shape=[shape] * 4, compiler_params=_params(("arbitrary",)),
    )(parts, w, m, v)


def _ada_grad_adam(c_all, dmod_cols, w, m, v, tr):
    rows, cols = w.shape

    def body(c_ref, dm_ref, w_ref, m_ref, v_ref, g_ref, d_ref, nm_ref, nv_ref):
        cv = c_ref[...]
        act = cv * jax.nn.sigmoid(cv)
        g = lax.dot_general(act, dm_ref[...], TN, preferred_element_type=F32, precision=lax.Precision.HIGHEST)
        g_ref[...] = g
        d_ref[...], nm_ref[...], nv_ref[...] = _adam_math(w_ref[...], g, m_ref[...], v_ref[...])

    spec = pl.BlockSpec((tr, cols), lambda i: (i, 0))
    shape = jax.ShapeDtypeStruct((rows, cols), F32)
    return pl.pallas_call(
        body, name="ada_grad_adam", grid=(rows // tr,),
        in_specs=[pl.BlockSpec((N_DEV, tr), lambda i: (0, i)), pl.BlockSpec((N_DEV, cols), lambda i: (0, 0)), spec, spec, spec],
        out_specs=[spec] * 4, out_shape=[shape] * 4, compiler_params=_params(("arbitrary",)),
    )(c_all, dmod_cols, w, m, v)


def _rope_tables(positions):
    inv_freq = ROPE_THETA ** (-jnp.arange(0, 2 * ROT_HALF, 2, dtype=F32) / (2 * ROT_HALF))
    ang = positions.astype(F32)[:, None] * inv_freq
    rows = jnp.concatenate([jnp.cos(ang), jnp.sin(ang), jnp.ones_like(ang)], axis=1)
    spread = [[[0.0] * LANES for _ in range(3 * ROT_HALF)] for _ in range(3)]
    for lane in range(LANES):
        p, j = lane % HEAD_DIM, lane % ROT_HALF
        if p < ROT_HALF:
            spread[0][j][lane] = 1.0
            spread[1][ROT_HALF + j][lane] = -1.0
        elif p < 2 * ROT_HALF:
            spread[0][j][lane] = 1.0
            spread[2][ROT_HALF + j][lane] = 1.0
        else:
            spread[0][2 * ROT_HALF][lane] = 1.0
    return rows, jnp.array(spread, F32)


def _pad_rows(a, rows):
    return jnp.pad(a, ((0, rows - a.shape[0]), (0, 0)))


def _sequence_step(xs, target, rope, mods, gains, w_in_t, w_out_t, relay_ffn, fetch_ffn, send_ffn_grads, send_mix_grads, w_blk_b, b_pool_r,
                   pool_scale_r, conv_w_all, conv_b):
    sh_m, sc_m, gt_m, sh_f, sc_f, gt_f = mods
    g_pre_mix, g_post_mix, g_pre_ffn, g_post_ffn = gains
    h1, u_pool, qkv = _premix_inproj(xs, sh_m, sc_m, g_pre_mix, w_in_t, rope, tm=512)
    o_g, lse_g = _attn_fwd(qkv)
    x1, y1, h2, cat, attn, lse_all = _mix_out(xs, u_pool, o_g, lse_g, w_blk_b, b_pool_r, pool_scale_r, w_out_t,
                                              gt_m, g_post_mix, g_pre_ffn, sc_f, sh_f, tm=256)
    token = relay_ffn(x1)
    w_up_t, w_down_f = fetch_ffn(x1 if token is None else token)
    gate, a_ffn, act, vd, dy2, dout, sums_ffn, loss_loc = _ffn_fwd_loss(h2, x1, target, w_up_t, w_down_f, conv_w_all, conv_b,
                                                              gt_f, g_post_ffn, tm=256, ck=256)

    dgc, dval, dw_down, dconv = _ffn_bwd_act(dy2, gate, a_ffn, act, vd, w_down_f, tm=512, tf=1408, ck=256)
    dup, dh2 = _ffn_bwd_up(dgc, dval, w_up_t, conv_w_all, tm=256)
    dw_up_t = _wgrad(dup, h2, "wgrad_up", tk=2048, tmm=1408)
    token = send_ffn_grads(dw_up_t, dw_down)
    if token is not None:
        sc_f = sc_f + token[0:1, 0:1]
    dx1, dpool, dattn, delta, dw_out_t, sums_mix = _mix_bwd(dh2, dout, x1, y1, cat, attn, w_out_t, sc_f, g_pre_ffn,
                                                           gt_m, g_post_mix, tm=256)
    du, dw_blk, sums_pool = _pool_bwd(dpool, u_pool, w_blk_b, b_pool_r, pool_scale_r, tm=512)
    dproj = _dproj_assemble(du, _attn_bwd(qkv, dattn, lse_all, delta), rope, tm=512)
    dw_in_t = _wgrad(dproj, h1, "wgrad_in", tk=2048, tmm=1280)
    token = send_mix_grads(dw_in_t, dw_out_t)
    if token is not None:
        sc_m = sc_m + token[0:1, 0:1]
    grad_x, sums_in = _inproj_bwd(dproj, w_in_t, xs, dx1, sc_m, g_pre_mix, tm=256)
    return (loss_loc, grad_x, dw_in_t, dw_out_t, dw_up_t, dw_down, dw_blk, dconv,
            sums_in, sums_mix, sums_ffn, sums_pool)


def kernel(x, c, positions, w_ada, b_ada, g_pre_mix, g_post_mix, g_pre_ffn, g_post_ffn, w_in, w_pool, b_pool, pool_scale, w_out, w_up, conv_w, conv_b, w_down, loss_target, m_w_ada, m_b_ada, m_g_pre_mix, m_g_post_mix, m_g_pre_ffn, m_g_post_ffn, m_w_in, m_w_pool, m_b_pool, m_pool_scale, m_w_out, m_w_up, m_conv_w, m_conv_b, m_w_down, v_w_ada, v_b_ada, v_g_pre_mix, v_g_post_mix, v_g_pre_ffn, v_g_post_ffn, v_w_in, v_w_pool, v_b_pool, v_pool_scale, v_w_out, v_w_up, v_conv_w, v_conv_b, v_w_down):
    s_len, d = x.shape[1], x.shape[2]
    d_ff = w_down.shape[1] * N_DEV
    me = _index(_place())
    xs, target = x[0], loss_target[0]

    ncol = w_ada.shape[2]
    b_cols = lax.dynamic_slice(b_ada, (0, me * ncol), (1, ncol))
    c_all, mod, taps_all, (w_in_t, w_out_t) = _entry_exchange(
        jnp.broadcast_to(c, (8, d)), w_ada[0], b_cols, _pad_rows(conv_w[0], 8),
        [w_in[0].T.astype(BF16), w_out[0].T.astype(BF16)])
    c_all = c_all[:, 0, :]
    conv_w_all = jnp.transpose(taps_all[:, :3, :], (1, 0, 2)).reshape(3, d_ff)
    sh_m, sc_m, gt_m, sh_f, sc_f, gt_f = [mod[:, 0, :].reshape(1, -1)[:, k * d:(k + 1) * d] for k in range(6)]

    rope = _rope_tables(positions[0])
    w_blk = jnp.zeros((256, 256), F32)
    for gi in range(4):
        w_blk = lax.dynamic_update_slice(w_blk, w_pool[0, gi], (gi * HEAD_DIM, gi * HEAD_DIM))
    w_blk_b = w_blk.astype(BF16)
    b_pool_r, pool_scale_r = b_pool.reshape(1, 256), pool_scale.reshape(1, 256)

    up_sh, down_sh = w_up[0].T.astype(BF16), w_down[0].astype(BF16)
    w_in_t, conv_w_all, up_sh, down_sh = lax.optimization_barrier((w_in_t, conv_w_all, up_sh, down_sh))
    lands = [_landing_zone("gather", s, me, "land_" + nm) for s, nm in ((up_sh, "w_up"), (down_sh, "w_down"))]
    w_send, w_recv, w_src, w_land, w_token = _exchange_start("gather_ici", [up_sh, down_sh], lands, "ffn_weights_ici_start")
    relay = []

    def relay_ffn(after):
        arrived = _exchange_wait("gather_ici", w_send, w_recv, w_src, w_land, after, "ffn_weights_ici_wait")
        relay.extend(_exchange_start("gather_d2d", [], arrived, "ffn_weights_d2d_start"))
        return relay[4]

    def fetch_ffn(after):
        return _exchange_wait("gather_d2d", relay[0], relay[1], [], relay[3], after, "ffn_weights_d2d_wait")

    flight = []

    def send_ffn_grads(dw_up_t, dw_down):
        lands = [_landing_zone("scatter", dw_up_t, me, "land_dw_up"), _landing_zone("scatter", dw_down, me, "land_dw_down")]
        flight.extend(_exchange_start("scatter", [dw_up_t, dw_down], lands, "ffn_grads_start"))
        return flight[4]

    mix_flight = []

    def send_mix_grads(dw_in_t, dw_out_t):
        lands = [_landing_zone("scatter", dw_in_t, me, "land_dw_in"), _landing_zone("scatter", dw_out_t, me, "land_dw_out")]
        mix_flight.extend(_exchange_start("scatter", [dw_in_t, dw_out_t], lands, "mix_grads_start"))
        return mix_flight[4]

    (loss_loc, grad_x, dw_in_t, dw_out_t, _, _, dw_pool, dconv,
     sums_in, sums_mix, sums_ffn, sums_pool) = _sequence_step(
        xs, target, rope, (sh_m + w_token[0:1, 0:1], sc_m, gt_m, sh_f, sc_f, gt_f),
        (g_pre_mix, g_post_mix, g_pre_ffn, g_post_ffn),
        w_in_t, w_out_t, relay_ffn, fetch_ffn, send_ffn_grads, send_mix_grads, w_blk_b, b_pool_r, pool_scale_r, conv_w_all, conv_b)

    small = [sums_in, sums_mix, sums_ffn, sums_pool, dw_pool, dconv, loss_loc]
    small_flight = _exchange_start("allgather", small, [lax.empty((N_DEV,) + a.shape, F32) for a in small], "small_start")

    parts_ffn = _exchange_wait("scatter", *flight[:4], small_flight[4], "ffn_grads_wait")
    big = {
        "w_up": [a.T for a in _sum_adam(parts_ffn[0], w_up[0].T, m_w_up[0].T, v_w_up[0].T, "adam_w_up", 64)],
        "w_down": _sum_adam(parts_ffn[1], w_down[0], m_w_down[0], v_w_down[0], "adam_w_down", 32),
    }

    rep_w = [b_ada, g_pre_mix, g_post_mix, g_pre_ffn, g_post_ffn, w_pool, b_pool, pool_scale, conv_b]
    rep_m = [m_b_ada, m_g_pre_mix, m_g_post_mix, m_g_pre_ffn, m_g_post_ffn, m_w_pool, m_b_pool, m_pool_scale, m_conv_b]
    rep_v = [v_b_ada, v_g_pre_mix, v_g_post_mix, v_g_pre_ffn, v_g_post_ffn, v_w_pool, v_b_pool, v_pool_scale, v_conv_b]
    parts_mix = _exchange_wait("scatter", *mix_flight[:4], big["w_down"][0], "mix_grads_wait")
    big["w_in"] = [a.T for a in _sum_adam(parts_mix[0], w_in[0].T, m_w_in[0].T, v_w_in[0].T, "adam_w_in", 64)]
    big["w_out"] = [a.T for a in _sum_adam(parts_mix[1], w_out[0].T, m_w_out[0].T, v_w_out[0].T, "adam_w_out", 128)]
    gathered = _exchange_wait("allgather", *small_flight[:4], big["w_out"][0], "small_wait")
    totals, dmod_all = _small_sum(small, gathered)
    dconv_tot, loss_tot = totals[5], totals[6]
    rep_out = _small_adam(totals, rep_w, rep_m, rep_v)
    g_rep, d_rep, nm_rep, nv_rep = (rep_out[k::4] for k in range(4))

    fcol = d_ff // N_DEV
    g_cw = lax.dynamic_slice(dconv_tot, (0, me * fcol), (3, fcol))
    d_cw, nm_cw, nv_cw = _adam(conv_w[0], g_cw, m_conv_w[0], v_conv_w[0], "adam_conv_w", 3)

    dmod_cols = lax.dynamic_slice(dmod_all, (0, me * ncol), (N_DEV, ncol))
    g_ada, d_ada, nm_ada, nv_ada = _ada_grad_adam(c_all, dmod_cols, w_ada[0], m_w_ada[0], v_w_ada[0], 256)

    loss = loss_tot[0, 0]

    def group(k):
        rep = (g_rep, d_rep, nm_rep, nv_rep)[k]
        ada = (g_ada, d_ada, nm_ada, nv_ada)[k][None]
        cw = (g_cw, d_cw, nm_cw, nv_cw)[k][None]
        return [ada, rep[0], rep[1], rep[2], rep[3], rep[4], big["w_in"][k][None], rep[5], rep[6], rep[7],
                big["w_out"][k][None], big["w_up"][k][None], cw, rep[8], big["w_down"][k][None]]

    return (loss, grad_x[None], *group(0), *group(1), *group(2), *group(3))
```

```python
import functools
import math

import jax
import jax.numpy as jnp
from jax import lax
from jax.experimental import pallas as pl
from jax.experimental.pallas import tpu as pltpu

F32 = jnp.float32
BF16 = jnp.bfloat16
MESH = pl.DeviceIdType.MESH

N_DEV = 8
HEAD_DIM = 64
ROT_HALF = 8
ROPE_THETA = 500000.0
POOL_WINDOWS = (2, 4, 8, 16)
DILATIONS = (1, 4, 16)
BLOCK = 128
NORM_EPS = 1e-6
HALO = 16
MASKED = -1e30
ATTN_FWD_UNROLL = 8
ATTN_BWD_UNROLL = 8

ADAM_LR = 0.001
ADAM_B1 = 0.9
ADAM_B2 = 0.999
ADAM_EPS = 1e-08
ADAM_WD = 0.01
ADAM_STEP = 10

V7X_VMEM_LIMIT = 56 * 1024 * 1024
LANES = 128

NT = (((1,), (1,)), ((), ()))
NN = (((1,), (0,)), ((), ()))
TN = (((0,), (0,)), ((), ()))


def _dot(a, b, dims):
    return lax.dot_general(a, b, dims, preferred_element_type=F32)


def _params(sem=None, vmem=V7X_VMEM_LIMIT):
    if sem is None:
        return pltpu.CompilerParams(vmem_limit_bytes=vmem)
    return pltpu.CompilerParams(dimension_semantics=sem, vmem_limit_bytes=vmem)


def _rstd(v):
    return lax.rsqrt(jnp.mean(v * v, axis=-1, keepdims=True) + NORM_EPS)


def _norm_bwd(dn, n, rstd):
    return rstd * (dn - n * jnp.mean(dn * n, axis=-1, keepdims=True))


def _rope_lanes(cs_ref, spread_ref):
    return [lax.dot_general(cs_ref[...], spread_ref[k], NN, preferred_element_type=F32, precision=lax.Precision.HIGHEST)
            for k in range(3)]


def _rope_fwd(p, lanes):
    return p * lanes[0] + pltpu.roll(p, LANES - ROT_HALF, 1) * lanes[1] + pltpu.roll(p, ROT_HALF, 1) * lanes[2]


def _rope_bwd(dp, lanes):
    return dp * lanes[0] + pltpu.roll(dp * lanes[1], ROT_HALF, 1) + pltpu.roll(dp * lanes[2], LANES - ROT_HALF, 1)


def _gelu_parts(v):
    k2 = 2.0 * math.sqrt(2.0 / math.pi)
    c = 0.044715
    v2 = v * v
    s = jax.nn.sigmoid(v * (k2 + (k2 * c) * v2))
    g = v * s
    dg = s + g * (1.0 - s) * (k2 + (3.0 * k2 * c) * v2)
    return g, dg


def _halo_before(i, tile):
    return jnp.maximum(i * (tile // HALO) - 1, 0)


def _premix_inproj(x, sh, sc, g, w_in_t, rope, tm):
    s_len, d = x.shape
    n_proj = w_in_t.shape[0]
    n_slab = (n_proj - 256) // LANES

    def body(x_ref, sh_ref, sc_ref, g_ref, w_ref, cs_ref, spread_ref, h_ref, up_ref, qkv_ref):
        xv = x_ref[...]
        h = (xv * _rstd(xv) * g_ref[...]) * (1.0 + sc_ref[...]) + sh_ref[...]
        hb = h.astype(BF16)
        h_ref[...] = hb
        up_ref[...] = _dot(hb, w_ref[0:256, :], NT)
        lanes = _rope_lanes(cs_ref, spread_ref)
        for pair in range(n_slab // 2):
            p = _dot(hb, w_ref[256 + 256 * pair:512 + 256 * pair, :], NT)
            for half in range(2):
                ph = p[:, half * LANES:(half + 1) * LANES]
                if pair < 6:
                    ph = _rope_fwd(ph, lanes)
                if pair < 3:
                    ph = ph * (HEAD_DIM ** -0.5)
                qkv_ref[2 * pair + half] = ph

    vec = pl.BlockSpec((1, d), lambda i: (0, 0))
    return pl.pallas_call(
        body, name="premix_inproj", grid=(s_len // tm,),
        in_specs=[pl.BlockSpec((tm, d), lambda i: (i, 0)), vec, vec, vec,
                  pl.BlockSpec((n_proj, d), lambda i: (0, 0)),
                  pl.BlockSpec((tm, rope[0].shape[1]), lambda i: (i, 0)), pl.BlockSpec(rope[1].shape, lambda i: (0, 0, 0))],
        out_specs=[pl.BlockSpec((tm, d), lambda i: (i, 0)),
                   pl.BlockSpec((tm, 256), lambda i: (i, 0)),
                   pl.BlockSpec((n_slab, tm, LANES), lambda i: (0, i, 0))],
        out_shape=[jax.ShapeDtypeStruct((s_len, d), BF16),
                   jax.ShapeDtypeStruct((s_len, 256), F32),
                   jax.ShapeDtypeStruct((n_slab, s_len, LANES), F32)],
        compiler_params=_params(("arbitrary",)),
    )(x, sh, sc, g, w_in_t, *rope)


def _block_rows(n, r, dil):
    start = n * (BLOCK * dil) + r
    if dil == 1:
        return pl.ds(pl.multiple_of(start, BLOCK), BLOCK)
    return pl.ds(start, BLOCK, stride=dil)


def _band_mask(n):
    ri = lax.broadcasted_iota(jnp.int32, (BLOCK, 2 * BLOCK), 0)
    cj = lax.broadcasted_iota(jnp.int32, (BLOCK, 2 * BLOCK), 1)
    cur = (cj >= BLOCK) & (cj - BLOCK <= ri)
    prev = (cj < BLOCK) & (cj >= ri) & (n > 0)
    return cur | prev


def _attn_fwd(qkv):
    s_len = qkv.shape[1]
    n_g = len(DILATIONS)

    def body(q_ref, k_ref, v_ref, o_ref, lse_ref):
        lane = lax.broadcasted_iota(jnp.int32, (BLOCK, LANES), 1)
        first = lane < HEAD_DIM

        def group(dil):
            nb = s_len // (BLOCK * dil)

            def block(t, carry):
                r, n = t // nb, t % nb
                cur = _block_rows(n, r, dil)
                prev = _block_rows(jnp.maximum(n - 1, 0), r, dil)
                q = q_ref[0, cur, :]
                kcat = jnp.concatenate([k_ref[0, prev, :], k_ref[0, cur, :]], axis=0).astype(BF16)
                vcat = jnp.concatenate([v_ref[0, prev, :], v_ref[0, cur, :]], axis=0).astype(BF16)
                valid = _band_mask(n)
                q2 = jnp.concatenate([jnp.where(first, q, 0.0), jnp.where(first, 0.0, q)], axis=0).astype(BF16)
                s = jnp.where(jnp.concatenate([valid, valid], axis=0), _dot(q2, kcat, NT), MASKED)
                m = jnp.max(s, axis=-1, keepdims=True)
                p = jnp.exp(s - m)
                den = jnp.sum(p, axis=-1, keepdims=True)
                o2 = _dot(p.astype(BF16), vcat, NN) / den
                lse2 = m + jnp.log(den)
                o_ref[0, 0, cur, :] = jnp.where(first, o2[:BLOCK], o2[BLOCK:])
                lse_ref[0, 0, cur, :] = jnp.where(first, lse2[:BLOCK], lse2[BLOCK:])
                return carry

            lax.fori_loop(0, nb * dil, block, 0, unroll=ATTN_FWD_UNROLL)

        for gi, dil in enumerate(DILATIONS):
            pl.when(pl.program_id(0) == gi)(functools.partial(group, dil))

    def slab(base):
        return pl.BlockSpec((1, s_len, LANES), lambda g, s: (base + 2 * g + s, 0, 0))

    out = pl.BlockSpec((1, 1, s_len, LANES), lambda g, s: (g, s, 0, 0))
    shape = jax.ShapeDtypeStruct((n_g, 2, s_len, LANES), F32)
    return pl.pallas_call(
        body, name="attn_fwd", grid=(n_g, 2),
        in_specs=[slab(0), slab(6), slab(12)], out_specs=[out, out], out_shape=[shape, shape],
        compiler_params=_params(("arbitrary", "arbitrary")),
    )(qkv, qkv, qkv)


def _pool_mixed(u, halo, i, tm):
    ue = jnp.concatenate([halo, u], axis=0)
    s2 = ue + pltpu.roll(ue, 1, 0)
    s4 = s2 + pltpu.roll(s2, 2, 0)
    s8 = s4 + pltpu.roll(s4, 4, 0)
    s16 = s8 + pltpu.roll(s8, 8, 0)
    grp = lax.broadcasted_iota(jnp.int32, (tm, 256), 1) // HEAD_DIM
    pick = lambda a, b, c, e: jnp.where(grp == 0, a, jnp.where(grp == 1, b, jnp.where(grp == 2, c, e)))
    win_sum = pick(s2[HALO:], s4[HALO:], s8[HALO:], s16[HALO:])
    pos = (i * tm + lax.broadcasted_iota(jnp.int32, (tm, 256), 0)).astype(F32)
    count = jnp.minimum(pos + 1.0, pick(*[float(w) for w in POOL_WINDOWS]))
    return win_sum / count - u, count


def _mix_out(x, u_pool, o_g, lse_g, w_blk, b_pool, pool_scale, w_out_t, gt_m, g_post_mix, g_pre_ffn, sc_f, sh_f, tm):
    s_len, d = x.shape

    def body(x_ref, u_ref, uh_ref, o_ref, l_ref, wb_ref, bp_ref, ps_ref, wo_ref,
             gt_ref, g1_ref, g2_ref, sc_ref, sh_ref,
             x1_ref, y1_ref, h2_ref, cat_ref, attn_ref, lall_ref):
        (o0, o1, o2), (l0, l1, l2) = (o_ref.at[g] for g in range(3)), (l_ref.at[g] for g in range(3))
        i = pl.program_id(0)
        u = u_ref[...]
        halo = uh_ref[...] * (i > 0).astype(F32)
        mixed, _ = _pool_mixed(u, halo, i, tm)
        y = _dot(mixed.astype(BF16), wb_ref[...], NN) + bp_ref[...]
        pool = y * ps_ref[...]
        attn = []
        for s in range(2):
            la, lb, lc = l0[s], l1[s], l2[s]
            mx = jnp.maximum(jnp.maximum(la, lb), lc)
            ea, eb, ec = jnp.exp(la - mx), jnp.exp(lb - mx), jnp.exp(lc - mx)
            den = ea + eb + ec
            lall_ref[s] = mx + jnp.log(den)
            attn.append((ea / den) * o0[s] + (eb / den) * o1[s] + (ec / den) * o2[s])
        attn = jnp.concatenate(attn, axis=1)
        attn_ref[...] = attn
        cat = jnp.concatenate([pool, attn], axis=1).astype(BF16)
        cat_ref[...] = cat
        y1 = _dot(cat, wo_ref[...], NT)
        y1_ref[...] = y1.astype(BF16)
        x1 = x_ref[...] + gt_ref[...] * (y1 * _rstd(y1) * g1_ref[...])
        x1_ref[...] = x1
        h2 = (x1 * _rstd(x1) * g2_ref[...]) * (1.0 + sc_ref[...]) + sh_ref[...]
        h2_ref[...] = h2.astype(BF16)

    tile = lambda w: pl.BlockSpec((tm, w), lambda i: (i, 0))
    slab = pl.BlockSpec((2, tm, LANES), lambda i: (0, i, 0))
    groups = pl.BlockSpec((len(DILATIONS), 2, tm, LANES), lambda i: (0, 0, i, 0))
    const = lambda a: pl.BlockSpec(a.shape, lambda i: (0,) * a.ndim)
    return pl.pallas_call(
        body, name="mix_out", grid=(s_len // tm,),
        in_specs=[tile(d), tile(256), pl.BlockSpec((HALO, 256), lambda i: (_halo_before(i, tm), 0)),
                  groups, groups,
                  const(w_blk), const(b_pool), const(pool_scale), const(w_out_t),
                  const(gt_m), const(g_post_mix), const(g_pre_ffn), const(sc_f), const(sh_f)],
        out_specs=[tile(d), tile(d), tile(d), tile(512), tile(256), slab],
        out_shape=[jax.ShapeDtypeStruct((s_len, d), F32), jax.ShapeDtypeStruct((s_len, d), BF16),
                   jax.ShapeDtypeStruct((s_len, d), BF16), jax.ShapeDtypeStruct((s_len, 512), BF16),
                   jax.ShapeDtypeStruct((s_len, 256), F32), jax.ShapeDtypeStruct((2, s_len, LANES), F32)],
        compiler_params=_params(("arbitrary",)),
    )(x, u_pool, u_pool, o_g, lse_g, w_blk, b_pool, pool_scale, w_out_t, gt_m, g_post_mix, g_pre_ffn, sc_f, sh_f)


def _conv_gate(gate_ext, cw, cb):
    gc = gate_ext * cw[2:3, :] + pltpu.roll(gate_ext, 1, 0) * cw[1:2, :] + pltpu.roll(gate_ext, 2, 0) * cw[0:1, :]
    return gc[HALO:] + cb


def _ffn_fwd_loss(h2, x1, target, w_up_t, w_down, conv_w, conv_b, gt_f, g_post_ffn, tm, ck):
    s_len, d = x1.shape
    d_ff = w_down.shape[0]
    n_t, n_c = s_len // tm, d_ff // ck

    def body(h_ref, hh_ref, x1_ref, tgt_ref, wg_ref, wv_ref, wd_ref, cw_ref, cb_ref, gt_ref, g_ref,
             gate_ref, a_ref, act_ref, vd_ref, dy2_ref, dout_ref, sums_ref, loss_ref, acc_ref):
        i = pl.program_id(0)

        @pl.when(i == 0)
        def _():
            sums_ref[...] = jnp.zeros_like(sums_ref)
            loss_ref[...] = jnp.zeros_like(loss_ref)
            acc_ref[...] = jnp.zeros_like(acc_ref)

        def finish(live):
            y2 = acc_ref[...]
            rstd = _rstd(y2)
            n = y2 * rstd
            rn = n * g_ref[...]
            err = x1_ref[...] + gt_ref[...] * rn - tgt_ref[...]
            keep = lambda v: jnp.where(live, v, 0.0)
            loss_ref[...] += keep(0.5 * jnp.sum(jnp.mean(err * err, axis=-1, keepdims=True), axis=0, keepdims=True))
            dout = err * (1.0 / d)
            dout_ref[...] = dout
            drn = dout * gt_ref[...]
            sums_ref[0:1, :] += keep(jnp.sum(dout * rn, axis=0, keepdims=True))
            sums_ref[1:2, :] += keep(jnp.sum(drn * n, axis=0, keepdims=True))
            dy2_ref[...] = _norm_bwd(drn * g_ref[...], n, rstd).astype(BF16)

        @pl.when(i < n_t)
        def _():
            h = h_ref[...]
            h_ext = jnp.concatenate([hh_ref[...], h], axis=0)
            row = lax.broadcasted_iota(jnp.int32, (tm + HALO, ck), 0)
            no_halo = (row < HALO) & (i == 0)

            def up(c):
                cs = slice(c * ck, (c + 1) * ck)
                return jnp.where(no_halo, 0.0, _dot(h_ext, wg_ref[cs, :], NT)), _dot(h, wv_ref[cs, :], NT)

            part = None
            nxt = up(0)
            finish(i > 0)
            for c in range(n_c):
                cs = slice(c * ck, (c + 1) * ck)
                gate_ext, val = nxt
                if c + 1 < n_c:
                    nxt = up(c + 1)
                act, dact = _gelu_parts(_conv_gate(gate_ext, cw_ref[:, cs], cb_ref[:, cs]))
                a = (act * val).astype(BF16)
                gate_ref[:, cs] = gate_ext[HALO:].astype(BF16)
                a_ref[:, cs] = a
                act_ref[:, cs] = act.astype(BF16)
                vd_ref[:, cs] = (val * dact).astype(BF16)
                p = _dot(a, wd_ref[cs, :], NN)
                part = p if part is None else part + p
            acc_ref[...] = part

        @pl.when(i == n_t)
        def _():
            finish(True)

    this = lambda i: jnp.minimum(i, n_t - 1)
    before = lambda i: jnp.maximum(i - 1, 0)
    tok = lambda w, at: pl.BlockSpec((tm, w), lambda i: (at(i), 0))
    vec = pl.BlockSpec((1, d), lambda i: (0, 0))
    once = lambda shape, imap: pl.BlockSpec(shape, imap, pipeline_mode=pl.Buffered(1))
    return pl.pallas_call(
        body, name="ffn_fwd_loss", grid=(n_t + 1,),
        in_specs=[tok(d, this), pl.BlockSpec((HALO, d), lambda i: (_halo_before(this(i), tm), 0)),
                  tok(d, before), tok(d, before),
                  once((d_ff, d), lambda i: (0, 0)), once((d_ff, d), lambda i: (1, 0)), once((d_ff, d), lambda i: (0, 0)),
                  pl.BlockSpec((3, d_ff), lambda i: (0, 0)), pl.BlockSpec((1, d_ff), lambda i: (0, 0)), vec, vec],
        out_specs=[tok(d_ff, this)] * 4 + [tok(d, before), tok(d, before), pl.BlockSpec((8, d), lambda i: (0, 0)),
                                          pl.BlockSpec((8, LANES), lambda i: (0, 0))],
        out_shape=[jax.ShapeDtypeStruct((s_len, d_ff), BF16)] * 4
        + [jax.ShapeDtypeStruct((s_len, d), BF16), jax.ShapeDtypeStruct((s_len, d), F32),
           jax.ShapeDtypeStruct((8, d), F32), jax.ShapeDtypeStruct((8, LANES), F32)],
        scratch_shapes=[pltpu.VMEM((tm, d), F32)],
        compiler_params=_params(("arbitrary",)),
    )(h2, h2, x1, target, w_up_t, w_up_t, w_down, conv_w, conv_b, gt_f, g_post_ffn)


def _ffn_bwd_act(dy2, gate, a, act, vd, w_down, tm, tf, ck):
    s_len, d = dy2.shape
    d_ff = w_down.shape[0]
    n_t = s_len // tm
    chunks = [slice(lo, min(lo + ck, tf)) for lo in range(0, tf, ck)]

    def body(dy_ref, g_ref, gh_ref, a_ref, act_ref, vd_ref, wd_ref, dgc_ref, dval_ref, dwd_ref, dconv_ref, acc_ref):
        i = pl.program_id(1)

        @pl.when(i == 0)
        def _():
            acc_ref[...] = jnp.zeros_like(acc_ref)
            dconv_ref[...] = jnp.zeros_like(dconv_ref)

        dy = dy_ref[...]

        def down(cs):
            return _dot(dy, wd_ref[cs, :], NT)

        nxt = down(chunks[0])
        for c, cs in enumerate(chunks):
            width = cs.stop - cs.start
            da = nxt
            if c + 1 < len(chunks):
                nxt = down(chunks[c + 1])
            acc_ref[cs, :] += _dot(a_ref[:, cs], dy, TN)
            row = lax.broadcasted_iota(jnp.int32, (tm + HALO, width), 0)
            gate_ext = jnp.where((row < HALO) & (i == 0), 0.0,
                                 jnp.concatenate([gh_ref[:, cs], g_ref[:, cs]], axis=0).astype(F32))
            dgc = da * vd_ref[:, cs].astype(F32)
            dgc_ref[:, cs] = dgc.astype(BF16)
            dval_ref[:, cs] = (da * act_ref[:, cs].astype(F32)).astype(BF16)
            rows = [jnp.sum(dgc * pltpu.roll(gate_ext, 2 - k, 0)[HALO:], axis=0, keepdims=True) for k in range(2)]
            rows += [jnp.sum(dgc * gate_ext[HALO:], axis=0, keepdims=True), jnp.sum(dgc, axis=0, keepdims=True),
                     jnp.zeros((4, width), F32)]
            dconv_ref[:, cs] += jnp.concatenate(rows, axis=0)

        @pl.when(i == n_t - 1)
        def _():
            dwd_ref[...] = acc_ref[...].astype(BF16)

    tokf = pl.BlockSpec((tm, tf), lambda j, i: (i, j))
    return pl.pallas_call(
        body, name="ffn_bwd_act", grid=(d_ff // tf, n_t),
        in_specs=[pl.BlockSpec((tm, d), lambda j, i: (i, 0)), tokf,
                  pl.BlockSpec((HALO, tf), lambda j, i: (_halo_before(i, tm), j)), tokf, tokf, tokf,
                  pl.BlockSpec((tf, d), lambda j, i: (j, 0))],
        out_specs=[tokf, tokf, pl.BlockSpec((tf, d), lambda j, i: (j, 0)), pl.BlockSpec((8, tf), lambda j, i: (0, j))],
        out_shape=[jax.ShapeDtypeStruct((s_len, d_ff), BF16), jax.ShapeDtypeStruct((s_len, d_ff), BF16),
                   jax.ShapeDtypeStruct((d_ff, d), BF16), jax.ShapeDtypeStruct((8, d_ff), F32)],
        scratch_shapes=[pltpu.VMEM((tf, d), F32)],
        compiler_params=_params(("arbitrary", "arbitrary")),
    )(dy2, gate, gate, a, act, vd, w_down)


def _ffn_up_mix_bwd(dgc, dval, w_up_t, conv_w, dout, x1, y1, cat, attn, w_out_t, sc_f, g_pre_ffn, gt_m, g_post_mix, tm):
    s_len, d_ff = dgc.shape
    d = w_up_t.shape[1]
    n_t = s_len // tm
    half = tm // 2

    def body(dg_ref, dgn_ref, dv_ref, cw_ref, w_ref, do_ref, x1_ref, y1_ref, cat_ref, at_ref, wo_ref,
             sc_ref, g2_ref, gt_ref, g1_ref,
             dup_ref, dx1_ref, dpool_ref, dattn_ref, delta_ref, dwo_ref, sums_ref, acc_ref):
        i = pl.program_id(0)
        nxt = dgn_ref[...].astype(F32) * (i < n_t - 1).astype(F32)
        ext = jnp.concatenate([dg_ref[...].astype(F32), nxt], axis=0)
        rows = tm + HALO
        dgate = (ext * cw_ref[2:3, :] + pltpu.roll(ext, rows - 1, 0) * cw_ref[1:2, :]
                 + pltpu.roll(ext, rows - 2, 0) * cw_ref[0:1, :])[:tm]
        dup_ref[:, 0:d_ff] = dgate.astype(BF16)
        dup_ref[:, d_ff:2 * d_ff] = dv_ref[...]
        halves = [slice(0, half), slice(half, tm)]
        dhs = [_dot(dup_ref[rs, :], w_ref[...], NN) for rs in halves]
        lane = lax.broadcasted_iota(jnp.int32, (half, LANES), 1)
        first = lane < HEAD_DIM
        dwo = sums = None
        for rs, dh in zip(halves, dhs):
            x1 = x1_ref[rs, :]
            r2 = _rstd(x1)
            n2 = x1 * r2
            ng = n2 * g2_ref[...]
            dng = dh * (1.0 + sc_ref[...])
            dx1 = do_ref[rs, :] + _norm_bwd(dng * g2_ref[...], n2, r2)
            dx1_ref[rs, :] = dx1
            y1 = y1_ref[rs, :].astype(F32)
            r1 = _rstd(y1)
            n1 = y1 * r1
            drn = dx1 * gt_ref[...]
            dy1 = _norm_bwd(drn * g1_ref[...], n1, r1).astype(BF16)
            dcat = _dot(dy1, wo_ref[...], NN)
            dpool_ref[rs, :] = dcat[:, 0:256]
            for s in range(2):
                da = dcat[:, 256 + s * LANES:256 + (s + 1) * LANES]
                dattn_ref[s, rs, :] = da
                prod = da * at_ref[rs, s * LANES:(s + 1) * LANES]
                tot = jnp.sum(prod, axis=-1, keepdims=True)
                lo = jnp.sum(jnp.where(first, prod, 0.0), axis=-1, keepdims=True)
                delta_ref[s, rs, :] = jnp.where(first, lo, tot - lo)
            dwo_h = _dot(dy1, cat_ref[rs, :], TN)
            sums_h = jnp.concatenate(
                [jnp.sum(dh, axis=0, keepdims=True), jnp.sum(dh * ng, axis=0, keepdims=True),
                 jnp.sum(dng * n2, axis=0, keepdims=True), jnp.sum(dx1 * (n1 * g1_ref[...]), axis=0, keepdims=True),
                 jnp.sum(drn * n1, axis=0, keepdims=True), jnp.zeros((3, d), F32)], axis=0)
            dwo = dwo_h if dwo is None else dwo + dwo_h
            sums = sums_h if sums is None else sums + sums_h

        @pl.when(i == 0)
        def _():
            acc_ref[...] = dwo
            sums_ref[...] = sums

        @pl.when(i > 0)
        def _():
            acc_ref[...] += dwo
            sums_ref[...] += sums

        @pl.when(i == n_t - 1)
        def _():
            dwo_ref[...] = acc_ref[...].astype(BF16)

    tile = lambda w: pl.BlockSpec((tm, w), lambda i: (i, 0))
    slab = pl.BlockSpec((2, tm, LANES), lambda i: (0, i, 0))
    vec = pl.BlockSpec((1, d), lambda i: (0, 0))
    once = lambda shape: pl.BlockSpec(shape, lambda i: (0, 0), pipeline_mode=pl.Buffered(1))
    return pl.pallas_call(
        body, name="ffn_up_mix_bwd", grid=(n_t,),
        in_specs=[tile(d_ff), pl.BlockSpec((HALO, d_ff), lambda i: (jnp.minimum((i + 1) * (tm // HALO), s_len // HALO - 1), 0)),
                  tile(d_ff), pl.BlockSpec((3, d_ff), lambda i: (0, 0)), once((2 * d_ff, d)),
                  tile(d), tile(d), tile(d), tile(512), tile(256), once((d, 512)), vec, vec, vec, vec],
        out_specs=[tile(2 * d_ff), tile(d), tile(256), slab, slab, pl.BlockSpec((d, 512), lambda i: (0, 0)),
                   pl.BlockSpec((8, d), lambda i: (0, 0))],
        out_shape=[jax.ShapeDtypeStruct((s_len, 2 * d_ff), BF16), jax.ShapeDtypeStruct((s_len, d), F32),
                   jax.ShapeDtypeStruct((s_len, 256), F32),
                   jax.ShapeDtypeStruct((2, s_len, LANES), F32), jax.ShapeDtypeStruct((2, s_len, LANES), F32),
                   jax.ShapeDtypeStruct((d, 512), BF16), jax.ShapeDtypeStruct((8, d), F32)],
        scratch_shapes=[pltpu.VMEM((d, 512), F32)],
        compiler_params=_params(("arbitrary",)),
    )(dgc, dgc, dval, conv_w, w_up_t, dout, x1, y1, cat, attn, w_out_t, sc_f, g_pre_ffn, gt_m, g_post_mix)


def _ffn_bwd_up(dgc, dval, w_up_t, conv_w, tm):
    s_len, d_ff = dgc.shape
    d = w_up_t.shape[1]
    n_t = s_len // tm

    def body(dg_ref, dgn_ref, dv_ref, cw_ref, w_ref, dup_ref, dh_ref):
        i = pl.program_id(0)
        nxt = dgn_ref[...].astype(F32) * (i < n_t - 1).astype(F32)
        ext = jnp.concatenate([dg_ref[...].astype(F32), nxt], axis=0)
        rows = tm + HALO
        dgate = (ext * cw_ref[2:3, :] + pltpu.roll(ext, rows - 1, 0) * cw_ref[1:2, :]
                 + pltpu.roll(ext, rows - 2, 0) * cw_ref[0:1, :])[:tm]
        dup = jnp.concatenate([dgate.astype(BF16), dv_ref[...]], axis=1)
        dup_ref[...] = dup
        dh_ref[...] = _dot(dup, w_ref[...], NN).astype(BF16)

    tokf = pl.BlockSpec((tm, d_ff), lambda i: (i, 0))
    return pl.pallas_call(
        body, name="ffn_bwd_up", grid=(n_t,),
        in_specs=[tokf, pl.BlockSpec((HALO, d_ff), lambda i: (jnp.minimum((i + 1) * (tm // HALO), s_len // HALO - 1), 0)),
                  tokf, pl.BlockSpec((3, d_ff), lambda i: (0, 0)), pl.BlockSpec((2 * d_ff, d), lambda i: (0, 0))],
        out_specs=[pl.BlockSpec((tm, 2 * d_ff), lambda i: (i, 0)), pl.BlockSpec((tm, d), lambda i: (i, 0))],
        out_shape=[jax.ShapeDtypeStruct((s_len, 2 * d_ff), BF16), jax.ShapeDtypeStruct((s_len, d), BF16)],
        compiler_params=_params(("arbitrary",)),
    )(dgc, dgc, dval, conv_w, w_up_t)


def _mix_bwd(dh2, dout, x1, y1, cat, attn, w_out_t, sc_f, g_pre_ffn, gt_m, g_post_mix, tm):
    s_len, d = x1.shape
    n_t = s_len // tm

    def body(dh_ref, do_ref, x1_ref, y1_ref, cat_ref, at_ref, wo_ref, sc_ref, g2_ref, gt_ref, g1_ref,
             dx1_ref, dpool_ref, dattn_ref, delta_ref, dwo_ref, sums_ref, acc_ref):
        i = pl.program_id(0)
        dh = dh_ref[...].astype(F32)
        x1 = x1_ref[...]
        r2 = _rstd(x1)
        n2 = x1 * r2
        ng = n2 * g2_ref[...]
        dng = dh * (1.0 + sc_ref[...])
        dx1 = do_ref[...] + _norm_bwd(dng * g2_ref[...], n2, r2)
        dx1_ref[...] = dx1
        y1 = y1_ref[...].astype(F32)
        r1 = _rstd(y1)
        n1 = y1 * r1
        drn = dx1 * gt_ref[...]
        dy1 = _norm_bwd(drn * g1_ref[...], n1, r1).astype(BF16)
        dcat = _dot(dy1, wo_ref[...], NN)
        dpool_ref[...] = dcat[:, 0:256]
        lane = lax.broadcasted_iota(jnp.int32, (tm, LANES), 1)
        first = lane < HEAD_DIM
        for s in range(2):
            da = dcat[:, 256 + s * LANES:256 + (s + 1) * LANES]
            dattn_ref[s] = da
            prod = da * at_ref[:, s * LANES:(s + 1) * LANES]
            tot = jnp.sum(prod, axis=-1, keepdims=True)
            lo = jnp.sum(jnp.where(first, prod, 0.0), axis=-1, keepdims=True)
            delta_ref[s] = jnp.where(first, lo, tot - lo)
        dwo = _dot(dy1, cat_ref[...], TN)
        sums = jnp.concatenate(
            [jnp.sum(dh, axis=0, keepdims=True), jnp.sum(dh * ng, axis=0, keepdims=True),
             jnp.sum(dng * n2, axis=0, keepdims=True), jnp.sum(dx1 * (n1 * g1_ref[...]), axis=0, keepdims=True),
             jnp.sum(drn * n1, axis=0, keepdims=True), jnp.zeros((3, d), F32)], axis=0)

        @pl.when(i == 0)
        def _():
            acc_ref[...] = dwo
            sums_ref[...] = sums

        @pl.when(i > 0)
        def _():
            acc_ref[...] += dwo
            sums_ref[...] += sums

        @pl.when(i == n_t - 1)
        def _():
            dwo_ref[...] = acc_ref[...].astype(BF16)

    tile = lambda w: pl.BlockSpec((tm, w), lambda i: (i, 0))
    slab = pl.BlockSpec((2, tm, LANES), lambda i: (0, i, 0))
    vec = pl.BlockSpec((1, d), lambda i: (0, 0))
    return pl.pallas_call(
        body, name="mix_bwd", grid=(n_t,),
        in_specs=[tile(d), tile(d), tile(d), tile(d), tile(512), tile(256),
                  pl.BlockSpec((d, 512), lambda i: (0, 0)), vec, vec, vec, vec],
        out_specs=[tile(d), tile(256), slab, slab, pl.BlockSpec((d, 512), lambda i: (0, 0)),
                   pl.BlockSpec((8, d), lambda i: (0, 0))],
        out_shape=[jax.ShapeDtypeStruct((s_len, d), F32), jax.ShapeDtypeStruct((s_len, 256), F32),
                   jax.ShapeDtypeStruct((2, s_len, LANES), F32), jax.ShapeDtypeStruct((2, s_len, LANES), F32),
                   jax.ShapeDtypeStruct((d, 512), BF16), jax.ShapeDtypeStruct((8, d), F32)],
        scratch_shapes=[pltpu.VMEM((d, 512), F32)],
        compiler_params=_params(("arbitrary",)),
    )(dh2, dout, x1, y1, cat, attn, w_out_t, sc_f, g_pre_ffn, gt_m, g_post_mix)


def _pool_bwd(dpool, u_pool, w_blk, b_pool, pool_scale, tm):
    s_len = dpool.shape[0]
    n_t = s_len // tm

    def body(dp_ref, dpn_ref, u_ref, uh_ref, wb_ref, bp_ref, ps_ref, du_ref, dwp_ref, sums_ref, acc_ref):
        i = pl.program_id(0)
        u = u_ref[...]
        mixed, _ = _pool_mixed(u, uh_ref[...] * (i > 0).astype(F32), i, tm)
        mixed_b = mixed.astype(BF16)
        y = _dot(mixed_b, wb_ref[...], NN) + bp_ref[...]
        dp = dp_ref[...]
        dy = dp * ps_ref[...]
        dwb = _dot(mixed_b, dy.astype(BF16), TN)
        sums = jnp.concatenate([jnp.sum(dy, axis=0, keepdims=True), jnp.sum(dp * y, axis=0, keepdims=True),
                                jnp.zeros((6, 256), F32)], axis=0)
        dp_ext = jnp.concatenate([dp, dpn_ref[...] * (i < n_t - 1).astype(F32)], axis=0)
        dmix = _dot((dp_ext * ps_ref[...]).astype(BF16), wb_ref[...], NT)
        rows = tm + HALO
        grp = lax.broadcasted_iota(jnp.int32, (rows, 256), 1) // HEAD_DIM
        pick = lambda a, b, c, e: jnp.where(grp == 0, a, jnp.where(grp == 1, b, jnp.where(grp == 2, c, e)))
        pos = (i * tm + lax.broadcasted_iota(jnp.int32, (rows, 256), 0)).astype(F32)
        z = dmix / jnp.minimum(pos + 1.0, pick(*[float(w) for w in POOL_WINDOWS]))
        f2 = z + pltpu.roll(z, rows - 1, 0)
        f4 = f2 + pltpu.roll(f2, rows - 2, 0)
        f8 = f4 + pltpu.roll(f4, rows - 4, 0)
        f16 = f8 + pltpu.roll(f8, rows - 8, 0)
        du_ref[...] = (pick(f2, f4, f8, f16) - dmix)[:tm]

        @pl.when(i == 0)
        def _():
            acc_ref[...] = dwb
            sums_ref[...] = sums

        @pl.when(i > 0)
        def _():
            acc_ref[...] += dwb
            sums_ref[...] += sums

        @pl.when(i == n_t - 1)
        def _():
            full = acc_ref[...]
            for gi in range(len(POOL_WINDOWS)):
                lo = gi * HEAD_DIM
                dwp_ref[gi] = full[lo:lo + HEAD_DIM, lo:lo + HEAD_DIM]

    n_g = len(POOL_WINDOWS)
    tile = pl.BlockSpec((tm, 256), lambda i: (i, 0))
    const = lambda a: pl.BlockSpec(a.shape, lambda i: (0,) * a.ndim)
    return pl.pallas_call(
        body, name="pool_bwd", grid=(n_t,),
        in_specs=[tile, pl.BlockSpec((HALO, 256), lambda i: (jnp.minimum((i + 1) * (tm // HALO), s_len // HALO - 1), 0)),
                  tile, pl.BlockSpec((HALO, 256), lambda i: (_halo_before(i, tm), 0)),
                  const(w_blk), const(b_pool), const(pool_scale)],
        out_specs=[tile, pl.BlockSpec((n_g, HEAD_DIM, HEAD_DIM), lambda i: (0, 0, 0)), pl.BlockSpec((8, 256), lambda i: (0, 0))],
        out_shape=[jax.ShapeDtypeStruct((s_len, 256), F32), jax.ShapeDtypeStruct((n_g, HEAD_DIM, HEAD_DIM), F32),
                   jax.ShapeDtypeStruct((8, 256), F32)],
        scratch_shapes=[pltpu.VMEM((256, 256), F32)],
        compiler_params=_params(("arbitrary",)),
    )(dpool, dpool, u_pool, u_pool, w_blk, b_pool, pool_scale)


def _attn_bwd(qkv, dattn, lse_all, delta):
    s_len = qkv.shape[1]
    n_g = len(DILATIONS)

    def body(q_ref, k_ref, v_ref, do_ref, l_ref, dl_ref, dq_ref, dk_ref, dv_ref):
        lane = lax.broadcasted_iota(jnp.int32, (BLOCK, LANES), 1)
        first = lane < HEAD_DIM

        def group(dil):
            nb = s_len // (BLOCK * dil)

            def block(t, carry):
                dk_part, dv_part = carry
                r, n = t // nb, t % nb
                cur = _block_rows(n, r, dil)
                prev = _block_rows(jnp.maximum(n - 1, 0), r, dil)
                q = q_ref[0, cur, :]
                do = do_ref[0, cur, :]
                lse = l_ref[0, cur, :]
                dlt = dl_ref[0, cur, :]
                kcat = jnp.concatenate([k_ref[0, prev, :], k_ref[0, cur, :]], axis=0).astype(BF16)
                vcat = jnp.concatenate([v_ref[0, prev, :], v_ref[0, cur, :]], axis=0).astype(BF16)
                valid = _band_mask(n)
                stack = lambda a: jnp.concatenate([jnp.where(first, a, 0.0), jnp.where(first, 0.0, a)], axis=0)
                rows2 = lambda a: jnp.concatenate([a[:, 0:1], a[:, HEAD_DIM:HEAD_DIM + 1]], axis=0)
                q2, do2 = stack(q).astype(BF16), stack(do).astype(BF16)
                valid2 = jnp.concatenate([valid, valid], axis=0)
                p = jnp.where(valid2, jnp.exp(_dot(q2, kcat, NT) - rows2(lse)), 0.0)
                ds = (p * (_dot(do2, vcat, NT) - rows2(dlt))).astype(BF16)
                dq2 = _dot(ds, kcat, NN)
                dq_ref[0, 0, cur, :] = jnp.where(first, dq2[:BLOCK], dq2[BLOCK:])
                dkc = _dot(ds, q2, TN)
                dvc = _dot(p.astype(BF16), do2, TN)
                dk_ref[0, 0, prev, :] = dk_part + dkc[:BLOCK]
                dv_ref[0, 0, prev, :] = dv_part + dvc[:BLOCK]
                dk_ref[0, 0, cur, :] = dkc[BLOCK:]
                dv_ref[0, 0, cur, :] = dvc[BLOCK:]
                return dkc[BLOCK:], dvc[BLOCK:]

            def blocks(tt, carry):
                for u in range(ATTN_BWD_UNROLL):
                    carry = block(tt * ATTN_BWD_UNROLL + u, carry)
                return carry

            zero = jnp.zeros((BLOCK, LANES), F32)
            lax.fori_loop(0, nb * dil // ATTN_BWD_UNROLL, blocks, (zero, zero))

        for gi, dil in enumerate(DILATIONS):
            pl.when(pl.program_id(1) == gi)(functools.partial(group, dil))

    def slab(base):
        return pl.BlockSpec((1, s_len, LANES), lambda s, g: (base + 2 * g + s, 0, 0))

    one = pl.BlockSpec((1, s_len, LANES), lambda s, g: (s, 0, 0))
    out = pl.BlockSpec((1, 1, s_len, LANES), lambda s, g: (g, s, 0, 0))
    shape = jax.ShapeDtypeStruct((n_g, 2, s_len, LANES), F32)
    return pl.pallas_call(
        body, name="attn_bwd", grid=(2, n_g),
        in_specs=[slab(0), slab(6), slab(12), one, one, one],
        out_specs=[out, out, out], out_shape=[shape, shape, shape],
        compiler_params=_params(("arbitrary", "arbitrary")),
    )(qkv, qkv, qkv, dattn, lse_all, delta)


def _dproj_assemble(du, dqkv, rope, tm):
    s_len = du.shape[0]
    n_proj = 256 + 18 * LANES

    def body(du_ref, dq_ref, dk_ref, dv_ref, cs_ref, spread_ref, dproj_ref):
        dproj_ref[:, 0:256] = du_ref[...].astype(BF16)
        lanes = _rope_lanes(cs_ref, spread_ref)
        col = 256
        for kind, dref in enumerate((dq_ref, dk_ref, dv_ref)):
            for grp in range(3):
                for s in range(2):
                    piece = dref[grp, s]
                    if kind < 2:
                        piece = _rope_bwd(piece, lanes)
                    if kind == 0:
                        piece = piece * (HEAD_DIM ** -0.5)
                    dproj_ref[:, col:col + LANES] = piece.astype(BF16)
                    col += LANES

    groups = pl.BlockSpec((len(DILATIONS), 2, tm, LANES), lambda i: (0, 0, i, 0))
    return pl.pallas_call(
        body, name="dproj_assemble", grid=(s_len // tm,),
        in_specs=[pl.BlockSpec((tm, 256), lambda i: (i, 0))] + [groups] * 3
        + [pl.BlockSpec((tm, rope[0].shape[1]), lambda i: (i, 0)), pl.BlockSpec(rope[1].shape, lambda i: (0, 0, 0))],
        out_specs=pl.BlockSpec((tm, n_proj), lambda i: (i, 0)),
        out_shape=jax.ShapeDtypeStruct((s_len, n_proj), BF16),
        compiler_params=_params(("arbitrary",)),
    )(du, *dqkv, *rope)


def _inproj_bwd(dproj, w_in_t, x, dx1, sc_m, g_pre_mix, tm):
    s_len, d = x.shape
    n_proj = w_in_t.shape[0]
    n_t = s_len // tm

    def body(dproj_ref, w_ref, x_ref, dx1_ref, sc_ref, g_ref, dx_ref, sums_ref):
        i = pl.program_id(0)
        halves = [slice(0, tm // 2), slice(tm // 2, tm)]
        dhs = [_dot(dproj_ref[rs, :], w_ref[...], NN) for rs in halves]
        sums = None
        for rs, dh in zip(halves, dhs):
            xv = x_ref[rs, :]
            r = _rstd(xv)
            n = xv * r
            dng = dh * (1.0 + sc_ref[...])
            dx_ref[rs, :] = dx1_ref[rs, :] + _norm_bwd(dng * g_ref[...], n, r)
            part = jnp.concatenate([jnp.sum(dh, axis=0, keepdims=True), jnp.sum(dh * (n * g_ref[...]), axis=0, keepdims=True),
                                    jnp.sum(dng * n, axis=0, keepdims=True), jnp.zeros((5, d), F32)], axis=0)
            sums = part if sums is None else sums + part

        @pl.when(i == 0)
        def _():
            sums_ref[...] = sums

        @pl.when(i > 0)
        def _():
            sums_ref[...] += sums

    tile = lambda w: pl.BlockSpec((tm, w), lambda i: (i, 0))
    vec = pl.BlockSpec((1, d), lambda i: (0, 0))
    return pl.pallas_call(
        body, name="inproj_bwd", grid=(n_t,),
        in_specs=[tile(n_proj), pl.BlockSpec((n_proj, d), lambda i: (0, 0)), tile(d), tile(d), vec, vec],
        out_specs=[tile(d), pl.BlockSpec((8, d), lambda i: (0, 0))],
        out_shape=[jax.ShapeDtypeStruct((s_len, d), F32), jax.ShapeDtypeStruct((8, d), F32)],
        compiler_params=_params(("arbitrary",)),
    )(dproj, w_in_t, x, dx1, sc_m, g_pre_mix)


def _wgrad(a, b, name, tk, tmm):
    s_len, m = a.shape
    n = b.shape[1]
    n_k = s_len // tk

    def body(a_ref, b_ref, o_ref, acc_ref):
        k = pl.program_id(1)
        part = _dot(a_ref[...], b_ref[...], TN)

        @pl.when(k == 0)
        def _():
            acc_ref[...] = part

        @pl.when(k > 0)
        def _():
            acc_ref[...] += part

        @pl.when(k == n_k - 1)
        def _():
            o_ref[...] = acc_ref[...].astype(BF16)

    return pl.pallas_call(
        body, name=name, grid=(m // tmm, n_k),
        in_specs=[pl.BlockSpec((tk, tmm), lambda j, k: (k, j)), pl.BlockSpec((tk, n), lambda j, k: (k, 0))],
        out_specs=pl.BlockSpec((tmm, n), lambda j, k: (j, 0)),
        out_shape=jax.ShapeDtypeStruct((m, n), BF16),
        scratch_shapes=[pltpu.VMEM((tmm, n), F32)],
        compiler_params=_params(("arbitrary", "arbitrary")),
    )(a, b)


def _place():
    return lax.axis_index("x"), lax.axis_index("y"), lax.axis_index("c")


def _peer(k):
    x, y, c = _place()
    bx, by, bc = (k >> 2) & 1, (k >> 1) & 1, k & 1
    return (x ^ bx if bx else x, y ^ by if by else y, c ^ bc if bc else c)


def _index(pos):
    return 4 * pos[0] + 2 * pos[1] + pos[2]


def _entry_exchange(c_rows, w_ada, b_ada_cols, taps, shards):
    d = c_rows.shape[1]
    ncol = w_ada.shape[1]
    n_w = len(shards)

    def body(c_ref, w_ref, b_ref, t_ref, *rest):
        srcs = rest[:n_w]
        call_ref, mod_ref, tall_ref = rest[n_w:n_w + 3]
        outs = rest[n_w + 3:2 * n_w + 3]
        stage_ref, s_send, s_recv, w_send, w_recv, local_sems = rest[2 * n_w + 3:]
        x, y, c = _place()
        here, sibling = (x, y, c), (x, y, 1 - c)
        chips = [(1 - x, y), (x, 1 - y), (1 - x, 1 - y)]
        me = _index(here)

        def small(kind, src, dst, k):
            return pltpu.make_async_remote_copy(src_ref=src, dst_ref=dst, send_sem=s_send.at[kind, k - 1],
                                                recv_sem=s_recv.at[kind, k - 1], device_id=_peer(k), device_id_type=MESH)

        gather = lambda k: small(0, c_ref, call_ref.at[me], k)
        scatter = lambda k: small(1, stage_ref.at[_index(_peer(k))], mod_ref.at[me], k)
        gather_taps = lambda k: small(2, t_ref, tall_ref.at[me], k)

        def rows(w, pos):
            r = shards[w].shape[0]
            return outs[w].at[pl.ds(pl.multiple_of(_index(pos) * r, 16), r), :]

        def block(k, w, pos, to, own=False):
            return pltpu.make_async_remote_copy(
                src_ref=srcs[w] if own else rows(w, pos), dst_ref=rows(w, pos),
                send_sem=w_send.at[k, w], recv_sem=w_recv.at[k, w], device_id=to, device_id_type=MESH)

        call_ref[me] = c_ref[...]
        tall_ref[me] = t_ref[...]
        for k in range(1, N_DEV):
            gather(k).start()
        for k in range(1, N_DEV):
            gather_taps(k).start()
        mine = [pltpu.make_async_copy(srcs[w], rows(w, here), local_sems.at[w]) for w in range(n_w)]
        for cp in mine:
            cp.start()
        first = [block(0, w, here, sibling, own=True) for w in range(n_w)]
        first += [block(1 + j, w, here, (*chip, c), own=True) for j, chip in enumerate(chips) for w in range(n_w)]
        for cp in first:
            cp.start()

        for k in range(1, N_DEV):
            gather(k).wait_recv()
        cv = jnp.concatenate([call_ref[b, 0:1, :] for b in range(N_DEV)], axis=0)
        act = cv * jax.nn.sigmoid(cv)
        mod = lax.dot_general(act, w_ref[...], NN, preferred_element_type=F32,
                              precision=lax.Precision.HIGHEST) + b_ref[...]
        for b in range(N_DEV):
            stage_ref[b] = jnp.broadcast_to(mod[b:b + 1, :], (8, ncol))
        mod_ref[me] = stage_ref[me]
        for k in range(1, N_DEV):
            scatter(k).start()

        passed = []
        for j, chip in enumerate(chips):
            for w in range(n_w):
                block(1 + j, w, (*chip, c), here).wait_recv()
                fwd = block(4 + j, w, (*chip, c), sibling)
                fwd.start()
                passed.append(fwd)
        for w in range(n_w):
            block(0, w, sibling, here).wait_recv()
        for j, chip in enumerate(chips):
            for w in range(n_w):
                block(4 + j, w, (*chip, 1 - c), here).wait_recv()
        for k in range(1, N_DEV):
            scatter(k).wait_recv()
            gather_taps(k).wait_recv()
        for cp in first + passed:
            cp.wait_send()
        for k in range(1, N_DEV):
            gather(k).wait_send()
            scatter(k).wait_send()
            gather_taps(k).wait_send()
        for cp in mine:
            cp.wait()

    vmem, hbm = pl.BlockSpec(memory_space=pltpu.VMEM), pl.BlockSpec(memory_space=pltpu.HBM)
    out = pl.pallas_call(
        body, name="entry_exchange",
        in_specs=[vmem] * 4 + [hbm] * n_w, out_specs=[vmem] * 3 + [hbm] * n_w,
        out_shape=[jax.ShapeDtypeStruct((N_DEV, 8, d), F32), jax.ShapeDtypeStruct((N_DEV, 8, ncol), F32),
                   jax.ShapeDtypeStruct((N_DEV,) + taps.shape, F32)]
        + [jax.ShapeDtypeStruct((N_DEV * s.shape[0], s.shape[1]), s.dtype) for s in shards],
        scratch_shapes=[pltpu.VMEM((N_DEV, 8, ncol), F32), pltpu.SemaphoreType.DMA((3, N_DEV - 1)),
                        pltpu.SemaphoreType.DMA((3, N_DEV - 1)), pltpu.SemaphoreType.DMA((N_DEV - 1, n_w)),
                        pltpu.SemaphoreType.DMA((N_DEV - 1, n_w)), pltpu.SemaphoreType.DMA((n_w,))],
        compiler_params=_params(),
    )(c_rows, w_ada, b_ada_cols, taps, *shards)
    return out[0], out[1], out[2], out[3:]


def _peer_copies(mode, srcs, lands, send_sems, recv_sems):
    if mode in ("gather_ici", "gather_d2d"):
        x, y, c = _place()
        sibling = (x, y, 1 - c)
        chips = [(1 - x, y), (x, 1 - y), (1 - x, 1 - y)]
        n = len(lands)

        def rows(w, pos):
            r = lands[w].shape[0] // N_DEV
            return lands[w].at[pl.ds(pl.multiple_of(_index(pos) * r, 16), r), :]

        def copy(k, w, src, dst, to):
            return pltpu.make_async_remote_copy(src_ref=src, dst_ref=dst, send_sem=send_sems.at[k * n + w],
                                                recv_sem=recv_sems.at[k * n + w], device_id=to, device_id_type=MESH)

        if mode == "gather_ici":
            targets = [sibling] + [(*chip, c) for chip in chips]
            return [copy(k, w, srcs[w], rows(w, (x, y, c)), to) for k, to in enumerate(targets) for w in range(n)]
        return [copy(j, w, rows(w, (*chip, c)), rows(w, (*chip, c)), sibling)
                for j, chip in enumerate(chips) for w in range(n)]
    me = _index(_place())
    copies = []
    for k in range(1, N_DEV):
        peer = _peer(k)
        for w, (src, land) in enumerate(zip(srcs, lands)):
            if mode == "gather":
                r = src.shape[0]
                dst = land.at[pl.ds(pl.multiple_of(me * r, 16), r), :]
            elif mode == "allgather":
                dst = land.at[me]
            else:
                r = src.shape[0] // N_DEV
                src = src.at[pl.ds(pl.multiple_of(_index(peer) * r, 16), r), :]
                dst = land.at[me]
            copies.append(pltpu.make_async_remote_copy(
                src_ref=src, dst_ref=dst, send_sem=send_sems.at[(k - 1) * len(srcs) + w],
                recv_sem=recv_sems.at[(k - 1) * len(srcs) + w],
                device_id=peer, device_id_type=MESH))
    return copies


def _landing_zone(mode, src, me, name):
    cols = src.shape[1]
    if mode == "gather":
        r = src.shape[0]
        in_spec = pl.BlockSpec((r, cols), lambda i, me_ref: (0, 0))
        out_spec = pl.BlockSpec((r, cols), lambda i, me_ref: (me_ref[0], 0))
        out_shape = jax.ShapeDtypeStruct((N_DEV * r, cols), src.dtype)
    else:
        r = src.shape[0] // N_DEV
        in_spec = pl.BlockSpec((r, cols), lambda i, me_ref: (me_ref[0], 0))
        out_spec = pl.BlockSpec((1, r, cols), lambda i, me_ref: (me_ref[0], 0, 0))
        out_shape = jax.ShapeDtypeStruct((N_DEV, r, cols), src.dtype)

    def body(me_ref, s_ref, o_ref):
        o_ref[...] = s_ref[...].reshape(o_ref.shape)

    return pl.pallas_call(
        body, name=name, out_shape=out_shape,
        grid_spec=pltpu.PrefetchScalarGridSpec(num_scalar_prefetch=1, grid=(1,), in_specs=[in_spec], out_specs=out_spec),
        compiler_params=_params(("arbitrary",)),
    )(me.reshape(1).astype(jnp.int32), src)


def _exchange_start(mode, srcs, lands, name):
    n_s, n_a = len(srcs), len(srcs) + len(lands)
    n_cp = _COPIES_PER_ARRAY.get(mode, N_DEV - 1) * len(lands)

    def body(*refs):
        for cp in _peer_copies(mode, refs[:n_s], refs[n_s:n_a], refs[n_a], refs[n_a + 1]):
            cp.start()
        refs[-1][...] = jnp.zeros_like(refs[-1])

    hbm, sem = pl.BlockSpec(memory_space=pltpu.HBM), pl.BlockSpec(memory_space=pltpu.SEMAPHORE)
    arrays = list(srcs) + list(lands)
    out = pl.pallas_call(
        body, name=name,
        out_shape=(pltpu.SemaphoreType.DMA((n_cp,)), pltpu.SemaphoreType.DMA((n_cp,)),
                   *[pltpu.HBM(a.shape, a.dtype) for a in arrays], jax.ShapeDtypeStruct((8, LANES), F32)),
        in_specs=[hbm] * n_a, out_specs=(sem, sem, *[hbm] * n_a, pl.BlockSpec(memory_space=pltpu.VMEM)),
        input_output_aliases={i: 2 + i for i in range(n_a)},
        compiler_params=pltpu.CompilerParams(has_side_effects=pltpu.SideEffectType.DATAFLOW_SIDE_EFFECTING),
    )(*[pltpu.with_memory_space_constraint(a, pltpu.HBM) for a in arrays])
    return out[0], out[1], out[2:2 + n_s], out[2 + n_s:2 + n_a], out[-1]


_COPIES_PER_ARRAY = {"gather_ici": 4, "gather_d2d": 3}


def _exchange_wait(mode, send_sems, recv_sems, srcs, lands, after, name):
    n_s, n_a = len(srcs), len(srcs) + len(lands)

    def body(*refs):
        copies = _peer_copies(mode, refs[:n_s], refs[n_s:n_a], refs[n_a], refs[n_a + 1])
        for cp in copies:
            cp.wait_send()
        for cp in copies:
            cp.wait_recv()

    hbm, sem = pl.BlockSpec(memory_space=pltpu.HBM), pl.BlockSpec(memory_space=pltpu.SEMAPHORE)
    arrays = list(srcs) + list(lands)
    out = pl.pallas_call(
        body, name=name, out_shape=tuple(pltpu.HBM(a.shape, a.dtype) for a in arrays),
        in_specs=[hbm] * n_a + [sem, sem, pl.BlockSpec(memory_space=pl.ANY)], out_specs=tuple([hbm] * n_a),
        input_output_aliases={i: i for i in range(n_a)},
        compiler_params=pltpu.CompilerParams(has_side_effects=pltpu.SideEffectType.DATAFLOW_SIDE_EFFECTING),
    )(*arrays, send_sems, recv_sems, after)
    return out[n_s:]


SMALL_WEIGHTS = ("b_ada", "g_pre_mix", "g_post_mix", "g_pre_ffn", "g_post_ffn", "w_pool", "b_pool", "pool_scale", "conv_b")


MOD_ROWS = ((0, 0), (0, 1), (1, 3), (1, 0), (1, 1), (2, 0))


def _small_sum(mine, gathered):
    n_l = len(mine)
    d = mine[0].shape[1]

    def body(*refs):
        loc, got = refs[:n_l], refs[n_l:2 * n_l]
        tot_refs, dmod_ref = refs[2 * n_l:3 * n_l], refs[3 * n_l]
        me = _index(_place())
        part = lambda a, dev: jnp.where(dev == me, loc[a][...], got[a][dev])
        for a in range(n_l):
            tot = part(a, 0)
            for dev in range(1, N_DEV):
                tot = tot + part(a, dev)
            tot_refs[a][...] = tot
        for dev in range(N_DEV):
            for k, (a, r) in enumerate(MOD_ROWS):
                dmod_ref[dev:dev + 1, k * d:(k + 1) * d] = part(a, dev)[r:r + 1, :]

    vmem = pl.BlockSpec(memory_space=pltpu.VMEM)
    out = pl.pallas_call(
        body, name="small_sum", in_specs=[vmem] * (2 * n_l), out_specs=[vmem] * (n_l + 1),
        out_shape=[jax.ShapeDtypeStruct(a.shape, F32) for a in mine] + [jax.ShapeDtypeStruct((N_DEV, 6 * d), F32)],
        compiler_params=_params(),
    )(*mine, *gathered)
    return out[:n_l], out[n_l]


def _small_adam(totals, weights, moms, vels):
    n_t, n_w = len(totals), len(weights)

    def body(*refs):
        t_in, t_mix, t_ffn, t_pool, t_blk, t_conv, _ = (r[...] for r in refs[:n_t])
        w_refs, m_refs, v_refs = (refs[n_t + k * n_w:n_t + (k + 1) * n_w] for k in range(3))
        outs = refs[n_t + 3 * n_w:]

        def update(idx, g, at=()):
            sel = lambda ref: ref.at[at] if at else ref
            delta, nm, nv = _adam_math(sel(w_refs[idx])[...], g, sel(m_refs[idx])[...], sel(v_refs[idx])[...])
            for k, val in enumerate((g, delta, nm, nv)):
                sel(outs[4 * idx + k])[...] = val

        tots = (t_in, t_mix, t_ffn)
        update(0, jnp.concatenate([tots[a][r:r + 1] for a, r in MOD_ROWS], axis=1))
        update(1, t_in[2:3])
        update(2, t_mix[4:5])
        update(3, t_mix[2:3])
        update(4, t_ffn[1:2])
        for gi in range(len(POOL_WINDOWS)):
            update(5, t_blk[gi], at=(0, gi))
        update(6, jnp.concatenate([t_pool[0:1, gi * HEAD_DIM:(gi + 1) * HEAD_DIM] for gi in range(len(POOL_WINDOWS))], axis=0),
               at=(0,))
        update(7, t_pool[1:2])
        update(8, t_conv[3:4])

    vmem = pl.BlockSpec(memory_space=pltpu.VMEM)
    return pl.pallas_call(
        body, name="small_adam", in_specs=[vmem] * (n_t + 3 * n_w), out_specs=[vmem] * (4 * n_w),
        out_shape=[jax.ShapeDtypeStruct(w.shape, F32) for w in weights for _ in range(4)],
        compiler_params=_params(),
    )(*totals, *weights, *moms, *vels)


def _adam_math(w, g, m, v):
    m = ADAM_B1 * m + (1.0 - ADAM_B1) * g
    v = ADAM_B2 * v + (1.0 - ADAM_B2) * (g * g)
    m_hat = m / (1.0 - ADAM_B1 ** ADAM_STEP)
    v_hat = v / (1.0 - ADAM_B2 ** ADAM_STEP)
    delta = -ADAM_LR * (m_hat / (jnp.sqrt(v_hat) + ADAM_EPS) + ADAM_WD * w)
    return delta, m, v


def _adam(w, g, m, v, name, tr):
    rows, cols = w.shape

    def body(w_ref, g_ref, m_ref, v_ref, d_ref, nm_ref, nv_ref):
        d_ref[...], nm_ref[...], nv_ref[...] = _adam_math(w_ref[...], g_ref[...], m_ref[...], v_ref[...])

    spec = pl.BlockSpec((tr, cols), lambda i: (i, 0))
    shape = jax.ShapeDtypeStruct((rows, cols), F32)
    return pl.pallas_call(
        body, name=name, grid=(rows // tr,), in_specs=[spec] * 4, out_specs=[spec] * 3,
        out_shape=[shape] * 3, compiler_params=_params(("arbitrary",)),
    )(w, g, m, v)


def _sum_adam(parts, w, m, v, name, tr):
    _, rows, cols = parts.shape

    def body(p_ref, w_ref, m_ref, v_ref, g_ref, d_ref, nm_ref, nv_ref):
        g = p_ref[0].astype(F32)
        for dev in range(1, N_DEV):
            g = g + p_ref[dev].astype(F32)
        g_ref[...] = g
        d_ref[...], nm_ref[...], nv_ref[...] = _adam_math(w_ref[...], g, m_ref[...], v_ref[...])

    spec = pl.BlockSpec((tr, cols), lambda i: (i, 0))
    shape = jax.ShapeDtypeStruct((rows, cols), F32)
    return pl.pallas_call(
        body, name=name, grid=(rows // tr,),
        in_specs=[pl.BlockSpec((N_DEV, tr, cols), lambda i: (0, i, 0)), spec, spec, spec],
        out_specs=[spec] * 4, out_shape=[shape] * 4, compiler_params=_params(("arbitrary",)),
    )(parts, w, m, v)


def _ada_grad_adam(c_all, dmod_cols, w, m, v, tr):
    rows, cols = w.shape

    def body(c_ref, dm_ref, w_ref, m_ref, v_ref, g_ref, d_ref, nm_ref, nv_ref):
        cv = c_ref[...]
        act = cv * jax.nn.sigmoid(cv)
        g = lax.dot_general(act, dm_ref[...], TN, preferred_element_type=F32, precision=lax.Precision.HIGHEST)
        g_ref[...] = g
        d_ref[...], nm_ref[...], nv_ref[...] = _adam_math(w_ref[...], g, m_ref[...], v_ref[...])

    spec = pl.BlockSpec((tr, cols), lambda i: (i, 0))
    shape = jax.ShapeDtypeStruct((rows, cols), F32)
    return pl.pallas_call(
        body, name="ada_grad_adam", grid=(rows // tr,),
        in_specs=[pl.BlockSpec((N_DEV, tr), lambda i: (0, i)), pl.BlockSpec((N_DEV, cols), lambda i: (0, 0)), spec, spec, spec],
        out_specs=[spec] * 4, out_shape=[shape] * 4, compiler_params=_params(("arbitrary",)),
    )(c_all, dmod_cols, w, m, v)


def _rope_tables(positions):
    inv_freq = ROPE_THETA ** (-jnp.arange(0, 2 * ROT_HALF, 2, dtype=F32) / (2 * ROT_HALF))
    ang = positions.astype(F32)[:, None] * inv_freq
    rows = jnp.concatenate([jnp.cos(ang), jnp.sin(ang), jnp.ones_like(ang)], axis=1)
    spread = [[[0.0] * LANES for _ in range(3 * ROT_HALF)] for _ in range(3)]
    for lane in range(LANES):
        p, j = lane % HEAD_DIM, lane % ROT_HALF
        if p < ROT_HALF:
            spread[0][j][lane] = 1.0
            spread[1][ROT_HALF + j][lane] = -1.0
        elif p < 2 * ROT_HALF:
            spread[0][j][lane] = 1.0
            spread[2][ROT_HALF + j][lane] = 1.0
        else:
            spread[0][2 * ROT_HALF][lane] = 1.0
    return rows, jnp.array(spread, F32)


def _pad_rows(a, rows):
    return jnp.pad(a, ((0, rows - a.shape[0]), (0, 0)))


def _sequence_step(xs, target, rope, mods, gains, w_in_t, w_out_t, relay_ffn, fetch_ffn, send_ffn_grads, send_mix_grads, w_blk_b, b_pool_r,
                   pool_scale_r, conv_w_all, conv_b):
    sh_m, sc_m, gt_m, sh_f, sc_f, gt_f = mods
    g_pre_mix, g_post_mix, g_pre_ffn, g_post_ffn = gains
    h1, u_pool, qkv = _premix_inproj(xs, sh_m, sc_m, g_pre_mix, w_in_t, rope, tm=512)
    o_g, lse_g = _attn_fwd(qkv)
    x1, y1, h2, cat, attn, lse_all = _mix_out(xs, u_pool, o_g, lse_g, w_blk_b, b_pool_r, pool_scale_r, w_out_t,
                                              gt_m, g_post_mix, g_pre_ffn, sc_f, sh_f, tm=256)
    token = relay_ffn(x1)
    w_up_t, w_down_f = fetch_ffn(x1 if token is None else token)
    gate, a_ffn, act, vd, dy2, dout, sums_ffn, loss_loc = _ffn_fwd_loss(h2, x1, target, w_up_t, w_down_f, conv_w_all, conv_b,
                                                              gt_f, g_post_ffn, tm=256, ck=256)

    dgc, dval, dw_down, dconv = _ffn_bwd_act(dy2, gate, a_ffn, act, vd, w_down_f, tm=512, tf=1408, ck=256)
    dup, dx1, dpool, dattn, delta, dw_out_t, sums_mix = _ffn_up_mix_bwd(
        dgc, dval, w_up_t, conv_w_all, dout, x1, y1, cat, attn, w_out_t, sc_f, g_pre_ffn, gt_m, g_post_mix, tm=256)
    dw_up_t = _wgrad(dup, h2, "wgrad_up", tk=2048, tmm=1408)
    token = send_ffn_grads(dw_up_t, dw_down)
    du, dw_blk, sums_pool = _pool_bwd(dpool, u_pool, w_blk_b, b_pool_r,
                                      pool_scale_r if token is None else pool_scale_r + token[0:1, 0:1], tm=512)
    dproj = _dproj_assemble(du, _attn_bwd(qkv, dattn, lse_all, delta), rope, tm=512)
    dw_in_t = _wgrad(dproj, h1, "wgrad_in", tk=2048, tmm=1280)
    token = send_mix_grads(dw_in_t, dw_out_t)
    if token is not None:
        sc_m = sc_m + token[0:1, 0:1]
    grad_x, sums_in = _inproj_bwd(dproj, w_in_t, xs, dx1, sc_m, g_pre_mix, tm=256)
    return (loss_loc, grad_x, dw_in_t, dw_out_t, dw_up_t, dw_down, dw_blk, dconv,
            sums_in, sums_mix, sums_ffn, sums_pool)


def kernel(x, c, positions, w_ada, b_ada, g_pre_mix, g_post_mix, g_pre_ffn, g_post_ffn, w_in, w_pool, b_pool, pool_scale, w_out, w_up, conv_w, conv_b, w_down, loss_target, m_w_ada, m_b_ada, m_g_pre_mix, m_g_post_mix, m_g_pre_ffn, m_g_post_ffn, m_w_in, m_w_pool, m_b_pool, m_pool_scale, m_w_out, m_w_up, m_conv_w, m_conv_b, m_w_down, v_w_ada, v_b_ada, v_g_pre_mix, v_g_post_mix, v_g_pre_ffn, v_g_post_ffn, v_w_in, v_w_pool, v_b_pool, v_pool_scale, v_w_out, v_w_up, v_conv_w, v_conv_b, v_w_down):
    s_len, d = x.shape[1], x.shape[2]
    d_ff = w_down.shape[1] * N_DEV
    me = _index(_place())
    xs, target = x[0], loss_target[0]

    ncol = w_ada.shape[2]
    b_cols = lax.dynamic_slice(b_ada, (0, me * ncol), (1, ncol))
    c_all, mod, taps_all, (w_in_t, w_out_t) = _entry_exchange(
        jnp.broadcast_to(c, (8, d)), w_ada[0], b_cols, _pad_rows(conv_w[0], 8),
        [w_in[0].T.astype(BF16), w_out[0].T.astype(BF16)])
    c_all = c_all[:, 0, :]
    conv_w_all = jnp.transpose(taps_all[:, :3, :], (1, 0, 2)).reshape(3, d_ff)
    sh_m, sc_m, gt_m, sh_f, sc_f, gt_f = [mod[:, 0, :].reshape(1, -1)[:, k * d:(k + 1) * d] for k in range(6)]

    rope = _rope_tables(positions[0])
    w_blk = jnp.zeros((256, 256), F32)
    for gi in range(4):
        w_blk = lax.dynamic_update_slice(w_blk, w_pool[0, gi], (gi * HEAD_DIM, gi * HEAD_DIM))
    w_blk_b = w_blk.astype(BF16)
    b_pool_r, pool_scale_r = b_pool.reshape(1, 256), pool_scale.reshape(1, 256)

    up_sh, down_sh = w_up[0].T.astype(BF16), w_down[0].astype(BF16)
    w_in_t, conv_w_all, up_sh, down_sh = lax.optimization_barrier((w_in_t, conv_w_all, up_sh, down_sh))
    lands = [_landing_zone("gather", s, me, "land_" + nm) for s, nm in ((up_sh, "w_up"), (down_sh, "w_down"))]
    w_send, w_recv, w_src, w_land, w_token = _exchange_start("gather_ici", [up_sh, down_sh], lands, "ffn_weights_ici_start")
    relay = []

    def relay_ffn(after):
        arrived = _exchange_wait("gather_ici", w_send, w_recv, w_src, w_land, after, "ffn_weights_ici_wait")
        relay.extend(_exchange_start("gather_d2d", [], arrived, "ffn_weights_d2d_start"))
        return relay[4]

    def fetch_ffn(after):
        return _exchange_wait("gather_d2d", relay[0], relay[1], [], relay[3], after, "ffn_weights_d2d_wait")

    flight = []

    def send_ffn_grads(dw_up_t, dw_down):
        lands = [_landing_zone("scatter", dw_up_t, me, "land_dw_up"), _landing_zone("scatter", dw_down, me, "land_dw_down")]
        flight.extend(_exchange_start("scatter", [dw_up_t, dw_down], lands, "ffn_grads_start"))
        return flight[4]

    mix_flight = []

    def send_mix_grads(dw_in_t, dw_out_t):
        lands = [_landing_zone("scatter", dw_in_t, me, "land_dw_in"), _landing_zone("scatter", dw_out_t, me, "land_dw_out")]
        mix_flight.extend(_exchange_start("scatter", [dw_in_t, dw_out_t], lands, "mix_grads_start"))
        return mix_flight[4]

    (loss_loc, grad_x, dw_in_t, dw_out_t, _, _, dw_pool, dconv,
     sums_in, sums_mix, sums_ffn, sums_pool) = _sequence_step(
        xs, target, rope, (sh_m + w_token[0:1, 0:1], sc_m, gt_m, sh_f, sc_f, gt_f),
        (g_pre_mix, g_post_mix, g_pre_ffn, g_post_ffn),
        w_in_t, w_out_t, relay_ffn, fetch_ffn, send_ffn_grads, send_mix_grads, w_blk_b, b_pool_r, pool_scale_r, conv_w_all, conv_b)

    small = [sums_in, sums_mix, sums_ffn, sums_pool, dw_pool, dconv, loss_loc]
    small_flight = _exchange_start("allgather", small, [lax.empty((N_DEV,) + a.shape, F32) for a in small], "small_start")

    parts_ffn = _exchange_wait("scatter", *flight[:4], small_flight[4], "ffn_grads_wait")
    big = {
        "w_up": [a.T for a in _sum_adam(parts_ffn[0], w_up[0].T, m_w_up[0].T, v_w_up[0].T, "adam_w_up", 64)],
        "w_down": _sum_adam(parts_ffn[1], w_down[0], m_w_down[0], v_w_down[0], "adam_w_down", 32),
    }

    rep_w = [b_ada, g_pre_mix, g_post_mix, g_pre_ffn, g_post_ffn, w_pool, b_pool, pool_scale, conv_b]
    rep_m = [m_b_ada, m_g_pre_mix, m_g_post_mix, m_g_pre_ffn, m_g_post_ffn, m_w_pool, m_b_pool, m_pool_scale, m_conv_b]
    rep_v = [v_b_ada, v_g_pre_mix, v_g_post_mix, v_g_pre_ffn, v_g_post_ffn, v_w_pool, v_b_pool, v_pool_scale, v_conv_b]
    parts_mix = _exchange_wait("scatter", *mix_flight[:4], big["w_down"][0], "mix_grads_wait")
    big["w_in"] = [a.T for a in _sum_adam(parts_mix[0], w_in[0].T, m_w_in[0].T, v_w_in[0].T, "adam_w_in", 64)]
    big["w_out"] = [a.T for a in _sum_adam(parts_mix[1], w_out[0].T, m_w_out[0].T, v_w_out[0].T, "adam_w_out", 128)]
    gathered = _exchange_wait("allgather", *small_flight[:4], big["w_out"][0], "small_wait")
    totals, dmod_all = _small_sum(small, gathered)
    dconv_tot, loss_tot = totals[5], totals[6]
    rep_out = _small_adam(totals, rep_w, rep_m, rep_v)
    g_rep, d_rep, nm_rep, nv_rep = (rep_out[k::4] for k in range(4))

    fcol = d_ff // N_DEV
    g_cw = lax.dynamic_slice(dconv_tot, (0, me * fcol), (3, fcol))
    d_cw, nm_cw, nv_cw = _adam(conv_w[0], g_cw, m_conv_w[0], v_conv_w[0], "adam_conv_w", 3)

    dmod_cols = lax.dynamic_slice(dmod_all, (0, me * ncol), (N_DEV, ncol))
    g_ada, d_ada, nm_ada, nv_ada = _ada_grad_adam(c_all, dmod_cols, w_ada[0], m_w_ada[0], v_w_ada[0], 256)

    loss = loss_tot[0, 0]

    def group(k):
        rep = (g_rep, d_rep, nm_rep, nv_rep)[k]
        ada = (g_ada, d_ada, nm_ada, nv_ada)[k][None]
        cw = (g_cw, d_cw, nm_cw, nv_cw)[k][None]
        return [ada, rep[0], rep[1], rep[2], rep[3], rep[4], big["w_in"][k][None], rep[5], rep[6], rep[7],
                big["w_out"][k][None], big["w_up"][k][None], cw, rep[8], big["w_down"][k][None]]

    return (loss, grad_x[None], *group(0), *group(1), *group(2), *group(3))
```

```python
import functools
import math

import jax
import jax.numpy as jnp
from jax import lax
from jax.experimental import pallas as pl
from jax.experimental.pallas import tpu as pltpu

F32 = jnp.float32
BF16 = jnp.bfloat16
MESH = pl.DeviceIdType.MESH

N_DEV = 8
HEAD_DIM = 64
ROT_HALF = 8
ROPE_THETA = 500000.0
POOL_WINDOWS = (2, 4, 8, 16)
DILATIONS = (1, 4, 16)
BLOCK = 128
NORM_EPS = 1e-6
HALO = 16
MASKED = -1e30
ATTN_FWD_UNROLL = 8
ATTN_BWD_UNROLL = 8

ADAM_LR = 0.001
ADAM_B1 = 0.9
ADAM_B2 = 0.999
ADAM_EPS = 1e-08
ADAM_WD = 0.01
ADAM_STEP = 10

V7X_VMEM_LIMIT = 56 * 1024 * 1024
LANES = 128

NT = (((1,), (1,)), ((), ()))
NN = (((1,), (0,)), ((), ()))
TN = (((0,), (0,)), ((), ()))


def _dot(a, b, dims):
    return lax.dot_general(a, b, dims, preferred_element_type=F32)


def _params(sem=None, vmem=V7X_VMEM_LIMIT):
    if sem is None:
        return pltpu.CompilerParams(vmem_limit_bytes=vmem)
    return pltpu.CompilerParams(dimension_semantics=sem, vmem_limit_bytes=vmem)


def _rstd(v):
    return lax.rsqrt(jnp.mean(v * v, axis=-1, keepdims=True) + NORM_EPS)


def _norm_bwd(dn, n, rstd):
    return rstd * (dn - n * jnp.mean(dn * n, axis=-1, keepdims=True))


def _rope_lanes(cs_ref, spread_ref):
    return [lax.dot_general(cs_ref[...], spread_ref[k], NN, preferred_element_type=F32, precision=lax.Precision.HIGHEST)
            for k in range(3)]


def _rope_fwd(p, lanes):
    return p * lanes[0] + pltpu.roll(p, LANES - ROT_HALF, 1) * lanes[1] + pltpu.roll(p, ROT_HALF, 1) * lanes[2]


def _rope_bwd(dp, lanes):
    return dp * lanes[0] + pltpu.roll(dp * lanes[1], ROT_HALF, 1) + pltpu.roll(dp * lanes[2], LANES - ROT_HALF, 1)


def _gelu_parts(v):
    k2 = 2.0 * math.sqrt(2.0 / math.pi)
    c = 0.044715
    v2 = v * v
    s = jax.nn.sigmoid(v * (k2 + (k2 * c) * v2))
    g = v * s
    dg = s + g * (1.0 - s) * (k2 + (3.0 * k2 * c) * v2)
    return g, dg


def _halo_before(i, tile):
    return jnp.maximum(i * (tile // HALO) - 1, 0)


def _premix_inproj(x, sh, sc, g, w_in_t, rope, tm):
    s_len, d = x.shape
    n_proj = w_in_t.shape[0]
    n_slab = (n_proj - 256) // LANES

    def body(x_ref, sh_ref, sc_ref, g_ref, w_ref, cs_ref, spread_ref, h_ref, up_ref, qkv_ref):
        xv = x_ref[...]
        h = (xv * _rstd(xv) * g_ref[...]) * (1.0 + sc_ref[...]) + sh_ref[...]
        hb = h.astype(BF16)
        h_ref[...] = hb
        up_ref[...] = _dot(hb, w_ref[0:256, :], NT)
        lanes = _rope_lanes(cs_ref, spread_ref)
        for pair in range(n_slab // 2):
            p = _dot(hb, w_ref[256 + 256 * pair:512 + 256 * pair, :], NT)
            for half in range(2):
                ph = p[:, half * LANES:(half + 1) * LANES]
                if pair < 6:
                    ph = _rope_fwd(ph, lanes)
                if pair < 3:
                    ph = ph * (HEAD_DIM ** -0.5)
                qkv_ref[2 * pair + half] = ph

    vec = pl.BlockSpec((1, d), lambda i: (0, 0))
    return pl.pallas_call(
        body, name="premix_inproj", grid=(s_len // tm,),
        in_specs=[pl.BlockSpec((tm, d), lambda i: (i, 0)), vec, vec, vec,
                  pl.BlockSpec((n_proj, d), lambda i: (0, 0)),
                  pl.BlockSpec((tm, rope[0].shape[1]), lambda i: (i, 0)), pl.BlockSpec(rope[1].shape, lambda i: (0, 0, 0))],
        out_specs=[pl.BlockSpec((tm, d), lambda i: (i, 0)),
                   pl.BlockSpec((tm, 256), lambda i: (i, 0)),
                   pl.BlockSpec((n_slab, tm, LANES), lambda i: (0, i, 0))],
        out_shape=[jax.ShapeDtypeStruct((s_len, d), BF16),
                   jax.ShapeDtypeStruct((s_len, 256), F32),
                   jax.ShapeDtypeStruct((n_slab, s_len, LANES), F32)],
        compiler_params=_params(("arbitrary",)),
    )(x, sh, sc, g, w_in_t, *rope)


def _block_rows(n, r, dil):
    start = n * (BLOCK * dil) + r
    if dil == 1:
        return pl.ds(pl.multiple_of(start, BLOCK), BLOCK)
    return pl.ds(start, BLOCK, stride=dil)


def _band_mask(n):
    ri = lax.broadcasted_iota(jnp.int32, (BLOCK, 2 * BLOCK), 0)
    cj = lax.broadcasted_iota(jnp.int32, (BLOCK, 2 * BLOCK), 1)
    cur = (cj >= BLOCK) & (cj - BLOCK <= ri)
    prev = (cj < BLOCK) & (cj >= ri) & (n > 0)
    return cur | prev


def _attn_fwd(qkv):
    s_len = qkv.shape[1]
    n_g = len(DILATIONS)

    def body(q_ref, k_ref, v_ref, o_ref, lse_ref):
        lane = lax.broadcasted_iota(jnp.int32, (BLOCK, LANES), 1)
        first = lane < HEAD_DIM

        def group(dil):
            nb = s_len // (BLOCK * dil)

            def block(t, carry):
                r, n = t // nb, t % nb
                cur = _block_rows(n, r, dil)
                prev = _block_rows(jnp.maximum(n - 1, 0), r, dil)
                q = q_ref[0, cur, :]
                kcat = jnp.concatenate([k_ref[0, prev, :], k_ref[0, cur, :]], axis=0).astype(BF16)
                vcat = jnp.concatenate([v_ref[0, prev, :], v_ref[0, cur, :]], axis=0).astype(BF16)
                valid = _band_mask(n)
                q2 = jnp.concatenate([jnp.where(first, q, 0.0), jnp.where(first, 0.0, q)], axis=0).astype(BF16)
                s = jnp.where(jnp.concatenate([valid, valid], axis=0), _dot(q2, kcat, NT), MASKED)
                m = jnp.max(s, axis=-1, keepdims=True)
                p = jnp.exp(s - m)
                den = jnp.sum(p, axis=-1, keepdims=True)
                o2 = _dot(p.astype(BF16), vcat, NN) / den
                lse2 = m + jnp.log(den)
                o_ref[0, 0, cur, :] = jnp.where(first, o2[:BLOCK], o2[BLOCK:])
                lse_ref[0, 0, cur, :] = jnp.where(first, lse2[:BLOCK], lse2[BLOCK:])
                return carry

            lax.fori_loop(0, nb * dil, block, 0, unroll=ATTN_FWD_UNROLL)

        for gi, dil in enumerate(DILATIONS):
            pl.when(pl.program_id(0) == gi)(functools.partial(group, dil))

    def slab(base):
        return pl.BlockSpec((1, s_len, LANES), lambda g, s: (base + 2 * g + s, 0, 0))

    out = pl.BlockSpec((1, 1, s_len, LANES), lambda g, s: (g, s, 0, 0))
    shape = jax.ShapeDtypeStruct((n_g, 2, s_len, LANES), F32)
    return pl.pallas_call(
        body, name="attn_fwd", grid=(n_g, 2),
        in_specs=[slab(0), slab(6), slab(12)], out_specs=[out, out], out_shape=[shape, shape],
        compiler_params=_params(("arbitrary", "arbitrary")),
    )(qkv, qkv, qkv)


def _pool_mixed(u, halo, i, tm):
    ue = jnp.concatenate([halo, u], axis=0)
    s2 = ue + pltpu.roll(ue, 1, 0)
    s4 = s2 + pltpu.roll(s2, 2, 0)
    s8 = s4 + pltpu.roll(s4, 4, 0)
    s16 = s8 + pltpu.roll(s8, 8, 0)
    grp = lax.broadcasted_iota(jnp.int32, (tm, 256), 1) // HEAD_DIM
    pick = lambda a, b, c, e: jnp.where(grp == 0, a, jnp.where(grp == 1, b, jnp.where(grp == 2, c, e)))
    win_sum = pick(s2[HALO:], s4[HALO:], s8[HALO:], s16[HALO:])
    pos = (i * tm + lax.broadcasted_iota(jnp.int32, (tm, 256), 0)).astype(F32)
    count = jnp.minimum(pos + 1.0, pick(*[float(w) for w in POOL_WINDOWS]))
    return win_sum / count - u, count


def _mix_out(x, u_pool, o_g, lse_g, w_blk, b_pool, pool_scale, w_out_t, gt_m, g_post_mix, g_pre_ffn, sc_f, sh_f, tm):
    s_len, d = x.shape

    def body(x_ref, u_ref, uh_ref, o_ref, l_ref, wb_ref, bp_ref, ps_ref, wo_ref,
             gt_ref, g1_ref, g2_ref, sc_ref, sh_ref,
             x1_ref, y1_ref, h2_ref, cat_ref, attn_ref, lall_ref):
        (o0, o1, o2), (l0, l1, l2) = (o_ref.at[g] for g in range(3)), (l_ref.at[g] for g in range(3))
        i = pl.program_id(0)
        u = u_ref[...]
        halo = uh_ref[...] * (i > 0).astype(F32)
        mixed, _ = _pool_mixed(u, halo, i, tm)
        y = _dot(mixed.astype(BF16), wb_ref[...], NN) + bp_ref[...]
        pool = y * ps_ref[...]
        attn = []
        for s in range(2):
            la, lb, lc = l0[s], l1[s], l2[s]
            mx = jnp.maximum(jnp.maximum(la, lb), lc)
            ea, eb, ec = jnp.exp(la - mx), jnp.exp(lb - mx), jnp.exp(lc - mx)
            den = ea + eb + ec
            lall_ref[s] = mx + jnp.log(den)
            attn.append((ea / den) * o0[s] + (eb / den) * o1[s] + (ec / den) * o2[s])
        attn = jnp.concatenate(attn, axis=1)
        attn_ref[...] = attn
        cat = jnp.concatenate([pool, attn], axis=1).astype(BF16)
        cat_ref[...] = cat
        y1 = _dot(cat, wo_ref[...], NT)
        y1_ref[...] = y1.astype(BF16)
        x1 = x_ref[...] + gt_ref[...] * (y1 * _rstd(y1) * g1_ref[...])
        x1_ref[...] = x1
        h2 = (x1 * _rstd(x1) * g2_ref[...]) * (1.0 + sc_ref[...]) + sh_ref[...]
        h2_ref[...] = h2.astype(BF16)

    tile = lambda w: pl.BlockSpec((tm, w), lambda i: (i, 0))
    slab = pl.BlockSpec((2, tm, LANES), lambda i: (0, i, 0))
    groups = pl.BlockSpec((len(DILATIONS), 2, tm, LANES), lambda i: (0, 0, i, 0))
    const = lambda a: pl.BlockSpec(a.shape, lambda i: (0,) * a.ndim)
    return pl.pallas_call(
        body, name="mix_out", grid=(s_len // tm,),
        in_specs=[tile(d), tile(256), pl.BlockSpec((HALO, 256), lambda i: (_halo_before(i, tm), 0)),
                  groups, groups,
                  const(w_blk), const(b_pool), const(pool_scale), const(w_out_t),
                  const(gt_m), const(g_post_mix), const(g_pre_ffn), const(sc_f), const(sh_f)],
        out_specs=[tile(d), tile(d), tile(d), tile(512), tile(256), slab],
        out_shape=[jax.ShapeDtypeStruct((s_len, d), F32), jax.ShapeDtypeStruct((s_len, d), BF16),
                   jax.ShapeDtypeStruct((s_len, d), BF16), jax.ShapeDtypeStruct((s_len, 512), BF16),
                   jax.ShapeDtypeStruct((s_len, 256), F32), jax.ShapeDtypeStruct((2, s_len, LANES), F32)],
        compiler_params=_params(("arbitrary",)),
    )(x, u_pool, u_pool, o_g, lse_g, w_blk, b_pool, pool_scale, w_out_t, gt_m, g_post_mix, g_pre_ffn, sc_f, sh_f)


def _conv_gate(gate_ext, cw, cb):
    gc = gate_ext * cw[2:3, :] + pltpu.roll(gate_ext, 1, 0) * cw[1:2, :] + pltpu.roll(gate_ext, 2, 0) * cw[0:1, :]
    return gc[HALO:] + cb


def _ffn_fwd_loss(h2, x1, target, w_up_t, w_down, conv_w, conv_b, gt_f, g_post_ffn, tm, ck):
    s_len, d = x1.shape
    d_ff = w_down.shape[0]
    n_t, n_c = s_len // tm, d_ff // ck

    def body(h_ref, hh_ref, x1_ref, tgt_ref, wg_ref, wv_ref, wd_ref, cw_ref, cb_ref, gt_ref, g_ref,
             gate_ref, a_ref, act_ref, vd_ref, dy2_ref, dout_ref, sums_ref, loss_ref, acc_ref):
        i = pl.program_id(0)

        @pl.when(i == 0)
        def _():
            sums_ref[...] = jnp.zeros_like(sums_ref)
            loss_ref[...] = jnp.zeros_like(loss_ref)
            acc_ref[...] = jnp.zeros_like(acc_ref)

        def finish(live):
            y2 = acc_ref[...]
            rstd = _rstd(y2)
            n = y2 * rstd
            rn = n * g_ref[...]
            err = x1_ref[...] + gt_ref[...] * rn - tgt_ref[...]
            keep = lambda v: jnp.where(live, v, 0.0)
            loss_ref[...] += keep(0.5 * jnp.sum(jnp.mean(err * err, axis=-1, keepdims=True), axis=0, keepdims=True))
            dout = err * (1.0 / d)
            dout_ref[...] = dout
            drn = dout * gt_ref[...]
            sums_ref[0:1, :] += keep(jnp.sum(dout * rn, axis=0, keepdims=True))
            sums_ref[1:2, :] += keep(jnp.sum(drn * n, axis=0, keepdims=True))
            dy2_ref[...] = _norm_bwd(drn * g_ref[...], n, rstd).astype(BF16)

        @pl.when(i < n_t)
        def _():
            h = h_ref[...]
            h_ext = jnp.concatenate([hh_ref[...], h], axis=0)
            row = lax.broadcasted_iota(jnp.int32, (tm + HALO, ck), 0)
            no_halo = (row < HALO) & (i == 0)

            def up(c):
                cs = slice(c * ck, (c + 1) * ck)
                return jnp.where(no_halo, 0.0, _dot(h_ext, wg_ref[cs, :], NT)), _dot(h, wv_ref[cs, :], NT)

            part = None
            nxt = up(0)
            finish(i > 0)
            for c in range(n_c):
                cs = slice(c * ck, (c + 1) * ck)
                gate_ext, val = nxt
                if c + 1 < n_c:
                    nxt = up(c + 1)
                act, dact = _gelu_parts(_conv_gate(gate_ext, cw_ref[:, cs], cb_ref[:, cs]))
                a = (act * val).astype(BF16)
                gate_ref[:, cs] = gate_ext[HALO:].astype(BF16)
                a_ref[:, cs] = a
                act_ref[:, cs] = act.astype(BF16)
                vd_ref[:, cs] = (val * dact).astype(BF16)
                p = _dot(a, wd_ref[cs, :], NN)
                part = p if part is None else part + p
            acc_ref[...] = part

        @pl.when(i == n_t)
        def _():
            finish(True)

    this = lambda i: jnp.minimum(i, n_t - 1)
    before = lambda i: jnp.maximum(i - 1, 0)
    tok = lambda w, at: pl.BlockSpec((tm, w), lambda i: (at(i), 0))
    vec = pl.BlockSpec((1, d), lambda i: (0, 0))
    once = lambda shape, imap: pl.BlockSpec(shape, imap, pipeline_mode=pl.Buffered(1))
    return pl.pallas_call(
        body, name="ffn_fwd_loss", grid=(n_t + 1,),
        in_specs=[tok(d, this), pl.BlockSpec((HALO, d), lambda i: (_halo_before(this(i), tm), 0)),
                  tok(d, before), tok(d, before),
                  once((d_ff, d), lambda i: (0, 0)), once((d_ff, d), lambda i: (1, 0)), once((d_ff, d), lambda i: (0, 0)),
                  pl.BlockSpec((3, d_ff), lambda i: (0, 0)), pl.BlockSpec((1, d_ff), lambda i: (0, 0)), vec, vec],
        out_specs=[tok(d_ff, this)] * 4 + [tok(d, before), tok(d, before), pl.BlockSpec((8, d), lambda i: (0, 0)),
                                          pl.BlockSpec((8, LANES), lambda i: (0, 0))],
        out_shape=[jax.ShapeDtypeStruct((s_len, d_ff), BF16)] * 4
        + [jax.ShapeDtypeStruct((s_len, d), BF16), jax.ShapeDtypeStruct((s_len, d), F32),
           jax.ShapeDtypeStruct((8, d), F32), jax.ShapeDtypeStruct((8, LANES), F32)],
        scratch_shapes=[pltpu.VMEM((tm, d), F32)],
        compiler_params=_params(("arbitrary",)),
    )(h2, h2, x1, target, w_up_t, w_up_t, w_down, conv_w, conv_b, gt_f, g_post_ffn)


def _ffn_bwd_act(dy2, gate, a, act, vd, w_down, tm, tf, ck):
    s_len, d = dy2.shape
    d_ff = w_down.shape[0]
    n_t = s_len // tm
    chunks = [slice(lo, min(lo + ck, tf)) for lo in range(0, tf, ck)]

    def body(dy_ref, g_ref, gh_ref, a_ref, act_ref, vd_ref, wd_ref, dgc_ref, dval_ref, dwd_ref, dconv_ref, acc_ref):
        i = pl.program_id(1)

        @pl.when(i == 0)
        def _():
            acc_ref[...] = jnp.zeros_like(acc_ref)
            dconv_ref[...] = jnp.zeros_like(dconv_ref)

        dy = dy_ref[...]

        def down(cs):
            return _dot(dy, wd_ref[cs, :], NT)

        nxt = down(chunks[0])
        for c, cs in enumerate(chunks):
            width = cs.stop - cs.start
            da = nxt
            if c + 1 < len(chunks):
                nxt = down(chunks[c + 1])
            acc_ref[cs, :] += _dot(a_ref[:, cs], dy, TN)
            row = lax.broadcasted_iota(jnp.int32, (tm + HALO, width), 0)
            gate_ext = jnp.where((row < HALO) & (i == 0), 0.0,
                                 jnp.concatenate([gh_ref[:, cs], g_ref[:, cs]], axis=0).astype(F32))
            dgc = da * vd_ref[:, cs].astype(F32)
            dgc_ref[:, cs] = dgc.astype(BF16)
            dval_ref[:, cs] = (da * act_ref[:, cs].astype(F32)).astype(BF16)
            rows = [jnp.sum(dgc * pltpu.roll(gate_ext, 2 - k, 0)[HALO:], axis=0, keepdims=True) for k in range(2)]
            rows += [jnp.sum(dgc * gate_ext[HALO:], axis=0, keepdims=True), jnp.sum(dgc, axis=0, keepdims=True),
                     jnp.zeros((4, width), F32)]
            dconv_ref[:, cs] += jnp.concatenate(rows, axis=0)

        @pl.when(i == n_t - 1)
        def _():
            dwd_ref[...] = acc_ref[...].astype(BF16)

    tokf = pl.BlockSpec((tm, tf), lambda j, i: (i, j))
    return pl.pallas_call(
        body, name="ffn_bwd_act", grid=(d_ff // tf, n_t),
        in_specs=[pl.BlockSpec((tm, d), lambda j, i: (i, 0)), tokf,
                  pl.BlockSpec((HALO, tf), lambda j, i: (_halo_before(i, tm), j)), tokf, tokf, tokf,
                  pl.BlockSpec((tf, d), lambda j, i: (j, 0))],
        out_specs=[tokf, tokf, pl.BlockSpec((tf, d), lambda j, i: (j, 0)), pl.BlockSpec((8, tf), lambda j, i: (0, j))],
        out_shape=[jax.ShapeDtypeStruct((s_len, d_ff), BF16), jax.ShapeDtypeStruct((s_len, d_ff), BF16),
                   jax.ShapeDtypeStruct((d_ff, d), BF16), jax.ShapeDtypeStruct((8, d_ff), F32)],
        scratch_shapes=[pltpu.VMEM((tf, d), F32)],
        compiler_params=_params(("arbitrary", "arbitrary")),
    )(dy2, gate, gate, a, act, vd, w_down)


def _ffn_up_mix_bwd(dgc, dval, w_up_t, conv_w, dout, x1, y1, cat, attn, w_out_t, sc_f, g_pre_ffn, gt_m, g_post_mix, tm):
    s_len, d_ff = dgc.shape
    d = w_up_t.shape[1]
    n_t = s_len // tm
    half = tm // 2

    def body(dg_ref, dgn_ref, dv_ref, cw_ref, w_ref, do_ref, x1_ref, y1_ref, cat_ref, at_ref, wo_ref,
             sc_ref, g2_ref, gt_ref, g1_ref,
             dup_ref, dx1_ref, dpool_ref, dattn_ref, delta_ref, dwo_ref, sums_ref, acc_ref):
        i = pl.program_id(0)
        nxt = dgn_ref[...].astype(F32) * (i < n_t - 1).astype(F32)
        ext = jnp.concatenate([dg_ref[...].astype(F32), nxt], axis=0)
        rows = tm + HALO
        dgate = (ext * cw_ref[2:3, :] + pltpu.roll(ext, rows - 1, 0) * cw_ref[1:2, :]
                 + pltpu.roll(ext, rows - 2, 0) * cw_ref[0:1, :])[:tm]
        dup_ref[:, 0:d_ff] = dgate.astype(BF16)
        dup_ref[:, d_ff:2 * d_ff] = dv_ref[...]
        halves = [slice(0, half), slice(half, tm)]
        dhs = [_dot(dup_ref[rs, :], w_ref[...], NN) for rs in halves]
        lane = lax.broadcasted_iota(jnp.int32, (half, LANES), 1)
        first = lane < HEAD_DIM
        dwo = sums = None
        for rs, dh in zip(halves, dhs):
            x1 = x1_ref[rs, :]
            r2 = _rstd(x1)
            n2 = x1 * r2
            ng = n2 * g2_ref[...]
            dng = dh * (1.0 + sc_ref[...])
            dx1 = do_ref[rs, :] + _norm_bwd(dng * g2_ref[...], n2, r2)
            dx1_ref[rs, :] = dx1
            y1 = y1_ref[rs, :].astype(F32)
            r1 = _rstd(y1)
            n1 = y1 * r1
            drn = dx1 * gt_ref[...]
            dy1 = _norm_bwd(drn * g1_ref[...], n1, r1).astype(BF16)
            dcat = _dot(dy1, wo_ref[...], NN)
            dpool_ref[rs, :] = dcat[:, 0:256]
            for s in range(2):
                da = dcat[:, 256 + s * LANES:256 + (s + 1) * LANES]
                dattn_ref[s, rs, :] = da
                prod = da * at_ref[rs, s * LANES:(s + 1) * LANES]
                tot = jnp.sum(prod, axis=-1, keepdims=True)
                lo = jnp.sum(jnp.where(first, prod, 0.0), axis=-1, keepdims=True)
                delta_ref[s, rs, :] = jnp.where(first, lo, tot - lo)
            dwo_h = _dot(dy1, cat_ref[rs, :], TN)
            sums_h = jnp.concatenate(
                [jnp.sum(dh, axis=0, keepdims=True), jnp.sum(dh * ng, axis=0, keepdims=True),
                 jnp.sum(dng * n2, axis=0, keepdims=True), jnp.sum(dx1 * (n1 * g1_ref[...]), axis=0, keepdims=True),
                 jnp.sum(drn * n1, axis=0, keepdims=True), jnp.zeros((3, d), F32)], axis=0)
            dwo = dwo_h if dwo is None else dwo + dwo_h
            sums = sums_h if sums is None else sums + sums_h

        @pl.when(i == 0)
        def _():
            acc_ref[...] = dwo
            sums_ref[...] = sums

        @pl.when(i > 0)
        def _():
            acc_ref[...] += dwo
            sums_ref[...] += sums

        @pl.when(i == n_t - 1)
        def _():
            dwo_ref[...] = acc_ref[...].astype(BF16)

    tile = lambda w: pl.BlockSpec((tm, w), lambda i: (i, 0))
    slab = pl.BlockSpec((2, tm, LANES), lambda i: (0, i, 0))
    vec = pl.BlockSpec((1, d), lambda i: (0, 0))
    once = lambda shape: pl.BlockSpec(shape, lambda i: (0, 0), pipeline_mode=pl.Buffered(1))
    return pl.pallas_call(
        body, name="ffn_up_mix_bwd", grid=(n_t,),
        in_specs=[tile(d_ff), pl.BlockSpec((HALO, d_ff), lambda i: (jnp.minimum((i + 1) * (tm // HALO), s_len // HALO - 1), 0)),
                  tile(d_ff), pl.BlockSpec((3, d_ff), lambda i: (0, 0)), once((2 * d_ff, d)),
                  tile(d), tile(d), tile(d), tile(512), tile(256), once((d, 512)), vec, vec, vec, vec],
        out_specs=[tile(2 * d_ff), tile(d), tile(256), slab, slab, pl.BlockSpec((d, 512), lambda i: (0, 0)),
                   pl.BlockSpec((8, d), lambda i: (0, 0))],
        out_shape=[jax.ShapeDtypeStruct((s_len, 2 * d_ff), BF16), jax.ShapeDtypeStruct((s_len, d), F32),
                   jax.ShapeDtypeStruct((s_len, 256), F32),
                   jax.ShapeDtypeStruct((2, s_len, LANES), F32), jax.ShapeDtypeStruct((2, s_len, LANES), F32),
                   jax.ShapeDtypeStruct((d, 512), BF16), jax.ShapeDtypeStruct((8, d), F32)],
        scratch_shapes=[pltpu.VMEM((d, 512), F32)],
        compiler_params=_params(("arbitrary",)),
    )(dgc, dgc, dval, conv_w, w_up_t, dout, x1, y1, cat, attn, w_out_t, sc_f, g_pre_ffn, gt_m, g_post_mix)


def _ffn_bwd_up(dgc, dval, w_up_t, conv_w, tm):
    s_len, d_ff = dgc.shape
    d = w_up_t.shape[1]
    n_t = s_len // tm

    def body(dg_ref, dgn_ref, dv_ref, cw_ref, w_ref, dup_ref, dh_ref):
        i = pl.program_id(0)
        nxt = dgn_ref[...].astype(F32) * (i < n_t - 1).astype(F32)
        ext = jnp.concatenate([dg_ref[...].astype(F32), nxt], axis=0)
        rows = tm + HALO
        dgate = (ext * cw_ref[2:3, :] + pltpu.roll(ext, rows - 1, 0) * cw_ref[1:2, :]
                 + pltpu.roll(ext, rows - 2, 0) * cw_ref[0:1, :])[:tm]
        dup = jnp.concatenate([dgate.astype(BF16), dv_ref[...]], axis=1)
        dup_ref[...] = dup
        dh_ref[...] = _dot(dup, w_ref[...], NN).astype(BF16)

    tokf = pl.BlockSpec((tm, d_ff), lambda i: (i, 0))
    return pl.pallas_call(
        body, name="ffn_bwd_up", grid=(n_t,),
        in_specs=[tokf, pl.BlockSpec((HALO, d_ff), lambda i: (jnp.minimum((i + 1) * (tm // HALO), s_len // HALO - 1), 0)),
                  tokf, pl.BlockSpec((3, d_ff), lambda i: (0, 0)), pl.BlockSpec((2 * d_ff, d), lambda i: (0, 0))],
        out_specs=[pl.BlockSpec((tm, 2 * d_ff), lambda i: (i, 0)), pl.BlockSpec((tm, d), lambda i: (i, 0))],
        out_shape=[jax.ShapeDtypeStruct((s_len, 2 * d_ff), BF16), jax.ShapeDtypeStruct((s_len, d), BF16)],
        compiler_params=_params(("arbitrary",)),
    )(dgc, dgc, dval, conv_w, w_up_t)


def _mix_bwd(dh2, dout, x1, y1, cat, attn, w_out_t, sc_f, g_pre_ffn, gt_m, g_post_mix, tm):
    s_len, d = x1.shape
    n_t = s_len // tm

    def body(dh_ref, do_ref, x1_ref, y1_ref, cat_ref, at_ref, wo_ref, sc_ref, g2_ref, gt_ref, g1_ref,
             dx1_ref, dpool_ref, dattn_ref, delta_ref, dwo_ref, sums_ref, acc_ref):
        i = pl.program_id(0)
        dh = dh_ref[...].astype(F32)
        x1 = x1_ref[...]
        r2 = _rstd(x1)
        n2 = x1 * r2
        ng = n2 * g2_ref[...]
        dng = dh * (1.0 + sc_ref[...])
        dx1 = do_ref[...] + _norm_bwd(dng * g2_ref[...], n2, r2)
        dx1_ref[...] = dx1
        y1 = y1_ref[...].astype(F32)
        r1 = _rstd(y1)
        n1 = y1 * r1
        drn = dx1 * gt_ref[...]
        dy1 = _norm_bwd(drn * g1_ref[...], n1, r1).astype(BF16)
        dcat = _dot(dy1, wo_ref[...], NN)
        dpool_ref[...] = dcat[:, 0:256]
        lane = lax.broadcasted_iota(jnp.int32, (tm, LANES), 1)
        first = lane < HEAD_DIM
        for s in range(2):
            da = dcat[:, 256 + s * LANES:256 + (s + 1) * LANES]
            dattn_ref[s] = da
            prod = da * at_ref[:, s * LANES:(s + 1) * LANES]
            tot = jnp.sum(prod, axis=-1, keepdims=True)
            lo = jnp.sum(jnp.where(first, prod, 0.0), axis=-1, keepdims=True)
            delta_ref[s] = jnp.where(first, lo, tot - lo)
        dwo = _dot(dy1, cat_ref[...], TN)
        sums = jnp.concatenate(
            [jnp.sum(dh, axis=0, keepdims=True), jnp.sum(dh * ng, axis=0, keepdims=True),
             jnp.sum(dng * n2, axis=0, keepdims=True), jnp.sum(dx1 * (n1 * g1_ref[...]), axis=0, keepdims=True),
             jnp.sum(drn * n1, axis=0, keepdims=True), jnp.zeros((3, d), F32)], axis=0)

        @pl.when(i == 0)
        def _():
            acc_ref[...] = dwo
            sums_ref[...] = sums

        @pl.when(i > 0)
        def _():
            acc_ref[...] += dwo
            sums_ref[...] += sums

        @pl.when(i == n_t - 1)
        def _():
            dwo_ref[...] = acc_ref[...].astype(BF16)

    tile = lambda w: pl.BlockSpec((tm, w), lambda i: (i, 0))
    slab = pl.BlockSpec((2, tm, LANES), lambda i: (0, i, 0))
    vec = pl.BlockSpec((1, d), lambda i: (0, 0))
    return pl.pallas_call(
        body, name="mix_bwd", grid=(n_t,),
        in_specs=[tile(d), tile(d), tile(d), tile(d), tile(512), tile(256),
                  pl.BlockSpec((d, 512), lambda i: (0, 0)), vec, vec, vec, vec],
        out_specs=[tile(d), tile(256), slab, slab, pl.BlockSpec((d, 512), lambda i: (0, 0)),
                   pl.BlockSpec((8, d), lambda i: (0, 0))],
        out_shape=[jax.ShapeDtypeStruct((s_len, d), F32), jax.ShapeDtypeStruct((s_len, 256), F32),
                   jax.ShapeDtypeStruct((2, s_len, LANES), F32), jax.ShapeDtypeStruct((2, s_len, LANES), F32),
                   jax.ShapeDtypeStruct((d, 512), BF16), jax.ShapeDtypeStruct((8, d), F32)],
        scratch_shapes=[pltpu.VMEM((d, 512), F32)],
        compiler_params=_params(("arbitrary",)),
    )(dh2, dout, x1, y1, cat, attn, w_out_t, sc_f, g_pre_ffn, gt_m, g_post_mix)


def _pool_bwd(dpool, u_pool, w_blk, b_pool, pool_scale, tm):
    s_len = dpool.shape[0]
    n_t = s_len // tm

    def body(dp_ref, dpn_ref, u_ref, uh_ref, wb_ref, bp_ref, ps_ref, du_ref, dwp_ref, sums_ref, acc_ref):
        i = pl.program_id(0)
        u = u_ref[...]
        mixed, _ = _pool_mixed(u, uh_ref[...] * (i > 0).astype(F32), i, tm)
        mixed_b = mixed.astype(BF16)
        y = _dot(mixed_b, wb_ref[...], NN) + bp_ref[...]
        dp = dp_ref[...]
        dy = dp * ps_ref[...]
        dwb = _dot(mixed_b, dy.astype(BF16), TN)
        sums = jnp.concatenate([jnp.sum(dy, axis=0, keepdims=True), jnp.sum(dp * y, axis=0, keepdims=True),
                                jnp.zeros((6, 256), F32)], axis=0)
        dp_ext = jnp.concatenate([dp, dpn_ref[...] * (i < n_t - 1).astype(F32)], axis=0)
        dmix = _dot((dp_ext * ps_ref[...]).astype(BF16), wb_ref[...], NT)
        rows = tm + HALO
        grp = lax.broadcasted_iota(jnp.int32, (rows, 256), 1) // HEAD_DIM
        pick = lambda a, b, c, e: jnp.where(grp == 0, a, jnp.where(grp == 1, b, jnp.where(grp == 2, c, e)))
        pos = (i * tm + lax.broadcasted_iota(jnp.int32, (rows, 256), 0)).astype(F32)
        z = dmix / jnp.minimum(pos + 1.0, pick(*[float(w) for w in POOL_WINDOWS]))
        f2 = z + pltpu.roll(z, rows - 1, 0)
        f4 = f2 + pltpu.roll(f2, rows - 2, 0)
        f8 = f4 + pltpu.roll(f4, rows - 4, 0)
        f16 = f8 + pltpu.roll(f8, rows - 8, 0)
        du_ref[...] = (pick(f2, f4, f8, f16) - dmix)[:tm]

        @pl.when(i == 0)
        def _():
            acc_ref[...] = dwb
            sums_ref[...] = sums

        @pl.when(i > 0)
        def _():
            acc_ref[...] += dwb
            sums_ref[...] += sums

        @pl.when(i == n_t - 1)
        def _():
            full = acc_ref[...]
            for gi in range(len(POOL_WINDOWS)):
                lo = gi * HEAD_DIM
                dwp_ref[gi] = full[lo:lo + HEAD_DIM, lo:lo + HEAD_DIM]

    n_g = len(POOL_WINDOWS)
    tile = pl.BlockSpec((tm, 256), lambda i: (i, 0))
    const = lambda a: pl.BlockSpec(a.shape, lambda i: (0,) * a.ndim)
    return pl.pallas_call(
        body, name="pool_bwd", grid=(n_t,),
        in_specs=[tile, pl.BlockSpec((HALO, 256), lambda i: (jnp.minimum((i + 1) * (tm // HALO), s_len // HALO - 1), 0)),
                  tile, pl.BlockSpec((HALO, 256), lambda i: (_halo_before(i, tm), 0)),
                  const(w_blk), const(b_pool), const(pool_scale)],
        out_specs=[tile, pl.BlockSpec((n_g, HEAD_DIM, HEAD_DIM), lambda i: (0, 0, 0)), pl.BlockSpec((8, 256), lambda i: (0, 0))],
        out_shape=[jax.ShapeDtypeStruct((s_len, 256), F32), jax.ShapeDtypeStruct((n_g, HEAD_DIM, HEAD_DIM), F32),
                   jax.ShapeDtypeStruct((8, 256), F32)],
        scratch_shapes=[pltpu.VMEM((256, 256), F32)],
        compiler_params=_params(("arbitrary",)),
    )(dpool, dpool, u_pool, u_pool, w_blk, b_pool, pool_scale)


def _attn_bwd(qkv, dattn, lse_all, delta):
    s_len = qkv.shape[1]
    n_g = len(DILATIONS)

    def body(q_ref, k_ref, v_ref, do_ref, l_ref, dl_ref, dq_ref, dk_ref, dv_ref):
        lane = lax.broadcasted_iota(jnp.int32, (BLOCK, LANES), 1)
        first = lane < HEAD_DIM

        def group(dil):
            nb = s_len // (BLOCK * dil)

            def block(t, carry):
                dk_part, dv_part = carry
                r, n = t // nb, t % nb
                cur = _block_rows(n, r, dil)
                prev = _block_rows(jnp.maximum(n - 1, 0), r, dil)
                q = q_ref[0, cur, :]
                do = do_ref[0, cur, :]
                lse = l_ref[0, cur, :]
                dlt = dl_ref[0, cur, :]
                kcat = jnp.concatenate([k_ref[0, prev, :], k_ref[0, cur, :]], axis=0).astype(BF16)
                vcat = jnp.concatenate([v_ref[0, prev, :], v_ref[0, cur, :]], axis=0).astype(BF16)
                valid = _band_mask(n)
                stack = lambda a: jnp.concatenate([jnp.where(first, a, 0.0), jnp.where(first, 0.0, a)], axis=0)
                rows2 = lambda a: jnp.concatenate([a[:, 0:1], a[:, HEAD_DIM:HEAD_DIM + 1]], axis=0)
                q2, do2 = stack(q).astype(BF16), stack(do).astype(BF16)
                valid2 = jnp.concatenate([valid, valid], axis=0)
                p = jnp.where(valid2, jnp.exp(_dot(q2, kcat, NT) - rows2(lse)), 0.0)
                ds = (p * (_dot(do2, vcat, NT) - rows2(dlt))).astype(BF16)
                dq2 = _dot(ds, kcat, NN)
                dq_ref[0, 0, cur, :] = jnp.where(first, dq2[:BLOCK], dq2[BLOCK:])
                dkc = _dot(ds, q2, TN)
                dvc = _dot(p.astype(BF16), do2, TN)
                dk_ref[0, 0, prev, :] = dk_part + dkc[:BLOCK]
                dv_ref[0, 0, prev, :] = dv_part + dvc[:BLOCK]
                dk_ref[0, 0, cur, :] = dkc[BLOCK:]
                dv_ref[0, 0, cur, :] = dvc[BLOCK:]
                return dkc[BLOCK:], dvc[BLOCK:]

            def blocks(tt, carry):
                for u in range(ATTN_BWD_UNROLL):
                    carry = block(tt * ATTN_BWD_UNROLL + u, carry)
                return carry

            zero = jnp.zeros((BLOCK, LANES), F32)
            lax.fori_loop(0, nb * dil // ATTN_BWD_UNROLL, blocks, (zero, zero))

        for gi, dil in enumerate(DILATIONS):
            pl.when(pl.program_id(1) == gi)(functools.partial(group, dil))

    def slab(base):
        return pl.BlockSpec((1, s_len, LANES), lambda s, g: (base + 2 * g + s, 0, 0))

    one = pl.BlockSpec((1, s_len, LANES), lambda s, g: (s, 0, 0))
    out = pl.BlockSpec((1, 1, s_len, LANES), lambda s, g: (g, s, 0, 0))
    shape = jax.ShapeDtypeStruct((n_g, 2, s_len, LANES), F32)
    return pl.pallas_call(
        body, name="attn_bwd", grid=(2, n_g),
        in_specs=[slab(0), slab(6), slab(12), one, one, one],
        out_specs=[out, out, out], out_shape=[shape, shape, shape],
        compiler_params=_params(("arbitrary", "arbitrary")),
    )(qkv, qkv, qkv, dattn, lse_all, delta)


def _dproj_assemble(du, dqkv, rope, tm):
    s_len = du.shape[0]
    n_proj = 256 + 18 * LANES

    def body(du_ref, dq_ref, dk_ref, dv_ref, cs_ref, spread_ref, dproj_ref):
        dproj_ref[:, 0:256] = du_ref[...].astype(BF16)
        lanes = _rope_lanes(cs_ref, spread_ref)
        col = 256
        for kind, dref in enumerate((dq_ref, dk_ref, dv_ref)):
            for grp in range(3):
                for s in range(2):
                    piece = dref[grp, s]
                    if kind < 2:
                        piece = _rope_bwd(piece, lanes)
                    if kind == 0:
                        piece = piece * (HEAD_DIM ** -0.5)
                    dproj_ref[:, col:col + LANES] = piece.astype(BF16)
                    col += LANES

    groups = pl.BlockSpec((len(DILATIONS), 2, tm, LANES), lambda i: (0, 0, i, 0))
    return pl.pallas_call(
        body, name="dproj_assemble", grid=(s_len // tm,),
        in_specs=[pl.BlockSpec((tm, 256), lambda i: (i, 0))] + [groups] * 3
        + [pl.BlockSpec((tm, rope[0].shape[1]), lambda i: (i, 0)), pl.BlockSpec(rope[1].shape, lambda i: (0, 0, 0))],
        out_specs=pl.BlockSpec((tm, n_proj), lambda i: (i, 0)),
        out_shape=jax.ShapeDtypeStruct((s_len, n_proj), BF16),
        compiler_params=_params(("arbitrary",)),
    )(du, *dqkv, *rope)


def _inproj_bwd(dproj, w_in_t, x, dx1, sc_m, g_pre_mix, tm):
    s_len, d = x.shape
    n_proj = w_in_t.shape[0]
    n_t = s_len // tm

    def body(dproj_ref, w_ref, x_ref, dx1_ref, sc_ref, g_ref, dx_ref, sums_ref):
        i = pl.program_id(0)
        halves = [slice(0, tm // 2), slice(tm // 2, tm)]
        dhs = [_dot(dproj_ref[rs, :], w_ref[...], NN) for rs in halves]
        sums = None
        for rs, dh in zip(halves, dhs):
            xv = x_ref[rs, :]
            r = _rstd(xv)
            n = xv * r
            dng = dh * (1.0 + sc_ref[...])
            dx_ref[rs, :] = dx1_ref[rs, :] + _norm_bwd(dng * g_ref[...], n, r)
            part = jnp.concatenate([jnp.sum(dh, axis=0, keepdims=True), jnp.sum(dh * (n * g_ref[...]), axis=0, keepdims=True),
                                    jnp.sum(dng * n, axis=0, keepdims=True), jnp.zeros((5, d), F32)], axis=0)
            sums = part if sums is None else sums + part

        @pl.when(i == 0)
        def _():
            sums_ref[...] = sums

        @pl.when(i > 0)
        def _():
            sums_ref[...] += sums

    tile = lambda w: pl.BlockSpec((tm, w), lambda i: (i, 0))
    vec = pl.BlockSpec((1, d), lambda i: (0, 0))
    return pl.pallas_call(
        body, name="inproj_bwd", grid=(n_t,),
        in_specs=[tile(n_proj), pl.BlockSpec((n_proj, d), lambda i: (0, 0)), tile(d), tile(d), vec, vec],
        out_specs=[tile(d), pl.BlockSpec((8, d), lambda i: (0, 0))],
        out_shape=[jax.ShapeDtypeStruct((s_len, d), F32), jax.ShapeDtypeStruct((8, d), F32)],
        compiler_params=_params(("arbitrary",)),
    )(dproj, w_in_t, x, dx1, sc_m, g_pre_mix)


def _wgrad(a, b, name, tk, tmm):
    s_len, m = a.shape
    n = b.shape[1]
    n_k = s_len // tk

    def body(a_ref, b_ref, o_ref, acc_ref):
        k = pl.program_id(1)
        part = _dot(a_ref[...], b_ref[...], TN)

        @pl.when(k == 0)
        def _():
            acc_ref[...] = part

        @pl.when(k > 0)
        def _():
            acc_ref[...] += part

        @pl.when(k == n_k - 1)
        def _():
            o_ref[...] = acc_ref[...].astype(BF16)

    return pl.pallas_call(
        body, name=name, grid=(m // tmm, n_k),
        in_specs=[pl.BlockSpec((tk, tmm), lambda j, k: (k, j)), pl.BlockSpec((tk, n), lambda j, k: (k, 0))],
        out_specs=pl.BlockSpec((tmm, n), lambda j, k: (j, 0)),
        out_shape=jax.ShapeDtypeStruct((m, n), BF16),
        scratch_shapes=[pltpu.VMEM((tmm, n), F32)],
        compiler_params=_params(("arbitrary", "arbitrary")),
    )(a, b)


def _place():
    return lax.axis_index("x"), lax.axis_index("y"), lax.axis_index("c")


def _peer(k):
    x, y, c = _place()
    bx, by, bc = (k >> 2) & 1, (k >> 1) & 1, k & 1
    return (x ^ bx if bx else x, y ^ by if by else y, c ^ bc if bc else c)


def _index(pos):
    return 4 * pos[0] + 2 * pos[1] + pos[2]


def _entry_exchange(c_rows, w_ada, b_ada_cols, taps, shards):
    d = c_rows.shape[1]
    ncol = w_ada.shape[1]
    n_w = len(shards)

    def body(c_ref, w_ref, b_ref, t_ref, *rest):
        srcs = rest[:n_w]
        call_ref, mod_ref, tall_ref = rest[n_w:n_w + 3]
        outs = rest[n_w + 3:2 * n_w + 3]
        stage_ref, s_send, s_recv, w_send, w_recv, local_sems = rest[2 * n_w + 3:]
        x, y, c = _place()
        here, sibling = (x, y, c), (x, y, 1 - c)
        chips = [(1 - x, y), (x, 1 - y), (1 - x, 1 - y)]
        me = _index(here)

        def small(kind, src, dst, k):
            return pltpu.make_async_remote_copy(src_ref=src, dst_ref=dst, send_sem=s_send.at[kind, k - 1],
                                                recv_sem=s_recv.at[kind, k - 1], device_id=_peer(k), device_id_type=MESH)

        gather = lambda k: small(0, c_ref, call_ref.at[me], k)
        scatter = lambda k: small(1, stage_ref.at[_index(_peer(k))], mod_ref.at[me], k)
        gather_taps = lambda k: small(2, t_ref, tall_ref.at[me], k)

        def rows(w, pos):
            r = shards[w].shape[0]
            return outs[w].at[pl.ds(pl.multiple_of(_index(pos) * r, 16), r), :]

        def block(k, w, pos, to, own=False):
            return pltpu.make_async_remote_copy(
                src_ref=srcs[w] if own else rows(w, pos), dst_ref=rows(w, pos),
                send_sem=w_send.at[k, w], recv_sem=w_recv.at[k, w], device_id=to, device_id_type=MESH)

        call_ref[me] = c_ref[...]
        tall_ref[me] = t_ref[...]
        for k in range(1, N_DEV):
            gather(k).start()
        for k in range(1, N_DEV):
            gather_taps(k).start()
        mine = [pltpu.make_async_copy(srcs[w], rows(w, here), local_sems.at[w]) for w in range(n_w)]
        for cp in mine:
            cp.start()
        first = [block(0, w, here, sibling, own=True) for w in range(n_w)]
        first += [block(1 + j, w, here, (*chip, c), own=True) for j, chip in enumerate(chips) for w in range(n_w)]
        for cp in first:
            cp.start()

        for k in range(1, N_DEV):
            gather(k).wait_recv()
        cv = jnp.concatenate([call_ref[b, 0:1, :] for b in range(N_DEV)], axis=0)
        act = cv * jax.nn.sigmoid(cv)
        mod = lax.dot_general(act, w_ref[...], NN, preferred_element_type=F32,
                              precision=lax.Precision.HIGHEST) + b_ref[...]
        for b in range(N_DEV):
            stage_ref[b] = jnp.broadcast_to(mod[b:b + 1, :], (8, ncol))
        mod_ref[me] = stage_ref[me]
        for k in range(1, N_DEV):
            scatter(k).start()

        passed = []
        for j, chip in enumerate(chips):
            for w in range(n_w):
                block(1 + j, w, (*chip, c), here).wait_recv()
                fwd = block(4 + j, w, (*chip, c), sibling)
                fwd.start()
                passed.append(fwd)
        for w in range(n_w):
            block(0, w, sibling, here).wait_recv()
        for j, chip in enumerate(chips):
            for w in range(n_w):
                block(4 + j, w, (*chip, 1 - c), here).wait_recv()
        for k in range(1, N_DEV):
            scatter(k).wait_recv()
            gather_taps(k).wait_recv()
        for cp in first + passed:
            cp.wait_send()
        for k in range(1, N_DEV):
            gather(k).wait_send()
            scatter(k).wait_send()
            gather_taps(k).wait_send()
        for cp in mine:
            cp.wait()

    vmem, hbm = pl.BlockSpec(memory_space=pltpu.VMEM), pl.BlockSpec(memory_space=pltpu.HBM)
    out = pl.pallas_call(
        body, name="entry_exchange",
        in_specs=[vmem] * 4 + [hbm] * n_w, out_specs=[vmem] * 3 + [hbm] * n_w,
        out_shape=[jax.ShapeDtypeStruct((N_DEV, 8, d), F32), jax.ShapeDtypeStruct((N_DEV, 8, ncol), F32),
                   jax.ShapeDtypeStruct((N_DEV,) + taps.shape, F32)]
        + [jax.ShapeDtypeStruct((N_DEV * s.shape[0], s.shape[1]), s.dtype) for s in shards],
        scratch_shapes=[pltpu.VMEM((N_DEV, 8, ncol), F32), pltpu.SemaphoreType.DMA((3, N_DEV - 1)),
                        pltpu.SemaphoreType.DMA((3, N_DEV - 1)), pltpu.SemaphoreType.DMA((N_DEV - 1, n_w)),
                        pltpu.SemaphoreType.DMA((N_DEV - 1, n_w)), pltpu.SemaphoreType.DMA((n_w,))],
        compiler_params=_params(),
    )(c_rows, w_ada, b_ada_cols, taps, *shards)
    return out[0], out[1], out[2], out[3:]


def _peer_copies(mode, srcs, lands, send_sems, recv_sems):
    if mode in ("gather_ici", "gather_d2d"):
        x, y, c = _place()
        sibling = (x, y, 1 - c)
        chips = [(1 - x, y), (x, 1 - y), (1 - x, 1 - y)]
        n = len(lands)

        def rows(w, pos):
            r = lands[w].shape[0] // N_DEV
            return lands[w].at[pl.ds(pl.multiple_of(_index(pos) * r, 16), r), :]

        def copy(k, w, src, dst, to):
            return pltpu.make_async_remote_copy(src_ref=src, dst_ref=dst, send_sem=send_sems.at[k * n + w],
                                                recv_sem=recv_sems.at[k * n + w], device_id=to, device_id_type=MESH)

        if mode == "gather_ici":
            targets = [sibling] + [(*chip, c) for chip in chips]
            return [copy(k, w, srcs[w], rows(w, (x, y, c)), to) for k, to in enumerate(targets) for w in range(n)]
        return [copy(j, w, rows(w, (*chip, c)), rows(w, (*chip, c)), sibling)
                for j, chip in enumerate(chips) for w in range(n)]
    me = _index(_place())
    copies = []
    for k in range(1, N_DEV):
        peer = _peer(k)
        for w, (src, land) in enumerate(zip(srcs, lands)):
            if mode == "gather":
                r = src.shape[0]
                dst = land.at[pl.ds(pl.multiple_of(me * r, 16), r), :]
            elif mode == "allgather":
                dst = land.at[me]
            else:
                r = src.shape[0] // N_DEV
                src = src.at[pl.ds(pl.multiple_of(_index(peer) * r, 16), r), :]
                dst = land.at[me]
            copies.append(pltpu.make_async_remote_copy(
                src_ref=src, dst_ref=dst, send_sem=send_sems.at[(k - 1) * len(srcs) + w],
                recv_sem=recv_sems.at[(k - 1) * len(srcs) + w],
                device_id=peer, device_id_type=MESH))
    return copies


def _landing_zone(mode, src, me, name):
    cols = src.shape[1]
    if mode == "gather":
        r = src.shape[0]
        in_spec = pl.BlockSpec((r, cols), lambda i, me_ref: (0, 0))
        out_spec = pl.BlockSpec((r, cols), lambda i, me_ref: (me_ref[0], 0))
        out_shape = jax.ShapeDtypeStruct((N_DEV * r, cols), src.dtype)
    else:
        r = src.shape[0] // N_DEV
        in_spec = pl.BlockSpec((r, cols), lambda i, me_ref: (me_ref[0], 0))
        out_spec = pl.BlockSpec((1, r, cols), lambda i, me_ref: (me_ref[0], 0, 0))
        out_shape = jax.ShapeDtypeStruct((N_DEV, r, cols), src.dtype)

    def body(me_ref, s_ref, o_ref):
        o_ref[...] = s_ref[...].reshape(o_ref.shape)

    return pl.pallas_call(
        body, name=name, out_shape=out_shape,
        grid_spec=pltpu.PrefetchScalarGridSpec(num_scalar_prefetch=1, grid=(1,), in_specs=[in_spec], out_specs=out_spec),
        compiler_params=_params(("arbitrary",)),
    )(me.reshape(1).astype(jnp.int32), src)


def _exchange_start(mode, srcs, lands, name):
    n_s, n_a = len(srcs), len(srcs) + len(lands)
    n_cp = _COPIES_PER_ARRAY.get(mode, N_DEV - 1) * len(lands)

    def body(*refs):
        for cp in _peer_copies(mode, refs[:n_s], refs[n_s:n_a], refs[n_a], refs[n_a + 1]):
            cp.start()
        refs[-1][...] = jnp.zeros_like(refs[-1])

    hbm, sem = pl.BlockSpec(memory_space=pltpu.HBM), pl.BlockSpec(memory_space=pltpu.SEMAPHORE)
    arrays = list(srcs) + list(lands)
    out = pl.pallas_call(
        body, name=name,
        out_shape=(pltpu.SemaphoreType.DMA((n_cp,)), pltpu.SemaphoreType.DMA((n_cp,)),
                   *[pltpu.HBM(a.shape, a.dtype) for a in arrays], jax.ShapeDtypeStruct((8, LANES), F32)),
        in_specs=[hbm] * n_a, out_specs=(sem, sem, *[hbm] * n_a, pl.BlockSpec(memory_space=pltpu.VMEM)),
        input_output_aliases={i: 2 + i for i in range(n_a)},
        compiler_params=pltpu.CompilerParams(has_side_effects=pltpu.SideEffectType.DATAFLOW_SIDE_EFFECTING),
    )(*[pltpu.with_memory_space_constraint(a, pltpu.HBM) for a in arrays])
    return out[0], out[1], out[2:2 + n_s], out[2 + n_s:2 + n_a], out[-1]


_COPIES_PER_ARRAY = {"gather_ici": 4, "gather_d2d": 3}


def _exchange_wait(mode, send_sems, recv_sems, srcs, lands, after, name):
    n_s, n_a = len(srcs), len(srcs) + len(lands)

    def body(*refs):
        copies = _peer_copies(mode, refs[:n_s], refs[n_s:n_a], refs[n_a], refs[n_a + 1])
        for cp in copies:
            cp.wait_send()
        for cp in copies:
            cp.wait_recv()

    hbm, sem = pl.BlockSpec(memory_space=pltpu.HBM), pl.BlockSpec(memory_space=pltpu.SEMAPHORE)
    arrays = list(srcs) + list(lands)
    out = pl.pallas_call(
        body, name=name, out_shape=tuple(pltpu.HBM(a.shape, a.dtype) for a in arrays),
        in_specs=[hbm] * n_a + [sem, sem, pl.BlockSpec(memory_space=pl.ANY)], out_specs=tuple([hbm] * n_a),
        input_output_aliases={i: i for i in range(n_a)},
        compiler_params=pltpu.CompilerParams(has_side_effects=pltpu.SideEffectType.DATAFLOW_SIDE_EFFECTING),
    )(*arrays, send_sems, recv_sems, after)
    return out[n_s:]


SMALL_WEIGHTS = ("b_ada", "g_pre_mix", "g_post_mix", "g_pre_ffn", "g_post_ffn", "w_pool", "b_pool", "pool_scale", "conv_b")


MOD_ROWS = ((0, 0), (0, 1), (1, 3), (1, 0), (1, 1), (2, 0))


def _small_sum(mine, gathered):
    n_l = len(mine)
    d = mine[0].shape[1]

    def body(*refs):
        loc, got = refs[:n_l], refs[n_l:2 * n_l]
        tot_refs, dmod_ref = refs[2 * n_l:3 * n_l], refs[3 * n_l]
        me = _index(_place())
        part = lambda a, dev: jnp.where(dev == me, loc[a][...], got[a][dev])
        for a in range(n_l):
            tot = part(a, 0)
            for dev in range(1, N_DEV):
                tot = tot + part(a, dev)
            tot_refs[a][...] = tot
        for dev in range(N_DEV):
            for k, (a, r) in enumerate(MOD_ROWS):
                dmod_ref[dev:dev + 1, k * d:(k + 1) * d] = part(a, dev)[r:r + 1, :]

    vmem = pl.BlockSpec(memory_space=pltpu.VMEM)
    out = pl.pallas_call(
        body, name="small_sum", in_specs=[vmem] * (2 * n_l), out_specs=[vmem] * (n_l + 1),
        out_shape=[jax.ShapeDtypeStruct(a.shape, F32) for a in mine] + [jax.ShapeDtypeStruct((N_DEV, 6 * d), F32)],
        compiler_params=_params(),
    )(*mine, *gathered)
    return out[:n_l], out[n_l]


def _small_adam(totals, weights, moms, vels):
    n_t, n_w = len(totals), len(weights)

    def body(*refs):
        t_in, t_mix, t_ffn, t_pool, t_blk, t_conv, _ = (r[...] for r in refs[:n_t])
        w_refs, m_refs, v_refs = (refs[n_t + k * n_w:n_t + (k + 1) * n_w] for k in range(3))
        outs = refs[n_t + 3 * n_w:]

        def update(idx, g, at=()):
            sel = lambda ref: ref.at[at] if at else ref
            delta, nm, nv = _adam_math(sel(w_refs[idx])[...], g, sel(m_refs[idx])[...], sel(v_refs[idx])[...])
            for k, val in enumerate((g, delta, nm, nv)):
                sel(outs[4 * idx + k])[...] = val

        tots = (t_in, t_mix, t_ffn)
        update(0, jnp.concatenate([tots[a][r:r + 1] for a, r in MOD_ROWS], axis=1))
        update(1, t_in[2:3])
        update(2, t_mix[4:5])
        update(3, t_mix[2:3])
        update(4, t_ffn[1:2])
        for gi in range(len(POOL_WINDOWS)):
            update(5, t_blk[gi], at=(0, gi))
        update(6, jnp.concatenate([t_pool[0:1, gi * HEAD_DIM:(gi + 1) * HEAD_DIM] for gi in range(len(POOL_WINDOWS))], axis=0),
               at=(0,))
        update(7, t_pool[1:2])
        update(8, t_conv[3:4])

    vmem = pl.BlockSpec(memory_space=pltpu.VMEM)
    return pl.pallas_call(
        body, name="small_adam", in_specs=[vmem] * (n_t + 3 * n_w), out_specs=[vmem] * (4 * n_w),
        out_shape=[jax.ShapeDtypeStruct(w.shape, F32) for w in weights for _ in range(4)],
        compiler_params=_params(),
    )(*totals, *weights, *moms, *vels)


def _adam_math(w, g, m, v):
    m = ADAM_B1 * m + (1.0 - ADAM_B1) * g
    v = ADAM_B2 * v + (1.0 - ADAM_B2) * (g * g)
    m_hat = m / (1.0 - ADAM_B1 ** ADAM_STEP)
    v_hat = v / (1.0 - ADAM_B2 ** ADAM_STEP)
    delta = -ADAM_LR * (m_hat / (jnp.sqrt(v_hat) + ADAM_EPS) + ADAM_WD * w)
    return delta, m, v


def _adam(w, g, m, v, name, tr):
    rows, cols = w.shape

    def body(w_ref, g_ref, m_ref, v_ref, d_ref, nm_ref, nv_ref):
        d_ref[...], nm_ref[...], nv_ref[...] = _adam_math(w_ref[...], g_ref[...], m_ref[...], v_ref[...])

    spec = pl.BlockSpec((tr, cols), lambda i: (i, 0))
    shape = jax.ShapeDtypeStruct((rows, cols), F32)
    return pl.pallas_call(
        body, name=name, grid=(rows // tr,), in_specs=[spec] * 4, out_specs=[spec] * 3,
        out_shape=[shape] * 3, compiler_params=_params(("arbitrary",)),
    )(w, g, m, v)


def _sum_adam(parts, w, m, v, name, tr):
    _, rows, cols = parts.shape

    def body(p_ref, w_ref, m_ref, v_ref, g_ref, d_ref, nm_ref, nv_ref):
        g = p_ref[0].astype(F32)
        for dev in range(1, N_DEV):
            g = g + p_ref[dev].astype(F32)
        g_ref[...] = g
        d_ref[...], nm_ref[...], nv_ref[...] = _adam_math(w_ref[...], g, m_ref[...], v_ref[...])

    spec = pl.BlockSpec((tr, cols), lambda i: (i, 0))
    shape = jax.ShapeDtypeStruct((rows, cols), F32)
    return pl.pallas_call(
        body, name=name, grid=(rows // tr,),
        in_specs=[pl.BlockSpec((N_DEV, tr, cols), lambda i: (0, i, 0)), spec, spec, spec],
        out_specs=[spec] * 4, out_shape=[shape] * 4, compiler_params=_params(("arbitrary",)),
    )(parts, w, m, v)


def _ada_grad_adam(c_all, dmod_cols, w, m, v, tr):
    rows, cols = w.shape

    def body(c_ref, dm_ref, w_ref, m_ref, v_ref, g_ref, d_ref, nm_ref, nv_ref):
        cv = c_ref[...]
        act = cv * jax.nn.sigmoid(cv)
        g = lax.dot_general(act, dm_ref[...], TN, preferred_element_type=F32, precision=lax.Precision.HIGHEST)
        g_ref[...] = g
        d_ref[...], nm_ref[...], nv_ref[...] = _adam_math(w_ref[...], g, m_ref[...], v_ref[...])

    spec = pl.BlockSpec((tr, cols), lambda i: (i, 0))
    shape = jax.ShapeDtypeStruct((rows, cols), F32)
    return pl.pallas_call(
        body, name="ada_grad_adam", grid=(rows // tr,),
        in_specs=[pl.BlockSpec((N_DEV, tr), lambda i: (0, i)), pl.BlockSpec((N_DEV, cols), lambda i: (0, 0)), spec, spec, spec],
        out_specs=[spec] * 4, out_shape=[shape] * 4, compiler_params=_params(("arbitrary",)),
    )(c_all, dmod_cols, w, m, v)


def _rope_tables(positions):
    inv_freq = ROPE_THETA ** (-jnp.arange(0, 2 * ROT_HALF, 2, dtype=F32) / (2 * ROT_HALF))
    ang = positions.astype(F32)[:, None] * inv_freq
    rows = jnp.concatenate([jnp.cos(ang), jnp.sin(ang), jnp.ones_like(ang)], axis=1)
    spread = [[[0.0] * LANES for _ in range(3 * ROT_HALF)] for _ in range(3)]
    for lane in range(LANES):
        p, j = lane % HEAD_DIM, lane % ROT_HALF
        if p < ROT_HALF:
            spread[0][j][lane] = 1.0
            spread[1][ROT_HALF + j][lane] = -1.0
        elif p < 2 * ROT_HALF:
            spread[0][j][lane] = 1.0
            spread[2][ROT_HALF + j][lane] = 1.0
        else:
            spread[0][2 * ROT_HALF][lane] = 1.0
    return rows, jnp.array(spread, F32)


def _pad_rows(a, rows):
    return jnp.pad(a, ((0, rows - a.shape[0]), (0, 0)))


def _sequence_step(xs, target, rope, mods, gains, w_in_t, w_out_t, relay_ffn, fetch_ffn, send_grads, w_blk_b, b_pool_r,
                   pool_scale_r, conv_w_all, conv_b):
    tie = lambda a, token: a if token is None else a + token[0:1, 0:1]
    sh_m, sc_m, gt_m, sh_f, sc_f, gt_f = mods
    g_pre_mix, g_post_mix, g_pre_ffn, g_post_ffn = gains
    h1, u_pool, qkv = _premix_inproj(xs, sh_m, sc_m, g_pre_mix, w_in_t, rope, tm=512)
    o_g, lse_g = _attn_fwd(qkv)
    x1, y1, h2, cat, attn, lse_all = _mix_out(xs, u_pool, o_g, lse_g, w_blk_b, b_pool_r, pool_scale_r, w_out_t,
                                              gt_m, g_post_mix, g_pre_ffn, sc_f, sh_f, tm=256)
    token = relay_ffn(x1)
    w_up_t, w_down_f = fetch_ffn(x1 if token is None else token)
    gate, a_ffn, act, vd, dy2, dout, sums_ffn, loss_loc = _ffn_fwd_loss(h2, x1, target, w_up_t, w_down_f, conv_w_all, conv_b,
                                                              gt_f, g_post_ffn, tm=256, ck=256)

    dgc, dval, dw_down, dconv = _ffn_bwd_act(dy2, gate, a_ffn, act, vd, w_down_f, tm=512, tf=1408, ck=256)
    token = send_grads("w_down", dw_down)
    dup, dx1, dpool, dattn, delta, dw_out_t, sums_mix = _ffn_up_mix_bwd(
        dgc, dval, w_up_t, conv_w_all, dout, x1, y1, cat, attn, w_out_t, tie(sc_f, token), g_pre_ffn, gt_m, g_post_mix,
        tm=256)
    dw_up_t = _wgrad(dup, h2, "wgrad_up", tk=2048, tmm=1408)
    token = send_grads("w_up", dw_up_t)
    du, dw_blk, sums_pool = _pool_bwd(dpool, u_pool, w_blk_b, b_pool_r, tie(pool_scale_r, token), tm=512)
    dproj = _dproj_assemble(du, _attn_bwd(qkv, dattn, lse_all, delta), rope, tm=512)
    dw_in_t = _wgrad(dproj, h1, "wgrad_in", tk=2048, tmm=1280)
    token = send_grads("mix", dw_in_t, dw_out_t)
    grad_x, sums_in = _inproj_bwd(dproj, w_in_t, xs, dx1, tie(sc_m, token), g_pre_mix, tm=256)
    return (loss_loc, grad_x, dw_in_t, dw_out_t, dw_up_t, dw_down, dw_blk, dconv,
            sums_in, sums_mix, sums_ffn, sums_pool)


def kernel(x, c, positions, w_ada, b_ada, g_pre_mix, g_post_mix, g_pre_ffn, g_post_ffn, w_in, w_pool, b_pool, pool_scale, w_out, w_up, conv_w, conv_b, w_down, loss_target, m_w_ada, m_b_ada, m_g_pre_mix, m_g_post_mix, m_g_pre_ffn, m_g_post_ffn, m_w_in, m_w_pool, m_b_pool, m_pool_scale, m_w_out, m_w_up, m_conv_w, m_conv_b, m_w_down, v_w_ada, v_b_ada, v_g_pre_mix, v_g_post_mix, v_g_pre_ffn, v_g_post_ffn, v_w_in, v_w_pool, v_b_pool, v_pool_scale, v_w_out, v_w_up, v_conv_w, v_conv_b, v_w_down):
    s_len, d = x.shape[1], x.shape[2]
    d_ff = w_down.shape[1] * N_DEV
    me = _index(_place())
    xs, target = x[0], loss_target[0]

    ncol = w_ada.shape[2]
    b_cols = lax.dynamic_slice(b_ada, (0, me * ncol), (1, ncol))
    c_all, mod, taps_all, (w_in_t, w_out_t) = _entry_exchange(
        jnp.broadcast_to(c, (8, d)), w_ada[0], b_cols, _pad_rows(conv_w[0], 8),
        [w_in[0].T.astype(BF16), w_out[0].T.astype(BF16)])
    c_all = c_all[:, 0, :]
    conv_w_all = jnp.transpose(taps_all[:, :3, :], (1, 0, 2)).reshape(3, d_ff)
    sh_m, sc_m, gt_m, sh_f, sc_f, gt_f = [mod[:, 0, :].reshape(1, -1)[:, k * d:(k + 1) * d] for k in range(6)]

    rope = _rope_tables(positions[0])
    w_blk = jnp.zeros((256, 256), F32)
    for gi in range(4):
        w_blk = lax.dynamic_update_slice(w_blk, w_pool[0, gi], (gi * HEAD_DIM, gi * HEAD_DIM))
    w_blk_b = w_blk.astype(BF16)
    b_pool_r, pool_scale_r = b_pool.reshape(1, 256), pool_scale.reshape(1, 256)

    up_sh, down_sh = w_up[0].T.astype(BF16), w_down[0].astype(BF16)
    w_in_t, conv_w_all, up_sh, down_sh = lax.optimization_barrier((w_in_t, conv_w_all, up_sh, down_sh))
    lands = [_landing_zone("gather", s, me, "land_" + nm) for s, nm in ((up_sh, "w_up"), (down_sh, "w_down"))]
    w_send, w_recv, w_src, w_land, w_token = _exchange_start("gather_ici", [up_sh, down_sh], lands, "ffn_weights_ici_start")
    relay = []

    def relay_ffn(after):
        arrived = _exchange_wait("gather_ici", w_send, w_recv, w_src, w_land, after, "ffn_weights_ici_wait")
        relay.extend(_exchange_start("gather_d2d", [], arrived, "ffn_weights_d2d_start"))
        return relay[4]

    def fetch_ffn(after):
        return _exchange_wait("gather_d2d", relay[0], relay[1], [], relay[3], after, "ffn_weights_d2d_wait")

    flights = {}

    def send_grads(tag, *grads):
        lands = [_landing_zone("scatter", g, me, f"land_{tag}_{k}") for k, g in enumerate(grads)]
        flights[tag] = _exchange_start("scatter", list(grads), lands, f"grads_{tag}_start")
        return flights[tag][4]

    (loss_loc, grad_x, dw_in_t, dw_out_t, _, _, dw_pool, dconv,
     sums_in, sums_mix, sums_ffn, sums_pool) = _sequence_step(
        xs, target, rope, (sh_m + w_token[0:1, 0:1], sc_m, gt_m, sh_f, sc_f, gt_f),
        (g_pre_mix, g_post_mix, g_pre_ffn, g_post_ffn),
        w_in_t, w_out_t, relay_ffn, fetch_ffn, send_grads, w_blk_b, b_pool_r, pool_scale_r, conv_w_all, conv_b)

    small = [sums_in, sums_mix, sums_ffn, sums_pool, dw_pool, dconv, loss_loc]
    small_flight = _exchange_start("allgather", small, [lax.empty((N_DEV,) + a.shape, F32) for a in small], "small_start")

    parts_down, = _exchange_wait("scatter", *flights["w_down"][:4], small_flight[4], "grads_w_down_wait")
    big = {"w_down": _sum_adam(parts_down, w_down[0], m_w_down[0], v_w_down[0], "adam_w_down", 32)}
    parts_up, = _exchange_wait("scatter", *flights["w_up"][:4], big["w_down"][0], "grads_w_up_wait")
    big["w_up"] = [a.T for a in _sum_adam(parts_up, w_up[0].T, m_w_up[0].T, v_w_up[0].T, "adam_w_up", 64)]

    rep_w = [b_ada, g_pre_mix, g_post_mix, g_pre_ffn, g_post_ffn, w_pool, b_pool, pool_scale, conv_b]
    rep_m = [m_b_ada, m_g_pre_mix, m_g_post_mix, m_g_pre_ffn, m_g_post_ffn, m_w_pool, m_b_pool, m_pool_scale, m_conv_b]
    rep_v = [v_b_ada, v_g_pre_mix, v_g_post_mix, v_g_pre_ffn, v_g_post_ffn, v_w_pool, v_b_pool, v_pool_scale, v_conv_b]
    parts_mix = _exchange_wait("scatter", *flights["mix"][:4], big["w_up"][0], "grads_mix_wait")
    big["w_in"] = [a.T for a in _sum_adam(parts_mix[0], w_in[0].T, m_w_in[0].T, v_w_in[0].T, "adam_w_in", 64)]
    big["w_out"] = [a.T for a in _sum_adam(parts_mix[1], w_out[0].T, m_w_out[0].T, v_w_out[0].T, "adam_w_out", 128)]
    gathered = _exchange_wait("allgather", *small_flight[:4], big["w_out"][0], "small_wait")
    totals, dmod_all = _small_sum(small, gathered)
    dconv_tot, loss_tot = totals[5], totals[6]
    rep_out = _small_adam(totals, rep_w, rep_m, rep_v)
    g_rep, d_rep, nm_rep, nv_rep = (rep_out[k::4] for k in range(4))

    fcol = d_ff // N_DEV
    g_cw = lax.dynamic_slice(dconv_tot, (0, me * fcol), (3, fcol))
    d_cw, nm_cw, nv_cw = _adam(conv_w[0], g_cw, m_conv_w[0], v_conv_w[0], "adam_conv_w", 3)

    dmod_cols = lax.dynamic_slice(dmod_all, (0, me * ncol), (N_DEV, ncol))
    g_ada, d_ada, nm_ada, nv_ada = _ada_grad_adam(c_all, dmod_cols, w_ada[0], m_w_ada[0], v_w_ada[0], 256)

    loss = loss_tot[0, 0]

    def group(k):
        rep = (g_rep, d_rep, nm_rep, nv_rep)[k]
        ada = (g_ada, d_ada, nm_ada, nv_ada)[k][None]
        cw = (g_cw, d_cw, nm_cw, nv_cw)[k][None]
        return [ada, rep[0], rep[1], rep[2], rep[3], rep[4], big["w_in"][k][None], rep[5], rep[6], rep[7],
                big["w_out"][k][None], big["w_up"][k][None], cw, rep[8], big["w_down"][k][None]]

    return (loss, grad_x[None], *group(0), *group(1), *group(2), *group(3))
```

```python
import functools
import math

import jax
import jax.numpy as jnp
from jax import lax
from jax.experimental import pallas as pl
from jax.experimental.pallas import tpu as pltpu

F32 = jnp.float32
BF16 = jnp.bfloat16
MESH = pl.DeviceIdType.MESH

N_DEV = 8
HEAD_DIM = 64
ROT_HALF = 8
ROPE_THETA = 500000.0
POOL_WINDOWS = (2, 4, 8, 16)
DILATIONS = (1, 4, 16)
BLOCK = 128
NORM_EPS = 1e-6
HALO = 16
MASKED = -1e30
ATTN_FWD_UNROLL = 8
ATTN_BWD_UNROLL = 8

ADAM_LR = 0.001
ADAM_B1 = 0.9
ADAM_B2 = 0.999
ADAM_EPS = 1e-08
ADAM_WD = 0.01
ADAM_STEP = 10

V7X_VMEM_LIMIT = 56 * 1024 * 1024
LANES = 128

NT = (((1,), (1,)), ((), ()))
NN = (((1,), (0,)), ((), ()))
TN = (((0,), (0,)), ((), ()))


def _dot(a, b, dims):
    return lax.dot_general(a, b, dims, preferred_element_type=F32)


def _params(sem=None, vmem=V7X_VMEM_LIMIT):
    if sem is None:
        return pltpu.CompilerParams(vmem_limit_bytes=vmem)
    return pltpu.CompilerParams(dimension_semantics=sem, vmem_limit_bytes=vmem)


def _rstd(v):
    return lax.rsqrt(jnp.mean(v * v, axis=-1, keepdims=True) + NORM_EPS)


def _norm_bwd(dn, n, rstd):
    return rstd * (dn - n * jnp.mean(dn * n, axis=-1, keepdims=True))


def _rope_lanes(cs_ref, spread_ref):
    return [lax.dot_general(cs_ref[...], spread_ref[k], NN, preferred_element_type=F32, precision=lax.Precision.HIGHEST)
            for k in range(3)]


def _rope_fwd(p, lanes):
    return p * lanes[0] + pltpu.roll(p, LANES - ROT_HALF, 1) * lanes[1] + pltpu.roll(p, ROT_HALF, 1) * lanes[2]


def _rope_bwd(dp, lanes):
    return dp * lanes[0] + pltpu.roll(dp * lanes[1], ROT_HALF, 1) + pltpu.roll(dp * lanes[2], LANES - ROT_HALF, 1)


def _gelu_parts(v):
    k2 = 2.0 * math.sqrt(2.0 / math.pi)
    c = 0.044715
    v2 = v * v
    s = jax.nn.sigmoid(v * (k2 + (k2 * c) * v2))
    g = v * s
    dg = s + g * (1.0 - s) * (k2 + (3.0 * k2 * c) * v2)
    return g, dg


def _halo_before(i, tile):
    return jnp.maximum(i * (tile // HALO) - 1, 0)


def _premix_inproj(x, sh, sc, g, w_in_t, rope, tm):
    s_len, d = x.shape
    n_proj = w_in_t.shape[0]
    n_slab = (n_proj - 256) // LANES

    def body(x_ref, sh_ref, sc_ref, g_ref, w_ref, cs_ref, spread_ref, h_ref, up_ref, qkv_ref):
        xv = x_ref[...]
        h = (xv * _rstd(xv) * g_ref[...]) * (1.0 + sc_ref[...]) + sh_ref[...]
        hb = h.astype(BF16)
        h_ref[...] = hb
        up_ref[...] = _dot(hb, w_ref[0:256, :], NT)
        lanes = _rope_lanes(cs_ref, spread_ref)
        for pair in range(n_slab // 2):
            p = _dot(hb, w_ref[256 + 256 * pair:512 + 256 * pair, :], NT)
            for half in range(2):
                ph = p[:, half * LANES:(half + 1) * LANES]
                if pair < 6:
                    ph = _rope_fwd(ph, lanes)
                if pair < 3:
                    ph = ph * (HEAD_DIM ** -0.5)
                qkv_ref[2 * pair + half] = ph

    vec = pl.BlockSpec((1, d), lambda i: (0, 0))
    return pl.pallas_call(
        body, name="premix_inproj", grid=(s_len // tm,),
        in_specs=[pl.BlockSpec((tm, d), lambda i: (i, 0)), vec, vec, vec,
                  pl.BlockSpec((n_proj, d), lambda i: (0, 0)),
                  pl.BlockSpec((tm, rope[0].shape[1]), lambda i: (i, 0)), pl.BlockSpec(rope[1].shape, lambda i: (0, 0, 0))],
        out_specs=[pl.BlockSpec((tm, d), lambda i: (i, 0)),
                   pl.BlockSpec((tm, 256), lambda i: (i, 0)),
                   pl.BlockSpec((n_slab, tm, LANES), lambda i: (0, i, 0))],
        out_shape=[jax.ShapeDtypeStruct((s_len, d), BF16),
                   jax.ShapeDtypeStruct((s_len, 256), F32),
                   jax.ShapeDtypeStruct((n_slab, s_len, LANES), F32)],
        compiler_params=_params(("arbitrary",)),
    )(x, sh, sc, g, w_in_t, *rope)


def _block_rows(n, r, dil):
    start = n * (BLOCK * dil) + r
    if dil == 1:
        return pl.ds(pl.multiple_of(start, BLOCK), BLOCK)
    return pl.ds(start, BLOCK, stride=dil)


def _band_mask(n):
    ri = lax.broadcasted_iota(jnp.int32, (BLOCK, 2 * BLOCK), 0)
    cj = lax.broadcasted_iota(jnp.int32, (BLOCK, 2 * BLOCK), 1)
    cur = (cj >= BLOCK) & (cj - BLOCK <= ri)
    prev = (cj < BLOCK) & (cj >= ri) & (n > 0)
    return cur | prev


def _attn_fwd(qkv):
    s_len = qkv.shape[1]
    n_g = len(DILATIONS)

    def body(q_ref, k_ref, v_ref, o_ref, lse_ref):
        lane = lax.broadcasted_iota(jnp.int32, (BLOCK, LANES), 1)
        first = lane < HEAD_DIM

        def group(dil):
            nb = s_len // (BLOCK * dil)

            def block(t, carry):
                r, n = t // nb, t % nb
                cur = _block_rows(n, r, dil)
                prev = _block_rows(jnp.maximum(n - 1, 0), r, dil)
                q = q_ref[0, cur, :]
                kcat = jnp.concatenate([k_ref[0, prev, :], k_ref[0, cur, :]], axis=0).astype(BF16)
                vcat = jnp.concatenate([v_ref[0, prev, :], v_ref[0, cur, :]], axis=0).astype(BF16)
                valid = _band_mask(n)
                q2 = jnp.concatenate([jnp.where(first, q, 0.0), jnp.where(first, 0.0, q)], axis=0).astype(BF16)
                s = jnp.where(jnp.concatenate([valid, valid], axis=0), _dot(q2, kcat, NT), MASKED)
                m = jnp.max(s, axis=-1, keepdims=True)
                p = jnp.exp(s - m)
                den = jnp.sum(p, axis=-1, keepdims=True)
                o2 = _dot(p.astype(BF16), vcat, NN) / den
                lse2 = m + jnp.log(den)
                o_ref[0, 0, cur, :] = jnp.where(first, o2[:BLOCK], o2[BLOCK:])
                lse_ref[0, 0, cur, :] = jnp.where(first, lse2[:BLOCK], lse2[BLOCK:])
                return carry

            lax.fori_loop(0, nb * dil, block, 0, unroll=ATTN_FWD_UNROLL)

        for gi, dil in enumerate(DILATIONS):
            pl.when(pl.program_id(0) == gi)(functools.partial(group, dil))

    def slab(base):
        return pl.BlockSpec((1, s_len, LANES), lambda g, s: (base + 2 * g + s, 0, 0))

    out = pl.BlockSpec((1, 1, s_len, LANES), lambda g, s: (g, s, 0, 0))
    shape = jax.ShapeDtypeStruct((n_g, 2, s_len, LANES), F32)
    return pl.pallas_call(
        body, name="attn_fwd", grid=(n_g, 2),
        in_specs=[slab(0), slab(6), slab(12)], out_specs=[out, out], out_shape=[shape, shape],
        compiler_params=_params(("arbitrary", "arbitrary")),
    )(qkv, qkv, qkv)


def _pool_mixed(u, halo, i, tm):
    ue = jnp.concatenate([halo, u], axis=0)
    s2 = ue + pltpu.roll(ue, 1, 0)
    s4 = s2 + pltpu.roll(s2, 2, 0)
    s8 = s4 + pltpu.roll(s4, 4, 0)
    s16 = s8 + pltpu.roll(s8, 8, 0)
    grp = lax.broadcasted_iota(jnp.int32, (tm, 256), 1) // HEAD_DIM
    pick = lambda a, b, c, e: jnp.where(grp == 0, a, jnp.where(grp == 1, b, jnp.where(grp == 2, c, e)))
    win_sum = pick(s2[HALO:], s4[HALO:], s8[HALO:], s16[HALO:])
    pos = (i * tm + lax.broadcasted_iota(jnp.int32, (tm, 256), 0)).astype(F32)
    count = jnp.minimum(pos + 1.0, pick(*[float(w) for w in POOL_WINDOWS]))
    return win_sum / count - u, count


def _mix_out(x, u_pool, o_g, lse_g, w_blk, b_pool, pool_scale, w_out_t, gt_m, g_post_mix, g_pre_ffn, sc_f, sh_f, tm):
    s_len, d = x.shape

    def body(x_ref, u_ref, uh_ref, o_ref, l_ref, wb_ref, bp_ref, ps_ref, wo_ref,
             gt_ref, g1_ref, g2_ref, sc_ref, sh_ref,
             x1_ref, y1_ref, h2_ref, cat_ref, attn_ref, lall_ref):
        (o0, o1, o2), (l0, l1, l2) = (o_ref.at[g] for g in range(3)), (l_ref.at[g] for g in range(3))
        i = pl.program_id(0)
        u = u_ref[...]
        halo = uh_ref[...] * (i > 0).astype(F32)
        mixed, _ = _pool_mixed(u, halo, i, tm)
        y = _dot(mixed.astype(BF16), wb_ref[...], NN) + bp_ref[...]
        pool = y * ps_ref[...]
        attn = []
        for s in range(2):
            la, lb, lc = l0[s], l1[s], l2[s]
            mx = jnp.maximum(jnp.maximum(la, lb), lc)
            ea, eb, ec = jnp.exp(la - mx), jnp.exp(lb - mx), jnp.exp(lc - mx)
            den = ea + eb + ec
            lall_ref[s] = mx + jnp.log(den)
            attn.append((ea / den) * o0[s] + (eb / den) * o1[s] + (ec / den) * o2[s])
        attn = jnp.concatenate(attn, axis=1)
        attn_ref[...] = attn
        cat = jnp.concatenate([pool, attn], axis=1).astype(BF16)
        cat_ref[...] = cat
        y1 = _dot(cat, wo_ref[...], NT)
        y1_ref[...] = y1.astype(BF16)
        x1 = x_ref[...] + gt_ref[...] * (y1 * _rstd(y1) * g1_ref[...])
        x1_ref[...] = x1
        h2 = (x1 * _rstd(x1) * g2_ref[...]) * (1.0 + sc_ref[...]) + sh_ref[...]
        h2_ref[...] = h2.astype(BF16)

    tile = lambda w: pl.BlockSpec((tm, w), lambda i: (i, 0))
    slab = pl.BlockSpec((2, tm, LANES), lambda i: (0, i, 0))
    groups = pl.BlockSpec((len(DILATIONS), 2, tm, LANES), lambda i: (0, 0, i, 0))
    const = lambda a: pl.BlockSpec(a.shape, lambda i: (0,) * a.ndim)
    return pl.pallas_call(
        body, name="mix_out", grid=(s_len // tm,),
        in_specs=[tile(d), tile(256), pl.BlockSpec((HALO, 256), lambda i: (_halo_before(i, tm), 0)),
                  groups, groups,
                  const(w_blk), const(b_pool), const(pool_scale), const(w_out_t),
                  const(gt_m), const(g_post_mix), const(g_pre_ffn), const(sc_f), const(sh_f)],
        out_specs=[tile(d), tile(d), tile(d), tile(512), tile(256), slab],
        out_shape=[jax.ShapeDtypeStruct((s_len, d), F32), jax.ShapeDtypeStruct((s_len, d), BF16),
                   jax.ShapeDtypeStruct((s_len, d), BF16), jax.ShapeDtypeStruct((s_len, 512), BF16),
                   jax.ShapeDtypeStruct((s_len, 256), F32), jax.ShapeDtypeStruct((2, s_len, LANES), F32)],
        compiler_params=_params(("arbitrary",)),
    )(x, u_pool, u_pool, o_g, lse_g, w_blk, b_pool, pool_scale, w_out_t, gt_m, g_post_mix, g_pre_ffn, sc_f, sh_f)


def _conv_gate(gate_ext, cw, cb):
    gc = gate_ext * cw[2:3, :] + pltpu.roll(gate_ext, 1, 0) * cw[1:2, :] + pltpu.roll(gate_ext, 2, 0) * cw[0:1, :]
    return gc[HALO:] + cb


def _ffn_fwd_loss(h2, x1, target, w_up_t, w_down, conv_w, conv_b, gt_f, g_post_ffn, tm, ck):
    s_len, d = x1.shape
    d_ff = w_down.shape[0]
    n_t, n_c = s_len // tm, d_ff // ck

    def body(h_ref, hh_ref, x1_ref, tgt_ref, wg_ref, wv_ref, wd_ref, cw_ref, cb_ref, gt_ref, g_ref,
             gate_ref, a_ref, act_ref, vd_ref, dy2_ref, dout_ref, sums_ref, loss_ref, acc_ref):
        i = pl.program_id(0)

        @pl.when(i == 0)
        def _():
            sums_ref[...] = jnp.zeros_like(sums_ref)
            loss_ref[...] = jnp.zeros_like(loss_ref)
            acc_ref[...] = jnp.zeros_like(acc_ref)

        def finish(live):
            y2 = acc_ref[...]
            rstd = _rstd(y2)
            n = y2 * rstd
            rn = n * g_ref[...]
            err = x1_ref[...] + gt_ref[...] * rn - tgt_ref[...]
            keep = lambda v: jnp.where(live, v, 0.0)
            loss_ref[...] += keep(0.5 * jnp.sum(jnp.mean(err * err, axis=-1, keepdims=True), axis=0, keepdims=True))
            dout = err * (1.0 / d)
            dout_ref[...] = dout
            drn = dout * gt_ref[...]
            sums_ref[0:1, :] += keep(jnp.sum(dout * rn, axis=0, keepdims=True))
            sums_ref[1:2, :] += keep(jnp.sum(drn * n, axis=0, keepdims=True))
            dy2_ref[...] = _norm_bwd(drn * g_ref[...], n, rstd).astype(BF16)

        @pl.when(i < n_t)
        def _():
            h = h_ref[...]
            h_ext = jnp.concatenate([hh_ref[...], h], axis=0)
            row = lax.broadcasted_iota(jnp.int32, (tm + HALO, ck), 0)
            no_halo = (row < HALO) & (i == 0)

            def up(c):
                cs = slice(c * ck, (c + 1) * ck)
                return jnp.where(no_halo, 0.0, _dot(h_ext, wg_ref[cs, :], NT)), _dot(h, wv_ref[cs, :], NT)

            part = None
            nxt = up(0)
            finish(i > 0)
            for c in range(n_c):
                cs = slice(c * ck, (c + 1) * ck)
                gate_ext, val = nxt
                if c + 1 < n_c:
                    nxt = up(c + 1)
                act, dact = _gelu_parts(_conv_gate(gate_ext, cw_ref[:, cs], cb_ref[:, cs]))
                a = (act * val).astype(BF16)
                gate_ref[:, cs] = gate_ext[HALO:].astype(BF16)
                a_ref[:, cs] = a
                act_ref[:, cs] = act.astype(BF16)
                vd_ref[:, cs] = (val * dact).astype(BF16)
                p = _dot(a, wd_ref[cs, :], NN)
                part = p if part is None else part + p
            acc_ref[...] = part

        @pl.when(i == n_t)
        def _():
            finish(True)

    this = lambda i: jnp.minimum(i, n_t - 1)
    before = lambda i: jnp.maximum(i - 1, 0)
    tok = lambda w, at: pl.BlockSpec((tm, w), lambda i: (at(i), 0))
    vec = pl.BlockSpec((1, d), lambda i: (0, 0))
    once = lambda shape, imap: pl.BlockSpec(shape, imap, pipeline_mode=pl.Buffered(1))
    return pl.pallas_call(
        body, name="ffn_fwd_loss", grid=(n_t + 1,),
        in_specs=[tok(d, this), pl.BlockSpec((HALO, d), lambda i: (_halo_before(this(i), tm), 0)),
                  tok(d, before), tok(d, before),
                  once((d_ff, d), lambda i: (0, 0)), once((d_ff, d), lambda i: (1, 0)), once((d_ff, d), lambda i: (0, 0)),
                  pl.BlockSpec((3, d_ff), lambda i: (0, 0)), pl.BlockSpec((1, d_ff), lambda i: (0, 0)), vec, vec],
        out_specs=[tok(d_ff, this)] * 4 + [tok(d, before), tok(d, before), pl.BlockSpec((8, d), lambda i: (0, 0)),
                                          pl.BlockSpec((8, LANES), lambda i: (0, 0))],
        out_shape=[jax.ShapeDtypeStruct((s_len, d_ff), BF16)] * 4
        + [jax.ShapeDtypeStruct((s_len, d), BF16), jax.ShapeDtypeStruct((s_len, d), F32),
           jax.ShapeDtypeStruct((8, d), F32), jax.ShapeDtypeStruct((8, LANES), F32)],
        scratch_shapes=[pltpu.VMEM((tm, d), F32)],
        compiler_params=_params(("arbitrary",)),
    )(h2, h2, x1, target, w_up_t, w_up_t, w_down, conv_w, conv_b, gt_f, g_post_ffn)


def _ffn_bwd_act(dy2, gate, a, act, vd, w_down, tm, tf, ck):
    s_len, d = dy2.shape
    d_ff = w_down.shape[0]
    n_t = s_len // tm
    chunks = [slice(lo, min(lo + ck, tf)) for lo in range(0, tf, ck)]

    def body(dy_ref, g_ref, gh_ref, a_ref, act_ref, vd_ref, wd_ref, dgc_ref, dval_ref, dwd_ref, dconv_ref, acc_ref):
        i = pl.program_id(1)

        @pl.when(i == 0)
        def _():
            acc_ref[...] = jnp.zeros_like(acc_ref)
            dconv_ref[...] = jnp.zeros_like(dconv_ref)

        dy = dy_ref[...]

        def down(cs):
            return _dot(dy, wd_ref[cs, :], NT)

        nxt = down(chunks[0])
        for c, cs in enumerate(chunks):
            width = cs.stop - cs.start
            da = nxt
            if c + 1 < len(chunks):
                nxt = down(chunks[c + 1])
            acc_ref[cs, :] += _dot(a_ref[:, cs], dy, TN)
            row = lax.broadcasted_iota(jnp.int32, (tm + HALO, width), 0)
            gate_ext = jnp.where((row < HALO) & (i == 0), 0.0,
                                 jnp.concatenate([gh_ref[:, cs], g_ref[:, cs]], axis=0).astype(F32))
            dgc = da * vd_ref[:, cs].astype(F32)
            dgc_ref[:, cs] = dgc.astype(BF16)
            dval_ref[:, cs] = (da * act_ref[:, cs].astype(F32)).astype(BF16)
            rows = [jnp.sum(dgc * pltpu.roll(gate_ext, 2 - k, 0)[HALO:], axis=0, keepdims=True) for k in range(2)]
            rows += [jnp.sum(dgc * gate_ext[HALO:], axis=0, keepdims=True), jnp.sum(dgc, axis=0, keepdims=True),
                     jnp.zeros((4, width), F32)]
            dconv_ref[:, cs] += jnp.concatenate(rows, axis=0)

        @pl.when(i == n_t - 1)
        def _():
            dwd_ref[...] = acc_ref[...].astype(BF16)

    tokf = pl.BlockSpec((tm, tf), lambda j, i: (i, j))
    return pl.pallas_call(
        body, name="ffn_bwd_act", grid=(d_ff // tf, n_t),
        in_specs=[pl.BlockSpec((tm, d), lambda j, i: (i, 0)), tokf,
                  pl.BlockSpec((HALO, tf), lambda j, i: (_halo_before(i, tm), j)), tokf, tokf, tokf,
                  pl.BlockSpec((tf, d), lambda j, i: (j, 0))],
        out_specs=[tokf, tokf, pl.BlockSpec((tf, d), lambda j, i: (j, 0)), pl.BlockSpec((8, tf), lambda j, i: (0, j))],
        out_shape=[jax.ShapeDtypeStruct((s_len, d_ff), BF16), jax.ShapeDtypeStruct((s_len, d_ff), BF16),
                   jax.ShapeDtypeStruct((d_ff, d), BF16), jax.ShapeDtypeStruct((8, d_ff), F32)],
        scratch_shapes=[pltpu.VMEM((tf, d), F32)],
        compiler_params=_params(("arbitrary", "arbitrary")),
    )(dy2, gate, gate, a, act, vd, w_down)


def _ffn_bwd_up(dgc, dval, w_up_t, conv_w, tm):
    s_len, d_ff = dgc.shape
    d = w_up_t.shape[1]
    n_t = s_len // tm

    def body(dg_ref, dgn_ref, dv_ref, cw_ref, w_ref, dup_ref, dh_ref):
        i = pl.program_id(0)
        nxt = dgn_ref[...].astype(F32) * (i < n_t - 1).astype(F32)
        ext = jnp.concatenate([dg_ref[...].astype(F32), nxt], axis=0)
        rows = tm + HALO
        dgate = (ext * cw_ref[2:3, :] + pltpu.roll(ext, rows - 1, 0) * cw_ref[1:2, :]
                 + pltpu.roll(ext, rows - 2, 0) * cw_ref[0:1, :])[:tm]
        dup = jnp.concatenate([dgate.astype(BF16), dv_ref[...]], axis=1)
        dup_ref[...] = dup
        dh_ref[...] = _dot(dup, w_ref[...], NN).astype(BF16)

    tokf = pl.BlockSpec((tm, d_ff), lambda i: (i, 0))
    return pl.pallas_call(
        body, name="ffn_bwd_up", grid=(n_t,),
        in_specs=[tokf, pl.BlockSpec((HALO, d_ff), lambda i: (jnp.minimum((i + 1) * (tm // HALO), s_len // HALO - 1), 0)),
                  tokf, pl.BlockSpec((3, d_ff), lambda i: (0, 0)), pl.BlockSpec((2 * d_ff, d), lambda i: (0, 0))],
        out_specs=[pl.BlockSpec((tm, 2 * d_ff), lambda i: (i, 0)), pl.BlockSpec((tm, d), lambda i: (i, 0))],
        out_shape=[jax.ShapeDtypeStruct((s_len, 2 * d_ff), BF16), jax.ShapeDtypeStruct((s_len, d), BF16)],
        compiler_params=_params(("arbitrary",)),
    )(dgc, dgc, dval, conv_w, w_up_t)


def _mix_bwd(dh2, dout, x1, y1, cat, attn, w_out_t, sc_f, g_pre_ffn, gt_m, g_post_mix, tm):
    s_len, d = x1.shape
    n_t = s_len // tm

    def body(dh_ref, do_ref, x1_ref, y1_ref, cat_ref, at_ref, wo_ref, sc_ref, g2_ref, gt_ref, g1_ref,
             dx1_ref, dpool_ref, dattn_ref, delta_ref, dwo_ref, sums_ref, acc_ref):
        i = pl.program_id(0)
        dh = dh_ref[...].astype(F32)
        x1 = x1_ref[...]
        r2 = _rstd(x1)
        n2 = x1 * r2
        ng = n2 * g2_ref[...]
        dng = dh * (1.0 + sc_ref[...])
        dx1 = do_ref[...] + _norm_bwd(dng * g2_ref[...], n2, r2)
        dx1_ref[...] = dx1
        y1 = y1_ref[...].astype(F32)
        r1 = _rstd(y1)
        n1 = y1 * r1
        drn = dx1 * gt_ref[...]
        dy1 = _norm_bwd(drn * g1_ref[...], n1, r1).astype(BF16)
        dcat = _dot(dy1, wo_ref[...], NN)
        dpool_ref[...] = dcat[:, 0:256]
        lane = lax.broadcasted_iota(jnp.int32, (tm, LANES), 1)
        first = lane < HEAD_DIM
        for s in range(2):
            da = dcat[:, 256 + s * LANES:256 + (s + 1) * LANES]
            dattn_ref[s] = da
            prod = da * at_ref[:, s * LANES:(s + 1) * LANES]
            tot = jnp.sum(prod, axis=-1, keepdims=True)
            lo = jnp.sum(jnp.where(first, prod, 0.0), axis=-1, keepdims=True)
            delta_ref[s] = jnp.where(first, lo, tot - lo)
        dwo = _dot(dy1, cat_ref[...], TN)
        sums = jnp.concatenate(
            [jnp.sum(dh, axis=0, keepdims=True), jnp.sum(dh * ng, axis=0, keepdims=True),
             jnp.sum(dng * n2, axis=0, keepdims=True), jnp.sum(dx1 * (n1 * g1_ref[...]), axis=0, keepdims=True),
             jnp.sum(drn * n1, axis=0, keepdims=True), jnp.zeros((3, d), F32)], axis=0)

        @pl.when(i == 0)
        def _():
            acc_ref[...] = dwo
            sums_ref[...] = sums

        @pl.when(i > 0)
        def _():
            acc_ref[...] += dwo
            sums_ref[...] += sums

        @pl.when(i == n_t - 1)
        def _():
            dwo_ref[...] = acc_ref[...].astype(BF16)

    tile = lambda w: pl.BlockSpec((tm, w), lambda i: (i, 0))
    slab = pl.BlockSpec((2, tm, LANES), lambda i: (0, i, 0))
    vec = pl.BlockSpec((1, d), lambda i: (0, 0))
    return pl.pallas_call(
        body, name="mix_bwd", grid=(n_t,),
        in_specs=[tile(d), tile(d), tile(d), tile(d), tile(512), tile(256),
                  pl.BlockSpec((d, 512), lambda i: (0, 0)), vec, vec, vec, vec],
        out_specs=[tile(d), tile(256), slab, slab, pl.BlockSpec((d, 512), lambda i: (0, 0)),
                   pl.BlockSpec((8, d), lambda i: (0, 0))],
        out_shape=[jax.ShapeDtypeStruct((s_len, d), F32), jax.ShapeDtypeStruct((s_len, 256), F32),
                   jax.ShapeDtypeStruct((2, s_len, LANES), F32), jax.ShapeDtypeStruct((2, s_len, LANES), F32),
                   jax.ShapeDtypeStruct((d, 512), BF16), jax.ShapeDtypeStruct((8, d), F32)],
        scratch_shapes=[pltpu.VMEM((d, 512), F32)],
        compiler_params=_params(("arbitrary",)),
    )(dh2, dout, x1, y1, cat, attn, w_out_t, sc_f, g_pre_ffn, gt_m, g_post_mix)


def _pool_bwd(dpool, u_pool, w_blk, b_pool, pool_scale, tm):
    s_len = dpool.shape[0]
    n_t = s_len // tm

    def body(dp_ref, dpn_ref, u_ref, uh_ref, wb_ref, bp_ref, ps_ref, du_ref, dwp_ref, sums_ref, acc_ref):
        i = pl.program_id(0)
        u = u_ref[...]
        mixed, _ = _pool_mixed(u, uh_ref[...] * (i > 0).astype(F32), i, tm)
        mixed_b = mixed.astype(BF16)
        y = _dot(mixed_b, wb_ref[...], NN) + bp_ref[...]
        dp = dp_ref[...]
        dy = dp * ps_ref[...]
        dwb = _dot(mixed_b, dy.astype(BF16), TN)
        sums = jnp.concatenate([jnp.sum(dy, axis=0, keepdims=True), jnp.sum(dp * y, axis=0, keepdims=True),
                                jnp.zeros((6, 256), F32)], axis=0)
        dp_ext = jnp.concatenate([dp, dpn_ref[...] * (i < n_t - 1).astype(F32)], axis=0)
        dmix = _dot((dp_ext * ps_ref[...]).astype(BF16), wb_ref[...], NT)
        rows = tm + HALO
        grp = lax.broadcasted_iota(jnp.int32, (rows, 256), 1) // HEAD_DIM
        pick = lambda a, b, c, e: jnp.where(grp == 0, a, jnp.where(grp == 1, b, jnp.where(grp == 2, c, e)))
        pos = (i * tm + lax.broadcasted_iota(jnp.int32, (rows, 256), 0)).astype(F32)
        z = dmix / jnp.minimum(pos + 1.0, pick(*[float(w) for w in POOL_WINDOWS]))
        f2 = z + pltpu.roll(z, rows - 1, 0)
        f4 = f2 + pltpu.roll(f2, rows - 2, 0)
        f8 = f4 + pltpu.roll(f4, rows - 4, 0)
        f16 = f8 + pltpu.roll(f8, rows - 8, 0)
        du_ref[...] = (pick(f2, f4, f8, f16) - dmix)[:tm]

        @pl.when(i == 0)
        def _():
            acc_ref[...] = dwb
            sums_ref[...] = sums

        @pl.when(i > 0)
        def _():
            acc_ref[...] += dwb
            sums_ref[...] += sums

        @pl.when(i == n_t - 1)
        def _():
            full = acc_ref[...]
            for gi in range(len(POOL_WINDOWS)):
                lo = gi * HEAD_DIM
                dwp_ref[gi] = full[lo:lo + HEAD_DIM, lo:lo + HEAD_DIM]

    n_g = len(POOL_WINDOWS)
    tile = pl.BlockSpec((tm, 256), lambda i: (i, 0))
    const = lambda a: pl.BlockSpec(a.shape, lambda i: (0,) * a.ndim)
    return pl.pallas_call(
        body, name="pool_bwd", grid=(n_t,),
        in_specs=[tile, pl.BlockSpec((HALO, 256), lambda i: (jnp.minimum((i + 1) * (tm // HALO), s_len // HALO - 1), 0)),
                  tile, pl.BlockSpec((HALO, 256), lambda i: (_halo_before(i, tm), 0)),
                  const(w_blk), const(b_pool), const(pool_scale)],
        out_specs=[tile, pl.BlockSpec((n_g, HEAD_DIM, HEAD_DIM), lambda i: (0, 0, 0)), pl.BlockSpec((8, 256), lambda i: (0, 0))],
        out_shape=[jax.ShapeDtypeStruct((s_len, 256), F32), jax.ShapeDtypeStruct((n_g, HEAD_DIM, HEAD_DIM), F32),
                   jax.ShapeDtypeStruct((8, 256), F32)],
        scratch_shapes=[pltpu.VMEM((256, 256), F32)],
        compiler_params=_params(("arbitrary",)),
    )(dpool, dpool, u_pool, u_pool, w_blk, b_pool, pool_scale)


def _attn_bwd(qkv, dattn, lse_all, delta, after):
    s_len = qkv.shape[1]
    n_g = len(DILATIONS)

    def body(q_ref, k_ref, v_ref, do_ref, l_ref, dl_ref, after_ref, dq_ref, dk_ref, dv_ref):
        lane = lax.broadcasted_iota(jnp.int32, (BLOCK, LANES), 1)
        first = lane < HEAD_DIM

        def group(dil):
            nb = s_len // (BLOCK * dil)

            def block(t, carry):
                dk_part, dv_part = carry
                r, n = t // nb, t % nb
                cur = _block_rows(n, r, dil)
                prev = _block_rows(jnp.maximum(n - 1, 0), r, dil)
                q = q_ref[0, cur, :]
                do = do_ref[0, cur, :]
                lse = l_ref[0, cur, :]
                dlt = dl_ref[0, cur, :]
                kcat = jnp.concatenate([k_ref[0, prev, :], k_ref[0, cur, :]], axis=0).astype(BF16)
                vcat = jnp.concatenate([v_ref[0, prev, :], v_ref[0, cur, :]], axis=0).astype(BF16)
                valid = _band_mask(n)
                stack = lambda a: jnp.concatenate([jnp.where(first, a, 0.0), jnp.where(first, 0.0, a)], axis=0)
                rows2 = lambda a: jnp.concatenate([a[:, 0:1], a[:, HEAD_DIM:HEAD_DIM + 1]], axis=0)
                q2, do2 = stack(q).astype(BF16), stack(do).astype(BF16)
                valid2 = jnp.concatenate([valid, valid], axis=0)
                p = jnp.where(valid2, jnp.exp(_dot(q2, kcat, NT) - rows2(lse)), 0.0)
                ds = (p * (_dot(do2, vcat, NT) - rows2(dlt))).astype(BF16)
                dq2 = _dot(ds, kcat, NN)
                dq_ref[0, 0, cur, :] = jnp.where(first, dq2[:BLOCK], dq2[BLOCK:])
                dkc = _dot(ds, q2, TN)
                dvc = _dot(p.astype(BF16), do2, TN)
                dk_ref[0, 0, prev, :] = dk_part + dkc[:BLOCK]
                dv_ref[0, 0, prev, :] = dv_part + dvc[:BLOCK]
                dk_ref[0, 0, cur, :] = dkc[BLOCK:]
                dv_ref[0, 0, cur, :] = dvc[BLOCK:]
                return dkc[BLOCK:], dvc[BLOCK:]

            def blocks(tt, carry):
                for u in range(ATTN_BWD_UNROLL):
                    carry = block(tt * ATTN_BWD_UNROLL + u, carry)
                return carry

            zero = jnp.zeros((BLOCK, LANES), F32)
            lax.fori_loop(0, nb * dil // ATTN_BWD_UNROLL, blocks, (zero, zero))

        for gi, dil in enumerate(DILATIONS):
            pl.when(pl.program_id(1) == gi)(functools.partial(group, dil))

    def slab(base):
        return pl.BlockSpec((1, s_len, LANES), lambda s, g: (base + 2 * g + s, 0, 0))

    one = pl.BlockSpec((1, s_len, LANES), lambda s, g: (s, 0, 0))
    out = pl.BlockSpec((1, 1, s_len, LANES), lambda s, g: (g, s, 0, 0))
    shape = jax.ShapeDtypeStruct((n_g, 2, s_len, LANES), F32)
    return pl.pallas_call(
        body, name="attn_bwd", grid=(2, n_g),
        in_specs=[slab(0), slab(6), slab(12), one, one, one, pl.BlockSpec(memory_space=pl.ANY)],
        out_specs=[out, out, out], out_shape=[shape, shape, shape],
        compiler_params=_params(("arbitrary", "arbitrary")),
    )(qkv, qkv, qkv, dattn, lse_all, delta, after)


def _dproj_assemble(du, dqkv, rope, tm):
    s_len = du.shape[0]
    n_proj = 256 + 18 * LANES

    def body(du_ref, dq_ref, dk_ref, dv_ref, cs_ref, spread_ref, dproj_ref):
        dproj_ref[:, 0:256] = du_ref[...].astype(BF16)
        lanes = _rope_lanes(cs_ref, spread_ref)
        col = 256
        for kind, dref in enumerate((dq_ref, dk_ref, dv_ref)):
            for grp in range(3):
                for s in range(2):
                    piece = dref[grp, s]
                    if kind < 2:
                        piece = _rope_bwd(piece, lanes)
                    if kind == 0:
                        piece = piece * (HEAD_DIM ** -0.5)
                    dproj_ref[:, col:col + LANES] = piece.astype(BF16)
                    col += LANES

    groups = pl.BlockSpec((len(DILATIONS), 2, tm, LANES), lambda i: (0, 0, i, 0))
    return pl.pallas_call(
        body, name="dproj_assemble", grid=(s_len // tm,),
        in_specs=[pl.BlockSpec((tm, 256), lambda i: (i, 0))] + [groups] * 3
        + [pl.BlockSpec((tm, rope[0].shape[1]), lambda i: (i, 0)), pl.BlockSpec(rope[1].shape, lambda i: (0, 0, 0))],
        out_specs=pl.BlockSpec((tm, n_proj), lambda i: (i, 0)),
        out_shape=jax.ShapeDtypeStruct((s_len, n_proj), BF16),
        compiler_params=_params(("arbitrary",)),
    )(du, *dqkv, *rope)


def _inproj_bwd(dproj, w_in_t, x, dx1, sc_m, g_pre_mix, tm):
    s_len, d = x.shape
    n_proj = w_in_t.shape[0]
    n_t = s_len // tm

    def body(dproj_ref, w_ref, x_ref, dx1_ref, sc_ref, g_ref, dx_ref, sums_ref):
        i = pl.program_id(0)
        halves = [slice(0, tm // 2), slice(tm // 2, tm)]
        dhs = [_dot(dproj_ref[rs, :], w_ref[...], NN) for rs in halves]
        sums = None
        for rs, dh in zip(halves, dhs):
            xv = x_ref[rs, :]
            r = _rstd(xv)
            n = xv * r
            dng = dh * (1.0 + sc_ref[...])
            dx_ref[rs, :] = dx1_ref[rs, :] + _norm_bwd(dng * g_ref[...], n, r)
            part = jnp.concatenate([jnp.sum(dh, axis=0, keepdims=True), jnp.sum(dh * (n * g_ref[...]), axis=0, keepdims=True),
                                    jnp.sum(dng * n, axis=0, keepdims=True), jnp.zeros((5, d), F32)], axis=0)
            sums = part if sums is None else sums + part

        @pl.when(i == 0)
        def _():
            sums_ref[...] = sums

        @pl.when(i > 0)
        def _():
            sums_ref[...] += sums

    tile = lambda w: pl.BlockSpec((tm, w), lambda i: (i, 0))
    vec = pl.BlockSpec((1, d), lambda i: (0, 0))
    return pl.pallas_call(
        body, name="inproj_bwd", grid=(n_t,),
        in_specs=[tile(n_proj), pl.BlockSpec((n_proj, d), lambda i: (0, 0)), tile(d), tile(d), vec, vec],
        out_specs=[tile(d), pl.BlockSpec((8, d), lambda i: (0, 0))],
        out_shape=[jax.ShapeDtypeStruct((s_len, d), F32), jax.ShapeDtypeStruct((8, d), F32)],
        compiler_params=_params(("arbitrary",)),
    )(dproj, w_in_t, x, dx1, sc_m, g_pre_mix)


def _wgrad(a, b, name, tk, tmm):
    s_len, m = a.shape
    n = b.shape[1]
    n_k = s_len // tk

    def body(a_ref, b_ref, o_ref, acc_ref):
        k = pl.program_id(1)
        part = _dot(a_ref[...], b_ref[...], TN)

        @pl.when(k == 0)
        def _():
            acc_ref[...] = part

        @pl.when(k > 0)
        def _():
            acc_ref[...] += part

        @pl.when(k == n_k - 1)
        def _():
            o_ref[...] = acc_ref[...].astype(BF16)

    return pl.pallas_call(
        body, name=name, grid=(m // tmm, n_k),
        in_specs=[pl.BlockSpec((tk, tmm), lambda j, k: (k, j)), pl.BlockSpec((tk, n), lambda j, k: (k, 0))],
        out_specs=pl.BlockSpec((tmm, n), lambda j, k: (j, 0)),
        out_shape=jax.ShapeDtypeStruct((m, n), BF16),
        scratch_shapes=[pltpu.VMEM((tmm, n), F32)],
        compiler_params=_params(("arbitrary", "arbitrary")),
    )(a, b)


def _place():
    return lax.axis_index("x"), lax.axis_index("y"), lax.axis_index("c")


def _peer(k):
    x, y, c = _place()
    bx, by, bc = (k >> 2) & 1, (k >> 1) & 1, k & 1
    return (x ^ bx if bx else x, y ^ by if by else y, c ^ bc if bc else c)


def _index(pos):
    return 4 * pos[0] + 2 * pos[1] + pos[2]


def _entry_exchange(c_rows, w_ada, b_ada_cols, taps, shards):
    d = c_rows.shape[1]
    ncol = w_ada.shape[1]
    n_w = len(shards)

    def body(c_ref, w_ref, b_ref, t_ref, *rest):
        srcs = rest[:n_w]
        call_ref, mod_ref, tall_ref = rest[n_w:n_w + 3]
        outs = rest[n_w + 3:2 * n_w + 3]
        stage_ref, s_send, s_recv, w_send, w_recv, local_sems = rest[2 * n_w + 3:]
        x, y, c = _place()
        here, sibling = (x, y, c), (x, y, 1 - c)
        chips = [(1 - x, y), (x, 1 - y), (1 - x, 1 - y)]
        me = _index(here)

        def small(kind, src, dst, k):
            return pltpu.make_async_remote_copy(src_ref=src, dst_ref=dst, send_sem=s_send.at[kind, k - 1],
                                                recv_sem=s_recv.at[kind, k - 1], device_id=_peer(k), device_id_type=MESH)

        gather = lambda k: small(0, c_ref, call_ref.at[me], k)
        scatter = lambda k: small(1, stage_ref.at[_index(_peer(k))], mod_ref.at[me], k)
        gather_taps = lambda k: small(2, t_ref, tall_ref.at[me], k)

        def rows(w, pos):
            r = shards[w].shape[0]
            return outs[w].at[pl.ds(pl.multiple_of(_index(pos) * r, 16), r), :]

        def block(k, w, pos, to, own=False):
            return pltpu.make_async_remote_copy(
                src_ref=srcs[w] if own else rows(w, pos), dst_ref=rows(w, pos),
                send_sem=w_send.at[k, w], recv_sem=w_recv.at[k, w], device_id=to, device_id_type=MESH)

        call_ref[me] = c_ref[...]
        tall_ref[me] = t_ref[...]
        for k in range(1, N_DEV):
            gather(k).start()
        for k in range(1, N_DEV):
            gather_taps(k).start()
        mine = [pltpu.make_async_copy(srcs[w], rows(w, here), local_sems.at[w]) for w in range(n_w)]
        for cp in mine:
            cp.start()
        first = [block(0, w, here, sibling, own=True) for w in range(n_w)]
        first += [block(1 + j, w, here, (*chip, c), own=True) for j, chip in enumerate(chips) for w in range(n_w)]
        for cp in first:
            cp.start()

        for k in range(1, N_DEV):
            gather(k).wait_recv()
        cv = jnp.concatenate([call_ref[b, 0:1, :] for b in range(N_DEV)], axis=0)
        act = cv * jax.nn.sigmoid(cv)
        mod = lax.dot_general(act, w_ref[...], NN, preferred_element_type=F32,
                              precision=lax.Precision.HIGHEST) + b_ref[...]
        for b in range(N_DEV):
            stage_ref[b] = jnp.broadcast_to(mod[b:b + 1, :], (8, ncol))
        mod_ref[me] = stage_ref[me]
        for k in range(1, N_DEV):
            scatter(k).start()

        passed = []
        for j, chip in enumerate(chips):
            for w in range(n_w):
                block(1 + j, w, (*chip, c), here).wait_recv()
                fwd = block(4 + j, w, (*chip, c), sibling)
                fwd.start()
                passed.append(fwd)
        for w in range(n_w):
            block(0, w, sibling, here).wait_recv()
        for j, chip in enumerate(chips):
            for w in range(n_w):
                block(4 + j, w, (*chip, 1 - c), here).wait_recv()
        for k in range(1, N_DEV):
            scatter(k).wait_recv()
            gather_taps(k).wait_recv()
        for cp in first + passed:
            cp.wait_send()
        for k in range(1, N_DEV):
            gather(k).wait_send()
            scatter(k).wait_send()
            gather_taps(k).wait_send()
        for cp in mine:
            cp.wait()

    vmem, hbm = pl.BlockSpec(memory_space=pltpu.VMEM), pl.BlockSpec(memory_space=pltpu.HBM)
    out = pl.pallas_call(
        body, name="entry_exchange",
        in_specs=[vmem] * 4 + [hbm] * n_w, out_specs=[vmem] * 3 + [hbm] * n_w,
        out_shape=[jax.ShapeDtypeStruct((N_DEV, 8, d), F32), jax.ShapeDtypeStruct((N_DEV, 8, ncol), F32),
                   jax.ShapeDtypeStruct((N_DEV,) + taps.shape, F32)]
        + [jax.ShapeDtypeStruct((N_DEV * s.shape[0], s.shape[1]), s.dtype) for s in shards],
        scratch_shapes=[pltpu.VMEM((N_DEV, 8, ncol), F32), pltpu.SemaphoreType.DMA((3, N_DEV - 1)),
                        pltpu.SemaphoreType.DMA((3, N_DEV - 1)), pltpu.SemaphoreType.DMA((N_DEV - 1, n_w)),
                        pltpu.SemaphoreType.DMA((N_DEV - 1, n_w)), pltpu.SemaphoreType.DMA((n_w,))],
        compiler_params=_params(),
    )(c_rows, w_ada, b_ada_cols, taps, *shards)
    return out[0], out[1], out[2], out[3:]


def _peer_copies(mode, srcs, lands, send_sems, recv_sems):
    if mode in ("gather_ici", "gather_d2d"):
        x, y, c = _place()
        sibling = (x, y, 1 - c)
        chips = [(1 - x, y), (x, 1 - y), (1 - x, 1 - y)]
        n = len(lands)

        def rows(w, pos):
            r = lands[w].shape[0] // N_DEV
            return lands[w].at[pl.ds(pl.multiple_of(_index(pos) * r, 16), r), :]

        def copy(k, w, src, dst, to):
            return pltpu.make_async_remote_copy(src_ref=src, dst_ref=dst, send_sem=send_sems.at[k * n + w],
                                                recv_sem=recv_sems.at[k * n + w], device_id=to, device_id_type=MESH)

        if mode == "gather_ici":
            targets = [sibling] + [(*chip, c) for chip in chips]
            return [copy(k, w, srcs[w], rows(w, (x, y, c)), to) for k, to in enumerate(targets) for w in range(n)]
        return [copy(j, w, rows(w, (*chip, c)), rows(w, (*chip, c)), sibling)
                for j, chip in enumerate(chips) for w in range(n)]
    me = _index(_place())
    modes = (mode,) * len(srcs) if isinstance(mode, str) else mode
    copies = []
    for k in range(1, N_DEV):
        peer = _peer(k)
        for w, (src, land) in enumerate(zip(srcs, lands)):
            if modes[w] == "gather":
                r = src.shape[0]
                dst = land.at[pl.ds(pl.multiple_of(me * r, 16), r), :]
            elif modes[w] == "allgather":
                dst = land.at[me]
            else:
                r = src.shape[0] // N_DEV
                src = src.at[pl.ds(pl.multiple_of(_index(peer) * r, 16), r), :]
                dst = land.at[me]
            copies.append(pltpu.make_async_remote_copy(
                src_ref=src, dst_ref=dst, send_sem=send_sems.at[(k - 1) * len(srcs) + w],
                recv_sem=recv_sems.at[(k - 1) * len(srcs) + w],
                device_id=peer, device_id_type=MESH))
    return copies


def _landing_zone(mode, src, me, name):
    cols = src.shape[1]
    if mode == "gather":
        r = src.shape[0]
        in_spec = pl.BlockSpec((r, cols), lambda i, me_ref: (0, 0))
        out_spec = pl.BlockSpec((r, cols), lambda i, me_ref: (me_ref[0], 0))
        out_shape = jax.ShapeDtypeStruct((N_DEV * r, cols), src.dtype)
    else:
        r = src.shape[0] // N_DEV
        in_spec = pl.BlockSpec((r, cols), lambda i, me_ref: (me_ref[0], 0))
        out_spec = pl.BlockSpec((1, r, cols), lambda i, me_ref: (me_ref[0], 0, 0))
        out_shape = jax.ShapeDtypeStruct((N_DEV, r, cols), src.dtype)

    def body(me_ref, s_ref, o_ref):
        o_ref[...] = s_ref[...].reshape(o_ref.shape)

    return pl.pallas_call(
        body, name=name, out_shape=out_shape,
        grid_spec=pltpu.PrefetchScalarGridSpec(num_scalar_prefetch=1, grid=(1,), in_specs=[in_spec], out_specs=out_spec),
        compiler_params=_params(("arbitrary",)),
    )(me.reshape(1).astype(jnp.int32), src)


def _exchange_start(mode, srcs, lands, name):
    n_s, n_a = len(srcs), len(srcs) + len(lands)
    n_cp = _COPIES_PER_ARRAY.get(mode, N_DEV - 1) * len(lands)

    def body(*refs):
        for cp in _peer_copies(mode, refs[:n_s], refs[n_s:n_a], refs[n_a], refs[n_a + 1]):
            cp.start()
        refs[-1][...] = jnp.zeros_like(refs[-1])

    hbm, sem = pl.BlockSpec(memory_space=pltpu.HBM), pl.BlockSpec(memory_space=pltpu.SEMAPHORE)
    arrays = list(srcs) + list(lands)
    out = pl.pallas_call(
        body, name=name,
        out_shape=(pltpu.SemaphoreType.DMA((n_cp,)), pltpu.SemaphoreType.DMA((n_cp,)),
                   *[pltpu.HBM(a.shape, a.dtype) for a in arrays], jax.ShapeDtypeStruct((8, LANES), F32)),
        in_specs=[hbm] * n_a, out_specs=(sem, sem, *[hbm] * n_a, pl.BlockSpec(memory_space=pltpu.VMEM)),
        input_output_aliases={i: 2 + i for i in range(n_a)},
        compiler_params=pltpu.CompilerParams(has_side_effects=pltpu.SideEffectType.DATAFLOW_SIDE_EFFECTING),
    )(*[pltpu.with_memory_space_constraint(a, pltpu.HBM) for a in arrays])
    return out[0], out[1], out[2:2 + n_s], out[2 + n_s:2 + n_a], out[-1]


_COPIES_PER_ARRAY = {"gather_ici": 4, "gather_d2d": 3}


def _exchange_wait(mode, send_sems, recv_sems, srcs, lands, after, name):
    n_s, n_a = len(srcs), len(srcs) + len(lands)

    def body(*refs):
        copies = _peer_copies(mode, refs[:n_s], refs[n_s:n_a], refs[n_a], refs[n_a + 1])
        for cp in copies:
            cp.wait_send()
        for cp in copies:
            cp.wait_recv()

    hbm, sem = pl.BlockSpec(memory_space=pltpu.HBM), pl.BlockSpec(memory_space=pltpu.SEMAPHORE)
    arrays = list(srcs) + list(lands)
    out = pl.pallas_call(
        body, name=name, out_shape=tuple(pltpu.HBM(a.shape, a.dtype) for a in arrays),
        in_specs=[hbm] * n_a + [sem, sem, pl.BlockSpec(memory_space=pl.ANY)], out_specs=tuple([hbm] * n_a),
        input_output_aliases={i: i for i in range(n_a)},
        compiler_params=pltpu.CompilerParams(has_side_effects=pltpu.SideEffectType.DATAFLOW_SIDE_EFFECTING),
    )(*arrays, send_sems, recv_sems, after)
    return out[n_s:]


SMALL_WEIGHTS = ("b_ada", "g_pre_mix", "g_post_mix", "g_pre_ffn", "g_post_ffn", "w_pool", "b_pool", "pool_scale", "conv_b")


MOD_ROWS = ((0, 0), (0, 1), (1, 3), (1, 0), (1, 1), (2, 0))


def _small_sum(mine, gathered):
    n_l = len(mine)
    d = mine[0].shape[1]

    def body(*refs):
        loc, got = refs[:n_l], refs[n_l:2 * n_l]
        tot_refs, dmod_ref = refs[2 * n_l:3 * n_l], refs[3 * n_l]
        me = _index(_place())
        part = lambda a, dev: jnp.where(dev == me, loc[a][...], got[a][dev])
        for a in range(n_l):
            tot = part(a, 0)
            for dev in range(1, N_DEV):
                tot = tot + part(a, dev)
            tot_refs[a][...] = tot
        for dev in range(N_DEV):
            for k, (a, r) in enumerate(MOD_ROWS):
                dmod_ref[dev:dev + 1, k * d:(k + 1) * d] = part(a, dev)[r:r + 1, :]

    vmem = pl.BlockSpec(memory_space=pltpu.VMEM)
    out = pl.pallas_call(
        body, name="small_sum", in_specs=[vmem] * (2 * n_l), out_specs=[vmem] * (n_l + 1),
        out_shape=[jax.ShapeDtypeStruct(a.shape, F32) for a in mine] + [jax.ShapeDtypeStruct((N_DEV, 6 * d), F32)],
        compiler_params=_params(),
    )(*mine, *gathered)
    return out[:n_l], out[n_l]


def _small_adam(totals, weights, moms, vels):
    n_t, n_w = len(totals), len(weights)

    def body(*refs):
        t_in, t_mix, t_ffn, t_pool, t_blk, t_conv, _ = (r[...] for r in refs[:n_t])
        w_refs, m_refs, v_refs = (refs[n_t + k * n_w:n_t + (k + 1) * n_w] for k in range(3))
        outs = refs[n_t + 3 * n_w:]

        def update(idx, g, at=()):
            sel = lambda ref: ref.at[at] if at else ref
            delta, nm, nv = _adam_math(sel(w_refs[idx])[...], g, sel(m_refs[idx])[...], sel(v_refs[idx])[...])
            for k, val in enumerate((g, delta, nm, nv)):
                sel(outs[4 * idx + k])[...] = val

        tots = (t_in, t_mix, t_ffn)
        update(0, jnp.concatenate([tots[a][r:r + 1] for a, r in MOD_ROWS], axis=1))
        update(1, t_in[2:3])
        update(2, t_mix[4:5])
        update(3, t_mix[2:3])
        update(4, t_ffn[1:2])
        for gi in range(len(POOL_WINDOWS)):
            update(5, t_blk[gi], at=(0, gi))
        update(6, jnp.concatenate([t_pool[0:1, gi * HEAD_DIM:(gi + 1) * HEAD_DIM] for gi in range(len(POOL_WINDOWS))], axis=0),
               at=(0,))
        update(7, t_pool[1:2])
        update(8, t_conv[3:4])

    vmem = pl.BlockSpec(memory_space=pltpu.VMEM)
    return pl.pallas_call(
        body, name="small_adam", in_specs=[vmem] * (n_t + 3 * n_w), out_specs=[vmem] * (4 * n_w),
        out_shape=[jax.ShapeDtypeStruct(w.shape, F32) for w in weights for _ in range(4)],
        compiler_params=_params(),
    )(*totals, *weights, *moms, *vels)


def _adam_math(w, g, m, v):
    m = ADAM_B1 * m + (1.0 - ADAM_B1) * g
    v = ADAM_B2 * v + (1.0 - ADAM_B2) * (g * g)
    m_hat = m / (1.0 - ADAM_B1 ** ADAM_STEP)
    v_hat = v / (1.0 - ADAM_B2 ** ADAM_STEP)
    delta = -ADAM_LR * (m_hat / (jnp.sqrt(v_hat) + ADAM_EPS) + ADAM_WD * w)
    return delta, m, v


def _adam(w, g, m, v, name, tr):
    rows, cols = w.shape

    def body(w_ref, g_ref, m_ref, v_ref, d_ref, nm_ref, nv_ref):
        d_ref[...], nm_ref[...], nv_ref[...] = _adam_math(w_ref[...], g_ref[...], m_ref[...], v_ref[...])

    spec = pl.BlockSpec((tr, cols), lambda i: (i, 0))
    shape = jax.ShapeDtypeStruct((rows, cols), F32)
    return pl.pallas_call(
        body, name=name, grid=(rows // tr,), in_specs=[spec] * 4, out_specs=[spec] * 3,
        out_shape=[shape] * 3, compiler_params=_params(("arbitrary",)),
    )(w, g, m, v)


def _sum_adam(parts, w, m, v, name, tr):
    _, rows, cols = parts.shape

    def body(p_ref, w_ref, m_ref, v_ref, g_ref, d_ref, nm_ref, nv_ref):
        g = p_ref[0].astype(F32)
        for dev in range(1, N_DEV):
            g = g + p_ref[dev].astype(F32)
        g_ref[...] = g
        d_ref[...], nm_ref[...], nv_ref[...] = _adam_math(w_ref[...], g, m_ref[...], v_ref[...])

    spec = pl.BlockSpec((tr, cols), lambda i: (i, 0))
    shape = jax.ShapeDtypeStruct((rows, cols), F32)
    return pl.pallas_call(
        body, name=name, grid=(rows // tr,),
        in_specs=[pl.BlockSpec((N_DEV, tr, cols), lambda i: (0, i, 0)), spec, spec, spec],
        out_specs=[spec] * 4, out_shape=[shape] * 4, compiler_params=_params(("arbitrary",)),
    )(parts, w, m, v)


def _ada_grad_adam(c_all, dmod_cols, w, m, v, tr):
    rows, cols = w.shape

    def body(c_ref, dm_ref, w_ref, m_ref, v_ref, g_ref, d_ref, nm_ref, nv_ref):
        cv = c_ref[...]
        act = cv * jax.nn.sigmoid(cv)
        g = lax.dot_general(act, dm_ref[...], TN, preferred_element_type=F32, precision=lax.Precision.HIGHEST)
        g_ref[...] = g
        d_ref[...], nm_ref[...], nv_ref[...] = _adam_math(w_ref[...], g, m_ref[...], v_ref[...])

    spec = pl.BlockSpec((tr, cols), lambda i: (i, 0))
    shape = jax.ShapeDtypeStruct((rows, cols), F32)
    return pl.pallas_call(
        body, name="ada_grad_adam", grid=(rows // tr,),
        in_specs=[pl.BlockSpec((N_DEV, tr), lambda i: (0, i)), pl.BlockSpec((N_DEV, cols), lambda i: (0, 0)), spec, spec, spec],
        out_specs=[spec] * 4, out_shape=[shape] * 4, compiler_params=_params(("arbitrary",)),
    )(c_all, dmod_cols, w, m, v)


def _rope_tables(positions):
    inv_freq = ROPE_THETA ** (-jnp.arange(0, 2 * ROT_HALF, 2, dtype=F32) / (2 * ROT_HALF))
    ang = positions.astype(F32)[:, None] * inv_freq
    rows = jnp.concatenate([jnp.cos(ang), jnp.sin(ang), jnp.ones_like(ang)], axis=1)
    spread = [[[0.0] * LANES for _ in range(3 * ROT_HALF)] for _ in range(3)]
    for lane in range(LANES):
        p, j = lane % HEAD_DIM, lane % ROT_HALF
        if p < ROT_HALF:
            spread[0][j][lane] = 1.0
            spread[1][ROT_HALF + j][lane] = -1.0
        elif p < 2 * ROT_HALF:
            spread[0][j][lane] = 1.0
            spread[2][ROT_HALF + j][lane] = 1.0
        else:
            spread[0][2 * ROT_HALF][lane] = 1.0
    return rows, jnp.array(spread, F32)


def _pad_rows(a, rows):
    return jnp.pad(a, ((0, rows - a.shape[0]), (0, 0)))


def _sequence_step(xs, target, rope, mods, gains, w_in_t, w_out_t, relay_ffn, fetch_ffn, send_grads, w_blk_b, b_pool_r,
                   pool_scale_r, conv_w_all, conv_b):
    tie = lambda a, token: a if token is None else a + token[0:1, 0:1]
    sh_m, sc_m, gt_m, sh_f, sc_f, gt_f = mods
    g_pre_mix, g_post_mix, g_pre_ffn, g_post_ffn = gains
    h1, u_pool, qkv = _premix_inproj(xs, sh_m, sc_m, g_pre_mix, w_in_t, rope, tm=512)
    o_g, lse_g = _attn_fwd(qkv)
    x1, y1, h2, cat, attn, lse_all = _mix_out(xs, u_pool, o_g, lse_g, w_blk_b, b_pool_r, pool_scale_r, w_out_t,
                                              gt_m, g_post_mix, g_pre_ffn, sc_f, sh_f, tm=256)
    token = relay_ffn(x1)
    w_up_t, w_down_f = fetch_ffn(x1 if token is None else token)
    gate, a_ffn, act, vd, dy2, dout, sums_ffn, loss_loc = _ffn_fwd_loss(h2, x1, target, w_up_t, w_down_f, conv_w_all, conv_b,
                                                              gt_f, g_post_ffn, tm=256, ck=256)

    dgc, dval, dw_down, dconv = _ffn_bwd_act(dy2, gate, a_ffn, act, vd, w_down_f, tm=512, tf=1408, ck=256)
    dup, dh2 = _ffn_bwd_up(dgc, dval, w_up_t, conv_w_all, tm=256)
    dw_up_t = _wgrad(dup, h2, "wgrad_up", tk=2048, tmm=1408)
    token = send_grads("ffn", [dw_up_t, dw_down], [])
    dx1, dpool, dattn, delta, dw_out_t, sums_mix = _mix_bwd(dh2, dout, x1, y1, cat, attn, w_out_t, tie(sc_f, token),
                                                           g_pre_ffn, gt_m, g_post_mix, tm=256)
    du, dw_blk, sums_pool = _pool_bwd(dpool, u_pool, w_blk_b, b_pool_r, pool_scale_r, tm=512)
    token = send_grads("out", [dw_out_t], [sums_mix, sums_ffn, sums_pool, dw_blk, dconv, loss_loc])
    dqkv = _attn_bwd(qkv, dattn, lse_all, delta, jnp.zeros((8, LANES), F32) if token is None else token)
    dproj = _dproj_assemble(du, dqkv, rope, tm=512)
    dw_in_t = _wgrad(dproj, h1, "wgrad_in", tk=2048, tmm=1280)
    token = send_grads("in", [dw_in_t], [])
    grad_x, sums_in = _inproj_bwd(dproj, w_in_t, xs, dx1, tie(sc_m, token), g_pre_mix, tm=256)
    return (loss_loc, grad_x, dw_in_t, dw_out_t, dw_up_t, dw_down, dw_blk, dconv,
            sums_in, sums_mix, sums_ffn, sums_pool)


def kernel(x, c, positions, w_ada, b_ada, g_pre_mix, g_post_mix, g_pre_ffn, g_post_ffn, w_in, w_pool, b_pool, pool_scale, w_out, w_up, conv_w, conv_b, w_down, loss_target, m_w_ada, m_b_ada, m_g_pre_mix, m_g_post_mix, m_g_pre_ffn, m_g_post_ffn, m_w_in, m_w_pool, m_b_pool, m_pool_scale, m_w_out, m_w_up, m_conv_w, m_conv_b, m_w_down, v_w_ada, v_b_ada, v_g_pre_mix, v_g_post_mix, v_g_pre_ffn, v_g_post_ffn, v_w_in, v_w_pool, v_b_pool, v_pool_scale, v_w_out, v_w_up, v_conv_w, v_conv_b, v_w_down):
    s_len, d = x.shape[1], x.shape[2]
    d_ff = w_down.shape[1] * N_DEV
    me = _index(_place())
    xs, target = x[0], loss_target[0]

    ncol = w_ada.shape[2]
    b_cols = lax.dynamic_slice(b_ada, (0, me * ncol), (1, ncol))
    c_all, mod, taps_all, (w_in_t, w_out_t) = _entry_exchange(
        jnp.broadcast_to(c, (8, d)), w_ada[0], b_cols, _pad_rows(conv_w[0], 8),
        [w_in[0].T.astype(BF16), w_out[0].T.astype(BF16)])
    c_all = c_all[:, 0, :]
    conv_w_all = jnp.transpose(taps_all[:, :3, :], (1, 0, 2)).reshape(3, d_ff)
    sh_m, sc_m, gt_m, sh_f, sc_f, gt_f = [mod[:, 0, :].reshape(1, -1)[:, k * d:(k + 1) * d] for k in range(6)]

    rope = _rope_tables(positions[0])
    w_blk = jnp.zeros((256, 256), F32)
    for gi in range(4):
        w_blk = lax.dynamic_update_slice(w_blk, w_pool[0, gi], (gi * HEAD_DIM, gi * HEAD_DIM))
    w_blk_b = w_blk.astype(BF16)
    b_pool_r, pool_scale_r = b_pool.reshape(1, 256), pool_scale.reshape(1, 256)

    up_sh, down_sh = w_up[0].T.astype(BF16), w_down[0].astype(BF16)
    w_in_t, conv_w_all, up_sh, down_sh = lax.optimization_barrier((w_in_t, conv_w_all, up_sh, down_sh))
    lands = [_landing_zone("gather", s, me, "land_" + nm) for s, nm in ((up_sh, "w_up"), (down_sh, "w_down"))]
    w_send, w_recv, w_src, w_land, w_token = _exchange_start("gather_ici", [up_sh, down_sh], lands, "ffn_weights_ici_start")
    relay = []

    def relay_ffn(after):
        arrived = _exchange_wait("gather_ici", w_send, w_recv, w_src, w_land, after, "ffn_weights_ici_wait")
        relay.extend(_exchange_start("gather_d2d", [], arrived, "ffn_weights_d2d_start"))
        return relay[4]

    def fetch_ffn(after):
        return _exchange_wait("gather_d2d", relay[0], relay[1], [], relay[3], after, "ffn_weights_d2d_wait")

    flights = {}

    def send_grads(tag, slabs, whole):
        lands = [_landing_zone("scatter", g, me, f"land_{tag}_{k}") for k, g in enumerate(slabs)]
        lands += [lax.empty((N_DEV,) + a.shape, F32) for a in whole]
        modes = ("scatter",) * len(slabs) + ("allgather",) * len(whole)
        flights[tag] = (modes, *_exchange_start(modes, slabs + whole, lands, f"grads_{tag}_start"))
        return flights[tag][5]

    def arrived(tag, after):
        return _exchange_wait(*flights[tag][:5], after, f"grads_{tag}_wait")

    (loss_loc, grad_x, dw_in_t, dw_out_t, _, _, dw_pool, dconv,
     sums_in, sums_mix, sums_ffn, sums_pool) = _sequence_step(
        xs, target, rope, (sh_m + w_token[0:1, 0:1], sc_m, gt_m, sh_f, sc_f, gt_f),
        (g_pre_mix, g_post_mix, g_pre_ffn, g_post_ffn),
        w_in_t, w_out_t, relay_ffn, fetch_ffn, send_grads, w_blk_b, b_pool_r, pool_scale_r, conv_w_all, conv_b)

    send_grads("last", [], [sums_in])

    parts_up, parts_down = arrived("ffn", flights["last"][5])
    big = {"w_up": [a.T for a in _sum_adam(parts_up, w_up[0].T, m_w_up[0].T, v_w_up[0].T, "adam_w_up", 352)],
           "w_down": _sum_adam(parts_down, w_down[0], m_w_down[0], v_w_down[0], "adam_w_down", 176)}
    parts_out, *gathered = arrived("out", big["w_down"][0])
    big["w_out"] = [a.T for a in _sum_adam(parts_out, w_out[0].T, m_w_out[0].T, v_w_out[0].T, "adam_w_out", 128)]
    parts_in, = arrived("in", big["w_out"][0])
    big["w_in"] = [a.T for a in _sum_adam(parts_in, w_in[0].T, m_w_in[0].T, v_w_in[0].T, "adam_w_in", 160)]

    rep_w = [b_ada, g_pre_mix, g_post_mix, g_pre_ffn, g_post_ffn, w_pool, b_pool, pool_scale, conv_b]
    rep_m = [m_b_ada, m_g_pre_mix, m_g_post_mix, m_g_pre_ffn, m_g_post_ffn, m_w_pool, m_b_pool, m_pool_scale, m_conv_b]
    rep_v = [v_b_ada, v_g_pre_mix, v_g_post_mix, v_g_pre_ffn, v_g_post_ffn, v_w_pool, v_b_pool, v_pool_scale, v_conv_b]
    small = [sums_in, sums_mix, sums_ffn, sums_pool, dw_pool, dconv, loss_loc]
    gathered = [*arrived("last", big["w_in"][0]), *gathered]
    totals, dmod_all = _small_sum(small, gathered)
    dconv_tot, loss_tot = totals[5], totals[6]
    rep_out = _small_adam(totals, rep_w, rep_m, rep_v)
    g_rep, d_rep, nm_rep, nv_rep = (rep_out[k::4] for k in range(4))

    fcol = d_ff // N_DEV
    g_cw = lax.dynamic_slice(dconv_tot, (0, me * fcol), (3, fcol))
    d_cw, nm_cw, nv_cw = _adam(conv_w[0], g_cw, m_conv_w[0], v_conv_w[0], "adam_conv_w", 3)

    dmod_cols = lax.dynamic_slice(dmod_all, (0, me * ncol), (N_DEV, ncol))
    g_ada, d_ada, nm_ada, nv_ada = _ada_grad_adam(c_all, dmod_cols, w_ada[0], m_w_ada[0], v_w_ada[0], 256)

    loss = loss_tot[0, 0]

    def group(k):
        rep = (g_rep, d_rep, nm_rep, nv_rep)[k]
        ada = (g_ada, d_ada, nm_ada, nv_ada)[k][None]
        cw = (g_cw, d_cw, nm_cw, nv_cw)[k][None]
        return [ada, rep[0], rep[1], rep[2], rep[3], rep[4], big["w_in"][k][None], rep[5], rep[6], rep[7],
                big["w_out"][k][None], big["w_up"][k][None], cw, rep[8], big["w_down"][k][None]]

    return (loss, grad_x[None], *group(0), *group(1), *group(2), *group(3))
```

```python
import functools
import math

import jax
import jax.numpy as jnp
from jax import lax
from jax.experimental import pallas as pl
from jax.experimental.pallas import tpu as pltpu

F32 = jnp.float32
BF16 = jnp.bfloat16
MESH = pl.DeviceIdType.MESH

N_DEV = 8
HEAD_DIM = 64
ROT_HALF = 8
ROPE_THETA = 500000.0
POOL_WINDOWS = (2, 4, 8, 16)
DILATIONS = (1, 4, 16)
BLOCK = 128
NORM_EPS = 1e-6
HALO = 16
MASKED = -1e30
ATTN_FWD_UNROLL = 8
ATTN_BWD_UNROLL = 8

ADAM_LR = 0.001
ADAM_B1 = 0.9
ADAM_B2 = 0.999
ADAM_EPS = 1e-08
ADAM_WD = 0.01
ADAM_STEP = 10

V7X_VMEM_LIMIT = 56 * 1024 * 1024
LANES = 128

NT = (((1,), (1,)), ((), ()))
NN = (((1,), (0,)), ((), ()))
TN = (((0,), (0,)), ((), ()))


def _dot(a, b, dims):
    return lax.dot_general(a, b, dims, preferred_element_type=F32)


def _params(sem=None, vmem=V7X_VMEM_LIMIT):
    if sem is None:
        return pltpu.CompilerParams(vmem_limit_bytes=vmem)
    return pltpu.CompilerParams(dimension_semantics=sem, vmem_limit_bytes=vmem)


def _rstd(v):
    return lax.rsqrt(jnp.mean(v * v, axis=-1, keepdims=True) + NORM_EPS)


def _norm_bwd(dn, n, rstd):
    return rstd * (dn - n * jnp.mean(dn * n, axis=-1, keepdims=True))


def _rope_lanes(cs_ref, spread_ref):
    return [lax.dot_general(cs_ref[...], spread_ref[k], NN, preferred_element_type=F32, precision=lax.Precision.HIGHEST)
            for k in range(3)]


def _rope_fwd(p, lanes):
    return p * lanes[0] + pltpu.roll(p, LANES - ROT_HALF, 1) * lanes[1] + pltpu.roll(p, ROT_HALF, 1) * lanes[2]


def _rope_bwd(dp, lanes):
    return dp * lanes[0] + pltpu.roll(dp * lanes[1], ROT_HALF, 1) + pltpu.roll(dp * lanes[2], LANES - ROT_HALF, 1)


def _gelu_parts(v):
    k2 = 2.0 * math.sqrt(2.0 / math.pi)
    c = 0.044715
    v2 = v * v
    s = jax.nn.sigmoid(v * (k2 + (k2 * c) * v2))
    g = v * s
    dg = s + g * (1.0 - s) * (k2 + (3.0 * k2 * c) * v2)
    return g, dg


def _halo_before(i, tile):
    return jnp.maximum(i * (tile // HALO) - 1, 0)


def _premix_inproj(x, sh, sc, g, w_in_t, rope, after, tm):
    s_len, d = x.shape
    n_proj = w_in_t.shape[0]
    n_slab = (n_proj - 256) // LANES

    def body(x_ref, sh_ref, sc_ref, g_ref, w_ref, cs_ref, spread_ref, after_ref, h_ref, up_ref, qkv_ref):
        xv = x_ref[...]
        h = (xv * _rstd(xv) * g_ref[...]) * (1.0 + sc_ref[...]) + sh_ref[...]
        hb = h.astype(BF16)
        h_ref[...] = hb
        up_ref[...] = _dot(hb, w_ref[0:256, :], NT)
        lanes = _rope_lanes(cs_ref, spread_ref)
        for pair in range(n_slab // 2):
            p = _dot(hb, w_ref[256 + 256 * pair:512 + 256 * pair, :], NT)
            for half in range(2):
                ph = p[:, half * LANES:(half + 1) * LANES]
                if pair < 6:
                    ph = _rope_fwd(ph, lanes)
                if pair < 3:
                    ph = ph * (HEAD_DIM ** -0.5)
                qkv_ref[2 * pair + half] = ph

    vec = pl.BlockSpec((1, d), lambda i: (0, 0))
    return pl.pallas_call(
        body, name="premix_inproj", grid=(s_len // tm,),
        in_specs=[pl.BlockSpec((tm, d), lambda i: (i, 0)), vec, vec, vec,
                  pl.BlockSpec((n_proj, d), lambda i: (0, 0)),
                  pl.BlockSpec((tm, rope[0].shape[1]), lambda i: (i, 0)), pl.BlockSpec(rope[1].shape, lambda i: (0, 0, 0)),
                  pl.BlockSpec(memory_space=pl.ANY)],
        out_specs=[pl.BlockSpec((tm, d), lambda i: (i, 0)),
                   pl.BlockSpec((tm, 256), lambda i: (i, 0)),
                   pl.BlockSpec((n_slab, tm, LANES), lambda i: (0, i, 0))],
        out_shape=[jax.ShapeDtypeStruct((s_len, d), BF16),
                   jax.ShapeDtypeStruct((s_len, 256), F32),
                   jax.ShapeDtypeStruct((n_slab, s_len, LANES), F32)],
        compiler_params=_params(("arbitrary",)),
    )(x, sh, sc, g, w_in_t, *rope, after)


def _block_rows(n, r, dil):
    start = n * (BLOCK * dil) + r
    if dil == 1:
        return pl.ds(pl.multiple_of(start, BLOCK), BLOCK)
    return pl.ds(start, BLOCK, stride=dil)


def _band_mask(n):
    ri = lax.broadcasted_iota(jnp.int32, (BLOCK, 2 * BLOCK), 0)
    cj = lax.broadcasted_iota(jnp.int32, (BLOCK, 2 * BLOCK), 1)
    cur = (cj >= BLOCK) & (cj - BLOCK <= ri)
    prev = (cj < BLOCK) & (cj >= ri) & (n > 0)
    return cur | prev


def _attn_fwd(qkv):
    s_len = qkv.shape[1]
    n_g = len(DILATIONS)

    def body(q_ref, k_ref, v_ref, o_ref, lse_ref):
        lane = lax.broadcasted_iota(jnp.int32, (BLOCK, LANES), 1)
        first = lane < HEAD_DIM

        def group(dil):
            nb = s_len // (BLOCK * dil)

            def block(t, carry):
                r, n = t // nb, t % nb
                cur = _block_rows(n, r, dil)
                prev = _block_rows(jnp.maximum(n - 1, 0), r, dil)
                q = q_ref[0, cur, :]
                kcat = jnp.concatenate([k_ref[0, prev, :], k_ref[0, cur, :]], axis=0).astype(BF16)
                vcat = jnp.concatenate([v_ref[0, prev, :], v_ref[0, cur, :]], axis=0).astype(BF16)
                valid = _band_mask(n)
                q2 = jnp.concatenate([jnp.where(first, q, 0.0), jnp.where(first, 0.0, q)], axis=0).astype(BF16)
                s = jnp.where(jnp.concatenate([valid, valid], axis=0), _dot(q2, kcat, NT), MASKED)
                m = jnp.max(s, axis=-1, keepdims=True)
                p = jnp.exp(s - m)
                den = jnp.sum(p, axis=-1, keepdims=True)
                o2 = _dot(p.astype(BF16), vcat, NN) / den
                lse2 = m + jnp.log(den)
                o_ref[0, 0, cur, :] = jnp.where(first, o2[:BLOCK], o2[BLOCK:])
                lse_ref[0, 0, cur, :] = jnp.where(first, lse2[:BLOCK], lse2[BLOCK:])
                return carry

            lax.fori_loop(0, nb * dil, block, 0, unroll=ATTN_FWD_UNROLL)

        for gi, dil in enumerate(DILATIONS):
            pl.when(pl.program_id(0) == gi)(functools.partial(group, dil))

    def slab(base):
        return pl.BlockSpec((1, s_len, LANES), lambda g, s: (base + 2 * g + s, 0, 0))

    out = pl.BlockSpec((1, 1, s_len, LANES), lambda g, s: (g, s, 0, 0))
    shape = jax.ShapeDtypeStruct((n_g, 2, s_len, LANES), F32)
    return pl.pallas_call(
        body, name="attn_fwd", grid=(n_g, 2),
        in_specs=[slab(0), slab(6), slab(12)], out_specs=[out, out], out_shape=[shape, shape],
        compiler_params=_params(("arbitrary", "arbitrary")),
    )(qkv, qkv, qkv)


def _pool_mixed(u, halo, i, tm):
    ue = jnp.concatenate([halo, u], axis=0)
    s2 = ue + pltpu.roll(ue, 1, 0)
    s4 = s2 + pltpu.roll(s2, 2, 0)
    s8 = s4 + pltpu.roll(s4, 4, 0)
    s16 = s8 + pltpu.roll(s8, 8, 0)
    grp = lax.broadcasted_iota(jnp.int32, (tm, 256), 1) // HEAD_DIM
    pick = lambda a, b, c, e: jnp.where(grp == 0, a, jnp.where(grp == 1, b, jnp.where(grp == 2, c, e)))
    win_sum = pick(s2[HALO:], s4[HALO:], s8[HALO:], s16[HALO:])
    pos = (i * tm + lax.broadcasted_iota(jnp.int32, (tm, 256), 0)).astype(F32)
    count = jnp.minimum(pos + 1.0, pick(*[float(w) for w in POOL_WINDOWS]))
    return win_sum / count - u, count


def _mix_out(x, u_pool, o_g, lse_g, w_blk, b_pool, pool_scale, w_out_t, gt_m, g_post_mix, g_pre_ffn, sc_f, sh_f, tm):
    s_len, d = x.shape

    def body(x_ref, u_ref, uh_ref, o_ref, l_ref, wb_ref, bp_ref, ps_ref, wo_ref,
             gt_ref, g1_ref, g2_ref, sc_ref, sh_ref,
             x1_ref, y1_ref, h2_ref, cat_ref, attn_ref, lall_ref):
        (o0, o1, o2), (l0, l1, l2) = (o_ref.at[g] for g in range(3)), (l_ref.at[g] for g in range(3))
        i = pl.program_id(0)
        u = u_ref[...]
        halo = uh_ref[...] * (i > 0).astype(F32)
        mixed, _ = _pool_mixed(u, halo, i, tm)
        y = _dot(mixed.astype(BF16), wb_ref[...], NN) + bp_ref[...]
        pool = y * ps_ref[...]
        attn = []
        for s in range(2):
            la, lb, lc = l0[s], l1[s], l2[s]
            mx = jnp.maximum(jnp.maximum(la, lb), lc)
            ea, eb, ec = jnp.exp(la - mx), jnp.exp(lb - mx), jnp.exp(lc - mx)
            den = ea + eb + ec
            lall_ref[s] = mx + jnp.log(den)
            attn.append((ea / den) * o0[s] + (eb / den) * o1[s] + (ec / den) * o2[s])
        attn = jnp.concatenate(attn, axis=1)
        attn_ref[...] = attn
        cat = jnp.concatenate([pool, attn], axis=1).astype(BF16)
        cat_ref[...] = cat
        y1 = _dot(cat, wo_ref[...], NT)
        y1_ref[...] = y1.astype(BF16)
        x1 = x_ref[...] + gt_ref[...] * (y1 * _rstd(y1) * g1_ref[...])
        x1_ref[...] = x1
        h2 = (x1 * _rstd(x1) * g2_ref[...]) * (1.0 + sc_ref[...]) + sh_ref[...]
        h2_ref[...] = h2.astype(BF16)

    tile = lambda w: pl.BlockSpec((tm, w), lambda i: (i, 0))
    slab = pl.BlockSpec((2, tm, LANES), lambda i: (0, i, 0))
    groups = pl.BlockSpec((len(DILATIONS), 2, tm, LANES), lambda i: (0, 0, i, 0))
    const = lambda a: pl.BlockSpec(a.shape, lambda i: (0,) * a.ndim)
    return pl.pallas_call(
        body, name="mix_out", grid=(s_len // tm,),
        in_specs=[tile(d), tile(256), pl.BlockSpec((HALO, 256), lambda i: (_halo_before(i, tm), 0)),
                  groups, groups,
                  const(w_blk), const(b_pool), const(pool_scale), const(w_out_t),
                  const(gt_m), const(g_post_mix), const(g_pre_ffn), const(sc_f), const(sh_f)],
        out_specs=[tile(d), tile(d), tile(d), tile(512), tile(256), slab],
        out_shape=[jax.ShapeDtypeStruct((s_len, d), F32), jax.ShapeDtypeStruct((s_len, d), BF16),
                   jax.ShapeDtypeStruct((s_len, d), BF16), jax.ShapeDtypeStruct((s_len, 512), BF16),
                   jax.ShapeDtypeStruct((s_len, 256), F32), jax.ShapeDtypeStruct((2, s_len, LANES), F32)],
        compiler_params=_params(("arbitrary",)),
    )(x, u_pool, u_pool, o_g, lse_g, w_blk, b_pool, pool_scale, w_out_t, gt_m, g_post_mix, g_pre_ffn, sc_f, sh_f)


def _conv_gate(gate_ext, cw, cb):
    gc = gate_ext * cw[2:3, :] + pltpu.roll(gate_ext, 1, 0) * cw[1:2, :] + pltpu.roll(gate_ext, 2, 0) * cw[0:1, :]
    return gc[HALO:] + cb


def _ffn_fwd_loss(h2, x1, target, w_up_t, w_down, conv_w, conv_b, gt_f, g_post_ffn, tm, ck):
    s_len, d = x1.shape
    d_ff = w_down.shape[0]
    n_t, n_c = s_len // tm, d_ff // ck

    def body(h_ref, hh_ref, x1_ref, tgt_ref, wg_ref, wv_ref, wd_ref, cw_ref, cb_ref, gt_ref, g_ref,
             gate_ref, a_ref, act_ref, vd_ref, dy2_ref, dout_ref, sums_ref, loss_ref, acc_ref):
        i = pl.program_id(0)

        @pl.when(i == 0)
        def _():
            sums_ref[...] = jnp.zeros_like(sums_ref)
            loss_ref[...] = jnp.zeros_like(loss_ref)
            acc_ref[...] = jnp.zeros_like(acc_ref)

        def finish(live):
            y2 = acc_ref[...]
            rstd = _rstd(y2)
            n = y2 * rstd
            rn = n * g_ref[...]
            err = x1_ref[...] + gt_ref[...] * rn - tgt_ref[...]
            keep = lambda v: jnp.where(live, v, 0.0)
            loss_ref[...] += keep(0.5 * jnp.sum(jnp.mean(err * err, axis=-1, keepdims=True), axis=0, keepdims=True))
            dout = err * (1.0 / d)
            dout_ref[...] = dout
            drn = dout * gt_ref[...]
            sums_ref[0:1, :] += keep(jnp.sum(dout * rn, axis=0, keepdims=True))
            sums_ref[1:2, :] += keep(jnp.sum(drn * n, axis=0, keepdims=True))
            dy2_ref[...] = _norm_bwd(drn * g_ref[...], n, rstd).astype(BF16)

        @pl.when(i < n_t)
        def _():
            h = h_ref[...]
            h_ext = jnp.concatenate([hh_ref[...], h], axis=0)
            row = lax.broadcasted_iota(jnp.int32, (tm + HALO, ck), 0)
            no_halo = (row < HALO) & (i == 0)

            def up(c):
                cs = slice(c * ck, (c + 1) * ck)
                return jnp.where(no_halo, 0.0, _dot(h_ext, wg_ref[cs, :], NT)), _dot(h, wv_ref[cs, :], NT)

            part = None
            nxt = up(0)
            finish(i > 0)
            for c in range(n_c):
                cs = slice(c * ck, (c + 1) * ck)
                gate_ext, val = nxt
                if c + 1 < n_c:
                    nxt = up(c + 1)
                act, dact = _gelu_parts(_conv_gate(gate_ext, cw_ref[:, cs], cb_ref[:, cs]))
                a = (act * val).astype(BF16)
                gate_ref[:, cs] = gate_ext[HALO:].astype(BF16)
                a_ref[:, cs] = a
                act_ref[:, cs] = act.astype(BF16)
                vd_ref[:, cs] = (val * dact).astype(BF16)
                p = _dot(a, wd_ref[cs, :], NN)
                part = p if part is None else part + p
            acc_ref[...] = part

        @pl.when(i == n_t)
        def _():
            finish(True)

    this = lambda i: jnp.minimum(i, n_t - 1)
    before = lambda i: jnp.maximum(i - 1, 0)
    tok = lambda w, at: pl.BlockSpec((tm, w), lambda i: (at(i), 0))
    vec = pl.BlockSpec((1, d), lambda i: (0, 0))
    once = lambda shape, imap: pl.BlockSpec(shape, imap, pipeline_mode=pl.Buffered(1))
    return pl.pallas_call(
        body, name="ffn_fwd_loss", grid=(n_t + 1,),
        in_specs=[tok(d, this), pl.BlockSpec((HALO, d), lambda i: (_halo_before(this(i), tm), 0)),
                  tok(d, before), tok(d, before),
                  once((d_ff, d), lambda i: (0, 0)), once((d_ff, d), lambda i: (1, 0)), once((d_ff, d), lambda i: (0, 0)),
                  pl.BlockSpec((3, d_ff), lambda i: (0, 0)), pl.BlockSpec((1, d_ff), lambda i: (0, 0)), vec, vec],
        out_specs=[tok(d_ff, this)] * 4 + [tok(d, before), tok(d, before), pl.BlockSpec((8, d), lambda i: (0, 0)),
                                          pl.BlockSpec((8, LANES), lambda i: (0, 0))],
        out_shape=[jax.ShapeDtypeStruct((s_len, d_ff), BF16)] * 4
        + [jax.ShapeDtypeStruct((s_len, d), BF16), jax.ShapeDtypeStruct((s_len, d), F32),
           jax.ShapeDtypeStruct((8, d), F32), jax.ShapeDtypeStruct((8, LANES), F32)],
        scratch_shapes=[pltpu.VMEM((tm, d), F32)],
        compiler_params=_params(("arbitrary",)),
    )(h2, h2, x1, target, w_up_t, w_up_t, w_down, conv_w, conv_b, gt_f, g_post_ffn)


def _ffn_bwd_act(dy2, gate, a, act, vd, w_down, tm, tf, ck):
    s_len, d = dy2.shape
    d_ff = w_down.shape[0]
    n_t = s_len // tm
    chunks = [slice(lo, min(lo + ck, tf)) for lo in range(0, tf, ck)]

    def body(dy_ref, g_ref, gh_ref, a_ref, act_ref, vd_ref, wd_ref, dgc_ref, dval_ref, dwd_ref, dconv_ref, acc_ref):
        i = pl.program_id(1)

        @pl.when(i == 0)
        def _():
            acc_ref[...] = jnp.zeros_like(acc_ref)
            dconv_ref[...] = jnp.zeros_like(dconv_ref)

        dy = dy_ref[...]

        def down(cs):
            return _dot(dy, wd_ref[cs, :], NT)

        nxt = down(chunks[0])
        for c, cs in enumerate(chunks):
            width = cs.stop - cs.start
            da = nxt
            if c + 1 < len(chunks):
                nxt = down(chunks[c + 1])
            acc_ref[cs, :] += _dot(a_ref[:, cs], dy, TN)
            row = lax.broadcasted_iota(jnp.int32, (tm + HALO, width), 0)
            gate_ext = jnp.where((row < HALO) & (i == 0), 0.0,
                                 jnp.concatenate([gh_ref[:, cs], g_ref[:, cs]], axis=0).astype(F32))
            dgc = da * vd_ref[:, cs].astype(F32)
            dgc_ref[:, cs] = dgc.astype(BF16)
            dval_ref[:, cs] = (da * act_ref[:, cs].astype(F32)).astype(BF16)
            rows = [jnp.sum(dgc * pltpu.roll(gate_ext, 2 - k, 0)[HALO:], axis=0, keepdims=True) for k in range(2)]
            rows += [jnp.sum(dgc * gate_ext[HALO:], axis=0, keepdims=True), jnp.sum(dgc, axis=0, keepdims=True),
                     jnp.zeros((4, width), F32)]
            dconv_ref[:, cs] += jnp.concatenate(rows, axis=0)

        @pl.when(i == n_t - 1)
        def _():
            dwd_ref[...] = acc_ref[...].astype(BF16)

    tokf = pl.BlockSpec((tm, tf), lambda j, i: (i, j))
    return pl.pallas_call(
        body, name="ffn_bwd_act", grid=(d_ff // tf, n_t),
        in_specs=[pl.BlockSpec((tm, d), lambda j, i: (i, 0)), tokf,
                  pl.BlockSpec((HALO, tf), lambda j, i: (_halo_before(i, tm), j)), tokf, tokf, tokf,
                  pl.BlockSpec((tf, d), lambda j, i: (j, 0))],
        out_specs=[tokf, tokf, pl.BlockSpec((tf, d), lambda j, i: (j, 0)), pl.BlockSpec((8, tf), lambda j, i: (0, j))],
        out_shape=[jax.ShapeDtypeStruct((s_len, d_ff), BF16), jax.ShapeDtypeStruct((s_len, d_ff), BF16),
                   jax.ShapeDtypeStruct((d_ff, d), BF16), jax.ShapeDtypeStruct((8, d_ff), F32)],
        scratch_shapes=[pltpu.VMEM((tf, d), F32)],
        compiler_params=_params(("arbitrary", "arbitrary")),
    )(dy2, gate, gate, a, act, vd, w_down)


def _ffn_bwd_up(dgc, dval, w_up_t, conv_w, tm):
    s_len, d_ff = dgc.shape
    d = w_up_t.shape[1]
    n_t = s_len // tm

    def body(dg_ref, dgn_ref, dv_ref, cw_ref, w_ref, dup_ref, dh_ref):
        i = pl.program_id(0)
        nxt = dgn_ref[...].astype(F32) * (i < n_t - 1).astype(F32)
        ext = jnp.concatenate([dg_ref[...].astype(F32), nxt], axis=0)
        rows = tm + HALO
        dgate = (ext * cw_ref[2:3, :] + pltpu.roll(ext, rows - 1, 0) * cw_ref[1:2, :]
                 + pltpu.roll(ext, rows - 2, 0) * cw_ref[0:1, :])[:tm]
        dup = jnp.concatenate([dgate.astype(BF16), dv_ref[...]], axis=1)
        dup_ref[...] = dup
        dh_ref[...] = _dot(dup, w_ref[...], NN).astype(BF16)

    tokf = pl.BlockSpec((tm, d_ff), lambda i: (i, 0))
    return pl.pallas_call(
        body, name="ffn_bwd_up", grid=(n_t,),
        in_specs=[tokf, pl.BlockSpec((HALO, d_ff), lambda i: (jnp.minimum((i + 1) * (tm // HALO), s_len // HALO - 1), 0)),
                  tokf, pl.BlockSpec((3, d_ff), lambda i: (0, 0)), pl.BlockSpec((2 * d_ff, d), lambda i: (0, 0))],
        out_specs=[pl.BlockSpec((tm, 2 * d_ff), lambda i: (i, 0)), pl.BlockSpec((tm, d), lambda i: (i, 0))],
        out_shape=[jax.ShapeDtypeStruct((s_len, 2 * d_ff), BF16), jax.ShapeDtypeStruct((s_len, d), BF16)],
        compiler_params=_params(("arbitrary",)),
    )(dgc, dgc, dval, conv_w, w_up_t)


def _mix_bwd(dh2, dout, x1, y1, cat, attn, w_out_t, sc_f, g_pre_ffn, gt_m, g_post_mix, after, tm):
    s_len, d = x1.shape
    n_t = s_len // tm

    def body(dh_ref, do_ref, x1_ref, y1_ref, cat_ref, at_ref, wo_ref, sc_ref, g2_ref, gt_ref, g1_ref, after_ref,
             dx1_ref, dpool_ref, dattn_ref, delta_ref, dwo_ref, sums_ref, acc_ref):
        i = pl.program_id(0)
        dh = dh_ref[...].astype(F32)
        x1 = x1_ref[...]
        r2 = _rstd(x1)
        n2 = x1 * r2
        ng = n2 * g2_ref[...]
        dng = dh * (1.0 + sc_ref[...])
        dx1 = do_ref[...] + _norm_bwd(dng * g2_ref[...], n2, r2)
        dx1_ref[...] = dx1
        y1 = y1_ref[...].astype(F32)
        r1 = _rstd(y1)
        n1 = y1 * r1
        drn = dx1 * gt_ref[...]
        dy1 = _norm_bwd(drn * g1_ref[...], n1, r1).astype(BF16)
        dcat = _dot(dy1, wo_ref[...], NN)
        dpool_ref[...] = dcat[:, 0:256]
        lane = lax.broadcasted_iota(jnp.int32, (tm, LANES), 1)
        first = lane < HEAD_DIM
        for s in range(2):
            da = dcat[:, 256 + s * LANES:256 + (s + 1) * LANES]
            dattn_ref[s] = da
            prod = da * at_ref[:, s * LANES:(s + 1) * LANES]
            tot = jnp.sum(prod, axis=-1, keepdims=True)
            lo = jnp.sum(jnp.where(first, prod, 0.0), axis=-1, keepdims=True)
            delta_ref[s] = jnp.where(first, lo, tot - lo)
        dwo = _dot(dy1, cat_ref[...], TN)
        sums = jnp.concatenate(
            [jnp.sum(dh, axis=0, keepdims=True), jnp.sum(dh * ng, axis=0, keepdims=True),
             jnp.sum(dng * n2, axis=0, keepdims=True), jnp.sum(dx1 * (n1 * g1_ref[...]), axis=0, keepdims=True),
             jnp.sum(drn * n1, axis=0, keepdims=True), jnp.zeros((3, d), F32)], axis=0)

        @pl.when(i == 0)
        def _():
            acc_ref[...] = dwo
            sums_ref[...] = sums

        @pl.when(i > 0)
        def _():
            acc_ref[...] += dwo
            sums_ref[...] += sums

        @pl.when(i == n_t - 1)
        def _():
            dwo_ref[...] = acc_ref[...].astype(BF16)

    tile = lambda w: pl.BlockSpec((tm, w), lambda i: (i, 0))
    slab = pl.BlockSpec((2, tm, LANES), lambda i: (0, i, 0))
    vec = pl.BlockSpec((1, d), lambda i: (0, 0))
    return pl.pallas_call(
        body, name="mix_bwd", grid=(n_t,),
        in_specs=[tile(d), tile(d), tile(d), tile(d), tile(512), tile(256),
                  pl.BlockSpec((d, 512), lambda i: (0, 0)), vec, vec, vec, vec, pl.BlockSpec(memory_space=pl.ANY)],
        out_specs=[tile(d), tile(256), slab, slab, pl.BlockSpec((d, 512), lambda i: (0, 0)),
                   pl.BlockSpec((8, d), lambda i: (0, 0))],
        out_shape=[jax.ShapeDtypeStruct((s_len, d), F32), jax.ShapeDtypeStruct((s_len, 256), F32),
                   jax.ShapeDtypeStruct((2, s_len, LANES), F32), jax.ShapeDtypeStruct((2, s_len, LANES), F32),
                   jax.ShapeDtypeStruct((d, 512), BF16), jax.ShapeDtypeStruct((8, d), F32)],
        scratch_shapes=[pltpu.VMEM((d, 512), F32)],
        compiler_params=_params(("arbitrary",)),
    )(dh2, dout, x1, y1, cat, attn, w_out_t, sc_f, g_pre_ffn, gt_m, g_post_mix, after)


def _pool_bwd(dpool, u_pool, w_blk, b_pool, pool_scale, tm):
    s_len = dpool.shape[0]
    n_t = s_len // tm

    def body(dp_ref, dpn_ref, u_ref, uh_ref, wb_ref, bp_ref, ps_ref, du_ref, dwp_ref, sums_ref, acc_ref):
        i = pl.program_id(0)
        u = u_ref[...]
        mixed, _ = _pool_mixed(u, uh_ref[...] * (i > 0).astype(F32), i, tm)
        mixed_b = mixed.astype(BF16)
        y = _dot(mixed_b, wb_ref[...], NN) + bp_ref[...]
        dp = dp_ref[...]
        dy = dp * ps_ref[...]
        dwb = _dot(mixed_b, dy.astype(BF16), TN)
        sums = jnp.concatenate([jnp.sum(dy, axis=0, keepdims=True), jnp.sum(dp * y, axis=0, keepdims=True),
                                jnp.zeros((6, 256), F32)], axis=0)
        dp_ext = jnp.concatenate([dp, dpn_ref[...] * (i < n_t - 1).astype(F32)], axis=0)
        dmix = _dot((dp_ext * ps_ref[...]).astype(BF16), wb_ref[...], NT)
        rows = tm + HALO
        grp = lax.broadcasted_iota(jnp.int32, (rows, 256), 1) // HEAD_DIM
        pick = lambda a, b, c, e: jnp.where(grp == 0, a, jnp.where(grp == 1, b, jnp.where(grp == 2, c, e)))
        pos = (i * tm + lax.broadcasted_iota(jnp.int32, (rows, 256), 0)).astype(F32)
        z = dmix / jnp.minimum(pos + 1.0, pick(*[float(w) for w in POOL_WINDOWS]))
        f2 = z + pltpu.roll(z, rows - 1, 0)
        f4 = f2 + pltpu.roll(f2, rows - 2, 0)
        f8 = f4 + pltpu.roll(f4, rows - 4, 0)
        f16 = f8 + pltpu.roll(f8, rows - 8, 0)
        du_ref[...] = (pick(f2, f4, f8, f16) - dmix)[:tm]

        @pl.when(i == 0)
        def _():
            acc_ref[...] = dwb
            sums_ref[...] = sums

        @pl.when(i > 0)
        def _():
            acc_ref[...] += dwb
            sums_ref[...] += sums

        @pl.when(i == n_t - 1)
        def _():
            full = acc_ref[...]
            for gi in range(len(POOL_WINDOWS)):
                lo = gi * HEAD_DIM
                dwp_ref[gi] = full[lo:lo + HEAD_DIM, lo:lo + HEAD_DIM]

    n_g = len(POOL_WINDOWS)
    tile = pl.BlockSpec((tm, 256), lambda i: (i, 0))
    const = lambda a: pl.BlockSpec(a.shape, lambda i: (0,) * a.ndim)
    return pl.pallas_call(
        body, name="pool_bwd", grid=(n_t,),
        in_specs=[tile, pl.BlockSpec((HALO, 256), lambda i: (jnp.minimum((i + 1) * (tm // HALO), s_len // HALO - 1), 0)),
                  tile, pl.BlockSpec((HALO, 256), lambda i: (_halo_before(i, tm), 0)),
                  const(w_blk), const(b_pool), const(pool_scale)],
        out_specs=[tile, pl.BlockSpec((n_g, HEAD_DIM, HEAD_DIM), lambda i: (0, 0, 0)), pl.BlockSpec((8, 256), lambda i: (0, 0))],
        out_shape=[jax.ShapeDtypeStruct((s_len, 256), F32), jax.ShapeDtypeStruct((n_g, HEAD_DIM, HEAD_DIM), F32),
                   jax.ShapeDtypeStruct((8, 256), F32)],
        scratch_shapes=[pltpu.VMEM((256, 256), F32)],
        compiler_params=_params(("arbitrary",)),
    )(dpool, dpool, u_pool, u_pool, w_blk, b_pool, pool_scale)


def _attn_bwd(qkv, dattn, lse_all, delta, after):
    s_len = qkv.shape[1]
    n_g = len(DILATIONS)

    def body(q_ref, k_ref, v_ref, do_ref, l_ref, dl_ref, after_ref, dq_ref, dk_ref, dv_ref):
        lane = lax.broadcasted_iota(jnp.int32, (BLOCK, LANES), 1)
        first = lane < HEAD_DIM

        def group(dil):
            nb = s_len // (BLOCK * dil)

            def block(t, carry):
                dk_part, dv_part = carry
                r, n = t // nb, t % nb
                cur = _block_rows(n, r, dil)
                prev = _block_rows(jnp.maximum(n - 1, 0), r, dil)
                q = q_ref[0, cur, :]
                do = do_ref[0, cur, :]
                lse = l_ref[0, cur, :]
                dlt = dl_ref[0, cur, :]
                kcat = jnp.concatenate([k_ref[0, prev, :], k_ref[0, cur, :]], axis=0).astype(BF16)
                vcat = jnp.concatenate([v_ref[0, prev, :], v_ref[0, cur, :]], axis=0).astype(BF16)
                valid = _band_mask(n)
                stack = lambda a: jnp.concatenate([jnp.where(first, a, 0.0), jnp.where(first, 0.0, a)], axis=0)
                rows2 = lambda a: jnp.concatenate([a[:, 0:1], a[:, HEAD_DIM:HEAD_DIM + 1]], axis=0)
                q2, do2 = stack(q).astype(BF16), stack(do).astype(BF16)
                valid2 = jnp.concatenate([valid, valid], axis=0)
                p = jnp.where(valid2, jnp.exp(_dot(q2, kcat, NT) - rows2(lse)), 0.0)
                ds = (p * (_dot(do2, vcat, NT) - rows2(dlt))).astype(BF16)
                dq2 = _dot(ds, kcat, NN)
                dq_ref[0, 0, cur, :] = jnp.where(first, dq2[:BLOCK], dq2[BLOCK:])
                dkc = _dot(ds, q2, TN)
                dvc = _dot(p.astype(BF16), do2, TN)
                dk_ref[0, 0, prev, :] = dk_part + dkc[:BLOCK]
                dv_ref[0, 0, prev, :] = dv_part + dvc[:BLOCK]
                dk_ref[0, 0, cur, :] = dkc[BLOCK:]
                dv_ref[0, 0, cur, :] = dvc[BLOCK:]
                return dkc[BLOCK:], dvc[BLOCK:]

            def blocks(tt, carry):
                for u in range(ATTN_BWD_UNROLL):
                    carry = block(tt * ATTN_BWD_UNROLL + u, carry)
                return carry

            zero = jnp.zeros((BLOCK, LANES), F32)
            lax.fori_loop(0, nb * dil // ATTN_BWD_UNROLL, blocks, (zero, zero))

        for gi, dil in enumerate(DILATIONS):
            pl.when(pl.program_id(1) == gi)(functools.partial(group, dil))

    def slab(base):
        return pl.BlockSpec((1, s_len, LANES), lambda s, g: (base + 2 * g + s, 0, 0))

    one = pl.BlockSpec((1, s_len, LANES), lambda s, g: (s, 0, 0))
    out = pl.BlockSpec((1, 1, s_len, LANES), lambda s, g: (g, s, 0, 0))
    shape = jax.ShapeDtypeStruct((n_g, 2, s_len, LANES), F32)
    return pl.pallas_call(
        body, name="attn_bwd", grid=(2, n_g),
        in_specs=[slab(0), slab(6), slab(12), one, one, one, pl.BlockSpec(memory_space=pl.ANY)],
        out_specs=[out, out, out], out_shape=[shape, shape, shape],
        compiler_params=_params(("arbitrary", "arbitrary")),
    )(qkv, qkv, qkv, dattn, lse_all, delta, after)


def _dproj_assemble(du, dqkv, rope, tm):
    s_len = du.shape[0]
    n_proj = 256 + 18 * LANES

    def body(du_ref, dq_ref, dk_ref, dv_ref, cs_ref, spread_ref, dproj_ref):
        dproj_ref[:, 0:256] = du_ref[...].astype(BF16)
        lanes = _rope_lanes(cs_ref, spread_ref)
        col = 256
        for kind, dref in enumerate((dq_ref, dk_ref, dv_ref)):
            for grp in range(3):
                for s in range(2):
                    piece = dref[grp, s]
                    if kind < 2:
                        piece = _rope_bwd(piece, lanes)
                    if kind == 0:
                        piece = piece * (HEAD_DIM ** -0.5)
                    dproj_ref[:, col:col + LANES] = piece.astype(BF16)
                    col += LANES

    groups = pl.BlockSpec((len(DILATIONS), 2, tm, LANES), lambda i: (0, 0, i, 0))
    return pl.pallas_call(
        body, name="dproj_assemble", grid=(s_len // tm,),
        in_specs=[pl.BlockSpec((tm, 256), lambda i: (i, 0))] + [groups] * 3
        + [pl.BlockSpec((tm, rope[0].shape[1]), lambda i: (i, 0)), pl.BlockSpec(rope[1].shape, lambda i: (0, 0, 0))],
        out_specs=pl.BlockSpec((tm, n_proj), lambda i: (i, 0)),
        out_shape=jax.ShapeDtypeStruct((s_len, n_proj), BF16),
        compiler_params=_params(("arbitrary",)),
    )(du, *dqkv, *rope)


def _inproj_bwd(dproj, w_in_t, x, dx1, sc_m, g_pre_mix, after, tm):
    s_len, d = x.shape
    n_proj = w_in_t.shape[0]
    n_t = s_len // tm

    def body(dproj_ref, w_ref, x_ref, dx1_ref, sc_ref, g_ref, after_ref, dx_ref, sums_ref):
        i = pl.program_id(0)
        halves = [slice(0, tm // 2), slice(tm // 2, tm)]
        dhs = [_dot(dproj_ref[rs, :], w_ref[...], NN) for rs in halves]
        sums = None
        for rs, dh in zip(halves, dhs):
            xv = x_ref[rs, :]
            r = _rstd(xv)
            n = xv * r
            dng = dh * (1.0 + sc_ref[...])
            dx_ref[rs, :] = dx1_ref[rs, :] + _norm_bwd(dng * g_ref[...], n, r)
            part = jnp.concatenate([jnp.sum(dh, axis=0, keepdims=True), jnp.sum(dh * (n * g_ref[...]), axis=0, keepdims=True),
                                    jnp.sum(dng * n, axis=0, keepdims=True), jnp.zeros((5, d), F32)], axis=0)
            sums = part if sums is None else sums + part

        @pl.when(i == 0)
        def _():
            sums_ref[...] = sums

        @pl.when(i > 0)
        def _():
            sums_ref[...] += sums

    tile = lambda w: pl.BlockSpec((tm, w), lambda i: (i, 0))
    vec = pl.BlockSpec((1, d), lambda i: (0, 0))
    return pl.pallas_call(
        body, name="inproj_bwd", grid=(n_t,),
        in_specs=[tile(n_proj), pl.BlockSpec((n_proj, d), lambda i: (0, 0)), tile(d), tile(d), vec, vec,
                  pl.BlockSpec(memory_space=pl.ANY)],
        out_specs=[tile(d), pl.BlockSpec((8, d), lambda i: (0, 0))],
        out_shape=[jax.ShapeDtypeStruct((s_len, d), F32), jax.ShapeDtypeStruct((8, d), F32)],
        compiler_params=_params(("arbitrary",)),
    )(dproj, w_in_t, x, dx1, sc_m, g_pre_mix, after)


def _wgrad(a, b, name, tk, tmm):
    s_len, m = a.shape
    n = b.shape[1]
    n_k = s_len // tk

    def body(a_ref, b_ref, o_ref, acc_ref):
        k = pl.program_id(1)
        part = _dot(a_ref[...], b_ref[...], TN)

        @pl.when(k == 0)
        def _():
            acc_ref[...] = part

        @pl.when(k > 0)
        def _():
            acc_ref[...] += part

        @pl.when(k == n_k - 1)
        def _():
            o_ref[...] = acc_ref[...].astype(BF16)

    return pl.pallas_call(
        body, name=name, grid=(m // tmm, n_k),
        in_specs=[pl.BlockSpec((tk, tmm), lambda j, k: (k, j)), pl.BlockSpec((tk, n), lambda j, k: (k, 0))],
        out_specs=pl.BlockSpec((tmm, n), lambda j, k: (j, 0)),
        out_shape=jax.ShapeDtypeStruct((m, n), BF16),
        scratch_shapes=[pltpu.VMEM((tmm, n), F32)],
        compiler_params=_params(("arbitrary", "arbitrary")),
    )(a, b)


def _place():
    return lax.axis_index("x"), lax.axis_index("y"), lax.axis_index("c")


def _peer(k):
    x, y, c = _place()
    bx, by, bc = (k >> 2) & 1, (k >> 1) & 1, k & 1
    return (x ^ bx if bx else x, y ^ by if by else y, c ^ bc if bc else c)


def _index(pos):
    return 4 * pos[0] + 2 * pos[1] + pos[2]


def _entry_exchange(c_rows, w_ada, b_ada_cols, taps, shards):
    d = c_rows.shape[1]
    ncol = w_ada.shape[1]
    n_w = len(shards)

    def body(c_ref, w_ref, b_ref, t_ref, *rest):
        srcs = rest[:n_w]
        call_ref, mod_ref, tall_ref = rest[n_w:n_w + 3]
        outs = rest[n_w + 3:2 * n_w + 3]
        stage_ref, s_send, s_recv, w_send, w_recv, local_sems = rest[2 * n_w + 3:]
        x, y, c = _place()
        here, sibling = (x, y, c), (x, y, 1 - c)
        chips = [(1 - x, y), (x, 1 - y), (1 - x, 1 - y)]
        me = _index(here)

        def small(kind, src, dst, k):
            return pltpu.make_async_remote_copy(src_ref=src, dst_ref=dst, send_sem=s_send.at[kind, k - 1],
                                                recv_sem=s_recv.at[kind, k - 1], device_id=_peer(k), device_id_type=MESH)

        gather = lambda k: small(0, c_ref, call_ref.at[me], k)
        scatter = lambda k: small(1, stage_ref.at[_index(_peer(k))], mod_ref.at[me], k)
        gather_taps = lambda k: small(2, t_ref, tall_ref.at[me], k)

        def rows(w, pos):
            r = shards[w].shape[0]
            return outs[w].at[pl.ds(pl.multiple_of(_index(pos) * r, 16), r), :]

        def block(k, w, pos, to, own=False):
            return pltpu.make_async_remote_copy(
                src_ref=srcs[w] if own else rows(w, pos), dst_ref=rows(w, pos),
                send_sem=w_send.at[k, w], recv_sem=w_recv.at[k, w], device_id=to, device_id_type=MESH)

        call_ref[me] = c_ref[...]
        tall_ref[me] = t_ref[...]
        for k in range(1, N_DEV):
            gather(k).start()
        for k in range(1, N_DEV):
            gather_taps(k).start()
        mine = [pltpu.make_async_copy(srcs[w], rows(w, here), local_sems.at[w]) for w in range(n_w)]
        for cp in mine:
            cp.start()
        first = [block(0, w, here, sibling, own=True) for w in range(n_w)]
        first += [block(1 + j, w, here, (*chip, c), own=True) for j, chip in enumerate(chips) for w in range(n_w)]
        for cp in first:
            cp.start()

        for k in range(1, N_DEV):
            gather(k).wait_recv()
        cv = jnp.concatenate([call_ref[b, 0:1, :] for b in range(N_DEV)], axis=0)
        act = cv * jax.nn.sigmoid(cv)
        mod = lax.dot_general(act, w_ref[...], NN, preferred_element_type=F32,
                              precision=lax.Precision.HIGHEST) + b_ref[...]
        for b in range(N_DEV):
            stage_ref[b] = jnp.broadcast_to(mod[b:b + 1, :], (8, ncol))
        mod_ref[me] = stage_ref[me]
        for k in range(1, N_DEV):
            scatter(k).start()

        passed = []
        for j, chip in enumerate(chips):
            for w in range(n_w):
                block(1 + j, w, (*chip, c), here).wait_recv()
                fwd = block(4 + j, w, (*chip, c), sibling)
                fwd.start()
                passed.append(fwd)
        for w in range(n_w):
            block(0, w, sibling, here).wait_recv()
        for j, chip in enumerate(chips):
            for w in range(n_w):
                block(4 + j, w, (*chip, 1 - c), here).wait_recv()
        for k in range(1, N_DEV):
            scatter(k).wait_recv()
            gather_taps(k).wait_recv()
        for cp in first + passed:
            cp.wait_send()
        for k in range(1, N_DEV):
            gather(k).wait_send()
            scatter(k).wait_send()
            gather_taps(k).wait_send()
        for cp in mine:
            cp.wait()

    vmem, hbm = pl.BlockSpec(memory_space=pltpu.VMEM), pl.BlockSpec(memory_space=pltpu.HBM)
    out = pl.pallas_call(
        body, name="entry_exchange",
        in_specs=[vmem] * 4 + [hbm] * n_w, out_specs=[vmem] * 3 + [hbm] * n_w,
        out_shape=[jax.ShapeDtypeStruct((N_DEV, 8, d), F32), jax.ShapeDtypeStruct((N_DEV, 8, ncol), F32),
                   jax.ShapeDtypeStruct((N_DEV,) + taps.shape, F32)]
        + [jax.ShapeDtypeStruct((N_DEV * s.shape[0], s.shape[1]), s.dtype) for s in shards],
        scratch_shapes=[pltpu.VMEM((N_DEV, 8, ncol), F32), pltpu.SemaphoreType.DMA((3, N_DEV - 1)),
                        pltpu.SemaphoreType.DMA((3, N_DEV - 1)), pltpu.SemaphoreType.DMA((N_DEV - 1, n_w)),
                        pltpu.SemaphoreType.DMA((N_DEV - 1, n_w)), pltpu.SemaphoreType.DMA((n_w,))],
        compiler_params=_params(),
    )(c_rows, w_ada, b_ada_cols, taps, *shards)
    return out[0], out[1], out[2], out[3:]


def _peer_copies(mode, srcs, lands, send_sems, recv_sems):
    if mode in ("gather_ici", "gather_d2d"):
        x, y, c = _place()
        sibling = (x, y, 1 - c)
        chips = [(1 - x, y), (x, 1 - y), (1 - x, 1 - y)]
        n = len(lands)

        def rows(w, pos):
            r = lands[w].shape[0] // N_DEV
            return lands[w].at[pl.ds(pl.multiple_of(_index(pos) * r, 16), r), :]

        def copy(k, w, src, dst, to):
            return pltpu.make_async_remote_copy(src_ref=src, dst_ref=dst, send_sem=send_sems.at[k * n + w],
                                                recv_sem=recv_sems.at[k * n + w], device_id=to, device_id_type=MESH)

        if mode == "gather_ici":
            targets = [sibling] + [(*chip, c) for chip in chips]
            return [copy(k, w, srcs[w], rows(w, (x, y, c)), to) for k, to in enumerate(targets) for w in range(n)]
        return [copy(j, w, rows(w, (*chip, c)), rows(w, (*chip, c)), sibling)
                for j, chip in enumerate(chips) for w in range(n)]
    me = _index(_place())
    modes = (mode,) * len(srcs) if isinstance(mode, str) else mode
    copies = []
    for k in range(1, N_DEV):
        peer = _peer(k)
        for w, (src, land) in enumerate(zip(srcs, lands)):
            if modes[w] == "gather":
                r = src.shape[0]
                dst = land.at[pl.ds(pl.multiple_of(me * r, 16), r), :]
            elif modes[w] == "allgather":
                dst = land.at[me]
            else:
                r = src.shape[0] // N_DEV
                src = src.at[pl.ds(pl.multiple_of(_index(peer) * r, 16), r), :]
                dst = land.at[me]
            copies.append(pltpu.make_async_remote_copy(
                src_ref=src, dst_ref=dst, send_sem=send_sems.at[(k - 1) * len(srcs) + w],
                recv_sem=recv_sems.at[(k - 1) * len(srcs) + w],
                device_id=peer, device_id_type=MESH))
    return copies


def _landing_zone(mode, src, me, name):
    cols = src.shape[1]
    if mode == "gather":
        r = src.shape[0]
        in_spec = pl.BlockSpec((r, cols), lambda i, me_ref: (0, 0))
        out_spec = pl.BlockSpec((r, cols), lambda i, me_ref: (me_ref[0], 0))
        out_shape = jax.ShapeDtypeStruct((N_DEV * r, cols), src.dtype)
    else:
        r = src.shape[0] // N_DEV
        in_spec = pl.BlockSpec((r, cols), lambda i, me_ref: (me_ref[0], 0))
        out_spec = pl.BlockSpec((1, r, cols), lambda i, me_ref: (me_ref[0], 0, 0))
        out_shape = jax.ShapeDtypeStruct((N_DEV, r, cols), src.dtype)

    def body(me_ref, s_ref, o_ref):
        o_ref[...] = s_ref[...].reshape(o_ref.shape)

    return pl.pallas_call(
        body, name=name, out_shape=out_shape,
        grid_spec=pltpu.PrefetchScalarGridSpec(num_scalar_prefetch=1, grid=(1,), in_specs=[in_spec], out_specs=out_spec),
        compiler_params=_params(("arbitrary",)),
    )(me.reshape(1).astype(jnp.int32), src)


def _exchange_start(mode, srcs, lands, name):
    n_s, n_a = len(srcs), len(srcs) + len(lands)
    n_cp = _COPIES_PER_ARRAY.get(mode, N_DEV - 1) * len(lands)

    def body(*refs):
        for cp in _peer_copies(mode, refs[:n_s], refs[n_s:n_a], refs[n_a], refs[n_a + 1]):
            cp.start()
        refs[-1][...] = jnp.zeros_like(refs[-1])

    hbm, sem = pl.BlockSpec(memory_space=pltpu.HBM), pl.BlockSpec(memory_space=pltpu.SEMAPHORE)
    arrays = list(srcs) + list(lands)
    out = pl.pallas_call(
        body, name=name,
        out_shape=(pltpu.SemaphoreType.DMA((n_cp,)), pltpu.SemaphoreType.DMA((n_cp,)),
                   *[pltpu.HBM(a.shape, a.dtype) for a in arrays], jax.ShapeDtypeStruct((8, LANES), F32)),
        in_specs=[hbm] * n_a, out_specs=(sem, sem, *[hbm] * n_a, pl.BlockSpec(memory_space=pltpu.VMEM)),
        input_output_aliases={i: 2 + i for i in range(n_a)},
        compiler_params=pltpu.CompilerParams(has_side_effects=pltpu.SideEffectType.DATAFLOW_SIDE_EFFECTING),
    )(*[pltpu.with_memory_space_constraint(a, pltpu.HBM) for a in arrays])
    return out[0], out[1], out[2:2 + n_s], out[2 + n_s:2 + n_a], out[-1]


_COPIES_PER_ARRAY = {"gather_ici": 4, "gather_d2d": 3}


def _exchange_wait(mode, send_sems, recv_sems, srcs, lands, after, name):
    n_s, n_a = len(srcs), len(srcs) + len(lands)

    def body(*refs):
        copies = _peer_copies(mode, refs[:n_s], refs[n_s:n_a], refs[n_a], refs[n_a + 1])
        for cp in copies:
            cp.wait_send()
        for cp in copies:
            cp.wait_recv()

    hbm, sem = pl.BlockSpec(memory_space=pltpu.HBM), pl.BlockSpec(memory_space=pltpu.SEMAPHORE)
    arrays = list(srcs) + list(lands)
    out = pl.pallas_call(
        body, name=name, out_shape=tuple(pltpu.HBM(a.shape, a.dtype) for a in arrays),
        in_specs=[hbm] * n_a + [sem, sem, pl.BlockSpec(memory_space=pl.ANY)], out_specs=tuple([hbm] * n_a),
        input_output_aliases={i: i for i in range(n_a)},
        compiler_params=pltpu.CompilerParams(has_side_effects=pltpu.SideEffectType.DATAFLOW_SIDE_EFFECTING),
    )(*arrays, send_sems, recv_sems, after)
    return out[:n_s], out[n_s:]


SMALL_WEIGHTS = ("b_ada", "g_pre_mix", "g_post_mix", "g_pre_ffn", "g_post_ffn", "w_pool", "b_pool", "pool_scale", "conv_b")


MOD_ROWS = ((0, 0), (0, 1), (1, 3), (1, 0), (1, 1), (2, 0))


def _small_sum(mine, gathered):
    n_l = len(mine)
    d = mine[0].shape[1]

    def body(*refs):
        loc, got = refs[:n_l], refs[n_l:2 * n_l]
        tot_refs, dmod_ref = refs[2 * n_l:3 * n_l], refs[3 * n_l]
        me = _index(_place())
        part = lambda a, dev: jnp.where(dev == me, loc[a][...], got[a][dev])
        for a in range(n_l):
            tot = part(a, 0)
            for dev in range(1, N_DEV):
                tot = tot + part(a, dev)
            tot_refs[a][...] = tot
        for dev in range(N_DEV):
            for k, (a, r) in enumerate(MOD_ROWS):
                dmod_ref[dev:dev + 1, k * d:(k + 1) * d] = part(a, dev)[r:r + 1, :]

    vmem = pl.BlockSpec(memory_space=pltpu.VMEM)
    out = pl.pallas_call(
        body, name="small_sum", in_specs=[vmem] * (2 * n_l), out_specs=[vmem] * (n_l + 1),
        out_shape=[jax.ShapeDtypeStruct(a.shape, F32) for a in mine] + [jax.ShapeDtypeStruct((N_DEV, 6 * d), F32)],
        compiler_params=_params(),
    )(*mine, *gathered)
    return out[:n_l], out[n_l]


def _small_adam(totals, weights, moms, vels):
    n_t, n_w = len(totals), len(weights)

    def body(*refs):
        t_in, t_mix, t_ffn, t_pool, t_blk, t_conv, _ = (r[...] for r in refs[:n_t])
        w_refs, m_refs, v_refs = (refs[n_t + k * n_w:n_t + (k + 1) * n_w] for k in range(3))
        outs = refs[n_t + 3 * n_w:]

        def update(idx, g, at=()):
            sel = lambda ref: ref.at[at] if at else ref
            delta, nm, nv = _adam_math(sel(w_refs[idx])[...], g, sel(m_refs[idx])[...], sel(v_refs[idx])[...])
            for k, val in enumerate((g, delta, nm, nv)):
                sel(outs[4 * idx + k])[...] = val

        tots = (t_in, t_mix, t_ffn)
        update(0, jnp.concatenate([tots[a][r:r + 1] for a, r in MOD_ROWS], axis=1))
        update(1, t_in[2:3])
        update(2, t_mix[4:5])
        update(3, t_mix[2:3])
        update(4, t_ffn[1:2])
        for gi in range(len(POOL_WINDOWS)):
            update(5, t_blk[gi], at=(0, gi))
        update(6, jnp.concatenate([t_pool[0:1, gi * HEAD_DIM:(gi + 1) * HEAD_DIM] for gi in range(len(POOL_WINDOWS))], axis=0),
               at=(0,))
        update(7, t_pool[1:2])
        update(8, t_conv[3:4])

    vmem = pl.BlockSpec(memory_space=pltpu.VMEM)
    return pl.pallas_call(
        body, name="small_adam", in_specs=[vmem] * (n_t + 3 * n_w), out_specs=[vmem] * (4 * n_w),
        out_shape=[jax.ShapeDtypeStruct(w.shape, F32) for w in weights for _ in range(4)],
        compiler_params=_params(),
    )(*totals, *weights, *moms, *vels)


def _adam_math(w, g, m, v):
    m = ADAM_B1 * m + (1.0 - ADAM_B1) * g
    v = ADAM_B2 * v + (1.0 - ADAM_B2) * (g * g)
    m_hat = m / (1.0 - ADAM_B1 ** ADAM_STEP)
    v_hat = v / (1.0 - ADAM_B2 ** ADAM_STEP)
    delta = -ADAM_LR * (m_hat / (jnp.sqrt(v_hat) + ADAM_EPS) + ADAM_WD * w)
    return delta, m, v


def _adam(w, g, m, v, name, tr):
    rows, cols = w.shape

    def body(w_ref, g_ref, m_ref, v_ref, d_ref, nm_ref, nv_ref):
        d_ref[...], nm_ref[...], nv_ref[...] = _adam_math(w_ref[...], g_ref[...], m_ref[...], v_ref[...])

    spec = pl.BlockSpec((tr, cols), lambda i: (i, 0))
    shape = jax.ShapeDtypeStruct((rows, cols), F32)
    return pl.pallas_call(
        body, name=name, grid=(rows // tr,), in_specs=[spec] * 4, out_specs=[spec] * 3,
        out_shape=[shape] * 3, compiler_params=_params(("arbitrary",)),
    )(w, g, m, v)


def _sum_adam(parts, w, m, v, name, tr):
    _, rows, cols = parts.shape
    turned = w.shape == (cols, rows) and rows != cols
    assert tr == rows or not turned

    def body(p_ref, w_ref, m_ref, v_ref, g_ref, d_ref, nm_ref, nv_ref):
        g = p_ref[0].astype(F32)
        for dev in range(1, N_DEV):
            g = g + p_ref[dev].astype(F32)
        g = g.T if turned else g
        g_ref[...] = g
        d_ref[...], nm_ref[...], nv_ref[...] = _adam_math(w_ref[...], g, m_ref[...], v_ref[...])

    spec = pl.BlockSpec((cols, rows) if turned else (tr, cols), lambda i: (i, 0))
    shape = jax.ShapeDtypeStruct(w.shape, F32)
    return pl.pallas_call(
        body, name=name, grid=(rows // tr,),
        in_specs=[pl.BlockSpec((N_DEV, tr, cols), lambda i: (0, i, 0)), spec, spec, spec],
        out_specs=[spec] * 4, out_shape=[shape] * 4, compiler_params=_params(("arbitrary",)),
    )(parts, w, m, v)


def _ada_grad_adam(c_all, dmod_cols, w, m, v, tr):
    rows, cols = w.shape

    def body(c_ref, dm_ref, w_ref, m_ref, v_ref, g_ref, d_ref, nm_ref, nv_ref):
        cv = c_ref[...]
        act = cv * jax.nn.sigmoid(cv)
        g = lax.dot_general(act, dm_ref[...], TN, preferred_element_type=F32, precision=lax.Precision.HIGHEST)
        g_ref[...] = g
        d_ref[...], nm_ref[...], nv_ref[...] = _adam_math(w_ref[...], g, m_ref[...], v_ref[...])

    spec = pl.BlockSpec((tr, cols), lambda i: (i, 0))
    shape = jax.ShapeDtypeStruct((rows, cols), F32)
    return pl.pallas_call(
        body, name="ada_grad_adam", grid=(rows // tr,),
        in_specs=[pl.BlockSpec((N_DEV, tr), lambda i: (0, i)), pl.BlockSpec((N_DEV, cols), lambda i: (0, 0)), spec, spec, spec],
        out_specs=[spec] * 4, out_shape=[shape] * 4, compiler_params=_params(("arbitrary",)),
    )(c_all, dmod_cols, w, m, v)


def _rope_tables(positions):
    inv_freq = ROPE_THETA ** (-jnp.arange(0, 2 * ROT_HALF, 2, dtype=F32) / (2 * ROT_HALF))
    ang = positions.astype(F32)[:, None] * inv_freq
    rows = jnp.concatenate([jnp.cos(ang), jnp.sin(ang), jnp.ones_like(ang)], axis=1)
    spread = [[[0.0] * LANES for _ in range(3 * ROT_HALF)] for _ in range(3)]
    for lane in range(LANES):
        p, j = lane % HEAD_DIM, lane % ROT_HALF
        if p < ROT_HALF:
            spread[0][j][lane] = 1.0
            spread[1][ROT_HALF + j][lane] = -1.0
        elif p < 2 * ROT_HALF:
            spread[0][j][lane] = 1.0
            spread[2][ROT_HALF + j][lane] = 1.0
        else:
            spread[0][2 * ROT_HALF][lane] = 1.0
    return rows, jnp.array(spread, F32)


def _pad_rows(a, rows):
    return jnp.pad(a, ((0, rows - a.shape[0]), (0, 0)))


def _sequence_step(xs, target, rope, mods, gains, w_in_t, w_out_t, relay_ffn, fetch_ffn, send_grads, w_blk_b, b_pool_r,
                   pool_scale_r, conv_w_all, conv_b, after):
    sh_m, sc_m, gt_m, sh_f, sc_f, gt_f = mods
    g_pre_mix, g_post_mix, g_pre_ffn, g_post_ffn = gains
    h1, u_pool, qkv = _premix_inproj(xs, sh_m, sc_m, g_pre_mix, w_in_t, rope, after, tm=512)
    o_g, lse_g = _attn_fwd(qkv)
    x1, y1, h2, cat, attn, lse_all = _mix_out(xs, u_pool, o_g, lse_g, w_blk_b, b_pool_r, pool_scale_r, w_out_t,
                                              gt_m, g_post_mix, g_pre_ffn, sc_f, sh_f, tm=256)
    token = relay_ffn(x1)
    w_up_t, w_down_f = fetch_ffn(x1 if token is None else token)
    gate, a_ffn, act, vd, dy2, dout, sums_ffn, loss_loc = _ffn_fwd_loss(h2, x1, target, w_up_t, w_down_f, conv_w_all, conv_b,
                                                              gt_f, g_post_ffn, tm=256, ck=256)

    dgc, dval, dw_down, dconv = _ffn_bwd_act(dy2, gate, a_ffn, act, vd, w_down_f, tm=512, tf=1408, ck=256)
    dup, dh2 = _ffn_bwd_up(dgc, dval, w_up_t, conv_w_all, tm=256)
    dw_up_t = _wgrad(dup, h2, "wgrad_up", tk=2048, tmm=1408)
    token = send_grads("ffn", [dw_up_t, dw_down], [])
    dx1, dpool, dattn, delta, dw_out_t, sums_mix = _mix_bwd(dh2, dout, x1, y1, cat, attn, w_out_t, sc_f,
                                                           g_pre_ffn, gt_m, g_post_mix, token, tm=256)
    du, dw_blk, sums_pool = _pool_bwd(dpool, u_pool, w_blk_b, b_pool_r, pool_scale_r, tm=512)
    token = send_grads("out", [dw_out_t], [sums_mix, sums_ffn, sums_pool, dw_blk, dconv, loss_loc])
    dproj = _dproj_assemble(du, _attn_bwd(qkv, dattn, lse_all, delta, token), rope, tm=512)
    dw_in_t = _wgrad(dproj, h1, "wgrad_in", tk=2048, tmm=1280)
    token = send_grads("in", [dw_in_t], [])
    grad_x, sums_in = _inproj_bwd(dproj, w_in_t, xs, dx1, sc_m, g_pre_mix, token, tm=256)
    return (loss_loc, grad_x, dw_in_t, dw_out_t, dw_up_t, dw_down, dw_blk, dconv,
            sums_in, sums_mix, sums_ffn, sums_pool)


def kernel(x, c, positions, w_ada, b_ada, g_pre_mix, g_post_mix, g_pre_ffn, g_post_ffn, w_in, w_pool, b_pool, pool_scale, w_out, w_up, conv_w, conv_b, w_down, loss_target, m_w_ada, m_b_ada, m_g_pre_mix, m_g_post_mix, m_g_pre_ffn, m_g_post_ffn, m_w_in, m_w_pool, m_b_pool, m_pool_scale, m_w_out, m_w_up, m_conv_w, m_conv_b, m_w_down, v_w_ada, v_b_ada, v_g_pre_mix, v_g_post_mix, v_g_pre_ffn, v_g_post_ffn, v_w_in, v_w_pool, v_b_pool, v_pool_scale, v_w_out, v_w_up, v_conv_w, v_conv_b, v_w_down):
    s_len, d = x.shape[1], x.shape[2]
    d_ff = w_down.shape[1] * N_DEV
    me = _index(_place())
    xs, target = x[0], loss_target[0]

    ncol = w_ada.shape[2]
    b_cols = lax.dynamic_slice(b_ada, (0, me * ncol), (1, ncol))
    c_all, mod, taps_all, (w_in_t, w_out_t) = _entry_exchange(
        jnp.broadcast_to(c, (8, d)), w_ada[0], b_cols, _pad_rows(conv_w[0], 8),
        [w_in[0].T.astype(BF16), w_out[0].T.astype(BF16)])
    c_all = c_all[:, 0, :]
    conv_w_all = jnp.transpose(taps_all[:, :3, :], (1, 0, 2)).reshape(3, d_ff)
    sh_m, sc_m, gt_m, sh_f, sc_f, gt_f = [mod[:, 0, :].reshape(1, -1)[:, k * d:(k + 1) * d] for k in range(6)]

    rope = _rope_tables(positions[0])
    w_blk = jnp.zeros((256, 256), F32)
    for gi in range(4):
        w_blk = lax.dynamic_update_slice(w_blk, w_pool[0, gi], (gi * HEAD_DIM, gi * HEAD_DIM))
    w_blk_b = w_blk.astype(BF16)
    b_pool_r, pool_scale_r = b_pool.reshape(1, 256), pool_scale.reshape(1, 256)

    up_sh, down_sh = w_up[0].T.astype(BF16), w_down[0].astype(BF16)
    w_in_t, conv_w_all, up_sh, down_sh = lax.optimization_barrier((w_in_t, conv_w_all, up_sh, down_sh))
    lands = [_landing_zone("gather", s, me, "land_" + nm) for s, nm in ((up_sh, "w_up"), (down_sh, "w_down"))]
    w_send, w_recv, w_src, w_land, w_token = _exchange_start("gather_ici", [up_sh, down_sh], lands, "ffn_weights_ici_start")
    relay = []

    def relay_ffn(after):
        _, blocks = _exchange_wait("gather_ici", w_send, w_recv, w_src, w_land, after, "ffn_weights_ici_wait")
        relay.extend(_exchange_start("gather_d2d", [], blocks, "ffn_weights_d2d_start"))
        return relay[4]

    def fetch_ffn(after):
        return _exchange_wait("gather_d2d", relay[0], relay[1], [], relay[3], after, "ffn_weights_d2d_wait")[1]

    flights = {}

    def send_grads(tag, slabs, whole):
        lands = [_landing_zone("scatter", g, me, f"land_{tag}_{k}") for k, g in enumerate(slabs)]
        lands += [lax.empty((N_DEV,) + a.shape, F32) for a in whole]
        modes = ("scatter",) * len(slabs) + ("allgather",) * len(whole)
        flights[tag] = (modes, *_exchange_start(modes, slabs + whole, lands, f"grads_{tag}_start"))
        return flights[tag][5]

    def arrived(tag, after):
        return _exchange_wait(*flights[tag][:5], after, f"grads_{tag}_wait")

    _, grad_x, *_, sums_in, _, _, _ = _sequence_step(
        xs, target, rope, (sh_m, sc_m, gt_m, sh_f, sc_f, gt_f), (g_pre_mix, g_post_mix, g_pre_ffn, g_post_ffn),
        w_in_t, w_out_t, relay_ffn, fetch_ffn, send_grads, w_blk_b, b_pool_r, pool_scale_r, conv_w_all, conv_b,
        w_token)

    send_grads("last", [], [sums_in])

    _, (parts_up, parts_down) = arrived("ffn", flights["last"][5])
    new_up = _sum_adam(parts_up, w_up[0].T, m_w_up[0].T, v_w_up[0].T, "adam_w_up", 352)
    new_down = _sum_adam(parts_down, w_down[0], m_w_down[0], v_w_down[0], "adam_w_down", 176)
    (_, *small), (parts_out, *gathered) = arrived("out", new_down[0])
    new_out = _sum_adam(parts_out, w_out[0], m_w_out[0], v_w_out[0], "adam_w_out", 128)
    _, (parts_in,) = arrived("in", new_out[0])
    new_in = _sum_adam(parts_in, w_in[0].T, m_w_in[0].T, v_w_in[0].T, "adam_w_in", 160)
    big = {"w_up": [a.T for a in new_up], "w_down": new_down, "w_out": new_out, "w_in": [a.T for a in new_in]}

    rep_w = [b_ada, g_pre_mix, g_post_mix, g_pre_ffn, g_post_ffn, w_pool, b_pool, pool_scale, conv_b]
    rep_m = [m_b_ada, m_g_pre_mix, m_g_post_mix, m_g_pre_ffn, m_g_post_ffn, m_w_pool, m_b_pool, m_pool_scale, m_conv_b]
    rep_v = [v_b_ada, v_g_pre_mix, v_g_post_mix, v_g_pre_ffn, v_g_post_ffn, v_w_pool, v_b_pool, v_pool_scale, v_conv_b]
    mine_last, got_last = arrived("last", new_in[0])
    small, gathered = [*mine_last, *small], [*got_last, *gathered]
    totals, dmod_all = _small_sum(small, gathered)
    dconv_tot, loss_tot = totals[5], totals[6]
    rep_out = _small_adam(totals, rep_w, rep_m, rep_v)
    g_rep, d_rep, nm_rep, nv_rep = (rep_out[k::4] for k in range(4))

    fcol = d_ff // N_DEV
    g_cw = lax.dynamic_slice(dconv_tot, (0, me * fcol), (3, fcol))
    d_cw, nm_cw, nv_cw = _adam(conv_w[0], g_cw, m_conv_w[0], v_conv_w[0], "adam_conv_w", 3)

    dmod_cols = lax.dynamic_slice(dmod_all, (0, me * ncol), (N_DEV, ncol))
    g_ada, d_ada, nm_ada, nv_ada = _ada_grad_adam(c_all, dmod_cols, w_ada[0], m_w_ada[0], v_w_ada[0], 256)

    loss = loss_tot[0, 0]

    def group(k):
        rep = (g_rep, d_rep, nm_rep, nv_rep)[k]
        ada = (g_ada, d_ada, nm_ada, nv_ada)[k][None]
        cw = (g_cw, d_cw, nm_cw, nv_cw)[k][None]
        return [ada, rep[0], rep[1], rep[2], rep[3], rep[4], big["w_in"][k][None], rep[5], rep[6], rep[7],
                big["w_out"][k][None], big["w_up"][k][None], cw, rep[8], big["w_down"][k][None]]

    return (loss, grad_x[None], *group(0), *group(1), *group(2), *group(3))
```

```python
import functools
import math

import jax
import jax.numpy as jnp
from jax import lax
from jax.experimental import pallas as pl
from jax.experimental.pallas import tpu as pltpu

F32 = jnp.float32
BF16 = jnp.bfloat16
MESH = pl.DeviceIdType.MESH

N_DEV = 8
HEAD_DIM = 64
ROT_HALF = 8
ROPE_THETA = 500000.0
POOL_WINDOWS = (2, 4, 8, 16)
DILATIONS = (1, 4, 16)
BLOCK = 128
NORM_EPS = 1e-6
HALO = 16
MASKED = -1e30
ATTN_FWD_UNROLL = 8
ATTN_BWD_UNROLL = 8

ADAM_LR = 0.001
ADAM_B1 = 0.9
ADAM_B2 = 0.999
ADAM_EPS = 1e-08
ADAM_WD = 0.01
ADAM_STEP = 10

V7X_VMEM_LIMIT = 56 * 1024 * 1024
LANES = 128

NT = (((1,), (1,)), ((), ()))
NN = (((1,), (0,)), ((), ()))
TN = (((0,), (0,)), ((), ()))


def _dot(a, b, dims):
    return lax.dot_general(a, b, dims, preferred_element_type=F32)


def _params(sem=None, vmem=V7X_VMEM_LIMIT):
    if sem is None:
        return pltpu.CompilerParams(vmem_limit_bytes=vmem)
    return pltpu.CompilerParams(dimension_semantics=sem, vmem_limit_bytes=vmem)


def _rstd(v):
    return lax.rsqrt(jnp.mean(v * v, axis=-1, keepdims=True) + NORM_EPS)


def _norm_bwd(dn, n, rstd):
    return rstd * (dn - n * jnp.mean(dn * n, axis=-1, keepdims=True))


def _rope_lanes(cs_ref, spread_ref):
    return [lax.dot_general(cs_ref[...], spread_ref[k], NN, preferred_element_type=F32, precision=lax.Precision.HIGHEST)
            for k in range(3)]


def _rope_fwd(p, lanes):
    return p * lanes[0] + pltpu.roll(p, LANES - ROT_HALF, 1) * lanes[1] + pltpu.roll(p, ROT_HALF, 1) * lanes[2]


def _rope_bwd(dp, lanes):
    return dp * lanes[0] + pltpu.roll(dp * lanes[1], ROT_HALF, 1) + pltpu.roll(dp * lanes[2], LANES - ROT_HALF, 1)


def _gelu_parts(v):
    k2 = 2.0 * math.sqrt(2.0 / math.pi)
    c = 0.044715
    v2 = v * v
    s = jax.nn.sigmoid(v * (k2 + (k2 * c) * v2))
    g = v * s
    dg = s + g * (1.0 - s) * (k2 + (3.0 * k2 * c) * v2)
    return g, dg


def _halo_before(i, tile):
    return jnp.maximum(i * (tile // HALO) - 1, 0)


def _premix_inproj(x, sh, sc, g, w_in_t, rope, after, tm):
    s_len, d = x.shape
    n_proj = w_in_t.shape[0]
    n_slab = (n_proj - 256) // LANES

    def body(x_ref, sh_ref, sc_ref, g_ref, w_ref, cs_ref, spread_ref, after_ref, h_ref, up_ref, qkv_ref):
        xv = x_ref[...]
        h = (xv * _rstd(xv) * g_ref[...]) * (1.0 + sc_ref[...]) + sh_ref[...]
        hb = h.astype(BF16)
        h_ref[...] = hb
        up_ref[...] = _dot(hb, w_ref[0:256, :], NT)
        lanes = _rope_lanes(cs_ref, spread_ref)
        for pair in range(n_slab // 2):
            p = _dot(hb, w_ref[256 + 256 * pair:512 + 256 * pair, :], NT)
            for half in range(2):
                ph = p[:, half * LANES:(half + 1) * LANES]
                if pair < 6:
                    ph = _rope_fwd(ph, lanes)
                if pair < 3:
                    ph = ph * (HEAD_DIM ** -0.5)
                qkv_ref[2 * pair + half] = ph

    vec = pl.BlockSpec((1, d), lambda i: (0, 0))
    return pl.pallas_call(
        body, name="premix_inproj", grid=(s_len // tm,),
        in_specs=[pl.BlockSpec((tm, d), lambda i: (i, 0)), vec, vec, vec,
                  pl.BlockSpec((n_proj, d), lambda i: (0, 0)),
                  pl.BlockSpec((tm, rope[0].shape[1]), lambda i: (i, 0)), pl.BlockSpec(rope[1].shape, lambda i: (0, 0, 0)),
                  pl.BlockSpec(memory_space=pl.ANY)],
        out_specs=[pl.BlockSpec((tm, d), lambda i: (i, 0)),
                   pl.BlockSpec((tm, 256), lambda i: (i, 0)),
                   pl.BlockSpec((n_slab, tm, LANES), lambda i: (0, i, 0))],
        out_shape=[jax.ShapeDtypeStruct((s_len, d), BF16),
                   jax.ShapeDtypeStruct((s_len, 256), F32),
                   jax.ShapeDtypeStruct((n_slab, s_len, LANES), F32)],
        compiler_params=_params(("arbitrary",)),
    )(x, sh, sc, g, w_in_t, *rope, after)


def _block_rows(n, r, dil):
    start = n * (BLOCK * dil) + r
    if dil == 1:
        return pl.ds(pl.multiple_of(start, BLOCK), BLOCK)
    return pl.ds(start, BLOCK, stride=dil)


def _band_mask(n):
    ri = lax.broadcasted_iota(jnp.int32, (BLOCK, 2 * BLOCK), 0)
    cj = lax.broadcasted_iota(jnp.int32, (BLOCK, 2 * BLOCK), 1)
    cur = (cj >= BLOCK) & (cj - BLOCK <= ri)
    prev = (cj < BLOCK) & (cj >= ri) & (n > 0)
    return cur | prev


def _attn_fwd(qkv):
    s_len = qkv.shape[1]
    n_g = len(DILATIONS)

    def body(q_ref, k_ref, v_ref, o_ref, lse_ref):
        lane = lax.broadcasted_iota(jnp.int32, (BLOCK, LANES), 1)
        first = lane < HEAD_DIM

        def group(dil):
            nb = s_len // (BLOCK * dil)

            def block(t, carry):
                r, n = t // nb, t % nb
                cur = _block_rows(n, r, dil)
                prev = _block_rows(jnp.maximum(n - 1, 0), r, dil)
                q = q_ref[0, cur, :]
                kcat = jnp.concatenate([k_ref[0, prev, :], k_ref[0, cur, :]], axis=0).astype(BF16)
                vcat = jnp.concatenate([v_ref[0, prev, :], v_ref[0, cur, :]], axis=0).astype(BF16)
                valid = _band_mask(n)
                q2 = jnp.concatenate([jnp.where(first, q, 0.0), jnp.where(first, 0.0, q)], axis=0).astype(BF16)
                s = jnp.where(jnp.concatenate([valid, valid], axis=0), _dot(q2, kcat, NT), MASKED)
                m = jnp.max(s, axis=-1, keepdims=True)
                p = jnp.exp(s - m)
                den = jnp.sum(p, axis=-1, keepdims=True)
                o2 = _dot(p.astype(BF16), vcat, NN) / den
                lse2 = m + jnp.log(den)
                o_ref[0, 0, cur, :] = jnp.where(first, o2[:BLOCK], o2[BLOCK:])
                lse_ref[0, 0, cur, :] = jnp.where(first, lse2[:BLOCK], lse2[BLOCK:])
                return carry

            lax.fori_loop(0, nb * dil, block, 0, unroll=ATTN_FWD_UNROLL)

        for gi, dil in enumerate(DILATIONS):
            pl.when(pl.program_id(0) == gi)(functools.partial(group, dil))

    def slab(base):
        return pl.BlockSpec((1, s_len, LANES), lambda g, s: (base + 2 * g + s, 0, 0))

    out = pl.BlockSpec((1, 1, s_len, LANES), lambda g, s: (g, s, 0, 0))
    shape = jax.ShapeDtypeStruct((n_g, 2, s_len, LANES), F32)
    return pl.pallas_call(
        body, name="attn_fwd", grid=(n_g, 2),
        in_specs=[slab(0), slab(6), slab(12)], out_specs=[out, out], out_shape=[shape, shape],
        compiler_params=_params(("arbitrary", "arbitrary")),
    )(qkv, qkv, qkv)


def _pool_mixed(u, halo, i, tm):
    ue = jnp.concatenate([halo, u], axis=0)
    s2 = ue + pltpu.roll(ue, 1, 0)
    s4 = s2 + pltpu.roll(s2, 2, 0)
    s8 = s4 + pltpu.roll(s4, 4, 0)
    s16 = s8 + pltpu.roll(s8, 8, 0)
    grp = lax.broadcasted_iota(jnp.int32, (tm, 256), 1) // HEAD_DIM
    pick = lambda a, b, c, e: jnp.where(grp == 0, a, jnp.where(grp == 1, b, jnp.where(grp == 2, c, e)))
    win_sum = pick(s2[HALO:], s4[HALO:], s8[HALO:], s16[HALO:])
    pos = (i * tm + lax.broadcasted_iota(jnp.int32, (tm, 256), 0)).astype(F32)
    count = jnp.minimum(pos + 1.0, pick(*[float(w) for w in POOL_WINDOWS]))
    return win_sum / count - u, count


def _mix_out(x, u_pool, o_g, lse_g, w_blk, b_pool, pool_scale, w_out_t, gt_m, g_post_mix, g_pre_ffn, sc_f, sh_f, tm):
    s_len, d = x.shape

    def body(x_ref, u_ref, uh_ref, o_ref, l_ref, wb_ref, bp_ref, ps_ref, wo_ref,
             gt_ref, g1_ref, g2_ref, sc_ref, sh_ref,
             x1_ref, y1_ref, h2_ref, cat_ref, attn_ref, lall_ref):
        (o0, o1, o2), (l0, l1, l2) = (o_ref.at[g] for g in range(3)), (l_ref.at[g] for g in range(3))
        i = pl.program_id(0)
        u = u_ref[...]
        halo = uh_ref[...] * (i > 0).astype(F32)
        mixed, _ = _pool_mixed(u, halo, i, tm)
        y = _dot(mixed.astype(BF16), wb_ref[...], NN) + bp_ref[...]
        pool = y * ps_ref[...]
        attn = []
        for s in range(2):
            la, lb, lc = l0[s], l1[s], l2[s]
            mx = jnp.maximum(jnp.maximum(la, lb), lc)
            ea, eb, ec = jnp.exp(la - mx), jnp.exp(lb - mx), jnp.exp(lc - mx)
            den = ea + eb + ec
            lall_ref[s] = mx + jnp.log(den)
            attn.append((ea / den) * o0[s] + (eb / den) * o1[s] + (ec / den) * o2[s])
        attn = jnp.concatenate(attn, axis=1)
        attn_ref[...] = attn
        cat = jnp.concatenate([pool, attn], axis=1).astype(BF16)
        cat_ref[...] = cat
        y1 = _dot(cat, wo_ref[...], NT)
        y1_ref[...] = y1.astype(BF16)
        x1 = x_ref[...] + gt_ref[...] * (y1 * _rstd(y1) * g1_ref[...])
        x1_ref[...] = x1
        h2 = (x1 * _rstd(x1) * g2_ref[...]) * (1.0 + sc_ref[...]) + sh_ref[...]
        h2_ref[...] = h2.astype(BF16)

    tile = lambda w: pl.BlockSpec((tm, w), lambda i: (i, 0))
    slab = pl.BlockSpec((2, tm, LANES), lambda i: (0, i, 0))
    groups = pl.BlockSpec((len(DILATIONS), 2, tm, LANES), lambda i: (0, 0, i, 0))
    const = lambda a: pl.BlockSpec(a.shape, lambda i: (0,) * a.ndim)
    return pl.pallas_call(
        body, name="mix_out", grid=(s_len // tm,),
        in_specs=[tile(d), tile(256), pl.BlockSpec((HALO, 256), lambda i: (_halo_before(i, tm), 0)),
                  groups, groups,
                  const(w_blk), const(b_pool), const(pool_scale), const(w_out_t),
                  const(gt_m), const(g_post_mix), const(g_pre_ffn), const(sc_f), const(sh_f)],
        out_specs=[tile(d), tile(d), tile(d), tile(512), tile(256), slab],
        out_shape=[jax.ShapeDtypeStruct((s_len, d), F32), jax.ShapeDtypeStruct((s_len, d), BF16),
                   jax.ShapeDtypeStruct((s_len, d), BF16), jax.ShapeDtypeStruct((s_len, 512), BF16),
                   jax.ShapeDtypeStruct((s_len, 256), F32), jax.ShapeDtypeStruct((2, s_len, LANES), F32)],
        compiler_params=_params(("arbitrary",)),
    )(x, u_pool, u_pool, o_g, lse_g, w_blk, b_pool, pool_scale, w_out_t, gt_m, g_post_mix, g_pre_ffn, sc_f, sh_f)


def _conv_gate(gate_ext, cw, cb):
    gc = gate_ext * cw[2:3, :] + pltpu.roll(gate_ext, 1, 0) * cw[1:2, :] + pltpu.roll(gate_ext, 2, 0) * cw[0:1, :]
    return gc[HALO:] + cb


def _ffn_fwd_loss(h2, x1, target, w_up_t, w_down, conv_w, conv_b, gt_f, g_post_ffn, tm, ck):
    s_len, d = x1.shape
    d_ff = w_down.shape[0]
    n_t, n_c = s_len // tm, d_ff // ck

    def body(h_ref, hh_ref, x1_ref, tgt_ref, wg_ref, wv_ref, wd_ref, cw_ref, cb_ref, gt_ref, g_ref,
             gate_ref, a_ref, act_ref, vd_ref, dy2_ref, dout_ref, sums_ref, loss_ref, acc_ref):
        i = pl.program_id(0)

        @pl.when(i == 0)
        def _():
            sums_ref[...] = jnp.zeros_like(sums_ref)
            loss_ref[...] = jnp.zeros_like(loss_ref)
            acc_ref[...] = jnp.zeros_like(acc_ref)

        def finish(live):
            y2 = acc_ref[...]
            rstd = _rstd(y2)
            n = y2 * rstd
            rn = n * g_ref[...]
            err = x1_ref[...] + gt_ref[...] * rn - tgt_ref[...]
            keep = lambda v: jnp.where(live, v, 0.0)
            loss_ref[...] += keep(0.5 * jnp.sum(jnp.mean(err * err, axis=-1, keepdims=True), axis=0, keepdims=True))
            dout = err * (1.0 / d)
            dout_ref[...] = dout
            drn = dout * gt_ref[...]
            sums_ref[0:1, :] += keep(jnp.sum(dout * rn, axis=0, keepdims=True))
            sums_ref[1:2, :] += keep(jnp.sum(drn * n, axis=0, keepdims=True))
            dy2_ref[...] = _norm_bwd(drn * g_ref[...], n, rstd).astype(BF16)

        @pl.when(i < n_t)
        def _():
            h = h_ref[...]
            h_ext = jnp.concatenate([hh_ref[...], h], axis=0)
            row = lax.broadcasted_iota(jnp.int32, (tm + HALO, ck), 0)
            no_halo = (row < HALO) & (i == 0)

            def up(c):
                cs = slice(c * ck, (c + 1) * ck)
                return jnp.where(no_halo, 0.0, _dot(h_ext, wg_ref[cs, :], NT)), _dot(h, wv_ref[cs, :], NT)

            part = None
            nxt = up(0)
            finish(i > 0)
            for c in range(n_c):
                cs = slice(c * ck, (c + 1) * ck)
                gate_ext, val = nxt
                if c + 1 < n_c:
                    nxt = up(c + 1)
                act, dact = _gelu_parts(_conv_gate(gate_ext, cw_ref[:, cs], cb_ref[:, cs]))
                a = (act * val).astype(BF16)
                gate_ref[:, cs] = gate_ext[HALO:].astype(BF16)
                a_ref[:, cs] = a
                act_ref[:, cs] = act.astype(BF16)
                vd_ref[:, cs] = (val * dact).astype(BF16)
                p = _dot(a, wd_ref[cs, :], NN)
                part = p if part is None else part + p
            acc_ref[...] = part

        @pl.when(i == n_t)
        def _():
            finish(True)

    this = lambda i: jnp.minimum(i, n_t - 1)
    before = lambda i: jnp.maximum(i - 1, 0)
    tok = lambda w, at: pl.BlockSpec((tm, w), lambda i: (at(i), 0))
    vec = pl.BlockSpec((1, d), lambda i: (0, 0))
    once = lambda shape, imap: pl.BlockSpec(shape, imap, pipeline_mode=pl.Buffered(1))
    return pl.pallas_call(
        body, name="ffn_fwd_loss", grid=(n_t + 1,),
        in_specs=[tok(d, this), pl.BlockSpec((HALO, d), lambda i: (_halo_before(this(i), tm), 0)),
                  tok(d, before), tok(d, before),
                  once((d_ff, d), lambda i: (0, 0)), once((d_ff, d), lambda i: (1, 0)), once((d_ff, d), lambda i: (0, 0)),
                  pl.BlockSpec((3, d_ff), lambda i: (0, 0)), pl.BlockSpec((1, d_ff), lambda i: (0, 0)), vec, vec],
        out_specs=[tok(d_ff, this)] * 4 + [tok(d, before), tok(d, before), pl.BlockSpec((8, d), lambda i: (0, 0)),
                                          pl.BlockSpec((8, LANES), lambda i: (0, 0))],
        out_shape=[jax.ShapeDtypeStruct((s_len, d_ff), BF16)] * 4
        + [jax.ShapeDtypeStruct((s_len, d), BF16), jax.ShapeDtypeStruct((s_len, d), F32),
           jax.ShapeDtypeStruct((8, d), F32), jax.ShapeDtypeStruct((8, LANES), F32)],
        scratch_shapes=[pltpu.VMEM((tm, d), F32)],
        compiler_params=_params(("arbitrary",)),
    )(h2, h2, x1, target, w_up_t, w_up_t, w_down, conv_w, conv_b, gt_f, g_post_ffn)


def _ffn_bwd_act(dy2, gate, a, act, vd, w_down, tm, tf, ck):
    s_len, d = dy2.shape
    d_ff = w_down.shape[0]
    n_t = s_len // tm
    chunks = [slice(lo, min(lo + ck, tf)) for lo in range(0, tf, ck)]

    def body(dy_ref, g_ref, gh_ref, a_ref, act_ref, vd_ref, wd_ref, dgc_ref, dval_ref, dwd_ref, dconv_ref, acc_ref):
        i = pl.program_id(1)

        @pl.when(i == 0)
        def _():
            acc_ref[...] = jnp.zeros_like(acc_ref)
            dconv_ref[...] = jnp.zeros_like(dconv_ref)

        dy = dy_ref[...]

        def down(cs):
            return _dot(dy, wd_ref[cs, :], NT)

        nxt = down(chunks[0])
        for c, cs in enumerate(chunks):
            width = cs.stop - cs.start
            da = nxt
            if c + 1 < len(chunks):
                nxt = down(chunks[c + 1])
            acc_ref[cs, :] += _dot(a_ref[:, cs], dy, TN)
            row = lax.broadcasted_iota(jnp.int32, (tm + HALO, width), 0)
            gate_ext = jnp.where((row < HALO) & (i == 0), 0.0,
                                 jnp.concatenate([gh_ref[:, cs], g_ref[:, cs]], axis=0).astype(F32))
            dgc = da * vd_ref[:, cs].astype(F32)
            dgc_ref[:, cs] = dgc.astype(BF16)
            dval_ref[:, cs] = (da * act_ref[:, cs].astype(F32)).astype(BF16)
            rows = [jnp.sum(dgc * pltpu.roll(gate_ext, 2 - k, 0)[HALO:], axis=0, keepdims=True) for k in range(2)]
            rows += [jnp.sum(dgc * gate_ext[HALO:], axis=0, keepdims=True), jnp.sum(dgc, axis=0, keepdims=True),
                     jnp.zeros((4, width), F32)]
            dconv_ref[:, cs] += jnp.concatenate(rows, axis=0)

        @pl.when(i == n_t - 1)
        def _():
            dwd_ref[...] = acc_ref[...].astype(BF16)

    tokf = pl.BlockSpec((tm, tf), lambda j, i: (i, j))
    return pl.pallas_call(
        body, name="ffn_bwd_act", grid=(d_ff // tf, n_t),
        in_specs=[pl.BlockSpec((tm, d), lambda j, i: (i, 0)), tokf,
                  pl.BlockSpec((HALO, tf), lambda j, i: (_halo_before(i, tm), j)), tokf, tokf, tokf,
                  pl.BlockSpec((tf, d), lambda j, i: (j, 0))],
        out_specs=[tokf, tokf, pl.BlockSpec((tf, d), lambda j, i: (j, 0)), pl.BlockSpec((8, tf), lambda j, i: (0, j))],
        out_shape=[jax.ShapeDtypeStruct((s_len, d_ff), BF16), jax.ShapeDtypeStruct((s_len, d_ff), BF16),
                   jax.ShapeDtypeStruct((d_ff, d), BF16), jax.ShapeDtypeStruct((8, d_ff), F32)],
        scratch_shapes=[pltpu.VMEM((tf, d), F32)],
        compiler_params=_params(("arbitrary", "arbitrary")),
    )(dy2, gate, gate, a, act, vd, w_down)


def _ffn_bwd_up(dgc, dval, w_up_t, conv_w, tm):
    s_len, d_ff = dgc.shape
    d = w_up_t.shape[1]
    n_t = s_len // tm

    def body(dg_ref, dgn_ref, dv_ref, cw_ref, w_ref, dup_ref, dh_ref):
        i = pl.program_id(0)
        nxt = dgn_ref[...].astype(F32) * (i < n_t - 1).astype(F32)
        ext = jnp.concatenate([dg_ref[...].astype(F32), nxt], axis=0)
        rows = tm + HALO
        dgate = (ext * cw_ref[2:3, :] + pltpu.roll(ext, rows - 1, 0) * cw_ref[1:2, :]
                 + pltpu.roll(ext, rows - 2, 0) * cw_ref[0:1, :])[:tm]
        dup = jnp.concatenate([dgate.astype(BF16), dv_ref[...]], axis=1)
        dup_ref[...] = dup
        dh_ref[...] = _dot(dup, w_ref[...], NN).astype(BF16)

    tokf = pl.BlockSpec((tm, d_ff), lambda i: (i, 0))
    return pl.pallas_call(
        body, name="ffn_bwd_up", grid=(n_t,),
        in_specs=[tokf, pl.BlockSpec((HALO, d_ff), lambda i: (jnp.minimum((i + 1) * (tm // HALO), s_len // HALO - 1), 0)),
                  tokf, pl.BlockSpec((3, d_ff), lambda i: (0, 0)), pl.BlockSpec((2 * d_ff, d), lambda i: (0, 0))],
        out_specs=[pl.BlockSpec((tm, 2 * d_ff), lambda i: (i, 0)), pl.BlockSpec((tm, d), lambda i: (i, 0))],
        out_shape=[jax.ShapeDtypeStruct((s_len, 2 * d_ff), BF16), jax.ShapeDtypeStruct((s_len, d), BF16)],
        compiler_params=_params(("arbitrary",)),
    )(dgc, dgc, dval, conv_w, w_up_t)


def _mix_bwd(dh2, dout, x1, y1, cat, attn, w_out_t, sc_f, g_pre_ffn, gt_m, g_post_mix, after, tm):
    s_len, d = x1.shape
    n_t = s_len // tm

    def body(dh_ref, do_ref, x1_ref, y1_ref, cat_ref, at_ref, wo_ref, sc_ref, g2_ref, gt_ref, g1_ref, after_ref,
             dx1_ref, dpool_ref, dattn_ref, delta_ref, dwo_ref, sums_ref, acc_ref):
        i = pl.program_id(0)
        dh = dh_ref[...].astype(F32)
        x1 = x1_ref[...]
        r2 = _rstd(x1)
        n2 = x1 * r2
        ng = n2 * g2_ref[...]
        dng = dh * (1.0 + sc_ref[...])
        dx1 = do_ref[...] + _norm_bwd(dng * g2_ref[...], n2, r2)
        dx1_ref[...] = dx1
        y1 = y1_ref[...].astype(F32)
        r1 = _rstd(y1)
        n1 = y1 * r1
        drn = dx1 * gt_ref[...]
        dy1 = _norm_bwd(drn * g1_ref[...], n1, r1).astype(BF16)
        dcat = _dot(dy1, wo_ref[...], NN)
        dpool_ref[...] = dcat[:, 0:256]
        lane = lax.broadcasted_iota(jnp.int32, (tm, LANES), 1)
        first = lane < HEAD_DIM
        for s in range(2):
            da = dcat[:, 256 + s * LANES:256 + (s + 1) * LANES]
            dattn_ref[s] = da
            prod = da * at_ref[:, s * LANES:(s + 1) * LANES]
            tot = jnp.sum(prod, axis=-1, keepdims=True)
            lo = jnp.sum(jnp.where(first, prod, 0.0), axis=-1, keepdims=True)
            delta_ref[s] = jnp.where(first, lo, tot - lo)
        dwo = _dot(dy1, cat_ref[...], TN)
        sums = jnp.concatenate(
            [jnp.sum(dh, axis=0, keepdims=True), jnp.sum(dh * ng, axis=0, keepdims=True),
             jnp.sum(dng * n2, axis=0, keepdims=True), jnp.sum(dx1 * (n1 * g1_ref[...]), axis=0, keepdims=True),
             jnp.sum(drn * n1, axis=0, keepdims=True), jnp.zeros((3, d), F32)], axis=0)

        @pl.when(i == 0)
        def _():
            acc_ref[...] = dwo
            sums_ref[...] = sums

        @pl.when(i > 0)
        def _():
            acc_ref[...] += dwo
            sums_ref[...] += sums

        @pl.when(i == n_t - 1)
        def _():
            dwo_ref[...] = acc_ref[...].astype(BF16)

    tile = lambda w: pl.BlockSpec((tm, w), lambda i: (i, 0))
    slab = pl.BlockSpec((2, tm, LANES), lambda i: (0, i, 0))
    vec = pl.BlockSpec((1, d), lambda i: (0, 0))
    return pl.pallas_call(
        body, name="mix_bwd", grid=(n_t,),
        in_specs=[tile(d), tile(d), tile(d), tile(d), tile(512), tile(256),
                  pl.BlockSpec((d, 512), lambda i: (0, 0)), vec, vec, vec, vec, pl.BlockSpec(memory_space=pl.ANY)],
        out_specs=[tile(d), tile(256), slab, slab, pl.BlockSpec((d, 512), lambda i: (0, 0)),
                   pl.BlockSpec((8, d), lambda i: (0, 0))],
        out_shape=[jax.ShapeDtypeStruct((s_len, d), F32), jax.ShapeDtypeStruct((s_len, 256), F32),
                   jax.ShapeDtypeStruct((2, s_len, LANES), F32), jax.ShapeDtypeStruct((2, s_len, LANES), F32),
                   jax.ShapeDtypeStruct((d, 512), BF16), jax.ShapeDtypeStruct((8, d), F32)],
        scratch_shapes=[pltpu.VMEM((d, 512), F32)],
        compiler_params=_params(("arbitrary",)),
    )(dh2, dout, x1, y1, cat, attn, w_out_t, sc_f, g_pre_ffn, gt_m, g_post_mix, after)


def _pool_bwd(dpool, u_pool, w_blk, b_pool, pool_scale, tm):
    s_len = dpool.shape[0]
    n_t = s_len // tm

    def body(dp_ref, dpn_ref, u_ref, uh_ref, wb_ref, bp_ref, ps_ref, du_ref, dwp_ref, sums_ref, acc_ref):
        i = pl.program_id(0)
        u = u_ref[...]
        mixed, _ = _pool_mixed(u, uh_ref[...] * (i > 0).astype(F32), i, tm)
        mixed_b = mixed.astype(BF16)
        y = _dot(mixed_b, wb_ref[...], NN) + bp_ref[...]
        dp = dp_ref[...]
        dy = dp * ps_ref[...]
        dwb = _dot(mixed_b, dy.astype(BF16), TN)
        sums = jnp.concatenate([jnp.sum(dy, axis=0, keepdims=True), jnp.sum(dp * y, axis=0, keepdims=True),
                                jnp.zeros((6, 256), F32)], axis=0)
        dp_ext = jnp.concatenate([dp, dpn_ref[...] * (i < n_t - 1).astype(F32)], axis=0)
        dmix = _dot((dp_ext * ps_ref[...]).astype(BF16), wb_ref[...], NT)
        rows = tm + HALO
        grp = lax.broadcasted_iota(jnp.int32, (rows, 256), 1) // HEAD_DIM
        pick = lambda a, b, c, e: jnp.where(grp == 0, a, jnp.where(grp == 1, b, jnp.where(grp == 2, c, e)))
        pos = (i * tm + lax.broadcasted_iota(jnp.int32, (rows, 256), 0)).astype(F32)
        z = dmix / jnp.minimum(pos + 1.0, pick(*[float(w) for w in POOL_WINDOWS]))
        f2 = z + pltpu.roll(z, rows - 1, 0)
        f4 = f2 + pltpu.roll(f2, rows - 2, 0)
        f8 = f4 + pltpu.roll(f4, rows - 4, 0)
        f16 = f8 + pltpu.roll(f8, rows - 8, 0)
        du_ref[...] = (pick(f2, f4, f8, f16) - dmix)[:tm]

        @pl.when(i == 0)
        def _():
            acc_ref[...] = dwb
            sums_ref[...] = sums

        @pl.when(i > 0)
        def _():
            acc_ref[...] += dwb
            sums_ref[...] += sums

        @pl.when(i == n_t - 1)
        def _():
            full = acc_ref[...]
            for gi in range(len(POOL_WINDOWS)):
                lo = gi * HEAD_DIM
                dwp_ref[gi] = full[lo:lo + HEAD_DIM, lo:lo + HEAD_DIM]

    n_g = len(POOL_WINDOWS)
    tile = pl.BlockSpec((tm, 256), lambda i: (i, 0))
    const = lambda a: pl.BlockSpec(a.shape, lambda i: (0,) * a.ndim)
    return pl.pallas_call(
        body, name="pool_bwd", grid=(n_t,),
        in_specs=[tile, pl.BlockSpec((HALO, 256), lambda i: (jnp.minimum((i + 1) * (tm // HALO), s_len // HALO - 1), 0)),
                  tile, pl.BlockSpec((HALO, 256), lambda i: (_halo_before(i, tm), 0)),
                  const(w_blk), const(b_pool), const(pool_scale)],
        out_specs=[tile, pl.BlockSpec((n_g, HEAD_DIM, HEAD_DIM), lambda i: (0, 0, 0)), pl.BlockSpec((8, 256), lambda i: (0, 0))],
        out_shape=[jax.ShapeDtypeStruct((s_len, 256), F32), jax.ShapeDtypeStruct((n_g, HEAD_DIM, HEAD_DIM), F32),
                   jax.ShapeDtypeStruct((8, 256), F32)],
        scratch_shapes=[pltpu.VMEM((256, 256), F32)],
        compiler_params=_params(("arbitrary",)),
    )(dpool, dpool, u_pool, u_pool, w_blk, b_pool, pool_scale)


def _attn_bwd(qkv, dattn, lse_all, delta, after):
    s_len = qkv.shape[1]
    n_g = len(DILATIONS)

    def body(q_ref, k_ref, v_ref, do_ref, l_ref, dl_ref, after_ref, dq_ref, dk_ref, dv_ref):
        lane = lax.broadcasted_iota(jnp.int32, (BLOCK, LANES), 1)
        first = lane < HEAD_DIM

        def group(dil):
            nb = s_len // (BLOCK * dil)

            def block(t, carry):
                dk_part, dv_part = carry
                r, n = t // nb, t % nb
                cur = _block_rows(n, r, dil)
                prev = _block_rows(jnp.maximum(n - 1, 0), r, dil)
                q = q_ref[0, cur, :]
                do = do_ref[0, cur, :]
                lse = l_ref[0, cur, :]
                dlt = dl_ref[0, cur, :]
                kcat = jnp.concatenate([k_ref[0, prev, :], k_ref[0, cur, :]], axis=0).astype(BF16)
                vcat = jnp.concatenate([v_ref[0, prev, :], v_ref[0, cur, :]], axis=0).astype(BF16)
                valid = _band_mask(n)
                stack = lambda a: jnp.concatenate([jnp.where(first, a, 0.0), jnp.where(first, 0.0, a)], axis=0)
                rows2 = lambda a: jnp.concatenate([a[:, 0:1], a[:, HEAD_DIM:HEAD_DIM + 1]], axis=0)
                q2, do2 = stack(q).astype(BF16), stack(do).astype(BF16)
                valid2 = jnp.concatenate([valid, valid], axis=0)
                p = jnp.where(valid2, jnp.exp(_dot(q2, kcat, NT) - rows2(lse)), 0.0)
                ds = (p * (_dot(do2, vcat, NT) - rows2(dlt))).astype(BF16)
                dq2 = _dot(ds, kcat, NN)
                dq_ref[0, 0, cur, :] = jnp.where(first, dq2[:BLOCK], dq2[BLOCK:])
                dkc = _dot(ds, q2, TN)
                dvc = _dot(p.astype(BF16), do2, TN)
                dk_ref[0, 0, prev, :] = dk_part + dkc[:BLOCK]
                dv_ref[0, 0, prev, :] = dv_part + dvc[:BLOCK]
                dk_ref[0, 0, cur, :] = dkc[BLOCK:]
                dv_ref[0, 0, cur, :] = dvc[BLOCK:]
                return dkc[BLOCK:], dvc[BLOCK:]

            def blocks(tt, carry):
                for u in range(ATTN_BWD_UNROLL):
                    carry = block(tt * ATTN_BWD_UNROLL + u, carry)
                return carry

            zero = jnp.zeros((BLOCK, LANES), F32)
            lax.fori_loop(0, nb * dil // ATTN_BWD_UNROLL, blocks, (zero, zero))

        for gi, dil in enumerate(DILATIONS):
            pl.when(pl.program_id(1) == gi)(functools.partial(group, dil))

    def slab(base):
        return pl.BlockSpec((1, s_len, LANES), lambda s, g: (base + 2 * g + s, 0, 0))

    one = pl.BlockSpec((1, s_len, LANES), lambda s, g: (s, 0, 0))
    out = pl.BlockSpec((1, 1, s_len, LANES), lambda s, g: (g, s, 0, 0))
    shape = jax.ShapeDtypeStruct((n_g, 2, s_len, LANES), F32)
    return pl.pallas_call(
        body, name="attn_bwd", grid=(2, n_g),
        in_specs=[slab(0), slab(6), slab(12), one, one, one, pl.BlockSpec(memory_space=pl.ANY)],
        out_specs=[out, out, out], out_shape=[shape, shape, shape],
        compiler_params=_params(("arbitrary", "arbitrary")),
    )(qkv, qkv, qkv, dattn, lse_all, delta, after)


def _dproj_assemble(du, dqkv, rope, tm):
    s_len = du.shape[0]
    n_proj = 256 + 18 * LANES

    def body(du_ref, dq_ref, dk_ref, dv_ref, cs_ref, spread_ref, dproj_ref):
        dproj_ref[:, 0:256] = du_ref[...].astype(BF16)
        lanes = _rope_lanes(cs_ref, spread_ref)
        col = 256
        for kind, dref in enumerate((dq_ref, dk_ref, dv_ref)):
            for grp in range(3):
                for s in range(2):
                    piece = dref[grp, s]
                    if kind < 2:
                        piece = _rope_bwd(piece, lanes)
                    if kind == 0:
                        piece = piece * (HEAD_DIM ** -0.5)
                    dproj_ref[:, col:col + LANES] = piece.astype(BF16)
                    col += LANES

    groups = pl.BlockSpec((len(DILATIONS), 2, tm, LANES), lambda i: (0, 0, i, 0))
    return pl.pallas_call(
        body, name="dproj_assemble", grid=(s_len // tm,),
        in_specs=[pl.BlockSpec((tm, 256), lambda i: (i, 0))] + [groups] * 3
        + [pl.BlockSpec((tm, rope[0].shape[1]), lambda i: (i, 0)), pl.BlockSpec(rope[1].shape, lambda i: (0, 0, 0))],
        out_specs=pl.BlockSpec((tm, n_proj), lambda i: (i, 0)),
        out_shape=jax.ShapeDtypeStruct((s_len, n_proj), BF16),
        compiler_params=_params(("arbitrary",)),
    )(du, *dqkv, *rope)


def _inproj_bwd(dproj, w_in_t, x, dx1, sc_m, g_pre_mix, after, tm):
    s_len, d = x.shape
    n_proj = w_in_t.shape[0]
    n_t = s_len // tm

    def body(dproj_ref, w_ref, x_ref, dx1_ref, sc_ref, g_ref, after_ref, dx_ref, sums_ref):
        i = pl.program_id(0)
        halves = [slice(0, tm // 2), slice(tm // 2, tm)]
        dhs = [_dot(dproj_ref[rs, :], w_ref[...], NN) for rs in halves]
        sums = None
        for rs, dh in zip(halves, dhs):
            xv = x_ref[rs, :]
            r = _rstd(xv)
            n = xv * r
            dng = dh * (1.0 + sc_ref[...])
            dx_ref[rs, :] = dx1_ref[rs, :] + _norm_bwd(dng * g_ref[...], n, r)
            part = jnp.concatenate([jnp.sum(dh, axis=0, keepdims=True), jnp.sum(dh * (n * g_ref[...]), axis=0, keepdims=True),
                                    jnp.sum(dng * n, axis=0, keepdims=True), jnp.zeros((5, d), F32)], axis=0)
            sums = part if sums is None else sums + part

        @pl.when(i == 0)
        def _():
            sums_ref[...] = sums

        @pl.when(i > 0)
        def _():
            sums_ref[...] += sums

    tile = lambda w: pl.BlockSpec((tm, w), lambda i: (i, 0))
    vec = pl.BlockSpec((1, d), lambda i: (0, 0))
    return pl.pallas_call(
        body, name="inproj_bwd", grid=(n_t,),
        in_specs=[tile(n_proj), pl.BlockSpec((n_proj, d), lambda i: (0, 0)), tile(d), tile(d), vec, vec,
                  pl.BlockSpec(memory_space=pl.ANY)],
        out_specs=[tile(d), pl.BlockSpec((8, d), lambda i: (0, 0))],
        out_shape=[jax.ShapeDtypeStruct((s_len, d), F32), jax.ShapeDtypeStruct((8, d), F32)],
        compiler_params=_params(("arbitrary",)),
    )(dproj, w_in_t, x, dx1, sc_m, g_pre_mix, after)


def _wgrad(a, b, name, tk, tmm):
    s_len, m = a.shape
    n = b.shape[1]
    n_k = s_len // tk

    def body(a_ref, b_ref, o_ref, acc_ref):
        k = pl.program_id(1)
        part = _dot(a_ref[...], b_ref[...], TN)

        @pl.when(k == 0)
        def _():
            acc_ref[...] = part

        @pl.when(k > 0)
        def _():
            acc_ref[...] += part

        @pl.when(k == n_k - 1)
        def _():
            o_ref[...] = acc_ref[...].astype(BF16)

    return pl.pallas_call(
        body, name=name, grid=(m // tmm, n_k),
        in_specs=[pl.BlockSpec((tk, tmm), lambda j, k: (k, j)), pl.BlockSpec((tk, n), lambda j, k: (k, 0))],
        out_specs=pl.BlockSpec((tmm, n), lambda j, k: (j, 0)),
        out_shape=jax.ShapeDtypeStruct((m, n), BF16),
        scratch_shapes=[pltpu.VMEM((tmm, n), F32)],
        compiler_params=_params(("arbitrary", "arbitrary")),
    )(a, b)


def _place():
    return lax.axis_index("x"), lax.axis_index("y"), lax.axis_index("c")


def _peer(k):
    x, y, c = _place()
    bx, by, bc = (k >> 2) & 1, (k >> 1) & 1, k & 1
    return (x ^ bx if bx else x, y ^ by if by else y, c ^ bc if bc else c)


def _index(pos):
    return 4 * pos[0] + 2 * pos[1] + pos[2]


def _entry_exchange(c_rows, w_ada, b_ada, taps, shards):
    d = c_rows.shape[1]
    ncol = w_ada.shape[1]
    n_w = len(shards)

    def body(c_ref, w_ref, b_ref, t_ref, *rest):
        srcs = rest[:n_w]
        call_ref, mod_ref, tall_ref = rest[n_w:n_w + 3]
        outs = rest[n_w + 3:2 * n_w + 3]
        stage_ref, s_send, s_recv, w_send, w_recv, local_sems = rest[2 * n_w + 3:]
        x, y, c = _place()
        here, sibling = (x, y, c), (x, y, 1 - c)
        chips = [(1 - x, y), (x, 1 - y), (1 - x, 1 - y)]
        me = _index(here)

        def small(kind, src, dst, k):
            return pltpu.make_async_remote_copy(src_ref=src, dst_ref=dst, send_sem=s_send.at[kind, k - 1],
                                                recv_sem=s_recv.at[kind, k - 1], device_id=_peer(k), device_id_type=MESH)

        gather = lambda k: small(0, c_ref, call_ref.at[me], k)
        scatter = lambda k: small(1, stage_ref.at[_index(_peer(k))], mod_ref.at[me], k)
        gather_taps = lambda k: small(2, t_ref, tall_ref.at[me], k)

        def rows(w, pos):
            r = shards[w].shape[0]
            return outs[w].at[pl.ds(pl.multiple_of(_index(pos) * r, 16), r), :]

        def block(k, w, pos, to, own=False):
            return pltpu.make_async_remote_copy(
                src_ref=srcs[w] if own else rows(w, pos), dst_ref=rows(w, pos),
                send_sem=w_send.at[k, w], recv_sem=w_recv.at[k, w], device_id=to, device_id_type=MESH)

        call_ref[me] = c_ref[...]
        tall_ref[me] = t_ref[...]
        for k in range(1, N_DEV):
            gather(k).start()
        for k in range(1, N_DEV):
            gather_taps(k).start()
        mine = [pltpu.make_async_copy(srcs[w], rows(w, here), local_sems.at[w]) for w in range(n_w)]
        for cp in mine:
            cp.start()
        first = [block(0, w, here, sibling, own=True) for w in range(n_w)]
        first += [block(1 + j, w, here, (*chip, c), own=True) for j, chip in enumerate(chips) for w in range(n_w)]
        for cp in first:
            cp.start()

        for k in range(1, N_DEV):
            gather(k).wait_recv()
        cv = jnp.concatenate([call_ref[b, 0:1, :] for b in range(N_DEV)], axis=0)
        act = cv * jax.nn.sigmoid(cv)
        mod = lax.dot_general(act, w_ref[...], NN, preferred_element_type=F32,
                              precision=lax.Precision.HIGHEST) + b_ref[:, pl.ds(pl.multiple_of(me * ncol, LANES), ncol)]
        for b in range(N_DEV):
            stage_ref[b] = jnp.broadcast_to(mod[b:b + 1, :], (8, ncol))
        mod_ref[me] = stage_ref[me]
        for k in range(1, N_DEV):
            scatter(k).start()

        passed = []
        for j, chip in enumerate(chips):
            for w in range(n_w):
                block(1 + j, w, (*chip, c), here).wait_recv()
                fwd = block(4 + j, w, (*chip, c), sibling)
                fwd.start()
                passed.append(fwd)
        for w in range(n_w):
            block(0, w, sibling, here).wait_recv()
        for j, chip in enumerate(chips):
            for w in range(n_w):
                block(4 + j, w, (*chip, 1 - c), here).wait_recv()
        for k in range(1, N_DEV):
            scatter(k).wait_recv()
            gather_taps(k).wait_recv()
        for cp in first + passed:
            cp.wait_send()
        for k in range(1, N_DEV):
            gather(k).wait_send()
            scatter(k).wait_send()
            gather_taps(k).wait_send()
        for cp in mine:
            cp.wait()

    vmem, hbm = pl.BlockSpec(memory_space=pltpu.VMEM), pl.BlockSpec(memory_space=pltpu.HBM)
    out = pl.pallas_call(
        body, name="entry_exchange",
        in_specs=[vmem] * 4 + [hbm] * n_w, out_specs=[vmem] * 3 + [hbm] * n_w,
        out_shape=[jax.ShapeDtypeStruct((N_DEV, 8, d), F32), jax.ShapeDtypeStruct((N_DEV, 8, ncol), F32),
                   jax.ShapeDtypeStruct((N_DEV,) + taps.shape, F32)]
        + [jax.ShapeDtypeStruct((N_DEV * s.shape[0], s.shape[1]), s.dtype) for s in shards],
        scratch_shapes=[pltpu.VMEM((N_DEV, 8, ncol), F32), pltpu.SemaphoreType.DMA((3, N_DEV - 1)),
                        pltpu.SemaphoreType.DMA((3, N_DEV - 1)), pltpu.SemaphoreType.DMA((N_DEV - 1, n_w)),
                        pltpu.SemaphoreType.DMA((N_DEV - 1, n_w)), pltpu.SemaphoreType.DMA((n_w,))],
        compiler_params=_params(),
    )(c_rows, w_ada, b_ada, taps, *shards)
    return out[0], out[1], out[2], out[3:]


def _peer_copies(mode, srcs, lands, send_sems, recv_sems):
    if mode in ("gather_ici", "gather_d2d"):
        x, y, c = _place()
        sibling = (x, y, 1 - c)
        chips = [(1 - x, y), (x, 1 - y), (1 - x, 1 - y)]
        n = len(lands)

        def rows(w, pos):
            r = lands[w].shape[0] // N_DEV
            return lands[w].at[pl.ds(pl.multiple_of(_index(pos) * r, 16), r), :]

        def copy(k, w, src, dst, to):
            return pltpu.make_async_remote_copy(src_ref=src, dst_ref=dst, send_sem=send_sems.at[k * n + w],
                                                recv_sem=recv_sems.at[k * n + w], device_id=to, device_id_type=MESH)

        if mode == "gather_ici":
            targets = [sibling] + [(*chip, c) for chip in chips]
            return [copy(k, w, rows(w, (x, y, c)), rows(w, (x, y, c)), to) for k, to in enumerate(targets) for w in range(n)]
        return [copy(j, w, rows(w, (*chip, c)), rows(w, (*chip, c)), sibling)
                for j, chip in enumerate(chips) for w in range(n)]
    me = _index(_place())
    modes = (mode,) * len(srcs) if isinstance(mode, str) else mode
    copies = []
    for k in range(1, N_DEV):
        peer = _peer(k)
        for w, (src, land) in enumerate(zip(srcs, lands)):
            if modes[w] == "gather":
                r = src.shape[0]
                dst = land.at[pl.ds(pl.multiple_of(me * r, 16), r), :]
            elif modes[w] == "allgather":
                dst = land.at[me]
            else:
                r = src.shape[0] // N_DEV
                src = src.at[pl.ds(pl.multiple_of(_index(peer) * r, 16), r), :]
                dst = land.at[me]
            copies.append(pltpu.make_async_remote_copy(
                src_ref=src, dst_ref=dst, send_sem=send_sems.at[(k - 1) * len(srcs) + w],
                recv_sem=recv_sems.at[(k - 1) * len(srcs) + w],
                device_id=peer, device_id_type=MESH))
    return copies


def _landing_zone(src, me, name, tr):
    r, cols = src.shape
    n_t = r // tr

    def body(me_ref, s_ref, o_ref):
        o_ref[...] = s_ref[...].astype(BF16)

    return pl.pallas_call(
        body, name=name, out_shape=jax.ShapeDtypeStruct((N_DEV * r, cols), BF16),
        grid_spec=pltpu.PrefetchScalarGridSpec(
            num_scalar_prefetch=1, grid=(n_t,), in_specs=[pl.BlockSpec((tr, cols), lambda i, me_ref: (i, 0))],
            out_specs=pl.BlockSpec((tr, cols), lambda i, me_ref: (me_ref[0] * n_t + i, 0))),
        compiler_params=_params(("arbitrary",)),
    )(me.reshape(1).astype(jnp.int32), src)


def _exchange_start(mode, srcs, lands, name):
    n_s, n_a = len(srcs), len(srcs) + len(lands)
    n_cp = _COPIES_PER_ARRAY.get(mode, N_DEV - 1) * len(lands)

    def body(*refs):
        for cp in _peer_copies(mode, refs[:n_s], refs[n_s:n_a], refs[n_a], refs[n_a + 1]):
            cp.start()
        refs[-1][...] = jnp.zeros_like(refs[-1])

    hbm, sem = pl.BlockSpec(memory_space=pltpu.HBM), pl.BlockSpec(memory_space=pltpu.SEMAPHORE)
    arrays = list(srcs) + list(lands)
    out = pl.pallas_call(
        body, name=name,
        out_shape=(pltpu.SemaphoreType.DMA((n_cp,)), pltpu.SemaphoreType.DMA((n_cp,)),
                   *[pltpu.HBM(a.shape, a.dtype) for a in arrays], jax.ShapeDtypeStruct((8, LANES), F32)),
        in_specs=[hbm] * n_a, out_specs=(sem, sem, *[hbm] * n_a, pl.BlockSpec(memory_space=pltpu.VMEM)),
        input_output_aliases={i: 2 + i for i in range(n_a)},
        compiler_params=pltpu.CompilerParams(has_side_effects=pltpu.SideEffectType.DATAFLOW_SIDE_EFFECTING),
    )(*[pltpu.with_memory_space_constraint(a, pltpu.HBM) for a in arrays])
    return out[0], out[1], out[2:2 + n_s], out[2 + n_s:2 + n_a], out[-1]


_COPIES_PER_ARRAY = {"gather_ici": 4, "gather_d2d": 3}


def _exchange_wait(mode, send_sems, recv_sems, srcs, lands, after, name):
    n_s, n_a = len(srcs), len(srcs) + len(lands)

    def body(*refs):
        copies = _peer_copies(mode, refs[:n_s], refs[n_s:n_a], refs[n_a], refs[n_a + 1])
        for cp in copies:
            cp.wait_send()
        for cp in copies:
            cp.wait_recv()

    hbm, sem = pl.BlockSpec(memory_space=pltpu.HBM), pl.BlockSpec(memory_space=pltpu.SEMAPHORE)
    arrays = list(srcs) + list(lands)
    out = pl.pallas_call(
        body, name=name, out_shape=tuple(pltpu.HBM(a.shape, a.dtype) for a in arrays),
        in_specs=[hbm] * n_a + [sem, sem, pl.BlockSpec(memory_space=pl.ANY)], out_specs=tuple([hbm] * n_a),
        input_output_aliases={i: i for i in range(n_a)},
        compiler_params=pltpu.CompilerParams(has_side_effects=pltpu.SideEffectType.DATAFLOW_SIDE_EFFECTING),
    )(*arrays, send_sems, recv_sems, after)
    return out[:n_s], out[n_s:]


SMALL_WEIGHTS = ("b_ada", "g_pre_mix", "g_post_mix", "g_pre_ffn", "g_post_ffn", "w_pool", "b_pool", "pool_scale", "conv_b")


MOD_ROWS = ((0, 0), (0, 1), (1, 3), (1, 0), (1, 1), (2, 0))


def _small_sum(mine, gathered):
    n_l = len(mine)
    d = mine[0].shape[1]

    def body(*refs):
        loc, got = refs[:n_l], refs[n_l:2 * n_l]
        tot_refs, dmod_ref = refs[2 * n_l:3 * n_l], refs[3 * n_l]
        me = _index(_place())
        part = lambda a, dev: jnp.where(dev == me, loc[a][...], got[a][dev])
        for a in range(n_l):
            tot = part(a, 0)
            for dev in range(1, N_DEV):
                tot = tot + part(a, dev)
            tot_refs[a][...] = tot
        for dev in range(N_DEV):
            for k, (a, r) in enumerate(MOD_ROWS):
                dmod_ref[dev:dev + 1, k * d:(k + 1) * d] = part(a, dev)[r:r + 1, :]

    vmem = pl.BlockSpec(memory_space=pltpu.VMEM)
    out = pl.pallas_call(
        body, name="small_sum", in_specs=[vmem] * (2 * n_l), out_specs=[vmem] * (n_l + 1),
        out_shape=[jax.ShapeDtypeStruct(a.shape, F32) for a in mine] + [jax.ShapeDtypeStruct((N_DEV, 6 * d), F32)],
        compiler_params=_params(),
    )(*mine, *gathered)
    return out[:n_l], out[n_l]


def _small_adam(totals, weights, moms, vels):
    n_t, n_w = len(totals), len(weights)

    def body(*refs):
        t_in, t_mix, t_ffn, t_pool, t_blk, t_conv, _ = (r[...] for r in refs[:n_t])
        w_refs, m_refs, v_refs = (refs[n_t + k * n_w:n_t + (k + 1) * n_w] for k in range(3))
        outs = refs[n_t + 3 * n_w:]

        def update(idx, g, at=()):
            sel = lambda ref: ref.at[at] if at else ref
            delta, nm, nv = _adam_math(sel(w_refs[idx])[...], g, sel(m_refs[idx])[...], sel(v_refs[idx])[...])
            for k, val in enumerate((g, delta, nm, nv)):
                sel(outs[4 * idx + k])[...] = val

        tots = (t_in, t_mix, t_ffn)
        update(0, jnp.concatenate([tots[a][r:r + 1] for a, r in MOD_ROWS], axis=1))
        update(1, t_in[2:3])
        update(2, t_mix[4:5])
        update(3, t_mix[2:3])
        update(4, t_ffn[1:2])
        for gi in range(len(POOL_WINDOWS)):
            update(5, t_blk[gi], at=(0, gi))
        update(6, jnp.concatenate([t_pool[0:1, gi * HEAD_DIM:(gi + 1) * HEAD_DIM] for gi in range(len(POOL_WINDOWS))], axis=0),
               at=(0,))
        update(7, t_pool[1:2])
        update(8, t_conv[3:4])

    vmem = pl.BlockSpec(memory_space=pltpu.VMEM)
    return pl.pallas_call(
        body, name="small_adam", in_specs=[vmem] * (n_t + 3 * n_w), out_specs=[vmem] * (4 * n_w),
        out_shape=[jax.ShapeDtypeStruct(w.shape, F32) for w in weights for _ in range(4)],
        compiler_params=_params(),
    )(*totals, *weights, *moms, *vels)


def _adam_math(w, g, m, v):
    m = ADAM_B1 * m + (1.0 - ADAM_B1) * g
    v = ADAM_B2 * v + (1.0 - ADAM_B2) * (g * g)
    m_hat = m / (1.0 - ADAM_B1 ** ADAM_STEP)
    v_hat = v / (1.0 - ADAM_B2 ** ADAM_STEP)
    delta = -ADAM_LR * (m_hat / (jnp.sqrt(v_hat) + ADAM_EPS) + ADAM_WD * w)
    return delta, m, v


def _adam(w, g, m, v, name, tr):
    rows, cols = w.shape

    def body(w_ref, g_ref, m_ref, v_ref, d_ref, nm_ref, nv_ref):
        d_ref[...], nm_ref[...], nv_ref[...] = _adam_math(w_ref[...], g_ref[...], m_ref[...], v_ref[...])

    spec = pl.BlockSpec((tr, cols), lambda i: (i, 0))
    shape = jax.ShapeDtypeStruct((rows, cols), F32)
    return pl.pallas_call(
        body, name=name, grid=(rows // tr,), in_specs=[spec] * 4, out_specs=[spec] * 3,
        out_shape=[shape] * 3, compiler_params=_params(("arbitrary",)),
    )(w, g, m, v)


def _sum_adam(own, parts, w, m, v, me, name, tr):
    _, rows, cols = parts.shape
    turned = w.shape == (cols, rows) and rows != cols
    assert tr == rows or not turned
    n_t = rows // tr

    def body(me_ref, own_ref, p_ref, w_ref, m_ref, v_ref, g_ref, d_ref, nm_ref, nv_ref):
        part = lambda dev: jnp.where(dev == me_ref[0], own_ref[...], p_ref[dev]).astype(F32)
        g = part(0)
        for dev in range(1, N_DEV):
            g = g + part(dev)
        g = g.T if turned else g
        g_ref[...] = g
        d_ref[...], nm_ref[...], nv_ref[...] = _adam_math(w_ref[...], g, m_ref[...], v_ref[...])

    spec = pl.BlockSpec((cols, rows) if turned else (tr, cols), lambda i, me_ref: (i, 0))
    shape = jax.ShapeDtypeStruct(w.shape, F32)
    return pl.pallas_call(
        body, name=name, out_shape=[shape] * 4,
        grid_spec=pltpu.PrefetchScalarGridSpec(
            num_scalar_prefetch=1, grid=(n_t,),
            in_specs=[pl.BlockSpec((tr, cols), lambda i, me_ref: (me_ref[0] * n_t + i, 0)),
                      pl.BlockSpec((N_DEV, tr, cols), lambda i, me_ref: (0, i, 0)), spec, spec, spec],
            out_specs=[spec] * 4),
        compiler_params=_params(("arbitrary",)),
    )(me.reshape(1).astype(jnp.int32), own, parts, w, m, v)


def _ada_grad_adam(c_all, dmod_all, w, m, v, tr):
    rows, cols = w.shape

    def body(c_ref, dm_ref, w_ref, m_ref, v_ref, g_ref, d_ref, nm_ref, nv_ref):
        cv = c_ref[...]
        act = cv * jax.nn.sigmoid(cv)
        dmod = dm_ref[:, pl.ds(pl.multiple_of(_index(_place()) * cols, LANES), cols)]
        g = lax.dot_general(act, dmod, TN, preferred_element_type=F32, precision=lax.Precision.HIGHEST)
        g_ref[...] = g
        d_ref[...], nm_ref[...], nv_ref[...] = _adam_math(w_ref[...], g, m_ref[...], v_ref[...])

    spec = pl.BlockSpec((tr, cols), lambda i: (i, 0))
    shape = jax.ShapeDtypeStruct((rows, cols), F32)
    return pl.pallas_call(
        body, name="ada_grad_adam", grid=(rows // tr,),
        in_specs=[pl.BlockSpec((N_DEV, tr), lambda i: (0, i)), pl.BlockSpec(dmod_all.shape, lambda i: (0, 0)), spec, spec, spec],
        out_specs=[spec] * 4, out_shape=[shape] * 4, compiler_params=_params(("arbitrary",)),
    )(c_all, dmod_all, w, m, v)


def _rope_tables(positions):
    inv_freq = ROPE_THETA ** (-jnp.arange(0, 2 * ROT_HALF, 2, dtype=F32) / (2 * ROT_HALF))
    ang = positions.astype(F32)[:, None] * inv_freq
    rows = jnp.concatenate([jnp.cos(ang), jnp.sin(ang), jnp.ones_like(ang)], axis=1)
    spread = [[[0.0] * LANES for _ in range(3 * ROT_HALF)] for _ in range(3)]
    for lane in range(LANES):
        p, j = lane % HEAD_DIM, lane % ROT_HALF
        if p < ROT_HALF:
            spread[0][j][lane] = 1.0
            spread[1][ROT_HALF + j][lane] = -1.0
        elif p < 2 * ROT_HALF:
            spread[0][j][lane] = 1.0
            spread[2][ROT_HALF + j][lane] = 1.0
        else:
            spread[0][2 * ROT_HALF][lane] = 1.0
    return rows, jnp.array(spread, F32)


def _pad_rows(a, rows):
    return jnp.pad(a, ((0, rows - a.shape[0]), (0, 0)))


def _sequence_step(xs, target, rope, mods, gains, w_in_t, w_out_t, relay_ffn, fetch_ffn, send_grads, w_blk_b, b_pool_r,
                   pool_scale_r, conv_w_all, conv_b, after):
    sh_m, sc_m, gt_m, sh_f, sc_f, gt_f = mods
    g_pre_mix, g_post_mix, g_pre_ffn, g_post_ffn = gains
    h1, u_pool, qkv = _premix_inproj(xs, sh_m, sc_m, g_pre_mix, w_in_t, rope, after, tm=512)
    o_g, lse_g = _attn_fwd(qkv)
    x1, y1, h2, cat, attn, lse_all = _mix_out(xs, u_pool, o_g, lse_g, w_blk_b, b_pool_r, pool_scale_r, w_out_t,
                                              gt_m, g_post_mix, g_pre_ffn, sc_f, sh_f, tm=256)
    token = relay_ffn(x1)
    w_up_t, w_down_f = fetch_ffn(x1 if token is None else token)
    gate, a_ffn, act, vd, dy2, dout, sums_ffn, loss_loc = _ffn_fwd_loss(h2, x1, target, w_up_t, w_down_f, conv_w_all, conv_b,
                                                              gt_f, g_post_ffn, tm=256, ck=256)

    dgc, dval, dw_down, dconv = _ffn_bwd_act(dy2, gate, a_ffn, act, vd, w_down_f, tm=512, tf=1408, ck=256)
    dup, dh2 = _ffn_bwd_up(dgc, dval, w_up_t, conv_w_all, tm=256)
    dw_up_t = _wgrad(dup, h2, "wgrad_up", tk=2048, tmm=1408)
    token = send_grads("ffn", [dw_up_t, dw_down], [])
    dx1, dpool, dattn, delta, dw_out_t, sums_mix = _mix_bwd(dh2, dout, x1, y1, cat, attn, w_out_t, sc_f,
                                                           g_pre_ffn, gt_m, g_post_mix, token, tm=256)
    du, dw_blk, sums_pool = _pool_bwd(dpool, u_pool, w_blk_b, b_pool_r, pool_scale_r, tm=512)
    token = send_grads("out", [dw_out_t], [sums_mix, sums_ffn, sums_pool, dw_blk, dconv, loss_loc])
    dproj = _dproj_assemble(du, _attn_bwd(qkv, dattn, lse_all, delta, token), rope, tm=512)
    dw_in_t = _wgrad(dproj, h1, "wgrad_in", tk=2048, tmm=1280)
    token = send_grads("in", [dw_in_t], [])
    grad_x, sums_in = _inproj_bwd(dproj, w_in_t, xs, dx1, sc_m, g_pre_mix, token, tm=256)
    return (loss_loc, grad_x, dw_in_t, dw_out_t, dw_up_t, dw_down, dw_blk, dconv,
            sums_in, sums_mix, sums_ffn, sums_pool)


def kernel(x, c, positions, w_ada, b_ada, g_pre_mix, g_post_mix, g_pre_ffn, g_post_ffn, w_in, w_pool, b_pool, pool_scale, w_out, w_up, conv_w, conv_b, w_down, loss_target, m_w_ada, m_b_ada, m_g_pre_mix, m_g_post_mix, m_g_pre_ffn, m_g_post_ffn, m_w_in, m_w_pool, m_b_pool, m_pool_scale, m_w_out, m_w_up, m_conv_w, m_conv_b, m_w_down, v_w_ada, v_b_ada, v_g_pre_mix, v_g_post_mix, v_g_pre_ffn, v_g_post_ffn, v_w_in, v_w_pool, v_b_pool, v_pool_scale, v_w_out, v_w_up, v_conv_w, v_conv_b, v_w_down):
    s_len, d = x.shape[1], x.shape[2]
    d_ff = w_down.shape[1] * N_DEV
    me = _index(_place())
    xs, target = x[0], loss_target[0]

    c_all, mod, taps_all, (w_in_t, w_out_t) = _entry_exchange(
        jnp.broadcast_to(c, (8, d)), w_ada[0], b_ada, _pad_rows(conv_w[0], 8),
        [w_in[0].T.astype(BF16), w_out[0].T.astype(BF16)])
    c_all = c_all[:, 0, :]
    conv_w_all = jnp.transpose(taps_all[:, :3, :], (1, 0, 2)).reshape(3, d_ff)
    sh_m, sc_m, gt_m, sh_f, sc_f, gt_f = [mod[:, 0, :].reshape(1, -1)[:, k * d:(k + 1) * d] for k in range(6)]

    rope = _rope_tables(positions[0])
    w_blk = jnp.zeros((256, 256), F32)
    for gi in range(4):
        w_blk = lax.dynamic_update_slice(w_blk, w_pool[0, gi], (gi * HEAD_DIM, gi * HEAD_DIM))
    w_blk_b = w_blk.astype(BF16)
    b_pool_r, pool_scale_r = b_pool.reshape(1, 256), pool_scale.reshape(1, 256)

    lands = [_landing_zone(s, me, "land_" + nm, 176) for s, nm in ((w_up[0].T, "w_up"), (w_down[0], "w_down"))]
    w_in_t, conv_w_all, *lands = lax.optimization_barrier((w_in_t, conv_w_all, *lands))
    w_send, w_recv, w_src, w_land, w_token = _exchange_start("gather_ici", [], lands, "ffn_weights_ici_start")
    relay = []

    def relay_ffn(after):
        _, blocks = _exchange_wait("gather_ici", w_send, w_recv, w_src, w_land, after, "ffn_weights_ici_wait")
        relay.extend(_exchange_start("gather_d2d", [], blocks, "ffn_weights_d2d_start"))
        return relay[4]

    def fetch_ffn(after):
        return _exchange_wait("gather_d2d", relay[0], relay[1], [], relay[3], after, "ffn_weights_d2d_wait")[1]

    flights = {}

    def send_grads(tag, slabs, whole):
        lands = [lax.empty((N_DEV, g.shape[0] // N_DEV, g.shape[1]), g.dtype) for g in slabs]
        lands += [lax.empty((N_DEV,) + a.shape, F32) for a in whole]
        modes = ("scatter",) * len(slabs) + ("allgather",) * len(whole)
        flights[tag] = (modes, *_exchange_start(modes, slabs + whole, lands, f"grads_{tag}_start"))
        return flights[tag][5]

    def arrived(tag, after):
        return _exchange_wait(*flights[tag][:5], after, f"grads_{tag}_wait")

    _, grad_x, *_, sums_in, _, _, _ = _sequence_step(
        xs, target, rope, (sh_m, sc_m, gt_m, sh_f, sc_f, gt_f), (g_pre_mix, g_post_mix, g_pre_ffn, g_post_ffn),
        w_in_t, w_out_t, relay_ffn, fetch_ffn, send_grads, w_blk_b, b_pool_r, pool_scale_r, conv_w_all, conv_b,
        w_token)

    send_grads("last", [], [sums_in])

    (own_up, own_down), (parts_up, parts_down) = arrived("ffn", flights["last"][5])
    new_up = _sum_adam(own_up, parts_up, w_up[0].T, m_w_up[0].T, v_w_up[0].T, me, "adam_w_up", 352)
    new_down = _sum_adam(own_down, parts_down, w_down[0], m_w_down[0], v_w_down[0], me, "adam_w_down", 176)
    (own_out, *small), (parts_out, *gathered) = arrived("out", new_down[0])
    new_out = _sum_adam(own_out, parts_out, w_out[0], m_w_out[0], v_w_out[0], me, "adam_w_out", 128)
    (own_in,), (parts_in,) = arrived("in", new_out[0])
    new_in = _sum_adam(own_in, parts_in, w_in[0].T, m_w_in[0].T, v_w_in[0].T, me, "adam_w_in", 160)
    big = {"w_up": [a.T for a in new_up], "w_down": new_down, "w_out": new_out, "w_in": [a.T for a in new_in]}

    rep_w = [b_ada, g_pre_mix, g_post_mix, g_pre_ffn, g_post_ffn, w_pool, b_pool, pool_scale, conv_b]
    rep_m = [m_b_ada, m_g_pre_mix, m_g_post_mix, m_g_pre_ffn, m_g_post_ffn, m_w_pool, m_b_pool, m_pool_scale, m_conv_b]
    rep_v = [v_b_ada, v_g_pre_mix, v_g_post_mix, v_g_pre_ffn, v_g_post_ffn, v_w_pool, v_b_pool, v_pool_scale, v_conv_b]
    mine_last, got_last = arrived("last", new_in[0])
    small, gathered = [*mine_last, *small], [*got_last, *gathered]
    totals, dmod_all = _small_sum(small, gathered)
    dconv_tot, loss_tot = totals[5], totals[6]
    rep_out = _small_adam(totals, rep_w, rep_m, rep_v)
    g_rep, d_rep, nm_rep, nv_rep = (rep_out[k::4] for k in range(4))

    fcol = d_ff // N_DEV
    g_cw = lax.dynamic_slice(dconv_tot, (0, me * fcol), (3, fcol))
    d_cw, nm_cw, nv_cw = _adam(conv_w[0], g_cw, m_conv_w[0], v_conv_w[0], "adam_conv_w", 3)

    g_ada, d_ada, nm_ada, nv_ada = _ada_grad_adam(c_all, dmod_all, w_ada[0], m_w_ada[0], v_w_ada[0], 256)

    loss = loss_tot[0, 0]

    def group(k):
        rep = (g_rep, d_rep, nm_rep, nv_rep)[k]
        ada = (g_ada, d_ada, nm_ada, nv_ada)[k][None]
        cw = (g_cw, d_cw, nm_cw, nv_cw)[k][None]
        return [ada, rep[0], rep[1], rep[2], rep[3], rep[4], big["w_in"][k][None], rep[5], rep[6], rep[7],
                big["w_out"][k][None], big["w_up"][k][None], cw, rep[8], big["w_down"][k][None]]

    return (loss, grad_x[None], *group(0), *group(1), *group(2), *group(3))
```

```python
import functools
import math

import jax
import jax.numpy as jnp
from jax import lax
from jax.experimental import pallas as pl
from jax.experimental.pallas import tpu as pltpu

F32 = jnp.float32
BF16 = jnp.bfloat16
MESH = pl.DeviceIdType.MESH

N_DEV = 8
HEAD_DIM = 64
ROT_HALF = 8
ROPE_THETA = 500000.0
POOL_WINDOWS = (2, 4, 8, 16)
DILATIONS = (1, 4, 16)
BLOCK = 128
NORM_EPS = 1e-6
HALO = 16
MASKED = -1e30
ATTN_FWD_UNROLL = 8
ATTN_BWD_UNROLL = 8

ADAM_LR = 0.001
ADAM_B1 = 0.9
ADAM_B2 = 0.999
ADAM_EPS = 1e-08
ADAM_WD = 0.01
ADAM_STEP = 10

V7X_VMEM_LIMIT = 56 * 1024 * 1024
LANES = 128

NT = (((1,), (1,)), ((), ()))
NN = (((1,), (0,)), ((), ()))
TN = (((0,), (0,)), ((), ()))


def _dot(a, b, dims):
    return lax.dot_general(a, b, dims, preferred_element_type=F32)


def _params(sem=None, vmem=V7X_VMEM_LIMIT):
    if sem is None:
        return pltpu.CompilerParams(vmem_limit_bytes=vmem)
    return pltpu.CompilerParams(dimension_semantics=sem, vmem_limit_bytes=vmem)


def _rstd(v):
    return lax.rsqrt(jnp.mean(v * v, axis=-1, keepdims=True) + NORM_EPS)


def _norm_bwd(dn, n, rstd):
    return rstd * (dn - n * jnp.mean(dn * n, axis=-1, keepdims=True))


def _rope_lanes(cs_ref, spread_ref):
    return [lax.dot_general(cs_ref[...], spread_ref[k], NN, preferred_element_type=F32, precision=lax.Precision.HIGHEST)
            for k in range(3)]


def _rope_fwd(p, lanes):
    return p * lanes[0] + pltpu.roll(p, LANES - ROT_HALF, 1) * lanes[1] + pltpu.roll(p, ROT_HALF, 1) * lanes[2]


def _rope_bwd(dp, lanes):
    return dp * lanes[0] + pltpu.roll(dp * lanes[1], ROT_HALF, 1) + pltpu.roll(dp * lanes[2], LANES - ROT_HALF, 1)


def _gelu_parts(v):
    k2 = 2.0 * math.sqrt(2.0 / math.pi)
    c = 0.044715
    v2 = v * v
    s = jax.nn.sigmoid(v * (k2 + (k2 * c) * v2))
    g = v * s
    dg = s + g * (1.0 - s) * (k2 + (3.0 * k2 * c) * v2)
    return g, dg


def _halo_before(i, tile):
    return jnp.maximum(i * (tile // HALO) - 1, 0)


def _premix_inproj(x, sh, sc, g, w_in_t, rope, after, tm):
    s_len, d = x.shape
    n_proj = w_in_t.shape[0]
    n_slab = (n_proj - 256) // LANES

    def body(x_ref, sh_ref, sc_ref, g_ref, w_ref, cs_ref, spread_ref, after_ref, h_ref, up_ref, qkv_ref):
        xv = x_ref[...]
        h = (xv * _rstd(xv) * g_ref[...]) * (1.0 + sc_ref[...]) + sh_ref[...]
        hb = h.astype(BF16)
        h_ref[...] = hb
        up_ref[...] = _dot(hb, w_ref[0:256, :], NT)
        lanes = _rope_lanes(cs_ref, spread_ref)
        for pair in range(n_slab // 2):
            p = _dot(hb, w_ref[256 + 256 * pair:512 + 256 * pair, :], NT)
            for half in range(2):
                ph = p[:, half * LANES:(half + 1) * LANES]
                if pair < 6:
                    ph = _rope_fwd(ph, lanes)
                if pair < 3:
                    ph = ph * (HEAD_DIM ** -0.5)
                qkv_ref[2 * pair + half] = ph

    vec = pl.BlockSpec((1, d), lambda i: (0, 0))
    return pl.pallas_call(
        body, name="premix_inproj", grid=(s_len // tm,),
        in_specs=[pl.BlockSpec((tm, d), lambda i: (i, 0)), vec, vec, vec,
                  pl.BlockSpec((n_proj, d), lambda i: (0, 0)),
                  pl.BlockSpec((tm, rope[0].shape[1]), lambda i: (i, 0)), pl.BlockSpec(rope[1].shape, lambda i: (0, 0, 0)),
                  pl.BlockSpec(memory_space=pl.ANY)],
        out_specs=[pl.BlockSpec((tm, d), lambda i: (i, 0)),
                   pl.BlockSpec((tm, 256), lambda i: (i, 0)),
                   pl.BlockSpec((n_slab, tm, LANES), lambda i: (0, i, 0))],
        out_shape=[jax.ShapeDtypeStruct((s_len, d), BF16),
                   jax.ShapeDtypeStruct((s_len, 256), F32),
                   jax.ShapeDtypeStruct((n_slab, s_len, LANES), F32)],
        compiler_params=_params(("arbitrary",)),
    )(x, sh, sc, g, w_in_t, *rope, after)


def _block_rows(n, r, dil):
    start = n * (BLOCK * dil) + r
    if dil == 1:
        return pl.ds(pl.multiple_of(start, BLOCK), BLOCK)
    return pl.ds(start, BLOCK, stride=dil)


def _band_mask(n):
    ri = lax.broadcasted_iota(jnp.int32, (BLOCK, 2 * BLOCK), 0)
    cj = lax.broadcasted_iota(jnp.int32, (BLOCK, 2 * BLOCK), 1)
    cur = (cj >= BLOCK) & (cj - BLOCK <= ri)
    prev = (cj < BLOCK) & (cj >= ri) & (n > 0)
    return cur | prev


def _attn_fwd(qkv):
    s_len = qkv.shape[1]
    n_g = len(DILATIONS)

    def body(q_ref, k_ref, v_ref, o_ref, lse_ref):
        lane = lax.broadcasted_iota(jnp.int32, (BLOCK, LANES), 1)
        first = lane < HEAD_DIM

        def group(dil):
            nb = s_len // (BLOCK * dil)

            def block(t, carry):
                r, n = t // nb, t % nb
                cur = _block_rows(n, r, dil)
                prev = _block_rows(jnp.maximum(n - 1, 0), r, dil)
                q = q_ref[0, cur, :]
                kcat = jnp.concatenate([k_ref[0, prev, :], k_ref[0, cur, :]], axis=0).astype(BF16)
                vcat = jnp.concatenate([v_ref[0, prev, :], v_ref[0, cur, :]], axis=0).astype(BF16)
                valid = _band_mask(n)
                q2 = jnp.concatenate([jnp.where(first, q, 0.0), jnp.where(first, 0.0, q)], axis=0).astype(BF16)
                s = jnp.where(jnp.concatenate([valid, valid], axis=0), _dot(q2, kcat, NT), MASKED)
                m = jnp.max(s, axis=-1, keepdims=True)
                p = jnp.exp(s - m)
                den = jnp.sum(p, axis=-1, keepdims=True)
                o2 = _dot(p.astype(BF16), vcat, NN) / den
                lse2 = m + jnp.log(den)
                o_ref[0, 0, cur, :] = jnp.where(first, o2[:BLOCK], o2[BLOCK:])
                lse_ref[0, 0, cur, :] = jnp.where(first, lse2[:BLOCK], lse2[BLOCK:])
                return carry

            lax.fori_loop(0, nb * dil, block, 0, unroll=ATTN_FWD_UNROLL)

        for gi, dil in enumerate(DILATIONS):
            pl.when(pl.program_id(0) == gi)(functools.partial(group, dil))

    def slab(base):
        return pl.BlockSpec((1, s_len, LANES), lambda g, s: (base + 2 * g + s, 0, 0))

    out = pl.BlockSpec((1, 1, s_len, LANES), lambda g, s: (g, s, 0, 0))
    shape = jax.ShapeDtypeStruct((n_g, 2, s_len, LANES), F32)
    return pl.pallas_call(
        body, name="attn_fwd", grid=(n_g, 2),
        in_specs=[slab(0), slab(6), slab(12)], out_specs=[out, out], out_shape=[shape, shape],
        compiler_params=_params(("arbitrary", "arbitrary")),
    )(qkv, qkv, qkv)


def _pool_mixed(u, halo, i, tm):
    ue = jnp.concatenate([halo, u], axis=0)
    s2 = ue + pltpu.roll(ue, 1, 0)
    s4 = s2 + pltpu.roll(s2, 2, 0)
    s8 = s4 + pltpu.roll(s4, 4, 0)
    s16 = s8 + pltpu.roll(s8, 8, 0)
    grp = lax.broadcasted_iota(jnp.int32, (tm, 256), 1) // HEAD_DIM
    pick = lambda a, b, c, e: jnp.where(grp == 0, a, jnp.where(grp == 1, b, jnp.where(grp == 2, c, e)))
    win_sum = pick(s2[HALO:], s4[HALO:], s8[HALO:], s16[HALO:])
    pos = (i * tm + lax.broadcasted_iota(jnp.int32, (tm, 256), 0)).astype(F32)
    count = jnp.minimum(pos + 1.0, pick(*[float(w) for w in POOL_WINDOWS]))
    return win_sum / count - u, count


def _mix_out(x, u_pool, o_g, lse_g, w_blk, b_pool, pool_scale, w_out_t, gt_m, g_post_mix, g_pre_ffn, sc_f, sh_f, tm):
    s_len, d = x.shape

    def body(x_ref, u_ref, uh_ref, o_ref, l_ref, wb_ref, bp_ref, ps_ref, wo_ref,
             gt_ref, g1_ref, g2_ref, sc_ref, sh_ref,
             x1_ref, y1_ref, h2_ref, cat_ref, attn_ref, lall_ref):
        (o0, o1, o2), (l0, l1, l2) = (o_ref.at[g] for g in range(3)), (l_ref.at[g] for g in range(3))
        i = pl.program_id(0)
        u = u_ref[...]
        halo = uh_ref[...] * (i > 0).astype(F32)
        mixed, _ = _pool_mixed(u, halo, i, tm)
        y = _dot(mixed.astype(BF16), wb_ref[...], NN) + bp_ref[...]
        pool = y * ps_ref[...]
        attn = []
        for s in range(2):
            la, lb, lc = l0[s], l1[s], l2[s]
            mx = jnp.maximum(jnp.maximum(la, lb), lc)
            ea, eb, ec = jnp.exp(la - mx), jnp.exp(lb - mx), jnp.exp(lc - mx)
            den = ea + eb + ec
            lall_ref[s] = mx + jnp.log(den)
            attn.append((ea / den) * o0[s] + (eb / den) * o1[s] + (ec / den) * o2[s])
        attn = jnp.concatenate(attn, axis=1)
        attn_ref[...] = attn
        cat = jnp.concatenate([pool, attn], axis=1).astype(BF16)
        cat_ref[...] = cat
        y1 = _dot(cat, wo_ref[...], NT)
        y1_ref[...] = y1.astype(BF16)
        x1 = x_ref[...] + gt_ref[...] * (y1 * _rstd(y1) * g1_ref[...])
        x1_ref[...] = x1
        h2 = (x1 * _rstd(x1) * g2_ref[...]) * (1.0 + sc_ref[...]) + sh_ref[...]
        h2_ref[...] = h2.astype(BF16)

    tile = lambda w: pl.BlockSpec((tm, w), lambda i: (i, 0))
    slab = pl.BlockSpec((2, tm, LANES), lambda i: (0, i, 0))
    groups = pl.BlockSpec((len(DILATIONS), 2, tm, LANES), lambda i: (0, 0, i, 0))
    const = lambda a: pl.BlockSpec(a.shape, lambda i: (0,) * a.ndim)
    return pl.pallas_call(
        body, name="mix_out", grid=(s_len // tm,),
        in_specs=[tile(d), tile(256), pl.BlockSpec((HALO, 256), lambda i: (_halo_before(i, tm), 0)),
                  groups, groups,
                  const(w_blk), const(b_pool), const(pool_scale), const(w_out_t),
                  const(gt_m), const(g_post_mix), const(g_pre_ffn), const(sc_f), const(sh_f)],
        out_specs=[tile(d), tile(d), tile(d), tile(512), tile(256), slab],
        out_shape=[jax.ShapeDtypeStruct((s_len, d), F32), jax.ShapeDtypeStruct((s_len, d), BF16),
                   jax.ShapeDtypeStruct((s_len, d), BF16), jax.ShapeDtypeStruct((s_len, 512), BF16),
                   jax.ShapeDtypeStruct((s_len, 256), F32), jax.ShapeDtypeStruct((2, s_len, LANES), F32)],
        compiler_params=_params(("arbitrary",)),
    )(x, u_pool, u_pool, o_g, lse_g, w_blk, b_pool, pool_scale, w_out_t, gt_m, g_post_mix, g_pre_ffn, sc_f, sh_f)


def _conv_gate(gate_ext, cw, cb):
    gc = gate_ext * cw[2:3, :] + pltpu.roll(gate_ext, 1, 0) * cw[1:2, :] + pltpu.roll(gate_ext, 2, 0) * cw[0:1, :]
    return gc[HALO:] + cb


def _ffn_fwd_loss(h2, x1, target, w_up_t, w_down, conv_w, conv_b, gt_f, g_post_ffn, tm, ck):
    s_len, d = x1.shape
    d_ff = w_down.shape[0]
    n_t, n_c = s_len // tm, d_ff // ck

    def body(h_ref, hh_ref, x1_ref, tgt_ref, wg_ref, wv_ref, wd_ref, cw_ref, cb_ref, gt_ref, g_ref,
             gate_ref, a_ref, act_ref, vd_ref, dy2_ref, dout_ref, sums_ref, loss_ref, acc_ref):
        i = pl.program_id(0)

        @pl.when(i == 0)
        def _():
            sums_ref[...] = jnp.zeros_like(sums_ref)
            loss_ref[...] = jnp.zeros_like(loss_ref)
            acc_ref[...] = jnp.zeros_like(acc_ref)

        def finish(live):
            y2 = acc_ref[...]
            rstd = _rstd(y2)
            n = y2 * rstd
            rn = n * g_ref[...]
            err = x1_ref[...] + gt_ref[...] * rn - tgt_ref[...]
            keep = lambda v: jnp.where(live, v, 0.0)
            loss_ref[...] += keep(0.5 * jnp.sum(jnp.mean(err * err, axis=-1, keepdims=True), axis=0, keepdims=True))
            dout = err * (1.0 / d)
            dout_ref[...] = dout
            drn = dout * gt_ref[...]
            sums_ref[0:1, :] += keep(jnp.sum(dout * rn, axis=0, keepdims=True))
            sums_ref[1:2, :] += keep(jnp.sum(drn * n, axis=0, keepdims=True))
            dy2_ref[...] = _norm_bwd(drn * g_ref[...], n, rstd).astype(BF16)

        @pl.when(i < n_t)
        def _():
            h = h_ref[...]
            h_ext = jnp.concatenate([hh_ref[...], h], axis=0)
            row = lax.broadcasted_iota(jnp.int32, (tm + HALO, ck), 0)
            no_halo = (row < HALO) & (i == 0)

            def up(c):
                cs = slice(c * ck, (c + 1) * ck)
                return jnp.where(no_halo, 0.0, _dot(h_ext, wg_ref[cs, :], NT)), _dot(h, wv_ref[cs, :], NT)

            part = None
            nxt = up(0)
            finish(i > 0)
            for c in range(n_c):
                cs = slice(c * ck, (c + 1) * ck)
                gate_ext, val = nxt
                if c + 1 < n_c:
                    nxt = up(c + 1)
                act, dact = _gelu_parts(_conv_gate(gate_ext, cw_ref[:, cs], cb_ref[:, cs]))
                a = (act * val).astype(BF16)
                gate_ref[:, cs] = gate_ext[HALO:].astype(BF16)
                a_ref[:, cs] = a
                act_ref[:, cs] = act.astype(BF16)
                vd_ref[:, cs] = (val * dact).astype(BF16)
                p = _dot(a, wd_ref[cs, :], NN)
                part = p if part is None else part + p
            acc_ref[...] = part

        @pl.when(i == n_t)
        def _():
            finish(True)

    this = lambda i: jnp.minimum(i, n_t - 1)
    before = lambda i: jnp.maximum(i - 1, 0)
    tok = lambda w, at: pl.BlockSpec((tm, w), lambda i: (at(i), 0))
    vec = pl.BlockSpec((1, d), lambda i: (0, 0))
    once = lambda shape, imap: pl.BlockSpec(shape, imap, pipeline_mode=pl.Buffered(1))
    return pl.pallas_call(
        body, name="ffn_fwd_loss", grid=(n_t + 1,),
        in_specs=[tok(d, this), pl.BlockSpec((HALO, d), lambda i: (_halo_before(this(i), tm), 0)),
                  tok(d, before), tok(d, before),
                  once((d_ff, d), lambda i: (0, 0)), once((d_ff, d), lambda i: (1, 0)), once((d_ff, d), lambda i: (0, 0)),
                  pl.BlockSpec((3, d_ff), lambda i: (0, 0)), pl.BlockSpec((1, d_ff), lambda i: (0, 0)), vec, vec],
        out_specs=[tok(d_ff, this)] * 4 + [tok(d, before), tok(d, before), pl.BlockSpec((8, d), lambda i: (0, 0)),
                                          pl.BlockSpec((8, LANES), lambda i: (0, 0))],
        out_shape=[jax.ShapeDtypeStruct((s_len, d_ff), BF16)] * 4
        + [jax.ShapeDtypeStruct((s_len, d), BF16), jax.ShapeDtypeStruct((s_len, d), F32),
           jax.ShapeDtypeStruct((8, d), F32), jax.ShapeDtypeStruct((8, LANES), F32)],
        scratch_shapes=[pltpu.VMEM((tm, d), F32)],
        compiler_params=_params(("arbitrary",)),
    )(h2, h2, x1, target, w_up_t, w_up_t, w_down, conv_w, conv_b, gt_f, g_post_ffn)


def _ffn_bwd_act(dy2, gate, a, act, vd, w_down, tm, tf, ck):
    s_len, d = dy2.shape
    d_ff = w_down.shape[0]
    n_t = s_len // tm
    chunks = [slice(lo, min(lo + ck, tf)) for lo in range(0, tf, ck)]

    def body(dy_ref, g_ref, gh_ref, a_ref, act_ref, vd_ref, wd_ref, dgc_ref, dval_ref, dwd_ref, dconv_ref, acc_ref):
        i = pl.program_id(1)

        @pl.when(i == 0)
        def _():
            acc_ref[...] = jnp.zeros_like(acc_ref)
            dconv_ref[...] = jnp.zeros_like(dconv_ref)

        dy = dy_ref[...]

        def down(cs):
            return _dot(dy, wd_ref[cs, :], NT)

        nxt = down(chunks[0])
        for c, cs in enumerate(chunks):
            width = cs.stop - cs.start
            da = nxt
            if c + 1 < len(chunks):
                nxt = down(chunks[c + 1])
            acc_ref[cs, :] += _dot(a_ref[:, cs], dy, TN)
            row = lax.broadcasted_iota(jnp.int32, (tm + HALO, width), 0)
            gate_ext = jnp.where((row < HALO) & (i == 0), 0.0,
                                 jnp.concatenate([gh_ref[:, cs], g_ref[:, cs]], axis=0).astype(F32))
            dgc = da * vd_ref[:, cs].astype(F32)
            dgc_ref[:, cs] = dgc.astype(BF16)
            dval_ref[:, cs] = (da * act_ref[:, cs].astype(F32)).astype(BF16)
            rows = [jnp.sum(dgc * pltpu.roll(gate_ext, 2 - k, 0)[HALO:], axis=0, keepdims=True) for k in range(2)]
            rows += [jnp.sum(dgc * gate_ext[HALO:], axis=0, keepdims=True), jnp.sum(dgc, axis=0, keepdims=True),
                     jnp.zeros((4, width), F32)]
            dconv_ref[:, cs] += jnp.concatenate(rows, axis=0)

        @pl.when(i == n_t - 1)
        def _():
            dwd_ref[...] = acc_ref[...].astype(BF16)

    tokf = pl.BlockSpec((tm, tf), lambda j, i: (i, j))
    return pl.pallas_call(
        body, name="ffn_bwd_act", grid=(d_ff // tf, n_t),
        in_specs=[pl.BlockSpec((tm, d), lambda j, i: (i, 0)), tokf,
                  pl.BlockSpec((HALO, tf), lambda j, i: (_halo_before(i, tm), j)), tokf, tokf, tokf,
                  pl.BlockSpec((tf, d), lambda j, i: (j, 0))],
        out_specs=[tokf, tokf, pl.BlockSpec((tf, d), lambda j, i: (j, 0)), pl.BlockSpec((8, tf), lambda j, i: (0, j))],
        out_shape=[jax.ShapeDtypeStruct((s_len, d_ff), BF16), jax.ShapeDtypeStruct((s_len, d_ff), BF16),
                   jax.ShapeDtypeStruct((d_ff, d), BF16), jax.ShapeDtypeStruct((8, d_ff), F32)],
        scratch_shapes=[pltpu.VMEM((tf, d), F32)],
        compiler_params=_params(("arbitrary", "arbitrary")),
    )(dy2, gate, gate, a, act, vd, w_down)


def _ffn_bwd_up(dgc, dval, w_up_t, conv_w, tm):
    s_len, d_ff = dgc.shape
    d = w_up_t.shape[1]
    n_t = s_len // tm

    def body(dg_ref, dgn_ref, dv_ref, cw_ref, w_ref, dup_ref, dh_ref):
        i = pl.program_id(0)
        nxt = dgn_ref[...].astype(F32) * (i < n_t - 1).astype(F32)
        ext = jnp.concatenate([dg_ref[...].astype(F32), nxt], axis=0)
        rows = tm + HALO
        dgate = (ext * cw_ref[2:3, :] + pltpu.roll(ext, rows - 1, 0) * cw_ref[1:2, :]
                 + pltpu.roll(ext, rows - 2, 0) * cw_ref[0:1, :])[:tm]
        dup = jnp.concatenate([dgate.astype(BF16), dv_ref[...]], axis=1)
        dup_ref[...] = dup
        dh_ref[...] = _dot(dup, w_ref[...], NN).astype(BF16)

    tokf = pl.BlockSpec((tm, d_ff), lambda i: (i, 0))
    return pl.pallas_call(
        body, name="ffn_bwd_up", grid=(n_t,),
        in_specs=[tokf, pl.BlockSpec((HALO, d_ff), lambda i: (jnp.minimum((i + 1) * (tm // HALO), s_len // HALO - 1), 0)),
                  tokf, pl.BlockSpec((3, d_ff), lambda i: (0, 0)), pl.BlockSpec((2 * d_ff, d), lambda i: (0, 0))],
        out_specs=[pl.BlockSpec((tm, 2 * d_ff), lambda i: (i, 0)), pl.BlockSpec((tm, d), lambda i: (i, 0))],
        out_shape=[jax.ShapeDtypeStruct((s_len, 2 * d_ff), BF16), jax.ShapeDtypeStruct((s_len, d), BF16)],
        compiler_params=_params(("arbitrary",)),
    )(dgc, dgc, dval, conv_w, w_up_t)


def _mix_bwd(dh2, dout, x1, y1, cat, attn, w_out_t, sc_f, g_pre_ffn, gt_m, g_post_mix, after, tm):
    s_len, d = x1.shape
    n_t = s_len // tm

    def body(dh_ref, do_ref, x1_ref, y1_ref, cat_ref, at_ref, wo_ref, sc_ref, g2_ref, gt_ref, g1_ref, after_ref,
             dx1_ref, dpool_ref, dattn_ref, delta_ref, dwo_ref, sums_ref, acc_ref):
        i = pl.program_id(0)
        dh = dh_ref[...].astype(F32)
        x1 = x1_ref[...]
        r2 = _rstd(x1)
        n2 = x1 * r2
        ng = n2 * g2_ref[...]
        dng = dh * (1.0 + sc_ref[...])
        dx1 = do_ref[...] + _norm_bwd(dng * g2_ref[...], n2, r2)
        dx1_ref[...] = dx1
        y1 = y1_ref[...].astype(F32)
        r1 = _rstd(y1)
        n1 = y1 * r1
        drn = dx1 * gt_ref[...]
        dy1 = _norm_bwd(drn * g1_ref[...], n1, r1).astype(BF16)
        dcat = _dot(dy1, wo_ref[...], NN)
        dpool_ref[...] = dcat[:, 0:256]
        lane = lax.broadcasted_iota(jnp.int32, (tm, LANES), 1)
        first = lane < HEAD_DIM
        for s in range(2):
            da = dcat[:, 256 + s * LANES:256 + (s + 1) * LANES]
            dattn_ref[s] = da
            prod = da * at_ref[:, s * LANES:(s + 1) * LANES]
            tot = jnp.sum(prod, axis=-1, keepdims=True)
            lo = jnp.sum(jnp.where(first, prod, 0.0), axis=-1, keepdims=True)
            delta_ref[s] = jnp.where(first, lo, tot - lo)
        dwo = _dot(dy1, cat_ref[...], TN)
        sums = jnp.concatenate(
            [jnp.sum(dh, axis=0, keepdims=True), jnp.sum(dh * ng, axis=0, keepdims=True),
             jnp.sum(dng * n2, axis=0, keepdims=True), jnp.sum(dx1 * (n1 * g1_ref[...]), axis=0, keepdims=True),
             jnp.sum(drn * n1, axis=0, keepdims=True), jnp.zeros((3, d), F32)], axis=0)

        @pl.when(i == 0)
        def _():
            acc_ref[...] = dwo
            sums_ref[...] = sums

        @pl.when(i > 0)
        def _():
            acc_ref[...] += dwo
            sums_ref[...] += sums

        @pl.when(i == n_t - 1)
        def _():
            dwo_ref[...] = acc_ref[...].astype(BF16)

    tile = lambda w: pl.BlockSpec((tm, w), lambda i: (i, 0))
    slab = pl.BlockSpec((2, tm, LANES), lambda i: (0, i, 0))
    vec = pl.BlockSpec((1, d), lambda i: (0, 0))
    return pl.pallas_call(
        body, name="mix_bwd", grid=(n_t,),
        in_specs=[tile(d), tile(d), tile(d), tile(d), tile(512), tile(256),
                  pl.BlockSpec((d, 512), lambda i: (0, 0)), vec, vec, vec, vec, pl.BlockSpec(memory_space=pl.ANY)],
        out_specs=[tile(d), tile(256), slab, slab, pl.BlockSpec((d, 512), lambda i: (0, 0)),
                   pl.BlockSpec((8, d), lambda i: (0, 0))],
        out_shape=[jax.ShapeDtypeStruct((s_len, d), F32), jax.ShapeDtypeStruct((s_len, 256), F32),
                   jax.ShapeDtypeStruct((2, s_len, LANES), F32), jax.ShapeDtypeStruct((2, s_len, LANES), F32),
                   jax.ShapeDtypeStruct((d, 512), BF16), jax.ShapeDtypeStruct((8, d), F32)],
        scratch_shapes=[pltpu.VMEM((d, 512), F32)],
        compiler_params=_params(("arbitrary",)),
    )(dh2, dout, x1, y1, cat, attn, w_out_t, sc_f, g_pre_ffn, gt_m, g_post_mix, after)


def _pool_bwd(dpool, u_pool, w_blk, b_pool, pool_scale, tm):
    s_len = dpool.shape[0]
    n_t = s_len // tm

    def body(dp_ref, dpn_ref, u_ref, uh_ref, wb_ref, bp_ref, ps_ref, du_ref, dwp_ref, sums_ref, acc_ref):
        i = pl.program_id(0)
        u = u_ref[...]
        mixed, _ = _pool_mixed(u, uh_ref[...] * (i > 0).astype(F32), i, tm)
        mixed_b = mixed.astype(BF16)
        y = _dot(mixed_b, wb_ref[...], NN) + bp_ref[...]
        dp = dp_ref[...]
        dy = dp * ps_ref[...]
        dwb = _dot(mixed_b, dy.astype(BF16), TN)
        sums = jnp.concatenate([jnp.sum(dy, axis=0, keepdims=True), jnp.sum(dp * y, axis=0, keepdims=True),
                                jnp.zeros((6, 256), F32)], axis=0)
        dp_ext = jnp.concatenate([dp, dpn_ref[...] * (i < n_t - 1).astype(F32)], axis=0)
        dmix = _dot((dp_ext * ps_ref[...]).astype(BF16), wb_ref[...], NT)
        rows = tm + HALO
        grp = lax.broadcasted_iota(jnp.int32, (rows, 256), 1) // HEAD_DIM
        pick = lambda a, b, c, e: jnp.where(grp == 0, a, jnp.where(grp == 1, b, jnp.where(grp == 2, c, e)))
        pos = (i * tm + lax.broadcasted_iota(jnp.int32, (rows, 256), 0)).astype(F32)
        z = dmix / jnp.minimum(pos + 1.0, pick(*[float(w) for w in POOL_WINDOWS]))
        f2 = z + pltpu.roll(z, rows - 1, 0)
        f4 = f2 + pltpu.roll(f2, rows - 2, 0)
        f8 = f4 + pltpu.roll(f4, rows - 4, 0)
        f16 = f8 + pltpu.roll(f8, rows - 8, 0)
        du_ref[...] = (pick(f2, f4, f8, f16) - dmix)[:tm]

        @pl.when(i == 0)
        def _():
            acc_ref[...] = dwb
            sums_ref[...] = sums

        @pl.when(i > 0)
        def _():
            acc_ref[...] += dwb
            sums_ref[...] += sums

        @pl.when(i == n_t - 1)
        def _():
            full = acc_ref[...]
            for gi in range(len(POOL_WINDOWS)):
                lo = gi * HEAD_DIM
                dwp_ref[gi] = full[lo:lo + HEAD_DIM, lo:lo + HEAD_DIM]

    n_g = len(POOL_WINDOWS)
    tile = pl.BlockSpec((tm, 256), lambda i: (i, 0))
    const = lambda a: pl.BlockSpec(a.shape, lambda i: (0,) * a.ndim)
    return pl.pallas_call(
        body, name="pool_bwd", grid=(n_t,),
        in_specs=[tile, pl.BlockSpec((HALO, 256), lambda i: (jnp.minimum((i + 1) * (tm // HALO), s_len // HALO - 1), 0)),
                  tile, pl.BlockSpec((HALO, 256), lambda i: (_halo_before(i, tm), 0)),
                  const(w_blk), const(b_pool), const(pool_scale)],
        out_specs=[tile, pl.BlockSpec((n_g, HEAD_DIM, HEAD_DIM), lambda i: (0, 0, 0)), pl.BlockSpec((8, 256), lambda i: (0, 0))],
        out_shape=[jax.ShapeDtypeStruct((s_len, 256), F32), jax.ShapeDtypeStruct((n_g, HEAD_DIM, HEAD_DIM), F32),
                   jax.ShapeDtypeStruct((8, 256), F32)],
        scratch_shapes=[pltpu.VMEM((256, 256), F32)],
        compiler_params=_params(("arbitrary",)),
    )(dpool, dpool, u_pool, u_pool, w_blk, b_pool, pool_scale)


def _attn_bwd(qkv, dattn, lse_all, delta, after):
    s_len = qkv.shape[1]
    n_g = len(DILATIONS)

    def body(q_ref, k_ref, v_ref, do_ref, l_ref, dl_ref, after_ref, dq_ref, dk_ref, dv_ref):
        lane = lax.broadcasted_iota(jnp.int32, (BLOCK, LANES), 1)
        first = lane < HEAD_DIM

        def group(dil):
            nb = s_len // (BLOCK * dil)

            def block(t, carry):
                dk_part, dv_part = carry
                r, n = t // nb, t % nb
                cur = _block_rows(n, r, dil)
                prev = _block_rows(jnp.maximum(n - 1, 0), r, dil)
                q = q_ref[0, cur, :]
                do = do_ref[0, cur, :]
                lse = l_ref[0, cur, :]
                dlt = dl_ref[0, cur, :]
                kcat = jnp.concatenate([k_ref[0, prev, :], k_ref[0, cur, :]], axis=0).astype(BF16)
                vcat = jnp.concatenate([v_ref[0, prev, :], v_ref[0, cur, :]], axis=0).astype(BF16)
                valid = _band_mask(n)
                stack = lambda a: jnp.concatenate([jnp.where(first, a, 0.0), jnp.where(first, 0.0, a)], axis=0)
                rows2 = lambda a: jnp.concatenate([a[:, 0:1], a[:, HEAD_DIM:HEAD_DIM + 1]], axis=0)
                q2, do2 = stack(q).astype(BF16), stack(do).astype(BF16)
                valid2 = jnp.concatenate([valid, valid], axis=0)
                p = jnp.where(valid2, jnp.exp(_dot(q2, kcat, NT) - rows2(lse)), 0.0)
                ds = (p * (_dot(do2, vcat, NT) - rows2(dlt))).astype(BF16)
                dq2 = _dot(ds, kcat, NN)
                dq_ref[0, 0, cur, :] = jnp.where(first, dq2[:BLOCK], dq2[BLOCK:])
                dkc = _dot(ds, q2, TN)
                dvc = _dot(p.astype(BF16), do2, TN)
                dk_ref[0, 0, prev, :] = dk_part + dkc[:BLOCK]
                dv_ref[0, 0, prev, :] = dv_part + dvc[:BLOCK]
                dk_ref[0, 0, cur, :] = dkc[BLOCK:]
                dv_ref[0, 0, cur, :] = dvc[BLOCK:]
                return dkc[BLOCK:], dvc[BLOCK:]

            def blocks(tt, carry):
                for u in range(ATTN_BWD_UNROLL):
                    carry = block(tt * ATTN_BWD_UNROLL + u, carry)
                return carry

            zero = jnp.zeros((BLOCK, LANES), F32)
            lax.fori_loop(0, nb * dil // ATTN_BWD_UNROLL, blocks, (zero, zero))

        for gi, dil in enumerate(DILATIONS):
            pl.when(pl.program_id(1) == gi)(functools.partial(group, dil))

    def slab(base):
        return pl.BlockSpec((1, s_len, LANES), lambda s, g: (base + 2 * g + s, 0, 0))

    one = pl.BlockSpec((1, s_len, LANES), lambda s, g: (s, 0, 0))
    out = pl.BlockSpec((1, 1, s_len, LANES), lambda s, g: (g, s, 0, 0))
    shape = jax.ShapeDtypeStruct((n_g, 2, s_len, LANES), F32)
    return pl.pallas_call(
        body, name="attn_bwd", grid=(2, n_g),
        in_specs=[slab(0), slab(6), slab(12), one, one, one, pl.BlockSpec(memory_space=pl.ANY)],
        out_specs=[out, out, out], out_shape=[shape, shape, shape],
        compiler_params=_params(("arbitrary", "arbitrary")),
    )(qkv, qkv, qkv, dattn, lse_all, delta, after)


def _dproj_wgrad_in(du, dqkv, rope, h1, tm, cm):
    s_len = du.shape[0]
    d = h1.shape[1]
    n_proj = 256 + 18 * LANES
    n_t = s_len // tm

    def body(du_ref, dq_ref, dk_ref, dv_ref, cs_ref, spread_ref, h_ref, dproj_ref, dw_ref, acc_ref):
        i = pl.program_id(0)

        @pl.when(i == 0)
        def _():
            acc_ref[...] = jnp.zeros_like(acc_ref)

        dproj_ref[:, 0:256] = du_ref[...].astype(BF16)
        lanes = _rope_lanes(cs_ref, spread_ref)
        col = 256
        for kind, dref in enumerate((dq_ref, dk_ref, dv_ref)):
            for grp in range(3):
                for s in range(2):
                    piece = dref[grp, s]
                    if kind < 2:
                        piece = _rope_bwd(piece, lanes)
                    if kind == 0:
                        piece = piece * (HEAD_DIM ** -0.5)
                    dproj_ref[:, col:col + LANES] = piece.astype(BF16)
                    col += LANES

        for c0 in range(0, n_proj, cm):
            acc_ref[c0:c0 + cm, :] += _dot(dproj_ref[:, c0:c0 + cm], h_ref[...], TN)

        @pl.when(i == n_t - 1)
        def _():
            dw_ref[...] = acc_ref[...].astype(BF16)

    groups = pl.BlockSpec((len(DILATIONS), 2, tm, LANES), lambda i: (0, 0, i, 0))
    return pl.pallas_call(
        body, name="dproj_wgrad_in", grid=(n_t,),
        in_specs=[pl.BlockSpec((tm, 256), lambda i: (i, 0))] + [groups] * 3
        + [pl.BlockSpec((tm, rope[0].shape[1]), lambda i: (i, 0)), pl.BlockSpec(rope[1].shape, lambda i: (0, 0, 0)),
           pl.BlockSpec((tm, d), lambda i: (i, 0))],
        out_specs=[pl.BlockSpec((tm, n_proj), lambda i: (i, 0)), pl.BlockSpec((n_proj, d), lambda i: (0, 0))],
        out_shape=[jax.ShapeDtypeStruct((s_len, n_proj), BF16), jax.ShapeDtypeStruct((n_proj, d), BF16)],
        scratch_shapes=[pltpu.VMEM((n_proj, d), F32)],
        compiler_params=_params(("arbitrary",)),
    )(du, *dqkv, *rope, h1)


def _inproj_bwd(dproj, w_in_t, x, dx1, sc_m, g_pre_mix, after, tm):
    s_len, d = x.shape
    n_proj = w_in_t.shape[0]
    n_t = s_len // tm

    def body(dproj_ref, w_ref, x_ref, dx1_ref, sc_ref, g_ref, after_ref, dx_ref, sums_ref):
        i = pl.program_id(0)
        halves = [slice(0, tm // 2), slice(tm // 2, tm)]
        dhs = [_dot(dproj_ref[rs, :], w_ref[...], NN) for rs in halves]
        sums = None
        for rs, dh in zip(halves, dhs):
            xv = x_ref[rs, :]
            r = _rstd(xv)
            n = xv * r
            dng = dh * (1.0 + sc_ref[...])
            dx_ref[rs, :] = dx1_ref[rs, :] + _norm_bwd(dng * g_ref[...], n, r)
            part = jnp.concatenate([jnp.sum(dh, axis=0, keepdims=True), jnp.sum(dh * (n * g_ref[...]), axis=0, keepdims=True),
                                    jnp.sum(dng * n, axis=0, keepdims=True), jnp.zeros((5, d), F32)], axis=0)
            sums = part if sums is None else sums + part

        @pl.when(i == 0)
        def _():
            sums_ref[...] = sums

        @pl.when(i > 0)
        def _():
            sums_ref[...] += sums

    tile = lambda w: pl.BlockSpec((tm, w), lambda i: (i, 0))
    vec = pl.BlockSpec((1, d), lambda i: (0, 0))
    return pl.pallas_call(
        body, name="inproj_bwd", grid=(n_t,),
        in_specs=[tile(n_proj), pl.BlockSpec((n_proj, d), lambda i: (0, 0)), tile(d), tile(d), vec, vec,
                  pl.BlockSpec(memory_space=pl.ANY)],
        out_specs=[tile(d), pl.BlockSpec((8, d), lambda i: (0, 0))],
        out_shape=[jax.ShapeDtypeStruct((s_len, d), F32), jax.ShapeDtypeStruct((8, d), F32)],
        compiler_params=_params(("arbitrary",)),
    )(dproj, w_in_t, x, dx1, sc_m, g_pre_mix, after)


def _wgrad(a, b, name, tk, tmm):
    s_len, m = a.shape
    n = b.shape[1]
    n_k = s_len // tk

    def body(a_ref, b_ref, o_ref, acc_ref):
        k = pl.program_id(1)
        part = _dot(a_ref[...], b_ref[...], TN)

        @pl.when(k == 0)
        def _():
            acc_ref[...] = part

        @pl.when(k > 0)
        def _():
            acc_ref[...] += part

        @pl.when(k == n_k - 1)
        def _():
            o_ref[...] = acc_ref[...].astype(BF16)

    return pl.pallas_call(
        body, name=name, grid=(m // tmm, n_k),
        in_specs=[pl.BlockSpec((tk, tmm), lambda j, k: (k, j)), pl.BlockSpec((tk, n), lambda j, k: (k, 0))],
        out_specs=pl.BlockSpec((tmm, n), lambda j, k: (j, 0)),
        out_shape=jax.ShapeDtypeStruct((m, n), BF16),
        scratch_shapes=[pltpu.VMEM((tmm, n), F32)],
        compiler_params=_params(("arbitrary", "arbitrary")),
    )(a, b)


def _place():
    return lax.axis_index("x"), lax.axis_index("y"), lax.axis_index("c")


def _peer(k):
    x, y, c = _place()
    bx, by, bc = (k >> 2) & 1, (k >> 1) & 1, k & 1
    return (x ^ bx if bx else x, y ^ by if by else y, c ^ bc if bc else c)


def _index(pos):
    return 4 * pos[0] + 2 * pos[1] + pos[2]


def _entry_exchange(c_rows, w_ada, b_ada, taps, shards):
    d = c_rows.shape[1]
    ncol = w_ada.shape[1]
    n_w = len(shards)

    def body(c_ref, w_ref, b_ref, t_ref, *rest):
        srcs = rest[:n_w]
        call_ref, mod_ref, tall_ref = rest[n_w:n_w + 3]
        outs = rest[n_w + 3:2 * n_w + 3]
        stage_ref, s_send, s_recv, w_send, w_recv, local_sems = rest[2 * n_w + 3:]
        x, y, c = _place()
        here, sibling = (x, y, c), (x, y, 1 - c)
        chips = [(1 - x, y), (x, 1 - y), (1 - x, 1 - y)]
        me = _index(here)

        def small(kind, src, dst, k):
            return pltpu.make_async_remote_copy(src_ref=src, dst_ref=dst, send_sem=s_send.at[kind, k - 1],
                                                recv_sem=s_recv.at[kind, k - 1], device_id=_peer(k), device_id_type=MESH)

        gather = lambda k: small(0, c_ref, call_ref.at[me], k)
        scatter = lambda k: small(1, stage_ref.at[_index(_peer(k))], mod_ref.at[me], k)
        gather_taps = lambda k: small(2, t_ref, tall_ref.at[me], k)

        def rows(w, pos):
            r = shards[w].shape[0]
            return outs[w].at[pl.ds(pl.multiple_of(_index(pos) * r, 16), r), :]

        def block(k, w, pos, to, own=False):
            return pltpu.make_async_remote_copy(
                src_ref=srcs[w] if own else rows(w, pos), dst_ref=rows(w, pos),
                send_sem=w_send.at[k, w], recv_sem=w_recv.at[k, w], device_id=to, device_id_type=MESH)

        call_ref[me] = c_ref[...]
        tall_ref[me] = t_ref[...]
        for k in range(1, N_DEV):
            gather(k).start()
        for k in range(1, N_DEV):
            gather_taps(k).start()
        mine = [pltpu.make_async_copy(srcs[w], rows(w, here), local_sems.at[w]) for w in range(n_w)]
        for cp in mine:
            cp.start()
        first = [block(0, w, here, sibling, own=True) for w in range(n_w)]
        first += [block(1 + j, w, here, (*chip, c), own=True) for j, chip in enumerate(chips) for w in range(n_w)]
        for cp in first:
            cp.start()

        for k in range(1, N_DEV):
            gather(k).wait_recv()
        cv = jnp.concatenate([call_ref[b, 0:1, :] for b in range(N_DEV)], axis=0)
        act = cv * jax.nn.sigmoid(cv)
        mod = lax.dot_general(act, w_ref[...], NN, preferred_element_type=F32,
                              precision=lax.Precision.HIGHEST) + b_ref[:, pl.ds(pl.multiple_of(me * ncol, LANES), ncol)]
        for b in range(N_DEV):
            stage_ref[b] = jnp.broadcast_to(mod[b:b + 1, :], (8, ncol))
        mod_ref[me] = stage_ref[me]
        for k in range(1, N_DEV):
            scatter(k).start()

        passed = []
        for j, chip in enumerate(chips):
            for w in range(n_w):
                block(1 + j, w, (*chip, c), here).wait_recv()
                fwd = block(4 + j, w, (*chip, c), sibling)
                fwd.start()
                passed.append(fwd)
        for w in range(n_w):
            block(0, w, sibling, here).wait_recv()
        for j, chip in enumerate(chips):
            for w in range(n_w):
                block(4 + j, w, (*chip, 1 - c), here).wait_recv()
        for k in range(1, N_DEV):
            scatter(k).wait_recv()
            gather_taps(k).wait_recv()
        for cp in first + passed:
            cp.wait_send()
        for k in range(1, N_DEV):
            gather(k).wait_send()
            scatter(k).wait_send()
            gather_taps(k).wait_send()
        for cp in mine:
            cp.wait()

    vmem, hbm = pl.BlockSpec(memory_space=pltpu.VMEM), pl.BlockSpec(memory_space=pltpu.HBM)
    out = pl.pallas_call(
        body, name="entry_exchange",
        in_specs=[vmem] * 4 + [hbm] * n_w, out_specs=[vmem] * 3 + [hbm] * n_w,
        out_shape=[jax.ShapeDtypeStruct((N_DEV, 8, d), F32), jax.ShapeDtypeStruct((N_DEV, 8, ncol), F32),
                   jax.ShapeDtypeStruct((N_DEV,) + taps.shape, F32)]
        + [jax.ShapeDtypeStruct((N_DEV * s.shape[0], s.shape[1]), s.dtype) for s in shards],
        scratch_shapes=[pltpu.VMEM((N_DEV, 8, ncol), F32), pltpu.SemaphoreType.DMA((3, N_DEV - 1)),
                        pltpu.SemaphoreType.DMA((3, N_DEV - 1)), pltpu.SemaphoreType.DMA((N_DEV - 1, n_w)),
                        pltpu.SemaphoreType.DMA((N_DEV - 1, n_w)), pltpu.SemaphoreType.DMA((n_w,))],
        compiler_params=_params(),
    )(c_rows, w_ada, b_ada, taps, *shards)
    return out[0], out[1], out[2], out[3:]


def _peer_copies(mode, srcs, lands, send_sems, recv_sems):
    if mode in ("gather_ici", "gather_d2d"):
        x, y, c = _place()
        sibling = (x, y, 1 - c)
        chips = [(1 - x, y), (x, 1 - y), (1 - x, 1 - y)]
        n = len(lands)

        def rows(w, pos):
            r = lands[w].shape[0] // N_DEV
            return lands[w].at[pl.ds(pl.multiple_of(_index(pos) * r, 16), r), :]

        def copy(k, w, src, dst, to):
            return pltpu.make_async_remote_copy(src_ref=src, dst_ref=dst, send_sem=send_sems.at[k * n + w],
                                                recv_sem=recv_sems.at[k * n + w], device_id=to, device_id_type=MESH)

        if mode == "gather_ici":
            targets = [sibling] + [(*chip, c) for chip in chips]
            return [copy(k, w, rows(w, (x, y, c)), rows(w, (x, y, c)), to) for k, to in enumerate(targets) for w in range(n)]
        return [copy(j, w, rows(w, (*chip, c)), rows(w, (*chip, c)), sibling)
                for j, chip in enumerate(chips) for w in range(n)]
    me = _index(_place())
    modes = (mode,) * len(srcs) if isinstance(mode, str) else mode
    copies = []
    for k in range(1, N_DEV):
        peer = _peer(k)
        for w, (src, land) in enumerate(zip(srcs, lands)):
            if modes[w] == "gather":
                r = src.shape[0]
                dst = land.at[pl.ds(pl.multiple_of(me * r, 16), r), :]
            elif modes[w] == "allgather":
                dst = land.at[me]
            else:
                r = src.shape[0] // N_DEV
                src = src.at[pl.ds(pl.multiple_of(_index(peer) * r, 16), r), :]
                dst = land.at[me]
            copies.append(pltpu.make_async_remote_copy(
                src_ref=src, dst_ref=dst, send_sem=send_sems.at[(k - 1) * len(srcs) + w],
                recv_sem=recv_sems.at[(k - 1) * len(srcs) + w],
                device_id=peer, device_id_type=MESH))
    return copies


def _landing_zone(src, me, name, tr):
    r, cols = src.shape
    n_t = r // tr

    def body(me_ref, s_ref, o_ref):
        o_ref[...] = s_ref[...].astype(BF16)

    return pl.pallas_call(
        body, name=name, out_shape=jax.ShapeDtypeStruct((N_DEV * r, cols), BF16),
        grid_spec=pltpu.PrefetchScalarGridSpec(
            num_scalar_prefetch=1, grid=(n_t,), in_specs=[pl.BlockSpec((tr, cols), lambda i, me_ref: (i, 0))],
            out_specs=pl.BlockSpec((tr, cols), lambda i, me_ref: (me_ref[0] * n_t + i, 0))),
        compiler_params=_params(("arbitrary",)),
    )(me.reshape(1).astype(jnp.int32), src)


def _exchange_start(mode, srcs, lands, name):
    n_s, n_a = len(srcs), len(srcs) + len(lands)
    n_cp = _COPIES_PER_ARRAY.get(mode, N_DEV - 1) * len(lands)

    def body(*refs):
        for cp in _peer_copies(mode, refs[:n_s], refs[n_s:n_a], refs[n_a], refs[n_a + 1]):
            cp.start()
        refs[-1][...] = jnp.zeros_like(refs[-1])

    hbm, sem = pl.BlockSpec(memory_space=pltpu.HBM), pl.BlockSpec(memory_space=pltpu.SEMAPHORE)
    arrays = list(srcs) + list(lands)
    out = pl.pallas_call(
        body, name=name,
        out_shape=(pltpu.SemaphoreType.DMA((n_cp,)), pltpu.SemaphoreType.DMA((n_cp,)),
                   *[pltpu.HBM(a.shape, a.dtype) for a in arrays], jax.ShapeDtypeStruct((8, LANES), F32)),
        in_specs=[hbm] * n_a, out_specs=(sem, sem, *[hbm] * n_a, pl.BlockSpec(memory_space=pltpu.VMEM)),
        input_output_aliases={i: 2 + i for i in range(n_a)},
        compiler_params=pltpu.CompilerParams(has_side_effects=pltpu.SideEffectType.DATAFLOW_SIDE_EFFECTING),
    )(*[pltpu.with_memory_space_constraint(a, pltpu.HBM) for a in arrays])
    return out[0], out[1], out[2:2 + n_s], out[2 + n_s:2 + n_a], out[-1]


_COPIES_PER_ARRAY = {"gather_ici": 4, "gather_d2d": 3}


def _exchange_wait(mode, send_sems, recv_sems, srcs, lands, after, name):
    n_s, n_a = len(srcs), len(srcs) + len(lands)

    def body(*refs):
        copies = _peer_copies(mode, refs[:n_s], refs[n_s:n_a], refs[n_a], refs[n_a + 1])
        for cp in copies:
            cp.wait_send()
        for cp in copies:
            cp.wait_recv()

    hbm, sem = pl.BlockSpec(memory_space=pltpu.HBM), pl.BlockSpec(memory_space=pltpu.SEMAPHORE)
    arrays = list(srcs) + list(lands)
    out = pl.pallas_call(
        body, name=name, out_shape=tuple(pltpu.HBM(a.shape, a.dtype) for a in arrays),
        in_specs=[hbm] * n_a + [sem, sem, pl.BlockSpec(memory_space=pl.ANY)], out_specs=tuple([hbm] * n_a),
        input_output_aliases={i: i for i in range(n_a)},
        compiler_params=pltpu.CompilerParams(has_side_effects=pltpu.SideEffectType.DATAFLOW_SIDE_EFFECTING),
    )(*arrays, send_sems, recv_sems, after)
    return out[:n_s], out[n_s:]


SMALL_WEIGHTS = ("b_ada", "g_pre_mix", "g_post_mix", "g_pre_ffn", "g_post_ffn", "w_pool", "b_pool", "pool_scale", "conv_b")


MOD_ROWS = ((0, 0), (0, 1), (1, 3), (1, 0), (1, 1), (2, 0))


def _small_sum(mine, gathered):
    n_l = len(mine)
    d = mine[0].shape[1]

    def body(*refs):
        loc, got = refs[:n_l], refs[n_l:2 * n_l]
        tot_refs, dmod_ref = refs[2 * n_l:3 * n_l], refs[3 * n_l]
        me = _index(_place())
        part = lambda a, dev: jnp.where(dev == me, loc[a][...], got[a][dev])
        for a in range(n_l):
            tot = part(a, 0)
            for dev in range(1, N_DEV):
                tot = tot + part(a, dev)
            tot_refs[a][...] = tot
        for dev in range(N_DEV):
            for k, (a, r) in enumerate(MOD_ROWS):
                dmod_ref[dev:dev + 1, k * d:(k + 1) * d] = part(a, dev)[r:r + 1, :]

    vmem = pl.BlockSpec(memory_space=pltpu.VMEM)
    out = pl.pallas_call(
        body, name="small_sum", in_specs=[vmem] * (2 * n_l), out_specs=[vmem] * (n_l + 1),
        out_shape=[jax.ShapeDtypeStruct(a.shape, F32) for a in mine] + [jax.ShapeDtypeStruct((N_DEV, 6 * d), F32)],
        compiler_params=_params(),
    )(*mine, *gathered)
    return out[:n_l], out[n_l]


def _small_adam(totals, weights, moms, vels):
    n_t, n_w = len(totals), len(weights)

    def body(*refs):
        t_in, t_mix, t_ffn, t_pool, t_blk, t_conv, _ = (r[...] for r in refs[:n_t])
        w_refs, m_refs, v_refs = (refs[n_t + k * n_w:n_t + (k + 1) * n_w] for k in range(3))
        outs = refs[n_t + 3 * n_w:]

        def update(idx, g, at=()):
            sel = lambda ref: ref.at[at] if at else ref
            delta, nm, nv = _adam_math(sel(w_refs[idx])[...], g, sel(m_refs[idx])[...], sel(v_refs[idx])[...])
            for k, val in enumerate((g, delta, nm, nv)):
                sel(outs[4 * idx + k])[...] = val

        tots = (t_in, t_mix, t_ffn)
        update(0, jnp.concatenate([tots[a][r:r + 1] for a, r in MOD_ROWS], axis=1))
        update(1, t_in[2:3])
        update(2, t_mix[4:5])
        update(3, t_mix[2:3])
        update(4, t_ffn[1:2])
        for gi in range(len(POOL_WINDOWS)):
            update(5, t_blk[gi], at=(0, gi))
        update(6, jnp.concatenate([t_pool[0:1, gi * HEAD_DIM:(gi + 1) * HEAD_DIM] for gi in range(len(POOL_WINDOWS))], axis=0),
               at=(0,))
        update(7, t_pool[1:2])
        update(8, t_conv[3:4])

    vmem = pl.BlockSpec(memory_space=pltpu.VMEM)
    return pl.pallas_call(
        body, name="small_adam", in_specs=[vmem] * (n_t + 3 * n_w), out_specs=[vmem] * (4 * n_w),
        out_shape=[jax.ShapeDtypeStruct(w.shape, F32) for w in weights for _ in range(4)],
        compiler_params=_params(),
    )(*totals, *weights, *moms, *vels)


def _adam_math(w, g, m, v):
    m = ADAM_B1 * m + (1.0 - ADAM_B1) * g
    v = ADAM_B2 * v + (1.0 - ADAM_B2) * (g * g)
    m_hat = m / (1.0 - ADAM_B1 ** ADAM_STEP)
    v_hat = v / (1.0 - ADAM_B2 ** ADAM_STEP)
    delta = -ADAM_LR * (m_hat / (jnp.sqrt(v_hat) + ADAM_EPS) + ADAM_WD * w)
    return delta, m, v


def _adam(w, g, m, v, name, tr):
    rows, cols = w.shape

    def body(w_ref, g_ref, m_ref, v_ref, d_ref, nm_ref, nv_ref):
        d_ref[...], nm_ref[...], nv_ref[...] = _adam_math(w_ref[...], g_ref[...], m_ref[...], v_ref[...])

    spec = pl.BlockSpec((tr, cols), lambda i: (i, 0))
    shape = jax.ShapeDtypeStruct((rows, cols), F32)
    return pl.pallas_call(
        body, name=name, grid=(rows // tr,), in_specs=[spec] * 4, out_specs=[spec] * 3,
        out_shape=[shape] * 3, compiler_params=_params(("arbitrary",)),
    )(w, g, m, v)


def _sum_adam(own, parts, w, m, v, me, name, tr):
    _, rows, cols = parts.shape
    turned = w.shape == (cols, rows) and rows != cols
    assert tr == rows or not turned
    n_t = rows // tr

    def body(me_ref, own_ref, p_ref, w_ref, m_ref, v_ref, g_ref, d_ref, nm_ref, nv_ref):
        part = lambda dev: jnp.where(dev == me_ref[0], own_ref[...], p_ref[dev]).astype(F32)
        g = part(0)
        for dev in range(1, N_DEV):
            g = g + part(dev)
        g = g.T if turned else g
        g_ref[...] = g
        d_ref[...], nm_ref[...], nv_ref[...] = _adam_math(w_ref[...], g, m_ref[...], v_ref[...])

    spec = pl.BlockSpec((cols, rows) if turned else (tr, cols), lambda i, me_ref: (i, 0))
    shape = jax.ShapeDtypeStruct(w.shape, F32)
    return pl.pallas_call(
        body, name=name, out_shape=[shape] * 4,
        grid_spec=pltpu.PrefetchScalarGridSpec(
            num_scalar_prefetch=1, grid=(n_t,),
            in_specs=[pl.BlockSpec((tr, cols), lambda i, me_ref: (me_ref[0] * n_t + i, 0)),
                      pl.BlockSpec((N_DEV, tr, cols), lambda i, me_ref: (0, i, 0)), spec, spec, spec],
            out_specs=[spec] * 4),
        compiler_params=_params(("arbitrary",)),
    )(me.reshape(1).astype(jnp.int32), own, parts, w, m, v)


def _ada_grad_adam(c_all, dmod_all, w, m, v, tr):
    rows, cols = w.shape

    def body(c_ref, dm_ref, w_ref, m_ref, v_ref, g_ref, d_ref, nm_ref, nv_ref):
        cv = c_ref[...]
        act = cv * jax.nn.sigmoid(cv)
        dmod = dm_ref[:, pl.ds(pl.multiple_of(_index(_place()) * cols, LANES), cols)]
        g = lax.dot_general(act, dmod, TN, preferred_element_type=F32, precision=lax.Precision.HIGHEST)
        g_ref[...] = g
        d_ref[...], nm_ref[...], nv_ref[...] = _adam_math(w_ref[...], g, m_ref[...], v_ref[...])

    spec = pl.BlockSpec((tr, cols), lambda i: (i, 0))
    shape = jax.ShapeDtypeStruct((rows, cols), F32)
    return pl.pallas_call(
        body, name="ada_grad_adam", grid=(rows // tr,),
        in_specs=[pl.BlockSpec((N_DEV, tr), lambda i: (0, i)), pl.BlockSpec(dmod_all.shape, lambda i: (0, 0)), spec, spec, spec],
        out_specs=[spec] * 4, out_shape=[shape] * 4, compiler_params=_params(("arbitrary",)),
    )(c_all, dmod_all, w, m, v)


def _rope_tables(positions):
    inv_freq = ROPE_THETA ** (-jnp.arange(0, 2 * ROT_HALF, 2, dtype=F32) / (2 * ROT_HALF))
    ang = positions.astype(F32)[:, None] * inv_freq
    rows = jnp.concatenate([jnp.cos(ang), jnp.sin(ang), jnp.ones_like(ang)], axis=1)
    spread = [[[0.0] * LANES for _ in range(3 * ROT_HALF)] for _ in range(3)]
    for lane in range(LANES):
        p, j = lane % HEAD_DIM, lane % ROT_HALF
        if p < ROT_HALF:
            spread[0][j][lane] = 1.0
            spread[1][ROT_HALF + j][lane] = -1.0
        elif p < 2 * ROT_HALF:
            spread[0][j][lane] = 1.0
            spread[2][ROT_HALF + j][lane] = 1.0
        else:
            spread[0][2 * ROT_HALF][lane] = 1.0
    return rows, jnp.array(spread, F32)


def _pad_rows(a, rows):
    return jnp.pad(a, ((0, rows - a.shape[0]), (0, 0)))


def _sequence_step(xs, target, rope, mods, gains, w_in_t, w_out_t, relay_ffn, fetch_ffn, send_grads, w_blk_b, b_pool_r,
                   pool_scale_r, conv_w_all, conv_b, after):
    sh_m, sc_m, gt_m, sh_f, sc_f, gt_f = mods
    g_pre_mix, g_post_mix, g_pre_ffn, g_post_ffn = gains
    h1, u_pool, qkv = _premix_inproj(xs, sh_m, sc_m, g_pre_mix, w_in_t, rope, after, tm=512)
    o_g, lse_g = _attn_fwd(qkv)
    x1, y1, h2, cat, attn, lse_all = _mix_out(xs, u_pool, o_g, lse_g, w_blk_b, b_pool_r, pool_scale_r, w_out_t,
                                              gt_m, g_post_mix, g_pre_ffn, sc_f, sh_f, tm=256)
    token = relay_ffn(x1)
    w_up_t, w_down_f = fetch_ffn(x1 if token is None else token)
    gate, a_ffn, act, vd, dy2, dout, sums_ffn, loss_loc = _ffn_fwd_loss(h2, x1, target, w_up_t, w_down_f, conv_w_all, conv_b,
                                                              gt_f, g_post_ffn, tm=256, ck=256)

    dgc, dval, dw_down, dconv = _ffn_bwd_act(dy2, gate, a_ffn, act, vd, w_down_f, tm=512, tf=1408, ck=256)
    dup, dh2 = _ffn_bwd_up(dgc, dval, w_up_t, conv_w_all, tm=256)
    dw_up_t = _wgrad(dup, h2, "wgrad_up", tk=2048, tmm=1408)
    token = send_grads("ffn", [dw_up_t, dw_down], [])
    dx1, dpool, dattn, delta, dw_out_t, sums_mix = _mix_bwd(dh2, dout, x1, y1, cat, attn, w_out_t, sc_f,
                                                           g_pre_ffn, gt_m, g_post_mix, token, tm=256)
    du, dw_blk, sums_pool = _pool_bwd(dpool, u_pool, w_blk_b, b_pool_r, pool_scale_r, tm=512)
    token = send_grads("out", [dw_out_t], [sums_mix, sums_ffn, sums_pool, dw_blk, dconv, loss_loc])
    dproj, dw_in_t = _dproj_wgrad_in(du, _attn_bwd(qkv, dattn, lse_all, delta, token), rope, h1, tm=512, cm=512)
    token = send_grads("in", [dw_in_t], [])
    grad_x, sums_in = _inproj_bwd(dproj, w_in_t, xs, dx1, sc_m, g_pre_mix, token, tm=256)
    return (loss_loc, grad_x, dw_in_t, dw_out_t, dw_up_t, dw_down, dw_blk, dconv,
            sums_in, sums_mix, sums_ffn, sums_pool)


def kernel(x, c, positions, w_ada, b_ada, g_pre_mix, g_post_mix, g_pre_ffn, g_post_ffn, w_in, w_pool, b_pool, pool_scale, w_out, w_up, conv_w, conv_b, w_down, loss_target, m_w_ada, m_b_ada, m_g_pre_mix, m_g_post_mix, m_g_pre_ffn, m_g_post_ffn, m_w_in, m_w_pool, m_b_pool, m_pool_scale, m_w_out, m_w_up, m_conv_w, m_conv_b, m_w_down, v_w_ada, v_b_ada, v_g_pre_mix, v_g_post_mix, v_g_pre_ffn, v_g_post_ffn, v_w_in, v_w_pool, v_b_pool, v_pool_scale, v_w_out, v_w_up, v_conv_w, v_conv_b, v_w_down):
    s_len, d = x.shape[1], x.shape[2]
    d_ff = w_down.shape[1] * N_DEV
    me = _index(_place())
    xs, target = x[0], loss_target[0]

    c_all, mod, taps_all, (w_in_t, w_out_t) = _entry_exchange(
        jnp.broadcast_to(c, (8, d)), w_ada[0], b_ada, _pad_rows(conv_w[0], 8),
        [w_in[0].T.astype(BF16), w_out[0].T.astype(BF16)])
    c_all = c_all[:, 0, :]
    conv_w_all = jnp.transpose(taps_all[:, :3, :], (1, 0, 2)).reshape(3, d_ff)
    sh_m, sc_m, gt_m, sh_f, sc_f, gt_f = [mod[:, 0, :].reshape(1, -1)[:, k * d:(k + 1) * d] for k in range(6)]

    rope = _rope_tables(positions[0])
    w_blk = jnp.zeros((256, 256), F32)
    for gi in range(4):
        w_blk = lax.dynamic_update_slice(w_blk, w_pool[0, gi], (gi * HEAD_DIM, gi * HEAD_DIM))
    w_blk_b = w_blk.astype(BF16)
    b_pool_r, pool_scale_r = b_pool.reshape(1, 256), pool_scale.reshape(1, 256)

    lands = [_landing_zone(s, me, "land_" + nm, 176) for s, nm in ((w_up[0].T, "w_up"), (w_down[0], "w_down"))]
    w_in_t, conv_w_all, *lands = lax.optimization_barrier((w_in_t, conv_w_all, *lands))
    w_send, w_recv, w_src, w_land, w_token = _exchange_start("gather_ici", [], lands, "ffn_weights_ici_start")
    relay = []

    def relay_ffn(after):
        _, blocks = _exchange_wait("gather_ici", w_send, w_recv, w_src, w_land, after, "ffn_weights_ici_wait")
        relay.extend(_exchange_start("gather_d2d", [], blocks, "ffn_weights_d2d_start"))
        return relay[4]

    def fetch_ffn(after):
        return _exchange_wait("gather_d2d", relay[0], relay[1], [], relay[3], after, "ffn_weights_d2d_wait")[1]

    flights = {}

    def send_grads(tag, slabs, whole):
        lands = [lax.empty((N_DEV, g.shape[0] // N_DEV, g.shape[1]), g.dtype) for g in slabs]
        lands += [lax.empty((N_DEV,) + a.shape, F32) for a in whole]
        modes = ("scatter",) * len(slabs) + ("allgather",) * len(whole)
        flights[tag] = (modes, *_exchange_start(modes, slabs + whole, lands, f"grads_{tag}_start"))
        return flights[tag][5]

    def arrived(tag, after):
        return _exchange_wait(*flights[tag][:5], after, f"grads_{tag}_wait")

    _, grad_x, *_, sums_in, _, _, _ = _sequence_step(
        xs, target, rope, (sh_m, sc_m, gt_m, sh_f, sc_f, gt_f), (g_pre_mix, g_post_mix, g_pre_ffn, g_post_ffn),
        w_in_t, w_out_t, relay_ffn, fetch_ffn, send_grads, w_blk_b, b_pool_r, pool_scale_r, conv_w_all, conv_b,
        w_token)

    send_grads("last", [], [sums_in])

    (own_up, own_down), (parts_up, parts_down) = arrived("ffn", flights["last"][5])
    new_up = _sum_adam(own_up, parts_up, w_up[0].T, m_w_up[0].T, v_w_up[0].T, me, "adam_w_up", 352)
    new_down = _sum_adam(own_down, parts_down, w_down[0], m_w_down[0], v_w_down[0], me, "adam_w_down", 176)
    (own_out, *small), (parts_out, *gathered) = arrived("out", new_down[0])
    new_out = _sum_adam(own_out, parts_out, w_out[0], m_w_out[0], v_w_out[0], me, "adam_w_out", 128)
    (own_in,), (parts_in,) = arrived("in", new_out[0])
    new_in = _sum_adam(own_in, parts_in, w_in[0].T, m_w_in[0].T, v_w_in[0].T, me, "adam_w_in", 160)
    big = {"w_up": [a.T for a in new_up], "w_down": new_down, "w_out": new_out, "w_in": [a.T for a in new_in]}

    rep_w = [b_ada, g_pre_mix, g_post_mix, g_pre_ffn, g_post_ffn, w_pool, b_pool, pool_scale, conv_b]
    rep_m = [m_b_ada, m_g_pre_mix, m_g_post_mix, m_g_pre_ffn, m_g_post_ffn, m_w_pool, m_b_pool, m_pool_scale, m_conv_b]
    rep_v = [v_b_ada, v_g_pre_mix, v_g_post_mix, v_g_pre_ffn, v_g_post_ffn, v_w_pool, v_b_pool, v_pool_scale, v_conv_b]
    mine_last, got_last = arrived("last", new_in[0])
    small, gathered = [*mine_last, *small], [*got_last, *gathered]
    totals, dmod_all = _small_sum(small, gathered)
    dconv_tot, loss_tot = totals[5], totals[6]
    rep_out = _small_adam(totals, rep_w, rep_m, rep_v)
    g_rep, d_rep, nm_rep, nv_rep = (rep_out[k::4] for k in range(4))

    fcol = d_ff // N_DEV
    g_cw = lax.dynamic_slice(dconv_tot, (0, me * fcol), (3, fcol))
    d_cw, nm_cw, nv_cw = _adam(conv_w[0], g_cw, m_conv_w[0], v_conv_w[0], "adam_conv_w", 3)

    g_ada, d_ada, nm_ada, nv_ada = _ada_grad_adam(c_all, dmod_all, w_ada[0], m_w_ada[0], v_w_ada[0], 256)

    loss = loss_tot[0, 0]

    def group(k):
        rep = (g_rep, d_rep, nm_rep, nv_rep)[k]
        ada = (g_ada, d_ada, nm_ada, nv_ada)[k][None]
        cw = (g_cw, d_cw, nm_cw, nv_cw)[k][None]
        return [ada, rep[0], rep[1], rep[2], rep[3], rep[4], big["w_in"][k][None], rep[5], rep[6], rep[7],
                big["w_out"][k][None], big["w_up"][k][None], cw, rep[8], big["w_down"][k][None]]

    return (loss, grad_x[None], *group(0), *group(1), *group(2), *group(3))
```

```python
import functools
import math

import jax
import jax.numpy as jnp
from jax import lax
from jax.experimental import pallas as pl
from jax.experimental.pallas import tpu as pltpu

F32 = jnp.float32
BF16 = jnp.bfloat16
MESH = pl.DeviceIdType.MESH

N_DEV = 8
HEAD_DIM = 64
ROT_HALF = 8
ROPE_THETA = 500000.0
POOL_WINDOWS = (2, 4, 8, 16)
DILATIONS = (1, 4, 16)
BLOCK = 128
NORM_EPS = 1e-6
HALO = 16
MASKED = -1e30
ATTN_FWD_UNROLL = 8
ATTN_BWD_UNROLL = 8

ADAM_LR = 0.001
ADAM_B1 = 0.9
ADAM_B2 = 0.999
ADAM_EPS = 1e-08
ADAM_WD = 0.01
ADAM_STEP = 10

V7X_VMEM_LIMIT = 56 * 1024 * 1024
LANES = 128

NT = (((1,), (1,)), ((), ()))
NN = (((1,), (0,)), ((), ()))
TN = (((0,), (0,)), ((), ()))


def _dot(a, b, dims):
    return lax.dot_general(a, b, dims, preferred_element_type=F32)


def _params(sem=None, vmem=V7X_VMEM_LIMIT):
    if sem is None:
        return pltpu.CompilerParams(vmem_limit_bytes=vmem)
    return pltpu.CompilerParams(dimension_semantics=sem, vmem_limit_bytes=vmem)


def _rstd(v):
    return lax.rsqrt(jnp.mean(v * v, axis=-1, keepdims=True) + NORM_EPS)


def _norm_bwd(dn, n, rstd):
    return rstd * (dn - n * jnp.mean(dn * n, axis=-1, keepdims=True))


def _rope_lanes(cs_ref, spread_ref):
    return [lax.dot_general(cs_ref[...], spread_ref[k], NN, preferred_element_type=F32, precision=lax.Precision.HIGHEST)
            for k in range(3)]


def _rope_fwd(p, lanes):
    return p * lanes[0] + pltpu.roll(p, LANES - ROT_HALF, 1) * lanes[1] + pltpu.roll(p, ROT_HALF, 1) * lanes[2]


def _rope_bwd(dp, lanes):
    return dp * lanes[0] + pltpu.roll(dp * lanes[1], ROT_HALF, 1) + pltpu.roll(dp * lanes[2], LANES - ROT_HALF, 1)


def _gelu_parts(v):
    k2 = 2.0 * math.sqrt(2.0 / math.pi)
    c = 0.044715
    v2 = v * v
    s = jax.nn.sigmoid(v * (k2 + (k2 * c) * v2))
    g = v * s
    dg = s + g * (1.0 - s) * (k2 + (3.0 * k2 * c) * v2)
    return g, dg


def _halo_before(i, tile):
    return jnp.maximum(i * (tile // HALO) - 1, 0)


def _premix_inproj(x, sh, sc, g, w_in_t, rope, after, tm):
    s_len, d = x.shape
    n_proj = w_in_t.shape[0]
    n_slab = (n_proj - 256) // LANES

    def body(x_ref, sh_ref, sc_ref, g_ref, w_ref, cs_ref, spread_ref, after_ref, h_ref, up_ref, qkv_ref):
        xv = x_ref[...]
        h = (xv * _rstd(xv) * g_ref[...]) * (1.0 + sc_ref[...]) + sh_ref[...]
        hb = h.astype(BF16)
        h_ref[...] = hb
        up_ref[...] = _dot(hb, w_ref[0:256, :], NT)
        lanes = _rope_lanes(cs_ref, spread_ref)
        for pair in range(n_slab // 2):
            p = _dot(hb, w_ref[256 + 256 * pair:512 + 256 * pair, :], NT)
            for half in range(2):
                ph = p[:, half * LANES:(half + 1) * LANES]
                if pair < 6:
                    ph = _rope_fwd(ph, lanes)
                if pair < 3:
                    ph = ph * (HEAD_DIM ** -0.5)
                qkv_ref[2 * pair + half] = ph

    vec = pl.BlockSpec((1, d), lambda i: (0, 0))
    return pl.pallas_call(
        body, name="premix_inproj", grid=(s_len // tm,),
        in_specs=[pl.BlockSpec((tm, d), lambda i: (i, 0)), vec, vec, vec,
                  pl.BlockSpec((n_proj, d), lambda i: (0, 0)),
                  pl.BlockSpec((tm, rope[0].shape[1]), lambda i: (i, 0)), pl.BlockSpec(rope[1].shape, lambda i: (0, 0, 0)),
                  pl.BlockSpec(memory_space=pl.ANY)],
        out_specs=[pl.BlockSpec((tm, d), lambda i: (i, 0)),
                   pl.BlockSpec((tm, 256), lambda i: (i, 0)),
                   pl.BlockSpec((n_slab, tm, LANES), lambda i: (0, i, 0))],
        out_shape=[jax.ShapeDtypeStruct((s_len, d), BF16),
                   jax.ShapeDtypeStruct((s_len, 256), F32),
                   jax.ShapeDtypeStruct((n_slab, s_len, LANES), F32)],
        compiler_params=_params(("arbitrary",)),
    )(x, sh, sc, g, w_in_t, *rope, after)


def _block_rows(n, r, dil):
    start = n * (BLOCK * dil) + r
    if dil == 1:
        return pl.ds(pl.multiple_of(start, BLOCK), BLOCK)
    return pl.ds(start, BLOCK, stride=dil)


def _band_mask(n):
    ri = lax.broadcasted_iota(jnp.int32, (BLOCK, 2 * BLOCK), 0)
    cj = lax.broadcasted_iota(jnp.int32, (BLOCK, 2 * BLOCK), 1)
    cur = (cj >= BLOCK) & (cj - BLOCK <= ri)
    prev = (cj < BLOCK) & (cj >= ri) & (n > 0)
    return cur | prev


def _attn_fwd(qkv):
    s_len = qkv.shape[1]
    n_g = len(DILATIONS)

    def body(q_ref, k_ref, v_ref, o_ref, lse_ref):
        lane = lax.broadcasted_iota(jnp.int32, (BLOCK, LANES), 1)
        first = lane < HEAD_DIM

        def group(dil):
            nb = s_len // (BLOCK * dil)

            def block(t, carry):
                r, n = t // nb, t % nb
                cur = _block_rows(n, r, dil)
                prev = _block_rows(jnp.maximum(n - 1, 0), r, dil)
                q = q_ref[0, cur, :]
                kcat = jnp.concatenate([k_ref[0, prev, :], k_ref[0, cur, :]], axis=0).astype(BF16)
                vcat = jnp.concatenate([v_ref[0, prev, :], v_ref[0, cur, :]], axis=0).astype(BF16)
                valid = _band_mask(n)
                q2 = jnp.concatenate([jnp.where(first, q, 0.0), jnp.where(first, 0.0, q)], axis=0).astype(BF16)
                s = jnp.where(jnp.concatenate([valid, valid], axis=0), _dot(q2, kcat, NT), MASKED)
                m = jnp.max(s, axis=-1, keepdims=True)
                p = jnp.exp(s - m)
                den = jnp.sum(p, axis=-1, keepdims=True)
                o2 = _dot(p.astype(BF16), vcat, NN) / den
                lse2 = m + jnp.log(den)
                o_ref[0, 0, cur, :] = jnp.where(first, o2[:BLOCK], o2[BLOCK:])
                lse_ref[0, 0, cur, :] = jnp.where(first, lse2[:BLOCK], lse2[BLOCK:])
                return carry

            lax.fori_loop(0, nb * dil, block, 0, unroll=ATTN_FWD_UNROLL)

        for gi, dil in enumerate(DILATIONS):
            pl.when(pl.program_id(0) == gi)(functools.partial(group, dil))

    def slab(base):
        return pl.BlockSpec((1, s_len, LANES), lambda g, s: (base + 2 * g + s, 0, 0))

    out = pl.BlockSpec((1, 1, s_len, LANES), lambda g, s: (g, s, 0, 0))
    shape = jax.ShapeDtypeStruct((n_g, 2, s_len, LANES), F32)
    return pl.pallas_call(
        body, name="attn_fwd", grid=(n_g, 2),
        in_specs=[slab(0), slab(6), slab(12)], out_specs=[out, out], out_shape=[shape, shape],
        compiler_params=_params(("arbitrary", "arbitrary")),
    )(qkv, qkv, qkv)


def _pool_mixed(u, halo, i, tm):
    ue = jnp.concatenate([halo, u], axis=0)
    s2 = ue + pltpu.roll(ue, 1, 0)
    s4 = s2 + pltpu.roll(s2, 2, 0)
    s8 = s4 + pltpu.roll(s4, 4, 0)
    s16 = s8 + pltpu.roll(s8, 8, 0)
    grp = lax.broadcasted_iota(jnp.int32, (tm, 256), 1) // HEAD_DIM
    pick = lambda a, b, c, e: jnp.where(grp == 0, a, jnp.where(grp == 1, b, jnp.where(grp == 2, c, e)))
    win_sum = pick(s2[HALO:], s4[HALO:], s8[HALO:], s16[HALO:])
    pos = (i * tm + lax.broadcasted_iota(jnp.int32, (tm, 256), 0)).astype(F32)
    count = jnp.minimum(pos + 1.0, pick(*[float(w) for w in POOL_WINDOWS]))
    return win_sum / count - u, count


def _mix_out(x, u_pool, o_g, lse_g, w_blk, b_pool, pool_scale, w_out_t, gt_m, g_post_mix, g_pre_ffn, sc_f, sh_f, tm):
    s_len, d = x.shape

    def body(x_ref, u_ref, uh_ref, o_ref, l_ref, wb_ref, bp_ref, ps_ref, wo_ref,
             gt_ref, g1_ref, g2_ref, sc_ref, sh_ref,
             x1_ref, y1_ref, h2_ref, cat_ref, attn_ref, lall_ref):
        (o0, o1, o2), (l0, l1, l2) = (o_ref.at[g] for g in range(3)), (l_ref.at[g] for g in range(3))
        i = pl.program_id(0)
        u = u_ref[...]
        halo = uh_ref[...] * (i > 0).astype(F32)
        mixed, _ = _pool_mixed(u, halo, i, tm)
        y = _dot(mixed.astype(BF16), wb_ref[...], NN) + bp_ref[...]
        pool = y * ps_ref[...]
        attn = []
        for s in range(2):
            la, lb, lc = l0[s], l1[s], l2[s]
            mx = jnp.maximum(jnp.maximum(la, lb), lc)
            ea, eb, ec = jnp.exp(la - mx), jnp.exp(lb - mx), jnp.exp(lc - mx)
            den = ea + eb + ec
            lall_ref[s] = mx + jnp.log(den)
            attn.append((ea / den) * o0[s] + (eb / den) * o1[s] + (ec / den) * o2[s])
        attn = jnp.concatenate(attn, axis=1)
        attn_ref[...] = attn
        cat = jnp.concatenate([pool, attn], axis=1).astype(BF16)
        cat_ref[...] = cat
        y1 = _dot(cat, wo_ref[...], NT)
        y1_ref[...] = y1.astype(BF16)
        x1 = x_ref[...] + gt_ref[...] * (y1 * _rstd(y1) * g1_ref[...])
        x1_ref[...] = x1
        h2 = (x1 * _rstd(x1) * g2_ref[...]) * (1.0 + sc_ref[...]) + sh_ref[...]
        h2_ref[...] = h2.astype(BF16)

    tile = lambda w: pl.BlockSpec((tm, w), lambda i: (i, 0))
    slab = pl.BlockSpec((2, tm, LANES), lambda i: (0, i, 0))
    groups = pl.BlockSpec((len(DILATIONS), 2, tm, LANES), lambda i: (0, 0, i, 0))
    const = lambda a: pl.BlockSpec(a.shape, lambda i: (0,) * a.ndim)
    return pl.pallas_call(
        body, name="mix_out", grid=(s_len // tm,),
        in_specs=[tile(d), tile(256), pl.BlockSpec((HALO, 256), lambda i: (_halo_before(i, tm), 0)),
                  groups, groups,
                  const(w_blk), const(b_pool), const(pool_scale), const(w_out_t),
                  const(gt_m), const(g_post_mix), const(g_pre_ffn), const(sc_f), const(sh_f)],
        out_specs=[tile(d), tile(d), tile(d), tile(512), tile(256), slab],
        out_shape=[jax.ShapeDtypeStruct((s_len, d), F32), jax.ShapeDtypeStruct((s_len, d), BF16),
                   jax.ShapeDtypeStruct((s_len, d), BF16), jax.ShapeDtypeStruct((s_len, 512), BF16),
                   jax.ShapeDtypeStruct((s_len, 256), F32), jax.ShapeDtypeStruct((2, s_len, LANES), F32)],
        compiler_params=_params(("arbitrary",)),
    )(x, u_pool, u_pool, o_g, lse_g, w_blk, b_pool, pool_scale, w_out_t, gt_m, g_post_mix, g_pre_ffn, sc_f, sh_f)


def _conv_gate(gate_ext, cw, cb):
    gc = gate_ext * cw[2:3, :] + pltpu.roll(gate_ext, 1, 0) * cw[1:2, :] + pltpu.roll(gate_ext, 2, 0) * cw[0:1, :]
    return gc[HALO:] + cb


def _ffn_fwd_loss(h2, x1, target, w_up_t, w_down, conv_w, conv_b, gt_f, g_post_ffn, tm, ck):
    s_len, d = x1.shape
    d_ff = w_down.shape[0]
    n_t, n_c = s_len // tm, d_ff // ck

    def body(h_ref, hh_ref, x1_ref, tgt_ref, wg_ref, wv_ref, wd_ref, cw_ref, cb_ref, gt_ref, g_ref,
             gate_ref, a_ref, act_ref, vd_ref, dy2_ref, dout_ref, sums_ref, loss_ref, acc_ref):
        i = pl.program_id(0)

        @pl.when(i == 0)
        def _():
            sums_ref[...] = jnp.zeros_like(sums_ref)
            loss_ref[...] = jnp.zeros_like(loss_ref)
            acc_ref[...] = jnp.zeros_like(acc_ref)

        def finish(live):
            y2 = acc_ref[...]
            rstd = _rstd(y2)
            n = y2 * rstd
            rn = n * g_ref[...]
            err = x1_ref[...] + gt_ref[...] * rn - tgt_ref[...]
            keep = lambda v: jnp.where(live, v, 0.0)
            loss_ref[...] += keep(0.5 * jnp.sum(jnp.mean(err * err, axis=-1, keepdims=True), axis=0, keepdims=True))
            dout = err * (1.0 / d)
            dout_ref[...] = dout
            drn = dout * gt_ref[...]
            sums_ref[0:1, :] += keep(jnp.sum(dout * rn, axis=0, keepdims=True))
            sums_ref[1:2, :] += keep(jnp.sum(drn * n, axis=0, keepdims=True))
            dy2_ref[...] = _norm_bwd(drn * g_ref[...], n, rstd).astype(BF16)

        @pl.when(i < n_t)
        def _():
            h = h_ref[...]
            h_ext = jnp.concatenate([hh_ref[...], h], axis=0)
            row = lax.broadcasted_iota(jnp.int32, (tm + HALO, ck), 0)
            no_halo = (row < HALO) & (i == 0)

            def up(c):
                cs = slice(c * ck, (c + 1) * ck)
                return jnp.where(no_halo, 0.0, _dot(h_ext, wg_ref[cs, :], NT)), _dot(h, wv_ref[cs, :], NT)

            part = None
            nxt = up(0)
            finish(i > 0)
            for c in range(n_c):
                cs = slice(c * ck, (c + 1) * ck)
                gate_ext, val = nxt
                if c + 1 < n_c:
                    nxt = up(c + 1)
                act, dact = _gelu_parts(_conv_gate(gate_ext, cw_ref[:, cs], cb_ref[:, cs]))
                a = (act * val).astype(BF16)
                gate_ref[:, cs] = gate_ext[HALO:].astype(BF16)
                a_ref[:, cs] = a
                act_ref[:, cs] = act.astype(BF16)
                vd_ref[:, cs] = (val * dact).astype(BF16)
                p = _dot(a, wd_ref[cs, :], NN)
                part = p if part is None else part + p
            acc_ref[...] = part

        @pl.when(i == n_t)
        def _():
            finish(True)

    this = lambda i: jnp.minimum(i, n_t - 1)
    before = lambda i: jnp.maximum(i - 1, 0)
    tok = lambda w, at: pl.BlockSpec((tm, w), lambda i: (at(i), 0))
    vec = pl.BlockSpec((1, d), lambda i: (0, 0))
    once = lambda shape, imap: pl.BlockSpec(shape, imap, pipeline_mode=pl.Buffered(1))
    return pl.pallas_call(
        body, name="ffn_fwd_loss", grid=(n_t + 1,),
        in_specs=[tok(d, this), pl.BlockSpec((HALO, d), lambda i: (_halo_before(this(i), tm), 0)),
                  tok(d, before), tok(d, before),
                  once((d_ff, d), lambda i: (0, 0)), once((d_ff, d), lambda i: (1, 0)), once((d_ff, d), lambda i: (0, 0)),
                  pl.BlockSpec((3, d_ff), lambda i: (0, 0)), pl.BlockSpec((1, d_ff), lambda i: (0, 0)), vec, vec],
        out_specs=[tok(d_ff, this)] * 4 + [tok(d, before), tok(d, before), pl.BlockSpec((8, d), lambda i: (0, 0)),
                                          pl.BlockSpec((8, LANES), lambda i: (0, 0))],
        out_shape=[jax.ShapeDtypeStruct((s_len, d_ff), BF16)] * 4
        + [jax.ShapeDtypeStruct((s_len, d), BF16), jax.ShapeDtypeStruct((s_len, d), F32),
           jax.ShapeDtypeStruct((8, d), F32), jax.ShapeDtypeStruct((8, LANES), F32)],
        scratch_shapes=[pltpu.VMEM((tm, d), F32)],
        compiler_params=_params(("arbitrary",)),
    )(h2, h2, x1, target, w_up_t, w_up_t, w_down, conv_w, conv_b, gt_f, g_post_ffn)


def _ffn_bwd_act(dy2, gate, a, act, vd, w_down, tm, tf, ck):
    s_len, d = dy2.shape
    d_ff = w_down.shape[0]
    n_t = s_len // tm
    chunks = [slice(lo, min(lo + ck, tf)) for lo in range(0, tf, ck)]

    def body(dy_ref, g_ref, gh_ref, a_ref, act_ref, vd_ref, wd_ref, dgc_ref, dval_ref, dwd_ref, dconv_ref, acc_ref):
        i = pl.program_id(1)

        @pl.when(i == 0)
        def _():
            acc_ref[...] = jnp.zeros_like(acc_ref)
            dconv_ref[...] = jnp.zeros_like(dconv_ref)

        dy = dy_ref[...]

        def down(cs):
            return _dot(dy, wd_ref[cs, :], NT)

        nxt = down(chunks[0])
        for c, cs in enumerate(chunks):
            width = cs.stop - cs.start
            da = nxt
            if c + 1 < len(chunks):
                nxt = down(chunks[c + 1])
            acc_ref[cs, :] += _dot(a_ref[:, cs], dy, TN)
            row = lax.broadcasted_iota(jnp.int32, (tm + HALO, width), 0)
            gate_ext = jnp.where((row < HALO) & (i == 0), 0.0,
                                 jnp.concatenate([gh_ref[:, cs], g_ref[:, cs]], axis=0).astype(F32))
            dgc = da * vd_ref[:, cs].astype(F32)
            dgc_ref[:, cs] = dgc.astype(BF16)
            dval_ref[:, cs] = (da * act_ref[:, cs].astype(F32)).astype(BF16)
            rows = [jnp.sum(dgc * pltpu.roll(gate_ext, 2 - k, 0)[HALO:], axis=0, keepdims=True) for k in range(2)]
            rows += [jnp.sum(dgc * gate_ext[HALO:], axis=0, keepdims=True), jnp.sum(dgc, axis=0, keepdims=True),
                     jnp.zeros((4, width), F32)]
            dconv_ref[:, cs] += jnp.concatenate(rows, axis=0)

        @pl.when(i == n_t - 1)
        def _():
            dwd_ref[...] = acc_ref[...].astype(BF16)

    tokf = pl.BlockSpec((tm, tf), lambda j, i: (i, j))
    return pl.pallas_call(
        body, name="ffn_bwd_act", grid=(d_ff // tf, n_t),
        in_specs=[pl.BlockSpec((tm, d), lambda j, i: (i, 0)), tokf,
                  pl.BlockSpec((HALO, tf), lambda j, i: (_halo_before(i, tm), j)), tokf, tokf, tokf,
                  pl.BlockSpec((tf, d), lambda j, i: (j, 0))],
        out_specs=[tokf, tokf, pl.BlockSpec((tf, d), lambda j, i: (j, 0)), pl.BlockSpec((8, tf), lambda j, i: (0, j))],
        out_shape=[jax.ShapeDtypeStruct((s_len, d_ff), BF16), jax.ShapeDtypeStruct((s_len, d_ff), BF16),
                   jax.ShapeDtypeStruct((d_ff, d), BF16), jax.ShapeDtypeStruct((8, d_ff), F32)],
        scratch_shapes=[pltpu.VMEM((tf, d), F32)],
        compiler_params=_params(("arbitrary", "arbitrary")),
    )(dy2, gate, gate, a, act, vd, w_down)


def _ffn_bwd_up(dgc, dval, w_up_t, conv_w, tm):
    s_len, d_ff = dgc.shape
    d = w_up_t.shape[1]
    n_t = s_len // tm

    def body(dg_ref, dgn_ref, dv_ref, cw_ref, w_ref, dup_ref, dh_ref):
        i = pl.program_id(0)
        nxt = dgn_ref[...].astype(F32) * (i < n_t - 1).astype(F32)
        ext = jnp.concatenate([dg_ref[...].astype(F32), nxt], axis=0)
        rows = tm + HALO
        dgate = (ext * cw_ref[2:3, :] + pltpu.roll(ext, rows - 1, 0) * cw_ref[1:2, :]
                 + pltpu.roll(ext, rows - 2, 0) * cw_ref[0:1, :])[:tm]
        dup = jnp.concatenate([dgate.astype(BF16), dv_ref[...]], axis=1)
        dup_ref[...] = dup
        dh_ref[...] = _dot(dup, w_ref[...], NN).astype(BF16)

    tokf = pl.BlockSpec((tm, d_ff), lambda i: (i, 0))
    return pl.pallas_call(
        body, name="ffn_bwd_up", grid=(n_t,),
        in_specs=[tokf, pl.BlockSpec((HALO, d_ff), lambda i: (jnp.minimum((i + 1) * (tm // HALO), s_len // HALO - 1), 0)),
                  tokf, pl.BlockSpec((3, d_ff), lambda i: (0, 0)), pl.BlockSpec((2 * d_ff, d), lambda i: (0, 0))],
        out_specs=[pl.BlockSpec((tm, 2 * d_ff), lambda i: (i, 0)), pl.BlockSpec((tm, d), lambda i: (i, 0))],
        out_shape=[jax.ShapeDtypeStruct((s_len, 2 * d_ff), BF16), jax.ShapeDtypeStruct((s_len, d), BF16)],
        compiler_params=_params(("arbitrary",)),
    )(dgc, dgc, dval, conv_w, w_up_t)


def _mix_bwd(dh2, dout, x1, y1, cat, attn, w_out_t, sc_f, g_pre_ffn, gt_m, g_post_mix, after, tm):
    s_len, d = x1.shape
    n_t = s_len // tm

    def body(dh_ref, do_ref, x1_ref, y1_ref, cat_ref, at_ref, wo_ref, sc_ref, g2_ref, gt_ref, g1_ref, after_ref,
             dx1_ref, dpool_ref, dattn_ref, delta_ref, dwo_ref, sums_ref, acc_ref):
        i = pl.program_id(0)
        dh = dh_ref[...].astype(F32)
        x1 = x1_ref[...]
        r2 = _rstd(x1)
        n2 = x1 * r2
        ng = n2 * g2_ref[...]
        dng = dh * (1.0 + sc_ref[...])
        dx1 = do_ref[...] + _norm_bwd(dng * g2_ref[...], n2, r2)
        dx1_ref[...] = dx1
        y1 = y1_ref[...].astype(F32)
        r1 = _rstd(y1)
        n1 = y1 * r1
        drn = dx1 * gt_ref[...]
        dy1 = _norm_bwd(drn * g1_ref[...], n1, r1).astype(BF16)
        dcat = _dot(dy1, wo_ref[...], NN)
        dpool_ref[...] = dcat[:, 0:256]
        lane = lax.broadcasted_iota(jnp.int32, (tm, LANES), 1)
        first = lane < HEAD_DIM
        for s in range(2):
            da = dcat[:, 256 + s * LANES:256 + (s + 1) * LANES]
            dattn_ref[s] = da
            prod = da * at_ref[:, s * LANES:(s + 1) * LANES]
            tot = jnp.sum(prod, axis=-1, keepdims=True)
            lo = jnp.sum(jnp.where(first, prod, 0.0), axis=-1, keepdims=True)
            delta_ref[s] = jnp.where(first, lo, tot - lo)
        dwo = _dot(dy1, cat_ref[...], TN)
        sums = jnp.concatenate(
            [jnp.sum(dh, axis=0, keepdims=True), jnp.sum(dh * ng, axis=0, keepdims=True),
             jnp.sum(dng * n2, axis=0, keepdims=True), jnp.sum(dx1 * (n1 * g1_ref[...]), axis=0, keepdims=True),
             jnp.sum(drn * n1, axis=0, keepdims=True), jnp.zeros((3, d), F32)], axis=0)

        @pl.when(i == 0)
        def _():
            acc_ref[...] = dwo
            sums_ref[...] = sums

        @pl.when(i > 0)
        def _():
            acc_ref[...] += dwo
            sums_ref[...] += sums

        @pl.when(i == n_t - 1)
        def _():
            dwo_ref[...] = acc_ref[...].astype(BF16)

    tile = lambda w: pl.BlockSpec((tm, w), lambda i: (i, 0))
    slab = pl.BlockSpec((2, tm, LANES), lambda i: (0, i, 0))
    vec = pl.BlockSpec((1, d), lambda i: (0, 0))
    return pl.pallas_call(
        body, name="mix_bwd", grid=(n_t,),
        in_specs=[tile(d), tile(d), tile(d), tile(d), tile(512), tile(256),
                  pl.BlockSpec((d, 512), lambda i: (0, 0)), vec, vec, vec, vec, pl.BlockSpec(memory_space=pl.ANY)],
        out_specs=[tile(d), tile(256), slab, slab, pl.BlockSpec((d, 512), lambda i: (0, 0)),
                   pl.BlockSpec((8, d), lambda i: (0, 0))],
        out_shape=[jax.ShapeDtypeStruct((s_len, d), F32), jax.ShapeDtypeStruct((s_len, 256), F32),
                   jax.ShapeDtypeStruct((2, s_len, LANES), F32), jax.ShapeDtypeStruct((2, s_len, LANES), F32),
                   jax.ShapeDtypeStruct((d, 512), BF16), jax.ShapeDtypeStruct((8, d), F32)],
        scratch_shapes=[pltpu.VMEM((d, 512), F32)],
        compiler_params=_params(("arbitrary",)),
    )(dh2, dout, x1, y1, cat, attn, w_out_t, sc_f, g_pre_ffn, gt_m, g_post_mix, after)


def _pool_bwd(dpool, u_pool, w_blk, b_pool, pool_scale, tm):
    s_len = dpool.shape[0]
    n_t = s_len // tm

    def body(dp_ref, dpn_ref, u_ref, uh_ref, wb_ref, bp_ref, ps_ref, du_ref, dwp_ref, sums_ref, acc_ref):
        i = pl.program_id(0)
        u = u_ref[...]
        mixed, _ = _pool_mixed(u, uh_ref[...] * (i > 0).astype(F32), i, tm)
        mixed_b = mixed.astype(BF16)
        y = _dot(mixed_b, wb_ref[...], NN) + bp_ref[...]
        dp = dp_ref[...]
        dy = dp * ps_ref[...]
        dwb = _dot(mixed_b, dy.astype(BF16), TN)
        sums = jnp.concatenate([jnp.sum(dy, axis=0, keepdims=True), jnp.sum(dp * y, axis=0, keepdims=True),
                                jnp.zeros((6, 256), F32)], axis=0)
        dp_ext = jnp.concatenate([dp, dpn_ref[...] * (i < n_t - 1).astype(F32)], axis=0)
        dmix = _dot((dp_ext * ps_ref[...]).astype(BF16), wb_ref[...], NT)
        rows = tm + HALO
        grp = lax.broadcasted_iota(jnp.int32, (rows, 256), 1) // HEAD_DIM
        pick = lambda a, b, c, e: jnp.where(grp == 0, a, jnp.where(grp == 1, b, jnp.where(grp == 2, c, e)))
        pos = (i * tm + lax.broadcasted_iota(jnp.int32, (rows, 256), 0)).astype(F32)
        z = dmix / jnp.minimum(pos + 1.0, pick(*[float(w) for w in POOL_WINDOWS]))
        f2 = z + pltpu.roll(z, rows - 1, 0)
        f4 = f2 + pltpu.roll(f2, rows - 2, 0)
        f8 = f4 + pltpu.roll(f4, rows - 4, 0)
        f16 = f8 + pltpu.roll(f8, rows - 8, 0)
        du_ref[...] = (pick(f2, f4, f8, f16) - dmix)[:tm]

        @pl.when(i == 0)
        def _():
            acc_ref[...] = dwb
            sums_ref[...] = sums

        @pl.when(i > 0)
        def _():
            acc_ref[...] += dwb
            sums_ref[...] += sums

        @pl.when(i == n_t - 1)
        def _():
            full = acc_ref[...]
            for gi in range(len(POOL_WINDOWS)):
                lo = gi * HEAD_DIM
                dwp_ref[gi] = full[lo:lo + HEAD_DIM, lo:lo + HEAD_DIM]

    n_g = len(POOL_WINDOWS)
    tile = pl.BlockSpec((tm, 256), lambda i: (i, 0))
    const = lambda a: pl.BlockSpec(a.shape, lambda i: (0,) * a.ndim)
    return pl.pallas_call(
        body, name="pool_bwd", grid=(n_t,),
        in_specs=[tile, pl.BlockSpec((HALO, 256), lambda i: (jnp.minimum((i + 1) * (tm // HALO), s_len // HALO - 1), 0)),
                  tile, pl.BlockSpec((HALO, 256), lambda i: (_halo_before(i, tm), 0)),
                  const(w_blk), const(b_pool), const(pool_scale)],
        out_specs=[tile, pl.BlockSpec((n_g, HEAD_DIM, HEAD_DIM), lambda i: (0, 0, 0)), pl.BlockSpec((8, 256), lambda i: (0, 0))],
        out_shape=[jax.ShapeDtypeStruct((s_len, 256), F32), jax.ShapeDtypeStruct((n_g, HEAD_DIM, HEAD_DIM), F32),
                   jax.ShapeDtypeStruct((8, 256), F32)],
        scratch_shapes=[pltpu.VMEM((256, 256), F32)],
        compiler_params=_params(("arbitrary",)),
    )(dpool, dpool, u_pool, u_pool, w_blk, b_pool, pool_scale)


def _attn_bwd(qkv, dattn, lse_all, delta, after):
    s_len = qkv.shape[1]
    n_g = len(DILATIONS)

    def body(q_ref, k_ref, v_ref, do_ref, l_ref, dl_ref, after_ref, dq_ref, dk_ref, dv_ref):
        lane = lax.broadcasted_iota(jnp.int32, (BLOCK, LANES), 1)
        first = lane < HEAD_DIM

        def group(dil):
            nb = s_len // (BLOCK * dil)

            def block(t, carry):
                dk_part, dv_part = carry
                r, n = t // nb, t % nb
                cur = _block_rows(n, r, dil)
                prev = _block_rows(jnp.maximum(n - 1, 0), r, dil)
                q = q_ref[0, cur, :]
                do = do_ref[0, cur, :]
                lse = l_ref[0, cur, :]
                dlt = dl_ref[0, cur, :]
                kcat = jnp.concatenate([k_ref[0, prev, :], k_ref[0, cur, :]], axis=0).astype(BF16)
                vcat = jnp.concatenate([v_ref[0, prev, :], v_ref[0, cur, :]], axis=0).astype(BF16)
                valid = _band_mask(n)
                stack = lambda a: jnp.concatenate([jnp.where(first, a, 0.0), jnp.where(first, 0.0, a)], axis=0)
                rows2 = lambda a: jnp.concatenate([a[:, 0:1], a[:, HEAD_DIM:HEAD_DIM + 1]], axis=0)
                q2, do2 = stack(q).astype(BF16), stack(do).astype(BF16)
                valid2 = jnp.concatenate([valid, valid], axis=0)
                p = jnp.where(valid2, jnp.exp(_dot(q2, kcat, NT) - rows2(lse)), 0.0)
                ds = (p * (_dot(do2, vcat, NT) - rows2(dlt))).astype(BF16)
                dq2 = _dot(ds, kcat, NN)
                dq_ref[0, 0, cur, :] = jnp.where(first, dq2[:BLOCK], dq2[BLOCK:])
                dkc = _dot(ds, q2, TN)
                dvc = _dot(p.astype(BF16), do2, TN)
                dk_ref[0, 0, prev, :] = dk_part + dkc[:BLOCK]
                dv_ref[0, 0, prev, :] = dv_part + dvc[:BLOCK]
                dk_ref[0, 0, cur, :] = dkc[BLOCK:]
                dv_ref[0, 0, cur, :] = dvc[BLOCK:]
                return dkc[BLOCK:], dvc[BLOCK:]

            def blocks(tt, carry):
                for u in range(ATTN_BWD_UNROLL):
                    carry = block(tt * ATTN_BWD_UNROLL + u, carry)
                return carry

            zero = jnp.zeros((BLOCK, LANES), F32)
            lax.fori_loop(0, nb * dil // ATTN_BWD_UNROLL, blocks, (zero, zero))

        for gi, dil in enumerate(DILATIONS):
            pl.when(pl.program_id(1) == gi)(functools.partial(group, dil))

    def slab(base):
        return pl.BlockSpec((1, s_len, LANES), lambda s, g: (base + 2 * g + s, 0, 0))

    one = pl.BlockSpec((1, s_len, LANES), lambda s, g: (s, 0, 0))
    out = pl.BlockSpec((1, 1, s_len, LANES), lambda s, g: (g, s, 0, 0))
    shape = jax.ShapeDtypeStruct((n_g, 2, s_len, LANES), F32)
    return pl.pallas_call(
        body, name="attn_bwd", grid=(2, n_g),
        in_specs=[slab(0), slab(6), slab(12), one, one, one, pl.BlockSpec(memory_space=pl.ANY)],
        out_specs=[out, out, out], out_shape=[shape, shape, shape],
        compiler_params=_params(("arbitrary", "arbitrary")),
    )(qkv, qkv, qkv, dattn, lse_all, delta, after)


def _dproj_wgrad_in(du, dqkv, rope, h1, tm, cm):
    s_len = du.shape[0]
    d = h1.shape[1]
    n_proj = 256 + 18 * LANES
    n_t = s_len // tm

    def body(du_ref, dq_ref, dk_ref, dv_ref, cs_ref, spread_ref, h_ref, dproj_ref, dw_ref, acc_ref):
        i = pl.program_id(0)

        @pl.when(i == 0)
        def _():
            acc_ref[...] = jnp.zeros_like(acc_ref)

        dproj_ref[:, 0:256] = du_ref[...].astype(BF16)
        lanes = _rope_lanes(cs_ref, spread_ref)
        col = 256
        for kind, dref in enumerate((dq_ref, dk_ref, dv_ref)):
            for grp in range(3):
                for s in range(2):
                    piece = dref[grp, s]
                    if kind < 2:
                        piece = _rope_bwd(piece, lanes)
                    if kind == 0:
                        piece = piece * (HEAD_DIM ** -0.5)
                    dproj_ref[:, col:col + LANES] = piece.astype(BF16)
                    col += LANES

        for c0 in range(0, n_proj, cm):
            acc_ref[c0:c0 + cm, :] += _dot(dproj_ref[:, c0:c0 + cm], h_ref[...], TN)

        @pl.when(i == n_t - 1)
        def _():
            dw_ref[...] = acc_ref[...].astype(BF16)

    groups = pl.BlockSpec((len(DILATIONS), 2, tm, LANES), lambda i: (0, 0, i, 0))
    return pl.pallas_call(
        body, name="dproj_wgrad_in", grid=(n_t,),
        in_specs=[pl.BlockSpec((tm, 256), lambda i: (i, 0))] + [groups] * 3
        + [pl.BlockSpec((tm, rope[0].shape[1]), lambda i: (i, 0)), pl.BlockSpec(rope[1].shape, lambda i: (0, 0, 0)),
           pl.BlockSpec((tm, d), lambda i: (i, 0))],
        out_specs=[pl.BlockSpec((tm, n_proj), lambda i: (i, 0)), pl.BlockSpec((n_proj, d), lambda i: (0, 0))],
        out_shape=[jax.ShapeDtypeStruct((s_len, n_proj), BF16), jax.ShapeDtypeStruct((n_proj, d), BF16)],
        scratch_shapes=[pltpu.VMEM((n_proj, d), F32)],
        compiler_params=_params(("arbitrary",)),
    )(du, *dqkv, *rope, h1)


def _inproj_bwd(dproj, w_in_t, x, dx1, sc_m, g_pre_mix, after, tm):
    s_len, d = x.shape
    n_proj = w_in_t.shape[0]
    n_t = s_len // tm

    def body(dproj_ref, w_ref, x_ref, dx1_ref, sc_ref, g_ref, after_ref, dx_ref, sums_ref):
        i = pl.program_id(0)
        halves = [slice(0, tm // 2), slice(tm // 2, tm)]
        dhs = [_dot(dproj_ref[rs, :], w_ref[...], NN) for rs in halves]
        sums = None
        for rs, dh in zip(halves, dhs):
            xv = x_ref[rs, :]
            r = _rstd(xv)
            n = xv * r
            dng = dh * (1.0 + sc_ref[...])
            dx_ref[rs, :] = dx1_ref[rs, :] + _norm_bwd(dng * g_ref[...], n, r)
            part = jnp.concatenate([jnp.sum(dh, axis=0, keepdims=True), jnp.sum(dh * (n * g_ref[...]), axis=0, keepdims=True),
                                    jnp.sum(dng * n, axis=0, keepdims=True), jnp.zeros((5, d), F32)], axis=0)
            sums = part if sums is None else sums + part

        @pl.when(i == 0)
        def _():
            sums_ref[...] = sums

        @pl.when(i > 0)
        def _():
            sums_ref[...] += sums

    tile = lambda w: pl.BlockSpec((tm, w), lambda i: (i, 0))
    vec = pl.BlockSpec((1, d), lambda i: (0, 0))
    return pl.pallas_call(
        body, name="inproj_bwd", grid=(n_t,),
        in_specs=[tile(n_proj), pl.BlockSpec((n_proj, d), lambda i: (0, 0)), tile(d), tile(d), vec, vec,
                  pl.BlockSpec(memory_space=pl.ANY)],
        out_specs=[tile(d), pl.BlockSpec((8, d), lambda i: (0, 0))],
        out_shape=[jax.ShapeDtypeStruct((s_len, d), F32), jax.ShapeDtypeStruct((8, d), F32)],
        compiler_params=_params(("arbitrary",)),
    )(dproj, w_in_t, x, dx1, sc_m, g_pre_mix, after)


def _wgrad(a, b, name, tk, tmm):
    s_len, m = a.shape
    n = b.shape[1]
    n_k = s_len // tk

    def body(a_ref, b_ref, o_ref, acc_ref):
        k = pl.program_id(1)
        part = _dot(a_ref[...], b_ref[...], TN)

        @pl.when(k == 0)
        def _():
            acc_ref[...] = part

        @pl.when(k > 0)
        def _():
            acc_ref[...] += part

        @pl.when(k == n_k - 1)
        def _():
            o_ref[...] = acc_ref[...].astype(BF16)

    return pl.pallas_call(
        body, name=name, grid=(m // tmm, n_k),
        in_specs=[pl.BlockSpec((tk, tmm), lambda j, k: (k, j)), pl.BlockSpec((tk, n), lambda j, k: (k, 0))],
        out_specs=pl.BlockSpec((tmm, n), lambda j, k: (j, 0)),
        out_shape=jax.ShapeDtypeStruct((m, n), BF16),
        scratch_shapes=[pltpu.VMEM((tmm, n), F32)],
        compiler_params=_params(("arbitrary", "arbitrary")),
    )(a, b)


def _place():
    return lax.axis_index("x"), lax.axis_index("y"), lax.axis_index("c")


def _peer(k):
    x, y, c = _place()
    bx, by, bc = (k >> 2) & 1, (k >> 1) & 1, k & 1
    return (x ^ bx if bx else x, y ^ by if by else y, c ^ bc if bc else c)


def _index(pos):
    return 4 * pos[0] + 2 * pos[1] + pos[2]


def _entry_exchange(c_rows, w_ada, b_ada, taps, shards):
    d = c_rows.shape[1]
    ncol = w_ada.shape[1]
    n_w = len(shards)

    def body(c_ref, w_ref, b_ref, t_ref, *rest):
        srcs = rest[:n_w]
        call_ref, mod_ref, tall_ref = rest[n_w:n_w + 3]
        outs = rest[n_w + 3:2 * n_w + 3]
        stage_ref, s_send, s_recv, w_send, w_recv, local_sems = rest[2 * n_w + 3:]
        x, y, c = _place()
        here, sibling = (x, y, c), (x, y, 1 - c)
        chips = [(1 - x, y), (x, 1 - y), (1 - x, 1 - y)]
        me = _index(here)

        def small(kind, src, dst, k):
            return pltpu.make_async_remote_copy(src_ref=src, dst_ref=dst, send_sem=s_send.at[kind, k - 1],
                                                recv_sem=s_recv.at[kind, k - 1], device_id=_peer(k), device_id_type=MESH)

        gather = lambda k: small(0, c_ref, call_ref.at[me], k)
        scatter = lambda k: small(1, stage_ref.at[_index(_peer(k))], mod_ref.at[me], k)
        gather_taps = lambda k: small(2, t_ref, tall_ref.at[me], k)

        def rows(w, pos):
            r = shards[w].shape[0]
            return outs[w].at[pl.ds(pl.multiple_of(_index(pos) * r, 16), r), :]

        def block(k, w, pos, to, own=False):
            return pltpu.make_async_remote_copy(
                src_ref=srcs[w] if own else rows(w, pos), dst_ref=rows(w, pos),
                send_sem=w_send.at[k, w], recv_sem=w_recv.at[k, w], device_id=to, device_id_type=MESH)

        call_ref[me] = c_ref[...]
        tall_ref[me] = t_ref[...]
        for k in range(1, N_DEV):
            gather(k).start()
        for k in range(1, N_DEV):
            gather_taps(k).start()
        mine = [pltpu.make_async_copy(srcs[w], rows(w, here), local_sems.at[w]) for w in range(n_w)]
        for cp in mine:
            cp.start()
        first = [block(0, w, here, sibling, own=True) for w in range(n_w)]
        first += [block(1 + j, w, here, (*chip, c), own=True) for j, chip in enumerate(chips) for w in range(n_w)]
        for cp in first:
            cp.start()

        for k in range(1, N_DEV):
            gather(k).wait_recv()
        cv = jnp.concatenate([call_ref[b, 0:1, :] for b in range(N_DEV)], axis=0)
        act = cv * jax.nn.sigmoid(cv)
        mod = lax.dot_general(act, w_ref[...], NN, preferred_element_type=F32,
                              precision=lax.Precision.HIGHEST) + b_ref[:, pl.ds(pl.multiple_of(me * ncol, LANES), ncol)]
        for b in range(N_DEV):
            stage_ref[b] = jnp.broadcast_to(mod[b:b + 1, :], (8, ncol))
        mod_ref[me] = stage_ref[me]
        for k in range(1, N_DEV):
            scatter(k).start()

        passed = []
        for j, chip in enumerate(chips):
            for w in range(n_w):
                block(1 + j, w, (*chip, c), here).wait_recv()
                fwd = block(4 + j, w, (*chip, c), sibling)
                fwd.start()
                passed.append(fwd)
        for w in range(n_w):
            block(0, w, sibling, here).wait_recv()
        for j, chip in enumerate(chips):
            for w in range(n_w):
                block(4 + j, w, (*chip, 1 - c), here).wait_recv()
        for k in range(1, N_DEV):
            scatter(k).wait_recv()
            gather_taps(k).wait_recv()
        for cp in first + passed:
            cp.wait_send()
        for k in range(1, N_DEV):
            gather(k).wait_send()
            scatter(k).wait_send()
            gather_taps(k).wait_send()
        for cp in mine:
            cp.wait()

    vmem, hbm = pl.BlockSpec(memory_space=pltpu.VMEM), pl.BlockSpec(memory_space=pltpu.HBM)
    out = pl.pallas_call(
        body, name="entry_exchange",
        in_specs=[vmem] * 4 + [hbm] * n_w, out_specs=[vmem] * 3 + [hbm] * n_w,
        out_shape=[jax.ShapeDtypeStruct((N_DEV, 8, d), F32), jax.ShapeDtypeStruct((N_DEV, 8, ncol), F32),
                   jax.ShapeDtypeStruct((N_DEV,) + taps.shape, F32)]
        + [jax.ShapeDtypeStruct((N_DEV * s.shape[0], s.shape[1]), s.dtype) for s in shards],
        scratch_shapes=[pltpu.VMEM((N_DEV, 8, ncol), F32), pltpu.SemaphoreType.DMA((3, N_DEV - 1)),
                        pltpu.SemaphoreType.DMA((3, N_DEV - 1)), pltpu.SemaphoreType.DMA((N_DEV - 1, n_w)),
                        pltpu.SemaphoreType.DMA((N_DEV - 1, n_w)), pltpu.SemaphoreType.DMA((n_w,))],
        compiler_params=_params(),
    )(c_rows, w_ada, b_ada, taps, *shards)
    return out[0], out[1], out[2], out[3:]


def _peer_copies(mode, srcs, lands, send_sems, recv_sems):
    if mode in ("gather_ici", "gather_d2d"):
        x, y, c = _place()
        sibling = (x, y, 1 - c)
        chips = [(1 - x, y), (x, 1 - y), (1 - x, 1 - y)]
        n = len(lands)

        def rows(w, pos):
            r = lands[w].shape[0] // N_DEV
            return lands[w].at[pl.ds(pl.multiple_of(_index(pos) * r, 16), r), :]

        def copy(k, w, src, dst, to):
            return pltpu.make_async_remote_copy(src_ref=src, dst_ref=dst, send_sem=send_sems.at[k * n + w],
                                                recv_sem=recv_sems.at[k * n + w], device_id=to, device_id_type=MESH)

        if mode == "gather_ici":
            targets = [sibling] + [(*chip, c) for chip in chips]
            return [copy(k, w, rows(w, (x, y, c)), rows(w, (x, y, c)), to) for k, to in enumerate(targets) for w in range(n)]
        return [copy(j, w, rows(w, (*chip, c)), rows(w, (*chip, c)), sibling)
                for j, chip in enumerate(chips) for w in range(n)]
    me = _index(_place())
    modes = (mode,) * len(srcs) if isinstance(mode, str) else mode
    copies = []
    for k in range(1, N_DEV):
        peer = _peer(k)
        for w, (src, land) in enumerate(zip(srcs, lands)):
            if modes[w] == "gather":
                r = src.shape[0]
                dst = land.at[pl.ds(pl.multiple_of(me * r, 16), r), :]
            elif modes[w] == "allgather":
                dst = land.at[me]
            else:
                r = src.shape[0] // N_DEV
                src = src.at[pl.ds(pl.multiple_of(_index(peer) * r, 16), r), :]
                dst = land.at[me]
            copies.append(pltpu.make_async_remote_copy(
                src_ref=src, dst_ref=dst, send_sem=send_sems.at[(k - 1) * len(srcs) + w],
                recv_sem=recv_sems.at[(k - 1) * len(srcs) + w],
                device_id=peer, device_id_type=MESH))
    return copies


def _landing_zone(src, me, name, tr):
    r, cols = src.shape
    n_t = r // tr

    def body(me_ref, s_ref, o_ref):
        o_ref[...] = s_ref[...].astype(BF16)

    return pl.pallas_call(
        body, name=name, out_shape=jax.ShapeDtypeStruct((N_DEV * r, cols), BF16),
        grid_spec=pltpu.PrefetchScalarGridSpec(
            num_scalar_prefetch=1, grid=(n_t,), in_specs=[pl.BlockSpec((tr, cols), lambda i, me_ref: (i, 0))],
            out_specs=pl.BlockSpec((tr, cols), lambda i, me_ref: (me_ref[0] * n_t + i, 0))),
        compiler_params=_params(("arbitrary",)),
    )(me.reshape(1).astype(jnp.int32), src)


def _exchange_start(mode, srcs, lands, name):
    n_s, n_a = len(srcs), len(srcs) + len(lands)
    n_cp = _COPIES_PER_ARRAY.get(mode, N_DEV - 1) * len(lands)

    def body(*refs):
        for cp in _peer_copies(mode, refs[:n_s], refs[n_s:n_a], refs[n_a], refs[n_a + 1]):
            cp.start()

    hbm, sem = pl.BlockSpec(memory_space=pltpu.HBM), pl.BlockSpec(memory_space=pltpu.SEMAPHORE)
    arrays = list(srcs) + list(lands)
    out = pl.pallas_call(
        body, name=name,
        out_shape=(pltpu.SemaphoreType.DMA((n_cp,)), pltpu.SemaphoreType.DMA((n_cp,)),
                   *[pltpu.HBM(a.shape, a.dtype) for a in arrays]),
        in_specs=[hbm] * n_a, out_specs=(sem, sem, *[hbm] * n_a),
        input_output_aliases={i: 2 + i for i in range(n_a)},
        compiler_params=pltpu.CompilerParams(has_side_effects=pltpu.SideEffectType.DATAFLOW_SIDE_EFFECTING),
    )(*[pltpu.with_memory_space_constraint(a, pltpu.HBM) for a in arrays])
    return out[0], out[1], out[2:2 + n_s], out[2 + n_s:2 + n_a], out[2]


_COPIES_PER_ARRAY = {"gather_ici": 4, "gather_d2d": 3}


def _exchange_wait(mode, send_sems, recv_sems, srcs, lands, after, name):
    n_s, n_a = len(srcs), len(srcs) + len(lands)

    def body(*refs):
        copies = _peer_copies(mode, refs[:n_s], refs[n_s:n_a], refs[n_a], refs[n_a + 1])
        for cp in copies:
            cp.wait_send()
        for cp in copies:
            cp.wait_recv()

    hbm, sem = pl.BlockSpec(memory_space=pltpu.HBM), pl.BlockSpec(memory_space=pltpu.SEMAPHORE)
    arrays = list(srcs) + list(lands)
    out = pl.pallas_call(
        body, name=name, out_shape=tuple(pltpu.HBM(a.shape, a.dtype) for a in arrays),
        in_specs=[hbm] * n_a + [sem, sem, pl.BlockSpec(memory_space=pl.ANY)], out_specs=tuple([hbm] * n_a),
        input_output_aliases={i: i for i in range(n_a)},
        compiler_params=pltpu.CompilerParams(has_side_effects=pltpu.SideEffectType.DATAFLOW_SIDE_EFFECTING),
    )(*arrays, send_sems, recv_sems, after)
    return out[:n_s], out[n_s:]


SMALL_WEIGHTS = ("b_ada", "g_pre_mix", "g_post_mix", "g_pre_ffn", "g_post_ffn", "w_pool", "b_pool", "pool_scale", "conv_b")


MOD_ROWS = ((0, 0), (0, 1), (1, 3), (1, 0), (1, 1), (2, 0))


def _small_sum(mine, gathered):
    n_l = len(mine)
    d = mine[0].shape[1]

    def body(*refs):
        loc, got = refs[:n_l], refs[n_l:2 * n_l]
        tot_refs, dmod_ref = refs[2 * n_l:3 * n_l], refs[3 * n_l]
        me = _index(_place())
        part = lambda a, dev: jnp.where(dev == me, loc[a][...], got[a][dev])
        for a in range(n_l):
            tot = part(a, 0)
            for dev in range(1, N_DEV):
                tot = tot + part(a, dev)
            tot_refs[a][...] = tot
        for dev in range(N_DEV):
            for k, (a, r) in enumerate(MOD_ROWS):
                dmod_ref[dev:dev + 1, k * d:(k + 1) * d] = part(a, dev)[r:r + 1, :]

    vmem = pl.BlockSpec(memory_space=pltpu.VMEM)
    out = pl.pallas_call(
        body, name="small_sum", in_specs=[vmem] * (2 * n_l), out_specs=[vmem] * (n_l + 1),
        out_shape=[jax.ShapeDtypeStruct(a.shape, F32) for a in mine] + [jax.ShapeDtypeStruct((N_DEV, 6 * d), F32)],
        compiler_params=_params(),
    )(*mine, *gathered)
    return out[:n_l], out[n_l]


def _small_adam(totals, weights, moms, vels):
    n_t, n_w = len(totals), len(weights)

    def body(*refs):
        t_in, t_mix, t_ffn, t_pool, t_blk, t_conv, _ = (r[...] for r in refs[:n_t])
        w_refs, m_refs, v_refs = (refs[n_t + k * n_w:n_t + (k + 1) * n_w] for k in range(3))
        outs = refs[n_t + 3 * n_w:]

        def update(idx, g, at=()):
            sel = lambda ref: ref.at[at] if at else ref
            delta, nm, nv = _adam_math(sel(w_refs[idx])[...], g, sel(m_refs[idx])[...], sel(v_refs[idx])[...])
            for k, val in enumerate((g, delta, nm, nv)):
                sel(outs[4 * idx + k])[...] = val

        tots = (t_in, t_mix, t_ffn)
        update(0, jnp.concatenate([tots[a][r:r + 1] for a, r in MOD_ROWS], axis=1))
        update(1, t_in[2:3])
        update(2, t_mix[4:5])
        update(3, t_mix[2:3])
        update(4, t_ffn[1:2])
        for gi in range(len(POOL_WINDOWS)):
            update(5, t_blk[gi], at=(0, gi))
        update(6, jnp.concatenate([t_pool[0:1, gi * HEAD_DIM:(gi + 1) * HEAD_DIM] for gi in range(len(POOL_WINDOWS))], axis=0),
               at=(0,))
        update(7, t_pool[1:2])
        update(8, t_conv[3:4])

    vmem = pl.BlockSpec(memory_space=pltpu.VMEM)
    return pl.pallas_call(
        body, name="small_adam", in_specs=[vmem] * (n_t + 3 * n_w), out_specs=[vmem] * (4 * n_w),
        out_shape=[jax.ShapeDtypeStruct(w.shape, F32) for w in weights for _ in range(4)],
        compiler_params=_params(),
    )(*totals, *weights, *moms, *vels)


def _adam_math(w, g, m, v):
    m = ADAM_B1 * m + (1.0 - ADAM_B1) * g
    v = ADAM_B2 * v + (1.0 - ADAM_B2) * (g * g)
    m_hat = m / (1.0 - ADAM_B1 ** ADAM_STEP)
    v_hat = v / (1.0 - ADAM_B2 ** ADAM_STEP)
    delta = -ADAM_LR * (m_hat / (jnp.sqrt(v_hat) + ADAM_EPS) + ADAM_WD * w)
    return delta, m, v


def _adam(w, g, m, v, name, tr):
    rows, cols = w.shape

    def body(w_ref, g_ref, m_ref, v_ref, d_ref, nm_ref, nv_ref):
        d_ref[...], nm_ref[...], nv_ref[...] = _adam_math(w_ref[...], g_ref[...], m_ref[...], v_ref[...])

    spec = pl.BlockSpec((tr, cols), lambda i: (i, 0))
    shape = jax.ShapeDtypeStruct((rows, cols), F32)
    return pl.pallas_call(
        body, name=name, grid=(rows // tr,), in_specs=[spec] * 4, out_specs=[spec] * 3,
        out_shape=[shape] * 3, compiler_params=_params(("arbitrary",)),
    )(w, g, m, v)


def _sum_adam(own, parts, w, m, v, me, name, tr):
    _, rows, cols = parts.shape
    turned = w.shape == (cols, rows) and rows != cols
    assert tr == rows or not turned
    n_t = rows // tr

    def body(me_ref, own_ref, p_ref, w_ref, m_ref, v_ref, g_ref, d_ref, nm_ref, nv_ref):
        part = lambda dev: jnp.where(dev == me_ref[0], own_ref[...], p_ref[dev]).astype(F32)
        g = part(0)
        for dev in range(1, N_DEV):
            g = g + part(dev)
        g = g.T if turned else g
        g_ref[...] = g
        d_ref[...], nm_ref[...], nv_ref[...] = _adam_math(w_ref[...], g, m_ref[...], v_ref[...])

    spec = pl.BlockSpec((cols, rows) if turned else (tr, cols), lambda i, me_ref: (i, 0))
    shape = jax.ShapeDtypeStruct(w.shape, F32)
    return pl.pallas_call(
        body, name=name, out_shape=[shape] * 4,
        grid_spec=pltpu.PrefetchScalarGridSpec(
            num_scalar_prefetch=1, grid=(n_t,),
            in_specs=[pl.BlockSpec((tr, cols), lambda i, me_ref: (me_ref[0] * n_t + i, 0)),
                      pl.BlockSpec((N_DEV, tr, cols), lambda i, me_ref: (0, i, 0)), spec, spec, spec],
            out_specs=[spec] * 4),
        compiler_params=_params(("arbitrary",)),
    )(me.reshape(1).astype(jnp.int32), own, parts, w, m, v)


def _ada_grad_adam(c_all, dmod_all, w, m, v, tr):
    rows, cols = w.shape

    def body(c_ref, dm_ref, w_ref, m_ref, v_ref, g_ref, d_ref, nm_ref, nv_ref):
        cv = c_ref[...]
        act = cv * jax.nn.sigmoid(cv)
        dmod = dm_ref[:, pl.ds(pl.multiple_of(_index(_place()) * cols, LANES), cols)]
        g = lax.dot_general(act, dmod, TN, preferred_element_type=F32, precision=lax.Precision.HIGHEST)
        g_ref[...] = g
        d_ref[...], nm_ref[...], nv_ref[...] = _adam_math(w_ref[...], g, m_ref[...], v_ref[...])

    spec = pl.BlockSpec((tr, cols), lambda i: (i, 0))
    shape = jax.ShapeDtypeStruct((rows, cols), F32)
    return pl.pallas_call(
        body, name="ada_grad_adam", grid=(rows // tr,),
        in_specs=[pl.BlockSpec((N_DEV, tr), lambda i: (0, i)), pl.BlockSpec(dmod_all.shape, lambda i: (0, 0)), spec, spec, spec],
        out_specs=[spec] * 4, out_shape=[shape] * 4, compiler_params=_params(("arbitrary",)),
    )(c_all, dmod_all, w, m, v)


def _rope_tables(positions):
    inv_freq = ROPE_THETA ** (-jnp.arange(0, 2 * ROT_HALF, 2, dtype=F32) / (2 * ROT_HALF))
    ang = positions.astype(F32)[:, None] * inv_freq
    rows = jnp.concatenate([jnp.cos(ang), jnp.sin(ang), jnp.ones_like(ang)], axis=1)
    spread = [[[0.0] * LANES for _ in range(3 * ROT_HALF)] for _ in range(3)]
    for lane in range(LANES):
        p, j = lane % HEAD_DIM, lane % ROT_HALF
        if p < ROT_HALF:
            spread[0][j][lane] = 1.0
            spread[1][ROT_HALF + j][lane] = -1.0
        elif p < 2 * ROT_HALF:
            spread[0][j][lane] = 1.0
            spread[2][ROT_HALF + j][lane] = 1.0
        else:
            spread[0][2 * ROT_HALF][lane] = 1.0
    return rows, jnp.array(spread, F32)


def _pad_rows(a, rows):
    return jnp.pad(a, ((0, rows - a.shape[0]), (0, 0)))


def _sequence_step(xs, target, rope, mods, gains, w_in_t, w_out_t, relay_ffn, fetch_ffn, send_grads, w_blk_b, b_pool_r,
                   pool_scale_r, conv_w_all, conv_b, after):
    sh_m, sc_m, gt_m, sh_f, sc_f, gt_f = mods
    g_pre_mix, g_post_mix, g_pre_ffn, g_post_ffn = gains
    h1, u_pool, qkv = _premix_inproj(xs, sh_m, sc_m, g_pre_mix, w_in_t, rope, after, tm=512)
    o_g, lse_g = _attn_fwd(qkv)
    x1, y1, h2, cat, attn, lse_all = _mix_out(xs, u_pool, o_g, lse_g, w_blk_b, b_pool_r, pool_scale_r, w_out_t,
                                              gt_m, g_post_mix, g_pre_ffn, sc_f, sh_f, tm=256)
    relay_ffn(x1)
    w_up_t, w_down_f = fetch_ffn(x1)
    gate, a_ffn, act, vd, dy2, dout, sums_ffn, loss_loc = _ffn_fwd_loss(h2, x1, target, w_up_t, w_down_f, conv_w_all, conv_b,
                                                              gt_f, g_post_ffn, tm=256, ck=256)

    dgc, dval, dw_down, dconv = _ffn_bwd_act(dy2, gate, a_ffn, act, vd, w_down_f, tm=512, tf=1408, ck=256)
    dup, dh2 = _ffn_bwd_up(dgc, dval, w_up_t, conv_w_all, tm=256)
    dw_up_t = _wgrad(dup, h2, "wgrad_up", tk=2048, tmm=1408)
    token = send_grads("ffn", [dw_up_t, dw_down], [])
    dx1, dpool, dattn, delta, dw_out_t, sums_mix = _mix_bwd(dh2, dout, x1, y1, cat, attn, w_out_t, sc_f,
                                                           g_pre_ffn, gt_m, g_post_mix, token, tm=256)
    du, dw_blk, sums_pool = _pool_bwd(dpool, u_pool, w_blk_b, b_pool_r, pool_scale_r, tm=512)
    token = send_grads("out", [dw_out_t], [sums_mix, sums_ffn, sums_pool, dw_blk, dconv, loss_loc])
    dproj, dw_in_t = _dproj_wgrad_in(du, _attn_bwd(qkv, dattn, lse_all, delta, token), rope, h1, tm=512, cm=512)
    token = send_grads("in", [dw_in_t], [])
    grad_x, sums_in = _inproj_bwd(dproj, w_in_t, xs, dx1, sc_m, g_pre_mix, token, tm=256)
    return (loss_loc, grad_x, dw_in_t, dw_out_t, dw_up_t, dw_down, dw_blk, dconv,
            sums_in, sums_mix, sums_ffn, sums_pool)


def kernel(x, c, positions, w_ada, b_ada, g_pre_mix, g_post_mix, g_pre_ffn, g_post_ffn, w_in, w_pool, b_pool, pool_scale, w_out, w_up, conv_w, conv_b, w_down, loss_target, m_w_ada, m_b_ada, m_g_pre_mix, m_g_post_mix, m_g_pre_ffn, m_g_post_ffn, m_w_in, m_w_pool, m_b_pool, m_pool_scale, m_w_out, m_w_up, m_conv_w, m_conv_b, m_w_down, v_w_ada, v_b_ada, v_g_pre_mix, v_g_post_mix, v_g_pre_ffn, v_g_post_ffn, v_w_in, v_w_pool, v_b_pool, v_pool_scale, v_w_out, v_w_up, v_conv_w, v_conv_b, v_w_down):
    s_len, d = x.shape[1], x.shape[2]
    d_ff = w_down.shape[1] * N_DEV
    me = _index(_place())
    xs, target = x[0], loss_target[0]

    c_all, mod, taps_all, (w_in_t, w_out_t) = _entry_exchange(
        jnp.broadcast_to(c, (8, d)), w_ada[0], b_ada, _pad_rows(conv_w[0], 8),
        [w_in[0].T.astype(BF16), w_out[0].T.astype(BF16)])
    c_all = c_all[:, 0, :]
    conv_w_all = jnp.transpose(taps_all[:, :3, :], (1, 0, 2)).reshape(3, d_ff)
    sh_m, sc_m, gt_m, sh_f, sc_f, gt_f = [mod[:, 0, :].reshape(1, -1)[:, k * d:(k + 1) * d] for k in range(6)]

    rope = _rope_tables(positions[0])
    w_blk = jnp.zeros((256, 256), F32)
    for gi in range(4):
        w_blk = lax.dynamic_update_slice(w_blk, w_pool[0, gi], (gi * HEAD_DIM, gi * HEAD_DIM))
    w_blk_b = w_blk.astype(BF16)
    b_pool_r, pool_scale_r = b_pool.reshape(1, 256), pool_scale.reshape(1, 256)

    lands = [_landing_zone(s, me, "land_" + nm, 176) for s, nm in ((w_up[0].T, "w_up"), (w_down[0], "w_down"))]
    w_in_t, conv_w_all, *lands = lax.optimization_barrier((w_in_t, conv_w_all, *lands))
    w_send, w_recv, w_src, w_land, w_token = _exchange_start("gather_ici", [], lands, "ffn_weights_ici_start")
    relay = []

    def relay_ffn(after):
        _, blocks = _exchange_wait("gather_ici", w_send, w_recv, w_src, w_land, after, "ffn_weights_ici_wait")
        relay.extend(_exchange_start("gather_d2d", [], blocks, "ffn_weights_d2d_start"))

    def fetch_ffn(after):
        return _exchange_wait("gather_d2d", relay[0], relay[1], [], relay[3], after, "ffn_weights_d2d_wait")[1]

    flights = {}

    def send_grads(tag, slabs, whole):
        lands = [lax.empty((N_DEV, g.shape[0] // N_DEV, g.shape[1]), g.dtype) for g in slabs]
        lands += [lax.empty((N_DEV,) + a.shape, F32) for a in whole]
        modes = ("scatter",) * len(slabs) + ("allgather",) * len(whole)
        flights[tag] = (modes, *_exchange_start(modes, slabs + whole, lands, f"grads_{tag}_start"))
        return flights[tag][5]

    def arrived(tag, after):
        return _exchange_wait(*flights[tag][:5], after, f"grads_{tag}_wait")

    _, grad_x, *_, sums_in, _, _, _ = _sequence_step(
        xs, target, rope, (sh_m, sc_m, gt_m, sh_f, sc_f, gt_f), (g_pre_mix, g_post_mix, g_pre_ffn, g_post_ffn),
        w_in_t, w_out_t, relay_ffn, fetch_ffn, send_grads, w_blk_b, b_pool_r, pool_scale_r, conv_w_all, conv_b,
        w_token)

    send_grads("last", [], [sums_in])

    (own_up, own_down), (parts_up, parts_down) = arrived("ffn", flights["last"][5])
    new_up = _sum_adam(own_up, parts_up, w_up[0].T, m_w_up[0].T, v_w_up[0].T, me, "adam_w_up", 352)
    new_down = _sum_adam(own_down, parts_down, w_down[0], m_w_down[0], v_w_down[0], me, "adam_w_down", 176)
    (own_out, *small), (parts_out, *gathered) = arrived("out", new_down[0])
    new_out = _sum_adam(own_out, parts_out, w_out[0], m_w_out[0], v_w_out[0], me, "adam_w_out", 128)
    (own_in,), (parts_in,) = arrived("in", new_out[0])
    new_in = _sum_adam(own_in, parts_in, w_in[0].T, m_w_in[0].T, v_w_in[0].T, me, "adam_w_in", 160)
    big = {"w_up": [a.T for a in new_up], "w_down": new_down, "w_out": new_out, "w_in": [a.T for a in new_in]}

    rep_w = [b_ada, g_pre_mix, g_post_mix, g_pre_ffn, g_post_ffn, w_pool, b_pool, pool_scale, conv_b]
    rep_m = [m_b_ada, m_g_pre_mix, m_g_post_mix, m_g_pre_ffn, m_g_post_ffn, m_w_pool, m_b_pool, m_pool_scale, m_conv_b]
    rep_v = [v_b_ada, v_g_pre_mix, v_g_post_mix, v_g_pre_ffn, v_g_post_ffn, v_w_pool, v_b_pool, v_pool_scale, v_conv_b]
    mine_last, got_last = arrived("last", new_in[0])
    small, gathered = [*mine_last, *small], [*got_last, *gathered]
    totals, dmod_all = _small_sum(small, gathered)
    dconv_tot, loss_tot = totals[5], totals[6]
    rep_out = _small_adam(totals, rep_w, rep_m, rep_v)
    g_rep, d_rep, nm_rep, nv_rep = (rep_out[k::4] for k in range(4))

    fcol = d_ff // N_DEV
    g_cw = lax.dynamic_slice(dconv_tot, (0, me * fcol), (3, fcol))
    d_cw, nm_cw, nv_cw = _adam(conv_w[0], g_cw, m_conv_w[0], v_conv_w[0], "adam_conv_w", 3)

    g_ada, d_ada, nm_ada, nv_ada = _ada_grad_adam(c_all, dmod_all, w_ada[0], m_w_ada[0], v_w_ada[0], 256)

    loss = loss_tot[0, 0]

    def group(k):
        rep = (g_rep, d_rep, nm_rep, nv_rep)[k]
        ada = (g_ada, d_ada, nm_ada, nv_ada)[k][None]
        cw = (g_cw, d_cw, nm_cw, nv_cw)[k][None]
        return [ada, rep[0], rep[1], rep[2], rep[3], rep[4], big["w_in"][k][None], rep[5], rep[6], rep[7],
                big["w_out"][k][None], big["w_up"][k][None], cw, rep[8], big["w_down"][k][None]]

    return (loss, grad_x[None], *group(0), *group(1), *group(2), *group(3))
```

```python
import functools
import math

import jax
import jax.numpy as jnp
from jax import lax
from jax.experimental import pallas as pl
from jax.experimental.pallas import tpu as pltpu

F32 = jnp.float32
BF16 = jnp.bfloat16
MESH = pl.DeviceIdType.MESH

N_DEV = 8
HEAD_DIM = 64
ROT_HALF = 8
ROPE_THETA = 500000.0
POOL_WINDOWS = (2, 4, 8, 16)
DILATIONS = (1, 4, 16)
BLOCK = 128
NORM_EPS = 1e-6
HALO = 16
MASKED = -1e30
ATTN_FWD_UNROLL = 8
ATTN_BWD_UNROLL = 8

ADAM_LR = 0.001
ADAM_B1 = 0.9
ADAM_B2 = 0.999
ADAM_EPS = 1e-08
ADAM_WD = 0.01
ADAM_STEP = 10

V7X_VMEM_LIMIT = 56 * 1024 * 1024
LANES = 128

NT = (((1,), (1,)), ((), ()))
NN = (((1,), (0,)), ((), ()))
TN = (((0,), (0,)), ((), ()))


def _dot(a, b, dims):
    return lax.dot_general(a, b, dims, preferred_element_type=F32)


def _params(sem=None, vmem=V7X_VMEM_LIMIT):
    if sem is None:
        return pltpu.CompilerParams(vmem_limit_bytes=vmem)
    return pltpu.CompilerParams(dimension_semantics=sem, vmem_limit_bytes=vmem)


def _rstd(v):
    return lax.rsqrt(jnp.mean(v * v, axis=-1, keepdims=True) + NORM_EPS)


def _norm_bwd(dn, n, rstd):
    return rstd * (dn - n * jnp.mean(dn * n, axis=-1, keepdims=True))


def _rope_lanes(cs_ref, spread_ref):
    return [lax.dot_general(cs_ref[...], spread_ref[k], TN, preferred_element_type=F32, precision=lax.Precision.HIGHEST)
            for k in range(3)]


def _rope_fwd(p, lanes):
    return p * lanes[0] + pltpu.roll(p, LANES - ROT_HALF, 1) * lanes[1] + pltpu.roll(p, ROT_HALF, 1) * lanes[2]


def _rope_bwd(dp, lanes):
    return dp * lanes[0] + pltpu.roll(dp * lanes[1], ROT_HALF, 1) + pltpu.roll(dp * lanes[2], LANES - ROT_HALF, 1)


def _gelu_parts(v):
    k2 = 2.0 * math.sqrt(2.0 / math.pi)
    c = 0.044715
    v2 = v * v
    s = jax.nn.sigmoid(v * (k2 + (k2 * c) * v2))
    g = v * s
    dg = s + g * (1.0 - s) * (k2 + (3.0 * k2 * c) * v2)
    return g, dg


def _halo_before(i, tile):
    return jnp.maximum(i * (tile // HALO) - 1, 0)


def _premix_inproj(x, sh, sc, g, w_in_t, rope, after, tm):
    s_len, d = x.shape
    n_proj = w_in_t.shape[0]
    n_slab = (n_proj - 256) // LANES

    def body(x_ref, sh_ref, sc_ref, g_ref, w_ref, cs_ref, spread_ref, after_ref, h_ref, up_ref, qkv_ref):
        xv = x_ref[...]
        h = (xv * _rstd(xv) * g_ref[...]) * (1.0 + sc_ref[...]) + sh_ref[...]
        hb = h.astype(BF16)
        h_ref[...] = hb
        up_ref[...] = _dot(hb, w_ref[0:256, :], NT)
        lanes = _rope_lanes(cs_ref, spread_ref)
        for pair in range(n_slab // 2):
            p = _dot(hb, w_ref[256 + 256 * pair:512 + 256 * pair, :], NT)
            for half in range(2):
                ph = p[:, half * LANES:(half + 1) * LANES]
                if pair < 6:
                    ph = _rope_fwd(ph, lanes)
                if pair < 3:
                    ph = ph * (HEAD_DIM ** -0.5)
                qkv_ref[2 * pair + half] = ph

    vec = pl.BlockSpec((1, d), lambda i: (0, 0))
    return pl.pallas_call(
        body, name="premix_inproj", grid=(s_len // tm,),
        in_specs=[pl.BlockSpec((tm, d), lambda i: (i, 0)), vec, vec, vec,
                  pl.BlockSpec((n_proj, d), lambda i: (0, 0)),
                  pl.BlockSpec((rope[0].shape[0], tm), lambda i: (0, i)), pl.BlockSpec(rope[1].shape, lambda i: (0, 0, 0)),
                  pl.BlockSpec(memory_space=pl.ANY)],
        out_specs=[pl.BlockSpec((tm, d), lambda i: (i, 0)),
                   pl.BlockSpec((tm, 256), lambda i: (i, 0)),
                   pl.BlockSpec((n_slab, tm, LANES), lambda i: (0, i, 0))],
        out_shape=[jax.ShapeDtypeStruct((s_len, d), BF16),
                   jax.ShapeDtypeStruct((s_len, 256), F32),
                   jax.ShapeDtypeStruct((n_slab, s_len, LANES), F32)],
        compiler_params=_params(("arbitrary",)),
    )(x, sh, sc, g, w_in_t, *rope, after)


def _block_rows(n, r, dil):
    start = n * (BLOCK * dil) + r
    if dil == 1:
        return pl.ds(pl.multiple_of(start, BLOCK), BLOCK)
    return pl.ds(start, BLOCK, stride=dil)


def _band_mask(n):
    ri = lax.broadcasted_iota(jnp.int32, (BLOCK, 2 * BLOCK), 0)
    cj = lax.broadcasted_iota(jnp.int32, (BLOCK, 2 * BLOCK), 1)
    cur = (cj >= BLOCK) & (cj - BLOCK <= ri)
    prev = (cj < BLOCK) & (cj >= ri) & (n > 0)
    return cur | prev


def _attn_fwd(qkv):
    s_len = qkv.shape[1]
    n_g = len(DILATIONS)

    def body(q_ref, k_ref, v_ref, o_ref, lse_ref):
        lane = lax.broadcasted_iota(jnp.int32, (BLOCK, LANES), 1)
        first = lane < HEAD_DIM

        def group(dil):
            nb = s_len // (BLOCK * dil)

            def block(t, carry):
                r, n = t // nb, t % nb
                cur = _block_rows(n, r, dil)
                prev = _block_rows(jnp.maximum(n - 1, 0), r, dil)
                q = q_ref[0, cur, :]
                kcat = jnp.concatenate([k_ref[0, prev, :], k_ref[0, cur, :]], axis=0).astype(BF16)
                vcat = jnp.concatenate([v_ref[0, prev, :], v_ref[0, cur, :]], axis=0).astype(BF16)
                valid = _band_mask(n)
                q2 = jnp.concatenate([jnp.where(first, q, 0.0), jnp.where(first, 0.0, q)], axis=0).astype(BF16)
                s = jnp.where(jnp.concatenate([valid, valid], axis=0), _dot(q2, kcat, NT), MASKED)
                m = jnp.max(s, axis=-1, keepdims=True)
                p = jnp.exp(s - m)
                den = jnp.sum(p, axis=-1, keepdims=True)
                o2 = _dot(p.astype(BF16), vcat, NN) / den
                lse2 = m + jnp.log(den)
                o_ref[0, 0, cur, :] = jnp.where(first, o2[:BLOCK], o2[BLOCK:])
                lse_ref[0, 0, cur, :] = jnp.where(first, lse2[:BLOCK], lse2[BLOCK:])
                return carry

            lax.fori_loop(0, nb * dil, block, 0, unroll=ATTN_FWD_UNROLL)

        for gi, dil in enumerate(DILATIONS):
            pl.when(pl.program_id(0) == gi)(functools.partial(group, dil))

    def slab(base):
        return pl.BlockSpec((1, s_len, LANES), lambda g, s: (base + 2 * g + s, 0, 0))

    out = pl.BlockSpec((1, 1, s_len, LANES), lambda g, s: (g, s, 0, 0))
    shape = jax.ShapeDtypeStruct((n_g, 2, s_len, LANES), F32)
    return pl.pallas_call(
        body, name="attn_fwd", grid=(n_g, 2),
        in_specs=[slab(0), slab(6), slab(12)], out_specs=[out, out], out_shape=[shape, shape],
        compiler_params=_params(("arbitrary", "arbitrary")),
    )(qkv, qkv, qkv)


def _pool_mixed(u, halo, i, tm):
    ue = jnp.concatenate([halo, u], axis=0)
    s2 = ue + pltpu.roll(ue, 1, 0)
    s4 = s2 + pltpu.roll(s2, 2, 0)
    s8 = s4 + pltpu.roll(s4, 4, 0)
    s16 = s8 + pltpu.roll(s8, 8, 0)
    grp = lax.broadcasted_iota(jnp.int32, (tm, 256), 1) // HEAD_DIM
    pick = lambda a, b, c, e: jnp.where(grp == 0, a, jnp.where(grp == 1, b, jnp.where(grp == 2, c, e)))
    win_sum = pick(s2[HALO:], s4[HALO:], s8[HALO:], s16[HALO:])
    pos = (i * tm + lax.broadcasted_iota(jnp.int32, (tm, 256), 0)).astype(F32)
    count = jnp.minimum(pos + 1.0, pick(*[float(w) for w in POOL_WINDOWS]))
    return win_sum / count - u, count


def _mix_out(x, u_pool, o_g, lse_g, w_blk, b_pool, pool_scale, w_out_t, gt_m, g_post_mix, g_pre_ffn, sc_f, sh_f, tm):
    s_len, d = x.shape

    def body(x_ref, u_ref, uh_ref, o_ref, l_ref, wb_ref, bp_ref, ps_ref, wo_ref,
             gt_ref, g1_ref, g2_ref, sc_ref, sh_ref,
             x1_ref, y1_ref, h2_ref, cat_ref, attn_ref, lall_ref):
        (o0, o1, o2), (l0, l1, l2) = (o_ref.at[g] for g in range(3)), (l_ref.at[g] for g in range(3))
        i = pl.program_id(0)
        u = u_ref[...]
        halo = uh_ref[...] * (i > 0).astype(F32)
        mixed, _ = _pool_mixed(u, halo, i, tm)
        y = _dot(mixed.astype(BF16), wb_ref[...], NN) + bp_ref[...]
        pool = y * ps_ref[...]
        attn = []
        for s in range(2):
            la, lb, lc = l0[s], l1[s], l2[s]
            mx = jnp.maximum(jnp.maximum(la, lb), lc)
            ea, eb, ec = jnp.exp(la - mx), jnp.exp(lb - mx), jnp.exp(lc - mx)
            den = ea + eb + ec
            lall_ref[s] = mx + jnp.log(den)
            attn.append((ea / den) * o0[s] + (eb / den) * o1[s] + (ec / den) * o2[s])
        attn = jnp.concatenate(attn, axis=1)
        attn_ref[...] = attn
        cat = jnp.concatenate([pool, attn], axis=1).astype(BF16)
        cat_ref[...] = cat
        y1 = _dot(cat, wo_ref[...], NT)
        y1_ref[...] = y1.astype(BF16)
        x1 = x_ref[...] + gt_ref[...] * (y1 * _rstd(y1) * g1_ref[...])
        x1_ref[...] = x1
        h2 = (x1 * _rstd(x1) * g2_ref[...]) * (1.0 + sc_ref[...]) + sh_ref[...]
        h2_ref[...] = h2.astype(BF16)

    tile = lambda w: pl.BlockSpec((tm, w), lambda i: (i, 0))
    slab = pl.BlockSpec((2, tm, LANES), lambda i: (0, i, 0))
    groups = pl.BlockSpec((len(DILATIONS), 2, tm, LANES), lambda i: (0, 0, i, 0))
    const = lambda a: pl.BlockSpec(a.shape, lambda i: (0,) * a.ndim)
    return pl.pallas_call(
        body, name="mix_out", grid=(s_len // tm,),
        in_specs=[tile(d), tile(256), pl.BlockSpec((HALO, 256), lambda i: (_halo_before(i, tm), 0)),
                  groups, groups,
                  const(w_blk), const(b_pool), const(pool_scale), const(w_out_t),
                  const(gt_m), const(g_post_mix), const(g_pre_ffn), const(sc_f), const(sh_f)],
        out_specs=[tile(d), tile(d), tile(d), tile(512), tile(256), slab],
        out_shape=[jax.ShapeDtypeStruct((s_len, d), F32), jax.ShapeDtypeStruct((s_len, d), BF16),
                   jax.ShapeDtypeStruct((s_len, d), BF16), jax.ShapeDtypeStruct((s_len, 512), BF16),
                   jax.ShapeDtypeStruct((s_len, 256), F32), jax.ShapeDtypeStruct((2, s_len, LANES), F32)],
        compiler_params=_params(("arbitrary",)),
    )(x, u_pool, u_pool, o_g, lse_g, w_blk, b_pool, pool_scale, w_out_t, gt_m, g_post_mix, g_pre_ffn, sc_f, sh_f)


def _conv_gate(gate_ext, cw, cb):
    gc = gate_ext * cw[2:3, :] + pltpu.roll(gate_ext, 1, 0) * cw[1:2, :] + pltpu.roll(gate_ext, 2, 0) * cw[0:1, :]
    return gc[HALO:] + cb


def _ffn_fwd_loss(h2, x1, target, w_up_t, w_down, conv_w, conv_b, gt_f, g_post_ffn, tm, ck):
    s_len, d = x1.shape
    d_ff = w_down.shape[0]
    n_t, n_c = s_len // tm, d_ff // ck

    def body(h_ref, hh_ref, x1_ref, tgt_ref, wg_ref, wv_ref, wd_ref, cw_ref, cb_ref, gt_ref, g_ref,
             gate_ref, a_ref, act_ref, vd_ref, dy2_ref, dout_ref, sums_ref, loss_ref, acc_ref):
        i = pl.program_id(0)

        @pl.when(i == 0)
        def _():
            sums_ref[...] = jnp.zeros_like(sums_ref)
            loss_ref[...] = jnp.zeros_like(loss_ref)
            acc_ref[...] = jnp.zeros_like(acc_ref)

        def finish(live):
            y2 = acc_ref[...]
            rstd = _rstd(y2)
            n = y2 * rstd
            rn = n * g_ref[...]
            err = x1_ref[...] + gt_ref[...] * rn - tgt_ref[...]
            keep = lambda v: jnp.where(live, v, 0.0)
            loss_ref[...] += keep(0.5 * jnp.sum(jnp.mean(err * err, axis=-1, keepdims=True), axis=0, keepdims=True))
            dout = err * (1.0 / d)
            dout_ref[...] = dout
            drn = dout * gt_ref[...]
            sums_ref[0:1, :] += keep(jnp.sum(dout * rn, axis=0, keepdims=True))
            sums_ref[1:2, :] += keep(jnp.sum(drn * n, axis=0, keepdims=True))
            dy2_ref[...] = _norm_bwd(drn * g_ref[...], n, rstd).astype(BF16)

        @pl.when(i < n_t)
        def _():
            h = h_ref[...]
            h_ext = jnp.concatenate([hh_ref[...], h], axis=0)
            row = lax.broadcasted_iota(jnp.int32, (tm + HALO, ck), 0)
            no_halo = (row < HALO) & (i == 0)

            def up(c):
                cs = slice(c * ck, (c + 1) * ck)
                return jnp.where(no_halo, 0.0, _dot(h_ext, wg_ref[cs, :], NT)), _dot(h, wv_ref[cs, :], NT)

            part = None
            nxt = up(0)
            finish(i > 0)
            for c in range(n_c):
                cs = slice(c * ck, (c + 1) * ck)
                gate_ext, val = nxt
                if c + 1 < n_c:
                    nxt = up(c + 1)
                act, dact = _gelu_parts(_conv_gate(gate_ext, cw_ref[:, cs], cb_ref[:, cs]))
                a = (act * val).astype(BF16)
                gate_ref[:, cs] = gate_ext[HALO:].astype(BF16)
                a_ref[:, cs] = a
                act_ref[:, cs] = act.astype(BF16)
                vd_ref[:, cs] = (val * dact).astype(BF16)
                p = _dot(a, wd_ref[cs, :], NN)
                part = p if part is None else part + p
            acc_ref[...] = part

        @pl.when(i == n_t)
        def _():
            finish(True)

    this = lambda i: jnp.minimum(i, n_t - 1)
    before = lambda i: jnp.maximum(i - 1, 0)
    tok = lambda w, at: pl.BlockSpec((tm, w), lambda i: (at(i), 0))
    vec = pl.BlockSpec((1, d), lambda i: (0, 0))
    once = lambda shape, imap: pl.BlockSpec(shape, imap, pipeline_mode=pl.Buffered(1))
    return pl.pallas_call(
        body, name="ffn_fwd_loss", grid=(n_t + 1,),
        in_specs=[tok(d, this), pl.BlockSpec((HALO, d), lambda i: (_halo_before(this(i), tm), 0)),
                  tok(d, before), tok(d, before),
                  once((d_ff, d), lambda i: (0, 0)), once((d_ff, d), lambda i: (1, 0)), once((d_ff, d), lambda i: (0, 0)),
                  pl.BlockSpec((3, d_ff), lambda i: (0, 0)), pl.BlockSpec((1, d_ff), lambda i: (0, 0)), vec, vec],
        out_specs=[tok(d_ff, this)] * 4 + [tok(d, before), tok(d, before), pl.BlockSpec((8, d), lambda i: (0, 0)),
                                          pl.BlockSpec((8, LANES), lambda i: (0, 0))],
        out_shape=[jax.ShapeDtypeStruct((s_len, d_ff), BF16)] * 4
        + [jax.ShapeDtypeStruct((s_len, d), BF16), jax.ShapeDtypeStruct((s_len, d), F32),
           jax.ShapeDtypeStruct((8, d), F32), jax.ShapeDtypeStruct((8, LANES), F32)],
        scratch_shapes=[pltpu.VMEM((tm, d), F32)],
        compiler_params=_params(("arbitrary",)),
    )(h2, h2, x1, target, w_up_t, w_up_t, w_down, conv_w, conv_b, gt_f, g_post_ffn)


def _ffn_bwd_act(dy2, gate, a, act, vd, w_down, tm, tf, ck):
    s_len, d = dy2.shape
    d_ff = w_down.shape[0]
    n_t = s_len // tm
    chunks = [slice(lo, min(lo + ck, tf)) for lo in range(0, tf, ck)]

    def body(dy_ref, g_ref, gh_ref, a_ref, act_ref, vd_ref, wd_ref, dgc_ref, dval_ref, dwd_ref, dconv_ref, acc_ref):
        i = pl.program_id(1)

        @pl.when(i == 0)
        def _():
            acc_ref[...] = jnp.zeros_like(acc_ref)
            dconv_ref[...] = jnp.zeros_like(dconv_ref)

        dy = dy_ref[...]

        def down(cs):
            return _dot(dy, wd_ref[cs, :], NT)

        nxt = down(chunks[0])
        for c, cs in enumerate(chunks):
            width = cs.stop - cs.start
            da = nxt
            if c + 1 < len(chunks):
                nxt = down(chunks[c + 1])
            acc_ref[cs, :] += _dot(a_ref[:, cs], dy, TN)
            row = lax.broadcasted_iota(jnp.int32, (tm + HALO, width), 0)
            gate_ext = jnp.where((row < HALO) & (i == 0), 0.0,
                                 jnp.concatenate([gh_ref[:, cs], g_ref[:, cs]], axis=0).astype(F32))
            dgc = da * vd_ref[:, cs].astype(F32)
            dgc_ref[:, cs] = dgc.astype(BF16)
            dval_ref[:, cs] = (da * act_ref[:, cs].astype(F32)).astype(BF16)
            rows = [jnp.sum(dgc * pltpu.roll(gate_ext, 2 - k, 0)[HALO:], axis=0, keepdims=True) for k in range(2)]
            rows += [jnp.sum(dgc * gate_ext[HALO:], axis=0, keepdims=True), jnp.sum(dgc, axis=0, keepdims=True),
                     jnp.zeros((4, width), F32)]
            dconv_ref[:, cs] += jnp.concatenate(rows, axis=0)

        @pl.when(i == n_t - 1)
        def _():
            dwd_ref[...] = acc_ref[...].astype(BF16)

    tokf = pl.BlockSpec((tm, tf), lambda j, i: (i, j))
    return pl.pallas_call(
        body, name="ffn_bwd_act", grid=(d_ff // tf, n_t),
        in_specs=[pl.BlockSpec((tm, d), lambda j, i: (i, 0)), tokf,
                  pl.BlockSpec((HALO, tf), lambda j, i: (_halo_before(i, tm), j)), tokf, tokf, tokf,
                  pl.BlockSpec((tf, d), lambda j, i: (j, 0))],
        out_specs=[tokf, tokf, pl.BlockSpec((tf, d), lambda j, i: (j, 0)), pl.BlockSpec((8, tf), lambda j, i: (0, j))],
        out_shape=[jax.ShapeDtypeStruct((s_len, d_ff), BF16), jax.ShapeDtypeStruct((s_len, d_ff), BF16),
                   jax.ShapeDtypeStruct((d_ff, d), BF16), jax.ShapeDtypeStruct((8, d_ff), F32)],
        scratch_shapes=[pltpu.VMEM((tf, d), F32)],
        compiler_params=_params(("arbitrary", "arbitrary")),
    )(dy2, gate, gate, a, act, vd, w_down)


def _ffn_bwd_up(dgc, dval, w_up_t, conv_w, tm):
    s_len, d_ff = dgc.shape
    d = w_up_t.shape[1]
    n_t = s_len // tm

    def body(dg_ref, dgn_ref, dv_ref, cw_ref, w_ref, dup_ref, dh_ref):
        i = pl.program_id(0)
        nxt = dgn_ref[...].astype(F32) * (i < n_t - 1).astype(F32)
        ext = jnp.concatenate([dg_ref[...].astype(F32), nxt], axis=0)
        rows = tm + HALO
        dgate = (ext * cw_ref[2:3, :] + pltpu.roll(ext, rows - 1, 0) * cw_ref[1:2, :]
                 + pltpu.roll(ext, rows - 2, 0) * cw_ref[0:1, :])[:tm]
        dup = jnp.concatenate([dgate.astype(BF16), dv_ref[...]], axis=1)
        dup_ref[...] = dup
        dh_ref[...] = _dot(dup, w_ref[...], NN).astype(BF16)

    tokf = pl.BlockSpec((tm, d_ff), lambda i: (i, 0))
    return pl.pallas_call(
        body, name="ffn_bwd_up", grid=(n_t,),
        in_specs=[tokf, pl.BlockSpec((HALO, d_ff), lambda i: (jnp.minimum((i + 1) * (tm // HALO), s_len // HALO - 1), 0)),
                  tokf, pl.BlockSpec((3, d_ff), lambda i: (0, 0)), pl.BlockSpec((2 * d_ff, d), lambda i: (0, 0))],
        out_specs=[pl.BlockSpec((tm, 2 * d_ff), lambda i: (i, 0)), pl.BlockSpec((tm, d), lambda i: (i, 0))],
        out_shape=[jax.ShapeDtypeStruct((s_len, 2 * d_ff), BF16), jax.ShapeDtypeStruct((s_len, d), BF16)],
        compiler_params=_params(("arbitrary",)),
    )(dgc, dgc, dval, conv_w, w_up_t)


def _mix_bwd(dh2, dout, x1, y1, cat, attn, w_out_t, sc_f, g_pre_ffn, gt_m, g_post_mix, after, tm):
    s_len, d = x1.shape
    n_t = s_len // tm

    def body(dh_ref, do_ref, x1_ref, y1_ref, cat_ref, at_ref, wo_ref, sc_ref, g2_ref, gt_ref, g1_ref, after_ref,
             dx1_ref, dpool_ref, dattn_ref, delta_ref, dwo_ref, sums_ref, acc_ref):
        i = pl.program_id(0)
        dh = dh_ref[...].astype(F32)
        x1 = x1_ref[...]
        r2 = _rstd(x1)
        n2 = x1 * r2
        ng = n2 * g2_ref[...]
        dng = dh * (1.0 + sc_ref[...])
        dx1 = do_ref[...] + _norm_bwd(dng * g2_ref[...], n2, r2)
        dx1_ref[...] = dx1
        y1 = y1_ref[...].astype(F32)
        r1 = _rstd(y1)
        n1 = y1 * r1
        drn = dx1 * gt_ref[...]
        dy1 = _norm_bwd(drn * g1_ref[...], n1, r1).astype(BF16)
        dcat = _dot(dy1, wo_ref[...], NN)
        dpool_ref[...] = dcat[:, 0:256]
        lane = lax.broadcasted_iota(jnp.int32, (tm, LANES), 1)
        first = lane < HEAD_DIM
        for s in range(2):
            da = dcat[:, 256 + s * LANES:256 + (s + 1) * LANES]
            dattn_ref[s] = da
            prod = da * at_ref[:, s * LANES:(s + 1) * LANES]
            tot = jnp.sum(prod, axis=-1, keepdims=True)
            lo = jnp.sum(jnp.where(first, prod, 0.0), axis=-1, keepdims=True)
            delta_ref[s] = jnp.where(first, lo, tot - lo)
        dwo = _dot(dy1, cat_ref[...], TN)
        sums = jnp.concatenate(
            [jnp.sum(dh, axis=0, keepdims=True), jnp.sum(dh * ng, axis=0, keepdims=True),
             jnp.sum(dng * n2, axis=0, keepdims=True), jnp.sum(dx1 * (n1 * g1_ref[...]), axis=0, keepdims=True),
             jnp.sum(drn * n1, axis=0, keepdims=True), jnp.zeros((3, d), F32)], axis=0)

        @pl.when(i == 0)
        def _():
            acc_ref[...] = dwo
            sums_ref[...] = sums

        @pl.when(i > 0)
        def _():
            acc_ref[...] += dwo
            sums_ref[...] += sums

        @pl.when(i == n_t - 1)
        def _():
            dwo_ref[...] = acc_ref[...].astype(BF16)

    tile = lambda w: pl.BlockSpec((tm, w), lambda i: (i, 0))
    slab = pl.BlockSpec((2, tm, LANES), lambda i: (0, i, 0))
    vec = pl.BlockSpec((1, d), lambda i: (0, 0))
    return pl.pallas_call(
        body, name="mix_bwd", grid=(n_t,),
        in_specs=[tile(d), tile(d), tile(d), tile(d), tile(512), tile(256),
                  pl.BlockSpec((d, 512), lambda i: (0, 0)), vec, vec, vec, vec, pl.BlockSpec(memory_space=pl.ANY)],
        out_specs=[tile(d), tile(256), slab, slab, pl.BlockSpec((d, 512), lambda i: (0, 0)),
                   pl.BlockSpec((8, d), lambda i: (0, 0))],
        out_shape=[jax.ShapeDtypeStruct((s_len, d), F32), jax.ShapeDtypeStruct((s_len, 256), F32),
                   jax.ShapeDtypeStruct((2, s_len, LANES), F32), jax.ShapeDtypeStruct((2, s_len, LANES), F32),
                   jax.ShapeDtypeStruct((d, 512), BF16), jax.ShapeDtypeStruct((8, d), F32)],
        scratch_shapes=[pltpu.VMEM((d, 512), F32)],
        compiler_params=_params(("arbitrary",)),
    )(dh2, dout, x1, y1, cat, attn, w_out_t, sc_f, g_pre_ffn, gt_m, g_post_mix, after)


def _pool_bwd(dpool, u_pool, w_blk, b_pool, pool_scale, tm):
    s_len = dpool.shape[0]
    n_t = s_len // tm

    def body(dp_ref, dpn_ref, u_ref, uh_ref, wb_ref, bp_ref, ps_ref, du_ref, dwp_ref, sums_ref, acc_ref):
        i = pl.program_id(0)
        u = u_ref[...]
        mixed, _ = _pool_mixed(u, uh_ref[...] * (i > 0).astype(F32), i, tm)
        mixed_b = mixed.astype(BF16)
        y = _dot(mixed_b, wb_ref[...], NN) + bp_ref[...]
        dp = dp_ref[...]
        dy = dp * ps_ref[...]
        dwb = _dot(mixed_b, dy.astype(BF16), TN)
        sums = jnp.concatenate([jnp.sum(dy, axis=0, keepdims=True), jnp.sum(dp * y, axis=0, keepdims=True),
                                jnp.zeros((6, 256), F32)], axis=0)
        dp_ext = jnp.concatenate([dp, dpn_ref[...] * (i < n_t - 1).astype(F32)], axis=0)
        dmix = _dot((dp_ext * ps_ref[...]).astype(BF16), wb_ref[...], NT)
        rows = tm + HALO
        grp = lax.broadcasted_iota(jnp.int32, (rows, 256), 1) // HEAD_DIM
        pick = lambda a, b, c, e: jnp.where(grp == 0, a, jnp.where(grp == 1, b, jnp.where(grp == 2, c, e)))
        pos = (i * tm + lax.broadcasted_iota(jnp.int32, (rows, 256), 0)).astype(F32)
        z = dmix / jnp.minimum(pos + 1.0, pick(*[float(w) for w in POOL_WINDOWS]))
        f2 = z + pltpu.roll(z, rows - 1, 0)
        f4 = f2 + pltpu.roll(f2, rows - 2, 0)
        f8 = f4 + pltpu.roll(f4, rows - 4, 0)
        f16 = f8 + pltpu.roll(f8, rows - 8, 0)
        du_ref[...] = (pick(f2, f4, f8, f16) - dmix)[:tm]

        @pl.when(i == 0)
        def _():
            acc_ref[...] = dwb
            sums_ref[...] = sums

        @pl.when(i > 0)
        def _():
            acc_ref[...] += dwb
            sums_ref[...] += sums

        @pl.when(i == n_t - 1)
        def _():
            full = acc_ref[...]
            for gi in range(len(POOL_WINDOWS)):
                lo = gi * HEAD_DIM
                dwp_ref[gi] = full[lo:lo + HEAD_DIM, lo:lo + HEAD_DIM]

    n_g = len(POOL_WINDOWS)
    tile = pl.BlockSpec((tm, 256), lambda i: (i, 0))
    const = lambda a: pl.BlockSpec(a.shape, lambda i: (0,) * a.ndim)
    return pl.pallas_call(
        body, name="pool_bwd", grid=(n_t,),
        in_specs=[tile, pl.BlockSpec((HALO, 256), lambda i: (jnp.minimum((i + 1) * (tm // HALO), s_len // HALO - 1), 0)),
                  tile, pl.BlockSpec((HALO, 256), lambda i: (_halo_before(i, tm), 0)),
                  const(w_blk), const(b_pool), const(pool_scale)],
        out_specs=[tile, pl.BlockSpec((n_g, HEAD_DIM, HEAD_DIM), lambda i: (0, 0, 0)), pl.BlockSpec((8, 256), lambda i: (0, 0))],
        out_shape=[jax.ShapeDtypeStruct((s_len, 256), F32), jax.ShapeDtypeStruct((n_g, HEAD_DIM, HEAD_DIM), F32),
                   jax.ShapeDtypeStruct((8, 256), F32)],
        scratch_shapes=[pltpu.VMEM((256, 256), F32)],
        compiler_params=_params(("arbitrary",)),
    )(dpool, dpool, u_pool, u_pool, w_blk, b_pool, pool_scale)


def _attn_bwd(qkv, dattn, lse_all, delta, after):
    s_len = qkv.shape[1]
    n_g = len(DILATIONS)

    def body(q_ref, k_ref, v_ref, do_ref, l_ref, dl_ref, after_ref, dq_ref, dk_ref, dv_ref):
        lane = lax.broadcasted_iota(jnp.int32, (BLOCK, LANES), 1)
        first = lane < HEAD_DIM

        def group(dil):
            nb = s_len // (BLOCK * dil)

            def block(t, carry):
                dk_part, dv_part = carry
                r, n = t // nb, t % nb
                cur = _block_rows(n, r, dil)
                prev = _block_rows(jnp.maximum(n - 1, 0), r, dil)
                q = q_ref[0, cur, :]
                do = do_ref[0, cur, :]
                lse = l_ref[0, cur, :]
                dlt = dl_ref[0, cur, :]
                kcat = jnp.concatenate([k_ref[0, prev, :], k_ref[0, cur, :]], axis=0).astype(BF16)
                vcat = jnp.concatenate([v_ref[0, prev, :], v_ref[0, cur, :]], axis=0).astype(BF16)
                valid = _band_mask(n)
                stack = lambda a: jnp.concatenate([jnp.where(first, a, 0.0), jnp.where(first, 0.0, a)], axis=0)
                rows2 = lambda a: jnp.concatenate([a[:, 0:1], a[:, HEAD_DIM:HEAD_DIM + 1]], axis=0)
                q2, do2 = stack(q).astype(BF16), stack(do).astype(BF16)
                valid2 = jnp.concatenate([valid, valid], axis=0)
                p = jnp.where(valid2, jnp.exp(_dot(q2, kcat, NT) - rows2(lse)), 0.0)
                ds = (p * (_dot(do2, vcat, NT) - rows2(dlt))).astype(BF16)
                dq2 = _dot(ds, kcat, NN)
                dq_ref[0, 0, cur, :] = jnp.where(first, dq2[:BLOCK], dq2[BLOCK:])
                dkc = _dot(ds, q2, TN)
                dvc = _dot(p.astype(BF16), do2, TN)
                dk_ref[0, 0, prev, :] = dk_part + dkc[:BLOCK]
                dv_ref[0, 0, prev, :] = dv_part + dvc[:BLOCK]
                dk_ref[0, 0, cur, :] = dkc[BLOCK:]
                dv_ref[0, 0, cur, :] = dvc[BLOCK:]
                return dkc[BLOCK:], dvc[BLOCK:]

            def blocks(tt, carry):
                for u in range(ATTN_BWD_UNROLL):
                    carry = block(tt * ATTN_BWD_UNROLL + u, carry)
                return carry

            zero = jnp.zeros((BLOCK, LANES), F32)
            lax.fori_loop(0, nb * dil // ATTN_BWD_UNROLL, blocks, (zero, zero))

        for gi, dil in enumerate(DILATIONS):
            pl.when(pl.program_id(1) == gi)(functools.partial(group, dil))

    def slab(base):
        return pl.BlockSpec((1, s_len, LANES), lambda s, g: (base + 2 * g + s, 0, 0))

    one = pl.BlockSpec((1, s_len, LANES), lambda s, g: (s, 0, 0))
    out = pl.BlockSpec((1, 1, s_len, LANES), lambda s, g: (g, s, 0, 0))
    shape = jax.ShapeDtypeStruct((n_g, 2, s_len, LANES), F32)
    return pl.pallas_call(
        body, name="attn_bwd", grid=(2, n_g),
        in_specs=[slab(0), slab(6), slab(12), one, one, one, pl.BlockSpec(memory_space=pl.ANY)],
        out_specs=[out, out, out], out_shape=[shape, shape, shape],
        compiler_params=_params(("arbitrary", "arbitrary")),
    )(qkv, qkv, qkv, dattn, lse_all, delta, after)


def _dproj_wgrad_in(du, dqkv, rope, h1, tm, cm):
    s_len = du.shape[0]
    d = h1.shape[1]
    n_proj = 256 + 18 * LANES
    n_t = s_len // tm

    def body(du_ref, dq_ref, dk_ref, dv_ref, cs_ref, spread_ref, h_ref, dproj_ref, dw_ref, acc_ref):
        i = pl.program_id(0)

        @pl.when(i == 0)
        def _():
            acc_ref[...] = jnp.zeros_like(acc_ref)

        dproj_ref[:, 0:256] = du_ref[...].astype(BF16)
        lanes = _rope_lanes(cs_ref, spread_ref)
        col = 256
        for kind, dref in enumerate((dq_ref, dk_ref, dv_ref)):
            for grp in range(3):
                for s in range(2):
                    piece = dref[grp, s]
                    if kind < 2:
                        piece = _rope_bwd(piece, lanes)
                    if kind == 0:
                        piece = piece * (HEAD_DIM ** -0.5)
                    dproj_ref[:, col:col + LANES] = piece.astype(BF16)
                    col += LANES

        for c0 in range(0, n_proj, cm):
            acc_ref[c0:c0 + cm, :] += _dot(dproj_ref[:, c0:c0 + cm], h_ref[...], TN)

        @pl.when(i == n_t - 1)
        def _():
            dw_ref[...] = acc_ref[...].astype(BF16)

    groups = pl.BlockSpec((len(DILATIONS), 2, tm, LANES), lambda i: (0, 0, i, 0))
    return pl.pallas_call(
        body, name="dproj_wgrad_in", grid=(n_t,),
        in_specs=[pl.BlockSpec((tm, 256), lambda i: (i, 0))] + [groups] * 3
        + [pl.BlockSpec((rope[0].shape[0], tm), lambda i: (0, i)), pl.BlockSpec(rope[1].shape, lambda i: (0, 0, 0)),
           pl.BlockSpec((tm, d), lambda i: (i, 0))],
        out_specs=[pl.BlockSpec((tm, n_proj), lambda i: (i, 0)), pl.BlockSpec((n_proj, d), lambda i: (0, 0))],
        out_shape=[jax.ShapeDtypeStruct((s_len, n_proj), BF16), jax.ShapeDtypeStruct((n_proj, d), BF16)],
        scratch_shapes=[pltpu.VMEM((n_proj, d), F32)],
        compiler_params=_params(("arbitrary",)),
    )(du, *dqkv, *rope, h1)


def _inproj_bwd(dproj, w_in_t, x, dx1, sc_m, g_pre_mix, after, tm):
    s_len, d = x.shape
    n_proj = w_in_t.shape[0]
    n_t = s_len // tm

    def body(dproj_ref, w_ref, x_ref, dx1_ref, sc_ref, g_ref, after_ref, dx_ref, sums_ref):
        i = pl.program_id(0)
        halves = [slice(0, tm // 2), slice(tm // 2, tm)]
        dhs = [_dot(dproj_ref[rs, :], w_ref[...], NN) for rs in halves]
        sums = None
        for rs, dh in zip(halves, dhs):
            xv = x_ref[rs, :]
            r = _rstd(xv)
            n = xv * r
            dng = dh * (1.0 + sc_ref[...])
            dx_ref[rs, :] = dx1_ref[rs, :] + _norm_bwd(dng * g_ref[...], n, r)
            part = jnp.concatenate([jnp.sum(dh, axis=0, keepdims=True), jnp.sum(dh * (n * g_ref[...]), axis=0, keepdims=True),
                                    jnp.sum(dng * n, axis=0, keepdims=True), jnp.zeros((5, d), F32)], axis=0)
            sums = part if sums is None else sums + part

        @pl.when(i == 0)
        def _():
            sums_ref[...] = sums

        @pl.when(i > 0)
        def _():
            sums_ref[...] += sums

    tile = lambda w: pl.BlockSpec((tm, w), lambda i: (i, 0))
    vec = pl.BlockSpec((1, d), lambda i: (0, 0))
    return pl.pallas_call(
        body, name="inproj_bwd", grid=(n_t,),
        in_specs=[tile(n_proj), pl.BlockSpec((n_proj, d), lambda i: (0, 0)), tile(d), tile(d), vec, vec,
                  pl.BlockSpec(memory_space=pl.ANY)],
        out_specs=[tile(d), pl.BlockSpec((8, d), lambda i: (0, 0))],
        out_shape=[jax.ShapeDtypeStruct((s_len, d), F32), jax.ShapeDtypeStruct((8, d), F32)],
        compiler_params=_params(("arbitrary",)),
    )(dproj, w_in_t, x, dx1, sc_m, g_pre_mix, after)


def _wgrad(a, b, name, tk, tmm):
    s_len, m = a.shape
    n = b.shape[1]
    n_k = s_len // tk

    def body(a_ref, b_ref, o_ref, acc_ref):
        k = pl.program_id(1)
        part = _dot(a_ref[...], b_ref[...], TN)

        @pl.when(k == 0)
        def _():
            acc_ref[...] = part

        @pl.when(k > 0)
        def _():
            acc_ref[...] += part

        @pl.when(k == n_k - 1)
        def _():
            o_ref[...] = acc_ref[...].astype(BF16)

    return pl.pallas_call(
        body, name=name, grid=(m // tmm, n_k),
        in_specs=[pl.BlockSpec((tk, tmm), lambda j, k: (k, j)), pl.BlockSpec((tk, n), lambda j, k: (k, 0))],
        out_specs=pl.BlockSpec((tmm, n), lambda j, k: (j, 0)),
        out_shape=jax.ShapeDtypeStruct((m, n), BF16),
        scratch_shapes=[pltpu.VMEM((tmm, n), F32)],
        compiler_params=_params(("arbitrary", "arbitrary")),
    )(a, b)


def _place():
    return lax.axis_index("x"), lax.axis_index("y"), lax.axis_index("c")


def _peer(k):
    x, y, c = _place()
    bx, by, bc = (k >> 2) & 1, (k >> 1) & 1, k & 1
    return (x ^ bx if bx else x, y ^ by if by else y, c ^ bc if bc else c)


def _index(pos):
    return 4 * pos[0] + 2 * pos[1] + pos[2]


def _entry_exchange(c_rows, w_ada, b_ada, taps, shards):
    d = c_rows.shape[1]
    ncol = w_ada.shape[1]
    n_w = len(shards)

    def body(c_ref, w_ref, b_ref, t_ref, *rest):
        srcs = rest[:n_w]
        call_ref, mod_ref, tall_ref = rest[n_w:n_w + 3]
        outs = rest[n_w + 3:2 * n_w + 3]
        stage_ref, s_send, s_recv, w_send, w_recv, local_sems = rest[2 * n_w + 3:]
        x, y, c = _place()
        here, sibling = (x, y, c), (x, y, 1 - c)
        chips = [(1 - x, y), (x, 1 - y), (1 - x, 1 - y)]
        me = _index(here)

        def small(kind, src, dst, k):
            return pltpu.make_async_remote_copy(src_ref=src, dst_ref=dst, send_sem=s_send.at[kind, k - 1],
                                                recv_sem=s_recv.at[kind, k - 1], device_id=_peer(k), device_id_type=MESH)

        gather = lambda k: small(0, c_ref, call_ref.at[me], k)
        scatter = lambda k: small(1, stage_ref.at[_index(_peer(k))], mod_ref.at[me], k)
        gather_taps = lambda k: small(2, t_ref, tall_ref.at[me], k)

        def rows(w, pos):
            r = shards[w].shape[0]
            return outs[w].at[pl.ds(pl.multiple_of(_index(pos) * r, 16), r), :]

        def block(k, w, pos, to, own=False):
            return pltpu.make_async_remote_copy(
                src_ref=srcs[w] if own else rows(w, pos), dst_ref=rows(w, pos),
                send_sem=w_send.at[k, w], recv_sem=w_recv.at[k, w], device_id=to, device_id_type=MESH)

        call_ref[me] = c_ref[...]
        tall_ref[me] = t_ref[...]
        for k in range(1, N_DEV):
            gather(k).start()
        for k in range(1, N_DEV):
            gather_taps(k).start()
        mine = [pltpu.make_async_copy(srcs[w], rows(w, here), local_sems.at[w]) for w in range(n_w)]
        for cp in mine:
            cp.start()
        first = [block(0, w, here, sibling, own=True) for w in range(n_w)]
        first += [block(1 + j, w, here, (*chip, c), own=True) for j, chip in enumerate(chips) for w in range(n_w)]
        for cp in first:
            cp.start()

        for k in range(1, N_DEV):
            gather(k).wait_recv()
        cv = jnp.concatenate([call_ref[b, 0:1, :] for b in range(N_DEV)], axis=0)
        act = cv * jax.nn.sigmoid(cv)
        mod = lax.dot_general(act, w_ref[...], NN, preferred_element_type=F32,
                              precision=lax.Precision.HIGHEST) + b_ref[:, pl.ds(pl.multiple_of(me * ncol, LANES), ncol)]
        for b in range(N_DEV):
            stage_ref[b] = jnp.broadcast_to(mod[b:b + 1, :], (8, ncol))
        mod_ref[me] = stage_ref[me]
        for k in range(1, N_DEV):
            scatter(k).start()

        passed = []
        for j, chip in enumerate(chips):
            for w in range(n_w):
                block(1 + j, w, (*chip, c), here).wait_recv()
                fwd = block(4 + j, w, (*chip, c), sibling)
                fwd.start()
                passed.append(fwd)
        for w in range(n_w):
            block(0, w, sibling, here).wait_recv()
        for j, chip in enumerate(chips):
            for w in range(n_w):
                block(4 + j, w, (*chip, 1 - c), here).wait_recv()
        for k in range(1, N_DEV):
            scatter(k).wait_recv()
            gather_taps(k).wait_recv()
        for cp in first + passed:
            cp.wait_send()
        for k in range(1, N_DEV):
            gather(k).wait_send()
            scatter(k).wait_send()
            gather_taps(k).wait_send()
        for cp in mine:
            cp.wait()

    vmem, hbm = pl.BlockSpec(memory_space=pltpu.VMEM), pl.BlockSpec(memory_space=pltpu.HBM)
    out = pl.pallas_call(
        body, name="entry_exchange",
        in_specs=[vmem] * 4 + [hbm] * n_w, out_specs=[vmem] * 3 + [hbm] * n_w,
        out_shape=[jax.ShapeDtypeStruct((N_DEV, 8, d), F32), jax.ShapeDtypeStruct((N_DEV, 8, ncol), F32),
                   jax.ShapeDtypeStruct((N_DEV,) + taps.shape, F32)]
        + [jax.ShapeDtypeStruct((N_DEV * s.shape[0], s.shape[1]), s.dtype) for s in shards],
        scratch_shapes=[pltpu.VMEM((N_DEV, 8, ncol), F32), pltpu.SemaphoreType.DMA((3, N_DEV - 1)),
                        pltpu.SemaphoreType.DMA((3, N_DEV - 1)), pltpu.SemaphoreType.DMA((N_DEV - 1, n_w)),
                        pltpu.SemaphoreType.DMA((N_DEV - 1, n_w)), pltpu.SemaphoreType.DMA((n_w,))],
        compiler_params=_params(),
    )(c_rows, w_ada, b_ada, taps, *shards)
    return out[0], out[1], out[2], out[3:]


def _peer_copies(mode, srcs, lands, send_sems, recv_sems):
    if mode in ("gather_ici", "gather_d2d"):
        x, y, c = _place()
        sibling = (x, y, 1 - c)
        chips = [(1 - x, y), (x, 1 - y), (1 - x, 1 - y)]
        n = len(lands)

        def rows(w, pos):
            r = lands[w].shape[0] // N_DEV
            return lands[w].at[pl.ds(pl.multiple_of(_index(pos) * r, 16), r), :]

        def copy(k, w, src, dst, to):
            return pltpu.make_async_remote_copy(src_ref=src, dst_ref=dst, send_sem=send_sems.at[k * n + w],
                                                recv_sem=recv_sems.at[k * n + w], device_id=to, device_id_type=MESH)

        if mode == "gather_ici":
            targets = [sibling] + [(*chip, c) for chip in chips]
            return [copy(k, w, rows(w, (x, y, c)), rows(w, (x, y, c)), to) for k, to in enumerate(targets) for w in range(n)]
        return [copy(j, w, rows(w, (*chip, c)), rows(w, (*chip, c)), sibling)
                for j, chip in enumerate(chips) for w in range(n)]
    me = _index(_place())
    modes = (mode,) * len(srcs) if isinstance(mode, str) else mode
    copies = []
    for k in range(1, N_DEV):
        peer = _peer(k)
        for w, (src, land) in enumerate(zip(srcs, lands)):
            if modes[w] == "gather":
                r = src.shape[0]
                dst = land.at[pl.ds(pl.multiple_of(me * r, 16), r), :]
            elif modes[w] == "allgather":
                dst = land.at[me]
            else:
                r = src.shape[0] // N_DEV
                src = src.at[pl.ds(pl.multiple_of(_index(peer) * r, 16), r), :]
                dst = land.at[me]
            copies.append(pltpu.make_async_remote_copy(
                src_ref=src, dst_ref=dst, send_sem=send_sems.at[(k - 1) * len(srcs) + w],
                recv_sem=recv_sems.at[(k - 1) * len(srcs) + w],
                device_id=peer, device_id_type=MESH))
    return copies


def _landing_zone(src, me, name, tr):
    r, cols = src.shape
    n_t = r // tr

    def body(me_ref, s_ref, o_ref):
        o_ref[...] = s_ref[...].astype(BF16)

    return pl.pallas_call(
        body, name=name, out_shape=jax.ShapeDtypeStruct((N_DEV * r, cols), BF16),
        grid_spec=pltpu.PrefetchScalarGridSpec(
            num_scalar_prefetch=1, grid=(n_t,), in_specs=[pl.BlockSpec((tr, cols), lambda i, me_ref: (i, 0))],
            out_specs=pl.BlockSpec((tr, cols), lambda i, me_ref: (me_ref[0] * n_t + i, 0))),
        compiler_params=_params(("arbitrary",)),
    )(me.reshape(1).astype(jnp.int32), src)


def _exchange_start(mode, srcs, lands, name):
    n_s, n_a = len(srcs), len(srcs) + len(lands)
    n_cp = _COPIES_PER_ARRAY.get(mode, N_DEV - 1) * len(lands)

    def body(*refs):
        for cp in _peer_copies(mode, refs[:n_s], refs[n_s:n_a], refs[n_a], refs[n_a + 1]):
            cp.start()

    hbm, sem = pl.BlockSpec(memory_space=pltpu.HBM), pl.BlockSpec(memory_space=pltpu.SEMAPHORE)
    arrays = list(srcs) + list(lands)
    out = pl.pallas_call(
        body, name=name,
        out_shape=(pltpu.SemaphoreType.DMA((n_cp,)), pltpu.SemaphoreType.DMA((n_cp,)),
                   *[pltpu.HBM(a.shape, a.dtype) for a in arrays]),
        in_specs=[hbm] * n_a, out_specs=(sem, sem, *[hbm] * n_a),
        input_output_aliases={i: 2 + i for i in range(n_a)},
        compiler_params=pltpu.CompilerParams(has_side_effects=pltpu.SideEffectType.DATAFLOW_SIDE_EFFECTING),
    )(*[pltpu.with_memory_space_constraint(a, pltpu.HBM) for a in arrays])
    return out[0], out[1], out[2:2 + n_s], out[2 + n_s:2 + n_a], out[2]


_COPIES_PER_ARRAY = {"gather_ici": 4, "gather_d2d": 3}


def _exchange_wait(mode, send_sems, recv_sems, srcs, lands, after, name):
    n_s, n_a = len(srcs), len(srcs) + len(lands)

    def body(*refs):
        copies = _peer_copies(mode, refs[:n_s], refs[n_s:n_a], refs[n_a], refs[n_a + 1])
        for cp in copies:
            cp.wait_send()
        for cp in copies:
            cp.wait_recv()

    hbm, sem = pl.BlockSpec(memory_space=pltpu.HBM), pl.BlockSpec(memory_space=pltpu.SEMAPHORE)
    arrays = list(srcs) + list(lands)
    out = pl.pallas_call(
        body, name=name, out_shape=tuple(pltpu.HBM(a.shape, a.dtype) for a in arrays),
        in_specs=[hbm] * n_a + [sem, sem, pl.BlockSpec(memory_space=pl.ANY)], out_specs=tuple([hbm] * n_a),
        input_output_aliases={i: i for i in range(n_a)},
        compiler_params=pltpu.CompilerParams(has_side_effects=pltpu.SideEffectType.DATAFLOW_SIDE_EFFECTING),
    )(*arrays, send_sems, recv_sems, after)
    return out[:n_s], out[n_s:]


SMALL_WEIGHTS = ("b_ada", "g_pre_mix", "g_post_mix", "g_pre_ffn", "g_post_ffn", "w_pool", "b_pool", "pool_scale", "conv_b")


MOD_ROWS = ((0, 0), (0, 1), (1, 3), (1, 0), (1, 1), (2, 0))


def _small_sum_adam(mine, gathered, weights, moms, vels):
    n_l, n_w = len(mine), len(weights)
    d = mine[0].shape[1]

    def body(*refs):
        loc, got = refs[:n_l], refs[n_l:2 * n_l]
        w_refs, m_refs, v_refs = (refs[2 * n_l + k * n_w:2 * n_l + (k + 1) * n_w] for k in range(3))
        outs = refs[2 * n_l + 3 * n_w:]
        dmod_ref, conv_ref, loss_ref = outs[4 * n_w:]
        me = _index(_place())
        part = lambda a, dev: jnp.where(dev == me, loc[a][...], got[a][dev])
        totals = []
        for a in range(n_l):
            tot = part(a, 0)
            for dev in range(1, N_DEV):
                tot = tot + part(a, dev)
            totals.append(tot)
        t_in, t_mix, t_ffn, t_pool, t_blk, t_conv, t_loss = totals
        conv_ref[...] = t_conv
        loss_ref[...] = t_loss
        for dev in range(N_DEV):
            for k, (a, r) in enumerate(MOD_ROWS):
                dmod_ref[dev:dev + 1, k * d:(k + 1) * d] = part(a, dev)[r:r + 1, :]

        def update(idx, g, at=()):
            sel = lambda ref: ref.at[at] if at else ref
            delta, nm, nv = _adam_math(sel(w_refs[idx])[...], g, sel(m_refs[idx])[...], sel(v_refs[idx])[...])
            for k, val in enumerate((g, delta, nm, nv)):
                sel(outs[4 * idx + k])[...] = val

        tots = (t_in, t_mix, t_ffn)
        update(0, jnp.concatenate([tots[a][r:r + 1] for a, r in MOD_ROWS], axis=1))
        update(1, t_in[2:3])
        update(2, t_mix[4:5])
        update(3, t_mix[2:3])
        update(4, t_ffn[1:2])
        for gi in range(len(POOL_WINDOWS)):
            update(5, t_blk[gi], at=(0, gi))
        update(6, jnp.concatenate([t_pool[0:1, gi * HEAD_DIM:(gi + 1) * HEAD_DIM] for gi in range(len(POOL_WINDOWS))], axis=0),
               at=(0,))
        update(7, t_pool[1:2])
        update(8, t_conv[3:4])

    vmem = pl.BlockSpec(memory_space=pltpu.VMEM)
    out = pl.pallas_call(
        body, name="small_sum_adam", in_specs=[vmem] * (2 * n_l + 3 * n_w), out_specs=[vmem] * (4 * n_w + 3),
        out_shape=[jax.ShapeDtypeStruct(w.shape, F32) for w in weights for _ in range(4)]
        + [jax.ShapeDtypeStruct((N_DEV, 6 * d), F32), jax.ShapeDtypeStruct(mine[5].shape, F32),
           jax.ShapeDtypeStruct(mine[6].shape, F32)],
        compiler_params=_params(),
    )(*mine, *gathered, *weights, *moms, *vels)
    return out[:4 * n_w], out[4 * n_w], out[4 * n_w + 1], out[4 * n_w + 2]


def _adam_math(w, g, m, v):
    m = ADAM_B1 * m + (1.0 - ADAM_B1) * g
    v = ADAM_B2 * v + (1.0 - ADAM_B2) * (g * g)
    m_hat = m / (1.0 - ADAM_B1 ** ADAM_STEP)
    v_hat = v / (1.0 - ADAM_B2 ** ADAM_STEP)
    delta = -ADAM_LR * (m_hat / (jnp.sqrt(v_hat) + ADAM_EPS) + ADAM_WD * w)
    return delta, m, v


def _adam(w, g, m, v, name):
    def body(w_ref, g_ref, m_ref, v_ref, d_ref, nm_ref, nv_ref):
        d_ref[...], nm_ref[...], nv_ref[...] = _adam_math(w_ref[...], g_ref[...], m_ref[...], v_ref[...])

    vmem = pl.BlockSpec(memory_space=pltpu.VMEM)
    return pl.pallas_call(
        body, name=name, in_specs=[vmem] * 4, out_specs=[vmem] * 3,
        out_shape=[jax.ShapeDtypeStruct(w.shape, F32)] * 3, compiler_params=_params(),
    )(w, g, m, v)


def _sum_adam(own, parts, w, m, v, me, name, tr):
    _, rows, cols = parts.shape
    turned = w.shape == (cols, rows) and rows != cols
    assert tr == rows or not turned
    n_t = rows // tr

    def body(me_ref, own_ref, p_ref, w_ref, m_ref, v_ref, g_ref, d_ref, nm_ref, nv_ref):
        part = lambda dev: jnp.where(dev == me_ref[0], own_ref[...], p_ref[dev]).astype(F32)
        g = part(0)
        for dev in range(1, N_DEV):
            g = g + part(dev)
        g = g.T if turned else g
        g_ref[...] = g
        d_ref[...], nm_ref[...], nv_ref[...] = _adam_math(w_ref[...], g, m_ref[...], v_ref[...])

    spec = pl.BlockSpec((cols, rows) if turned else (tr, cols), lambda i, me_ref: (i, 0))
    shape = jax.ShapeDtypeStruct(w.shape, F32)
    return pl.pallas_call(
        body, name=name, out_shape=[shape] * 4,
        grid_spec=pltpu.PrefetchScalarGridSpec(
            num_scalar_prefetch=1, grid=(n_t,),
            in_specs=[pl.BlockSpec((tr, cols), lambda i, me_ref: (me_ref[0] * n_t + i, 0)),
                      pl.BlockSpec((N_DEV, tr, cols), lambda i, me_ref: (0, i, 0)), spec, spec, spec],
            out_specs=[spec] * 4),
        compiler_params=_params(("arbitrary",)),
    )(me.reshape(1).astype(jnp.int32), own, parts, w, m, v)


def _ada_grad_adam(c_all, dmod_all, w, m, v, tr):
    rows, cols = w.shape

    def body(c_ref, dm_ref, w_ref, m_ref, v_ref, g_ref, d_ref, nm_ref, nv_ref):
        cv = c_ref[...]
        act = cv * jax.nn.sigmoid(cv)
        dmod = dm_ref[:, pl.ds(pl.multiple_of(_index(_place()) * cols, LANES), cols)]
        g = lax.dot_general(act, dmod, TN, preferred_element_type=F32, precision=lax.Precision.HIGHEST)
        g_ref[...] = g
        d_ref[...], nm_ref[...], nv_ref[...] = _adam_math(w_ref[...], g, m_ref[...], v_ref[...])

    spec = pl.BlockSpec((tr, cols), lambda i: (i, 0))
    shape = jax.ShapeDtypeStruct((rows, cols), F32)
    return pl.pallas_call(
        body, name="ada_grad_adam", grid=(rows // tr,),
        in_specs=[pl.BlockSpec((N_DEV, tr), lambda i: (0, i)), pl.BlockSpec(dmod_all.shape, lambda i: (0, 0)), spec, spec, spec],
        out_specs=[spec] * 4, out_shape=[shape] * 4, compiler_params=_params(("arbitrary",)),
    )(c_all, dmod_all, w, m, v)


def _rope_tables(positions):
    inv_freq = ROPE_THETA ** (-jnp.arange(0, 2 * ROT_HALF, 2, dtype=F32) / (2 * ROT_HALF))
    ang = inv_freq[:, None] * positions.astype(F32)[None, :]
    rows = jnp.concatenate([jnp.cos(ang), jnp.sin(ang), jnp.ones_like(ang)], axis=0)
    spread = [[[0.0] * LANES for _ in range(3 * ROT_HALF)] for _ in range(3)]
    for lane in range(LANES):
        p, j = lane % HEAD_DIM, lane % ROT_HALF
        if p < ROT_HALF:
            spread[0][j][lane] = 1.0
            spread[1][ROT_HALF + j][lane] = -1.0
        elif p < 2 * ROT_HALF:
            spread[0][j][lane] = 1.0
            spread[2][ROT_HALF + j][lane] = 1.0
        else:
            spread[0][2 * ROT_HALF][lane] = 1.0
    return rows, jnp.array(spread, F32)


def _pad_rows(a, rows):
    return jnp.pad(a, ((0, rows - a.shape[0]), (0, 0)))


def _sequence_step(xs, target, rope, mods, gains, w_in_t, w_out_t, relay_ffn, fetch_ffn, send_grads, w_blk_b, b_pool_r,
                   pool_scale_r, conv_w_all, conv_b, after):
    sh_m, sc_m, gt_m, sh_f, sc_f, gt_f = mods
    g_pre_mix, g_post_mix, g_pre_ffn, g_post_ffn = gains
    h1, u_pool, qkv = _premix_inproj(xs, sh_m, sc_m, g_pre_mix, w_in_t, rope, after, tm=512)
    o_g, lse_g = _attn_fwd(qkv)
    x1, y1, h2, cat, attn, lse_all = _mix_out(xs, u_pool, o_g, lse_g, w_blk_b, b_pool_r, pool_scale_r, w_out_t,
                                              gt_m, g_post_mix, g_pre_ffn, sc_f, sh_f, tm=256)
    relay_ffn(x1)
    w_up_t, w_down_f = fetch_ffn(x1)
    gate, a_ffn, act, vd, dy2, dout, sums_ffn, loss_loc = _ffn_fwd_loss(h2, x1, target, w_up_t, w_down_f, conv_w_all, conv_b,
                                                              gt_f, g_post_ffn, tm=256, ck=256)

    dgc, dval, dw_down, dconv = _ffn_bwd_act(dy2, gate, a_ffn, act, vd, w_down_f, tm=512, tf=1408, ck=256)
    dup, dh2 = _ffn_bwd_up(dgc, dval, w_up_t, conv_w_all, tm=256)
    dw_up_t = _wgrad(dup, h2, "wgrad_up", tk=2048, tmm=1408)
    token = send_grads("ffn", [dw_up_t, dw_down], [])
    dx1, dpool, dattn, delta, dw_out_t, sums_mix = _mix_bwd(dh2, dout, x1, y1, cat, attn, w_out_t, sc_f,
                                                           g_pre_ffn, gt_m, g_post_mix, token, tm=256)
    du, dw_blk, sums_pool = _pool_bwd(dpool, u_pool, w_blk_b, b_pool_r, pool_scale_r, tm=512)
    token = send_grads("out", [dw_out_t], [sums_mix, sums_ffn, sums_pool, dw_blk, dconv, loss_loc])
    dproj, dw_in_t = _dproj_wgrad_in(du, _attn_bwd(qkv, dattn, lse_all, delta, token), rope, h1, tm=512, cm=512)
    token = send_grads("in", [dw_in_t], [])
    grad_x, sums_in = _inproj_bwd(dproj, w_in_t, xs, dx1, sc_m, g_pre_mix, token, tm=256)
    return (loss_loc, grad_x, dw_in_t, dw_out_t, dw_up_t, dw_down, dw_blk, dconv,
            sums_in, sums_mix, sums_ffn, sums_pool)


def kernel(x, c, positions, w_ada, b_ada, g_pre_mix, g_post_mix, g_pre_ffn, g_post_ffn, w_in, w_pool, b_pool, pool_scale, w_out, w_up, conv_w, conv_b, w_down, loss_target, m_w_ada, m_b_ada, m_g_pre_mix, m_g_post_mix, m_g_pre_ffn, m_g_post_ffn, m_w_in, m_w_pool, m_b_pool, m_pool_scale, m_w_out, m_w_up, m_conv_w, m_conv_b, m_w_down, v_w_ada, v_b_ada, v_g_pre_mix, v_g_post_mix, v_g_pre_ffn, v_g_post_ffn, v_w_in, v_w_pool, v_b_pool, v_pool_scale, v_w_out, v_w_up, v_conv_w, v_conv_b, v_w_down):
    s_len, d = x.shape[1], x.shape[2]
    d_ff = w_down.shape[1] * N_DEV
    me = _index(_place())
    xs, target = x[0], loss_target[0]

    c_all, mod, taps_all, (w_in_t, w_out_t) = _entry_exchange(
        jnp.broadcast_to(c, (8, d)), w_ada[0], b_ada, _pad_rows(conv_w[0], 8),
        [w_in[0].T.astype(BF16), w_out[0].T.astype(BF16)])
    c_all = c_all[:, 0, :]
    conv_w_all = jnp.transpose(taps_all[:, :3, :], (1, 0, 2)).reshape(3, d_ff)
    sh_m, sc_m, gt_m, sh_f, sc_f, gt_f = [mod[:, 0, :].reshape(1, -1)[:, k * d:(k + 1) * d] for k in range(6)]

    rope = _rope_tables(positions[0])
    w_blk = jnp.zeros((256, 256), F32)
    for gi in range(4):
        w_blk = lax.dynamic_update_slice(w_blk, w_pool[0, gi], (gi * HEAD_DIM, gi * HEAD_DIM))
    w_blk_b = w_blk.astype(BF16)
    b_pool_r, pool_scale_r = b_pool.reshape(1, 256), pool_scale.reshape(1, 256)

    lands = [_landing_zone(s, me, "land_" + nm, 176) for s, nm in ((w_up[0].T, "w_up"), (w_down[0], "w_down"))]
    w_in_t, conv_w_all, *lands = lax.optimization_barrier((w_in_t, conv_w_all, *lands))
    w_send, w_recv, w_src, w_land, w_token = _exchange_start("gather_ici", [], lands, "ffn_weights_ici_start")
    relay = []

    def relay_ffn(after):
        _, blocks = _exchange_wait("gather_ici", w_send, w_recv, w_src, w_land, after, "ffn_weights_ici_wait")
        relay.extend(_exchange_start("gather_d2d", [], blocks, "ffn_weights_d2d_start"))

    def fetch_ffn(after):
        return _exchange_wait("gather_d2d", relay[0], relay[1], [], relay[3], after, "ffn_weights_d2d_wait")[1]

    flights = {}

    def send_grads(tag, slabs, whole):
        lands = [lax.empty((N_DEV, g.shape[0] // N_DEV, g.shape[1]), g.dtype) for g in slabs]
        lands += [lax.empty((N_DEV,) + a.shape, F32) for a in whole]
        modes = ("scatter",) * len(slabs) + ("allgather",) * len(whole)
        flights[tag] = (modes, *_exchange_start(modes, slabs + whole, lands, f"grads_{tag}_start"))
        return flights[tag][5]

    def arrived(tag, after):
        return _exchange_wait(*flights[tag][:5], after, f"grads_{tag}_wait")

    _, grad_x, *_, sums_in, _, _, _ = _sequence_step(
        xs, target, rope, (sh_m, sc_m, gt_m, sh_f, sc_f, gt_f), (g_pre_mix, g_post_mix, g_pre_ffn, g_post_ffn),
        w_in_t, w_out_t, relay_ffn, fetch_ffn, send_grads, w_blk_b, b_pool_r, pool_scale_r, conv_w_all, conv_b,
        w_token)

    send_grads("last", [], [sums_in])

    (own_up, own_down), (parts_up, parts_down) = arrived("ffn", flights["last"][5])
    new_up = _sum_adam(own_up, parts_up, w_up[0].T, m_w_up[0].T, v_w_up[0].T, me, "adam_w_up", 352)
    new_down = _sum_adam(own_down, parts_down, w_down[0], m_w_down[0], v_w_down[0], me, "adam_w_down", 176)
    (own_out, *small), (parts_out, *gathered) = arrived("out", new_down[0])
    new_out = _sum_adam(own_out, parts_out, w_out[0], m_w_out[0], v_w_out[0], me, "adam_w_out", 128)
    (own_in,), (parts_in,) = arrived("in", new_out[0])
    new_in = _sum_adam(own_in, parts_in, w_in[0].T, m_w_in[0].T, v_w_in[0].T, me, "adam_w_in", 160)
    big = {"w_up": [a.T for a in new_up], "w_down": new_down, "w_out": new_out, "w_in": [a.T for a in new_in]}

    rep_w = [b_ada, g_pre_mix, g_post_mix, g_pre_ffn, g_post_ffn, w_pool, b_pool, pool_scale, conv_b]
    rep_m = [m_b_ada, m_g_pre_mix, m_g_post_mix, m_g_pre_ffn, m_g_post_ffn, m_w_pool, m_b_pool, m_pool_scale, m_conv_b]
    rep_v = [v_b_ada, v_g_pre_mix, v_g_post_mix, v_g_pre_ffn, v_g_post_ffn, v_w_pool, v_b_pool, v_pool_scale, v_conv_b]
    mine_last, got_last = arrived("last", new_in[0])
    small, gathered = [*mine_last, *small], [*got_last, *gathered]
    rep_out, dmod_all, dconv_tot, loss_tot = _small_sum_adam(small, gathered, rep_w, rep_m, rep_v)
    g_rep, d_rep, nm_rep, nv_rep = (rep_out[k::4] for k in range(4))

    fcol = d_ff // N_DEV
    taps = lambda a: jnp.transpose(a, (1, 0, 2))
    g_cw = lax.dynamic_slice(dconv_tot, (0, me * fcol), (3, fcol))[None]
    d_cw, nm_cw, nv_cw = [taps(a) for a in _adam(taps(conv_w), taps(g_cw), taps(m_conv_w), taps(v_conv_w), "adam_conv_w")]

    g_ada, d_ada, nm_ada, nv_ada = _ada_grad_adam(c_all, dmod_all, w_ada[0], m_w_ada[0], v_w_ada[0], 256)

    loss = loss_tot[0, 0]

    def group(k):
        rep = (g_rep, d_rep, nm_rep, nv_rep)[k]
        ada = (g_ada, d_ada, nm_ada, nv_ada)[k][None]
        cw = (g_cw, d_cw, nm_cw, nv_cw)[k]
        return [ada, rep[0], rep[1], rep[2], rep[3], rep[4], big["w_in"][k][None], rep[5], rep[6], rep[7],
                big["w_out"][k][None], big["w_up"][k][None], cw, rep[8], big["w_down"][k][None]]

    return (loss, grad_x[None], *group(0), *group(1), *group(2), *group(3))
```

```python
import functools
import math

import jax
import jax.numpy as jnp
from jax import lax
from jax.experimental import pallas as pl
from jax.experimental.pallas import tpu as pltpu

F32 = jnp.float32
BF16 = jnp.bfloat16
MESH = pl.DeviceIdType.MESH

N_DEV = 8
HEAD_DIM = 64
ROT_HALF = 8
ROPE_THETA = 500000.0
POOL_WINDOWS = (2, 4, 8, 16)
DILATIONS = (1, 4, 16)
BLOCK = 128
NORM_EPS = 1e-6
HALO = 16
MASKED = -1e30
ATTN_FWD_UNROLL = 8
ATTN_BWD_UNROLL = 8

ADAM_LR = 0.001
ADAM_B1 = 0.9
ADAM_B2 = 0.999
ADAM_EPS = 1e-08
ADAM_WD = 0.01
ADAM_STEP = 10

V7X_VMEM_LIMIT = 56 * 1024 * 1024
LANES = 128

NT = (((1,), (1,)), ((), ()))
NN = (((1,), (0,)), ((), ()))
TN = (((0,), (0,)), ((), ()))


def _dot(a, b, dims):
    return lax.dot_general(a, b, dims, preferred_element_type=F32)


def _params(sem=None, vmem=V7X_VMEM_LIMIT):
    if sem is None:
        return pltpu.CompilerParams(vmem_limit_bytes=vmem)
    return pltpu.CompilerParams(dimension_semantics=sem, vmem_limit_bytes=vmem)


def _rstd(v):
    return lax.rsqrt(jnp.mean(v * v, axis=-1, keepdims=True) + NORM_EPS)


def _norm_bwd(dn, n, rstd):
    return rstd * (dn - n * jnp.mean(dn * n, axis=-1, keepdims=True))


def _rope_lanes(cs_ref, spread_ref):
    return [lax.dot_general(cs_ref[...], spread_ref[k], TN, preferred_element_type=F32, precision=lax.Precision.HIGHEST)
            for k in range(3)]


def _rope_fwd(p, lanes):
    return p * lanes[0] + pltpu.roll(p, LANES - ROT_HALF, 1) * lanes[1] + pltpu.roll(p, ROT_HALF, 1) * lanes[2]


def _rope_bwd(dp, lanes):
    return dp * lanes[0] + pltpu.roll(dp * lanes[1], ROT_HALF, 1) + pltpu.roll(dp * lanes[2], LANES - ROT_HALF, 1)


def _gelu_parts(v):
    k2 = 2.0 * math.sqrt(2.0 / math.pi)
    c = 0.044715
    v2 = v * v
    s = jax.nn.sigmoid(v * (k2 + (k2 * c) * v2))
    g = v * s
    dg = s + g * (1.0 - s) * (k2 + (3.0 * k2 * c) * v2)
    return g, dg


def _halo_before(i, tile):
    return jnp.maximum(i * (tile // HALO) - 1, 0)


def _premix_inproj(x, sh, sc, g, w_in_t, rope, after, tm):
    s_len, d = x.shape
    n_proj = w_in_t.shape[0]
    n_slab = (n_proj - 256) // LANES

    def body(x_ref, sh_ref, sc_ref, g_ref, w_ref, cs_ref, spread_ref, after_ref, h_ref, up_ref, qkv_ref):
        xv = x_ref[...]
        h = (xv * _rstd(xv) * g_ref[...]) * (1.0 + sc_ref[...]) + sh_ref[...]
        hb = h.astype(BF16)
        h_ref[...] = hb
        up_ref[...] = _dot(hb, w_ref[0:256, :], NT)
        lanes = _rope_lanes(cs_ref, spread_ref)
        for pair in range(n_slab // 2):
            p = _dot(hb, w_ref[256 + 256 * pair:512 + 256 * pair, :], NT)
            for half in range(2):
                ph = p[:, half * LANES:(half + 1) * LANES]
                if pair < 6:
                    ph = _rope_fwd(ph, lanes)
                if pair < 3:
                    ph = ph * (HEAD_DIM ** -0.5)
                qkv_ref[2 * pair + half] = ph

    vec = pl.BlockSpec((1, d), lambda i: (0, 0))
    return pl.pallas_call(
        body, name="premix_inproj", grid=(s_len // tm,),
        in_specs=[pl.BlockSpec((tm, d), lambda i: (i, 0)), vec, vec, vec,
                  pl.BlockSpec((n_proj, d), lambda i: (0, 0)),
                  pl.BlockSpec((rope[0].shape[0], tm), lambda i: (0, i)), pl.BlockSpec(rope[1].shape, lambda i: (0, 0, 0)),
                  pl.BlockSpec(memory_space=pl.ANY)],
        out_specs=[pl.BlockSpec((tm, d), lambda i: (i, 0)),
                   pl.BlockSpec((tm, 256), lambda i: (i, 0)),
                   pl.BlockSpec((n_slab, tm, LANES), lambda i: (0, i, 0))],
        out_shape=[jax.ShapeDtypeStruct((s_len, d), BF16),
                   jax.ShapeDtypeStruct((s_len, 256), F32),
                   jax.ShapeDtypeStruct((n_slab, s_len, LANES), F32)],
        compiler_params=_params(("arbitrary",)),
    )(x, sh, sc, g, w_in_t, *rope, after)


def _block_rows(n, r, dil):
    start = n * (BLOCK * dil) + r
    if dil == 1:
        return pl.ds(pl.multiple_of(start, BLOCK), BLOCK)
    return pl.ds(start, BLOCK, stride=dil)


def _band_mask(n):
    ri = lax.broadcasted_iota(jnp.int32, (BLOCK, 2 * BLOCK), 0)
    cj = lax.broadcasted_iota(jnp.int32, (BLOCK, 2 * BLOCK), 1)
    cur = (cj >= BLOCK) & (cj - BLOCK <= ri)
    prev = (cj < BLOCK) & (cj >= ri) & (n > 0)
    return cur | prev


def _attn_fwd(qkv):
    s_len = qkv.shape[1]
    n_g = len(DILATIONS)

    def body(q_ref, k_ref, v_ref, o_ref, lse_ref):
        lane = lax.broadcasted_iota(jnp.int32, (BLOCK, LANES), 1)
        first = lane < HEAD_DIM

        def group(dil):
            nb = s_len // (BLOCK * dil)

            def block(t, carry):
                r, n = t // nb, t % nb
                cur = _block_rows(n, r, dil)
                prev = _block_rows(jnp.maximum(n - 1, 0), r, dil)
                q = q_ref[0, cur, :]
                kcat = jnp.concatenate([k_ref[0, prev, :], k_ref[0, cur, :]], axis=0).astype(BF16)
                vcat = jnp.concatenate([v_ref[0, prev, :], v_ref[0, cur, :]], axis=0).astype(BF16)
                valid = _band_mask(n)
                q2 = jnp.concatenate([jnp.where(first, q, 0.0), jnp.where(first, 0.0, q)], axis=0).astype(BF16)
                s = jnp.where(jnp.concatenate([valid, valid], axis=0), _dot(q2, kcat, NT), MASKED)
                m = jnp.max(s, axis=-1, keepdims=True)
                p = jnp.exp(s - m)
                den = jnp.sum(p, axis=-1, keepdims=True)
                o2 = _dot(p.astype(BF16), vcat, NN) / den
                lse2 = m + jnp.log(den)
                o_ref[0, 0, cur, :] = jnp.where(first, o2[:BLOCK], o2[BLOCK:])
                lse_ref[0, 0, cur, :] = jnp.where(first, lse2[:BLOCK], lse2[BLOCK:])
                return carry

            lax.fori_loop(0, nb * dil, block, 0, unroll=ATTN_FWD_UNROLL)

        for gi, dil in enumerate(DILATIONS):
            pl.when(pl.program_id(0) == gi)(functools.partial(group, dil))

    def slab(base):
        return pl.BlockSpec((1, s_len, LANES), lambda g, s: (base + 2 * g + s, 0, 0))

    out = pl.BlockSpec((1, 1, s_len, LANES), lambda g, s: (g, s, 0, 0))
    shape = jax.ShapeDtypeStruct((n_g, 2, s_len, LANES), F32)
    return pl.pallas_call(
        body, name="attn_fwd", grid=(n_g, 2),
        in_specs=[slab(0), slab(6), slab(12)], out_specs=[out, out], out_shape=[shape, shape],
        compiler_params=_params(("arbitrary", "arbitrary")),
    )(qkv, qkv, qkv)


def _pool_mixed(u, halo, i, tm):
    ue = jnp.concatenate([halo, u], axis=0)
    s2 = ue + pltpu.roll(ue, 1, 0)
    s4 = s2 + pltpu.roll(s2, 2, 0)
    s8 = s4 + pltpu.roll(s4, 4, 0)
    s16 = s8 + pltpu.roll(s8, 8, 0)
    grp = lax.broadcasted_iota(jnp.int32, (tm, 256), 1) // HEAD_DIM
    pick = lambda a, b, c, e: jnp.where(grp == 0, a, jnp.where(grp == 1, b, jnp.where(grp == 2, c, e)))
    win_sum = pick(s2[HALO:], s4[HALO:], s8[HALO:], s16[HALO:])
    pos = (i * tm + lax.broadcasted_iota(jnp.int32, (tm, 256), 0)).astype(F32)
    count = jnp.minimum(pos + 1.0, pick(*[float(w) for w in POOL_WINDOWS]))
    return win_sum / count - u, count


def _mix_out(x, u_pool, o_g, lse_g, w_blk, b_pool, pool_scale, w_out_t, gt_m, g_post_mix, g_pre_ffn, sc_f, sh_f, tm):
    s_len, d = x.shape

    def body(x_ref, u_ref, uh_ref, o_ref, l_ref, wb_ref, bp_ref, ps_ref, wo_ref,
             gt_ref, g1_ref, g2_ref, sc_ref, sh_ref,
             x1_ref, y1_ref, h2_ref, cat_ref, attn_ref, lall_ref):
        (o0, o1, o2), (l0, l1, l2) = (o_ref.at[g] for g in range(3)), (l_ref.at[g] for g in range(3))
        i = pl.program_id(0)
        u = u_ref[...]
        halo = uh_ref[...] * (i > 0).astype(F32)
        mixed, _ = _pool_mixed(u, halo, i, tm)
        y = _dot(mixed.astype(BF16), wb_ref[...], NN) + bp_ref[...]
        pool = y * ps_ref[...]
        attn = []
        for s in range(2):
            la, lb, lc = l0[s], l1[s], l2[s]
            mx = jnp.maximum(jnp.maximum(la, lb), lc)
            ea, eb, ec = jnp.exp(la - mx), jnp.exp(lb - mx), jnp.exp(lc - mx)
            den = ea + eb + ec
            lall_ref[s] = mx + jnp.log(den)
            attn.append((ea / den) * o0[s] + (eb / den) * o1[s] + (ec / den) * o2[s])
        attn = jnp.concatenate(attn, axis=1)
        attn_ref[...] = attn
        cat = jnp.concatenate([pool, attn], axis=1).astype(BF16)
        cat_ref[...] = cat
        y1 = _dot(cat, wo_ref[...], NT)
        y1_ref[...] = y1.astype(BF16)
        x1 = x_ref[...] + gt_ref[...] * (y1 * _rstd(y1) * g1_ref[...])
        x1_ref[...] = x1
        h2 = (x1 * _rstd(x1) * g2_ref[...]) * (1.0 + sc_ref[...]) + sh_ref[...]
        h2_ref[...] = h2.astype(BF16)

    tile = lambda w: pl.BlockSpec((tm, w), lambda i: (i, 0))
    slab = pl.BlockSpec((2, tm, LANES), lambda i: (0, i, 0))
    groups = pl.BlockSpec((len(DILATIONS), 2, tm, LANES), lambda i: (0, 0, i, 0))
    const = lambda a: pl.BlockSpec(a.shape, lambda i: (0,) * a.ndim)
    return pl.pallas_call(
        body, name="mix_out", grid=(s_len // tm,),
        in_specs=[tile(d), tile(256), pl.BlockSpec((HALO, 256), lambda i: (_halo_before(i, tm), 0)),
                  groups, groups,
                  const(w_blk), const(b_pool), const(pool_scale), const(w_out_t),
                  const(gt_m), const(g_post_mix), const(g_pre_ffn), const(sc_f), const(sh_f)],
        out_specs=[tile(d), tile(d), tile(d), tile(512), tile(256), slab],
        out_shape=[jax.ShapeDtypeStruct((s_len, d), F32), jax.ShapeDtypeStruct((s_len, d), BF16),
                   jax.ShapeDtypeStruct((s_len, d), BF16), jax.ShapeDtypeStruct((s_len, 512), BF16),
                   jax.ShapeDtypeStruct((s_len, 256), F32), jax.ShapeDtypeStruct((2, s_len, LANES), F32)],
        compiler_params=_params(("arbitrary",)),
    )(x, u_pool, u_pool, o_g, lse_g, w_blk, b_pool, pool_scale, w_out_t, gt_m, g_post_mix, g_pre_ffn, sc_f, sh_f)


def _conv_gate(gate_ext, cw, cb):
    gc = gate_ext * cw[2:3, :] + pltpu.roll(gate_ext, 1, 0) * cw[1:2, :] + pltpu.roll(gate_ext, 2, 0) * cw[0:1, :]
    return gc[HALO:] + cb


def _ffn_fwd_loss(h2, x1, target, w_up_t, w_down, conv_w, conv_b, gt_f, g_post_ffn, tm, ck):
    s_len, d = x1.shape
    d_ff = w_down.shape[0]
    n_t, n_c = s_len // tm, d_ff // ck

    def body(h_ref, hh_ref, x1_ref, tgt_ref, wg_ref, wv_ref, wd_ref, cw_ref, cb_ref, gt_ref, g_ref,
             gate_ref, a_ref, act_ref, vd_ref, dy2_ref, dout_ref, sums_ref, loss_ref, acc_ref):
        i = pl.program_id(0)

        @pl.when(i == 0)
        def _():
            sums_ref[...] = jnp.zeros_like(sums_ref)
            loss_ref[...] = jnp.zeros_like(loss_ref)
            acc_ref[...] = jnp.zeros_like(acc_ref)

        def finish(live):
            y2 = acc_ref[...]
            rstd = _rstd(y2)
            n = y2 * rstd
            rn = n * g_ref[...]
            err = x1_ref[...] + gt_ref[...] * rn - tgt_ref[...]
            keep = lambda v: jnp.where(live, v, 0.0)
            loss_ref[...] += keep(0.5 * jnp.sum(jnp.mean(err * err, axis=-1, keepdims=True), axis=0, keepdims=True))
            dout = err * (1.0 / d)
            dout_ref[...] = dout
            drn = dout * gt_ref[...]
            sums_ref[0:1, :] += keep(jnp.sum(dout * rn, axis=0, keepdims=True))
            sums_ref[1:2, :] += keep(jnp.sum(drn * n, axis=0, keepdims=True))
            dy2_ref[...] = _norm_bwd(drn * g_ref[...], n, rstd).astype(BF16)

        @pl.when(i < n_t)
        def _():
            h = h_ref[...]
            h_ext = jnp.concatenate([hh_ref[...], h], axis=0)
            row = lax.broadcasted_iota(jnp.int32, (tm + HALO, ck), 0)
            no_halo = (row < HALO) & (i == 0)

            def up(c):
                cs = slice(c * ck, (c + 1) * ck)
                return jnp.where(no_halo, 0.0, _dot(h_ext, wg_ref[cs, :], NT)), _dot(h, wv_ref[cs, :], NT)

            part = None
            nxt = up(0)
            finish(i > 0)
            for c in range(n_c):
                cs = slice(c * ck, (c + 1) * ck)
                gate_ext, val = nxt
                if c + 1 < n_c:
                    nxt = up(c + 1)
                act, dact = _gelu_parts(_conv_gate(gate_ext, cw_ref[:, cs], cb_ref[:, cs]))
                a = (act * val).astype(BF16)
                gate_ref[:, cs] = gate_ext[HALO:].astype(BF16)
                a_ref[:, cs] = a
                act_ref[:, cs] = act.astype(BF16)
                vd_ref[:, cs] = (val * dact).astype(BF16)
                p = _dot(a, wd_ref[cs, :], NN)
                part = p if part is None else part + p
            acc_ref[...] = part

        @pl.when(i == n_t)
        def _():
            finish(True)

    this = lambda i: jnp.minimum(i, n_t - 1)
    before = lambda i: jnp.maximum(i - 1, 0)
    tok = lambda w, at: pl.BlockSpec((tm, w), lambda i: (at(i), 0))
    vec = pl.BlockSpec((1, d), lambda i: (0, 0))
    once = lambda shape, imap: pl.BlockSpec(shape, imap, pipeline_mode=pl.Buffered(1))
    return pl.pallas_call(
        body, name="ffn_fwd_loss", grid=(n_t + 1,),
        in_specs=[tok(d, this), pl.BlockSpec((HALO, d), lambda i: (_halo_before(this(i), tm), 0)),
                  tok(d, before), tok(d, before),
                  once((d_ff, d), lambda i: (0, 0)), once((d_ff, d), lambda i: (1, 0)), once((d_ff, d), lambda i: (0, 0)),
                  pl.BlockSpec((3, d_ff), lambda i: (0, 0)), pl.BlockSpec((1, d_ff), lambda i: (0, 0)), vec, vec],
        out_specs=[tok(d_ff, this)] * 4 + [tok(d, before), tok(d, before), pl.BlockSpec((8, d), lambda i: (0, 0)),
                                          pl.BlockSpec((8, LANES), lambda i: (0, 0))],
        out_shape=[jax.ShapeDtypeStruct((s_len, d_ff), BF16)] * 4
        + [jax.ShapeDtypeStruct((s_len, d), BF16), jax.ShapeDtypeStruct((s_len, d), F32),
           jax.ShapeDtypeStruct((8, d), F32), jax.ShapeDtypeStruct((8, LANES), F32)],
        scratch_shapes=[pltpu.VMEM((tm, d), F32)],
        compiler_params=_params(("arbitrary",)),
    )(h2, h2, x1, target, w_up_t, w_up_t, w_down, conv_w, conv_b, gt_f, g_post_ffn)


def _ffn_bwd_act(dy2, gate, a, act, vd, w_down, tm, tf, ck):
    s_len, d = dy2.shape
    d_ff = w_down.shape[0]
    n_t = s_len // tm
    chunks = [slice(lo, min(lo + ck, tf)) for lo in range(0, tf, ck)]

    def body(dy_ref, g_ref, gh_ref, a_ref, act_ref, vd_ref, wd_ref, dgc_ref, dval_ref, dwd_ref, dconv_ref, acc_ref):
        i = pl.program_id(1)

        @pl.when(i == 0)
        def _():
            acc_ref[...] = jnp.zeros_like(acc_ref)
            dconv_ref[...] = jnp.zeros_like(dconv_ref)

        dy = dy_ref[...]

        def down(cs):
            return _dot(dy, wd_ref[cs, :], NT)

        nxt = down(chunks[0])
        for c, cs in enumerate(chunks):
            width = cs.stop - cs.start
            da = nxt
            if c + 1 < len(chunks):
                nxt = down(chunks[c + 1])
            acc_ref[cs, :] += _dot(a_ref[:, cs], dy, TN)
            row = lax.broadcasted_iota(jnp.int32, (tm + HALO, width), 0)
            gate_ext = jnp.where((row < HALO) & (i == 0), 0.0,
                                 jnp.concatenate([gh_ref[:, cs], g_ref[:, cs]], axis=0).astype(F32))
            dgc = da * vd_ref[:, cs].astype(F32)
            dgc_ref[:, cs] = dgc.astype(BF16)
            dval_ref[:, cs] = (da * act_ref[:, cs].astype(F32)).astype(BF16)
            rows = [jnp.sum(dgc * pltpu.roll(gate_ext, 2 - k, 0)[HALO:], axis=0, keepdims=True) for k in range(2)]
            rows += [jnp.sum(dgc * gate_ext[HALO:], axis=0, keepdims=True), jnp.sum(dgc, axis=0, keepdims=True),
                     jnp.zeros((4, width), F32)]
            dconv_ref[:, cs] += jnp.concatenate(rows, axis=0)

        @pl.when(i == n_t - 1)
        def _():
            dwd_ref[...] = acc_ref[...].astype(BF16)

    tokf = pl.BlockSpec((tm, tf), lambda j, i: (i, j))
    return pl.pallas_call(
        body, name="ffn_bwd_act", grid=(d_ff // tf, n_t),
        in_specs=[pl.BlockSpec((tm, d), lambda j, i: (i, 0)), tokf,
                  pl.BlockSpec((HALO, tf), lambda j, i: (_halo_before(i, tm), j)), tokf, tokf, tokf,
                  pl.BlockSpec((tf, d), lambda j, i: (j, 0))],
        out_specs=[tokf, tokf, pl.BlockSpec((tf, d), lambda j, i: (j, 0)), pl.BlockSpec((8, tf), lambda j, i: (0, j))],
        out_shape=[jax.ShapeDtypeStruct((s_len, d_ff), BF16), jax.ShapeDtypeStruct((s_len, d_ff), BF16),
                   jax.ShapeDtypeStruct((d_ff, d), BF16), jax.ShapeDtypeStruct((8, d_ff), F32)],
        scratch_shapes=[pltpu.VMEM((tf, d), F32)],
        compiler_params=_params(("arbitrary", "arbitrary")),
    )(dy2, gate, gate, a, act, vd, w_down)


def _ffn_bwd_up(dgc, dval, w_up_t, conv_w, tm):
    s_len, d_ff = dgc.shape
    d = w_up_t.shape[1]
    n_t = s_len // tm

    def body(dg_ref, dgn_ref, dv_ref, cw_ref, w_ref, dup_ref, dh_ref):
        i = pl.program_id(0)
        nxt = dgn_ref[...].astype(F32) * (i < n_t - 1).astype(F32)
        ext = jnp.concatenate([dg_ref[...].astype(F32), nxt], axis=0)
        rows = tm + HALO
        dgate = (ext * cw_ref[2:3, :] + pltpu.roll(ext, rows - 1, 0) * cw_ref[1:2, :]
                 + pltpu.roll(ext, rows - 2, 0) * cw_ref[0:1, :])[:tm]
        dup = jnp.concatenate([dgate.astype(BF16), dv_ref[...]], axis=1)
        dup_ref[...] = dup
        dh_ref[...] = _dot(dup, w_ref[...], NN).astype(BF16)

    tokf = pl.BlockSpec((tm, d_ff), lambda i: (i, 0))
    return pl.pallas_call(
        body, name="ffn_bwd_up", grid=(n_t,),
        in_specs=[tokf, pl.BlockSpec((HALO, d_ff), lambda i: (jnp.minimum((i + 1) * (tm // HALO), s_len // HALO - 1), 0)),
                  tokf, pl.BlockSpec((3, d_ff), lambda i: (0, 0)), pl.BlockSpec((2 * d_ff, d), lambda i: (0, 0))],
        out_specs=[pl.BlockSpec((tm, 2 * d_ff), lambda i: (i, 0)), pl.BlockSpec((tm, d), lambda i: (i, 0))],
        out_shape=[jax.ShapeDtypeStruct((s_len, 2 * d_ff), BF16), jax.ShapeDtypeStruct((s_len, d), BF16)],
        compiler_params=_params(("arbitrary",)),
    )(dgc, dgc, dval, conv_w, w_up_t)


def _mix_bwd(dh2, dout, x1, y1, cat, attn, w_out_t, sc_f, g_pre_ffn, gt_m, g_post_mix, after, tm):
    s_len, d = x1.shape
    n_t = s_len // tm

    def body(dh_ref, do_ref, x1_ref, y1_ref, cat_ref, at_ref, wo_ref, sc_ref, g2_ref, gt_ref, g1_ref, after_ref,
             dx1_ref, dpool_ref, dattn_ref, delta_ref, dwo_ref, sums_ref, acc_ref):
        i = pl.program_id(0)
        dh = dh_ref[...].astype(F32)
        x1 = x1_ref[...]
        r2 = _rstd(x1)
        n2 = x1 * r2
        ng = n2 * g2_ref[...]
        dng = dh * (1.0 + sc_ref[...])
        dx1 = do_ref[...] + _norm_bwd(dng * g2_ref[...], n2, r2)
        dx1_ref[...] = dx1
        y1 = y1_ref[...].astype(F32)
        r1 = _rstd(y1)
        n1 = y1 * r1
        drn = dx1 * gt_ref[...]
        dy1 = _norm_bwd(drn * g1_ref[...], n1, r1).astype(BF16)
        dcat = _dot(dy1, wo_ref[...], NN)
        dpool_ref[...] = dcat[:, 0:256]
        lane = lax.broadcasted_iota(jnp.int32, (tm, LANES), 1)
        first = lane < HEAD_DIM
        for s in range(2):
            da = dcat[:, 256 + s * LANES:256 + (s + 1) * LANES]
            dattn_ref[s] = da
            prod = da * at_ref[:, s * LANES:(s + 1) * LANES]
            tot = jnp.sum(prod, axis=-1, keepdims=True)
            lo = jnp.sum(jnp.where(first, prod, 0.0), axis=-1, keepdims=True)
            delta_ref[s] = jnp.where(first, lo, tot - lo)
        dwo = _dot(dy1, cat_ref[...], TN)
        sums = jnp.concatenate(
            [jnp.sum(dh, axis=0, keepdims=True), jnp.sum(dh * ng, axis=0, keepdims=True),
             jnp.sum(dng * n2, axis=0, keepdims=True), jnp.sum(dx1 * (n1 * g1_ref[...]), axis=0, keepdims=True),
             jnp.sum(drn * n1, axis=0, keepdims=True), jnp.zeros((3, d), F32)], axis=0)

        @pl.when(i == 0)
        def _():
            acc_ref[...] = dwo
            sums_ref[...] = sums

        @pl.when(i > 0)
        def _():
            acc_ref[...] += dwo
            sums_ref[...] += sums

        @pl.when(i == n_t - 1)
        def _():
            dwo_ref[...] = acc_ref[...].astype(BF16)

    tile = lambda w: pl.BlockSpec((tm, w), lambda i: (i, 0))
    slab = pl.BlockSpec((2, tm, LANES), lambda i: (0, i, 0))
    vec = pl.BlockSpec((1, d), lambda i: (0, 0))
    return pl.pallas_call(
        body, name="mix_bwd", grid=(n_t,),
        in_specs=[tile(d), tile(d), tile(d), tile(d), tile(512), tile(256),
                  pl.BlockSpec((d, 512), lambda i: (0, 0)), vec, vec, vec, vec, pl.BlockSpec(memory_space=pl.ANY)],
        out_specs=[tile(d), tile(256), slab, slab, pl.BlockSpec((d, 512), lambda i: (0, 0)),
                   pl.BlockSpec((8, d), lambda i: (0, 0))],
        out_shape=[jax.ShapeDtypeStruct((s_len, d), F32), jax.ShapeDtypeStruct((s_len, 256), F32),
                   jax.ShapeDtypeStruct((2, s_len, LANES), F32), jax.ShapeDtypeStruct((2, s_len, LANES), F32),
                   jax.ShapeDtypeStruct((d, 512), BF16), jax.ShapeDtypeStruct((8, d), F32)],
        scratch_shapes=[pltpu.VMEM((d, 512), F32)],
        compiler_params=_params(("arbitrary",)),
    )(dh2, dout, x1, y1, cat, attn, w_out_t, sc_f, g_pre_ffn, gt_m, g_post_mix, after)


def _pool_bwd(dpool, u_pool, w_blk, b_pool, pool_scale, tm):
    s_len = dpool.shape[0]
    n_t = s_len // tm

    def body(dp_ref, dpn_ref, u_ref, uh_ref, wb_ref, bp_ref, ps_ref, du_ref, dwp_ref, sums_ref, acc_ref):
        i = pl.program_id(0)
        u = u_ref[...]
        mixed, _ = _pool_mixed(u, uh_ref[...] * (i > 0).astype(F32), i, tm)
        mixed_b = mixed.astype(BF16)
        y = _dot(mixed_b, wb_ref[...], NN) + bp_ref[...]
        dp = dp_ref[...]
        dy = dp * ps_ref[...]
        dwb = _dot(mixed_b, dy.astype(BF16), TN)
        sums = jnp.concatenate([jnp.sum(dy, axis=0, keepdims=True), jnp.sum(dp * y, axis=0, keepdims=True),
                                jnp.zeros((6, 256), F32)], axis=0)
        dp_ext = jnp.concatenate([dp, dpn_ref[...] * (i < n_t - 1).astype(F32)], axis=0)
        dmix = _dot((dp_ext * ps_ref[...]).astype(BF16), wb_ref[...], NT)
        rows = tm + HALO
        grp = lax.broadcasted_iota(jnp.int32, (rows, 256), 1) // HEAD_DIM
        pick = lambda a, b, c, e: jnp.where(grp == 0, a, jnp.where(grp == 1, b, jnp.where(grp == 2, c, e)))
        pos = (i * tm + lax.broadcasted_iota(jnp.int32, (rows, 256), 0)).astype(F32)
        z = dmix / jnp.minimum(pos + 1.0, pick(*[float(w) for w in POOL_WINDOWS]))
        f2 = z + pltpu.roll(z, rows - 1, 0)
        f4 = f2 + pltpu.roll(f2, rows - 2, 0)
        f8 = f4 + pltpu.roll(f4, rows - 4, 0)
        f16 = f8 + pltpu.roll(f8, rows - 8, 0)
        du_ref[...] = (pick(f2, f4, f8, f16) - dmix)[:tm]

        @pl.when(i == 0)
        def _():
            acc_ref[...] = dwb
            sums_ref[...] = sums

        @pl.when(i > 0)
        def _():
            acc_ref[...] += dwb
            sums_ref[...] += sums

        @pl.when(i == n_t - 1)
        def _():
            full = acc_ref[...]
            for gi in range(len(POOL_WINDOWS)):
                lo = gi * HEAD_DIM
                dwp_ref[gi] = full[lo:lo + HEAD_DIM, lo:lo + HEAD_DIM]

    n_g = len(POOL_WINDOWS)
    tile = pl.BlockSpec((tm, 256), lambda i: (i, 0))
    const = lambda a: pl.BlockSpec(a.shape, lambda i: (0,) * a.ndim)
    return pl.pallas_call(
        body, name="pool_bwd", grid=(n_t,),
        in_specs=[tile, pl.BlockSpec((HALO, 256), lambda i: (jnp.minimum((i + 1) * (tm // HALO), s_len // HALO - 1), 0)),
                  tile, pl.BlockSpec((HALO, 256), lambda i: (_halo_before(i, tm), 0)),
                  const(w_blk), const(b_pool), const(pool_scale)],
        out_specs=[tile, pl.BlockSpec((n_g, HEAD_DIM, HEAD_DIM), lambda i: (0, 0, 0)), pl.BlockSpec((8, 256), lambda i: (0, 0))],
        out_shape=[jax.ShapeDtypeStruct((s_len, 256), F32), jax.ShapeDtypeStruct((n_g, HEAD_DIM, HEAD_DIM), F32),
                   jax.ShapeDtypeStruct((8, 256), F32)],
        scratch_shapes=[pltpu.VMEM((256, 256), F32)],
        compiler_params=_params(("arbitrary",)),
    )(dpool, dpool, u_pool, u_pool, w_blk, b_pool, pool_scale)


def _attn_bwd(qkv, dattn, lse_all, delta, after):
    s_len = qkv.shape[1]
    n_g = len(DILATIONS)

    def body(q_ref, k_ref, v_ref, do_ref, l_ref, dl_ref, after_ref, dq_ref, dk_ref, dv_ref):
        lane = lax.broadcasted_iota(jnp.int32, (BLOCK, LANES), 1)
        first = lane < HEAD_DIM

        def group(dil):
            nb = s_len // (BLOCK * dil)

            def block(t, carry):
                dk_part, dv_part = carry
                r, n = t // nb, t % nb
                cur = _block_rows(n, r, dil)
                prev = _block_rows(jnp.maximum(n - 1, 0), r, dil)
                q = q_ref[0, cur, :]
                do = do_ref[0, cur, :]
                lse = l_ref[0, cur, :]
                dlt = dl_ref[0, cur, :]
                kcat = jnp.concatenate([k_ref[0, prev, :], k_ref[0, cur, :]], axis=0).astype(BF16)
                vcat = jnp.concatenate([v_ref[0, prev, :], v_ref[0, cur, :]], axis=0).astype(BF16)
                valid = _band_mask(n)
                stack = lambda a: jnp.concatenate([jnp.where(first, a, 0.0), jnp.where(first, 0.0, a)], axis=0)
                rows2 = lambda a: jnp.concatenate([a[:, 0:1], a[:, HEAD_DIM:HEAD_DIM + 1]], axis=0)
                q2, do2 = stack(q).astype(BF16), stack(do).astype(BF16)
                valid2 = jnp.concatenate([valid, valid], axis=0)
                p = jnp.where(valid2, jnp.exp(_dot(q2, kcat, NT) - rows2(lse)), 0.0)
                ds = (p * (_dot(do2, vcat, NT) - rows2(dlt))).astype(BF16)
                dq2 = _dot(ds, kcat, NN)
                dq_ref[0, 0, cur, :] = jnp.where(first, dq2[:BLOCK], dq2[BLOCK:])
                dkc = _dot(ds, q2, TN)
                dvc = _dot(p.astype(BF16), do2, TN)
                dk_ref[0, 0, prev, :] = dk_part + dkc[:BLOCK]
                dv_ref[0, 0, prev, :] = dv_part + dvc[:BLOCK]
                dk_ref[0, 0, cur, :] = dkc[BLOCK:]
                dv_ref[0, 0, cur, :] = dvc[BLOCK:]
                return dkc[BLOCK:], dvc[BLOCK:]

            def blocks(tt, carry):
                for u in range(ATTN_BWD_UNROLL):
                    carry = block(tt * ATTN_BWD_UNROLL + u, carry)
                return carry

            zero = jnp.zeros((BLOCK, LANES), F32)
            lax.fori_loop(0, nb * dil // ATTN_BWD_UNROLL, blocks, (zero, zero))

        for gi, dil in enumerate(DILATIONS):
            pl.when(pl.program_id(1) == gi)(functools.partial(group, dil))

    def slab(base):
        return pl.BlockSpec((1, s_len, LANES), lambda s, g: (base + 2 * g + s, 0, 0))

    one = pl.BlockSpec((1, s_len, LANES), lambda s, g: (s, 0, 0))
    out = pl.BlockSpec((1, 1, s_len, LANES), lambda s, g: (g, s, 0, 0))
    shape = jax.ShapeDtypeStruct((n_g, 2, s_len, LANES), F32)
    return pl.pallas_call(
        body, name="attn_bwd", grid=(2, n_g),
        in_specs=[slab(0), slab(6), slab(12), one, one, one, pl.BlockSpec(memory_space=pl.ANY)],
        out_specs=[out, out, out], out_shape=[shape, shape, shape],
        compiler_params=_params(("arbitrary", "arbitrary")),
    )(qkv, qkv, qkv, dattn, lse_all, delta, after)


def _dproj_wgrad_in(du, dqkv, rope, h1, tm, cm):
    s_len = du.shape[0]
    d = h1.shape[1]
    n_proj = 256 + 18 * LANES
    n_t = s_len // tm

    def body(du_ref, dq_ref, dk_ref, dv_ref, cs_ref, spread_ref, h_ref, dproj_ref, dw_ref, acc_ref):
        i = pl.program_id(0)

        @pl.when(i == 0)
        def _():
            acc_ref[...] = jnp.zeros_like(acc_ref)

        dproj_ref[:, 0:256] = du_ref[...].astype(BF16)
        lanes = _rope_lanes(cs_ref, spread_ref)
        col = 256
        for kind, dref in enumerate((dq_ref, dk_ref, dv_ref)):
            for grp in range(3):
                for s in range(2):
                    piece = dref[grp, s]
                    if kind < 2:
                        piece = _rope_bwd(piece, lanes)
                    if kind == 0:
                        piece = piece * (HEAD_DIM ** -0.5)
                    dproj_ref[:, col:col + LANES] = piece.astype(BF16)
                    col += LANES

        for c0 in range(0, n_proj, cm):
            acc_ref[c0:c0 + cm, :] += _dot(dproj_ref[:, c0:c0 + cm], h_ref[...], TN)

        @pl.when(i == n_t - 1)
        def _():
            dw_ref[...] = acc_ref[...].astype(BF16)

    groups = pl.BlockSpec((len(DILATIONS), 2, tm, LANES), lambda i: (0, 0, i, 0))
    return pl.pallas_call(
        body, name="dproj_wgrad_in", grid=(n_t,),
        in_specs=[pl.BlockSpec((tm, 256), lambda i: (i, 0))] + [groups] * 3
        + [pl.BlockSpec((rope[0].shape[0], tm), lambda i: (0, i)), pl.BlockSpec(rope[1].shape, lambda i: (0, 0, 0)),
           pl.BlockSpec((tm, d), lambda i: (i, 0))],
        out_specs=[pl.BlockSpec((tm, n_proj), lambda i: (i, 0)), pl.BlockSpec((n_proj, d), lambda i: (0, 0))],
        out_shape=[jax.ShapeDtypeStruct((s_len, n_proj), BF16), jax.ShapeDtypeStruct((n_proj, d), BF16)],
        scratch_shapes=[pltpu.VMEM((n_proj, d), F32)],
        compiler_params=_params(("arbitrary",)),
    )(du, *dqkv, *rope, h1)


def _inproj_bwd(dproj, w_in_t, x, dx1, sc_m, g_pre_mix, after, tm):
    s_len, d = x.shape
    n_proj = w_in_t.shape[0]
    n_t = s_len // tm

    def body(dproj_ref, w_ref, x_ref, dx1_ref, sc_ref, g_ref, after_ref, dx_ref, sums_ref):
        i = pl.program_id(0)
        halves = [slice(0, tm // 2), slice(tm // 2, tm)]
        dhs = [_dot(dproj_ref[rs, :], w_ref[...], NN) for rs in halves]
        sums = None
        for rs, dh in zip(halves, dhs):
            xv = x_ref[rs, :]
            r = _rstd(xv)
            n = xv * r
            dng = dh * (1.0 + sc_ref[...])
            dx_ref[rs, :] = dx1_ref[rs, :] + _norm_bwd(dng * g_ref[...], n, r)
            part = jnp.concatenate([jnp.sum(dh, axis=0, keepdims=True), jnp.sum(dh * (n * g_ref[...]), axis=0, keepdims=True),
                                    jnp.sum(dng * n, axis=0, keepdims=True), jnp.zeros((5, d), F32)], axis=0)
            sums = part if sums is None else sums + part

        @pl.when(i == 0)
        def _():
            sums_ref[...] = sums

        @pl.when(i > 0)
        def _():
            sums_ref[...] += sums

    tile = lambda w: pl.BlockSpec((tm, w), lambda i: (i, 0))
    vec = pl.BlockSpec((1, d), lambda i: (0, 0))
    return pl.pallas_call(
        body, name="inproj_bwd", grid=(n_t,),
        in_specs=[tile(n_proj), pl.BlockSpec((n_proj, d), lambda i: (0, 0)), tile(d), tile(d), vec, vec,
                  pl.BlockSpec(memory_space=pl.ANY)],
        out_specs=[tile(d), pl.BlockSpec((8, d), lambda i: (0, 0))],
        out_shape=[jax.ShapeDtypeStruct((s_len, d), F32), jax.ShapeDtypeStruct((8, d), F32)],
        compiler_params=_params(("arbitrary",)),
    )(dproj, w_in_t, x, dx1, sc_m, g_pre_mix, after)


def _wgrad(a, b, name, tk, tmm):
    s_len, m = a.shape
    n = b.shape[1]
    n_k = s_len // tk

    def body(a_ref, b_ref, o_ref, acc_ref):
        k = pl.program_id(1)
        part = _dot(a_ref[...], b_ref[...], TN)

        @pl.when(k == 0)
        def _():
            acc_ref[...] = part

        @pl.when(k > 0)
        def _():
            acc_ref[...] += part

        @pl.when(k == n_k - 1)
        def _():
            o_ref[...] = acc_ref[...].astype(BF16)

    return pl.pallas_call(
        body, name=name, grid=(m // tmm, n_k),
        in_specs=[pl.BlockSpec((tk, tmm), lambda j, k: (k, j)), pl.BlockSpec((tk, n), lambda j, k: (k, 0))],
        out_specs=pl.BlockSpec((tmm, n), lambda j, k: (j, 0)),
        out_shape=jax.ShapeDtypeStruct((m, n), BF16),
        scratch_shapes=[pltpu.VMEM((tmm, n), F32)],
        compiler_params=_params(("arbitrary", "arbitrary")),
    )(a, b)


def _place():
    return lax.axis_index("x"), lax.axis_index("y"), lax.axis_index("c")


def _peer(k):
    x, y, c = _place()
    bx, by, bc = (k >> 2) & 1, (k >> 1) & 1, k & 1
    return (x ^ bx if bx else x, y ^ by if by else y, c ^ bc if bc else c)


def _index(pos):
    return 4 * pos[0] + 2 * pos[1] + pos[2]


def _entry_exchange(c_rows, w_ada, b_ada, taps, shards, later):
    d = c_rows.shape[1]
    ncol = w_ada.shape[1]
    n_w, n_p = len(shards), len(later)

    def body(c_ref, w_ref, b_ref, t_ref, *rest):
        srcs, rest = rest[:n_w], rest[n_w:]
        later_refs, rest = rest[:n_p], rest[n_p:]
        (call_ref, mod_ref, tall_ref), rest = rest[:3], rest[3:]
        outs, rest = rest[:n_w], rest[n_w:]
        zones, rest = rest[:n_p], rest[n_p:]
        stage_ref, s_send, s_recv, w_send, w_recv, local_sems = rest[:6]
        wide, narrow, place_sems = rest[6:6 + n_p], rest[6 + n_p:6 + 2 * n_p], rest[6 + 2 * n_p]
        x, y, c = _place()
        here, sibling = (x, y, c), (x, y, 1 - c)
        chips = [(1 - x, y), (x, 1 - y), (1 - x, 1 - y)]
        me = _index(here)

        def small(kind, src, dst, k):
            return pltpu.make_async_remote_copy(src_ref=src, dst_ref=dst, send_sem=s_send.at[kind, k - 1],
                                                recv_sem=s_recv.at[kind, k - 1], device_id=_peer(k), device_id_type=MESH)

        gather = lambda k: small(0, c_ref, call_ref.at[me], k)
        scatter = lambda k: small(1, stage_ref.at[_index(_peer(k))], mod_ref.at[me], k)
        gather_taps = lambda k: small(2, t_ref, tall_ref.at[me], k)

        def rows(w, pos):
            r = shards[w].shape[0]
            return outs[w].at[pl.ds(pl.multiple_of(_index(pos) * r, 16), r), :]

        def block(k, w, pos, to, own=False):
            return pltpu.make_async_remote_copy(
                src_ref=srcs[w] if own else rows(w, pos), dst_ref=rows(w, pos),
                send_sem=w_send.at[k, w], recv_sem=w_recv.at[k, w], device_id=to, device_id_type=MESH)

        call_ref[me] = c_ref[...]
        tall_ref[me] = t_ref[...]
        for k in range(1, N_DEV):
            gather(k).start()
        for k in range(1, N_DEV):
            gather_taps(k).start()
        mine = [pltpu.make_async_copy(srcs[w], rows(w, here), local_sems.at[w]) for w in range(n_w)]
        for cp in mine:
            cp.start()
        first = [block(0, w, here, sibling, own=True) for w in range(n_w)]
        first += [block(1 + j, w, here, (*chip, c), own=True) for j, chip in enumerate(chips) for w in range(n_w)]
        for cp in first:
            cp.start()
        fetch = [pltpu.make_async_copy(later_refs[w], wide[w], place_sems.at[0, w]) for w in range(n_p)]
        for cp in fetch:
            cp.start()

        for k in range(1, N_DEV):
            gather(k).wait_recv()
        cv = jnp.concatenate([call_ref[b, 0:1, :] for b in range(N_DEV)], axis=0)
        act = cv * jax.nn.sigmoid(cv)
        mod = lax.dot_general(act, w_ref[...], NN, preferred_element_type=F32,
                              precision=lax.Precision.HIGHEST) + b_ref[:, pl.ds(pl.multiple_of(me * ncol, LANES), ncol)]
        for b in range(N_DEV):
            stage_ref[b] = jnp.broadcast_to(mod[b:b + 1, :], (8, ncol))
        mod_ref[me] = stage_ref[me]
        for k in range(1, N_DEV):
            scatter(k).start()

        placed = []
        for w in range(n_p):
            fetch[w].wait()
            narrow[w][...] = wide[w][...].astype(BF16)
            r = later[w].shape[0]
            placed.append(pltpu.make_async_copy(narrow[w], zones[w].at[pl.ds(pl.multiple_of(me * r, 16), r), :],
                                                place_sems.at[1, w]))
            placed[-1].start()

        passed = []
        for j, chip in enumerate(chips):
            for w in range(n_w):
                block(1 + j, w, (*chip, c), here).wait_recv()
                fwd = block(4 + j, w, (*chip, c), sibling)
                fwd.start()
                passed.append(fwd)
        for w in range(n_w):
            block(0, w, sibling, here).wait_recv()
        for j, chip in enumerate(chips):
            for w in range(n_w):
                block(4 + j, w, (*chip, 1 - c), here).wait_recv()
        for k in range(1, N_DEV):
            scatter(k).wait_recv()
            gather_taps(k).wait_recv()
        for cp in first + passed:
            cp.wait_send()
        for k in range(1, N_DEV):
            gather(k).wait_send()
            scatter(k).wait_send()
            gather_taps(k).wait_send()
        for cp in mine + placed:
            cp.wait()

    vmem, hbm = pl.BlockSpec(memory_space=pltpu.VMEM), pl.BlockSpec(memory_space=pltpu.HBM)
    out = pl.pallas_call(
        body, name="entry_exchange",
        in_specs=[vmem] * 4 + [hbm] * (n_w + n_p), out_specs=[vmem] * 3 + [hbm] * (n_w + n_p),
        out_shape=[jax.ShapeDtypeStruct((N_DEV, 8, d), F32), jax.ShapeDtypeStruct((N_DEV, 8, ncol), F32),
                   jax.ShapeDtypeStruct((N_DEV,) + taps.shape, F32)]
        + [jax.ShapeDtypeStruct((N_DEV * s.shape[0], s.shape[1]), s.dtype) for s in shards]
        + [jax.ShapeDtypeStruct((N_DEV * s.shape[0], s.shape[1]), BF16) for s in later],
        scratch_shapes=[pltpu.VMEM((N_DEV, 8, ncol), F32), pltpu.SemaphoreType.DMA((3, N_DEV - 1)),
                        pltpu.SemaphoreType.DMA((3, N_DEV - 1)), pltpu.SemaphoreType.DMA((N_DEV - 1, n_w)),
                        pltpu.SemaphoreType.DMA((N_DEV - 1, n_w)), pltpu.SemaphoreType.DMA((n_w,))]
        + [pltpu.VMEM(s.shape, F32) for s in later] + [pltpu.VMEM(s.shape, BF16) for s in later]
        + [pltpu.SemaphoreType.DMA((2, n_p))],
        compiler_params=_params(),
    )(c_rows, w_ada, b_ada, taps, *shards, *later)
    return out[0], out[1], out[2], out[3:3 + n_w], out[3 + n_w:]


def _peer_copies(mode, srcs, lands, send_sems, recv_sems):
    if mode in ("gather_ici", "gather_d2d"):
        x, y, c = _place()
        sibling = (x, y, 1 - c)
        chips = [(1 - x, y), (x, 1 - y), (1 - x, 1 - y)]
        n = len(lands)

        def rows(w, pos):
            r = lands[w].shape[0] // N_DEV
            return lands[w].at[pl.ds(pl.multiple_of(_index(pos) * r, 16), r), :]

        def copy(k, w, src, dst, to):
            return pltpu.make_async_remote_copy(src_ref=src, dst_ref=dst, send_sem=send_sems.at[k * n + w],
                                                recv_sem=recv_sems.at[k * n + w], device_id=to, device_id_type=MESH)

        if mode == "gather_ici":
            targets = [sibling] + [(*chip, c) for chip in chips]
            return [copy(k, w, rows(w, (x, y, c)), rows(w, (x, y, c)), to) for k, to in enumerate(targets) for w in range(n)]
        return [copy(j, w, rows(w, (*chip, c)), rows(w, (*chip, c)), sibling)
                for j, chip in enumerate(chips) for w in range(n)]
    me = _index(_place())
    modes = (mode,) * len(srcs) if isinstance(mode, str) else mode
    copies = []
    for k in range(1, N_DEV):
        peer = _peer(k)
        for w, (src, land) in enumerate(zip(srcs, lands)):
            if modes[w] == "gather":
                r = src.shape[0]
                dst = land.at[pl.ds(pl.multiple_of(me * r, 16), r), :]
            elif modes[w] == "allgather":
                dst = land.at[me]
            else:
                r = src.shape[0] // N_DEV
                src = src.at[pl.ds(pl.multiple_of(_index(peer) * r, 16), r), :]
                dst = land.at[me]
            copies.append(pltpu.make_async_remote_copy(
                src_ref=src, dst_ref=dst, send_sem=send_sems.at[(k - 1) * len(srcs) + w],
                recv_sem=recv_sems.at[(k - 1) * len(srcs) + w],
                device_id=peer, device_id_type=MESH))
    return copies


def _exchange_start(mode, srcs, lands, name):
    n_s, n_a = len(srcs), len(srcs) + len(lands)
    n_cp = _COPIES_PER_ARRAY.get(mode, N_DEV - 1) * len(lands)

    def body(*refs):
        for cp in _peer_copies(mode, refs[:n_s], refs[n_s:n_a], refs[n_a], refs[n_a + 1]):
            cp.start()

    hbm, sem = pl.BlockSpec(memory_space=pltpu.HBM), pl.BlockSpec(memory_space=pltpu.SEMAPHORE)
    arrays = list(srcs) + list(lands)
    out = pl.pallas_call(
        body, name=name,
        out_shape=(pltpu.SemaphoreType.DMA((n_cp,)), pltpu.SemaphoreType.DMA((n_cp,)),
                   *[pltpu.HBM(a.shape, a.dtype) for a in arrays]),
        in_specs=[hbm] * n_a, out_specs=(sem, sem, *[hbm] * n_a),
        input_output_aliases={i: 2 + i for i in range(n_a)},
        compiler_params=pltpu.CompilerParams(has_side_effects=pltpu.SideEffectType.DATAFLOW_SIDE_EFFECTING),
    )(*[pltpu.with_memory_space_constraint(a, pltpu.HBM) for a in arrays])
    return out[0], out[1], out[2:2 + n_s], out[2 + n_s:2 + n_a], out[2]


_COPIES_PER_ARRAY = {"gather_ici": 4, "gather_d2d": 3}


def _exchange_wait(mode, send_sems, recv_sems, srcs, lands, after, name):
    n_s, n_a = len(srcs), len(srcs) + len(lands)

    def body(*refs):
        copies = _peer_copies(mode, refs[:n_s], refs[n_s:n_a], refs[n_a], refs[n_a + 1])
        for cp in copies:
            cp.wait_send()
        for cp in copies:
            cp.wait_recv()

    hbm, sem = pl.BlockSpec(memory_space=pltpu.HBM), pl.BlockSpec(memory_space=pltpu.SEMAPHORE)
    arrays = list(srcs) + list(lands)
    out = pl.pallas_call(
        body, name=name, out_shape=tuple(pltpu.HBM(a.shape, a.dtype) for a in arrays),
        in_specs=[hbm] * n_a + [sem, sem, pl.BlockSpec(memory_space=pl.ANY)], out_specs=tuple([hbm] * n_a),
        input_output_aliases={i: i for i in range(n_a)},
        compiler_params=pltpu.CompilerParams(has_side_effects=pltpu.SideEffectType.DATAFLOW_SIDE_EFFECTING),
    )(*arrays, send_sems, recv_sems, after)
    return out[:n_s], out[n_s:]


SMALL_WEIGHTS = ("b_ada", "g_pre_mix", "g_post_mix", "g_pre_ffn", "g_post_ffn", "w_pool", "b_pool", "pool_scale", "conv_b")


MOD_ROWS = ((0, 0), (0, 1), (1, 3), (1, 0), (1, 1), (2, 0))


def _small_sum_adam(mine, gathered, weights, moms, vels):
    n_l, n_w = len(mine), len(weights)
    d = mine[0].shape[1]

    def body(*refs):
        loc, got = refs[:n_l], refs[n_l:2 * n_l]
        w_refs, m_refs, v_refs = (refs[2 * n_l + k * n_w:2 * n_l + (k + 1) * n_w] for k in range(3))
        outs = refs[2 * n_l + 3 * n_w:]
        dmod_ref, conv_ref, loss_ref = outs[4 * n_w:]
        me = _index(_place())
        part = lambda a, dev: jnp.where(dev == me, loc[a][...], got[a][dev])
        totals = []
        for a in range(n_l):
            tot = part(a, 0)
            for dev in range(1, N_DEV):
                tot = tot + part(a, dev)
            totals.append(tot)
        t_in, t_mix, t_ffn, t_pool, t_blk, t_conv, t_loss = totals
        conv_ref[...] = t_conv
        loss_ref[...] = t_loss
        for dev in range(N_DEV):
            for k, (a, r) in enumerate(MOD_ROWS):
                dmod_ref[dev:dev + 1, k * d:(k + 1) * d] = part(a, dev)[r:r + 1, :]

        def update(idx, g, at=()):
            sel = lambda ref: ref.at[at] if at else ref
            delta, nm, nv = _adam_math(sel(w_refs[idx])[...], g, sel(m_refs[idx])[...], sel(v_refs[idx])[...])
            for k, val in enumerate((g, delta, nm, nv)):
                sel(outs[4 * idx + k])[...] = val

        tots = (t_in, t_mix, t_ffn)
        update(0, jnp.concatenate([tots[a][r:r + 1] for a, r in MOD_ROWS], axis=1))
        update(1, t_in[2:3])
        update(2, t_mix[4:5])
        update(3, t_mix[2:3])
        update(4, t_ffn[1:2])
        for gi in range(len(POOL_WINDOWS)):
            update(5, t_blk[gi], at=(0, gi))
        update(6, jnp.concatenate([t_pool[0:1, gi * HEAD_DIM:(gi + 1) * HEAD_DIM] for gi in range(len(POOL_WINDOWS))], axis=0),
               at=(0,))
        update(7, t_pool[1:2])
        update(8, t_conv[3:4])

    vmem = pl.BlockSpec(memory_space=pltpu.VMEM)
    out = pl.pallas_call(
        body, name="small_sum_adam", in_specs=[vmem] * (2 * n_l + 3 * n_w), out_specs=[vmem] * (4 * n_w + 3),
        out_shape=[jax.ShapeDtypeStruct(w.shape, F32) for w in weights for _ in range(4)]
        + [jax.ShapeDtypeStruct((N_DEV, 6 * d), F32), jax.ShapeDtypeStruct(mine[5].shape, F32),
           jax.ShapeDtypeStruct(mine[6].shape, F32)],
        compiler_params=_params(),
    )(*mine, *gathered, *weights, *moms, *vels)
    return out[:4 * n_w], out[4 * n_w], out[4 * n_w + 1], out[4 * n_w + 2]


def _adam_math(w, g, m, v):
    m = ADAM_B1 * m + (1.0 - ADAM_B1) * g
    v = ADAM_B2 * v + (1.0 - ADAM_B2) * (g * g)
    m_hat = m / (1.0 - ADAM_B1 ** ADAM_STEP)
    v_hat = v / (1.0 - ADAM_B2 ** ADAM_STEP)
    delta = -ADAM_LR * (m_hat / (jnp.sqrt(v_hat) + ADAM_EPS) + ADAM_WD * w)
    return delta, m, v


def _adam(w, g, m, v, name):
    def body(w_ref, g_ref, m_ref, v_ref, d_ref, nm_ref, nv_ref):
        d_ref[...], nm_ref[...], nv_ref[...] = _adam_math(w_ref[...], g_ref[...], m_ref[...], v_ref[...])

    vmem = pl.BlockSpec(memory_space=pltpu.VMEM)
    return pl.pallas_call(
        body, name=name, in_specs=[vmem] * 4, out_specs=[vmem] * 3,
        out_shape=[jax.ShapeDtypeStruct(w.shape, F32)] * 3, compiler_params=_params(),
    )(w, g, m, v)


def _sum_adam(own, parts, w, m, v, me, name, tr):
    _, rows, cols = parts.shape
    turned = w.shape == (cols, rows) and rows != cols
    assert tr == rows or not turned
    n_t = rows // tr

    def body(me_ref, own_ref, p_ref, w_ref, m_ref, v_ref, g_ref, d_ref, nm_ref, nv_ref):
        part = lambda dev: jnp.where(dev == me_ref[0], own_ref[...], p_ref[dev]).astype(F32)
        g = part(0)
        for dev in range(1, N_DEV):
            g = g + part(dev)
        g = g.T if turned else g
        g_ref[...] = g
        d_ref[...], nm_ref[...], nv_ref[...] = _adam_math(w_ref[...], g, m_ref[...], v_ref[...])

    spec = pl.BlockSpec((cols, rows) if turned else (tr, cols), lambda i, me_ref: (i, 0))
    shape = jax.ShapeDtypeStruct(w.shape, F32)
    return pl.pallas_call(
        body, name=name, out_shape=[shape] * 4,
        grid_spec=pltpu.PrefetchScalarGridSpec(
            num_scalar_prefetch=1, grid=(n_t,),
            in_specs=[pl.BlockSpec((tr, cols), lambda i, me_ref: (me_ref[0] * n_t + i, 0)),
                      pl.BlockSpec((N_DEV, tr, cols), lambda i, me_ref: (0, i, 0)), spec, spec, spec],
            out_specs=[spec] * 4),
        compiler_params=_params(("arbitrary",)),
    )(me.reshape(1).astype(jnp.int32), own, parts, w, m, v)


def _ada_grad_adam(c_all, dmod_all, w, m, v, tr):
    rows, cols = w.shape

    def body(c_ref, dm_ref, w_ref, m_ref, v_ref, g_ref, d_ref, nm_ref, nv_ref):
        cv = c_ref[...]
        act = cv * jax.nn.sigmoid(cv)
        dmod = dm_ref[:, pl.ds(pl.multiple_of(_index(_place()) * cols, LANES), cols)]
        g = lax.dot_general(act, dmod, TN, preferred_element_type=F32, precision=lax.Precision.HIGHEST)
        g_ref[...] = g
        d_ref[...], nm_ref[...], nv_ref[...] = _adam_math(w_ref[...], g, m_ref[...], v_ref[...])

    spec = pl.BlockSpec((tr, cols), lambda i: (i, 0))
    shape = jax.ShapeDtypeStruct((rows, cols), F32)
    return pl.pallas_call(
        body, name="ada_grad_adam", grid=(rows // tr,),
        in_specs=[pl.BlockSpec((N_DEV, tr), lambda i: (0, i)), pl.BlockSpec(dmod_all.shape, lambda i: (0, 0)), spec, spec, spec],
        out_specs=[spec] * 4, out_shape=[shape] * 4, compiler_params=_params(("arbitrary",)),
    )(c_all, dmod_all, w, m, v)


def _rope_tables(positions):
    inv_freq = ROPE_THETA ** (-jnp.arange(0, 2 * ROT_HALF, 2, dtype=F32) / (2 * ROT_HALF))
    ang = inv_freq[:, None] * positions.astype(F32)[None, :]
    rows = jnp.concatenate([jnp.cos(ang), jnp.sin(ang), jnp.ones_like(ang)], axis=0)
    spread = [[[0.0] * LANES for _ in range(3 * ROT_HALF)] for _ in range(3)]
    for lane in range(LANES):
        p, j = lane % HEAD_DIM, lane % ROT_HALF
        if p < ROT_HALF:
            spread[0][j][lane] = 1.0
            spread[1][ROT_HALF + j][lane] = -1.0
        elif p < 2 * ROT_HALF:
            spread[0][j][lane] = 1.0
            spread[2][ROT_HALF + j][lane] = 1.0
        else:
            spread[0][2 * ROT_HALF][lane] = 1.0
    return rows, jnp.array(spread, F32)


def _pad_rows(a, rows):
    return jnp.pad(a, ((0, rows - a.shape[0]), (0, 0)))


def _sequence_step(xs, target, rope, mods, gains, w_in_t, w_out_t, relay_ffn, fetch_ffn, send_grads, w_blk_b, b_pool_r,
                   pool_scale_r, conv_w_all, conv_b, after):
    sh_m, sc_m, gt_m, sh_f, sc_f, gt_f = mods
    g_pre_mix, g_post_mix, g_pre_ffn, g_post_ffn = gains
    h1, u_pool, qkv = _premix_inproj(xs, sh_m, sc_m, g_pre_mix, w_in_t, rope, after, tm=512)
    o_g, lse_g = _attn_fwd(qkv)
    x1, y1, h2, cat, attn, lse_all = _mix_out(xs, u_pool, o_g, lse_g, w_blk_b, b_pool_r, pool_scale_r, w_out_t,
                                              gt_m, g_post_mix, g_pre_ffn, sc_f, sh_f, tm=256)
    relay_ffn(x1)
    w_up_t, w_down_f = fetch_ffn(x1)
    gate, a_ffn, act, vd, dy2, dout, sums_ffn, loss_loc = _ffn_fwd_loss(h2, x1, target, w_up_t, w_down_f, conv_w_all, conv_b,
                                                              gt_f, g_post_ffn, tm=256, ck=256)

    dgc, dval, dw_down, dconv = _ffn_bwd_act(dy2, gate, a_ffn, act, vd, w_down_f, tm=512, tf=1408, ck=256)
    dup, dh2 = _ffn_bwd_up(dgc, dval, w_up_t, conv_w_all, tm=256)
    dw_up_t = _wgrad(dup, h2, "wgrad_up", tk=2048, tmm=1408)
    token = send_grads("ffn", [dw_up_t, dw_down], [])
    dx1, dpool, dattn, delta, dw_out_t, sums_mix = _mix_bwd(dh2, dout, x1, y1, cat, attn, w_out_t, sc_f,
                                                           g_pre_ffn, gt_m, g_post_mix, token, tm=256)
    du, dw_blk, sums_pool = _pool_bwd(dpool, u_pool, w_blk_b, b_pool_r, pool_scale_r, tm=512)
    token = send_grads("out", [dw_out_t], [sums_mix, sums_ffn, sums_pool, dw_blk, dconv, loss_loc])
    dproj, dw_in_t = _dproj_wgrad_in(du, _attn_bwd(qkv, dattn, lse_all, delta, token), rope, h1, tm=512, cm=512)
    token = send_grads("in", [dw_in_t], [])
    grad_x, sums_in = _inproj_bwd(dproj, w_in_t, xs, dx1, sc_m, g_pre_mix, token, tm=256)
    return (loss_loc, grad_x, dw_in_t, dw_out_t, dw_up_t, dw_down, dw_blk, dconv,
            sums_in, sums_mix, sums_ffn, sums_pool)


def kernel(x, c, positions, w_ada, b_ada, g_pre_mix, g_post_mix, g_pre_ffn, g_post_ffn, w_in, w_pool, b_pool, pool_scale, w_out, w_up, conv_w, conv_b, w_down, loss_target, m_w_ada, m_b_ada, m_g_pre_mix, m_g_post_mix, m_g_pre_ffn, m_g_post_ffn, m_w_in, m_w_pool, m_b_pool, m_pool_scale, m_w_out, m_w_up, m_conv_w, m_conv_b, m_w_down, v_w_ada, v_b_ada, v_g_pre_mix, v_g_post_mix, v_g_pre_ffn, v_g_post_ffn, v_w_in, v_w_pool, v_b_pool, v_pool_scale, v_w_out, v_w_up, v_conv_w, v_conv_b, v_w_down):
    s_len, d = x.shape[1], x.shape[2]
    d_ff = w_down.shape[1] * N_DEV
    me = _index(_place())
    xs, target = x[0], loss_target[0]

    c_all, mod, taps_all, (w_in_t, w_out_t), lands = _entry_exchange(
        jnp.broadcast_to(c, (8, d)), w_ada[0], b_ada, _pad_rows(conv_w[0], 8),
        [w_in[0].T.astype(BF16), w_out[0].T.astype(BF16)], [w_up[0].T, w_down[0]])
    c_all = c_all[:, 0, :]
    conv_w_all = jnp.transpose(taps_all[:, :3, :], (1, 0, 2)).reshape(3, d_ff)
    sh_m, sc_m, gt_m, sh_f, sc_f, gt_f = [mod[:, 0, :].reshape(1, -1)[:, k * d:(k + 1) * d] for k in range(6)]

    rope = _rope_tables(positions[0])
    w_blk = jnp.zeros((256, 256), F32)
    for gi in range(4):
        w_blk = lax.dynamic_update_slice(w_blk, w_pool[0, gi], (gi * HEAD_DIM, gi * HEAD_DIM))
    w_blk_b = w_blk.astype(BF16)
    b_pool_r, pool_scale_r = b_pool.reshape(1, 256), pool_scale.reshape(1, 256)

    w_send, w_recv, w_src, w_land, w_token = _exchange_start("gather_ici", [], lands, "ffn_weights_ici_start")
    relay = []

    def relay_ffn(after):
        _, blocks = _exchange_wait("gather_ici", w_send, w_recv, w_src, w_land, after, "ffn_weights_ici_wait")
        relay.extend(_exchange_start("gather_d2d", [], blocks, "ffn_weights_d2d_start"))

    def fetch_ffn(after):
        return _exchange_wait("gather_d2d", relay[0], relay[1], [], relay[3], after, "ffn_weights_d2d_wait")[1]

    flights = {}

    def send_grads(tag, slabs, whole):
        lands = [lax.empty((N_DEV, g.shape[0] // N_DEV, g.shape[1]), g.dtype) for g in slabs]
        lands += [lax.empty((N_DEV,) + a.shape, F32) for a in whole]
        modes = ("scatter",) * len(slabs) + ("allgather",) * len(whole)
        flights[tag] = (modes, *_exchange_start(modes, slabs + whole, lands, f"grads_{tag}_start"))
        return flights[tag][5]

    def arrived(tag, after):
        return _exchange_wait(*flights[tag][:5], after, f"grads_{tag}_wait")

    _, grad_x, *_, sums_in, _, _, _ = _sequence_step(
        xs, target, rope, (sh_m, sc_m, gt_m, sh_f, sc_f, gt_f), (g_pre_mix, g_post_mix, g_pre_ffn, g_post_ffn),
        w_in_t, w_out_t, relay_ffn, fetch_ffn, send_grads, w_blk_b, b_pool_r, pool_scale_r, conv_w_all, conv_b,
        w_token)

    send_grads("last", [], [sums_in])

    (own_up, own_down), (parts_up, parts_down) = arrived("ffn", flights["last"][5])
    new_up = _sum_adam(own_up, parts_up, w_up[0].T, m_w_up[0].T, v_w_up[0].T, me, "adam_w_up", 352)
    new_down = _sum_adam(own_down, parts_down, w_down[0], m_w_down[0], v_w_down[0], me, "adam_w_down", 176)
    (own_out, *small), (parts_out, *gathered) = arrived("out", new_down[0])
    new_out = _sum_adam(own_out, parts_out, w_out[0], m_w_out[0], v_w_out[0], me, "adam_w_out", 128)
    (own_in,), (parts_in,) = arrived("in", new_out[0])
    new_in = _sum_adam(own_in, parts_in, w_in[0].T, m_w_in[0].T, v_w_in[0].T, me, "adam_w_in", 160)
    big = {"w_up": [a.T for a in new_up], "w_down": new_down, "w_out": new_out, "w_in": [a.T for a in new_in]}

    rep_w = [b_ada, g_pre_mix, g_post_mix, g_pre_ffn, g_post_ffn, w_pool, b_pool, pool_scale, conv_b]
    rep_m = [m_b_ada, m_g_pre_mix, m_g_post_mix, m_g_pre_ffn, m_g_post_ffn, m_w_pool, m_b_pool, m_pool_scale, m_conv_b]
    rep_v = [v_b_ada, v_g_pre_mix, v_g_post_mix, v_g_pre_ffn, v_g_post_ffn, v_w_pool, v_b_pool, v_pool_scale, v_conv_b]
    mine_last, got_last = arrived("last", new_in[0])
    small, gathered = [*mine_last, *small], [*got_last, *gathered]
    rep_out, dmod_all, dconv_tot, loss_tot = _small_sum_adam(small, gathered, rep_w, rep_m, rep_v)
    g_rep, d_rep, nm_rep, nv_rep = (rep_out[k::4] for k in range(4))

    fcol = d_ff // N_DEV
    taps = lambda a: jnp.transpose(a, (1, 0, 2))
    g_cw = lax.dynamic_slice(dconv_tot, (0, me * fcol), (3, fcol))[None]
    d_cw, nm_cw, nv_cw = [taps(a) for a in _adam(taps(conv_w), taps(g_cw), taps(m_conv_w), taps(v_conv_w), "adam_conv_w")]

    g_ada, d_ada, nm_ada, nv_ada = _ada_grad_adam(c_all, dmod_all, w_ada[0], m_w_ada[0], v_w_ada[0], 256)

    loss = loss_tot[0, 0]

    def group(k):
        rep = (g_rep, d_rep, nm_rep, nv_rep)[k]
        ada = (g_ada, d_ada, nm_ada, nv_ada)[k][None]
        cw = (g_cw, d_cw, nm_cw, nv_cw)[k]
        return [ada, rep[0], rep[1], rep[2], rep[3], rep[4], big["w_in"][k][None], rep[5], rep[6], rep[7],
                big["w_out"][k][None], big["w_up"][k][None], cw, rep[8], big["w_down"][k][None]]

    return (loss, grad_x[None], *group(0), *group(1), *group(2), *group(3))
```

```python
import functools
import math

import jax
import jax.numpy as jnp
from jax import lax
from jax.experimental import pallas as pl
from jax.experimental.pallas import tpu as pltpu

F32 = jnp.float32
BF16 = jnp.bfloat16
MESH = pl.DeviceIdType.MESH

N_DEV = 8
HEAD_DIM = 64
ROT_HALF = 8
ROPE_THETA = 500000.0
POOL_WINDOWS = (2, 4, 8, 16)
DILATIONS = (1, 4, 16)
BLOCK = 128
NORM_EPS = 1e-6
HALO = 16
MASKED = -1e30
ATTN_FWD_UNROLL = 8
ATTN_BWD_UNROLL = 8

ADAM_LR = 0.001
ADAM_B1 = 0.9
ADAM_B2 = 0.999
ADAM_EPS = 1e-08
ADAM_WD = 0.01
ADAM_STEP = 10

V7X_VMEM_LIMIT = 56 * 1024 * 1024
LANES = 128

NT = (((1,), (1,)), ((), ()))
NN = (((1,), (0,)), ((), ()))
TN = (((0,), (0,)), ((), ()))


def _dot(a, b, dims):
    return lax.dot_general(a, b, dims, preferred_element_type=F32)


def _params(sem=None, vmem=V7X_VMEM_LIMIT):
    if sem is None:
        return pltpu.CompilerParams(vmem_limit_bytes=vmem)
    return pltpu.CompilerParams(dimension_semantics=sem, vmem_limit_bytes=vmem)


def _rstd(v):
    return lax.rsqrt(jnp.mean(v * v, axis=-1, keepdims=True) + NORM_EPS)


def _norm_bwd(dn, n, rstd):
    return rstd * (dn - n * jnp.mean(dn * n, axis=-1, keepdims=True))


def _rope_lanes(cs_ref, spread_ref):
    return [lax.dot_general(cs_ref[...], spread_ref[k], TN, preferred_element_type=F32, precision=lax.Precision.HIGHEST)
            for k in range(3)]


def _rope_fwd(p, lanes):
    return p * lanes[0] + pltpu.roll(p, LANES - ROT_HALF, 1) * lanes[1] + pltpu.roll(p, ROT_HALF, 1) * lanes[2]


def _rope_bwd(dp, lanes):
    return dp * lanes[0] + pltpu.roll(dp * lanes[1], ROT_HALF, 1) + pltpu.roll(dp * lanes[2], LANES - ROT_HALF, 1)


def _gelu_parts(v):
    k2 = 2.0 * math.sqrt(2.0 / math.pi)
    c = 0.044715
    v2 = v * v
    s = jax.nn.sigmoid(v * (k2 + (k2 * c) * v2))
    g = v * s
    dg = s + g * (1.0 - s) * (k2 + (3.0 * k2 * c) * v2)
    return g, dg


def _halo_before(i, tile):
    return jnp.maximum(i * (tile // HALO) - 1, 0)


def _premix_inproj(x, sh, sc, g, w_in_t, rope, after, tm):
    s_len, d = x.shape
    n_proj = w_in_t.shape[0]
    n_slab = (n_proj - 256) // LANES

    def body(x_ref, sh_ref, sc_ref, g_ref, w_ref, cs_ref, spread_ref, after_ref, h_ref, up_ref, qkv_ref):
        xv = x_ref[...]
        h = (xv * _rstd(xv) * g_ref[...]) * (1.0 + sc_ref[...]) + sh_ref[...]
        hb = h.astype(BF16)
        h_ref[...] = hb
        up_ref[...] = _dot(hb, w_ref[0:256, :], NT)
        lanes = _rope_lanes(cs_ref, spread_ref)
        for pair in range(n_slab // 2):
            p = _dot(hb, w_ref[256 + 256 * pair:512 + 256 * pair, :], NT)
            for half in range(2):
                ph = p[:, half * LANES:(half + 1) * LANES]
                if pair < 6:
                    ph = _rope_fwd(ph, lanes)
                if pair < 3:
                    ph = ph * (HEAD_DIM ** -0.5)
                qkv_ref[2 * pair + half] = ph

    vec = pl.BlockSpec((1, d), lambda i: (0, 0))
    return pl.pallas_call(
        body, name="premix_inproj", grid=(s_len // tm,),
        in_specs=[pl.BlockSpec((tm, d), lambda i: (i, 0)), vec, vec, vec,
                  pl.BlockSpec((n_proj, d), lambda i: (0, 0)),
                  pl.BlockSpec((rope[0].shape[0], tm), lambda i: (0, i)), pl.BlockSpec(rope[1].shape, lambda i: (0, 0, 0)),
                  pl.BlockSpec(memory_space=pl.ANY)],
        out_specs=[pl.BlockSpec((tm, d), lambda i: (i, 0)),
                   pl.BlockSpec((tm, 256), lambda i: (i, 0)),
                   pl.BlockSpec((n_slab, tm, LANES), lambda i: (0, i, 0))],
        out_shape=[jax.ShapeDtypeStruct((s_len, d), BF16),
                   jax.ShapeDtypeStruct((s_len, 256), F32),
                   jax.ShapeDtypeStruct((n_slab, s_len, LANES), F32)],
        compiler_params=_params(("arbitrary",)),
    )(x, sh, sc, g, w_in_t, *rope, after)


def _block_rows(n, r, dil):
    start = n * (BLOCK * dil) + r
    if dil == 1:
        return pl.ds(pl.multiple_of(start, BLOCK), BLOCK)
    return pl.ds(start, BLOCK, stride=dil)


def _band_mask(n):
    ri = lax.broadcasted_iota(jnp.int32, (BLOCK, 2 * BLOCK), 0)
    cj = lax.broadcasted_iota(jnp.int32, (BLOCK, 2 * BLOCK), 1)
    cur = (cj >= BLOCK) & (cj - BLOCK <= ri)
    prev = (cj < BLOCK) & (cj >= ri) & (n > 0)
    return cur | prev


def _attn_fwd(qkv):
    s_len = qkv.shape[1]
    n_g = len(DILATIONS)

    def body(q_ref, k_ref, v_ref, o_ref, lse_ref):
        lane = lax.broadcasted_iota(jnp.int32, (BLOCK, LANES), 1)
        first = lane < HEAD_DIM

        def group(dil):
            nb = s_len // (BLOCK * dil)

            def block(t, carry):
                r, n = t // nb, t % nb
                cur = _block_rows(n, r, dil)
                prev = _block_rows(jnp.maximum(n - 1, 0), r, dil)
                q = q_ref[0, cur, :]
                kcat = jnp.concatenate([k_ref[0, prev, :], k_ref[0, cur, :]], axis=0).astype(BF16)
                vcat = jnp.concatenate([v_ref[0, prev, :], v_ref[0, cur, :]], axis=0).astype(BF16)
                valid = _band_mask(n)
                q2 = jnp.concatenate([jnp.where(first, q, 0.0), jnp.where(first, 0.0, q)], axis=0).astype(BF16)
                s = jnp.where(jnp.concatenate([valid, valid], axis=0), _dot(q2, kcat, NT), MASKED)
                m = jnp.max(s, axis=-1, keepdims=True)
                p = jnp.exp(s - m)
                den = jnp.sum(p, axis=-1, keepdims=True)
                o2 = _dot(p.astype(BF16), vcat, NN) / den
                lse2 = m + jnp.log(den)
                o_ref[0, 0, cur, :] = jnp.where(first, o2[:BLOCK], o2[BLOCK:])
                lse_ref[0, 0, cur, :] = jnp.where(first, lse2[:BLOCK], lse2[BLOCK:])
                return carry

            lax.fori_loop(0, nb * dil, block, 0, unroll=ATTN_FWD_UNROLL)

        for gi, dil in enumerate(DILATIONS):
            pl.when(pl.program_id(0) == gi)(functools.partial(group, dil))

    def slab(base):
        return pl.BlockSpec((1, s_len, LANES), lambda g, s: (base + 2 * g + s, 0, 0))

    out = pl.BlockSpec((1, 1, s_len, LANES), lambda g, s: (g, s, 0, 0))
    shape = jax.ShapeDtypeStruct((n_g, 2, s_len, LANES), F32)
    return pl.pallas_call(
        body, name="attn_fwd", grid=(n_g, 2),
        in_specs=[slab(0), slab(6), slab(12)], out_specs=[out, out], out_shape=[shape, shape],
        compiler_params=_params(("arbitrary", "arbitrary")),
    )(qkv, qkv, qkv)


def _pool_mixed(u, halo, i, tm):
    ue = jnp.concatenate([halo, u], axis=0)
    s2 = ue + pltpu.roll(ue, 1, 0)
    s4 = s2 + pltpu.roll(s2, 2, 0)
    s8 = s4 + pltpu.roll(s4, 4, 0)
    s16 = s8 + pltpu.roll(s8, 8, 0)
    grp = lax.broadcasted_iota(jnp.int32, (tm, 256), 1) // HEAD_DIM
    pick = lambda a, b, c, e: jnp.where(grp == 0, a, jnp.where(grp == 1, b, jnp.where(grp == 2, c, e)))
    win_sum = pick(s2[HALO:], s4[HALO:], s8[HALO:], s16[HALO:])
    pos = (i * tm + lax.broadcasted_iota(jnp.int32, (tm, 256), 0)).astype(F32)
    count = jnp.minimum(pos + 1.0, pick(*[float(w) for w in POOL_WINDOWS]))
    return win_sum / count - u, count


def _mix_out(x, u_pool, o_g, lse_g, w_blk, b_pool, pool_scale, w_out_t, gt_m, g_post_mix, g_pre_ffn, sc_f, sh_f, tm):
    s_len, d = x.shape

    def body(x_ref, u_ref, uh_ref, o_ref, l_ref, wb_ref, bp_ref, ps_ref, wo_ref,
             gt_ref, g1_ref, g2_ref, sc_ref, sh_ref,
             x1_ref, y1_ref, h2_ref, cat_ref, attn_ref, lall_ref):
        (o0, o1, o2), (l0, l1, l2) = (o_ref.at[g] for g in range(3)), (l_ref.at[g] for g in range(3))
        i = pl.program_id(0)
        u = u_ref[...]
        halo = uh_ref[...] * (i > 0).astype(F32)
        mixed, _ = _pool_mixed(u, halo, i, tm)
        y = _dot(mixed.astype(BF16), wb_ref[...], NN) + bp_ref[...]
        pool = y * ps_ref[...]
        attn = []
        for s in range(2):
            la, lb, lc = l0[s], l1[s], l2[s]
            mx = jnp.maximum(jnp.maximum(la, lb), lc)
            ea, eb, ec = jnp.exp(la - mx), jnp.exp(lb - mx), jnp.exp(lc - mx)
            den = ea + eb + ec
            lall_ref[s] = mx + jnp.log(den)
            attn.append((ea / den) * o0[s] + (eb / den) * o1[s] + (ec / den) * o2[s])
        attn = jnp.concatenate(attn, axis=1)
        attn_ref[...] = attn
        cat = jnp.concatenate([pool, attn], axis=1).astype(BF16)
        cat_ref[...] = cat
        y1 = _dot(cat, wo_ref[...], NT)
        y1_ref[...] = y1.astype(BF16)
        x1 = x_ref[...] + gt_ref[...] * (y1 * _rstd(y1) * g1_ref[...])
        x1_ref[...] = x1
        h2 = (x1 * _rstd(x1) * g2_ref[...]) * (1.0 + sc_ref[...]) + sh_ref[...]
        h2_ref[...] = h2.astype(BF16)

    tile = lambda w: pl.BlockSpec((tm, w), lambda i: (i, 0))
    slab = pl.BlockSpec((2, tm, LANES), lambda i: (0, i, 0))
    groups = pl.BlockSpec((len(DILATIONS), 2, tm, LANES), lambda i: (0, 0, i, 0))
    const = lambda a: pl.BlockSpec(a.shape, lambda i: (0,) * a.ndim)
    return pl.pallas_call(
        body, name="mix_out", grid=(s_len // tm,),
        in_specs=[tile(d), tile(256), pl.BlockSpec((HALO, 256), lambda i: (_halo_before(i, tm), 0)),
                  groups, groups,
                  const(w_blk), const(b_pool), const(pool_scale), const(w_out_t),
                  const(gt_m), const(g_post_mix), const(g_pre_ffn), const(sc_f), const(sh_f)],
        out_specs=[tile(d), tile(d), tile(d), tile(512), tile(256), slab],
        out_shape=[jax.ShapeDtypeStruct((s_len, d), F32), jax.ShapeDtypeStruct((s_len, d), BF16),
                   jax.ShapeDtypeStruct((s_len, d), BF16), jax.ShapeDtypeStruct((s_len, 512), BF16),
                   jax.ShapeDtypeStruct((s_len, 256), F32), jax.ShapeDtypeStruct((2, s_len, LANES), F32)],
        compiler_params=_params(("arbitrary",)),
    )(x, u_pool, u_pool, o_g, lse_g, w_blk, b_pool, pool_scale, w_out_t, gt_m, g_post_mix, g_pre_ffn, sc_f, sh_f)


def _conv_gate(gate_ext, cw, cb):
    gc = gate_ext * cw[2:3, :] + pltpu.roll(gate_ext, 1, 0) * cw[1:2, :] + pltpu.roll(gate_ext, 2, 0) * cw[0:1, :]
    return gc[HALO:] + cb


def _ffn_fwd_loss(h2, x1, target, w_up_t, w_down, conv_w, conv_b, gt_f, g_post_ffn, tm, ck):
    s_len, d = x1.shape
    d_ff = w_down.shape[0]
    n_t, n_c = s_len // tm, d_ff // ck

    def body(h_ref, hh_ref, x1_ref, tgt_ref, wg_ref, wv_ref, wd_ref, cw_ref, cb_ref, gt_ref, g_ref,
             gate_ref, a_ref, act_ref, vd_ref, dy2_ref, dout_ref, sums_ref, loss_ref, acc_ref):
        i = pl.program_id(0)

        @pl.when(i == 0)
        def _():
            sums_ref[...] = jnp.zeros_like(sums_ref)
            loss_ref[...] = jnp.zeros_like(loss_ref)
            acc_ref[...] = jnp.zeros_like(acc_ref)

        def finish(live):
            y2 = acc_ref[...]
            rstd = _rstd(y2)
            n = y2 * rstd
            rn = n * g_ref[...]
            err = x1_ref[...] + gt_ref[...] * rn - tgt_ref[...]
            keep = lambda v: jnp.where(live, v, 0.0)
            loss_ref[...] += keep(0.5 * jnp.sum(jnp.mean(err * err, axis=-1, keepdims=True), axis=0, keepdims=True))
            dout = err * (1.0 / d)
            dout_ref[...] = dout.astype(BF16)
            drn = dout * gt_ref[...]
            sums_ref[0:1, :] += keep(jnp.sum(dout * rn, axis=0, keepdims=True))
            sums_ref[1:2, :] += keep(jnp.sum(drn * n, axis=0, keepdims=True))
            dy2_ref[...] = _norm_bwd(drn * g_ref[...], n, rstd).astype(BF16)

        @pl.when(i < n_t)
        def _():
            h = h_ref[...]
            h_ext = jnp.concatenate([hh_ref[...], h], axis=0)
            row = lax.broadcasted_iota(jnp.int32, (tm + HALO, ck), 0)
            no_halo = (row < HALO) & (i == 0)

            def up(c):
                cs = slice(c * ck, (c + 1) * ck)
                return jnp.where(no_halo, 0.0, _dot(h_ext, wg_ref[cs, :], NT)), _dot(h, wv_ref[cs, :], NT)

            part = None
            nxt = up(0)
            finish(i > 0)
            for c in range(n_c):
                cs = slice(c * ck, (c + 1) * ck)
                gate_ext, val = nxt
                if c + 1 < n_c:
                    nxt = up(c + 1)
                act, dact = _gelu_parts(_conv_gate(gate_ext, cw_ref[:, cs], cb_ref[:, cs]))
                a = (act * val).astype(BF16)
                gate_ref[:, cs] = gate_ext[HALO:].astype(BF16)
                a_ref[:, cs] = a
                act_ref[:, cs] = act.astype(BF16)
                vd_ref[:, cs] = (val * dact).astype(BF16)
                p = _dot(a, wd_ref[cs, :], NN)
                part = p if part is None else part + p
            acc_ref[...] = part

        @pl.when(i == n_t)
        def _():
            finish(True)

    this = lambda i: jnp.minimum(i, n_t - 1)
    before = lambda i: jnp.maximum(i - 1, 0)
    tok = lambda w, at: pl.BlockSpec((tm, w), lambda i: (at(i), 0))
    vec = pl.BlockSpec((1, d), lambda i: (0, 0))
    once = lambda shape, imap: pl.BlockSpec(shape, imap, pipeline_mode=pl.Buffered(1))
    return pl.pallas_call(
        body, name="ffn_fwd_loss", grid=(n_t + 1,),
        in_specs=[tok(d, this), pl.BlockSpec((HALO, d), lambda i: (_halo_before(this(i), tm), 0)),
                  tok(d, before), tok(d, before),
                  once((d_ff, d), lambda i: (0, 0)), once((d_ff, d), lambda i: (1, 0)), once((d_ff, d), lambda i: (0, 0)),
                  pl.BlockSpec((3, d_ff), lambda i: (0, 0)), pl.BlockSpec((1, d_ff), lambda i: (0, 0)), vec, vec],
        out_specs=[tok(d_ff, this)] * 4 + [tok(d, before), tok(d, before), pl.BlockSpec((8, d), lambda i: (0, 0)),
                                          pl.BlockSpec((8, LANES), lambda i: (0, 0))],
        out_shape=[jax.ShapeDtypeStruct((s_len, d_ff), BF16)] * 4
        + [jax.ShapeDtypeStruct((s_len, d), BF16), jax.ShapeDtypeStruct((s_len, d), BF16),
           jax.ShapeDtypeStruct((8, d), F32), jax.ShapeDtypeStruct((8, LANES), F32)],
        scratch_shapes=[pltpu.VMEM((tm, d), F32)],
        compiler_params=_params(("arbitrary",)),
    )(h2, h2, x1, target, w_up_t, w_up_t, w_down, conv_w, conv_b, gt_f, g_post_ffn)


def _ffn_bwd_act(dy2, gate, a, act, vd, w_down, tm, tf, ck):
    s_len, d = dy2.shape
    d_ff = w_down.shape[0]
    n_t = s_len // tm
    chunks = [slice(lo, min(lo + ck, tf)) for lo in range(0, tf, ck)]

    def body(dy_ref, g_ref, gh_ref, a_ref, act_ref, vd_ref, wd_ref, dgc_ref, dval_ref, dwd_ref, dconv_ref, acc_ref):
        i = pl.program_id(1)

        @pl.when(i == 0)
        def _():
            acc_ref[...] = jnp.zeros_like(acc_ref)
            dconv_ref[...] = jnp.zeros_like(dconv_ref)

        dy = dy_ref[...]

        def down(cs):
            return _dot(dy, wd_ref[cs, :], NT)

        nxt = down(chunks[0])
        for c, cs in enumerate(chunks):
            width = cs.stop - cs.start
            da = nxt
            if c + 1 < len(chunks):
                nxt = down(chunks[c + 1])
            acc_ref[cs, :] += _dot(a_ref[:, cs], dy, TN)
            row = lax.broadcasted_iota(jnp.int32, (tm + HALO, width), 0)
            gate_ext = jnp.where((row < HALO) & (i == 0), 0.0,
                                 jnp.concatenate([gh_ref[:, cs], g_ref[:, cs]], axis=0).astype(F32))
            dgc = da * vd_ref[:, cs].astype(F32)
            dgc_ref[:, cs] = dgc.astype(BF16)
            dval_ref[:, cs] = (da * act_ref[:, cs].astype(F32)).astype(BF16)
            rows = [jnp.sum(dgc * pltpu.roll(gate_ext, 2 - k, 0)[HALO:], axis=0, keepdims=True) for k in range(2)]
            rows += [jnp.sum(dgc * gate_ext[HALO:], axis=0, keepdims=True), jnp.sum(dgc, axis=0, keepdims=True),
                     jnp.zeros((4, width), F32)]
            dconv_ref[:, cs] += jnp.concatenate(rows, axis=0)

        @pl.when(i == n_t - 1)
        def _():
            dwd_ref[...] = acc_ref[...].astype(BF16)

    tokf = pl.BlockSpec((tm, tf), lambda j, i: (i, j))
    return pl.pallas_call(
        body, name="ffn_bwd_act", grid=(d_ff // tf, n_t),
        in_specs=[pl.BlockSpec((tm, d), lambda j, i: (i, 0)), tokf,
                  pl.BlockSpec((HALO, tf), lambda j, i: (_halo_before(i, tm), j)), tokf, tokf, tokf,
                  pl.BlockSpec((tf, d), lambda j, i: (j, 0))],
        out_specs=[tokf, tokf, pl.BlockSpec((tf, d), lambda j, i: (j, 0)), pl.BlockSpec((8, tf), lambda j, i: (0, j))],
        out_shape=[jax.ShapeDtypeStruct((s_len, d_ff), BF16), jax.ShapeDtypeStruct((s_len, d_ff), BF16),
                   jax.ShapeDtypeStruct((d_ff, d), BF16), jax.ShapeDtypeStruct((8, d_ff), F32)],
        scratch_shapes=[pltpu.VMEM((tf, d), F32)],
        compiler_params=_params(("arbitrary", "arbitrary")),
    )(dy2, gate, gate, a, act, vd, w_down)


def _ffn_bwd_up(dgc, dval, w_up_t, conv_w, tm):
    s_len, d_ff = dgc.shape
    d = w_up_t.shape[1]
    n_t = s_len // tm

    def body(dg_ref, dgn_ref, dv_ref, cw_ref, w_ref, dup_ref, dh_ref):
        i = pl.program_id(0)
        nxt = dgn_ref[...].astype(F32) * (i < n_t - 1).astype(F32)
        ext = jnp.concatenate([dg_ref[...].astype(F32), nxt], axis=0)
        rows = tm + HALO
        dgate = (ext * cw_ref[2:3, :] + pltpu.roll(ext, rows - 1, 0) * cw_ref[1:2, :]
                 + pltpu.roll(ext, rows - 2, 0) * cw_ref[0:1, :])[:tm]
        dup = jnp.concatenate([dgate.astype(BF16), dv_ref[...]], axis=1)
        dup_ref[...] = dup
        dh_ref[...] = _dot(dup, w_ref[...], NN).astype(BF16)

    tokf = pl.BlockSpec((tm, d_ff), lambda i: (i, 0))
    return pl.pallas_call(
        body, name="ffn_bwd_up", grid=(n_t,),
        in_specs=[tokf, pl.BlockSpec((HALO, d_ff), lambda i: (jnp.minimum((i + 1) * (tm // HALO), s_len // HALO - 1), 0)),
                  tokf, pl.BlockSpec((3, d_ff), lambda i: (0, 0)), pl.BlockSpec((2 * d_ff, d), lambda i: (0, 0))],
        out_specs=[pl.BlockSpec((tm, 2 * d_ff), lambda i: (i, 0)), pl.BlockSpec((tm, d), lambda i: (i, 0))],
        out_shape=[jax.ShapeDtypeStruct((s_len, 2 * d_ff), BF16), jax.ShapeDtypeStruct((s_len, d), BF16)],
        compiler_params=_params(("arbitrary",)),
    )(dgc, dgc, dval, conv_w, w_up_t)


def _mix_bwd(dh2, dout, x1, y1, cat, attn, w_out_t, sc_f, g_pre_ffn, gt_m, g_post_mix, after, tm):
    s_len, d = x1.shape
    n_t = s_len // tm

    def body(dh_ref, do_ref, x1_ref, y1_ref, cat_ref, at_ref, wo_ref, sc_ref, g2_ref, gt_ref, g1_ref, after_ref,
             dx1_ref, dpool_ref, dattn_ref, delta_ref, dwo_ref, sums_ref, acc_ref):
        i = pl.program_id(0)
        dh = dh_ref[...].astype(F32)
        x1 = x1_ref[...]
        r2 = _rstd(x1)
        n2 = x1 * r2
        ng = n2 * g2_ref[...]
        dng = dh * (1.0 + sc_ref[...])
        dx1 = do_ref[...].astype(F32) + _norm_bwd(dng * g2_ref[...], n2, r2)
        dx1_ref[...] = dx1.astype(BF16)
        y1 = y1_ref[...].astype(F32)
        r1 = _rstd(y1)
        n1 = y1 * r1
        drn = dx1 * gt_ref[...]
        dy1 = _norm_bwd(drn * g1_ref[...], n1, r1).astype(BF16)
        dcat = _dot(dy1, wo_ref[...], NN)
        dpool_ref[...] = dcat[:, 0:256]
        lane = lax.broadcasted_iota(jnp.int32, (tm, LANES), 1)
        first = lane < HEAD_DIM
        for s in range(2):
            da = dcat[:, 256 + s * LANES:256 + (s + 1) * LANES]
            dattn_ref[s] = da
            prod = da * at_ref[:, s * LANES:(s + 1) * LANES]
            tot = jnp.sum(prod, axis=-1, keepdims=True)
            lo = jnp.sum(jnp.where(first, prod, 0.0), axis=-1, keepdims=True)
            delta_ref[s] = jnp.where(first, lo, tot - lo)
        dwo = _dot(dy1, cat_ref[...], TN)
        sums = jnp.concatenate(
            [jnp.sum(dh, axis=0, keepdims=True), jnp.sum(dh * ng, axis=0, keepdims=True),
             jnp.sum(dng * n2, axis=0, keepdims=True), jnp.sum(dx1 * (n1 * g1_ref[...]), axis=0, keepdims=True),
             jnp.sum(drn * n1, axis=0, keepdims=True), jnp.zeros((3, d), F32)], axis=0)

        @pl.when(i == 0)
        def _():
            acc_ref[...] = dwo
            sums_ref[...] = sums

        @pl.when(i > 0)
        def _():
            acc_ref[...] += dwo
            sums_ref[...] += sums

        @pl.when(i == n_t - 1)
        def _():
            dwo_ref[...] = acc_ref[...].astype(BF16)

    tile = lambda w: pl.BlockSpec((tm, w), lambda i: (i, 0))
    slab = pl.BlockSpec((2, tm, LANES), lambda i: (0, i, 0))
    vec = pl.BlockSpec((1, d), lambda i: (0, 0))
    return pl.pallas_call(
        body, name="mix_bwd", grid=(n_t,),
        in_specs=[tile(d), tile(d), tile(d), tile(d), tile(512), tile(256),
                  pl.BlockSpec((d, 512), lambda i: (0, 0)), vec, vec, vec, vec, pl.BlockSpec(memory_space=pl.ANY)],
        out_specs=[tile(d), tile(256), slab, slab, pl.BlockSpec((d, 512), lambda i: (0, 0)),
                   pl.BlockSpec((8, d), lambda i: (0, 0))],
        out_shape=[jax.ShapeDtypeStruct((s_len, d), BF16), jax.ShapeDtypeStruct((s_len, 256), F32),
                   jax.ShapeDtypeStruct((2, s_len, LANES), F32), jax.ShapeDtypeStruct((2, s_len, LANES), F32),
                   jax.ShapeDtypeStruct((d, 512), BF16), jax.ShapeDtypeStruct((8, d), F32)],
        scratch_shapes=[pltpu.VMEM((d, 512), F32)],
        compiler_params=_params(("arbitrary",)),
    )(dh2, dout, x1, y1, cat, attn, w_out_t, sc_f, g_pre_ffn, gt_m, g_post_mix, after)


def _pool_bwd(dpool, u_pool, w_blk, b_pool, pool_scale, tm):
    s_len = dpool.shape[0]
    n_t = s_len // tm

    def body(dp_ref, dpn_ref, u_ref, uh_ref, wb_ref, bp_ref, ps_ref, du_ref, dwp_ref, sums_ref, acc_ref):
        i = pl.program_id(0)
        u = u_ref[...]
        mixed, _ = _pool_mixed(u, uh_ref[...] * (i > 0).astype(F32), i, tm)
        mixed_b = mixed.astype(BF16)
        y = _dot(mixed_b, wb_ref[...], NN) + bp_ref[...]
        dp = dp_ref[...]
        dy = dp * ps_ref[...]
        dwb = _dot(mixed_b, dy.astype(BF16), TN)
        sums = jnp.concatenate([jnp.sum(dy, axis=0, keepdims=True), jnp.sum(dp * y, axis=0, keepdims=True),
                                jnp.zeros((6, 256), F32)], axis=0)
        dp_ext = jnp.concatenate([dp, dpn_ref[...] * (i < n_t - 1).astype(F32)], axis=0)
        dmix = _dot((dp_ext * ps_ref[...]).astype(BF16), wb_ref[...], NT)
        rows = tm + HALO
        grp = lax.broadcasted_iota(jnp.int32, (rows, 256), 1) // HEAD_DIM
        pick = lambda a, b, c, e: jnp.where(grp == 0, a, jnp.where(grp == 1, b, jnp.where(grp == 2, c, e)))
        pos = (i * tm + lax.broadcasted_iota(jnp.int32, (rows, 256), 0)).astype(F32)
        z = dmix / jnp.minimum(pos + 1.0, pick(*[float(w) for w in POOL_WINDOWS]))
        f2 = z + pltpu.roll(z, rows - 1, 0)
        f4 = f2 + pltpu.roll(f2, rows - 2, 0)
        f8 = f4 + pltpu.roll(f4, rows - 4, 0)
        f16 = f8 + pltpu.roll(f8, rows - 8, 0)
        du_ref[...] = (pick(f2, f4, f8, f16) - dmix)[:tm]

        @pl.when(i == 0)
        def _():
            acc_ref[...] = dwb
            sums_ref[...] = sums

        @pl.when(i > 0)
        def _():
            acc_ref[...] += dwb
            sums_ref[...] += sums

        @pl.when(i == n_t - 1)
        def _():
            full = acc_ref[...]
            for gi in range(len(POOL_WINDOWS)):
                lo = gi * HEAD_DIM
                dwp_ref[gi] = full[lo:lo + HEAD_DIM, lo:lo + HEAD_DIM]

    n_g = len(POOL_WINDOWS)
    tile = pl.BlockSpec((tm, 256), lambda i: (i, 0))
    const = lambda a: pl.BlockSpec(a.shape, lambda i: (0,) * a.ndim)
    return pl.pallas_call(
        body, name="pool_bwd", grid=(n_t,),
        in_specs=[tile, pl.BlockSpec((HALO, 256), lambda i: (jnp.minimum((i + 1) * (tm // HALO), s_len // HALO - 1), 0)),
                  tile, pl.BlockSpec((HALO, 256), lambda i: (_halo_before(i, tm), 0)),
                  const(w_blk), const(b_pool), const(pool_scale)],
        out_specs=[tile, pl.BlockSpec((n_g, HEAD_DIM, HEAD_DIM), lambda i: (0, 0, 0)), pl.BlockSpec((8, 256), lambda i: (0, 0))],
        out_shape=[jax.ShapeDtypeStruct((s_len, 256), F32), jax.ShapeDtypeStruct((n_g, HEAD_DIM, HEAD_DIM), F32),
                   jax.ShapeDtypeStruct((8, 256), F32)],
        scratch_shapes=[pltpu.VMEM((256, 256), F32)],
        compiler_params=_params(("arbitrary",)),
    )(dpool, dpool, u_pool, u_pool, w_blk, b_pool, pool_scale)


def _attn_bwd(qkv, dattn, lse_all, delta, after):
    s_len = qkv.shape[1]
    n_g = len(DILATIONS)

    def body(q_ref, k_ref, v_ref, do_ref, l_ref, dl_ref, after_ref, dq_ref, dk_ref, dv_ref):
        lane = lax.broadcasted_iota(jnp.int32, (BLOCK, LANES), 1)
        first = lane < HEAD_DIM

        def group(dil):
            nb = s_len // (BLOCK * dil)

            def block(t, carry):
                dk_part, dv_part = carry
                r, n = t // nb, t % nb
                cur = _block_rows(n, r, dil)
                prev = _block_rows(jnp.maximum(n - 1, 0), r, dil)
                q = q_ref[0, cur, :]
                do = do_ref[0, cur, :]
                lse = l_ref[0, cur, :]
                dlt = dl_ref[0, cur, :]
                kcat = jnp.concatenate([k_ref[0, prev, :], k_ref[0, cur, :]], axis=0).astype(BF16)
                vcat = jnp.concatenate([v_ref[0, prev, :], v_ref[0, cur, :]], axis=0).astype(BF16)
                valid = _band_mask(n)
                stack = lambda a: jnp.concatenate([jnp.where(first, a, 0.0), jnp.where(first, 0.0, a)], axis=0)
                rows2 = lambda a: jnp.concatenate([a[:, 0:1], a[:, HEAD_DIM:HEAD_DIM + 1]], axis=0)
                q2, do2 = stack(q).astype(BF16), stack(do).astype(BF16)
                valid2 = jnp.concatenate([valid, valid], axis=0)
                p = jnp.where(valid2, jnp.exp(_dot(q2, kcat, NT) - rows2(lse)), 0.0)
                ds = (p * (_dot(do2, vcat, NT) - rows2(dlt))).astype(BF16)
                dq2 = _dot(ds, kcat, NN)
                dq_ref[0, 0, cur, :] = jnp.where(first, dq2[:BLOCK], dq2[BLOCK:])
                dkc = _dot(ds, q2, TN)
                dvc = _dot(p.astype(BF16), do2, TN)
                dk_ref[0, 0, prev, :] = dk_part + dkc[:BLOCK]
                dv_ref[0, 0, prev, :] = dv_part + dvc[:BLOCK]
                dk_ref[0, 0, cur, :] = dkc[BLOCK:]
                dv_ref[0, 0, cur, :] = dvc[BLOCK:]
                return dkc[BLOCK:], dvc[BLOCK:]

            def blocks(tt, carry):
                for u in range(ATTN_BWD_UNROLL):
                    carry = block(tt * ATTN_BWD_UNROLL + u, carry)
                return carry

            zero = jnp.zeros((BLOCK, LANES), F32)
            lax.fori_loop(0, nb * dil // ATTN_BWD_UNROLL, blocks, (zero, zero))

        for gi, dil in enumerate(DILATIONS):
            pl.when(pl.program_id(1) == gi)(functools.partial(group, dil))

    def slab(base):
        return pl.BlockSpec((1, s_len, LANES), lambda s, g: (base + 2 * g + s, 0, 0))

    one = pl.BlockSpec((1, s_len, LANES), lambda s, g: (s, 0, 0))
    out = pl.BlockSpec((1, 1, s_len, LANES), lambda s, g: (g, s, 0, 0))
    shape = jax.ShapeDtypeStruct((n_g, 2, s_len, LANES), F32)
    return pl.pallas_call(
        body, name="attn_bwd", grid=(2, n_g),
        in_specs=[slab(0), slab(6), slab(12), one, one, one, pl.BlockSpec(memory_space=pl.ANY)],
        out_specs=[out, out, out], out_shape=[shape, shape, shape],
        compiler_params=_params(("arbitrary", "arbitrary")),
    )(qkv, qkv, qkv, dattn, lse_all, delta, after)


def _dproj_wgrad_in(du, dqkv, rope, h1, tm, cm):
    s_len = du.shape[0]
    d = h1.shape[1]
    n_proj = 256 + 18 * LANES
    n_t = s_len // tm

    def body(du_ref, dq_ref, dk_ref, dv_ref, cs_ref, spread_ref, h_ref, dproj_ref, dw_ref, acc_ref):
        i = pl.program_id(0)

        @pl.when(i == 0)
        def _():
            acc_ref[...] = jnp.zeros_like(acc_ref)

        dproj_ref[:, 0:256] = du_ref[...].astype(BF16)
        lanes = _rope_lanes(cs_ref, spread_ref)
        col = 256
        for kind, dref in enumerate((dq_ref, dk_ref, dv_ref)):
            for grp in range(3):
                for s in range(2):
                    piece = dref[grp, s]
                    if kind < 2:
                        piece = _rope_bwd(piece, lanes)
                    if kind == 0:
                        piece = piece * (HEAD_DIM ** -0.5)
                    dproj_ref[:, col:col + LANES] = piece.astype(BF16)
                    col += LANES

        for c0 in range(0, n_proj, cm):
            acc_ref[c0:c0 + cm, :] += _dot(dproj_ref[:, c0:c0 + cm], h_ref[...], TN)

        @pl.when(i == n_t - 1)
        def _():
            dw_ref[...] = acc_ref[...].astype(BF16)

    groups = pl.BlockSpec((len(DILATIONS), 2, tm, LANES), lambda i: (0, 0, i, 0))
    return pl.pallas_call(
        body, name="dproj_wgrad_in", grid=(n_t,),
        in_specs=[pl.BlockSpec((tm, 256), lambda i: (i, 0))] + [groups] * 3
        + [pl.BlockSpec((rope[0].shape[0], tm), lambda i: (0, i)), pl.BlockSpec(rope[1].shape, lambda i: (0, 0, 0)),
           pl.BlockSpec((tm, d), lambda i: (i, 0))],
        out_specs=[pl.BlockSpec((tm, n_proj), lambda i: (i, 0)), pl.BlockSpec((n_proj, d), lambda i: (0, 0))],
        out_shape=[jax.ShapeDtypeStruct((s_len, n_proj), BF16), jax.ShapeDtypeStruct((n_proj, d), BF16)],
        scratch_shapes=[pltpu.VMEM((n_proj, d), F32)],
        compiler_params=_params(("arbitrary",)),
    )(du, *dqkv, *rope, h1)


def _inproj_bwd(dproj, w_in_t, x, dx1, sc_m, g_pre_mix, after, tm):
    s_len, d = x.shape
    n_proj = w_in_t.shape[0]
    n_t = s_len // tm

    def body(dproj_ref, w_ref, x_ref, dx1_ref, sc_ref, g_ref, after_ref, dx_ref, sums_ref):
        i = pl.program_id(0)
        halves = [slice(0, tm // 2), slice(tm // 2, tm)]
        dhs = [_dot(dproj_ref[rs, :], w_ref[...], NN) for rs in halves]
        sums = None
        for rs, dh in zip(halves, dhs):
            xv = x_ref[rs, :]
            r = _rstd(xv)
            n = xv * r
            dng = dh * (1.0 + sc_ref[...])
            dx_ref[rs, :] = dx1_ref[rs, :].astype(F32) + _norm_bwd(dng * g_ref[...], n, r)
            part = jnp.concatenate([jnp.sum(dh, axis=0, keepdims=True), jnp.sum(dh * (n * g_ref[...]), axis=0, keepdims=True),
                                    jnp.sum(dng * n, axis=0, keepdims=True), jnp.zeros((5, d), F32)], axis=0)
            sums = part if sums is None else sums + part

        @pl.when(i == 0)
        def _():
            sums_ref[...] = sums

        @pl.when(i > 0)
        def _():
            sums_ref[...] += sums

    tile = lambda w: pl.BlockSpec((tm, w), lambda i: (i, 0))
    vec = pl.BlockSpec((1, d), lambda i: (0, 0))
    return pl.pallas_call(
        body, name="inproj_bwd", grid=(n_t,),
        in_specs=[tile(n_proj), pl.BlockSpec((n_proj, d), lambda i: (0, 0)), tile(d), tile(d), vec, vec,
                  pl.BlockSpec(memory_space=pl.ANY)],
        out_specs=[tile(d), pl.BlockSpec((8, d), lambda i: (0, 0))],
        out_shape=[jax.ShapeDtypeStruct((s_len, d), F32), jax.ShapeDtypeStruct((8, d), F32)],
        compiler_params=_params(("arbitrary",)),
    )(dproj, w_in_t, x, dx1, sc_m, g_pre_mix, after)


def _wgrad(a, b, name, tk, tmm):
    s_len, m = a.shape
    n = b.shape[1]
    n_k = s_len // tk

    def body(a_ref, b_ref, o_ref, acc_ref):
        k = pl.program_id(1)
        part = _dot(a_ref[...], b_ref[...], TN)

        @pl.when(k == 0)
        def _():
            acc_ref[...] = part

        @pl.when(k > 0)
        def _():
            acc_ref[...] += part

        @pl.when(k == n_k - 1)
        def _():
            o_ref[...] = acc_ref[...].astype(BF16)

    return pl.pallas_call(
        body, name=name, grid=(m // tmm, n_k),
        in_specs=[pl.BlockSpec((tk, tmm), lambda j, k: (k, j)), pl.BlockSpec((tk, n), lambda j, k: (k, 0))],
        out_specs=pl.BlockSpec((tmm, n), lambda j, k: (j, 0)),
        out_shape=jax.ShapeDtypeStruct((m, n), BF16),
        scratch_shapes=[pltpu.VMEM((tmm, n), F32)],
        compiler_params=_params(("arbitrary", "arbitrary")),
    )(a, b)


def _place():
    return lax.axis_index("x"), lax.axis_index("y"), lax.axis_index("c")


def _peer(k):
    x, y, c = _place()
    bx, by, bc = (k >> 2) & 1, (k >> 1) & 1, k & 1
    return (x ^ bx if bx else x, y ^ by if by else y, c ^ bc if bc else c)


def _index(pos):
    return 4 * pos[0] + 2 * pos[1] + pos[2]


def _entry_exchange(c_rows, w_ada, b_ada, taps, shards, later):
    d = c_rows.shape[1]
    ncol = w_ada.shape[1]
    n_w, n_p = len(shards), len(later)

    def body(c_ref, w_ref, b_ref, t_ref, *rest):
        srcs, rest = rest[:n_w], rest[n_w:]
        later_refs, rest = rest[:n_p], rest[n_p:]
        (call_ref, mod_ref, tall_ref), rest = rest[:3], rest[3:]
        outs, rest = rest[:n_w], rest[n_w:]
        zones, rest = rest[:n_p], rest[n_p:]
        stage_ref, s_send, s_recv, w_send, w_recv, local_sems = rest[:6]
        wide, narrow, place_sems = rest[6:6 + n_p], rest[6 + n_p:6 + 2 * n_p], rest[6 + 2 * n_p]
        x, y, c = _place()
        here, sibling = (x, y, c), (x, y, 1 - c)
        chips = [(1 - x, y), (x, 1 - y), (1 - x, 1 - y)]
        me = _index(here)

        def small(kind, src, dst, k):
            return pltpu.make_async_remote_copy(src_ref=src, dst_ref=dst, send_sem=s_send.at[kind, k - 1],
                                                recv_sem=s_recv.at[kind, k - 1], device_id=_peer(k), device_id_type=MESH)

        gather = lambda k: small(0, c_ref, call_ref.at[me], k)
        scatter = lambda k: small(1, stage_ref.at[_index(_peer(k))], mod_ref.at[me], k)
        gather_taps = lambda k: small(2, t_ref, tall_ref.at[me], k)

        def rows(w, pos):
            r = shards[w].shape[0]
            return outs[w].at[pl.ds(pl.multiple_of(_index(pos) * r, 16), r), :]

        def block(k, w, pos, to, own=False):
            return pltpu.make_async_remote_copy(
                src_ref=srcs[w] if own else rows(w, pos), dst_ref=rows(w, pos),
                send_sem=w_send.at[k, w], recv_sem=w_recv.at[k, w], device_id=to, device_id_type=MESH)

        call_ref[me] = c_ref[...]
        tall_ref[me] = t_ref[...]
        for k in range(1, N_DEV):
            gather(k).start()
        for k in range(1, N_DEV):
            gather_taps(k).start()
        mine = [pltpu.make_async_copy(srcs[w], rows(w, here), local_sems.at[w]) for w in range(n_w)]
        for cp in mine:
            cp.start()
        first = [block(0, w, here, sibling, own=True) for w in range(n_w)]
        first += [block(1 + j, w, here, (*chip, c), own=True) for j, chip in enumerate(chips) for w in range(n_w)]
        for cp in first:
            cp.start()
        fetch = [pltpu.make_async_copy(later_refs[w], wide[w], place_sems.at[0, w]) for w in range(n_p)]
        for cp in fetch:
            cp.start()

        for k in range(1, N_DEV):
            gather(k).wait_recv()
        cv = jnp.concatenate([call_ref[b, 0:1, :] for b in range(N_DEV)], axis=0)
        act = cv * jax.nn.sigmoid(cv)
        mod = lax.dot_general(act, w_ref[...], NN, preferred_element_type=F32,
                              precision=lax.Precision.HIGHEST) + b_ref[:, pl.ds(pl.multiple_of(me * ncol, LANES), ncol)]
        for b in range(N_DEV):
            stage_ref[b] = jnp.broadcast_to(mod[b:b + 1, :], (8, ncol))
        mod_ref[me] = stage_ref[me]
        for k in range(1, N_DEV):
            scatter(k).start()

        placed = []
        for w in range(n_p):
            fetch[w].wait()
            narrow[w][...] = wide[w][...].astype(BF16)
            r = later[w].shape[0]
            placed.append(pltpu.make_async_copy(narrow[w], zones[w].at[pl.ds(pl.multiple_of(me * r, 16), r), :],
                                                place_sems.at[1, w]))
            placed[-1].start()

        passed = []
        for j, chip in enumerate(chips):
            for w in range(n_w):
                block(1 + j, w, (*chip, c), here).wait_recv()
                fwd = block(4 + j, w, (*chip, c), sibling)
                fwd.start()
                passed.append(fwd)
        for w in range(n_w):
            block(0, w, sibling, here).wait_recv()
        for j, chip in enumerate(chips):
            for w in range(n_w):
                block(4 + j, w, (*chip, 1 - c), here).wait_recv()
        for k in range(1, N_DEV):
            scatter(k).wait_recv()
            gather_taps(k).wait_recv()
        for cp in first + passed:
            cp.wait_send()
        for k in range(1, N_DEV):
            gather(k).wait_send()
            scatter(k).wait_send()
            gather_taps(k).wait_send()
        for cp in mine + placed:
            cp.wait()

    vmem, hbm = pl.BlockSpec(memory_space=pltpu.VMEM), pl.BlockSpec(memory_space=pltpu.HBM)
    out = pl.pallas_call(
        body, name="entry_exchange",
        in_specs=[vmem] * 4 + [hbm] * (n_w + n_p), out_specs=[vmem] * 3 + [hbm] * (n_w + n_p),
        out_shape=[jax.ShapeDtypeStruct((N_DEV, 8, d), F32), jax.ShapeDtypeStruct((N_DEV, 8, ncol), F32),
                   jax.ShapeDtypeStruct((N_DEV,) + taps.shape, F32)]
        + [jax.ShapeDtypeStruct((N_DEV * s.shape[0], s.shape[1]), s.dtype) for s in shards]
        + [jax.ShapeDtypeStruct((N_DEV * s.shape[0], s.shape[1]), BF16) for s in later],
        scratch_shapes=[pltpu.VMEM((N_DEV, 8, ncol), F32), pltpu.SemaphoreType.DMA((3, N_DEV - 1)),
                        pltpu.SemaphoreType.DMA((3, N_DEV - 1)), pltpu.SemaphoreType.DMA((N_DEV - 1, n_w)),
                        pltpu.SemaphoreType.DMA((N_DEV - 1, n_w)), pltpu.SemaphoreType.DMA((n_w,))]
        + [pltpu.VMEM(s.shape, F32) for s in later] + [pltpu.VMEM(s.shape, BF16) for s in later]
        + [pltpu.SemaphoreType.DMA((2, n_p))],
        compiler_params=_params(),
    )(c_rows, w_ada, b_ada, taps, *shards, *later)
    return out[0], out[1], out[2], out[3:3 + n_w], out[3 + n_w:]


def _peer_copies(mode, srcs, lands, send_sems, recv_sems):
    if mode in ("gather_ici", "gather_d2d"):
        x, y, c = _place()
        sibling = (x, y, 1 - c)
        chips = [(1 - x, y), (x, 1 - y), (1 - x, 1 - y)]
        n = len(lands)

        def rows(w, pos):
            r = lands[w].shape[0] // N_DEV
            return lands[w].at[pl.ds(pl.multiple_of(_index(pos) * r, 16), r), :]

        def copy(k, w, src, dst, to):
            return pltpu.make_async_remote_copy(src_ref=src, dst_ref=dst, send_sem=send_sems.at[k * n + w],
                                                recv_sem=recv_sems.at[k * n + w], device_id=to, device_id_type=MESH)

        if mode == "gather_ici":
            targets = [sibling] + [(*chip, c) for chip in chips]
            return [copy(k, w, rows(w, (x, y, c)), rows(w, (x, y, c)), to) for k, to in enumerate(targets) for w in range(n)]
        return [copy(j, w, rows(w, (*chip, c)), rows(w, (*chip, c)), sibling)
                for j, chip in enumerate(chips) for w in range(n)]
    me = _index(_place())
    modes = (mode,) * len(srcs) if isinstance(mode, str) else mode
    copies = []
    for k in range(1, N_DEV):
        peer = _peer(k)
        for w, (src, land) in enumerate(zip(srcs, lands)):
            if modes[w] == "gather":
                r = src.shape[0]
                dst = land.at[pl.ds(pl.multiple_of(me * r, 16), r), :]
            elif modes[w] == "allgather":
                dst = land.at[me]
            else:
                r = src.shape[0] // N_DEV
                src = src.at[pl.ds(pl.multiple_of(_index(peer) * r, 16), r), :]
                dst = land.at[me]
            copies.append(pltpu.make_async_remote_copy(
                src_ref=src, dst_ref=dst, send_sem=send_sems.at[(k - 1) * len(srcs) + w],
                recv_sem=recv_sems.at[(k - 1) * len(srcs) + w],
                device_id=peer, device_id_type=MESH))
    return copies


def _exchange_start(mode, srcs, lands, name):
    n_s, n_a = len(srcs), len(srcs) + len(lands)
    n_cp = _COPIES_PER_ARRAY.get(mode, N_DEV - 1) * len(lands)

    def body(*refs):
        for cp in _peer_copies(mode, refs[:n_s], refs[n_s:n_a], refs[n_a], refs[n_a + 1]):
            cp.start()

    hbm, sem = pl.BlockSpec(memory_space=pltpu.HBM), pl.BlockSpec(memory_space=pltpu.SEMAPHORE)
    arrays = list(srcs) + list(lands)
    out = pl.pallas_call(
        body, name=name,
        out_shape=(pltpu.SemaphoreType.DMA((n_cp,)), pltpu.SemaphoreType.DMA((n_cp,)),
                   *[pltpu.HBM(a.shape, a.dtype) for a in arrays]),
        in_specs=[hbm] * n_a, out_specs=(sem, sem, *[hbm] * n_a),
        input_output_aliases={i: 2 + i for i in range(n_a)},
        compiler_params=pltpu.CompilerParams(has_side_effects=pltpu.SideEffectType.DATAFLOW_SIDE_EFFECTING),
    )(*[pltpu.with_memory_space_constraint(a, pltpu.HBM) for a in arrays])
    return out[0], out[1], out[2:2 + n_s], out[2 + n_s:2 + n_a], out[2]


_COPIES_PER_ARRAY = {"gather_ici": 4, "gather_d2d": 3}


def _exchange_wait(mode, send_sems, recv_sems, srcs, lands, after, name):
    n_s, n_a = len(srcs), len(srcs) + len(lands)

    def body(*refs):
        copies = _peer_copies(mode, refs[:n_s], refs[n_s:n_a], refs[n_a], refs[n_a + 1])
        for cp in copies:
            cp.wait_send()
        for cp in copies:
            cp.wait_recv()

    hbm, sem = pl.BlockSpec(memory_space=pltpu.HBM), pl.BlockSpec(memory_space=pltpu.SEMAPHORE)
    arrays = list(srcs) + list(lands)
    out = pl.pallas_call(
        body, name=name, out_shape=tuple(pltpu.HBM(a.shape, a.dtype) for a in arrays),
        in_specs=[hbm] * n_a + [sem, sem, pl.BlockSpec(memory_space=pl.ANY)], out_specs=tuple([hbm] * n_a),
        input_output_aliases={i: i for i in range(n_a)},
        compiler_params=pltpu.CompilerParams(has_side_effects=pltpu.SideEffectType.DATAFLOW_SIDE_EFFECTING),
    )(*arrays, send_sems, recv_sems, after)
    return out[:n_s], out[n_s:]


SMALL_WEIGHTS = ("b_ada", "g_pre_mix", "g_post_mix", "g_pre_ffn", "g_post_ffn", "w_pool", "b_pool", "pool_scale", "conv_b")


MOD_ROWS = ((0, 0), (0, 1), (1, 3), (1, 0), (1, 1), (2, 0))


def _small_sum_adam(mine, gathered, weights, moms, vels):
    n_l, n_w = len(mine), len(weights)
    d = mine[0].shape[1]

    def body(*refs):
        loc, got = refs[:n_l], refs[n_l:2 * n_l]
        w_refs, m_refs, v_refs = (refs[2 * n_l + k * n_w:2 * n_l + (k + 1) * n_w] for k in range(3))
        outs = refs[2 * n_l + 3 * n_w:]
        dmod_ref, conv_ref, loss_ref = outs[4 * n_w:]
        me = _index(_place())
        part = lambda a, dev: jnp.where(dev == me, loc[a][...], got[a][dev])
        totals = []
        for a in range(n_l):
            tot = part(a, 0)
            for dev in range(1, N_DEV):
                tot = tot + part(a, dev)
            totals.append(tot)
        t_in, t_mix, t_ffn, t_pool, t_blk, t_conv, t_loss = totals
        conv_ref[...] = t_conv
        loss_ref[...] = t_loss
        for dev in range(N_DEV):
            for k, (a, r) in enumerate(MOD_ROWS):
                dmod_ref[dev:dev + 1, k * d:(k + 1) * d] = part(a, dev)[r:r + 1, :]

        def update(idx, g, at=()):
            sel = lambda ref: ref.at[at] if at else ref
            delta, nm, nv = _adam_math(sel(w_refs[idx])[...], g, sel(m_refs[idx])[...], sel(v_refs[idx])[...])
            for k, val in enumerate((g, delta, nm, nv)):
                sel(outs[4 * idx + k])[...] = val

        tots = (t_in, t_mix, t_ffn)
        update(0, jnp.concatenate([tots[a][r:r + 1] for a, r in MOD_ROWS], axis=1))
        update(1, t_in[2:3])
        update(2, t_mix[4:5])
        update(3, t_mix[2:3])
        update(4, t_ffn[1:2])
        for gi in range(len(POOL_WINDOWS)):
            update(5, t_blk[gi], at=(0, gi))
        update(6, jnp.concatenate([t_pool[0:1, gi * HEAD_DIM:(gi + 1) * HEAD_DIM] for gi in range(len(POOL_WINDOWS))], axis=0),
               at=(0,))
        update(7, t_pool[1:2])
        update(8, t_conv[3:4])

    vmem = pl.BlockSpec(memory_space=pltpu.VMEM)
    out = pl.pallas_call(
        body, name="small_sum_adam", in_specs=[vmem] * (2 * n_l + 3 * n_w), out_specs=[vmem] * (4 * n_w + 3),
        out_shape=[jax.ShapeDtypeStruct(w.shape, F32) for w in weights for _ in range(4)]
        + [jax.ShapeDtypeStruct((N_DEV, 6 * d), F32), jax.ShapeDtypeStruct(mine[5].shape, F32),
           jax.ShapeDtypeStruct(mine[6].shape, F32)],
        compiler_params=_params(),
    )(*mine, *gathered, *weights, *moms, *vels)
    return out[:4 * n_w], out[4 * n_w], out[4 * n_w + 1], out[4 * n_w + 2]


def _adam_math(w, g, m, v):
    m = ADAM_B1 * m + (1.0 - ADAM_B1) * g
    v = ADAM_B2 * v + (1.0 - ADAM_B2) * (g * g)
    m_hat = m / (1.0 - ADAM_B1 ** ADAM_STEP)
    v_hat = v / (1.0 - ADAM_B2 ** ADAM_STEP)
    delta = -ADAM_LR * (m_hat / (jnp.sqrt(v_hat) + ADAM_EPS) + ADAM_WD * w)
    return delta, m, v


def _adam(w, g, m, v, name):
    def body(w_ref, g_ref, m_ref, v_ref, d_ref, nm_ref, nv_ref):
        d_ref[...], nm_ref[...], nv_ref[...] = _adam_math(w_ref[...], g_ref[...], m_ref[...], v_ref[...])

    vmem = pl.BlockSpec(memory_space=pltpu.VMEM)
    return pl.pallas_call(
        body, name=name, in_specs=[vmem] * 4, out_specs=[vmem] * 3,
        out_shape=[jax.ShapeDtypeStruct(w.shape, F32)] * 3, compiler_params=_params(),
    )(w, g, m, v)


def _sum_adam(own, parts, w, m, v, me, name, tr):
    _, rows, cols = parts.shape
    turned = w.shape == (cols, rows) and rows != cols
    assert tr == rows or not turned
    n_t = rows // tr

    def body(me_ref, own_ref, p_ref, w_ref, m_ref, v_ref, g_ref, d_ref, nm_ref, nv_ref):
        part = lambda dev: jnp.where(dev == me_ref[0], own_ref[...], p_ref[dev]).astype(F32)
        g = part(0)
        for dev in range(1, N_DEV):
            g = g + part(dev)
        g = g.T if turned else g
        g_ref[...] = g
        d_ref[...], nm_ref[...], nv_ref[...] = _adam_math(w_ref[...], g, m_ref[...], v_ref[...])

    spec = pl.BlockSpec((cols, rows) if turned else (tr, cols), lambda i, me_ref: (i, 0))
    shape = jax.ShapeDtypeStruct(w.shape, F32)
    return pl.pallas_call(
        body, name=name, out_shape=[shape] * 4,
        grid_spec=pltpu.PrefetchScalarGridSpec(
            num_scalar_prefetch=1, grid=(n_t,),
            in_specs=[pl.BlockSpec((tr, cols), lambda i, me_ref: (me_ref[0] * n_t + i, 0)),
                      pl.BlockSpec((N_DEV, tr, cols), lambda i, me_ref: (0, i, 0)), spec, spec, spec],
            out_specs=[spec] * 4),
        compiler_params=_params(("arbitrary",)),
    )(me.reshape(1).astype(jnp.int32), own, parts, w, m, v)


def _ada_grad_adam(c_all, dmod_all, w, m, v, tr):
    rows, cols = w.shape

    def body(c_ref, dm_ref, w_ref, m_ref, v_ref, g_ref, d_ref, nm_ref, nv_ref):
        cv = c_ref[...]
        act = cv * jax.nn.sigmoid(cv)
        dmod = dm_ref[:, pl.ds(pl.multiple_of(_index(_place()) * cols, LANES), cols)]
        g = lax.dot_general(act, dmod, TN, preferred_element_type=F32, precision=lax.Precision.HIGHEST)
        g_ref[...] = g
        d_ref[...], nm_ref[...], nv_ref[...] = _adam_math(w_ref[...], g, m_ref[...], v_ref[...])

    spec = pl.BlockSpec((tr, cols), lambda i: (i, 0))
    shape = jax.ShapeDtypeStruct((rows, cols), F32)
    return pl.pallas_call(
        body, name="ada_grad_adam", grid=(rows // tr,),
        in_specs=[pl.BlockSpec((N_DEV, tr), lambda i: (0, i)), pl.BlockSpec(dmod_all.shape, lambda i: (0, 0)), spec, spec, spec],
        out_specs=[spec] * 4, out_shape=[shape] * 4, compiler_params=_params(("arbitrary",)),
    )(c_all, dmod_all, w, m, v)


def _rope_tables(positions):
    inv_freq = ROPE_THETA ** (-jnp.arange(0, 2 * ROT_HALF, 2, dtype=F32) / (2 * ROT_HALF))
    ang = inv_freq[:, None] * positions.astype(F32)[None, :]
    rows = jnp.concatenate([jnp.cos(ang), jnp.sin(ang), jnp.ones_like(ang)], axis=0)
    spread = [[[0.0] * LANES for _ in range(3 * ROT_HALF)] for _ in range(3)]
    for lane in range(LANES):
        p, j = lane % HEAD_DIM, lane % ROT_HALF
        if p < ROT_HALF:
            spread[0][j][lane] = 1.0
            spread[1][ROT_HALF + j][lane] = -1.0
        elif p < 2 * ROT_HALF:
            spread[0][j][lane] = 1.0
            spread[2][ROT_HALF + j][lane] = 1.0
        else:
            spread[0][2 * ROT_HALF][lane] = 1.0
    return rows, jnp.array(spread, F32)


def _pad_rows(a, rows):
    return jnp.pad(a, ((0, rows - a.shape[0]), (0, 0)))


def _sequence_step(xs, target, rope, mods, gains, w_in_t, w_out_t, relay_ffn, fetch_ffn, send_grads, w_blk_b, b_pool_r,
                   pool_scale_r, conv_w_all, conv_b, after):
    sh_m, sc_m, gt_m, sh_f, sc_f, gt_f = mods
    g_pre_mix, g_post_mix, g_pre_ffn, g_post_ffn = gains
    h1, u_pool, qkv = _premix_inproj(xs, sh_m, sc_m, g_pre_mix, w_in_t, rope, after, tm=512)
    o_g, lse_g = _attn_fwd(qkv)
    x1, y1, h2, cat, attn, lse_all = _mix_out(xs, u_pool, o_g, lse_g, w_blk_b, b_pool_r, pool_scale_r, w_out_t,
                                              gt_m, g_post_mix, g_pre_ffn, sc_f, sh_f, tm=256)
    relay_ffn(x1)
    w_up_t, w_down_f = fetch_ffn(x1)
    gate, a_ffn, act, vd, dy2, dout, sums_ffn, loss_loc = _ffn_fwd_loss(h2, x1, target, w_up_t, w_down_f, conv_w_all, conv_b,
                                                              gt_f, g_post_ffn, tm=256, ck=256)

    dgc, dval, dw_down, dconv = _ffn_bwd_act(dy2, gate, a_ffn, act, vd, w_down_f, tm=512, tf=1408, ck=256)
    dup, dh2 = _ffn_bwd_up(dgc, dval, w_up_t, conv_w_all, tm=256)
    dw_up_t = _wgrad(dup, h2, "wgrad_up", tk=2048, tmm=1408)
    token = send_grads("ffn", [dw_up_t, dw_down], [])
    dx1, dpool, dattn, delta, dw_out_t, sums_mix = _mix_bwd(dh2, dout, x1, y1, cat, attn, w_out_t, sc_f,
                                                           g_pre_ffn, gt_m, g_post_mix, token, tm=256)
    du, dw_blk, sums_pool = _pool_bwd(dpool, u_pool, w_blk_b, b_pool_r, pool_scale_r, tm=512)
    token = send_grads("out", [dw_out_t], [sums_mix, sums_ffn, sums_pool, dw_blk, dconv, loss_loc])
    dproj, dw_in_t = _dproj_wgrad_in(du, _attn_bwd(qkv, dattn, lse_all, delta, token), rope, h1, tm=512, cm=512)
    token = send_grads("in", [dw_in_t], [])
    grad_x, sums_in = _inproj_bwd(dproj, w_in_t, xs, dx1, sc_m, g_pre_mix, token, tm=256)
    return (loss_loc, grad_x, dw_in_t, dw_out_t, dw_up_t, dw_down, dw_blk, dconv,
            sums_in, sums_mix, sums_ffn, sums_pool)


def kernel(x, c, positions, w_ada, b_ada, g_pre_mix, g_post_mix, g_pre_ffn, g_post_ffn, w_in, w_pool, b_pool, pool_scale, w_out, w_up, conv_w, conv_b, w_down, loss_target, m_w_ada, m_b_ada, m_g_pre_mix, m_g_post_mix, m_g_pre_ffn, m_g_post_ffn, m_w_in, m_w_pool, m_b_pool, m_pool_scale, m_w_out, m_w_up, m_conv_w, m_conv_b, m_w_down, v_w_ada, v_b_ada, v_g_pre_mix, v_g_post_mix, v_g_pre_ffn, v_g_post_ffn, v_w_in, v_w_pool, v_b_pool, v_pool_scale, v_w_out, v_w_up, v_conv_w, v_conv_b, v_w_down):
    s_len, d = x.shape[1], x.shape[2]
    d_ff = w_down.shape[1] * N_DEV
    me = _index(_place())
    xs, target = x[0], loss_target[0]

    c_all, mod, taps_all, (w_in_t, w_out_t), lands = _entry_exchange(
        jnp.broadcast_to(c, (8, d)), w_ada[0], b_ada, _pad_rows(conv_w[0], 8),
        [w_in[0].T.astype(BF16), w_out[0].T.astype(BF16)], [w_up[0].T, w_down[0]])
    c_all = c_all[:, 0, :]
    conv_w_all = jnp.transpose(taps_all[:, :3, :], (1, 0, 2)).reshape(3, d_ff)
    sh_m, sc_m, gt_m, sh_f, sc_f, gt_f = [mod[:, 0, :].reshape(1, -1)[:, k * d:(k + 1) * d] for k in range(6)]

    rope = _rope_tables(positions[0])
    w_blk = jnp.zeros((256, 256), F32)
    for gi in range(4):
        w_blk = lax.dynamic_update_slice(w_blk, w_pool[0, gi], (gi * HEAD_DIM, gi * HEAD_DIM))
    w_blk_b = w_blk.astype(BF16)
    b_pool_r, pool_scale_r = b_pool.reshape(1, 256), pool_scale.reshape(1, 256)

    w_send, w_recv, w_src, w_land, w_token = _exchange_start("gather_ici", [], lands, "ffn_weights_ici_start")
    relay = []

    def relay_ffn(after):
        _, blocks = _exchange_wait("gather_ici", w_send, w_recv, w_src, w_land, after, "ffn_weights_ici_wait")
        relay.extend(_exchange_start("gather_d2d", [], blocks, "ffn_weights_d2d_start"))

    def fetch_ffn(after):
        return _exchange_wait("gather_d2d", relay[0], relay[1], [], relay[3], after, "ffn_weights_d2d_wait")[1]

    flights = {}

    def send_grads(tag, slabs, whole):
        lands = [lax.empty((N_DEV, g.shape[0] // N_DEV, g.shape[1]), g.dtype) for g in slabs]
        lands += [lax.empty((N_DEV,) + a.shape, F32) for a in whole]
        modes = ("scatter",) * len(slabs) + ("allgather",) * len(whole)
        flights[tag] = (modes, *_exchange_start(modes, slabs + whole, lands, f"grads_{tag}_start"))
        return flights[tag][5]

    def arrived(tag, after):
        return _exchange_wait(*flights[tag][:5], after, f"grads_{tag}_wait")

    _, grad_x, *_, sums_in, _, _, _ = _sequence_step(
        xs, target, rope, (sh_m, sc_m, gt_m, sh_f, sc_f, gt_f), (g_pre_mix, g_post_mix, g_pre_ffn, g_post_ffn),
        w_in_t, w_out_t, relay_ffn, fetch_ffn, send_grads, w_blk_b, b_pool_r, pool_scale_r, conv_w_all, conv_b,
        w_token)

    send_grads("last", [], [sums_in])

    (own_up, own_down), (parts_up, parts_down) = arrived("ffn", flights["last"][5])
    new_up = _sum_adam(own_up, parts_up, w_up[0].T, m_w_up[0].T, v_w_up[0].T, me, "adam_w_up", 352)
    new_down = _sum_adam(own_down, parts_down, w_down[0], m_w_down[0], v_w_down[0], me, "adam_w_down", 176)
    (own_out, *small), (parts_out, *gathered) = arrived("out", new_down[0])
    new_out = _sum_adam(own_out, parts_out, w_out[0], m_w_out[0], v_w_out[0], me, "adam_w_out", 128)
    (own_in,), (parts_in,) = arrived("in", new_out[0])
    new_in = _sum_adam(own_in, parts_in, w_in[0].T, m_w_in[0].T, v_w_in[0].T, me, "adam_w_in", 160)
    big = {"w_up": [a.T for a in new_up], "w_down": new_down, "w_out": new_out, "w_in": [a.T for a in new_in]}

    rep_w = [b_ada, g_pre_mix, g_post_mix, g_pre_ffn, g_post_ffn, w_pool, b_pool, pool_scale, conv_b]
    rep_m = [m_b_ada, m_g_pre_mix, m_g_post_mix, m_g_pre_ffn, m_g_post_ffn, m_w_pool, m_b_pool, m_pool_scale, m_conv_b]
    rep_v = [v_b_ada, v_g_pre_mix, v_g_post_mix, v_g_pre_ffn, v_g_post_ffn, v_w_pool, v_b_pool, v_pool_scale, v_conv_b]
    mine_last, got_last = arrived("last", new_in[0])
    small, gathered = [*mine_last, *small], [*got_last, *gathered]
    rep_out, dmod_all, dconv_tot, loss_tot = _small_sum_adam(small, gathered, rep_w, rep_m, rep_v)
    g_rep, d_rep, nm_rep, nv_rep = (rep_out[k::4] for k in range(4))

    fcol = d_ff // N_DEV
    taps = lambda a: jnp.transpose(a, (1, 0, 2))
    g_cw = lax.dynamic_slice(dconv_tot, (0, me * fcol), (3, fcol))[None]
    d_cw, nm_cw, nv_cw = [taps(a) for a in _adam(taps(conv_w), taps(g_cw), taps(m_conv_w), taps(v_conv_w), "adam_conv_w")]

    g_ada, d_ada, nm_ada, nv_ada = _ada_grad_adam(c_all, dmod_all, w_ada[0], m_w_ada[0], v_w_ada[0], 256)

    loss = loss_tot[0, 0]

    def group(k):
        rep = (g_rep, d_rep, nm_rep, nv_rep)[k]
        ada = (g_ada, d_ada, nm_ada, nv_ada)[k][None]
        cw = (g_cw, d_cw, nm_cw, nv_cw)[k]
        return [ada, rep[0], rep[1], rep[2], rep[3], rep[4], big["w_in"][k][None], rep[5], rep[6], rep[7],
                big["w_out"][k][None], big["w_up"][k][None], cw, rep[8], big["w_down"][k][None]]

    return (loss, grad_x[None], *group(0), *group(1), *group(2), *group(3))
```

```python
import functools
import math

import jax
import jax.numpy as jnp
from jax import lax
from jax.experimental import pallas as pl
from jax.experimental.pallas import tpu as pltpu

F32 = jnp.float32
BF16 = jnp.bfloat16
MESH = pl.DeviceIdType.MESH

N_DEV = 8
HEAD_DIM = 64
ROT_HALF = 8
ROPE_THETA = 500000.0
POOL_WINDOWS = (2, 4, 8, 16)
DILATIONS = (1, 4, 16)
BLOCK = 128
NORM_EPS = 1e-6
HALO = 16
MASKED = -1e30
ATTN_FWD_UNROLL = 8
ATTN_BWD_UNROLL = 8

ADAM_LR = 0.001
ADAM_B1 = 0.9
ADAM_B2 = 0.999
ADAM_EPS = 1e-08
ADAM_WD = 0.01
ADAM_STEP = 10

V7X_VMEM_LIMIT = 56 * 1024 * 1024
LANES = 128

NT = (((1,), (1,)), ((), ()))
NN = (((1,), (0,)), ((), ()))
TN = (((0,), (0,)), ((), ()))


def _dot(a, b, dims):
    return lax.dot_general(a, b, dims, preferred_element_type=F32)


def _params(sem=None, vmem=V7X_VMEM_LIMIT):
    if sem is None:
        return pltpu.CompilerParams(vmem_limit_bytes=vmem)
    return pltpu.CompilerParams(dimension_semantics=sem, vmem_limit_bytes=vmem)


def _rstd(v):
    return lax.rsqrt(jnp.mean(v * v, axis=-1, keepdims=True) + NORM_EPS)


def _norm_bwd(dn, n, rstd):
    return rstd * (dn - n * jnp.mean(dn * n, axis=-1, keepdims=True))


def _rope_lanes(cs_ref, spread_ref):
    return [lax.dot_general(cs_ref[...], spread_ref[k], TN, preferred_element_type=F32, precision=lax.Precision.HIGHEST)
            for k in range(3)]


def _rope_fwd(p, lanes):
    return p * lanes[0] + pltpu.roll(p, LANES - ROT_HALF, 1) * lanes[1] + pltpu.roll(p, ROT_HALF, 1) * lanes[2]


def _rope_bwd(dp, lanes):
    return dp * lanes[0] + pltpu.roll(dp * lanes[1], ROT_HALF, 1) + pltpu.roll(dp * lanes[2], LANES - ROT_HALF, 1)


def _gelu_parts(v):
    k2 = 2.0 * math.sqrt(2.0 / math.pi)
    c = 0.044715
    v2 = v * v
    s = jax.nn.sigmoid(v * (k2 + (k2 * c) * v2))
    g = v * s
    dg = s + g * (1.0 - s) * (k2 + (3.0 * k2 * c) * v2)
    return g, dg


def _halo_before(i, tile):
    return jnp.maximum(i * (tile // HALO) - 1, 0)


def _premix_inproj(x, sh, sc, g, w_in_t, rope, after, tm):
    s_len, d = x.shape
    n_proj = w_in_t.shape[0]
    n_slab = (n_proj - 256) // LANES

    def body(x_ref, sh_ref, sc_ref, g_ref, w_ref, cs_ref, spread_ref, after_ref, h_ref, up_ref, qkv_ref):
        xv = x_ref[...]
        h = (xv * _rstd(xv) * g_ref[...]) * (1.0 + sc_ref[...]) + sh_ref[...]
        hb = h.astype(BF16)
        h_ref[...] = hb
        up_ref[...] = _dot(hb, w_ref[0:256, :], NT)
        lanes = _rope_lanes(cs_ref, spread_ref)
        for pair in range(n_slab // 2):
            p = _dot(hb, w_ref[256 + 256 * pair:512 + 256 * pair, :], NT)
            for half in range(2):
                ph = p[:, half * LANES:(half + 1) * LANES]
                if pair < 6:
                    ph = _rope_fwd(ph, lanes)
                if pair < 3:
                    ph = ph * (HEAD_DIM ** -0.5)
                qkv_ref[2 * pair + half] = ph

    vec = pl.BlockSpec((1, d), lambda i: (0, 0))
    return pl.pallas_call(
        body, name="premix_inproj", grid=(s_len // tm,),
        in_specs=[pl.BlockSpec((tm, d), lambda i: (i, 0)), vec, vec, vec,
                  pl.BlockSpec((n_proj, d), lambda i: (0, 0)),
                  pl.BlockSpec((rope[0].shape[0], tm), lambda i: (0, i)), pl.BlockSpec(rope[1].shape, lambda i: (0, 0, 0)),
                  pl.BlockSpec(memory_space=pl.ANY)],
        out_specs=[pl.BlockSpec((tm, d), lambda i: (i, 0)),
                   pl.BlockSpec((tm, 256), lambda i: (i, 0)),
                   pl.BlockSpec((n_slab, tm, LANES), lambda i: (0, i, 0))],
        out_shape=[jax.ShapeDtypeStruct((s_len, d), BF16),
                   jax.ShapeDtypeStruct((s_len, 256), F32),
                   jax.ShapeDtypeStruct((n_slab, s_len, LANES), F32)],
        compiler_params=_params(("arbitrary",)),
    )(x, sh, sc, g, w_in_t, *rope, after)


def _block_rows(n, r, dil):
    start = n * (BLOCK * dil) + r
    if dil == 1:
        return pl.ds(pl.multiple_of(start, BLOCK), BLOCK)
    return pl.ds(start, BLOCK, stride=dil)


def _band_mask(n):
    ri = lax.broadcasted_iota(jnp.int32, (BLOCK, 2 * BLOCK), 0)
    cj = lax.broadcasted_iota(jnp.int32, (BLOCK, 2 * BLOCK), 1)
    cur = (cj >= BLOCK) & (cj - BLOCK <= ri)
    prev = (cj < BLOCK) & (cj >= ri) & (n > 0)
    return cur | prev


def _attn_fwd(qkv):
    s_len = qkv.shape[1]
    n_g = len(DILATIONS)

    def body(q_ref, k_ref, v_ref, o_ref, lse_ref):
        lane = lax.broadcasted_iota(jnp.int32, (BLOCK, LANES), 1)
        first = lane < HEAD_DIM

        def group(dil):
            nb = s_len // (BLOCK * dil)

            def block(t, carry):
                r, n = t // nb, t % nb
                cur = _block_rows(n, r, dil)
                prev = _block_rows(jnp.maximum(n - 1, 0), r, dil)
                q = q_ref[0, cur, :]
                kcat = jnp.concatenate([k_ref[0, prev, :], k_ref[0, cur, :]], axis=0).astype(BF16)
                vcat = jnp.concatenate([v_ref[0, prev, :], v_ref[0, cur, :]], axis=0).astype(BF16)
                valid = _band_mask(n)
                q2 = jnp.concatenate([jnp.where(first, q, 0.0), jnp.where(first, 0.0, q)], axis=0).astype(BF16)
                s = jnp.where(jnp.concatenate([valid, valid], axis=0), _dot(q2, kcat, NT), MASKED)
                m = jnp.max(s, axis=-1, keepdims=True)
                p = jnp.exp(s - m)
                den = jnp.sum(p, axis=-1, keepdims=True)
                o2 = _dot(p.astype(BF16), vcat, NN) / den
                lse2 = m + jnp.log(den)
                o_ref[0, 0, cur, :] = jnp.where(first, o2[:BLOCK], o2[BLOCK:])
                lse_ref[0, 0, cur, :] = jnp.where(first, lse2[:BLOCK], lse2[BLOCK:])
                return carry

            lax.fori_loop(0, nb * dil, block, 0, unroll=ATTN_FWD_UNROLL)

        for gi, dil in enumerate(DILATIONS):
            pl.when(pl.program_id(0) == gi)(functools.partial(group, dil))

    def slab(base):
        return pl.BlockSpec((1, s_len, LANES), lambda g, s: (base + 2 * g + s, 0, 0))

    out = pl.BlockSpec((1, 1, s_len, LANES), lambda g, s: (g, s, 0, 0))
    shape = jax.ShapeDtypeStruct((n_g, 2, s_len, LANES), F32)
    return pl.pallas_call(
        body, name="attn_fwd", grid=(n_g, 2),
        in_specs=[slab(0), slab(6), slab(12)], out_specs=[out, out], out_shape=[shape, shape],
        compiler_params=_params(("arbitrary", "arbitrary")),
    )(qkv, qkv, qkv)


def _pool_mixed(u, halo, i, tm):
    ue = jnp.concatenate([halo, u], axis=0)
    s2 = ue + pltpu.roll(ue, 1, 0)
    s4 = s2 + pltpu.roll(s2, 2, 0)
    s8 = s4 + pltpu.roll(s4, 4, 0)
    s16 = s8 + pltpu.roll(s8, 8, 0)
    grp = lax.broadcasted_iota(jnp.int32, (tm, 256), 1) // HEAD_DIM
    pick = lambda a, b, c, e: jnp.where(grp == 0, a, jnp.where(grp == 1, b, jnp.where(grp == 2, c, e)))
    win_sum = pick(s2[HALO:], s4[HALO:], s8[HALO:], s16[HALO:])
    pos = (i * tm + lax.broadcasted_iota(jnp.int32, (tm, 256), 0)).astype(F32)
    count = jnp.minimum(pos + 1.0, pick(*[float(w) for w in POOL_WINDOWS]))
    return win_sum / count - u, count


def _mix_out(x, u_pool, o_g, lse_g, w_blk, b_pool, pool_scale, w_out_t, gt_m, g_post_mix, g_pre_ffn, sc_f, sh_f, tm):
    s_len, d = x.shape

    def body(x_ref, u_ref, uh_ref, o_ref, l_ref, wb_ref, bp_ref, ps_ref, wo_ref,
             gt_ref, g1_ref, g2_ref, sc_ref, sh_ref,
             x1_ref, y1_ref, h2_ref, cat_ref, attn_ref, lall_ref):
        (o0, o1, o2), (l0, l1, l2) = (o_ref.at[g] for g in range(3)), (l_ref.at[g] for g in range(3))
        i = pl.program_id(0)
        u = u_ref[...]
        halo = uh_ref[...] * (i > 0).astype(F32)
        mixed, _ = _pool_mixed(u, halo, i, tm)
        y = _dot(mixed.astype(BF16), wb_ref[...], NN) + bp_ref[...]
        pool = y * ps_ref[...]
        attn = []
        for s in range(2):
            la, lb, lc = l0[s], l1[s], l2[s]
            mx = jnp.maximum(jnp.maximum(la, lb), lc)
            ea, eb, ec = jnp.exp(la - mx), jnp.exp(lb - mx), jnp.exp(lc - mx)
            den = ea + eb + ec
            lall_ref[s] = mx + jnp.log(den)
            attn.append((ea / den) * o0[s] + (eb / den) * o1[s] + (ec / den) * o2[s])
        attn = jnp.concatenate(attn, axis=1)
        attn_ref[...] = attn
        cat = jnp.concatenate([pool, attn], axis=1).astype(BF16)
        cat_ref[...] = cat
        y1 = _dot(cat, wo_ref[...], NT)
        y1_ref[...] = y1.astype(BF16)
        x1 = x_ref[...] + gt_ref[...] * (y1 * _rstd(y1) * g1_ref[...])
        x1_ref[...] = x1
        h2 = (x1 * _rstd(x1) * g2_ref[...]) * (1.0 + sc_ref[...]) + sh_ref[...]
        h2_ref[...] = h2.astype(BF16)

    tile = lambda w: pl.BlockSpec((tm, w), lambda i: (i, 0))
    slab = pl.BlockSpec((2, tm, LANES), lambda i: (0, i, 0))
    groups = pl.BlockSpec((len(DILATIONS), 2, tm, LANES), lambda i: (0, 0, i, 0))
    const = lambda a: pl.BlockSpec(a.shape, lambda i: (0,) * a.ndim)
    return pl.pallas_call(
        body, name="mix_out", grid=(s_len // tm,),
        in_specs=[tile(d), tile(256), pl.BlockSpec((HALO, 256), lambda i: (_halo_before(i, tm), 0)),
                  groups, groups,
                  const(w_blk), const(b_pool), const(pool_scale), const(w_out_t),
                  const(gt_m), const(g_post_mix), const(g_pre_ffn), const(sc_f), const(sh_f)],
        out_specs=[tile(d), tile(d), tile(d), tile(512), tile(256), slab],
        out_shape=[jax.ShapeDtypeStruct((s_len, d), F32), jax.ShapeDtypeStruct((s_len, d), BF16),
                   jax.ShapeDtypeStruct((s_len, d), BF16), jax.ShapeDtypeStruct((s_len, 512), BF16),
                   jax.ShapeDtypeStruct((s_len, 256), F32), jax.ShapeDtypeStruct((2, s_len, LANES), F32)],
        compiler_params=_params(("arbitrary",)),
    )(x, u_pool, u_pool, o_g, lse_g, w_blk, b_pool, pool_scale, w_out_t, gt_m, g_post_mix, g_pre_ffn, sc_f, sh_f)


def _conv_gate(gate_ext, cw, cb):
    gc = gate_ext * cw[2:3, :] + pltpu.roll(gate_ext, 1, 0) * cw[1:2, :] + pltpu.roll(gate_ext, 2, 0) * cw[0:1, :]
    return gc[HALO:] + cb


def _ffn_fwd_loss(h2, x1, target, w_up_t, w_down, conv_w, conv_b, gt_f, g_post_ffn, tm, ck):
    s_len, d = x1.shape
    d_ff = w_down.shape[0]
    n_t, n_c = s_len // tm, d_ff // ck

    def body(h_ref, hh_ref, x1_ref, tgt_ref, wg_ref, wv_ref, wd_ref, cw_ref, cb_ref, gt_ref, g_ref,
             gate_ref, a_ref, act_ref, vd_ref, dy2_ref, dout_ref, sums_ref, loss_ref, acc_ref):
        i = pl.program_id(0)

        @pl.when(i == 0)
        def _():
            sums_ref[...] = jnp.zeros_like(sums_ref)
            loss_ref[...] = jnp.zeros_like(loss_ref)
            acc_ref[...] = jnp.zeros_like(acc_ref)

        def finish(live):
            y2 = acc_ref[...]
            rstd = _rstd(y2)
            n = y2 * rstd
            rn = n * g_ref[...]
            err = x1_ref[...] + gt_ref[...] * rn - tgt_ref[...]
            keep = lambda v: jnp.where(live, v, 0.0)
            loss_ref[...] += keep(0.5 * jnp.sum(jnp.mean(err * err, axis=-1, keepdims=True), axis=0, keepdims=True))
            dout = err * (1.0 / d)
            dout_ref[...] = dout.astype(BF16)
            drn = dout * gt_ref[...]
            sums_ref[0:1, :] += keep(jnp.sum(dout * rn, axis=0, keepdims=True))
            sums_ref[1:2, :] += keep(jnp.sum(drn * n, axis=0, keepdims=True))
            dy2_ref[...] = _norm_bwd(drn * g_ref[...], n, rstd).astype(BF16)

        @pl.when(i < n_t)
        def _():
            h = h_ref[...]
            h_ext = jnp.concatenate([hh_ref[...], h], axis=0)
            row = lax.broadcasted_iota(jnp.int32, (tm + HALO, ck), 0)
            no_halo = (row < HALO) & (i == 0)

            def up(c):
                cs = slice(c * ck, (c + 1) * ck)
                return jnp.where(no_halo, 0.0, _dot(h_ext, wg_ref[cs, :], NT)), _dot(h, wv_ref[cs, :], NT)

            part = None
            nxt = up(0)
            finish(i > 0)
            for c in range(n_c):
                cs = slice(c * ck, (c + 1) * ck)
                gate_ext, val = nxt
                if c + 1 < n_c:
                    nxt = up(c + 1)
                act, dact = _gelu_parts(_conv_gate(gate_ext, cw_ref[:, cs], cb_ref[:, cs]))
                a = (act * val).astype(BF16)
                gate_ref[:, cs] = gate_ext[HALO:].astype(BF16)
                a_ref[:, cs] = a
                act_ref[:, cs] = act.astype(BF16)
                vd_ref[:, cs] = (val * dact).astype(BF16)
                p = _dot(a, wd_ref[cs, :], NN)
                part = p if part is None else part + p
            acc_ref[...] = part

        @pl.when(i == n_t)
        def _():
            finish(True)

    this = lambda i: jnp.minimum(i, n_t - 1)
    before = lambda i: jnp.maximum(i - 1, 0)
    tok = lambda w, at: pl.BlockSpec((tm, w), lambda i: (at(i), 0))
    vec = pl.BlockSpec((1, d), lambda i: (0, 0))
    once = lambda shape, imap: pl.BlockSpec(shape, imap, pipeline_mode=pl.Buffered(1))
    return pl.pallas_call(
        body, name="ffn_fwd_loss", grid=(n_t + 1,),
        in_specs=[tok(d, this), pl.BlockSpec((HALO, d), lambda i: (_halo_before(this(i), tm), 0)),
                  tok(d, before), tok(d, before),
                  once((d_ff, d), lambda i: (0, 0)), once((d_ff, d), lambda i: (1, 0)), once((d_ff, d), lambda i: (0, 0)),
                  pl.BlockSpec((3, d_ff), lambda i: (0, 0)), pl.BlockSpec((1, d_ff), lambda i: (0, 0)), vec, vec],
        out_specs=[tok(d_ff, this)] * 4 + [tok(d, before), tok(d, before), pl.BlockSpec((8, d), lambda i: (0, 0)),
                                          pl.BlockSpec((8, LANES), lambda i: (0, 0))],
        out_shape=[jax.ShapeDtypeStruct((s_len, d_ff), BF16)] * 4
        + [jax.ShapeDtypeStruct((s_len, d), BF16), jax.ShapeDtypeStruct((s_len, d), BF16),
           jax.ShapeDtypeStruct((8, d), F32), jax.ShapeDtypeStruct((8, LANES), F32)],
        scratch_shapes=[pltpu.VMEM((tm, d), F32)],
        compiler_params=_params(("arbitrary",)),
    )(h2, h2, x1, target, w_up_t, w_up_t, w_down, conv_w, conv_b, gt_f, g_post_ffn)


def _ffn_bwd_act(dy2, gate, a, act, vd, w_down, tm, tf, ck):
    s_len, d = dy2.shape
    d_ff = w_down.shape[0]
    n_t = s_len // tm
    chunks = [slice(lo, min(lo + ck, tf)) for lo in range(0, tf, ck)]

    def body(dy_ref, g_ref, gh_ref, a_ref, act_ref, vd_ref, wd_ref, dgc_ref, dval_ref, dwd_ref, dconv_ref, acc_ref):
        i = pl.program_id(1)

        @pl.when(i == 0)
        def _():
            acc_ref[...] = jnp.zeros_like(acc_ref)
            dconv_ref[...] = jnp.zeros_like(dconv_ref)

        dy = dy_ref[...]

        def down(cs):
            return _dot(dy, wd_ref[cs, :], NT)

        nxt = down(chunks[0])
        for c, cs in enumerate(chunks):
            width = cs.stop - cs.start
            da = nxt
            if c + 1 < len(chunks):
                nxt = down(chunks[c + 1])
            acc_ref[cs, :] += _dot(a_ref[:, cs], dy, TN)
            row = lax.broadcasted_iota(jnp.int32, (tm + HALO, width), 0)
            gate_ext = jnp.where((row < HALO) & (i == 0), 0.0,
                                 jnp.concatenate([gh_ref[:, cs], g_ref[:, cs]], axis=0).astype(F32))
            dgc = da * vd_ref[:, cs].astype(F32)
            dgc_ref[:, cs] = dgc.astype(BF16)
            dval_ref[:, cs] = (da * act_ref[:, cs].astype(F32)).astype(BF16)
            rows = [jnp.sum(dgc * pltpu.roll(gate_ext, 2 - k, 0)[HALO:], axis=0, keepdims=True) for k in range(2)]
            rows += [jnp.sum(dgc * gate_ext[HALO:], axis=0, keepdims=True), jnp.sum(dgc, axis=0, keepdims=True),
                     jnp.zeros((4, width), F32)]
            dconv_ref[:, cs] += jnp.concatenate(rows, axis=0)

        @pl.when(i == n_t - 1)
        def _():
            dwd_ref[...] = acc_ref[...].astype(BF16)

    tokf = pl.BlockSpec((tm, tf), lambda j, i: (i, j))
    return pl.pallas_call(
        body, name="ffn_bwd_act", grid=(d_ff // tf, n_t),
        in_specs=[pl.BlockSpec((tm, d), lambda j, i: (i, 0)), tokf,
                  pl.BlockSpec((HALO, tf), lambda j, i: (_halo_before(i, tm), j)), tokf, tokf, tokf,
                  pl.BlockSpec((tf, d), lambda j, i: (j, 0))],
        out_specs=[tokf, tokf, pl.BlockSpec((tf, d), lambda j, i: (j, 0)), pl.BlockSpec((8, tf), lambda j, i: (0, j))],
        out_shape=[jax.ShapeDtypeStruct((s_len, d_ff), BF16), jax.ShapeDtypeStruct((s_len, d_ff), BF16),
                   jax.ShapeDtypeStruct((d_ff, d), BF16), jax.ShapeDtypeStruct((8, d_ff), F32)],
        scratch_shapes=[pltpu.VMEM((tf, d), F32)],
        compiler_params=_params(("arbitrary", "arbitrary")),
    )(dy2, gate, gate, a, act, vd, w_down)


def _ffn_bwd_up(dgc, dval, w_up_t, conv_w, after, tm):
    s_len, d_ff = dgc.shape
    d = w_up_t.shape[1]
    n_t = s_len // tm

    def body(dg_ref, dgn_ref, dv_ref, cw_ref, w_ref, after_ref, dup_ref, dh_ref):
        i = pl.program_id(0)
        nxt = dgn_ref[...].astype(F32) * (i < n_t - 1).astype(F32)
        ext = jnp.concatenate([dg_ref[...].astype(F32), nxt], axis=0)
        rows = tm + HALO
        dgate = (ext * cw_ref[2:3, :] + pltpu.roll(ext, rows - 1, 0) * cw_ref[1:2, :]
                 + pltpu.roll(ext, rows - 2, 0) * cw_ref[0:1, :])[:tm]
        dup = jnp.concatenate([dgate.astype(BF16), dv_ref[...]], axis=1)
        dup_ref[...] = dup
        dh_ref[...] = _dot(dup, w_ref[...], NN).astype(BF16)

    tokf = pl.BlockSpec((tm, d_ff), lambda i: (i, 0))
    return pl.pallas_call(
        body, name="ffn_bwd_up", grid=(n_t,),
        in_specs=[tokf, pl.BlockSpec((HALO, d_ff), lambda i: (jnp.minimum((i + 1) * (tm // HALO), s_len // HALO - 1), 0)),
                  tokf, pl.BlockSpec((3, d_ff), lambda i: (0, 0)), pl.BlockSpec((2 * d_ff, d), lambda i: (0, 0)),
                  pl.BlockSpec(memory_space=pl.ANY)],
        out_specs=[pl.BlockSpec((tm, 2 * d_ff), lambda i: (i, 0)), pl.BlockSpec((tm, d), lambda i: (i, 0))],
        out_shape=[jax.ShapeDtypeStruct((s_len, 2 * d_ff), BF16), jax.ShapeDtypeStruct((s_len, d), BF16)],
        compiler_params=_params(("arbitrary",)),
    )(dgc, dgc, dval, conv_w, w_up_t, after)


def _mix_bwd(dh2, dout, x1, y1, cat, attn, w_out_t, sc_f, g_pre_ffn, gt_m, g_post_mix, after, tm):
    s_len, d = x1.shape
    n_t = s_len // tm

    def body(dh_ref, do_ref, x1_ref, y1_ref, cat_ref, at_ref, wo_ref, sc_ref, g2_ref, gt_ref, g1_ref, after_ref,
             dx1_ref, dpool_ref, dattn_ref, delta_ref, dwo_ref, sums_ref, acc_ref):
        i = pl.program_id(0)
        dh = dh_ref[...].astype(F32)
        x1 = x1_ref[...]
        r2 = _rstd(x1)
        n2 = x1 * r2
        ng = n2 * g2_ref[...]
        dng = dh * (1.0 + sc_ref[...])
        dx1 = do_ref[...].astype(F32) + _norm_bwd(dng * g2_ref[...], n2, r2)
        dx1_ref[...] = dx1.astype(BF16)
        y1 = y1_ref[...].astype(F32)
        r1 = _rstd(y1)
        n1 = y1 * r1
        drn = dx1 * gt_ref[...]
        dy1 = _norm_bwd(drn * g1_ref[...], n1, r1).astype(BF16)
        dcat = _dot(dy1, wo_ref[...], NN)
        dpool_ref[...] = dcat[:, 0:256]
        lane = lax.broadcasted_iota(jnp.int32, (tm, LANES), 1)
        first = lane < HEAD_DIM
        for s in range(2):
            da = dcat[:, 256 + s * LANES:256 + (s + 1) * LANES]
            dattn_ref[s] = da
            prod = da * at_ref[:, s * LANES:(s + 1) * LANES]
            tot = jnp.sum(prod, axis=-1, keepdims=True)
            lo = jnp.sum(jnp.where(first, prod, 0.0), axis=-1, keepdims=True)
            delta_ref[s] = jnp.where(first, lo, tot - lo)
        dwo = _dot(dy1, cat_ref[...], TN)
        sums = jnp.concatenate(
            [jnp.sum(dh, axis=0, keepdims=True), jnp.sum(dh * ng, axis=0, keepdims=True),
             jnp.sum(dng * n2, axis=0, keepdims=True), jnp.sum(dx1 * (n1 * g1_ref[...]), axis=0, keepdims=True),
             jnp.sum(drn * n1, axis=0, keepdims=True), jnp.zeros((3, d), F32)], axis=0)

        @pl.when(i == 0)
        def _():
            acc_ref[...] = dwo
            sums_ref[...] = sums

        @pl.when(i > 0)
        def _():
            acc_ref[...] += dwo
            sums_ref[...] += sums

        @pl.when(i == n_t - 1)
        def _():
            dwo_ref[...] = acc_ref[...].astype(BF16)

    tile = lambda w: pl.BlockSpec((tm, w), lambda i: (i, 0))
    slab = pl.BlockSpec((2, tm, LANES), lambda i: (0, i, 0))
    vec = pl.BlockSpec((1, d), lambda i: (0, 0))
    return pl.pallas_call(
        body, name="mix_bwd", grid=(n_t,),
        in_specs=[tile(d), tile(d), tile(d), tile(d), tile(512), tile(256),
                  pl.BlockSpec((d, 512), lambda i: (0, 0)), vec, vec, vec, vec, pl.BlockSpec(memory_space=pl.ANY)],
        out_specs=[tile(d), tile(256), slab, slab, pl.BlockSpec((d, 512), lambda i: (0, 0)),
                   pl.BlockSpec((8, d), lambda i: (0, 0))],
        out_shape=[jax.ShapeDtypeStruct((s_len, d), BF16), jax.ShapeDtypeStruct((s_len, 256), F32),
                   jax.ShapeDtypeStruct((2, s_len, LANES), F32), jax.ShapeDtypeStruct((2, s_len, LANES), F32),
                   jax.ShapeDtypeStruct((d, 512), BF16), jax.ShapeDtypeStruct((8, d), F32)],
        scratch_shapes=[pltpu.VMEM((d, 512), F32)],
        compiler_params=_params(("arbitrary",)),
    )(dh2, dout, x1, y1, cat, attn, w_out_t, sc_f, g_pre_ffn, gt_m, g_post_mix, after)


def _pool_bwd(dpool, u_pool, w_blk, b_pool, pool_scale, tm):
    s_len = dpool.shape[0]
    n_t = s_len // tm

    def body(dp_ref, dpn_ref, u_ref, uh_ref, wb_ref, bp_ref, ps_ref, du_ref, dwp_ref, sums_ref, acc_ref):
        i = pl.program_id(0)
        u = u_ref[...]
        mixed, _ = _pool_mixed(u, uh_ref[...] * (i > 0).astype(F32), i, tm)
        mixed_b = mixed.astype(BF16)
        y = _dot(mixed_b, wb_ref[...], NN) + bp_ref[...]
        dp = dp_ref[...]
        dy = dp * ps_ref[...]
        dwb = _dot(mixed_b, dy.astype(BF16), TN)
        sums = jnp.concatenate([jnp.sum(dy, axis=0, keepdims=True), jnp.sum(dp * y, axis=0, keepdims=True),
                                jnp.zeros((6, 256), F32)], axis=0)
        dp_ext = jnp.concatenate([dp, dpn_ref[...] * (i < n_t - 1).astype(F32)], axis=0)
        dmix = _dot((dp_ext * ps_ref[...]).astype(BF16), wb_ref[...], NT)
        rows = tm + HALO
        grp = lax.broadcasted_iota(jnp.int32, (rows, 256), 1) // HEAD_DIM
        pick = lambda a, b, c, e: jnp.where(grp == 0, a, jnp.where(grp == 1, b, jnp.where(grp == 2, c, e)))
        pos = (i * tm + lax.broadcasted_iota(jnp.int32, (rows, 256), 0)).astype(F32)
        z = dmix / jnp.minimum(pos + 1.0, pick(*[float(w) for w in POOL_WINDOWS]))
        f2 = z + pltpu.roll(z, rows - 1, 0)
        f4 = f2 + pltpu.roll(f2, rows - 2, 0)
        f8 = f4 + pltpu.roll(f4, rows - 4, 0)
        f16 = f8 + pltpu.roll(f8, rows - 8, 0)
        du_ref[...] = (pick(f2, f4, f8, f16) - dmix)[:tm]

        @pl.when(i == 0)
        def _():
            acc_ref[...] = dwb
            sums_ref[...] = sums

        @pl.when(i > 0)
        def _():
            acc_ref[...] += dwb
            sums_ref[...] += sums

        @pl.when(i == n_t - 1)
        def _():
            full = acc_ref[...]
            for gi in range(len(POOL_WINDOWS)):
                lo = gi * HEAD_DIM
                dwp_ref[gi] = full[lo:lo + HEAD_DIM, lo:lo + HEAD_DIM]

    n_g = len(POOL_WINDOWS)
    tile = pl.BlockSpec((tm, 256), lambda i: (i, 0))
    const = lambda a: pl.BlockSpec(a.shape, lambda i: (0,) * a.ndim)
    return pl.pallas_call(
        body, name="pool_bwd", grid=(n_t,),
        in_specs=[tile, pl.BlockSpec((HALO, 256), lambda i: (jnp.minimum((i + 1) * (tm // HALO), s_len // HALO - 1), 0)),
                  tile, pl.BlockSpec((HALO, 256), lambda i: (_halo_before(i, tm), 0)),
                  const(w_blk), const(b_pool), const(pool_scale)],
        out_specs=[tile, pl.BlockSpec((n_g, HEAD_DIM, HEAD_DIM), lambda i: (0, 0, 0)), pl.BlockSpec((8, 256), lambda i: (0, 0))],
        out_shape=[jax.ShapeDtypeStruct((s_len, 256), F32), jax.ShapeDtypeStruct((n_g, HEAD_DIM, HEAD_DIM), F32),
                   jax.ShapeDtypeStruct((8, 256), F32)],
        scratch_shapes=[pltpu.VMEM((256, 256), F32)],
        compiler_params=_params(("arbitrary",)),
    )(dpool, dpool, u_pool, u_pool, w_blk, b_pool, pool_scale)


def _attn_bwd(qkv, dattn, lse_all, delta, after):
    s_len = qkv.shape[1]
    n_g = len(DILATIONS)

    def body(q_ref, k_ref, v_ref, do_ref, l_ref, dl_ref, after_ref, dq_ref, dk_ref, dv_ref):
        lane = lax.broadcasted_iota(jnp.int32, (BLOCK, LANES), 1)
        first = lane < HEAD_DIM

        def group(dil):
            nb = s_len // (BLOCK * dil)

            def block(t, carry):
                dk_part, dv_part = carry
                r, n = t // nb, t % nb
                cur = _block_rows(n, r, dil)
                prev = _block_rows(jnp.maximum(n - 1, 0), r, dil)
                q = q_ref[0, cur, :]
                do = do_ref[0, cur, :]
                lse = l_ref[0, cur, :]
                dlt = dl_ref[0, cur, :]
                kcat = jnp.concatenate([k_ref[0, prev, :], k_ref[0, cur, :]], axis=0).astype(BF16)
                vcat = jnp.concatenate([v_ref[0, prev, :], v_ref[0, cur, :]], axis=0).astype(BF16)
                valid = _band_mask(n)
                stack = lambda a: jnp.concatenate([jnp.where(first, a, 0.0), jnp.where(first, 0.0, a)], axis=0)
                rows2 = lambda a: jnp.concatenate([a[:, 0:1], a[:, HEAD_DIM:HEAD_DIM + 1]], axis=0)
                q2, do2 = stack(q).astype(BF16), stack(do).astype(BF16)
                valid2 = jnp.concatenate([valid, valid], axis=0)
                p = jnp.where(valid2, jnp.exp(_dot(q2, kcat, NT) - rows2(lse)), 0.0)
                ds = (p * (_dot(do2, vcat, NT) - rows2(dlt))).astype(BF16)
                dq2 = _dot(ds, kcat, NN)
                dq_ref[0, 0, cur, :] = jnp.where(first, dq2[:BLOCK], dq2[BLOCK:])
                dkc = _dot(ds, q2, TN)
                dvc = _dot(p.astype(BF16), do2, TN)
                dk_ref[0, 0, prev, :] = dk_part + dkc[:BLOCK]
                dv_ref[0, 0, prev, :] = dv_part + dvc[:BLOCK]
                dk_ref[0, 0, cur, :] = dkc[BLOCK:]
                dv_ref[0, 0, cur, :] = dvc[BLOCK:]
                return dkc[BLOCK:], dvc[BLOCK:]

            def blocks(tt, carry):
                for u in range(ATTN_BWD_UNROLL):
                    carry = block(tt * ATTN_BWD_UNROLL + u, carry)
                return carry

            zero = jnp.zeros((BLOCK, LANES), F32)
            lax.fori_loop(0, nb * dil // ATTN_BWD_UNROLL, blocks, (zero, zero))

        for gi, dil in enumerate(DILATIONS):
            pl.when(pl.program_id(1) == gi)(functools.partial(group, dil))

    def slab(base):
        return pl.BlockSpec((1, s_len, LANES), lambda s, g: (base + 2 * g + s, 0, 0))

    one = pl.BlockSpec((1, s_len, LANES), lambda s, g: (s, 0, 0))
    out = pl.BlockSpec((1, 1, s_len, LANES), lambda s, g: (g, s, 0, 0))
    shape = jax.ShapeDtypeStruct((n_g, 2, s_len, LANES), F32)
    return pl.pallas_call(
        body, name="attn_bwd", grid=(2, n_g),
        in_specs=[slab(0), slab(6), slab(12), one, one, one, pl.BlockSpec(memory_space=pl.ANY)],
        out_specs=[out, out, out], out_shape=[shape, shape, shape],
        compiler_params=_params(("arbitrary", "arbitrary")),
    )(qkv, qkv, qkv, dattn, lse_all, delta, after)


def _dproj_wgrad_in(du, dqkv, rope, h1, tm, cm):
    s_len = du.shape[0]
    d = h1.shape[1]
    n_proj = 256 + 18 * LANES
    n_t = s_len // tm

    def body(du_ref, dq_ref, dk_ref, dv_ref, cs_ref, spread_ref, h_ref, dproj_ref, dw_ref, acc_ref):
        i = pl.program_id(0)

        @pl.when(i == 0)
        def _():
            acc_ref[...] = jnp.zeros_like(acc_ref)

        dproj_ref[:, 0:256] = du_ref[...].astype(BF16)
        lanes = _rope_lanes(cs_ref, spread_ref)
        col = 256
        for kind, dref in enumerate((dq_ref, dk_ref, dv_ref)):
            for grp in range(3):
                for s in range(2):
                    piece = dref[grp, s]
                    if kind < 2:
                        piece = _rope_bwd(piece, lanes)
                    if kind == 0:
                        piece = piece * (HEAD_DIM ** -0.5)
                    dproj_ref[:, col:col + LANES] = piece.astype(BF16)
                    col += LANES

        for c0 in range(0, n_proj, cm):
            acc_ref[c0:c0 + cm, :] += _dot(dproj_ref[:, c0:c0 + cm], h_ref[...], TN)

        @pl.when(i == n_t - 1)
        def _():
            dw_ref[...] = acc_ref[...].astype(BF16)

    groups = pl.BlockSpec((len(DILATIONS), 2, tm, LANES), lambda i: (0, 0, i, 0))
    return pl.pallas_call(
        body, name="dproj_wgrad_in", grid=(n_t,),
        in_specs=[pl.BlockSpec((tm, 256), lambda i: (i, 0))] + [groups] * 3
        + [pl.BlockSpec((rope[0].shape[0], tm), lambda i: (0, i)), pl.BlockSpec(rope[1].shape, lambda i: (0, 0, 0)),
           pl.BlockSpec((tm, d), lambda i: (i, 0))],
        out_specs=[pl.BlockSpec((tm, n_proj), lambda i: (i, 0)), pl.BlockSpec((n_proj, d), lambda i: (0, 0))],
        out_shape=[jax.ShapeDtypeStruct((s_len, n_proj), BF16), jax.ShapeDtypeStruct((n_proj, d), BF16)],
        scratch_shapes=[pltpu.VMEM((n_proj, d), F32)],
        compiler_params=_params(("arbitrary",)),
    )(du, *dqkv, *rope, h1)


def _inproj_bwd(dproj, w_in_t, x, dx1, sc_m, g_pre_mix, after, tm):
    s_len, d = x.shape
    n_proj = w_in_t.shape[0]
    n_t = s_len // tm

    def body(dproj_ref, w_ref, x_ref, dx1_ref, sc_ref, g_ref, after_ref, dx_ref, sums_ref):
        i = pl.program_id(0)
        halves = [slice(0, tm // 2), slice(tm // 2, tm)]
        dhs = [_dot(dproj_ref[rs, :], w_ref[...], NN) for rs in halves]
        sums = None
        for rs, dh in zip(halves, dhs):
            xv = x_ref[rs, :]
            r = _rstd(xv)
            n = xv * r
            dng = dh * (1.0 + sc_ref[...])
            dx_ref[rs, :] = dx1_ref[rs, :].astype(F32) + _norm_bwd(dng * g_ref[...], n, r)
            part = jnp.concatenate([jnp.sum(dh, axis=0, keepdims=True), jnp.sum(dh * (n * g_ref[...]), axis=0, keepdims=True),
                                    jnp.sum(dng * n, axis=0, keepdims=True), jnp.zeros((5, d), F32)], axis=0)
            sums = part if sums is None else sums + part

        @pl.when(i == 0)
        def _():
            sums_ref[...] = sums

        @pl.when(i > 0)
        def _():
            sums_ref[...] += sums

    tile = lambda w: pl.BlockSpec((tm, w), lambda i: (i, 0))
    vec = pl.BlockSpec((1, d), lambda i: (0, 0))
    return pl.pallas_call(
        body, name="inproj_bwd", grid=(n_t,),
        in_specs=[tile(n_proj), pl.BlockSpec((n_proj, d), lambda i: (0, 0)), tile(d), tile(d), vec, vec,
                  pl.BlockSpec(memory_space=pl.ANY)],
        out_specs=[tile(d), pl.BlockSpec((8, d), lambda i: (0, 0))],
        out_shape=[jax.ShapeDtypeStruct((s_len, d), F32), jax.ShapeDtypeStruct((8, d), F32)],
        compiler_params=_params(("arbitrary",)),
    )(dproj, w_in_t, x, dx1, sc_m, g_pre_mix, after)


def _wgrad(a, b, name, tk, tmm):
    s_len, m = a.shape
    n = b.shape[1]
    n_k = s_len // tk

    def body(a_ref, b_ref, o_ref, acc_ref):
        k = pl.program_id(1)
        part = _dot(a_ref[...], b_ref[...], TN)

        @pl.when(k == 0)
        def _():
            acc_ref[...] = part

        @pl.when(k > 0)
        def _():
            acc_ref[...] += part

        @pl.when(k == n_k - 1)
        def _():
            o_ref[...] = acc_ref[...].astype(BF16)

    return pl.pallas_call(
        body, name=name, grid=(m // tmm, n_k),
        in_specs=[pl.BlockSpec((tk, tmm), lambda j, k: (k, j)), pl.BlockSpec((tk, n), lambda j, k: (k, 0))],
        out_specs=pl.BlockSpec((tmm, n), lambda j, k: (j, 0)),
        out_shape=jax.ShapeDtypeStruct((m, n), BF16),
        scratch_shapes=[pltpu.VMEM((tmm, n), F32)],
        compiler_params=_params(("arbitrary", "arbitrary")),
    )(a, b)


def _place():
    return lax.axis_index("x"), lax.axis_index("y"), lax.axis_index("c")


def _peer(k):
    x, y, c = _place()
    bx, by, bc = (k >> 2) & 1, (k >> 1) & 1, k & 1
    return (x ^ bx if bx else x, y ^ by if by else y, c ^ bc if bc else c)


def _index(pos):
    return 4 * pos[0] + 2 * pos[1] + pos[2]


def _entry_exchange(c_rows, w_ada, b_ada, taps, shards, later):
    d = c_rows.shape[1]
    ncol = w_ada.shape[1]
    n_w, n_p = len(shards), len(later)

    def body(c_ref, w_ref, b_ref, t_ref, *rest):
        srcs, rest = rest[:n_w], rest[n_w:]
        later_refs, rest = rest[:n_p], rest[n_p:]
        (call_ref, mod_ref, tall_ref), rest = rest[:3], rest[3:]
        outs, rest = rest[:n_w], rest[n_w:]
        zones, rest = rest[:n_p], rest[n_p:]
        stage_ref, s_send, s_recv, w_send, w_recv, local_sems = rest[:6]
        wide, narrow, place_sems = rest[6:6 + n_p], rest[6 + n_p:6 + 2 * n_p], rest[6 + 2 * n_p]
        x, y, c = _place()
        here, sibling = (x, y, c), (x, y, 1 - c)
        chips = [(1 - x, y), (x, 1 - y), (1 - x, 1 - y)]
        me = _index(here)

        def small(kind, src, dst, k):
            return pltpu.make_async_remote_copy(src_ref=src, dst_ref=dst, send_sem=s_send.at[kind, k - 1],
                                                recv_sem=s_recv.at[kind, k - 1], device_id=_peer(k), device_id_type=MESH)

        gather = lambda k: small(0, c_ref, call_ref.at[me], k)
        scatter = lambda k: small(1, stage_ref.at[_index(_peer(k))], mod_ref.at[me], k)
        gather_taps = lambda k: small(2, t_ref, tall_ref.at[me], k)

        def rows(w, pos):
            r = shards[w].shape[0]
            return outs[w].at[pl.ds(pl.multiple_of(_index(pos) * r, 16), r), :]

        def block(k, w, pos, to, own=False):
            return pltpu.make_async_remote_copy(
                src_ref=srcs[w] if own else rows(w, pos), dst_ref=rows(w, pos),
                send_sem=w_send.at[k, w], recv_sem=w_recv.at[k, w], device_id=to, device_id_type=MESH)

        call_ref[me] = c_ref[...]
        tall_ref[me] = t_ref[...]
        for k in range(1, N_DEV):
            gather(k).start()
        for k in range(1, N_DEV):
            gather_taps(k).start()
        mine = [pltpu.make_async_copy(srcs[w], rows(w, here), local_sems.at[w]) for w in range(n_w)]
        for cp in mine:
            cp.start()
        first = [block(0, w, here, sibling, own=True) for w in range(n_w)]
        first += [block(1 + j, w, here, (*chip, c), own=True) for j, chip in enumerate(chips) for w in range(n_w)]
        for cp in first:
            cp.start()
        fetch = [pltpu.make_async_copy(later_refs[w], wide[w], place_sems.at[0, w]) for w in range(n_p)]
        for cp in fetch:
            cp.start()

        for k in range(1, N_DEV):
            gather(k).wait_recv()
        cv = jnp.concatenate([call_ref[b, 0:1, :] for b in range(N_DEV)], axis=0)
        act = cv * jax.nn.sigmoid(cv)
        mod = lax.dot_general(act, w_ref[...], NN, preferred_element_type=F32,
                              precision=lax.Precision.HIGHEST) + b_ref[:, pl.ds(pl.multiple_of(me * ncol, LANES), ncol)]
        for b in range(N_DEV):
            stage_ref[b] = jnp.broadcast_to(mod[b:b + 1, :], (8, ncol))
        mod_ref[me] = stage_ref[me]
        for k in range(1, N_DEV):
            scatter(k).start()

        placed = []
        for w in range(n_p):
            fetch[w].wait()
            narrow[w][...] = wide[w][...].astype(BF16)
            r = later[w].shape[0]
            placed.append(pltpu.make_async_copy(narrow[w], zones[w].at[pl.ds(pl.multiple_of(me * r, 16), r), :],
                                                place_sems.at[1, w]))
            placed[-1].start()

        passed = []
        for j, chip in enumerate(chips):
            for w in range(n_w):
                block(1 + j, w, (*chip, c), here).wait_recv()
                fwd = block(4 + j, w, (*chip, c), sibling)
                fwd.start()
                passed.append(fwd)
        for w in range(n_w):
            block(0, w, sibling, here).wait_recv()
        for j, chip in enumerate(chips):
            for w in range(n_w):
                block(4 + j, w, (*chip, 1 - c), here).wait_recv()
        for k in range(1, N_DEV):
            scatter(k).wait_recv()
            gather_taps(k).wait_recv()
        for cp in first + passed:
            cp.wait_send()
        for k in range(1, N_DEV):
            gather(k).wait_send()
            scatter(k).wait_send()
            gather_taps(k).wait_send()
        for cp in mine + placed:
            cp.wait()

    vmem, hbm = pl.BlockSpec(memory_space=pltpu.VMEM), pl.BlockSpec(memory_space=pltpu.HBM)
    out = pl.pallas_call(
        body, name="entry_exchange",
        in_specs=[vmem] * 4 + [hbm] * (n_w + n_p), out_specs=[vmem] * 3 + [hbm] * (n_w + n_p),
        out_shape=[jax.ShapeDtypeStruct((N_DEV, 8, d), F32), jax.ShapeDtypeStruct((N_DEV, 8, ncol), F32),
                   jax.ShapeDtypeStruct((N_DEV,) + taps.shape, F32)]
        + [jax.ShapeDtypeStruct((N_DEV * s.shape[0], s.shape[1]), s.dtype) for s in shards]
        + [jax.ShapeDtypeStruct((N_DEV * s.shape[0], s.shape[1]), BF16) for s in later],
        scratch_shapes=[pltpu.VMEM((N_DEV, 8, ncol), F32), pltpu.SemaphoreType.DMA((3, N_DEV - 1)),
                        pltpu.SemaphoreType.DMA((3, N_DEV - 1)), pltpu.SemaphoreType.DMA((N_DEV - 1, n_w)),
                        pltpu.SemaphoreType.DMA((N_DEV - 1, n_w)), pltpu.SemaphoreType.DMA((n_w,))]
        + [pltpu.VMEM(s.shape, F32) for s in later] + [pltpu.VMEM(s.shape, BF16) for s in later]
        + [pltpu.SemaphoreType.DMA((2, n_p))],
        compiler_params=_params(),
    )(c_rows, w_ada, b_ada, taps, *shards, *later)
    return out[0], out[1], out[2], out[3:3 + n_w], out[3 + n_w:]


def _peer_copies(mode, srcs, lands, send_sems, recv_sems):
    if mode in ("gather_ici", "gather_d2d"):
        x, y, c = _place()
        sibling = (x, y, 1 - c)
        chips = [(1 - x, y), (x, 1 - y), (1 - x, 1 - y)]
        n = len(lands)

        def rows(w, pos):
            r = lands[w].shape[0] // N_DEV
            return lands[w].at[pl.ds(pl.multiple_of(_index(pos) * r, 16), r), :]

        def copy(k, w, src, dst, to):
            return pltpu.make_async_remote_copy(src_ref=src, dst_ref=dst, send_sem=send_sems.at[k * n + w],
                                                recv_sem=recv_sems.at[k * n + w], device_id=to, device_id_type=MESH)

        if mode == "gather_ici":
            targets = [sibling] + [(*chip, c) for chip in chips]
            return [copy(k, w, rows(w, (x, y, c)), rows(w, (x, y, c)), to) for k, to in enumerate(targets) for w in range(n)]
        return [copy(j, w, rows(w, (*chip, c)), rows(w, (*chip, c)), sibling)
                for j, chip in enumerate(chips) for w in range(n)]
    me = _index(_place())
    modes = (mode,) * len(srcs) if isinstance(mode, str) else mode
    copies = []
    for k in range(1, N_DEV):
        peer = _peer(k)
        for w, (src, land) in enumerate(zip(srcs, lands)):
            if modes[w] == "gather":
                r = src.shape[0]
                dst = land.at[pl.ds(pl.multiple_of(me * r, 16), r), :]
            elif modes[w] == "allgather":
                dst = land.at[me]
            else:
                r = src.shape[0] // N_DEV
                src = src.at[pl.ds(pl.multiple_of(_index(peer) * r, 16), r), :]
                dst = land.at[me]
            copies.append(pltpu.make_async_remote_copy(
                src_ref=src, dst_ref=dst, send_sem=send_sems.at[(k - 1) * len(srcs) + w],
                recv_sem=recv_sems.at[(k - 1) * len(srcs) + w],
                device_id=peer, device_id_type=MESH))
    return copies


def _exchange_start(mode, srcs, lands, name):
    n_s, n_a = len(srcs), len(srcs) + len(lands)
    n_cp = _COPIES_PER_ARRAY.get(mode, N_DEV - 1) * len(lands)

    def body(*refs):
        for cp in _peer_copies(mode, refs[:n_s], refs[n_s:n_a], refs[n_a], refs[n_a + 1]):
            cp.start()

    hbm, sem = pl.BlockSpec(memory_space=pltpu.HBM), pl.BlockSpec(memory_space=pltpu.SEMAPHORE)
    arrays = list(srcs) + list(lands)
    out = pl.pallas_call(
        body, name=name,
        out_shape=(pltpu.SemaphoreType.DMA((n_cp,)), pltpu.SemaphoreType.DMA((n_cp,)),
                   *[pltpu.HBM(a.shape, a.dtype) for a in arrays]),
        in_specs=[hbm] * n_a, out_specs=(sem, sem, *[hbm] * n_a),
        input_output_aliases={i: 2 + i for i in range(n_a)},
        compiler_params=pltpu.CompilerParams(has_side_effects=pltpu.SideEffectType.DATAFLOW_SIDE_EFFECTING),
    )(*[pltpu.with_memory_space_constraint(a, pltpu.HBM) for a in arrays])
    return out[0], out[1], out[2:2 + n_s], out[2 + n_s:2 + n_a], out[2]


_COPIES_PER_ARRAY = {"gather_ici": 4, "gather_d2d": 3}


def _exchange_wait(mode, send_sems, recv_sems, srcs, lands, after, name):
    n_s, n_a = len(srcs), len(srcs) + len(lands)

    def body(*refs):
        copies = _peer_copies(mode, refs[:n_s], refs[n_s:n_a], refs[n_a], refs[n_a + 1])
        for cp in copies:
            cp.wait_send()
        for cp in copies:
            cp.wait_recv()

    hbm, sem = pl.BlockSpec(memory_space=pltpu.HBM), pl.BlockSpec(memory_space=pltpu.SEMAPHORE)
    arrays = list(srcs) + list(lands)
    out = pl.pallas_call(
        body, name=name, out_shape=tuple(pltpu.HBM(a.shape, a.dtype) for a in arrays),
        in_specs=[hbm] * n_a + [sem, sem, pl.BlockSpec(memory_space=pl.ANY)], out_specs=tuple([hbm] * n_a),
        input_output_aliases={i: i for i in range(n_a)},
        compiler_params=pltpu.CompilerParams(has_side_effects=pltpu.SideEffectType.DATAFLOW_SIDE_EFFECTING),
    )(*arrays, send_sems, recv_sems, after)
    return out[:n_s], out[n_s:]


SMALL_WEIGHTS = ("b_ada", "g_pre_mix", "g_post_mix", "g_pre_ffn", "g_post_ffn", "w_pool", "b_pool", "pool_scale", "conv_b")


MOD_ROWS = ((0, 0), (0, 1), (1, 3), (1, 0), (1, 1), (2, 0))


def _small_sum_adam(mine, gathered, weights, moms, vels):
    n_l, n_w = len(mine), len(weights)
    d = mine[0].shape[1]

    def body(*refs):
        loc, got = refs[:n_l], refs[n_l:2 * n_l]
        w_refs, m_refs, v_refs = (refs[2 * n_l + k * n_w:2 * n_l + (k + 1) * n_w] for k in range(3))
        outs = refs[2 * n_l + 3 * n_w:]
        dmod_ref, conv_ref, loss_ref = outs[4 * n_w:]
        me = _index(_place())
        part = lambda a, dev: jnp.where(dev == me, loc[a][...], got[a][dev])
        totals = []
        for a in range(n_l):
            tot = part(a, 0)
            for dev in range(1, N_DEV):
                tot = tot + part(a, dev)
            totals.append(tot)
        t_in, t_mix, t_ffn, t_pool, t_blk, t_conv, t_loss = totals
        conv_ref[...] = t_conv
        loss_ref[...] = t_loss
        for dev in range(N_DEV):
            for k, (a, r) in enumerate(MOD_ROWS):
                dmod_ref[dev:dev + 1, k * d:(k + 1) * d] = part(a, dev)[r:r + 1, :]

        def update(idx, g, at=()):
            sel = lambda ref: ref.at[at] if at else ref
            delta, nm, nv = _adam_math(sel(w_refs[idx])[...], g, sel(m_refs[idx])[...], sel(v_refs[idx])[...])
            for k, val in enumerate((g, delta, nm, nv)):
                sel(outs[4 * idx + k])[...] = val

        tots = (t_in, t_mix, t_ffn)
        update(0, jnp.concatenate([tots[a][r:r + 1] for a, r in MOD_ROWS], axis=1))
        update(1, t_in[2:3])
        update(2, t_mix[4:5])
        update(3, t_mix[2:3])
        update(4, t_ffn[1:2])
        for gi in range(len(POOL_WINDOWS)):
            update(5, t_blk[gi], at=(0, gi))
        update(6, jnp.concatenate([t_pool[0:1, gi * HEAD_DIM:(gi + 1) * HEAD_DIM] for gi in range(len(POOL_WINDOWS))], axis=0),
               at=(0,))
        update(7, t_pool[1:2])
        update(8, t_conv[3:4])

    vmem = pl.BlockSpec(memory_space=pltpu.VMEM)
    out = pl.pallas_call(
        body, name="small_sum_adam", in_specs=[vmem] * (2 * n_l + 3 * n_w), out_specs=[vmem] * (4 * n_w + 3),
        out_shape=[jax.ShapeDtypeStruct(w.shape, F32) for w in weights for _ in range(4)]
        + [jax.ShapeDtypeStruct((N_DEV, 6 * d), F32), jax.ShapeDtypeStruct(mine[5].shape, F32),
           jax.ShapeDtypeStruct(mine[6].shape, F32)],
        compiler_params=_params(),
    )(*mine, *gathered, *weights, *moms, *vels)
    return out[:4 * n_w], out[4 * n_w], out[4 * n_w + 1], out[4 * n_w + 2]


def _adam_math(w, g, m, v):
    m = ADAM_B1 * m + (1.0 - ADAM_B1) * g
    v = ADAM_B2 * v + (1.0 - ADAM_B2) * (g * g)
    m_hat = m / (1.0 - ADAM_B1 ** ADAM_STEP)
    v_hat = v / (1.0 - ADAM_B2 ** ADAM_STEP)
    delta = -ADAM_LR * (m_hat / (jnp.sqrt(v_hat) + ADAM_EPS) + ADAM_WD * w)
    return delta, m, v


def _adam(w, g, m, v, name):
    def body(w_ref, g_ref, m_ref, v_ref, d_ref, nm_ref, nv_ref):
        d_ref[...], nm_ref[...], nv_ref[...] = _adam_math(w_ref[...], g_ref[...], m_ref[...], v_ref[...])

    vmem = pl.BlockSpec(memory_space=pltpu.VMEM)
    return pl.pallas_call(
        body, name=name, in_specs=[vmem] * 4, out_specs=[vmem] * 3,
        out_shape=[jax.ShapeDtypeStruct(w.shape, F32)] * 3, compiler_params=_params(),
    )(w, g, m, v)


def _sum_adam(own, parts, w, m, v, me, name, tr):
    _, rows, cols = parts.shape
    turned = w.shape == (cols, rows) and rows != cols
    assert tr == rows or not turned
    n_t = rows // tr

    def body(me_ref, own_ref, p_ref, w_ref, m_ref, v_ref, g_ref, d_ref, nm_ref, nv_ref):
        part = lambda dev: jnp.where(dev == me_ref[0], own_ref[...], p_ref[dev]).astype(F32)
        g = part(0)
        for dev in range(1, N_DEV):
            g = g + part(dev)
        g = g.T if turned else g
        g_ref[...] = g
        d_ref[...], nm_ref[...], nv_ref[...] = _adam_math(w_ref[...], g, m_ref[...], v_ref[...])

    spec = pl.BlockSpec((cols, rows) if turned else (tr, cols), lambda i, me_ref: (i, 0))
    shape = jax.ShapeDtypeStruct(w.shape, F32)
    return pl.pallas_call(
        body, name=name, out_shape=[shape] * 4,
        grid_spec=pltpu.PrefetchScalarGridSpec(
            num_scalar_prefetch=1, grid=(n_t,),
            in_specs=[pl.BlockSpec((tr, cols), lambda i, me_ref: (me_ref[0] * n_t + i, 0)),
                      pl.BlockSpec((N_DEV, tr, cols), lambda i, me_ref: (0, i, 0)), spec, spec, spec],
            out_specs=[spec] * 4),
        compiler_params=_params(("arbitrary",)),
    )(me.reshape(1).astype(jnp.int32), own, parts, w, m, v)


def _ada_grad_adam(c_all, dmod_all, w, m, v, tr):
    rows, cols = w.shape

    def body(c_ref, dm_ref, w_ref, m_ref, v_ref, g_ref, d_ref, nm_ref, nv_ref):
        cv = c_ref[...]
        act = cv * jax.nn.sigmoid(cv)
        dmod = dm_ref[:, pl.ds(pl.multiple_of(_index(_place()) * cols, LANES), cols)]
        g = lax.dot_general(act, dmod, TN, preferred_element_type=F32, precision=lax.Precision.HIGHEST)
        g_ref[...] = g
        d_ref[...], nm_ref[...], nv_ref[...] = _adam_math(w_ref[...], g, m_ref[...], v_ref[...])

    spec = pl.BlockSpec((tr, cols), lambda i: (i, 0))
    shape = jax.ShapeDtypeStruct((rows, cols), F32)
    return pl.pallas_call(
        body, name="ada_grad_adam", grid=(rows // tr,),
        in_specs=[pl.BlockSpec((N_DEV, tr), lambda i: (0, i)), pl.BlockSpec(dmod_all.shape, lambda i: (0, 0)), spec, spec, spec],
        out_specs=[spec] * 4, out_shape=[shape] * 4, compiler_params=_params(("arbitrary",)),
    )(c_all, dmod_all, w, m, v)


def _rope_tables(positions):
    inv_freq = ROPE_THETA ** (-jnp.arange(0, 2 * ROT_HALF, 2, dtype=F32) / (2 * ROT_HALF))
    ang = inv_freq[:, None] * positions.astype(F32)[None, :]
    rows = jnp.concatenate([jnp.cos(ang), jnp.sin(ang), jnp.ones_like(ang)], axis=0)
    spread = [[[0.0] * LANES for _ in range(3 * ROT_HALF)] for _ in range(3)]
    for lane in range(LANES):
        p, j = lane % HEAD_DIM, lane % ROT_HALF
        if p < ROT_HALF:
            spread[0][j][lane] = 1.0
            spread[1][ROT_HALF + j][lane] = -1.0
        elif p < 2 * ROT_HALF:
            spread[0][j][lane] = 1.0
            spread[2][ROT_HALF + j][lane] = 1.0
        else:
            spread[0][2 * ROT_HALF][lane] = 1.0
    return rows, jnp.array(spread, F32)


def _pad_rows(a, rows):
    return jnp.pad(a, ((0, rows - a.shape[0]), (0, 0)))


def _sequence_step(xs, target, rope, mods, gains, w_in_t, w_out_t, relay_ffn, fetch_ffn, send_grads, w_blk_b, b_pool_r,
                   pool_scale_r, conv_w_all, conv_b, after):
    sh_m, sc_m, gt_m, sh_f, sc_f, gt_f = mods
    g_pre_mix, g_post_mix, g_pre_ffn, g_post_ffn = gains
    h1, u_pool, qkv = _premix_inproj(xs, sh_m, sc_m, g_pre_mix, w_in_t, rope, after, tm=512)
    o_g, lse_g = _attn_fwd(qkv)
    x1, y1, h2, cat, attn, lse_all = _mix_out(xs, u_pool, o_g, lse_g, w_blk_b, b_pool_r, pool_scale_r, w_out_t,
                                              gt_m, g_post_mix, g_pre_ffn, sc_f, sh_f, tm=256)
    relay_ffn(x1)
    w_up_t, w_down_f = fetch_ffn(x1)
    gate, a_ffn, act, vd, dy2, dout, sums_ffn, loss_loc = _ffn_fwd_loss(h2, x1, target, w_up_t, w_down_f, conv_w_all, conv_b,
                                                              gt_f, g_post_ffn, tm=256, ck=256)

    dgc, dval, dw_down, dconv = _ffn_bwd_act(dy2, gate, a_ffn, act, vd, w_down_f, tm=512, tf=1408, ck=256)
    token = send_grads("down", [dw_down], [])
    dup, dh2 = _ffn_bwd_up(dgc, dval, w_up_t, conv_w_all, token, tm=256)
    dw_up_t = _wgrad(dup, h2, "wgrad_up", tk=2048, tmm=1408)
    token = send_grads("up", [dw_up_t], [])
    dx1, dpool, dattn, delta, dw_out_t, sums_mix = _mix_bwd(dh2, dout, x1, y1, cat, attn, w_out_t, sc_f,
                                                           g_pre_ffn, gt_m, g_post_mix, token, tm=256)
    du, dw_blk, sums_pool = _pool_bwd(dpool, u_pool, w_blk_b, b_pool_r, pool_scale_r, tm=512)
    token = send_grads("out", [dw_out_t], [sums_mix, sums_ffn, sums_pool, dw_blk, dconv, loss_loc])
    dproj, dw_in_t = _dproj_wgrad_in(du, _attn_bwd(qkv, dattn, lse_all, delta, token), rope, h1, tm=512, cm=512)
    token = send_grads("in", [dw_in_t], [])
    grad_x, sums_in = _inproj_bwd(dproj, w_in_t, xs, dx1, sc_m, g_pre_mix, token, tm=256)
    return (loss_loc, grad_x, dw_in_t, dw_out_t, dw_up_t, dw_down, dw_blk, dconv,
            sums_in, sums_mix, sums_ffn, sums_pool)


def kernel(x, c, positions, w_ada, b_ada, g_pre_mix, g_post_mix, g_pre_ffn, g_post_ffn, w_in, w_pool, b_pool, pool_scale, w_out, w_up, conv_w, conv_b, w_down, loss_target, m_w_ada, m_b_ada, m_g_pre_mix, m_g_post_mix, m_g_pre_ffn, m_g_post_ffn, m_w_in, m_w_pool, m_b_pool, m_pool_scale, m_w_out, m_w_up, m_conv_w, m_conv_b, m_w_down, v_w_ada, v_b_ada, v_g_pre_mix, v_g_post_mix, v_g_pre_ffn, v_g_post_ffn, v_w_in, v_w_pool, v_b_pool, v_pool_scale, v_w_out, v_w_up, v_conv_w, v_conv_b, v_w_down):
    s_len, d = x.shape[1], x.shape[2]
    d_ff = w_down.shape[1] * N_DEV
    me = _index(_place())
    xs, target = x[0], loss_target[0]

    c_all, mod, taps_all, (w_in_t, w_out_t), lands = _entry_exchange(
        jnp.broadcast_to(c, (8, d)), w_ada[0], b_ada, _pad_rows(conv_w[0], 8),
        [w_in[0].T.astype(BF16), w_out[0].T.astype(BF16)], [w_up[0].T, w_down[0]])
    c_all = c_all[:, 0, :]
    conv_w_all = jnp.transpose(taps_all[:, :3, :], (1, 0, 2)).reshape(3, d_ff)
    sh_m, sc_m, gt_m, sh_f, sc_f, gt_f = [mod[:, 0, :].reshape(1, -1)[:, k * d:(k + 1) * d] for k in range(6)]

    rope = _rope_tables(positions[0])
    w_blk = jnp.zeros((256, 256), F32)
    for gi in range(4):
        w_blk = lax.dynamic_update_slice(w_blk, w_pool[0, gi], (gi * HEAD_DIM, gi * HEAD_DIM))
    w_blk_b = w_blk.astype(BF16)
    b_pool_r, pool_scale_r = b_pool.reshape(1, 256), pool_scale.reshape(1, 256)

    w_send, w_recv, w_src, w_land, w_token = _exchange_start("gather_ici", [], lands, "ffn_weights_ici_start")
    relay = []

    def relay_ffn(after):
        _, blocks = _exchange_wait("gather_ici", w_send, w_recv, w_src, w_land, after, "ffn_weights_ici_wait")
        relay.extend(_exchange_start("gather_d2d", [], blocks, "ffn_weights_d2d_start"))

    def fetch_ffn(after):
        return _exchange_wait("gather_d2d", relay[0], relay[1], [], relay[3], after, "ffn_weights_d2d_wait")[1]

    flights = {}

    def send_grads(tag, slabs, whole):
        lands = [lax.empty((N_DEV, g.shape[0] // N_DEV, g.shape[1]), g.dtype) for g in slabs]
        lands += [lax.empty((N_DEV,) + a.shape, F32) for a in whole]
        modes = ("scatter",) * len(slabs) + ("allgather",) * len(whole)
        flights[tag] = (modes, *_exchange_start(modes, slabs + whole, lands, f"grads_{tag}_start"))
        return flights[tag][5]

    def arrived(tag, after):
        return _exchange_wait(*flights[tag][:5], after, f"grads_{tag}_wait")

    _, grad_x, *_, sums_in, _, _, _ = _sequence_step(
        xs, target, rope, (sh_m, sc_m, gt_m, sh_f, sc_f, gt_f), (g_pre_mix, g_post_mix, g_pre_ffn, g_post_ffn),
        w_in_t, w_out_t, relay_ffn, fetch_ffn, send_grads, w_blk_b, b_pool_r, pool_scale_r, conv_w_all, conv_b,
        w_token)

    send_grads("last", [], [sums_in])

    (own_down,), (parts_down,) = arrived("down", flights["last"][5])
    new_down = _sum_adam(own_down, parts_down, w_down[0], m_w_down[0], v_w_down[0], me, "adam_w_down", 176)
    (own_up,), (parts_up,) = arrived("up", new_down[0])
    new_up = _sum_adam(own_up, parts_up, w_up[0].T, m_w_up[0].T, v_w_up[0].T, me, "adam_w_up", 352)
    (own_out, *small), (parts_out, *gathered) = arrived("out", new_up[0])
    new_out = _sum_adam(own_out, parts_out, w_out[0], m_w_out[0], v_w_out[0], me, "adam_w_out", 128)
    (own_in,), (parts_in,) = arrived("in", new_out[0])
    new_in = _sum_adam(own_in, parts_in, w_in[0].T, m_w_in[0].T, v_w_in[0].T, me, "adam_w_in", 160)
    big = {"w_up": [a.T for a in new_up], "w_down": new_down, "w_out": new_out, "w_in": [a.T for a in new_in]}

    rep_w = [b_ada, g_pre_mix, g_post_mix, g_pre_ffn, g_post_ffn, w_pool, b_pool, pool_scale, conv_b]
    rep_m = [m_b_ada, m_g_pre_mix, m_g_post_mix, m_g_pre_ffn, m_g_post_ffn, m_w_pool, m_b_pool, m_pool_scale, m_conv_b]
    rep_v = [v_b_ada, v_g_pre_mix, v_g_post_mix, v_g_pre_ffn, v_g_post_ffn, v_w_pool, v_b_pool, v_pool_scale, v_conv_b]
    mine_last, got_last = arrived("last", new_in[0])
    small, gathered = [*mine_last, *small], [*got_last, *gathered]
    rep_out, dmod_all, dconv_tot, loss_tot = _small_sum_adam(small, gathered, rep_w, rep_m, rep_v)
    g_rep, d_rep, nm_rep, nv_rep = (rep_out[k::4] for k in range(4))

    fcol = d_ff // N_DEV
    taps = lambda a: jnp.transpose(a, (1, 0, 2))
    g_cw = lax.dynamic_slice(dconv_tot, (0, me * fcol), (3, fcol))[None]
    d_cw, nm_cw, nv_cw = [taps(a) for a in _adam(taps(conv_w), taps(g_cw), taps(m_conv_w), taps(v_conv_w), "adam_conv_w")]

    g_ada, d_ada, nm_ada, nv_ada = _ada_grad_adam(c_all, dmod_all, w_ada[0], m_w_ada[0], v_w_ada[0], 256)

    loss = loss_tot[0, 0]

    def group(k):
        rep = (g_rep, d_rep, nm_rep, nv_rep)[k]
        ada = (g_ada, d_ada, nm_ada, nv_ada)[k][None]
        cw = (g_cw, d_cw, nm_cw, nv_cw)[k]
        return [ada, rep[0], rep[1], rep[2], rep[3], rep[4], big["w_in"][k][None], rep[5], rep[6], rep[7],
                big["w_out"][k][None], big["w_up"][k][None], cw, rep[8], big["w_down"][k][None]]

    return (loss, grad_x[None], *group(0), *group(1), *group(2), *group(3))
```

```python
import functools
import math

import jax
import jax.numpy as jnp
from jax import lax
from jax.experimental import pallas as pl
from jax.experimental.pallas import tpu as pltpu

F32 = jnp.float32
BF16 = jnp.bfloat16
MESH = pl.DeviceIdType.MESH

N_DEV = 8
HEAD_DIM = 64
ROT_HALF = 8
ROPE_THETA = 500000.0
POOL_WINDOWS = (2, 4, 8, 16)
DILATIONS = (1, 4, 16)
BLOCK = 128
NORM_EPS = 1e-6
HALO = 16
MASKED = -1e30
ATTN_FWD_UNROLL = 8
ATTN_BWD_UNROLL = 8

ADAM_LR = 0.001
ADAM_B1 = 0.9
ADAM_B2 = 0.999
ADAM_EPS = 1e-08
ADAM_WD = 0.01
ADAM_STEP = 10

V7X_VMEM_LIMIT = 56 * 1024 * 1024
LANES = 128

NT = (((1,), (1,)), ((), ()))
NN = (((1,), (0,)), ((), ()))
TN = (((0,), (0,)), ((), ()))


def _dot(a, b, dims):
    return lax.dot_general(a, b, dims, preferred_element_type=F32)


def _params(sem=None, vmem=V7X_VMEM_LIMIT):
    if sem is None:
        return pltpu.CompilerParams(vmem_limit_bytes=vmem)
    return pltpu.CompilerParams(dimension_semantics=sem, vmem_limit_bytes=vmem)


def _rstd(v):
    return lax.rsqrt(jnp.mean(v * v, axis=-1, keepdims=True) + NORM_EPS)


def _norm_bwd(dn, n, rstd):
    return rstd * (dn - n * jnp.mean(dn * n, axis=-1, keepdims=True))


def _rope_lanes(cs_ref, spread_ref):
    return [lax.dot_general(cs_ref[...], spread_ref[k], TN, preferred_element_type=F32, precision=lax.Precision.HIGHEST)
            for k in range(3)]


def _rope_fwd(p, lanes):
    return p * lanes[0] + pltpu.roll(p, LANES - ROT_HALF, 1) * lanes[1] + pltpu.roll(p, ROT_HALF, 1) * lanes[2]


def _rope_bwd(dp, lanes):
    return dp * lanes[0] + pltpu.roll(dp * lanes[1], ROT_HALF, 1) + pltpu.roll(dp * lanes[2], LANES - ROT_HALF, 1)


def _gelu_parts(v):
    k2 = 2.0 * math.sqrt(2.0 / math.pi)
    c = 0.044715
    v2 = v * v
    s = jax.nn.sigmoid(v * (k2 + (k2 * c) * v2))
    g = v * s
    dg = s + g * (1.0 - s) * (k2 + (3.0 * k2 * c) * v2)
    return g, dg


def _halo_before(i, tile):
    return jnp.maximum(i * (tile // HALO) - 1, 0)


def _premix_inproj(x, sh, sc, g, w_in_t, rope, after, tm):
    s_len, d = x.shape
    n_proj = w_in_t.shape[0]
    n_slab = (n_proj - 256) // LANES

    def body(x_ref, sh_ref, sc_ref, g_ref, w_ref, cs_ref, spread_ref, after_ref, h_ref, up_ref, qkv_ref):
        xv = x_ref[...]
        h = (xv * _rstd(xv) * g_ref[...]) * (1.0 + sc_ref[...]) + sh_ref[...]
        hb = h.astype(BF16)
        h_ref[...] = hb
        up_ref[...] = _dot(hb, w_ref[0:256, :], NT)
        lanes = _rope_lanes(cs_ref, spread_ref)
        for pair in range(n_slab // 2):
            p = _dot(hb, w_ref[256 + 256 * pair:512 + 256 * pair, :], NT)
            for half in range(2):
                ph = p[:, half * LANES:(half + 1) * LANES]
                if pair < 6:
                    ph = _rope_fwd(ph, lanes)
                if pair < 3:
                    ph = ph * (HEAD_DIM ** -0.5)
                qkv_ref[2 * pair + half] = ph

    vec = pl.BlockSpec((1, d), lambda i: (0, 0))
    return pl.pallas_call(
        body, name="premix_inproj", grid=(s_len // tm,),
        in_specs=[pl.BlockSpec((tm, d), lambda i: (i, 0)), vec, vec, vec,
                  pl.BlockSpec((n_proj, d), lambda i: (0, 0)),
                  pl.BlockSpec((rope[0].shape[0], tm), lambda i: (0, i)), pl.BlockSpec(rope[1].shape, lambda i: (0, 0, 0)),
                  pl.BlockSpec(memory_space=pl.ANY)],
        out_specs=[pl.BlockSpec((tm, d), lambda i: (i, 0)),
                   pl.BlockSpec((tm, 256), lambda i: (i, 0)),
                   pl.BlockSpec((n_slab, tm, LANES), lambda i: (0, i, 0))],
        out_shape=[jax.ShapeDtypeStruct((s_len, d), BF16),
                   jax.ShapeDtypeStruct((s_len, 256), F32),
                   jax.ShapeDtypeStruct((n_slab, s_len, LANES), F32)],
        compiler_params=_params(("arbitrary",)),
    )(x, sh, sc, g, w_in_t, *rope, after)


def _block_rows(n, r, dil):
    start = n * (BLOCK * dil) + r
    if dil == 1:
        return pl.ds(pl.multiple_of(start, BLOCK), BLOCK)
    return pl.ds(start, BLOCK, stride=dil)


def _band_mask(n):
    ri = lax.broadcasted_iota(jnp.int32, (BLOCK, 2 * BLOCK), 0)
    cj = lax.broadcasted_iota(jnp.int32, (BLOCK, 2 * BLOCK), 1)
    cur = (cj >= BLOCK) & (cj - BLOCK <= ri)
    prev = (cj < BLOCK) & (cj >= ri) & (n > 0)
    return cur | prev


def _attn_fwd(qkv):
    s_len = qkv.shape[1]
    n_g = len(DILATIONS)

    def body(q_ref, k_ref, v_ref, o_ref, lse_ref):
        lane = lax.broadcasted_iota(jnp.int32, (BLOCK, LANES), 1)
        first = lane < HEAD_DIM

        def group(dil):
            nb = s_len // (BLOCK * dil)

            def block(t, carry):
                r, n = t // nb, t % nb
                cur = _block_rows(n, r, dil)
                prev = _block_rows(jnp.maximum(n - 1, 0), r, dil)
                q = q_ref[0, cur, :]
                kcat = jnp.concatenate([k_ref[0, prev, :], k_ref[0, cur, :]], axis=0).astype(BF16)
                vcat = jnp.concatenate([v_ref[0, prev, :], v_ref[0, cur, :]], axis=0).astype(BF16)
                valid = _band_mask(n)
                q2 = jnp.concatenate([jnp.where(first, q, 0.0), jnp.where(first, 0.0, q)], axis=0).astype(BF16)
                s = jnp.where(jnp.concatenate([valid, valid], axis=0), _dot(q2, kcat, NT), MASKED)
                m = jnp.max(s, axis=-1, keepdims=True)
                p = jnp.exp(s - m)
                den = jnp.sum(p, axis=-1, keepdims=True)
                o2 = _dot(p.astype(BF16), vcat, NN) / den
                lse2 = m + jnp.log(den)
                o_ref[0, 0, cur, :] = jnp.where(first, o2[:BLOCK], o2[BLOCK:])
                lse_ref[0, 0, cur, :] = jnp.where(first, lse2[:BLOCK], lse2[BLOCK:])
                return carry

            lax.fori_loop(0, nb * dil, block, 0, unroll=ATTN_FWD_UNROLL)

        for gi, dil in enumerate(DILATIONS):
            pl.when(pl.program_id(0) == gi)(functools.partial(group, dil))

    def slab(base):
        return pl.BlockSpec((1, s_len, LANES), lambda g, s: (base + 2 * g + s, 0, 0))

    out = pl.BlockSpec((1, 1, s_len, LANES), lambda g, s: (g, s, 0, 0))
    shape = jax.ShapeDtypeStruct((n_g, 2, s_len, LANES), F32)
    return pl.pallas_call(
        body, name="attn_fwd", grid=(n_g, 2),
        in_specs=[slab(0), slab(6), slab(12)], out_specs=[out, out], out_shape=[shape, shape],
        compiler_params=_params(("arbitrary", "arbitrary")),
    )(qkv, qkv, qkv)


def _pool_mixed(u, halo, i, tm):
    ue = jnp.concatenate([halo, u], axis=0)
    s2 = ue + pltpu.roll(ue, 1, 0)
    s4 = s2 + pltpu.roll(s2, 2, 0)
    s8 = s4 + pltpu.roll(s4, 4, 0)
    s16 = s8 + pltpu.roll(s8, 8, 0)
    grp = lax.broadcasted_iota(jnp.int32, (tm, 256), 1) // HEAD_DIM
    pick = lambda a, b, c, e: jnp.where(grp == 0, a, jnp.where(grp == 1, b, jnp.where(grp == 2, c, e)))
    win_sum = pick(s2[HALO:], s4[HALO:], s8[HALO:], s16[HALO:])
    pos = (i * tm + lax.broadcasted_iota(jnp.int32, (tm, 256), 0)).astype(F32)
    count = jnp.minimum(pos + 1.0, pick(*[float(w) for w in POOL_WINDOWS]))
    return win_sum / count - u, count


def _mix_out(x, u_pool, o_g, lse_g, w_blk, b_pool, pool_scale, w_out_t, gt_m, g_post_mix, g_pre_ffn, sc_f, sh_f, tm):
    s_len, d = x.shape

    def body(x_ref, u_ref, uh_ref, o_ref, l_ref, wb_ref, bp_ref, ps_ref, wo_ref,
             gt_ref, g1_ref, g2_ref, sc_ref, sh_ref,
             x1_ref, y1_ref, h2_ref, cat_ref, attn_ref, lall_ref):
        (o0, o1, o2), (l0, l1, l2) = (o_ref.at[g] for g in range(3)), (l_ref.at[g] for g in range(3))
        i = pl.program_id(0)
        u = u_ref[...]
        halo = uh_ref[...] * (i > 0).astype(F32)
        mixed, _ = _pool_mixed(u, halo, i, tm)
        y = _dot(mixed.astype(BF16), wb_ref[...], NN) + bp_ref[...]
        pool = y * ps_ref[...]
        attn = []
        for s in range(2):
            la, lb, lc = l0[s], l1[s], l2[s]
            mx = jnp.maximum(jnp.maximum(la, lb), lc)
            ea, eb, ec = jnp.exp(la - mx), jnp.exp(lb - mx), jnp.exp(lc - mx)
            den = ea + eb + ec
            lall_ref[s] = mx + jnp.log(den)
            attn.append((ea / den) * o0[s] + (eb / den) * o1[s] + (ec / den) * o2[s])
        attn = jnp.concatenate(attn, axis=1)
        attn_ref[...] = attn
        cat = jnp.concatenate([pool, attn], axis=1).astype(BF16)
        cat_ref[...] = cat
        y1 = _dot(cat, wo_ref[...], NT)
        y1_ref[...] = y1.astype(BF16)
        x1 = x_ref[...] + gt_ref[...] * (y1 * _rstd(y1) * g1_ref[...])
        x1_ref[...] = x1
        h2 = (x1 * _rstd(x1) * g2_ref[...]) * (1.0 + sc_ref[...]) + sh_ref[...]
        h2_ref[...] = h2.astype(BF16)

    tile = lambda w: pl.BlockSpec((tm, w), lambda i: (i, 0))
    slab = pl.BlockSpec((2, tm, LANES), lambda i: (0, i, 0))
    groups = pl.BlockSpec((len(DILATIONS), 2, tm, LANES), lambda i: (0, 0, i, 0))
    const = lambda a: pl.BlockSpec(a.shape, lambda i: (0,) * a.ndim)
    return pl.pallas_call(
        body, name="mix_out", grid=(s_len // tm,),
        in_specs=[tile(d), tile(256), pl.BlockSpec((HALO, 256), lambda i: (_halo_before(i, tm), 0)),
                  groups, groups,
                  const(w_blk), const(b_pool), const(pool_scale), const(w_out_t),
                  const(gt_m), const(g_post_mix), const(g_pre_ffn), const(sc_f), const(sh_f)],
        out_specs=[tile(d), tile(d), tile(d), tile(512), tile(256), slab],
        out_shape=[jax.ShapeDtypeStruct((s_len, d), F32), jax.ShapeDtypeStruct((s_len, d), BF16),
                   jax.ShapeDtypeStruct((s_len, d), BF16), jax.ShapeDtypeStruct((s_len, 512), BF16),
                   jax.ShapeDtypeStruct((s_len, 256), F32), jax.ShapeDtypeStruct((2, s_len, LANES), F32)],
        compiler_params=_params(("arbitrary",)),
    )(x, u_pool, u_pool, o_g, lse_g, w_blk, b_pool, pool_scale, w_out_t, gt_m, g_post_mix, g_pre_ffn, sc_f, sh_f)


def _conv_gate(gate_ext, cw, cb):
    gc = gate_ext * cw[2:3, :] + pltpu.roll(gate_ext, 1, 0) * cw[1:2, :] + pltpu.roll(gate_ext, 2, 0) * cw[0:1, :]
    return gc[HALO:] + cb


def _ffn_fwd_loss(h2, x1, target, w_up_t, w_down, conv_w, conv_b, gt_f, g_post_ffn, tm, ck):
    s_len, d = x1.shape
    d_ff = w_down.shape[0]
    n_t, n_c = s_len // tm, d_ff // ck

    def body(h_ref, hh_ref, x1_ref, tgt_ref, wg_ref, wv_ref, wd_ref, cw_ref, cb_ref, gt_ref, g_ref,
             gate_ref, a_ref, act_ref, vd_ref, dy2_ref, dout_ref, sums_ref, loss_ref, acc_ref):
        i = pl.program_id(0)

        @pl.when(i == 0)
        def _():
            sums_ref[...] = jnp.zeros_like(sums_ref)
            loss_ref[...] = jnp.zeros_like(loss_ref)
            acc_ref[...] = jnp.zeros_like(acc_ref)

        def finish(live):
            y2 = acc_ref[...]
            rstd = _rstd(y2)
            n = y2 * rstd
            rn = n * g_ref[...]
            err = x1_ref[...] + gt_ref[...] * rn - tgt_ref[...]
            keep = lambda v: jnp.where(live, v, 0.0)
            loss_ref[...] += keep(0.5 * jnp.sum(jnp.mean(err * err, axis=-1, keepdims=True), axis=0, keepdims=True))
            dout = err * (1.0 / d)
            dout_ref[...] = dout.astype(BF16)
            drn = dout * gt_ref[...]
            sums_ref[0:1, :] += keep(jnp.sum(dout * rn, axis=0, keepdims=True))
            sums_ref[1:2, :] += keep(jnp.sum(drn * n, axis=0, keepdims=True))
            dy2_ref[...] = _norm_bwd(drn * g_ref[...], n, rstd).astype(BF16)

        @pl.when(i < n_t)
        def _():
            h = h_ref[...]
            h_ext = jnp.concatenate([hh_ref[...], h], axis=0)
            row = lax.broadcasted_iota(jnp.int32, (tm + HALO, ck), 0)
            no_halo = (row < HALO) & (i == 0)

            def up(c):
                cs = slice(c * ck, (c + 1) * ck)
                return jnp.where(no_halo, 0.0, _dot(h_ext, wg_ref[cs, :], NT)), _dot(h, wv_ref[cs, :], NT)

            part = None
            nxt = up(0)
            finish(i > 0)
            for c in range(n_c):
                cs = slice(c * ck, (c + 1) * ck)
                gate_ext, val = nxt
                if c + 1 < n_c:
                    nxt = up(c + 1)
                act, dact = _gelu_parts(_conv_gate(gate_ext, cw_ref[:, cs], cb_ref[:, cs]))
                a = (act * val).astype(BF16)
                gate_ref[:, cs] = gate_ext[HALO:].astype(BF16)
                a_ref[:, cs] = a
                act_ref[:, cs] = act.astype(BF16)
                vd_ref[:, cs] = (val * dact).astype(BF16)
                p = _dot(a, wd_ref[cs, :], NN)
                part = p if part is None else part + p
            acc_ref[...] = part

        @pl.when(i == n_t)
        def _():
            finish(True)

    this = lambda i: jnp.minimum(i, n_t - 1)
    before = lambda i: jnp.maximum(i - 1, 0)
    tok = lambda w, at: pl.BlockSpec((tm, w), lambda i: (at(i), 0))
    vec = pl.BlockSpec((1, d), lambda i: (0, 0))
    once = lambda shape, imap: pl.BlockSpec(shape, imap, pipeline_mode=pl.Buffered(1))
    return pl.pallas_call(
        body, name="ffn_fwd_loss", grid=(n_t + 1,),
        in_specs=[tok(d, this), pl.BlockSpec((HALO, d), lambda i: (_halo_before(this(i), tm), 0)),
                  tok(d, before), tok(d, before),
                  once((d_ff, d), lambda i: (0, 0)), once((d_ff, d), lambda i: (1, 0)), once((d_ff, d), lambda i: (0, 0)),
                  pl.BlockSpec((3, d_ff), lambda i: (0, 0)), pl.BlockSpec((1, d_ff), lambda i: (0, 0)), vec, vec],
        out_specs=[tok(d_ff, this)] * 4 + [tok(d, before), tok(d, before), pl.BlockSpec((8, d), lambda i: (0, 0)),
                                          pl.BlockSpec((8, LANES), lambda i: (0, 0))],
        out_shape=[jax.ShapeDtypeStruct((s_len, d_ff), BF16)] * 4
        + [jax.ShapeDtypeStruct((s_len, d), BF16), jax.ShapeDtypeStruct((s_len, d), BF16),
           jax.ShapeDtypeStruct((8, d), F32), jax.ShapeDtypeStruct((8, LANES), F32)],
        scratch_shapes=[pltpu.VMEM((tm, d), F32)],
        compiler_params=_params(("arbitrary",)),
    )(h2, h2, x1, target, w_up_t, w_up_t, w_down, conv_w, conv_b, gt_f, g_post_ffn)


def _ffn_bwd_act(dy2, gate, a, act, vd, w_down, tm, tf, ck):
    s_len, d = dy2.shape
    d_ff = w_down.shape[0]
    n_t = s_len // tm
    chunks = [slice(lo, min(lo + ck, tf)) for lo in range(0, tf, ck)]

    def body(dy_ref, g_ref, gh_ref, a_ref, act_ref, vd_ref, wd_ref, dgc_ref, dval_ref, dwd_ref, dconv_ref, acc_ref):
        i = pl.program_id(1)

        @pl.when(i == 0)
        def _():
            acc_ref[...] = jnp.zeros_like(acc_ref)
            dconv_ref[...] = jnp.zeros_like(dconv_ref)

        dy = dy_ref[...]

        def down(cs):
            return _dot(dy, wd_ref[cs, :], NT)

        nxt = down(chunks[0])
        for c, cs in enumerate(chunks):
            width = cs.stop - cs.start
            da = nxt
            if c + 1 < len(chunks):
                nxt = down(chunks[c + 1])
            acc_ref[cs, :] += _dot(a_ref[:, cs], dy, TN)
            row = lax.broadcasted_iota(jnp.int32, (tm + HALO, width), 0)
            gate_ext = jnp.where((row < HALO) & (i == 0), 0.0,
                                 jnp.concatenate([gh_ref[:, cs], g_ref[:, cs]], axis=0).astype(F32))
            dgc = da * vd_ref[:, cs].astype(F32)
            dgc_ref[:, cs] = dgc.astype(BF16)
            dval_ref[:, cs] = (da * act_ref[:, cs].astype(F32)).astype(BF16)
            rows = [jnp.sum(dgc * pltpu.roll(gate_ext, 2 - k, 0)[HALO:], axis=0, keepdims=True) for k in range(2)]
            rows += [jnp.sum(dgc * gate_ext[HALO:], axis=0, keepdims=True), jnp.sum(dgc, axis=0, keepdims=True),
                     jnp.zeros((4, width), F32)]
            dconv_ref[:, cs] += jnp.concatenate(rows, axis=0)

        @pl.when(i == n_t - 1)
        def _():
            dwd_ref[...] = acc_ref[...].astype(BF16)

    tokf = pl.BlockSpec((tm, tf), lambda j, i: (i, j))
    return pl.pallas_call(
        body, name="ffn_bwd_act", grid=(d_ff // tf, n_t),
        in_specs=[pl.BlockSpec((tm, d), lambda j, i: (i, 0)), tokf,
                  pl.BlockSpec((HALO, tf), lambda j, i: (_halo_before(i, tm), j)), tokf, tokf, tokf,
                  pl.BlockSpec((tf, d), lambda j, i: (j, 0))],
        out_specs=[tokf, tokf, pl.BlockSpec((tf, d), lambda j, i: (j, 0)), pl.BlockSpec((8, tf), lambda j, i: (0, j))],
        out_shape=[jax.ShapeDtypeStruct((s_len, d_ff), BF16), jax.ShapeDtypeStruct((s_len, d_ff), BF16),
                   jax.ShapeDtypeStruct((d_ff, d), BF16), jax.ShapeDtypeStruct((8, d_ff), F32)],
        scratch_shapes=[pltpu.VMEM((tf, d), F32)],
        compiler_params=_params(("arbitrary", "arbitrary")),
    )(dy2, gate, gate, a, act, vd, w_down)


def _ffn_bwd_up(dgc, dval, w_up_t, conv_w, after, tm):
    s_len, d_ff = dgc.shape
    d = w_up_t.shape[1]
    n_t = s_len // tm

    def body(dg_ref, dgn_ref, dv_ref, cw_ref, w_ref, after_ref, dup_ref, dh_ref):
        i = pl.program_id(0)
        nxt = dgn_ref[...].astype(F32) * (i < n_t - 1).astype(F32)
        ext = jnp.concatenate([dg_ref[...].astype(F32), nxt], axis=0)
        rows = tm + HALO
        dgate = (ext * cw_ref[2:3, :] + pltpu.roll(ext, rows - 1, 0) * cw_ref[1:2, :]
                 + pltpu.roll(ext, rows - 2, 0) * cw_ref[0:1, :])[:tm]
        dup = jnp.concatenate([dgate.astype(BF16), dv_ref[...]], axis=1)
        dup_ref[...] = dup
        dh_ref[...] = _dot(dup, w_ref[...], NN).astype(BF16)

    tokf = pl.BlockSpec((tm, d_ff), lambda i: (i, 0))
    return pl.pallas_call(
        body, name="ffn_bwd_up", grid=(n_t,),
        in_specs=[tokf, pl.BlockSpec((HALO, d_ff), lambda i: (jnp.minimum((i + 1) * (tm // HALO), s_len // HALO - 1), 0)),
                  tokf, pl.BlockSpec((3, d_ff), lambda i: (0, 0)), pl.BlockSpec((2 * d_ff, d), lambda i: (0, 0)),
                  pl.BlockSpec(memory_space=pl.ANY)],
        out_specs=[pl.BlockSpec((tm, 2 * d_ff), lambda i: (i, 0)), pl.BlockSpec((tm, d), lambda i: (i, 0))],
        out_shape=[jax.ShapeDtypeStruct((s_len, 2 * d_ff), BF16), jax.ShapeDtypeStruct((s_len, d), BF16)],
        compiler_params=_params(("arbitrary",)),
    )(dgc, dgc, dval, conv_w, w_up_t, after)


def _mix_bwd(dh2, dout, x1, y1, cat, attn, w_out_t, sc_f, g_pre_ffn, gt_m, g_post_mix, after, tm):
    s_len, d = x1.shape
    n_t = s_len // tm

    def body(dh_ref, do_ref, x1_ref, y1_ref, cat_ref, at_ref, wo_ref, sc_ref, g2_ref, gt_ref, g1_ref, after_ref,
             dx1_ref, dpool_ref, dattn_ref, delta_ref, dwo_ref, sums_ref, acc_ref):
        i = pl.program_id(0)
        dh = dh_ref[...].astype(F32)
        x1 = x1_ref[...]
        r2 = _rstd(x1)
        n2 = x1 * r2
        ng = n2 * g2_ref[...]
        dng = dh * (1.0 + sc_ref[...])
        dx1 = do_ref[...].astype(F32) + _norm_bwd(dng * g2_ref[...], n2, r2)
        dx1_ref[...] = dx1.astype(BF16)
        y1 = y1_ref[...].astype(F32)
        r1 = _rstd(y1)
        n1 = y1 * r1
        drn = dx1 * gt_ref[...]
        dy1 = _norm_bwd(drn * g1_ref[...], n1, r1).astype(BF16)
        dcat = _dot(dy1, wo_ref[...], NN)
        dpool_ref[...] = dcat[:, 0:256]
        lane = lax.broadcasted_iota(jnp.int32, (tm, LANES), 1)
        first = lane < HEAD_DIM
        for s in range(2):
            da = dcat[:, 256 + s * LANES:256 + (s + 1) * LANES]
            dattn_ref[s] = da
            prod = da * at_ref[:, s * LANES:(s + 1) * LANES]
            tot = jnp.sum(prod, axis=-1, keepdims=True)
            lo = jnp.sum(jnp.where(first, prod, 0.0), axis=-1, keepdims=True)
            delta_ref[s] = jnp.where(first, lo, tot - lo)
        dwo = _dot(dy1, cat_ref[...], TN)
        sums = jnp.concatenate(
            [jnp.sum(dh, axis=0, keepdims=True), jnp.sum(dh * ng, axis=0, keepdims=True),
             jnp.sum(dng * n2, axis=0, keepdims=True), jnp.sum(dx1 * (n1 * g1_ref[...]), axis=0, keepdims=True),
             jnp.sum(drn * n1, axis=0, keepdims=True), jnp.zeros((3, d), F32)], axis=0)

        @pl.when(i == 0)
        def _():
            acc_ref[...] = dwo
            sums_ref[...] = sums

        @pl.when(i > 0)
        def _():
            acc_ref[...] += dwo
            sums_ref[...] += sums

        @pl.when(i == n_t - 1)
        def _():
            dwo_ref[...] = acc_ref[...].astype(BF16)

    tile = lambda w: pl.BlockSpec((tm, w), lambda i: (i, 0))
    slab = pl.BlockSpec((2, tm, LANES), lambda i: (0, i, 0))
    vec = pl.BlockSpec((1, d), lambda i: (0, 0))
    return pl.pallas_call(
        body, name="mix_bwd", grid=(n_t,),
        in_specs=[tile(d), tile(d), tile(d), tile(d), tile(512), tile(256),
                  pl.BlockSpec((d, 512), lambda i: (0, 0)), vec, vec, vec, vec, pl.BlockSpec(memory_space=pl.ANY)],
        out_specs=[tile(d), tile(256), slab, slab, pl.BlockSpec((d, 512), lambda i: (0, 0)),
                   pl.BlockSpec((8, d), lambda i: (0, 0))],
        out_shape=[jax.ShapeDtypeStruct((s_len, d), BF16), jax.ShapeDtypeStruct((s_len, 256), F32),
                   jax.ShapeDtypeStruct((2, s_len, LANES), F32), jax.ShapeDtypeStruct((2, s_len, LANES), F32),
                   jax.ShapeDtypeStruct((d, 512), BF16), jax.ShapeDtypeStruct((8, d), F32)],
        scratch_shapes=[pltpu.VMEM((d, 512), F32)],
        compiler_params=_params(("arbitrary",)),
    )(dh2, dout, x1, y1, cat, attn, w_out_t, sc_f, g_pre_ffn, gt_m, g_post_mix, after)


def _pool_bwd(dpool, u_pool, w_blk, b_pool, pool_scale, tm):
    s_len = dpool.shape[0]
    n_t = s_len // tm

    def body(dp_ref, dpn_ref, u_ref, uh_ref, wb_ref, bp_ref, ps_ref, du_ref, dwp_ref, sums_ref, acc_ref):
        i = pl.program_id(0)
        u = u_ref[...]
        mixed, _ = _pool_mixed(u, uh_ref[...] * (i > 0).astype(F32), i, tm)
        mixed_b = mixed.astype(BF16)
        y = _dot(mixed_b, wb_ref[...], NN) + bp_ref[...]
        dp = dp_ref[...]
        dy = dp * ps_ref[...]
        dwb = _dot(mixed_b, dy.astype(BF16), TN)
        sums = jnp.concatenate([jnp.sum(dy, axis=0, keepdims=True), jnp.sum(dp * y, axis=0, keepdims=True),
                                jnp.zeros((6, 256), F32)], axis=0)
        dp_ext = jnp.concatenate([dp, dpn_ref[...] * (i < n_t - 1).astype(F32)], axis=0)
        dmix = _dot((dp_ext * ps_ref[...]).astype(BF16), wb_ref[...], NT)
        rows = tm + HALO
        grp = lax.broadcasted_iota(jnp.int32, (rows, 256), 1) // HEAD_DIM
        pick = lambda a, b, c, e: jnp.where(grp == 0, a, jnp.where(grp == 1, b, jnp.where(grp == 2, c, e)))
        pos = (i * tm + lax.broadcasted_iota(jnp.int32, (rows, 256), 0)).astype(F32)
        z = dmix / jnp.minimum(pos + 1.0, pick(*[float(w) for w in POOL_WINDOWS]))
        f2 = z + pltpu.roll(z, rows - 1, 0)
        f4 = f2 + pltpu.roll(f2, rows - 2, 0)
        f8 = f4 + pltpu.roll(f4, rows - 4, 0)
        f16 = f8 + pltpu.roll(f8, rows - 8, 0)
        du_ref[...] = (pick(f2, f4, f8, f16) - dmix)[:tm]

        @pl.when(i == 0)
        def _():
            acc_ref[...] = dwb
            sums_ref[...] = sums

        @pl.when(i > 0)
        def _():
            acc_ref[...] += dwb
            sums_ref[...] += sums

        @pl.when(i == n_t - 1)
        def _():
            full = acc_ref[...]
            for gi in range(len(POOL_WINDOWS)):
                lo = gi * HEAD_DIM
                dwp_ref[gi] = full[lo:lo + HEAD_DIM, lo:lo + HEAD_DIM]

    n_g = len(POOL_WINDOWS)
    tile = pl.BlockSpec((tm, 256), lambda i: (i, 0))
    const = lambda a: pl.BlockSpec(a.shape, lambda i: (0,) * a.ndim)
    return pl.pallas_call(
        body, name="pool_bwd", grid=(n_t,),
        in_specs=[tile, pl.BlockSpec((HALO, 256), lambda i: (jnp.minimum((i + 1) * (tm // HALO), s_len // HALO - 1), 0)),
                  tile, pl.BlockSpec((HALO, 256), lambda i: (_halo_before(i, tm), 0)),
                  const(w_blk), const(b_pool), const(pool_scale)],
        out_specs=[tile, pl.BlockSpec((n_g, HEAD_DIM, HEAD_DIM), lambda i: (0, 0, 0)), pl.BlockSpec((8, 256), lambda i: (0, 0))],
        out_shape=[jax.ShapeDtypeStruct((s_len, 256), F32), jax.ShapeDtypeStruct((n_g, HEAD_DIM, HEAD_DIM), F32),
                   jax.ShapeDtypeStruct((8, 256), F32)],
        scratch_shapes=[pltpu.VMEM((256, 256), F32)],
        compiler_params=_params(("arbitrary",)),
    )(dpool, dpool, u_pool, u_pool, w_blk, b_pool, pool_scale)


def _attn_bwd(qkv, dattn, lse_all, delta, after):
    s_len = qkv.shape[1]
    n_g = len(DILATIONS)

    def body(q_ref, k_ref, v_ref, do_ref, l_ref, dl_ref, after_ref, dq_ref, dk_ref, dv_ref):
        lane = lax.broadcasted_iota(jnp.int32, (BLOCK, LANES), 1)
        first = lane < HEAD_DIM

        def group(dil):
            nb = s_len // (BLOCK * dil)

            def block(t, carry):
                dk_part, dv_part = carry
                r, n = t // nb, t % nb
                cur = _block_rows(n, r, dil)
                prev = _block_rows(jnp.maximum(n - 1, 0), r, dil)
                q = q_ref[0, cur, :]
                do = do_ref[0, cur, :]
                lse = l_ref[0, cur, :]
                dlt = dl_ref[0, cur, :]
                kcat = jnp.concatenate([k_ref[0, prev, :], k_ref[0, cur, :]], axis=0).astype(BF16)
                vcat = jnp.concatenate([v_ref[0, prev, :], v_ref[0, cur, :]], axis=0).astype(BF16)
                valid = _band_mask(n)
                stack = lambda a: jnp.concatenate([jnp.where(first, a, 0.0), jnp.where(first, 0.0, a)], axis=0)
                rows2 = lambda a: jnp.concatenate([a[:, 0:1], a[:, HEAD_DIM:HEAD_DIM + 1]], axis=0)
                q2, do2 = stack(q).astype(BF16), stack(do).astype(BF16)
                valid2 = jnp.concatenate([valid, valid], axis=0)
                p = jnp.where(valid2, jnp.exp(_dot(q2, kcat, NT) - rows2(lse)), 0.0)
                ds = (p * (_dot(do2, vcat, NT) - rows2(dlt))).astype(BF16)
                dq2 = _dot(ds, kcat, NN)
                dq_ref[0, 0, cur, :] = jnp.where(first, dq2[:BLOCK], dq2[BLOCK:])
                dkc = _dot(ds, q2, TN)
                dvc = _dot(p.astype(BF16), do2, TN)
                dk_ref[0, 0, prev, :] = dk_part + dkc[:BLOCK]
                dv_ref[0, 0, prev, :] = dv_part + dvc[:BLOCK]
                dk_ref[0, 0, cur, :] = dkc[BLOCK:]
                dv_ref[0, 0, cur, :] = dvc[BLOCK:]
                return dkc[BLOCK:], dvc[BLOCK:]

            def blocks(tt, carry):
                for u in range(ATTN_BWD_UNROLL):
                    carry = block(tt * ATTN_BWD_UNROLL + u, carry)
                return carry

            zero = jnp.zeros((BLOCK, LANES), F32)
            lax.fori_loop(0, nb * dil // ATTN_BWD_UNROLL, blocks, (zero, zero))

        for gi, dil in enumerate(DILATIONS):
            pl.when(pl.program_id(1) == gi)(functools.partial(group, dil))

    def slab(base):
        return pl.BlockSpec((1, s_len, LANES), lambda s, g: (base + 2 * g + s, 0, 0))

    one = pl.BlockSpec((1, s_len, LANES), lambda s, g: (s, 0, 0))
    out = pl.BlockSpec((1, 1, s_len, LANES), lambda s, g: (g, s, 0, 0))
    shape = jax.ShapeDtypeStruct((n_g, 2, s_len, LANES), F32)
    return pl.pallas_call(
        body, name="attn_bwd", grid=(2, n_g),
        in_specs=[slab(0), slab(6), slab(12), one, one, one, pl.BlockSpec(memory_space=pl.ANY)],
        out_specs=[out, out, out], out_shape=[shape, shape, shape],
        compiler_params=_params(("arbitrary", "arbitrary")),
    )(qkv, qkv, qkv, dattn, lse_all, delta, after)


def _dproj_wgrad_in(du, dqkv, rope, h1, tm, cm):
    s_len = du.shape[0]
    d = h1.shape[1]
    n_proj = 256 + 18 * LANES
    n_t = s_len // tm

    def body(du_ref, dq_ref, dk_ref, dv_ref, cs_ref, spread_ref, h_ref, dproj_ref, dw_ref, acc_ref):
        i = pl.program_id(0)

        @pl.when(i == 0)
        def _():
            acc_ref[...] = jnp.zeros_like(acc_ref)

        dproj_ref[:, 0:256] = du_ref[...].astype(BF16)
        lanes = _rope_lanes(cs_ref, spread_ref)
        col = 256
        for kind, dref in enumerate((dq_ref, dk_ref, dv_ref)):
            for grp in range(3):
                for s in range(2):
                    piece = dref[grp, s]
                    if kind < 2:
                        piece = _rope_bwd(piece, lanes)
                    if kind == 0:
                        piece = piece * (HEAD_DIM ** -0.5)
                    dproj_ref[:, col:col + LANES] = piece.astype(BF16)
                    col += LANES

        for c0 in range(0, n_proj, cm):
            acc_ref[c0:c0 + cm, :] += _dot(dproj_ref[:, c0:c0 + cm], h_ref[...], TN)

        @pl.when(i == n_t - 1)
        def _():
            dw_ref[...] = acc_ref[...].astype(BF16)

    groups = pl.BlockSpec((len(DILATIONS), 2, tm, LANES), lambda i: (0, 0, i, 0))
    return pl.pallas_call(
        body, name="dproj_wgrad_in", grid=(n_t,),
        in_specs=[pl.BlockSpec((tm, 256), lambda i: (i, 0))] + [groups] * 3
        + [pl.BlockSpec((rope[0].shape[0], tm), lambda i: (0, i)), pl.BlockSpec(rope[1].shape, lambda i: (0, 0, 0)),
           pl.BlockSpec((tm, d), lambda i: (i, 0))],
        out_specs=[pl.BlockSpec((tm, n_proj), lambda i: (i, 0)), pl.BlockSpec((n_proj, d), lambda i: (0, 0))],
        out_shape=[jax.ShapeDtypeStruct((s_len, n_proj), BF16), jax.ShapeDtypeStruct((n_proj, d), BF16)],
        scratch_shapes=[pltpu.VMEM((n_proj, d), F32)],
        compiler_params=_params(("arbitrary",)),
    )(du, *dqkv, *rope, h1)


def _inproj_bwd(dproj, w_in_t, x, dx1, sc_m, g_pre_mix, after, tm):
    s_len, d = x.shape
    n_proj = w_in_t.shape[0]
    n_t = s_len // tm

    def body(dproj_ref, w_ref, x_ref, dx1_ref, sc_ref, g_ref, after_ref, dx_ref, sums_ref):
        i = pl.program_id(0)
        halves = [slice(0, tm // 2), slice(tm // 2, tm)]
        dhs = [_dot(dproj_ref[rs, :], w_ref[...], NN) for rs in halves]
        sums = None
        for rs, dh in zip(halves, dhs):
            xv = x_ref[rs, :]
            r = _rstd(xv)
            n = xv * r
            dng = dh * (1.0 + sc_ref[...])
            dx_ref[rs, :] = dx1_ref[rs, :].astype(F32) + _norm_bwd(dng * g_ref[...], n, r)
            part = jnp.concatenate([jnp.sum(dh, axis=0, keepdims=True), jnp.sum(dh * (n * g_ref[...]), axis=0, keepdims=True),
                                    jnp.sum(dng * n, axis=0, keepdims=True), jnp.zeros((5, d), F32)], axis=0)
            sums = part if sums is None else sums + part

        @pl.when(i == 0)
        def _():
            sums_ref[...] = sums

        @pl.when(i > 0)
        def _():
            sums_ref[...] += sums

    tile = lambda w: pl.BlockSpec((tm, w), lambda i: (i, 0))
    vec = pl.BlockSpec((1, d), lambda i: (0, 0))
    return pl.pallas_call(
        body, name="inproj_bwd", grid=(n_t,),
        in_specs=[tile(n_proj), pl.BlockSpec((n_proj, d), lambda i: (0, 0)), tile(d), tile(d), vec, vec,
                  pl.BlockSpec(memory_space=pl.ANY)],
        out_specs=[tile(d), pl.BlockSpec((8, d), lambda i: (0, 0))],
        out_shape=[jax.ShapeDtypeStruct((s_len, d), F32), jax.ShapeDtypeStruct((8, d), F32)],
        compiler_params=_params(("arbitrary",)),
    )(dproj, w_in_t, x, dx1, sc_m, g_pre_mix, after)


def _wgrad(a, b, name, tk, tmm):
    s_len, m = a.shape
    n = b.shape[1]
    n_k = s_len // tk

    def body(a_ref, b_ref, o_ref, acc_ref):
        k = pl.program_id(1)
        part = _dot(a_ref[...], b_ref[...], TN)

        @pl.when(k == 0)
        def _():
            acc_ref[...] = part

        @pl.when(k > 0)
        def _():
            acc_ref[...] += part

        @pl.when(k == n_k - 1)
        def _():
            o_ref[...] = acc_ref[...].astype(BF16)

    return pl.pallas_call(
        body, name=name, grid=(m // tmm, n_k),
        in_specs=[pl.BlockSpec((tk, tmm), lambda j, k: (k, j)), pl.BlockSpec((tk, n), lambda j, k: (k, 0))],
        out_specs=pl.BlockSpec((tmm, n), lambda j, k: (j, 0)),
        out_shape=jax.ShapeDtypeStruct((m, n), BF16),
        scratch_shapes=[pltpu.VMEM((tmm, n), F32)],
        compiler_params=_params(("arbitrary", "arbitrary")),
    )(a, b)


def _place():
    return lax.axis_index("x"), lax.axis_index("y"), lax.axis_index("c")


def _peer(k):
    x, y, c = _place()
    bx, by, bc = (k >> 2) & 1, (k >> 1) & 1, k & 1
    return (x ^ bx if bx else x, y ^ by if by else y, c ^ bc if bc else c)


def _index(pos):
    return 4 * pos[0] + 2 * pos[1] + pos[2]


def _entry_exchange(c_rows, w_ada, b_ada, taps, shards, later):
    d = c_rows.shape[1]
    ncol = w_ada.shape[1]
    n_w, n_p = len(shards), len(later)

    def body(c_ref, w_ref, b_ref, t_ref, *rest):
        srcs, rest = rest[:n_w], rest[n_w:]
        later_refs, rest = rest[:n_p], rest[n_p:]
        (call_ref, mod_ref, tall_ref), rest = rest[:3], rest[3:]
        outs, rest = rest[:n_w], rest[n_w:]
        zones, rest = rest[:n_p], rest[n_p:]
        stage_ref, s_send, s_recv, w_send, w_recv, local_sems = rest[:6]
        wide, narrow, place_sems = rest[6:6 + n_p], rest[6 + n_p:6 + 2 * n_p], rest[6 + 2 * n_p]
        x, y, c = _place()
        here, sibling = (x, y, c), (x, y, 1 - c)
        chips = [(1 - x, y), (x, 1 - y), (1 - x, 1 - y)]
        me = _index(here)

        def small(kind, src, dst, k):
            return pltpu.make_async_remote_copy(src_ref=src, dst_ref=dst, send_sem=s_send.at[kind, k - 1],
                                                recv_sem=s_recv.at[kind, k - 1], device_id=_peer(k), device_id_type=MESH)

        gather = lambda k: small(0, c_ref, call_ref.at[me], k)
        scatter = lambda k: small(1, stage_ref.at[_index(_peer(k))], mod_ref.at[me], k)
        gather_taps = lambda k: small(2, t_ref, tall_ref.at[me], k)

        def rows(w, pos):
            r = shards[w].shape[0]
            return outs[w].at[pl.ds(pl.multiple_of(_index(pos) * r, 16), r), :]

        def block(k, w, pos, to, own=False):
            return pltpu.make_async_remote_copy(
                src_ref=srcs[w] if own else rows(w, pos), dst_ref=rows(w, pos),
                send_sem=w_send.at[k, w], recv_sem=w_recv.at[k, w], device_id=to, device_id_type=MESH)

        call_ref[me] = c_ref[...]
        tall_ref[me] = t_ref[...]
        for k in range(1, N_DEV):
            gather(k).start()
        for k in range(1, N_DEV):
            gather_taps(k).start()
        mine = [pltpu.make_async_copy(srcs[w], rows(w, here), local_sems.at[w]) for w in range(n_w)]
        for cp in mine:
            cp.start()
        first = [block(0, w, here, sibling, own=True) for w in range(n_w)]
        first += [block(1 + j, w, here, (*chip, c), own=True) for j, chip in enumerate(chips) for w in range(n_w)]
        for cp in first:
            cp.start()
        fetch = [pltpu.make_async_copy(later_refs[w], wide[w], place_sems.at[0, w]) for w in range(n_p)]
        for cp in fetch:
            cp.start()

        for k in range(1, N_DEV):
            gather(k).wait_recv()
        cv = jnp.concatenate([call_ref[b, 0:1, :] for b in range(N_DEV)], axis=0)
        act = cv * jax.nn.sigmoid(cv)
        mod = lax.dot_general(act, w_ref[...], NN, preferred_element_type=F32,
                              precision=lax.Precision.HIGHEST) + b_ref[:, pl.ds(pl.multiple_of(me * ncol, LANES), ncol)]
        for b in range(N_DEV):
            stage_ref[b] = jnp.broadcast_to(mod[b:b + 1, :], (8, ncol))
        mod_ref[me] = stage_ref[me]
        for k in range(1, N_DEV):
            scatter(k).start()

        placed = []
        for w in range(n_p):
            fetch[w].wait()
            narrow[w][...] = wide[w][...].astype(BF16)
            r = later[w].shape[0]
            placed.append(pltpu.make_async_copy(narrow[w], zones[w].at[pl.ds(pl.multiple_of(me * r, 16), r), :],
                                                place_sems.at[1, w]))
            placed[-1].start()

        passed = []
        for j, chip in enumerate(chips):
            for w in range(n_w):
                block(1 + j, w, (*chip, c), here).wait_recv()
                fwd = block(4 + j, w, (*chip, c), sibling)
                fwd.start()
                passed.append(fwd)
        for w in range(n_w):
            block(0, w, sibling, here).wait_recv()
        for j, chip in enumerate(chips):
            for w in range(n_w):
                block(4 + j, w, (*chip, 1 - c), here).wait_recv()
        for k in range(1, N_DEV):
            scatter(k).wait_recv()
            gather_taps(k).wait_recv()
        for cp in first + passed:
            cp.wait_send()
        for k in range(1, N_DEV):
            gather(k).wait_send()
            scatter(k).wait_send()
            gather_taps(k).wait_send()
        for cp in mine + placed:
            cp.wait()

    vmem, hbm = pl.BlockSpec(memory_space=pltpu.VMEM), pl.BlockSpec(memory_space=pltpu.HBM)
    out = pl.pallas_call(
        body, name="entry_exchange",
        in_specs=[vmem] * 4 + [hbm] * (n_w + n_p), out_specs=[vmem] * 3 + [hbm] * (n_w + n_p),
        out_shape=[jax.ShapeDtypeStruct((N_DEV, 8, d), F32), jax.ShapeDtypeStruct((N_DEV, 8, ncol), F32),
                   jax.ShapeDtypeStruct((N_DEV,) + taps.shape, F32)]
        + [jax.ShapeDtypeStruct((N_DEV * s.shape[0], s.shape[1]), s.dtype) for s in shards]
        + [jax.ShapeDtypeStruct((N_DEV * s.shape[0], s.shape[1]), BF16) for s in later],
        scratch_shapes=[pltpu.VMEM((N_DEV, 8, ncol), F32), pltpu.SemaphoreType.DMA((3, N_DEV - 1)),
                        pltpu.SemaphoreType.DMA((3, N_DEV - 1)), pltpu.SemaphoreType.DMA((N_DEV - 1, n_w)),
                        pltpu.SemaphoreType.DMA((N_DEV - 1, n_w)), pltpu.SemaphoreType.DMA((n_w,))]
        + [pltpu.VMEM(s.shape, F32) for s in later] + [pltpu.VMEM(s.shape, BF16) for s in later]
        + [pltpu.SemaphoreType.DMA((2, n_p))],
        compiler_params=_params(),
    )(c_rows, w_ada, b_ada, taps, *shards, *later)
    return out[0], out[1], out[2], out[3:3 + n_w], out[3 + n_w:]


def _peer_copies(mode, srcs, lands, send_sems, recv_sems):
    if mode in ("gather_ici", "gather_d2d"):
        x, y, c = _place()
        sibling = (x, y, 1 - c)
        chips = [(1 - x, y), (x, 1 - y), (1 - x, 1 - y)]
        n = len(lands)

        def rows(w, pos):
            r = lands[w].shape[0] // N_DEV
            return lands[w].at[pl.ds(pl.multiple_of(_index(pos) * r, 16), r), :]

        def copy(k, w, src, dst, to):
            return pltpu.make_async_remote_copy(src_ref=src, dst_ref=dst, send_sem=send_sems.at[k * n + w],
                                                recv_sem=recv_sems.at[k * n + w], device_id=to, device_id_type=MESH)

        if mode == "gather_ici":
            targets = [sibling] + [(*chip, c) for chip in chips]
            return [copy(k, w, rows(w, (x, y, c)), rows(w, (x, y, c)), to) for k, to in enumerate(targets) for w in range(n)]
        return [copy(j, w, rows(w, (*chip, c)), rows(w, (*chip, c)), sibling)
                for j, chip in enumerate(chips) for w in range(n)]
    me = _index(_place())
    modes = (mode,) * len(srcs) if isinstance(mode, str) else mode
    copies = []
    for k in range(1, N_DEV):
        peer = _peer(k)
        for w, (src, land) in enumerate(zip(srcs, lands)):
            if modes[w] == "gather":
                r = src.shape[0]
                dst = land.at[pl.ds(pl.multiple_of(me * r, 16), r), :]
            elif modes[w] == "allgather":
                dst = land.at[me]
            else:
                r = src.shape[0] // N_DEV
                src = src.at[pl.ds(pl.multiple_of(_index(peer) * r, 16), r), :]
                dst = land.at[me]
            copies.append(pltpu.make_async_remote_copy(
                src_ref=src, dst_ref=dst, send_sem=send_sems.at[(k - 1) * len(srcs) + w],
                recv_sem=recv_sems.at[(k - 1) * len(srcs) + w],
                device_id=peer, device_id_type=MESH))
    return copies


def _exchange_start(mode, srcs, lands, name):
    n_s, n_a = len(srcs), len(srcs) + len(lands)
    n_cp = _COPIES_PER_ARRAY.get(mode, N_DEV - 1) * len(lands)

    def body(*refs):
        for cp in _peer_copies(mode, refs[:n_s], refs[n_s:n_a], refs[n_a], refs[n_a + 1]):
            cp.start()

    hbm, sem = pl.BlockSpec(memory_space=pltpu.HBM), pl.BlockSpec(memory_space=pltpu.SEMAPHORE)
    arrays = list(srcs) + list(lands)
    out = pl.pallas_call(
        body, name=name,
        out_shape=(pltpu.SemaphoreType.DMA((n_cp,)), pltpu.SemaphoreType.DMA((n_cp,)),
                   *[pltpu.HBM(a.shape, a.dtype) for a in arrays]),
        in_specs=[hbm] * n_a, out_specs=(sem, sem, *[hbm] * n_a),
        input_output_aliases={i: 2 + i for i in range(n_a)},
        compiler_params=pltpu.CompilerParams(has_side_effects=pltpu.SideEffectType.DATAFLOW_SIDE_EFFECTING),
    )(*[pltpu.with_memory_space_constraint(a, pltpu.HBM) for a in arrays])
    return out[0], out[1], out[2:2 + n_s], out[2 + n_s:2 + n_a], out[2]


_COPIES_PER_ARRAY = {"gather_ici": 4, "gather_d2d": 3}


def _exchange_wait(mode, send_sems, recv_sems, srcs, lands, after, name):
    n_s, n_a = len(srcs), len(srcs) + len(lands)

    def body(*refs):
        copies = _peer_copies(mode, refs[:n_s], refs[n_s:n_a], refs[n_a], refs[n_a + 1])
        for cp in copies:
            cp.wait_send()
        for cp in copies:
            cp.wait_recv()

    hbm, sem = pl.BlockSpec(memory_space=pltpu.HBM), pl.BlockSpec(memory_space=pltpu.SEMAPHORE)
    arrays = list(srcs) + list(lands)
    out = pl.pallas_call(
        body, name=name, out_shape=tuple(pltpu.HBM(a.shape, a.dtype) for a in arrays),
        in_specs=[hbm] * n_a + [sem, sem, pl.BlockSpec(memory_space=pl.ANY)], out_specs=tuple([hbm] * n_a),
        input_output_aliases={i: i for i in range(n_a)},
        compiler_params=pltpu.CompilerParams(has_side_effects=pltpu.SideEffectType.DATAFLOW_SIDE_EFFECTING),
    )(*arrays, send_sems, recv_sems, after)
    return out[:n_s], out[n_s:]


SMALL_WEIGHTS = ("b_ada", "g_pre_mix", "g_post_mix", "g_pre_ffn", "g_post_ffn", "w_pool", "b_pool", "pool_scale", "conv_b")


MOD_ROWS = ((0, 0), (0, 1), (1, 3), (1, 0), (1, 1), (2, 0))


def _small_sum_adam(mine, gathered, weights, moms, vels):
    n_l, n_w = len(mine), len(weights)
    d = mine[0].shape[1]

    def body(*refs):
        loc, got = refs[:n_l], refs[n_l:2 * n_l]
        w_refs, m_refs, v_refs = (refs[2 * n_l + k * n_w:2 * n_l + (k + 1) * n_w] for k in range(3))
        outs = refs[2 * n_l + 3 * n_w:]
        dmod_ref, conv_ref, loss_ref = outs[4 * n_w:]
        me = _index(_place())
        part = lambda a, dev: jnp.where(dev == me, loc[a][...], got[a][dev])
        totals = []
        for a in range(n_l):
            tot = part(a, 0)
            for dev in range(1, N_DEV):
                tot = tot + part(a, dev)
            totals.append(tot)
        t_in, t_mix, t_ffn, t_pool, t_blk, t_conv, t_loss = totals
        conv_ref[...] = t_conv
        loss_ref[...] = t_loss
        for dev in range(N_DEV):
            for k, (a, r) in enumerate(MOD_ROWS):
                dmod_ref[dev:dev + 1, k * d:(k + 1) * d] = part(a, dev)[r:r + 1, :]

        def update(idx, g, at=()):
            sel = lambda ref: ref.at[at] if at else ref
            delta, nm, nv = _adam_math(sel(w_refs[idx])[...], g, sel(m_refs[idx])[...], sel(v_refs[idx])[...])
            for k, val in enumerate((g, delta, nm, nv)):
                sel(outs[4 * idx + k])[...] = val

        tots = (t_in, t_mix, t_ffn)
        update(0, jnp.concatenate([tots[a][r:r + 1] for a, r in MOD_ROWS], axis=1))
        update(1, t_in[2:3])
        update(2, t_mix[4:5])
        update(3, t_mix[2:3])
        update(4, t_ffn[1:2])
        for gi in range(len(POOL_WINDOWS)):
            update(5, t_blk[gi], at=(0, gi))
        update(6, jnp.concatenate([t_pool[0:1, gi * HEAD_DIM:(gi + 1) * HEAD_DIM] for gi in range(len(POOL_WINDOWS))], axis=0),
               at=(0,))
        update(7, t_pool[1:2])
        update(8, t_conv[3:4])

    vmem = pl.BlockSpec(memory_space=pltpu.VMEM)
    out = pl.pallas_call(
        body, name="small_sum_adam", in_specs=[vmem] * (2 * n_l + 3 * n_w), out_specs=[vmem] * (4 * n_w + 3),
        out_shape=[jax.ShapeDtypeStruct(w.shape, F32) for w in weights for _ in range(4)]
        + [jax.ShapeDtypeStruct((N_DEV, 6 * d), F32), jax.ShapeDtypeStruct(mine[5].shape, F32),
           jax.ShapeDtypeStruct(mine[6].shape, F32)],
        compiler_params=_params(),
    )(*mine, *gathered, *weights, *moms, *vels)
    return out[:4 * n_w], out[4 * n_w], out[4 * n_w + 1], out[4 * n_w + 2]


def _adam_math(w, g, m, v):
    m = ADAM_B1 * m + (1.0 - ADAM_B1) * g
    v = ADAM_B2 * v + (1.0 - ADAM_B2) * (g * g)
    m_hat = m / (1.0 - ADAM_B1 ** ADAM_STEP)
    v_hat = v / (1.0 - ADAM_B2 ** ADAM_STEP)
    delta = -ADAM_LR * (m_hat / (jnp.sqrt(v_hat) + ADAM_EPS) + ADAM_WD * w)
    return delta, m, v


def _adam(w, g, m, v, name):
    def body(w_ref, g_ref, m_ref, v_ref, d_ref, nm_ref, nv_ref):
        d_ref[...], nm_ref[...], nv_ref[...] = _adam_math(w_ref[...], g_ref[...], m_ref[...], v_ref[...])

    vmem = pl.BlockSpec(memory_space=pltpu.VMEM)
    return pl.pallas_call(
        body, name=name, in_specs=[vmem] * 4, out_specs=[vmem] * 3,
        out_shape=[jax.ShapeDtypeStruct(w.shape, F32)] * 3, compiler_params=_params(),
    )(w, g, m, v)


def _sum_adam(own, parts, w, m, v, me, name, tr):
    _, rows, cols = parts.shape
    turned = w.shape == (cols, rows) and rows != cols
    assert tr == rows or not turned
    n_t = rows // tr

    def body(me_ref, own_ref, p_ref, w_ref, m_ref, v_ref, g_ref, d_ref, nm_ref, nv_ref):
        part = lambda dev: jnp.where(dev == me_ref[0], own_ref[...], p_ref[dev]).astype(F32)
        g = part(0)
        for dev in range(1, N_DEV):
            g = g + part(dev)
        g = g.T if turned else g
        g_ref[...] = g
        d_ref[...], nm_ref[...], nv_ref[...] = _adam_math(w_ref[...], g, m_ref[...], v_ref[...])

    spec = pl.BlockSpec((cols, rows) if turned else (tr, cols), lambda i, me_ref: (i, 0))
    shape = jax.ShapeDtypeStruct(w.shape, F32)
    return pl.pallas_call(
        body, name=name, out_shape=[shape] * 4,
        grid_spec=pltpu.PrefetchScalarGridSpec(
            num_scalar_prefetch=1, grid=(n_t,),
            in_specs=[pl.BlockSpec((tr, cols), lambda i, me_ref: (me_ref[0] * n_t + i, 0)),
                      pl.BlockSpec((N_DEV, tr, cols), lambda i, me_ref: (0, i, 0)), spec, spec, spec],
            out_specs=[spec] * 4),
        compiler_params=_params(("arbitrary",)),
    )(me.reshape(1).astype(jnp.int32), own, parts, w, m, v)


def _ada_grad_adam(c_all, dmod_all, w, m, v, tr):
    rows, cols = w.shape

    def body(c_ref, dm_ref, w_ref, m_ref, v_ref, g_ref, d_ref, nm_ref, nv_ref):
        cv = c_ref[...]
        act = cv * jax.nn.sigmoid(cv)
        dmod = dm_ref[:, pl.ds(pl.multiple_of(_index(_place()) * cols, LANES), cols)]
        g = lax.dot_general(act, dmod, TN, preferred_element_type=F32, precision=lax.Precision.HIGHEST)
        g_ref[...] = g
        d_ref[...], nm_ref[...], nv_ref[...] = _adam_math(w_ref[...], g, m_ref[...], v_ref[...])

    spec = pl.BlockSpec((tr, cols), lambda i: (i, 0))
    shape = jax.ShapeDtypeStruct((rows, cols), F32)
    return pl.pallas_call(
        body, name="ada_grad_adam", grid=(rows // tr,),
        in_specs=[pl.BlockSpec((N_DEV, tr), lambda i: (0, i)), pl.BlockSpec(dmod_all.shape, lambda i: (0, 0)), spec, spec, spec],
        out_specs=[spec] * 4, out_shape=[shape] * 4, compiler_params=_params(("arbitrary",)),
    )(c_all, dmod_all, w, m, v)


def _rope_tables(positions):
    inv_freq = ROPE_THETA ** (-jnp.arange(0, 2 * ROT_HALF, 2, dtype=F32) / (2 * ROT_HALF))
    ang = inv_freq[:, None] * positions.astype(F32)[None, :]
    rows = jnp.concatenate([jnp.cos(ang), jnp.sin(ang), jnp.ones_like(ang)], axis=0)
    spread = [[[0.0] * LANES for _ in range(3 * ROT_HALF)] for _ in range(3)]
    for lane in range(LANES):
        p, j = lane % HEAD_DIM, lane % ROT_HALF
        if p < ROT_HALF:
            spread[0][j][lane] = 1.0
            spread[1][ROT_HALF + j][lane] = -1.0
        elif p < 2 * ROT_HALF:
            spread[0][j][lane] = 1.0
            spread[2][ROT_HALF + j][lane] = 1.0
        else:
            spread[0][2 * ROT_HALF][lane] = 1.0
    return rows, jnp.array(spread, F32)


def _pad_rows(a, rows):
    return jnp.pad(a, ((0, rows - a.shape[0]), (0, 0)))


def _sequence_step(xs, target, rope, mods, gains, w_in_t, w_out_t, relay_ffn, fetch_ffn, send_grads, w_blk_b, b_pool_r,
                   pool_scale_r, conv_w_all, conv_b, after):
    sh_m, sc_m, gt_m, sh_f, sc_f, gt_f = mods
    g_pre_mix, g_post_mix, g_pre_ffn, g_post_ffn = gains
    h1, u_pool, qkv = _premix_inproj(xs, sh_m, sc_m, g_pre_mix, w_in_t, rope, after, tm=512)
    o_g, lse_g = _attn_fwd(qkv)
    x1, y1, h2, cat, attn, lse_all = _mix_out(xs, u_pool, o_g, lse_g, w_blk_b, b_pool_r, pool_scale_r, w_out_t,
                                              gt_m, g_post_mix, g_pre_ffn, sc_f, sh_f, tm=256)
    relay_ffn(x1)
    w_up_t, w_down_f = fetch_ffn(x1)
    gate, a_ffn, act, vd, dy2, dout, sums_ffn, loss_loc = _ffn_fwd_loss(h2, x1, target, w_up_t, w_down_f, conv_w_all, conv_b,
                                                              gt_f, g_post_ffn, tm=256, ck=256)

    dgc, dval, dw_down, dconv = _ffn_bwd_act(dy2, gate, a_ffn, act, vd, w_down_f, tm=512, tf=1408, ck=256)
    token = send_grads("down", [dw_down], [])
    dup, dh2 = _ffn_bwd_up(dgc, dval, w_up_t, conv_w_all, token, tm=256)
    dw_up_t = _wgrad(dup, h2, "wgrad_up", tk=2048, tmm=1408)
    token = send_grads("up", [dw_up_t], [])
    dx1, dpool, dattn, delta, dw_out_t, sums_mix = _mix_bwd(dh2, dout, x1, y1, cat, attn, w_out_t, sc_f,
                                                           g_pre_ffn, gt_m, g_post_mix, token, tm=512)
    du, dw_blk, sums_pool = _pool_bwd(dpool, u_pool, w_blk_b, b_pool_r, pool_scale_r, tm=512)
    token = send_grads("out", [dw_out_t], [sums_mix, sums_ffn, sums_pool, dw_blk, dconv, loss_loc])
    dproj, dw_in_t = _dproj_wgrad_in(du, _attn_bwd(qkv, dattn, lse_all, delta, token), rope, h1, tm=512, cm=512)
    token = send_grads("in", [dw_in_t], [])
    grad_x, sums_in = _inproj_bwd(dproj, w_in_t, xs, dx1, sc_m, g_pre_mix, token, tm=512)
    return (loss_loc, grad_x, dw_in_t, dw_out_t, dw_up_t, dw_down, dw_blk, dconv,
            sums_in, sums_mix, sums_ffn, sums_pool)


def kernel(x, c, positions, w_ada, b_ada, g_pre_mix, g_post_mix, g_pre_ffn, g_post_ffn, w_in, w_pool, b_pool, pool_scale, w_out, w_up, conv_w, conv_b, w_down, loss_target, m_w_ada, m_b_ada, m_g_pre_mix, m_g_post_mix, m_g_pre_ffn, m_g_post_ffn, m_w_in, m_w_pool, m_b_pool, m_pool_scale, m_w_out, m_w_up, m_conv_w, m_conv_b, m_w_down, v_w_ada, v_b_ada, v_g_pre_mix, v_g_post_mix, v_g_pre_ffn, v_g_post_ffn, v_w_in, v_w_pool, v_b_pool, v_pool_scale, v_w_out, v_w_up, v_conv_w, v_conv_b, v_w_down):
    s_len, d = x.shape[1], x.shape[2]
    d_ff = w_down.shape[1] * N_DEV
    me = _index(_place())
    xs, target = x[0], loss_target[0]

    c_all, mod, taps_all, (w_in_t, w_out_t), lands = _entry_exchange(
        jnp.broadcast_to(c, (8, d)), w_ada[0], b_ada, _pad_rows(conv_w[0], 8),
        [w_in[0].T.astype(BF16), w_out[0].T.astype(BF16)], [w_up[0].T, w_down[0]])
    c_all = c_all[:, 0, :]
    conv_w_all = jnp.transpose(taps_all[:, :3, :], (1, 0, 2)).reshape(3, d_ff)
    sh_m, sc_m, gt_m, sh_f, sc_f, gt_f = [mod[:, 0, :].reshape(1, -1)[:, k * d:(k + 1) * d] for k in range(6)]

    rope = _rope_tables(positions[0])
    w_blk = jnp.zeros((256, 256), F32)
    for gi in range(4):
        w_blk = lax.dynamic_update_slice(w_blk, w_pool[0, gi], (gi * HEAD_DIM, gi * HEAD_DIM))
    w_blk_b = w_blk.astype(BF16)
    b_pool_r, pool_scale_r = b_pool.reshape(1, 256), pool_scale.reshape(1, 256)

    w_send, w_recv, w_src, w_land, w_token = _exchange_start("gather_ici", [], lands, "ffn_weights_ici_start")
    relay = []

    def relay_ffn(after):
        _, blocks = _exchange_wait("gather_ici", w_send, w_recv, w_src, w_land, after, "ffn_weights_ici_wait")
        relay.extend(_exchange_start("gather_d2d", [], blocks, "ffn_weights_d2d_start"))

    def fetch_ffn(after):
        return _exchange_wait("gather_d2d", relay[0], relay[1], [], relay[3], after, "ffn_weights_d2d_wait")[1]

    flights = {}

    def send_grads(tag, slabs, whole):
        lands = [lax.empty((N_DEV, g.shape[0] // N_DEV, g.shape[1]), g.dtype) for g in slabs]
        lands += [lax.empty((N_DEV,) + a.shape, F32) for a in whole]
        modes = ("scatter",) * len(slabs) + ("allgather",) * len(whole)
        flights[tag] = (modes, *_exchange_start(modes, slabs + whole, lands, f"grads_{tag}_start"))
        return flights[tag][5]

    def arrived(tag, after):
        return _exchange_wait(*flights[tag][:5], after, f"grads_{tag}_wait")

    _, grad_x, *_, sums_in, _, _, _ = _sequence_step(
        xs, target, rope, (sh_m, sc_m, gt_m, sh_f, sc_f, gt_f), (g_pre_mix, g_post_mix, g_pre_ffn, g_post_ffn),
        w_in_t, w_out_t, relay_ffn, fetch_ffn, send_grads, w_blk_b, b_pool_r, pool_scale_r, conv_w_all, conv_b,
        w_token)

    send_grads("last", [], [sums_in])

    (own_down,), (parts_down,) = arrived("down", flights["last"][5])
    new_down = _sum_adam(own_down, parts_down, w_down[0], m_w_down[0], v_w_down[0], me, "adam_w_down", 176)
    (own_up,), (parts_up,) = arrived("up", new_down[0])
    new_up = _sum_adam(own_up, parts_up, w_up[0].T, m_w_up[0].T, v_w_up[0].T, me, "adam_w_up", 352)
    (own_out, *small), (parts_out, *gathered) = arrived("out", new_up[0])
    new_out = _sum_adam(own_out, parts_out, w_out[0], m_w_out[0], v_w_out[0], me, "adam_w_out", 128)
    (own_in,), (parts_in,) = arrived("in", new_out[0])
    new_in = _sum_adam(own_in, parts_in, w_in[0].T, m_w_in[0].T, v_w_in[0].T, me, "adam_w_in", 160)
    big = {"w_up": [a.T for a in new_up], "w_down": new_down, "w_out": new_out, "w_in": [a.T for a in new_in]}

    rep_w = [b_ada, g_pre_mix, g_post_mix, g_pre_ffn, g_post_ffn, w_pool, b_pool, pool_scale, conv_b]
    rep_m = [m_b_ada, m_g_pre_mix, m_g_post_mix, m_g_pre_ffn, m_g_post_ffn, m_w_pool, m_b_pool, m_pool_scale, m_conv_b]
    rep_v = [v_b_ada, v_g_pre_mix, v_g_post_mix, v_g_pre_ffn, v_g_post_ffn, v_w_pool, v_b_pool, v_pool_scale, v_conv_b]
    mine_last, got_last = arrived("last", new_in[0])
    small, gathered = [*mine_last, *small], [*got_last, *gathered]
    rep_out, dmod_all, dconv_tot, loss_tot = _small_sum_adam(small, gathered, rep_w, rep_m, rep_v)
    g_rep, d_rep, nm_rep, nv_rep = (rep_out[k::4] for k in range(4))

    fcol = d_ff // N_DEV
    taps = lambda a: jnp.transpose(a, (1, 0, 2))
    g_cw = lax.dynamic_slice(dconv_tot, (0, me * fcol), (3, fcol))[None]
    d_cw, nm_cw, nv_cw = [taps(a) for a in _adam(taps(conv_w), taps(g_cw), taps(m_conv_w), taps(v_conv_w), "adam_conv_w")]

    g_ada, d_ada, nm_ada, nv_ada = _ada_grad_adam(c_all, dmod_all, w_ada[0], m_w_ada[0], v_w_ada[0], 256)

    loss = loss_tot[0, 0]

    def group(k):
        rep = (g_rep, d_rep, nm_rep, nv_rep)[k]
        ada = (g_ada, d_ada, nm_ada, nv_ada)[k][None]
        cw = (g_cw, d_cw, nm_cw, nv_cw)[k]
        return [ada, rep[0], rep[1], rep[2], rep[3], rep[4], big["w_in"][k][None], rep[5], rep[6], rep[7],
                big["w_out"][k][None], big["w_up"][k][None], cw, rep[8], big["w_down"][k][None]]

    return (loss, grad_x[None], *group(0), *group(1), *group(2), *group(3))
```

```python
import functools
import math

import jax
import jax.numpy as jnp
from jax import lax
from jax.experimental import pallas as pl
from jax.experimental.pallas import tpu as pltpu

F32 = jnp.float32
BF16 = jnp.bfloat16
MESH = pl.DeviceIdType.MESH

N_DEV = 8
HEAD_DIM = 64
ROT_HALF = 8
ROPE_THETA = 500000.0
POOL_WINDOWS = (2, 4, 8, 16)
DILATIONS = (1, 4, 16)
BLOCK = 128
NORM_EPS = 1e-6
HALO = 16
MASKED = -1e30
ATTN_FWD_UNROLL = 8
ATTN_BWD_UNROLL = 8

ADAM_LR = 0.001
ADAM_B1 = 0.9
ADAM_B2 = 0.999
ADAM_EPS = 1e-08
ADAM_WD = 0.01
ADAM_STEP = 10

V7X_VMEM_LIMIT = 56 * 1024 * 1024
LANES = 128

NT = (((1,), (1,)), ((), ()))
NN = (((1,), (0,)), ((), ()))
TN = (((0,), (0,)), ((), ()))


def _dot(a, b, dims):
    return lax.dot_general(a, b, dims, preferred_element_type=F32)


def _params(sem=None, vmem=V7X_VMEM_LIMIT):
    if sem is None:
        return pltpu.CompilerParams(vmem_limit_bytes=vmem)
    return pltpu.CompilerParams(dimension_semantics=sem, vmem_limit_bytes=vmem)


def _rstd(v):
    return lax.rsqrt(jnp.mean(v * v, axis=-1, keepdims=True) + NORM_EPS)


def _norm_bwd(dn, n, rstd):
    return rstd * (dn - n * jnp.mean(dn * n, axis=-1, keepdims=True))


def _rope_lanes(cs_ref, spread_ref):
    return [lax.dot_general(cs_ref[...], spread_ref[k], TN, preferred_element_type=F32, precision=lax.Precision.HIGHEST)
            for k in range(3)]


def _rope_fwd(p, lanes):
    return p * lanes[0] + pltpu.roll(p, LANES - ROT_HALF, 1) * lanes[1] + pltpu.roll(p, ROT_HALF, 1) * lanes[2]


def _rope_bwd(dp, lanes):
    return dp * lanes[0] + pltpu.roll(dp * lanes[1], ROT_HALF, 1) + pltpu.roll(dp * lanes[2], LANES - ROT_HALF, 1)


def _gelu_parts(v):
    k2 = 2.0 * math.sqrt(2.0 / math.pi)
    c = 0.044715
    v2 = v * v
    s = jax.nn.sigmoid(v * (k2 + (k2 * c) * v2))
    g = v * s
    dg = s + g * (1.0 - s) * (k2 + (3.0 * k2 * c) * v2)
    return g, dg


def _halo_before(i, tile):
    return jnp.maximum(i * (tile // HALO) - 1, 0)


def _premix_inproj(x, sh, sc, g, w_in_t, rope, after, tm):
    s_len, d = x.shape
    n_proj = w_in_t.shape[0]
    n_slab = (n_proj - 256) // LANES

    def body(x_ref, sh_ref, sc_ref, g_ref, w_ref, cs_ref, spread_ref, after_ref, h_ref, up_ref, qkv_ref):
        xv = x_ref[...]
        h = (xv * _rstd(xv) * g_ref[...]) * (1.0 + sc_ref[...]) + sh_ref[...]
        hb = h.astype(BF16)
        h_ref[...] = hb
        up_ref[...] = _dot(hb, w_ref[0:256, :], NT)
        lanes = _rope_lanes(cs_ref, spread_ref)
        for pair in range(n_slab // 2):
            p = _dot(hb, w_ref[256 + 256 * pair:512 + 256 * pair, :], NT)
            for half in range(2):
                ph = p[:, half * LANES:(half + 1) * LANES]
                if pair < 6:
                    ph = _rope_fwd(ph, lanes)
                if pair < 3:
                    ph = ph * (HEAD_DIM ** -0.5)
                qkv_ref[2 * pair + half] = ph

    vec = pl.BlockSpec((1, d), lambda i: (0, 0))
    return pl.pallas_call(
        body, name="premix_inproj", grid=(s_len // tm,),
        in_specs=[pl.BlockSpec((tm, d), lambda i: (i, 0)), vec, vec, vec,
                  pl.BlockSpec((n_proj, d), lambda i: (0, 0)),
                  pl.BlockSpec((rope[0].shape[0], tm), lambda i: (0, i)), pl.BlockSpec(rope[1].shape, lambda i: (0, 0, 0)),
                  pl.BlockSpec(memory_space=pl.ANY)],
        out_specs=[pl.BlockSpec((tm, d), lambda i: (i, 0)),
                   pl.BlockSpec((tm, 256), lambda i: (i, 0)),
                   pl.BlockSpec((n_slab, tm, LANES), lambda i: (0, i, 0))],
        out_shape=[jax.ShapeDtypeStruct((s_len, d), BF16),
                   jax.ShapeDtypeStruct((s_len, 256), F32),
                   jax.ShapeDtypeStruct((n_slab, s_len, LANES), F32)],
        compiler_params=_params(("arbitrary",)),
    )(x, sh, sc, g, w_in_t, *rope, after)


def _block_rows(n, r, dil):
    start = n * (BLOCK * dil) + r
    if dil == 1:
        return pl.ds(pl.multiple_of(start, BLOCK), BLOCK)
    return pl.ds(start, BLOCK, stride=dil)


def _band_mask(n):
    ri = lax.broadcasted_iota(jnp.int32, (BLOCK, 2 * BLOCK), 0)
    cj = lax.broadcasted_iota(jnp.int32, (BLOCK, 2 * BLOCK), 1)
    cur = (cj >= BLOCK) & (cj - BLOCK <= ri)
    prev = (cj < BLOCK) & (cj >= ri) & (n > 0)
    return cur | prev


def _attn_fwd(qkv):
    s_len = qkv.shape[1]
    n_g = len(DILATIONS)

    def body(q_ref, k_ref, v_ref, o_ref, lse_ref):
        lane = lax.broadcasted_iota(jnp.int32, (BLOCK, LANES), 1)
        first = lane < HEAD_DIM

        def group(dil):
            nb = s_len // (BLOCK * dil)

            def block(t, carry):
                r, n = t // nb, t % nb
                cur = _block_rows(n, r, dil)
                prev = _block_rows(jnp.maximum(n - 1, 0), r, dil)
                q = q_ref[0, cur, :]
                kcat = jnp.concatenate([k_ref[0, prev, :], k_ref[0, cur, :]], axis=0).astype(BF16)
                vcat = jnp.concatenate([v_ref[0, prev, :], v_ref[0, cur, :]], axis=0).astype(BF16)
                valid = _band_mask(n)
                q2 = jnp.concatenate([jnp.where(first, q, 0.0), jnp.where(first, 0.0, q)], axis=0).astype(BF16)
                s = jnp.where(jnp.concatenate([valid, valid], axis=0), _dot(q2, kcat, NT), MASKED)
                m = jnp.max(s, axis=-1, keepdims=True)
                p = jnp.exp(s - m)
                den = jnp.sum(p, axis=-1, keepdims=True)
                o2 = _dot(p.astype(BF16), vcat, NN) / den
                lse2 = m + jnp.log(den)
                o_ref[0, 0, cur, :] = jnp.where(first, o2[:BLOCK], o2[BLOCK:])
                lse_ref[0, 0, cur, :] = jnp.where(first, lse2[:BLOCK], lse2[BLOCK:])
                return carry

            lax.fori_loop(0, nb * dil, block, 0, unroll=ATTN_FWD_UNROLL)

        for gi, dil in enumerate(DILATIONS):
            pl.when(pl.program_id(0) == gi)(functools.partial(group, dil))

    def slab(base):
        return pl.BlockSpec((1, s_len, LANES), lambda g, s: (base + 2 * g + s, 0, 0))

    out = pl.BlockSpec((1, 1, s_len, LANES), lambda g, s: (g, s, 0, 0))
    shape = jax.ShapeDtypeStruct((n_g, 2, s_len, LANES), F32)
    return pl.pallas_call(
        body, name="attn_fwd", grid=(n_g, 2),
        in_specs=[slab(0), slab(6), slab(12)], out_specs=[out, out], out_shape=[shape, shape],
        compiler_params=_params(("arbitrary", "arbitrary")),
    )(qkv, qkv, qkv)


def _pool_mixed(u, halo, i, tm):
    ue = jnp.concatenate([halo, u], axis=0)
    s2 = ue + pltpu.roll(ue, 1, 0)
    s4 = s2 + pltpu.roll(s2, 2, 0)
    s8 = s4 + pltpu.roll(s4, 4, 0)
    s16 = s8 + pltpu.roll(s8, 8, 0)
    grp = lax.broadcasted_iota(jnp.int32, (tm, 256), 1) // HEAD_DIM
    pick = lambda a, b, c, e: jnp.where(grp == 0, a, jnp.where(grp == 1, b, jnp.where(grp == 2, c, e)))
    win_sum = pick(s2[HALO:], s4[HALO:], s8[HALO:], s16[HALO:])
    pos = (i * tm + lax.broadcasted_iota(jnp.int32, (tm, 256), 0)).astype(F32)
    count = jnp.minimum(pos + 1.0, pick(*[float(w) for w in POOL_WINDOWS]))
    return win_sum / count - u, count


def _mix_out(x, u_pool, o_g, lse_g, w_blk, b_pool, pool_scale, w_out_t, gt_m, g_post_mix, g_pre_ffn, sc_f, sh_f, tm):
    s_len, d = x.shape

    def body(x_ref, u_ref, uh_ref, o_ref, l_ref, wb_ref, bp_ref, ps_ref, wo_ref,
             gt_ref, g1_ref, g2_ref, sc_ref, sh_ref,
             x1_ref, y1_ref, h2_ref, cat_ref, attn_ref, lall_ref):
        (o0, o1, o2), (l0, l1, l2) = (o_ref.at[g] for g in range(3)), (l_ref.at[g] for g in range(3))
        i = pl.program_id(0)
        u = u_ref[...]
        halo = uh_ref[...] * (i > 0).astype(F32)
        mixed, _ = _pool_mixed(u, halo, i, tm)
        y = _dot(mixed.astype(BF16), wb_ref[...], NN) + bp_ref[...]
        pool = y * ps_ref[...]
        attn = []
        for s in range(2):
            la, lb, lc = l0[s], l1[s], l2[s]
            mx = jnp.maximum(jnp.maximum(la, lb), lc)
            ea, eb, ec = jnp.exp(la - mx), jnp.exp(lb - mx), jnp.exp(lc - mx)
            den = ea + eb + ec
            lall_ref[s] = mx + jnp.log(den)
            attn.append((ea / den) * o0[s] + (eb / den) * o1[s] + (ec / den) * o2[s])
        attn = jnp.concatenate(attn, axis=1)
        attn_ref[...] = attn
        cat = jnp.concatenate([pool, attn], axis=1).astype(BF16)
        cat_ref[...] = cat
        y1 = _dot(cat, wo_ref[...], NT)
        y1_ref[...] = y1.astype(BF16)
        x1 = x_ref[...] + gt_ref[...] * (y1 * _rstd(y1) * g1_ref[...])
        x1_ref[...] = x1
        h2 = (x1 * _rstd(x1) * g2_ref[...]) * (1.0 + sc_ref[...]) + sh_ref[...]
        h2_ref[...] = h2.astype(BF16)

    tile = lambda w: pl.BlockSpec((tm, w), lambda i: (i, 0))
    slab = pl.BlockSpec((2, tm, LANES), lambda i: (0, i, 0))
    groups = pl.BlockSpec((len(DILATIONS), 2, tm, LANES), lambda i: (0, 0, i, 0))
    const = lambda a: pl.BlockSpec(a.shape, lambda i: (0,) * a.ndim)
    return pl.pallas_call(
        body, name="mix_out", grid=(s_len // tm,),
        in_specs=[tile(d), tile(256), pl.BlockSpec((HALO, 256), lambda i: (_halo_before(i, tm), 0)),
                  groups, groups,
                  const(w_blk), const(b_pool), const(pool_scale), const(w_out_t),
                  const(gt_m), const(g_post_mix), const(g_pre_ffn), const(sc_f), const(sh_f)],
        out_specs=[tile(d), tile(d), tile(d), tile(512), tile(256), slab],
        out_shape=[jax.ShapeDtypeStruct((s_len, d), F32), jax.ShapeDtypeStruct((s_len, d), BF16),
                   jax.ShapeDtypeStruct((s_len, d), BF16), jax.ShapeDtypeStruct((s_len, 512), BF16),
                   jax.ShapeDtypeStruct((s_len, 256), F32), jax.ShapeDtypeStruct((2, s_len, LANES), F32)],
        compiler_params=_params(("arbitrary",)),
    )(x, u_pool, u_pool, o_g, lse_g, w_blk, b_pool, pool_scale, w_out_t, gt_m, g_post_mix, g_pre_ffn, sc_f, sh_f)


def _conv_gate(gate_ext, cw, cb):
    gc = gate_ext * cw[2:3, :] + pltpu.roll(gate_ext, 1, 0) * cw[1:2, :] + pltpu.roll(gate_ext, 2, 0) * cw[0:1, :]
    return gc[HALO:] + cb


def _ffn_fwd_loss(h2, x1, target, w_up_t, w_down, conv_w, conv_b, gt_f, g_post_ffn, tm, ck):
    s_len, d = x1.shape
    d_ff = w_down.shape[0]
    n_t, n_c = s_len // tm, d_ff // ck

    def body(h_ref, hh_ref, x1_ref, tgt_ref, wg_ref, wv_ref, wd_ref, cw_ref, cb_ref, gt_ref, g_ref,
             gate_ref, a_ref, act_ref, vd_ref, dy2_ref, dout_ref, sums_ref, loss_ref, acc_ref):
        i = pl.program_id(0)

        @pl.when(i == 0)
        def _():
            sums_ref[...] = jnp.zeros_like(sums_ref)
            loss_ref[...] = jnp.zeros_like(loss_ref)
            acc_ref[...] = jnp.zeros_like(acc_ref)

        def finish(live):
            y2 = acc_ref[...]
            rstd = _rstd(y2)
            n = y2 * rstd
            rn = n * g_ref[...]
            err = x1_ref[...] + gt_ref[...] * rn - tgt_ref[...]
            keep = lambda v: jnp.where(live, v, 0.0)
            loss_ref[...] += keep(0.5 * jnp.sum(jnp.mean(err * err, axis=-1, keepdims=True), axis=0, keepdims=True))
            dout = err * (1.0 / d)
            dout_ref[...] = dout.astype(BF16)
            drn = dout * gt_ref[...]
            sums_ref[0:1, :] += keep(jnp.sum(dout * rn, axis=0, keepdims=True))
            sums_ref[1:2, :] += keep(jnp.sum(drn * n, axis=0, keepdims=True))
            dy2_ref[...] = _norm_bwd(drn * g_ref[...], n, rstd).astype(BF16)

        @pl.when(i < n_t)
        def _():
            h = h_ref[...]
            h_ext = jnp.concatenate([hh_ref[...], h], axis=0)
            row = lax.broadcasted_iota(jnp.int32, (tm + HALO, ck), 0)
            no_halo = (row < HALO) & (i == 0)

            def up(c):
                cs = slice(c * ck, (c + 1) * ck)
                return jnp.where(no_halo, 0.0, _dot(h_ext, wg_ref[cs, :], NT)), _dot(h, wv_ref[cs, :], NT)

            part = None
            nxt = up(0)
            finish(i > 0)
            for c in range(n_c):
                cs = slice(c * ck, (c + 1) * ck)
                gate_ext, val = nxt
                if c + 1 < n_c:
                    nxt = up(c + 1)
                act, dact = _gelu_parts(_conv_gate(gate_ext, cw_ref[:, cs], cb_ref[:, cs]))
                a = (act * val).astype(BF16)
                gate_ref[:, cs] = gate_ext[HALO:].astype(BF16)
                a_ref[:, cs] = a
                act_ref[:, cs] = act.astype(BF16)
                vd_ref[:, cs] = (val * dact).astype(BF16)
                p = _dot(a, wd_ref[cs, :], NN)
                part = p if part is None else part + p
            acc_ref[...] = part

        @pl.when(i == n_t)
        def _():
            finish(True)

    this = lambda i: jnp.minimum(i, n_t - 1)
    before = lambda i: jnp.maximum(i - 1, 0)
    tok = lambda w, at: pl.BlockSpec((tm, w), lambda i: (at(i), 0))
    vec = pl.BlockSpec((1, d), lambda i: (0, 0))
    once = lambda shape, imap: pl.BlockSpec(shape, imap, pipeline_mode=pl.Buffered(1))
    return pl.pallas_call(
        body, name="ffn_fwd_loss", grid=(n_t + 1,),
        in_specs=[tok(d, this), pl.BlockSpec((HALO, d), lambda i: (_halo_before(this(i), tm), 0)),
                  tok(d, before), tok(d, before),
                  once((d_ff, d), lambda i: (0, 0)), once((d_ff, d), lambda i: (1, 0)), once((d_ff, d), lambda i: (0, 0)),
                  pl.BlockSpec((3, d_ff), lambda i: (0, 0)), pl.BlockSpec((1, d_ff), lambda i: (0, 0)), vec, vec],
        out_specs=[tok(d_ff, this)] * 4 + [tok(d, before), tok(d, before), pl.BlockSpec((8, d), lambda i: (0, 0)),
                                          pl.BlockSpec((8, LANES), lambda i: (0, 0))],
        out_shape=[jax.ShapeDtypeStruct((s_len, d_ff), BF16)] * 4
        + [jax.ShapeDtypeStruct((s_len, d), BF16), jax.ShapeDtypeStruct((s_len, d), BF16),
           jax.ShapeDtypeStruct((8, d), F32), jax.ShapeDtypeStruct((8, LANES), F32)],
        scratch_shapes=[pltpu.VMEM((tm, d), F32)],
        compiler_params=_params(("arbitrary",)),
    )(h2, h2, x1, target, w_up_t, w_up_t, w_down, conv_w, conv_b, gt_f, g_post_ffn)


def _ffn_bwd_act(dy2, gate, a, act, vd, w_down, tm, tf, ck):
    s_len, d = dy2.shape
    d_ff = w_down.shape[0]
    n_t = s_len // tm
    chunks = [slice(lo, min(lo + ck, tf)) for lo in range(0, tf, ck)]

    def body(dy_ref, g_ref, gh_ref, a_ref, act_ref, vd_ref, wd_ref, dgc_ref, dval_ref, dwd_ref, dconv_ref, acc_ref):
        i = pl.program_id(1)

        @pl.when(i == 0)
        def _():
            acc_ref[...] = jnp.zeros_like(acc_ref)
            dconv_ref[...] = jnp.zeros_like(dconv_ref)

        dy = dy_ref[...]

        def down(cs):
            return _dot(dy, wd_ref[cs, :], NT)

        nxt = down(chunks[0])
        for c, cs in enumerate(chunks):
            width = cs.stop - cs.start
            da = nxt
            if c + 1 < len(chunks):
                nxt = down(chunks[c + 1])
            acc_ref[cs, :] += _dot(a_ref[:, cs], dy, TN)
            row = lax.broadcasted_iota(jnp.int32, (tm + HALO, width), 0)
            gate_ext = jnp.where((row < HALO) & (i == 0), 0.0,
                                 jnp.concatenate([gh_ref[:, cs], g_ref[:, cs]], axis=0).astype(F32))
            dgc = da * vd_ref[:, cs].astype(F32)
            dgc_ref[:, cs] = dgc.astype(BF16)
            dval_ref[:, cs] = (da * act_ref[:, cs].astype(F32)).astype(BF16)
            rows = [jnp.sum(dgc * pltpu.roll(gate_ext, 2 - k, 0)[HALO:], axis=0, keepdims=True) for k in range(2)]
            rows += [jnp.sum(dgc * gate_ext[HALO:], axis=0, keepdims=True), jnp.sum(dgc, axis=0, keepdims=True),
                     jnp.zeros((4, width), F32)]
            dconv_ref[:, cs] += jnp.concatenate(rows, axis=0)

        @pl.when(i == n_t - 1)
        def _():
            dwd_ref[...] = acc_ref[...].astype(BF16)

    tokf = pl.BlockSpec((tm, tf), lambda j, i: (i, j))
    return pl.pallas_call(
        body, name="ffn_bwd_act", grid=(d_ff // tf, n_t),
        in_specs=[pl.BlockSpec((tm, d), lambda j, i: (i, 0)), tokf,
                  pl.BlockSpec((HALO, tf), lambda j, i: (_halo_before(i, tm), j)), tokf, tokf, tokf,
                  pl.BlockSpec((tf, d), lambda j, i: (j, 0))],
        out_specs=[tokf, tokf, pl.BlockSpec((tf, d), lambda j, i: (j, 0)), pl.BlockSpec((8, tf), lambda j, i: (0, j))],
        out_shape=[jax.ShapeDtypeStruct((s_len, d_ff), BF16), jax.ShapeDtypeStruct((s_len, d_ff), BF16),
                   jax.ShapeDtypeStruct((d_ff, d), BF16), jax.ShapeDtypeStruct((8, d_ff), F32)],
        scratch_shapes=[pltpu.VMEM((tf, d), F32)],
        compiler_params=_params(("arbitrary", "arbitrary")),
    )(dy2, gate, gate, a, act, vd, w_down)


def _ffn_bwd_up(dgc, dval, w_up_t, conv_w, after, tm):
    s_len, d_ff = dgc.shape
    d = w_up_t.shape[1]
    n_t = s_len // tm

    def body(dg_ref, dgn_ref, dv_ref, cw_ref, w_ref, after_ref, dup_ref, dh_ref):
        i = pl.program_id(0)
        nxt = dgn_ref[...].astype(F32) * (i < n_t - 1).astype(F32)
        ext = jnp.concatenate([dg_ref[...].astype(F32), nxt], axis=0)
        rows = tm + HALO
        dgate = (ext * cw_ref[2:3, :] + pltpu.roll(ext, rows - 1, 0) * cw_ref[1:2, :]
                 + pltpu.roll(ext, rows - 2, 0) * cw_ref[0:1, :])[:tm]
        dup = jnp.concatenate([dgate.astype(BF16), dv_ref[...]], axis=1)
        dup_ref[...] = dup
        dh_ref[...] = _dot(dup, w_ref[...], NN).astype(BF16)

    tokf = pl.BlockSpec((tm, d_ff), lambda i: (i, 0))
    return pl.pallas_call(
        body, name="ffn_bwd_up", grid=(n_t,),
        in_specs=[tokf, pl.BlockSpec((HALO, d_ff), lambda i: (jnp.minimum((i + 1) * (tm // HALO), s_len // HALO - 1), 0)),
                  tokf, pl.BlockSpec((3, d_ff), lambda i: (0, 0)), pl.BlockSpec((2 * d_ff, d), lambda i: (0, 0)),
                  pl.BlockSpec(memory_space=pl.ANY)],
        out_specs=[pl.BlockSpec((tm, 2 * d_ff), lambda i: (i, 0)), pl.BlockSpec((tm, d), lambda i: (i, 0))],
        out_shape=[jax.ShapeDtypeStruct((s_len, 2 * d_ff), BF16), jax.ShapeDtypeStruct((s_len, d), BF16)],
        compiler_params=_params(("arbitrary",)),
    )(dgc, dgc, dval, conv_w, w_up_t, after)


def _mix_bwd(dh2, dout, x1, y1, cat, attn, w_out_t, sc_f, g_pre_ffn, gt_m, g_post_mix, after, tm):
    s_len, d = x1.shape
    n_t = s_len // tm

    def body(dh_ref, do_ref, x1_ref, y1_ref, cat_ref, at_ref, wo_ref, sc_ref, g2_ref, gt_ref, g1_ref, after_ref,
             dx1_ref, dpool_ref, dattn_ref, delta_ref, dwo_ref, sums_ref, acc_ref):
        i = pl.program_id(0)
        dh = dh_ref[...].astype(F32)
        x1 = x1_ref[...]
        r2 = _rstd(x1)
        n2 = x1 * r2
        ng = n2 * g2_ref[...]
        dng = dh * (1.0 + sc_ref[...])
        dx1 = do_ref[...].astype(F32) + _norm_bwd(dng * g2_ref[...], n2, r2)
        dx1_ref[...] = dx1.astype(BF16)
        y1 = y1_ref[...].astype(F32)
        r1 = _rstd(y1)
        n1 = y1 * r1
        drn = dx1 * gt_ref[...]
        dy1 = _norm_bwd(drn * g1_ref[...], n1, r1).astype(BF16)
        dcat = _dot(dy1, wo_ref[...], NN)
        dpool_ref[...] = dcat[:, 0:256]
        lane = lax.broadcasted_iota(jnp.int32, (tm, LANES), 1)
        first = lane < HEAD_DIM
        for s in range(2):
            da = dcat[:, 256 + s * LANES:256 + (s + 1) * LANES]
            dattn_ref[s] = da
            prod = da * at_ref[:, s * LANES:(s + 1) * LANES]
            tot = jnp.sum(prod, axis=-1, keepdims=True)
            lo = jnp.sum(jnp.where(first, prod, 0.0), axis=-1, keepdims=True)
            delta_ref[s] = jnp.where(first, lo, tot - lo)
        dwo = _dot(dy1, cat_ref[...], TN)
        sums = jnp.concatenate(
            [jnp.sum(dh, axis=0, keepdims=True), jnp.sum(dh * ng, axis=0, keepdims=True),
             jnp.sum(dng * n2, axis=0, keepdims=True), jnp.sum(dx1 * (n1 * g1_ref[...]), axis=0, keepdims=True),
             jnp.sum(drn * n1, axis=0, keepdims=True), jnp.zeros((3, d), F32)], axis=0)

        @pl.when(i == 0)
        def _():
            acc_ref[...] = dwo
            sums_ref[...] = sums

        @pl.when(i > 0)
        def _():
            acc_ref[...] += dwo
            sums_ref[...] += sums

        @pl.when(i == n_t - 1)
        def _():
            dwo_ref[...] = acc_ref[...].astype(BF16)

    tile = lambda w: pl.BlockSpec((tm, w), lambda i: (i, 0))
    slab = pl.BlockSpec((2, tm, LANES), lambda i: (0, i, 0))
    vec = pl.BlockSpec((1, d), lambda i: (0, 0))
    return pl.pallas_call(
        body, name="mix_bwd", grid=(n_t,),
        in_specs=[tile(d), tile(d), tile(d), tile(d), tile(512), tile(256),
                  pl.BlockSpec((d, 512), lambda i: (0, 0)), vec, vec, vec, vec, pl.BlockSpec(memory_space=pl.ANY)],
        out_specs=[tile(d), tile(256), slab, slab, pl.BlockSpec((d, 512), lambda i: (0, 0)),
                   pl.BlockSpec((8, d), lambda i: (0, 0))],
        out_shape=[jax.ShapeDtypeStruct((s_len, d), BF16), jax.ShapeDtypeStruct((s_len, 256), F32),
                   jax.ShapeDtypeStruct((2, s_len, LANES), F32), jax.ShapeDtypeStruct((2, s_len, LANES), F32),
                   jax.ShapeDtypeStruct((d, 512), BF16), jax.ShapeDtypeStruct((8, d), F32)],
        scratch_shapes=[pltpu.VMEM((d, 512), F32)],
        compiler_params=_params(("arbitrary",)),
    )(dh2, dout, x1, y1, cat, attn, w_out_t, sc_f, g_pre_ffn, gt_m, g_post_mix, after)


def _pool_bwd(dpool, u_pool, w_blk, b_pool, pool_scale, tm):
    s_len = dpool.shape[0]
    n_t = s_len // tm

    def body(dp_ref, dpn_ref, u_ref, uh_ref, wb_ref, bp_ref, ps_ref, du_ref, dwp_ref, sums_ref, acc_ref):
        i = pl.program_id(0)
        u = u_ref[...]
        mixed, _ = _pool_mixed(u, uh_ref[...] * (i > 0).astype(F32), i, tm)
        mixed_b = mixed.astype(BF16)
        y = _dot(mixed_b, wb_ref[...], NN) + bp_ref[...]
        dp = dp_ref[...]
        dy = dp * ps_ref[...]
        dwb = _dot(mixed_b, dy.astype(BF16), TN)
        sums = jnp.concatenate([jnp.sum(dy, axis=0, keepdims=True), jnp.sum(dp * y, axis=0, keepdims=True),
                                jnp.zeros((6, 256), F32)], axis=0)
        dp_ext = jnp.concatenate([dp, dpn_ref[...] * (i < n_t - 1).astype(F32)], axis=0)
        dmix = _dot((dp_ext * ps_ref[...]).astype(BF16), wb_ref[...], NT)
        rows = tm + HALO
        grp = lax.broadcasted_iota(jnp.int32, (rows, 256), 1) // HEAD_DIM
        pick = lambda a, b, c, e: jnp.where(grp == 0, a, jnp.where(grp == 1, b, jnp.where(grp == 2, c, e)))
        pos = (i * tm + lax.broadcasted_iota(jnp.int32, (rows, 256), 0)).astype(F32)
        z = dmix / jnp.minimum(pos + 1.0, pick(*[float(w) for w in POOL_WINDOWS]))
        f2 = z + pltpu.roll(z, rows - 1, 0)
        f4 = f2 + pltpu.roll(f2, rows - 2, 0)
        f8 = f4 + pltpu.roll(f4, rows - 4, 0)
        f16 = f8 + pltpu.roll(f8, rows - 8, 0)
        du_ref[...] = (pick(f2, f4, f8, f16) - dmix)[:tm]

        @pl.when(i == 0)
        def _():
            acc_ref[...] = dwb
            sums_ref[...] = sums

        @pl.when(i > 0)
        def _():
            acc_ref[...] += dwb
            sums_ref[...] += sums

        @pl.when(i == n_t - 1)
        def _():
            full = acc_ref[...]
            for gi in range(len(POOL_WINDOWS)):
                lo = gi * HEAD_DIM
                dwp_ref[gi] = full[lo:lo + HEAD_DIM, lo:lo + HEAD_DIM]

    n_g = len(POOL_WINDOWS)
    tile = pl.BlockSpec((tm, 256), lambda i: (i, 0))
    const = lambda a: pl.BlockSpec(a.shape, lambda i: (0,) * a.ndim)
    return pl.pallas_call(
        body, name="pool_bwd", grid=(n_t,),
        in_specs=[tile, pl.BlockSpec((HALO, 256), lambda i: (jnp.minimum((i + 1) * (tm // HALO), s_len // HALO - 1), 0)),
                  tile, pl.BlockSpec((HALO, 256), lambda i: (_halo_before(i, tm), 0)),
                  const(w_blk), const(b_pool), const(pool_scale)],
        out_specs=[tile, pl.BlockSpec((n_g, HEAD_DIM, HEAD_DIM), lambda i: (0, 0, 0)), pl.BlockSpec((8, 256), lambda i: (0, 0))],
        out_shape=[jax.ShapeDtypeStruct((s_len, 256), F32), jax.ShapeDtypeStruct((n_g, HEAD_DIM, HEAD_DIM), F32),
                   jax.ShapeDtypeStruct((8, 256), F32)],
        scratch_shapes=[pltpu.VMEM((256, 256), F32)],
        compiler_params=_params(("arbitrary",)),
    )(dpool, dpool, u_pool, u_pool, w_blk, b_pool, pool_scale)


def _attn_bwd(qkv, dattn, lse_all, delta, after):
    s_len = qkv.shape[1]
    n_g = len(DILATIONS)

    def body(q_ref, k_ref, v_ref, do_ref, l_ref, dl_ref, after_ref, dq_ref, dk_ref, dv_ref):
        lane = lax.broadcasted_iota(jnp.int32, (BLOCK, LANES), 1)
        first = lane < HEAD_DIM

        def group(dil):
            nb = s_len // (BLOCK * dil)

            def block(t, carry):
                dk_part, dv_part = carry
                r, n = t // nb, t % nb
                cur = _block_rows(n, r, dil)
                prev = _block_rows(jnp.maximum(n - 1, 0), r, dil)
                q = q_ref[0, cur, :]
                do = do_ref[0, cur, :]
                lse = l_ref[0, cur, :]
                dlt = dl_ref[0, cur, :]
                kcat = jnp.concatenate([k_ref[0, prev, :], k_ref[0, cur, :]], axis=0).astype(BF16)
                vcat = jnp.concatenate([v_ref[0, prev, :], v_ref[0, cur, :]], axis=0).astype(BF16)
                valid = _band_mask(n)
                stack = lambda a: jnp.concatenate([jnp.where(first, a, 0.0), jnp.where(first, 0.0, a)], axis=0)
                rows2 = lambda a: jnp.concatenate([a[:, 0:1], a[:, HEAD_DIM:HEAD_DIM + 1]], axis=0)
                q2, do2 = stack(q).astype(BF16), stack(do).astype(BF16)
                valid2 = jnp.concatenate([valid, valid], axis=0)
                p = jnp.where(valid2, jnp.exp(_dot(q2, kcat, NT) - rows2(lse)), 0.0)
                ds = (p * (_dot(do2, vcat, NT) - rows2(dlt))).astype(BF16)
                dq2 = _dot(ds, kcat, NN)
                dq_ref[0, 0, cur, :] = jnp.where(first, dq2[:BLOCK], dq2[BLOCK:])
                dkc = _dot(ds, q2, TN)
                dvc = _dot(p.astype(BF16), do2, TN)
                dk_ref[0, 0, prev, :] = dk_part + dkc[:BLOCK]
                dv_ref[0, 0, prev, :] = dv_part + dvc[:BLOCK]
                dk_ref[0, 0, cur, :] = dkc[BLOCK:]
                dv_ref[0, 0, cur, :] = dvc[BLOCK:]
                return dkc[BLOCK:], dvc[BLOCK:]

            def blocks(tt, carry):
                for u in range(ATTN_BWD_UNROLL):
                    carry = block(tt * ATTN_BWD_UNROLL + u, carry)
                return carry

            zero = jnp.zeros((BLOCK, LANES), F32)
            lax.fori_loop(0, nb * dil // ATTN_BWD_UNROLL, blocks, (zero, zero))

        for gi, dil in enumerate(DILATIONS):
            pl.when(pl.program_id(1) == gi)(functools.partial(group, dil))

    def slab(base):
        return pl.BlockSpec((1, s_len, LANES), lambda s, g: (base + 2 * g + s, 0, 0))

    one = pl.BlockSpec((1, s_len, LANES), lambda s, g: (s, 0, 0))
    out = pl.BlockSpec((1, 1, s_len, LANES), lambda s, g: (g, s, 0, 0))
    shape = jax.ShapeDtypeStruct((n_g, 2, s_len, LANES), F32)
    return pl.pallas_call(
        body, name="attn_bwd", grid=(2, n_g),
        in_specs=[slab(0), slab(6), slab(12), one, one, one, pl.BlockSpec(memory_space=pl.ANY)],
        out_specs=[out, out, out], out_shape=[shape, shape, shape],
        compiler_params=_params(("arbitrary", "arbitrary")),
    )(qkv, qkv, qkv, dattn, lse_all, delta, after)


def _dproj_wgrad_in(du, dqkv, rope, h1, tm, cm):
    s_len = du.shape[0]
    d = h1.shape[1]
    n_proj = 256 + 18 * LANES
    n_t = s_len // tm

    def body(du_ref, dq_ref, dk_ref, dv_ref, cs_ref, spread_ref, h_ref, dproj_ref, dw_ref, acc_ref):
        i = pl.program_id(0)

        @pl.when(i == 0)
        def _():
            acc_ref[...] = jnp.zeros_like(acc_ref)

        dproj_ref[:, 0:256] = du_ref[...].astype(BF16)
        lanes = _rope_lanes(cs_ref, spread_ref)
        col = 256
        for kind, dref in enumerate((dq_ref, dk_ref, dv_ref)):
            for grp in range(3):
                for s in range(2):
                    piece = dref[grp, s]
                    if kind < 2:
                        piece = _rope_bwd(piece, lanes)
                    if kind == 0:
                        piece = piece * (HEAD_DIM ** -0.5)
                    dproj_ref[:, col:col + LANES] = piece.astype(BF16)
                    col += LANES

        for c0 in range(0, n_proj, cm):
            acc_ref[c0:c0 + cm, :] += _dot(dproj_ref[:, c0:c0 + cm], h_ref[...], TN)

        @pl.when(i == n_t - 1)
        def _():
            dw_ref[...] = acc_ref[...].astype(BF16)

    groups = pl.BlockSpec((len(DILATIONS), 2, tm, LANES), lambda i: (0, 0, i, 0))
    return pl.pallas_call(
        body, name="dproj_wgrad_in", grid=(n_t,),
        in_specs=[pl.BlockSpec((tm, 256), lambda i: (i, 0))] + [groups] * 3
        + [pl.BlockSpec((rope[0].shape[0], tm), lambda i: (0, i)), pl.BlockSpec(rope[1].shape, lambda i: (0, 0, 0)),
           pl.BlockSpec((tm, d), lambda i: (i, 0))],
        out_specs=[pl.BlockSpec((tm, n_proj), lambda i: (i, 0)), pl.BlockSpec((n_proj, d), lambda i: (0, 0))],
        out_shape=[jax.ShapeDtypeStruct((s_len, n_proj), BF16), jax.ShapeDtypeStruct((n_proj, d), BF16)],
        scratch_shapes=[pltpu.VMEM((n_proj, d), F32)],
        compiler_params=_params(("arbitrary",)),
    )(du, *dqkv, *rope, h1)


def _inproj_bwd(dproj, w_in_t, x, dx1, sc_m, g_pre_mix, after, tm):
    s_len, d = x.shape
    n_proj = w_in_t.shape[0]
    n_t = s_len // tm

    def body(dproj_ref, w_ref, x_ref, dx1_ref, sc_ref, g_ref, after_ref, dx_ref, sums_ref):
        i = pl.program_id(0)
        halves = [slice(0, tm // 2), slice(tm // 2, tm)]
        dhs = [_dot(dproj_ref[rs, :], w_ref[...], NN) for rs in halves]
        sums = None
        for rs, dh in zip(halves, dhs):
            xv = x_ref[rs, :]
            r = _rstd(xv)
            n = xv * r
            dng = dh * (1.0 + sc_ref[...])
            dx_ref[rs, :] = dx1_ref[rs, :].astype(F32) + _norm_bwd(dng * g_ref[...], n, r)
            part = jnp.concatenate([jnp.sum(dh, axis=0, keepdims=True), jnp.sum(dh * (n * g_ref[...]), axis=0, keepdims=True),
                                    jnp.sum(dng * n, axis=0, keepdims=True), jnp.zeros((5, d), F32)], axis=0)
            sums = part if sums is None else sums + part

        @pl.when(i == 0)
        def _():
            sums_ref[...] = sums

        @pl.when(i > 0)
        def _():
            sums_ref[...] += sums

    tile = lambda w: pl.BlockSpec((tm, w), lambda i: (i, 0))
    vec = pl.BlockSpec((1, d), lambda i: (0, 0))
    return pl.pallas_call(
        body, name="inproj_bwd", grid=(n_t,),
        in_specs=[tile(n_proj), pl.BlockSpec((n_proj, d), lambda i: (0, 0)), tile(d), tile(d), vec, vec,
                  pl.BlockSpec(memory_space=pl.ANY)],
        out_specs=[tile(d), pl.BlockSpec((8, d), lambda i: (0, 0))],
        out_shape=[jax.ShapeDtypeStruct((s_len, d), F32), jax.ShapeDtypeStruct((8, d), F32)],
        compiler_params=_params(("arbitrary",)),
    )(dproj, w_in_t, x, dx1, sc_m, g_pre_mix, after)


def _wgrad(a, b, name, tk, tmm):
    s_len, m = a.shape
    n = b.shape[1]
    n_k = s_len // tk

    def body(a_ref, b_ref, o_ref, acc_ref):
        k = pl.program_id(1)
        part = _dot(a_ref[...], b_ref[...], TN)

        @pl.when(k == 0)
        def _():
            acc_ref[...] = part

        @pl.when(k > 0)
        def _():
            acc_ref[...] += part

        @pl.when(k == n_k - 1)
        def _():
            o_ref[...] = acc_ref[...].astype(BF16)

    return pl.pallas_call(
        body, name=name, grid=(m // tmm, n_k),
        in_specs=[pl.BlockSpec((tk, tmm), lambda j, k: (k, j)), pl.BlockSpec((tk, n), lambda j, k: (k, 0))],
        out_specs=pl.BlockSpec((tmm, n), lambda j, k: (j, 0)),
        out_shape=jax.ShapeDtypeStruct((m, n), BF16),
        scratch_shapes=[pltpu.VMEM((tmm, n), F32)],
        compiler_params=_params(("arbitrary", "arbitrary")),
    )(a, b)


def _place():
    return lax.axis_index("x"), lax.axis_index("y"), lax.axis_index("c")


def _peer(k):
    x, y, c = _place()
    bx, by, bc = (k >> 2) & 1, (k >> 1) & 1, k & 1
    return (x ^ bx if bx else x, y ^ by if by else y, c ^ bc if bc else c)


def _index(pos):
    return 4 * pos[0] + 2 * pos[1] + pos[2]


def _entry_exchange(c_rows, w_ada, b_ada, taps, shards, later):
    d = c_rows.shape[1]
    ncol = w_ada.shape[1]
    n_w, n_p = len(shards), len(later)

    def body(c_ref, w_ref, b_ref, t_ref, *rest):
        srcs, rest = rest[:n_w], rest[n_w:]
        later_refs, rest = rest[:n_p], rest[n_p:]
        (call_ref, mod_ref, tall_ref), rest = rest[:3], rest[3:]
        outs, rest = rest[:n_w], rest[n_w:]
        zones, rest = rest[:n_p], rest[n_p:]
        stage_ref, s_send, s_recv, w_send, w_recv, local_sems = rest[:6]
        wide, narrow, place_sems = rest[6:6 + n_p], rest[6 + n_p:6 + 2 * n_p], rest[6 + 2 * n_p]
        x, y, c = _place()
        here, sibling = (x, y, c), (x, y, 1 - c)
        chips = [(1 - x, y), (x, 1 - y), (1 - x, 1 - y)]
        me = _index(here)

        def small(kind, src, dst, k):
            return pltpu.make_async_remote_copy(src_ref=src, dst_ref=dst, send_sem=s_send.at[kind, k - 1],
                                                recv_sem=s_recv.at[kind, k - 1], device_id=_peer(k), device_id_type=MESH)

        gather = lambda k: small(0, c_ref, call_ref.at[me], k)
        scatter = lambda k: small(1, stage_ref.at[_index(_peer(k))], mod_ref.at[me], k)
        gather_taps = lambda k: small(2, t_ref, tall_ref.at[me], k)

        def rows(w, pos):
            r = shards[w].shape[0]
            return outs[w].at[pl.ds(pl.multiple_of(_index(pos) * r, 16), r), :]

        def block(k, w, pos, to, own=False):
            return pltpu.make_async_remote_copy(
                src_ref=srcs[w] if own else rows(w, pos), dst_ref=rows(w, pos),
                send_sem=w_send.at[k, w], recv_sem=w_recv.at[k, w], device_id=to, device_id_type=MESH)

        call_ref[me] = c_ref[...]
        tall_ref[me] = t_ref[...]
        for k in range(1, N_DEV):
            gather(k).start()
        for k in range(1, N_DEV):
            gather_taps(k).start()
        mine = [pltpu.make_async_copy(srcs[w], rows(w, here), local_sems.at[w]) for w in range(n_w)]
        for cp in mine:
            cp.start()
        first = [block(0, w, here, sibling, own=True) for w in range(n_w)]
        first += [block(1 + j, w, here, (*chip, c), own=True) for j, chip in enumerate(chips) for w in range(n_w)]
        for cp in first:
            cp.start()
        fetch = [pltpu.make_async_copy(later_refs[w], wide[w], place_sems.at[0, w]) for w in range(n_p)]
        for cp in fetch:
            cp.start()

        for k in range(1, N_DEV):
            gather(k).wait_recv()
        cv = jnp.concatenate([call_ref[b, 0:1, :] for b in range(N_DEV)], axis=0)
        act = cv * jax.nn.sigmoid(cv)
        mod = lax.dot_general(act, w_ref[...], NN, preferred_element_type=F32,
                              precision=lax.Precision.HIGHEST) + b_ref[:, pl.ds(pl.multiple_of(me * ncol, LANES), ncol)]
        for b in range(N_DEV):
            stage_ref[b] = jnp.broadcast_to(mod[b:b + 1, :], (8, ncol))
        mod_ref[me] = stage_ref[me]
        for k in range(1, N_DEV):
            scatter(k).start()

        placed = []
        for w in range(n_p):
            fetch[w].wait()
            narrow[w][...] = wide[w][...].astype(BF16)
            r = later[w].shape[0]
            placed.append(pltpu.make_async_copy(narrow[w], zones[w].at[pl.ds(pl.multiple_of(me * r, 16), r), :],
                                                place_sems.at[1, w]))
            placed[-1].start()

        passed = []
        for j, chip in enumerate(chips):
            for w in range(n_w):
                block(1 + j, w, (*chip, c), here).wait_recv()
                fwd = block(4 + j, w, (*chip, c), sibling)
                fwd.start()
                passed.append(fwd)
        for w in range(n_w):
            block(0, w, sibling, here).wait_recv()
        for j, chip in enumerate(chips):
            for w in range(n_w):
                block(4 + j, w, (*chip, 1 - c), here).wait_recv()
        for k in range(1, N_DEV):
            scatter(k).wait_recv()
            gather_taps(k).wait_recv()
        for cp in first + passed:
            cp.wait_send()
        for k in range(1, N_DEV):
            gather(k).wait_send()
            scatter(k).wait_send()
            gather_taps(k).wait_send()
        for cp in mine + placed:
            cp.wait()

    vmem, hbm = pl.BlockSpec(memory_space=pltpu.VMEM), pl.BlockSpec(memory_space=pltpu.HBM)
    out = pl.pallas_call(
        body, name="entry_exchange",
        in_specs=[vmem] * 4 + [hbm] * (n_w + n_p), out_specs=[vmem] * 3 + [hbm] * (n_w + n_p),
        out_shape=[jax.ShapeDtypeStruct((N_DEV, 8, d), F32), jax.ShapeDtypeStruct((N_DEV, 8, ncol), F32),
                   jax.ShapeDtypeStruct((N_DEV,) + taps.shape, F32)]
        + [jax.ShapeDtypeStruct((N_DEV * s.shape[0], s.shape[1]), s.dtype) for s in shards]
        + [jax.ShapeDtypeStruct((N_DEV * s.shape[0], s.shape[1]), BF16) for s in later],
        scratch_shapes=[pltpu.VMEM((N_DEV, 8, ncol), F32), pltpu.SemaphoreType.DMA((3, N_DEV - 1)),
                        pltpu.SemaphoreType.DMA((3, N_DEV - 1)), pltpu.SemaphoreType.DMA((N_DEV - 1, n_w)),
                        pltpu.SemaphoreType.DMA((N_DEV - 1, n_w)), pltpu.SemaphoreType.DMA((n_w,))]
        + [pltpu.VMEM(s.shape, F32) for s in later] + [pltpu.VMEM(s.shape, BF16) for s in later]
        + [pltpu.SemaphoreType.DMA((2, n_p))],
        compiler_params=_params(),
    )(c_rows, w_ada, b_ada, taps, *shards, *later)
    return out[0], out[1], out[2], out[3:3 + n_w], out[3 + n_w:]


def _peer_copies(mode, srcs, lands, send_sems, recv_sems):
    if mode in ("gather_ici", "gather_d2d"):
        x, y, c = _place()
        sibling = (x, y, 1 - c)
        chips = [(1 - x, y), (x, 1 - y), (1 - x, 1 - y)]
        n = len(lands)

        def rows(w, pos):
            r = lands[w].shape[0] // N_DEV
            return lands[w].at[pl.ds(pl.multiple_of(_index(pos) * r, 16), r), :]

        def copy(k, w, src, dst, to):
            return pltpu.make_async_remote_copy(src_ref=src, dst_ref=dst, send_sem=send_sems.at[k * n + w],
                                                recv_sem=recv_sems.at[k * n + w], device_id=to, device_id_type=MESH)

        if mode == "gather_ici":
            targets = [sibling] + [(*chip, c) for chip in chips]
            return [copy(k, w, rows(w, (x, y, c)), rows(w, (x, y, c)), to) for k, to in enumerate(targets) for w in range(n)]
        return [copy(j, w, rows(w, (*chip, c)), rows(w, (*chip, c)), sibling)
                for j, chip in enumerate(chips) for w in range(n)]
    me = _index(_place())
    modes = (mode,) * len(srcs) if isinstance(mode, str) else mode
    copies = []
    for k in range(1, N_DEV):
        peer = _peer(k)
        for w, (src, land) in enumerate(zip(srcs, lands)):
            if modes[w] == "gather":
                r = src.shape[0]
                dst = land.at[pl.ds(pl.multiple_of(me * r, 16), r), :]
            elif modes[w] == "allgather":
                dst = land.at[me]
            else:
                r = src.shape[0] // N_DEV
                src = src.at[pl.ds(pl.multiple_of(_index(peer) * r, 16), r), :]
                dst = land.at[me]
            copies.append(pltpu.make_async_remote_copy(
                src_ref=src, dst_ref=dst, send_sem=send_sems.at[(k - 1) * len(srcs) + w],
                recv_sem=recv_sems.at[(k - 1) * len(srcs) + w],
                device_id=peer, device_id_type=MESH))
    return copies


def _exchange_start(mode, srcs, lands, name):
    n_s, n_a = len(srcs), len(srcs) + len(lands)
    n_cp = _COPIES_PER_ARRAY.get(mode, N_DEV - 1) * len(lands)

    def body(*refs):
        for cp in _peer_copies(mode, refs[:n_s], refs[n_s:n_a], refs[n_a], refs[n_a + 1]):
            cp.start()

    hbm, sem = pl.BlockSpec(memory_space=pltpu.HBM), pl.BlockSpec(memory_space=pltpu.SEMAPHORE)
    arrays = list(srcs) + list(lands)
    out = pl.pallas_call(
        body, name=name,
        out_shape=(pltpu.SemaphoreType.DMA((n_cp,)), pltpu.SemaphoreType.DMA((n_cp,)),
                   *[pltpu.HBM(a.shape, a.dtype) for a in arrays]),
        in_specs=[hbm] * n_a, out_specs=(sem, sem, *[hbm] * n_a),
        input_output_aliases={i: 2 + i for i in range(n_a)},
        compiler_params=pltpu.CompilerParams(has_side_effects=pltpu.SideEffectType.DATAFLOW_SIDE_EFFECTING),
    )(*[pltpu.with_memory_space_constraint(a, pltpu.HBM) for a in arrays])
    return out[0], out[1], out[2:2 + n_s], out[2 + n_s:2 + n_a], out[2]


_COPIES_PER_ARRAY = {"gather_ici": 4, "gather_d2d": 3}


def _exchange_wait(mode, send_sems, recv_sems, srcs, lands, after, name):
    n_s, n_a = len(srcs), len(srcs) + len(lands)

    def body(*refs):
        copies = _peer_copies(mode, refs[:n_s], refs[n_s:n_a], refs[n_a], refs[n_a + 1])
        for cp in copies:
            cp.wait_send()
        for cp in copies:
            cp.wait_recv()

    hbm, sem = pl.BlockSpec(memory_space=pltpu.HBM), pl.BlockSpec(memory_space=pltpu.SEMAPHORE)
    arrays = list(srcs) + list(lands)
    out = pl.pallas_call(
        body, name=name, out_shape=tuple(pltpu.HBM(a.shape, a.dtype) for a in arrays),
        in_specs=[hbm] * n_a + [sem, sem, pl.BlockSpec(memory_space=pl.ANY)], out_specs=tuple([hbm] * n_a),
        input_output_aliases={i: i for i in range(n_a)},
        compiler_params=pltpu.CompilerParams(has_side_effects=pltpu.SideEffectType.DATAFLOW_SIDE_EFFECTING),
    )(*arrays, send_sems, recv_sems, after)
    return out[:n_s], out[n_s:]


SMALL_WEIGHTS = ("b_ada", "g_pre_mix", "g_post_mix", "g_pre_ffn", "g_post_ffn", "w_pool", "b_pool", "pool_scale", "conv_b")


MOD_ROWS = ((0, 0), (0, 1), (1, 3), (1, 0), (1, 1), (2, 0))


def _small_sum_adam(mine, gathered, weights, moms, vels):
    n_l, n_w = len(mine), len(weights)
    d = mine[0].shape[1]

    def body(*refs):
        loc, got = refs[:n_l], refs[n_l:2 * n_l]
        w_refs, m_refs, v_refs = (refs[2 * n_l + k * n_w:2 * n_l + (k + 1) * n_w] for k in range(3))
        outs = refs[2 * n_l + 3 * n_w:]
        dmod_ref, conv_ref, loss_ref = outs[4 * n_w:]
        me = _index(_place())
        part = lambda a, dev: jnp.where(dev == me, loc[a][...], got[a][dev])
        totals = []
        for a in range(n_l):
            tot = part(a, 0)
            for dev in range(1, N_DEV):
                tot = tot + part(a, dev)
            totals.append(tot)
        t_in, t_mix, t_ffn, t_pool, t_blk, t_conv, t_loss = totals
        conv_ref[...] = t_conv
        loss_ref[...] = t_loss
        for dev in range(N_DEV):
            for k, (a, r) in enumerate(MOD_ROWS):
                dmod_ref[dev:dev + 1, k * d:(k + 1) * d] = part(a, dev)[r:r + 1, :]

        def update(idx, g, at=()):
            sel = lambda ref: ref.at[at] if at else ref
            delta, nm, nv = _adam_math(sel(w_refs[idx])[...], g, sel(m_refs[idx])[...], sel(v_refs[idx])[...])
            for k, val in enumerate((g, delta, nm, nv)):
                sel(outs[4 * idx + k])[...] = val

        tots = (t_in, t_mix, t_ffn)
        update(0, jnp.concatenate([tots[a][r:r + 1] for a, r in MOD_ROWS], axis=1))
        update(1, t_in[2:3])
        update(2, t_mix[4:5])
        update(3, t_mix[2:3])
        update(4, t_ffn[1:2])
        for gi in range(len(POOL_WINDOWS)):
            update(5, t_blk[gi], at=(0, gi))
        update(6, jnp.concatenate([t_pool[0:1, gi * HEAD_DIM:(gi + 1) * HEAD_DIM] for gi in range(len(POOL_WINDOWS))], axis=0),
               at=(0,))
        update(7, t_pool[1:2])
        update(8, t_conv[3:4])

    vmem = pl.BlockSpec(memory_space=pltpu.VMEM)
    out = pl.pallas_call(
        body, name="small_sum_adam", in_specs=[vmem] * (2 * n_l + 3 * n_w), out_specs=[vmem] * (4 * n_w + 3),
        out_shape=[jax.ShapeDtypeStruct(w.shape, F32) for w in weights for _ in range(4)]
        + [jax.ShapeDtypeStruct((N_DEV, 6 * d), F32), jax.ShapeDtypeStruct(mine[5].shape, F32),
           jax.ShapeDtypeStruct(mine[6].shape, F32)],
        compiler_params=_params(),
    )(*mine, *gathered, *weights, *moms, *vels)
    return out[:4 * n_w], out[4 * n_w], out[4 * n_w + 1], out[4 * n_w + 2]


def _adam_math(w, g, m, v):
    m = ADAM_B1 * m + (1.0 - ADAM_B1) * g
    v = ADAM_B2 * v + (1.0 - ADAM_B2) * (g * g)
    m_hat = m / (1.0 - ADAM_B1 ** ADAM_STEP)
    v_hat = v / (1.0 - ADAM_B2 ** ADAM_STEP)
    delta = -ADAM_LR * (m_hat / (jnp.sqrt(v_hat) + ADAM_EPS) + ADAM_WD * w)
    return delta, m, v


def _adam(w, g, m, v, name):
    def body(w_ref, g_ref, m_ref, v_ref, d_ref, nm_ref, nv_ref):
        d_ref[...], nm_ref[...], nv_ref[...] = _adam_math(w_ref[...], g_ref[...], m_ref[...], v_ref[...])

    vmem = pl.BlockSpec(memory_space=pltpu.VMEM)
    return pl.pallas_call(
        body, name=name, in_specs=[vmem] * 4, out_specs=[vmem] * 3,
        out_shape=[jax.ShapeDtypeStruct(w.shape, F32)] * 3, compiler_params=_params(),
    )(w, g, m, v)


def _sum_adam(own, parts, w, m, v, me, name, tr):
    _, rows, cols = parts.shape
    turned = w.shape == (cols, rows) and rows != cols
    assert tr == rows or not turned
    n_t = rows // tr

    def body(me_ref, own_ref, p_ref, w_ref, m_ref, v_ref, g_ref, d_ref, nm_ref, nv_ref):
        part = lambda dev: jnp.where(dev == me_ref[0], own_ref[...], p_ref[dev]).astype(F32)
        g = part(0)
        for dev in range(1, N_DEV):
            g = g + part(dev)
        g = g.T if turned else g
        g_ref[...] = g
        d_ref[...], nm_ref[...], nv_ref[...] = _adam_math(w_ref[...], g, m_ref[...], v_ref[...])

    spec = pl.BlockSpec((cols, rows) if turned else (tr, cols), lambda i, me_ref: (i, 0))
    shape = jax.ShapeDtypeStruct(w.shape, F32)
    return pl.pallas_call(
        body, name=name, out_shape=[shape] * 4,
        grid_spec=pltpu.PrefetchScalarGridSpec(
            num_scalar_prefetch=1, grid=(n_t,),
            in_specs=[pl.BlockSpec((tr, cols), lambda i, me_ref: (me_ref[0] * n_t + i, 0)),
                      pl.BlockSpec((N_DEV, tr, cols), lambda i, me_ref: (0, i, 0)), spec, spec, spec],
            out_specs=[spec] * 4),
        compiler_params=_params(("arbitrary",)),
    )(me.reshape(1).astype(jnp.int32), own, parts, w, m, v)


def _ada_grad_adam(c_all, dmod_all, w, m, v, tr):
    rows, cols = w.shape

    def body(c_ref, dm_ref, w_ref, m_ref, v_ref, g_ref, d_ref, nm_ref, nv_ref):
        cv = c_ref[...]
        act = cv * jax.nn.sigmoid(cv)
        dmod = dm_ref[:, pl.ds(pl.multiple_of(_index(_place()) * cols, LANES), cols)]
        g = lax.dot_general(act, dmod, TN, preferred_element_type=F32, precision=lax.Precision.HIGHEST)
        g_ref[...] = g
        d_ref[...], nm_ref[...], nv_ref[...] = _adam_math(w_ref[...], g, m_ref[...], v_ref[...])

    spec = pl.BlockSpec((tr, cols), lambda i: (i, 0))
    shape = jax.ShapeDtypeStruct((rows, cols), F32)
    return pl.pallas_call(
        body, name="ada_grad_adam", grid=(rows // tr,),
        in_specs=[pl.BlockSpec((N_DEV, tr), lambda i: (0, i)), pl.BlockSpec(dmod_all.shape, lambda i: (0, 0)), spec, spec, spec],
        out_specs=[spec] * 4, out_shape=[shape] * 4, compiler_params=_params(("arbitrary",)),
    )(c_all, dmod_all, w, m, v)


def _rope_tables(positions):
    inv_freq = ROPE_THETA ** (-jnp.arange(0, 2 * ROT_HALF, 2, dtype=F32) / (2 * ROT_HALF))
    ang = inv_freq[:, None] * positions.astype(F32)[None, :]
    rows = jnp.concatenate([jnp.cos(ang), jnp.sin(ang), jnp.ones_like(ang)], axis=0)
    spread = [[[0.0] * LANES for _ in range(3 * ROT_HALF)] for _ in range(3)]
    for lane in range(LANES):
        p, j = lane % HEAD_DIM, lane % ROT_HALF
        if p < ROT_HALF:
            spread[0][j][lane] = 1.0
            spread[1][ROT_HALF + j][lane] = -1.0
        elif p < 2 * ROT_HALF:
            spread[0][j][lane] = 1.0
            spread[2][ROT_HALF + j][lane] = 1.0
        else:
            spread[0][2 * ROT_HALF][lane] = 1.0
    return rows, jnp.array(spread, F32)


def _pad_rows(a, rows):
    return jnp.pad(a, ((0, rows - a.shape[0]), (0, 0)))


def _sequence_step(xs, target, rope, mods, gains, w_in_t, w_out_t, relay_ffn, fetch_ffn, send_grads, w_blk_b, b_pool_r,
                   pool_scale_r, conv_w_all, conv_b, after):
    sh_m, sc_m, gt_m, sh_f, sc_f, gt_f = mods
    g_pre_mix, g_post_mix, g_pre_ffn, g_post_ffn = gains
    h1, u_pool, qkv = _premix_inproj(xs, sh_m, sc_m, g_pre_mix, w_in_t, rope, after, tm=512)
    o_g, lse_g = _attn_fwd(qkv)
    x1, y1, h2, cat, attn, lse_all = _mix_out(xs, u_pool, o_g, lse_g, w_blk_b, b_pool_r, pool_scale_r, w_out_t,
                                              gt_m, g_post_mix, g_pre_ffn, sc_f, sh_f, tm=512)
    relay_ffn(x1)
    w_up_t, w_down_f = fetch_ffn(x1)
    gate, a_ffn, act, vd, dy2, dout, sums_ffn, loss_loc = _ffn_fwd_loss(h2, x1, target, w_up_t, w_down_f, conv_w_all, conv_b,
                                                              gt_f, g_post_ffn, tm=256, ck=256)

    dgc, dval, dw_down, dconv = _ffn_bwd_act(dy2, gate, a_ffn, act, vd, w_down_f, tm=512, tf=1408, ck=256)
    token = send_grads("down", [dw_down], [])
    dup, dh2 = _ffn_bwd_up(dgc, dval, w_up_t, conv_w_all, token, tm=512)
    dw_up_t = _wgrad(dup, h2, "wgrad_up", tk=2048, tmm=1408)
    token = send_grads("up", [dw_up_t], [])
    dx1, dpool, dattn, delta, dw_out_t, sums_mix = _mix_bwd(dh2, dout, x1, y1, cat, attn, w_out_t, sc_f,
                                                           g_pre_ffn, gt_m, g_post_mix, token, tm=512)
    du, dw_blk, sums_pool = _pool_bwd(dpool, u_pool, w_blk_b, b_pool_r, pool_scale_r, tm=512)
    token = send_grads("out", [dw_out_t], [sums_mix, sums_ffn, sums_pool, dw_blk, dconv, loss_loc])
    dproj, dw_in_t = _dproj_wgrad_in(du, _attn_bwd(qkv, dattn, lse_all, delta, token), rope, h1, tm=512, cm=512)
    token = send_grads("in", [dw_in_t], [])
    grad_x, sums_in = _inproj_bwd(dproj, w_in_t, xs, dx1, sc_m, g_pre_mix, token, tm=512)
    return (loss_loc, grad_x, dw_in_t, dw_out_t, dw_up_t, dw_down, dw_blk, dconv,
            sums_in, sums_mix, sums_ffn, sums_pool)


def kernel(x, c, positions, w_ada, b_ada, g_pre_mix, g_post_mix, g_pre_ffn, g_post_ffn, w_in, w_pool, b_pool, pool_scale, w_out, w_up, conv_w, conv_b, w_down, loss_target, m_w_ada, m_b_ada, m_g_pre_mix, m_g_post_mix, m_g_pre_ffn, m_g_post_ffn, m_w_in, m_w_pool, m_b_pool, m_pool_scale, m_w_out, m_w_up, m_conv_w, m_conv_b, m_w_down, v_w_ada, v_b_ada, v_g_pre_mix, v_g_post_mix, v_g_pre_ffn, v_g_post_ffn, v_w_in, v_w_pool, v_b_pool, v_pool_scale, v_w_out, v_w_up, v_conv_w, v_conv_b, v_w_down):
    s_len, d = x.shape[1], x.shape[2]
    d_ff = w_down.shape[1] * N_DEV
    me = _index(_place())
    xs, target = x[0], loss_target[0]

    c_all, mod, taps_all, (w_in_t, w_out_t), lands = _entry_exchange(
        jnp.broadcast_to(c, (8, d)), w_ada[0], b_ada, _pad_rows(conv_w[0], 8),
        [w_in[0].T.astype(BF16), w_out[0].T.astype(BF16)], [w_up[0].T, w_down[0]])
    c_all = c_all[:, 0, :]
    conv_w_all = jnp.transpose(taps_all[:, :3, :], (1, 0, 2)).reshape(3, d_ff)
    sh_m, sc_m, gt_m, sh_f, sc_f, gt_f = [mod[:, 0, :].reshape(1, -1)[:, k * d:(k + 1) * d] for k in range(6)]

    rope = _rope_tables(positions[0])
    w_blk = jnp.zeros((256, 256), F32)
    for gi in range(4):
        w_blk = lax.dynamic_update_slice(w_blk, w_pool[0, gi], (gi * HEAD_DIM, gi * HEAD_DIM))
    w_blk_b = w_blk.astype(BF16)
    b_pool_r, pool_scale_r = b_pool.reshape(1, 256), pool_scale.reshape(1, 256)

    w_send, w_recv, w_src, w_land, w_token = _exchange_start("gather_ici", [], lands, "ffn_weights_ici_start")
    relay = []

    def relay_ffn(after):
        _, blocks = _exchange_wait("gather_ici", w_send, w_recv, w_src, w_land, after, "ffn_weights_ici_wait")
        relay.extend(_exchange_start("gather_d2d", [], blocks, "ffn_weights_d2d_start"))

    def fetch_ffn(after):
        return _exchange_wait("gather_d2d", relay[0], relay[1], [], relay[3], after, "ffn_weights_d2d_wait")[1]

    flights = {}

    def send_grads(tag, slabs, whole):
        lands = [lax.empty((N_DEV, g.shape[0] // N_DEV, g.shape[1]), g.dtype) for g in slabs]
        lands += [lax.empty((N_DEV,) + a.shape, F32) for a in whole]
        modes = ("scatter",) * len(slabs) + ("allgather",) * len(whole)
        flights[tag] = (modes, *_exchange_start(modes, slabs + whole, lands, f"grads_{tag}_start"))
        return flights[tag][5]

    def arrived(tag, after):
        return _exchange_wait(*flights[tag][:5], after, f"grads_{tag}_wait")

    _, grad_x, *_, sums_in, _, _, _ = _sequence_step(
        xs, target, rope, (sh_m, sc_m, gt_m, sh_f, sc_f, gt_f), (g_pre_mix, g_post_mix, g_pre_ffn, g_post_ffn),
        w_in_t, w_out_t, relay_ffn, fetch_ffn, send_grads, w_blk_b, b_pool_r, pool_scale_r, conv_w_all, conv_b,
        w_token)

    send_grads("last", [], [sums_in])

    (own_down,), (parts_down,) = arrived("down", flights["last"][5])
    new_down = _sum_adam(own_down, parts_down, w_down[0], m_w_down[0], v_w_down[0], me, "adam_w_down", 176)
    (own_up,), (parts_up,) = arrived("up", new_down[0])
    new_up = _sum_adam(own_up, parts_up, w_up[0].T, m_w_up[0].T, v_w_up[0].T, me, "adam_w_up", 352)
    (own_out, *small), (parts_out, *gathered) = arrived("out", new_up[0])
    new_out = _sum_adam(own_out, parts_out, w_out[0], m_w_out[0], v_w_out[0], me, "adam_w_out", 128)
    (own_in,), (parts_in,) = arrived("in", new_out[0])
    new_in = _sum_adam(own_in, parts_in, w_in[0].T, m_w_in[0].T, v_w_in[0].T, me, "adam_w_in", 160)
    big = {"w_up": [a.T for a in new_up], "w_down": new_down, "w_out": new_out, "w_in": [a.T for a in new_in]}

    rep_w = [b_ada, g_pre_mix, g_post_mix, g_pre_ffn, g_post_ffn, w_pool, b_pool, pool_scale, conv_b]
    rep_m = [m_b_ada, m_g_pre_mix, m_g_post_mix, m_g_pre_ffn, m_g_post_ffn, m_w_pool, m_b_pool, m_pool_scale, m_conv_b]
    rep_v = [v_b_ada, v_g_pre_mix, v_g_post_mix, v_g_pre_ffn, v_g_post_ffn, v_w_pool, v_b_pool, v_pool_scale, v_conv_b]
    mine_last, got_last = arrived("last", new_in[0])
    small, gathered = [*mine_last, *small], [*got_last, *gathered]
    rep_out, dmod_all, dconv_tot, loss_tot = _small_sum_adam(small, gathered, rep_w, rep_m, rep_v)
    g_rep, d_rep, nm_rep, nv_rep = (rep_out[k::4] for k in range(4))

    fcol = d_ff // N_DEV
    taps = lambda a: jnp.transpose(a, (1, 0, 2))
    g_cw = lax.dynamic_slice(dconv_tot, (0, me * fcol), (3, fcol))[None]
    d_cw, nm_cw, nv_cw = [taps(a) for a in _adam(taps(conv_w), taps(g_cw), taps(m_conv_w), taps(v_conv_w), "adam_conv_w")]

    g_ada, d_ada, nm_ada, nv_ada = _ada_grad_adam(c_all, dmod_all, w_ada[0], m_w_ada[0], v_w_ada[0], 256)

    loss = loss_tot[0, 0]

    def group(k):
        rep = (g_rep, d_rep, nm_rep, nv_rep)[k]
        ada = (g_ada, d_ada, nm_ada, nv_ada)[k][None]
        cw = (g_cw, d_cw, nm_cw, nv_cw)[k]
        return [ada, rep[0], rep[1], rep[2], rep[3], rep[4], big["w_in"][k][None], rep[5], rep[6], rep[7],
                big["w_out"][k][None], big["w_up"][k][None], cw, rep[8], big["w_down"][k][None]]

    return (loss, grad_x[None], *group(0), *group(1), *group(2), *group(3))
```

```python
import functools
import math

import jax
import jax.numpy as jnp
from jax import lax
from jax.experimental import pallas as pl
from jax.experimental.pallas import tpu as pltpu

F32 = jnp.float32
BF16 = jnp.bfloat16
MESH = pl.DeviceIdType.MESH

N_DEV = 8
HEAD_DIM = 64
ROT_HALF = 8
ROPE_THETA = 500000.0
POOL_WINDOWS = (2, 4, 8, 16)
DILATIONS = (1, 4, 16)
BLOCK = 128
NORM_EPS = 1e-6
HALO = 16
MASKED = -1e30
ATTN_FWD_UNROLL = 8
ATTN_BWD_UNROLL = 8

ADAM_LR = 0.001
ADAM_B1 = 0.9
ADAM_B2 = 0.999
ADAM_EPS = 1e-08
ADAM_WD = 0.01
ADAM_STEP = 10

V7X_VMEM_LIMIT = 56 * 1024 * 1024
LANES = 128

NT = (((1,), (1,)), ((), ()))
NN = (((1,), (0,)), ((), ()))
TN = (((0,), (0,)), ((), ()))


def _dot(a, b, dims):
    return lax.dot_general(a, b, dims, preferred_element_type=F32)


def _params(sem=None, vmem=V7X_VMEM_LIMIT):
    if sem is None:
        return pltpu.CompilerParams(vmem_limit_bytes=vmem)
    return pltpu.CompilerParams(dimension_semantics=sem, vmem_limit_bytes=vmem)


def _rstd(v):
    return lax.rsqrt(jnp.mean(v * v, axis=-1, keepdims=True) + NORM_EPS)


def _norm_bwd(dn, n, rstd):
    return rstd * (dn - n * jnp.mean(dn * n, axis=-1, keepdims=True))


def _rope_lanes(cs_ref, spread_ref):
    return [lax.dot_general(cs_ref[...], spread_ref[k], TN, preferred_element_type=F32, precision=lax.Precision.HIGHEST)
            for k in range(3)]


def _rope_fwd(p, lanes):
    return p * lanes[0] + pltpu.roll(p, LANES - ROT_HALF, 1) * lanes[1] + pltpu.roll(p, ROT_HALF, 1) * lanes[2]


def _rope_bwd(dp, lanes):
    return dp * lanes[0] + pltpu.roll(dp * lanes[1], ROT_HALF, 1) + pltpu.roll(dp * lanes[2], LANES - ROT_HALF, 1)


def _gelu_parts(v):
    k2 = 2.0 * math.sqrt(2.0 / math.pi)
    c = 0.044715
    v2 = v * v
    s = jax.nn.sigmoid(v * (k2 + (k2 * c) * v2))
    g = v * s
    dg = s + g * (1.0 - s) * (k2 + (3.0 * k2 * c) * v2)
    return g, dg


def _halo_before(i, tile):
    return jnp.maximum(i * (tile // HALO) - 1, 0)


def _premix_inproj(x, sh, sc, g, w_in_t, rope, after, tm):
    s_len, d = x.shape
    n_proj = w_in_t.shape[0]
    n_slab = (n_proj - 256) // LANES

    def body(x_ref, sh_ref, sc_ref, g_ref, w_ref, cs_ref, spread_ref, after_ref, h_ref, up_ref, qkv_ref):
        xv = x_ref[...]
        h = (xv * _rstd(xv) * g_ref[...]) * (1.0 + sc_ref[...]) + sh_ref[...]
        hb = h.astype(BF16)
        h_ref[...] = hb
        up_ref[...] = _dot(hb, w_ref[0:256, :], NT)
        lanes = _rope_lanes(cs_ref, spread_ref)
        for pair in range(n_slab // 2):
            p = _dot(hb, w_ref[256 + 256 * pair:512 + 256 * pair, :], NT)
            for half in range(2):
                ph = p[:, half * LANES:(half + 1) * LANES]
                if pair < 6:
                    ph = _rope_fwd(ph, lanes)
                if pair < 3:
                    ph = ph * (HEAD_DIM ** -0.5)
                qkv_ref[2 * pair + half] = ph

    vec = pl.BlockSpec((1, d), lambda i: (0, 0))
    return pl.pallas_call(
        body, name="premix_inproj", grid=(s_len // tm,),
        in_specs=[pl.BlockSpec((tm, d), lambda i: (i, 0)), vec, vec, vec,
                  pl.BlockSpec((n_proj, d), lambda i: (0, 0)),
                  pl.BlockSpec((rope[0].shape[0], tm), lambda i: (0, i)), pl.BlockSpec(rope[1].shape, lambda i: (0, 0, 0)),
                  pl.BlockSpec(memory_space=pl.ANY)],
        out_specs=[pl.BlockSpec((tm, d), lambda i: (i, 0)),
                   pl.BlockSpec((tm, 256), lambda i: (i, 0)),
                   pl.BlockSpec((n_slab, tm, LANES), lambda i: (0, i, 0))],
        out_shape=[jax.ShapeDtypeStruct((s_len, d), BF16),
                   jax.ShapeDtypeStruct((s_len, 256), F32),
                   jax.ShapeDtypeStruct((n_slab, s_len, LANES), F32)],
        compiler_params=_params(("arbitrary",)),
    )(x, sh, sc, g, w_in_t, *rope, after)


def _block_rows(n, r, dil):
    start = n * (BLOCK * dil) + r
    if dil == 1:
        return pl.ds(pl.multiple_of(start, BLOCK), BLOCK)
    return pl.ds(start, BLOCK, stride=dil)


def _band_mask(n):
    ri = lax.broadcasted_iota(jnp.int32, (BLOCK, 2 * BLOCK), 0)
    cj = lax.broadcasted_iota(jnp.int32, (BLOCK, 2 * BLOCK), 1)
    cur = (cj >= BLOCK) & (cj - BLOCK <= ri)
    prev = (cj < BLOCK) & (cj >= ri) & (n > 0)
    return cur | prev


def _attn_fwd(qkv):
    s_len = qkv.shape[1]
    n_g = len(DILATIONS)

    def body(q_ref, k_ref, v_ref, o_ref, lse_ref):
        lane = lax.broadcasted_iota(jnp.int32, (BLOCK, LANES), 1)
        first = lane < HEAD_DIM

        def group(dil):
            nb = s_len // (BLOCK * dil)

            def block(t, carry):
                r, n = t // nb, t % nb
                cur = _block_rows(n, r, dil)
                prev = _block_rows(jnp.maximum(n - 1, 0), r, dil)
                q = q_ref[0, cur, :]
                kcat = jnp.concatenate([k_ref[0, prev, :], k_ref[0, cur, :]], axis=0).astype(BF16)
                vcat = jnp.concatenate([v_ref[0, prev, :], v_ref[0, cur, :]], axis=0).astype(BF16)
                valid = _band_mask(n)
                q2 = jnp.concatenate([jnp.where(first, q, 0.0), jnp.where(first, 0.0, q)], axis=0).astype(BF16)
                s = jnp.where(jnp.concatenate([valid, valid], axis=0), _dot(q2, kcat, NT), MASKED)
                m = jnp.max(s, axis=-1, keepdims=True)
                p = jnp.exp(s - m)
                den = jnp.sum(p, axis=-1, keepdims=True)
                o2 = _dot(p.astype(BF16), vcat, NN) / den
                lse2 = m + jnp.log(den)
                o_ref[0, 0, cur, :] = jnp.where(first, o2[:BLOCK], o2[BLOCK:])
                lse_ref[0, 0, cur, :] = jnp.where(first, lse2[:BLOCK], lse2[BLOCK:])
                return carry

            lax.fori_loop(0, nb * dil, block, 0, unroll=ATTN_FWD_UNROLL)

        for gi, dil in enumerate(DILATIONS):
            pl.when(pl.program_id(0) == gi)(functools.partial(group, dil))

    def slab(base):
        return pl.BlockSpec((1, s_len, LANES), lambda g, s: (base + 2 * g + s, 0, 0))

    out = pl.BlockSpec((1, 1, s_len, LANES), lambda g, s: (g, s, 0, 0))
    shape = jax.ShapeDtypeStruct((n_g, 2, s_len, LANES), F32)
    return pl.pallas_call(
        body, name="attn_fwd", grid=(n_g, 2),
        in_specs=[slab(0), slab(6), slab(12)], out_specs=[out, out], out_shape=[shape, shape],
        compiler_params=_params(("arbitrary", "arbitrary")),
    )(qkv, qkv, qkv)


def _pool_mixed(u, halo, i, tm):
    ue = jnp.concatenate([halo, u], axis=0)
    s2 = ue + pltpu.roll(ue, 1, 0)
    s4 = s2 + pltpu.roll(s2, 2, 0)
    s8 = s4 + pltpu.roll(s4, 4, 0)
    s16 = s8 + pltpu.roll(s8, 8, 0)
    grp = lax.broadcasted_iota(jnp.int32, (tm, 256), 1) // HEAD_DIM
    pick = lambda a, b, c, e: jnp.where(grp == 0, a, jnp.where(grp == 1, b, jnp.where(grp == 2, c, e)))
    win_sum = pick(s2[HALO:], s4[HALO:], s8[HALO:], s16[HALO:])
    pos = (i * tm + lax.broadcasted_iota(jnp.int32, (tm, 256), 0)).astype(F32)
    count = jnp.minimum(pos + 1.0, pick(*[float(w) for w in POOL_WINDOWS]))
    return win_sum / count - u, count


def _mix_out(x, u_pool, o_g, lse_g, w_blk, b_pool, pool_scale, w_out_t, gt_m, g_post_mix, g_pre_ffn, sc_f, sh_f, tm):
    s_len, d = x.shape

    def body(x_ref, u_ref, uh_ref, o_ref, l_ref, wb_ref, bp_ref, ps_ref, wo_ref,
             gt_ref, g1_ref, g2_ref, sc_ref, sh_ref,
             x1_ref, y1_ref, h2_ref, cat_ref, attn_ref, lall_ref):
        (o0, o1, o2), (l0, l1, l2) = (o_ref.at[g] for g in range(3)), (l_ref.at[g] for g in range(3))
        i = pl.program_id(0)
        u = u_ref[...]
        halo = uh_ref[...] * (i > 0).astype(F32)
        mixed, _ = _pool_mixed(u, halo, i, tm)
        y = _dot(mixed.astype(BF16), wb_ref[...], NN) + bp_ref[...]
        pool = y * ps_ref[...]
        attn = []
        for s in range(2):
            la, lb, lc = l0[s], l1[s], l2[s]
            mx = jnp.maximum(jnp.maximum(la, lb), lc)
            ea, eb, ec = jnp.exp(la - mx), jnp.exp(lb - mx), jnp.exp(lc - mx)
            den = ea + eb + ec
            lall_ref[s] = mx + jnp.log(den)
            attn.append((ea / den) * o0[s] + (eb / den) * o1[s] + (ec / den) * o2[s])
        attn = jnp.concatenate(attn, axis=1)
        attn_ref[...] = attn
        cat = jnp.concatenate([pool, attn], axis=1).astype(BF16)
        cat_ref[...] = cat
        y1 = _dot(cat, wo_ref[...], NT)
        y1_ref[...] = y1.astype(BF16)
        x1 = x_ref[...] + gt_ref[...] * (y1 * _rstd(y1) * g1_ref[...])
        x1_ref[...] = x1
        h2 = (x1 * _rstd(x1) * g2_ref[...]) * (1.0 + sc_ref[...]) + sh_ref[...]
        h2_ref[...] = h2.astype(BF16)

    tile = lambda w: pl.BlockSpec((tm, w), lambda i: (i, 0))
    slab = pl.BlockSpec((2, tm, LANES), lambda i: (0, i, 0))
    groups = pl.BlockSpec((len(DILATIONS), 2, tm, LANES), lambda i: (0, 0, i, 0))
    const = lambda a: pl.BlockSpec(a.shape, lambda i: (0,) * a.ndim)
    return pl.pallas_call(
        body, name="mix_out", grid=(s_len // tm,),
        in_specs=[tile(d), tile(256), pl.BlockSpec((HALO, 256), lambda i: (_halo_before(i, tm), 0)),
                  groups, groups,
                  const(w_blk), const(b_pool), const(pool_scale), const(w_out_t),
                  const(gt_m), const(g_post_mix), const(g_pre_ffn), const(sc_f), const(sh_f)],
        out_specs=[tile(d), tile(d), tile(d), tile(512), tile(256), slab],
        out_shape=[jax.ShapeDtypeStruct((s_len, d), F32), jax.ShapeDtypeStruct((s_len, d), BF16),
                   jax.ShapeDtypeStruct((s_len, d), BF16), jax.ShapeDtypeStruct((s_len, 512), BF16),
                   jax.ShapeDtypeStruct((s_len, 256), F32), jax.ShapeDtypeStruct((2, s_len, LANES), F32)],
        compiler_params=_params(("arbitrary",)),
    )(x, u_pool, u_pool, o_g, lse_g, w_blk, b_pool, pool_scale, w_out_t, gt_m, g_post_mix, g_pre_ffn, sc_f, sh_f)


def _conv_gate(gate_ext, cw, cb):
    gc = gate_ext * cw[2:3, :] + pltpu.roll(gate_ext, 1, 0) * cw[1:2, :] + pltpu.roll(gate_ext, 2, 0) * cw[0:1, :]
    return gc[HALO:] + cb


def _ffn_fwd_loss(h2, x1, target, w_up_t, w_down, conv_w, conv_b, gt_f, g_post_ffn, tm, ck):
    s_len, d = x1.shape
    d_ff = w_down.shape[0]
    n_t, n_c = s_len // tm, d_ff // ck

    def body(h_ref, hh_ref, x1_ref, tgt_ref, wg_ref, wv_ref, wd_ref, cw_ref, cb_ref, gt_ref, g_ref,
             gate_ref, a_ref, act_ref, vd_ref, dy2_ref, dout_ref, sums_ref, loss_ref, acc_ref):
        i = pl.program_id(0)

        @pl.when(i == 0)
        def _():
            sums_ref[...] = jnp.zeros_like(sums_ref)
            loss_ref[...] = jnp.zeros_like(loss_ref)
            acc_ref[...] = jnp.zeros_like(acc_ref)

        def finish(live):
            y2 = acc_ref[...]
            rstd = _rstd(y2)
            n = y2 * rstd
            rn = n * g_ref[...]
            err = x1_ref[...] + gt_ref[...] * rn - tgt_ref[...]
            keep = lambda v: jnp.where(live, v, 0.0)
            loss_ref[...] += keep(0.5 * jnp.sum(jnp.mean(err * err, axis=-1, keepdims=True), axis=0, keepdims=True))
            dout = err * (1.0 / d)
            dout_ref[...] = dout.astype(BF16)
            drn = dout * gt_ref[...]
            sums_ref[0:1, :] += keep(jnp.sum(dout * rn, axis=0, keepdims=True))
            sums_ref[1:2, :] += keep(jnp.sum(drn * n, axis=0, keepdims=True))
            dy2_ref[...] = _norm_bwd(drn * g_ref[...], n, rstd).astype(BF16)

        @pl.when(i < n_t)
        def _():
            h = h_ref[...]
            h_ext = jnp.concatenate([hh_ref[...], h], axis=0)
            row = lax.broadcasted_iota(jnp.int32, (tm + HALO, ck), 0)
            no_halo = (row < HALO) & (i == 0)

            def up(c):
                cs = slice(c * ck, (c + 1) * ck)
                return jnp.where(no_halo, 0.0, _dot(h_ext, wg_ref[cs, :], NT)), _dot(h, wv_ref[cs, :], NT)

            part = None
            nxt = up(0)
            finish(i > 0)
            for c in range(n_c):
                cs = slice(c * ck, (c + 1) * ck)
                gate_ext, val = nxt
                if c + 1 < n_c:
                    nxt = up(c + 1)
                act, dact = _gelu_parts(_conv_gate(gate_ext, cw_ref[:, cs], cb_ref[:, cs]))
                a = (act * val).astype(BF16)
                gate_ref[:, cs] = gate_ext[HALO:].astype(BF16)
                a_ref[:, cs] = a
                act_ref[:, cs] = act.astype(BF16)
                vd_ref[:, cs] = (val * dact).astype(BF16)
                p = _dot(a, wd_ref[cs, :], NN)
                part = p if part is None else part + p
            acc_ref[...] = part

        @pl.when(i == n_t)
        def _():
            finish(True)

    this = lambda i: jnp.minimum(i, n_t - 1)
    before = lambda i: jnp.maximum(i - 1, 0)
    tok = lambda w, at: pl.BlockSpec((tm, w), lambda i: (at(i), 0))
    vec = pl.BlockSpec((1, d), lambda i: (0, 0))
    once = lambda shape, imap: pl.BlockSpec(shape, imap, pipeline_mode=pl.Buffered(1))
    return pl.pallas_call(
        body, name="ffn_fwd_loss", grid=(n_t + 1,),
        in_specs=[tok(d, this), pl.BlockSpec((HALO, d), lambda i: (_halo_before(this(i), tm), 0)),
                  tok(d, before), tok(d, before),
                  once((d_ff, d), lambda i: (0, 0)), once((d_ff, d), lambda i: (1, 0)), once((d_ff, d), lambda i: (0, 0)),
                  pl.BlockSpec((3, d_ff), lambda i: (0, 0)), pl.BlockSpec((1, d_ff), lambda i: (0, 0)), vec, vec],
        out_specs=[tok(d_ff, this)] * 4 + [tok(d, before), tok(d, before), pl.BlockSpec((8, d), lambda i: (0, 0)),
                                          pl.BlockSpec((8, LANES), lambda i: (0, 0))],
        out_shape=[jax.ShapeDtypeStruct((s_len, d_ff), BF16)] * 4
        + [jax.ShapeDtypeStruct((s_len, d), BF16), jax.ShapeDtypeStruct((s_len, d), BF16),
           jax.ShapeDtypeStruct((8, d), F32), jax.ShapeDtypeStruct((8, LANES), F32)],
        scratch_shapes=[pltpu.VMEM((tm, d), F32)],
        compiler_params=_params(("arbitrary",)),
    )(h2, h2, x1, target, w_up_t, w_up_t, w_down, conv_w, conv_b, gt_f, g_post_ffn)


def _ffn_bwd_act(dy2, gate, a, act, vd, w_down, tm, tf, ck):
    s_len, d = dy2.shape
    d_ff = w_down.shape[0]
    n_t = s_len // tm
    chunks = [slice(lo, min(lo + ck, tf)) for lo in range(0, tf, ck)]

    def body(dy_ref, g_ref, gh_ref, a_ref, act_ref, vd_ref, wd_ref, dgc_ref, dval_ref, dwd_ref, dconv_ref, acc_ref):
        i = pl.program_id(1)

        @pl.when(i == 0)
        def _():
            acc_ref[...] = jnp.zeros_like(acc_ref)
            dconv_ref[...] = jnp.zeros_like(dconv_ref)

        dy = dy_ref[...]

        def down(cs):
            return _dot(dy, wd_ref[cs, :], NT)

        nxt = down(chunks[0])
        for c, cs in enumerate(chunks):
            width = cs.stop - cs.start
            da = nxt
            if c + 1 < len(chunks):
                nxt = down(chunks[c + 1])
            acc_ref[cs, :] += _dot(a_ref[:, cs], dy, TN)
            row = lax.broadcasted_iota(jnp.int32, (tm + HALO, width), 0)
            gate_ext = jnp.where((row < HALO) & (i == 0), 0.0,
                                 jnp.concatenate([gh_ref[:, cs], g_ref[:, cs]], axis=0).astype(F32))
            dgc = da * vd_ref[:, cs].astype(F32)
            dgc_ref[:, cs] = dgc.astype(BF16)
            dval_ref[:, cs] = (da * act_ref[:, cs].astype(F32)).astype(BF16)
            rows = [jnp.sum(dgc * pltpu.roll(gate_ext, 2 - k, 0)[HALO:], axis=0, keepdims=True) for k in range(2)]
            rows += [jnp.sum(dgc * gate_ext[HALO:], axis=0, keepdims=True), jnp.sum(dgc, axis=0, keepdims=True),
                     jnp.zeros((4, width), F32)]
            dconv_ref[:, cs] += jnp.concatenate(rows, axis=0)

        @pl.when(i == n_t - 1)
        def _():
            dwd_ref[...] = acc_ref[...].astype(BF16)

    tokf = pl.BlockSpec((tm, tf), lambda j, i: (i, j))
    return pl.pallas_call(
        body, name="ffn_bwd_act", grid=(d_ff // tf, n_t),
        in_specs=[pl.BlockSpec((tm, d), lambda j, i: (i, 0)), tokf,
                  pl.BlockSpec((HALO, tf), lambda j, i: (_halo_before(i, tm), j)), tokf, tokf, tokf,
                  pl.BlockSpec((tf, d), lambda j, i: (j, 0))],
        out_specs=[tokf, tokf, pl.BlockSpec((tf, d), lambda j, i: (j, 0)), pl.BlockSpec((8, tf), lambda j, i: (0, j))],
        out_shape=[jax.ShapeDtypeStruct((s_len, d_ff), BF16), jax.ShapeDtypeStruct((s_len, d_ff), BF16),
                   jax.ShapeDtypeStruct((d_ff, d), BF16), jax.ShapeDtypeStruct((8, d_ff), F32)],
        scratch_shapes=[pltpu.VMEM((tf, d), F32)],
        compiler_params=_params(("arbitrary", "arbitrary")),
    )(dy2, gate, gate, a, act, vd, w_down)


def _ffn_bwd_up(dgc, dval, w_up_t, conv_w, after, tm):
    s_len, d_ff = dgc.shape
    d = w_up_t.shape[1]
    n_t = s_len // tm

    def body(dg_ref, dgn_ref, dv_ref, cw_ref, w_ref, after_ref, dup_ref, dh_ref):
        i = pl.program_id(0)
        nxt = dgn_ref[...].astype(F32) * (i < n_t - 1).astype(F32)
        ext = jnp.concatenate([dg_ref[...].astype(F32), nxt], axis=0)
        rows = tm + HALO
        dgate = (ext * cw_ref[2:3, :] + pltpu.roll(ext, rows - 1, 0) * cw_ref[1:2, :]
                 + pltpu.roll(ext, rows - 2, 0) * cw_ref[0:1, :])[:tm]
        dup = jnp.concatenate([dgate.astype(BF16), dv_ref[...]], axis=1)
        dup_ref[...] = dup
        dh_ref[...] = _dot(dup, w_ref[...], NN).astype(BF16)

    tokf = pl.BlockSpec((tm, d_ff), lambda i: (i, 0))
    return pl.pallas_call(
        body, name="ffn_bwd_up", grid=(n_t,),
        in_specs=[tokf, pl.BlockSpec((HALO, d_ff), lambda i: (jnp.minimum((i + 1) * (tm // HALO), s_len // HALO - 1), 0)),
                  tokf, pl.BlockSpec((3, d_ff), lambda i: (0, 0)), pl.BlockSpec((2 * d_ff, d), lambda i: (0, 0)),
                  pl.BlockSpec(memory_space=pl.ANY)],
        out_specs=[pl.BlockSpec((tm, 2 * d_ff), lambda i: (i, 0)), pl.BlockSpec((tm, d), lambda i: (i, 0))],
        out_shape=[jax.ShapeDtypeStruct((s_len, 2 * d_ff), BF16), jax.ShapeDtypeStruct((s_len, d), BF16)],
        compiler_params=_params(("arbitrary",)),
    )(dgc, dgc, dval, conv_w, w_up_t, after)


def _mix_bwd(dh2, dout, x1, y1, cat, attn, w_out_t, sc_f, g_pre_ffn, gt_m, g_post_mix, after, tm):
    s_len, d = x1.shape
    n_t = s_len // tm

    def body(dh_ref, do_ref, x1_ref, y1_ref, cat_ref, at_ref, wo_ref, sc_ref, g2_ref, gt_ref, g1_ref, after_ref,
             dx1_ref, dpool_ref, dattn_ref, delta_ref, dwo_ref, sums_ref, acc_ref):
        i = pl.program_id(0)
        dh = dh_ref[...].astype(F32)
        x1 = x1_ref[...]
        r2 = _rstd(x1)
        n2 = x1 * r2
        ng = n2 * g2_ref[...]
        dng = dh * (1.0 + sc_ref[...])
        dx1 = do_ref[...].astype(F32) + _norm_bwd(dng * g2_ref[...], n2, r2)
        dx1_ref[...] = dx1.astype(BF16)
        y1 = y1_ref[...].astype(F32)
        r1 = _rstd(y1)
        n1 = y1 * r1
        drn = dx1 * gt_ref[...]
        dy1 = _norm_bwd(drn * g1_ref[...], n1, r1).astype(BF16)
        dcat = _dot(dy1, wo_ref[...], NN)
        dpool_ref[...] = dcat[:, 0:256]
        lane = lax.broadcasted_iota(jnp.int32, (tm, LANES), 1)
        first = lane < HEAD_DIM
        for s in range(2):
            da = dcat[:, 256 + s * LANES:256 + (s + 1) * LANES]
            dattn_ref[s] = da
            prod = da * at_ref[:, s * LANES:(s + 1) * LANES]
            tot = jnp.sum(prod, axis=-1, keepdims=True)
            lo = jnp.sum(jnp.where(first, prod, 0.0), axis=-1, keepdims=True)
            delta_ref[s] = jnp.where(first, lo, tot - lo)
        dwo = _dot(dy1, cat_ref[...], TN)
        sums = jnp.concatenate(
            [jnp.sum(dh, axis=0, keepdims=True), jnp.sum(dh * ng, axis=0, keepdims=True),
             jnp.sum(dng * n2, axis=0, keepdims=True), jnp.sum(dx1 * (n1 * g1_ref[...]), axis=0, keepdims=True),
             jnp.sum(drn * n1, axis=0, keepdims=True), jnp.zeros((3, d), F32)], axis=0)

        @pl.when(i == 0)
        def _():
            acc_ref[...] = dwo
            sums_ref[...] = sums

        @pl.when(i > 0)
        def _():
            acc_ref[...] += dwo
            sums_ref[...] += sums

        @pl.when(i == n_t - 1)
        def _():
            dwo_ref[...] = acc_ref[...].astype(BF16)

    tile = lambda w: pl.BlockSpec((tm, w), lambda i: (i, 0))
    slab = pl.BlockSpec((2, tm, LANES), lambda i: (0, i, 0))
    vec = pl.BlockSpec((1, d), lambda i: (0, 0))
    return pl.pallas_call(
        body, name="mix_bwd", grid=(n_t,),
        in_specs=[tile(d), tile(d), tile(d), tile(d), tile(512), tile(256),
                  pl.BlockSpec((d, 512), lambda i: (0, 0)), vec, vec, vec, vec, pl.BlockSpec(memory_space=pl.ANY)],
        out_specs=[tile(d), tile(256), slab, slab, pl.BlockSpec((d, 512), lambda i: (0, 0)),
                   pl.BlockSpec((8, d), lambda i: (0, 0))],
        out_shape=[jax.ShapeDtypeStruct((s_len, d), BF16), jax.ShapeDtypeStruct((s_len, 256), F32),
                   jax.ShapeDtypeStruct((2, s_len, LANES), F32), jax.ShapeDtypeStruct((2, s_len, LANES), F32),
                   jax.ShapeDtypeStruct((d, 512), BF16), jax.ShapeDtypeStruct((8, d), F32)],
        scratch_shapes=[pltpu.VMEM((d, 512), F32)],
        compiler_params=_params(("arbitrary",)),
    )(dh2, dout, x1, y1, cat, attn, w_out_t, sc_f, g_pre_ffn, gt_m, g_post_mix, after)


def _pool_bwd(dpool, u_pool, w_blk, b_pool, pool_scale, tm):
    s_len = dpool.shape[0]
    n_t = s_len // tm

    def body(dp_ref, dpn_ref, u_ref, uh_ref, wb_ref, bp_ref, ps_ref, du_ref, dwp_ref, sums_ref, acc_ref):
        i = pl.program_id(0)
        u = u_ref[...]
        mixed, _ = _pool_mixed(u, uh_ref[...] * (i > 0).astype(F32), i, tm)
        mixed_b = mixed.astype(BF16)
        y = _dot(mixed_b, wb_ref[...], NN) + bp_ref[...]
        dp = dp_ref[...]
        dy = dp * ps_ref[...]
        dwb = _dot(mixed_b, dy.astype(BF16), TN)
        sums = jnp.concatenate([jnp.sum(dy, axis=0, keepdims=True), jnp.sum(dp * y, axis=0, keepdims=True),
                                jnp.zeros((6, 256), F32)], axis=0)
        dp_ext = jnp.concatenate([dp, dpn_ref[...] * (i < n_t - 1).astype(F32)], axis=0)
        dmix = _dot((dp_ext * ps_ref[...]).astype(BF16), wb_ref[...], NT)
        rows = tm + HALO
        grp = lax.broadcasted_iota(jnp.int32, (rows, 256), 1) // HEAD_DIM
        pick = lambda a, b, c, e: jnp.where(grp == 0, a, jnp.where(grp == 1, b, jnp.where(grp == 2, c, e)))
        pos = (i * tm + lax.broadcasted_iota(jnp.int32, (rows, 256), 0)).astype(F32)
        z = dmix / jnp.minimum(pos + 1.0, pick(*[float(w) for w in POOL_WINDOWS]))
        f2 = z + pltpu.roll(z, rows - 1, 0)
        f4 = f2 + pltpu.roll(f2, rows - 2, 0)
        f8 = f4 + pltpu.roll(f4, rows - 4, 0)
        f16 = f8 + pltpu.roll(f8, rows - 8, 0)
        du_ref[...] = (pick(f2, f4, f8, f16) - dmix)[:tm]

        @pl.when(i == 0)
        def _():
            acc_ref[...] = dwb
            sums_ref[...] = sums

        @pl.when(i > 0)
        def _():
            acc_ref[...] += dwb
            sums_ref[...] += sums

        @pl.when(i == n_t - 1)
        def _():
            full = acc_ref[...]
            for gi in range(len(POOL_WINDOWS)):
                lo = gi * HEAD_DIM
                dwp_ref[gi] = full[lo:lo + HEAD_DIM, lo:lo + HEAD_DIM]

    n_g = len(POOL_WINDOWS)
    tile = pl.BlockSpec((tm, 256), lambda i: (i, 0))
    const = lambda a: pl.BlockSpec(a.shape, lambda i: (0,) * a.ndim)
    return pl.pallas_call(
        body, name="pool_bwd", grid=(n_t,),
        in_specs=[tile, pl.BlockSpec((HALO, 256), lambda i: (jnp.minimum((i + 1) * (tm // HALO), s_len // HALO - 1), 0)),
                  tile, pl.BlockSpec((HALO, 256), lambda i: (_halo_before(i, tm), 0)),
                  const(w_blk), const(b_pool), const(pool_scale)],
        out_specs=[tile, pl.BlockSpec((n_g, HEAD_DIM, HEAD_DIM), lambda i: (0, 0, 0)), pl.BlockSpec((8, 256), lambda i: (0, 0))],
        out_shape=[jax.ShapeDtypeStruct((s_len, 256), F32), jax.ShapeDtypeStruct((n_g, HEAD_DIM, HEAD_DIM), F32),
                   jax.ShapeDtypeStruct((8, 256), F32)],
        scratch_shapes=[pltpu.VMEM((256, 256), F32)],
        compiler_params=_params(("arbitrary",)),
    )(dpool, dpool, u_pool, u_pool, w_blk, b_pool, pool_scale)


def _attn_bwd(qkv, dattn, lse_all, delta, after):
    s_len = qkv.shape[1]
    n_g = len(DILATIONS)

    def body(q_ref, k_ref, v_ref, do_ref, l_ref, dl_ref, after_ref, dq_ref, dk_ref, dv_ref):
        lane = lax.broadcasted_iota(jnp.int32, (BLOCK, LANES), 1)
        first = lane < HEAD_DIM

        def group(dil):
            nb = s_len // (BLOCK * dil)

            def block(t, carry):
                dk_part, dv_part = carry
                r, n = t // nb, t % nb
                cur = _block_rows(n, r, dil)
                prev = _block_rows(jnp.maximum(n - 1, 0), r, dil)
                q = q_ref[0, cur, :]
                do = do_ref[0, cur, :]
                lse = l_ref[0, cur, :]
                dlt = dl_ref[0, cur, :]
                kcat = jnp.concatenate([k_ref[0, prev, :], k_ref[0, cur, :]], axis=0).astype(BF16)
                vcat = jnp.concatenate([v_ref[0, prev, :], v_ref[0, cur, :]], axis=0).astype(BF16)
                valid = _band_mask(n)
                stack = lambda a: jnp.concatenate([jnp.where(first, a, 0.0), jnp.where(first, 0.0, a)], axis=0)
                rows2 = lambda a: jnp.concatenate([a[:, 0:1], a[:, HEAD_DIM:HEAD_DIM + 1]], axis=0)
                q2, do2 = stack(q).astype(BF16), stack(do).astype(BF16)
                valid2 = jnp.concatenate([valid, valid], axis=0)
                p = jnp.where(valid2, jnp.exp(_dot(q2, kcat, NT) - rows2(lse)), 0.0)
                ds = (p * (_dot(do2, vcat, NT) - rows2(dlt))).astype(BF16)
                dq2 = _dot(ds, kcat, NN)
                dq_ref[0, 0, cur, :] = jnp.where(first, dq2[:BLOCK], dq2[BLOCK:])
                dkc = _dot(ds, q2, TN)
                dvc = _dot(p.astype(BF16), do2, TN)
                dk_ref[0, 0, prev, :] = dk_part + dkc[:BLOCK]
                dv_ref[0, 0, prev, :] = dv_part + dvc[:BLOCK]
                dk_ref[0, 0, cur, :] = dkc[BLOCK:]
                dv_ref[0, 0, cur, :] = dvc[BLOCK:]
                return dkc[BLOCK:], dvc[BLOCK:]

            def blocks(tt, carry):
                for u in range(ATTN_BWD_UNROLL):
                    carry = block(tt * ATTN_BWD_UNROLL + u, carry)
                return carry

            zero = jnp.zeros((BLOCK, LANES), F32)
            lax.fori_loop(0, nb * dil // ATTN_BWD_UNROLL, blocks, (zero, zero))

        for gi, dil in enumerate(DILATIONS):
            pl.when(pl.program_id(1) == gi)(functools.partial(group, dil))

    def slab(base):
        return pl.BlockSpec((1, s_len, LANES), lambda s, g: (base + 2 * g + s, 0, 0))

    one = pl.BlockSpec((1, s_len, LANES), lambda s, g: (s, 0, 0))
    out = pl.BlockSpec((1, 1, s_len, LANES), lambda s, g: (g, s, 0, 0))
    shape = jax.ShapeDtypeStruct((n_g, 2, s_len, LANES), F32)
    return pl.pallas_call(
        body, name="attn_bwd", grid=(2, n_g),
        in_specs=[slab(0), slab(6), slab(12), one, one, one, pl.BlockSpec(memory_space=pl.ANY)],
        out_specs=[out, out, out], out_shape=[shape, shape, shape],
        compiler_params=_params(("arbitrary", "arbitrary")),
    )(qkv, qkv, qkv, dattn, lse_all, delta, after)


def _dproj_wgrad_in(du, dqkv, rope, h1, tm, cm):
    s_len = du.shape[0]
    d = h1.shape[1]
    n_proj = 256 + 18 * LANES
    n_t = s_len // tm

    def body(du_ref, dq_ref, dk_ref, dv_ref, cs_ref, spread_ref, h_ref, dproj_ref, dw_ref, acc_ref):
        i = pl.program_id(0)

        @pl.when(i == 0)
        def _():
            acc_ref[...] = jnp.zeros_like(acc_ref)

        dproj_ref[:, 0:256] = du_ref[...].astype(BF16)
        lanes = _rope_lanes(cs_ref, spread_ref)
        col = 256
        for kind, dref in enumerate((dq_ref, dk_ref, dv_ref)):
            for grp in range(3):
                for s in range(2):
                    piece = dref[grp, s]
                    if kind < 2:
                        piece = _rope_bwd(piece, lanes)
                    if kind == 0:
                        piece = piece * (HEAD_DIM ** -0.5)
                    dproj_ref[:, col:col + LANES] = piece.astype(BF16)
                    col += LANES

        for c0 in range(0, n_proj, cm):
            acc_ref[c0:c0 + cm, :] += _dot(dproj_ref[:, c0:c0 + cm], h_ref[...], TN)

        @pl.when(i == n_t - 1)
        def _():
            dw_ref[...] = acc_ref[...].astype(BF16)

    groups = pl.BlockSpec((len(DILATIONS), 2, tm, LANES), lambda i: (0, 0, i, 0))
    return pl.pallas_call(
        body, name="dproj_wgrad_in", grid=(n_t,),
        in_specs=[pl.BlockSpec((tm, 256), lambda i: (i, 0))] + [groups] * 3
        + [pl.BlockSpec((rope[0].shape[0], tm), lambda i: (0, i)), pl.BlockSpec(rope[1].shape, lambda i: (0, 0, 0)),
           pl.BlockSpec((tm, d), lambda i: (i, 0))],
        out_specs=[pl.BlockSpec((tm, n_proj), lambda i: (i, 0)), pl.BlockSpec((n_proj, d), lambda i: (0, 0))],
        out_shape=[jax.ShapeDtypeStruct((s_len, n_proj), BF16), jax.ShapeDtypeStruct((n_proj, d), BF16)],
        scratch_shapes=[pltpu.VMEM((n_proj, d), F32)],
        compiler_params=_params(("arbitrary",)),
    )(du, *dqkv, *rope, h1)


def _inproj_bwd(dproj, w_in_t, x, dx1, sc_m, g_pre_mix, after, tm):
    s_len, d = x.shape
    n_proj = w_in_t.shape[0]
    n_t = s_len // tm

    def body(dproj_ref, w_ref, x_ref, dx1_ref, sc_ref, g_ref, after_ref, dx_ref, sums_ref):
        i = pl.program_id(0)
        halves = [slice(0, tm // 2), slice(tm // 2, tm)]
        dhs = [_dot(dproj_ref[rs, :], w_ref[...], NN) for rs in halves]
        sums = None
        for rs, dh in zip(halves, dhs):
            xv = x_ref[rs, :]
            r = _rstd(xv)
            n = xv * r
            dng = dh * (1.0 + sc_ref[...])
            dx_ref[rs, :] = dx1_ref[rs, :].astype(F32) + _norm_bwd(dng * g_ref[...], n, r)
            part = jnp.concatenate([jnp.sum(dh, axis=0, keepdims=True), jnp.sum(dh * (n * g_ref[...]), axis=0, keepdims=True),
                                    jnp.sum(dng * n, axis=0, keepdims=True), jnp.zeros((5, d), F32)], axis=0)
            sums = part if sums is None else sums + part

        @pl.when(i == 0)
        def _():
            sums_ref[...] = sums

        @pl.when(i > 0)
        def _():
            sums_ref[...] += sums

    tile = lambda w: pl.BlockSpec((tm, w), lambda i: (i, 0))
    vec = pl.BlockSpec((1, d), lambda i: (0, 0))
    return pl.pallas_call(
        body, name="inproj_bwd", grid=(n_t,),
        in_specs=[tile(n_proj), pl.BlockSpec((n_proj, d), lambda i: (0, 0)), tile(d), tile(d), vec, vec,
                  pl.BlockSpec(memory_space=pl.ANY)],
        out_specs=[tile(d), pl.BlockSpec((8, d), lambda i: (0, 0))],
        out_shape=[jax.ShapeDtypeStruct((s_len, d), F32), jax.ShapeDtypeStruct((8, d), F32)],
        compiler_params=_params(("arbitrary",)),
    )(dproj, w_in_t, x, dx1, sc_m, g_pre_mix, after)


def _wgrad(a, b, name, tk, tmm):
    s_len, m = a.shape
    n = b.shape[1]
    n_k = s_len // tk

    def body(a_ref, b_ref, o_ref, acc_ref):
        k = pl.program_id(1)
        part = _dot(a_ref[...], b_ref[...], TN)

        @pl.when(k == 0)
        def _():
            acc_ref[...] = part

        @pl.when(k > 0)
        def _():
            acc_ref[...] += part

        @pl.when(k == n_k - 1)
        def _():
            o_ref[...] = acc_ref[...].astype(BF16)

    return pl.pallas_call(
        body, name=name, grid=(m // tmm, n_k),
        in_specs=[pl.BlockSpec((tk, tmm), lambda j, k: (k, j)), pl.BlockSpec((tk, n), lambda j, k: (k, 0))],
        out_specs=pl.BlockSpec((tmm, n), lambda j, k: (j, 0)),
        out_shape=jax.ShapeDtypeStruct((m, n), BF16),
        scratch_shapes=[pltpu.VMEM((tmm, n), F32)],
        compiler_params=_params(("arbitrary", "arbitrary")),
    )(a, b)


def _place():
    return lax.axis_index("x"), lax.axis_index("y"), lax.axis_index("c")


def _peer(k):
    x, y, c = _place()
    bx, by, bc = (k >> 2) & 1, (k >> 1) & 1, k & 1
    return (x ^ bx if bx else x, y ^ by if by else y, c ^ bc if bc else c)


def _index(pos):
    return 4 * pos[0] + 2 * pos[1] + pos[2]


def _entry_exchange(c_rows, w_ada, b_ada, taps, shards, later):
    d = c_rows.shape[1]
    ncol = w_ada.shape[1]
    n_w, n_p = len(shards), len(later)

    def body(c_ref, w_ref, b_ref, t_ref, *rest):
        srcs, rest = rest[:n_w], rest[n_w:]
        later_refs, rest = rest[:n_p], rest[n_p:]
        (call_ref, mod_ref, tall_ref), rest = rest[:3], rest[3:]
        outs, rest = rest[:n_w], rest[n_w:]
        zones, rest = rest[:n_p], rest[n_p:]
        stage_ref, s_send, s_recv, w_send, w_recv, local_sems = rest[:6]
        wide, narrow, place_sems = rest[6:6 + n_p], rest[6 + n_p:6 + 2 * n_p], rest[6 + 2 * n_p]
        x, y, c = _place()
        here, sibling = (x, y, c), (x, y, 1 - c)
        chips = [(1 - x, y), (x, 1 - y), (1 - x, 1 - y)]
        me = _index(here)

        def small(kind, src, dst, k):
            return pltpu.make_async_remote_copy(src_ref=src, dst_ref=dst, send_sem=s_send.at[kind, k - 1],
                                                recv_sem=s_recv.at[kind, k - 1], device_id=_peer(k), device_id_type=MESH)

        gather = lambda k: small(0, c_ref, call_ref.at[me], k)
        scatter = lambda k: small(1, stage_ref.at[_index(_peer(k))], mod_ref.at[me], k)
        gather_taps = lambda k: small(2, t_ref, tall_ref.at[me], k)

        def rows(w, pos):
            r = shards[w].shape[0]
            return outs[w].at[pl.ds(pl.multiple_of(_index(pos) * r, 16), r), :]

        def block(k, w, pos, to, own=False):
            return pltpu.make_async_remote_copy(
                src_ref=srcs[w] if own else rows(w, pos), dst_ref=rows(w, pos),
                send_sem=w_send.at[k, w], recv_sem=w_recv.at[k, w], device_id=to, device_id_type=MESH)

        call_ref[me] = c_ref[...]
        tall_ref[me] = t_ref[...]
        for k in range(1, N_DEV):
            gather(k).start()
        for k in range(1, N_DEV):
            gather_taps(k).start()
        mine = [pltpu.make_async_copy(srcs[w], rows(w, here), local_sems.at[w]) for w in range(n_w)]
        for cp in mine:
            cp.start()
        first = [block(0, w, here, sibling, own=True) for w in range(n_w)]
        first += [block(1 + j, w, here, (*chip, c), own=True) for j, chip in enumerate(chips) for w in range(n_w)]
        for cp in first:
            cp.start()
        fetch = [pltpu.make_async_copy(later_refs[w], wide[w], place_sems.at[0, w]) for w in range(n_p)]
        for cp in fetch:
            cp.start()

        for k in range(1, N_DEV):
            gather(k).wait_recv()
        cv = jnp.concatenate([call_ref[b, 0:1, :] for b in range(N_DEV)], axis=0)
        act = cv * jax.nn.sigmoid(cv)
        mod = lax.dot_general(act, w_ref[...], NN, preferred_element_type=F32,
                              precision=lax.Precision.HIGHEST) + b_ref[:, pl.ds(pl.multiple_of(me * ncol, LANES), ncol)]
        for b in range(N_DEV):
            stage_ref[b] = jnp.broadcast_to(mod[b:b + 1, :], (8, ncol))
        mod_ref[me] = stage_ref[me]
        for k in range(1, N_DEV):
            scatter(k).start()

        placed = []
        for w in range(n_p):
            fetch[w].wait()
            narrow[w][...] = wide[w][...].astype(BF16)
            r = later[w].shape[0]
            placed.append(pltpu.make_async_copy(narrow[w], zones[w].at[pl.ds(pl.multiple_of(me * r, 16), r), :],
                                                place_sems.at[1, w]))
            placed[-1].start()

        passed = []
        for j, chip in enumerate(chips):
            for w in range(n_w):
                block(1 + j, w, (*chip, c), here).wait_recv()
                fwd = block(4 + j, w, (*chip, c), sibling)
                fwd.start()
                passed.append(fwd)
        for w in range(n_w):
            block(0, w, sibling, here).wait_recv()
        for j, chip in enumerate(chips):
            for w in range(n_w):
                block(4 + j, w, (*chip, 1 - c), here).wait_recv()
        for k in range(1, N_DEV):
            scatter(k).wait_recv()
            gather_taps(k).wait_recv()
        for cp in first + passed:
            cp.wait_send()
        for k in range(1, N_DEV):
            gather(k).wait_send()
            scatter(k).wait_send()
            gather_taps(k).wait_send()
        for cp in mine + placed:
            cp.wait()

    vmem, hbm = pl.BlockSpec(memory_space=pltpu.VMEM), pl.BlockSpec(memory_space=pltpu.HBM)
    out = pl.pallas_call(
        body, name="entry_exchange",
        in_specs=[vmem] * 4 + [hbm] * (n_w + n_p), out_specs=[vmem] * 3 + [hbm] * (n_w + n_p),
        out_shape=[jax.ShapeDtypeStruct((N_DEV, 8, d), F32), jax.ShapeDtypeStruct((N_DEV, 8, ncol), F32),
                   jax.ShapeDtypeStruct((N_DEV,) + taps.shape, F32)]
        + [jax.ShapeDtypeStruct((N_DEV * s.shape[0], s.shape[1]), s.dtype) for s in shards]
        + [jax.ShapeDtypeStruct((N_DEV * s.shape[0], s.shape[1]), BF16) for s in later],
        scratch_shapes=[pltpu.VMEM((N_DEV, 8, ncol), F32), pltpu.SemaphoreType.DMA((3, N_DEV - 1)),
                        pltpu.SemaphoreType.DMA((3, N_DEV - 1)), pltpu.SemaphoreType.DMA((N_DEV - 1, n_w)),
                        pltpu.SemaphoreType.DMA((N_DEV - 1, n_w)), pltpu.SemaphoreType.DMA((n_w,))]
        + [pltpu.VMEM(s.shape, F32) for s in later] + [pltpu.VMEM(s.shape, BF16) for s in later]
        + [pltpu.SemaphoreType.DMA((2, n_p))],
        compiler_params=_params(),
    )(c_rows, w_ada, b_ada, taps, *shards, *later)
    return out[0], out[1], out[2], out[3:3 + n_w], out[3 + n_w:]


def _peer_copies(mode, srcs, lands, send_sems, recv_sems):
    if mode in ("gather_ici", "gather_d2d"):
        x, y, c = _place()
        sibling = (x, y, 1 - c)
        chips = [(1 - x, y), (x, 1 - y), (1 - x, 1 - y)]
        n = len(lands)

        def rows(w, pos):
            r = lands[w].shape[0] // N_DEV
            return lands[w].at[pl.ds(pl.multiple_of(_index(pos) * r, 16), r), :]

        def copy(k, w, src, dst, to):
            return pltpu.make_async_remote_copy(src_ref=src, dst_ref=dst, send_sem=send_sems.at[k * n + w],
                                                recv_sem=recv_sems.at[k * n + w], device_id=to, device_id_type=MESH)

        if mode == "gather_ici":
            targets = [sibling] + [(*chip, c) for chip in chips]
            return [copy(k, w, rows(w, (x, y, c)), rows(w, (x, y, c)), to) for k, to in enumerate(targets) for w in range(n)]
        return [copy(j, w, rows(w, (*chip, c)), rows(w, (*chip, c)), sibling)
                for j, chip in enumerate(chips) for w in range(n)]
    me = _index(_place())
    modes = (mode,) * len(srcs) if isinstance(mode, str) else mode
    copies = []
    for k in range(1, N_DEV):
        peer = _peer(k)
        for w, (src, land) in enumerate(zip(srcs, lands)):
            if modes[w] == "gather":
                r = src.shape[0]
                dst = land.at[pl.ds(pl.multiple_of(me * r, 16), r), :]
            elif modes[w] == "allgather":
                dst = land.at[me]
            else:
                r = src.shape[0] // N_DEV
                src = src.at[pl.ds(pl.multiple_of(_index(peer) * r, 16), r), :]
                dst = land.at[me]
            copies.append(pltpu.make_async_remote_copy(
                src_ref=src, dst_ref=dst, send_sem=send_sems.at[(k - 1) * len(srcs) + w],
                recv_sem=recv_sems.at[(k - 1) * len(srcs) + w],
                device_id=peer, device_id_type=MESH))
    return copies


def _exchange_start(mode, srcs, lands, name):
    n_s, n_a = len(srcs), len(srcs) + len(lands)
    n_cp = _COPIES_PER_ARRAY.get(mode, N_DEV - 1) * len(lands)

    def body(*refs):
        for cp in _peer_copies(mode, refs[:n_s], refs[n_s:n_a], refs[n_a], refs[n_a + 1]):
            cp.start()

    hbm, sem = pl.BlockSpec(memory_space=pltpu.HBM), pl.BlockSpec(memory_space=pltpu.SEMAPHORE)
    arrays = list(srcs) + list(lands)
    out = pl.pallas_call(
        body, name=name,
        out_shape=(pltpu.SemaphoreType.DMA((n_cp,)), pltpu.SemaphoreType.DMA((n_cp,)),
                   *[pltpu.HBM(a.shape, a.dtype) for a in arrays]),
        in_specs=[hbm] * n_a, out_specs=(sem, sem, *[hbm] * n_a),
        input_output_aliases={i: 2 + i for i in range(n_a)},
        compiler_params=pltpu.CompilerParams(has_side_effects=pltpu.SideEffectType.DATAFLOW_SIDE_EFFECTING),
    )(*[pltpu.with_memory_space_constraint(a, pltpu.HBM) for a in arrays])
    return out[0], out[1], out[2:2 + n_s], out[2 + n_s:2 + n_a], out[2]


_COPIES_PER_ARRAY = {"gather_ici": 4, "gather_d2d": 3}


def _exchange_wait(mode, send_sems, recv_sems, srcs, lands, after, name):
    n_s, n_a = len(srcs), len(srcs) + len(lands)

    def body(*refs):
        copies = _peer_copies(mode, refs[:n_s], refs[n_s:n_a], refs[n_a], refs[n_a + 1])
        for cp in copies:
            cp.wait_send()
        for cp in copies:
            cp.wait_recv()

    hbm, sem = pl.BlockSpec(memory_space=pltpu.HBM), pl.BlockSpec(memory_space=pltpu.SEMAPHORE)
    arrays = list(srcs) + list(lands)
    out = pl.pallas_call(
        body, name=name, out_shape=tuple(pltpu.HBM(a.shape, a.dtype) for a in arrays),
        in_specs=[hbm] * n_a + [sem, sem, pl.BlockSpec(memory_space=pl.ANY)], out_specs=tuple([hbm] * n_a),
        input_output_aliases={i: i for i in range(n_a)},
        compiler_params=pltpu.CompilerParams(has_side_effects=pltpu.SideEffectType.DATAFLOW_SIDE_EFFECTING),
    )(*arrays, send_sems, recv_sems, after)
    return out[:n_s], out[n_s:]


SMALL_WEIGHTS = ("b_ada", "g_pre_mix", "g_post_mix", "g_pre_ffn", "g_post_ffn", "w_pool", "b_pool", "pool_scale", "conv_b")


MOD_ROWS = ((0, 0), (0, 1), (1, 3), (1, 0), (1, 1), (2, 0))


def _small_sum_adam(mine, gathered, weights, moms, vels):
    n_l, n_w = len(mine), len(weights)
    d = mine[0].shape[1]

    def body(*refs):
        loc, got = refs[:n_l], refs[n_l:2 * n_l]
        w_refs, m_refs, v_refs = (refs[2 * n_l + k * n_w:2 * n_l + (k + 1) * n_w] for k in range(3))
        outs = refs[2 * n_l + 3 * n_w:]
        dmod_ref, conv_ref, loss_ref = outs[4 * n_w:]
        me = _index(_place())
        part = lambda a, dev: jnp.where(dev == me, loc[a][...], got[a][dev])
        totals = []
        for a in range(n_l):
            tot = part(a, 0)
            for dev in range(1, N_DEV):
                tot = tot + part(a, dev)
            totals.append(tot)
        t_in, t_mix, t_ffn, t_pool, t_blk, t_conv, t_loss = totals
        conv_ref[...] = t_conv
        loss_ref[...] = t_loss
        for dev in range(N_DEV):
            for k, (a, r) in enumerate(MOD_ROWS):
                dmod_ref[dev:dev + 1, k * d:(k + 1) * d] = part(a, dev)[r:r + 1, :]

        def update(idx, g, at=()):
            sel = lambda ref: ref.at[at] if at else ref
            delta, nm, nv = _adam_math(sel(w_refs[idx])[...], g, sel(m_refs[idx])[...], sel(v_refs[idx])[...])
            for k, val in enumerate((g, delta, nm, nv)):
                sel(outs[4 * idx + k])[...] = val

        tots = (t_in, t_mix, t_ffn)
        update(0, jnp.concatenate([tots[a][r:r + 1] for a, r in MOD_ROWS], axis=1))
        update(1, t_in[2:3])
        update(2, t_mix[4:5])
        update(3, t_mix[2:3])
        update(4, t_ffn[1:2])
        for gi in range(len(POOL_WINDOWS)):
            update(5, t_blk[gi], at=(0, gi))
        update(6, jnp.concatenate([t_pool[0:1, gi * HEAD_DIM:(gi + 1) * HEAD_DIM] for gi in range(len(POOL_WINDOWS))], axis=0),
               at=(0,))
        update(7, t_pool[1:2])
        update(8, t_conv[3:4])

    vmem = pl.BlockSpec(memory_space=pltpu.VMEM)
    out = pl.pallas_call(
        body, name="small_sum_adam", in_specs=[vmem] * (2 * n_l + 3 * n_w), out_specs=[vmem] * (4 * n_w + 3),
        out_shape=[jax.ShapeDtypeStruct(w.shape, F32) for w in weights for _ in range(4)]
        + [jax.ShapeDtypeStruct((N_DEV, 6 * d), F32), jax.ShapeDtypeStruct(mine[5].shape, F32),
           jax.ShapeDtypeStruct(mine[6].shape, F32)],
        compiler_params=_params(),
    )(*mine, *gathered, *weights, *moms, *vels)
    return out[:4 * n_w], out[4 * n_w], out[4 * n_w + 1], out[4 * n_w + 2]


def _adam_math(w, g, m, v):
    m = ADAM_B1 * m + (1.0 - ADAM_B1) * g
    v = ADAM_B2 * v + (1.0 - ADAM_B2) * (g * g)
    m_hat = m / (1.0 - ADAM_B1 ** ADAM_STEP)
    v_hat = v / (1.0 - ADAM_B2 ** ADAM_STEP)
    delta = -ADAM_LR * (m_hat / (jnp.sqrt(v_hat) + ADAM_EPS) + ADAM_WD * w)
    return delta, m, v


def _adam(w, g, m, v, name):
    def body(w_ref, g_ref, m_ref, v_ref, d_ref, nm_ref, nv_ref):
        d_ref[...], nm_ref[...], nv_ref[...] = _adam_math(w_ref[...], g_ref[...], m_ref[...], v_ref[...])

    vmem = pl.BlockSpec(memory_space=pltpu.VMEM)
    return pl.pallas_call(
        body, name=name, in_specs=[vmem] * 4, out_specs=[vmem] * 3,
        out_shape=[jax.ShapeDtypeStruct(w.shape, F32)] * 3, compiler_params=_params(),
    )(w, g, m, v)


def _sum_adam(own, parts, w, m, v, me, name, tr):
    _, rows, cols = parts.shape
    turned = w.shape == (cols, rows) and rows != cols
    assert tr == rows or not turned
    n_t = rows // tr

    def body(me_ref, own_ref, p_ref, w_ref, m_ref, v_ref, g_ref, d_ref, nm_ref, nv_ref):
        part = lambda dev: jnp.where(dev == me_ref[0], own_ref[...], p_ref[dev]).astype(F32)
        g = part(0)
        for dev in range(1, N_DEV):
            g = g + part(dev)
        g = g.T if turned else g
        g_ref[...] = g
        d_ref[...], nm_ref[...], nv_ref[...] = _adam_math(w_ref[...], g, m_ref[...], v_ref[...])

    spec = pl.BlockSpec((cols, rows) if turned else (tr, cols), lambda i, me_ref: (i, 0))
    shape = jax.ShapeDtypeStruct(w.shape, F32)
    return pl.pallas_call(
        body, name=name, out_shape=[shape] * 4,
        grid_spec=pltpu.PrefetchScalarGridSpec(
            num_scalar_prefetch=1, grid=(n_t,),
            in_specs=[pl.BlockSpec((tr, cols), lambda i, me_ref: (me_ref[0] * n_t + i, 0)),
                      pl.BlockSpec((N_DEV, tr, cols), lambda i, me_ref: (0, i, 0)), spec, spec, spec],
            out_specs=[spec] * 4),
        compiler_params=_params(("arbitrary",)),
    )(me.reshape(1).astype(jnp.int32), own, parts, w, m, v)


def _ada_grad_adam(c_all, dmod_all, w, m, v, tr):
    rows, cols = w.shape

    def body(c_ref, dm_ref, w_ref, m_ref, v_ref, g_ref, d_ref, nm_ref, nv_ref):
        cv = c_ref[...]
        act = cv * jax.nn.sigmoid(cv)
        dmod = dm_ref[:, pl.ds(pl.multiple_of(_index(_place()) * cols, LANES), cols)]
        g = lax.dot_general(act, dmod, TN, preferred_element_type=F32, precision=lax.Precision.HIGHEST)
        g_ref[...] = g
        d_ref[...], nm_ref[...], nv_ref[...] = _adam_math(w_ref[...], g, m_ref[...], v_ref[...])

    spec = pl.BlockSpec((tr, cols), lambda i: (i, 0))
    shape = jax.ShapeDtypeStruct((rows, cols), F32)
    return pl.pallas_call(
        body, name="ada_grad_adam", grid=(rows // tr,),
        in_specs=[pl.BlockSpec((N_DEV, tr), lambda i: (0, i)), pl.BlockSpec(dmod_all.shape, lambda i: (0, 0)), spec, spec, spec],
        out_specs=[spec] * 4, out_shape=[shape] * 4, compiler_params=_params(("arbitrary",)),
    )(c_all, dmod_all, w, m, v)


def _rope_tables(positions):
    inv_freq = ROPE_THETA ** (-jnp.arange(0, 2 * ROT_HALF, 2, dtype=F32) / (2 * ROT_HALF))
    ang = inv_freq[:, None] * positions.astype(F32)[None, :]
    rows = jnp.concatenate([jnp.cos(ang), jnp.sin(ang), jnp.ones_like(ang)], axis=0)
    spread = [[[0.0] * LANES for _ in range(3 * ROT_HALF)] for _ in range(3)]
    for lane in range(LANES):
        p, j = lane % HEAD_DIM, lane % ROT_HALF
        if p < ROT_HALF:
            spread[0][j][lane] = 1.0
            spread[1][ROT_HALF + j][lane] = -1.0
        elif p < 2 * ROT_HALF:
            spread[0][j][lane] = 1.0
            spread[2][ROT_HALF + j][lane] = 1.0
        else:
            spread[0][2 * ROT_HALF][lane] = 1.0
    return rows, jnp.array(spread, F32)


def _pad_rows(a, rows):
    return jnp.pad(a, ((0, rows - a.shape[0]), (0, 0)))


def _sequence_step(xs, target, rope, mods, gains, w_in_t, w_out_t, relay_ffn, fetch_ffn, send_grads, w_blk_b, b_pool_r,
                   pool_scale_r, conv_w_all, conv_b, after):
    sh_m, sc_m, gt_m, sh_f, sc_f, gt_f = mods
    g_pre_mix, g_post_mix, g_pre_ffn, g_post_ffn = gains
    h1, u_pool, qkv = _premix_inproj(xs, sh_m, sc_m, g_pre_mix, w_in_t, rope, after, tm=512)
    o_g, lse_g = _attn_fwd(qkv)
    x1, y1, h2, cat, attn, lse_all = _mix_out(xs, u_pool, o_g, lse_g, w_blk_b, b_pool_r, pool_scale_r, w_out_t,
                                              gt_m, g_post_mix, g_pre_ffn, sc_f, sh_f, tm=512)
    relay_ffn(x1)
    w_up_t, w_down_f = fetch_ffn(x1)
    gate, a_ffn, act, vd, dy2, dout, sums_ffn, loss_loc = _ffn_fwd_loss(h2, x1, target, w_up_t, w_down_f, conv_w_all, conv_b,
                                                              gt_f, g_post_ffn, tm=256, ck=256)

    dgc, dval, dw_down, dconv = _ffn_bwd_act(dy2, gate, a_ffn, act, vd, w_down_f, tm=512, tf=1408, ck=256)
    token = send_grads("down", [dw_down], [])
    dup, dh2 = _ffn_bwd_up(dgc, dval, w_up_t, conv_w_all, token, tm=512)
    dw_up_t = _wgrad(dup, h2, "wgrad_up", tk=2048, tmm=1408)
    token = send_grads("up", [dw_up_t], [])
    dx1, dpool, dattn, delta, dw_out_t, sums_mix = _mix_bwd(dh2, dout, x1, y1, cat, attn, w_out_t, sc_f,
                                                           g_pre_ffn, gt_m, g_post_mix, token, tm=512)
    du, dw_blk, sums_pool = _pool_bwd(dpool, u_pool, w_blk_b, b_pool_r, pool_scale_r, tm=1024)
    token = send_grads("out", [dw_out_t], [sums_mix, sums_ffn, sums_pool, dw_blk, dconv, loss_loc])
    dproj, dw_in_t = _dproj_wgrad_in(du, _attn_bwd(qkv, dattn, lse_all, delta, token), rope, h1, tm=512, cm=512)
    token = send_grads("in", [dw_in_t], [])
    grad_x, sums_in = _inproj_bwd(dproj, w_in_t, xs, dx1, sc_m, g_pre_mix, token, tm=512)
    return (loss_loc, grad_x, dw_in_t, dw_out_t, dw_up_t, dw_down, dw_blk, dconv,
            sums_in, sums_mix, sums_ffn, sums_pool)


def kernel(x, c, positions, w_ada, b_ada, g_pre_mix, g_post_mix, g_pre_ffn, g_post_ffn, w_in, w_pool, b_pool, pool_scale, w_out, w_up, conv_w, conv_b, w_down, loss_target, m_w_ada, m_b_ada, m_g_pre_mix, m_g_post_mix, m_g_pre_ffn, m_g_post_ffn, m_w_in, m_w_pool, m_b_pool, m_pool_scale, m_w_out, m_w_up, m_conv_w, m_conv_b, m_w_down, v_w_ada, v_b_ada, v_g_pre_mix, v_g_post_mix, v_g_pre_ffn, v_g_post_ffn, v_w_in, v_w_pool, v_b_pool, v_pool_scale, v_w_out, v_w_up, v_conv_w, v_conv_b, v_w_down):
    s_len, d = x.shape[1], x.shape[2]
    d_ff = w_down.shape[1] * N_DEV
    me = _index(_place())
    xs, target = x[0], loss_target[0]

    c_all, mod, taps_all, (w_in_t, w_out_t), lands = _entry_exchange(
        jnp.broadcast_to(c, (8, d)), w_ada[0], b_ada, _pad_rows(conv_w[0], 8),
        [w_in[0].T.astype(BF16), w_out[0].T.astype(BF16)], [w_up[0].T, w_down[0]])
    c_all = c_all[:, 0, :]
    conv_w_all = jnp.transpose(taps_all[:, :3, :], (1, 0, 2)).reshape(3, d_ff)
    sh_m, sc_m, gt_m, sh_f, sc_f, gt_f = [mod[:, 0, :].reshape(1, -1)[:, k * d:(k + 1) * d] for k in range(6)]

    rope = _rope_tables(positions[0])
    w_blk = jnp.zeros((256, 256), F32)
    for gi in range(4):
        w_blk = lax.dynamic_update_slice(w_blk, w_pool[0, gi], (gi * HEAD_DIM, gi * HEAD_DIM))
    w_blk_b = w_blk.astype(BF16)
    b_pool_r, pool_scale_r = b_pool.reshape(1, 256), pool_scale.reshape(1, 256)

    w_send, w_recv, w_src, w_land, w_token = _exchange_start("gather_ici", [], lands, "ffn_weights_ici_start")
    relay = []

    def relay_ffn(after):
        _, blocks = _exchange_wait("gather_ici", w_send, w_recv, w_src, w_land, after, "ffn_weights_ici_wait")
        relay.extend(_exchange_start("gather_d2d", [], blocks, "ffn_weights_d2d_start"))

    def fetch_ffn(after):
        return _exchange_wait("gather_d2d", relay[0], relay[1], [], relay[3], after, "ffn_weights_d2d_wait")[1]

    flights = {}

    def send_grads(tag, slabs, whole):
        lands = [lax.empty((N_DEV, g.shape[0] // N_DEV, g.shape[1]), g.dtype) for g in slabs]
        lands += [lax.empty((N_DEV,) + a.shape, F32) for a in whole]
        modes = ("scatter",) * len(slabs) + ("allgather",) * len(whole)
        flights[tag] = (modes, *_exchange_start(modes, slabs + whole, lands, f"grads_{tag}_start"))
        return flights[tag][5]

    def arrived(tag, after):
        return _exchange_wait(*flights[tag][:5], after, f"grads_{tag}_wait")

    _, grad_x, *_, sums_in, _, _, _ = _sequence_step(
        xs, target, rope, (sh_m, sc_m, gt_m, sh_f, sc_f, gt_f), (g_pre_mix, g_post_mix, g_pre_ffn, g_post_ffn),
        w_in_t, w_out_t, relay_ffn, fetch_ffn, send_grads, w_blk_b, b_pool_r, pool_scale_r, conv_w_all, conv_b,
        w_token)

    send_grads("last", [], [sums_in])

    (own_down,), (parts_down,) = arrived("down", flights["last"][5])
    new_down = _sum_adam(own_down, parts_down, w_down[0], m_w_down[0], v_w_down[0], me, "adam_w_down", 176)
    (own_up,), (parts_up,) = arrived("up", new_down[0])
    new_up = _sum_adam(own_up, parts_up, w_up[0].T, m_w_up[0].T, v_w_up[0].T, me, "adam_w_up", 176)
    (own_out, *small), (parts_out, *gathered) = arrived("out", new_up[0])
    new_out = _sum_adam(own_out, parts_out, w_out[0], m_w_out[0], v_w_out[0], me, "adam_w_out", 128)
    (own_in,), (parts_in,) = arrived("in", new_out[0])
    new_in = _sum_adam(own_in, parts_in, w_in[0].T, m_w_in[0].T, v_w_in[0].T, me, "adam_w_in", 160)
    big = {"w_up": [a.T for a in new_up], "w_down": new_down, "w_out": new_out, "w_in": [a.T for a in new_in]}

    rep_w = [b_ada, g_pre_mix, g_post_mix, g_pre_ffn, g_post_ffn, w_pool, b_pool, pool_scale, conv_b]
    rep_m = [m_b_ada, m_g_pre_mix, m_g_post_mix, m_g_pre_ffn, m_g_post_ffn, m_w_pool, m_b_pool, m_pool_scale, m_conv_b]
    rep_v = [v_b_ada, v_g_pre_mix, v_g_post_mix, v_g_pre_ffn, v_g_post_ffn, v_w_pool, v_b_pool, v_pool_scale, v_conv_b]
    mine_last, got_last = arrived("last", new_in[0])
    small, gathered = [*mine_last, *small], [*got_last, *gathered]
    rep_out, dmod_all, dconv_tot, loss_tot = _small_sum_adam(small, gathered, rep_w, rep_m, rep_v)
    g_rep, d_rep, nm_rep, nv_rep = (rep_out[k::4] for k in range(4))

    fcol = d_ff // N_DEV
    taps = lambda a: jnp.transpose(a, (1, 0, 2))
    g_cw = lax.dynamic_slice(dconv_tot, (0, me * fcol), (3, fcol))[None]
    d_cw, nm_cw, nv_cw = [taps(a) for a in _adam(taps(conv_w), taps(g_cw), taps(m_conv_w), taps(v_conv_w), "adam_conv_w")]

    g_ada, d_ada, nm_ada, nv_ada = _ada_grad_adam(c_all, dmod_all, w_ada[0], m_w_ada[0], v_w_ada[0], 128)

    loss = loss_tot[0, 0]

    def group(k):
        rep = (g_rep, d_rep, nm_rep, nv_rep)[k]
        ada = (g_ada, d_ada, nm_ada, nv_ada)[k][None]
        cw = (g_cw, d_cw, nm_cw, nv_cw)[k]
        return [ada, rep[0], rep[1], rep[2], rep[3], rep[4], big["w_in"][k][None], rep[5], rep[6], rep[7],
                big["w_out"][k][None], big["w_up"][k][None], cw, rep[8], big["w_down"][k][None]]

    return (loss, grad_x[None], *group(0), *group(1), *group(2), *group(3))
```

```python
import functools
import math

import jax
import jax.numpy as jnp
from jax import lax
from jax.experimental import pallas as pl
from jax.experimental.pallas import tpu as pltpu

F32 = jnp.float32
BF16 = jnp.bfloat16
MESH = pl.DeviceIdType.MESH

N_DEV = 8
HEAD_DIM = 64
ROT_HALF = 8
ROPE_THETA = 500000.0
POOL_WINDOWS = (2, 4, 8, 16)
DILATIONS = (1, 4, 16)
BLOCK = 128
NORM_EPS = 1e-6
HALO = 16
MASKED = -1e30
ATTN_FWD_UNROLL = 8
ATTN_BWD_UNROLL = 8

ADAM_LR = 0.001
ADAM_B1 = 0.9
ADAM_B2 = 0.999
ADAM_EPS = 1e-08
ADAM_WD = 0.01
ADAM_STEP = 10

V7X_VMEM_LIMIT = 56 * 1024 * 1024
LANES = 128

NT = (((1,), (1,)), ((), ()))
NN = (((1,), (0,)), ((), ()))
TN = (((0,), (0,)), ((), ()))


def _dot(a, b, dims):
    return lax.dot_general(a, b, dims, preferred_element_type=F32)


def _params(sem=None, vmem=V7X_VMEM_LIMIT):
    if sem is None:
        return pltpu.CompilerParams(vmem_limit_bytes=vmem)
    return pltpu.CompilerParams(dimension_semantics=sem, vmem_limit_bytes=vmem)


def _rstd(v):
    return lax.rsqrt(jnp.mean(v * v, axis=-1, keepdims=True) + NORM_EPS)


def _norm_bwd(dn, n, rstd):
    return rstd * (dn - n * jnp.mean(dn * n, axis=-1, keepdims=True))


def _rope_lanes(cs_ref, spread_ref):
    return [lax.dot_general(cs_ref[...], spread_ref[k], TN, preferred_element_type=F32, precision=lax.Precision.HIGHEST)
            for k in range(3)]


def _rope_fwd(p, lanes):
    return p * lanes[0] + pltpu.roll(p, LANES - ROT_HALF, 1) * lanes[1] + pltpu.roll(p, ROT_HALF, 1) * lanes[2]


def _rope_bwd(dp, lanes):
    return dp * lanes[0] + pltpu.roll(dp * lanes[1], ROT_HALF, 1) + pltpu.roll(dp * lanes[2], LANES - ROT_HALF, 1)


def _gelu_parts(v):
    k2 = 2.0 * math.sqrt(2.0 / math.pi)
    c = 0.044715
    v2 = v * v
    s = jax.nn.sigmoid(v * (k2 + (k2 * c) * v2))
    g = v * s
    dg = s + g * (1.0 - s) * (k2 + (3.0 * k2 * c) * v2)
    return g, dg


def _halo_before(i, tile):
    return jnp.maximum(i * (tile // HALO) - 1, 0)


def _premix_inproj(x, sh, sc, g, w_in_t, rope, after, tm):
    s_len, d = x.shape
    n_proj = w_in_t.shape[0]
    n_slab = (n_proj - 256) // LANES

    def body(x_ref, sh_ref, sc_ref, g_ref, w_ref, cs_ref, spread_ref, after_ref, h_ref, up_ref, qkv_ref):
        xv = x_ref[...]
        h = (xv * _rstd(xv) * g_ref[...]) * (1.0 + sc_ref[...]) + sh_ref[...]
        hb = h.astype(BF16)
        h_ref[...] = hb
        up_ref[...] = _dot(hb, w_ref[0:256, :], NT)
        lanes = _rope_lanes(cs_ref, spread_ref)
        for pair in range(n_slab // 2):
            p = _dot(hb, w_ref[256 + 256 * pair:512 + 256 * pair, :], NT)
            for half in range(2):
                ph = p[:, half * LANES:(half + 1) * LANES]
                if pair < 6:
                    ph = _rope_fwd(ph, lanes)
                if pair < 3:
                    ph = ph * (HEAD_DIM ** -0.5)
                qkv_ref[2 * pair + half] = ph

    vec = pl.BlockSpec((1, d), lambda i: (0, 0))
    return pl.pallas_call(
        body, name="premix_inproj", grid=(s_len // tm,),
        in_specs=[pl.BlockSpec((tm, d), lambda i: (i, 0)), vec, vec, vec,
                  pl.BlockSpec((n_proj, d), lambda i: (0, 0)),
                  pl.BlockSpec((rope[0].shape[0], tm), lambda i: (0, i)), pl.BlockSpec(rope[1].shape, lambda i: (0, 0, 0)),
                  pl.BlockSpec(memory_space=pl.ANY)],
        out_specs=[pl.BlockSpec((tm, d), lambda i: (i, 0)),
                   pl.BlockSpec((tm, 256), lambda i: (i, 0)),
                   pl.BlockSpec((n_slab, tm, LANES), lambda i: (0, i, 0))],
        out_shape=[jax.ShapeDtypeStruct((s_len, d), BF16),
                   jax.ShapeDtypeStruct((s_len, 256), F32),
                   jax.ShapeDtypeStruct((n_slab, s_len, LANES), F32)],
        compiler_params=_params(("arbitrary",)),
    )(x, sh, sc, g, w_in_t, *rope, after)


def _block_rows(n, r, dil):
    start = n * (BLOCK * dil) + r
    if dil == 1:
        return pl.ds(pl.multiple_of(start, BLOCK), BLOCK)
    return pl.ds(start, BLOCK, stride=dil)


def _band_mask(n):
    ri = lax.broadcasted_iota(jnp.int32, (BLOCK, 2 * BLOCK), 0)
    cj = lax.broadcasted_iota(jnp.int32, (BLOCK, 2 * BLOCK), 1)
    cur = (cj >= BLOCK) & (cj - BLOCK <= ri)
    prev = (cj < BLOCK) & (cj >= ri) & (n > 0)
    return cur | prev


def _attn_fwd(qkv):
    s_len = qkv.shape[1]
    n_g = len(DILATIONS)

    def body(q_ref, k_ref, v_ref, o_ref, lse_ref):
        lane = lax.broadcasted_iota(jnp.int32, (BLOCK, LANES), 1)
        first = lane < HEAD_DIM

        def group(dil):
            nb = s_len // (BLOCK * dil)

            def block(t, carry):
                r, n = t // nb, t % nb
                cur = _block_rows(n, r, dil)
                prev = _block_rows(jnp.maximum(n - 1, 0), r, dil)
                q = q_ref[0, cur, :]
                kcat = jnp.concatenate([k_ref[0, prev, :], k_ref[0, cur, :]], axis=0).astype(BF16)
                vcat = jnp.concatenate([v_ref[0, prev, :], v_ref[0, cur, :]], axis=0).astype(BF16)
                valid = _band_mask(n)
                q2 = jnp.concatenate([jnp.where(first, q, 0.0), jnp.where(first, 0.0, q)], axis=0).astype(BF16)
                s = jnp.where(jnp.concatenate([valid, valid], axis=0), _dot(q2, kcat, NT), MASKED)
                m = jnp.max(s, axis=-1, keepdims=True)
                p = jnp.exp(s - m)
                den = jnp.sum(p, axis=-1, keepdims=True)
                o2 = _dot(p.astype(BF16), vcat, NN) / den
                lse2 = m + jnp.log(den)
                o_ref[0, 0, cur, :] = jnp.where(first, o2[:BLOCK], o2[BLOCK:])
                lse_ref[0, 0, cur, :] = jnp.where(first, lse2[:BLOCK], lse2[BLOCK:])
                return carry

            lax.fori_loop(0, nb * dil, block, 0, unroll=ATTN_FWD_UNROLL)

        for gi, dil in enumerate(DILATIONS):
            pl.when(pl.program_id(0) == gi)(functools.partial(group, dil))

    def slab(base):
        return pl.BlockSpec((1, s_len, LANES), lambda g, s: (base + 2 * g + s, 0, 0))

    out = pl.BlockSpec((1, 1, s_len, LANES), lambda g, s: (g, s, 0, 0))
    shape = jax.ShapeDtypeStruct((n_g, 2, s_len, LANES), F32)
    return pl.pallas_call(
        body, name="attn_fwd", grid=(n_g, 2),
        in_specs=[slab(0), slab(6), slab(12)], out_specs=[out, out], out_shape=[shape, shape],
        compiler_params=_params(("arbitrary", "arbitrary")),
    )(qkv, qkv, qkv)


def _pool_mixed(u, halo, i, tm):
    ue = jnp.concatenate([halo, u], axis=0)
    s2 = ue + pltpu.roll(ue, 1, 0)
    s4 = s2 + pltpu.roll(s2, 2, 0)
    s8 = s4 + pltpu.roll(s4, 4, 0)
    s16 = s8 + pltpu.roll(s8, 8, 0)
    grp = lax.broadcasted_iota(jnp.int32, (tm, 256), 1) // HEAD_DIM
    pick = lambda a, b, c, e: jnp.where(grp == 0, a, jnp.where(grp == 1, b, jnp.where(grp == 2, c, e)))
    win_sum = pick(s2[HALO:], s4[HALO:], s8[HALO:], s16[HALO:])
    pos = (i * tm + lax.broadcasted_iota(jnp.int32, (tm, 256), 0)).astype(F32)
    count = jnp.minimum(pos + 1.0, pick(*[float(w) for w in POOL_WINDOWS]))
    return win_sum / count - u, count


def _mix_out(x, u_pool, o_g, lse_g, w_blk, b_pool, pool_scale, w_out_t, gt_m, g_post_mix, g_pre_ffn, sc_f, sh_f, tm):
    s_len, d = x.shape

    def body(x_ref, u_ref, uh_ref, o_ref, l_ref, wb_ref, bp_ref, ps_ref, wo_ref,
             gt_ref, g1_ref, g2_ref, sc_ref, sh_ref,
             x1_ref, y1_ref, h2_ref, cat_ref, attn_ref, lall_ref):
        (o0, o1, o2), (l0, l1, l2) = (o_ref.at[g] for g in range(3)), (l_ref.at[g] for g in range(3))
        i = pl.program_id(0)
        u = u_ref[...]
        halo = uh_ref[...] * (i > 0).astype(F32)
        mixed, _ = _pool_mixed(u, halo, i, tm)
        y = _dot(mixed.astype(BF16), wb_ref[...], NN) + bp_ref[...]
        pool = y * ps_ref[...]
        attn = []
        for s in range(2):
            la, lb, lc = l0[s], l1[s], l2[s]
            mx = jnp.maximum(jnp.maximum(la, lb), lc)
            ea, eb, ec = jnp.exp(la - mx), jnp.exp(lb - mx), jnp.exp(lc - mx)
            den = ea + eb + ec
            lall_ref[s] = mx + jnp.log(den)
            attn.append((ea / den) * o0[s] + (eb / den) * o1[s] + (ec / den) * o2[s])
        attn = jnp.concatenate(attn, axis=1)
        attn_ref[...] = attn
        cat = jnp.concatenate([pool, attn], axis=1).astype(BF16)
        cat_ref[...] = cat
        y1 = _dot(cat, wo_ref[...], NT)
        y1_ref[...] = y1.astype(BF16)
        x1 = x_ref[...] + gt_ref[...] * (y1 * _rstd(y1) * g1_ref[...])
        x1_ref[...] = x1
        h2 = (x1 * _rstd(x1) * g2_ref[...]) * (1.0 + sc_ref[...]) + sh_ref[...]
        h2_ref[...] = h2.astype(BF16)

    tile = lambda w: pl.BlockSpec((tm, w), lambda i: (i, 0))
    slab = pl.BlockSpec((2, tm, LANES), lambda i: (0, i, 0))
    groups = pl.BlockSpec((len(DILATIONS), 2, tm, LANES), lambda i: (0, 0, i, 0))
    const = lambda a: pl.BlockSpec(a.shape, lambda i: (0,) * a.ndim)
    return pl.pallas_call(
        body, name="mix_out", grid=(s_len // tm,),
        in_specs=[tile(d), tile(256), pl.BlockSpec((HALO, 256), lambda i: (_halo_before(i, tm), 0)),
                  groups, groups,
                  const(w_blk), const(b_pool), const(pool_scale), const(w_out_t),
                  const(gt_m), const(g_post_mix), const(g_pre_ffn), const(sc_f), const(sh_f)],
        out_specs=[tile(d), tile(d), tile(d), tile(512), tile(256), slab],
        out_shape=[jax.ShapeDtypeStruct((s_len, d), F32), jax.ShapeDtypeStruct((s_len, d), BF16),
                   jax.ShapeDtypeStruct((s_len, d), BF16), jax.ShapeDtypeStruct((s_len, 512), BF16),
                   jax.ShapeDtypeStruct((s_len, 256), F32), jax.ShapeDtypeStruct((2, s_len, LANES), F32)],
        compiler_params=_params(("arbitrary",)),
    )(x, u_pool, u_pool, o_g, lse_g, w_blk, b_pool, pool_scale, w_out_t, gt_m, g_post_mix, g_pre_ffn, sc_f, sh_f)


def _conv_gate(gate_ext, cw, cb):
    gc = gate_ext * cw[2:3, :] + pltpu.roll(gate_ext, 1, 0) * cw[1:2, :] + pltpu.roll(gate_ext, 2, 0) * cw[0:1, :]
    return gc[HALO:] + cb


def _ffn_fwd_loss(h2, x1, target, w_up_t, w_down, conv_w, conv_b, gt_f, g_post_ffn, tm, ck):
    s_len, d = x1.shape
    d_ff = w_down.shape[0]
    n_t, n_c = s_len // tm, d_ff // ck

    def body(h_ref, hh_ref, x1_ref, tgt_ref, wg_ref, wv_ref, wd_ref, cw_ref, cb_ref, gt_ref, g_ref,
             gate_ref, a_ref, act_ref, vd_ref, dy2_ref, dout_ref, sums_ref, loss_ref, acc_ref):
        i = pl.program_id(0)

        @pl.when(i == 0)
        def _():
            sums_ref[...] = jnp.zeros_like(sums_ref)
            loss_ref[...] = jnp.zeros_like(loss_ref)
            acc_ref[...] = jnp.zeros_like(acc_ref)

        def finish(live):
            y2 = acc_ref[...]
            rstd = _rstd(y2)
            n = y2 * rstd
            rn = n * g_ref[...]
            err = x1_ref[...] + gt_ref[...] * rn - tgt_ref[...]
            keep = lambda v: jnp.where(live, v, 0.0)
            loss_ref[...] += keep(0.5 * jnp.sum(jnp.mean(err * err, axis=-1, keepdims=True), axis=0, keepdims=True))
            dout = err * (1.0 / d)
            dout_ref[...] = dout.astype(BF16)
            drn = dout * gt_ref[...]
            sums_ref[0:1, :] += keep(jnp.sum(dout * rn, axis=0, keepdims=True))
            sums_ref[1:2, :] += keep(jnp.sum(drn * n, axis=0, keepdims=True))
            dy2_ref[...] = _norm_bwd(drn * g_ref[...], n, rstd).astype(BF16)

        @pl.when(i < n_t)
        def _():
            h = h_ref[...]
            h_ext = jnp.concatenate([hh_ref[...], h], axis=0)
            row = lax.broadcasted_iota(jnp.int32, (tm + HALO, ck), 0)
            no_halo = (row < HALO) & (i == 0)

            def up(c):
                cs = slice(c * ck, (c + 1) * ck)
                return jnp.where(no_halo, 0.0, _dot(h_ext, wg_ref[cs, :], NT)), _dot(h, wv_ref[cs, :], NT)

            part = None
            nxt = up(0)
            finish(i > 0)
            for c in range(n_c):
                cs = slice(c * ck, (c + 1) * ck)
                gate_ext, val = nxt
                if c + 1 < n_c:
                    nxt = up(c + 1)
                act, dact = _gelu_parts(_conv_gate(gate_ext, cw_ref[:, cs], cb_ref[:, cs]))
                a = (act * val).astype(BF16)
                gate_ref[:, cs] = gate_ext[HALO:].astype(BF16)
                a_ref[:, cs] = a
                act_ref[:, cs] = act.astype(BF16)
                vd_ref[:, cs] = (val * dact).astype(BF16)
                p = _dot(a, wd_ref[cs, :], NN)
                part = p if part is None else part + p
            acc_ref[...] = part

        @pl.when(i == n_t)
        def _():
            finish(True)

    this = lambda i: jnp.minimum(i, n_t - 1)
    before = lambda i: jnp.maximum(i - 1, 0)
    tok = lambda w, at: pl.BlockSpec((tm, w), lambda i: (at(i), 0))
    vec = pl.BlockSpec((1, d), lambda i: (0, 0))
    once = lambda shape, imap: pl.BlockSpec(shape, imap, pipeline_mode=pl.Buffered(1))
    return pl.pallas_call(
        body, name="ffn_fwd_loss", grid=(n_t + 1,),
        in_specs=[tok(d, this), pl.BlockSpec((HALO, d), lambda i: (_halo_before(this(i), tm), 0)),
                  tok(d, before), tok(d, before),
                  once((d_ff, d), lambda i: (0, 0)), once((d_ff, d), lambda i: (1, 0)), once((d_ff, d), lambda i: (0, 0)),
                  pl.BlockSpec((3, d_ff), lambda i: (0, 0)), pl.BlockSpec((1, d_ff), lambda i: (0, 0)), vec, vec],
        out_specs=[tok(d_ff, this)] * 4 + [tok(d, before), tok(d, before), pl.BlockSpec((8, d), lambda i: (0, 0)),
                                          pl.BlockSpec((8, LANES), lambda i: (0, 0))],
        out_shape=[jax.ShapeDtypeStruct((s_len, d_ff), BF16)] * 4
        + [jax.ShapeDtypeStruct((s_len, d), BF16), jax.ShapeDtypeStruct((s_len, d), BF16),
           jax.ShapeDtypeStruct((8, d), F32), jax.ShapeDtypeStruct((8, LANES), F32)],
        scratch_shapes=[pltpu.VMEM((tm, d), F32)],
        compiler_params=_params(("arbitrary",)),
    )(h2, h2, x1, target, w_up_t, w_up_t, w_down, conv_w, conv_b, gt_f, g_post_ffn)


def _ffn_bwd_act(dy2, gate, a, act, vd, w_down, tm, tf, ck):
    s_len, d = dy2.shape
    d_ff = w_down.shape[0]
    n_t = s_len // tm
    chunks = [slice(lo, min(lo + ck, tf)) for lo in range(0, tf, ck)]

    def body(dy_ref, g_ref, gh_ref, a_ref, act_ref, vd_ref, wd_ref, dgc_ref, dval_ref, dwd_ref, dconv_ref, acc_ref):
        i = pl.program_id(1)

        @pl.when(i == 0)
        def _():
            acc_ref[...] = jnp.zeros_like(acc_ref)
            dconv_ref[...] = jnp.zeros_like(dconv_ref)

        dy = dy_ref[...]

        def down(cs):
            return _dot(dy, wd_ref[cs, :], NT)

        nxt = down(chunks[0])
        for c, cs in enumerate(chunks):
            width = cs.stop - cs.start
            da = nxt
            if c + 1 < len(chunks):
                nxt = down(chunks[c + 1])
            acc_ref[cs, :] += _dot(a_ref[:, cs], dy, TN)
            row = lax.broadcasted_iota(jnp.int32, (tm + HALO, width), 0)
            gate_ext = jnp.where((row < HALO) & (i == 0), 0.0,
                                 jnp.concatenate([gh_ref[:, cs], g_ref[:, cs]], axis=0).astype(F32))
            dgc = da * vd_ref[:, cs].astype(F32)
            dgc_ref[:, cs] = dgc.astype(BF16)
            dval_ref[:, cs] = (da * act_ref[:, cs].astype(F32)).astype(BF16)
            rows = [jnp.sum(dgc * pltpu.roll(gate_ext, 2 - k, 0)[HALO:], axis=0, keepdims=True) for k in range(2)]
            rows += [jnp.sum(dgc * gate_ext[HALO:], axis=0, keepdims=True), jnp.sum(dgc, axis=0, keepdims=True),
                     jnp.zeros((4, width), F32)]
            dconv_ref[:, cs] += jnp.concatenate(rows, axis=0)

        @pl.when(i == n_t - 1)
        def _():
            dwd_ref[...] = acc_ref[...].astype(BF16)

    tokf = pl.BlockSpec((tm, tf), lambda j, i: (i, j))
    return pl.pallas_call(
        body, name="ffn_bwd_act", grid=(d_ff // tf, n_t),
        in_specs=[pl.BlockSpec((tm, d), lambda j, i: (i, 0)), tokf,
                  pl.BlockSpec((HALO, tf), lambda j, i: (_halo_before(i, tm), j)), tokf, tokf, tokf,
                  pl.BlockSpec((tf, d), lambda j, i: (j, 0))],
        out_specs=[tokf, tokf, pl.BlockSpec((tf, d), lambda j, i: (j, 0)), pl.BlockSpec((8, tf), lambda j, i: (0, j))],
        out_shape=[jax.ShapeDtypeStruct((s_len, d_ff), BF16), jax.ShapeDtypeStruct((s_len, d_ff), BF16),
                   jax.ShapeDtypeStruct((d_ff, d), BF16), jax.ShapeDtypeStruct((8, d_ff), F32)],
        scratch_shapes=[pltpu.VMEM((tf, d), F32)],
        compiler_params=_params(("arbitrary", "arbitrary")),
    )(dy2, gate, gate, a, act, vd, w_down)


def _ffn_bwd_up(dgc, dval, w_up_t, conv_w, after, tm):
    s_len, d_ff = dgc.shape
    d = w_up_t.shape[1]
    n_t = s_len // tm

    def body(dg_ref, dgn_ref, dv_ref, cw_ref, w_ref, after_ref, dup_ref, dh_ref):
        i = pl.program_id(0)
        nxt = dgn_ref[...].astype(F32) * (i < n_t - 1).astype(F32)
        ext = jnp.concatenate([dg_ref[...].astype(F32), nxt], axis=0)
        rows = tm + HALO
        dgate = (ext * cw_ref[2:3, :] + pltpu.roll(ext, rows - 1, 0) * cw_ref[1:2, :]
                 + pltpu.roll(ext, rows - 2, 0) * cw_ref[0:1, :])[:tm]
        dup = jnp.concatenate([dgate.astype(BF16), dv_ref[...]], axis=1)
        dup_ref[...] = dup
        dh_ref[...] = _dot(dup, w_ref[...], NN).astype(BF16)

    tokf = pl.BlockSpec((tm, d_ff), lambda i: (i, 0))
    return pl.pallas_call(
        body, name="ffn_bwd_up", grid=(n_t,),
        in_specs=[tokf, pl.BlockSpec((HALO, d_ff), lambda i: (jnp.minimum((i + 1) * (tm // HALO), s_len // HALO - 1), 0)),
                  tokf, pl.BlockSpec((3, d_ff), lambda i: (0, 0)), pl.BlockSpec((2 * d_ff, d), lambda i: (0, 0)),
                  pl.BlockSpec(memory_space=pl.ANY)],
        out_specs=[pl.BlockSpec((tm, 2 * d_ff), lambda i: (i, 0)), pl.BlockSpec((tm, d), lambda i: (i, 0))],
        out_shape=[jax.ShapeDtypeStruct((s_len, 2 * d_ff), BF16), jax.ShapeDtypeStruct((s_len, d), BF16)],
        compiler_params=_params(("arbitrary",)),
    )(dgc, dgc, dval, conv_w, w_up_t, after)


def _mix_bwd(dh2, dout, x1, y1, cat, attn, w_out_t, sc_f, g_pre_ffn, gt_m, g_post_mix, after, tm):
    s_len, d = x1.shape
    n_t = s_len // tm

    def body(dh_ref, do_ref, x1_ref, y1_ref, cat_ref, at_ref, wo_ref, sc_ref, g2_ref, gt_ref, g1_ref, after_ref,
             dx1_ref, dpool_ref, dattn_ref, delta_ref, dwo_ref, sums_ref, acc_ref):
        i = pl.program_id(0)
        dh = dh_ref[...].astype(F32)
        x1 = x1_ref[...]
        r2 = _rstd(x1)
        n2 = x1 * r2
        ng = n2 * g2_ref[...]
        dng = dh * (1.0 + sc_ref[...])
        dx1 = do_ref[...].astype(F32) + _norm_bwd(dng * g2_ref[...], n2, r2)
        dx1_ref[...] = dx1.astype(BF16)
        y1 = y1_ref[...].astype(F32)
        r1 = _rstd(y1)
        n1 = y1 * r1
        drn = dx1 * gt_ref[...]
        dy1 = _norm_bwd(drn * g1_ref[...], n1, r1).astype(BF16)
        dcat = _dot(dy1, wo_ref[...], NN)
        dpool_ref[...] = dcat[:, 0:256]
        lane = lax.broadcasted_iota(jnp.int32, (tm, LANES), 1)
        first = lane < HEAD_DIM
        for s in range(2):
            da = dcat[:, 256 + s * LANES:256 + (s + 1) * LANES]
            dattn_ref[s] = da
            prod = da * at_ref[:, s * LANES:(s + 1) * LANES]
            tot = jnp.sum(prod, axis=-1, keepdims=True)
            lo = jnp.sum(jnp.where(first, prod, 0.0), axis=-1, keepdims=True)
            delta_ref[s] = jnp.where(first, lo, tot - lo)
        dwo = _dot(dy1, cat_ref[...], TN)
        sums = jnp.concatenate(
            [jnp.sum(dh, axis=0, keepdims=True), jnp.sum(dh * ng, axis=0, keepdims=True),
             jnp.sum(dng * n2, axis=0, keepdims=True), jnp.sum(dx1 * (n1 * g1_ref[...]), axis=0, keepdims=True),
             jnp.sum(drn * n1, axis=0, keepdims=True), jnp.zeros((3, d), F32)], axis=0)

        @pl.when(i == 0)
        def _():
            acc_ref[...] = dwo
            sums_ref[...] = sums

        @pl.when(i > 0)
        def _():
            acc_ref[...] += dwo
            sums_ref[...] += sums

        @pl.when(i == n_t - 1)
        def _():
            dwo_ref[...] = acc_ref[...].astype(BF16)

    tile = lambda w: pl.BlockSpec((tm, w), lambda i: (i, 0))
    slab = pl.BlockSpec((2, tm, LANES), lambda i: (0, i, 0))
    vec = pl.BlockSpec((1, d), lambda i: (0, 0))
    return pl.pallas_call(
        body, name="mix_bwd", grid=(n_t,),
        in_specs=[tile(d), tile(d), tile(d), tile(d), tile(512), tile(256),
                  pl.BlockSpec((d, 512), lambda i: (0, 0)), vec, vec, vec, vec, pl.BlockSpec(memory_space=pl.ANY)],
        out_specs=[tile(d), tile(256), slab, slab, pl.BlockSpec((d, 512), lambda i: (0, 0)),
                   pl.BlockSpec((8, d), lambda i: (0, 0))],
        out_shape=[jax.ShapeDtypeStruct((s_len, d), BF16), jax.ShapeDtypeStruct((s_len, 256), F32),
                   jax.ShapeDtypeStruct((2, s_len, LANES), F32), jax.ShapeDtypeStruct((2, s_len, LANES), F32),
                   jax.ShapeDtypeStruct((d, 512), BF16), jax.ShapeDtypeStruct((8, d), F32)],
        scratch_shapes=[pltpu.VMEM((d, 512), F32)],
        compiler_params=_params(("arbitrary",)),
    )(dh2, dout, x1, y1, cat, attn, w_out_t, sc_f, g_pre_ffn, gt_m, g_post_mix, after)


def _pool_bwd(dpool, u_pool, w_blk, b_pool, pool_scale, tm):
    s_len = dpool.shape[0]
    n_t = s_len // tm

    def body(dp_ref, dpn_ref, u_ref, uh_ref, wb_ref, bp_ref, ps_ref, du_ref, dwp_ref, sums_ref, acc_ref):
        i = pl.program_id(0)
        u = u_ref[...]
        mixed, _ = _pool_mixed(u, uh_ref[...] * (i > 0).astype(F32), i, tm)
        mixed_b = mixed.astype(BF16)
        y = _dot(mixed_b, wb_ref[...], NN) + bp_ref[...]
        dp = dp_ref[...]
        dy = dp * ps_ref[...]
        dwb = _dot(mixed_b, dy.astype(BF16), TN)
        sums = jnp.concatenate([jnp.sum(dy, axis=0, keepdims=True), jnp.sum(dp * y, axis=0, keepdims=True),
                                jnp.zeros((6, 256), F32)], axis=0)
        dp_ext = jnp.concatenate([dp, dpn_ref[...] * (i < n_t - 1).astype(F32)], axis=0)
        dmix = _dot((dp_ext * ps_ref[...]).astype(BF16), wb_ref[...], NT)
        rows = tm + HALO
        grp = lax.broadcasted_iota(jnp.int32, (rows, 256), 1) // HEAD_DIM
        pick = lambda a, b, c, e: jnp.where(grp == 0, a, jnp.where(grp == 1, b, jnp.where(grp == 2, c, e)))
        pos = (i * tm + lax.broadcasted_iota(jnp.int32, (rows, 256), 0)).astype(F32)
        z = dmix / jnp.minimum(pos + 1.0, pick(*[float(w) for w in POOL_WINDOWS]))
        f2 = z + pltpu.roll(z, rows - 1, 0)
        f4 = f2 + pltpu.roll(f2, rows - 2, 0)
        f8 = f4 + pltpu.roll(f4, rows - 4, 0)
        f16 = f8 + pltpu.roll(f8, rows - 8, 0)
        du_ref[...] = (pick(f2, f4, f8, f16) - dmix)[:tm]

        @pl.when(i == 0)
        def _():
            acc_ref[...] = dwb
            sums_ref[...] = sums

        @pl.when(i > 0)
        def _():
            acc_ref[...] += dwb
            sums_ref[...] += sums

        @pl.when(i == n_t - 1)
        def _():
            full = acc_ref[...]
            for gi in range(len(POOL_WINDOWS)):
                lo = gi * HEAD_DIM
                dwp_ref[gi] = full[lo:lo + HEAD_DIM, lo:lo + HEAD_DIM]

    n_g = len(POOL_WINDOWS)
    tile = pl.BlockSpec((tm, 256), lambda i: (i, 0))
    const = lambda a: pl.BlockSpec(a.shape, lambda i: (0,) * a.ndim)
    return pl.pallas_call(
        body, name="pool_bwd", grid=(n_t,),
        in_specs=[tile, pl.BlockSpec((HALO, 256), lambda i: (jnp.minimum((i + 1) * (tm // HALO), s_len // HALO - 1), 0)),
                  tile, pl.BlockSpec((HALO, 256), lambda i: (_halo_before(i, tm), 0)),
                  const(w_blk), const(b_pool), const(pool_scale)],
        out_specs=[tile, pl.BlockSpec((n_g, HEAD_DIM, HEAD_DIM), lambda i: (0, 0, 0)), pl.BlockSpec((8, 256), lambda i: (0, 0))],
        out_shape=[jax.ShapeDtypeStruct((s_len, 256), F32), jax.ShapeDtypeStruct((n_g, HEAD_DIM, HEAD_DIM), F32),
                   jax.ShapeDtypeStruct((8, 256), F32)],
        scratch_shapes=[pltpu.VMEM((256, 256), F32)],
        compiler_params=_params(("arbitrary",)),
    )(dpool, dpool, u_pool, u_pool, w_blk, b_pool, pool_scale)


def _attn_bwd(qkv, dattn, lse_all, delta, after):
    s_len = qkv.shape[1]
    n_g = len(DILATIONS)

    def body(q_ref, k_ref, v_ref, do_ref, l_ref, dl_ref, after_ref, dq_ref, dk_ref, dv_ref):
        lane = lax.broadcasted_iota(jnp.int32, (BLOCK, LANES), 1)
        first = lane < HEAD_DIM

        def group(dil):
            nb = s_len // (BLOCK * dil)

            def block(t, carry):
                dk_part, dv_part = carry
                r, n = t // nb, t % nb
                cur = _block_rows(n, r, dil)
                prev = _block_rows(jnp.maximum(n - 1, 0), r, dil)
                q = q_ref[0, cur, :]
                do = do_ref[0, cur, :]
                lse = l_ref[0, cur, :]
                dlt = dl_ref[0, cur, :]
                kcat = jnp.concatenate([k_ref[0, prev, :], k_ref[0, cur, :]], axis=0).astype(BF16)
                vcat = jnp.concatenate([v_ref[0, prev, :], v_ref[0, cur, :]], axis=0).astype(BF16)
                valid = _band_mask(n)
                stack = lambda a: jnp.concatenate([jnp.where(first, a, 0.0), jnp.where(first, 0.0, a)], axis=0)
                rows2 = lambda a: jnp.concatenate([a[:, 0:1], a[:, HEAD_DIM:HEAD_DIM + 1]], axis=0)
                q2, do2 = stack(q).astype(BF16), stack(do).astype(BF16)
                valid2 = jnp.concatenate([valid, valid], axis=0)
                p = jnp.where(valid2, jnp.exp(_dot(q2, kcat, NT) - rows2(lse)), 0.0)
                ds = (p * (_dot(do2, vcat, NT) - rows2(dlt))).astype(BF16)
                dq2 = _dot(ds, kcat, NN)
                dq_ref[0, 0, cur, :] = jnp.where(first, dq2[:BLOCK], dq2[BLOCK:])
                dkc = _dot(ds, q2, TN)
                dvc = _dot(p.astype(BF16), do2, TN)
                dk_ref[0, 0, prev, :] = dk_part + dkc[:BLOCK]
                dv_ref[0, 0, prev, :] = dv_part + dvc[:BLOCK]
                dk_ref[0, 0, cur, :] = dkc[BLOCK:]
                dv_ref[0, 0, cur, :] = dvc[BLOCK:]
                return dkc[BLOCK:], dvc[BLOCK:]

            def blocks(tt, carry):
                for u in range(ATTN_BWD_UNROLL):
                    carry = block(tt * ATTN_BWD_UNROLL + u, carry)
                return carry

            zero = jnp.zeros((BLOCK, LANES), F32)
            lax.fori_loop(0, nb * dil // ATTN_BWD_UNROLL, blocks, (zero, zero))

        for gi, dil in enumerate(DILATIONS):
            pl.when(pl.program_id(1) == gi)(functools.partial(group, dil))

    def slab(base):
        return pl.BlockSpec((1, s_len, LANES), lambda s, g: (base + 2 * g + s, 0, 0))

    one = pl.BlockSpec((1, s_len, LANES), lambda s, g: (s, 0, 0))
    out = pl.BlockSpec((1, 1, s_len, LANES), lambda s, g: (g, s, 0, 0))
    shape = jax.ShapeDtypeStruct((n_g, 2, s_len, LANES), F32)
    return pl.pallas_call(
        body, name="attn_bwd", grid=(2, n_g),
        in_specs=[slab(0), slab(6), slab(12), one, one, one, pl.BlockSpec(memory_space=pl.ANY)],
        out_specs=[out, out, out], out_shape=[shape, shape, shape],
        compiler_params=_params(("arbitrary", "arbitrary")),
    )(qkv, qkv, qkv, dattn, lse_all, delta, after)


def _dproj_wgrad_in(du, dqkv, rope, h1, tm, cm):
    s_len = du.shape[0]
    d = h1.shape[1]
    n_proj = 256 + 18 * LANES
    n_t = s_len // tm

    def body(du_ref, dq_ref, dk_ref, dv_ref, cs_ref, spread_ref, h_ref, dproj_ref, dw_ref, acc_ref):
        i = pl.program_id(0)

        @pl.when(i == 0)
        def _():
            acc_ref[...] = jnp.zeros_like(acc_ref)

        dproj_ref[:, 0:256] = du_ref[...].astype(BF16)
        lanes = _rope_lanes(cs_ref, spread_ref)
        col = 256
        for kind, dref in enumerate((dq_ref, dk_ref, dv_ref)):
            for grp in range(3):
                for s in range(2):
                    piece = dref[grp, s]
                    if kind < 2:
                        piece = _rope_bwd(piece, lanes)
                    if kind == 0:
                        piece = piece * (HEAD_DIM ** -0.5)
                    dproj_ref[:, col:col + LANES] = piece.astype(BF16)
                    col += LANES

        for c0 in range(0, n_proj, cm):
            acc_ref[c0:c0 + cm, :] += _dot(dproj_ref[:, c0:c0 + cm], h_ref[...], TN)

        @pl.when(i == n_t - 1)
        def _():
            dw_ref[...] = acc_ref[...].astype(BF16)

    groups = pl.BlockSpec((len(DILATIONS), 2, tm, LANES), lambda i: (0, 0, i, 0))
    return pl.pallas_call(
        body, name="dproj_wgrad_in", grid=(n_t,),
        in_specs=[pl.BlockSpec((tm, 256), lambda i: (i, 0))] + [groups] * 3
        + [pl.BlockSpec((rope[0].shape[0], tm), lambda i: (0, i)), pl.BlockSpec(rope[1].shape, lambda i: (0, 0, 0)),
           pl.BlockSpec((tm, d), lambda i: (i, 0))],
        out_specs=[pl.BlockSpec((tm, n_proj), lambda i: (i, 0)), pl.BlockSpec((n_proj, d), lambda i: (0, 0))],
        out_shape=[jax.ShapeDtypeStruct((s_len, n_proj), BF16), jax.ShapeDtypeStruct((n_proj, d), BF16)],
        scratch_shapes=[pltpu.VMEM((n_proj, d), F32)],
        compiler_params=_params(("arbitrary",)),
    )(du, *dqkv, *rope, h1)


def _inproj_bwd(dproj, w_in_t, x, dx1, sc_m, g_pre_mix, after, tm):
    s_len, d = x.shape
    n_proj = w_in_t.shape[0]
    n_t = s_len // tm

    def body(dproj_ref, w_ref, x_ref, dx1_ref, sc_ref, g_ref, after_ref, dx_ref, sums_ref):
        i = pl.program_id(0)
        halves = [slice(0, tm // 2), slice(tm // 2, tm)]
        dhs = [_dot(dproj_ref[rs, :], w_ref[...], NN) for rs in halves]
        sums = None
        for rs, dh in zip(halves, dhs):
            xv = x_ref[rs, :]
            r = _rstd(xv)
            n = xv * r
            dng = dh * (1.0 + sc_ref[...])
            dx_ref[rs, :] = dx1_ref[rs, :].astype(F32) + _norm_bwd(dng * g_ref[...], n, r)
            part = jnp.concatenate([jnp.sum(dh, axis=0, keepdims=True), jnp.sum(dh * (n * g_ref[...]), axis=0, keepdims=True),
                                    jnp.sum(dng * n, axis=0, keepdims=True), jnp.zeros((5, d), F32)], axis=0)
            sums = part if sums is None else sums + part

        @pl.when(i == 0)
        def _():
            sums_ref[...] = sums

        @pl.when(i > 0)
        def _():
            sums_ref[...] += sums

    tile = lambda w: pl.BlockSpec((tm, w), lambda i: (i, 0))
    vec = pl.BlockSpec((1, d), lambda i: (0, 0))
    return pl.pallas_call(
        body, name="inproj_bwd", grid=(n_t,),
        in_specs=[tile(n_proj), pl.BlockSpec((n_proj, d), lambda i: (0, 0)), tile(d), tile(d), vec, vec,
                  pl.BlockSpec(memory_space=pl.ANY)],
        out_specs=[tile(d), pl.BlockSpec((8, d), lambda i: (0, 0))],
        out_shape=[jax.ShapeDtypeStruct((s_len, d), F32), jax.ShapeDtypeStruct((8, d), F32)],
        compiler_params=_params(("arbitrary",)),
    )(dproj, w_in_t, x, dx1, sc_m, g_pre_mix, after)


def _wgrad(a, b, name, tk, tmm):
    s_len, m = a.shape
    n = b.shape[1]
    n_k = s_len // tk

    def body(a_ref, b_ref, o_ref, acc_ref):
        k = pl.program_id(1)
        part = _dot(a_ref[...], b_ref[...], TN)

        @pl.when(k == 0)
        def _():
            acc_ref[...] = part

        @pl.when(k > 0)
        def _():
            acc_ref[...] += part

        @pl.when(k == n_k - 1)
        def _():
            o_ref[...] = acc_ref[...].astype(BF16)

    return pl.pallas_call(
        body, name=name, grid=(m // tmm, n_k),
        in_specs=[pl.BlockSpec((tk, tmm), lambda j, k: (k, j)), pl.BlockSpec((tk, n), lambda j, k: (k, 0))],
        out_specs=pl.BlockSpec((tmm, n), lambda j, k: (j, 0)),
        out_shape=jax.ShapeDtypeStruct((m, n), BF16),
        scratch_shapes=[pltpu.VMEM((tmm, n), F32)],
        compiler_params=_params(("arbitrary", "arbitrary")),
    )(a, b)


def _place():
    return lax.axis_index("x"), lax.axis_index("y"), lax.axis_index("c")


def _peer(k):
    x, y, c = _place()
    bx, by, bc = (k >> 2) & 1, (k >> 1) & 1, k & 1
    return (x ^ bx if bx else x, y ^ by if by else y, c ^ bc if bc else c)


def _index(pos):
    return 4 * pos[0] + 2 * pos[1] + pos[2]


def _entry_exchange(c_rows, w_ada, b_ada, taps, shards, later):
    d = c_rows.shape[1]
    ncol = w_ada.shape[1]
    n_w, n_p = len(shards), len(later)

    def body(c_ref, w_ref, b_ref, t_ref, *rest):
        srcs, rest = rest[:n_w], rest[n_w:]
        later_refs, rest = rest[:n_p], rest[n_p:]
        (call_ref, mod_ref, tall_ref), rest = rest[:3], rest[3:]
        outs, rest = rest[:n_w], rest[n_w:]
        zones, rest = rest[:n_p], rest[n_p:]
        stage_ref, s_send, s_recv, w_send, w_recv, local_sems = rest[:6]
        wide, narrow, place_sems = rest[6:6 + n_p], rest[6 + n_p:6 + 2 * n_p], rest[6 + 2 * n_p]
        x, y, c = _place()
        here, sibling = (x, y, c), (x, y, 1 - c)
        chips = [(1 - x, y), (x, 1 - y), (1 - x, 1 - y)]
        me = _index(here)

        def small(kind, src, dst, k):
            return pltpu.make_async_remote_copy(src_ref=src, dst_ref=dst, send_sem=s_send.at[kind, k - 1],
                                                recv_sem=s_recv.at[kind, k - 1], device_id=_peer(k), device_id_type=MESH)

        gather = lambda k: small(0, c_ref, call_ref.at[me], k)
        scatter = lambda k: small(1, stage_ref.at[_index(_peer(k))], mod_ref.at[me], k)
        gather_taps = lambda k: small(2, t_ref, tall_ref.at[me], k)

        def rows(w, pos):
            r = shards[w].shape[0]
            return outs[w].at[pl.ds(pl.multiple_of(_index(pos) * r, 16), r), :]

        def block(k, w, pos, to, own=False):
            return pltpu.make_async_remote_copy(
                src_ref=srcs[w] if own else rows(w, pos), dst_ref=rows(w, pos),
                send_sem=w_send.at[k, w], recv_sem=w_recv.at[k, w], device_id=to, device_id_type=MESH)

        call_ref[me] = c_ref[...]
        tall_ref[me] = t_ref[...]
        for k in range(1, N_DEV):
            gather(k).start()
        for k in range(1, N_DEV):
            gather_taps(k).start()
        mine = [pltpu.make_async_copy(srcs[w], rows(w, here), local_sems.at[w]) for w in range(n_w)]
        for cp in mine:
            cp.start()
        first = [block(0, w, here, sibling, own=True) for w in range(n_w)]
        first += [block(1 + j, w, here, (*chip, c), own=True) for j, chip in enumerate(chips) for w in range(n_w)]
        for cp in first:
            cp.start()
        fetch = [pltpu.make_async_copy(later_refs[w], wide[w], place_sems.at[0, w]) for w in range(n_p)]
        for cp in fetch:
            cp.start()

        for k in range(1, N_DEV):
            gather(k).wait_recv()
        cv = jnp.concatenate([call_ref[b, 0:1, :] for b in range(N_DEV)], axis=0)
        act = cv * jax.nn.sigmoid(cv)
        mod = lax.dot_general(act, w_ref[...], NN, preferred_element_type=F32,
                              precision=lax.Precision.HIGHEST) + b_ref[:, pl.ds(pl.multiple_of(me * ncol, LANES), ncol)]
        for b in range(N_DEV):
            stage_ref[b] = jnp.broadcast_to(mod[b:b + 1, :], (8, ncol))
        mod_ref[me] = stage_ref[me]
        for k in range(1, N_DEV):
            scatter(k).start()

        placed = []
        for w in range(n_p):
            fetch[w].wait()
            narrow[w][...] = wide[w][...].astype(BF16)
            r = later[w].shape[0]
            placed.append(pltpu.make_async_copy(narrow[w], zones[w].at[pl.ds(pl.multiple_of(me * r, 16), r), :],
                                                place_sems.at[1, w]))
            placed[-1].start()

        passed = []
        for j, chip in enumerate(chips):
            for w in range(n_w):
                block(1 + j, w, (*chip, c), here).wait_recv()
                fwd = block(4 + j, w, (*chip, c), sibling)
                fwd.start()
                passed.append(fwd)
        for w in range(n_w):
            block(0, w, sibling, here).wait_recv()
        for j, chip in enumerate(chips):
            for w in range(n_w):
                block(4 + j, w, (*chip, 1 - c), here).wait_recv()
        for k in range(1, N_DEV):
            scatter(k).wait_recv()
            gather_taps(k).wait_recv()
        for cp in first + passed:
            cp.wait_send()
        for k in range(1, N_DEV):
            gather(k).wait_send()
            scatter(k).wait_send()
            gather_taps(k).wait_send()
        for cp in mine + placed:
            cp.wait()

    vmem, hbm = pl.BlockSpec(memory_space=pltpu.VMEM), pl.BlockSpec(memory_space=pltpu.HBM)
    out = pl.pallas_call(
        body, name="entry_exchange",
        in_specs=[vmem] * 4 + [hbm] * (n_w + n_p), out_specs=[vmem] * 3 + [hbm] * (n_w + n_p),
        out_shape=[jax.ShapeDtypeStruct((N_DEV, 8, d), F32), jax.ShapeDtypeStruct((N_DEV, 8, ncol), F32),
                   jax.ShapeDtypeStruct((N_DEV,) + taps.shape, F32)]
        + [jax.ShapeDtypeStruct((N_DEV * s.shape[0], s.shape[1]), s.dtype) for s in shards]
        + [jax.ShapeDtypeStruct((N_DEV * s.shape[0], s.shape[1]), BF16) for s in later],
        scratch_shapes=[pltpu.VMEM((N_DEV, 8, ncol), F32), pltpu.SemaphoreType.DMA((3, N_DEV - 1)),
                        pltpu.SemaphoreType.DMA((3, N_DEV - 1)), pltpu.SemaphoreType.DMA((N_DEV - 1, n_w)),
                        pltpu.SemaphoreType.DMA((N_DEV - 1, n_w)), pltpu.SemaphoreType.DMA((n_w,))]
        + [pltpu.VMEM(s.shape, F32) for s in later] + [pltpu.VMEM(s.shape, BF16) for s in later]
        + [pltpu.SemaphoreType.DMA((2, n_p))],
        compiler_params=_params(),
    )(c_rows, w_ada, b_ada, taps, *shards, *later)
    return out[0], out[1], out[2], out[3:3 + n_w], out[3 + n_w:]


def _peer_copies(mode, srcs, lands, send_sems, recv_sems):
    if mode in ("gather_ici", "gather_d2d"):
        x, y, c = _place()
        sibling = (x, y, 1 - c)
        chips = [(1 - x, y), (x, 1 - y), (1 - x, 1 - y)]
        n = len(lands)

        def rows(w, pos):
            r = lands[w].shape[0] // N_DEV
            return lands[w].at[pl.ds(pl.multiple_of(_index(pos) * r, 16), r), :]

        def copy(k, w, src, dst, to):
            return pltpu.make_async_remote_copy(src_ref=src, dst_ref=dst, send_sem=send_sems.at[k * n + w],
                                                recv_sem=recv_sems.at[k * n + w], device_id=to, device_id_type=MESH)

        if mode == "gather_ici":
            targets = [sibling] + [(*chip, c) for chip in chips]
            return [copy(k, w, rows(w, (x, y, c)), rows(w, (x, y, c)), to) for k, to in enumerate(targets) for w in range(n)]
        return [copy(j, w, rows(w, (*chip, c)), rows(w, (*chip, c)), sibling)
                for j, chip in enumerate(chips) for w in range(n)]
    me = _index(_place())
    modes = (mode,) * len(srcs) if isinstance(mode, str) else mode
    copies = []
    for k in range(1, N_DEV):
        peer = _peer(k)
        for w, (src, land) in enumerate(zip(srcs, lands)):
            if modes[w] == "gather":
                r = src.shape[0]
                dst = land.at[pl.ds(pl.multiple_of(me * r, 16), r), :]
            elif modes[w] == "allgather":
                dst = land.at[me]
            else:
                r = src.shape[0] // N_DEV
                src = src.at[pl.ds(pl.multiple_of(_index(peer) * r, 16), r), :]
                dst = land.at[me]
            copies.append(pltpu.make_async_remote_copy(
                src_ref=src, dst_ref=dst, send_sem=send_sems.at[(k - 1) * len(srcs) + w],
                recv_sem=recv_sems.at[(k - 1) * len(srcs) + w],
                device_id=peer, device_id_type=MESH))
    return copies


def _exchange_start(mode, srcs, lands, name):
    n_s, n_a = len(srcs), len(srcs) + len(lands)
    n_cp = _COPIES_PER_ARRAY.get(mode, N_DEV - 1) * len(lands)

    def body(*refs):
        for cp in _peer_copies(mode, refs[:n_s], refs[n_s:n_a], refs[n_a], refs[n_a + 1]):
            cp.start()

    hbm, sem = pl.BlockSpec(memory_space=pltpu.HBM), pl.BlockSpec(memory_space=pltpu.SEMAPHORE)
    arrays = list(srcs) + list(lands)
    out = pl.pallas_call(
        body, name=name,
        out_shape=(pltpu.SemaphoreType.DMA((n_cp,)), pltpu.SemaphoreType.DMA((n_cp,)),
                   *[pltpu.HBM(a.shape, a.dtype) for a in arrays]),
        in_specs=[hbm] * n_a, out_specs=(sem, sem, *[hbm] * n_a),
        input_output_aliases={i: 2 + i for i in range(n_a)},
        compiler_params=pltpu.CompilerParams(has_side_effects=pltpu.SideEffectType.DATAFLOW_SIDE_EFFECTING),
    )(*[pltpu.with_memory_space_constraint(a, pltpu.HBM) for a in arrays])
    return out[0], out[1], out[2:2 + n_s], out[2 + n_s:2 + n_a], out[2]


_COPIES_PER_ARRAY = {"gather_ici": 4, "gather_d2d": 3}


def _exchange_wait(mode, send_sems, recv_sems, srcs, lands, after, name):
    n_s, n_a = len(srcs), len(srcs) + len(lands)

    def body(*refs):
        copies = _peer_copies(mode, refs[:n_s], refs[n_s:n_a], refs[n_a], refs[n_a + 1])
        for cp in copies:
            cp.wait_send()
        for cp in copies:
            cp.wait_recv()

    hbm, sem = pl.BlockSpec(memory_space=pltpu.HBM), pl.BlockSpec(memory_space=pltpu.SEMAPHORE)
    arrays = list(srcs) + list(lands)
    out = pl.pallas_call(
        body, name=name, out_shape=tuple(pltpu.HBM(a.shape, a.dtype) for a in arrays),
        in_specs=[hbm] * n_a + [sem, sem, pl.BlockSpec(memory_space=pl.ANY)], out_specs=tuple([hbm] * n_a),
        input_output_aliases={i: i for i in range(n_a)},
        compiler_params=pltpu.CompilerParams(has_side_effects=pltpu.SideEffectType.DATAFLOW_SIDE_EFFECTING),
    )(*arrays, send_sems, recv_sems, after)
    return out[:n_s], out[n_s:]


SMALL_WEIGHTS = ("b_ada", "g_pre_mix", "g_post_mix", "g_pre_ffn", "g_post_ffn", "w_pool", "b_pool", "pool_scale", "conv_b")


MOD_ROWS = ((0, 0), (0, 1), (1, 3), (1, 0), (1, 1), (2, 0))


def _small_sum_adam(mine, gathered, weights, moms, vels):
    n_l, n_w = len(mine), len(weights)
    d = mine[0].shape[1]

    def body(*refs):
        loc, got = refs[:n_l], refs[n_l:2 * n_l]
        w_refs, m_refs, v_refs = (refs[2 * n_l + k * n_w:2 * n_l + (k + 1) * n_w] for k in range(3))
        outs = refs[2 * n_l + 3 * n_w:]
        dmod_ref, conv_ref, loss_ref = outs[4 * n_w:]
        me = _index(_place())
        part = lambda a, dev: jnp.where(dev == me, loc[a][...], got[a][dev])
        totals = []
        for a in range(n_l):
            tot = part(a, 0)
            for dev in range(1, N_DEV):
                tot = tot + part(a, dev)
            totals.append(tot)
        t_in, t_mix, t_ffn, t_pool, t_blk, t_conv, t_loss = totals
        conv_ref[...] = t_conv
        loss_ref[...] = t_loss
        for dev in range(N_DEV):
            for k, (a, r) in enumerate(MOD_ROWS):
                dmod_ref[dev:dev + 1, k * d:(k + 1) * d] = part(a, dev)[r:r + 1, :]

        def update(idx, g, at=()):
            sel = lambda ref: ref.at[at] if at else ref
            delta, nm, nv = _adam_math(sel(w_refs[idx])[...], g, sel(m_refs[idx])[...], sel(v_refs[idx])[...])
            for k, val in enumerate((g, delta, nm, nv)):
                sel(outs[4 * idx + k])[...] = val

        tots = (t_in, t_mix, t_ffn)
        update(0, jnp.concatenate([tots[a][r:r + 1] for a, r in MOD_ROWS], axis=1))
        update(1, t_in[2:3])
        update(2, t_mix[4:5])
        update(3, t_mix[2:3])
        update(4, t_ffn[1:2])
        for gi in range(len(POOL_WINDOWS)):
            update(5, t_blk[gi], at=(0, gi))
        update(6, jnp.concatenate([t_pool[0:1, gi * HEAD_DIM:(gi + 1) * HEAD_DIM] for gi in range(len(POOL_WINDOWS))], axis=0),
               at=(0,))
        update(7, t_pool[1:2])
        update(8, t_conv[3:4])

    vmem = pl.BlockSpec(memory_space=pltpu.VMEM)
    out = pl.pallas_call(
        body, name="small_sum_adam", in_specs=[vmem] * (2 * n_l + 3 * n_w), out_specs=[vmem] * (4 * n_w + 3),
        out_shape=[jax.ShapeDtypeStruct(w.shape, F32) for w in weights for _ in range(4)]
        + [jax.ShapeDtypeStruct((N_DEV, 6 * d), F32), jax.ShapeDtypeStruct(mine[5].shape, F32),
           jax.ShapeDtypeStruct(mine[6].shape, F32)],
        compiler_params=_params(),
    )(*mine, *gathered, *weights, *moms, *vels)
    return out[:4 * n_w], out[4 * n_w], out[4 * n_w + 1], out[4 * n_w + 2]


def _adam_math(w, g, m, v):
    m = ADAM_B1 * m + (1.0 - ADAM_B1) * g
    v = ADAM_B2 * v + (1.0 - ADAM_B2) * (g * g)
    m_hat = m / (1.0 - ADAM_B1 ** ADAM_STEP)
    v_hat = v / (1.0 - ADAM_B2 ** ADAM_STEP)
    delta = -ADAM_LR * (m_hat / (jnp.sqrt(v_hat) + ADAM_EPS) + ADAM_WD * w)
    return delta, m, v


def _adam(w, g, m, v, name):
    def body(w_ref, g_ref, m_ref, v_ref, d_ref, nm_ref, nv_ref):
        d_ref[...], nm_ref[...], nv_ref[...] = _adam_math(w_ref[...], g_ref[...], m_ref[...], v_ref[...])

    vmem = pl.BlockSpec(memory_space=pltpu.VMEM)
    return pl.pallas_call(
        body, name=name, in_specs=[vmem] * 4, out_specs=[vmem] * 3,
        out_shape=[jax.ShapeDtypeStruct(w.shape, F32)] * 3, compiler_params=_params(),
    )(w, g, m, v)


def _sum_adam(own, parts, w, m, v, me, name, tr):
    _, rows, cols = parts.shape
    turned = w.shape == (cols, rows) and rows != cols
    assert tr == rows or not turned
    n_t = rows // tr

    def body(me_ref, own_ref, p_ref, w_ref, m_ref, v_ref, g_ref, d_ref, nm_ref, nv_ref):
        part = lambda dev: jnp.where(dev == me_ref[0], own_ref[...], p_ref[dev]).astype(F32)
        g = part(0)
        for dev in range(1, N_DEV):
            g = g + part(dev)
        g = g.T if turned else g
        g_ref[...] = g
        d_ref[...], nm_ref[...], nv_ref[...] = _adam_math(w_ref[...], g, m_ref[...], v_ref[...])

    spec = pl.BlockSpec((cols, rows) if turned else (tr, cols), lambda i, me_ref: (i, 0))
    shape = jax.ShapeDtypeStruct(w.shape, F32)
    return pl.pallas_call(
        body, name=name, out_shape=[shape] * 4,
        grid_spec=pltpu.PrefetchScalarGridSpec(
            num_scalar_prefetch=1, grid=(n_t,),
            in_specs=[pl.BlockSpec((tr, cols), lambda i, me_ref: (me_ref[0] * n_t + i, 0)),
                      pl.BlockSpec((N_DEV, tr, cols), lambda i, me_ref: (0, i, 0)), spec, spec, spec],
            out_specs=[spec] * 4),
        compiler_params=_params(("arbitrary",)),
    )(me.reshape(1).astype(jnp.int32), own, parts, w, m, v)


def _ada_grad_adam(c_all, dmod_all, w, m, v, tr):
    rows, cols = w.shape

    def body(c_ref, dm_ref, w_ref, m_ref, v_ref, g_ref, d_ref, nm_ref, nv_ref):
        cv = c_ref[...]
        act = cv * jax.nn.sigmoid(cv)
        dmod = dm_ref[:, pl.ds(pl.multiple_of(_index(_place()) * cols, LANES), cols)]
        g = lax.dot_general(act, dmod, TN, preferred_element_type=F32, precision=lax.Precision.HIGHEST)
        g_ref[...] = g
        d_ref[...], nm_ref[...], nv_ref[...] = _adam_math(w_ref[...], g, m_ref[...], v_ref[...])

    spec = pl.BlockSpec((tr, cols), lambda i: (i, 0))
    shape = jax.ShapeDtypeStruct((rows, cols), F32)
    return pl.pallas_call(
        body, name="ada_grad_adam", grid=(rows // tr,),
        in_specs=[pl.BlockSpec((N_DEV, tr), lambda i: (0, i)), pl.BlockSpec(dmod_all.shape, lambda i: (0, 0)), spec, spec, spec],
        out_specs=[spec] * 4, out_shape=[shape] * 4, compiler_params=_params(("arbitrary",)),
    )(c_all, dmod_all, w, m, v)


def _rope_tables(positions):
    inv_freq = ROPE_THETA ** (-jnp.arange(0, 2 * ROT_HALF, 2, dtype=F32) / (2 * ROT_HALF))
    ang = inv_freq[:, None] * positions.astype(F32)[None, :]
    rows = jnp.concatenate([jnp.cos(ang), jnp.sin(ang), jnp.ones_like(ang)], axis=0)
    spread = [[[0.0] * LANES for _ in range(3 * ROT_HALF)] for _ in range(3)]
    for lane in range(LANES):
        p, j = lane % HEAD_DIM, lane % ROT_HALF
        if p < ROT_HALF:
            spread[0][j][lane] = 1.0
            spread[1][ROT_HALF + j][lane] = -1.0
        elif p < 2 * ROT_HALF:
            spread[0][j][lane] = 1.0
            spread[2][ROT_HALF + j][lane] = 1.0
        else:
            spread[0][2 * ROT_HALF][lane] = 1.0
    return rows, jnp.array(spread, F32)


def _pad_rows(a, rows):
    return jnp.pad(a, ((0, rows - a.shape[0]), (0, 0)))


def _sequence_step(xs, target, rope, mods, gains, w_in_t, w_out_t, relay_ffn, fetch_ffn, send_grads, w_blk_b, b_pool_r,
                   pool_scale_r, conv_w_all, conv_b, after):
    sh_m, sc_m, gt_m, sh_f, sc_f, gt_f = mods
    g_pre_mix, g_post_mix, g_pre_ffn, g_post_ffn = gains
    h1, u_pool, qkv = _premix_inproj(xs, sh_m, sc_m, g_pre_mix, w_in_t, rope, after, tm=512)
    o_g, lse_g = _attn_fwd(qkv)
    x1, y1, h2, cat, attn, lse_all = _mix_out(xs, u_pool, o_g, lse_g, w_blk_b, b_pool_r, pool_scale_r, w_out_t,
                                              gt_m, g_post_mix, g_pre_ffn, sc_f, sh_f, tm=512)
    relay_ffn(x1)
    w_up_t, w_down_f = fetch_ffn(x1)
    gate, a_ffn, act, vd, dy2, dout, sums_ffn, loss_loc = _ffn_fwd_loss(h2, x1, target, w_up_t, w_down_f, conv_w_all, conv_b,
                                                              gt_f, g_post_ffn, tm=256, ck=256)

    dgc, dval, dw_down, dconv = _ffn_bwd_act(dy2, gate, a_ffn, act, vd, w_down_f, tm=512, tf=1408, ck=256)
    token = send_grads("down", [dw_down], [])
    dup, dh2 = _ffn_bwd_up(dgc, dval, w_up_t, conv_w_all, token, tm=512)
    dw_up_t = _wgrad(dup, h2, "wgrad_up", tk=2048, tmm=1408)
    token = send_grads("up", [dw_up_t], [])
    dx1, dpool, dattn, delta, dw_out_t, sums_mix = _mix_bwd(dh2, dout, x1, y1, cat, attn, w_out_t, sc_f,
                                                           g_pre_ffn, gt_m, g_post_mix, token, tm=512)
    du, dw_blk, sums_pool = _pool_bwd(dpool, u_pool, w_blk_b, b_pool_r, pool_scale_r, tm=1024)
    token = send_grads("out", [dw_out_t], [sums_mix, sums_ffn, sums_pool, dw_blk, dconv, loss_loc])
    dproj, dw_in_t = _dproj_wgrad_in(du, _attn_bwd(qkv, dattn, lse_all, delta, token), rope, h1, tm=512, cm=512)
    token = send_grads("in", [dw_in_t], [])
    grad_x, sums_in = _inproj_bwd(dproj, w_in_t, xs, dx1, sc_m, g_pre_mix, token, tm=512)
    return (loss_loc, grad_x, dw_in_t, dw_out_t, dw_up_t, dw_down, dw_blk, dconv,
            sums_in, sums_mix, sums_ffn, sums_pool)


def kernel(x, c, positions, w_ada, b_ada, g_pre_mix, g_post_mix, g_pre_ffn, g_post_ffn, w_in, w_pool, b_pool, pool_scale, w_out, w_up, conv_w, conv_b, w_down, loss_target, m_w_ada, m_b_ada, m_g_pre_mix, m_g_post_mix, m_g_pre_ffn, m_g_post_ffn, m_w_in, m_w_pool, m_b_pool, m_pool_scale, m_w_out, m_w_up, m_conv_w, m_conv_b, m_w_down, v_w_ada, v_b_ada, v_g_pre_mix, v_g_post_mix, v_g_pre_ffn, v_g_post_ffn, v_w_in, v_w_pool, v_b_pool, v_pool_scale, v_w_out, v_w_up, v_conv_w, v_conv_b, v_w_down):
    s_len, d = x.shape[1], x.shape[2]
    d_ff = w_down.shape[1] * N_DEV
    me = _index(_place())
    xs, target = x[0], loss_target[0]

    c_all, mod, taps_all, (w_in_t, w_out_t), lands = _entry_exchange(
        jnp.broadcast_to(c, (8, d)), w_ada[0], b_ada, _pad_rows(conv_w[0], 8),
        [w_in[0].T.astype(BF16), w_out[0].T.astype(BF16)], [w_up[0].T, w_down[0]])
    c_all = c_all[:, 0, :]
    conv_w_all = jnp.transpose(taps_all[:, :3, :], (1, 0, 2)).reshape(3, d_ff)
    sh_m, sc_m, gt_m, sh_f, sc_f, gt_f = [mod[:, 0, :].reshape(1, -1)[:, k * d:(k + 1) * d] for k in range(6)]

    rope = _rope_tables(positions[0])
    w_blk = jnp.zeros((256, 256), F32)
    for gi in range(4):
        w_blk = lax.dynamic_update_slice(w_blk, w_pool[0, gi], (gi * HEAD_DIM, gi * HEAD_DIM))
    w_blk_b = w_blk.astype(BF16)
    b_pool_r, pool_scale_r = b_pool.reshape(1, 256), pool_scale.reshape(1, 256)

    w_send, w_recv, w_src, w_land, w_token = _exchange_start("gather_ici", [], lands, "ffn_weights_ici_start")
    relay = []

    def relay_ffn(after):
        _, blocks = _exchange_wait("gather_ici", w_send, w_recv, w_src, w_land, after, "ffn_weights_ici_wait")
        relay.extend(_exchange_start("gather_d2d", [], blocks, "ffn_weights_d2d_start"))

    def fetch_ffn(after):
        return _exchange_wait("gather_d2d", relay[0], relay[1], [], relay[3], after, "ffn_weights_d2d_wait")[1]

    flights = {}

    def send_grads(tag, slabs, whole):
        lands = [lax.empty((N_DEV, g.shape[0] // N_DEV, g.shape[1]), g.dtype) for g in slabs]
        lands += [lax.empty((N_DEV,) + a.shape, F32) for a in whole]
        modes = ("scatter",) * len(slabs) + ("allgather",) * len(whole)
        flights[tag] = (modes, *_exchange_start(modes, slabs + whole, lands, f"grads_{tag}_start"))
        return flights[tag][5]

    def arrived(tag, after):
        return _exchange_wait(*flights[tag][:5], after, f"grads_{tag}_wait")

    _, grad_x, *_, sums_in, _, _, _ = _sequence_step(
        xs, target, rope, (sh_m, sc_m, gt_m, sh_f, sc_f, gt_f), (g_pre_mix, g_post_mix, g_pre_ffn, g_post_ffn),
        w_in_t, w_out_t, relay_ffn, fetch_ffn, send_grads, w_blk_b, b_pool_r, pool_scale_r, conv_w_all, conv_b,
        w_token)

    send_grads("last", [], [sums_in])

    (own_down,), (parts_down,) = arrived("down", flights["last"][5])
    new_down = _sum_adam(own_down, parts_down, w_down[0], m_w_down[0], v_w_down[0], me, "adam_w_down", 352)
    (own_up,), (parts_up,) = arrived("up", new_down[0])
    new_up = _sum_adam(own_up, parts_up, w_up[0].T, m_w_up[0].T, v_w_up[0].T, me, "adam_w_up", 352)
    (own_out, *small), (parts_out, *gathered) = arrived("out", new_up[0])
    new_out = _sum_adam(own_out, parts_out, w_out[0], m_w_out[0], v_w_out[0], me, "adam_w_out", 128)
    (own_in,), (parts_in,) = arrived("in", new_out[0])
    new_in = _sum_adam(own_in, parts_in, w_in[0].T, m_w_in[0].T, v_w_in[0].T, me, "adam_w_in", 320)
    big = {"w_up": [a.T for a in new_up], "w_down": new_down, "w_out": new_out, "w_in": [a.T for a in new_in]}

    rep_w = [b_ada, g_pre_mix, g_post_mix, g_pre_ffn, g_post_ffn, w_pool, b_pool, pool_scale, conv_b]
    rep_m = [m_b_ada, m_g_pre_mix, m_g_post_mix, m_g_pre_ffn, m_g_post_ffn, m_w_pool, m_b_pool, m_pool_scale, m_conv_b]
    rep_v = [v_b_ada, v_g_pre_mix, v_g_post_mix, v_g_pre_ffn, v_g_post_ffn, v_w_pool, v_b_pool, v_pool_scale, v_conv_b]
    mine_last, got_last = arrived("last", new_in[0])
    small, gathered = [*mine_last, *small], [*got_last, *gathered]
    rep_out, dmod_all, dconv_tot, loss_tot = _small_sum_adam(small, gathered, rep_w, rep_m, rep_v)
    g_rep, d_rep, nm_rep, nv_rep = (rep_out[k::4] for k in range(4))

    fcol = d_ff // N_DEV
    taps = lambda a: jnp.transpose(a, (1, 0, 2))
    g_cw = lax.dynamic_slice(dconv_tot, (0, me * fcol), (3, fcol))[None]
    d_cw, nm_cw, nv_cw = [taps(a) for a in _adam(taps(conv_w), taps(g_cw), taps(m_conv_w), taps(v_conv_w), "adam_conv_w")]

    g_ada, d_ada, nm_ada, nv_ada = _ada_grad_adam(c_all, dmod_all, w_ada[0], m_w_ada[0], v_w_ada[0], 512)

    loss = loss_tot[0, 0]

    def group(k):
        rep = (g_rep, d_rep, nm_rep, nv_rep)[k]
        ada = (g_ada, d_ada, nm_ada, nv_ada)[k][None]
        cw = (g_cw, d_cw, nm_cw, nv_cw)[k]
        return [ada, rep[0], rep[1], rep[2], rep[3], rep[4], big["w_in"][k][None], rep[5], rep[6], rep[7],
                big["w_out"][k][None], big["w_up"][k][None], cw, rep[8], big["w_down"][k][None]]

    return (loss, grad_x[None], *group(0), *group(1), *group(2), *group(3))
```

```python
import functools
import math

import jax
import jax.numpy as jnp
from jax import lax
from jax.experimental import pallas as pl
from jax.experimental.pallas import tpu as pltpu

F32 = jnp.float32
BF16 = jnp.bfloat16
MESH = pl.DeviceIdType.MESH

N_DEV = 8
HEAD_DIM = 64
ROT_HALF = 8
ROPE_THETA = 500000.0
POOL_WINDOWS = (2, 4, 8, 16)
DILATIONS = (1, 4, 16)
BLOCK = 128
NORM_EPS = 1e-6
HALO = 16
MASKED = -1e30
ATTN_FWD_UNROLL = 8
ATTN_BWD_UNROLL = 8

ADAM_LR = 0.001
ADAM_B1 = 0.9
ADAM_B2 = 0.999
ADAM_EPS = 1e-08
ADAM_WD = 0.01
ADAM_STEP = 10

V7X_VMEM_LIMIT = 56 * 1024 * 1024
LANES = 128

NT = (((1,), (1,)), ((), ()))
NN = (((1,), (0,)), ((), ()))
TN = (((0,), (0,)), ((), ()))


def _dot(a, b, dims):
    return lax.dot_general(a, b, dims, preferred_element_type=F32)


def _params(sem=None, vmem=V7X_VMEM_LIMIT):
    if sem is None:
        return pltpu.CompilerParams(vmem_limit_bytes=vmem)
    return pltpu.CompilerParams(dimension_semantics=sem, vmem_limit_bytes=vmem)


def _rstd(v):
    return lax.rsqrt(jnp.mean(v * v, axis=-1, keepdims=True) + NORM_EPS)


def _norm_bwd(dn, n, rstd):
    return rstd * (dn - n * jnp.mean(dn * n, axis=-1, keepdims=True))


def _rope_lanes(cs_ref, spread_ref):
    return [lax.dot_general(cs_ref[...], spread_ref[k], TN, preferred_element_type=F32, precision=lax.Precision.HIGHEST)
            for k in range(3)]


def _rope_fwd(p, lanes):
    return p * lanes[0] + pltpu.roll(p, LANES - ROT_HALF, 1) * lanes[1] + pltpu.roll(p, ROT_HALF, 1) * lanes[2]


def _rope_bwd(dp, lanes):
    return dp * lanes[0] + pltpu.roll(dp * lanes[1], ROT_HALF, 1) + pltpu.roll(dp * lanes[2], LANES - ROT_HALF, 1)


def _gelu_parts(v):
    k2 = 2.0 * math.sqrt(2.0 / math.pi)
    c = 0.044715
    v2 = v * v
    s = jax.nn.sigmoid(v * (k2 + (k2 * c) * v2))
    g = v * s
    dg = s + g * (1.0 - s) * (k2 + (3.0 * k2 * c) * v2)
    return g, dg


def _halo_before(i, tile):
    return jnp.maximum(i * (tile // HALO) - 1, 0)


def _premix_inproj(x, sh, sc, g, w_in_t, rope, after, tm):
    s_len, d = x.shape
    n_proj = w_in_t.shape[0]
    n_slab = (n_proj - 256) // LANES

    def body(x_ref, sh_ref, sc_ref, g_ref, w_ref, cs_ref, spread_ref, after_ref, h_ref, up_ref, qkv_ref):
        xv = x_ref[...]
        h = (xv * _rstd(xv) * g_ref[...]) * (1.0 + sc_ref[...]) + sh_ref[...]
        hb = h.astype(BF16)
        h_ref[...] = hb
        up_ref[...] = _dot(hb, w_ref[0:256, :], NT)
        lanes = _rope_lanes(cs_ref, spread_ref)
        for pair in range(n_slab // 2):
            p = _dot(hb, w_ref[256 + 256 * pair:512 + 256 * pair, :], NT)
            for half in range(2):
                ph = p[:, half * LANES:(half + 1) * LANES]
                if pair < 6:
                    ph = _rope_fwd(ph, lanes)
                if pair < 3:
                    ph = ph * (HEAD_DIM ** -0.5)
                qkv_ref[2 * pair + half] = ph

    vec = pl.BlockSpec((1, d), lambda i: (0, 0))
    return pl.pallas_call(
        body, name="premix_inproj", grid=(s_len // tm,),
        in_specs=[pl.BlockSpec((tm, d), lambda i: (i, 0)), vec, vec, vec,
                  pl.BlockSpec((n_proj, d), lambda i: (0, 0)),
                  pl.BlockSpec((rope[0].shape[0], tm), lambda i: (0, i)), pl.BlockSpec(rope[1].shape, lambda i: (0, 0, 0)),
                  pl.BlockSpec(memory_space=pl.ANY)],
        out_specs=[pl.BlockSpec((tm, d), lambda i: (i, 0)),
                   pl.BlockSpec((tm, 256), lambda i: (i, 0)),
                   pl.BlockSpec((n_slab, tm, LANES), lambda i: (0, i, 0))],
        out_shape=[jax.ShapeDtypeStruct((s_len, d), BF16),
                   jax.ShapeDtypeStruct((s_len, 256), F32),
                   jax.ShapeDtypeStruct((n_slab, s_len, LANES), F32)],
        compiler_params=_params(("arbitrary",)),
    )(x, sh, sc, g, w_in_t, *rope, after)


def _block_rows(n, r, dil):
    start = n * (BLOCK * dil) + r
    if dil == 1:
        return pl.ds(pl.multiple_of(start, BLOCK), BLOCK)
    return pl.ds(start, BLOCK, stride=dil)


def _band_mask(n):
    ri = lax.broadcasted_iota(jnp.int32, (BLOCK, 2 * BLOCK), 0)
    cj = lax.broadcasted_iota(jnp.int32, (BLOCK, 2 * BLOCK), 1)
    cur = (cj >= BLOCK) & (cj - BLOCK <= ri)
    prev = (cj < BLOCK) & (cj >= ri) & (n > 0)
    return cur | prev


def _attn_fwd(qkv):
    s_len = qkv.shape[1]
    n_g = len(DILATIONS)

    def body(q_ref, k_ref, v_ref, o_ref, lse_ref):
        lane = lax.broadcasted_iota(jnp.int32, (BLOCK, LANES), 1)
        first = lane < HEAD_DIM

        def group(dil):
            nb = s_len // (BLOCK * dil)

            def block(t, carry):
                r, n = t // nb, t % nb
                cur = _block_rows(n, r, dil)
                prev = _block_rows(jnp.maximum(n - 1, 0), r, dil)
                q = q_ref[0, cur, :]
                kcat = jnp.concatenate([k_ref[0, prev, :], k_ref[0, cur, :]], axis=0).astype(BF16)
                vcat = jnp.concatenate([v_ref[0, prev, :], v_ref[0, cur, :]], axis=0).astype(BF16)
                valid = _band_mask(n)
                q2 = jnp.concatenate([jnp.where(first, q, 0.0), jnp.where(first, 0.0, q)], axis=0).astype(BF16)
                s = jnp.where(jnp.concatenate([valid, valid], axis=0), _dot(q2, kcat, NT), MASKED)
                m = jnp.max(s, axis=-1, keepdims=True)
                p = jnp.exp(s - m)
                den = jnp.sum(p, axis=-1, keepdims=True)
                o2 = _dot(p.astype(BF16), vcat, NN) / den
                lse2 = m + jnp.log(den)
                o_ref[0, 0, cur, :] = jnp.where(first, o2[:BLOCK], o2[BLOCK:])
                lse_ref[0, 0, cur, :] = jnp.where(first, lse2[:BLOCK], lse2[BLOCK:])
                return carry

            lax.fori_loop(0, nb * dil, block, 0, unroll=ATTN_FWD_UNROLL)

        for gi, dil in enumerate(DILATIONS):
            pl.when(pl.program_id(0) == gi)(functools.partial(group, dil))

    def slab(base):
        return pl.BlockSpec((1, s_len, LANES), lambda g, s: (base + 2 * g + s, 0, 0))

    out = pl.BlockSpec((1, 1, s_len, LANES), lambda g, s: (g, s, 0, 0))
    shape = jax.ShapeDtypeStruct((n_g, 2, s_len, LANES), F32)
    return pl.pallas_call(
        body, name="attn_fwd", grid=(n_g, 2),
        in_specs=[slab(0), slab(6), slab(12)], out_specs=[out, out], out_shape=[shape, shape],
        compiler_params=_params(("arbitrary", "arbitrary")),
    )(qkv, qkv, qkv)


def _pool_mixed(u, halo, i, tm):
    ue = jnp.concatenate([halo, u], axis=0)
    s2 = ue + pltpu.roll(ue, 1, 0)
    s4 = s2 + pltpu.roll(s2, 2, 0)
    s8 = s4 + pltpu.roll(s4, 4, 0)
    s16 = s8 + pltpu.roll(s8, 8, 0)
    grp = lax.broadcasted_iota(jnp.int32, (tm, 256), 1) // HEAD_DIM
    pick = lambda a, b, c, e: jnp.where(grp == 0, a, jnp.where(grp == 1, b, jnp.where(grp == 2, c, e)))
    win_sum = pick(s2[HALO:], s4[HALO:], s8[HALO:], s16[HALO:])
    pos = (i * tm + lax.broadcasted_iota(jnp.int32, (tm, 256), 0)).astype(F32)
    count = jnp.minimum(pos + 1.0, pick(*[float(w) for w in POOL_WINDOWS]))
    return win_sum / count - u, count


def _mix_out(x, u_pool, o_g, lse_g, w_blk, b_pool, pool_scale, w_out_t, gt_m, g_post_mix, g_pre_ffn, sc_f, sh_f, tm):
    s_len, d = x.shape

    def body(x_ref, u_ref, uh_ref, o_ref, l_ref, wb_ref, bp_ref, ps_ref, wo_ref,
             gt_ref, g1_ref, g2_ref, sc_ref, sh_ref,
             x1_ref, y1_ref, h2_ref, cat_ref, attn_ref, lall_ref):
        (o0, o1, o2), (l0, l1, l2) = (o_ref.at[g] for g in range(3)), (l_ref.at[g] for g in range(3))
        i = pl.program_id(0)
        u = u_ref[...]
        halo = uh_ref[...] * (i > 0).astype(F32)
        mixed, _ = _pool_mixed(u, halo, i, tm)
        y = _dot(mixed.astype(BF16), wb_ref[...], NN) + bp_ref[...]
        pool = y * ps_ref[...]
        attn = []
        for s in range(2):
            la, lb, lc = l0[s], l1[s], l2[s]
            mx = jnp.maximum(jnp.maximum(la, lb), lc)
            ea, eb, ec = jnp.exp(la - mx), jnp.exp(lb - mx), jnp.exp(lc - mx)
            den = ea + eb + ec
            lall_ref[s] = mx + jnp.log(den)
            attn.append((ea / den) * o0[s] + (eb / den) * o1[s] + (ec / den) * o2[s])
        attn = jnp.concatenate(attn, axis=1)
        attn_ref[...] = attn
        cat = jnp.concatenate([pool, attn], axis=1).astype(BF16)
        cat_ref[...] = cat
        y1 = _dot(cat, wo_ref[...], NT)
        y1_ref[...] = y1.astype(BF16)
        x1 = x_ref[...] + gt_ref[...] * (y1 * _rstd(y1) * g1_ref[...])
        x1_ref[...] = x1
        h2 = (x1 * _rstd(x1) * g2_ref[...]) * (1.0 + sc_ref[...]) + sh_ref[...]
        h2_ref[...] = h2.astype(BF16)

    tile = lambda w: pl.BlockSpec((tm, w), lambda i: (i, 0))
    slab = pl.BlockSpec((2, tm, LANES), lambda i: (0, i, 0))
    groups = pl.BlockSpec((len(DILATIONS), 2, tm, LANES), lambda i: (0, 0, i, 0))
    const = lambda a: pl.BlockSpec(a.shape, lambda i: (0,) * a.ndim)
    return pl.pallas_call(
        body, name="mix_out", grid=(s_len // tm,),
        in_specs=[tile(d), tile(256), pl.BlockSpec((HALO, 256), lambda i: (_halo_before(i, tm), 0)),
                  groups, groups,
                  const(w_blk), const(b_pool), const(pool_scale), const(w_out_t),
                  const(gt_m), const(g_post_mix), const(g_pre_ffn), const(sc_f), const(sh_f)],
        out_specs=[tile(d), tile(d), tile(d), tile(512), tile(256), slab],
        out_shape=[jax.ShapeDtypeStruct((s_len, d), F32), jax.ShapeDtypeStruct((s_len, d), BF16),
                   jax.ShapeDtypeStruct((s_len, d), BF16), jax.ShapeDtypeStruct((s_len, 512), BF16),
                   jax.ShapeDtypeStruct((s_len, 256), F32), jax.ShapeDtypeStruct((2, s_len, LANES), F32)],
        compiler_params=_params(("arbitrary",)),
    )(x, u_pool, u_pool, o_g, lse_g, w_blk, b_pool, pool_scale, w_out_t, gt_m, g_post_mix, g_pre_ffn, sc_f, sh_f)


def _conv_gate(gate_ext, cw, cb):
    gc = gate_ext * cw[2:3, :] + pltpu.roll(gate_ext, 1, 0) * cw[1:2, :] + pltpu.roll(gate_ext, 2, 0) * cw[0:1, :]
    return gc[HALO:] + cb


def _ffn_fwd_loss(h2, x1, target, w_up_t, w_down, conv_w, conv_b, gt_f, g_post_ffn, tm, ck):
    s_len, d = x1.shape
    d_ff = w_down.shape[0]
    n_t, n_c = s_len // tm, d_ff // ck

    def body(h_ref, hh_ref, x1_ref, tgt_ref, wg_ref, wv_ref, wd_ref, cw_ref, cb_ref, gt_ref, g_ref,
             gate_ref, a_ref, act_ref, vd_ref, dy2_ref, dout_ref, sums_ref, loss_ref, acc_ref):
        i = pl.program_id(0)

        @pl.when(i == 0)
        def _():
            sums_ref[...] = jnp.zeros_like(sums_ref)
            loss_ref[...] = jnp.zeros_like(loss_ref)
            acc_ref[...] = jnp.zeros_like(acc_ref)

        def finish(live):
            y2 = acc_ref[...]
            rstd = _rstd(y2)
            n = y2 * rstd
            rn = n * g_ref[...]
            err = x1_ref[...] + gt_ref[...] * rn - tgt_ref[...]
            keep = lambda v: jnp.where(live, v, 0.0)
            loss_ref[...] += keep(0.5 * jnp.sum(jnp.mean(err * err, axis=-1, keepdims=True), axis=0, keepdims=True))
            dout = err * (1.0 / d)
            dout_ref[...] = dout.astype(BF16)
            drn = dout * gt_ref[...]
            sums_ref[0:1, :] += keep(jnp.sum(dout * rn, axis=0, keepdims=True))
            sums_ref[1:2, :] += keep(jnp.sum(drn * n, axis=0, keepdims=True))
            dy2_ref[...] = _norm_bwd(drn * g_ref[...], n, rstd).astype(BF16)

        @pl.when(i < n_t)
        def _():
            h = h_ref[...]
            h_ext = jnp.concatenate([hh_ref[...], h], axis=0)
            row = lax.broadcasted_iota(jnp.int32, (tm + HALO, ck), 0)
            no_halo = (row < HALO) & (i == 0)

            def up(c):
                cs = slice(c * ck, (c + 1) * ck)
                return jnp.where(no_halo, 0.0, _dot(h_ext, wg_ref[cs, :], NT)), _dot(h, wv_ref[cs, :], NT)

            part = None
            nxt = up(0)
            finish(i > 0)
            for c in range(n_c):
                cs = slice(c * ck, (c + 1) * ck)
                gate_ext, val = nxt
                if c + 1 < n_c:
                    nxt = up(c + 1)
                act, dact = _gelu_parts(_conv_gate(gate_ext, cw_ref[:, cs], cb_ref[:, cs]))
                a = (act * val).astype(BF16)
                gate_ref[:, cs] = gate_ext[HALO:].astype(BF16)
                a_ref[:, cs] = a
                act_ref[:, cs] = act.astype(BF16)
                vd_ref[:, cs] = (val * dact).astype(BF16)
                p = _dot(a, wd_ref[cs, :], NN)
                part = p if part is None else part + p
            acc_ref[...] = part

        @pl.when(i == n_t)
        def _():
            finish(True)

    this = lambda i: jnp.minimum(i, n_t - 1)
    before = lambda i: jnp.maximum(i - 1, 0)
    tok = lambda w, at: pl.BlockSpec((tm, w), lambda i: (at(i), 0))
    vec = pl.BlockSpec((1, d), lambda i: (0, 0))
    once = lambda shape, imap: pl.BlockSpec(shape, imap, pipeline_mode=pl.Buffered(1))
    return pl.pallas_call(
        body, name="ffn_fwd_loss", grid=(n_t + 1,),
        in_specs=[tok(d, this), pl.BlockSpec((HALO, d), lambda i: (_halo_before(this(i), tm), 0)),
                  tok(d, before), tok(d, before),
                  once((d_ff, d), lambda i: (0, 0)), once((d_ff, d), lambda i: (1, 0)), once((d_ff, d), lambda i: (0, 0)),
                  pl.BlockSpec((3, d_ff), lambda i: (0, 0)), pl.BlockSpec((1, d_ff), lambda i: (0, 0)), vec, vec],
        out_specs=[tok(d_ff, this)] * 4 + [tok(d, before), tok(d, before), pl.BlockSpec((8, d), lambda i: (0, 0)),
                                          pl.BlockSpec((8, LANES), lambda i: (0, 0))],
        out_shape=[jax.ShapeDtypeStruct((s_len, d_ff), BF16)] * 4
        + [jax.ShapeDtypeStruct((s_len, d), BF16), jax.ShapeDtypeStruct((s_len, d), BF16),
           jax.ShapeDtypeStruct((8, d), F32), jax.ShapeDtypeStruct((8, LANES), F32)],
        scratch_shapes=[pltpu.VMEM((tm, d), F32)],
        compiler_params=_params(("arbitrary",)),
    )(h2, h2, x1, target, w_up_t, w_up_t, w_down, conv_w, conv_b, gt_f, g_post_ffn)


def _ffn_bwd_act(dy2, gate, a, act, vd, w_down, tm, tf, ck):
    s_len, d = dy2.shape
    d_ff = w_down.shape[0]
    n_t = s_len // tm
    chunks = [slice(lo, min(lo + ck, tf)) for lo in range(0, tf, ck)]

    def body(dy_ref, g_ref, gh_ref, a_ref, act_ref, vd_ref, wd_ref, dgc_ref, dval_ref, dwd_ref, dconv_ref, acc_ref):
        i = pl.program_id(1)

        @pl.when(i == 0)
        def _():
            acc_ref[...] = jnp.zeros_like(acc_ref)
            dconv_ref[...] = jnp.zeros_like(dconv_ref)

        dy = dy_ref[...]

        def down(cs):
            return _dot(dy, wd_ref[cs, :], NT)

        nxt = down(chunks[0])
        for c, cs in enumerate(chunks):
            width = cs.stop - cs.start
            da = nxt
            if c + 1 < len(chunks):
                nxt = down(chunks[c + 1])
            acc_ref[cs, :] += _dot(a_ref[:, cs], dy, TN)
            row = lax.broadcasted_iota(jnp.int32, (tm + HALO, width), 0)
            gate_ext = jnp.where((row < HALO) & (i == 0), 0.0,
                                 jnp.concatenate([gh_ref[:, cs], g_ref[:, cs]], axis=0).astype(F32))
            dgc = da * vd_ref[:, cs].astype(F32)
            dgc_ref[:, cs] = dgc.astype(BF16)
            dval_ref[:, cs] = (da * act_ref[:, cs].astype(F32)).astype(BF16)
            rows = [jnp.sum(dgc * pltpu.roll(gate_ext, 2 - k, 0)[HALO:], axis=0, keepdims=True) for k in range(2)]
            rows += [jnp.sum(dgc * gate_ext[HALO:], axis=0, keepdims=True), jnp.sum(dgc, axis=0, keepdims=True),
                     jnp.zeros((4, width), F32)]
            dconv_ref[:, cs] += jnp.concatenate(rows, axis=0)

        @pl.when(i == n_t - 1)
        def _():
            dwd_ref[...] = acc_ref[...].astype(BF16)

    tokf = pl.BlockSpec((tm, tf), lambda j, i: (i, j))
    return pl.pallas_call(
        body, name="ffn_bwd_act", grid=(d_ff // tf, n_t),
        in_specs=[pl.BlockSpec((tm, d), lambda j, i: (i, 0)), tokf,
                  pl.BlockSpec((HALO, tf), lambda j, i: (_halo_before(i, tm), j)), tokf, tokf, tokf,
                  pl.BlockSpec((tf, d), lambda j, i: (j, 0))],
        out_specs=[tokf, tokf, pl.BlockSpec((tf, d), lambda j, i: (j, 0)), pl.BlockSpec((8, tf), lambda j, i: (0, j))],
        out_shape=[jax.ShapeDtypeStruct((s_len, d_ff), BF16), jax.ShapeDtypeStruct((s_len, d_ff), BF16),
                   jax.ShapeDtypeStruct((d_ff, d), BF16), jax.ShapeDtypeStruct((8, d_ff), F32)],
        scratch_shapes=[pltpu.VMEM((tf, d), F32)],
        compiler_params=_params(("arbitrary", "arbitrary")),
    )(dy2, gate, gate, a, act, vd, w_down)


def _ffn_bwd_up(dgc, dval, w_up_t, conv_w, after, tm):
    s_len, d_ff = dgc.shape
    d = w_up_t.shape[1]
    n_t = s_len // tm

    def body(dg_ref, dgn_ref, dv_ref, cw_ref, w_ref, after_ref, dup_ref, dh_ref):
        i = pl.program_id(0)
        nxt = dgn_ref[...].astype(F32) * (i < n_t - 1).astype(F32)
        ext = jnp.concatenate([dg_ref[...].astype(F32), nxt], axis=0)
        rows = tm + HALO
        dgate = (ext * cw_ref[2:3, :] + pltpu.roll(ext, rows - 1, 0) * cw_ref[1:2, :]
                 + pltpu.roll(ext, rows - 2, 0) * cw_ref[0:1, :])[:tm]
        dup = jnp.concatenate([dgate.astype(BF16), dv_ref[...]], axis=1)
        dup_ref[...] = dup
        dh_ref[...] = _dot(dup, w_ref[...], NN).astype(BF16)

    tokf = pl.BlockSpec((tm, d_ff), lambda i: (i, 0))
    return pl.pallas_call(
        body, name="ffn_bwd_up", grid=(n_t,),
        in_specs=[tokf, pl.BlockSpec((HALO, d_ff), lambda i: (jnp.minimum((i + 1) * (tm // HALO), s_len // HALO - 1), 0)),
                  tokf, pl.BlockSpec((3, d_ff), lambda i: (0, 0)), pl.BlockSpec((2 * d_ff, d), lambda i: (0, 0)),
                  pl.BlockSpec(memory_space=pl.ANY)],
        out_specs=[pl.BlockSpec((tm, 2 * d_ff), lambda i: (i, 0)), pl.BlockSpec((tm, d), lambda i: (i, 0))],
        out_shape=[jax.ShapeDtypeStruct((s_len, 2 * d_ff), BF16), jax.ShapeDtypeStruct((s_len, d), BF16)],
        compiler_params=_params(("arbitrary",)),
    )(dgc, dgc, dval, conv_w, w_up_t, after)


def _mix_bwd(dh2, dout, x1, y1, cat, attn, w_out_t, sc_f, g_pre_ffn, gt_m, g_post_mix, after, tm):
    s_len, d = x1.shape
    n_t = s_len // tm

    def body(dh_ref, do_ref, x1_ref, y1_ref, cat_ref, at_ref, wo_ref, sc_ref, g2_ref, gt_ref, g1_ref, after_ref,
             dx1_ref, dpool_ref, dattn_ref, delta_ref, dwo_ref, sums_ref, acc_ref):
        i = pl.program_id(0)
        dh = dh_ref[...].astype(F32)
        x1 = x1_ref[...]
        r2 = _rstd(x1)
        n2 = x1 * r2
        ng = n2 * g2_ref[...]
        dng = dh * (1.0 + sc_ref[...])
        dx1 = do_ref[...].astype(F32) + _norm_bwd(dng * g2_ref[...], n2, r2)
        dx1_ref[...] = dx1.astype(BF16)
        y1 = y1_ref[...].astype(F32)
        r1 = _rstd(y1)
        n1 = y1 * r1
        drn = dx1 * gt_ref[...]
        dy1 = _norm_bwd(drn * g1_ref[...], n1, r1).astype(BF16)
        dcat = _dot(dy1, wo_ref[...], NN)
        dpool_ref[...] = dcat[:, 0:256]
        lane = lax.broadcasted_iota(jnp.int32, (tm, LANES), 1)
        first = lane < HEAD_DIM
        for s in range(2):
            da = dcat[:, 256 + s * LANES:256 + (s + 1) * LANES]
            dattn_ref[s] = da
            prod = da * at_ref[:, s * LANES:(s + 1) * LANES]
            tot = jnp.sum(prod, axis=-1, keepdims=True)
            lo = jnp.sum(jnp.where(first, prod, 0.0), axis=-1, keepdims=True)
            delta_ref[s] = jnp.where(first, lo, tot - lo)
        dwo = _dot(dy1, cat_ref[...], TN)
        sums = jnp.concatenate(
            [jnp.sum(dh, axis=0, keepdims=True), jnp.sum(dh * ng, axis=0, keepdims=True),
             jnp.sum(dng * n2, axis=0, keepdims=True), jnp.sum(dx1 * (n1 * g1_ref[...]), axis=0, keepdims=True),
             jnp.sum(drn * n1, axis=0, keepdims=True), jnp.zeros((3, d), F32)], axis=0)

        @pl.when(i == 0)
        def _():
            acc_ref[...] = dwo
            sums_ref[...] = sums

        @pl.when(i > 0)
        def _():
            acc_ref[...] += dwo
            sums_ref[...] += sums

        @pl.when(i == n_t - 1)
        def _():
            dwo_ref[...] = acc_ref[...].astype(BF16)

    tile = lambda w: pl.BlockSpec((tm, w), lambda i: (i, 0))
    slab = pl.BlockSpec((2, tm, LANES), lambda i: (0, i, 0))
    vec = pl.BlockSpec((1, d), lambda i: (0, 0))
    return pl.pallas_call(
        body, name="mix_bwd", grid=(n_t,),
        in_specs=[tile(d), tile(d), tile(d), tile(d), tile(512), tile(256),
                  pl.BlockSpec((d, 512), lambda i: (0, 0)), vec, vec, vec, vec, pl.BlockSpec(memory_space=pl.ANY)],
        out_specs=[tile(d), tile(256), slab, slab, pl.BlockSpec((d, 512), lambda i: (0, 0)),
                   pl.BlockSpec((8, d), lambda i: (0, 0))],
        out_shape=[jax.ShapeDtypeStruct((s_len, d), BF16), jax.ShapeDtypeStruct((s_len, 256), F32),
                   jax.ShapeDtypeStruct((2, s_len, LANES), F32), jax.ShapeDtypeStruct((2, s_len, LANES), F32),
                   jax.ShapeDtypeStruct((d, 512), BF16), jax.ShapeDtypeStruct((8, d), F32)],
        scratch_shapes=[pltpu.VMEM((d, 512), F32)],
        compiler_params=_params(("arbitrary",)),
    )(dh2, dout, x1, y1, cat, attn, w_out_t, sc_f, g_pre_ffn, gt_m, g_post_mix, after)


def _pool_bwd(dpool, u_pool, w_blk, b_pool, pool_scale, tm):
    s_len = dpool.shape[0]
    n_t = s_len // tm

    def body(dp_ref, dpn_ref, u_ref, uh_ref, wb_ref, bp_ref, ps_ref, du_ref, dwp_ref, sums_ref, acc_ref):
        i = pl.program_id(0)
        u = u_ref[...]
        mixed, _ = _pool_mixed(u, uh_ref[...] * (i > 0).astype(F32), i, tm)
        mixed_b = mixed.astype(BF16)
        y = _dot(mixed_b, wb_ref[...], NN) + bp_ref[...]
        dp = dp_ref[...]
        dy = dp * ps_ref[...]
        dwb = _dot(mixed_b, dy.astype(BF16), TN)
        sums = jnp.concatenate([jnp.sum(dy, axis=0, keepdims=True), jnp.sum(dp * y, axis=0, keepdims=True),
                                jnp.zeros((6, 256), F32)], axis=0)
        dp_ext = jnp.concatenate([dp, dpn_ref[...] * (i < n_t - 1).astype(F32)], axis=0)
        dmix = _dot((dp_ext * ps_ref[...]).astype(BF16), wb_ref[...], NT)
        rows = tm + HALO
        grp = lax.broadcasted_iota(jnp.int32, (rows, 256), 1) // HEAD_DIM
        pick = lambda a, b, c, e: jnp.where(grp == 0, a, jnp.where(grp == 1, b, jnp.where(grp == 2, c, e)))
        pos = (i * tm + lax.broadcasted_iota(jnp.int32, (rows, 256), 0)).astype(F32)
        z = dmix / jnp.minimum(pos + 1.0, pick(*[float(w) for w in POOL_WINDOWS]))
        f2 = z + pltpu.roll(z, rows - 1, 0)
        f4 = f2 + pltpu.roll(f2, rows - 2, 0)
        f8 = f4 + pltpu.roll(f4, rows - 4, 0)
        f16 = f8 + pltpu.roll(f8, rows - 8, 0)
        du_ref[...] = (pick(f2, f4, f8, f16) - dmix)[:tm]

        @pl.when(i == 0)
        def _():
            acc_ref[...] = dwb
            sums_ref[...] = sums

        @pl.when(i > 0)
        def _():
            acc_ref[...] += dwb
            sums_ref[...] += sums

        @pl.when(i == n_t - 1)
        def _():
            full = acc_ref[...]
            for gi in range(len(POOL_WINDOWS)):
                lo = gi * HEAD_DIM
                dwp_ref[gi] = full[lo:lo + HEAD_DIM, lo:lo + HEAD_DIM]

    n_g = len(POOL_WINDOWS)
    tile = pl.BlockSpec((tm, 256), lambda i: (i, 0))
    const = lambda a: pl.BlockSpec(a.shape, lambda i: (0,) * a.ndim)
    return pl.pallas_call(
        body, name="pool_bwd", grid=(n_t,),
        in_specs=[tile, pl.BlockSpec((HALO, 256), lambda i: (jnp.minimum((i + 1) * (tm // HALO), s_len // HALO - 1), 0)),
                  tile, pl.BlockSpec((HALO, 256), lambda i: (_halo_before(i, tm), 0)),
                  const(w_blk), const(b_pool), const(pool_scale)],
        out_specs=[tile, pl.BlockSpec((n_g, HEAD_DIM, HEAD_DIM), lambda i: (0, 0, 0)), pl.BlockSpec((8, 256), lambda i: (0, 0))],
        out_shape=[jax.ShapeDtypeStruct((s_len, 256), F32), jax.ShapeDtypeStruct((n_g, HEAD_DIM, HEAD_DIM), F32),
                   jax.ShapeDtypeStruct((8, 256), F32)],
        scratch_shapes=[pltpu.VMEM((256, 256), F32)],
        compiler_params=_params(("arbitrary",)),
    )(dpool, dpool, u_pool, u_pool, w_blk, b_pool, pool_scale)


def _attn_bwd(qkv, dattn, lse_all, delta, after):
    s_len = qkv.shape[1]
    n_g = len(DILATIONS)

    def body(q_ref, k_ref, v_ref, do_ref, l_ref, dl_ref, after_ref, dq_ref, dk_ref, dv_ref):
        lane = lax.broadcasted_iota(jnp.int32, (BLOCK, LANES), 1)
        first = lane < HEAD_DIM

        def group(dil):
            nb = s_len // (BLOCK * dil)

            def block(t, carry):
                dk_part, dv_part = carry
                r, n = t // nb, t % nb
                cur = _block_rows(n, r, dil)
                prev = _block_rows(jnp.maximum(n - 1, 0), r, dil)
                q = q_ref[0, cur, :]
                do = do_ref[0, cur, :]
                lse = l_ref[0, cur, :]
                dlt = dl_ref[0, cur, :]
                kcat = jnp.concatenate([k_ref[0, prev, :], k_ref[0, cur, :]], axis=0).astype(BF16)
                vcat = jnp.concatenate([v_ref[0, prev, :], v_ref[0, cur, :]], axis=0).astype(BF16)
                valid = _band_mask(n)
                stack = lambda a: jnp.concatenate([jnp.where(first, a, 0.0), jnp.where(first, 0.0, a)], axis=0)
                rows2 = lambda a: jnp.concatenate([a[:, 0:1], a[:, HEAD_DIM:HEAD_DIM + 1]], axis=0)
                q2, do2 = stack(q).astype(BF16), stack(do).astype(BF16)
                valid2 = jnp.concatenate([valid, valid], axis=0)
                p = jnp.where(valid2, jnp.exp(_dot(q2, kcat, NT) - rows2(lse)), 0.0)
                ds = (p * (_dot(do2, vcat, NT) - rows2(dlt))).astype(BF16)
                dq2 = _dot(ds, kcat, NN)
                dq_ref[0, 0, cur, :] = jnp.where(first, dq2[:BLOCK], dq2[BLOCK:])
                dkc = _dot(ds, q2, TN)
                dvc = _dot(p.astype(BF16), do2, TN)
                dk_ref[0, 0, prev, :] = dk_part + dkc[:BLOCK]
                dv_ref[0, 0, prev, :] = dv_part + dvc[:BLOCK]
                dk_ref[0, 0, cur, :] = dkc[BLOCK:]
                dv_ref[0, 0, cur, :] = dvc[BLOCK:]
                return dkc[BLOCK:], dvc[BLOCK:]

            def blocks(tt, carry):
                for u in range(ATTN_BWD_UNROLL):
                    carry = block(tt * ATTN_BWD_UNROLL + u, carry)
                return carry

            zero = jnp.zeros((BLOCK, LANES), F32)
            lax.fori_loop(0, nb * dil // ATTN_BWD_UNROLL, blocks, (zero, zero))

        for gi, dil in enumerate(DILATIONS):
            pl.when(pl.program_id(1) == gi)(functools.partial(group, dil))

    def slab(base):
        return pl.BlockSpec((1, s_len, LANES), lambda s, g: (base + 2 * g + s, 0, 0))

    one = pl.BlockSpec((1, s_len, LANES), lambda s, g: (s, 0, 0))
    out = pl.BlockSpec((1, 1, s_len, LANES), lambda s, g: (g, s, 0, 0))
    shape = jax.ShapeDtypeStruct((n_g, 2, s_len, LANES), F32)
    return pl.pallas_call(
        body, name="attn_bwd", grid=(2, n_g),
        in_specs=[slab(0), slab(6), slab(12), one, one, one, pl.BlockSpec(memory_space=pl.ANY)],
        out_specs=[out, out, out], out_shape=[shape, shape, shape],
        compiler_params=_params(("arbitrary", "arbitrary")),
    )(qkv, qkv, qkv, dattn, lse_all, delta, after)


def _dproj_wgrad_in(du, dqkv, rope, h1, tm, cm):
    s_len = du.shape[0]
    d = h1.shape[1]
    n_proj = 256 + 18 * LANES
    n_t = s_len // tm

    def body(du_ref, dq_ref, dk_ref, dv_ref, cs_ref, spread_ref, h_ref, dproj_ref, dw_ref, acc_ref):
        i = pl.program_id(0)

        @pl.when(i == 0)
        def _():
            acc_ref[...] = jnp.zeros_like(acc_ref)

        dproj_ref[:, 0:256] = du_ref[...].astype(BF16)
        lanes = _rope_lanes(cs_ref, spread_ref)
        col = 256
        for kind, dref in enumerate((dq_ref, dk_ref, dv_ref)):
            for grp in range(3):
                for s in range(2):
                    piece = dref[grp, s]
                    if kind < 2:
                        piece = _rope_bwd(piece, lanes)
                    if kind == 0:
                        piece = piece * (HEAD_DIM ** -0.5)
                    dproj_ref[:, col:col + LANES] = piece.astype(BF16)
                    col += LANES

        for c0 in range(0, n_proj, cm):
            acc_ref[c0:c0 + cm, :] += _dot(dproj_ref[:, c0:c0 + cm], h_ref[...], TN)

        @pl.when(i == n_t - 1)
        def _():
            dw_ref[...] = acc_ref[...].astype(BF16)

    groups = pl.BlockSpec((len(DILATIONS), 2, tm, LANES), lambda i: (0, 0, i, 0))
    return pl.pallas_call(
        body, name="dproj_wgrad_in", grid=(n_t,),
        in_specs=[pl.BlockSpec((tm, 256), lambda i: (i, 0))] + [groups] * 3
        + [pl.BlockSpec((rope[0].shape[0], tm), lambda i: (0, i)), pl.BlockSpec(rope[1].shape, lambda i: (0, 0, 0)),
           pl.BlockSpec((tm, d), lambda i: (i, 0))],
        out_specs=[pl.BlockSpec((tm, n_proj), lambda i: (i, 0)), pl.BlockSpec((n_proj, d), lambda i: (0, 0))],
        out_shape=[jax.ShapeDtypeStruct((s_len, n_proj), BF16), jax.ShapeDtypeStruct((n_proj, d), BF16)],
        scratch_shapes=[pltpu.VMEM((n_proj, d), F32)],
        compiler_params=_params(("arbitrary",)),
    )(du, *dqkv, *rope, h1)


def _inproj_bwd(dproj, w_in_t, x, dx1, sc_m, g_pre_mix, after, tm):
    s_len, d = x.shape
    n_proj = w_in_t.shape[0]
    n_t = s_len // tm

    def body(dproj_ref, w_ref, x_ref, dx1_ref, sc_ref, g_ref, after_ref, dx_ref, sums_ref):
        i = pl.program_id(0)
        halves = [slice(0, tm // 2), slice(tm // 2, tm)]
        dhs = [_dot(dproj_ref[rs, :], w_ref[...], NN) for rs in halves]
        sums = None
        for rs, dh in zip(halves, dhs):
            xv = x_ref[rs, :]
            r = _rstd(xv)
            n = xv * r
            dng = dh * (1.0 + sc_ref[...])
            dx_ref[rs, :] = dx1_ref[rs, :].astype(F32) + _norm_bwd(dng * g_ref[...], n, r)
            part = jnp.concatenate([jnp.sum(dh, axis=0, keepdims=True), jnp.sum(dh * (n * g_ref[...]), axis=0, keepdims=True),
                                    jnp.sum(dng * n, axis=0, keepdims=True), jnp.zeros((5, d), F32)], axis=0)
            sums = part if sums is None else sums + part

        @pl.when(i == 0)
        def _():
            sums_ref[...] = sums

        @pl.when(i > 0)
        def _():
            sums_ref[...] += sums

    tile = lambda w: pl.BlockSpec((tm, w), lambda i: (i, 0))
    vec = pl.BlockSpec((1, d), lambda i: (0, 0))
    return pl.pallas_call(
        body, name="inproj_bwd", grid=(n_t,),
        in_specs=[tile(n_proj), pl.BlockSpec((n_proj, d), lambda i: (0, 0)), tile(d), tile(d), vec, vec,
                  pl.BlockSpec(memory_space=pl.ANY)],
        out_specs=[tile(d), pl.BlockSpec((8, d), lambda i: (0, 0))],
        out_shape=[jax.ShapeDtypeStruct((s_len, d), F32), jax.ShapeDtypeStruct((8, d), F32)],
        compiler_params=_params(("arbitrary",)),
    )(dproj, w_in_t, x, dx1, sc_m, g_pre_mix, after)


def _wgrad(a, b, name, tk, tmm):
    s_len, m = a.shape
    n = b.shape[1]
    n_k = s_len // tk

    def body(a_ref, b_ref, o_ref, acc_ref):
        k = pl.program_id(1)
        part = _dot(a_ref[...], b_ref[...], TN)

        @pl.when(k == 0)
        def _():
            acc_ref[...] = part

        @pl.when(k > 0)
        def _():
            acc_ref[...] += part

        @pl.when(k == n_k - 1)
        def _():
            o_ref[...] = acc_ref[...].astype(BF16)

    return pl.pallas_call(
        body, name=name, grid=(m // tmm, n_k),
        in_specs=[pl.BlockSpec((tk, tmm), lambda j, k: (k, j)), pl.BlockSpec((tk, n), lambda j, k: (k, 0))],
        out_specs=pl.BlockSpec((tmm, n), lambda j, k: (j, 0)),
        out_shape=jax.ShapeDtypeStruct((m, n), BF16),
        scratch_shapes=[pltpu.VMEM((tmm, n), F32)],
        compiler_params=_params(("arbitrary", "arbitrary")),
    )(a, b)


def _place():
    return lax.axis_index("x"), lax.axis_index("y"), lax.axis_index("c")


def _peer(k):
    x, y, c = _place()
    bx, by, bc = (k >> 2) & 1, (k >> 1) & 1, k & 1
    return (x ^ bx if bx else x, y ^ by if by else y, c ^ bc if bc else c)


def _index(pos):
    return 4 * pos[0] + 2 * pos[1] + pos[2]


def _entry_exchange(c_rows, w_ada, b_ada, taps, shards, later):
    d = c_rows.shape[1]
    ncol = w_ada.shape[1]
    n_w, n_p = len(shards), len(later)

    def body(c_ref, w_ref, b_ref, t_ref, *rest):
        srcs, rest = rest[:n_w], rest[n_w:]
        later_refs, rest = rest[:n_p], rest[n_p:]
        (call_ref, mod_ref, tall_ref), rest = rest[:3], rest[3:]
        outs, rest = rest[:n_w], rest[n_w:]
        zones, rest = rest[:n_p], rest[n_p:]
        stage_ref, s_send, s_recv, w_send, w_recv, local_sems = rest[:6]
        wide, narrow, place_sems = rest[6:6 + n_p], rest[6 + n_p:6 + 2 * n_p], rest[6 + 2 * n_p]
        x, y, c = _place()
        here, sibling = (x, y, c), (x, y, 1 - c)
        chips = [(1 - x, y), (x, 1 - y), (1 - x, 1 - y)]
        me = _index(here)

        def small(kind, src, dst, k):
            return pltpu.make_async_remote_copy(src_ref=src, dst_ref=dst, send_sem=s_send.at[kind, k - 1],
                                                recv_sem=s_recv.at[kind, k - 1], device_id=_peer(k), device_id_type=MESH)

        gather = lambda k: small(0, c_ref, call_ref.at[me], k)
        scatter = lambda k: small(1, stage_ref.at[_index(_peer(k))], mod_ref.at[me], k)
        gather_taps = lambda k: small(2, t_ref, tall_ref.at[me], k)

        def rows(w, pos):
            r = shards[w].shape[0]
            return outs[w].at[pl.ds(pl.multiple_of(_index(pos) * r, 16), r), :]

        def block(k, w, pos, to, own=False):
            return pltpu.make_async_remote_copy(
                src_ref=srcs[w] if own else rows(w, pos), dst_ref=rows(w, pos),
                send_sem=w_send.at[k, w], recv_sem=w_recv.at[k, w], device_id=to, device_id_type=MESH)

        call_ref[me] = c_ref[...]
        tall_ref[me] = t_ref[...]
        for k in range(1, N_DEV):
            gather(k).start()
        for k in range(1, N_DEV):
            gather_taps(k).start()
        mine = [pltpu.make_async_copy(srcs[w], rows(w, here), local_sems.at[w]) for w in range(n_w)]
        for cp in mine:
            cp.start()
        first = [block(0, w, here, sibling, own=True) for w in range(n_w)]
        first += [block(1 + j, w, here, (*chip, c), own=True) for j, chip in enumerate(chips) for w in range(n_w)]
        for cp in first:
            cp.start()
        fetch = [pltpu.make_async_copy(later_refs[w], wide[w], place_sems.at[0, w]) for w in range(n_p)]
        for cp in fetch:
            cp.start()

        for k in range(1, N_DEV):
            gather(k).wait_recv()
        cv = jnp.concatenate([call_ref[b, 0:1, :] for b in range(N_DEV)], axis=0)
        act = cv * jax.nn.sigmoid(cv)
        mod = lax.dot_general(act, w_ref[...], NN, preferred_element_type=F32,
                              precision=lax.Precision.HIGHEST) + b_ref[:, pl.ds(pl.multiple_of(me * ncol, LANES), ncol)]
        for b in range(N_DEV):
            stage_ref[b] = jnp.broadcast_to(mod[b:b + 1, :], (8, ncol))
        mod_ref[me] = stage_ref[me]
        for k in range(1, N_DEV):
            scatter(k).start()

        placed = []
        for w in range(n_p):
            fetch[w].wait()
            narrow[w][...] = wide[w][...].astype(BF16)
            r = later[w].shape[0]
            placed.append(pltpu.make_async_copy(narrow[w], zones[w].at[pl.ds(pl.multiple_of(me * r, 16), r), :],
                                                place_sems.at[1, w]))
            placed[-1].start()

        passed = []
        for j, chip in enumerate(chips):
            for w in range(n_w):
                block(1 + j, w, (*chip, c), here).wait_recv()
                fwd = block(4 + j, w, (*chip, c), sibling)
                fwd.start()
                passed.append(fwd)
        for w in range(n_w):
            block(0, w, sibling, here).wait_recv()
        for j, chip in enumerate(chips):
            for w in range(n_w):
                block(4 + j, w, (*chip, 1 - c), here).wait_recv()
        for k in range(1, N_DEV):
            scatter(k).wait_recv()
            gather_taps(k).wait_recv()
        for cp in first + passed:
            cp.wait_send()
        for k in range(1, N_DEV):
            gather(k).wait_send()
            scatter(k).wait_send()
            gather_taps(k).wait_send()
        for cp in mine + placed:
            cp.wait()

    vmem, hbm = pl.BlockSpec(memory_space=pltpu.VMEM), pl.BlockSpec(memory_space=pltpu.HBM)
    out = pl.pallas_call(
        body, name="entry_exchange",
        in_specs=[vmem] * 4 + [hbm] * (n_w + n_p), out_specs=[vmem] * 3 + [hbm] * (n_w + n_p),
        out_shape=[jax.ShapeDtypeStruct((N_DEV, 8, d), F32), jax.ShapeDtypeStruct((N_DEV, 8, ncol), F32),
                   jax.ShapeDtypeStruct((N_DEV,) + taps.shape, F32)]
        + [jax.ShapeDtypeStruct((N_DEV * s.shape[0], s.shape[1]), s.dtype) for s in shards]
        + [jax.ShapeDtypeStruct((N_DEV * s.shape[0], s.shape[1]), BF16) for s in later],
        scratch_shapes=[pltpu.VMEM((N_DEV, 8, ncol), F32), pltpu.SemaphoreType.DMA((3, N_DEV - 1)),
                        pltpu.SemaphoreType.DMA((3, N_DEV - 1)), pltpu.SemaphoreType.DMA((N_DEV - 1, n_w)),
                        pltpu.SemaphoreType.DMA((N_DEV - 1, n_w)), pltpu.SemaphoreType.DMA((n_w,))]
        + [pltpu.VMEM(s.shape, F32) for s in later] + [pltpu.VMEM(s.shape, BF16) for s in later]
        + [pltpu.SemaphoreType.DMA((2, n_p))],
        compiler_params=_params(),
    )(c_rows, w_ada, b_ada, taps, *shards, *later)
    return out[0], out[1], out[2], out[3:3 + n_w], out[3 + n_w:]


def _peer_copies(mode, srcs, lands, send_sems, recv_sems):
    if mode in ("gather_ici", "gather_d2d"):
        x, y, c = _place()
        sibling = (x, y, 1 - c)
        chips = [(1 - x, y), (x, 1 - y), (1 - x, 1 - y)]
        n = len(lands)

        def rows(w, pos):
            r = lands[w].shape[0] // N_DEV
            return lands[w].at[pl.ds(pl.multiple_of(_index(pos) * r, 16), r), :]

        def copy(k, w, src, dst, to):
            return pltpu.make_async_remote_copy(src_ref=src, dst_ref=dst, send_sem=send_sems.at[k * n + w],
                                                recv_sem=recv_sems.at[k * n + w], device_id=to, device_id_type=MESH)

        if mode == "gather_ici":
            targets = [sibling] + [(*chip, c) for chip in chips]
            return [copy(k, w, rows(w, (x, y, c)), rows(w, (x, y, c)), to) for k, to in enumerate(targets) for w in range(n)]
        return [copy(j, w, rows(w, (*chip, c)), rows(w, (*chip, c)), sibling)
                for j, chip in enumerate(chips) for w in range(n)]
    me = _index(_place())
    modes = (mode,) * len(srcs) if isinstance(mode, str) else mode
    copies = []
    for k in range(1, N_DEV):
        peer = _peer(k)
        for w, (src, land) in enumerate(zip(srcs, lands)):
            if modes[w] == "gather":
                r = src.shape[0]
                dst = land.at[pl.ds(pl.multiple_of(me * r, 16), r), :]
            elif modes[w] == "allgather":
                dst = land.at[me]
            else:
                r = src.shape[0] // N_DEV
                src = src.at[pl.ds(pl.multiple_of(_index(peer) * r, 16), r), :]
                dst = land.at[me]
            copies.append(pltpu.make_async_remote_copy(
                src_ref=src, dst_ref=dst, send_sem=send_sems.at[(k - 1) * len(srcs) + w],
                recv_sem=recv_sems.at[(k - 1) * len(srcs) + w],
                device_id=peer, device_id_type=MESH))
    return copies


def _exchange_start(mode, srcs, lands, name):
    n_s, n_a = len(srcs), len(srcs) + len(lands)
    n_cp = _COPIES_PER_ARRAY.get(mode, N_DEV - 1) * len(lands)

    def body(*refs):
        for cp in _peer_copies(mode, refs[:n_s], refs[n_s:n_a], refs[n_a], refs[n_a + 1]):
            cp.start()

    hbm, sem = pl.BlockSpec(memory_space=pltpu.HBM), pl.BlockSpec(memory_space=pltpu.SEMAPHORE)
    arrays = list(srcs) + list(lands)
    out = pl.pallas_call(
        body, name=name,
        out_shape=(pltpu.SemaphoreType.DMA((n_cp,)), pltpu.SemaphoreType.DMA((n_cp,)),
                   *[pltpu.HBM(a.shape, a.dtype) for a in arrays]),
        in_specs=[hbm] * n_a, out_specs=(sem, sem, *[hbm] * n_a),
        input_output_aliases={i: 2 + i for i in range(n_a)},
        compiler_params=pltpu.CompilerParams(has_side_effects=pltpu.SideEffectType.DATAFLOW_SIDE_EFFECTING),
    )(*[pltpu.with_memory_space_constraint(a, pltpu.HBM) for a in arrays])
    return out[0], out[1], out[2:2 + n_s], out[2 + n_s:2 + n_a], out[2]


_COPIES_PER_ARRAY = {"gather_ici": 4, "gather_d2d": 3}


def _exchange_wait(mode, send_sems, recv_sems, srcs, lands, after, name):
    n_s, n_a = len(srcs), len(srcs) + len(lands)

    def body(*refs):
        copies = _peer_copies(mode, refs[:n_s], refs[n_s:n_a], refs[n_a], refs[n_a + 1])
        for cp in copies:
            cp.wait_send()
        for cp in copies:
            cp.wait_recv()

    hbm, sem = pl.BlockSpec(memory_space=pltpu.HBM), pl.BlockSpec(memory_space=pltpu.SEMAPHORE)
    arrays = list(srcs) + list(lands)
    out = pl.pallas_call(
        body, name=name, out_shape=tuple(pltpu.HBM(a.shape, a.dtype) for a in arrays),
        in_specs=[hbm] * n_a + [sem, sem, pl.BlockSpec(memory_space=pl.ANY)], out_specs=tuple([hbm] * n_a),
        input_output_aliases={i: i for i in range(n_a)},
        compiler_params=pltpu.CompilerParams(has_side_effects=pltpu.SideEffectType.DATAFLOW_SIDE_EFFECTING),
    )(*arrays, send_sems, recv_sems, after)
    return out[:n_s], out[n_s:]


SMALL_WEIGHTS = ("b_ada", "g_pre_mix", "g_post_mix", "g_pre_ffn", "g_post_ffn", "w_pool", "b_pool", "pool_scale", "conv_b")


MOD_ROWS = ((0, 0), (0, 1), (1, 3), (1, 0), (1, 1), (2, 0))


def _small_sum_adam(mine, gathered, weights, moms, vels):
    n_l, n_w = len(mine), len(weights)
    d = mine[0].shape[1]

    def body(*refs):
        loc, got = refs[:n_l], refs[n_l:2 * n_l]
        w_refs, m_refs, v_refs = (refs[2 * n_l + k * n_w:2 * n_l + (k + 1) * n_w] for k in range(3))
        outs = refs[2 * n_l + 3 * n_w:]
        dmod_ref, conv_ref, loss_ref = outs[4 * n_w:]
        me = _index(_place())
        part = lambda a, dev: jnp.where(dev == me, loc[a][...], got[a][dev])
        totals = []
        for a in range(n_l):
            tot = part(a, 0)
            for dev in range(1, N_DEV):
                tot = tot + part(a, dev)
            totals.append(tot)
        t_in, t_mix, t_ffn, t_pool, t_blk, t_conv, t_loss = totals
        conv_ref[...] = t_conv
        loss_ref[...] = t_loss
        for dev in range(N_DEV):
            for k, (a, r) in enumerate(MOD_ROWS):
                dmod_ref[dev:dev + 1, k * d:(k + 1) * d] = part(a, dev)[r:r + 1, :]

        def update(idx, g, at=()):
            sel = lambda ref: ref.at[at] if at else ref
            delta, nm, nv = _adam_math(sel(w_refs[idx])[...], g, sel(m_refs[idx])[...], sel(v_refs[idx])[...])
            for k, val in enumerate((g, delta, nm, nv)):
                sel(outs[4 * idx + k])[...] = val

        tots = (t_in, t_mix, t_ffn)
        update(0, jnp.concatenate([tots[a][r:r + 1] for a, r in MOD_ROWS], axis=1))
        update(1, t_in[2:3])
        update(2, t_mix[4:5])
        update(3, t_mix[2:3])
        update(4, t_ffn[1:2])
        for gi in range(len(POOL_WINDOWS)):
            update(5, t_blk[gi], at=(0, gi))
        update(6, jnp.concatenate([t_pool[0:1, gi * HEAD_DIM:(gi + 1) * HEAD_DIM] for gi in range(len(POOL_WINDOWS))], axis=0),
               at=(0,))
        update(7, t_pool[1:2])
        update(8, t_conv[3:4])

    vmem = pl.BlockSpec(memory_space=pltpu.VMEM)
    out = pl.pallas_call(
        body, name="small_sum_adam", in_specs=[vmem] * (2 * n_l + 3 * n_w), out_specs=[vmem] * (4 * n_w + 3),
        out_shape=[jax.ShapeDtypeStruct(w.shape, F32) for w in weights for _ in range(4)]
        + [jax.ShapeDtypeStruct((N_DEV, 6 * d), F32), jax.ShapeDtypeStruct(mine[5].shape, F32),
           jax.ShapeDtypeStruct(mine[6].shape, F32)],
        compiler_params=_params(),
    )(*mine, *gathered, *weights, *moms, *vels)
    return out[:4 * n_w], out[4 * n_w], out[4 * n_w + 1], out[4 * n_w + 2]


def _adam_math(w, g, m, v):
    m = ADAM_B1 * m + (1.0 - ADAM_B1) * g
    v = ADAM_B2 * v + (1.0 - ADAM_B2) * (g * g)
    m_hat = m / (1.0 - ADAM_B1 ** ADAM_STEP)
    v_hat = v / (1.0 - ADAM_B2 ** ADAM_STEP)
    delta = -ADAM_LR * (m_hat / (jnp.sqrt(v_hat) + ADAM_EPS) + ADAM_WD * w)
    return delta, m, v


def _adam(w, g, m, v, name):
    def body(w_ref, g_ref, m_ref, v_ref, d_ref, nm_ref, nv_ref):
        d_ref[...], nm_ref[...], nv_ref[...] = _adam_math(w_ref[...], g_ref[...], m_ref[...], v_ref[...])

    vmem = pl.BlockSpec(memory_space=pltpu.VMEM)
    return pl.pallas_call(
        body, name=name, in_specs=[vmem] * 4, out_specs=[vmem] * 3,
        out_shape=[jax.ShapeDtypeStruct(w.shape, F32)] * 3, compiler_params=_params(),
    )(w, g, m, v)


def _sum_adam(own, parts, w, m, v, me, name, tr):
    _, rows, cols = parts.shape
    turned = w.shape == (cols, rows) and rows != cols
    assert tr == rows or not turned
    n_t = rows // tr

    def body(me_ref, own_ref, p_ref, w_ref, m_ref, v_ref, g_ref, d_ref, nm_ref, nv_ref):
        part = lambda dev: jnp.where(dev == me_ref[0], own_ref[...], p_ref[dev]).astype(F32)
        g = part(0)
        for dev in range(1, N_DEV):
            g = g + part(dev)
        g = g.T if turned else g
        g_ref[...] = g
        d_ref[...], nm_ref[...], nv_ref[...] = _adam_math(w_ref[...], g, m_ref[...], v_ref[...])

    spec = pl.BlockSpec((cols, rows) if turned else (tr, cols), lambda i, me_ref: (i, 0))
    shape = jax.ShapeDtypeStruct(w.shape, F32)
    return pl.pallas_call(
        body, name=name, out_shape=[shape] * 4,
        grid_spec=pltpu.PrefetchScalarGridSpec(
            num_scalar_prefetch=1, grid=(n_t,),
            in_specs=[pl.BlockSpec((tr, cols), lambda i, me_ref: (me_ref[0] * n_t + i, 0)),
                      pl.BlockSpec((N_DEV, tr, cols), lambda i, me_ref: (0, i, 0)), spec, spec, spec],
            out_specs=[spec] * 4),
        compiler_params=_params(("arbitrary",)),
    )(me.reshape(1).astype(jnp.int32), own, parts, w, m, v)


def _ada_grad_adam(c_all, dmod_all, w, m, v, tr):
    rows, cols = w.shape

    def body(c_ref, dm_ref, w_ref, m_ref, v_ref, g_ref, d_ref, nm_ref, nv_ref):
        cv = c_ref[...]
        act = cv * jax.nn.sigmoid(cv)
        dmod = dm_ref[:, pl.ds(pl.multiple_of(_index(_place()) * cols, LANES), cols)]
        g = lax.dot_general(act, dmod, TN, preferred_element_type=F32, precision=lax.Precision.HIGHEST)
        g_ref[...] = g
        d_ref[...], nm_ref[...], nv_ref[...] = _adam_math(w_ref[...], g, m_ref[...], v_ref[...])

    spec = pl.BlockSpec((tr, cols), lambda i: (i, 0))
    shape = jax.ShapeDtypeStruct((rows, cols), F32)
    return pl.pallas_call(
        body, name="ada_grad_adam", grid=(rows // tr,),
        in_specs=[pl.BlockSpec((N_DEV, tr), lambda i: (0, i)), pl.BlockSpec(dmod_all.shape, lambda i: (0, 0)), spec, spec, spec],
        out_specs=[spec] * 4, out_shape=[shape] * 4, compiler_params=_params(("arbitrary",)),
    )(c_all, dmod_all, w, m, v)


def _rope_tables(positions):
    inv_freq = ROPE_THETA ** (-jnp.arange(0, 2 * ROT_HALF, 2, dtype=F32) / (2 * ROT_HALF))
    ang = inv_freq[:, None] * positions.astype(F32)[None, :]
    rows = jnp.concatenate([jnp.cos(ang), jnp.sin(ang), jnp.ones_like(ang)], axis=0)
    spread = [[[0.0] * LANES for _ in range(3 * ROT_HALF)] for _ in range(3)]
    for lane in range(LANES):
        p, j = lane % HEAD_DIM, lane % ROT_HALF
        if p < ROT_HALF:
            spread[0][j][lane] = 1.0
            spread[1][ROT_HALF + j][lane] = -1.0
        elif p < 2 * ROT_HALF:
            spread[0][j][lane] = 1.0
            spread[2][ROT_HALF + j][lane] = 1.0
        else:
            spread[0][2 * ROT_HALF][lane] = 1.0
    return rows, jnp.array(spread, F32)


def _pad_rows(a, rows):
    return jnp.pad(a, ((0, rows - a.shape[0]), (0, 0)))


def _sequence_step(xs, target, rope, mods, gains, w_in_t, w_out_t, relay_ffn, fetch_ffn, send_grads, w_blk_b, b_pool_r,
                   pool_scale_r, conv_w_all, conv_b, after):
    sh_m, sc_m, gt_m, sh_f, sc_f, gt_f = mods
    g_pre_mix, g_post_mix, g_pre_ffn, g_post_ffn = gains
    h1, u_pool, qkv = _premix_inproj(xs, sh_m, sc_m, g_pre_mix, w_in_t, rope, after, tm=512)
    o_g, lse_g = _attn_fwd(qkv)
    x1, y1, h2, cat, attn, lse_all = _mix_out(xs, u_pool, o_g, lse_g, w_blk_b, b_pool_r, pool_scale_r, w_out_t,
                                              gt_m, g_post_mix, g_pre_ffn, sc_f, sh_f, tm=512)
    relay_ffn(x1)
    w_up_t, w_down_f = fetch_ffn(x1)
    gate, a_ffn, act, vd, dy2, dout, sums_ffn, loss_loc = _ffn_fwd_loss(h2, x1, target, w_up_t, w_down_f, conv_w_all, conv_b,
                                                              gt_f, g_post_ffn, tm=256, ck=256)

    dgc, dval, dw_down, dconv = _ffn_bwd_act(dy2, gate, a_ffn, act, vd, w_down_f, tm=512, tf=1408, ck=256)
    token = send_grads("down", [dw_down], [])
    dup, dh2 = _ffn_bwd_up(dgc, dval, w_up_t, conv_w_all, token, tm=512)
    dw_up_t = _wgrad(dup, h2, "wgrad_up", tk=2048, tmm=1408)
    token = send_grads("up", [dw_up_t], [])
    dx1, dpool, dattn, delta, dw_out_t, sums_mix = _mix_bwd(dh2, dout, x1, y1, cat, attn, w_out_t, sc_f,
                                                           g_pre_ffn, gt_m, g_post_mix, token, tm=512)
    du, dw_blk, sums_pool = _pool_bwd(dpool, u_pool, w_blk_b, b_pool_r, pool_scale_r, tm=1024)
    token = send_grads("out", [dw_out_t], [sums_mix, sums_ffn, sums_pool, dw_blk, dconv, loss_loc])
    dproj, dw_in_t = _dproj_wgrad_in(du, _attn_bwd(qkv, dattn, lse_all, delta, token), rope, h1, tm=512, cm=512)
    token = send_grads("in", [dw_in_t], [])
    grad_x, sums_in = _inproj_bwd(dproj, w_in_t, xs, dx1, sc_m, g_pre_mix, token, tm=512)
    return (loss_loc, grad_x, dw_in_t, dw_out_t, dw_up_t, dw_down, dw_blk, dconv,
            sums_in, sums_mix, sums_ffn, sums_pool)


def kernel(x, c, positions, w_ada, b_ada, g_pre_mix, g_post_mix, g_pre_ffn, g_post_ffn, w_in, w_pool, b_pool, pool_scale, w_out, w_up, conv_w, conv_b, w_down, loss_target, m_w_ada, m_b_ada, m_g_pre_mix, m_g_post_mix, m_g_pre_ffn, m_g_post_ffn, m_w_in, m_w_pool, m_b_pool, m_pool_scale, m_w_out, m_w_up, m_conv_w, m_conv_b, m_w_down, v_w_ada, v_b_ada, v_g_pre_mix, v_g_post_mix, v_g_pre_ffn, v_g_post_ffn, v_w_in, v_w_pool, v_b_pool, v_pool_scale, v_w_out, v_w_up, v_conv_w, v_conv_b, v_w_down):
    s_len, d = x.shape[1], x.shape[2]
    d_ff = w_down.shape[1] * N_DEV
    me = _index(_place())
    xs, target = x[0], loss_target[0]

    c_all, mod, taps_all, (w_in_t, w_out_t), lands = _entry_exchange(
        jnp.broadcast_to(c, (8, d)), w_ada[0], b_ada, _pad_rows(conv_w[0], 8),
        [w_in[0].T.astype(BF16), w_out[0].T.astype(BF16)], [w_up[0].T, w_down[0]])
    c_all = c_all[:, 0, :]
    conv_w_all = jnp.transpose(taps_all[:, :3, :], (1, 0, 2)).reshape(3, d_ff)
    sh_m, sc_m, gt_m, sh_f, sc_f, gt_f = [mod[:, 0, :].reshape(1, -1)[:, k * d:(k + 1) * d] for k in range(6)]

    rope = _rope_tables(positions[0])
    w_blk = jnp.zeros((256, 256), F32)
    for gi in range(4):
        w_blk = lax.dynamic_update_slice(w_blk, w_pool[0, gi], (gi * HEAD_DIM, gi * HEAD_DIM))
    w_blk_b = w_blk.astype(BF16)
    b_pool_r, pool_scale_r = b_pool.reshape(1, 256), pool_scale.reshape(1, 256)

    w_send, w_recv, w_src, w_land, w_token = _exchange_start("gather_ici", [], lands, "ffn_weights_ici_start")
    relay = []

    def relay_ffn(after):
        _, blocks = _exchange_wait("gather_ici", w_send, w_recv, w_src, w_land, after, "ffn_weights_ici_wait")
        relay.extend(_exchange_start("gather_d2d", [], blocks, "ffn_weights_d2d_start"))

    def fetch_ffn(after):
        return _exchange_wait("gather_d2d", relay[0], relay[1], [], relay[3], after, "ffn_weights_d2d_wait")[1]

    flights = {}

    def send_grads(tag, slabs, whole):
        lands = [lax.empty((N_DEV, g.shape[0] // N_DEV, g.shape[1]), g.dtype) for g in slabs]
        lands += [lax.empty((N_DEV,) + a.shape, F32) for a in whole]
        modes = ("scatter",) * len(slabs) + ("allgather",) * len(whole)
        flights[tag] = (modes, *_exchange_start(modes, slabs + whole, lands, f"grads_{tag}_start"))
        return flights[tag][5]

    def arrived(tag, after):
        return _exchange_wait(*flights[tag][:5], after, f"grads_{tag}_wait")

    _, grad_x, *_, sums_in, _, _, _ = _sequence_step(
        xs, target, rope, (sh_m, sc_m, gt_m, sh_f, sc_f, gt_f), (g_pre_mix, g_post_mix, g_pre_ffn, g_post_ffn),
        w_in_t, w_out_t, relay_ffn, fetch_ffn, send_grads, w_blk_b, b_pool_r, pool_scale_r, conv_w_all, conv_b,
        w_token)

    send_grads("last", [], [sums_in])

    (own_down,), (parts_down,) = arrived("down", flights["last"][5])
    new_down = _sum_adam(own_down, parts_down, w_down[0], m_w_down[0], v_w_down[0], me, "adam_w_down", 176)
    (own_up,), (parts_up,) = arrived("up", new_down[0])
    new_up = _sum_adam(own_up, parts_up, w_up[0].T, m_w_up[0].T, v_w_up[0].T, me, "adam_w_up", 352)
    (own_out, *small), (parts_out, *gathered) = arrived("out", new_up[0])
    new_out = _sum_adam(own_out, parts_out, w_out[0], m_w_out[0], v_w_out[0], me, "adam_w_out", 128)
    (own_in,), (parts_in,) = arrived("in", new_out[0])
    new_in = _sum_adam(own_in, parts_in, w_in[0].T, m_w_in[0].T, v_w_in[0].T, me, "adam_w_in", 160)
    big = {"w_up": [a.T for a in new_up], "w_down": new_down, "w_out": new_out, "w_in": [a.T for a in new_in]}

    rep_w = [b_ada, g_pre_mix, g_post_mix, g_pre_ffn, g_post_ffn, w_pool, b_pool, pool_scale, conv_b]
    rep_m = [m_b_ada, m_g_pre_mix, m_g_post_mix, m_g_pre_ffn, m_g_post_ffn, m_w_pool, m_b_pool, m_pool_scale, m_conv_b]
    rep_v = [v_b_ada, v_g_pre_mix, v_g_post_mix, v_g_pre_ffn, v_g_post_ffn, v_w_pool, v_b_pool, v_pool_scale, v_conv_b]
    mine_last, got_last = arrived("last", new_in[0])
    small, gathered = [*mine_last, *small], [*got_last, *gathered]
    rep_out, dmod_all, dconv_tot, loss_tot = _small_sum_adam(small, gathered, rep_w, rep_m, rep_v)
    g_rep, d_rep, nm_rep, nv_rep = (rep_out[k::4] for k in range(4))

    fcol = d_ff // N_DEV
    taps = lambda a: jnp.transpose(a, (1, 0, 2))
    g_cw = lax.dynamic_slice(dconv_tot, (0, me * fcol), (3, fcol))[None]
    d_cw, nm_cw, nv_cw = [taps(a) for a in _adam(taps(conv_w), taps(g_cw), taps(m_conv_w), taps(v_conv_w), "adam_conv_w")]

    g_ada, d_ada, nm_ada, nv_ada = _ada_grad_adam(c_all, dmod_all, w_ada[0], m_w_ada[0], v_w_ada[0], 512)

    loss = loss_tot[0, 0]

    def group(k):
        rep = (g_rep, d_rep, nm_rep, nv_rep)[k]
        ada = (g_ada, d_ada, nm_ada, nv_ada)[k][None]
        cw = (g_cw, d_cw, nm_cw, nv_cw)[k]
        return [ada, rep[0], rep[1], rep[2], rep[3], rep[4], big["w_in"][k][None], rep[5], rep[6], rep[7],
                big["w_out"][k][None], big["w_up"][k][None], cw, rep[8], big["w_down"][k][None]]

    return (loss, grad_x[None], *group(0), *group(1), *group(2), *group(3))
```

```python
import functools
import math

import jax
import jax.numpy as jnp
from jax import lax
from jax.experimental import pallas as pl
from jax.experimental.pallas import tpu as pltpu

F32 = jnp.float32
BF16 = jnp.bfloat16
MESH = pl.DeviceIdType.MESH

N_DEV = 8
HEAD_DIM = 64
ROT_HALF = 8
ROPE_THETA = 500000.0
POOL_WINDOWS = (2, 4, 8, 16)
DILATIONS = (1, 4, 16)
BLOCK = 128
NORM_EPS = 1e-6
HALO = 16
MASKED = -1e30
ATTN_FWD_UNROLL = 8
ATTN_BWD_UNROLL = 8

ADAM_LR = 0.001
ADAM_B1 = 0.9
ADAM_B2 = 0.999
ADAM_EPS = 1e-08
ADAM_WD = 0.01
ADAM_STEP = 10

V7X_VMEM_LIMIT = 56 * 1024 * 1024
LANES = 128

NT = (((1,), (1,)), ((), ()))
NN = (((1,), (0,)), ((), ()))
TN = (((0,), (0,)), ((), ()))


def _dot(a, b, dims):
    return lax.dot_general(a, b, dims, preferred_element_type=F32)


def _params(sem=None, vmem=V7X_VMEM_LIMIT):
    if sem is None:
        return pltpu.CompilerParams(vmem_limit_bytes=vmem)
    return pltpu.CompilerParams(dimension_semantics=sem, vmem_limit_bytes=vmem)


def _rstd(v):
    return lax.rsqrt(jnp.mean(v * v, axis=-1, keepdims=True) + NORM_EPS)


def _norm_bwd(dn, n, rstd):
    return rstd * (dn - n * jnp.mean(dn * n, axis=-1, keepdims=True))


def _rope_lanes(cs_ref, spread_ref):
    return [lax.dot_general(cs_ref[...], spread_ref[k], TN, preferred_element_type=F32, precision=lax.Precision.HIGHEST)
            for k in range(3)]


def _rope_fwd(p, lanes):
    return p * lanes[0] + pltpu.roll(p, LANES - ROT_HALF, 1) * lanes[1] + pltpu.roll(p, ROT_HALF, 1) * lanes[2]


def _rope_bwd(dp, lanes):
    return dp * lanes[0] + pltpu.roll(dp * lanes[1], ROT_HALF, 1) + pltpu.roll(dp * lanes[2], LANES - ROT_HALF, 1)


def _gelu_parts(v):
    k2 = 2.0 * math.sqrt(2.0 / math.pi)
    c = 0.044715
    v2 = v * v
    s = jax.nn.sigmoid(v * (k2 + (k2 * c) * v2))
    g = v * s
    dg = s + g * (1.0 - s) * (k2 + (3.0 * k2 * c) * v2)
    return g, dg


def _halo_before(i, tile):
    return jnp.maximum(i * (tile // HALO) - 1, 0)


def _premix_inproj(x, sh, sc, g, w_in_t, rope, after, tm):
    s_len, d = x.shape
    n_proj = w_in_t.shape[0]
    n_slab = (n_proj - 256) // LANES

    def body(x_ref, sh_ref, sc_ref, g_ref, w_ref, cs_ref, spread_ref, after_ref, h_ref, up_ref, qkv_ref):
        xv = x_ref[...]
        h = (xv * _rstd(xv) * g_ref[...]) * (1.0 + sc_ref[...]) + sh_ref[...]
        hb = h.astype(BF16)
        h_ref[...] = hb
        up_ref[...] = _dot(hb, w_ref[0:256, :], NT)
        lanes = _rope_lanes(cs_ref, spread_ref)
        for pair in range(n_slab // 2):
            p = _dot(hb, w_ref[256 + 256 * pair:512 + 256 * pair, :], NT)
            for half in range(2):
                ph = p[:, half * LANES:(half + 1) * LANES]
                if pair < 6:
                    ph = _rope_fwd(ph, lanes)
                if pair < 3:
                    ph = ph * (HEAD_DIM ** -0.5)
                qkv_ref[2 * pair + half] = ph

    vec = pl.BlockSpec((1, d), lambda i: (0, 0))
    return pl.pallas_call(
        body, name="premix_inproj", grid=(s_len // tm,),
        in_specs=[pl.BlockSpec((tm, d), lambda i: (i, 0)), vec, vec, vec,
                  pl.BlockSpec((n_proj, d), lambda i: (0, 0)),
                  pl.BlockSpec((rope[0].shape[0], tm), lambda i: (0, i)), pl.BlockSpec(rope[1].shape, lambda i: (0, 0, 0)),
                  pl.BlockSpec(memory_space=pl.ANY)],
        out_specs=[pl.BlockSpec((tm, d), lambda i: (i, 0)),
                   pl.BlockSpec((tm, 256), lambda i: (i, 0)),
                   pl.BlockSpec((n_slab, tm, LANES), lambda i: (0, i, 0))],
        out_shape=[jax.ShapeDtypeStruct((s_len, d), BF16),
                   jax.ShapeDtypeStruct((s_len, 256), F32),
                   jax.ShapeDtypeStruct((n_slab, s_len, LANES), F32)],
        compiler_params=_params(("arbitrary",)),
    )(x, sh, sc, g, w_in_t, *rope, after)


def _block_rows(n, r, dil):
    start = n * (BLOCK * dil) + r
    if dil == 1:
        return pl.ds(pl.multiple_of(start, BLOCK), BLOCK)
    return pl.ds(start, BLOCK, stride=dil)


def _band_mask(n):
    ri = lax.broadcasted_iota(jnp.int32, (BLOCK, 2 * BLOCK), 0)
    cj = lax.broadcasted_iota(jnp.int32, (BLOCK, 2 * BLOCK), 1)
    cur = (cj >= BLOCK) & (cj - BLOCK <= ri)
    prev = (cj < BLOCK) & (cj >= ri) & (n > 0)
    return cur | prev


def _attn_fwd(qkv):
    s_len = qkv.shape[1]
    n_g = len(DILATIONS)

    def body(q_ref, k_ref, v_ref, o_ref, lse_ref):
        lane = lax.broadcasted_iota(jnp.int32, (BLOCK, LANES), 1)
        first = lane < HEAD_DIM

        def group(dil):
            nb = s_len // (BLOCK * dil)

            def block(t, carry):
                r, n = t // nb, t % nb
                cur = _block_rows(n, r, dil)
                prev = _block_rows(jnp.maximum(n - 1, 0), r, dil)
                q = q_ref[0, cur, :]
                kcat = jnp.concatenate([k_ref[0, prev, :], k_ref[0, cur, :]], axis=0).astype(BF16)
                vcat = jnp.concatenate([v_ref[0, prev, :], v_ref[0, cur, :]], axis=0).astype(BF16)
                valid = _band_mask(n)
                q2 = jnp.concatenate([jnp.where(first, q, 0.0), jnp.where(first, 0.0, q)], axis=0).astype(BF16)
                s = jnp.where(jnp.concatenate([valid, valid], axis=0), _dot(q2, kcat, NT), MASKED)
                m = jnp.max(s, axis=-1, keepdims=True)
                p = jnp.exp(s - m)
                den = jnp.sum(p, axis=-1, keepdims=True)
                o2 = _dot(p.astype(BF16), vcat, NN) / den
                lse2 = m + jnp.log(den)
                o_ref[0, 0, cur, :] = jnp.where(first, o2[:BLOCK], o2[BLOCK:])
                lse_ref[0, 0, cur, :] = jnp.where(first, lse2[:BLOCK], lse2[BLOCK:])
                return carry

            lax.fori_loop(0, nb * dil, block, 0, unroll=ATTN_FWD_UNROLL)

        for gi, dil in enumerate(DILATIONS):
            pl.when(pl.program_id(0) == gi)(functools.partial(group, dil))

    def slab(base):
        return pl.BlockSpec((1, s_len, LANES), lambda g, s: (base + 2 * g + s, 0, 0))

    out = pl.BlockSpec((1, 1, s_len, LANES), lambda g, s: (g, s, 0, 0))
    shape = jax.ShapeDtypeStruct((n_g, 2, s_len, LANES), F32)
    return pl.pallas_call(
        body, name="attn_fwd", grid=(n_g, 2),
        in_specs=[slab(0), slab(6), slab(12)], out_specs=[out, out], out_shape=[shape, shape],
        compiler_params=_params(("arbitrary", "arbitrary")),
    )(qkv, qkv, qkv)


def _pool_mixed(u, halo, i, tm):
    ue = jnp.concatenate([halo, u], axis=0)
    s2 = ue + pltpu.roll(ue, 1, 0)
    s4 = s2 + pltpu.roll(s2, 2, 0)
    s8 = s4 + pltpu.roll(s4, 4, 0)
    s16 = s8 + pltpu.roll(s8, 8, 0)
    grp = lax.broadcasted_iota(jnp.int32, (tm, 256), 1) // HEAD_DIM
    pick = lambda a, b, c, e: jnp.where(grp == 0, a, jnp.where(grp == 1, b, jnp.where(grp == 2, c, e)))
    win_sum = pick(s2[HALO:], s4[HALO:], s8[HALO:], s16[HALO:])
    pos = (i * tm + lax.broadcasted_iota(jnp.int32, (tm, 256), 0)).astype(F32)
    count = jnp.minimum(pos + 1.0, pick(*[float(w) for w in POOL_WINDOWS]))
    return win_sum / count - u, count


def _mix_out(x, u_pool, o_g, lse_g, w_blk, b_pool, pool_scale, w_out_t, gt_m, g_post_mix, g_pre_ffn, sc_f, sh_f, tm):
    s_len, d = x.shape

    def body(x_ref, u_ref, uh_ref, o_ref, l_ref, wb_ref, bp_ref, ps_ref, wo_ref,
             gt_ref, g1_ref, g2_ref, sc_ref, sh_ref,
             x1_ref, y1_ref, h2_ref, cat_ref, attn_ref, lall_ref):
        (o0, o1, o2), (l0, l1, l2) = (o_ref.at[g] for g in range(3)), (l_ref.at[g] for g in range(3))
        i = pl.program_id(0)
        u = u_ref[...]
        halo = uh_ref[...] * (i > 0).astype(F32)
        mixed, _ = _pool_mixed(u, halo, i, tm)
        y = _dot(mixed.astype(BF16), wb_ref[...], NN) + bp_ref[...]
        pool = y * ps_ref[...]
        attn = []
        for s in range(2):
            la, lb, lc = l0[s], l1[s], l2[s]
            mx = jnp.maximum(jnp.maximum(la, lb), lc)
            ea, eb, ec = jnp.exp(la - mx), jnp.exp(lb - mx), jnp.exp(lc - mx)
            den = ea + eb + ec
            lall_ref[s] = mx + jnp.log(den)
            attn.append((ea / den) * o0[s] + (eb / den) * o1[s] + (ec / den) * o2[s])
        attn = jnp.concatenate(attn, axis=1)
        attn_ref[...] = attn
        cat = jnp.concatenate([pool, attn], axis=1).astype(BF16)
        cat_ref[...] = cat
        y1 = _dot(cat, wo_ref[...], NT)
        y1_ref[...] = y1.astype(BF16)
        x1 = x_ref[...] + gt_ref[...] * (y1 * _rstd(y1) * g1_ref[...])
        x1_ref[...] = x1
        h2 = (x1 * _rstd(x1) * g2_ref[...]) * (1.0 + sc_ref[...]) + sh_ref[...]
        h2_ref[...] = h2.astype(BF16)

    tile = lambda w: pl.BlockSpec((tm, w), lambda i: (i, 0))
    slab = pl.BlockSpec((2, tm, LANES), lambda i: (0, i, 0))
    groups = pl.BlockSpec((len(DILATIONS), 2, tm, LANES), lambda i: (0, 0, i, 0))
    const = lambda a: pl.BlockSpec(a.shape, lambda i: (0,) * a.ndim)
    return pl.pallas_call(
        body, name="mix_out", grid=(s_len // tm,),
        in_specs=[tile(d), tile(256), pl.BlockSpec((HALO, 256), lambda i: (_halo_before(i, tm), 0)),
                  groups, groups,
                  const(w_blk), const(b_pool), const(pool_scale), const(w_out_t),
                  const(gt_m), const(g_post_mix), const(g_pre_ffn), const(sc_f), const(sh_f)],
        out_specs=[tile(d), tile(d), tile(d), tile(512), tile(256), slab],
        out_shape=[jax.ShapeDtypeStruct((s_len, d), F32), jax.ShapeDtypeStruct((s_len, d), BF16),
                   jax.ShapeDtypeStruct((s_len, d), BF16), jax.ShapeDtypeStruct((s_len, 512), BF16),
                   jax.ShapeDtypeStruct((s_len, 256), F32), jax.ShapeDtypeStruct((2, s_len, LANES), F32)],
        compiler_params=_params(("arbitrary",)),
    )(x, u_pool, u_pool, o_g, lse_g, w_blk, b_pool, pool_scale, w_out_t, gt_m, g_post_mix, g_pre_ffn, sc_f, sh_f)


def _conv_gate(gate_ext, cw, cb):
    gc = gate_ext * cw[2:3, :] + pltpu.roll(gate_ext, 1, 0) * cw[1:2, :] + pltpu.roll(gate_ext, 2, 0) * cw[0:1, :]
    return gc[HALO:] + cb


def _ffn_fwd_loss(h2, x1, target, w_up_t, w_down, conv_w, conv_b, gt_f, g_post_ffn, tm, ck):
    s_len, d = x1.shape
    d_ff = w_down.shape[0]
    n_t, n_c = s_len // tm, d_ff // ck

    def body(h_ref, hh_ref, x1_ref, tgt_ref, wg_ref, wv_ref, wd_ref, cw_ref, cb_ref, gt_ref, g_ref,
             gate_ref, a_ref, act_ref, vd_ref, dy2_ref, dout_ref, sums_ref, loss_ref, acc_ref):
        i = pl.program_id(0)

        @pl.when(i == 0)
        def _():
            sums_ref[...] = jnp.zeros_like(sums_ref)
            loss_ref[...] = jnp.zeros_like(loss_ref)
            acc_ref[...] = jnp.zeros_like(acc_ref)

        def finish(live):
            y2 = acc_ref[...]
            rstd = _rstd(y2)
            n = y2 * rstd
            rn = n * g_ref[...]
            err = x1_ref[...] + gt_ref[...] * rn - tgt_ref[...]
            keep = lambda v: jnp.where(live, v, 0.0)
            loss_ref[...] += keep(0.5 * jnp.sum(jnp.mean(err * err, axis=-1, keepdims=True), axis=0, keepdims=True))
            dout = err * (1.0 / d)
            dout_ref[...] = dout.astype(BF16)
            drn = dout * gt_ref[...]
            sums_ref[0:1, :] += keep(jnp.sum(dout * rn, axis=0, keepdims=True))
            sums_ref[1:2, :] += keep(jnp.sum(drn * n, axis=0, keepdims=True))
            dy2_ref[...] = _norm_bwd(drn * g_ref[...], n, rstd).astype(BF16)

        @pl.when(i < n_t)
        def _():
            h = h_ref[...]
            h_ext = jnp.concatenate([hh_ref[...], h], axis=0)
            row = lax.broadcasted_iota(jnp.int32, (tm + HALO, ck), 0)
            no_halo = (row < HALO) & (i == 0)

            def up(c):
                cs = slice(c * ck, (c + 1) * ck)
                return jnp.where(no_halo, 0.0, _dot(h_ext, wg_ref[cs, :], NT)), _dot(h, wv_ref[cs, :], NT)

            part = None
            nxt = up(0)
            finish(i > 0)
            for c in range(n_c):
                cs = slice(c * ck, (c + 1) * ck)
                gate_ext, val = nxt
                if c + 1 < n_c:
                    nxt = up(c + 1)
                act, dact = _gelu_parts(_conv_gate(gate_ext, cw_ref[:, cs], cb_ref[:, cs]))
                a = (act * val).astype(BF16)
                gate_ref[:, cs] = gate_ext[HALO:].astype(BF16)
                a_ref[:, cs] = a
                act_ref[:, cs] = act.astype(BF16)
                vd_ref[:, cs] = (val * dact).astype(BF16)
                p = _dot(a, wd_ref[cs, :], NN)
                part = p if part is None else part + p
            acc_ref[...] = part

        @pl.when(i == n_t)
        def _():
            finish(True)

    this = lambda i: jnp.minimum(i, n_t - 1)
    before = lambda i: jnp.maximum(i - 1, 0)
    tok = lambda w, at: pl.BlockSpec((tm, w), lambda i: (at(i), 0))
    vec = pl.BlockSpec((1, d), lambda i: (0, 0))
    once = lambda shape, imap: pl.BlockSpec(shape, imap, pipeline_mode=pl.Buffered(1))
    return pl.pallas_call(
        body, name="ffn_fwd_loss", grid=(n_t + 1,),
        in_specs=[tok(d, this), pl.BlockSpec((HALO, d), lambda i: (_halo_before(this(i), tm), 0)),
                  tok(d, before), tok(d, before),
                  once((d_ff, d), lambda i: (0, 0)), once((d_ff, d), lambda i: (1, 0)), once((d_ff, d), lambda i: (0, 0)),
                  pl.BlockSpec((3, d_ff), lambda i: (0, 0)), pl.BlockSpec((1, d_ff), lambda i: (0, 0)), vec, vec],
        out_specs=[tok(d_ff, this)] * 4 + [tok(d, before), tok(d, before), pl.BlockSpec((8, d), lambda i: (0, 0)),
                                          pl.BlockSpec((8, LANES), lambda i: (0, 0))],
        out_shape=[jax.ShapeDtypeStruct((s_len, d_ff), BF16)] * 4
        + [jax.ShapeDtypeStruct((s_len, d), BF16), jax.ShapeDtypeStruct((s_len, d), BF16),
           jax.ShapeDtypeStruct((8, d), F32), jax.ShapeDtypeStruct((8, LANES), F32)],
        scratch_shapes=[pltpu.VMEM((tm, d), F32)],
        compiler_params=_params(("arbitrary",)),
    )(h2, h2, x1, target, w_up_t, w_up_t, w_down, conv_w, conv_b, gt_f, g_post_ffn)


def _ffn_bwd_act(dy2, gate, a, act, vd, w_down, tm, tf, ck):
    s_len, d = dy2.shape
    d_ff = w_down.shape[0]
    n_t = s_len // tm
    chunks = [slice(lo, min(lo + ck, tf)) for lo in range(0, tf, ck)]

    def body(dy_ref, g_ref, gh_ref, a_ref, act_ref, vd_ref, wd_ref, dgc_ref, dval_ref, dwd_ref, dconv_ref, acc_ref):
        i = pl.program_id(1)

        @pl.when(i == 0)
        def _():
            acc_ref[...] = jnp.zeros_like(acc_ref)
            dconv_ref[...] = jnp.zeros_like(dconv_ref)

        dy = dy_ref[...]

        def down(cs):
            return _dot(dy, wd_ref[cs, :], NT)

        nxt = down(chunks[0])
        for c, cs in enumerate(chunks):
            width = cs.stop - cs.start
            da = nxt
            if c + 1 < len(chunks):
                nxt = down(chunks[c + 1])
            acc_ref[cs, :] += _dot(a_ref[:, cs], dy, TN)
            row = lax.broadcasted_iota(jnp.int32, (tm + HALO, width), 0)
            gate_ext = jnp.where((row < HALO) & (i == 0), 0.0,
                                 jnp.concatenate([gh_ref[:, cs], g_ref[:, cs]], axis=0).astype(F32))
            dgc = da * vd_ref[:, cs].astype(F32)
            dgc_ref[:, cs] = dgc.astype(BF16)
            dval_ref[:, cs] = (da * act_ref[:, cs].astype(F32)).astype(BF16)
            rows = [jnp.sum(dgc * pltpu.roll(gate_ext, 2 - k, 0)[HALO:], axis=0, keepdims=True) for k in range(2)]
            rows += [jnp.sum(dgc * gate_ext[HALO:], axis=0, keepdims=True), jnp.sum(dgc, axis=0, keepdims=True),
                     jnp.zeros((4, width), F32)]
            dconv_ref[:, cs] += jnp.concatenate(rows, axis=0)

        @pl.when(i == n_t - 1)
        def _():
            dwd_ref[...] = acc_ref[...].astype(BF16)

    tokf = pl.BlockSpec((tm, tf), lambda j, i: (i, j))
    return pl.pallas_call(
        body, name="ffn_bwd_act", grid=(d_ff // tf, n_t),
        in_specs=[pl.BlockSpec((tm, d), lambda j, i: (i, 0)), tokf,
                  pl.BlockSpec((HALO, tf), lambda j, i: (_halo_before(i, tm), j)), tokf, tokf, tokf,
                  pl.BlockSpec((tf, d), lambda j, i: (j, 0))],
        out_specs=[tokf, tokf, pl.BlockSpec((tf, d), lambda j, i: (j, 0)), pl.BlockSpec((8, tf), lambda j, i: (0, j))],
        out_shape=[jax.ShapeDtypeStruct((s_len, d_ff), BF16), jax.ShapeDtypeStruct((s_len, d_ff), BF16),
                   jax.ShapeDtypeStruct((d_ff, d), BF16), jax.ShapeDtypeStruct((8, d_ff), F32)],
        scratch_shapes=[pltpu.VMEM((tf, d), F32)],
        compiler_params=_params(("arbitrary", "arbitrary")),
    )(dy2, gate, gate, a, act, vd, w_down)


def _ffn_bwd_up(dgc, dval, w_up_t, conv_w, after, tm):
    s_len, d_ff = dgc.shape
    d = w_up_t.shape[1]
    n_t = s_len // tm

    def body(dg_ref, dgn_ref, dv_ref, cw_ref, w_ref, after_ref, dup_ref, dh_ref):
        i = pl.program_id(0)
        nxt = dgn_ref[...].astype(F32) * (i < n_t - 1).astype(F32)
        ext = jnp.concatenate([dg_ref[...].astype(F32), nxt], axis=0)
        rows = tm + HALO
        dgate = (ext * cw_ref[2:3, :] + pltpu.roll(ext, rows - 1, 0) * cw_ref[1:2, :]
                 + pltpu.roll(ext, rows - 2, 0) * cw_ref[0:1, :])[:tm]
        dup = jnp.concatenate([dgate.astype(BF16), dv_ref[...]], axis=1)
        dup_ref[...] = dup
        dh_ref[...] = _dot(dup, w_ref[...], NN).astype(BF16)

    tokf = pl.BlockSpec((tm, d_ff), lambda i: (i, 0))
    return pl.pallas_call(
        body, name="ffn_bwd_up", grid=(n_t,),
        in_specs=[tokf, pl.BlockSpec((HALO, d_ff), lambda i: (jnp.minimum((i + 1) * (tm // HALO), s_len // HALO - 1), 0)),
                  tokf, pl.BlockSpec((3, d_ff), lambda i: (0, 0)), pl.BlockSpec((2 * d_ff, d), lambda i: (0, 0)),
                  pl.BlockSpec(memory_space=pl.ANY)],
        out_specs=[pl.BlockSpec((tm, 2 * d_ff), lambda i: (i, 0)), pl.BlockSpec((tm, d), lambda i: (i, 0))],
        out_shape=[jax.ShapeDtypeStruct((s_len, 2 * d_ff), BF16), jax.ShapeDtypeStruct((s_len, d), BF16)],
        compiler_params=_params(("arbitrary",)),
    )(dgc, dgc, dval, conv_w, w_up_t, after)


def _mix_bwd(dh2, dout, x1, y1, cat, attn, w_out_t, sc_f, g_pre_ffn, gt_m, g_post_mix, after, tm):
    s_len, d = x1.shape
    n_t = s_len // tm

    def body(dh_ref, do_ref, x1_ref, y1_ref, cat_ref, at_ref, wo_ref, sc_ref, g2_ref, gt_ref, g1_ref, after_ref,
             dx1_ref, dpool_ref, dattn_ref, delta_ref, dwo_ref, sums_ref, acc_ref):
        i = pl.program_id(0)
        dh = dh_ref[...].astype(F32)
        x1 = x1_ref[...]
        r2 = _rstd(x1)
        n2 = x1 * r2
        ng = n2 * g2_ref[...]
        dng = dh * (1.0 + sc_ref[...])
        dx1 = do_ref[...].astype(F32) + _norm_bwd(dng * g2_ref[...], n2, r2)
        dx1_ref[...] = dx1.astype(BF16)
        y1 = y1_ref[...].astype(F32)
        r1 = _rstd(y1)
        n1 = y1 * r1
        drn = dx1 * gt_ref[...]
        dy1 = _norm_bwd(drn * g1_ref[...], n1, r1).astype(BF16)
        dcat = _dot(dy1, wo_ref[...], NN)
        dpool_ref[...] = dcat[:, 0:256]
        lane = lax.broadcasted_iota(jnp.int32, (tm, LANES), 1)
        first = lane < HEAD_DIM
        for s in range(2):
            da = dcat[:, 256 + s * LANES:256 + (s + 1) * LANES]
            dattn_ref[s] = da
            prod = da * at_ref[:, s * LANES:(s + 1) * LANES]
            tot = jnp.sum(prod, axis=-1, keepdims=True)
            lo = jnp.sum(jnp.where(first, prod, 0.0), axis=-1, keepdims=True)
            delta_ref[s] = jnp.where(first, lo, tot - lo)
        dwo = _dot(dy1, cat_ref[...], TN)
        sums = jnp.concatenate(
            [jnp.sum(dh, axis=0, keepdims=True), jnp.sum(dh * ng, axis=0, keepdims=True),
             jnp.sum(dng * n2, axis=0, keepdims=True), jnp.sum(dx1 * (n1 * g1_ref[...]), axis=0, keepdims=True),
             jnp.sum(drn * n1, axis=0, keepdims=True), jnp.zeros((3, d), F32)], axis=0)

        @pl.when(i == 0)
        def _():
            acc_ref[...] = dwo
            sums_ref[...] = sums

        @pl.when(i > 0)
        def _():
            acc_ref[...] += dwo
            sums_ref[...] += sums

        @pl.when(i == n_t - 1)
        def _():
            dwo_ref[...] = acc_ref[...].astype(BF16)

    tile = lambda w: pl.BlockSpec((tm, w), lambda i: (i, 0))
    slab = pl.BlockSpec((2, tm, LANES), lambda i: (0, i, 0))
    vec = pl.BlockSpec((1, d), lambda i: (0, 0))
    return pl.pallas_call(
        body, name="mix_bwd", grid=(n_t,),
        in_specs=[tile(d), tile(d), tile(d), tile(d), tile(512), tile(256),
                  pl.BlockSpec((d, 512), lambda i: (0, 0)), vec, vec, vec, vec, pl.BlockSpec(memory_space=pl.ANY)],
        out_specs=[tile(d), tile(256), slab, slab, pl.BlockSpec((d, 512), lambda i: (0, 0)),
                   pl.BlockSpec((8, d), lambda i: (0, 0))],
        out_shape=[jax.ShapeDtypeStruct((s_len, d), BF16), jax.ShapeDtypeStruct((s_len, 256), F32),
                   jax.ShapeDtypeStruct((2, s_len, LANES), F32), jax.ShapeDtypeStruct((2, s_len, LANES), F32),
                   jax.ShapeDtypeStruct((d, 512), BF16), jax.ShapeDtypeStruct((8, d), F32)],
        scratch_shapes=[pltpu.VMEM((d, 512), F32)],
        compiler_params=_params(("arbitrary",)),
    )(dh2, dout, x1, y1, cat, attn, w_out_t, sc_f, g_pre_ffn, gt_m, g_post_mix, after)


def _pool_bwd(dpool, u_pool, w_blk, b_pool, pool_scale, tm):
    s_len = dpool.shape[0]
    n_t = s_len // tm

    def body(dp_ref, dpn_ref, u_ref, uh_ref, wb_ref, bp_ref, ps_ref, du_ref, dwp_ref, sums_ref, acc_ref):
        i = pl.program_id(0)
        u = u_ref[...]
        mixed, _ = _pool_mixed(u, uh_ref[...] * (i > 0).astype(F32), i, tm)
        mixed_b = mixed.astype(BF16)
        y = _dot(mixed_b, wb_ref[...], NN) + bp_ref[...]
        dp = dp_ref[...]
        dy = dp * ps_ref[...]
        dwb = _dot(mixed_b, dy.astype(BF16), TN)
        sums = jnp.concatenate([jnp.sum(dy, axis=0, keepdims=True), jnp.sum(dp * y, axis=0, keepdims=True),
                                jnp.zeros((6, 256), F32)], axis=0)
        dp_ext = jnp.concatenate([dp, dpn_ref[...] * (i < n_t - 1).astype(F32)], axis=0)
        dmix = _dot((dp_ext * ps_ref[...]).astype(BF16), wb_ref[...], NT)
        rows = tm + HALO
        grp = lax.broadcasted_iota(jnp.int32, (rows, 256), 1) // HEAD_DIM
        pick = lambda a, b, c, e: jnp.where(grp == 0, a, jnp.where(grp == 1, b, jnp.where(grp == 2, c, e)))
        pos = (i * tm + lax.broadcasted_iota(jnp.int32, (rows, 256), 0)).astype(F32)
        z = dmix / jnp.minimum(pos + 1.0, pick(*[float(w) for w in POOL_WINDOWS]))
        f2 = z + pltpu.roll(z, rows - 1, 0)
        f4 = f2 + pltpu.roll(f2, rows - 2, 0)
        f8 = f4 + pltpu.roll(f4, rows - 4, 0)
        f16 = f8 + pltpu.roll(f8, rows - 8, 0)
        du_ref[...] = (pick(f2, f4, f8, f16) - dmix)[:tm]

        @pl.when(i == 0)
        def _():
            acc_ref[...] = dwb
            sums_ref[...] = sums

        @pl.when(i > 0)
        def _():
            acc_ref[...] += dwb
            sums_ref[...] += sums

        @pl.when(i == n_t - 1)
        def _():
            full = acc_ref[...]
            for gi in range(len(POOL_WINDOWS)):
                lo = gi * HEAD_DIM
                dwp_ref[gi] = full[lo:lo + HEAD_DIM, lo:lo + HEAD_DIM]

    n_g = len(POOL_WINDOWS)
    tile = pl.BlockSpec((tm, 256), lambda i: (i, 0))
    const = lambda a: pl.BlockSpec(a.shape, lambda i: (0,) * a.ndim)
    return pl.pallas_call(
        body, name="pool_bwd", grid=(n_t,),
        in_specs=[tile, pl.BlockSpec((HALO, 256), lambda i: (jnp.minimum((i + 1) * (tm // HALO), s_len // HALO - 1), 0)),
                  tile, pl.BlockSpec((HALO, 256), lambda i: (_halo_before(i, tm), 0)),
                  const(w_blk), const(b_pool), const(pool_scale)],
        out_specs=[tile, pl.BlockSpec((n_g, HEAD_DIM, HEAD_DIM), lambda i: (0, 0, 0)), pl.BlockSpec((8, 256), lambda i: (0, 0))],
        out_shape=[jax.ShapeDtypeStruct((s_len, 256), F32), jax.ShapeDtypeStruct((n_g, HEAD_DIM, HEAD_DIM), F32),
                   jax.ShapeDtypeStruct((8, 256), F32)],
        scratch_shapes=[pltpu.VMEM((256, 256), F32)],
        compiler_params=_params(("arbitrary",)),
    )(dpool, dpool, u_pool, u_pool, w_blk, b_pool, pool_scale)


def _attn_bwd(qkv, dattn, lse_all, delta, after):
    s_len = qkv.shape[1]
    n_g = len(DILATIONS)

    def body(q_ref, k_ref, v_ref, do_ref, l_ref, dl_ref, after_ref, dq_ref, dk_ref, dv_ref):
        lane = lax.broadcasted_iota(jnp.int32, (BLOCK, LANES), 1)
        first = lane < HEAD_DIM

        def group(dil):
            nb = s_len // (BLOCK * dil)

            def block(t, carry):
                dk_part, dv_part = carry
                r, n = t // nb, t % nb
                cur = _block_rows(n, r, dil)
                prev = _block_rows(jnp.maximum(n - 1, 0), r, dil)
                q = q_ref[0, cur, :]
                do = do_ref[0, cur, :]
                lse = l_ref[0, cur, :]
                dlt = dl_ref[0, cur, :]
                kcat = jnp.concatenate([k_ref[0, prev, :], k_ref[0, cur, :]], axis=0).astype(BF16)
                vcat = jnp.concatenate([v_ref[0, prev, :], v_ref[0, cur, :]], axis=0).astype(BF16)
                valid = _band_mask(n)
                stack = lambda a: jnp.concatenate([jnp.where(first, a, 0.0), jnp.where(first, 0.0, a)], axis=0)
                rows2 = lambda a: jnp.concatenate([a[:, 0:1], a[:, HEAD_DIM:HEAD_DIM + 1]], axis=0)
                q2, do2 = stack(q).astype(BF16), stack(do).astype(BF16)
                valid2 = jnp.concatenate([valid, valid], axis=0)
                p = jnp.where(valid2, jnp.exp(_dot(q2, kcat, NT) - rows2(lse)), 0.0)
                ds = (p * (_dot(do2, vcat, NT) - rows2(dlt))).astype(BF16)
                dq2 = _dot(ds, kcat, NN)
                dq_ref[0, 0, cur, :] = jnp.where(first, dq2[:BLOCK], dq2[BLOCK:])
                dkc = _dot(ds, q2, TN)
                dvc = _dot(p.astype(BF16), do2, TN)
                dk_ref[0, 0, prev, :] = dk_part + dkc[:BLOCK]
                dv_ref[0, 0, prev, :] = dv_part + dvc[:BLOCK]
                dk_ref[0, 0, cur, :] = dkc[BLOCK:]
                dv_ref[0, 0, cur, :] = dvc[BLOCK:]
                return dkc[BLOCK:], dvc[BLOCK:]

            def blocks(tt, carry):
                for u in range(ATTN_BWD_UNROLL):
                    carry = block(tt * ATTN_BWD_UNROLL + u, carry)
                return carry

            zero = jnp.zeros((BLOCK, LANES), F32)
            lax.fori_loop(0, nb * dil // ATTN_BWD_UNROLL, blocks, (zero, zero))

        for gi, dil in enumerate(DILATIONS):
            pl.when(pl.program_id(1) == gi)(functools.partial(group, dil))

    def slab(base):
        return pl.BlockSpec((1, s_len, LANES), lambda s, g: (base + 2 * g + s, 0, 0))

    one = pl.BlockSpec((1, s_len, LANES), lambda s, g: (s, 0, 0))
    out = pl.BlockSpec((1, 1, s_len, LANES), lambda s, g: (g, s, 0, 0))
    shape = jax.ShapeDtypeStruct((n_g, 2, s_len, LANES), F32)
    return pl.pallas_call(
        body, name="attn_bwd", grid=(2, n_g),
        in_specs=[slab(0), slab(6), slab(12), one, one, one, pl.BlockSpec(memory_space=pl.ANY)],
        out_specs=[out, out, out], out_shape=[shape, shape, shape],
        compiler_params=_params(("arbitrary", "arbitrary")),
    )(qkv, qkv, qkv, dattn, lse_all, delta, after)


def _dproj_wgrad_in(du, dqkv, rope, h1, tm, cm):
    s_len = du.shape[0]
    d = h1.shape[1]
    n_proj = 256 + 18 * LANES
    n_t = s_len // tm

    def body(du_ref, dq_ref, dk_ref, dv_ref, cs_ref, spread_ref, h_ref, dproj_ref, dw_ref, acc_ref):
        i = pl.program_id(0)

        @pl.when(i == 0)
        def _():
            acc_ref[...] = jnp.zeros_like(acc_ref)

        dproj_ref[:, 0:256] = du_ref[...].astype(BF16)
        lanes = _rope_lanes(cs_ref, spread_ref)
        col = 256
        for kind, dref in enumerate((dq_ref, dk_ref, dv_ref)):
            for grp in range(3):
                for s in range(2):
                    piece = dref[grp, s]
                    if kind < 2:
                        piece = _rope_bwd(piece, lanes)
                    if kind == 0:
                        piece = piece * (HEAD_DIM ** -0.5)
                    dproj_ref[:, col:col + LANES] = piece.astype(BF16)
                    col += LANES

        for c0 in range(0, n_proj, cm):
            acc_ref[c0:c0 + cm, :] += _dot(dproj_ref[:, c0:c0 + cm], h_ref[...], TN)

        @pl.when(i == n_t - 1)
        def _():
            dw_ref[...] = acc_ref[...].astype(BF16)

    groups = pl.BlockSpec((len(DILATIONS), 2, tm, LANES), lambda i: (0, 0, i, 0))
    return pl.pallas_call(
        body, name="dproj_wgrad_in", grid=(n_t,),
        in_specs=[pl.BlockSpec((tm, 256), lambda i: (i, 0))] + [groups] * 3
        + [pl.BlockSpec((rope[0].shape[0], tm), lambda i: (0, i)), pl.BlockSpec(rope[1].shape, lambda i: (0, 0, 0)),
           pl.BlockSpec((tm, d), lambda i: (i, 0))],
        out_specs=[pl.BlockSpec((tm, n_proj), lambda i: (i, 0)), pl.BlockSpec((n_proj, d), lambda i: (0, 0))],
        out_shape=[jax.ShapeDtypeStruct((s_len, n_proj), BF16), jax.ShapeDtypeStruct((n_proj, d), BF16)],
        scratch_shapes=[pltpu.VMEM((n_proj, d), F32)],
        compiler_params=_params(("arbitrary",)),
    )(du, *dqkv, *rope, h1)


def _inproj_bwd(dproj, w_in_t, x, dx1, sc_m, g_pre_mix, after, tm):
    s_len, d = x.shape
    n_proj = w_in_t.shape[0]
    n_t = s_len // tm

    def body(dproj_ref, w_ref, x_ref, dx1_ref, sc_ref, g_ref, after_ref, dx_ref, sums_ref):
        i = pl.program_id(0)
        halves = [slice(0, tm // 2), slice(tm // 2, tm)]
        dhs = [_dot(dproj_ref[rs, :], w_ref[...], NN) for rs in halves]
        sums = None
        for rs, dh in zip(halves, dhs):
            xv = x_ref[rs, :]
            r = _rstd(xv)
            n = xv * r
            dng = dh * (1.0 + sc_ref[...])
            dx_ref[rs, :] = dx1_ref[rs, :].astype(F32) + _norm_bwd(dng * g_ref[...], n, r)
            part = jnp.concatenate([jnp.sum(dh, axis=0, keepdims=True), jnp.sum(dh * (n * g_ref[...]), axis=0, keepdims=True),
                                    jnp.sum(dng * n, axis=0, keepdims=True), jnp.zeros((5, d), F32)], axis=0)
            sums = part if sums is None else sums + part

        @pl.when(i == 0)
        def _():
            sums_ref[...] = sums

        @pl.when(i > 0)
        def _():
            sums_ref[...] += sums

    tile = lambda w: pl.BlockSpec((tm, w), lambda i: (i, 0))
    vec = pl.BlockSpec((1, d), lambda i: (0, 0))
    return pl.pallas_call(
        body, name="inproj_bwd", grid=(n_t,),
        in_specs=[tile(n_proj), pl.BlockSpec((n_proj, d), lambda i: (0, 0)), tile(d), tile(d), vec, vec,
                  pl.BlockSpec(memory_space=pl.ANY)],
        out_specs=[tile(d), pl.BlockSpec((8, d), lambda i: (0, 0))],
        out_shape=[jax.ShapeDtypeStruct((s_len, d), F32), jax.ShapeDtypeStruct((8, d), F32)],
        compiler_params=_params(("arbitrary",)),
    )(dproj, w_in_t, x, dx1, sc_m, g_pre_mix, after)


def _wgrad(a, b, name, tk, tmm):
    s_len, m = a.shape
    n = b.shape[1]
    n_k = s_len // tk

    def body(a_ref, b_ref, o_ref, acc_ref):
        k = pl.program_id(1)
        part = _dot(a_ref[...], b_ref[...], TN)

        @pl.when(k == 0)
        def _():
            acc_ref[...] = part

        @pl.when(k > 0)
        def _():
            acc_ref[...] += part

        @pl.when(k == n_k - 1)
        def _():
            o_ref[...] = acc_ref[...].astype(BF16)

    return pl.pallas_call(
        body, name=name, grid=(m // tmm, n_k),
        in_specs=[pl.BlockSpec((tk, tmm), lambda j, k: (k, j)), pl.BlockSpec((tk, n), lambda j, k: (k, 0))],
        out_specs=pl.BlockSpec((tmm, n), lambda j, k: (j, 0)),
        out_shape=jax.ShapeDtypeStruct((m, n), BF16),
        scratch_shapes=[pltpu.VMEM((tmm, n), F32)],
        compiler_params=_params(("arbitrary", "arbitrary")),
    )(a, b)


def _place():
    return lax.axis_index("x"), lax.axis_index("y"), lax.axis_index("c")


def _peer(k):
    x, y, c = _place()
    bx, by, bc = (k >> 2) & 1, (k >> 1) & 1, k & 1
    return (x ^ bx if bx else x, y ^ by if by else y, c ^ bc if bc else c)


def _index(pos):
    return 4 * pos[0] + 2 * pos[1] + pos[2]


def _entry_exchange(c_rows, w_ada, b_ada, taps, shards, later):
    d = c_rows.shape[1]
    ncol = w_ada.shape[1]
    n_w, n_p = len(shards), len(later)

    def body(c_ref, w_ref, b_ref, t_ref, *rest):
        srcs, rest = rest[:n_w], rest[n_w:]
        later_refs, rest = rest[:n_p], rest[n_p:]
        (call_ref, mod_ref, tall_ref), rest = rest[:3], rest[3:]
        outs, rest = rest[:n_w], rest[n_w:]
        zones, rest = rest[:n_p], rest[n_p:]
        stage_ref, s_send, s_recv, w_send, w_recv, local_sems = rest[:6]
        wide, narrow, place_sems = rest[6:6 + n_p], rest[6 + n_p:6 + 2 * n_p], rest[6 + 2 * n_p]
        x, y, c = _place()
        here, sibling = (x, y, c), (x, y, 1 - c)
        chips = [(1 - x, y), (x, 1 - y), (1 - x, 1 - y)]
        me = _index(here)

        def small(kind, src, dst, k):
            return pltpu.make_async_remote_copy(src_ref=src, dst_ref=dst, send_sem=s_send.at[kind, k - 1],
                                                recv_sem=s_recv.at[kind, k - 1], device_id=_peer(k), device_id_type=MESH)

        gather = lambda k: small(0, c_ref, call_ref.at[me], k)
        scatter = lambda k: small(1, stage_ref.at[_index(_peer(k))], mod_ref.at[me], k)
        gather_taps = lambda k: small(2, t_ref, tall_ref.at[me], k)

        def rows(w, pos):
            r = shards[w].shape[0]
            return outs[w].at[pl.ds(pl.multiple_of(_index(pos) * r, 16), r), :]

        def block(k, w, pos, to, own=False):
            return pltpu.make_async_remote_copy(
                src_ref=srcs[w] if own else rows(w, pos), dst_ref=rows(w, pos),
                send_sem=w_send.at[k, w], recv_sem=w_recv.at[k, w], device_id=to, device_id_type=MESH)

        call_ref[me] = c_ref[...]
        tall_ref[me] = t_ref[...]
        for k in range(1, N_DEV):
            gather(k).start()
        for k in range(1, N_DEV):
            gather_taps(k).start()
        mine = [pltpu.make_async_copy(srcs[w], rows(w, here), local_sems.at[w]) for w in range(n_w)]
        for cp in mine:
            cp.start()
        first = [block(0, w, here, sibling, own=True) for w in range(n_w)]
        first += [block(1 + j, w, here, (*chip, c), own=True) for j, chip in enumerate(chips) for w in range(n_w)]
        for cp in first:
            cp.start()
        fetch = [pltpu.make_async_copy(later_refs[w], wide[w], place_sems.at[0, w]) for w in range(n_p)]
        for cp in fetch:
            cp.start()

        for k in range(1, N_DEV):
            gather(k).wait_recv()
        cv = jnp.concatenate([call_ref[b, 0:1, :] for b in range(N_DEV)], axis=0)
        act = cv * jax.nn.sigmoid(cv)
        mod = lax.dot_general(act, w_ref[...], NN, preferred_element_type=F32,
                              precision=lax.Precision.HIGHEST) + b_ref[:, pl.ds(pl.multiple_of(me * ncol, LANES), ncol)]
        for b in range(N_DEV):
            stage_ref[b] = jnp.broadcast_to(mod[b:b + 1, :], (8, ncol))
        mod_ref[me] = stage_ref[me]
        for k in range(1, N_DEV):
            scatter(k).start()

        placed = []
        for w in range(n_p):
            fetch[w].wait()
            narrow[w][...] = wide[w][...].astype(BF16)
            r = later[w].shape[0]
            placed.append(pltpu.make_async_copy(narrow[w], zones[w].at[pl.ds(pl.multiple_of(me * r, 16), r), :],
                                                place_sems.at[1, w]))
            placed[-1].start()

        passed = []
        for j, chip in enumerate(chips):
            for w in range(n_w):
                block(1 + j, w, (*chip, c), here).wait_recv()
                fwd = block(4 + j, w, (*chip, c), sibling)
                fwd.start()
                passed.append(fwd)
        for w in range(n_w):
            block(0, w, sibling, here).wait_recv()
        for j, chip in enumerate(chips):
            for w in range(n_w):
                block(4 + j, w, (*chip, 1 - c), here).wait_recv()
        for k in range(1, N_DEV):
            scatter(k).wait_recv()
            gather_taps(k).wait_recv()
        for cp in first + passed:
            cp.wait_send()
        for k in range(1, N_DEV):
            gather(k).wait_send()
            scatter(k).wait_send()
            gather_taps(k).wait_send()
        for cp in mine + placed:
            cp.wait()

    vmem, hbm = pl.BlockSpec(memory_space=pltpu.VMEM), pl.BlockSpec(memory_space=pltpu.HBM)
    out = pl.pallas_call(
        body, name="entry_exchange",
        in_specs=[vmem] * 4 + [hbm] * (n_w + n_p), out_specs=[vmem] * 3 + [hbm] * (n_w + n_p),
        out_shape=[jax.ShapeDtypeStruct((N_DEV, 8, d), F32), jax.ShapeDtypeStruct((N_DEV, 8, ncol), F32),
                   jax.ShapeDtypeStruct((N_DEV,) + taps.shape, F32)]
        + [jax.ShapeDtypeStruct((N_DEV * s.shape[0], s.shape[1]), s.dtype) for s in shards]
        + [jax.ShapeDtypeStruct((N_DEV * s.shape[0], s.shape[1]), BF16) for s in later],
        scratch_shapes=[pltpu.VMEM((N_DEV, 8, ncol), F32), pltpu.SemaphoreType.DMA((3, N_DEV - 1)),
                        pltpu.SemaphoreType.DMA((3, N_DEV - 1)), pltpu.SemaphoreType.DMA((N_DEV - 1, n_w)),
                        pltpu.SemaphoreType.DMA((N_DEV - 1, n_w)), pltpu.SemaphoreType.DMA((n_w,))]
        + [pltpu.VMEM(s.shape, F32) for s in later] + [pltpu.VMEM(s.shape, BF16) for s in later]
        + [pltpu.SemaphoreType.DMA((2, n_p))],
        compiler_params=_params(),
    )(c_rows, w_ada, b_ada, taps, *shards, *later)
    return out[0], out[1], out[2], out[3:3 + n_w], out[3 + n_w:]


def _peer_copies(mode, srcs, lands, send_sems, recv_sems):
    if mode in ("gather_ici", "gather_d2d"):
        x, y, c = _place()
        sibling = (x, y, 1 - c)
        chips = [(1 - x, y), (x, 1 - y), (1 - x, 1 - y)]
        n = len(lands)

        def rows(w, pos):
            r = lands[w].shape[0] // N_DEV
            return lands[w].at[pl.ds(pl.multiple_of(_index(pos) * r, 16), r), :]

        def copy(k, w, src, dst, to):
            return pltpu.make_async_remote_copy(src_ref=src, dst_ref=dst, send_sem=send_sems.at[k * n + w],
                                                recv_sem=recv_sems.at[k * n + w], device_id=to, device_id_type=MESH)

        if mode == "gather_ici":
            targets = [sibling] + [(*chip, c) for chip in chips]
            return [copy(k, w, rows(w, (x, y, c)), rows(w, (x, y, c)), to) for k, to in enumerate(targets) for w in range(n)]
        return [copy(j, w, rows(w, (*chip, c)), rows(w, (*chip, c)), sibling)
                for j, chip in enumerate(chips) for w in range(n)]
    me = _index(_place())
    modes = (mode,) * len(srcs) if isinstance(mode, str) else mode
    copies = []
    for k in range(1, N_DEV):
        peer = _peer(k)
        for w, (src, land) in enumerate(zip(srcs, lands)):
            if modes[w] == "gather":
                r = src.shape[0]
                dst = land.at[pl.ds(pl.multiple_of(me * r, 16), r), :]
            elif modes[w] == "allgather":
                dst = land.at[me]
            else:
                r = src.shape[0] // N_DEV
                src = src.at[pl.ds(pl.multiple_of(_index(peer) * r, 16), r), :]
                dst = land.at[me]
            copies.append(pltpu.make_async_remote_copy(
                src_ref=src, dst_ref=dst, send_sem=send_sems.at[(k - 1) * len(srcs) + w],
                recv_sem=recv_sems.at[(k - 1) * len(srcs) + w],
                device_id=peer, device_id_type=MESH))
    return copies


def _exchange_start(mode, srcs, lands, name):
    n_s, n_a = len(srcs), len(srcs) + len(lands)
    n_cp = _COPIES_PER_ARRAY.get(mode, N_DEV - 1) * len(lands)

    def body(*refs):
        for cp in _peer_copies(mode, refs[:n_s], refs[n_s:n_a], refs[n_a], refs[n_a + 1]):
            cp.start()

    hbm, sem = pl.BlockSpec(memory_space=pltpu.HBM), pl.BlockSpec(memory_space=pltpu.SEMAPHORE)
    arrays = list(srcs) + list(lands)
    out = pl.pallas_call(
        body, name=name,
        out_shape=(pltpu.SemaphoreType.DMA((n_cp,)), pltpu.SemaphoreType.DMA((n_cp,)),
                   *[pltpu.HBM(a.shape, a.dtype) for a in arrays]),
        in_specs=[hbm] * n_a, out_specs=(sem, sem, *[hbm] * n_a),
        input_output_aliases={i: 2 + i for i in range(n_a)},
        compiler_params=pltpu.CompilerParams(has_side_effects=pltpu.SideEffectType.DATAFLOW_SIDE_EFFECTING),
    )(*[pltpu.with_memory_space_constraint(a, pltpu.HBM) for a in arrays])
    return out[0], out[1], out[2:2 + n_s], out[2 + n_s:2 + n_a], out[2]


_COPIES_PER_ARRAY = {"gather_ici": 4, "gather_d2d": 3}


def _exchange_wait(mode, send_sems, recv_sems, srcs, lands, after, name):
    n_s, n_a = len(srcs), len(srcs) + len(lands)

    def body(*refs):
        copies = _peer_copies(mode, refs[:n_s], refs[n_s:n_a], refs[n_a], refs[n_a + 1])
        for cp in copies:
            cp.wait_send()
        for cp in copies:
            cp.wait_recv()

    hbm, sem = pl.BlockSpec(memory_space=pltpu.HBM), pl.BlockSpec(memory_space=pltpu.SEMAPHORE)
    arrays = list(srcs) + list(lands)
    out = pl.pallas_call(
        body, name=name, out_shape=tuple(pltpu.HBM(a.shape, a.dtype) for a in arrays),
        in_specs=[hbm] * n_a + [sem, sem, pl.BlockSpec(memory_space=pl.ANY)], out_specs=tuple([hbm] * n_a),
        input_output_aliases={i: i for i in range(n_a)},
        compiler_params=pltpu.CompilerParams(has_side_effects=pltpu.SideEffectType.DATAFLOW_SIDE_EFFECTING),
    )(*arrays, send_sems, recv_sems, after)
    return out[:n_s], out[n_s:]


SMALL_WEIGHTS = ("b_ada", "g_pre_mix", "g_post_mix", "g_pre_ffn", "g_post_ffn", "w_pool", "b_pool", "pool_scale", "conv_b")


MOD_ROWS = ((0, 0), (0, 1), (1, 3), (1, 0), (1, 1), (2, 0))


def _small_sum_adam(mine, gathered, weights, moms, vels):
    n_l, n_w = len(mine), len(weights)
    d = mine[0].shape[1]

    def body(*refs):
        loc, got = refs[:n_l], refs[n_l:2 * n_l]
        w_refs, m_refs, v_refs = (refs[2 * n_l + k * n_w:2 * n_l + (k + 1) * n_w] for k in range(3))
        outs = refs[2 * n_l + 3 * n_w:]
        dmod_ref, conv_ref, loss_ref = outs[4 * n_w:]
        me = _index(_place())
        part = lambda a, dev: jnp.where(dev == me, loc[a][...], got[a][dev])
        totals = []
        for a in range(n_l):
            tot = part(a, 0)
            for dev in range(1, N_DEV):
                tot = tot + part(a, dev)
            totals.append(tot)
        t_in, t_mix, t_ffn, t_pool, t_blk, t_conv, t_loss = totals
        conv_ref[...] = t_conv
        loss_ref[...] = t_loss
        for dev in range(N_DEV):
            for k, (a, r) in enumerate(MOD_ROWS):
                dmod_ref[dev:dev + 1, k * d:(k + 1) * d] = part(a, dev)[r:r + 1, :]

        def update(idx, g, at=()):
            sel = lambda ref: ref.at[at] if at else ref
            delta, nm, nv = _adam_math(sel(w_refs[idx])[...], g, sel(m_refs[idx])[...], sel(v_refs[idx])[...])
            for k, val in enumerate((g, delta, nm, nv)):
                sel(outs[4 * idx + k])[...] = val

        tots = (t_in, t_mix, t_ffn)
        update(0, jnp.concatenate([tots[a][r:r + 1] for a, r in MOD_ROWS], axis=1))
        update(1, t_in[2:3])
        update(2, t_mix[4:5])
        update(3, t_mix[2:3])
        update(4, t_ffn[1:2])
        for gi in range(len(POOL_WINDOWS)):
            update(5, t_blk[gi], at=(0, gi))
        update(6, jnp.concatenate([t_pool[0:1, gi * HEAD_DIM:(gi + 1) * HEAD_DIM] for gi in range(len(POOL_WINDOWS))], axis=0),
               at=(0,))
        update(7, t_pool[1:2])
        update(8, t_conv[3:4])

    vmem = pl.BlockSpec(memory_space=pltpu.VMEM)
    out = pl.pallas_call(
        body, name="small_sum_adam", in_specs=[vmem] * (2 * n_l + 3 * n_w), out_specs=[vmem] * (4 * n_w + 3),
        out_shape=[jax.ShapeDtypeStruct(w.shape, F32) for w in weights for _ in range(4)]
        + [jax.ShapeDtypeStruct((N_DEV, 6 * d), F32), jax.ShapeDtypeStruct(mine[5].shape, F32),
           jax.ShapeDtypeStruct(mine[6].shape, F32)],
        compiler_params=_params(),
    )(*mine, *gathered, *weights, *moms, *vels)
    return out[:4 * n_w], out[4 * n_w], out[4 * n_w + 1], out[4 * n_w + 2]


def _adam_math(w, g, m, v):
    m = ADAM_B1 * m + (1.0 - ADAM_B1) * g
    v = ADAM_B2 * v + (1.0 - ADAM_B2) * (g * g)
    m_hat = m / (1.0 - ADAM_B1 ** ADAM_STEP)
    v_hat = v / (1.0 - ADAM_B2 ** ADAM_STEP)
    delta = -ADAM_LR * (m_hat / (jnp.sqrt(v_hat) + ADAM_EPS) + ADAM_WD * w)
    return delta, m, v


def _adam(w, g, m, v, name):
    def body(w_ref, g_ref, m_ref, v_ref, d_ref, nm_ref, nv_ref):
        d_ref[...], nm_ref[...], nv_ref[...] = _adam_math(w_ref[...], g_ref[...], m_ref[...], v_ref[...])

    vmem = pl.BlockSpec(memory_space=pltpu.VMEM)
    return pl.pallas_call(
        body, name=name, in_specs=[vmem] * 4, out_specs=[vmem] * 3,
        out_shape=[jax.ShapeDtypeStruct(w.shape, F32)] * 3, compiler_params=_params(),
    )(w, g, m, v)


def _sum_adam(own, parts, w, m, v, me, name, tr):
    _, rows, cols = parts.shape
    turned = w.shape == (cols, rows) and rows != cols
    assert tr == rows or not turned
    n_t = rows // tr

    def body(me_ref, own_ref, p_hbm, w_ref, m_ref, v_ref, g_ref, d_ref, nm_ref, nv_ref, p_buf, sems):
        i = pl.program_id(0)
        tile = lambda t: pl.ds(pl.multiple_of(t * tr, 16), tr)
        fetch = lambda dev, t: pltpu.make_async_copy(p_hbm.at[dev, tile(t), :], p_buf.at[dev, tile(t), :], sems.at[dev, t])

        @pl.when(i == 0)
        def _():
            for t in range(n_t):
                for dev in range(N_DEV):
                    fetch(dev, t).start()

        for dev in range(N_DEV):
            fetch(dev, i).wait()
        part = lambda dev: jnp.where(dev == me_ref[0], own_ref[...], p_buf[dev, tile(i), :]).astype(F32)
        g = part(0)
        for dev in range(1, N_DEV):
            g = g + part(dev)
        g = g.T if turned else g
        g_ref[...] = g
        d_ref[...], nm_ref[...], nv_ref[...] = _adam_math(w_ref[...], g, m_ref[...], v_ref[...])

    spec = pl.BlockSpec((cols, rows) if turned else (tr, cols), lambda i, me_ref: (i, 0))
    shape = jax.ShapeDtypeStruct(w.shape, F32)
    return pl.pallas_call(
        body, name=name, out_shape=[shape] * 4,
        grid_spec=pltpu.PrefetchScalarGridSpec(
            num_scalar_prefetch=1, grid=(n_t,),
            in_specs=[pl.BlockSpec((tr, cols), lambda i, me_ref: (me_ref[0] * n_t + i, 0)),
                      pl.BlockSpec(memory_space=pl.ANY), spec, spec, spec],
            out_specs=[spec] * 4,
            scratch_shapes=[pltpu.VMEM(parts.shape, parts.dtype), pltpu.SemaphoreType.DMA((N_DEV, n_t))]),
        compiler_params=_params(("arbitrary",)),
    )(me.reshape(1).astype(jnp.int32), own, parts, w, m, v)


def _ada_grad_adam(c_all, dmod_all, w, m, v, tr):
    rows, cols = w.shape

    def body(c_ref, dm_ref, w_ref, m_ref, v_ref, g_ref, d_ref, nm_ref, nv_ref):
        cv = c_ref[...]
        act = cv * jax.nn.sigmoid(cv)
        dmod = dm_ref[:, pl.ds(pl.multiple_of(_index(_place()) * cols, LANES), cols)]
        g = lax.dot_general(act, dmod, TN, preferred_element_type=F32, precision=lax.Precision.HIGHEST)
        g_ref[...] = g
        d_ref[...], nm_ref[...], nv_ref[...] = _adam_math(w_ref[...], g, m_ref[...], v_ref[...])

    spec = pl.BlockSpec((tr, cols), lambda i: (i, 0))
    shape = jax.ShapeDtypeStruct((rows, cols), F32)
    return pl.pallas_call(
        body, name="ada_grad_adam", grid=(rows // tr,),
        in_specs=[pl.BlockSpec((N_DEV, tr), lambda i: (0, i)), pl.BlockSpec(dmod_all.shape, lambda i: (0, 0)), spec, spec, spec],
        out_specs=[spec] * 4, out_shape=[shape] * 4, compiler_params=_params(("arbitrary",)),
    )(c_all, dmod_all, w, m, v)


def _rope_tables(positions):
    inv_freq = ROPE_THETA ** (-jnp.arange(0, 2 * ROT_HALF, 2, dtype=F32) / (2 * ROT_HALF))
    ang = inv_freq[:, None] * positions.astype(F32)[None, :]
    rows = jnp.concatenate([jnp.cos(ang), jnp.sin(ang), jnp.ones_like(ang)], axis=0)
    spread = [[[0.0] * LANES for _ in range(3 * ROT_HALF)] for _ in range(3)]
    for lane in range(LANES):
        p, j = lane % HEAD_DIM, lane % ROT_HALF
        if p < ROT_HALF:
            spread[0][j][lane] = 1.0
            spread[1][ROT_HALF + j][lane] = -1.0
        elif p < 2 * ROT_HALF:
            spread[0][j][lane] = 1.0
            spread[2][ROT_HALF + j][lane] = 1.0
        else:
            spread[0][2 * ROT_HALF][lane] = 1.0
    return rows, jnp.array(spread, F32)


def _pad_rows(a, rows):
    return jnp.pad(a, ((0, rows - a.shape[0]), (0, 0)))


def _sequence_step(xs, target, rope, mods, gains, w_in_t, w_out_t, relay_ffn, fetch_ffn, send_grads, w_blk_b, b_pool_r,
                   pool_scale_r, conv_w_all, conv_b, after):
    sh_m, sc_m, gt_m, sh_f, sc_f, gt_f = mods
    g_pre_mix, g_post_mix, g_pre_ffn, g_post_ffn = gains
    h1, u_pool, qkv = _premix_inproj(xs, sh_m, sc_m, g_pre_mix, w_in_t, rope, after, tm=512)
    o_g, lse_g = _attn_fwd(qkv)
    x1, y1, h2, cat, attn, lse_all = _mix_out(xs, u_pool, o_g, lse_g, w_blk_b, b_pool_r, pool_scale_r, w_out_t,
                                              gt_m, g_post_mix, g_pre_ffn, sc_f, sh_f, tm=512)
    relay_ffn(x1)
    w_up_t, w_down_f = fetch_ffn(x1)
    gate, a_ffn, act, vd, dy2, dout, sums_ffn, loss_loc = _ffn_fwd_loss(h2, x1, target, w_up_t, w_down_f, conv_w_all, conv_b,
                                                              gt_f, g_post_ffn, tm=256, ck=256)

    dgc, dval, dw_down, dconv = _ffn_bwd_act(dy2, gate, a_ffn, act, vd, w_down_f, tm=512, tf=1408, ck=256)
    token = send_grads("down", [dw_down], [])
    dup, dh2 = _ffn_bwd_up(dgc, dval, w_up_t, conv_w_all, token, tm=512)
    dw_up_t = _wgrad(dup, h2, "wgrad_up", tk=2048, tmm=1408)
    token = send_grads("up", [dw_up_t], [])
    dx1, dpool, dattn, delta, dw_out_t, sums_mix = _mix_bwd(dh2, dout, x1, y1, cat, attn, w_out_t, sc_f,
                                                           g_pre_ffn, gt_m, g_post_mix, token, tm=512)
    du, dw_blk, sums_pool = _pool_bwd(dpool, u_pool, w_blk_b, b_pool_r, pool_scale_r, tm=512)
    token = send_grads("out", [dw_out_t], [sums_mix, sums_ffn, sums_pool, dw_blk, dconv, loss_loc])
    dproj, dw_in_t = _dproj_wgrad_in(du, _attn_bwd(qkv, dattn, lse_all, delta, token), rope, h1, tm=512, cm=512)
    token = send_grads("in", [dw_in_t], [])
    grad_x, sums_in = _inproj_bwd(dproj, w_in_t, xs, dx1, sc_m, g_pre_mix, token, tm=512)
    return (loss_loc, grad_x, dw_in_t, dw_out_t, dw_up_t, dw_down, dw_blk, dconv,
            sums_in, sums_mix, sums_ffn, sums_pool)


def kernel(x, c, positions, w_ada, b_ada, g_pre_mix, g_post_mix, g_pre_ffn, g_post_ffn, w_in, w_pool, b_pool, pool_scale, w_out, w_up, conv_w, conv_b, w_down, loss_target, m_w_ada, m_b_ada, m_g_pre_mix, m_g_post_mix, m_g_pre_ffn, m_g_post_ffn, m_w_in, m_w_pool, m_b_pool, m_pool_scale, m_w_out, m_w_up, m_conv_w, m_conv_b, m_w_down, v_w_ada, v_b_ada, v_g_pre_mix, v_g_post_mix, v_g_pre_ffn, v_g_post_ffn, v_w_in, v_w_pool, v_b_pool, v_pool_scale, v_w_out, v_w_up, v_conv_w, v_conv_b, v_w_down):
    s_len, d = x.shape[1], x.shape[2]
    d_ff = w_down.shape[1] * N_DEV
    me = _index(_place())
    xs, target = x[0], loss_target[0]

    c_all, mod, taps_all, (w_in_t, w_out_t), lands = _entry_exchange(
        jnp.broadcast_to(c, (8, d)), w_ada[0], b_ada, _pad_rows(conv_w[0], 8),
        [w_in[0].T.astype(BF16), w_out[0].T.astype(BF16)], [w_up[0].T, w_down[0]])
    c_all = c_all[:, 0, :]
    conv_w_all = jnp.transpose(taps_all[:, :3, :], (1, 0, 2)).reshape(3, d_ff)
    sh_m, sc_m, gt_m, sh_f, sc_f, gt_f = [mod[:, 0, :].reshape(1, -1)[:, k * d:(k + 1) * d] for k in range(6)]

    rope = _rope_tables(positions[0])
    w_blk = jnp.zeros((256, 256), F32)
    for gi in range(4):
        w_blk = lax.dynamic_update_slice(w_blk, w_pool[0, gi], (gi * HEAD_DIM, gi * HEAD_DIM))
    w_blk_b = w_blk.astype(BF16)
    b_pool_r, pool_scale_r = b_pool.reshape(1, 256), pool_scale.reshape(1, 256)

    w_send, w_recv, w_src, w_land, w_token = _exchange_start("gather_ici", [], lands, "ffn_weights_ici_start")
    relay = []

    def relay_ffn(after):
        _, blocks = _exchange_wait("gather_ici", w_send, w_recv, w_src, w_land, after, "ffn_weights_ici_wait")
        relay.extend(_exchange_start("gather_d2d", [], blocks, "ffn_weights_d2d_start"))

    def fetch_ffn(after):
        return _exchange_wait("gather_d2d", relay[0], relay[1], [], relay[3], after, "ffn_weights_d2d_wait")[1]

    flights = {}

    def send_grads(tag, slabs, whole):
        lands = [lax.empty((N_DEV, g.shape[0] // N_DEV, g.shape[1]), g.dtype) for g in slabs]
        lands += [lax.empty((N_DEV,) + a.shape, F32) for a in whole]
        modes = ("scatter",) * len(slabs) + ("allgather",) * len(whole)
        flights[tag] = (modes, *_exchange_start(modes, slabs + whole, lands, f"grads_{tag}_start"))
        return flights[tag][5]

    def arrived(tag, after):
        return _exchange_wait(*flights[tag][:5], after, f"grads_{tag}_wait")

    _, grad_x, *_, sums_in, _, _, _ = _sequence_step(
        xs, target, rope, (sh_m, sc_m, gt_m, sh_f, sc_f, gt_f), (g_pre_mix, g_post_mix, g_pre_ffn, g_post_ffn),
        w_in_t, w_out_t, relay_ffn, fetch_ffn, send_grads, w_blk_b, b_pool_r, pool_scale_r, conv_w_all, conv_b,
        w_token)

    send_grads("last", [], [sums_in])

    (own_down,), (parts_down,) = arrived("down", flights["last"][5])
    new_down = _sum_adam(own_down, parts_down, w_down[0], m_w_down[0], v_w_down[0], me, "adam_w_down", 176)
    (own_up,), (parts_up,) = arrived("up", new_down[0])
    new_up = _sum_adam(own_up, parts_up, w_up[0].T, m_w_up[0].T, v_w_up[0].T, me, "adam_w_up", 352)
    (own_out, *small), (parts_out, *gathered) = arrived("out", new_up[0])
    new_out = _sum_adam(own_out, parts_out, w_out[0], m_w_out[0], v_w_out[0], me, "adam_w_out", 128)
    (own_in,), (parts_in,) = arrived("in", new_out[0])
    new_in = _sum_adam(own_in, parts_in, w_in[0].T, m_w_in[0].T, v_w_in[0].T, me, "adam_w_in", 160)
    big = {"w_up": [a.T for a in new_up], "w_down": new_down, "w_out": new_out, "w_in": [a.T for a in new_in]}

    rep_w = [b_ada, g_pre_mix, g_post_mix, g_pre_ffn, g_post_ffn, w_pool, b_pool, pool_scale, conv_b]
    rep_m = [m_b_ada, m_g_pre_mix, m_g_post_mix, m_g_pre_ffn, m_g_post_ffn, m_w_pool, m_b_pool, m_pool_scale, m_conv_b]
    rep_v = [v_b_ada, v_g_pre_mix, v_g_post_mix, v_g_pre_ffn, v_g_post_ffn, v_w_pool, v_b_pool, v_pool_scale, v_conv_b]
    mine_last, got_last = arrived("last", new_in[0])
    small, gathered = [*mine_last, *small], [*got_last, *gathered]
    rep_out, dmod_all, dconv_tot, loss_tot = _small_sum_adam(small, gathered, rep_w, rep_m, rep_v)
    g_rep, d_rep, nm_rep, nv_rep = (rep_out[k::4] for k in range(4))

    fcol = d_ff // N_DEV
    taps = lambda a: jnp.transpose(a, (1, 0, 2))
    g_cw = lax.dynamic_slice(dconv_tot, (0, me * fcol), (3, fcol))[None]
    d_cw, nm_cw, nv_cw = [taps(a) for a in _adam(taps(conv_w), taps(g_cw), taps(m_conv_w), taps(v_conv_w), "adam_conv_w")]

    g_ada, d_ada, nm_ada, nv_ada = _ada_grad_adam(c_all, dmod_all, w_ada[0], m_w_ada[0], v_w_ada[0], 256)

    loss = loss_tot[0, 0]

    def group(k):
        rep = (g_rep, d_rep, nm_rep, nv_rep)[k]
        ada = (g_ada, d_ada, nm_ada, nv_ada)[k][None]
        cw = (g_cw, d_cw, nm_cw, nv_cw)[k]
        return [ada, rep[0], rep[1], rep[2], rep[3], rep[4], big["w_in"][k][None], rep[5], rep[6], rep[7],
                big["w_out"][k][None], big["w_up"][k][None], cw, rep[8], big["w_down"][k][None]]

    return (loss, grad_x[None], *group(0), *group(1), *group(2), *group(3))
```

```python
import functools
import math

import jax
import jax.numpy as jnp
from jax import lax
from jax.experimental import pallas as pl
from jax.experimental.pallas import tpu as pltpu

F32 = jnp.float32
BF16 = jnp.bfloat16
MESH = pl.DeviceIdType.MESH

N_DEV = 8
HEAD_DIM = 64
ROT_HALF = 8
ROPE_THETA = 500000.0
POOL_WINDOWS = (2, 4, 8, 16)
DILATIONS = (1, 4, 16)
BLOCK = 128
NORM_EPS = 1e-6
HALO = 16
MASKED = -1e30
ATTN_FWD_UNROLL = 8
ATTN_BWD_UNROLL = 8

ADAM_LR = 0.001
ADAM_B1 = 0.9
ADAM_B2 = 0.999
ADAM_EPS = 1e-08
ADAM_WD = 0.01
ADAM_STEP = 10

V7X_VMEM_LIMIT = 56 * 1024 * 1024
LANES = 128

NT = (((1,), (1,)), ((), ()))
NN = (((1,), (0,)), ((), ()))
TN = (((0,), (0,)), ((), ()))


def _dot(a, b, dims):
    return lax.dot_general(a, b, dims, preferred_element_type=F32)


def _params(sem=None, vmem=V7X_VMEM_LIMIT):
    if sem is None:
        return pltpu.CompilerParams(vmem_limit_bytes=vmem)
    return pltpu.CompilerParams(dimension_semantics=sem, vmem_limit_bytes=vmem)


def _rstd(v):
    return lax.rsqrt(jnp.mean(v * v, axis=-1, keepdims=True) + NORM_EPS)


def _norm_bwd(dn, n, rstd):
    return rstd * (dn - n * jnp.mean(dn * n, axis=-1, keepdims=True))


def _rope_lanes(cs_ref, spread_ref):
    return [lax.dot_general(cs_ref[...], spread_ref[k], TN, preferred_element_type=F32, precision=lax.Precision.HIGHEST)
            for k in range(3)]


def _rope_fwd(p, lanes):
    return p * lanes[0] + pltpu.roll(p, LANES - ROT_HALF, 1) * lanes[1] + pltpu.roll(p, ROT_HALF, 1) * lanes[2]


def _rope_bwd(dp, lanes):
    return dp * lanes[0] + pltpu.roll(dp * lanes[1], ROT_HALF, 1) + pltpu.roll(dp * lanes[2], LANES - ROT_HALF, 1)


def _gelu_parts(v):
    k2 = 2.0 * math.sqrt(2.0 / math.pi)
    c = 0.044715
    v2 = v * v
    s = jax.nn.sigmoid(v * (k2 + (k2 * c) * v2))
    g = v * s
    dg = s + g * (1.0 - s) * (k2 + (3.0 * k2 * c) * v2)
    return g, dg


def _halo_before(i, tile):
    return jnp.maximum(i * (tile // HALO) - 1, 0)


def _premix_inproj(x, sh, sc, g, w_in_t, rope, after, tm):
    s_len, d = x.shape
    n_proj = w_in_t.shape[0]
    n_slab = (n_proj - 256) // LANES

    def body(x_ref, sh_ref, sc_ref, g_ref, w_ref, cs_ref, spread_ref, after_ref, h_ref, up_ref, qkv_ref):
        xv = x_ref[...]
        h = (xv * _rstd(xv) * g_ref[...]) * (1.0 + sc_ref[...]) + sh_ref[...]
        hb = h.astype(BF16)
        h_ref[...] = hb
        up_ref[...] = _dot(hb, w_ref[0:256, :], NT)
        lanes = _rope_lanes(cs_ref, spread_ref)
        for pair in range(n_slab // 2):
            p = _dot(hb, w_ref[256 + 256 * pair:512 + 256 * pair, :], NT)
            for half in range(2):
                ph = p[:, half * LANES:(half + 1) * LANES]
                if pair < 6:
                    ph = _rope_fwd(ph, lanes)
                if pair < 3:
                    ph = ph * (HEAD_DIM ** -0.5)
                qkv_ref[2 * pair + half] = ph

    vec = pl.BlockSpec((1, d), lambda i: (0, 0))
    return pl.pallas_call(
        body, name="premix_inproj", grid=(s_len // tm,),
        in_specs=[pl.BlockSpec((tm, d), lambda i: (i, 0)), vec, vec, vec,
                  pl.BlockSpec((n_proj, d), lambda i: (0, 0)),
                  pl.BlockSpec((rope[0].shape[0], tm), lambda i: (0, i)), pl.BlockSpec(rope[1].shape, lambda i: (0, 0, 0)),
                  pl.BlockSpec(memory_space=pl.ANY)],
        out_specs=[pl.BlockSpec((tm, d), lambda i: (i, 0)),
                   pl.BlockSpec((tm, 256), lambda i: (i, 0)),
                   pl.BlockSpec((n_slab, tm, LANES), lambda i: (0, i, 0))],
        out_shape=[jax.ShapeDtypeStruct((s_len, d), BF16),
                   jax.ShapeDtypeStruct((s_len, 256), F32),
                   jax.ShapeDtypeStruct((n_slab, s_len, LANES), F32)],
        compiler_params=_params(("arbitrary",)),
    )(x, sh, sc, g, w_in_t, *rope, after)


def _block_rows(n, r, dil):
    start = n * (BLOCK * dil) + r
    if dil == 1:
        return pl.ds(pl.multiple_of(start, BLOCK), BLOCK)
    return pl.ds(start, BLOCK, stride=dil)


def _band_mask(n):
    ri = lax.broadcasted_iota(jnp.int32, (BLOCK, 2 * BLOCK), 0)
    cj = lax.broadcasted_iota(jnp.int32, (BLOCK, 2 * BLOCK), 1)
    cur = (cj >= BLOCK) & (cj - BLOCK <= ri)
    prev = (cj < BLOCK) & (cj >= ri) & (n > 0)
    return cur | prev


def _attn_fwd(qkv):
    s_len = qkv.shape[1]
    n_g = len(DILATIONS)

    def body(q_ref, k_ref, v_ref, o_ref, lse_ref):
        lane = lax.broadcasted_iota(jnp.int32, (BLOCK, LANES), 1)
        first = lane < HEAD_DIM

        def group(dil):
            nb = s_len // (BLOCK * dil)

            def block(t, carry):
                r, n = t // nb, t % nb
                cur = _block_rows(n, r, dil)
                prev = _block_rows(jnp.maximum(n - 1, 0), r, dil)
                q = q_ref[0, cur, :]
                kcat = jnp.concatenate([k_ref[0, prev, :], k_ref[0, cur, :]], axis=0).astype(BF16)
                vcat = jnp.concatenate([v_ref[0, prev, :], v_ref[0, cur, :]], axis=0).astype(BF16)
                valid = _band_mask(n)
                q2 = jnp.concatenate([jnp.where(first, q, 0.0), jnp.where(first, 0.0, q)], axis=0).astype(BF16)
                s = jnp.where(jnp.concatenate([valid, valid], axis=0), _dot(q2, kcat, NT), MASKED)
                m = jnp.max(s, axis=-1, keepdims=True)
                p = jnp.exp(s - m)
                den = jnp.sum(p, axis=-1, keepdims=True)
                o2 = _dot(p.astype(BF16), vcat, NN) / den
                lse2 = m + jnp.log(den)
                o_ref[0, 0, cur, :] = jnp.where(first, o2[:BLOCK], o2[BLOCK:])
                lse_ref[0, 0, cur, :] = jnp.where(first, lse2[:BLOCK], lse2[BLOCK:])
                return carry

            lax.fori_loop(0, nb * dil, block, 0, unroll=ATTN_FWD_UNROLL)

        for gi, dil in enumerate(DILATIONS):
            pl.when(pl.program_id(0) == gi)(functools.partial(group, dil))

    def slab(base):
        return pl.BlockSpec((1, s_len, LANES), lambda g, s: (base + 2 * g + s, 0, 0))

    out = pl.BlockSpec((1, 1, s_len, LANES), lambda g, s: (g, s, 0, 0))
    shape = jax.ShapeDtypeStruct((n_g, 2, s_len, LANES), F32)
    return pl.pallas_call(
        body, name="attn_fwd", grid=(n_g, 2),
        in_specs=[slab(0), slab(6), slab(12)], out_specs=[out, out], out_shape=[shape, shape],
        compiler_params=_params(("arbitrary", "arbitrary")),
    )(qkv, qkv, qkv)


def _pool_mixed(u, halo, i, tm):
    ue = jnp.concatenate([halo, u], axis=0)
    s2 = ue + pltpu.roll(ue, 1, 0)
    s4 = s2 + pltpu.roll(s2, 2, 0)
    s8 = s4 + pltpu.roll(s4, 4, 0)
    s16 = s8 + pltpu.roll(s8, 8, 0)
    grp = lax.broadcasted_iota(jnp.int32, (tm, 256), 1) // HEAD_DIM
    pick = lambda a, b, c, e: jnp.where(grp == 0, a, jnp.where(grp == 1, b, jnp.where(grp == 2, c, e)))
    win_sum = pick(s2[HALO:], s4[HALO:], s8[HALO:], s16[HALO:])
    pos = (i * tm + lax.broadcasted_iota(jnp.int32, (tm, 256), 0)).astype(F32)
    count = jnp.minimum(pos + 1.0, pick(*[float(w) for w in POOL_WINDOWS]))
    return win_sum / count - u, count


def _mix_out(x, u_pool, o_g, lse_g, w_blk, b_pool, pool_scale, w_out_t, gt_m, g_post_mix, g_pre_ffn, sc_f, sh_f, tm):
    s_len, d = x.shape

    def body(x_ref, u_ref, uh_ref, o_ref, l_ref, wb_ref, bp_ref, ps_ref, wo_ref,
             gt_ref, g1_ref, g2_ref, sc_ref, sh_ref,
             x1_ref, y1_ref, h2_ref, cat_ref, attn_ref, lall_ref):
        (o0, o1, o2), (l0, l1, l2) = (o_ref.at[g] for g in range(3)), (l_ref.at[g] for g in range(3))
        i = pl.program_id(0)
        u = u_ref[...]
        halo = uh_ref[...] * (i > 0).astype(F32)
        mixed, _ = _pool_mixed(u, halo, i, tm)
        y = _dot(mixed.astype(BF16), wb_ref[...], NN) + bp_ref[...]
        pool = y * ps_ref[...]
        attn = []
        for s in range(2):
            la, lb, lc = l0[s], l1[s], l2[s]
            mx = jnp.maximum(jnp.maximum(la, lb), lc)
            ea, eb, ec = jnp.exp(la - mx), jnp.exp(lb - mx), jnp.exp(lc - mx)
            den = ea + eb + ec
            lall_ref[s] = mx + jnp.log(den)
            attn.append((ea / den) * o0[s] + (eb / den) * o1[s] + (ec / den) * o2[s])
        attn = jnp.concatenate(attn, axis=1)
        attn_ref[...] = attn
        cat = jnp.concatenate([pool, attn], axis=1).astype(BF16)
        cat_ref[...] = cat
        y1 = _dot(cat, wo_ref[...], NT)
        y1_ref[...] = y1.astype(BF16)
        x1 = x_ref[...] + gt_ref[...] * (y1 * _rstd(y1) * g1_ref[...])
        x1_ref[...] = x1
        h2 = (x1 * _rstd(x1) * g2_ref[...]) * (1.0 + sc_ref[...]) + sh_ref[...]
        h2_ref[...] = h2.astype(BF16)

    tile = lambda w: pl.BlockSpec((tm, w), lambda i: (i, 0))
    slab = pl.BlockSpec((2, tm, LANES), lambda i: (0, i, 0))
    groups = pl.BlockSpec((len(DILATIONS), 2, tm, LANES), lambda i: (0, 0, i, 0))
    const = lambda a: pl.BlockSpec(a.shape, lambda i: (0,) * a.ndim)
    return pl.pallas_call(
        body, name="mix_out", grid=(s_len // tm,),
        in_specs=[tile(d), tile(256), pl.BlockSpec((HALO, 256), lambda i: (_halo_before(i, tm), 0)),
                  groups, groups,
                  const(w_blk), const(b_pool), const(pool_scale), const(w_out_t),
                  const(gt_m), const(g_post_mix), const(g_pre_ffn), const(sc_f), const(sh_f)],
        out_specs=[tile(d), tile(d), tile(d), tile(512), tile(256), slab],
        out_shape=[jax.ShapeDtypeStruct((s_len, d), F32), jax.ShapeDtypeStruct((s_len, d), BF16),
                   jax.ShapeDtypeStruct((s_len, d), BF16), jax.ShapeDtypeStruct((s_len, 512), BF16),
                   jax.ShapeDtypeStruct((s_len, 256), F32), jax.ShapeDtypeStruct((2, s_len, LANES), F32)],
        compiler_params=_params(("arbitrary",)),
    )(x, u_pool, u_pool, o_g, lse_g, w_blk, b_pool, pool_scale, w_out_t, gt_m, g_post_mix, g_pre_ffn, sc_f, sh_f)


def _conv_gate(gate_ext, cw, cb):
    gc = gate_ext * cw[2:3, :] + pltpu.roll(gate_ext, 1, 0) * cw[1:2, :] + pltpu.roll(gate_ext, 2, 0) * cw[0:1, :]
    return gc[HALO:] + cb


def _ffn_fwd_loss(h2, x1, target, w_up_t, w_down, conv_w, conv_b, gt_f, g_post_ffn, tm, ck):
    s_len, d = x1.shape
    d_ff = w_down.shape[0]
    n_t, n_c = s_len // tm, d_ff // ck

    def body(h_ref, hh_ref, x1_ref, tgt_ref, wg_ref, wv_ref, wd_ref, cw_ref, cb_ref, gt_ref, g_ref,
             gate_ref, a_ref, act_ref, vd_ref, dy2_ref, dout_ref, sums_ref, loss_ref, acc_ref):
        i = pl.program_id(0)

        @pl.when(i == 0)
        def _():
            sums_ref[...] = jnp.zeros_like(sums_ref)
            loss_ref[...] = jnp.zeros_like(loss_ref)
            acc_ref[...] = jnp.zeros_like(acc_ref)

        def finish(live):
            y2 = acc_ref[...]
            rstd = _rstd(y2)
            n = y2 * rstd
            rn = n * g_ref[...]
            err = x1_ref[...] + gt_ref[...] * rn - tgt_ref[...]
            keep = lambda v: jnp.where(live, v, 0.0)
            loss_ref[...] += keep(0.5 * jnp.sum(jnp.mean(err * err, axis=-1, keepdims=True), axis=0, keepdims=True))
            dout = err * (1.0 / d)
            dout_ref[...] = dout.astype(BF16)
            drn = dout * gt_ref[...]
            sums_ref[0:1, :] += keep(jnp.sum(dout * rn, axis=0, keepdims=True))
            sums_ref[1:2, :] += keep(jnp.sum(drn * n, axis=0, keepdims=True))
            dy2_ref[...] = _norm_bwd(drn * g_ref[...], n, rstd).astype(BF16)

        @pl.when(i < n_t)
        def _():
            h = h_ref[...]
            h_ext = jnp.concatenate([hh_ref[...], h], axis=0)
            row = lax.broadcasted_iota(jnp.int32, (tm + HALO, ck), 0)
            no_halo = (row < HALO) & (i == 0)

            def up(c):
                cs = slice(c * ck, (c + 1) * ck)
                return jnp.where(no_halo, 0.0, _dot(h_ext, wg_ref[cs, :], NT)), _dot(h, wv_ref[cs, :], NT)

            part = None
            nxt = up(0)
            finish(i > 0)
            for c in range(n_c):
                cs = slice(c * ck, (c + 1) * ck)
                gate_ext, val = nxt
                if c + 1 < n_c:
                    nxt = up(c + 1)
                act, dact = _gelu_parts(_conv_gate(gate_ext, cw_ref[:, cs], cb_ref[:, cs]))
                a = (act * val).astype(BF16)
                gate_ref[:, cs] = gate_ext[HALO:].astype(BF16)
                a_ref[:, cs] = a
                act_ref[:, cs] = act.astype(BF16)
                vd_ref[:, cs] = (val * dact).astype(BF16)
                p = _dot(a, wd_ref[cs, :], NN)
                part = p if part is None else part + p
            acc_ref[...] = part

        @pl.when(i == n_t)
        def _():
            finish(True)

    this = lambda i: jnp.minimum(i, n_t - 1)
    before = lambda i: jnp.maximum(i - 1, 0)
    tok = lambda w, at: pl.BlockSpec((tm, w), lambda i: (at(i), 0))
    vec = pl.BlockSpec((1, d), lambda i: (0, 0))
    once = lambda shape, imap: pl.BlockSpec(shape, imap, pipeline_mode=pl.Buffered(1))
    return pl.pallas_call(
        body, name="ffn_fwd_loss", grid=(n_t + 1,),
        in_specs=[tok(d, this), pl.BlockSpec((HALO, d), lambda i: (_halo_before(this(i), tm), 0)),
                  tok(d, before), tok(d, before),
                  once((d_ff, d), lambda i: (0, 0)), once((d_ff, d), lambda i: (1, 0)), once((d_ff, d), lambda i: (0, 0)),
                  pl.BlockSpec((3, d_ff), lambda i: (0, 0)), pl.BlockSpec((1, d_ff), lambda i: (0, 0)), vec, vec],
        out_specs=[tok(d_ff, this)] * 4 + [tok(d, before), tok(d, before), pl.BlockSpec((8, d), lambda i: (0, 0)),
                                          pl.BlockSpec((8, LANES), lambda i: (0, 0))],
        out_shape=[jax.ShapeDtypeStruct((s_len, d_ff), BF16)] * 4
        + [jax.ShapeDtypeStruct((s_len, d), BF16), jax.ShapeDtypeStruct((s_len, d), BF16),
           jax.ShapeDtypeStruct((8, d), F32), jax.ShapeDtypeStruct((8, LANES), F32)],
        scratch_shapes=[pltpu.VMEM((tm, d), F32)],
        compiler_params=_params(("arbitrary",)),
    )(h2, h2, x1, target, w_up_t, w_up_t, w_down, conv_w, conv_b, gt_f, g_post_ffn)


def _ffn_bwd_act(dy2, gate, a, act, vd, w_down, tm, tf, ck):
    s_len, d = dy2.shape
    d_ff = w_down.shape[0]
    n_t = s_len // tm
    chunks = [slice(lo, min(lo + ck, tf)) for lo in range(0, tf, ck)]

    def body(dy_ref, g_ref, gh_ref, a_ref, act_ref, vd_ref, wd_ref, dgc_ref, dval_ref, dwd_ref, dconv_ref, acc_ref):
        i = pl.program_id(1)

        @pl.when(i == 0)
        def _():
            acc_ref[...] = jnp.zeros_like(acc_ref)
            dconv_ref[...] = jnp.zeros_like(dconv_ref)

        dy = dy_ref[...]

        def down(cs):
            return _dot(dy, wd_ref[cs, :], NT)

        nxt = down(chunks[0])
        for c, cs in enumerate(chunks):
            width = cs.stop - cs.start
            da = nxt
            if c + 1 < len(chunks):
                nxt = down(chunks[c + 1])
            acc_ref[cs, :] += _dot(a_ref[:, cs], dy, TN)
            row = lax.broadcasted_iota(jnp.int32, (tm + HALO, width), 0)
            gate_ext = jnp.where((row < HALO) & (i == 0), 0.0,
                                 jnp.concatenate([gh_ref[:, cs], g_ref[:, cs]], axis=0).astype(F32))
            dgc = da * vd_ref[:, cs].astype(F32)
            dgc_ref[:, cs] = dgc.astype(BF16)
            dval_ref[:, cs] = (da * act_ref[:, cs].astype(F32)).astype(BF16)
            rows = [jnp.sum(dgc * pltpu.roll(gate_ext, 2 - k, 0)[HALO:], axis=0, keepdims=True) for k in range(2)]
            rows += [jnp.sum(dgc * gate_ext[HALO:], axis=0, keepdims=True), jnp.sum(dgc, axis=0, keepdims=True),
                     jnp.zeros((4, width), F32)]
            dconv_ref[:, cs] += jnp.concatenate(rows, axis=0)

        @pl.when(i == n_t - 1)
        def _():
            dwd_ref[...] = acc_ref[...].astype(BF16)

    tokf = pl.BlockSpec((tm, tf), lambda j, i: (i, j))
    return pl.pallas_call(
        body, name="ffn_bwd_act", grid=(d_ff // tf, n_t),
        in_specs=[pl.BlockSpec((tm, d), lambda j, i: (i, 0)), tokf,
                  pl.BlockSpec((HALO, tf), lambda j, i: (_halo_before(i, tm), j)), tokf, tokf, tokf,
                  pl.BlockSpec((tf, d), lambda j, i: (j, 0))],
        out_specs=[tokf, tokf, pl.BlockSpec((tf, d), lambda j, i: (j, 0)), pl.BlockSpec((8, tf), lambda j, i: (0, j))],
        out_shape=[jax.ShapeDtypeStruct((s_len, d_ff), BF16), jax.ShapeDtypeStruct((s_len, d_ff), BF16),
                   jax.ShapeDtypeStruct((d_ff, d), BF16), jax.ShapeDtypeStruct((8, d_ff), F32)],
        scratch_shapes=[pltpu.VMEM((tf, d), F32)],
        compiler_params=_params(("arbitrary", "arbitrary")),
    )(dy2, gate, gate, a, act, vd, w_down)


def _ffn_bwd_up(dgc, dval, w_up_t, conv_w, after, tm):
    s_len, d_ff = dgc.shape
    d = w_up_t.shape[1]
    n_t = s_len // tm

    def body(dg_ref, dgn_ref, dv_ref, cw_ref, w_ref, after_ref, dup_ref, dh_ref):
        i = pl.program_id(0)
        nxt = dgn_ref[...].astype(F32) * (i < n_t - 1).astype(F32)
        ext = jnp.concatenate([dg_ref[...].astype(F32), nxt], axis=0)
        rows = tm + HALO
        dgate = (ext * cw_ref[2:3, :] + pltpu.roll(ext, rows - 1, 0) * cw_ref[1:2, :]
                 + pltpu.roll(ext, rows - 2, 0) * cw_ref[0:1, :])[:tm]
        dup = jnp.concatenate([dgate.astype(BF16), dv_ref[...]], axis=1)
        dup_ref[...] = dup
        dh_ref[...] = _dot(dup, w_ref[...], NN).astype(BF16)

    tokf = pl.BlockSpec((tm, d_ff), lambda i: (i, 0))
    return pl.pallas_call(
        body, name="ffn_bwd_up", grid=(n_t,),
        in_specs=[tokf, pl.BlockSpec((HALO, d_ff), lambda i: (jnp.minimum((i + 1) * (tm // HALO), s_len // HALO - 1), 0)),
                  tokf, pl.BlockSpec((3, d_ff), lambda i: (0, 0)), pl.BlockSpec((2 * d_ff, d), lambda i: (0, 0)),
                  pl.BlockSpec(memory_space=pl.ANY)],
        out_specs=[pl.BlockSpec((tm, 2 * d_ff), lambda i: (i, 0)), pl.BlockSpec((tm, d), lambda i: (i, 0))],
        out_shape=[jax.ShapeDtypeStruct((s_len, 2 * d_ff), BF16), jax.ShapeDtypeStruct((s_len, d), BF16)],
        compiler_params=_params(("arbitrary",)),
    )(dgc, dgc, dval, conv_w, w_up_t, after)


def _mix_bwd(dh2, dout, x1, y1, cat, attn, w_out_t, sc_f, g_pre_ffn, gt_m, g_post_mix, after, tm):
    s_len, d = x1.shape
    n_t = s_len // tm

    def body(dh_ref, do_ref, x1_ref, y1_ref, cat_ref, at_ref, wo_ref, sc_ref, g2_ref, gt_ref, g1_ref, after_ref,
             dx1_ref, dpool_ref, dattn_ref, delta_ref, dwo_ref, sums_ref, acc_ref):
        i = pl.program_id(0)
        dh = dh_ref[...].astype(F32)
        x1 = x1_ref[...]
        r2 = _rstd(x1)
        n2 = x1 * r2
        ng = n2 * g2_ref[...]
        dng = dh * (1.0 + sc_ref[...])
        dx1 = do_ref[...].astype(F32) + _norm_bwd(dng * g2_ref[...], n2, r2)
        dx1_ref[...] = dx1.astype(BF16)
        y1 = y1_ref[...].astype(F32)
        r1 = _rstd(y1)
        n1 = y1 * r1
        drn = dx1 * gt_ref[...]
        dy1 = _norm_bwd(drn * g1_ref[...], n1, r1).astype(BF16)
        dcat = _dot(dy1, wo_ref[...], NN)
        dpool_ref[...] = dcat[:, 0:256]
        lane = lax.broadcasted_iota(jnp.int32, (tm, LANES), 1)
        first = lane < HEAD_DIM
        for s in range(2):
            da = dcat[:, 256 + s * LANES:256 + (s + 1) * LANES]
            dattn_ref[s] = da
            prod = da * at_ref[:, s * LANES:(s + 1) * LANES]
            tot = jnp.sum(prod, axis=-1, keepdims=True)
            lo = jnp.sum(jnp.where(first, prod, 0.0), axis=-1, keepdims=True)
            delta_ref[s] = jnp.where(first, lo, tot - lo)
        dwo = _dot(dy1, cat_ref[...], TN)
        sums = jnp.concatenate(
            [jnp.sum(dh, axis=0, keepdims=True), jnp.sum(dh * ng, axis=0, keepdims=True),
             jnp.sum(dng * n2, axis=0, keepdims=True), jnp.sum(dx1 * (n1 * g1_ref[...]), axis=0, keepdims=True),
             jnp.sum(drn * n1, axis=0, keepdims=True), jnp.zeros((3, d), F32)], axis=0)

        @pl.when(i == 0)
        def _():
            acc_ref[...] = dwo
            sums_ref[...] = sums

        @pl.when(i > 0)
        def _():
            acc_ref[...] += dwo
            sums_ref[...] += sums

        @pl.when(i == n_t - 1)
        def _():
            dwo_ref[...] = acc_ref[...].astype(BF16)

    tile = lambda w: pl.BlockSpec((tm, w), lambda i: (i, 0))
    slab = pl.BlockSpec((2, tm, LANES), lambda i: (0, i, 0))
    vec = pl.BlockSpec((1, d), lambda i: (0, 0))
    return pl.pallas_call(
        body, name="mix_bwd", grid=(n_t,),
        in_specs=[tile(d), tile(d), tile(d), tile(d), tile(512), tile(256),
                  pl.BlockSpec((d, 512), lambda i: (0, 0)), vec, vec, vec, vec, pl.BlockSpec(memory_space=pl.ANY)],
        out_specs=[tile(d), tile(256), slab, slab, pl.BlockSpec((d, 512), lambda i: (0, 0)),
                   pl.BlockSpec((8, d), lambda i: (0, 0))],
        out_shape=[jax.ShapeDtypeStruct((s_len, d), BF16), jax.ShapeDtypeStruct((s_len, 256), F32),
                   jax.ShapeDtypeStruct((2, s_len, LANES), F32), jax.ShapeDtypeStruct((2, s_len, LANES), F32),
                   jax.ShapeDtypeStruct((d, 512), BF16), jax.ShapeDtypeStruct((8, d), F32)],
        scratch_shapes=[pltpu.VMEM((d, 512), F32)],
        compiler_params=_params(("arbitrary",)),
    )(dh2, dout, x1, y1, cat, attn, w_out_t, sc_f, g_pre_ffn, gt_m, g_post_mix, after)


def _pool_bwd(dpool, u_pool, w_blk, b_pool, pool_scale, tm):
    s_len = dpool.shape[0]
    n_t = s_len // tm

    def body(dp_ref, dpn_ref, u_ref, uh_ref, wb_ref, bp_ref, ps_ref, du_ref, dwp_ref, sums_ref, acc_ref):
        i = pl.program_id(0)
        u = u_ref[...]
        mixed, _ = _pool_mixed(u, uh_ref[...] * (i > 0).astype(F32), i, tm)
        mixed_b = mixed.astype(BF16)
        y = _dot(mixed_b, wb_ref[...], NN) + bp_ref[...]
        dp = dp_ref[...]
        dy = dp * ps_ref[...]
        dwb = _dot(mixed_b, dy.astype(BF16), TN)
        sums = jnp.concatenate([jnp.sum(dy, axis=0, keepdims=True), jnp.sum(dp * y, axis=0, keepdims=True),
                                jnp.zeros((6, 256), F32)], axis=0)
        dp_ext = jnp.concatenate([dp, dpn_ref[...] * (i < n_t - 1).astype(F32)], axis=0)
        dmix = _dot((dp_ext * ps_ref[...]).astype(BF16), wb_ref[...], NT)
        rows = tm + HALO
        grp = lax.broadcasted_iota(jnp.int32, (rows, 256), 1) // HEAD_DIM
        pick = lambda a, b, c, e: jnp.where(grp == 0, a, jnp.where(grp == 1, b, jnp.where(grp == 2, c, e)))
        pos = (i * tm + lax.broadcasted_iota(jnp.int32, (rows, 256), 0)).astype(F32)
        z = dmix / jnp.minimum(pos + 1.0, pick(*[float(w) for w in POOL_WINDOWS]))
        f2 = z + pltpu.roll(z, rows - 1, 0)
        f4 = f2 + pltpu.roll(f2, rows - 2, 0)
        f8 = f4 + pltpu.roll(f4, rows - 4, 0)
        f16 = f8 + pltpu.roll(f8, rows - 8, 0)
        du_ref[...] = (pick(f2, f4, f8, f16) - dmix)[:tm]

        @pl.when(i == 0)
        def _():
            acc_ref[...] = dwb
            sums_ref[...] = sums

        @pl.when(i > 0)
        def _():
            acc_ref[...] += dwb
            sums_ref[...] += sums

        @pl.when(i == n_t - 1)
        def _():
            full = acc_ref[...]
            for gi in range(len(POOL_WINDOWS)):
                lo = gi * HEAD_DIM
                dwp_ref[gi] = full[lo:lo + HEAD_DIM, lo:lo + HEAD_DIM]

    n_g = len(POOL_WINDOWS)
    tile = pl.BlockSpec((tm, 256), lambda i: (i, 0))
    const = lambda a: pl.BlockSpec(a.shape, lambda i: (0,) * a.ndim)
    return pl.pallas_call(
        body, name="pool_bwd", grid=(n_t,),
        in_specs=[tile, pl.BlockSpec((HALO, 256), lambda i: (jnp.minimum((i + 1) * (tm // HALO), s_len // HALO - 1), 0)),
                  tile, pl.BlockSpec((HALO, 256), lambda i: (_halo_before(i, tm), 0)),
                  const(w_blk), const(b_pool), const(pool_scale)],
        out_specs=[tile, pl.BlockSpec((n_g, HEAD_DIM, HEAD_DIM), lambda i: (0, 0, 0)), pl.BlockSpec((8, 256), lambda i: (0, 0))],
        out_shape=[jax.ShapeDtypeStruct((s_len, 256), F32), jax.ShapeDtypeStruct((n_g, HEAD_DIM, HEAD_DIM), F32),
                   jax.ShapeDtypeStruct((8, 256), F32)],
        scratch_shapes=[pltpu.VMEM((256, 256), F32)],
        compiler_params=_params(("arbitrary",)),
    )(dpool, dpool, u_pool, u_pool, w_blk, b_pool, pool_scale)


def _attn_bwd(qkv, dattn, lse_all, delta, after):
    s_len = qkv.shape[1]
    n_g = len(DILATIONS)

    def body(q_ref, k_ref, v_ref, do_ref, l_ref, dl_ref, after_ref, dq_ref, dk_ref, dv_ref):
        lane = lax.broadcasted_iota(jnp.int32, (BLOCK, LANES), 1)
        first = lane < HEAD_DIM

        def group(dil):
            nb = s_len // (BLOCK * dil)

            def block(t, carry):
                dk_part, dv_part = carry
                r, n = t // nb, t % nb
                cur = _block_rows(n, r, dil)
                prev = _block_rows(jnp.maximum(n - 1, 0), r, dil)
                q = q_ref[0, cur, :]
                do = do_ref[0, cur, :]
                lse = l_ref[0, cur, :]
                dlt = dl_ref[0, cur, :]
                kcat = jnp.concatenate([k_ref[0, prev, :], k_ref[0, cur, :]], axis=0).astype(BF16)
                vcat = jnp.concatenate([v_ref[0, prev, :], v_ref[0, cur, :]], axis=0).astype(BF16)
                valid = _band_mask(n)
                stack = lambda a: jnp.concatenate([jnp.where(first, a, 0.0), jnp.where(first, 0.0, a)], axis=0)
                rows2 = lambda a: jnp.concatenate([a[:, 0:1], a[:, HEAD_DIM:HEAD_DIM + 1]], axis=0)
                q2, do2 = stack(q).astype(BF16), stack(do).astype(BF16)
                valid2 = jnp.concatenate([valid, valid], axis=0)
                p = jnp.where(valid2, jnp.exp(_dot(q2, kcat, NT) - rows2(lse)), 0.0)
                ds = (p * (_dot(do2, vcat, NT) - rows2(dlt))).astype(BF16)
                dq2 = _dot(ds, kcat, NN)
                dq_ref[0, 0, cur, :] = jnp.where(first, dq2[:BLOCK], dq2[BLOCK:])
                dkc = _dot(ds, q2, TN)
                dvc = _dot(p.astype(BF16), do2, TN)
                dk_ref[0, 0, prev, :] = dk_part + dkc[:BLOCK]
                dv_ref[0, 0, prev, :] = dv_part + dvc[:BLOCK]
                dk_ref[0, 0, cur, :] = dkc[BLOCK:]
                dv_ref[0, 0, cur, :] = dvc[BLOCK:]
                return dkc[BLOCK:], dvc[BLOCK:]

            def blocks(tt, carry):
                for u in range(ATTN_BWD_UNROLL):
                    carry = block(tt * ATTN_BWD_UNROLL + u, carry)
                return carry

            zero = jnp.zeros((BLOCK, LANES), F32)
            lax.fori_loop(0, nb * dil // ATTN_BWD_UNROLL, blocks, (zero, zero))

        for gi, dil in enumerate(DILATIONS):
            pl.when(pl.program_id(1) == gi)(functools.partial(group, dil))

    def slab(base):
        return pl.BlockSpec((1, s_len, LANES), lambda s, g: (base + 2 * g + s, 0, 0))

    one = pl.BlockSpec((1, s_len, LANES), lambda s, g: (s, 0, 0))
    out = pl.BlockSpec((1, 1, s_len, LANES), lambda s, g: (g, s, 0, 0))
    shape = jax.ShapeDtypeStruct((n_g, 2, s_len, LANES), F32)
    return pl.pallas_call(
        body, name="attn_bwd", grid=(2, n_g),
        in_specs=[slab(0), slab(6), slab(12), one, one, one, pl.BlockSpec(memory_space=pl.ANY)],
        out_specs=[out, out, out], out_shape=[shape, shape, shape],
        compiler_params=_params(("arbitrary", "arbitrary")),
    )(qkv, qkv, qkv, dattn, lse_all, delta, after)


def _dproj_wgrad_in(du, dqkv, rope, h1, tm, cm):
    s_len = du.shape[0]
    d = h1.shape[1]
    n_proj = 256 + 18 * LANES
    n_t = s_len // tm

    def body(du_ref, dq_ref, dk_ref, dv_ref, cs_ref, spread_ref, h_ref, dproj_ref, dw_ref, acc_ref):
        i = pl.program_id(0)

        @pl.when(i == 0)
        def _():
            acc_ref[...] = jnp.zeros_like(acc_ref)

        dproj_ref[:, 0:256] = du_ref[...].astype(BF16)
        lanes = _rope_lanes(cs_ref, spread_ref)
        col = 256
        for kind, dref in enumerate((dq_ref, dk_ref, dv_ref)):
            for grp in range(3):
                for s in range(2):
                    piece = dref[grp, s]
                    if kind < 2:
                        piece = _rope_bwd(piece, lanes)
                    if kind == 0:
                        piece = piece * (HEAD_DIM ** -0.5)
                    dproj_ref[:, col:col + LANES] = piece.astype(BF16)
                    col += LANES

        for c0 in range(0, n_proj, cm):
            acc_ref[c0:c0 + cm, :] += _dot(dproj_ref[:, c0:c0 + cm], h_ref[...], TN)

        @pl.when(i == n_t - 1)
        def _():
            dw_ref[...] = acc_ref[...].astype(BF16)

    groups = pl.BlockSpec((len(DILATIONS), 2, tm, LANES), lambda i: (0, 0, i, 0))
    return pl.pallas_call(
        body, name="dproj_wgrad_in", grid=(n_t,),
        in_specs=[pl.BlockSpec((tm, 256), lambda i: (i, 0))] + [groups] * 3
        + [pl.BlockSpec((rope[0].shape[0], tm), lambda i: (0, i)), pl.BlockSpec(rope[1].shape, lambda i: (0, 0, 0)),
           pl.BlockSpec((tm, d), lambda i: (i, 0))],
        out_specs=[pl.BlockSpec((tm, n_proj), lambda i: (i, 0)), pl.BlockSpec((n_proj, d), lambda i: (0, 0))],
        out_shape=[jax.ShapeDtypeStruct((s_len, n_proj), BF16), jax.ShapeDtypeStruct((n_proj, d), BF16)],
        scratch_shapes=[pltpu.VMEM((n_proj, d), F32)],
        compiler_params=_params(("arbitrary",)),
    )(du, *dqkv, *rope, h1)


def _inproj_bwd(dproj, w_in_t, x, dx1, sc_m, g_pre_mix, after, tm):
    s_len, d = x.shape
    n_proj = w_in_t.shape[0]
    n_t = s_len // tm

    def body(dproj_ref, w_ref, x_ref, dx1_ref, sc_ref, g_ref, after_ref, dx_ref, sums_ref):
        i = pl.program_id(0)
        halves = [slice(0, tm // 2), slice(tm // 2, tm)]
        dhs = [_dot(dproj_ref[rs, :], w_ref[...], NN) for rs in halves]
        sums = None
        for rs, dh in zip(halves, dhs):
            xv = x_ref[rs, :]
            r = _rstd(xv)
            n = xv * r
            dng = dh * (1.0 + sc_ref[...])
            dx_ref[rs, :] = dx1_ref[rs, :].astype(F32) + _norm_bwd(dng * g_ref[...], n, r)
            part = jnp.concatenate([jnp.sum(dh, axis=0, keepdims=True), jnp.sum(dh * (n * g_ref[...]), axis=0, keepdims=True),
                                    jnp.sum(dng * n, axis=0, keepdims=True), jnp.zeros((5, d), F32)], axis=0)
            sums = part if sums is None else sums + part

        @pl.when(i == 0)
        def _():
            sums_ref[...] = sums

        @pl.when(i > 0)
        def _():
            sums_ref[...] += sums

    tile = lambda w: pl.BlockSpec((tm, w), lambda i: (i, 0))
    vec = pl.BlockSpec((1, d), lambda i: (0, 0))
    return pl.pallas_call(
        body, name="inproj_bwd", grid=(n_t,),
        in_specs=[tile(n_proj), pl.BlockSpec((n_proj, d), lambda i: (0, 0)), tile(d), tile(d), vec, vec,
                  pl.BlockSpec(memory_space=pl.ANY)],
        out_specs=[tile(d), pl.BlockSpec((8, d), lambda i: (0, 0))],
        out_shape=[jax.ShapeDtypeStruct((s_len, d), F32), jax.ShapeDtypeStruct((8, d), F32)],
        compiler_params=_params(("arbitrary",)),
    )(dproj, w_in_t, x, dx1, sc_m, g_pre_mix, after)


def _wgrad(a, b, name, tk, tmm):
    s_len, m = a.shape
    n = b.shape[1]
    n_k = s_len // tk

    def body(a_ref, b_ref, o_ref, acc_ref):
        k = pl.program_id(1)
        part = _dot(a_ref[...], b_ref[...], TN)

        @pl.when(k == 0)
        def _():
            acc_ref[...] = part

        @pl.when(k > 0)
        def _():
            acc_ref[...] += part

        @pl.when(k == n_k - 1)
        def _():
            o_ref[...] = acc_ref[...].astype(BF16)

    return pl.pallas_call(
        body, name=name, grid=(m // tmm, n_k),
        in_specs=[pl.BlockSpec((tk, tmm), lambda j, k: (k, j)), pl.BlockSpec((tk, n), lambda j, k: (k, 0))],
        out_specs=pl.BlockSpec((tmm, n), lambda j, k: (j, 0)),
        out_shape=jax.ShapeDtypeStruct((m, n), BF16),
        scratch_shapes=[pltpu.VMEM((tmm, n), F32)],
        compiler_params=_params(("arbitrary", "arbitrary")),
    )(a, b)


def _place():
    return lax.axis_index("x"), lax.axis_index("y"), lax.axis_index("c")


def _peer(k):
    x, y, c = _place()
    bx, by, bc = (k >> 2) & 1, (k >> 1) & 1, k & 1
    return (x ^ bx if bx else x, y ^ by if by else y, c ^ bc if bc else c)


def _index(pos):
    return 4 * pos[0] + 2 * pos[1] + pos[2]


def _entry_exchange(c_rows, w_ada, b_ada, taps, shards, later):
    d = c_rows.shape[1]
    ncol = w_ada.shape[1]
    n_w, n_p = len(shards), len(later)

    def body(c_ref, w_ref, b_ref, t_ref, *rest):
        srcs, rest = rest[:n_w], rest[n_w:]
        later_refs, rest = rest[:n_p], rest[n_p:]
        (call_ref, mod_ref, tall_ref), rest = rest[:3], rest[3:]
        outs, rest = rest[:n_w], rest[n_w:]
        zones, rest = rest[:n_p], rest[n_p:]
        stage_ref, s_send, s_recv, w_send, w_recv, local_sems = rest[:6]
        wide, narrow, place_sems = rest[6:6 + n_p], rest[6 + n_p:6 + 2 * n_p], rest[6 + 2 * n_p]
        x, y, c = _place()
        here, sibling = (x, y, c), (x, y, 1 - c)
        chips = [(1 - x, y), (x, 1 - y), (1 - x, 1 - y)]
        me = _index(here)

        def small(kind, src, dst, k):
            return pltpu.make_async_remote_copy(src_ref=src, dst_ref=dst, send_sem=s_send.at[kind, k - 1],
                                                recv_sem=s_recv.at[kind, k - 1], device_id=_peer(k), device_id_type=MESH)

        gather = lambda k: small(0, c_ref, call_ref.at[me], k)
        scatter = lambda k: small(1, stage_ref.at[_index(_peer(k))], mod_ref.at[me], k)
        gather_taps = lambda k: small(2, t_ref, tall_ref.at[me], k)

        def rows(w, pos):
            r = shards[w].shape[0]
            return outs[w].at[pl.ds(pl.multiple_of(_index(pos) * r, 16), r), :]

        def block(k, w, pos, to, own=False):
            return pltpu.make_async_remote_copy(
                src_ref=srcs[w] if own else rows(w, pos), dst_ref=rows(w, pos),
                send_sem=w_send.at[k, w], recv_sem=w_recv.at[k, w], device_id=to, device_id_type=MESH)

        call_ref[me] = c_ref[...]
        tall_ref[me] = t_ref[...]
        for k in range(1, N_DEV):
            gather(k).start()
        for k in range(1, N_DEV):
            gather_taps(k).start()
        mine = [pltpu.make_async_copy(srcs[w], rows(w, here), local_sems.at[w]) for w in range(n_w)]
        for cp in mine:
            cp.start()
        first = [block(0, w, here, sibling, own=True) for w in range(n_w)]
        first += [block(1 + j, w, here, (*chip, c), own=True) for j, chip in enumerate(chips) for w in range(n_w)]
        for cp in first:
            cp.start()
        fetch = [pltpu.make_async_copy(later_refs[w], wide[w], place_sems.at[0, w]) for w in range(n_p)]
        for cp in fetch:
            cp.start()

        for k in range(1, N_DEV):
            gather(k).wait_recv()
        cv = jnp.concatenate([call_ref[b, 0:1, :] for b in range(N_DEV)], axis=0)
        act = cv * jax.nn.sigmoid(cv)
        mod = lax.dot_general(act, w_ref[...], NN, preferred_element_type=F32,
                              precision=lax.Precision.HIGHEST) + b_ref[:, pl.ds(pl.multiple_of(me * ncol, LANES), ncol)]
        for b in range(N_DEV):
            stage_ref[b] = jnp.broadcast_to(mod[b:b + 1, :], (8, ncol))
        mod_ref[me] = stage_ref[me]
        for k in range(1, N_DEV):
            scatter(k).start()

        placed = []
        for w in range(n_p):
            fetch[w].wait()
            narrow[w][...] = wide[w][...].astype(BF16)
            r = later[w].shape[0]
            placed.append(pltpu.make_async_copy(narrow[w], zones[w].at[pl.ds(pl.multiple_of(me * r, 16), r), :],
                                                place_sems.at[1, w]))
            placed[-1].start()

        passed = []
        for j, chip in enumerate(chips):
            for w in range(n_w):
                block(1 + j, w, (*chip, c), here).wait_recv()
                fwd = block(4 + j, w, (*chip, c), sibling)
                fwd.start()
                passed.append(fwd)
        for w in range(n_w):
            block(0, w, sibling, here).wait_recv()
        for j, chip in enumerate(chips):
            for w in range(n_w):
                block(4 + j, w, (*chip, 1 - c), here).wait_recv()
        for k in range(1, N_DEV):
            scatter(k).wait_recv()
            gather_taps(k).wait_recv()
        for cp in first + passed:
            cp.wait_send()
        for k in range(1, N_DEV):
            gather(k).wait_send()
            scatter(k).wait_send()
            gather_taps(k).wait_send()
        for cp in mine + placed:
            cp.wait()

    vmem, hbm = pl.BlockSpec(memory_space=pltpu.VMEM), pl.BlockSpec(memory_space=pltpu.HBM)
    out = pl.pallas_call(
        body, name="entry_exchange",
        in_specs=[vmem] * 4 + [hbm] * (n_w + n_p), out_specs=[vmem] * 3 + [hbm] * (n_w + n_p),
        out_shape=[jax.ShapeDtypeStruct((N_DEV, 8, d), F32), jax.ShapeDtypeStruct((N_DEV, 8, ncol), F32),
                   jax.ShapeDtypeStruct((N_DEV,) + taps.shape, F32)]
        + [jax.ShapeDtypeStruct((N_DEV * s.shape[0], s.shape[1]), s.dtype) for s in shards]
        + [jax.ShapeDtypeStruct((N_DEV * s.shape[0], s.shape[1]), BF16) for s in later],
        scratch_shapes=[pltpu.VMEM((N_DEV, 8, ncol), F32), pltpu.SemaphoreType.DMA((3, N_DEV - 1)),
                        pltpu.SemaphoreType.DMA((3, N_DEV - 1)), pltpu.SemaphoreType.DMA((N_DEV - 1, n_w)),
                        pltpu.SemaphoreType.DMA((N_DEV - 1, n_w)), pltpu.SemaphoreType.DMA((n_w,))]
        + [pltpu.VMEM(s.shape, F32) for s in later] + [pltpu.VMEM(s.shape, BF16) for s in later]
        + [pltpu.SemaphoreType.DMA((2, n_p))],
        compiler_params=_params(),
    )(c_rows, w_ada, b_ada, taps, *shards, *later)
    return out[0], out[1], out[2], out[3:3 + n_w], out[3 + n_w:]


def _peer_copies(mode, srcs, lands, send_sems, recv_sems):
    if mode in ("gather_ici", "gather_d2d"):
        x, y, c = _place()
        sibling = (x, y, 1 - c)
        chips = [(1 - x, y), (x, 1 - y), (1 - x, 1 - y)]
        n = len(lands)

        def rows(w, pos):
            r = lands[w].shape[0] // N_DEV
            return lands[w].at[pl.ds(pl.multiple_of(_index(pos) * r, 16), r), :]

        def copy(k, w, src, dst, to):
            return pltpu.make_async_remote_copy(src_ref=src, dst_ref=dst, send_sem=send_sems.at[k * n + w],
                                                recv_sem=recv_sems.at[k * n + w], device_id=to, device_id_type=MESH)

        if mode == "gather_ici":
            targets = [sibling] + [(*chip, c) for chip in chips]
            return [copy(k, w, rows(w, (x, y, c)), rows(w, (x, y, c)), to) for k, to in enumerate(targets) for w in range(n)]
        return [copy(j, w, rows(w, (*chip, c)), rows(w, (*chip, c)), sibling)
                for j, chip in enumerate(chips) for w in range(n)]
    me = _index(_place())
    modes = (mode,) * len(srcs) if isinstance(mode, str) else mode
    copies = []
    for k in range(1, N_DEV):
        peer = _peer(k)
        for w, (src, land) in enumerate(zip(srcs, lands)):
            if modes[w] == "gather":
                r = src.shape[0]
                dst = land.at[pl.ds(pl.multiple_of(me * r, 16), r), :]
            elif modes[w] == "allgather":
                dst = land.at[me]
            else:
                r = src.shape[0] // N_DEV
                src = src.at[pl.ds(pl.multiple_of(_index(peer) * r, 16), r), :]
                dst = land.at[me]
            copies.append(pltpu.make_async_remote_copy(
                src_ref=src, dst_ref=dst, send_sem=send_sems.at[(k - 1) * len(srcs) + w],
                recv_sem=recv_sems.at[(k - 1) * len(srcs) + w],
                device_id=peer, device_id_type=MESH))
    return copies


def _exchange_start(mode, srcs, lands, name):
    n_s, n_a = len(srcs), len(srcs) + len(lands)
    n_cp = _COPIES_PER_ARRAY.get(mode, N_DEV - 1) * len(lands)

    def body(*refs):
        for cp in _peer_copies(mode, refs[:n_s], refs[n_s:n_a], refs[n_a], refs[n_a + 1]):
            cp.start()

    hbm, sem = pl.BlockSpec(memory_space=pltpu.HBM), pl.BlockSpec(memory_space=pltpu.SEMAPHORE)
    arrays = list(srcs) + list(lands)
    out = pl.pallas_call(
        body, name=name,
        out_shape=(pltpu.SemaphoreType.DMA((n_cp,)), pltpu.SemaphoreType.DMA((n_cp,)),
                   *[pltpu.HBM(a.shape, a.dtype) for a in arrays]),
        in_specs=[hbm] * n_a, out_specs=(sem, sem, *[hbm] * n_a),
        input_output_aliases={i: 2 + i for i in range(n_a)},
        compiler_params=pltpu.CompilerParams(has_side_effects=pltpu.SideEffectType.DATAFLOW_SIDE_EFFECTING),
    )(*[pltpu.with_memory_space_constraint(a, pltpu.HBM) for a in arrays])
    return out[0], out[1], out[2:2 + n_s], out[2 + n_s:2 + n_a], out[2]


_COPIES_PER_ARRAY = {"gather_ici": 4, "gather_d2d": 3}


def _exchange_wait(mode, send_sems, recv_sems, srcs, lands, after, name):
    n_s, n_a = len(srcs), len(srcs) + len(lands)

    def body(*refs):
        copies = _peer_copies(mode, refs[:n_s], refs[n_s:n_a], refs[n_a], refs[n_a + 1])
        for cp in copies:
            cp.wait_send()
        for cp in copies:
            cp.wait_recv()

    hbm, sem = pl.BlockSpec(memory_space=pltpu.HBM), pl.BlockSpec(memory_space=pltpu.SEMAPHORE)
    arrays = list(srcs) + list(lands)
    out = pl.pallas_call(
        body, name=name, out_shape=tuple(pltpu.HBM(a.shape, a.dtype) for a in arrays),
        in_specs=[hbm] * n_a + [sem, sem, pl.BlockSpec(memory_space=pl.ANY)], out_specs=tuple([hbm] * n_a),
        input_output_aliases={i: i for i in range(n_a)},
        compiler_params=pltpu.CompilerParams(has_side_effects=pltpu.SideEffectType.DATAFLOW_SIDE_EFFECTING),
    )(*arrays, send_sems, recv_sems, after)
    return out[:n_s], out[n_s:]


SMALL_WEIGHTS = ("b_ada", "g_pre_mix", "g_post_mix", "g_pre_ffn", "g_post_ffn", "w_pool", "b_pool", "pool_scale", "conv_b")


MOD_ROWS = ((0, 0), (0, 1), (1, 3), (1, 0), (1, 1), (2, 0))


def _small_sum_adam(mine, gathered, weights, moms, vels):
    n_l, n_w = len(mine), len(weights)
    d = mine[0].shape[1]

    def body(*refs):
        loc, got = refs[:n_l], refs[n_l:2 * n_l]
        w_refs, m_refs, v_refs = (refs[2 * n_l + k * n_w:2 * n_l + (k + 1) * n_w] for k in range(3))
        outs = refs[2 * n_l + 3 * n_w:]
        dmod_ref, conv_ref, loss_ref = outs[4 * n_w:]
        me = _index(_place())
        part = lambda a, dev: jnp.where(dev == me, loc[a][...], got[a][dev])
        totals = []
        for a in range(n_l):
            tot = part(a, 0)
            for dev in range(1, N_DEV):
                tot = tot + part(a, dev)
            totals.append(tot)
        t_in, t_mix, t_ffn, t_pool, t_blk, t_conv, t_loss = totals
        conv_ref[...] = t_conv
        loss_ref[...] = t_loss
        for dev in range(N_DEV):
            for k, (a, r) in enumerate(MOD_ROWS):
                dmod_ref[dev:dev + 1, k * d:(k + 1) * d] = part(a, dev)[r:r + 1, :]

        def update(idx, g, at=()):
            sel = lambda ref: ref.at[at] if at else ref
            delta, nm, nv = _adam_math(sel(w_refs[idx])[...], g, sel(m_refs[idx])[...], sel(v_refs[idx])[...])
            for k, val in enumerate((g, delta, nm, nv)):
                sel(outs[4 * idx + k])[...] = val

        tots = (t_in, t_mix, t_ffn)
        update(0, jnp.concatenate([tots[a][r:r + 1] for a, r in MOD_ROWS], axis=1))
        update(1, t_in[2:3])
        update(2, t_mix[4:5])
        update(3, t_mix[2:3])
        update(4, t_ffn[1:2])
        for gi in range(len(POOL_WINDOWS)):
            update(5, t_blk[gi], at=(0, gi))
        update(6, jnp.concatenate([t_pool[0:1, gi * HEAD_DIM:(gi + 1) * HEAD_DIM] for gi in range(len(POOL_WINDOWS))], axis=0),
               at=(0,))
        update(7, t_pool[1:2])
        update(8, t_conv[3:4])

    vmem = pl.BlockSpec(memory_space=pltpu.VMEM)
    out = pl.pallas_call(
        body, name="small_sum_adam", in_specs=[vmem] * (2 * n_l + 3 * n_w), out_specs=[vmem] * (4 * n_w + 3),
        out_shape=[jax.ShapeDtypeStruct(w.shape, F32) for w in weights for _ in range(4)]
        + [jax.ShapeDtypeStruct((N_DEV, 6 * d), F32), jax.ShapeDtypeStruct(mine[5].shape, F32),
           jax.ShapeDtypeStruct(mine[6].shape, F32)],
        compiler_params=_params(),
    )(*mine, *gathered, *weights, *moms, *vels)
    return out[:4 * n_w], out[4 * n_w], out[4 * n_w + 1], out[4 * n_w + 2]


def _adam_math(w, g, m, v):
    m = ADAM_B1 * m + (1.0 - ADAM_B1) * g
    v = ADAM_B2 * v + (1.0 - ADAM_B2) * (g * g)
    m_hat = m / (1.0 - ADAM_B1 ** ADAM_STEP)
    v_hat = v / (1.0 - ADAM_B2 ** ADAM_STEP)
    delta = -ADAM_LR * (m_hat / (jnp.sqrt(v_hat) + ADAM_EPS) + ADAM_WD * w)
    return delta, m, v


def _adam(w, g, m, v, name):
    def body(w_ref, g_ref, m_ref, v_ref, d_ref, nm_ref, nv_ref):
        d_ref[...], nm_ref[...], nv_ref[...] = _adam_math(w_ref[...], g_ref[...], m_ref[...], v_ref[...])

    vmem = pl.BlockSpec(memory_space=pltpu.VMEM)
    return pl.pallas_call(
        body, name=name, in_specs=[vmem] * 4, out_specs=[vmem] * 3,
        out_shape=[jax.ShapeDtypeStruct(w.shape, F32)] * 3, compiler_params=_params(),
    )(w, g, m, v)


def _sum_adam(own, parts, w, m, v, me, name, tr):
    _, rows, cols = parts.shape
    turned = w.shape == (cols, rows) and rows != cols
    assert tr == rows or not turned
    n_t = rows // tr

    def body(me_ref, own_ref, p_hbm, w_ref, m_ref, v_ref, g_ref, d_ref, nm_ref, nv_ref, p_buf, sems):
        i = pl.program_id(0)
        tile = lambda t: pl.ds(pl.multiple_of(t * tr, 16), tr)
        fetch = lambda dev, t: pltpu.make_async_copy(p_hbm.at[dev, tile(t), :], p_buf.at[dev, tile(t), :], sems.at[dev, t])

        @pl.when(i == 0)
        def _():
            for t in range(n_t):
                for dev in range(N_DEV):
                    fetch(dev, t).start()

        for dev in range(N_DEV):
            fetch(dev, i).wait()
        part = lambda dev: jnp.where(dev == me_ref[0], own_ref[...], p_buf[dev, tile(i), :]).astype(F32)
        g = part(0)
        for dev in range(1, N_DEV):
            g = g + part(dev)
        g = g.T if turned else g
        g_ref[...] = g
        d_ref[...], nm_ref[...], nv_ref[...] = _adam_math(w_ref[...], g, m_ref[...], v_ref[...])

    spec = pl.BlockSpec((cols, rows) if turned else (tr, cols), lambda i, me_ref: (i, 0))
    shape = jax.ShapeDtypeStruct(w.shape, F32)
    return pl.pallas_call(
        body, name=name, out_shape=[shape] * 4,
        grid_spec=pltpu.PrefetchScalarGridSpec(
            num_scalar_prefetch=1, grid=(n_t,),
            in_specs=[pl.BlockSpec((tr, cols), lambda i, me_ref: (me_ref[0] * n_t + i, 0)),
                      pl.BlockSpec(memory_space=pl.ANY), spec, spec, spec],
            out_specs=[spec] * 4,
            scratch_shapes=[pltpu.VMEM(parts.shape, parts.dtype), pltpu.SemaphoreType.DMA((N_DEV, n_t))]),
        compiler_params=_params(("arbitrary",)),
    )(me.reshape(1).astype(jnp.int32), own, parts, w, m, v)


def _ada_grad_adam(c_all, dmod_all, w, m, v, tr):
    rows, cols = w.shape

    def body(c_ref, dm_ref, w_ref, m_ref, v_ref, g_ref, d_ref, nm_ref, nv_ref):
        cv = c_ref[...]
        act = cv * jax.nn.sigmoid(cv)
        dmod = dm_ref[:, pl.ds(pl.multiple_of(_index(_place()) * cols, LANES), cols)]
        g = lax.dot_general(act, dmod, TN, preferred_element_type=F32, precision=lax.Precision.HIGHEST)
        g_ref[...] = g
        d_ref[...], nm_ref[...], nv_ref[...] = _adam_math(w_ref[...], g, m_ref[...], v_ref[...])

    spec = pl.BlockSpec((tr, cols), lambda i: (i, 0))
    shape = jax.ShapeDtypeStruct((rows, cols), F32)
    return pl.pallas_call(
        body, name="ada_grad_adam", grid=(rows // tr,),
        in_specs=[pl.BlockSpec((N_DEV, tr), lambda i: (0, i)), pl.BlockSpec(dmod_all.shape, lambda i: (0, 0)), spec, spec, spec],
        out_specs=[spec] * 4, out_shape=[shape] * 4, compiler_params=_params(("arbitrary",)),
    )(c_all, dmod_all, w, m, v)


def _rope_tables(positions):
    inv_freq = ROPE_THETA ** (-jnp.arange(0, 2 * ROT_HALF, 2, dtype=F32) / (2 * ROT_HALF))
    ang = inv_freq[:, None] * positions.astype(F32)[None, :]
    rows = jnp.concatenate([jnp.cos(ang), jnp.sin(ang), jnp.ones_like(ang)], axis=0)
    spread = [[[0.0] * LANES for _ in range(3 * ROT_HALF)] for _ in range(3)]
    for lane in range(LANES):
        p, j = lane % HEAD_DIM, lane % ROT_HALF
        if p < ROT_HALF:
            spread[0][j][lane] = 1.0
            spread[1][ROT_HALF + j][lane] = -1.0
        elif p < 2 * ROT_HALF:
            spread[0][j][lane] = 1.0
            spread[2][ROT_HALF + j][lane] = 1.0
        else:
            spread[0][2 * ROT_HALF][lane] = 1.0
    return rows, jnp.array(spread, F32)


def _pad_rows(a, rows):
    return jnp.pad(a, ((0, rows - a.shape[0]), (0, 0)))


def _sequence_step(xs, target, rope, mods, gains, w_in_t, w_out_t, relay_ffn, fetch_ffn, send_grads, w_blk_b, b_pool_r,
                   pool_scale_r, conv_w_all, conv_b, after):
    sh_m, sc_m, gt_m, sh_f, sc_f, gt_f = mods
    g_pre_mix, g_post_mix, g_pre_ffn, g_post_ffn = gains
    h1, u_pool, qkv = _premix_inproj(xs, sh_m, sc_m, g_pre_mix, w_in_t, rope, after, tm=512)
    o_g, lse_g = _attn_fwd(qkv)
    x1, y1, h2, cat, attn, lse_all = _mix_out(xs, u_pool, o_g, lse_g, w_blk_b, b_pool_r, pool_scale_r, w_out_t,
                                              gt_m, g_post_mix, g_pre_ffn, sc_f, sh_f, tm=512)
    relay_ffn(x1)
    w_up_t, w_down_f = fetch_ffn(x1)
    gate, a_ffn, act, vd, dy2, dout, sums_ffn, loss_loc = _ffn_fwd_loss(h2, x1, target, w_up_t, w_down_f, conv_w_all, conv_b,
                                                              gt_f, g_post_ffn, tm=256, ck=256)

    dgc, dval, dw_down, dconv = _ffn_bwd_act(dy2, gate, a_ffn, act, vd, w_down_f, tm=512, tf=1408, ck=256)
    token = send_grads("down", [dw_down], [])
    dup, dh2 = _ffn_bwd_up(dgc, dval, w_up_t, conv_w_all, token, tm=512)
    dw_up_t = _wgrad(dup, h2, "wgrad_up", tk=2048, tmm=1408)
    token = send_grads("up", [dw_up_t], [])
    dx1, dpool, dattn, delta, dw_out_t, sums_mix = _mix_bwd(dh2, dout, x1, y1, cat, attn, w_out_t, sc_f,
                                                           g_pre_ffn, gt_m, g_post_mix, token, tm=512)
    du, dw_blk, sums_pool = _pool_bwd(dpool, u_pool, w_blk_b, b_pool_r, pool_scale_r, tm=512)
    token = send_grads("out", [dw_out_t], [sums_mix, sums_ffn, sums_pool, dw_blk, dconv, loss_loc])
    dproj, dw_in_t = _dproj_wgrad_in(du, _attn_bwd(qkv, dattn, lse_all, delta, token), rope, h1, tm=512, cm=512)
    token = send_grads("in", [dw_in_t], [])
    grad_x, sums_in = _inproj_bwd(dproj, w_in_t, xs, dx1, sc_m, g_pre_mix, token, tm=512)
    return (loss_loc, grad_x, dw_in_t, dw_out_t, dw_up_t, dw_down, dw_blk, dconv,
            sums_in, sums_mix, sums_ffn, sums_pool)


def kernel(x, c, positions, w_ada, b_ada, g_pre_mix, g_post_mix, g_pre_ffn, g_post_ffn, w_in, w_pool, b_pool, pool_scale, w_out, w_up, conv_w, conv_b, w_down, loss_target, m_w_ada, m_b_ada, m_g_pre_mix, m_g_post_mix, m_g_pre_ffn, m_g_post_ffn, m_w_in, m_w_pool, m_b_pool, m_pool_scale, m_w_out, m_w_up, m_conv_w, m_conv_b, m_w_down, v_w_ada, v_b_ada, v_g_pre_mix, v_g_post_mix, v_g_pre_ffn, v_g_post_ffn, v_w_in, v_w_pool, v_b_pool, v_pool_scale, v_w_out, v_w_up, v_conv_w, v_conv_b, v_w_down):
    s_len, d = x.shape[1], x.shape[2]
    d_ff = w_down.shape[1] * N_DEV
    me = _index(_place())
    xs, target = x[0], loss_target[0]

    c_all, mod, taps_all, (w_in_t, w_out_t), lands = _entry_exchange(
        jnp.broadcast_to(c, (8, d)), w_ada[0], b_ada, _pad_rows(conv_w[0], 8),
        [w_in[0].T.astype(BF16), w_out[0].T.astype(BF16)], [w_up[0].T, w_down[0]])
    c_all = c_all[:, 0, :]
    conv_w_all = jnp.transpose(taps_all[:, :3, :], (1, 0, 2)).reshape(3, d_ff)
    sh_m, sc_m, gt_m, sh_f, sc_f, gt_f = [mod[:, 0, :].reshape(1, -1)[:, k * d:(k + 1) * d] for k in range(6)]

    rope = _rope_tables(positions[0])
    w_blk = jnp.zeros((256, 256), F32)
    for gi in range(4):
        w_blk = lax.dynamic_update_slice(w_blk, w_pool[0, gi], (gi * HEAD_DIM, gi * HEAD_DIM))
    w_blk_b = w_blk.astype(BF16)
    b_pool_r, pool_scale_r = b_pool.reshape(1, 256), pool_scale.reshape(1, 256)

    w_send, w_recv, w_src, w_land, w_token = _exchange_start("gather_ici", [], lands, "ffn_weights_ici_start")
    relay = []

    def relay_ffn(after):
        _, blocks = _exchange_wait("gather_ici", w_send, w_recv, w_src, w_land, after, "ffn_weights_ici_wait")
        relay.extend(_exchange_start("gather_d2d", [], blocks, "ffn_weights_d2d_start"))

    def fetch_ffn(after):
        return _exchange_wait("gather_d2d", relay[0], relay[1], [], relay[3], after, "ffn_weights_d2d_wait")[1]

    flights = {}

    def send_grads(tag, slabs, whole):
        lands = [lax.empty((N_DEV, g.shape[0] // N_DEV, g.shape[1]), g.dtype) for g in slabs]
        lands += [lax.empty((N_DEV,) + a.shape, F32) for a in whole]
        modes = ("scatter",) * len(slabs) + ("allgather",) * len(whole)
        flights[tag] = (modes, *_exchange_start(modes, slabs + whole, lands, f"grads_{tag}_start"))
        return flights[tag][5]

    def arrived(tag, after):
        return _exchange_wait(*flights[tag][:5], after, f"grads_{tag}_wait")

    _, grad_x, *_, sums_in, _, _, _ = _sequence_step(
        xs, target, rope, (sh_m, sc_m, gt_m, sh_f, sc_f, gt_f), (g_pre_mix, g_post_mix, g_pre_ffn, g_post_ffn),
        w_in_t, w_out_t, relay_ffn, fetch_ffn, send_grads, w_blk_b, b_pool_r, pool_scale_r, conv_w_all, conv_b,
        w_token)

    send_grads("last", [], [sums_in])

    (own_down,), (parts_down,) = arrived("down", flights["last"][5])
    new_down = _sum_adam(own_down, parts_down, w_down[0], m_w_down[0], v_w_down[0], me, "adam_w_down", 32)
    (own_up,), (parts_up,) = arrived("up", new_down[0])
    new_up = _sum_adam(own_up, parts_up, w_up[0].T, m_w_up[0].T, v_w_up[0].T, me, "adam_w_up", 64)
    (own_out, *small), (parts_out, *gathered) = arrived("out", new_up[0])
    new_out = _sum_adam(own_out, parts_out, w_out[0], m_w_out[0], v_w_out[0], me, "adam_w_out", 128)
    (own_in,), (parts_in,) = arrived("in", new_out[0])
    new_in = _sum_adam(own_in, parts_in, w_in[0].T, m_w_in[0].T, v_w_in[0].T, me, "adam_w_in", 64)
    big = {"w_up": [a.T for a in new_up], "w_down": new_down, "w_out": new_out, "w_in": [a.T for a in new_in]}

    rep_w = [b_ada, g_pre_mix, g_post_mix, g_pre_ffn, g_post_ffn, w_pool, b_pool, pool_scale, conv_b]
    rep_m = [m_b_ada, m_g_pre_mix, m_g_post_mix, m_g_pre_ffn, m_g_post_ffn, m_w_pool, m_b_pool, m_pool_scale, m_conv_b]
    rep_v = [v_b_ada, v_g_pre_mix, v_g_post_mix, v_g_pre_ffn, v_g_post_ffn, v_w_pool, v_b_pool, v_pool_scale, v_conv_b]
    mine_last, got_last = arrived("last", new_in[0])
    small, gathered = [*mine_last, *small], [*got_last, *gathered]
    rep_out, dmod_all, dconv_tot, loss_tot = _small_sum_adam(small, gathered, rep_w, rep_m, rep_v)
    g_rep, d_rep, nm_rep, nv_rep = (rep_out[k::4] for k in range(4))

    fcol = d_ff // N_DEV
    taps = lambda a: jnp.transpose(a, (1, 0, 2))
    g_cw = lax.dynamic_slice(dconv_tot, (0, me * fcol), (3, fcol))[None]
    d_cw, nm_cw, nv_cw = [taps(a) for a in _adam(taps(conv_w), taps(g_cw), taps(m_conv_w), taps(v_conv_w), "adam_conv_w")]

    g_ada, d_ada, nm_ada, nv_ada = _ada_grad_adam(c_all, dmod_all, w_ada[0], m_w_ada[0], v_w_ada[0], 256)

    loss = loss_tot[0, 0]

    def group(k):
        rep = (g_rep, d_rep, nm_rep, nv_rep)[k]
        ada = (g_ada, d_ada, nm_ada, nv_ada)[k][None]
        cw = (g_cw, d_cw, nm_cw, nv_cw)[k]
        return [ada, rep[0], rep[1], rep[2], rep[3], rep[4], big["w_in"][k][None], rep[5], rep[6], rep[7],
                big["w_out"][k][None], big["w_up"][k][None], cw, rep[8], big["w_down"][k][None]]

    return (loss, grad_x[None], *group(0), *group(1), *group(2), *group(3))
```
